```python
import jax, jax.numpy as jnp
from jax import lax
import numpy as np

D_MODEL = 2048
BATCH = 8
SEQ = 2048
DEPTH = 2

EPS = 1e-6
NEG = -1e30
HEAD_DIM = 128
ROT_DIM = HEAD_DIM // 4
ROPE_THETA = 500000.0
DILATED_PATTERNS = ((128, 1), (512, 4), (2048, 16))
ATT_GROUPS = len(DILATED_PATTERNS)
HEADS_PER_GROUP = D_MODEL // (2 * HEAD_DIM)
ATT_QKV = ATT_GROUPS * HEADS_PER_GROUP * HEAD_DIM
ATT_OUT = HEADS_PER_GROUP * HEAD_DIM
POOL_WINDOWS = (2, 4, 8, 16)
POOL_GROUPS = len(POOL_WINDOWS)
POOL_WIDTH = D_MODEL // 2
POOL_CH = POOL_WIDTH // POOL_GROUPS
SGU_WIDTH = D_MODEL // 2
SGU_GROUPS = 4
SGU_CH = SGU_WIDTH // SGU_GROUPS
CHUNK = 128
CONV_WIDTH = D_MODEL // 2
CONV_K = 31
EVEN_COLS = (POOL_WIDTH, POOL_WIDTH, ATT_QKV, ATT_QKV, ATT_QKV, ATT_OUT)
ODD_COLS = (SGU_WIDTH, SGU_WIDTH, SGU_WIDTH, CONV_WIDTH, CONV_WIDTH, CONV_WIDTH)
EVEN_IN = sum(EVEN_COLS)
ODD_IN = sum(ODD_COLS)
MIX_OUT = POOL_WIDTH + ATT_OUT
N_EVEN = (DEPTH + 1) // 2
N_ODD = DEPTH // 2

kernel_name = "hybrid_pool_dilattn_sgu_conv"


def _split_points(cols):
    return [int(c) for c in np.cumsum(cols)[:-1]]


def rmsnorm(x, g):
    xf = x.astype(jnp.float32)
    y = xf * lax.rsqrt(jnp.mean(xf * xf, axis=-1, keepdims=True) + EPS) * g.astype(jnp.float32)
    return y.astype(x.dtype)


def layernorm(x, g, b):
    xf = x.astype(jnp.float32)
    mu = jnp.mean(xf, axis=-1, keepdims=True)
    var = jnp.mean(jnp.square(xf - mu), axis=-1, keepdims=True)
    y = (xf - mu) * lax.rsqrt(var + EPS) * g.astype(jnp.float32) + b.astype(jnp.float32)
    return y.astype(x.dtype)


def partial_rope(t, cos, sin):
    tf = t.astype(jnp.float32)
    half = ROT_DIM // 2
    t1, t2 = tf[..., :half], tf[..., half:ROT_DIM]
    c, s = cos[None, :, None, :], sin[None, :, None, :]
    out = jnp.concatenate([t1 * c - t2 * s, t2 * c + t1 * s, tf[..., ROT_DIM:]], axis=-1)
    return out.astype(t.dtype)


def causal_pool_mixer(xa, pool_w, pool_scale):
    B, S, _ = xa.shape
    xg = xa.reshape(B, S, POOL_GROUPS, POOL_CH).astype(jnp.float32)
    csp = jnp.concatenate([jnp.zeros((B, 1, POOL_GROUPS, POOL_CH), jnp.float32),
                           jnp.cumsum(xg, axis=1)], axis=1)
    t = jnp.arange(S)
    outs = []
    for g, w in enumerate(POOL_WINDOWS):
        upper = csp[:, 1:, g]
        lower = jnp.concatenate([jnp.zeros((B, w - 1, POOL_CH), jnp.float32),
                                 csp[:, :S + 1 - w, g]], axis=1)
        count = jnp.minimum(t + 1, w).astype(jnp.float32)[None, :, None]
        outs.append((upper - lower) / count - xg[:, :, g])
    pooled = jnp.stack(outs, axis=2).astype(xa.dtype)
    mixed = jnp.einsum('bsgc,gcd->bsgd', pooled, pool_w)
    return mixed.reshape(B, S, POOL_WIDTH) * pool_scale


def dilated_group(q, k, v, dilation, span):
    B, S, H, E = q.shape
    L = S // dilation
    nb = -(-L // span)
    Lp = nb * span

    def to_blocks(t):
        t = t.reshape(B, L, dilation, H, E)
        t = jnp.pad(t, ((0, 0), (0, Lp - L), (0, 0), (0, 0), (0, 0)))
        return t.reshape(B, nb, span, dilation, H, E)

    def with_prev(t):
        prev = jnp.pad(t, ((0, 0), (1, 0), (0, 0), (0, 0), (0, 0), (0, 0)))[:, :-1]
        return jnp.concatenate([prev, t], axis=2)

    qb = to_blocks(q)
    kk = with_prev(to_blocks(k))
    vv = with_prev(to_blocks(v))
    s = jnp.einsum('bnqrhe,bnkrhe->bnrhqk', qb, kk,
                   preferred_element_type=jnp.float32) * (HEAD_DIM ** -0.5)
    qi = jnp.arange(span)[:, None]
    kj = jnp.arange(2 * span)[None, :] - span
    dist = qi - kj
    blk = jnp.arange(nb)[:, None, None]
    valid = (dist >= 0)[None] & (dist <= span)[None] & (blk * span + kj[None] >= 0)
    s = jnp.where(valid[None, :, None, None], s, NEG)
    m = jnp.max(s, axis=-1, keepdims=True)
    p = jnp.exp(s - m)
    den = jnp.sum(p, axis=-1)
    o = jnp.einsum('bnrhqk,bnkrhe->bnqrhe', p.astype(vv.dtype), vv,
                   preferred_element_type=jnp.float32)
    den_t = jnp.transpose(den, (0, 1, 4, 2, 3))
    o = o / den_t[..., None]
    lse = jnp.transpose(m[..., 0], (0, 1, 4, 2, 3)) + jnp.log(den_t)
    o = o.reshape(B, Lp, dilation, H, E)[:, :L].reshape(B, S, H, E)
    lse = lse.reshape(B, Lp, dilation, H)[:, :L].reshape(B, S, H)
    return o, lse


def dilated_attention(q, k, v, cos, sin):
    B, S, _ = q.shape
    shp = (B, S, ATT_GROUPS * HEADS_PER_GROUP, HEAD_DIM)
    q = partial_rope(q.reshape(shp), cos, sin).reshape(B, S, ATT_GROUPS, HEADS_PER_GROUP, HEAD_DIM)
    k = partial_rope(k.reshape(shp), cos, sin).reshape(B, S, ATT_GROUPS, HEADS_PER_GROUP, HEAD_DIM)
    v = v.reshape(B, S, ATT_GROUPS, HEADS_PER_GROUP, HEAD_DIM)
    outs, lses = [], []
    for g, (window, dilation) in enumerate(DILATED_PATTERNS):
        o_g, lse_g = dilated_group(q[:, :, g], k[:, :, g], v[:, :, g], dilation, window // dilation)
        outs.append(o_g)
        lses.append(lse_g)
    wts = jax.nn.softmax(jnp.stack(lses, axis=0), axis=0)
    o = jnp.sum(wts[..., None] * jnp.stack(outs, axis=0), axis=0)
    return o.reshape(B, S, ATT_OUT).astype(q.dtype)


def chunked_sgu(u, v, g, b, w_s, b_s):
    B, S, _ = v.shape
    vn = layernorm(v, g, b).reshape(B, S // CHUNK, CHUNK, SGU_GROUPS, SGU_CH)
    mask = jnp.tril(jnp.ones((CHUNK, CHUNK), w_s.dtype))
    s = jnp.einsum('hij,bnjhc->bnihc', w_s * mask[None], vn) + b_s.T[None, None, :, :, None]
    return u * s.reshape(B, S, SGU_WIDTH)


def causal_depthwise_conv(x, w, b):
    out = lax.conv_general_dilated(x, w[:, None, :].astype(x.dtype), window_strides=(1,),
                                   padding=[(CONV_K - 1, 0)],
                                   dimension_numbers=('NWC', 'WIO', 'NWC'),
                                   feature_group_count=x.shape[-1])
    return out + b


def even_mixer(h, w_in, pool_w, pool_scale, w_out, cos, sin):
    z = h @ w_in
    a_in, a_gate, q, k, v, b_gate = jnp.split(z, _split_points(EVEN_COLS), axis=-1)
    ya = causal_pool_mixer(a_in, pool_w, pool_scale) * jax.nn.silu(a_gate)
    yb = dilated_attention(q, k, v, cos, sin) * jax.nn.silu(b_gate)
    return jnp.concatenate([ya, yb], axis=-1) @ w_out


def odd_mixer(h, w_in, sgu_g, sgu_b, sgu_w, sgu_bias, conv_w, conv_b, cn_g, cn_b, w_out):
    z = h @ w_in
    u, v, c_gate, d_val, d_glu, d_gate = jnp.split(z, _split_points(ODD_COLS), axis=-1)
    yc = chunked_sgu(u, v, sgu_g, sgu_b, sgu_w, sgu_bias) * jax.nn.silu(c_gate)
    d = d_val * jax.nn.sigmoid(d_glu)
    d = causal_depthwise_conv(d, conv_w, conv_b)
    d = jax.nn.silu(layernorm(d, cn_g, cn_b))
    yd = d * jax.nn.silu(d_gate)
    return jnp.concatenate([yc, yd], axis=-1) @ w_out


def _fwd_setup_inputs(seed: int = 0) -> dict:
    key = jax.random.key(seed)
    ks = jax.random.split(key, 20)
    f32 = jnp.float32

    def nrm(k, shape, scale):
        return jax.random.normal(k, shape, f32) * scale

    def gain(k, shape):
        return 1.0 + 0.05 * jax.random.normal(k, shape, f32)

    return {
        "x": jax.random.normal(ks[0], (BATCH, SEQ, D_MODEL), f32),
        "e_pre_norm": gain(ks[1], (N_EVEN, D_MODEL)),
        "e_w_in": nrm(ks[2], (N_EVEN, D_MODEL, EVEN_IN), D_MODEL ** -0.5),
        "e_pool_w": nrm(ks[3], (N_EVEN, POOL_GROUPS, POOL_CH, POOL_CH), POOL_CH ** -0.5),
        "e_pool_scale": gain(ks[4], (N_EVEN, POOL_WIDTH)),
        "e_w_out": nrm(ks[5], (N_EVEN, MIX_OUT, D_MODEL), MIX_OUT ** -0.5),
        "e_post_norm": gain(ks[6], (N_EVEN, D_MODEL)),
        "o_pre_norm": gain(ks[7], (N_ODD, D_MODEL)),
        "o_w_in": nrm(ks[8], (N_ODD, D_MODEL, ODD_IN), D_MODEL ** -0.5),
        "o_sgu_norm_g": gain(ks[9], (N_ODD, SGU_WIDTH)),
        "o_sgu_norm_b": nrm(ks[10], (N_ODD, SGU_WIDTH), 0.02),
        "o_sgu_w": nrm(ks[11], (N_ODD, SGU_GROUPS, CHUNK, CHUNK), CHUNK ** -0.5),
        "o_sgu_b": gain(ks[12], (N_ODD, SGU_GROUPS, CHUNK)),
        "o_conv_w": nrm(ks[13], (N_ODD, CONV_K, CONV_WIDTH), CONV_K ** -0.5),
        "o_conv_b": nrm(ks[14], (N_ODD, CONV_WIDTH), 0.02),
        "o_conv_norm_g": gain(ks[15], (N_ODD, CONV_WIDTH)),
        "o_conv_norm_b": nrm(ks[16], (N_ODD, CONV_WIDTH), 0.02),
        "o_w_out": nrm(ks[17], (N_ODD, MIX_OUT, D_MODEL), MIX_OUT ** -0.5),
        "o_post_norm": gain(ks[18], (N_ODD, D_MODEL)),
    }


def _fwd_reference(x, e_pre_norm, e_w_in, e_pool_w, e_pool_scale, e_w_out, e_post_norm,
              o_pre_norm, o_w_in, o_sgu_norm_g, o_sgu_norm_b, o_sgu_w, o_sgu_b,
              o_conv_w, o_conv_b, o_conv_norm_g, o_conv_norm_b, o_w_out, o_post_norm):
    S = x.shape[1]
    pos = jnp.arange(S, dtype=jnp.float32)
    inv_freq = jnp.power(ROPE_THETA, -jnp.arange(0, ROT_DIM, 2, dtype=jnp.float32) / ROT_DIM)
    ang = pos[:, None] * inv_freq[None, :]
    cos, sin = jnp.cos(ang), jnp.sin(ang)
    for i in range(DEPTH):
        j = i // 2
        if i % 2 == 0:
            h = rmsnorm(x, e_pre_norm[j])
            y = even_mixer(h, e_w_in[j], e_pool_w[j], e_pool_scale[j], e_w_out[j], cos, sin)
            x = x + rmsnorm(y, e_post_norm[j])
        else:
            h = rmsnorm(x, o_pre_norm[j])
            y = odd_mixer(h, o_w_in[j], o_sgu_norm_g[j], o_sgu_norm_b[j], o_sgu_w[j], o_sgu_b[j],
                          o_conv_w[j], o_conv_b[j], o_conv_norm_g[j], o_conv_norm_b[j], o_w_out[j])
            x = x + rmsnorm(y, o_post_norm[j])
    return x


import jax as _jax
import jax.numpy as _jnp

TWIN_FORMAT = 'train_step'
FWD_PARAMS = ['x', 'e_pre_norm', 'e_w_in', 'e_pool_w', 'e_pool_scale', 'e_w_out', 'e_post_norm', 'o_pre_norm', 'o_w_in', 'o_sgu_norm_g', 'o_sgu_norm_b', 'o_sgu_w', 'o_sgu_b', 'o_conv_w', 'o_conv_b', 'o_conv_norm_g', 'o_conv_norm_b', 'o_w_out', 'o_post_norm']
TWIN_WEIGHTS = ['e_pre_norm', 'e_w_in', 'e_pool_w', 'e_pool_scale', 'e_w_out', 'e_post_norm', 'o_pre_norm', 'o_w_in', 'o_sgu_norm_g', 'o_sgu_norm_b', 'o_sgu_w', 'o_sgu_b', 'o_conv_w', 'o_conv_b', 'o_conv_norm_g', 'o_conv_norm_b', 'o_w_out', 'o_post_norm']
TWIN_DIFF_INPUT = 'x'
TWIN_INPUTS = ['x', 'e_pre_norm', 'e_w_in', 'e_pool_w', 'e_pool_scale', 'e_w_out', 'e_post_norm', 'o_pre_norm', 'o_w_in', 'o_sgu_norm_g', 'o_sgu_norm_b', 'o_sgu_w', 'o_sgu_b', 'o_conv_w', 'o_conv_b', 'o_conv_norm_g', 'o_conv_norm_b', 'o_w_out', 'o_post_norm', 'loss_target', 'm_e_pre_norm', 'm_e_w_in', 'm_e_pool_w', 'm_e_pool_scale', 'm_e_w_out', 'm_e_post_norm', 'm_o_pre_norm', 'm_o_w_in', 'm_o_sgu_norm_g', 'm_o_sgu_norm_b', 'm_o_sgu_w', 'm_o_sgu_b', 'm_o_conv_w', 'm_o_conv_b', 'm_o_conv_norm_g', 'm_o_conv_norm_b', 'm_o_w_out', 'm_o_post_norm', 'v_e_pre_norm', 'v_e_w_in', 'v_e_pool_w', 'v_e_pool_scale', 'v_e_w_out', 'v_e_post_norm', 'v_o_pre_norm', 'v_o_w_in', 'v_o_sgu_norm_g', 'v_o_sgu_norm_b', 'v_o_sgu_w', 'v_o_sgu_b', 'v_o_conv_w', 'v_o_conv_b', 'v_o_conv_norm_g', 'v_o_conv_norm_b', 'v_o_w_out', 'v_o_post_norm']
TWIN_OUTPUTS = ['loss', 'grad_x', 'grad_e_pre_norm', 'grad_e_w_in', 'grad_e_pool_w', 'grad_e_pool_scale', 'grad_e_w_out', 'grad_e_post_norm', 'grad_o_pre_norm', 'grad_o_w_in', 'grad_o_sgu_norm_g', 'grad_o_sgu_norm_b', 'grad_o_sgu_w', 'grad_o_sgu_b', 'grad_o_conv_w', 'grad_o_conv_b', 'grad_o_conv_norm_g', 'grad_o_conv_norm_b', 'grad_o_w_out', 'grad_o_post_norm', 'delta_e_pre_norm', 'delta_e_w_in', 'delta_e_pool_w', 'delta_e_pool_scale', 'delta_e_w_out', 'delta_e_post_norm', 'delta_o_pre_norm', 'delta_o_w_in', 'delta_o_sgu_norm_g', 'delta_o_sgu_norm_b', 'delta_o_sgu_w', 'delta_o_sgu_b', 'delta_o_conv_w', 'delta_o_conv_b', 'delta_o_conv_norm_g', 'delta_o_conv_norm_b', 'delta_o_w_out', 'delta_o_post_norm', 'new_m_e_pre_norm', 'new_m_e_w_in', 'new_m_e_pool_w', 'new_m_e_pool_scale', 'new_m_e_w_out', 'new_m_e_post_norm', 'new_m_o_pre_norm', 'new_m_o_w_in', 'new_m_o_sgu_norm_g', 'new_m_o_sgu_norm_b', 'new_m_o_sgu_w', 'new_m_o_sgu_b', 'new_m_o_conv_w', 'new_m_o_conv_b', 'new_m_o_conv_norm_g', 'new_m_o_conv_norm_b', 'new_m_o_w_out', 'new_m_o_post_norm', 'new_v_e_pre_norm', 'new_v_e_w_in', 'new_v_e_pool_w', 'new_v_e_pool_scale', 'new_v_e_w_out', 'new_v_e_post_norm', 'new_v_o_pre_norm', 'new_v_o_w_in', 'new_v_o_sgu_norm_g', 'new_v_o_sgu_norm_b', 'new_v_o_sgu_w', 'new_v_o_sgu_b', 'new_v_o_conv_w', 'new_v_o_conv_b', 'new_v_o_conv_norm_g', 'new_v_o_conv_norm_b', 'new_v_o_w_out', 'new_v_o_post_norm']
TWIN_LEAF_KINDS = {'loss': 'loss', 'grad_x': 'grad_x', 'grad_e_pre_norm': 'grad_w', 'grad_e_w_in': 'grad_w', 'grad_e_pool_w': 'grad_w', 'grad_e_pool_scale': 'grad_w', 'grad_e_w_out': 'grad_w', 'grad_e_post_norm': 'grad_w', 'grad_o_pre_norm': 'grad_w', 'grad_o_w_in': 'grad_w', 'grad_o_sgu_norm_g': 'grad_w', 'grad_o_sgu_norm_b': 'grad_w', 'grad_o_sgu_w': 'grad_w', 'grad_o_sgu_b': 'grad_w', 'grad_o_conv_w': 'grad_w', 'grad_o_conv_b': 'grad_w', 'grad_o_conv_norm_g': 'grad_w', 'grad_o_conv_norm_b': 'grad_w', 'grad_o_w_out': 'grad_w', 'grad_o_post_norm': 'grad_w', 'delta_e_pre_norm': 'delta_w', 'delta_e_w_in': 'delta_w', 'delta_e_pool_w': 'delta_w', 'delta_e_pool_scale': 'delta_w', 'delta_e_w_out': 'delta_w', 'delta_e_post_norm': 'delta_w', 'delta_o_pre_norm': 'delta_w', 'delta_o_w_in': 'delta_w', 'delta_o_sgu_norm_g': 'delta_w', 'delta_o_sgu_norm_b': 'delta_w', 'delta_o_sgu_w': 'delta_w', 'delta_o_sgu_b': 'delta_w', 'delta_o_conv_w': 'delta_w', 'delta_o_conv_b': 'delta_w', 'delta_o_conv_norm_g': 'delta_w', 'delta_o_conv_norm_b': 'delta_w', 'delta_o_w_out': 'delta_w', 'delta_o_post_norm': 'delta_w', 'new_m_e_pre_norm': 'new_m', 'new_m_e_w_in': 'new_m', 'new_m_e_pool_w': 'new_m', 'new_m_e_pool_scale': 'new_m', 'new_m_e_w_out': 'new_m', 'new_m_e_post_norm': 'new_m', 'new_m_o_pre_norm': 'new_m', 'new_m_o_w_in': 'new_m', 'new_m_o_sgu_norm_g': 'new_m', 'new_m_o_sgu_norm_b': 'new_m', 'new_m_o_sgu_w': 'new_m', 'new_m_o_sgu_b': 'new_m', 'new_m_o_conv_w': 'new_m', 'new_m_o_conv_b': 'new_m', 'new_m_o_conv_norm_g': 'new_m', 'new_m_o_conv_norm_b': 'new_m', 'new_m_o_w_out': 'new_m', 'new_m_o_post_norm': 'new_m', 'new_v_e_pre_norm': 'new_v', 'new_v_e_w_in': 'new_v', 'new_v_e_pool_w': 'new_v', 'new_v_e_pool_scale': 'new_v', 'new_v_e_w_out': 'new_v', 'new_v_e_post_norm': 'new_v', 'new_v_o_pre_norm': 'new_v', 'new_v_o_w_in': 'new_v', 'new_v_o_sgu_norm_g': 'new_v', 'new_v_o_sgu_norm_b': 'new_v', 'new_v_o_sgu_w': 'new_v', 'new_v_o_sgu_b': 'new_v', 'new_v_o_conv_w': 'new_v', 'new_v_o_conv_b': 'new_v', 'new_v_o_conv_norm_g': 'new_v', 'new_v_o_conv_norm_b': 'new_v', 'new_v_o_w_out': 'new_v', 'new_v_o_post_norm': 'new_v'}


def _forward(args):
    return _fwd_reference(*[args[k] for k in FWD_PARAMS])


def _output_shape():
    out = _jax.eval_shape(lambda: _forward(_fwd_setup_inputs(0)))
    return out.shape, out.dtype

N_MICROBATCH = 1
ADAM_LR = 0.001
ADAM_B1 = 0.9
ADAM_B2 = 0.999
ADAM_EPS = 1e-08
ADAM_WD = 0.01
ADAM_STEP = 10
PER_EXAMPLE_BATCH_AXIS = {'x': 0, 'loss_target': 0}
SHARED_INPUTS = []
_WEIGHT_DTYPES = {'e_pre_norm': _jnp.float32, 'e_w_in': _jnp.float32, 'e_pool_w': _jnp.float32, 'e_pool_scale': _jnp.float32, 'e_w_out': _jnp.float32, 'e_post_norm': _jnp.float32, 'o_pre_norm': _jnp.float32, 'o_w_in': _jnp.float32, 'o_sgu_norm_g': _jnp.float32, 'o_sgu_norm_b': _jnp.float32, 'o_sgu_w': _jnp.float32, 'o_sgu_b': _jnp.float32, 'o_conv_w': _jnp.float32, 'o_conv_b': _jnp.float32, 'o_conv_norm_g': _jnp.float32, 'o_conv_norm_b': _jnp.float32, 'o_w_out': _jnp.float32, 'o_post_norm': _jnp.float32}
MOMENT_SCALE = {'e_pre_norm': 2.360882e-01, 'e_w_in': 9.685879e-02, 'e_pool_w': 2.350030e-01, 'e_pool_scale': 2.434856e-01, 'e_w_out': 1.740431e-01, 'e_post_norm': 8.004332e+00, 'o_pre_norm': 1.746552e-01, 'o_w_in': 1.035646e-01, 'o_sgu_norm_g': 7.760079e-02, 'o_sgu_norm_b': 7.918607e-02, 'o_sgu_w': 1.098343e-01, 'o_sgu_b': 1.493345e-01, 'o_conv_w': 7.304759e-02, 'o_conv_b': 2.167971e-01, 'o_conv_norm_g': 1.134688e-01, 'o_conv_norm_b': 1.299460e-01, 'o_w_out': 1.260855e-01, 'o_post_norm': 7.996222e+00}


def _to_microbatches(a, axis):
    t = _jnp.moveaxis(a, axis, 0)
    t = t.reshape((N_MICROBATCH, t.shape[0] // N_MICROBATCH) + t.shape[1:])
    return _jnp.moveaxis(t, 1, axis + 1)


def setup_inputs(seed: int = 0) -> dict:
    inp = _fwd_setup_inputs(seed)
    key = _jax.random.fold_in(_jax.random.key(seed), 7919)
    shape, _ = _output_shape()
    out = dict(inp)
    out["loss_target"] = _jax.random.normal(_jax.random.fold_in(key, 0), shape, _jnp.float32)
    for i, name in enumerate(TWIN_WEIGHTS):
        w = inp[name].astype(_jnp.float32)
        if MOMENT_SCALE is None:
            s = _jnp.sqrt(_jnp.mean(_jnp.square(w)) + 1e-30)
        else:
            s = MOMENT_SCALE[name]
        km, kv = _jax.random.split(_jax.random.fold_in(key, i + 1))
        out[name] = w
        out["m_" + name] = s * _jax.random.normal(km, w.shape, _jnp.float32)
        out["v_" + name] = (s * s) * _jax.random.uniform(kv, w.shape, _jnp.float32, 0.5, 1.5)
    if N_MICROBATCH > 1:
        for name, axis in PER_EXAMPLE_BATCH_AXIS.items():
            out[name] = _to_microbatches(out[name], axis)
    return {'x': out['x'], 'e_pre_norm': out['e_pre_norm'], 'e_w_in': out['e_w_in'], 'e_pool_w': out['e_pool_w'], 'e_pool_scale': out['e_pool_scale'], 'e_w_out': out['e_w_out'], 'e_post_norm': out['e_post_norm'], 'o_pre_norm': out['o_pre_norm'], 'o_w_in': out['o_w_in'], 'o_sgu_norm_g': out['o_sgu_norm_g'], 'o_sgu_norm_b': out['o_sgu_norm_b'], 'o_sgu_w': out['o_sgu_w'], 'o_sgu_b': out['o_sgu_b'], 'o_conv_w': out['o_conv_w'], 'o_conv_b': out['o_conv_b'], 'o_conv_norm_g': out['o_conv_norm_g'], 'o_conv_norm_b': out['o_conv_norm_b'], 'o_w_out': out['o_w_out'], 'o_post_norm': out['o_post_norm'], 'loss_target': out['loss_target'], 'm_e_pre_norm': out['m_e_pre_norm'], 'm_e_w_in': out['m_e_w_in'], 'm_e_pool_w': out['m_e_pool_w'], 'm_e_pool_scale': out['m_e_pool_scale'], 'm_e_w_out': out['m_e_w_out'], 'm_e_post_norm': out['m_e_post_norm'], 'm_o_pre_norm': out['m_o_pre_norm'], 'm_o_w_in': out['m_o_w_in'], 'm_o_sgu_norm_g': out['m_o_sgu_norm_g'], 'm_o_sgu_norm_b': out['m_o_sgu_norm_b'], 'm_o_sgu_w': out['m_o_sgu_w'], 'm_o_sgu_b': out['m_o_sgu_b'], 'm_o_conv_w': out['m_o_conv_w'], 'm_o_conv_b': out['m_o_conv_b'], 'm_o_conv_norm_g': out['m_o_conv_norm_g'], 'm_o_conv_norm_b': out['m_o_conv_norm_b'], 'm_o_w_out': out['m_o_w_out'], 'm_o_post_norm': out['m_o_post_norm'], 'v_e_pre_norm': out['v_e_pre_norm'], 'v_e_w_in': out['v_e_w_in'], 'v_e_pool_w': out['v_e_pool_w'], 'v_e_pool_scale': out['v_e_pool_scale'], 'v_e_w_out': out['v_e_w_out'], 'v_e_post_norm': out['v_e_post_norm'], 'v_o_pre_norm': out['v_o_pre_norm'], 'v_o_w_in': out['v_o_w_in'], 'v_o_sgu_norm_g': out['v_o_sgu_norm_g'], 'v_o_sgu_norm_b': out['v_o_sgu_norm_b'], 'v_o_sgu_w': out['v_o_sgu_w'], 'v_o_sgu_b': out['v_o_sgu_b'], 'v_o_conv_w': out['v_o_conv_w'], 'v_o_conv_b': out['v_o_conv_b'], 'v_o_conv_norm_g': out['v_o_conv_norm_g'], 'v_o_conv_norm_b': out['v_o_conv_norm_b'], 'v_o_w_out': out['v_o_w_out'], 'v_o_post_norm': out['v_o_post_norm']}


def _loss(weights, diff, rest, loss_target):
    with _jax.named_scope("forward"):
        args = {**rest, TWIN_DIFF_INPUT: diff, **{k: w.astype(_WEIGHT_DTYPES[k]) for k, w in weights.items()}}
        y = _forward(args)
    with _jax.named_scope("loss_head"):
        err = _jnp.square(y.astype(_jnp.float32) - loss_target)
        return 0.5 * _jnp.sum(_jnp.mean(err, axis=-1)) if err.ndim else 0.5 * err


def _adamw(w, g, m, v):
    m = ADAM_B1 * m + (1.0 - ADAM_B1) * g
    v = ADAM_B2 * v + (1.0 - ADAM_B2) * _jnp.square(g)
    m_hat = m / (1.0 - ADAM_B1 ** ADAM_STEP)
    v_hat = v / (1.0 - ADAM_B2 ** ADAM_STEP)
    delta = -ADAM_LR * (m_hat / (_jnp.sqrt(v_hat) + ADAM_EPS) + ADAM_WD * w)
    return delta, m, v


def reference(x, e_pre_norm, e_w_in, e_pool_w, e_pool_scale, e_w_out, e_post_norm, o_pre_norm, o_w_in, o_sgu_norm_g, o_sgu_norm_b, o_sgu_w, o_sgu_b, o_conv_w, o_conv_b, o_conv_norm_g, o_conv_norm_b, o_w_out, o_post_norm, loss_target, m_e_pre_norm, m_e_w_in, m_e_pool_w, m_e_pool_scale, m_e_w_out, m_e_post_norm, m_o_pre_norm, m_o_w_in, m_o_sgu_norm_g, m_o_sgu_norm_b, m_o_sgu_w, m_o_sgu_b, m_o_conv_w, m_o_conv_b, m_o_conv_norm_g, m_o_conv_norm_b, m_o_w_out, m_o_post_norm, v_e_pre_norm, v_e_w_in, v_e_pool_w, v_e_pool_scale, v_e_w_out, v_e_post_norm, v_o_pre_norm, v_o_w_in, v_o_sgu_norm_g, v_o_sgu_norm_b, v_o_sgu_w, v_o_sgu_b, v_o_conv_w, v_o_conv_b, v_o_conv_norm_g, v_o_conv_norm_b, v_o_w_out, v_o_post_norm):
    given = dict(x=x, e_pre_norm=e_pre_norm, e_w_in=e_w_in, e_pool_w=e_pool_w, e_pool_scale=e_pool_scale, e_w_out=e_w_out, e_post_norm=e_post_norm, o_pre_norm=o_pre_norm, o_w_in=o_w_in, o_sgu_norm_g=o_sgu_norm_g, o_sgu_norm_b=o_sgu_norm_b, o_sgu_w=o_sgu_w, o_sgu_b=o_sgu_b, o_conv_w=o_conv_w, o_conv_b=o_conv_b, o_conv_norm_g=o_conv_norm_g, o_conv_norm_b=o_conv_norm_b, o_w_out=o_w_out, o_post_norm=o_post_norm, loss_target=loss_target, m_e_pre_norm=m_e_pre_norm, m_e_w_in=m_e_w_in, m_e_pool_w=m_e_pool_w, m_e_pool_scale=m_e_pool_scale, m_e_w_out=m_e_w_out, m_e_post_norm=m_e_post_norm, m_o_pre_norm=m_o_pre_norm, m_o_w_in=m_o_w_in, m_o_sgu_norm_g=m_o_sgu_norm_g, m_o_sgu_norm_b=m_o_sgu_norm_b, m_o_sgu_w=m_o_sgu_w, m_o_sgu_b=m_o_sgu_b, m_o_conv_w=m_o_conv_w, m_o_conv_b=m_o_conv_b, m_o_conv_norm_g=m_o_conv_norm_g, m_o_conv_norm_b=m_o_conv_norm_b, m_o_w_out=m_o_w_out, m_o_post_norm=m_o_post_norm, v_e_pre_norm=v_e_pre_norm, v_e_w_in=v_e_w_in, v_e_pool_w=v_e_pool_w, v_e_pool_scale=v_e_pool_scale, v_e_w_out=v_e_w_out, v_e_post_norm=v_e_post_norm, v_o_pre_norm=v_o_pre_norm, v_o_w_in=v_o_w_in, v_o_sgu_norm_g=v_o_sgu_norm_g, v_o_sgu_norm_b=v_o_sgu_norm_b, v_o_sgu_w=v_o_sgu_w, v_o_sgu_b=v_o_sgu_b, v_o_conv_w=v_o_conv_w, v_o_conv_b=v_o_conv_b, v_o_conv_norm_g=v_o_conv_norm_g, v_o_conv_norm_b=v_o_conv_norm_b, v_o_w_out=v_o_w_out, v_o_post_norm=v_o_post_norm)
    weights = {n: given[n] for n in TWIN_WEIGHTS}
    shared = {n: given[n] for n in SHARED_INPUTS}
    per_example = {n: given[n] for n in ['x']}
    grad_fn = _jax.value_and_grad(_loss, argnums=(0, 1))

    def one_microbatch(ex, loss_target):
        ex = dict(ex)
        diff = ex.pop(TWIN_DIFF_INPUT)
        return grad_fn(weights, diff, {**shared, **ex}, loss_target)

    if N_MICROBATCH == 1:
        loss, (grad_w, grad_x) = one_microbatch(per_example, given["loss_target"])
    else:
        def body(carry, xs):
            loss_sum, grad_sum = carry
            l_k, (gw_k, gx_k) = one_microbatch(xs[0], xs[1])
            with _jax.named_scope("update"):
                return (loss_sum + l_k, _jax.tree.map(_jnp.add, grad_sum, gw_k)), gx_k

        init = (_jnp.zeros((), _jnp.float32), _jax.tree.map(_jnp.zeros_like, weights))
        (loss, grad_w), grad_x = _jax.lax.scan(body, init, (per_example, given["loss_target"]))
    with _jax.named_scope("update"):
        delta_w, new_m, new_v = {}, {}, {}
        for n in TWIN_WEIGHTS:
            delta_w[n], new_m[n], new_v[n] = _adamw(weights[n], grad_w[n], given["m_" + n], given["v_" + n])
    return (loss, grad_x, *[grad_w[n] for n in TWIN_WEIGHTS], *[delta_w[n] for n in TWIN_WEIGHTS],
            *[new_m[n] for n in TWIN_WEIGHTS], *[new_v[n] for n in TWIN_WEIGHTS])
```

```python
import functools

import numpy as np
import jax
import jax.numpy as jnp
from jax import lax
from jax.experimental import pallas as pl
from jax.experimental.pallas import tpu as pltpu

F32 = jnp.float32
BF16 = jnp.bfloat16

S = 2048
D = 2048
NDEV = 8
EPS = 1e-6
NEG = -1e30
HEAD_DIM = 128
ROT_DIM = 32
ROPE_THETA = 500000.0
PATTERNS = ((128, 1), (512, 4), (2048, 16))
BLK = 128
EVEN_IN = 12288
ODD_IN = 6144
HALF = 1024
CONV_K = 31
HALO = 32
TR = 256
SUB = 32

ADAM_LR = 0.001
ADAM_B1 = 0.9
ADAM_B2 = 0.999
ADAM_EPS = 1e-08
ADAM_WD = 0.01
ADAM_STEP = 10

VMEM_BIG = 56 * 1024 * 1024
MESH = pl.DeviceIdType.MESH

NN = (((1,), (0,)), ((), ()))
NT = (((1,), (1,)), ((), ()))
TN = (((0,), (0,)), ((), ()))


def _dot(a, b, dn=NN):
    return lax.dot_general(a, b, dn, preferred_element_type=F32)


def _sigmoid(x):
    return 1.0 / (1.0 + jnp.exp(-x))


def _silu_and_grad(x):
    sg = _sigmoid(x)
    return x * sg, sg * (1.0 + x * (1.0 - sg))


def _params(sem, vmem=None):
    return pltpu.CompilerParams(dimension_semantics=sem, vmem_limit_bytes=vmem)


def _matmul(a, b, *, dn, grid, a_spec, b_spec, o_spec, out_shape, out_dtype, acc_shape, name):
    nk = grid[2]

    def body(a_ref, b_ref, o_ref, *acc):
        if nk == 1:
            o_ref[...] = _dot(a_ref[...], b_ref[...], dn).astype(o_ref.dtype)
            return
        acc_ref = acc[0]
        k = pl.program_id(2)

        @pl.when(k == 0)
        def _():
            acc_ref[...] = jnp.zeros_like(acc_ref)

        acc_ref[...] += _dot(a_ref[...], b_ref[...], dn)

        @pl.when(k == nk - 1)
        def _():
            o_ref[...] = acc_ref[...].astype(o_ref.dtype)

    return pl.pallas_call(
        body, grid=grid, in_specs=[a_spec, b_spec], out_specs=o_spec,
        out_shape=jax.ShapeDtypeStruct(out_shape, out_dtype),
        scratch_shapes=[] if nk == 1 else [pltpu.VMEM(acc_shape, F32)],
        compiler_params=_params(("parallel", "parallel", "arbitrary"), VMEM_BIG), name=name,
    )(a, b)


TM = 512


def _mm_in(h, wg, name):
    nb = wg.shape[2]
    tn = 512 if nb % 512 == 0 else nb
    per = nb // tn
    return _matmul(
        h, wg, dn=NN, grid=(S // TM, NDEV * per, 1),
        a_spec=pl.BlockSpec((TM, D), lambda i, j, k: (i, 0)),
        b_spec=pl.BlockSpec((None, D, tn), lambda i, j, k: (j // per, 0, j % per)),
        o_spec=pl.BlockSpec((TM, tn), lambda i, j, k: (i, j)),
        out_shape=(S, NDEV * nb), out_dtype=F32, acc_shape=(TM, tn), name=name)


def _mm_in_dx(dz, wg, name):
    nb = wg.shape[2]
    return _matmul(
        dz, wg, dn=NT, grid=(S // TM, D // 512, NDEV),
        a_spec=pl.BlockSpec((TM, nb), lambda i, j, k: (i, k)),
        b_spec=pl.BlockSpec((None, 512, nb), lambda i, j, k: (k, j, 0)),
        o_spec=pl.BlockSpec((TM, 512), lambda i, j, k: (i, j)),
        out_shape=(S, D), out_dtype=F32, acc_shape=(TM, 512), name=name)


def _mm_in_dw(h, dz, nb, name):
    tn = 512 if nb % 512 == 0 else nb
    per = nb // tn
    return _matmul(
        h, dz, dn=TN, grid=(D // TM, NDEV * per, 1),
        a_spec=pl.BlockSpec((S, TM), lambda i, j, k: (0, i)),
        b_spec=pl.BlockSpec((S, tn), lambda i, j, k: (0, j)),
        o_spec=pl.BlockSpec((None, TM, tn), lambda i, j, k: (j // per, i, j % per)),
        out_shape=(NDEV, D, nb), out_dtype=BF16, acc_shape=(TM, tn), name=name)


def _mm_out(yc, w, name):
    return _matmul(
        yc, w, dn=NN, grid=(S // TM, D // 512, 1),
        a_spec=pl.BlockSpec((TM, 2048), lambda i, j, k: (i, 0)),
        b_spec=pl.BlockSpec((2048, 512), lambda i, j, k: (0, j)),
        o_spec=pl.BlockSpec((TM, 512), lambda i, j, k: (i, j)),
        out_shape=(S, D), out_dtype=F32, acc_shape=(TM, 512), name=name)


def _mm_out_dx(dy, w, name):
    return _matmul(
        dy, w, dn=NT, grid=(S // TM, 2048 // 512, 1),
        a_spec=pl.BlockSpec((TM, D), lambda i, j, k: (i, 0)),
        b_spec=pl.BlockSpec((512, D), lambda i, j, k: (j, 0)),
        o_spec=pl.BlockSpec((TM, 512), lambda i, j, k: (i, j)),
        out_shape=(S, 2048), out_dtype=F32, acc_shape=(TM, 512), name=name)


def _mm_out_dw(yc, dy, name):
    return _matmul(
        yc, dy, dn=TN, grid=(2048 // TM, D // 512, 1),
        a_spec=pl.BlockSpec((S, TM), lambda i, j, k: (0, i)),
        b_spec=pl.BlockSpec((S, 512), lambda i, j, k: (0, j)),
        o_spec=pl.BlockSpec((TM, 512), lambda i, j, k: (i, j)),
        out_shape=(2048, D), out_dtype=BF16, acc_shape=(TM, 512), name=name)


def _row_spec(w=D):
    return pl.BlockSpec((TR, w), lambda i: (i, 0))


def _vec_spec(w=D):
    return pl.BlockSpec((1, w), lambda i: (0, 0))


def _rms_stats(x):
    r = lax.rsqrt(jnp.mean(x * x, axis=-1, keepdims=True) + EPS)
    return x * r, r


def _rms_bwd(dn, xhat, r, g):
    dxh = dn * g
    return r * (dxh - xhat * jnp.mean(dxh * xhat, axis=-1, keepdims=True))


def _acc_rows(ref, val, i):
    s = jnp.sum(val, axis=0, keepdims=True)

    @pl.when(i == 0)
    def _():
        ref[...] = s

    @pl.when(i > 0)
    def _():
        ref[...] += s


def _pre0_fwd(x, g):
    def body(x_ref, g_ref, h_ref):
        xhat, _ = _rms_stats(x_ref[...])
        h_ref[...] = (xhat * g_ref[...]).astype(BF16)

    return pl.pallas_call(
        body, grid=(S // TR,), in_specs=[_row_spec(), _vec_spec()], out_specs=_row_spec(),
        out_shape=jax.ShapeDtypeStruct((S, D), BF16), compiler_params=_params(("parallel",)), name="pre0_fwd",
    )(x, g)


def _post0_fwd(x, y0, g_post, g_pre1):
    def body(x_ref, y_ref, gp_ref, g1_ref, x1_ref, h1_ref):
        yhat, _ = _rms_stats(y_ref[...])
        x1 = x_ref[...] + yhat * gp_ref[...]
        x1_ref[...] = x1
        xhat, _ = _rms_stats(x1)
        h1_ref[...] = (xhat * g1_ref[...]).astype(BF16)

    return pl.pallas_call(
        body, grid=(S // TR,), in_specs=[_row_spec(), _row_spec(), _vec_spec(), _vec_spec()],
        out_specs=[_row_spec(), _row_spec()],
        out_shape=[jax.ShapeDtypeStruct((S, D), F32), jax.ShapeDtypeStruct((S, D), BF16)],
        compiler_params=_params(("parallel",)), name="post0_fwd",
    )(x, y0, g_post, g_pre1)


def _post1_bwd(y1, x1, target, g_post):
    def body(y_ref, x1_ref, t_ref, g_ref, loss_ref, dx2_ref, dy_ref, dg_ref):
        i = pl.program_id(0)
        yhat, r = _rms_stats(y_ref[...])
        g = g_ref[...]
        err = x1_ref[...] + yhat * g - t_ref[...]
        part = jnp.sum(jnp.sum(err * err, axis=-1, keepdims=True), axis=0, keepdims=True) * (0.5 / D)
        _acc_rows(loss_ref, jnp.broadcast_to(part, (1, 128)), i)
        dx2 = err * (1.0 / D)
        dx2_ref[...] = dx2
        _acc_rows(dg_ref, dx2 * yhat, i)
        dy_ref[...] = _rms_bwd(dx2, yhat, r, g).astype(BF16)

    return pl.pallas_call(
        body, grid=(S // TR,), in_specs=[_row_spec(), _row_spec(), _row_spec(), _vec_spec()],
        out_specs=[_vec_spec(128), _row_spec(), _row_spec(), _vec_spec()],
        out_shape=[jax.ShapeDtypeStruct((1, 128), F32), jax.ShapeDtypeStruct((S, D), F32),
                   jax.ShapeDtypeStruct((S, D), BF16), jax.ShapeDtypeStruct((1, D), F32)],
        compiler_params=_params(("arbitrary",)), name="post1_bwd",
    )(y1, x1, target, g_post)


def _mid_bwd(dx2, dh1, x1, y0, g_pre1, g_post0):
    def body(dx2_ref, dh_ref, x1_ref, y_ref, g1_ref, gp_ref, dx1_ref, dy_ref, dg1_ref, dgp_ref):
        i = pl.program_id(0)
        xhat, r1 = _rms_stats(x1_ref[...])
        dh = dh_ref[...]
        _acc_rows(dg1_ref, dh * xhat, i)
        dx1 = dx2_ref[...] + _rms_bwd(dh, xhat, r1, g1_ref[...])
        dx1_ref[...] = dx1
        yhat, r0 = _rms_stats(y_ref[...])
        _acc_rows(dgp_ref, dx1 * yhat, i)
        dy_ref[...] = _rms_bwd(dx1, yhat, r0, gp_ref[...]).astype(BF16)

    return pl.pallas_call(
        body, grid=(S // TR,),
        in_specs=[_row_spec(), _row_spec(), _row_spec(), _row_spec(), _vec_spec(), _vec_spec()],
        out_specs=[_row_spec(), _row_spec(), _vec_spec(), _vec_spec()],
        out_shape=[jax.ShapeDtypeStruct((S, D), F32), jax.ShapeDtypeStruct((S, D), BF16),
                   jax.ShapeDtypeStruct((1, D), F32), jax.ShapeDtypeStruct((1, D), F32)],
        compiler_params=_params(("arbitrary",)), name="mid_bwd",
    )(dx2, dh1, x1, y0, g_pre1, g_post0)


def _pre0_bwd(dx1, dh0, x, g):
    def body(dx1_ref, dh_ref, x_ref, g_ref, gx_ref, dg_ref):
        i = pl.program_id(0)
        xhat, r = _rms_stats(x_ref[...])
        dh = dh_ref[...]
        _acc_rows(dg_ref, dh * xhat, i)
        gx_ref[...] = dx1_ref[...] + _rms_bwd(dh, xhat, r, g_ref[...])

    return pl.pallas_call(
        body, grid=(S // TR,), in_specs=[_row_spec(), _row_spec(), _row_spec(), _vec_spec()],
        out_specs=[_row_spec(), _vec_spec()],
        out_shape=[jax.ShapeDtypeStruct((S, D), F32), jax.ShapeDtypeStruct((1, D), F32)],
        compiler_params=_params(("arbitrary",)), name="pre0_bwd",
    )(dx1, dh0, x, g)


POOL_CH = 256


def _pool_apply(a, w, transpose):
    n = a.shape[0]
    row = lax.broadcasted_iota(jnp.int32, a.shape, 0)
    cnt = jnp.minimum(row + 1, w).astype(F32)
    s = a / cnt if transpose else a
    for k in (1, 2, 4, 8):
        if transpose:
            sh = jnp.where(row < n - k, pltpu.roll(s, n - k, 0), 0.0)
        else:
            sh = jnp.where(row >= k, pltpu.roll(s, k, 0), 0.0)
        s = jnp.where(w > k, s + sh, s)
    return s - a if transpose else s / cnt - a


def _pool_fwd(z0, pool_w, pool_scale):
    def body(a_ref, gate_ref, w_ref, sc_ref, out_ref):
        win = jnp.left_shift(2, pl.program_id(0))
        pooled = _pool_apply(a_ref[...], win, False)
        mixed = _dot(pooled.astype(BF16), w_ref[...])
        gate = gate_ref[...]
        out_ref[...] = (mixed * sc_ref[...] * (gate * _sigmoid(gate))).astype(BF16)

    return pl.pallas_call(
        body, grid=(4,),
        in_specs=[pl.BlockSpec((S, POOL_CH), lambda g: (0, g)), pl.BlockSpec((S, POOL_CH), lambda g: (0, 4 + g)),
                  pl.BlockSpec((None, POOL_CH, POOL_CH), lambda g: (g, 0, 0)),
                  pl.BlockSpec((1, POOL_CH), lambda g: (0, g))],
        out_specs=pl.BlockSpec((S, POOL_CH), lambda g: (0, g)),
        out_shape=jax.ShapeDtypeStruct((S, 2048), BF16),
        compiler_params=_params(("parallel",), VMEM_BIG), name="pool_fwd",
    )(z0, z0, pool_w, pool_scale)


def _pool_bwd(z0, dycat, pool_w, pool_scale):
    def body(a_ref, gate_ref, dy_ref, w_ref, sc_ref, da_ref, dgate_ref, dw_ref, dsc_ref):
        win = jnp.left_shift(2, pl.program_id(0))
        pooled = _pool_apply(a_ref[...], win, False).astype(BF16)
        w = w_ref[...]
        mixed = _dot(pooled, w)
        silu, dsilu = _silu_and_grad(gate_ref[...])
        dy = dy_ref[...]
        sc = sc_ref[...]
        dgate_ref[...] = (dy * (mixed * sc) * dsilu).astype(BF16)
        dms = dy * silu
        dsc_ref[...] = jnp.sum(dms * mixed, axis=0, keepdims=True)
        dmixed = (dms * sc).astype(BF16)
        dw_ref[...] = _dot(pooled, dmixed, TN)
        dpooled = _dot(dmixed, w, NT)
        da_ref[...] = _pool_apply(dpooled, win, True).astype(BF16)

    slab = lambda off: pl.BlockSpec((S, POOL_CH), lambda g: (0, off + g))
    return pl.pallas_call(
        body, grid=(4,),
        in_specs=[slab(0), slab(4), slab(0), pl.BlockSpec((None, POOL_CH, POOL_CH), lambda g: (g, 0, 0)),
                  pl.BlockSpec((1, POOL_CH), lambda g: (0, g))],
        out_specs=[slab(0), slab(0), pl.BlockSpec((None, POOL_CH, POOL_CH), lambda g: (g, 0, 0)),
                   pl.BlockSpec((1, POOL_CH), lambda g: (0, g))],
        out_shape=[jax.ShapeDtypeStruct((S, HALF), BF16), jax.ShapeDtypeStruct((S, HALF), BF16),
                   jax.ShapeDtypeStruct((4, POOL_CH, POOL_CH), F32), jax.ShapeDtypeStruct((1, HALF), F32)],
        compiler_params=_params(("parallel",), VMEM_BIG), name="pool_bwd",
    )(z0, z0, dycat, pool_w, pool_scale)


Q_COL, K_COL, V_COL, BG_COL = 2048 // 128, 5120 // 128, 8192 // 128, 11264 // 128
SCALE = HEAD_DIM ** -0.5


def _rope_tables():
    pos = jnp.arange(S, dtype=F32)
    inv_freq = jnp.power(ROPE_THETA, -jnp.arange(0, ROT_DIM, 2, dtype=F32) / ROT_DIM)
    ang = pos[:, None] * inv_freq[None, :]
    cos, sin = jnp.cos(ang), jnp.sin(ang)
    half = ROT_DIM // 2
    zeros = jnp.zeros((S, HEAD_DIM - ROT_DIM), F32)
    c = jnp.concatenate([cos, cos, jnp.ones((S, HEAD_DIM - ROT_DIM), F32)], axis=1)
    a = jnp.concatenate([-sin, jnp.zeros((S, half), F32), zeros], axis=1)
    b = jnp.concatenate([jnp.zeros((S, half), F32), sin, zeros], axis=1)
    return c, a, b


def _rope(t, c, a, b):
    half = ROT_DIM // 2
    return t * c + pltpu.roll(t, HEAD_DIM - half, 1) * a + pltpu.roll(t, half, 1) * b


def _rope_t(d, c, a, b):
    half = ROT_DIM // 2
    return d * c + pltpu.roll(d * a, half, 1) + pltpu.roll(d * b, HEAD_DIM - half, 1)


def _deinterleave(dst, src, dil, cast=None):
    length = S // dil
    for r in range(dil):
        v = src[...] if dil == 1 else src[pl.ds(r, length, stride=dil), :]
        dst[r * length:(r + 1) * length, :] = v if cast is None else v.astype(cast)


def _interleave(dst, src, dil):
    length = S // dil
    for r in range(dil):
        if dil == 1:
            dst[...] = src[...]
        else:
            dst[pl.ds(r, length, stride=dil), :] = src[r * length:(r + 1) * length, :]


def _unit_scores(u, nb, qd, kd):
    o0 = pl.multiple_of(u * BLK, BLK)
    p0 = pl.multiple_of(jnp.maximum(u - 1, 0) * BLK, BLK)
    q = qd[pl.ds(o0, BLK), :]
    row = lax.broadcasted_iota(jnp.int32, (BLK, BLK), 0)
    col = lax.broadcasted_iota(jnp.int32, (BLK, BLK), 1)
    s_own = jnp.where(col <= row, _dot(q, kd[pl.ds(o0, BLK), :], NT) * SCALE, NEG)
    if nb == 1:
        return o0, p0, q, s_own, None
    has_prev = (u % nb) != 0
    s_prev = jnp.where((col >= row) & has_prev, _dot(q, kd[pl.ds(p0, BLK), :], NT) * SCALE, NEG)
    return o0, p0, q, s_own, s_prev


def _attn_group_fwd(dil, q_ref, k_ref, v_ref, tabs, tmp, qd, kd, vd, od, ld, og, lg):
    nb = S // dil // BLK
    c, a, b = tabs
    tmp[...] = _rope(q_ref[...], c, a, b)
    _deinterleave(qd, tmp, dil, BF16)
    tmp[...] = _rope(k_ref[...], c, a, b)
    _deinterleave(kd, tmp, dil, BF16)
    _deinterleave(vd, v_ref, dil, BF16)

    def unit(u, carry):
        o0, p0, _, s_own, s_prev = _unit_scores(u, nb, qd, kd)
        m = jnp.max(s_own, axis=1, keepdims=True)
        if s_prev is not None:
            m = jnp.maximum(m, jnp.max(s_prev, axis=1, keepdims=True))
        p_own = jnp.exp(s_own - m)
        den = jnp.sum(p_own, axis=1, keepdims=True)
        acc = _dot(p_own.astype(BF16), vd[pl.ds(o0, BLK), :])
        if s_prev is not None:
            p_prev = jnp.exp(s_prev - m)
            den = den + jnp.sum(p_prev, axis=1, keepdims=True)
            acc = acc + _dot(p_prev.astype(BF16), vd[pl.ds(p0, BLK), :])
        od[pl.ds(o0, BLK), :] = acc / den
        ld[pl.ds(o0, BLK), :] = jnp.broadcast_to(m + jnp.log(den), (BLK, HEAD_DIM))
        return carry

    lax.fori_loop(0, S // BLK, unit, 0)
    _interleave(og, od, dil)
    _interleave(lg, ld, dil)


def _group_weights(lgs):
    l0, l1, l2 = lgs[0][...], lgs[1][...], lgs[2][...]
    mx = jnp.maximum(l0, jnp.maximum(l1, l2))
    e0, e1, e2 = jnp.exp(l0 - mx), jnp.exp(l1 - mx), jnp.exp(l2 - mx)
    den = e0 + e1 + e2
    return e0 / den, e1 / den, e2 / den


def _head_spec(base, ngroups_axis=True):
    return pl.BlockSpec((S, HEAD_DIM), lambda h, p: (0, base + (p % 3) * 8 + h))


ATTN_SCRATCH_FWD = [
    pltpu.VMEM((S, HEAD_DIM), F32),
    pltpu.VMEM((S, HEAD_DIM), BF16), pltpu.VMEM((S, HEAD_DIM), BF16), pltpu.VMEM((S, HEAD_DIM), BF16),
    pltpu.VMEM((S, HEAD_DIM), F32), pltpu.VMEM((S, HEAD_DIM), F32),
    pltpu.VMEM((S, HEAD_DIM), F32), pltpu.VMEM((S, HEAD_DIM), F32), pltpu.VMEM((S, HEAD_DIM), F32),
    pltpu.VMEM((S, HEAD_DIM), F32), pltpu.VMEM((S, HEAD_DIM), F32), pltpu.VMEM((S, HEAD_DIM), F32),
]


def _attn_fwd(z0, ycat, tabs):
    def body(q_ref, k_ref, v_ref, gate_ref, c_ref, a_ref, b_ref, ycat_ref, out_ref,
             tmp, qd, kd, vd, od, ld, og0, og1, og2, lg0, lg1, lg2):
        del ycat_ref
        p = pl.program_id(1)
        ogs, lgs = (og0, og1, og2), (lg0, lg1, lg2)
        tabs_v = (c_ref[...], a_ref[...], b_ref[...])
        for gi, (_, dil) in enumerate(PATTERNS):
            @pl.when(p == gi)
            def _(gi=gi, dil=dil):
                _attn_group_fwd(dil, q_ref, k_ref, v_ref, tabs_v, tmp, qd, kd, vd, od, ld, ogs[gi], lgs[gi])

        @pl.when(p == 2)
        def _():
            w0, w1, w2 = _group_weights(lgs)
            o = w0 * og0[...] + w1 * og1[...] + w2 * og2[...]
            gate = gate_ref[...]
            out_ref[...] = (o * (gate * _sigmoid(gate))).astype(BF16)

    tab = pl.BlockSpec((S, HEAD_DIM), lambda h, p: (0, 0))
    return pl.pallas_call(
        body, grid=(8, 3),
        in_specs=[_head_spec(Q_COL), _head_spec(K_COL), _head_spec(V_COL),
                  pl.BlockSpec((S, HEAD_DIM), lambda h, p: (0, BG_COL + h)), tab, tab, tab,
                  pl.BlockSpec(memory_space=pl.ANY)],
        out_specs=pl.BlockSpec((S, HEAD_DIM), lambda h, p: (0, 8 + h)),
        out_shape=jax.ShapeDtypeStruct((S, 2048), BF16),
        scratch_shapes=ATTN_SCRATCH_FWD, input_output_aliases={7: 0},
        compiler_params=_params(("parallel", "arbitrary"), VMEM_BIG), name="attn_fwd",
    )(z0, z0, z0, z0, *tabs, ycat)


def _attn_bwd(z0, dycat, tabs):
    def body(q_ref, k_ref, v_ref, gate_ref, dy_ref, c_ref, a_ref, b_ref,
             dq_ref, dk_ref, dv_ref, dbg_ref,
             tmp, qd, kd, vd, od, ld, og0, og1, og2, lg0, lg1, lg2, cg0, cg1, cg2, dod, cd, dqd, dkd, dvd):
        p = pl.program_id(1)
        ogs, lgs, cgs = (og0, og1, og2), (lg0, lg1, lg2), (cg0, cg1, cg2)
        tabs_v = (c_ref[...], a_ref[...], b_ref[...])
        for gi, (_, dil) in enumerate(PATTERNS):
            @pl.when(p == gi)
            def _(gi=gi, dil=dil):
                _attn_group_fwd(dil, q_ref, k_ref, v_ref, tabs_v, tmp, qd, kd, vd, od, ld, ogs[gi], lgs[gi])

        @pl.when(p == 2)
        def _():
            w = _group_weights(lgs)
            o = w[0] * og0[...] + w[1] * og1[...] + w[2] * og2[...]
            silu, dsilu = _silu_and_grad(gate_ref[...])
            dy = dy_ref[...]
            dbg_ref[...] = (dy * o * dsilu).astype(BF16)
            do = dy * silu
            dwbar = jnp.sum(do * o, axis=1, keepdims=True)
            for gi in range(3):
                ogs[gi][...] = w[gi] * do
                cgs[gi][...] = -w[gi] * dwbar

        for gi, (_, dil) in enumerate(PATTERNS):
            @pl.when(p == 3 + gi)
            def _(gi=gi, dil=dil):
                nb = S // dil // BLK
                c, a, b = tabs_v
                tmp[...] = _rope(q_ref[...], c, a, b)
                _deinterleave(qd, tmp, dil, BF16)
                tmp[...] = _rope(k_ref[...], c, a, b)
                _deinterleave(kd, tmp, dil, BF16)
                _deinterleave(vd, v_ref, dil, BF16)
                _deinterleave(dod, ogs[gi], dil, BF16)
                _deinterleave(ld, lgs[gi], dil)
                _deinterleave(cd, cgs[gi], dil)
                dkd[...] = jnp.zeros_like(dkd)
                dvd[...] = jnp.zeros_like(dvd)

                def unit(u, carry):
                    o0, p0, q, s_own, s_prev = _unit_scores(u, nb, qd, kd)
                    lse = ld[pl.ds(o0, BLK), :]
                    cv = cd[pl.ds(o0, BLK), :]
                    do = dod[pl.ds(o0, BLK), :]
                    p_own = jnp.exp(s_own - lse)
                    ds_own = (p_own * (_dot(do, vd[pl.ds(o0, BLK), :], NT) + cv) * SCALE).astype(BF16)
                    dq = _dot(ds_own, kd[pl.ds(o0, BLK), :])
                    dkd[pl.ds(o0, BLK), :] += _dot(ds_own, q, TN)
                    dvd[pl.ds(o0, BLK), :] += _dot(p_own.astype(BF16), do, TN)
                    if s_prev is not None:
                        p_prev = jnp.exp(s_prev - lse)
                        ds_prev = (p_prev * (_dot(do, vd[pl.ds(p0, BLK), :], NT) + cv) * SCALE).astype(BF16)
                        dq = dq + _dot(ds_prev, kd[pl.ds(p0, BLK), :])
                        dkd[pl.ds(p0, BLK), :] += _dot(ds_prev, q, TN)
                        dvd[pl.ds(p0, BLK), :] += _dot(p_prev.astype(BF16), do, TN)
                    dqd[pl.ds(o0, BLK), :] = dq
                    return carry

                lax.fori_loop(0, S // BLK, unit, 0)
                _interleave(tmp, dqd, dil)
                dq_ref[...] = _rope_t(tmp[...], c, a, b).astype(BF16)
                _interleave(tmp, dkd, dil)
                dk_ref[...] = _rope_t(tmp[...], c, a, b).astype(BF16)
                _interleave(tmp, dvd, dil)
                dv_ref[...] = tmp[...].astype(BF16)

    tab = pl.BlockSpec((S, HEAD_DIM), lambda h, p: (0, 0))
    hspec = lambda base: pl.BlockSpec((S, HEAD_DIM), lambda h, p: (0, base + h))
    gspec = pl.BlockSpec((S, HEAD_DIM), lambda h, p: (0, jnp.maximum(p - 3, 0) * 8 + h))
    slab = lambda: pltpu.VMEM((S, HEAD_DIM), F32)
    return pl.pallas_call(
        body, grid=(8, 6),
        in_specs=[_head_spec(Q_COL), _head_spec(K_COL), _head_spec(V_COL), hspec(BG_COL), hspec(8), tab, tab, tab],
        out_specs=[gspec, gspec, gspec, hspec(0)],
        out_shape=[jax.ShapeDtypeStruct((S, 3072), BF16)] * 3 + [jax.ShapeDtypeStruct((S, HALF), BF16)],
        scratch_shapes=ATTN_SCRATCH_FWD + [slab(), slab(), slab(), pltpu.VMEM((S, HEAD_DIM), BF16),
                                           slab(), slab(), slab(), slab()],
        compiler_params=_params(("parallel", "arbitrary"), VMEM_BIG), name="attn_bwd",
    )(z0, z0, z0, z0, dycat, *tabs)


SGU_CH = 256
NCHUNK = TR // 128


def _ln_stats(x):
    mu = jnp.mean(x, axis=-1, keepdims=True)
    xc = x - mu
    r = lax.rsqrt(jnp.mean(xc * xc, axis=-1, keepdims=True) + EPS)
    return xc * r, r


def _ln_bwd(dy, xhat, r, g):
    dxh = dy * g
    return r * (dxh - jnp.mean(dxh, axis=-1, keepdims=True) - xhat * jnp.mean(dxh * xhat, axis=-1, keepdims=True))


def _tril_bf16(w):
    row = lax.broadcasted_iota(jnp.int32, w.shape, 0)
    col = lax.broadcasted_iota(jnp.int32, w.shape, 1)
    return jnp.where(row >= col, w, 0.0).astype(BF16)


def _sgu_gate(vn_s, s_s, w_ref, bb_ref):
    for h in range(4):
        wm = _tril_bf16(w_ref[h])
        bias = bb_ref[h]
        for ch in range(NCHUNK):
            rows, cols = slice(ch * 128, (ch + 1) * 128), slice(h * SGU_CH, (h + 1) * SGU_CH)
            s_s[rows, cols] = _dot(wm, vn_s[rows, cols]) + jnp.concatenate([bias, bias], axis=1)


def _conv_fwd(i, dval_ref, dglu_ref, hval_ref, hglu_ref, cw_ref, cb_ref, xw, dcs):
    halo = hval_ref[...] * _sigmoid(hglu_ref[...])
    xw[0:HALO, :] = jnp.where(i > 0, halo, 0.0)
    xw[HALO:HALO + TR, :] = dval_ref[...] * _sigmoid(dglu_ref[...])
    for rb in range(TR // SUB):
        acc = jnp.broadcast_to(cb_ref[...], (SUB, HALF))
        for k in range(CONV_K):
            acc = acc + cw_ref[k:k + 1, :] * xw[pl.ds(rb * SUB + HALO - (CONV_K - 1) + k, SUB), :]
        dcs[rb * SUB:(rb + 1) * SUB, :] = acc


def _odd_in_specs():
    col = lambda j: pl.BlockSpec((TR, HALF), lambda i, *_: (i, j))
    prev = lambda j: pl.BlockSpec((HALO, HALF), lambda i, *_: (jnp.maximum(i * (TR // HALO) - 1, 0), j))
    return [col(0), col(1), col(2), col(3), col(4), col(5), prev(3), prev(4)]


def _full_spec(shape):
    return pl.BlockSpec(shape, lambda i, *_: (0,) * len(shape))


def _odd_fwd(z1, sgu_g, sgu_b, sgu_w, sgu_bb, conv_w, conv_b, cn_g, cn_b):
    def body(u_ref, v_ref, cg_ref, dval_ref, dglu_ref, dgate_ref, hval_ref, hglu_ref,
             g_ref, b_ref, w_ref, bb_ref, cw_ref, cb_ref, cng_ref, cnb_ref, out_ref, vn_s, s_s, xw, dcs):
        i = pl.program_id(0)
        vhat, _ = _ln_stats(v_ref[...])
        vn_s[...] = (vhat * g_ref[...] + b_ref[...]).astype(BF16)
        _sgu_gate(vn_s, s_s, w_ref, bb_ref)
        cg = cg_ref[...]
        out_ref[:, 0:HALF] = (u_ref[...] * s_s[...] * (cg * _sigmoid(cg))).astype(BF16)
        _conv_fwd(i, dval_ref, dglu_ref, hval_ref, hglu_ref, cw_ref, cb_ref, xw, dcs)
        dhat, _ = _ln_stats(dcs[...])
        dn = dhat * cng_ref[...] + cnb_ref[...]
        dgate = dgate_ref[...]
        out_ref[:, HALF:2 * HALF] = ((dn * _sigmoid(dn)) * (dgate * _sigmoid(dgate))).astype(BF16)

    vec = _full_spec((1, HALF))
    return pl.pallas_call(
        body, grid=(S // TR,),
        in_specs=_odd_in_specs() + [vec, vec, _full_spec((4, 128, 128)), _full_spec((4, 128, 128)),
                                    _full_spec((HALO, HALF)), vec, vec, vec],
        out_specs=pl.BlockSpec((TR, 2048), lambda i: (i, 0)),
        out_shape=jax.ShapeDtypeStruct((S, 2048), BF16),
        scratch_shapes=[pltpu.VMEM((TR, HALF), BF16), pltpu.VMEM((TR, HALF), F32),
                        pltpu.VMEM((HALO + TR, HALF), F32), pltpu.VMEM((TR, HALF), F32)],
        compiler_params=_params(("parallel",), VMEM_BIG), name="odd_fwd",
    )(z1, z1, z1, z1, z1, z1, z1, z1, sgu_g, sgu_b, sgu_w, sgu_bb, conv_w, conv_b, cn_g, cn_b)


def _odd_bwd_a(z1, dycat, sgu_g, sgu_b, sgu_w, sgu_bb, conv_w, conv_b, cn_g, cn_b):
    def body(u_ref, v_ref, cg_ref, dval_ref, dglu_ref, dgate_ref, hval_ref, hglu_ref, dy_ref,
             g_ref, b_ref, w_ref, bb_ref, cw_ref, cb_ref, cng_ref, cnb_ref,
             dz_ref, ddc_ref, dw_ref, dbb_ref, dg_ref, db_ref, dcng_ref, dcnb_ref, dcb_ref,
             vn_s, s_s, xw, dcs, ds_s, dvn_s):
        i = pl.program_id(0)
        vhat, rv = _ln_stats(v_ref[...])
        g = g_ref[...]
        vn_s[...] = (vhat * g + b_ref[...]).astype(BF16)
        _sgu_gate(vn_s, s_s, w_ref, bb_ref)
        silu_c, dsilu_c = _silu_and_grad(cg_ref[...])
        dyc = dy_ref[:, 0:HALF]
        u = u_ref[...]
        s = s_s[...]
        dz_ref[:, 0:HALF] = (dyc * s * silu_c).astype(BF16)
        dz_ref[:, 2 * HALF:3 * HALF] = (dyc * u * s * dsilu_c).astype(BF16)
        ds_s[...] = dyc * u * silu_c

        @pl.when(i == 0)
        def _():
            dw_ref[...] = jnp.zeros_like(dw_ref)
            dbb_ref[...] = jnp.zeros_like(dbb_ref)

        tril = lax.broadcasted_iota(jnp.int32, (128, 128), 0) >= lax.broadcasted_iota(jnp.int32, (128, 128), 1)
        for h in range(4):
            wm = _tril_bf16(w_ref[h])
            for ch in range(NCHUNK):
                rows, cols = slice(ch * 128, (ch + 1) * 128), slice(h * SGU_CH, (h + 1) * SGU_CH)
                ds = ds_s[rows, cols]
                dsb = ds.astype(BF16)
                dw_ref[h] += jnp.where(tril, _dot(dsb, vn_s[rows, cols], NT), 0.0)
                dbb_ref[h] += jnp.broadcast_to(jnp.sum(ds, axis=1, keepdims=True), (128, 128))
                dvn_s[rows, cols] = _dot(wm, dsb, TN)
        dvn = dvn_s[...]
        _acc_rows(dg_ref, dvn * vhat, i)
        _acc_rows(db_ref, dvn, i)
        dz_ref[:, HALF:2 * HALF] = _ln_bwd(dvn, vhat, rv, g).astype(BF16)

        _conv_fwd(i, dval_ref, dglu_ref, hval_ref, hglu_ref, cw_ref, cb_ref, xw, dcs)
        dhat, rd = _ln_stats(dcs[...])
        cng = cng_ref[...]
        silu_n, dsilu_n = _silu_and_grad(dhat * cng + cnb_ref[...])
        silu_g, dsilu_g = _silu_and_grad(dgate_ref[...])
        dyd = dy_ref[:, HALF:2 * HALF]
        dz_ref[:, 5 * HALF:6 * HALF] = (dyd * silu_n * dsilu_g).astype(BF16)
        ddn = dyd * silu_g * dsilu_n
        _acc_rows(dcng_ref, ddn * dhat, i)
        _acc_rows(dcnb_ref, ddn, i)
        ddc = _ln_bwd(ddn, dhat, rd, cng)
        ddc_ref[...] = ddc
        _acc_rows(dcb_ref, ddc, i)

    vec = _full_spec((1, HALF))
    sq = _full_spec((4, 128, 128))
    return pl.pallas_call(
        body, grid=(S // TR,),
        in_specs=_odd_in_specs() + [pl.BlockSpec((TR, 2048), lambda i: (i, 0)),
                                    vec, vec, sq, sq, _full_spec((HALO, HALF)), vec, vec, vec],
        out_specs=[pl.BlockSpec((TR, ODD_IN), lambda i: (i, 0)), pl.BlockSpec((TR, HALF), lambda i: (i, 0)),
                   sq, sq, vec, vec, vec, vec, vec],
        out_shape=[jax.ShapeDtypeStruct((S, ODD_IN), BF16), jax.ShapeDtypeStruct((S, HALF), F32),
                   jax.ShapeDtypeStruct((4, 128, 128), F32), jax.ShapeDtypeStruct((4, 128, 128), F32)]
                  + [jax.ShapeDtypeStruct((1, HALF), F32)] * 5,
        scratch_shapes=[pltpu.VMEM((TR, HALF), BF16), pltpu.VMEM((TR, HALF), F32),
                        pltpu.VMEM((HALO + TR, HALF), F32), pltpu.VMEM((TR, HALF), F32),
                        pltpu.VMEM((TR, HALF), F32), pltpu.VMEM((TR, HALF), F32)],
        compiler_params=_params(("arbitrary",), VMEM_BIG), name="odd_bwd_a",
    )(z1, z1, z1, z1, z1, z1, z1, z1, dycat, sgu_g, sgu_b, sgu_w, sgu_bb, conv_w, conv_b, cn_g, cn_b)


def _odd_bwd_b(z1, ddc, dz1, conv_w):
    nt = S // TR

    def body(dval_ref, dglu_ref, hval_ref, hglu_ref, ddc_ref, hddc_ref, cw_ref, dz_in_ref,
             dz_ref, dcw_ref, xw, dwin, dxs):
        del dz_in_ref
        i, j = pl.program_id(0), pl.program_id(1)
        sg = _sigmoid(dglu_ref[...])
        dval = dval_ref[...]

        @pl.when(j == 0)
        def _():
            halo = hval_ref[...] * _sigmoid(hglu_ref[...])
            xw[0:HALO, :] = jnp.where(i > 0, halo, 0.0)
            xw[HALO:HALO + TR, :] = dval * sg
            dwin[0:TR, :] = ddc_ref[...]
            dwin[TR:TR + HALO, :] = jnp.where(i < nt - 1, hddc_ref[...], 0.0)

            @pl.when(i == 0)
            def _():
                dcw_ref[...] = jnp.zeros_like(dcw_ref)

            for rb in range(TR // SUB):
                acc = jnp.zeros((SUB, HALF), F32)
                for k in range(CONV_K):
                    acc = acc + cw_ref[k:k + 1, :] * dwin[pl.ds(rb * SUB + (CONV_K - 1) - k, SUB), :]
                dxs[rb * SUB:(rb + 1) * SUB, :] = acc
            for k in range(CONV_K):
                acc = jnp.zeros((SUB, HALF), F32)
                for rb in range(TR // SUB):
                    acc = acc + dwin[rb * SUB:(rb + 1) * SUB, :] * xw[pl.ds(rb * SUB + HALO - (CONV_K - 1) + k, SUB), :]
                dcw_ref[k:k + 1, :] += jnp.sum(acc, axis=0, keepdims=True)
            dz_ref[...] = (dxs[...] * sg).astype(BF16)

        @pl.when(j == 1)
        def _():
            dz_ref[...] = (dxs[...] * dval * sg * (1.0 - sg)).astype(BF16)

    col = lambda c: pl.BlockSpec((TR, HALF), lambda i, j: (i, c))
    prev = lambda c: pl.BlockSpec((HALO, HALF), lambda i, j: (jnp.maximum(i * (TR // HALO) - 1, 0), c))
    nxt = pl.BlockSpec((HALO, HALF), lambda i, j: (jnp.minimum((i + 1) * (TR // HALO), S // HALO - 1), 0))
    return pl.pallas_call(
        body, grid=(nt, 2),
        in_specs=[col(3), col(4), prev(3), prev(4), pl.BlockSpec((TR, HALF), lambda i, j: (i, 0)), nxt,
                  _full_spec((HALO, HALF)), pl.BlockSpec(memory_space=pl.ANY)],
        out_specs=[pl.BlockSpec((TR, HALF), lambda i, j: (i, 3 + j)), _full_spec((HALO, HALF))],
        out_shape=[jax.ShapeDtypeStruct((S, ODD_IN), BF16), jax.ShapeDtypeStruct((HALO, HALF), F32)],
        scratch_shapes=[pltpu.VMEM((HALO + TR, HALF), F32), pltpu.VMEM((TR + HALO, HALF), F32),
                        pltpu.VMEM((TR, HALF), F32)],
        input_output_aliases={7: 0},
        compiler_params=_params(("arbitrary", "arbitrary"), VMEM_BIG), name="odd_bwd_b",
    )(z1, z1, z1, z1, ddc, ddc, conv_w, dz1)


def _cast_bf16(w, name):
    r, c = w.shape
    tr = min(r, 256)
    def body(i_ref, o_ref):
        o_ref[...] = i_ref[...].astype(BF16)

    return pl.pallas_call(
        body, grid=(r // tr,), in_specs=[pl.BlockSpec((tr, c), lambda i: (i, 0))],
        out_specs=pl.BlockSpec((tr, c), lambda i: (i, 0)), out_shape=jax.ShapeDtypeStruct((r, c), BF16),
        compiler_params=_params(("parallel",)), name=name,
    )(w)


def _adamw(w, g, m, v):
    m = ADAM_B1 * m + (1.0 - ADAM_B1) * g
    v = ADAM_B2 * v + (1.0 - ADAM_B2) * (g * g)
    m_hat = m / (1.0 - ADAM_B1 ** ADAM_STEP)
    v_hat = v / (1.0 - ADAM_B2 ** ADAM_STEP)
    delta = -ADAM_LR * (m_hat / (jnp.sqrt(v_hat) + ADAM_EPS) + ADAM_WD * w)
    return delta, m, v


def _adam_reduce(parts, w, m, v, name):
    r, c = w.shape
    tr = min(r, 128)

    def body(p_ref, w_ref, m_ref, v_ref, g_ref, d_ref, nm_ref, nv_ref):
        g = p_ref[0].astype(F32)
        for d in range(1, NDEV):
            g = g + p_ref[d].astype(F32)
        g_ref[...] = g
        d_ref[...], nm_ref[...], nv_ref[...] = _adamw(w_ref[...], g, m_ref[...], v_ref[...])

    spec = pl.BlockSpec((tr, c), lambda i: (i, 0))
    return pl.pallas_call(
        body, grid=(r // tr,), in_specs=[pl.BlockSpec((NDEV, tr, c), lambda i: (0, i, 0)), spec, spec, spec],
        out_specs=[spec] * 4, out_shape=[jax.ShapeDtypeStruct((r, c), F32)] * 4,
        compiler_params=_params(("parallel",), VMEM_BIG), name=name,
    )(parts, w, m, v)


def _sum_parts(parts, name):
    r = parts.shape[1]
    tr = 8
    for cand in (512, 256, 128, 64, 32, 16, 8):
        if r % cand == 0:
            tr = cand
            break

    def body(p_ref, o_ref):
        g = p_ref[0]
        for d in range(1, NDEV):
            g = g + p_ref[d]
        o_ref[...] = g

    return pl.pallas_call(
        body, grid=(r // tr,), in_specs=[pl.BlockSpec((NDEV, tr, 128), lambda i: (0, i, 0))],
        out_specs=pl.BlockSpec((tr, 128), lambda i: (i, 0)), out_shape=jax.ShapeDtypeStruct((r, 128), F32),
        compiler_params=_params(("parallel",)), name=name,
    )(parts)


def _adam_plain(w, g, m, v, name):
    r, c = w.shape

    def body(w_ref, g_ref, m_ref, v_ref, d_ref, nm_ref, nv_ref):
        d_ref[...], nm_ref[...], nv_ref[...] = _adamw(w_ref[...], g_ref[...], m_ref[...], v_ref[...])

    spec = pl.BlockSpec((r, c), lambda i: (0, 0))
    return pl.pallas_call(
        body, grid=(1,), in_specs=[spec] * 4, out_specs=[spec] * 3,
        out_shape=[jax.ShapeDtypeStruct((r, c), F32)] * 3,
        compiler_params=_params(("arbitrary",)), name=name,
    )(w, g, m, v)


MASKS = [(mx, my, mc) for mx in (0, 1) for my in (0, 1) for mc in (0, 1)][1:]


def _exchange(arrays, scatter, name):
    nt = len(arrays)
    out_shape = [jax.ShapeDtypeStruct(((NDEV,) + a.shape) if not scatter else a.shape, a.dtype) for a in arrays]

    def body(*refs):
        ins, outs = refs[:nt], refs[nt:2 * nt]
        send_sems, recv_sems, local_sems = refs[2 * nt:]
        x, y, c = lax.axis_index("x"), lax.axis_index("y"), lax.axis_index("c")
        me = 4 * x + 2 * y + c
        copies = []
        for t in range(nt):
            src_own = ins[t].at[me] if scatter else ins[t]
            loc = pltpu.make_async_copy(src_own, outs[t].at[me], local_sems.at[t])
            loc.start()
            copies.append(loc)
            for k, (mx, my, mc) in enumerate(MASKS):
                px, py, pc = (x + mx) % 2, (y + my) % 2, (c + mc) % 2
                peer = 4 * px + 2 * py + pc
                src = ins[t].at[peer] if scatter else ins[t]
                rc = pltpu.make_async_remote_copy(
                    src_ref=src, dst_ref=outs[t].at[me], send_sem=send_sems.at[t, k], recv_sem=recv_sems.at[t, k],
                    device_id=(px, py, pc), device_id_type=MESH)
                rc.start()
                copies.append(rc)
        for cp in copies:
            cp.wait()

    hbm = pl.BlockSpec(memory_space=pl.ANY)
    return pl.pallas_call(
        body, in_specs=[hbm] * nt, out_specs=[hbm] * nt, out_shape=out_shape,
        scratch_shapes=[pltpu.SemaphoreType.DMA((nt, 7)), pltpu.SemaphoreType.DMA((nt, 7)),
                        pltpu.SemaphoreType.DMA((nt,))],
        name=name,
    )(*arrays)


SMALL = {
    "e_pre_norm": ((2048,), None), "e_pool_w": ((4, 256, 256), 1), "e_pool_scale": ((1024,), None),
    "e_post_norm": ((2048,), None), "o_pre_norm": ((2048,), 0), "o_sgu_norm_g": ((1024,), 0),
    "o_sgu_norm_b": ((1024,), 0), "o_sgu_w": ((4, 128, 128), None), "o_sgu_b": ((4, 128), None),
    "o_conv_w": ((31, 1024), 1), "o_conv_b": ((1024,), 0), "o_conv_norm_g": ((1024,), 0),
    "o_conv_norm_b": ((1024,), 0), "o_post_norm": ((2048,), 0),
}
SMALL_SHARDED = [n for n, (_, ax) in SMALL.items() if ax is not None]


def _shard_shape(name):
    shape, ax = SMALL[name]
    if ax is None:
        return shape
    return tuple(s // NDEV if i == ax else s for i, s in enumerate(shape))


def _pack(arrs, row_multiple=1):
    flat = jnp.concatenate([a.reshape(-1) for a in arrs])
    pad = -flat.shape[0] % (128 * row_multiple)
    return jnp.concatenate([flat, jnp.zeros((pad,), F32)]).reshape(-1, 128)


def _unpack(buf, shapes):
    flat = buf.reshape(-1)
    out, off = [], 0
    for shp in shapes:
        n = int(np.prod(shp))
        out.append(flat[off:off + n].reshape(shp))
        off += n
    return out


def _take_shard(full, name, me):
    shape, ax = SMALL[name]
    if ax is None:
        return full
    n = shape[ax] // NDEV
    return lax.dynamic_slice_in_dim(full, me * n, n, axis=ax)


def _local_step(x, target, p, wg):
    row = lambda a: a.reshape(1, -1)
    tabs = _rope_tables()
    pool_w_bf = p["e_pool_w"].astype(BF16)
    sgu_bb = jnp.broadcast_to(p["o_sgu_b"][:, :, None], (4, 128, 128))
    conv_w = jnp.concatenate([p["o_conv_w"], jnp.zeros((HALO - CONV_K, HALF), F32)], axis=0)
    odd_p = (row(p["o_sgu_norm_g"]), row(p["o_sgu_norm_b"]), p["o_sgu_w"], sgu_bb, conv_w,
             row(p["o_conv_b"]), row(p["o_conv_norm_g"]), row(p["o_conv_norm_b"]))
    w_out_e = wg["e_w_out"].reshape(2048, D)
    w_out_o = wg["o_w_out"].reshape(2048, D)

    h0 = _pre0_fwd(x, row(p["e_pre_norm"]))
    z0 = _mm_in(h0, wg["e_w_in"], "mm_z0")
    ycat0 = _pool_fwd(z0, pool_w_bf, row(p["e_pool_scale"]))
    ycat0 = _attn_fwd(z0, ycat0, tabs)
    y0 = _mm_out(ycat0, w_out_e, "mm_y0")
    x1, h1 = _post0_fwd(x, y0, row(p["e_post_norm"]), row(p["o_pre_norm"]))
    z1 = _mm_in(h1, wg["o_w_in"], "mm_z1")
    ycat1 = _odd_fwd(z1, *odd_p)
    y1 = _mm_out(ycat1, w_out_o, "mm_y1")

    loss, dx2, dy1, d_o_post = _post1_bwd(y1, x1, target, row(p["o_post_norm"]))
    g = {"o_post_norm": d_o_post}
    g["o_w_out"] = _mm_out_dw(ycat1, dy1, "mm_dwout1")
    dycat1 = _mm_out_dx(dy1, w_out_o, "mm_dycat1")
    dz1, ddc, d_sgu_w, d_sgu_bb, g["o_sgu_norm_g"], g["o_sgu_norm_b"], g["o_conv_norm_g"], g["o_conv_norm_b"], \
        g["o_conv_b"] = _odd_bwd_a(z1, dycat1, *odd_p)
    dz1, d_conv_w = _odd_bwd_b(z1, ddc, dz1, conv_w)
    g["o_sgu_w"] = d_sgu_w
    g["o_sgu_b"] = d_sgu_bb[:, :, 0]
    g["o_conv_w"] = d_conv_w[:CONV_K]
    g["o_w_in"] = _mm_in_dw(h1, dz1, ODD_IN // NDEV, "mm_dwin1")
    dh1 = _mm_in_dx(dz1, wg["o_w_in"], "mm_dh1")
    dx1, dy0, g["o_pre_norm"], g["e_post_norm"] = _mid_bwd(dx2, dh1, x1, y0, row(p["o_pre_norm"]),
                                                           row(p["e_post_norm"]))
    g["e_w_out"] = _mm_out_dw(ycat0, dy0, "mm_dwout0")
    dycat0 = _mm_out_dx(dy0, w_out_e, "mm_dycat0")
    da_in, da_gate, g["e_pool_w"], g["e_pool_scale"] = _pool_bwd(z0, dycat0, pool_w_bf, row(p["e_pool_scale"]))
    dq, dk, dv, dbg = _attn_bwd(z0, dycat0, tabs)
    dz0 = jnp.concatenate([da_in, da_gate, dq, dk, dv, dbg], axis=1)
    g["e_w_in"] = _mm_in_dw(h0, dz0, EVEN_IN // NDEV, "mm_dwin0")
    dh0 = _mm_in_dx(dz0, wg["e_w_in"], "mm_dh0")
    grad_x, g["e_pre_norm"] = _pre0_bwd(dx1, dh0, x, row(p["e_pre_norm"]))
    return loss[0, 0], grad_x, g


BIG = ("e_w_in", "e_w_out", "o_w_in", "o_w_out")
WEIGHTS = ["e_pre_norm", "e_w_in", "e_pool_w", "e_pool_scale", "e_w_out", "e_post_norm", "o_pre_norm", "o_w_in",
           "o_sgu_norm_g", "o_sgu_norm_b", "o_sgu_w", "o_sgu_b", "o_conv_w", "o_conv_b", "o_conv_norm_g",
           "o_conv_norm_b", "o_w_out", "o_post_norm"]


def kernel(x, e_pre_norm, e_w_in, e_pool_w, e_pool_scale, e_w_out, e_post_norm, o_pre_norm, o_w_in, o_sgu_norm_g, o_sgu_norm_b, o_sgu_w, o_sgu_b, o_conv_w, o_conv_b, o_conv_norm_g, o_conv_norm_b, o_w_out, o_post_norm, loss_target, m_e_pre_norm, m_e_w_in, m_e_pool_w, m_e_pool_scale, m_e_w_out, m_e_post_norm, m_o_pre_norm, m_o_w_in, m_o_sgu_norm_g, m_o_sgu_norm_b, m_o_sgu_w, m_o_sgu_b, m_o_conv_w, m_o_conv_b, m_o_conv_norm_g, m_o_conv_norm_b, m_o_w_out, m_o_post_norm, v_e_pre_norm, v_e_w_in, v_e_pool_w, v_e_pool_scale, v_e_w_out, v_e_post_norm, v_o_pre_norm, v_o_w_in, v_o_sgu_norm_g, v_o_sgu_norm_b, v_o_sgu_w, v_o_sgu_b, v_o_conv_w, v_o_conv_b, v_o_conv_norm_g, v_o_conv_norm_b, v_o_w_out, v_o_post_norm):
    given = dict(locals())
    w = {n: given[n][0] for n in WEIGHTS}
    m = {n: given["m_" + n][0] for n in WEIGHTS}
    v = {n: given["v_" + n][0] for n in WEIGHTS}
    me = 4 * lax.axis_index("x") + 2 * lax.axis_index("y") + lax.axis_index("c")

    big_bf = [_cast_bf16(w[n], "cast_" + n) for n in BIG]
    small_pack = _pack([w[n] for n in SMALL_SHARDED])
    gathered = _exchange(big_bf + [small_pack], False, "gather_weights")
    wg = dict(zip(BIG, gathered[:4]))
    p = {n: w[n] for n in SMALL if SMALL[n][1] is None}
    rows = gathered[4].reshape(NDEV, -1)
    off = 0
    for n in SMALL_SHARDED:
        shp, ax = _shard_shape(n), SMALL[n][1]
        cnt = int(np.prod(shp))
        blk = rows[:, off:off + cnt].reshape((NDEV,) + shp)
        p[n] = jnp.moveaxis(blk, 0, ax).reshape(SMALL[n][0])
        off += cnt

    loss, grad_x, g = _local_step(x[0], loss_target[0], p, wg)
    loss = lax.psum(loss, ("x", "y", "c"))

    g_small_pack = _pack([g[n].reshape(SMALL[n][0]) for n in SMALL], 512)
    recv_big = _exchange([g["e_w_in"], g["e_w_out"].reshape(NDEV, 256, D), g["o_w_in"],
                          g["o_w_out"].reshape(NDEV, 256, D)], True, "scatter_grads")
    recv_small = _exchange([g_small_pack], False, "gather_small_grads")[0]

    grads, deltas, new_m, new_v = {}, {}, {}, {}
    for n, parts in zip(BIG, recv_big):
        grads[n], deltas[n], new_m[n], new_v[n] = _adam_reduce(parts, w[n], m[n], v[n], "adam_" + n)
    g_small = _unpack(_sum_parts(recv_small, "sum_small_grads"), [SMALL[n][0] for n in SMALL])
    for n, gf in zip(SMALL, g_small):
        grads[n] = _take_shard(gf, n, me)
    names = list(SMALL)
    shapes = [_shard_shape(n) for n in names]
    d_pack, m_pack, v_pack = _adam_plain(_pack([w[n] for n in names]), _pack([grads[n] for n in names]),
                                         _pack([m[n] for n in names]), _pack([v[n] for n in names]), "adam_small")
    for n, d_, m_, v_ in zip(names, _unpack(d_pack, shapes), _unpack(m_pack, shapes), _unpack(v_pack, shapes)):
        deltas[n], new_m[n], new_v[n] = d_, m_, v_

    lead = lambda a: a[None]
    return (loss, grad_x[None], *[lead(grads[n]) for n in WEIGHTS], *[lead(deltas[n]) for n in WEIGHTS],
            *[lead(new_m[n]) for n in WEIGHTS], *[lead(new_v[n]) for n in WEIGHTS])
```

```python
import functools

import numpy as np
import jax
import jax.numpy as jnp
from jax import lax
from jax.experimental import pallas as pl
from jax.experimental.pallas import tpu as pltpu

F32 = jnp.float32
BF16 = jnp.bfloat16

S = 2048
D = 2048
NDEV = 8
EPS = 1e-6
NEG = -1e30
HEAD_DIM = 128
ROT_DIM = 32
ROPE_THETA = 500000.0
PATTERNS = ((128, 1), (512, 4), (2048, 16))
BLK = 128
EVEN_IN = 12288
ODD_IN = 6144
HALF = 1024
CONV_K = 31
HALO = 32
TR = 256
SUB = 32

ADAM_LR = 0.001
ADAM_B1 = 0.9
ADAM_B2 = 0.999
ADAM_EPS = 1e-08
ADAM_WD = 0.01
ADAM_STEP = 10

VMEM_BIG = 56 * 1024 * 1024
MESH = pl.DeviceIdType.MESH

NN = (((1,), (0,)), ((), ()))
NT = (((1,), (1,)), ((), ()))
TN = (((0,), (0,)), ((), ()))


def _dot(a, b, dn=NN):
    return lax.dot_general(a, b, dn, preferred_element_type=F32)


def _sigmoid(x):
    return 1.0 / (1.0 + jnp.exp(-x))


def _silu_and_grad(x):
    sg = _sigmoid(x)
    return x * sg, sg * (1.0 + x * (1.0 - sg))


def _params(sem, vmem=None):
    return pltpu.CompilerParams(dimension_semantics=sem, vmem_limit_bytes=vmem)


ANY_SPEC = pl.BlockSpec(memory_space=pl.ANY)


def _matmul(a, b, *, dn, grid, a_spec, b_spec, o_spec, out_shape, out_dtype, acc_shape, name, dep=None):
    nk = grid[2]
    deps = [] if dep is None else [dep]

    def body(a_ref, b_ref, *rest):
        o_ref, acc = rest[len(deps)], rest[len(deps) + 1:]
        if nk == 1:
            o_ref[...] = _dot(a_ref[...], b_ref[...], dn).astype(o_ref.dtype)
            return
        acc_ref = acc[0]
        k = pl.program_id(2)

        @pl.when(k == 0)
        def _():
            acc_ref[...] = jnp.zeros_like(acc_ref)

        acc_ref[...] += _dot(a_ref[...], b_ref[...], dn)

        @pl.when(k == nk - 1)
        def _():
            o_ref[...] = acc_ref[...].astype(o_ref.dtype)

    return pl.pallas_call(
        body, grid=grid, in_specs=[a_spec, b_spec] + [ANY_SPEC] * len(deps), out_specs=o_spec,
        out_shape=jax.ShapeDtypeStruct(out_shape, out_dtype),
        scratch_shapes=[] if nk == 1 else [pltpu.VMEM(acc_shape, F32)],
        compiler_params=_params(("parallel", "parallel", "arbitrary"), VMEM_BIG), name=name,
    )(a, b, *deps)


TM = 512


def _mm_in(h, wg, name):
    nb = wg.shape[2]
    tn = 512 if nb % 512 == 0 else nb
    per = nb // tn
    return _matmul(
        h, wg, dn=NN, grid=(S // TM, NDEV * per, 1),
        a_spec=pl.BlockSpec((TM, D), lambda i, j, k: (i, 0)),
        b_spec=pl.BlockSpec((None, D, tn), lambda i, j, k: (j // per, 0, j % per)),
        o_spec=pl.BlockSpec((TM, tn), lambda i, j, k: (i, j)),
        out_shape=(S, NDEV * nb), out_dtype=F32, acc_shape=(TM, tn), name=name)


def _mm_in_dx(dz, wg, name, dep=None):
    nb = wg.shape[2]
    return _matmul(
        dz, wg, dn=NT, grid=(S // TM, D // 512, NDEV),
        a_spec=pl.BlockSpec((TM, nb), lambda i, j, k: (i, k)),
        b_spec=pl.BlockSpec((None, 512, nb), lambda i, j, k: (k, j, 0)),
        o_spec=pl.BlockSpec((TM, 512), lambda i, j, k: (i, j)),
        out_shape=(S, D), out_dtype=F32, acc_shape=(TM, 512), name=name, dep=dep)


def _mm_in_dw(h, dz, nb, name):
    tn = 512 if nb % 512 == 0 else nb
    per = nb // tn
    return _matmul(
        h, dz, dn=TN, grid=(D // TM, NDEV * per, 1),
        a_spec=pl.BlockSpec((S, TM), lambda i, j, k: (0, i)),
        b_spec=pl.BlockSpec((S, tn), lambda i, j, k: (0, j)),
        o_spec=pl.BlockSpec((None, TM, tn), lambda i, j, k: (j // per, i, j % per)),
        out_shape=(NDEV, D, nb), out_dtype=BF16, acc_shape=(TM, tn), name=name)


def _mm_out(yc, w, name):
    return _matmul(
        yc, w, dn=NN, grid=(S // TM, D // 512, 1),
        a_spec=pl.BlockSpec((TM, 2048), lambda i, j, k: (i, 0)),
        b_spec=pl.BlockSpec((2048, 512), lambda i, j, k: (0, j)),
        o_spec=pl.BlockSpec((TM, 512), lambda i, j, k: (i, j)),
        out_shape=(S, D), out_dtype=F32, acc_shape=(TM, 512), name=name)


def _mm_out_dx(dy, w, name, dep=None):
    return _matmul(
        dy, w, dn=NT, grid=(S // TM, 2048 // 512, 1),
        a_spec=pl.BlockSpec((TM, D), lambda i, j, k: (i, 0)),
        b_spec=pl.BlockSpec((512, D), lambda i, j, k: (j, 0)),
        o_spec=pl.BlockSpec((TM, 512), lambda i, j, k: (i, j)),
        out_shape=(S, 2048), out_dtype=F32, acc_shape=(TM, 512), name=name, dep=dep)


def _mm_out_dw(yc, dy, name):
    return _matmul(
        yc, dy, dn=TN, grid=(2048 // TM, D // 512, 1),
        a_spec=pl.BlockSpec((S, TM), lambda i, j, k: (0, i)),
        b_spec=pl.BlockSpec((S, 512), lambda i, j, k: (0, j)),
        o_spec=pl.BlockSpec((TM, 512), lambda i, j, k: (i, j)),
        out_shape=(2048, D), out_dtype=BF16, acc_shape=(TM, 512), name=name)


def _row_spec(w=D):
    return pl.BlockSpec((TR, w), lambda i: (i, 0))


def _vec_spec(w=D):
    return pl.BlockSpec((1, w), lambda i: (0, 0))


def _rms_stats(x):
    r = lax.rsqrt(jnp.mean(x * x, axis=-1, keepdims=True) + EPS)
    return x * r, r


def _rms_bwd(dn, xhat, r, g):
    dxh = dn * g
    return r * (dxh - xhat * jnp.mean(dxh * xhat, axis=-1, keepdims=True))


def _acc_rows(ref, val, i):
    s = jnp.sum(val, axis=0, keepdims=True)

    @pl.when(i == 0)
    def _():
        ref[...] = s

    @pl.when(i > 0)
    def _():
        ref[...] += s


def _pre0_fwd(x, g, dep=None):
    deps = [] if dep is None else [dep]

    def body(x_ref, g_ref, *rest):
        xhat, _ = _rms_stats(x_ref[...])
        rest[-1][...] = (xhat * g_ref[...]).astype(BF16)

    return pl.pallas_call(
        body, grid=(S // TR,), in_specs=[_row_spec(), _vec_spec()] + [ANY_SPEC] * len(deps), out_specs=_row_spec(),
        out_shape=jax.ShapeDtypeStruct((S, D), BF16), compiler_params=_params(("parallel",)), name="pre0_fwd",
    )(x, g, *deps)


def _post0_fwd(x, y0, g_post, g_pre1):
    def body(x_ref, y_ref, gp_ref, g1_ref, x1_ref, h1_ref):
        yhat, _ = _rms_stats(y_ref[...])
        x1 = x_ref[...] + yhat * gp_ref[...]
        x1_ref[...] = x1
        xhat, _ = _rms_stats(x1)
        h1_ref[...] = (xhat * g1_ref[...]).astype(BF16)

    return pl.pallas_call(
        body, grid=(S // TR,), in_specs=[_row_spec(), _row_spec(), _vec_spec(), _vec_spec()],
        out_specs=[_row_spec(), _row_spec()],
        out_shape=[jax.ShapeDtypeStruct((S, D), F32), jax.ShapeDtypeStruct((S, D), BF16)],
        compiler_params=_params(("parallel",)), name="post0_fwd",
    )(x, y0, g_post, g_pre1)


def _post1_bwd(y1, x1, target, g_post):
    def body(y_ref, x1_ref, t_ref, g_ref, loss_ref, dx2_ref, dy_ref, dg_ref):
        i = pl.program_id(0)
        yhat, r = _rms_stats(y_ref[...])
        g = g_ref[...]
        err = x1_ref[...] + yhat * g - t_ref[...]
        part = jnp.sum(jnp.sum(err * err, axis=-1, keepdims=True), axis=0, keepdims=True) * (0.5 / D)
        _acc_rows(loss_ref, jnp.broadcast_to(part, (1, 128)), i)
        dx2 = err * (1.0 / D)
        dx2_ref[...] = dx2
        _acc_rows(dg_ref, dx2 * yhat, i)
        dy_ref[...] = _rms_bwd(dx2, yhat, r, g).astype(BF16)

    return pl.pallas_call(
        body, grid=(S // TR,), in_specs=[_row_spec(), _row_spec(), _row_spec(), _vec_spec()],
        out_specs=[_vec_spec(128), _row_spec(), _row_spec(), _vec_spec()],
        out_shape=[jax.ShapeDtypeStruct((1, 128), F32), jax.ShapeDtypeStruct((S, D), F32),
                   jax.ShapeDtypeStruct((S, D), BF16), jax.ShapeDtypeStruct((1, D), F32)],
        compiler_params=_params(("arbitrary",)), name="post1_bwd",
    )(y1, x1, target, g_post)


def _mid_bwd(dx2, dh1, x1, y0, g_pre1, g_post0):
    def body(dx2_ref, dh_ref, x1_ref, y_ref, g1_ref, gp_ref, dx1_ref, dy_ref, dg1_ref, dgp_ref):
        i = pl.program_id(0)
        xhat, r1 = _rms_stats(x1_ref[...])
        dh = dh_ref[...]
        _acc_rows(dg1_ref, dh * xhat, i)
        dx1 = dx2_ref[...] + _rms_bwd(dh, xhat, r1, g1_ref[...])
        dx1_ref[...] = dx1
        yhat, r0 = _rms_stats(y_ref[...])
        _acc_rows(dgp_ref, dx1 * yhat, i)
        dy_ref[...] = _rms_bwd(dx1, yhat, r0, gp_ref[...]).astype(BF16)

    return pl.pallas_call(
        body, grid=(S // TR,),
        in_specs=[_row_spec(), _row_spec(), _row_spec(), _row_spec(), _vec_spec(), _vec_spec()],
        out_specs=[_row_spec(), _row_spec(), _vec_spec(), _vec_spec()],
        out_shape=[jax.ShapeDtypeStruct((S, D), F32), jax.ShapeDtypeStruct((S, D), BF16),
                   jax.ShapeDtypeStruct((1, D), F32), jax.ShapeDtypeStruct((1, D), F32)],
        compiler_params=_params(("arbitrary",)), name="mid_bwd",
    )(dx2, dh1, x1, y0, g_pre1, g_post0)


def _pre0_bwd(dx1, dh0, x, g):
    def body(dx1_ref, dh_ref, x_ref, g_ref, gx_ref, dg_ref):
        i = pl.program_id(0)
        xhat, r = _rms_stats(x_ref[...])
        dh = dh_ref[...]
        _acc_rows(dg_ref, dh * xhat, i)
        gx_ref[...] = dx1_ref[...] + _rms_bwd(dh, xhat, r, g_ref[...])

    return pl.pallas_call(
        body, grid=(S // TR,), in_specs=[_row_spec(), _row_spec(), _row_spec(), _vec_spec()],
        out_specs=[_row_spec(), _vec_spec()],
        out_shape=[jax.ShapeDtypeStruct((S, D), F32), jax.ShapeDtypeStruct((1, D), F32)],
        compiler_params=_params(("arbitrary",)), name="pre0_bwd",
    )(dx1, dh0, x, g)


POOL_CH = 256


def _pool_apply(a, w, transpose):
    n = a.shape[0]
    row = lax.broadcasted_iota(jnp.int32, a.shape, 0)
    cnt = jnp.minimum(row + 1, w).astype(F32)
    s = a / cnt if transpose else a
    for k in (1, 2, 4, 8):
        if transpose:
            sh = jnp.where(row < n - k, pltpu.roll(s, n - k, 0), 0.0)
        else:
            sh = jnp.where(row >= k, pltpu.roll(s, k, 0), 0.0)
        s = jnp.where(w > k, s + sh, s)
    return s - a if transpose else s / cnt - a


def _pool_fwd(z0, pool_w, pool_scale):
    def body(a_ref, gate_ref, w_ref, sc_ref, out_ref):
        win = jnp.left_shift(2, pl.program_id(0))
        pooled = _pool_apply(a_ref[...], win, False)
        mixed = _dot(pooled.astype(BF16), w_ref[...])
        gate = gate_ref[...]
        out_ref[...] = (mixed * sc_ref[...] * (gate * _sigmoid(gate))).astype(BF16)

    return pl.pallas_call(
        body, grid=(4,),
        in_specs=[pl.BlockSpec((S, POOL_CH), lambda g: (0, g)), pl.BlockSpec((S, POOL_CH), lambda g: (0, 4 + g)),
                  pl.BlockSpec((None, POOL_CH, POOL_CH), lambda g: (g, 0, 0)),
                  pl.BlockSpec((1, POOL_CH), lambda g: (0, g))],
        out_specs=pl.BlockSpec((S, POOL_CH), lambda g: (0, g)),
        out_shape=jax.ShapeDtypeStruct((S, 2048), BF16),
        compiler_params=_params(("parallel",), VMEM_BIG), name="pool_fwd",
    )(z0, z0, pool_w, pool_scale)


def _pool_bwd(z0, dycat, pool_w, pool_scale):
    def body(a_ref, gate_ref, dy_ref, w_ref, sc_ref, da_ref, dgate_ref, dw_ref, dsc_ref):
        win = jnp.left_shift(2, pl.program_id(0))
        pooled = _pool_apply(a_ref[...], win, False).astype(BF16)
        w = w_ref[...]
        mixed = _dot(pooled, w)
        silu, dsilu = _silu_and_grad(gate_ref[...])
        dy = dy_ref[...]
        sc = sc_ref[...]
        dgate_ref[...] = (dy * (mixed * sc) * dsilu).astype(BF16)
        dms = dy * silu
        dsc_ref[...] = jnp.sum(dms * mixed, axis=0, keepdims=True)
        dmixed = (dms * sc).astype(BF16)
        dw_ref[...] = _dot(pooled, dmixed, TN)
        dpooled = _dot(dmixed, w, NT)
        da_ref[...] = _pool_apply(dpooled, win, True).astype(BF16)

    slab = lambda off: pl.BlockSpec((S, POOL_CH), lambda g: (0, off + g))
    return pl.pallas_call(
        body, grid=(4,),
        in_specs=[slab(0), slab(4), slab(0), pl.BlockSpec((None, POOL_CH, POOL_CH), lambda g: (g, 0, 0)),
                  pl.BlockSpec((1, POOL_CH), lambda g: (0, g))],
        out_specs=[slab(0), slab(0), pl.BlockSpec((None, POOL_CH, POOL_CH), lambda g: (g, 0, 0)),
                   pl.BlockSpec((1, POOL_CH), lambda g: (0, g))],
        out_shape=[jax.ShapeDtypeStruct((S, HALF), BF16), jax.ShapeDtypeStruct((S, HALF), BF16),
                   jax.ShapeDtypeStruct((4, POOL_CH, POOL_CH), F32), jax.ShapeDtypeStruct((1, HALF), F32)],
        compiler_params=_params(("parallel",), VMEM_BIG), name="pool_bwd",
    )(z0, z0, dycat, pool_w, pool_scale)


Q_COL, K_COL, V_COL, BG_COL = 2048 // 128, 5120 // 128, 8192 // 128, 11264 // 128
SCALE = HEAD_DIM ** -0.5


def _rope_tables():
    pos = jnp.arange(S, dtype=F32)
    inv_freq = jnp.power(ROPE_THETA, -jnp.arange(0, ROT_DIM, 2, dtype=F32) / ROT_DIM)
    ang = pos[:, None] * inv_freq[None, :]
    cos, sin = jnp.cos(ang), jnp.sin(ang)
    half = ROT_DIM // 2
    zeros = jnp.zeros((S, HEAD_DIM - ROT_DIM), F32)
    c = jnp.concatenate([cos, cos, jnp.ones((S, HEAD_DIM - ROT_DIM), F32)], axis=1)
    a = jnp.concatenate([-sin, jnp.zeros((S, half), F32), zeros], axis=1)
    b = jnp.concatenate([jnp.zeros((S, half), F32), sin, zeros], axis=1)
    return c, a, b


def _rope(t, c, a, b):
    half = ROT_DIM // 2
    return t * c + pltpu.roll(t, HEAD_DIM - half, 1) * a + pltpu.roll(t, half, 1) * b


def _rope_t(d, c, a, b):
    half = ROT_DIM // 2
    return d * c + pltpu.roll(d * a, half, 1) + pltpu.roll(d * b, HEAD_DIM - half, 1)


def _deinterleave(dst, src, dil, cast=None):
    length = S // dil
    for r in range(dil):
        v = src[...] if dil == 1 else src[pl.ds(r, length, stride=dil), :]
        dst[r * length:(r + 1) * length, :] = v if cast is None else v.astype(cast)


def _interleave(dst, src, dil):
    length = S // dil
    for r in range(dil):
        if dil == 1:
            dst[...] = src[...]
        else:
            dst[pl.ds(r, length, stride=dil), :] = src[r * length:(r + 1) * length, :]


def _unit_scores(u, nb, qd, kd):
    o0 = pl.multiple_of(u * BLK, BLK)
    p0 = pl.multiple_of(jnp.maximum(u - 1, 0) * BLK, BLK)
    q = qd[pl.ds(o0, BLK), :]
    row = lax.broadcasted_iota(jnp.int32, (BLK, BLK), 0)
    col = lax.broadcasted_iota(jnp.int32, (BLK, BLK), 1)
    s_own = jnp.where(col <= row, _dot(q, kd[pl.ds(o0, BLK), :], NT) * SCALE, NEG)
    if nb == 1:
        return o0, p0, q, s_own, None
    has_prev = (u % nb) != 0
    s_prev = jnp.where((col >= row) & has_prev, _dot(q, kd[pl.ds(p0, BLK), :], NT) * SCALE, NEG)
    return o0, p0, q, s_own, s_prev


def _attn_group_fwd(dil, q_ref, k_ref, v_ref, tabs, tmp, qd, kd, vd, od, ld, og, lg):
    nb = S // dil // BLK
    c, a, b = tabs
    tmp[...] = _rope(q_ref[...], c, a, b)
    _deinterleave(qd, tmp, dil, BF16)
    tmp[...] = _rope(k_ref[...], c, a, b)
    _deinterleave(kd, tmp, dil, BF16)
    _deinterleave(vd, v_ref, dil, BF16)

    def unit(u, carry):
        o0, p0, _, s_own, s_prev = _unit_scores(u, nb, qd, kd)
        m = jnp.max(s_own, axis=1, keepdims=True)
        if s_prev is not None:
            m = jnp.maximum(m, jnp.max(s_prev, axis=1, keepdims=True))
        p_own = jnp.exp(s_own - m)
        den = jnp.sum(p_own, axis=1, keepdims=True)
        acc = _dot(p_own.astype(BF16), vd[pl.ds(o0, BLK), :])
        if s_prev is not None:
            p_prev = jnp.exp(s_prev - m)
            den = den + jnp.sum(p_prev, axis=1, keepdims=True)
            acc = acc + _dot(p_prev.astype(BF16), vd[pl.ds(p0, BLK), :])
        od[pl.ds(o0, BLK), :] = acc / den
        ld[pl.ds(o0, BLK), :] = jnp.broadcast_to(m + jnp.log(den), (BLK, HEAD_DIM))
        return carry

    lax.fori_loop(0, S // BLK, unit, 0)
    _interleave(og, od, dil)
    _interleave(lg, ld, dil)


def _group_weights(lgs):
    l0, l1, l2 = lgs[0][...], lgs[1][...], lgs[2][...]
    mx = jnp.maximum(l0, jnp.maximum(l1, l2))
    e0, e1, e2 = jnp.exp(l0 - mx), jnp.exp(l1 - mx), jnp.exp(l2 - mx)
    den = e0 + e1 + e2
    return e0 / den, e1 / den, e2 / den


def _head_spec(base, ngroups_axis=True):
    return pl.BlockSpec((S, HEAD_DIM), lambda h, p: (0, base + (p % 3) * 8 + h))


ATTN_SCRATCH_FWD = [
    pltpu.VMEM((S, HEAD_DIM), F32),
    pltpu.VMEM((S, HEAD_DIM), BF16), pltpu.VMEM((S, HEAD_DIM), BF16), pltpu.VMEM((S, HEAD_DIM), BF16),
    pltpu.VMEM((S, HEAD_DIM), F32), pltpu.VMEM((S, HEAD_DIM), F32),
    pltpu.VMEM((S, HEAD_DIM), F32), pltpu.VMEM((S, HEAD_DIM), F32), pltpu.VMEM((S, HEAD_DIM), F32),
    pltpu.VMEM((S, HEAD_DIM), F32), pltpu.VMEM((S, HEAD_DIM), F32), pltpu.VMEM((S, HEAD_DIM), F32),
]


def _attn_fwd(z0, ycat, tabs):
    def body(q_ref, k_ref, v_ref, gate_ref, c_ref, a_ref, b_ref, ycat_ref, out_ref,
             tmp, qd, kd, vd, od, ld, og0, og1, og2, lg0, lg1, lg2):
        del ycat_ref
        p = pl.program_id(1)
        ogs, lgs = (og0, og1, og2), (lg0, lg1, lg2)
        tabs_v = (c_ref[...], a_ref[...], b_ref[...])
        for gi, (_, dil) in enumerate(PATTERNS):
            @pl.when(p == gi)
            def _(gi=gi, dil=dil):
                _attn_group_fwd(dil, q_ref, k_ref, v_ref, tabs_v, tmp, qd, kd, vd, od, ld, ogs[gi], lgs[gi])

        @pl.when(p == 2)
        def _():
            w0, w1, w2 = _group_weights(lgs)
            o = w0 * og0[...] + w1 * og1[...] + w2 * og2[...]
            gate = gate_ref[...]
            out_ref[...] = (o * (gate * _sigmoid(gate))).astype(BF16)

    tab = pl.BlockSpec((S, HEAD_DIM), lambda h, p: (0, 0))
    return pl.pallas_call(
        body, grid=(8, 3),
        in_specs=[_head_spec(Q_COL), _head_spec(K_COL), _head_spec(V_COL),
                  pl.BlockSpec((S, HEAD_DIM), lambda h, p: (0, BG_COL + h)), tab, tab, tab,
                  pl.BlockSpec(memory_space=pl.ANY)],
        out_specs=pl.BlockSpec((S, HEAD_DIM), lambda h, p: (0, 8 + h)),
        out_shape=jax.ShapeDtypeStruct((S, 2048), BF16),
        scratch_shapes=ATTN_SCRATCH_FWD, input_output_aliases={7: 0},
        compiler_params=_params(("parallel", "arbitrary"), VMEM_BIG), name="attn_fwd",
    )(z0, z0, z0, z0, *tabs, ycat)


def _attn_bwd(z0, dycat, tabs):
    def body(q_ref, k_ref, v_ref, gate_ref, dy_ref, c_ref, a_ref, b_ref,
             dq_ref, dk_ref, dv_ref, dbg_ref,
             tmp, qd, kd, vd, od, ld, og0, og1, og2, lg0, lg1, lg2, cg0, cg1, cg2, dod, cd, dqd, dkd, dvd):
        p = pl.program_id(1)
        ogs, lgs, cgs = (og0, og1, og2), (lg0, lg1, lg2), (cg0, cg1, cg2)
        tabs_v = (c_ref[...], a_ref[...], b_ref[...])
        for gi, (_, dil) in enumerate(PATTERNS):
            @pl.when(p == gi)
            def _(gi=gi, dil=dil):
                _attn_group_fwd(dil, q_ref, k_ref, v_ref, tabs_v, tmp, qd, kd, vd, od, ld, ogs[gi], lgs[gi])

        @pl.when(p == 2)
        def _():
            w = _group_weights(lgs)
            o = w[0] * og0[...] + w[1] * og1[...] + w[2] * og2[...]
            silu, dsilu = _silu_and_grad(gate_ref[...])
            dy = dy_ref[...]
            dbg_ref[...] = (dy * o * dsilu).astype(BF16)
            do = dy * silu
            dwbar = jnp.sum(do * o, axis=1, keepdims=True)
            for gi in range(3):
                ogs[gi][...] = w[gi] * do
                cgs[gi][...] = -w[gi] * dwbar

        for gi, (_, dil) in enumerate(PATTERNS):
            @pl.when(p == 3 + gi)
            def _(gi=gi, dil=dil):
                nb = S // dil // BLK
                c, a, b = tabs_v
                tmp[...] = _rope(q_ref[...], c, a, b)
                _deinterleave(qd, tmp, dil, BF16)
                tmp[...] = _rope(k_ref[...], c, a, b)
                _deinterleave(kd, tmp, dil, BF16)
                _deinterleave(vd, v_ref, dil, BF16)
                _deinterleave(dod, ogs[gi], dil, BF16)
                _deinterleave(ld, lgs[gi], dil)
                _deinterleave(cd, cgs[gi], dil)
                dkd[...] = jnp.zeros_like(dkd)
                dvd[...] = jnp.zeros_like(dvd)

                def unit(u, carry):
                    o0, p0, q, s_own, s_prev = _unit_scores(u, nb, qd, kd)
                    lse = ld[pl.ds(o0, BLK), :]
                    cv = cd[pl.ds(o0, BLK), :]
                    do = dod[pl.ds(o0, BLK), :]
                    p_own = jnp.exp(s_own - lse)
                    ds_own = (p_own * (_dot(do, vd[pl.ds(o0, BLK), :], NT) + cv) * SCALE).astype(BF16)
                    dq = _dot(ds_own, kd[pl.ds(o0, BLK), :])
                    dkd[pl.ds(o0, BLK), :] += _dot(ds_own, q, TN)
                    dvd[pl.ds(o0, BLK), :] += _dot(p_own.astype(BF16), do, TN)
                    if s_prev is not None:
                        p_prev = jnp.exp(s_prev - lse)
                        ds_prev = (p_prev * (_dot(do, vd[pl.ds(p0, BLK), :], NT) + cv) * SCALE).astype(BF16)
                        dq = dq + _dot(ds_prev, kd[pl.ds(p0, BLK), :])
                        dkd[pl.ds(p0, BLK), :] += _dot(ds_prev, q, TN)
                        dvd[pl.ds(p0, BLK), :] += _dot(p_prev.astype(BF16), do, TN)
                    dqd[pl.ds(o0, BLK), :] = dq
                    return carry

                lax.fori_loop(0, S // BLK, unit, 0)
                _interleave(tmp, dqd, dil)
                dq_ref[...] = _rope_t(tmp[...], c, a, b).astype(BF16)
                _interleave(tmp, dkd, dil)
                dk_ref[...] = _rope_t(tmp[...], c, a, b).astype(BF16)
                _interleave(tmp, dvd, dil)
                dv_ref[...] = tmp[...].astype(BF16)

    tab = pl.BlockSpec((S, HEAD_DIM), lambda h, p: (0, 0))
    hspec = lambda base: pl.BlockSpec((S, HEAD_DIM), lambda h, p: (0, base + h))
    gspec = pl.BlockSpec((S, HEAD_DIM), lambda h, p: (0, jnp.maximum(p - 3, 0) * 8 + h))
    slab = lambda: pltpu.VMEM((S, HEAD_DIM), F32)
    return pl.pallas_call(
        body, grid=(8, 6),
        in_specs=[_head_spec(Q_COL), _head_spec(K_COL), _head_spec(V_COL), hspec(BG_COL), hspec(8), tab, tab, tab],
        out_specs=[gspec, gspec, gspec, hspec(0)],
        out_shape=[jax.ShapeDtypeStruct((S, 3072), BF16)] * 3 + [jax.ShapeDtypeStruct((S, HALF), BF16)],
        scratch_shapes=ATTN_SCRATCH_FWD + [slab(), slab(), slab(), pltpu.VMEM((S, HEAD_DIM), BF16),
                                           slab(), slab(), slab(), slab()],
        compiler_params=_params(("parallel", "arbitrary"), VMEM_BIG), name="attn_bwd",
    )(z0, z0, z0, z0, dycat, *tabs)


SGU_CH = 256
NCHUNK = TR // 128


def _ln_stats(x):
    mu = jnp.mean(x, axis=-1, keepdims=True)
    xc = x - mu
    r = lax.rsqrt(jnp.mean(xc * xc, axis=-1, keepdims=True) + EPS)
    return xc * r, r


def _ln_bwd(dy, xhat, r, g):
    dxh = dy * g
    return r * (dxh - jnp.mean(dxh, axis=-1, keepdims=True) - xhat * jnp.mean(dxh * xhat, axis=-1, keepdims=True))


def _tril_bf16(w):
    row = lax.broadcasted_iota(jnp.int32, w.shape, 0)
    col = lax.broadcasted_iota(jnp.int32, w.shape, 1)
    return jnp.where(row >= col, w, 0.0).astype(BF16)


def _sgu_gate(vn_s, s_s, w_ref, bb_ref):
    for h in range(4):
        wm = _tril_bf16(w_ref[h])
        bias = bb_ref[h]
        for ch in range(NCHUNK):
            rows, cols = slice(ch * 128, (ch + 1) * 128), slice(h * SGU_CH, (h + 1) * SGU_CH)
            s_s[rows, cols] = _dot(wm, vn_s[rows, cols]) + jnp.concatenate([bias, bias], axis=1)


def _conv_fwd(i, dval_ref, dglu_ref, hval_ref, hglu_ref, cw_ref, cb_ref, xw, dcs):
    halo = hval_ref[...] * _sigmoid(hglu_ref[...])
    xw[0:HALO, :] = jnp.where(i > 0, halo, 0.0)
    xw[HALO:HALO + TR, :] = dval_ref[...] * _sigmoid(dglu_ref[...])
    for rb in range(TR // SUB):
        acc = jnp.broadcast_to(cb_ref[...], (SUB, HALF))
        for k in range(CONV_K):
            acc = acc + cw_ref[k:k + 1, :] * xw[pl.ds(rb * SUB + HALO - (CONV_K - 1) + k, SUB), :]
        dcs[rb * SUB:(rb + 1) * SUB, :] = acc


def _odd_in_specs():
    col = lambda j: pl.BlockSpec((TR, HALF), lambda i, *_: (i, j))
    prev = lambda j: pl.BlockSpec((HALO, HALF), lambda i, *_: (jnp.maximum(i * (TR // HALO) - 1, 0), j))
    return [col(0), col(1), col(2), col(3), col(4), col(5), prev(3), prev(4)]


def _full_spec(shape):
    return pl.BlockSpec(shape, lambda i, *_: (0,) * len(shape))


def _odd_fwd(z1, sgu_g, sgu_b, sgu_w, sgu_bb, conv_w, conv_b, cn_g, cn_b):
    def body(u_ref, v_ref, cg_ref, dval_ref, dglu_ref, dgate_ref, hval_ref, hglu_ref,
             g_ref, b_ref, w_ref, bb_ref, cw_ref, cb_ref, cng_ref, cnb_ref, out_ref, vn_s, s_s, xw, dcs):
        i = pl.program_id(0)
        vhat, _ = _ln_stats(v_ref[...])
        vn_s[...] = (vhat * g_ref[...] + b_ref[...]).astype(BF16)
        _sgu_gate(vn_s, s_s, w_ref, bb_ref)
        cg = cg_ref[...]
        out_ref[:, 0:HALF] = (u_ref[...] * s_s[...] * (cg * _sigmoid(cg))).astype(BF16)
        _conv_fwd(i, dval_ref, dglu_ref, hval_ref, hglu_ref, cw_ref, cb_ref, xw, dcs)
        dhat, _ = _ln_stats(dcs[...])
        dn = dhat * cng_ref[...] + cnb_ref[...]
        dgate = dgate_ref[...]
        out_ref[:, HALF:2 * HALF] = ((dn * _sigmoid(dn)) * (dgate * _sigmoid(dgate))).astype(BF16)

    vec = _full_spec((1, HALF))
    return pl.pallas_call(
        body, grid=(S // TR,),
        in_specs=_odd_in_specs() + [vec, vec, _full_spec((4, 128, 128)), _full_spec((4, 128, 128)),
                                    _full_spec((HALO, HALF)), vec, vec, vec],
        out_specs=pl.BlockSpec((TR, 2048), lambda i: (i, 0)),
        out_shape=jax.ShapeDtypeStruct((S, 2048), BF16),
        scratch_shapes=[pltpu.VMEM((TR, HALF), BF16), pltpu.VMEM((TR, HALF), F32),
                        pltpu.VMEM((HALO + TR, HALF), F32), pltpu.VMEM((TR, HALF), F32)],
        compiler_params=_params(("parallel",), VMEM_BIG), name="odd_fwd",
    )(z1, z1, z1, z1, z1, z1, z1, z1, sgu_g, sgu_b, sgu_w, sgu_bb, conv_w, conv_b, cn_g, cn_b)


def _odd_bwd_a(z1, dycat, sgu_g, sgu_b, sgu_w, sgu_bb, conv_w, conv_b, cn_g, cn_b):
    def body(u_ref, v_ref, cg_ref, dval_ref, dglu_ref, dgate_ref, hval_ref, hglu_ref, dy_ref,
             g_ref, b_ref, w_ref, bb_ref, cw_ref, cb_ref, cng_ref, cnb_ref,
             dz_ref, ddc_ref, dw_ref, dbb_ref, dg_ref, db_ref, dcng_ref, dcnb_ref, dcb_ref,
             vn_s, s_s, xw, dcs, ds_s, dvn_s):
        i = pl.program_id(0)
        vhat, rv = _ln_stats(v_ref[...])
        g = g_ref[...]
        vn_s[...] = (vhat * g + b_ref[...]).astype(BF16)
        _sgu_gate(vn_s, s_s, w_ref, bb_ref)
        silu_c, dsilu_c = _silu_and_grad(cg_ref[...])
        dyc = dy_ref[:, 0:HALF]
        u = u_ref[...]
        s = s_s[...]
        dz_ref[:, 0:HALF] = (dyc * s * silu_c).astype(BF16)
        dz_ref[:, 2 * HALF:3 * HALF] = (dyc * u * s * dsilu_c).astype(BF16)
        ds_s[...] = dyc * u * silu_c

        @pl.when(i == 0)
        def _():
            dw_ref[...] = jnp.zeros_like(dw_ref)
            dbb_ref[...] = jnp.zeros_like(dbb_ref)

        tril = lax.broadcasted_iota(jnp.int32, (128, 128), 0) >= lax.broadcasted_iota(jnp.int32, (128, 128), 1)
        for h in range(4):
            wm = _tril_bf16(w_ref[h])
            for ch in range(NCHUNK):
                rows, cols = slice(ch * 128, (ch + 1) * 128), slice(h * SGU_CH, (h + 1) * SGU_CH)
                ds = ds_s[rows, cols]
                dsb = ds.astype(BF16)
                dw_ref[h] += jnp.where(tril, _dot(dsb, vn_s[rows, cols], NT), 0.0)
                dbb_ref[h] += jnp.broadcast_to(jnp.sum(ds, axis=1, keepdims=True), (128, 128))
                dvn_s[rows, cols] = _dot(wm, dsb, TN)
        dvn = dvn_s[...]
        _acc_rows(dg_ref, dvn * vhat, i)
        _acc_rows(db_ref, dvn, i)
        dz_ref[:, HALF:2 * HALF] = _ln_bwd(dvn, vhat, rv, g).astype(BF16)

        _conv_fwd(i, dval_ref, dglu_ref, hval_ref, hglu_ref, cw_ref, cb_ref, xw, dcs)
        dhat, rd = _ln_stats(dcs[...])
        cng = cng_ref[...]
        silu_n, dsilu_n = _silu_and_grad(dhat * cng + cnb_ref[...])
        silu_g, dsilu_g = _silu_and_grad(dgate_ref[...])
        dyd = dy_ref[:, HALF:2 * HALF]
        dz_ref[:, 5 * HALF:6 * HALF] = (dyd * silu_n * dsilu_g).astype(BF16)
        ddn = dyd * silu_g * dsilu_n
        _acc_rows(dcng_ref, ddn * dhat, i)
        _acc_rows(dcnb_ref, ddn, i)
        ddc = _ln_bwd(ddn, dhat, rd, cng)
        ddc_ref[...] = ddc
        _acc_rows(dcb_ref, ddc, i)

    vec = _full_spec((1, HALF))
    sq = _full_spec((4, 128, 128))
    return pl.pallas_call(
        body, grid=(S // TR,),
        in_specs=_odd_in_specs() + [pl.BlockSpec((TR, 2048), lambda i: (i, 0)),
                                    vec, vec, sq, sq, _full_spec((HALO, HALF)), vec, vec, vec],
        out_specs=[pl.BlockSpec((TR, ODD_IN), lambda i: (i, 0)), pl.BlockSpec((TR, HALF), lambda i: (i, 0)),
                   sq, sq, vec, vec, vec, vec, vec],
        out_shape=[jax.ShapeDtypeStruct((S, ODD_IN), BF16), jax.ShapeDtypeStruct((S, HALF), F32),
                   jax.ShapeDtypeStruct((4, 128, 128), F32), jax.ShapeDtypeStruct((4, 128, 128), F32)]
                  + [jax.ShapeDtypeStruct((1, HALF), F32)] * 5,
        scratch_shapes=[pltpu.VMEM((TR, HALF), BF16), pltpu.VMEM((TR, HALF), F32),
                        pltpu.VMEM((HALO + TR, HALF), F32), pltpu.VMEM((TR, HALF), F32),
                        pltpu.VMEM((TR, HALF), F32), pltpu.VMEM((TR, HALF), F32)],
        compiler_params=_params(("arbitrary",), VMEM_BIG), name="odd_bwd_a",
    )(z1, z1, z1, z1, z1, z1, z1, z1, dycat, sgu_g, sgu_b, sgu_w, sgu_bb, conv_w, conv_b, cn_g, cn_b)


def _odd_bwd_b(z1, ddc, dz1, conv_w):
    nt = S // TR

    def body(dval_ref, dglu_ref, hval_ref, hglu_ref, ddc_ref, hddc_ref, cw_ref, dz_in_ref,
             dz_ref, dcw_ref, xw, dwin, dxs):
        del dz_in_ref
        i, j = pl.program_id(0), pl.program_id(1)
        sg = _sigmoid(dglu_ref[...])
        dval = dval_ref[...]

        @pl.when(j == 0)
        def _():
            halo = hval_ref[...] * _sigmoid(hglu_ref[...])
            xw[0:HALO, :] = jnp.where(i > 0, halo, 0.0)
            xw[HALO:HALO + TR, :] = dval * sg
            dwin[0:TR, :] = ddc_ref[...]
            dwin[TR:TR + HALO, :] = jnp.where(i < nt - 1, hddc_ref[...], 0.0)

            @pl.when(i == 0)
            def _():
                dcw_ref[...] = jnp.zeros_like(dcw_ref)

            for rb in range(TR // SUB):
                acc = jnp.zeros((SUB, HALF), F32)
                for k in range(CONV_K):
                    acc = acc + cw_ref[k:k + 1, :] * dwin[pl.ds(rb * SUB + (CONV_K - 1) - k, SUB), :]
                dxs[rb * SUB:(rb + 1) * SUB, :] = acc
            for k in range(CONV_K):
                acc = jnp.zeros((SUB, HALF), F32)
                for rb in range(TR // SUB):
                    acc = acc + dwin[rb * SUB:(rb + 1) * SUB, :] * xw[pl.ds(rb * SUB + HALO - (CONV_K - 1) + k, SUB), :]
                dcw_ref[k:k + 1, :] += jnp.sum(acc, axis=0, keepdims=True)
            dz_ref[...] = (dxs[...] * sg).astype(BF16)

        @pl.when(j == 1)
        def _():
            dz_ref[...] = (dxs[...] * dval * sg * (1.0 - sg)).astype(BF16)

    col = lambda c: pl.BlockSpec((TR, HALF), lambda i, j: (i, c))
    prev = lambda c: pl.BlockSpec((HALO, HALF), lambda i, j: (jnp.maximum(i * (TR // HALO) - 1, 0), c))
    nxt = pl.BlockSpec((HALO, HALF), lambda i, j: (jnp.minimum((i + 1) * (TR // HALO), S // HALO - 1), 0))
    return pl.pallas_call(
        body, grid=(nt, 2),
        in_specs=[col(3), col(4), prev(3), prev(4), pl.BlockSpec((TR, HALF), lambda i, j: (i, 0)), nxt,
                  _full_spec((HALO, HALF)), pl.BlockSpec(memory_space=pl.ANY)],
        out_specs=[pl.BlockSpec((TR, HALF), lambda i, j: (i, 3 + j)), _full_spec((HALO, HALF))],
        out_shape=[jax.ShapeDtypeStruct((S, ODD_IN), BF16), jax.ShapeDtypeStruct((HALO, HALF), F32)],
        scratch_shapes=[pltpu.VMEM((HALO + TR, HALF), F32), pltpu.VMEM((TR + HALO, HALF), F32),
                        pltpu.VMEM((TR, HALF), F32)],
        input_output_aliases={7: 0},
        compiler_params=_params(("arbitrary", "arbitrary"), VMEM_BIG), name="odd_bwd_b",
    )(z1, z1, z1, z1, ddc, ddc, conv_w, dz1)


def _cast_bf16(w, name):
    r, c = w.shape
    tr = min(r, 256)
    def body(i_ref, o_ref):
        o_ref[...] = i_ref[...].astype(BF16)

    return pl.pallas_call(
        body, grid=(r // tr,), in_specs=[pl.BlockSpec((tr, c), lambda i: (i, 0))],
        out_specs=pl.BlockSpec((tr, c), lambda i: (i, 0)), out_shape=jax.ShapeDtypeStruct((r, c), BF16),
        compiler_params=_params(("parallel",)), name=name,
    )(w)


def _adamw(w, g, m, v):
    m = ADAM_B1 * m + (1.0 - ADAM_B1) * g
    v = ADAM_B2 * v + (1.0 - ADAM_B2) * (g * g)
    m_hat = m / (1.0 - ADAM_B1 ** ADAM_STEP)
    v_hat = v / (1.0 - ADAM_B2 ** ADAM_STEP)
    delta = -ADAM_LR * (m_hat / (jnp.sqrt(v_hat) + ADAM_EPS) + ADAM_WD * w)
    return delta, m, v


def _adam_reduce(parts, w, m, v, name):
    r, c = w.shape
    tr = min(r, 128)

    def body(p_ref, w_ref, m_ref, v_ref, g_ref, d_ref, nm_ref, nv_ref):
        g = p_ref[0].astype(F32)
        for d in range(1, NDEV):
            g = g + p_ref[d].astype(F32)
        g_ref[...] = g
        d_ref[...], nm_ref[...], nv_ref[...] = _adamw(w_ref[...], g, m_ref[...], v_ref[...])

    spec = pl.BlockSpec((tr, c), lambda i: (i, 0))
    return pl.pallas_call(
        body, grid=(r // tr,), in_specs=[pl.BlockSpec((NDEV, tr, c), lambda i: (0, i, 0)), spec, spec, spec],
        out_specs=[spec] * 4, out_shape=[jax.ShapeDtypeStruct((r, c), F32)] * 4,
        compiler_params=_params(("parallel",), VMEM_BIG), name=name,
    )(parts, w, m, v)


def _sum_parts(parts, name):
    r = parts.shape[1]
    tr = 8
    for cand in (512, 256, 128, 64, 32, 16, 8):
        if r % cand == 0:
            tr = cand
            break

    def body(p_ref, o_ref):
        g = p_ref[0]
        for d in range(1, NDEV):
            g = g + p_ref[d]
        o_ref[...] = g

    return pl.pallas_call(
        body, grid=(r // tr,), in_specs=[pl.BlockSpec((NDEV, tr, 128), lambda i: (0, i, 0))],
        out_specs=pl.BlockSpec((tr, 128), lambda i: (i, 0)), out_shape=jax.ShapeDtypeStruct((r, 128), F32),
        compiler_params=_params(("parallel",)), name=name,
    )(parts)


def _adam_plain(w, g, m, v, name):
    r, c = w.shape

    def body(w_ref, g_ref, m_ref, v_ref, d_ref, nm_ref, nv_ref):
        d_ref[...], nm_ref[...], nv_ref[...] = _adamw(w_ref[...], g_ref[...], m_ref[...], v_ref[...])

    spec = pl.BlockSpec((r, c), lambda i: (0, 0))
    return pl.pallas_call(
        body, grid=(1,), in_specs=[spec] * 4, out_specs=[spec] * 3,
        out_shape=[jax.ShapeDtypeStruct((r, c), F32)] * 3,
        compiler_params=_params(("arbitrary",)), name=name,
    )(w, g, m, v)


MASKS = [(mx, my, mc) for mx in (0, 1) for my in (0, 1) for mc in (0, 1)][1:]


def _exchange(arrays, scatter, name):
    nt = len(arrays)
    out_shape = [jax.ShapeDtypeStruct(((NDEV,) + a.shape) if not scatter else a.shape, a.dtype) for a in arrays]

    def body(*refs):
        ins, outs = refs[:nt], refs[nt:2 * nt]
        send_sems, recv_sems, local_sems = refs[2 * nt:]
        x, y, c = lax.axis_index("x"), lax.axis_index("y"), lax.axis_index("c")
        me = 4 * x + 2 * y + c
        copies = []
        for t in range(nt):
            src_own = ins[t].at[me] if scatter else ins[t]
            loc = pltpu.make_async_copy(src_own, outs[t].at[me], local_sems.at[t])
            loc.start()
            copies.append(loc)
            for k, (mx, my, mc) in enumerate(MASKS):
                px, py, pc = (x + mx) % 2, (y + my) % 2, (c + mc) % 2
                peer = 4 * px + 2 * py + pc
                src = ins[t].at[peer] if scatter else ins[t]
                rc = pltpu.make_async_remote_copy(
                    src_ref=src, dst_ref=outs[t].at[me], send_sem=send_sems.at[t, k], recv_sem=recv_sems.at[t, k],
                    device_id=(px, py, pc), device_id_type=MESH)
                rc.start()
                copies.append(rc)
        for cp in copies:
            cp.wait()

    hbm = pl.BlockSpec(memory_space=pl.ANY)
    return pl.pallas_call(
        body, in_specs=[hbm] * nt, out_specs=[hbm] * nt, out_shape=out_shape,
        scratch_shapes=[pltpu.SemaphoreType.DMA((nt, 7)), pltpu.SemaphoreType.DMA((nt, 7)),
                        pltpu.SemaphoreType.DMA((nt,))],
        name=name,
    )(*arrays)


SEM_SPEC = pl.BlockSpec(memory_space=pltpu.SEMAPHORE)
EFFECT = pltpu.SideEffectType.DATAFLOW_SIDE_EFFECTING


def _direct_plan(scatter):
    def plan(x, y, c, srcs, lands):
        me = 4 * x + 2 * y + c
        local, remote = [], []
        for src, land in zip(srcs, lands):
            local.append((src.at[me] if scatter else src, land.at[me]))
            for mx, my, mc in MASKS:
                px, py, pc = (x + mx) % 2, (y + my) % 2, (c + mc) % 2
                blk = src.at[4 * px + 2 * py + pc] if scatter else src
                remote.append((blk, land.at[me], (px, py, pc)))
        return local, remote
    return plan


def _split_start(name, srcs, land_shapes, plan, n_local, n_remote, dep=None):
    ns, nl = len(srcs), len(land_shapes)
    deps = [] if dep is None else [dep]
    lands = [lax.empty(s.shape, s.dtype) for s in land_shapes]

    def body(*refs):
        ins, lz = refs[:ns], refs[ns:ns + nl]
        outs = refs[ns + nl + len(deps):]
        send_sems, recv_sems, token, local_sems = outs[0], outs[1], outs[2 + ns + nl], outs[3 + ns + nl]
        local, remote = plan(lax.axis_index("x"), lax.axis_index("y"), lax.axis_index("c"), ins, lz)
        own = [pltpu.make_async_copy(src, dst, local_sems.at[i]) for i, (src, dst) in enumerate(local)]
        for cp in own:
            cp.start()
        for k, (src, dst, peer) in enumerate(remote):
            pltpu.make_async_remote_copy(src_ref=src, dst_ref=dst, send_sem=send_sems.at[k], recv_sem=recv_sems.at[k],
                                         device_id=peer, device_id_type=MESH).start()
        for cp in own:
            cp.wait()
        token[...] = jnp.zeros_like(token)

    hbm = lambda a: pltpu.HBM(a.shape, a.dtype)
    outs = pl.pallas_call(
        body, name=name,
        out_shape=(pltpu.SemaphoreType.DMA((n_remote,)), pltpu.SemaphoreType.DMA((n_remote,)),
                   *[hbm(a) for a in srcs], *[hbm(a) for a in lands], jax.ShapeDtypeStruct((8, 128), F32)),
        in_specs=[ANY_SPEC] * (ns + nl + len(deps)),
        out_specs=(SEM_SPEC, SEM_SPEC, *[ANY_SPEC] * (ns + nl), pl.BlockSpec(memory_space=pltpu.VMEM)),
        scratch_shapes=[pltpu.SemaphoreType.DMA((n_local,))],
        input_output_aliases={i: 2 + i for i in range(ns + nl)},
        compiler_params=pltpu.CompilerParams(has_side_effects=EFFECT),
    )(*[pltpu.with_memory_space_constraint(a, pltpu.HBM) for a in srcs],
      *[pltpu.with_memory_space_constraint(a, pltpu.HBM) for a in lands], *deps)
    return dict(sems=outs[:2], srcs=outs[2:2 + ns], lands=outs[2 + ns:2 + ns + nl], token=outs[-1],
                plan=plan, n_remote=n_remote)


def _split_wait(name, handle, after):
    srcs, lands, plan = handle["srcs"], handle["lands"], handle["plan"]
    ns, nl = len(srcs), len(lands)

    def body(*refs):
        ins, lz = refs[:ns], refs[ns:ns + nl]
        send_sems, recv_sems = refs[ns + nl], refs[ns + nl + 1]
        _, remote = plan(lax.axis_index("x"), lax.axis_index("y"), lax.axis_index("c"), ins, lz)
        for k, (src, dst, peer) in enumerate(remote):
            cp = pltpu.make_async_remote_copy(src_ref=src, dst_ref=dst, send_sem=send_sems.at[k],
                                              recv_sem=recv_sems.at[k], device_id=peer, device_id_type=MESH)
            cp.wait_send()
            cp.wait_recv()

    hbm = lambda a: pltpu.HBM(a.shape, a.dtype)
    outs = pl.pallas_call(
        body, name=name, out_shape=(*[hbm(a) for a in srcs], *[hbm(a) for a in lands]),
        in_specs=[ANY_SPEC] * (ns + nl) + [SEM_SPEC, SEM_SPEC, ANY_SPEC], out_specs=tuple([ANY_SPEC] * (ns + nl)),
        input_output_aliases={i: i for i in range(ns + nl)},
        compiler_params=pltpu.CompilerParams(has_side_effects=EFFECT),
    )(*srcs, *lands, *handle["sems"], after)
    return list(outs[ns:])


SMALL = {
    "e_pre_norm": ((2048,), None), "e_pool_w": ((4, 256, 256), 1), "e_pool_scale": ((1024,), None),
    "e_post_norm": ((2048,), None), "o_pre_norm": ((2048,), 0), "o_sgu_norm_g": ((1024,), 0),
    "o_sgu_norm_b": ((1024,), 0), "o_sgu_w": ((4, 128, 128), None), "o_sgu_b": ((4, 128), None),
    "o_conv_w": ((31, 1024), 1), "o_conv_b": ((1024,), 0), "o_conv_norm_g": ((1024,), 0),
    "o_conv_norm_b": ((1024,), 0), "o_post_norm": ((2048,), 0),
}
SMALL_SHARDED = [n for n, (_, ax) in SMALL.items() if ax is not None]


def _shard_shape(name):
    shape, ax = SMALL[name]
    if ax is None:
        return shape
    return tuple(s // NDEV if i == ax else s for i, s in enumerate(shape))


def _pack(arrs, row_multiple=1):
    flat = jnp.concatenate([a.reshape(-1) for a in arrs])
    pad = -flat.shape[0] % (128 * row_multiple)
    return jnp.concatenate([flat, jnp.zeros((pad,), F32)]).reshape(-1, 128)


def _unpack(buf, shapes):
    flat = buf.reshape(-1)
    out, off = [], 0
    for shp in shapes:
        n = int(np.prod(shp))
        out.append(flat[off:off + n].reshape(shp))
        off += n
    return out


def _take_shard(full, name, me):
    shape, ax = SMALL[name]
    if ax is None:
        return full
    n = shape[ax] // NDEV
    return lax.dynamic_slice_in_dim(full, me * n, n, axis=ax)


BIG = ("e_w_in", "e_w_out", "o_w_in", "o_w_out")
WEIGHTS = ["e_pre_norm", "e_w_in", "e_pool_w", "e_pool_scale", "e_w_out", "e_post_norm", "o_pre_norm", "o_w_in",
           "o_sgu_norm_g", "o_sgu_norm_b", "o_sgu_w", "o_sgu_b", "o_conv_w", "o_conv_b", "o_conv_norm_g",
           "o_conv_norm_b", "o_w_out", "o_post_norm"]


def kernel(x, e_pre_norm, e_w_in, e_pool_w, e_pool_scale, e_w_out, e_post_norm, o_pre_norm, o_w_in, o_sgu_norm_g, o_sgu_norm_b, o_sgu_w, o_sgu_b, o_conv_w, o_conv_b, o_conv_norm_g, o_conv_norm_b, o_w_out, o_post_norm, loss_target, m_e_pre_norm, m_e_w_in, m_e_pool_w, m_e_pool_scale, m_e_w_out, m_e_post_norm, m_o_pre_norm, m_o_w_in, m_o_sgu_norm_g, m_o_sgu_norm_b, m_o_sgu_w, m_o_sgu_b, m_o_conv_w, m_o_conv_b, m_o_conv_norm_g, m_o_conv_norm_b, m_o_w_out, m_o_post_norm, v_e_pre_norm, v_e_w_in, v_e_pool_w, v_e_pool_scale, v_e_w_out, v_e_post_norm, v_o_pre_norm, v_o_w_in, v_o_sgu_norm_g, v_o_sgu_norm_b, v_o_sgu_w, v_o_sgu_b, v_o_conv_w, v_o_conv_b, v_o_conv_norm_g, v_o_conv_norm_b, v_o_w_out, v_o_post_norm):
    given = dict(locals())
    w = {n: given[n][0] for n in WEIGHTS}
    m = {n: given["m_" + n][0] for n in WEIGHTS}
    v = {n: given["v_" + n][0] for n in WEIGHTS}
    me = 4 * lax.axis_index("x") + 2 * lax.axis_index("y") + lax.axis_index("c")
    x, target = x[0], loss_target[0]
    row = lambda a: a.reshape(1, -1)
    gathered = lambda a: jax.ShapeDtypeStruct((NDEV,) + a.shape, a.dtype)

    def gather_start(name, arrs, dep=None):
        return _split_start(name, arrs, [gathered(a) for a in arrs], _direct_plan(False), len(arrs), 7 * len(arrs), dep)

    def scatter_start(name, arrs, dep=None):
        return _split_start(name, arrs, arrs, _direct_plan(True), len(arrs), 7 * len(arrs), dep)

    bf = {n: _cast_bf16(w[n], "cast_" + n) for n in BIG}
    ga = gather_start("gather_a_start", [bf["e_w_in"], _pack([w[n] for n in SMALL_SHARDED])])
    gb = gather_start("gather_b_start", [bf["e_w_out"], bf["o_w_in"], bf["o_w_out"]], ga["token"])
    h0 = _pre0_fwd(x, row(w["e_pre_norm"]), gb["token"])
    wg_e_in, small_rows = _split_wait("gather_a_wait", ga, h0)
    p = {n: w[n] for n in SMALL if SMALL[n][1] is None}
    small_rows = small_rows.reshape(NDEV, -1)
    off = 0
    for n in SMALL_SHARDED:
        shp, ax = _shard_shape(n), SMALL[n][1]
        cnt = int(np.prod(shp))
        blk = small_rows[:, off:off + cnt].reshape((NDEV,) + shp)
        p[n] = jnp.moveaxis(blk, 0, ax).reshape(SMALL[n][0])
        off += cnt
    tabs = _rope_tables()
    pool_w_bf = p["e_pool_w"].astype(BF16)
    sgu_bb = jnp.broadcast_to(p["o_sgu_b"][:, :, None], (4, 128, 128))
    conv_w = jnp.concatenate([p["o_conv_w"], jnp.zeros((HALO - CONV_K, HALF), F32)], axis=0)
    odd_p = (row(p["o_sgu_norm_g"]), row(p["o_sgu_norm_b"]), p["o_sgu_w"], sgu_bb, conv_w,
             row(p["o_conv_b"]), row(p["o_conv_norm_g"]), row(p["o_conv_norm_b"]))

    z0 = _mm_in(h0, wg_e_in, "mm_z0")
    ycat0 = _pool_fwd(z0, pool_w_bf, row(p["e_pool_scale"]))
    ycat0 = _attn_fwd(z0, ycat0, tabs)
    wg_e_out, wg_o_in, wg_o_out = _split_wait("gather_b_wait", gb, ycat0)
    w_out_e, w_out_o = wg_e_out.reshape(2048, D), wg_o_out.reshape(2048, D)
    y0 = _mm_out(ycat0, w_out_e, "mm_y0")
    x1, h1 = _post0_fwd(x, y0, row(p["e_post_norm"]), row(p["o_pre_norm"]))
    z1 = _mm_in(h1, wg_o_in, "mm_z1")
    ycat1 = _odd_fwd(z1, *odd_p)
    y1 = _mm_out(ycat1, w_out_o, "mm_y1")

    g = {}
    loss, dx2, dy1, g["o_post_norm"] = _post1_bwd(y1, x1, target, row(p["o_post_norm"]))
    loss = lax.psum(loss[0, 0], ("x", "y", "c"))
    s1 = scatter_start("scatter_o_w_out_start", [_mm_out_dw(ycat1, dy1, "mm_dwout1").reshape(NDEV, 256, D)])
    dycat1 = _mm_out_dx(dy1, w_out_o, "mm_dycat1", s1["token"])
    dz1, ddc, g["o_sgu_w"], d_sgu_bb, g["o_sgu_norm_g"], g["o_sgu_norm_b"], g["o_conv_norm_g"], \
        g["o_conv_norm_b"], g["o_conv_b"] = _odd_bwd_a(z1, dycat1, *odd_p)
    dz1, d_conv_w = _odd_bwd_b(z1, ddc, dz1, conv_w)
    g["o_sgu_b"] = d_sgu_bb[:, :, 0]
    g["o_conv_w"] = d_conv_w[:CONV_K]
    s2 = scatter_start("scatter_o_w_in_start", [_mm_in_dw(h1, dz1, ODD_IN // NDEV, "mm_dwin1")])
    dh1 = _mm_in_dx(dz1, wg_o_in, "mm_dh1", s2["token"])
    dx1, dy0, g["o_pre_norm"], g["e_post_norm"] = _mid_bwd(dx2, dh1, x1, y0, row(p["o_pre_norm"]),
                                                           row(p["e_post_norm"]))
    s3 = scatter_start("scatter_e_w_out_start", [_mm_out_dw(ycat0, dy0, "mm_dwout0").reshape(NDEV, 256, D)])
    dycat0 = _mm_out_dx(dy0, w_out_e, "mm_dycat0", s3["token"])
    da_in, da_gate, g["e_pool_w"], g["e_pool_scale"] = _pool_bwd(z0, dycat0, pool_w_bf, row(p["e_pool_scale"]))
    dq, dk, dv, dbg = _attn_bwd(z0, dycat0, tabs)
    dz0 = jnp.concatenate([da_in, da_gate, dq, dk, dv, dbg], axis=1)
    late = [n for n in SMALL if n != "e_pre_norm"]
    s4 = scatter_start("scatter_e_w_in_start", [_mm_in_dw(h0, dz0, EVEN_IN // NDEV, "mm_dwin0")])
    s5 = gather_start("gather_small_grads_start", [_pack([g[n].reshape(SMALL[n][0]) for n in late], 512)], s4["token"])
    dh0 = _mm_in_dx(dz0, wg_e_in, "mm_dh0", s5["token"])
    grad_x, g["e_pre_norm"] = _pre0_bwd(dx1, dh0, x, row(p["e_pre_norm"]))
    last = _exchange([g["e_pre_norm"].reshape(16, 128)], False, "gather_e_pre_norm_grad")[0]

    grads, deltas, new_m, new_v = {}, {}, {}, {}
    for n, handle in (("o_w_out", s1), ("o_w_in", s2), ("e_w_out", s3), ("e_w_in", s4)):
        parts, = _split_wait("scatter_" + n + "_wait", handle, last)
        grads[n], deltas[n], new_m[n], new_v[n] = _adam_reduce(parts, w[n], m[n], v[n], "adam_" + n)
    recv_small, = _split_wait("gather_small_grads_wait", s5, last)
    g_small = dict(zip(late, _unpack(_sum_parts(recv_small, "sum_small_grads"), [SMALL[n][0] for n in late])))
    g_small["e_pre_norm"] = _sum_parts(last, "sum_e_pre_norm_grad").reshape(2048)
    for n in SMALL:
        grads[n] = _take_shard(g_small[n], n, me)
    names = list(SMALL)
    shapes = [_shard_shape(n) for n in names]
    d_pack, m_pack, v_pack = _adam_plain(_pack([w[n] for n in names]), _pack([grads[n] for n in names]),
                                         _pack([m[n] for n in names]), _pack([v[n] for n in names]), "adam_small")
    for n, d_, m_, v_ in zip(names, _unpack(d_pack, shapes), _unpack(m_pack, shapes), _unpack(v_pack, shapes)):
        deltas[n], new_m[n], new_v[n] = d_, m_, v_

    lead = lambda a: a[None]
    return (loss, grad_x[None], *[lead(grads[n]) for n in WEIGHTS], *[lead(deltas[n]) for n in WEIGHTS],
            *[lead(new_m[n]) for n in WEIGHTS], *[lead(new_v[n]) for n in WEIGHTS])
```

```python
import functools

import numpy as np
import jax
import jax.numpy as jnp
from jax import lax
from jax.experimental import pallas as pl
from jax.experimental.pallas import tpu as pltpu

F32 = jnp.float32
BF16 = jnp.bfloat16

S = 2048
D = 2048
NDEV = 8
EPS = 1e-6
NEG = -1e30
HEAD_DIM = 128
ROT_DIM = 32
ROPE_THETA = 500000.0
PATTERNS = ((128, 1), (512, 4), (2048, 16))
BLK = 128
EVEN_IN = 12288
ODD_IN = 6144
HALF = 1024
CONV_K = 31
HALO = 32
TR = 256
SUB = 32

ADAM_LR = 0.001
ADAM_B1 = 0.9
ADAM_B2 = 0.999
ADAM_EPS = 1e-08
ADAM_WD = 0.01
ADAM_STEP = 10

VMEM_BIG = 56 * 1024 * 1024
MESH = pl.DeviceIdType.MESH

NN = (((1,), (0,)), ((), ()))
NT = (((1,), (1,)), ((), ()))
TN = (((0,), (0,)), ((), ()))


def _dot(a, b, dn=NN):
    return lax.dot_general(a, b, dn, preferred_element_type=F32)


def _sigmoid(x):
    return 1.0 / (1.0 + jnp.exp(-x))


def _silu_and_grad(x):
    sg = _sigmoid(x)
    return x * sg, sg * (1.0 + x * (1.0 - sg))


def _params(sem, vmem=None):
    return pltpu.CompilerParams(dimension_semantics=sem, vmem_limit_bytes=vmem)


ANY_SPEC = pl.BlockSpec(memory_space=pl.ANY)


def _matmul(a, b, *, dn, grid, a_spec, b_spec, o_spec, out_shape, out_dtype, acc_shape, name, dep=None):
    nk = grid[2]
    deps = [] if dep is None else [dep]

    def body(a_ref, b_ref, *rest):
        o_ref, acc = rest[len(deps)], rest[len(deps) + 1:]
        if nk == 1:
            o_ref[...] = _dot(a_ref[...], b_ref[...], dn).astype(o_ref.dtype)
            return
        acc_ref = acc[0]
        k = pl.program_id(2)

        @pl.when(k == 0)
        def _():
            acc_ref[...] = jnp.zeros_like(acc_ref)

        acc_ref[...] += _dot(a_ref[...], b_ref[...], dn)

        @pl.when(k == nk - 1)
        def _():
            o_ref[...] = acc_ref[...].astype(o_ref.dtype)

    return pl.pallas_call(
        body, grid=grid, in_specs=[a_spec, b_spec] + [ANY_SPEC] * len(deps), out_specs=o_spec,
        out_shape=jax.ShapeDtypeStruct(out_shape, out_dtype),
        scratch_shapes=[] if nk == 1 else [pltpu.VMEM(acc_shape, F32)],
        compiler_params=_params(("parallel", "parallel", "arbitrary"), VMEM_BIG), name=name,
    )(a, b, *deps)


TM = 512


def _mm_in(h, wg, name):
    nb = wg.shape[2]
    tn = 512 if nb % 512 == 0 else nb
    per = nb // tn
    return _matmul(
        h, wg, dn=NN, grid=(S // TM, NDEV * per, 1),
        a_spec=pl.BlockSpec((TM, D), lambda i, j, k: (i, 0)),
        b_spec=pl.BlockSpec((None, D, tn), lambda i, j, k: (j // per, 0, j % per)),
        o_spec=pl.BlockSpec((TM, tn), lambda i, j, k: (i, j)),
        out_shape=(S, NDEV * nb), out_dtype=F32, acc_shape=(TM, tn), name=name)


def _mm_in_dx(dz, wg, name, dep=None):
    nb = wg.shape[2]
    return _matmul(
        dz, wg, dn=NT, grid=(S // TM, D // 512, NDEV),
        a_spec=pl.BlockSpec((TM, nb), lambda i, j, k: (i, k)),
        b_spec=pl.BlockSpec((None, 512, nb), lambda i, j, k: (k, j, 0)),
        o_spec=pl.BlockSpec((TM, 512), lambda i, j, k: (i, j)),
        out_shape=(S, D), out_dtype=F32, acc_shape=(TM, 512), name=name, dep=dep)


def _mm_in_dw(h, dz, nb, name):
    tn = 512 if nb % 512 == 0 else nb
    per = nb // tn
    return _matmul(
        h, dz, dn=TN, grid=(D // TM, NDEV * per, 1),
        a_spec=pl.BlockSpec((S, TM), lambda i, j, k: (0, i)),
        b_spec=pl.BlockSpec((S, tn), lambda i, j, k: (0, j)),
        o_spec=pl.BlockSpec((None, TM, tn), lambda i, j, k: (j // per, i, j % per)),
        out_shape=(NDEV, D, nb), out_dtype=BF16, acc_shape=(TM, tn), name=name)


def _mm_out(yc, w, name):
    return _matmul(
        yc, w, dn=NN, grid=(S // TM, D // 512, 1),
        a_spec=pl.BlockSpec((TM, 2048), lambda i, j, k: (i, 0)),
        b_spec=pl.BlockSpec((2048, 512), lambda i, j, k: (0, j)),
        o_spec=pl.BlockSpec((TM, 512), lambda i, j, k: (i, j)),
        out_shape=(S, D), out_dtype=F32, acc_shape=(TM, 512), name=name)


def _mm_out_dx(dy, w, name, dep=None):
    return _matmul(
        dy, w, dn=NT, grid=(S // TM, 2048 // 512, 1),
        a_spec=pl.BlockSpec((TM, D), lambda i, j, k: (i, 0)),
        b_spec=pl.BlockSpec((512, D), lambda i, j, k: (j, 0)),
        o_spec=pl.BlockSpec((TM, 512), lambda i, j, k: (i, j)),
        out_shape=(S, 2048), out_dtype=F32, acc_shape=(TM, 512), name=name, dep=dep)


def _mm_out_dw(yc, dy, name):
    return _matmul(
        yc, dy, dn=TN, grid=(2048 // TM, D // 512, 1),
        a_spec=pl.BlockSpec((S, TM), lambda i, j, k: (0, i)),
        b_spec=pl.BlockSpec((S, 512), lambda i, j, k: (0, j)),
        o_spec=pl.BlockSpec((TM, 512), lambda i, j, k: (i, j)),
        out_shape=(2048, D), out_dtype=BF16, acc_shape=(TM, 512), name=name)


def _row_spec(w=D):
    return pl.BlockSpec((TR, w), lambda i: (i, 0))


def _vec_spec(w=D):
    return pl.BlockSpec((1, w), lambda i: (0, 0))


def _rms_stats(x):
    r = lax.rsqrt(jnp.mean(x * x, axis=-1, keepdims=True) + EPS)
    return x * r, r


def _rms_bwd(dn, xhat, r, g):
    dxh = dn * g
    return r * (dxh - xhat * jnp.mean(dxh * xhat, axis=-1, keepdims=True))


def _acc_rows(ref, val, i):
    s = jnp.sum(val, axis=0, keepdims=True)

    @pl.when(i == 0)
    def _():
        ref[...] = s

    @pl.when(i > 0)
    def _():
        ref[...] += s


def _pre0_fwd(x, g, dep=None):
    deps = [] if dep is None else [dep]

    def body(x_ref, g_ref, *rest):
        xhat, _ = _rms_stats(x_ref[...])
        rest[-1][...] = (xhat * g_ref[...]).astype(BF16)

    return pl.pallas_call(
        body, grid=(S // TR,), in_specs=[_row_spec(), _vec_spec()] + [ANY_SPEC] * len(deps), out_specs=_row_spec(),
        out_shape=jax.ShapeDtypeStruct((S, D), BF16), compiler_params=_params(("parallel",)), name="pre0_fwd",
    )(x, g, *deps)


def _post0_fwd(x, y0, g_post, g_pre1):
    def body(x_ref, y_ref, gp_ref, g1_ref, x1_ref, h1_ref):
        yhat, _ = _rms_stats(y_ref[...])
        x1 = x_ref[...] + yhat * gp_ref[...]
        x1_ref[...] = x1
        xhat, _ = _rms_stats(x1)
        h1_ref[...] = (xhat * g1_ref[...]).astype(BF16)

    return pl.pallas_call(
        body, grid=(S // TR,), in_specs=[_row_spec(), _row_spec(), _vec_spec(), _vec_spec()],
        out_specs=[_row_spec(), _row_spec()],
        out_shape=[jax.ShapeDtypeStruct((S, D), F32), jax.ShapeDtypeStruct((S, D), BF16)],
        compiler_params=_params(("parallel",)), name="post0_fwd",
    )(x, y0, g_post, g_pre1)


def _post1_bwd(y1, x1, target, g_post):
    def body(y_ref, x1_ref, t_ref, g_ref, loss_ref, dx2_ref, dy_ref, dg_ref):
        i = pl.program_id(0)
        yhat, r = _rms_stats(y_ref[...])
        g = g_ref[...]
        err = x1_ref[...] + yhat * g - t_ref[...]
        part = jnp.sum(jnp.sum(err * err, axis=-1, keepdims=True), axis=0, keepdims=True) * (0.5 / D)
        _acc_rows(loss_ref, jnp.broadcast_to(part, (1, 128)), i)
        dx2 = err * (1.0 / D)
        dx2_ref[...] = dx2
        _acc_rows(dg_ref, dx2 * yhat, i)
        dy_ref[...] = _rms_bwd(dx2, yhat, r, g).astype(BF16)

    return pl.pallas_call(
        body, grid=(S // TR,), in_specs=[_row_spec(), _row_spec(), _row_spec(), _vec_spec()],
        out_specs=[_vec_spec(128), _row_spec(), _row_spec(), _vec_spec()],
        out_shape=[jax.ShapeDtypeStruct((1, 128), F32), jax.ShapeDtypeStruct((S, D), F32),
                   jax.ShapeDtypeStruct((S, D), BF16), jax.ShapeDtypeStruct((1, D), F32)],
        compiler_params=_params(("arbitrary",)), name="post1_bwd",
    )(y1, x1, target, g_post)


def _mid_bwd(dx2, dh1, x1, y0, g_pre1, g_post0):
    def body(dx2_ref, dh_ref, x1_ref, y_ref, g1_ref, gp_ref, dx1_ref, dy_ref, dg1_ref, dgp_ref):
        i = pl.program_id(0)
        xhat, r1 = _rms_stats(x1_ref[...])
        dh = dh_ref[...]
        _acc_rows(dg1_ref, dh * xhat, i)
        dx1 = dx2_ref[...] + _rms_bwd(dh, xhat, r1, g1_ref[...])
        dx1_ref[...] = dx1
        yhat, r0 = _rms_stats(y_ref[...])
        _acc_rows(dgp_ref, dx1 * yhat, i)
        dy_ref[...] = _rms_bwd(dx1, yhat, r0, gp_ref[...]).astype(BF16)

    return pl.pallas_call(
        body, grid=(S // TR,),
        in_specs=[_row_spec(), _row_spec(), _row_spec(), _row_spec(), _vec_spec(), _vec_spec()],
        out_specs=[_row_spec(), _row_spec(), _vec_spec(), _vec_spec()],
        out_shape=[jax.ShapeDtypeStruct((S, D), F32), jax.ShapeDtypeStruct((S, D), BF16),
                   jax.ShapeDtypeStruct((1, D), F32), jax.ShapeDtypeStruct((1, D), F32)],
        compiler_params=_params(("arbitrary",)), name="mid_bwd",
    )(dx2, dh1, x1, y0, g_pre1, g_post0)


def _pre0_bwd(dx1, dh0, x, g):
    def body(dx1_ref, dh_ref, x_ref, g_ref, gx_ref, dg_ref):
        i = pl.program_id(0)
        xhat, r = _rms_stats(x_ref[...])
        dh = dh_ref[...]
        _acc_rows(dg_ref, dh * xhat, i)
        gx_ref[...] = dx1_ref[...] + _rms_bwd(dh, xhat, r, g_ref[...])

    return pl.pallas_call(
        body, grid=(S // TR,), in_specs=[_row_spec(), _row_spec(), _row_spec(), _vec_spec()],
        out_specs=[_row_spec(), _vec_spec()],
        out_shape=[jax.ShapeDtypeStruct((S, D), F32), jax.ShapeDtypeStruct((1, D), F32)],
        compiler_params=_params(("arbitrary",)), name="pre0_bwd",
    )(dx1, dh0, x, g)


POOL_CH = 256


def _pool_apply(a, w, transpose):
    n = a.shape[0]
    row = lax.broadcasted_iota(jnp.int32, a.shape, 0)
    cnt = jnp.minimum(row + 1, w).astype(F32)
    s = a / cnt if transpose else a
    for k in (1, 2, 4, 8):
        if transpose:
            sh = jnp.where(row < n - k, pltpu.roll(s, n - k, 0), 0.0)
        else:
            sh = jnp.where(row >= k, pltpu.roll(s, k, 0), 0.0)
        s = jnp.where(w > k, s + sh, s)
    return s - a if transpose else s / cnt - a


def _pool_fwd(z0, pool_w, pool_scale):
    def body(a_ref, gate_ref, w_ref, sc_ref, out_ref):
        win = jnp.left_shift(2, pl.program_id(0))
        pooled = _pool_apply(a_ref[...], win, False)
        mixed = _dot(pooled.astype(BF16), w_ref[...])
        gate = gate_ref[...]
        out_ref[...] = (mixed * sc_ref[...] * (gate * _sigmoid(gate))).astype(BF16)

    return pl.pallas_call(
        body, grid=(4,),
        in_specs=[pl.BlockSpec((S, POOL_CH), lambda g: (0, g)), pl.BlockSpec((S, POOL_CH), lambda g: (0, 4 + g)),
                  pl.BlockSpec((None, POOL_CH, POOL_CH), lambda g: (g, 0, 0)),
                  pl.BlockSpec((1, POOL_CH), lambda g: (0, g))],
        out_specs=pl.BlockSpec((S, POOL_CH), lambda g: (0, g)),
        out_shape=jax.ShapeDtypeStruct((S, 2048), BF16),
        compiler_params=_params(("parallel",), VMEM_BIG), name="pool_fwd",
    )(z0, z0, pool_w, pool_scale)


def _pool_bwd(z0, dycat, pool_w, pool_scale):
    def body(a_ref, gate_ref, dy_ref, w_ref, sc_ref, da_ref, dgate_ref, dw_ref, dsc_ref):
        win = jnp.left_shift(2, pl.program_id(0))
        pooled = _pool_apply(a_ref[...], win, False).astype(BF16)
        w = w_ref[...]
        mixed = _dot(pooled, w)
        silu, dsilu = _silu_and_grad(gate_ref[...])
        dy = dy_ref[...]
        sc = sc_ref[...]
        dgate_ref[...] = (dy * (mixed * sc) * dsilu).astype(BF16)
        dms = dy * silu
        dsc_ref[...] = jnp.sum(dms * mixed, axis=0, keepdims=True)
        dmixed = (dms * sc).astype(BF16)
        dw_ref[...] = _dot(pooled, dmixed, TN)
        dpooled = _dot(dmixed, w, NT)
        da_ref[...] = _pool_apply(dpooled, win, True).astype(BF16)

    slab = lambda off: pl.BlockSpec((S, POOL_CH), lambda g: (0, off + g))
    return pl.pallas_call(
        body, grid=(4,),
        in_specs=[slab(0), slab(4), slab(0), pl.BlockSpec((None, POOL_CH, POOL_CH), lambda g: (g, 0, 0)),
                  pl.BlockSpec((1, POOL_CH), lambda g: (0, g))],
        out_specs=[slab(0), slab(0), pl.BlockSpec((None, POOL_CH, POOL_CH), lambda g: (g, 0, 0)),
                   pl.BlockSpec((1, POOL_CH), lambda g: (0, g))],
        out_shape=[jax.ShapeDtypeStruct((S, HALF), BF16), jax.ShapeDtypeStruct((S, HALF), BF16),
                   jax.ShapeDtypeStruct((4, POOL_CH, POOL_CH), F32), jax.ShapeDtypeStruct((1, HALF), F32)],
        compiler_params=_params(("parallel",), VMEM_BIG), name="pool_bwd",
    )(z0, z0, dycat, pool_w, pool_scale)


Q_COL, K_COL, V_COL, BG_COL = 2048 // 128, 5120 // 128, 8192 // 128, 11264 // 128
SCALE = HEAD_DIM ** -0.5


def _rope_tables():
    pos = jnp.arange(S, dtype=F32)
    inv_freq = jnp.power(ROPE_THETA, -jnp.arange(0, ROT_DIM, 2, dtype=F32) / ROT_DIM)
    ang = pos[:, None] * inv_freq[None, :]
    cos, sin = jnp.cos(ang), jnp.sin(ang)
    half = ROT_DIM // 2
    zeros = jnp.zeros((S, HEAD_DIM - ROT_DIM), F32)
    c = jnp.concatenate([cos, cos, jnp.ones((S, HEAD_DIM - ROT_DIM), F32)], axis=1)
    a = jnp.concatenate([-sin, jnp.zeros((S, half), F32), zeros], axis=1)
    b = jnp.concatenate([jnp.zeros((S, half), F32), sin, zeros], axis=1)
    return c, a, b


def _rope(t, c, a, b):
    half = ROT_DIM // 2
    return t * c + pltpu.roll(t, HEAD_DIM - half, 1) * a + pltpu.roll(t, half, 1) * b


def _rope_t(d, c, a, b):
    half = ROT_DIM // 2
    return d * c + pltpu.roll(d * a, half, 1) + pltpu.roll(d * b, HEAD_DIM - half, 1)


def _deinterleave(dst, src, dil, cast=None):
    length = S // dil
    for r in range(dil):
        v = src[...] if dil == 1 else src[pl.ds(r, length, stride=dil), :]
        dst[r * length:(r + 1) * length, :] = v if cast is None else v.astype(cast)


def _interleave(dst, src, dil):
    length = S // dil
    for r in range(dil):
        if dil == 1:
            dst[...] = src[...]
        else:
            dst[pl.ds(r, length, stride=dil), :] = src[r * length:(r + 1) * length, :]


def _unit_scores(u, nb, qd, kd):
    o0 = pl.multiple_of(u * BLK, BLK)
    p0 = pl.multiple_of(jnp.maximum(u - 1, 0) * BLK, BLK)
    q = qd[pl.ds(o0, BLK), :]
    row = lax.broadcasted_iota(jnp.int32, (BLK, BLK), 0)
    col = lax.broadcasted_iota(jnp.int32, (BLK, BLK), 1)
    s_own = jnp.where(col <= row, _dot(q, kd[pl.ds(o0, BLK), :], NT) * SCALE, NEG)
    if nb == 1:
        return o0, p0, q, s_own, None
    has_prev = (u % nb) != 0
    s_prev = jnp.where((col >= row) & has_prev, _dot(q, kd[pl.ds(p0, BLK), :], NT) * SCALE, NEG)
    return o0, p0, q, s_own, s_prev


def _attn_group_fwd(dil, q_ref, k_ref, v_ref, tabs, tmp, qd, kd, vd, od, ld, og, lg):
    nb = S // dil // BLK
    c, a, b = tabs
    tmp[...] = _rope(q_ref[...], c, a, b)
    _deinterleave(qd, tmp, dil, BF16)
    tmp[...] = _rope(k_ref[...], c, a, b)
    _deinterleave(kd, tmp, dil, BF16)
    _deinterleave(vd, v_ref, dil, BF16)

    def unit(u, carry):
        o0, p0, _, s_own, s_prev = _unit_scores(u, nb, qd, kd)
        m = jnp.max(s_own, axis=1, keepdims=True)
        if s_prev is not None:
            m = jnp.maximum(m, jnp.max(s_prev, axis=1, keepdims=True))
        p_own = jnp.exp(s_own - m)
        den = jnp.sum(p_own, axis=1, keepdims=True)
        acc = _dot(p_own.astype(BF16), vd[pl.ds(o0, BLK), :])
        if s_prev is not None:
            p_prev = jnp.exp(s_prev - m)
            den = den + jnp.sum(p_prev, axis=1, keepdims=True)
            acc = acc + _dot(p_prev.astype(BF16), vd[pl.ds(p0, BLK), :])
        od[pl.ds(o0, BLK), :] = acc / den
        ld[pl.ds(o0, BLK), :] = jnp.broadcast_to(m + jnp.log(den), (BLK, HEAD_DIM))
        return carry

    lax.fori_loop(0, S // BLK, unit, 0)
    _interleave(og, od, dil)
    _interleave(lg, ld, dil)


def _group_weights(lgs):
    l0, l1, l2 = lgs[0][...], lgs[1][...], lgs[2][...]
    mx = jnp.maximum(l0, jnp.maximum(l1, l2))
    e0, e1, e2 = jnp.exp(l0 - mx), jnp.exp(l1 - mx), jnp.exp(l2 - mx)
    den = e0 + e1 + e2
    return e0 / den, e1 / den, e2 / den


def _head_spec(base, ngroups_axis=True):
    return pl.BlockSpec((S, HEAD_DIM), lambda h, p: (0, base + (p % 3) * 8 + h))


ATTN_SCRATCH_FWD = [
    pltpu.VMEM((S, HEAD_DIM), F32),
    pltpu.VMEM((S, HEAD_DIM), BF16), pltpu.VMEM((S, HEAD_DIM), BF16), pltpu.VMEM((S, HEAD_DIM), BF16),
    pltpu.VMEM((S, HEAD_DIM), F32), pltpu.VMEM((S, HEAD_DIM), F32),
    pltpu.VMEM((S, HEAD_DIM), F32), pltpu.VMEM((S, HEAD_DIM), F32), pltpu.VMEM((S, HEAD_DIM), F32),
    pltpu.VMEM((S, HEAD_DIM), F32), pltpu.VMEM((S, HEAD_DIM), F32), pltpu.VMEM((S, HEAD_DIM), F32),
]


def _attn_fwd(z0, ycat, tabs):
    def body(q_ref, k_ref, v_ref, gate_ref, c_ref, a_ref, b_ref, ycat_ref, out_ref,
             tmp, qd, kd, vd, od, ld, og0, og1, og2, lg0, lg1, lg2):
        del ycat_ref
        p = pl.program_id(1)
        ogs, lgs = (og0, og1, og2), (lg0, lg1, lg2)
        tabs_v = (c_ref[...], a_ref[...], b_ref[...])
        for gi, (_, dil) in enumerate(PATTERNS):
            @pl.when(p == gi)
            def _(gi=gi, dil=dil):
                _attn_group_fwd(dil, q_ref, k_ref, v_ref, tabs_v, tmp, qd, kd, vd, od, ld, ogs[gi], lgs[gi])

        @pl.when(p == 2)
        def _():
            w0, w1, w2 = _group_weights(lgs)
            o = w0 * og0[...] + w1 * og1[...] + w2 * og2[...]
            gate = gate_ref[...]
            out_ref[...] = (o * (gate * _sigmoid(gate))).astype(BF16)

    tab = pl.BlockSpec((S, HEAD_DIM), lambda h, p: (0, 0))
    return pl.pallas_call(
        body, grid=(8, 3),
        in_specs=[_head_spec(Q_COL), _head_spec(K_COL), _head_spec(V_COL),
                  pl.BlockSpec((S, HEAD_DIM), lambda h, p: (0, BG_COL + h)), tab, tab, tab,
                  pl.BlockSpec(memory_space=pl.ANY)],
        out_specs=pl.BlockSpec((S, HEAD_DIM), lambda h, p: (0, 8 + h)),
        out_shape=jax.ShapeDtypeStruct((S, 2048), BF16),
        scratch_shapes=ATTN_SCRATCH_FWD, input_output_aliases={7: 0},
        compiler_params=_params(("parallel", "arbitrary"), VMEM_BIG), name="attn_fwd",
    )(z0, z0, z0, z0, *tabs, ycat)


def _attn_bwd(z0, dycat, tabs):
    def body(q_ref, k_ref, v_ref, gate_ref, dy_ref, c_ref, a_ref, b_ref,
             dq_ref, dk_ref, dv_ref, dbg_ref,
             tmp, qd, kd, vd, od, ld, og0, og1, og2, lg0, lg1, lg2, cg0, cg1, cg2, dod, cd, dqd, dkd, dvd):
        p = pl.program_id(1)
        ogs, lgs, cgs = (og0, og1, og2), (lg0, lg1, lg2), (cg0, cg1, cg2)
        tabs_v = (c_ref[...], a_ref[...], b_ref[...])
        for gi, (_, dil) in enumerate(PATTERNS):
            @pl.when(p == gi)
            def _(gi=gi, dil=dil):
                _attn_group_fwd(dil, q_ref, k_ref, v_ref, tabs_v, tmp, qd, kd, vd, od, ld, ogs[gi], lgs[gi])

        @pl.when(p == 2)
        def _():
            w = _group_weights(lgs)
            o = w[0] * og0[...] + w[1] * og1[...] + w[2] * og2[...]
            silu, dsilu = _silu_and_grad(gate_ref[...])
            dy = dy_ref[...]
            dbg_ref[...] = (dy * o * dsilu).astype(BF16)
            do = dy * silu
            dwbar = jnp.sum(do * o, axis=1, keepdims=True)
            for gi in range(3):
                ogs[gi][...] = w[gi] * do
                cgs[gi][...] = -w[gi] * dwbar

        for gi, (_, dil) in enumerate(PATTERNS):
            @pl.when(p == 3 + gi)
            def _(gi=gi, dil=dil):
                nb = S // dil // BLK
                c, a, b = tabs_v
                tmp[...] = _rope(q_ref[...], c, a, b)
                _deinterleave(qd, tmp, dil, BF16)
                tmp[...] = _rope(k_ref[...], c, a, b)
                _deinterleave(kd, tmp, dil, BF16)
                _deinterleave(vd, v_ref, dil, BF16)
                _deinterleave(dod, ogs[gi], dil, BF16)
                _deinterleave(ld, lgs[gi], dil)
                _deinterleave(cd, cgs[gi], dil)
                dkd[...] = jnp.zeros_like(dkd)
                dvd[...] = jnp.zeros_like(dvd)

                def unit(u, carry):
                    o0, p0, q, s_own, s_prev = _unit_scores(u, nb, qd, kd)
                    lse = ld[pl.ds(o0, BLK), :]
                    cv = cd[pl.ds(o0, BLK), :]
                    do = dod[pl.ds(o0, BLK), :]
                    p_own = jnp.exp(s_own - lse)
                    ds_own = (p_own * (_dot(do, vd[pl.ds(o0, BLK), :], NT) + cv) * SCALE).astype(BF16)
                    dq = _dot(ds_own, kd[pl.ds(o0, BLK), :])
                    dkd[pl.ds(o0, BLK), :] += _dot(ds_own, q, TN)
                    dvd[pl.ds(o0, BLK), :] += _dot(p_own.astype(BF16), do, TN)
                    if s_prev is not None:
                        p_prev = jnp.exp(s_prev - lse)
                        ds_prev = (p_prev * (_dot(do, vd[pl.ds(p0, BLK), :], NT) + cv) * SCALE).astype(BF16)
                        dq = dq + _dot(ds_prev, kd[pl.ds(p0, BLK), :])
                        dkd[pl.ds(p0, BLK), :] += _dot(ds_prev, q, TN)
                        dvd[pl.ds(p0, BLK), :] += _dot(p_prev.astype(BF16), do, TN)
                    dqd[pl.ds(o0, BLK), :] = dq
                    return carry

                lax.fori_loop(0, S // BLK, unit, 0)
                _interleave(tmp, dqd, dil)
                dq_ref[...] = _rope_t(tmp[...], c, a, b).astype(BF16)
                _interleave(tmp, dkd, dil)
                dk_ref[...] = _rope_t(tmp[...], c, a, b).astype(BF16)
                _interleave(tmp, dvd, dil)
                dv_ref[...] = tmp[...].astype(BF16)

    tab = pl.BlockSpec((S, HEAD_DIM), lambda h, p: (0, 0))
    hspec = lambda base: pl.BlockSpec((S, HEAD_DIM), lambda h, p: (0, base + h))
    gspec = pl.BlockSpec((S, HEAD_DIM), lambda h, p: (0, jnp.maximum(p - 3, 0) * 8 + h))
    slab = lambda: pltpu.VMEM((S, HEAD_DIM), F32)
    return pl.pallas_call(
        body, grid=(8, 6),
        in_specs=[_head_spec(Q_COL), _head_spec(K_COL), _head_spec(V_COL), hspec(BG_COL), hspec(8), tab, tab, tab],
        out_specs=[gspec, gspec, gspec, hspec(0)],
        out_shape=[jax.ShapeDtypeStruct((S, 3072), BF16)] * 3 + [jax.ShapeDtypeStruct((S, HALF), BF16)],
        scratch_shapes=ATTN_SCRATCH_FWD + [slab(), slab(), slab(), pltpu.VMEM((S, HEAD_DIM), BF16),
                                           slab(), slab(), slab(), slab()],
        compiler_params=_params(("parallel", "arbitrary"), VMEM_BIG), name="attn_bwd",
    )(z0, z0, z0, z0, dycat, *tabs)


SGU_CH = 256
NCHUNK = TR // 128


def _ln_stats(x):
    mu = jnp.mean(x, axis=-1, keepdims=True)
    xc = x - mu
    r = lax.rsqrt(jnp.mean(xc * xc, axis=-1, keepdims=True) + EPS)
    return xc * r, r


def _ln_bwd(dy, xhat, r, g):
    dxh = dy * g
    return r * (dxh - jnp.mean(dxh, axis=-1, keepdims=True) - xhat * jnp.mean(dxh * xhat, axis=-1, keepdims=True))


def _tril_bf16(w):
    row = lax.broadcasted_iota(jnp.int32, w.shape, 0)
    col = lax.broadcasted_iota(jnp.int32, w.shape, 1)
    return jnp.where(row >= col, w, 0.0).astype(BF16)


def _sgu_gate(vn_s, s_s, w_ref, bb_ref):
    for h in range(4):
        wm = _tril_bf16(w_ref[h])
        bias = bb_ref[h]
        for ch in range(NCHUNK):
            rows, cols = slice(ch * 128, (ch + 1) * 128), slice(h * SGU_CH, (h + 1) * SGU_CH)
            s_s[rows, cols] = _dot(wm, vn_s[rows, cols]) + jnp.concatenate([bias, bias], axis=1)


def _conv_fwd(i, dval_ref, dglu_ref, hval_ref, hglu_ref, cw_ref, cb_ref, xw, dcs):
    halo = hval_ref[...] * _sigmoid(hglu_ref[...])
    xw[0:HALO, :] = jnp.where(i > 0, halo, 0.0)
    xw[HALO:HALO + TR, :] = dval_ref[...] * _sigmoid(dglu_ref[...])
    for rb in range(TR // SUB):
        acc = jnp.broadcast_to(cb_ref[...], (SUB, HALF))
        for k in range(CONV_K):
            acc = acc + cw_ref[k:k + 1, :] * xw[pl.ds(rb * SUB + HALO - (CONV_K - 1) + k, SUB), :]
        dcs[rb * SUB:(rb + 1) * SUB, :] = acc


def _odd_in_specs():
    col = lambda j: pl.BlockSpec((TR, HALF), lambda i, *_: (i, j))
    prev = lambda j: pl.BlockSpec((HALO, HALF), lambda i, *_: (jnp.maximum(i * (TR // HALO) - 1, 0), j))
    return [col(0), col(1), col(2), col(3), col(4), col(5), prev(3), prev(4)]


def _full_spec(shape):
    return pl.BlockSpec(shape, lambda i, *_: (0,) * len(shape))


def _odd_fwd(z1, sgu_g, sgu_b, sgu_w, sgu_bb, conv_w, conv_b, cn_g, cn_b):
    def body(u_ref, v_ref, cg_ref, dval_ref, dglu_ref, dgate_ref, hval_ref, hglu_ref,
             g_ref, b_ref, w_ref, bb_ref, cw_ref, cb_ref, cng_ref, cnb_ref, out_ref, vn_s, s_s, xw, dcs):
        i = pl.program_id(0)
        vhat, _ = _ln_stats(v_ref[...])
        vn_s[...] = (vhat * g_ref[...] + b_ref[...]).astype(BF16)
        _sgu_gate(vn_s, s_s, w_ref, bb_ref)
        cg = cg_ref[...]
        out_ref[:, 0:HALF] = (u_ref[...] * s_s[...] * (cg * _sigmoid(cg))).astype(BF16)
        _conv_fwd(i, dval_ref, dglu_ref, hval_ref, hglu_ref, cw_ref, cb_ref, xw, dcs)
        dhat, _ = _ln_stats(dcs[...])
        dn = dhat * cng_ref[...] + cnb_ref[...]
        dgate = dgate_ref[...]
        out_ref[:, HALF:2 * HALF] = ((dn * _sigmoid(dn)) * (dgate * _sigmoid(dgate))).astype(BF16)

    vec = _full_spec((1, HALF))
    return pl.pallas_call(
        body, grid=(S // TR,),
        in_specs=_odd_in_specs() + [vec, vec, _full_spec((4, 128, 128)), _full_spec((4, 128, 128)),
                                    _full_spec((HALO, HALF)), vec, vec, vec],
        out_specs=pl.BlockSpec((TR, 2048), lambda i: (i, 0)),
        out_shape=jax.ShapeDtypeStruct((S, 2048), BF16),
        scratch_shapes=[pltpu.VMEM((TR, HALF), BF16), pltpu.VMEM((TR, HALF), F32),
                        pltpu.VMEM((HALO + TR, HALF), F32), pltpu.VMEM((TR, HALF), F32)],
        compiler_params=_params(("parallel",), VMEM_BIG), name="odd_fwd",
    )(z1, z1, z1, z1, z1, z1, z1, z1, sgu_g, sgu_b, sgu_w, sgu_bb, conv_w, conv_b, cn_g, cn_b)


def _odd_bwd_a(z1, dycat, sgu_g, sgu_b, sgu_w, sgu_bb, conv_w, conv_b, cn_g, cn_b):
    def body(u_ref, v_ref, cg_ref, dval_ref, dglu_ref, dgate_ref, hval_ref, hglu_ref, dy_ref,
             g_ref, b_ref, w_ref, bb_ref, cw_ref, cb_ref, cng_ref, cnb_ref,
             dz_ref, ddc_ref, dw_ref, dbb_ref, dg_ref, db_ref, dcng_ref, dcnb_ref, dcb_ref,
             vn_s, s_s, xw, dcs, ds_s, dvn_s):
        i = pl.program_id(0)
        vhat, rv = _ln_stats(v_ref[...])
        g = g_ref[...]
        vn_s[...] = (vhat * g + b_ref[...]).astype(BF16)
        _sgu_gate(vn_s, s_s, w_ref, bb_ref)
        silu_c, dsilu_c = _silu_and_grad(cg_ref[...])
        dyc = dy_ref[:, 0:HALF]
        u = u_ref[...]
        s = s_s[...]
        dz_ref[:, 0:HALF] = (dyc * s * silu_c).astype(BF16)
        dz_ref[:, 2 * HALF:3 * HALF] = (dyc * u * s * dsilu_c).astype(BF16)
        ds_s[...] = dyc * u * silu_c

        @pl.when(i == 0)
        def _():
            dw_ref[...] = jnp.zeros_like(dw_ref)
            dbb_ref[...] = jnp.zeros_like(dbb_ref)

        tril = lax.broadcasted_iota(jnp.int32, (128, 128), 0) >= lax.broadcasted_iota(jnp.int32, (128, 128), 1)
        for h in range(4):
            wm = _tril_bf16(w_ref[h])
            for ch in range(NCHUNK):
                rows, cols = slice(ch * 128, (ch + 1) * 128), slice(h * SGU_CH, (h + 1) * SGU_CH)
                ds = ds_s[rows, cols]
                dsb = ds.astype(BF16)
                dw_ref[h] += jnp.where(tril, _dot(dsb, vn_s[rows, cols], NT), 0.0)
                dbb_ref[h] += jnp.broadcast_to(jnp.sum(ds, axis=1, keepdims=True), (128, 128))
                dvn_s[rows, cols] = _dot(wm, dsb, TN)
        dvn = dvn_s[...]
        _acc_rows(dg_ref, dvn * vhat, i)
        _acc_rows(db_ref, dvn, i)
        dz_ref[:, HALF:2 * HALF] = _ln_bwd(dvn, vhat, rv, g).astype(BF16)

        _conv_fwd(i, dval_ref, dglu_ref, hval_ref, hglu_ref, cw_ref, cb_ref, xw, dcs)
        dhat, rd = _ln_stats(dcs[...])
        cng = cng_ref[...]
        silu_n, dsilu_n = _silu_and_grad(dhat * cng + cnb_ref[...])
        silu_g, dsilu_g = _silu_and_grad(dgate_ref[...])
        dyd = dy_ref[:, HALF:2 * HALF]
        dz_ref[:, 5 * HALF:6 * HALF] = (dyd * silu_n * dsilu_g).astype(BF16)
        ddn = dyd * silu_g * dsilu_n
        _acc_rows(dcng_ref, ddn * dhat, i)
        _acc_rows(dcnb_ref, ddn, i)
        ddc = _ln_bwd(ddn, dhat, rd, cng)
        ddc_ref[...] = ddc
        _acc_rows(dcb_ref, ddc, i)

    vec = _full_spec((1, HALF))
    sq = _full_spec((4, 128, 128))
    return pl.pallas_call(
        body, grid=(S // TR,),
        in_specs=_odd_in_specs() + [pl.BlockSpec((TR, 2048), lambda i: (i, 0)),
                                    vec, vec, sq, sq, _full_spec((HALO, HALF)), vec, vec, vec],
        out_specs=[pl.BlockSpec((TR, ODD_IN), lambda i: (i, 0)), pl.BlockSpec((TR, HALF), lambda i: (i, 0)),
                   sq, sq, vec, vec, vec, vec, vec],
        out_shape=[jax.ShapeDtypeStruct((S, ODD_IN), BF16), jax.ShapeDtypeStruct((S, HALF), F32),
                   jax.ShapeDtypeStruct((4, 128, 128), F32), jax.ShapeDtypeStruct((4, 128, 128), F32)]
                  + [jax.ShapeDtypeStruct((1, HALF), F32)] * 5,
        scratch_shapes=[pltpu.VMEM((TR, HALF), BF16), pltpu.VMEM((TR, HALF), F32),
                        pltpu.VMEM((HALO + TR, HALF), F32), pltpu.VMEM((TR, HALF), F32),
                        pltpu.VMEM((TR, HALF), F32), pltpu.VMEM((TR, HALF), F32)],
        compiler_params=_params(("arbitrary",), VMEM_BIG), name="odd_bwd_a",
    )(z1, z1, z1, z1, z1, z1, z1, z1, dycat, sgu_g, sgu_b, sgu_w, sgu_bb, conv_w, conv_b, cn_g, cn_b)


def _odd_bwd_b(z1, ddc, dz1, conv_w):
    nt = S // TR

    def body(dval_ref, dglu_ref, hval_ref, hglu_ref, ddc_ref, hddc_ref, cw_ref, dz_in_ref,
             dz_ref, dcw_ref, xw, dwin, dxs):
        del dz_in_ref
        i, j = pl.program_id(0), pl.program_id(1)
        sg = _sigmoid(dglu_ref[...])
        dval = dval_ref[...]

        @pl.when(j == 0)
        def _():
            halo = hval_ref[...] * _sigmoid(hglu_ref[...])
            xw[0:HALO, :] = jnp.where(i > 0, halo, 0.0)
            xw[HALO:HALO + TR, :] = dval * sg
            dwin[0:TR, :] = ddc_ref[...]
            dwin[TR:TR + HALO, :] = jnp.where(i < nt - 1, hddc_ref[...], 0.0)

            @pl.when(i == 0)
            def _():
                dcw_ref[...] = jnp.zeros_like(dcw_ref)

            for rb in range(TR // SUB):
                acc = jnp.zeros((SUB, HALF), F32)
                for k in range(CONV_K):
                    acc = acc + cw_ref[k:k + 1, :] * dwin[pl.ds(rb * SUB + (CONV_K - 1) - k, SUB), :]
                dxs[rb * SUB:(rb + 1) * SUB, :] = acc
            for k in range(CONV_K):
                acc = jnp.zeros((SUB, HALF), F32)
                for rb in range(TR // SUB):
                    acc = acc + dwin[rb * SUB:(rb + 1) * SUB, :] * xw[pl.ds(rb * SUB + HALO - (CONV_K - 1) + k, SUB), :]
                dcw_ref[k:k + 1, :] += jnp.sum(acc, axis=0, keepdims=True)
            dz_ref[...] = (dxs[...] * sg).astype(BF16)

        @pl.when(j == 1)
        def _():
            dz_ref[...] = (dxs[...] * dval * sg * (1.0 - sg)).astype(BF16)

    col = lambda c: pl.BlockSpec((TR, HALF), lambda i, j: (i, c))
    prev = lambda c: pl.BlockSpec((HALO, HALF), lambda i, j: (jnp.maximum(i * (TR // HALO) - 1, 0), c))
    nxt = pl.BlockSpec((HALO, HALF), lambda i, j: (jnp.minimum((i + 1) * (TR // HALO), S // HALO - 1), 0))
    return pl.pallas_call(
        body, grid=(nt, 2),
        in_specs=[col(3), col(4), prev(3), prev(4), pl.BlockSpec((TR, HALF), lambda i, j: (i, 0)), nxt,
                  _full_spec((HALO, HALF)), pl.BlockSpec(memory_space=pl.ANY)],
        out_specs=[pl.BlockSpec((TR, HALF), lambda i, j: (i, 3 + j)), _full_spec((HALO, HALF))],
        out_shape=[jax.ShapeDtypeStruct((S, ODD_IN), BF16), jax.ShapeDtypeStruct((HALO, HALF), F32)],
        scratch_shapes=[pltpu.VMEM((HALO + TR, HALF), F32), pltpu.VMEM((TR + HALO, HALF), F32),
                        pltpu.VMEM((TR, HALF), F32)],
        input_output_aliases={7: 0},
        compiler_params=_params(("arbitrary", "arbitrary"), VMEM_BIG), name="odd_bwd_b",
    )(z1, z1, z1, z1, ddc, ddc, conv_w, dz1)


def _cast_bf16(w, name):
    r, c = w.shape
    tr = min(r, 256)
    def body(i_ref, o_ref):
        o_ref[...] = i_ref[...].astype(BF16)

    return pl.pallas_call(
        body, grid=(r // tr,), in_specs=[pl.BlockSpec((tr, c), lambda i: (i, 0))],
        out_specs=pl.BlockSpec((tr, c), lambda i: (i, 0)), out_shape=jax.ShapeDtypeStruct((r, c), BF16),
        compiler_params=_params(("parallel",)), name=name,
    )(w)


def _adamw(w, g, m, v):
    m = ADAM_B1 * m + (1.0 - ADAM_B1) * g
    v = ADAM_B2 * v + (1.0 - ADAM_B2) * (g * g)
    m_hat = m / (1.0 - ADAM_B1 ** ADAM_STEP)
    v_hat = v / (1.0 - ADAM_B2 ** ADAM_STEP)
    delta = -ADAM_LR * (m_hat / (jnp.sqrt(v_hat) + ADAM_EPS) + ADAM_WD * w)
    return delta, m, v


def _adam_reduce(parts, w, m, v, name):
    r, c = w.shape
    tr = min(r, 128)

    def body(p_ref, w_ref, m_ref, v_ref, g_ref, d_ref, nm_ref, nv_ref):
        g = p_ref[0].astype(F32)
        for d in range(1, NDEV):
            g = g + p_ref[d].astype(F32)
        g_ref[...] = g
        d_ref[...], nm_ref[...], nv_ref[...] = _adamw(w_ref[...], g, m_ref[...], v_ref[...])

    spec = pl.BlockSpec((tr, c), lambda i: (i, 0))
    return pl.pallas_call(
        body, grid=(r // tr,), in_specs=[pl.BlockSpec((NDEV, tr, c), lambda i: (0, i, 0)), spec, spec, spec],
        out_specs=[spec] * 4, out_shape=[jax.ShapeDtypeStruct((r, c), F32)] * 4,
        compiler_params=_params(("parallel",), VMEM_BIG), name=name,
    )(parts, w, m, v)


def _sum_parts(parts, name):
    r = parts.shape[1]
    tr = 8
    for cand in (512, 256, 128, 64, 32, 16, 8):
        if r % cand == 0:
            tr = cand
            break

    def body(p_ref, o_ref):
        g = p_ref[0]
        for d in range(1, NDEV):
            g = g + p_ref[d]
        o_ref[...] = g

    return pl.pallas_call(
        body, grid=(r // tr,), in_specs=[pl.BlockSpec((NDEV, tr, 128), lambda i: (0, i, 0))],
        out_specs=pl.BlockSpec((tr, 128), lambda i: (i, 0)), out_shape=jax.ShapeDtypeStruct((r, 128), F32),
        compiler_params=_params(("parallel",)), name=name,
    )(parts)


def _adam_plain(w, g, m, v, name):
    r, c = w.shape

    def body(w_ref, g_ref, m_ref, v_ref, d_ref, nm_ref, nv_ref):
        d_ref[...], nm_ref[...], nv_ref[...] = _adamw(w_ref[...], g_ref[...], m_ref[...], v_ref[...])

    spec = pl.BlockSpec((r, c), lambda i: (0, 0))
    return pl.pallas_call(
        body, grid=(1,), in_specs=[spec] * 4, out_specs=[spec] * 3,
        out_shape=[jax.ShapeDtypeStruct((r, c), F32)] * 3,
        compiler_params=_params(("arbitrary",)), name=name,
    )(w, g, m, v)


MASKS = [(mx, my, mc) for mx in (0, 1) for my in (0, 1) for mc in (0, 1)][1:]


def _exchange(arrays, scatter, name):
    nt = len(arrays)
    out_shape = [jax.ShapeDtypeStruct(((NDEV,) + a.shape) if not scatter else a.shape, a.dtype) for a in arrays]

    def body(*refs):
        ins, outs = refs[:nt], refs[nt:2 * nt]
        send_sems, recv_sems, local_sems = refs[2 * nt:]
        x, y, c = lax.axis_index("x"), lax.axis_index("y"), lax.axis_index("c")
        me = 4 * x + 2 * y + c
        copies = []
        for t in range(nt):
            src_own = ins[t].at[me] if scatter else ins[t]
            loc = pltpu.make_async_copy(src_own, outs[t].at[me], local_sems.at[t])
            loc.start()
            copies.append(loc)
            for k, (mx, my, mc) in enumerate(MASKS):
                px, py, pc = (x + mx) % 2, (y + my) % 2, (c + mc) % 2
                peer = 4 * px + 2 * py + pc
                src = ins[t].at[peer] if scatter else ins[t]
                rc = pltpu.make_async_remote_copy(
                    src_ref=src, dst_ref=outs[t].at[me], send_sem=send_sems.at[t, k], recv_sem=recv_sems.at[t, k],
                    device_id=(px, py, pc), device_id_type=MESH)
                rc.start()
                copies.append(rc)
        for cp in copies:
            cp.wait()

    hbm = pl.BlockSpec(memory_space=pl.ANY)
    return pl.pallas_call(
        body, in_specs=[hbm] * nt, out_specs=[hbm] * nt, out_shape=out_shape,
        scratch_shapes=[pltpu.SemaphoreType.DMA((nt, 7)), pltpu.SemaphoreType.DMA((nt, 7)),
                        pltpu.SemaphoreType.DMA((nt,))],
        name=name,
    )(*arrays)


SEM_SPEC = pl.BlockSpec(memory_space=pltpu.SEMAPHORE)
EFFECT = pltpu.SideEffectType.DATAFLOW_SIDE_EFFECTING


def _direct_plan(scatter):
    def plan(x, y, c, srcs, lands):
        me = 4 * x + 2 * y + c
        local, remote = [], []
        for src, land in zip(srcs, lands):
            local.append((src.at[me] if scatter else src, land.at[me]))
            for mx, my, mc in MASKS:
                px, py, pc = (x + mx) % 2, (y + my) % 2, (c + mc) % 2
                blk = src.at[4 * px + 2 * py + pc] if scatter else src
                remote.append((blk, land.at[me], (px, py, pc)))
        return local, remote
    return plan


def _split_start(name, srcs, land_shapes, plan, n_local, n_remote, dep=None):
    ns, nl = len(srcs), len(land_shapes)
    deps = [] if dep is None else [dep]
    lands = [lax.empty(s.shape, s.dtype) for s in land_shapes]

    def body(*refs):
        ins, lz = refs[:ns], refs[ns:ns + nl]
        outs = refs[ns + nl + len(deps):]
        send_sems, recv_sems, token, local_sems = outs[0], outs[1], outs[2 + ns + nl], outs[3 + ns + nl]
        local, remote = plan(lax.axis_index("x"), lax.axis_index("y"), lax.axis_index("c"), ins, lz)
        own = [pltpu.make_async_copy(src, dst, local_sems.at[i]) for i, (src, dst) in enumerate(local)]
        for cp in own:
            cp.start()
        for cp in own:
            cp.wait()
        for k, (src, dst, peer) in enumerate(remote):
            pltpu.make_async_remote_copy(src_ref=src, dst_ref=dst, send_sem=send_sems.at[k], recv_sem=recv_sems.at[k],
                                         device_id=peer, device_id_type=MESH).start()
        token[...] = jnp.zeros_like(token)

    hbm = lambda a: pltpu.HBM(a.shape, a.dtype)
    outs = pl.pallas_call(
        body, name=name,
        out_shape=(pltpu.SemaphoreType.DMA((n_remote,)), pltpu.SemaphoreType.DMA((n_remote,)),
                   *[hbm(a) for a in srcs], *[hbm(a) for a in lands], jax.ShapeDtypeStruct((8, 128), F32)),
        in_specs=[ANY_SPEC] * (ns + nl + len(deps)),
        out_specs=(SEM_SPEC, SEM_SPEC, *[ANY_SPEC] * (ns + nl), pl.BlockSpec(memory_space=pltpu.VMEM)),
        scratch_shapes=[pltpu.SemaphoreType.DMA((n_local,))],
        input_output_aliases={i: 2 + i for i in range(ns + nl)},
        compiler_params=pltpu.CompilerParams(has_side_effects=EFFECT),
    )(*[pltpu.with_memory_space_constraint(a, pltpu.HBM) for a in srcs],
      *[pltpu.with_memory_space_constraint(a, pltpu.HBM) for a in lands], *deps)
    return dict(sems=outs[:2], srcs=outs[2:2 + ns], lands=outs[2 + ns:2 + ns + nl], token=outs[-1],
                plan=plan, n_remote=n_remote)


def _split_wait(name, handle, after):
    srcs, lands, plan = handle["srcs"], handle["lands"], handle["plan"]
    ns, nl = len(srcs), len(lands)

    def body(*refs):
        ins, lz = refs[:ns], refs[ns:ns + nl]
        send_sems, recv_sems = refs[ns + nl], refs[ns + nl + 1]
        _, remote = plan(lax.axis_index("x"), lax.axis_index("y"), lax.axis_index("c"), ins, lz)
        for k, (src, dst, peer) in enumerate(remote):
            cp = pltpu.make_async_remote_copy(src_ref=src, dst_ref=dst, send_sem=send_sems.at[k],
                                              recv_sem=recv_sems.at[k], device_id=peer, device_id_type=MESH)
            cp.wait_send()
            cp.wait_recv()

    hbm = lambda a: pltpu.HBM(a.shape, a.dtype)
    outs = pl.pallas_call(
        body, name=name, out_shape=(*[hbm(a) for a in srcs], *[hbm(a) for a in lands]),
        in_specs=[ANY_SPEC] * (ns + nl) + [SEM_SPEC, SEM_SPEC, ANY_SPEC], out_specs=tuple([ANY_SPEC] * (ns + nl)),
        input_output_aliases={i: i for i in range(ns + nl)},
        compiler_params=pltpu.CompilerParams(has_side_effects=EFFECT),
    )(*srcs, *lands, *handle["sems"], after)
    return list(outs[ns:])


SMALL = {
    "e_pre_norm": ((2048,), None), "e_pool_w": ((4, 256, 256), 1), "e_pool_scale": ((1024,), None),
    "e_post_norm": ((2048,), None), "o_pre_norm": ((2048,), 0), "o_sgu_norm_g": ((1024,), 0),
    "o_sgu_norm_b": ((1024,), 0), "o_sgu_w": ((4, 128, 128), None), "o_sgu_b": ((4, 128), None),
    "o_conv_w": ((31, 1024), 1), "o_conv_b": ((1024,), 0), "o_conv_norm_g": ((1024,), 0),
    "o_conv_norm_b": ((1024,), 0), "o_post_norm": ((2048,), 0),
}
SMALL_SHARDED = [n for n, (_, ax) in SMALL.items() if ax is not None]


def _shard_shape(name):
    shape, ax = SMALL[name]
    if ax is None:
        return shape
    return tuple(s // NDEV if i == ax else s for i, s in enumerate(shape))


def _pack(arrs, row_multiple=1):
    flat = jnp.concatenate([a.reshape(-1) for a in arrs])
    pad = -flat.shape[0] % (128 * row_multiple)
    return jnp.concatenate([flat, jnp.zeros((pad,), F32)]).reshape(-1, 128)


def _unpack(buf, shapes):
    flat = buf.reshape(-1)
    out, off = [], 0
    for shp in shapes:
        n = int(np.prod(shp))
        out.append(flat[off:off + n].reshape(shp))
        off += n
    return out


def _take_shard(full, name, me):
    shape, ax = SMALL[name]
    if ax is None:
        return full
    n = shape[ax] // NDEV
    return lax.dynamic_slice_in_dim(full, me * n, n, axis=ax)


BIG = ("e_w_in", "e_w_out", "o_w_in", "o_w_out")
WEIGHTS = ["e_pre_norm", "e_w_in", "e_pool_w", "e_pool_scale", "e_w_out", "e_post_norm", "o_pre_norm", "o_w_in",
           "o_sgu_norm_g", "o_sgu_norm_b", "o_sgu_w", "o_sgu_b", "o_conv_w", "o_conv_b", "o_conv_norm_g",
           "o_conv_norm_b", "o_w_out", "o_post_norm"]


def kernel(x, e_pre_norm, e_w_in, e_pool_w, e_pool_scale, e_w_out, e_post_norm, o_pre_norm, o_w_in, o_sgu_norm_g, o_sgu_norm_b, o_sgu_w, o_sgu_b, o_conv_w, o_conv_b, o_conv_norm_g, o_conv_norm_b, o_w_out, o_post_norm, loss_target, m_e_pre_norm, m_e_w_in, m_e_pool_w, m_e_pool_scale, m_e_w_out, m_e_post_norm, m_o_pre_norm, m_o_w_in, m_o_sgu_norm_g, m_o_sgu_norm_b, m_o_sgu_w, m_o_sgu_b, m_o_conv_w, m_o_conv_b, m_o_conv_norm_g, m_o_conv_norm_b, m_o_w_out, m_o_post_norm, v_e_pre_norm, v_e_w_in, v_e_pool_w, v_e_pool_scale, v_e_w_out, v_e_post_norm, v_o_pre_norm, v_o_w_in, v_o_sgu_norm_g, v_o_sgu_norm_b, v_o_sgu_w, v_o_sgu_b, v_o_conv_w, v_o_conv_b, v_o_conv_norm_g, v_o_conv_norm_b, v_o_w_out, v_o_post_norm):
    given = dict(locals())
    w = {n: given[n][0] for n in WEIGHTS}
    m = {n: given["m_" + n][0] for n in WEIGHTS}
    v = {n: given["v_" + n][0] for n in WEIGHTS}
    me = 4 * lax.axis_index("x") + 2 * lax.axis_index("y") + lax.axis_index("c")
    x, target = x[0], loss_target[0]
    row = lambda a: a.reshape(1, -1)
    gathered = lambda a: jax.ShapeDtypeStruct((NDEV,) + a.shape, a.dtype)

    def gather_start(name, arrs, dep=None):
        return _split_start(name, arrs, [gathered(a) for a in arrs], _direct_plan(False), len(arrs), 7 * len(arrs), dep)

    def scatter_start(name, arrs, dep=None):
        return _split_start(name, arrs, arrs, _direct_plan(True), len(arrs), 7 * len(arrs), dep)

    bf = {n: _cast_bf16(w[n], "cast_" + n) for n in BIG}
    ga = gather_start("gather_a_start", [bf["e_w_in"], _pack([w[n] for n in SMALL_SHARDED])])
    gb = gather_start("gather_b_start", [bf["e_w_out"], bf["o_w_in"], bf["o_w_out"]], ga["token"])
    h0 = _pre0_fwd(x, row(w["e_pre_norm"]), gb["token"])
    wg_e_in, small_rows = _split_wait("gather_a_wait", ga, h0)
    p = {n: w[n] for n in SMALL if SMALL[n][1] is None}
    small_rows = small_rows.reshape(NDEV, -1)
    off = 0
    for n in SMALL_SHARDED:
        shp, ax = _shard_shape(n), SMALL[n][1]
        cnt = int(np.prod(shp))
        blk = small_rows[:, off:off + cnt].reshape((NDEV,) + shp)
        p[n] = jnp.moveaxis(blk, 0, ax).reshape(SMALL[n][0])
        off += cnt
    tabs = _rope_tables()
    pool_w_bf = p["e_pool_w"].astype(BF16)
    sgu_bb = jnp.broadcast_to(p["o_sgu_b"][:, :, None], (4, 128, 128))
    conv_w = jnp.concatenate([p["o_conv_w"], jnp.zeros((HALO - CONV_K, HALF), F32)], axis=0)
    odd_p = (row(p["o_sgu_norm_g"]), row(p["o_sgu_norm_b"]), p["o_sgu_w"], sgu_bb, conv_w,
             row(p["o_conv_b"]), row(p["o_conv_norm_g"]), row(p["o_conv_norm_b"]))

    z0 = _mm_in(h0, wg_e_in, "mm_z0")
    ycat0 = _pool_fwd(z0, pool_w_bf, row(p["e_pool_scale"]))
    ycat0 = _attn_fwd(z0, ycat0, tabs)
    wg_e_out, wg_o_in, wg_o_out = _split_wait("gather_b_wait", gb, ycat0)
    w_out_e, w_out_o = wg_e_out.reshape(2048, D), wg_o_out.reshape(2048, D)
    y0 = _mm_out(ycat0, w_out_e, "mm_y0")
    x1, h1 = _post0_fwd(x, y0, row(p["e_post_norm"]), row(p["o_pre_norm"]))
    z1 = _mm_in(h1, wg_o_in, "mm_z1")
    ycat1 = _odd_fwd(z1, *odd_p)
    y1 = _mm_out(ycat1, w_out_o, "mm_y1")

    g = {}
    loss, dx2, dy1, g["o_post_norm"] = _post1_bwd(y1, x1, target, row(p["o_post_norm"]))
    loss = lax.psum(loss[0, 0], ("x", "y", "c"))
    s1 = scatter_start("scatter_o_w_out_start", [_mm_out_dw(ycat1, dy1, "mm_dwout1").reshape(NDEV, 256, D)])
    dycat1 = _mm_out_dx(dy1, w_out_o, "mm_dycat1", s1["token"])
    dz1, ddc, g["o_sgu_w"], d_sgu_bb, g["o_sgu_norm_g"], g["o_sgu_norm_b"], g["o_conv_norm_g"], \
        g["o_conv_norm_b"], g["o_conv_b"] = _odd_bwd_a(z1, dycat1, *odd_p)
    dz1, d_conv_w = _odd_bwd_b(z1, ddc, dz1, conv_w)
    g["o_sgu_b"] = d_sgu_bb[:, :, 0]
    g["o_conv_w"] = d_conv_w[:CONV_K]
    s2 = scatter_start("scatter_o_w_in_start", [_mm_in_dw(h1, dz1, ODD_IN // NDEV, "mm_dwin1")])
    dh1 = _mm_in_dx(dz1, wg_o_in, "mm_dh1", s2["token"])
    dx1, dy0, g["o_pre_norm"], g["e_post_norm"] = _mid_bwd(dx2, dh1, x1, y0, row(p["o_pre_norm"]),
                                                           row(p["e_post_norm"]))
    s3 = scatter_start("scatter_e_w_out_start", [_mm_out_dw(ycat0, dy0, "mm_dwout0").reshape(NDEV, 256, D)])
    dycat0 = _mm_out_dx(dy0, w_out_e, "mm_dycat0", s3["token"])
    da_in, da_gate, g["e_pool_w"], g["e_pool_scale"] = _pool_bwd(z0, dycat0, pool_w_bf, row(p["e_pool_scale"]))
    dq, dk, dv, dbg = _attn_bwd(z0, dycat0, tabs)
    dz0 = jnp.concatenate([da_in, da_gate, dq, dk, dv, dbg], axis=1)
    late = [n for n in SMALL if n != "e_pre_norm"]
    s4 = scatter_start("scatter_e_w_in_start", [_mm_in_dw(h0, dz0, EVEN_IN // NDEV, "mm_dwin0")])
    s5 = gather_start("gather_small_grads_start", [_pack([g[n].reshape(SMALL[n][0]) for n in late], 512)], s4["token"])
    dh0 = _mm_in_dx(dz0, wg_e_in, "mm_dh0", s5["token"])
    grad_x, g["e_pre_norm"] = _pre0_bwd(dx1, dh0, x, row(p["e_pre_norm"]))
    last = _exchange([g["e_pre_norm"].reshape(16, 128)], False, "gather_e_pre_norm_grad")[0]

    grads, deltas, new_m, new_v = {}, {}, {}, {}
    for n, handle in (("o_w_out", s1), ("o_w_in", s2), ("e_w_out", s3), ("e_w_in", s4)):
        parts, = _split_wait("scatter_" + n + "_wait", handle, last)
        grads[n], deltas[n], new_m[n], new_v[n] = _adam_reduce(parts, w[n], m[n], v[n], "adam_" + n)
    recv_small, = _split_wait("gather_small_grads_wait", s5, last)
    g_small = dict(zip(late, _unpack(_sum_parts(recv_small, "sum_small_grads"), [SMALL[n][0] for n in late])))
    g_small["e_pre_norm"] = _sum_parts(last, "sum_e_pre_norm_grad").reshape(2048)
    for n in SMALL:
        grads[n] = _take_shard(g_small[n], n, me)
    names = list(SMALL)
    shapes = [_shard_shape(n) for n in names]
    d_pack, m_pack, v_pack = _adam_plain(_pack([w[n] for n in names]), _pack([grads[n] for n in names]),
                                         _pack([m[n] for n in names]), _pack([v[n] for n in names]), "adam_small")
    for n, d_, m_, v_ in zip(names, _unpack(d_pack, shapes), _unpack(m_pack, shapes), _unpack(v_pack, shapes)):
        deltas[n], new_m[n], new_v[n] = d_, m_, v_

    lead = lambda a: a[None]
    return (loss, grad_x[None], *[lead(grads[n]) for n in WEIGHTS], *[lead(deltas[n]) for n in WEIGHTS],
            *[lead(new_m[n]) for n in WEIGHTS], *[lead(new_v[n]) for n in WEIGHTS])
```

```python
import functools

import numpy as np
import jax
import jax.numpy as jnp
from jax import lax
from jax.experimental import pallas as pl
from jax.experimental.pallas import tpu as pltpu
from jax.experimental.pallas import tpu_sc as plsc

F32 = jnp.float32
BF16 = jnp.bfloat16

S = 2048
D = 2048
NDEV = 8
EPS = 1e-6
NEG = -1e30
HEAD_DIM = 128
ROT_DIM = 32
ROPE_THETA = 500000.0
PATTERNS = ((128, 1), (512, 4), (2048, 16))
BLK = 128
EVEN_IN = 12288
ODD_IN = 6144
HALF = 1024
CONV_K = 31
HALO = 32
TR = 256
SUB = 32

ADAM_LR = 0.001
ADAM_B1 = 0.9
ADAM_B2 = 0.999
ADAM_EPS = 1e-08
ADAM_WD = 0.01
ADAM_STEP = 10

VMEM_BIG = 56 * 1024 * 1024
MESH = pl.DeviceIdType.MESH

NN = (((1,), (0,)), ((), ()))
NT = (((1,), (1,)), ((), ()))
TN = (((0,), (0,)), ((), ()))


def _dot(a, b, dn=NN):
    return lax.dot_general(a, b, dn, preferred_element_type=F32)


def _sigmoid(x):
    return 1.0 / (1.0 + jnp.exp(-x))


def _silu_and_grad(x):
    sg = _sigmoid(x)
    return x * sg, sg * (1.0 + x * (1.0 - sg))


def _params(sem, vmem=None):
    return pltpu.CompilerParams(dimension_semantics=sem, vmem_limit_bytes=vmem)


ANY_SPEC = pl.BlockSpec(memory_space=pl.ANY)


def _matmul(a, b, *, dn, grid, a_spec, b_spec, o_spec, out_shape, out_dtype, acc_shape, name, dep=None):
    nk = grid[2]
    deps = [] if dep is None else [dep]

    def body(a_ref, b_ref, *rest):
        o_ref, acc = rest[len(deps)], rest[len(deps) + 1:]
        if nk == 1:
            o_ref[...] = _dot(a_ref[...], b_ref[...], dn).astype(o_ref.dtype)
            return
        acc_ref = acc[0]
        k = pl.program_id(2)

        @pl.when(k == 0)
        def _():
            acc_ref[...] = jnp.zeros_like(acc_ref)

        acc_ref[...] += _dot(a_ref[...], b_ref[...], dn)

        @pl.when(k == nk - 1)
        def _():
            o_ref[...] = acc_ref[...].astype(o_ref.dtype)

    return pl.pallas_call(
        body, grid=grid, in_specs=[a_spec, b_spec] + [ANY_SPEC] * len(deps), out_specs=o_spec,
        out_shape=jax.ShapeDtypeStruct(out_shape, out_dtype),
        scratch_shapes=[] if nk == 1 else [pltpu.VMEM(acc_shape, F32)],
        compiler_params=_params(("parallel", "parallel", "arbitrary"), VMEM_BIG), name=name,
    )(a, b, *deps)


TM = 512


def _mm_in(h, wg, name):
    nb = wg.shape[2]
    tn = 512 if nb % 512 == 0 else nb
    per = nb // tn
    return _matmul(
        h, wg, dn=NN, grid=(S // TM, NDEV * per, 1),
        a_spec=pl.BlockSpec((TM, D), lambda i, j, k: (i, 0)),
        b_spec=pl.BlockSpec((None, D, tn), lambda i, j, k: (j // per, 0, j % per)),
        o_spec=pl.BlockSpec((TM, tn), lambda i, j, k: (i, j)),
        out_shape=(S, NDEV * nb), out_dtype=F32, acc_shape=(TM, tn), name=name)


def _mm_in_dx(dz, wg, name, dep=None):
    nb = wg.shape[2]
    return _matmul(
        dz, wg, dn=NT, grid=(S // TM, D // 512, NDEV),
        a_spec=pl.BlockSpec((TM, nb), lambda i, j, k: (i, k)),
        b_spec=pl.BlockSpec((None, 512, nb), lambda i, j, k: (k, j, 0)),
        o_spec=pl.BlockSpec((TM, 512), lambda i, j, k: (i, j)),
        out_shape=(S, D), out_dtype=F32, acc_shape=(TM, 512), name=name, dep=dep)


def _mm_in_dw(h, dz, nb, name):
    tn = 512 if nb % 512 == 0 else nb
    per = nb // tn
    return _matmul(
        h, dz, dn=TN, grid=(D // TM, NDEV * per, 1),
        a_spec=pl.BlockSpec((S, TM), lambda i, j, k: (0, i)),
        b_spec=pl.BlockSpec((S, tn), lambda i, j, k: (0, j)),
        o_spec=pl.BlockSpec((None, TM, tn), lambda i, j, k: (j // per, i, j % per)),
        out_shape=(NDEV, D, nb), out_dtype=BF16, acc_shape=(TM, tn), name=name)


def _mm_out(yc, w, name):
    return _matmul(
        yc, w, dn=NN, grid=(S // TM, D // 512, 1),
        a_spec=pl.BlockSpec((TM, 2048), lambda i, j, k: (i, 0)),
        b_spec=pl.BlockSpec((2048, 512), lambda i, j, k: (0, j)),
        o_spec=pl.BlockSpec((TM, 512), lambda i, j, k: (i, j)),
        out_shape=(S, D), out_dtype=F32, acc_shape=(TM, 512), name=name)


def _mm_out_dx(dy, w, name, dep=None):
    return _matmul(
        dy, w, dn=NT, grid=(S // TM, 2048 // 512, 1),
        a_spec=pl.BlockSpec((TM, D), lambda i, j, k: (i, 0)),
        b_spec=pl.BlockSpec((512, D), lambda i, j, k: (j, 0)),
        o_spec=pl.BlockSpec((TM, 512), lambda i, j, k: (i, j)),
        out_shape=(S, 2048), out_dtype=F32, acc_shape=(TM, 512), name=name, dep=dep)


def _mm_out_dw(yc, dy, name):
    return _matmul(
        yc, dy, dn=TN, grid=(2048 // TM, D // 512, 1),
        a_spec=pl.BlockSpec((S, TM), lambda i, j, k: (0, i)),
        b_spec=pl.BlockSpec((S, 512), lambda i, j, k: (0, j)),
        o_spec=pl.BlockSpec((TM, 512), lambda i, j, k: (i, j)),
        out_shape=(2048, D), out_dtype=BF16, acc_shape=(TM, 512), name=name)


def _row_spec(w=D):
    return pl.BlockSpec((TR, w), lambda i: (i, 0))


def _vec_spec(w=D):
    return pl.BlockSpec((1, w), lambda i: (0, 0))


def _rms_stats(x):
    r = lax.rsqrt(jnp.mean(x * x, axis=-1, keepdims=True) + EPS)
    return x * r, r


def _rms_bwd(dn, xhat, r, g):
    dxh = dn * g
    return r * (dxh - xhat * jnp.mean(dxh * xhat, axis=-1, keepdims=True))


def _acc_rows(ref, val, i):
    s = jnp.sum(val, axis=0, keepdims=True)

    @pl.when(i == 0)
    def _():
        ref[...] = s

    @pl.when(i > 0)
    def _():
        ref[...] += s


def _pre0_fwd(x, g, dep=None):
    deps = [] if dep is None else [dep]

    def body(x_ref, g_ref, *rest):
        xhat, _ = _rms_stats(x_ref[...])
        rest[-1][...] = (xhat * g_ref[...]).astype(BF16)

    return pl.pallas_call(
        body, grid=(S // TR,), in_specs=[_row_spec(), _vec_spec()] + [ANY_SPEC] * len(deps), out_specs=_row_spec(),
        out_shape=jax.ShapeDtypeStruct((S, D), BF16), compiler_params=_params(("parallel",)), name="pre0_fwd",
    )(x, g, *deps)


def _post0_fwd(x, y0, g_post, g_pre1):
    def body(x_ref, y_ref, gp_ref, g1_ref, x1_ref, h1_ref):
        yhat, _ = _rms_stats(y_ref[...])
        x1 = x_ref[...] + yhat * gp_ref[...]
        x1_ref[...] = x1
        xhat, _ = _rms_stats(x1)
        h1_ref[...] = (xhat * g1_ref[...]).astype(BF16)

    return pl.pallas_call(
        body, grid=(S // TR,), in_specs=[_row_spec(), _row_spec(), _vec_spec(), _vec_spec()],
        out_specs=[_row_spec(), _row_spec()],
        out_shape=[jax.ShapeDtypeStruct((S, D), F32), jax.ShapeDtypeStruct((S, D), BF16)],
        compiler_params=_params(("parallel",)), name="post0_fwd",
    )(x, y0, g_post, g_pre1)


def _post1_bwd(y1, x1, target, g_post):
    def body(y_ref, x1_ref, t_ref, g_ref, loss_ref, dx2_ref, dy_ref, dg_ref):
        i = pl.program_id(0)
        yhat, r = _rms_stats(y_ref[...])
        g = g_ref[...]
        err = x1_ref[...] + yhat * g - t_ref[...]
        part = jnp.sum(jnp.sum(err * err, axis=-1, keepdims=True), axis=0, keepdims=True) * (0.5 / D)
        _acc_rows(loss_ref, jnp.broadcast_to(part, (1, 128)), i)
        dx2 = err * (1.0 / D)
        dx2_ref[...] = dx2
        _acc_rows(dg_ref, dx2 * yhat, i)
        dy_ref[...] = _rms_bwd(dx2, yhat, r, g).astype(BF16)

    return pl.pallas_call(
        body, grid=(S // TR,), in_specs=[_row_spec(), _row_spec(), _row_spec(), _vec_spec()],
        out_specs=[_vec_spec(128), _row_spec(), _row_spec(), _vec_spec()],
        out_shape=[jax.ShapeDtypeStruct((1, 128), F32), jax.ShapeDtypeStruct((S, D), F32),
                   jax.ShapeDtypeStruct((S, D), BF16), jax.ShapeDtypeStruct((1, D), F32)],
        compiler_params=_params(("arbitrary",)), name="post1_bwd",
    )(y1, x1, target, g_post)


def _mid_bwd(dx2, dh1, x1, y0, g_pre1, g_post0):
    def body(dx2_ref, dh_ref, x1_ref, y_ref, g1_ref, gp_ref, dx1_ref, dy_ref, dg1_ref, dgp_ref):
        i = pl.program_id(0)
        xhat, r1 = _rms_stats(x1_ref[...])
        dh = dh_ref[...]
        _acc_rows(dg1_ref, dh * xhat, i)
        dx1 = dx2_ref[...] + _rms_bwd(dh, xhat, r1, g1_ref[...])
        dx1_ref[...] = dx1
        yhat, r0 = _rms_stats(y_ref[...])
        _acc_rows(dgp_ref, dx1 * yhat, i)
        dy_ref[...] = _rms_bwd(dx1, yhat, r0, gp_ref[...]).astype(BF16)

    return pl.pallas_call(
        body, grid=(S // TR,),
        in_specs=[_row_spec(), _row_spec(), _row_spec(), _row_spec(), _vec_spec(), _vec_spec()],
        out_specs=[_row_spec(), _row_spec(), _vec_spec(), _vec_spec()],
        out_shape=[jax.ShapeDtypeStruct((S, D), F32), jax.ShapeDtypeStruct((S, D), BF16),
                   jax.ShapeDtypeStruct((1, D), F32), jax.ShapeDtypeStruct((1, D), F32)],
        compiler_params=_params(("arbitrary",)), name="mid_bwd",
    )(dx2, dh1, x1, y0, g_pre1, g_post0)


def _pre0_bwd(dx1, dh0, x, g):
    def body(dx1_ref, dh_ref, x_ref, g_ref, gx_ref, dg_ref):
        i = pl.program_id(0)
        xhat, r = _rms_stats(x_ref[...])
        dh = dh_ref[...]
        _acc_rows(dg_ref, dh * xhat, i)
        gx_ref[...] = dx1_ref[...] + _rms_bwd(dh, xhat, r, g_ref[...])

    return pl.pallas_call(
        body, grid=(S // TR,), in_specs=[_row_spec(), _row_spec(), _row_spec(), _vec_spec()],
        out_specs=[_row_spec(), _vec_spec()],
        out_shape=[jax.ShapeDtypeStruct((S, D), F32), jax.ShapeDtypeStruct((1, D), F32)],
        compiler_params=_params(("arbitrary",)), name="pre0_bwd",
    )(dx1, dh0, x, g)


POOL_CH = 256


def _pool_apply(a, w, transpose):
    n = a.shape[0]
    row = lax.broadcasted_iota(jnp.int32, a.shape, 0)
    cnt = jnp.minimum(row + 1, w).astype(F32)
    s = a / cnt if transpose else a
    for k in (1, 2, 4, 8):
        if transpose:
            sh = jnp.where(row < n - k, pltpu.roll(s, n - k, 0), 0.0)
        else:
            sh = jnp.where(row >= k, pltpu.roll(s, k, 0), 0.0)
        s = jnp.where(w > k, s + sh, s)
    return s - a if transpose else s / cnt - a


def _pool_fwd(z0, pool_w, pool_scale):
    def body(a_ref, gate_ref, w_ref, sc_ref, out_ref):
        win = jnp.left_shift(2, pl.program_id(0))
        pooled = _pool_apply(a_ref[...], win, False)
        mixed = _dot(pooled.astype(BF16), w_ref[...])
        gate = gate_ref[...]
        out_ref[...] = (mixed * sc_ref[...] * (gate * _sigmoid(gate))).astype(BF16)

    return pl.pallas_call(
        body, grid=(4,),
        in_specs=[pl.BlockSpec((S, POOL_CH), lambda g: (0, g)), pl.BlockSpec((S, POOL_CH), lambda g: (0, 4 + g)),
                  pl.BlockSpec((None, POOL_CH, POOL_CH), lambda g: (g, 0, 0)),
                  pl.BlockSpec((1, POOL_CH), lambda g: (0, g))],
        out_specs=pl.BlockSpec((S, POOL_CH), lambda g: (0, g)),
        out_shape=jax.ShapeDtypeStruct((S, 2048), BF16),
        compiler_params=_params(("parallel",), VMEM_BIG), name="pool_fwd",
    )(z0, z0, pool_w, pool_scale)


def _pool_bwd(z0, dycat, pool_w, pool_scale):
    def body(a_ref, gate_ref, dy_ref, w_ref, sc_ref, da_ref, dgate_ref, dw_ref, dsc_ref):
        win = jnp.left_shift(2, pl.program_id(0))
        pooled = _pool_apply(a_ref[...], win, False).astype(BF16)
        w = w_ref[...]
        mixed = _dot(pooled, w)
        silu, dsilu = _silu_and_grad(gate_ref[...])
        dy = dy_ref[...]
        sc = sc_ref[...]
        dgate_ref[...] = (dy * (mixed * sc) * dsilu).astype(BF16)
        dms = dy * silu
        dsc_ref[...] = jnp.sum(dms * mixed, axis=0, keepdims=True)
        dmixed = (dms * sc).astype(BF16)
        dw_ref[...] = _dot(pooled, dmixed, TN)
        dpooled = _dot(dmixed, w, NT)
        da_ref[...] = _pool_apply(dpooled, win, True).astype(BF16)

    slab = lambda off: pl.BlockSpec((S, POOL_CH), lambda g: (0, off + g))
    return pl.pallas_call(
        body, grid=(4,),
        in_specs=[slab(0), slab(4), slab(0), pl.BlockSpec((None, POOL_CH, POOL_CH), lambda g: (g, 0, 0)),
                  pl.BlockSpec((1, POOL_CH), lambda g: (0, g))],
        out_specs=[slab(0), slab(0), pl.BlockSpec((None, POOL_CH, POOL_CH), lambda g: (g, 0, 0)),
                   pl.BlockSpec((1, POOL_CH), lambda g: (0, g))],
        out_shape=[jax.ShapeDtypeStruct((S, HALF), BF16), jax.ShapeDtypeStruct((S, HALF), BF16),
                   jax.ShapeDtypeStruct((4, POOL_CH, POOL_CH), F32), jax.ShapeDtypeStruct((1, HALF), F32)],
        compiler_params=_params(("parallel",), VMEM_BIG), name="pool_bwd",
    )(z0, z0, dycat, pool_w, pool_scale)


Q_COL, K_COL, V_COL, BG_COL = 2048 // 128, 5120 // 128, 8192 // 128, 11264 // 128
SCALE = HEAD_DIM ** -0.5


def _rope_tables():
    pos = jnp.arange(S, dtype=F32)
    inv_freq = jnp.power(ROPE_THETA, -jnp.arange(0, ROT_DIM, 2, dtype=F32) / ROT_DIM)
    ang = pos[:, None] * inv_freq[None, :]
    cos, sin = jnp.cos(ang), jnp.sin(ang)
    half = ROT_DIM // 2
    zeros = jnp.zeros((S, HEAD_DIM - ROT_DIM), F32)
    c = jnp.concatenate([cos, cos, jnp.ones((S, HEAD_DIM - ROT_DIM), F32)], axis=1)
    a = jnp.concatenate([-sin, jnp.zeros((S, half), F32), zeros], axis=1)
    b = jnp.concatenate([jnp.zeros((S, half), F32), sin, zeros], axis=1)
    return c, a, b


def _rope(t, c, a, b):
    half = ROT_DIM // 2
    return t * c + pltpu.roll(t, HEAD_DIM - half, 1) * a + pltpu.roll(t, half, 1) * b


def _rope_t(d, c, a, b):
    half = ROT_DIM // 2
    return d * c + pltpu.roll(d * a, half, 1) + pltpu.roll(d * b, HEAD_DIM - half, 1)


def _deinterleave(dst, src, dil, cast=None):
    length = S // dil
    for r in range(dil):
        v = src[...] if dil == 1 else src[pl.ds(r, length, stride=dil), :]
        dst[r * length:(r + 1) * length, :] = v if cast is None else v.astype(cast)


def _interleave(dst, src, dil):
    length = S // dil
    for r in range(dil):
        if dil == 1:
            dst[...] = src[...]
        else:
            dst[pl.ds(r, length, stride=dil), :] = src[r * length:(r + 1) * length, :]


def _unit_scores(u, nb, qd, kd):
    o0 = pl.multiple_of(u * BLK, BLK)
    p0 = pl.multiple_of(jnp.maximum(u - 1, 0) * BLK, BLK)
    q = qd[pl.ds(o0, BLK), :]
    row = lax.broadcasted_iota(jnp.int32, (BLK, BLK), 0)
    col = lax.broadcasted_iota(jnp.int32, (BLK, BLK), 1)
    s_own = jnp.where(col <= row, _dot(q, kd[pl.ds(o0, BLK), :], NT) * SCALE, NEG)
    if nb == 1:
        return o0, p0, q, s_own, None
    has_prev = (u % nb) != 0
    s_prev = jnp.where((col >= row) & has_prev, _dot(q, kd[pl.ds(p0, BLK), :], NT) * SCALE, NEG)
    return o0, p0, q, s_own, s_prev


def _attn_group_fwd(dil, q_ref, k_ref, v_ref, tabs, tmp, qd, kd, vd, od, ld, og, lg):
    nb = S // dil // BLK
    c, a, b = tabs
    tmp[...] = _rope(q_ref[...], c, a, b)
    _deinterleave(qd, tmp, dil, BF16)
    tmp[...] = _rope(k_ref[...], c, a, b)
    _deinterleave(kd, tmp, dil, BF16)
    _deinterleave(vd, v_ref, dil, BF16)

    def unit(u, carry):
        o0, p0, _, s_own, s_prev = _unit_scores(u, nb, qd, kd)
        m = jnp.max(s_own, axis=1, keepdims=True)
        if s_prev is not None:
            m = jnp.maximum(m, jnp.max(s_prev, axis=1, keepdims=True))
        p_own = jnp.exp(s_own - m)
        den = jnp.sum(p_own, axis=1, keepdims=True)
        acc = _dot(p_own.astype(BF16), vd[pl.ds(o0, BLK), :])
        if s_prev is not None:
            p_prev = jnp.exp(s_prev - m)
            den = den + jnp.sum(p_prev, axis=1, keepdims=True)
            acc = acc + _dot(p_prev.astype(BF16), vd[pl.ds(p0, BLK), :])
        od[pl.ds(o0, BLK), :] = acc / den
        ld[pl.ds(o0, BLK), :] = jnp.broadcast_to(m + jnp.log(den), (BLK, HEAD_DIM))
        return carry

    lax.fori_loop(0, S // BLK, unit, 0)
    _interleave(og, od, dil)
    _interleave(lg, ld, dil)


def _group_weights(lgs):
    l0, l1, l2 = lgs[0][...], lgs[1][...], lgs[2][...]
    mx = jnp.maximum(l0, jnp.maximum(l1, l2))
    e0, e1, e2 = jnp.exp(l0 - mx), jnp.exp(l1 - mx), jnp.exp(l2 - mx)
    den = e0 + e1 + e2
    return e0 / den, e1 / den, e2 / den


def _head_spec(base, ngroups_axis=True):
    return pl.BlockSpec((S, HEAD_DIM), lambda h, p: (0, base + (p % 3) * 8 + h))


ATTN_SCRATCH_FWD = [
    pltpu.VMEM((S, HEAD_DIM), F32),
    pltpu.VMEM((S, HEAD_DIM), BF16), pltpu.VMEM((S, HEAD_DIM), BF16), pltpu.VMEM((S, HEAD_DIM), BF16),
    pltpu.VMEM((S, HEAD_DIM), F32), pltpu.VMEM((S, HEAD_DIM), F32),
    pltpu.VMEM((S, HEAD_DIM), F32), pltpu.VMEM((S, HEAD_DIM), F32), pltpu.VMEM((S, HEAD_DIM), F32),
    pltpu.VMEM((S, HEAD_DIM), F32), pltpu.VMEM((S, HEAD_DIM), F32), pltpu.VMEM((S, HEAD_DIM), F32),
]


def _attn_fwd(z0, ycat, tabs):
    def body(q_ref, k_ref, v_ref, gate_ref, c_ref, a_ref, b_ref, ycat_ref, out_ref,
             tmp, qd, kd, vd, od, ld, og0, og1, og2, lg0, lg1, lg2):
        del ycat_ref
        p = pl.program_id(1)
        ogs, lgs = (og0, og1, og2), (lg0, lg1, lg2)
        tabs_v = (c_ref[...], a_ref[...], b_ref[...])
        for gi, (_, dil) in enumerate(PATTERNS):
            @pl.when(p == gi)
            def _(gi=gi, dil=dil):
                _attn_group_fwd(dil, q_ref, k_ref, v_ref, tabs_v, tmp, qd, kd, vd, od, ld, ogs[gi], lgs[gi])

        @pl.when(p == 2)
        def _():
            w0, w1, w2 = _group_weights(lgs)
            o = w0 * og0[...] + w1 * og1[...] + w2 * og2[...]
            gate = gate_ref[...]
            out_ref[...] = (o * (gate * _sigmoid(gate))).astype(BF16)

    tab = pl.BlockSpec((S, HEAD_DIM), lambda h, p: (0, 0))
    return pl.pallas_call(
        body, grid=(8, 3),
        in_specs=[_head_spec(Q_COL), _head_spec(K_COL), _head_spec(V_COL),
                  pl.BlockSpec((S, HEAD_DIM), lambda h, p: (0, BG_COL + h)), tab, tab, tab,
                  pl.BlockSpec(memory_space=pl.ANY)],
        out_specs=pl.BlockSpec((S, HEAD_DIM), lambda h, p: (0, 8 + h)),
        out_shape=jax.ShapeDtypeStruct((S, 2048), BF16),
        scratch_shapes=ATTN_SCRATCH_FWD, input_output_aliases={7: 0},
        compiler_params=_params(("parallel", "arbitrary"), VMEM_BIG), name="attn_fwd",
    )(z0, z0, z0, z0, *tabs, ycat)


def _attn_bwd(z0, dycat, tabs):
    def body(q_ref, k_ref, v_ref, gate_ref, dy_ref, c_ref, a_ref, b_ref,
             dq_ref, dk_ref, dv_ref, dbg_ref,
             tmp, qd, kd, vd, od, ld, og0, og1, og2, lg0, lg1, lg2, cg0, cg1, cg2, dod, cd, dqd, dkd, dvd):
        p = pl.program_id(1)
        ogs, lgs, cgs = (og0, og1, og2), (lg0, lg1, lg2), (cg0, cg1, cg2)
        tabs_v = (c_ref[...], a_ref[...], b_ref[...])
        for gi, (_, dil) in enumerate(PATTERNS):
            @pl.when(p == gi)
            def _(gi=gi, dil=dil):
                _attn_group_fwd(dil, q_ref, k_ref, v_ref, tabs_v, tmp, qd, kd, vd, od, ld, ogs[gi], lgs[gi])

        @pl.when(p == 2)
        def _():
            w = _group_weights(lgs)
            o = w[0] * og0[...] + w[1] * og1[...] + w[2] * og2[...]
            silu, dsilu = _silu_and_grad(gate_ref[...])
            dy = dy_ref[...]
            dbg_ref[...] = (dy * o * dsilu).astype(BF16)
            do = dy * silu
            dwbar = jnp.sum(do * o, axis=1, keepdims=True)
            for gi in range(3):
                ogs[gi][...] = w[gi] * do
                cgs[gi][...] = -w[gi] * dwbar

        for gi, (_, dil) in enumerate(PATTERNS):
            @pl.when(p == 3 + gi)
            def _(gi=gi, dil=dil):
                nb = S // dil // BLK
                c, a, b = tabs_v
                tmp[...] = _rope(q_ref[...], c, a, b)
                _deinterleave(qd, tmp, dil, BF16)
                tmp[...] = _rope(k_ref[...], c, a, b)
                _deinterleave(kd, tmp, dil, BF16)
                _deinterleave(vd, v_ref, dil, BF16)
                _deinterleave(dod, ogs[gi], dil, BF16)
                _deinterleave(ld, lgs[gi], dil)
                _deinterleave(cd, cgs[gi], dil)
                dkd[...] = jnp.zeros_like(dkd)
                dvd[...] = jnp.zeros_like(dvd)

                def unit(u, carry):
                    o0, p0, q, s_own, s_prev = _unit_scores(u, nb, qd, kd)
                    lse = ld[pl.ds(o0, BLK), :]
                    cv = cd[pl.ds(o0, BLK), :]
                    do = dod[pl.ds(o0, BLK), :]
                    p_own = jnp.exp(s_own - lse)
                    ds_own = (p_own * (_dot(do, vd[pl.ds(o0, BLK), :], NT) + cv) * SCALE).astype(BF16)
                    dq = _dot(ds_own, kd[pl.ds(o0, BLK), :])
                    dkd[pl.ds(o0, BLK), :] += _dot(ds_own, q, TN)
                    dvd[pl.ds(o0, BLK), :] += _dot(p_own.astype(BF16), do, TN)
                    if s_prev is not None:
                        p_prev = jnp.exp(s_prev - lse)
                        ds_prev = (p_prev * (_dot(do, vd[pl.ds(p0, BLK), :], NT) + cv) * SCALE).astype(BF16)
                        dq = dq + _dot(ds_prev, kd[pl.ds(p0, BLK), :])
                        dkd[pl.ds(p0, BLK), :] += _dot(ds_prev, q, TN)
                        dvd[pl.ds(p0, BLK), :] += _dot(p_prev.astype(BF16), do, TN)
                    dqd[pl.ds(o0, BLK), :] = dq
                    return carry

                lax.fori_loop(0, S // BLK, unit, 0)
                _interleave(tmp, dqd, dil)
                dq_ref[...] = _rope_t(tmp[...], c, a, b).astype(BF16)
                _interleave(tmp, dkd, dil)
                dk_ref[...] = _rope_t(tmp[...], c, a, b).astype(BF16)
                _interleave(tmp, dvd, dil)
                dv_ref[...] = tmp[...].astype(BF16)

    tab = pl.BlockSpec((S, HEAD_DIM), lambda h, p: (0, 0))
    hspec = lambda base: pl.BlockSpec((S, HEAD_DIM), lambda h, p: (0, base + h))
    gspec = pl.BlockSpec((S, HEAD_DIM), lambda h, p: (0, jnp.maximum(p - 3, 0) * 8 + h))
    slab = lambda: pltpu.VMEM((S, HEAD_DIM), F32)
    return pl.pallas_call(
        body, grid=(8, 6),
        in_specs=[_head_spec(Q_COL), _head_spec(K_COL), _head_spec(V_COL), hspec(BG_COL), hspec(8), tab, tab, tab],
        out_specs=[gspec, gspec, gspec, hspec(0)],
        out_shape=[jax.ShapeDtypeStruct((S, 3072), BF16)] * 3 + [jax.ShapeDtypeStruct((S, HALF), BF16)],
        scratch_shapes=ATTN_SCRATCH_FWD + [slab(), slab(), slab(), pltpu.VMEM((S, HEAD_DIM), BF16),
                                           slab(), slab(), slab(), slab()],
        compiler_params=_params(("parallel", "arbitrary"), VMEM_BIG), name="attn_bwd",
    )(z0, z0, z0, z0, dycat, *tabs)


SGU_CH = 256
NCHUNK = TR // 128


def _ln_stats(x):
    mu = jnp.mean(x, axis=-1, keepdims=True)
    xc = x - mu
    r = lax.rsqrt(jnp.mean(xc * xc, axis=-1, keepdims=True) + EPS)
    return xc * r, r


def _ln_bwd(dy, xhat, r, g):
    dxh = dy * g
    return r * (dxh - jnp.mean(dxh, axis=-1, keepdims=True) - xhat * jnp.mean(dxh * xhat, axis=-1, keepdims=True))


def _tril_bf16(w):
    row = lax.broadcasted_iota(jnp.int32, w.shape, 0)
    col = lax.broadcasted_iota(jnp.int32, w.shape, 1)
    return jnp.where(row >= col, w, 0.0).astype(BF16)


def _sgu_gate(vn_s, s_s, w_ref, bb_ref):
    for h in range(4):
        wm = _tril_bf16(w_ref[h])
        bias = bb_ref[h]
        for ch in range(NCHUNK):
            rows, cols = slice(ch * 128, (ch + 1) * 128), slice(h * SGU_CH, (h + 1) * SGU_CH)
            s_s[rows, cols] = _dot(wm, vn_s[rows, cols]) + jnp.concatenate([bias, bias], axis=1)


def _conv_fwd(i, dval_ref, dglu_ref, hval_ref, hglu_ref, cw_ref, cb_ref, xw, dcs):
    halo = hval_ref[...] * _sigmoid(hglu_ref[...])
    xw[0:HALO, :] = jnp.where(i > 0, halo, 0.0)
    xw[HALO:HALO + TR, :] = dval_ref[...] * _sigmoid(dglu_ref[...])
    for rb in range(TR // SUB):
        acc = jnp.broadcast_to(cb_ref[...], (SUB, HALF))
        for k in range(CONV_K):
            acc = acc + cw_ref[k:k + 1, :] * xw[pl.ds(rb * SUB + HALO - (CONV_K - 1) + k, SUB), :]
        dcs[rb * SUB:(rb + 1) * SUB, :] = acc


def _odd_in_specs():
    col = lambda j: pl.BlockSpec((TR, HALF), lambda i, *_: (i, j))
    prev = lambda j: pl.BlockSpec((HALO, HALF), lambda i, *_: (jnp.maximum(i * (TR // HALO) - 1, 0), j))
    return [col(0), col(1), col(2), col(3), col(4), col(5), prev(3), prev(4)]


def _full_spec(shape):
    return pl.BlockSpec(shape, lambda i, *_: (0,) * len(shape))


def _odd_fwd(z1, sgu_g, sgu_b, sgu_w, sgu_bb, conv_w, conv_b, cn_g, cn_b):
    def body(u_ref, v_ref, cg_ref, dval_ref, dglu_ref, dgate_ref, hval_ref, hglu_ref,
             g_ref, b_ref, w_ref, bb_ref, cw_ref, cb_ref, cng_ref, cnb_ref, out_ref, vn_s, s_s, xw, dcs):
        i = pl.program_id(0)
        vhat, _ = _ln_stats(v_ref[...])
        vn_s[...] = (vhat * g_ref[...] + b_ref[...]).astype(BF16)
        _sgu_gate(vn_s, s_s, w_ref, bb_ref)
        cg = cg_ref[...]
        out_ref[:, 0:HALF] = (u_ref[...] * s_s[...] * (cg * _sigmoid(cg))).astype(BF16)
        _conv_fwd(i, dval_ref, dglu_ref, hval_ref, hglu_ref, cw_ref, cb_ref, xw, dcs)
        dhat, _ = _ln_stats(dcs[...])
        dn = dhat * cng_ref[...] + cnb_ref[...]
        dgate = dgate_ref[...]
        out_ref[:, HALF:2 * HALF] = ((dn * _sigmoid(dn)) * (dgate * _sigmoid(dgate))).astype(BF16)

    vec = _full_spec((1, HALF))
    return pl.pallas_call(
        body, grid=(S // TR,),
        in_specs=_odd_in_specs() + [vec, vec, _full_spec((4, 128, 128)), _full_spec((4, 128, 128)),
                                    _full_spec((HALO, HALF)), vec, vec, vec],
        out_specs=pl.BlockSpec((TR, 2048), lambda i: (i, 0)),
        out_shape=jax.ShapeDtypeStruct((S, 2048), BF16),
        scratch_shapes=[pltpu.VMEM((TR, HALF), BF16), pltpu.VMEM((TR, HALF), F32),
                        pltpu.VMEM((HALO + TR, HALF), F32), pltpu.VMEM((TR, HALF), F32)],
        compiler_params=_params(("parallel",), VMEM_BIG), name="odd_fwd",
    )(z1, z1, z1, z1, z1, z1, z1, z1, sgu_g, sgu_b, sgu_w, sgu_bb, conv_w, conv_b, cn_g, cn_b)


def _odd_bwd_a(z1, dycat, sgu_g, sgu_b, sgu_w, sgu_bb, conv_w, conv_b, cn_g, cn_b):
    def body(u_ref, v_ref, cg_ref, dval_ref, dglu_ref, dgate_ref, hval_ref, hglu_ref, dy_ref,
             g_ref, b_ref, w_ref, bb_ref, cw_ref, cb_ref, cng_ref, cnb_ref,
             dz_ref, ddc_ref, dw_ref, dbb_ref, dg_ref, db_ref, dcng_ref, dcnb_ref, dcb_ref,
             vn_s, s_s, xw, dcs, ds_s, dvn_s):
        i = pl.program_id(0)
        vhat, rv = _ln_stats(v_ref[...])
        g = g_ref[...]
        vn_s[...] = (vhat * g + b_ref[...]).astype(BF16)
        _sgu_gate(vn_s, s_s, w_ref, bb_ref)
        silu_c, dsilu_c = _silu_and_grad(cg_ref[...])
        dyc = dy_ref[:, 0:HALF]
        u = u_ref[...]
        s = s_s[...]
        dz_ref[:, 0:HALF] = (dyc * s * silu_c).astype(BF16)
        dz_ref[:, 2 * HALF:3 * HALF] = (dyc * u * s * dsilu_c).astype(BF16)
        ds_s[...] = dyc * u * silu_c

        @pl.when(i == 0)
        def _():
            dw_ref[...] = jnp.zeros_like(dw_ref)
            dbb_ref[...] = jnp.zeros_like(dbb_ref)

        tril = lax.broadcasted_iota(jnp.int32, (128, 128), 0) >= lax.broadcasted_iota(jnp.int32, (128, 128), 1)
        for h in range(4):
            wm = _tril_bf16(w_ref[h])
            for ch in range(NCHUNK):
                rows, cols = slice(ch * 128, (ch + 1) * 128), slice(h * SGU_CH, (h + 1) * SGU_CH)
                ds = ds_s[rows, cols]
                dsb = ds.astype(BF16)
                dw_ref[h] += jnp.where(tril, _dot(dsb, vn_s[rows, cols], NT), 0.0)
                dbb_ref[h] += jnp.broadcast_to(jnp.sum(ds, axis=1, keepdims=True), (128, 128))
                dvn_s[rows, cols] = _dot(wm, dsb, TN)
        dvn = dvn_s[...]
        _acc_rows(dg_ref, dvn * vhat, i)
        _acc_rows(db_ref, dvn, i)
        dz_ref[:, HALF:2 * HALF] = _ln_bwd(dvn, vhat, rv, g).astype(BF16)

        _conv_fwd(i, dval_ref, dglu_ref, hval_ref, hglu_ref, cw_ref, cb_ref, xw, dcs)
        dhat, rd = _ln_stats(dcs[...])
        cng = cng_ref[...]
        silu_n, dsilu_n = _silu_and_grad(dhat * cng + cnb_ref[...])
        silu_g, dsilu_g = _silu_and_grad(dgate_ref[...])
        dyd = dy_ref[:, HALF:2 * HALF]
        dz_ref[:, 5 * HALF:6 * HALF] = (dyd * silu_n * dsilu_g).astype(BF16)
        ddn = dyd * silu_g * dsilu_n
        _acc_rows(dcng_ref, ddn * dhat, i)
        _acc_rows(dcnb_ref, ddn, i)
        ddc = _ln_bwd(ddn, dhat, rd, cng)
        ddc_ref[...] = ddc
        _acc_rows(dcb_ref, ddc, i)

    vec = _full_spec((1, HALF))
    sq = _full_spec((4, 128, 128))
    return pl.pallas_call(
        body, grid=(S // TR,),
        in_specs=_odd_in_specs() + [pl.BlockSpec((TR, 2048), lambda i: (i, 0)),
                                    vec, vec, sq, sq, _full_spec((HALO, HALF)), vec, vec, vec],
        out_specs=[pl.BlockSpec((TR, ODD_IN), lambda i: (i, 0)), pl.BlockSpec((TR, HALF), lambda i: (i, 0)),
                   sq, sq, vec, vec, vec, vec, vec],
        out_shape=[jax.ShapeDtypeStruct((S, ODD_IN), BF16), jax.ShapeDtypeStruct((S, HALF), F32),
                   jax.ShapeDtypeStruct((4, 128, 128), F32), jax.ShapeDtypeStruct((4, 128, 128), F32)]
                  + [jax.ShapeDtypeStruct((1, HALF), F32)] * 5,
        scratch_shapes=[pltpu.VMEM((TR, HALF), BF16), pltpu.VMEM((TR, HALF), F32),
                        pltpu.VMEM((HALO + TR, HALF), F32), pltpu.VMEM((TR, HALF), F32),
                        pltpu.VMEM((TR, HALF), F32), pltpu.VMEM((TR, HALF), F32)],
        compiler_params=_params(("arbitrary",), VMEM_BIG), name="odd_bwd_a",
    )(z1, z1, z1, z1, z1, z1, z1, z1, dycat, sgu_g, sgu_b, sgu_w, sgu_bb, conv_w, conv_b, cn_g, cn_b)


def _odd_bwd_b(z1, ddc, dz1, conv_w):
    nt = S // TR

    def body(dval_ref, dglu_ref, hval_ref, hglu_ref, ddc_ref, hddc_ref, cw_ref, dz_in_ref,
             dz_ref, dcw_ref, xw, dwin, dxs):
        del dz_in_ref
        i, j = pl.program_id(0), pl.program_id(1)
        sg = _sigmoid(dglu_ref[...])
        dval = dval_ref[...]

        @pl.when(j == 0)
        def _():
            halo = hval_ref[...] * _sigmoid(hglu_ref[...])
            xw[0:HALO, :] = jnp.where(i > 0, halo, 0.0)
            xw[HALO:HALO + TR, :] = dval * sg
            dwin[0:TR, :] = ddc_ref[...]
            dwin[TR:TR + HALO, :] = jnp.where(i < nt - 1, hddc_ref[...], 0.0)

            @pl.when(i == 0)
            def _():
                dcw_ref[...] = jnp.zeros_like(dcw_ref)

            for rb in range(TR // SUB):
                acc = jnp.zeros((SUB, HALF), F32)
                for k in range(CONV_K):
                    acc = acc + cw_ref[k:k + 1, :] * dwin[pl.ds(rb * SUB + (CONV_K - 1) - k, SUB), :]
                dxs[rb * SUB:(rb + 1) * SUB, :] = acc
            for k in range(CONV_K):
                acc = jnp.zeros((SUB, HALF), F32)
                for rb in range(TR // SUB):
                    acc = acc + dwin[rb * SUB:(rb + 1) * SUB, :] * xw[pl.ds(rb * SUB + HALO - (CONV_K - 1) + k, SUB), :]
                dcw_ref[k:k + 1, :] += jnp.sum(acc, axis=0, keepdims=True)
            dz_ref[...] = (dxs[...] * sg).astype(BF16)

        @pl.when(j == 1)
        def _():
            dz_ref[...] = (dxs[...] * dval * sg * (1.0 - sg)).astype(BF16)

    col = lambda c: pl.BlockSpec((TR, HALF), lambda i, j: (i, c))
    prev = lambda c: pl.BlockSpec((HALO, HALF), lambda i, j: (jnp.maximum(i * (TR // HALO) - 1, 0), c))
    nxt = pl.BlockSpec((HALO, HALF), lambda i, j: (jnp.minimum((i + 1) * (TR // HALO), S // HALO - 1), 0))
    return pl.pallas_call(
        body, grid=(nt, 2),
        in_specs=[col(3), col(4), prev(3), prev(4), pl.BlockSpec((TR, HALF), lambda i, j: (i, 0)), nxt,
                  _full_spec((HALO, HALF)), pl.BlockSpec(memory_space=pl.ANY)],
        out_specs=[pl.BlockSpec((TR, HALF), lambda i, j: (i, 3 + j)), _full_spec((HALO, HALF))],
        out_shape=[jax.ShapeDtypeStruct((S, ODD_IN), BF16), jax.ShapeDtypeStruct((HALO, HALF), F32)],
        scratch_shapes=[pltpu.VMEM((HALO + TR, HALF), F32), pltpu.VMEM((TR + HALO, HALF), F32),
                        pltpu.VMEM((TR, HALF), F32)],
        input_output_aliases={7: 0},
        compiler_params=_params(("arbitrary", "arbitrary"), VMEM_BIG), name="odd_bwd_b",
    )(z1, z1, z1, z1, ddc, ddc, conv_w, dz1)


def _cast_bf16(w, name):
    r, c = w.shape
    tr = min(r, 256)
    def body(i_ref, o_ref):
        o_ref[...] = i_ref[...].astype(BF16)

    return pl.pallas_call(
        body, grid=(r // tr,), in_specs=[pl.BlockSpec((tr, c), lambda i: (i, 0))],
        out_specs=pl.BlockSpec((tr, c), lambda i: (i, 0)), out_shape=jax.ShapeDtypeStruct((r, c), BF16),
        compiler_params=_params(("parallel",)), name=name,
    )(w)


def _adamw(w, g, m, v):
    m = ADAM_B1 * m + (1.0 - ADAM_B1) * g
    v = ADAM_B2 * v + (1.0 - ADAM_B2) * (g * g)
    m_hat = m / (1.0 - ADAM_B1 ** ADAM_STEP)
    v_hat = v / (1.0 - ADAM_B2 ** ADAM_STEP)
    delta = -ADAM_LR * (m_hat / (jnp.sqrt(v_hat) + ADAM_EPS) + ADAM_WD * w)
    return delta, m, v


def _adam_reduce(parts, w, m, v, name):
    r, c = w.shape
    tr = min(r, 128)

    def body(p_ref, w_ref, m_ref, v_ref, g_ref, d_ref, nm_ref, nv_ref):
        g = p_ref[0].astype(F32)
        for d in range(1, NDEV):
            g = g + p_ref[d].astype(F32)
        g_ref[...] = g
        d_ref[...], nm_ref[...], nv_ref[...] = _adamw(w_ref[...], g, m_ref[...], v_ref[...])

    spec = pl.BlockSpec((tr, c), lambda i: (i, 0))
    return pl.pallas_call(
        body, grid=(r // tr,), in_specs=[pl.BlockSpec((NDEV, tr, c), lambda i: (0, i, 0)), spec, spec, spec],
        out_specs=[spec] * 4, out_shape=[jax.ShapeDtypeStruct((r, c), F32)] * 4,
        compiler_params=_params(("parallel",), VMEM_BIG), name=name,
    )(parts, w, m, v)


def _sum_parts(parts, name):
    r = parts.shape[1]
    tr = 8
    for cand in (512, 256, 128, 64, 32, 16, 8):
        if r % cand == 0:
            tr = cand
            break

    def body(p_ref, o_ref):
        g = p_ref[0]
        for d in range(1, NDEV):
            g = g + p_ref[d]
        o_ref[...] = g

    return pl.pallas_call(
        body, grid=(r // tr,), in_specs=[pl.BlockSpec((NDEV, tr, 128), lambda i: (0, i, 0))],
        out_specs=pl.BlockSpec((tr, 128), lambda i: (i, 0)), out_shape=jax.ShapeDtypeStruct((r, 128), F32),
        compiler_params=_params(("parallel",)), name=name,
    )(parts)


def _adam_plain(w, g, m, v, name):
    r, c = w.shape

    def body(w_ref, g_ref, m_ref, v_ref, d_ref, nm_ref, nv_ref):
        d_ref[...], nm_ref[...], nv_ref[...] = _adamw(w_ref[...], g_ref[...], m_ref[...], v_ref[...])

    spec = pl.BlockSpec((r, c), lambda i: (0, 0))
    return pl.pallas_call(
        body, grid=(1,), in_specs=[spec] * 4, out_specs=[spec] * 3,
        out_shape=[jax.ShapeDtypeStruct((r, c), F32)] * 3,
        compiler_params=_params(("arbitrary",)), name=name,
    )(w, g, m, v)


MASKS = [(mx, my, mc) for mx in (0, 1) for my in (0, 1) for mc in (0, 1)][1:]


def _exchange(arrays, scatter, name):
    nt = len(arrays)
    out_shape = [jax.ShapeDtypeStruct(((NDEV,) + a.shape) if not scatter else a.shape, a.dtype) for a in arrays]

    def body(*refs):
        ins, outs = refs[:nt], refs[nt:2 * nt]
        send_sems, recv_sems, local_sems = refs[2 * nt:]
        x, y, c = lax.axis_index("x"), lax.axis_index("y"), lax.axis_index("c")
        me = 4 * x + 2 * y + c
        copies = []
        for t in range(nt):
            src_own = ins[t].at[me] if scatter else ins[t]
            loc = pltpu.make_async_copy(src_own, outs[t].at[me], local_sems.at[t])
            loc.start()
            copies.append(loc)
            for k, (mx, my, mc) in enumerate(MASKS):
                px, py, pc = (x + mx) % 2, (y + my) % 2, (c + mc) % 2
                peer = 4 * px + 2 * py + pc
                src = ins[t].at[peer] if scatter else ins[t]
                rc = pltpu.make_async_remote_copy(
                    src_ref=src, dst_ref=outs[t].at[me], send_sem=send_sems.at[t, k], recv_sem=recv_sems.at[t, k],
                    device_id=(px, py, pc), device_id_type=MESH)
                rc.start()
                copies.append(rc)
        for cp in copies:
            cp.wait()

    hbm = pl.BlockSpec(memory_space=pl.ANY)
    return pl.pallas_call(
        body, in_specs=[hbm] * nt, out_specs=[hbm] * nt, out_shape=out_shape,
        scratch_shapes=[pltpu.SemaphoreType.DMA((nt, 7)), pltpu.SemaphoreType.DMA((nt, 7)),
                        pltpu.SemaphoreType.DMA((nt,))],
        name=name,
    )(*arrays)


SEM_SPEC = pl.BlockSpec(memory_space=pltpu.SEMAPHORE)
EFFECT = pltpu.SideEffectType.DATAFLOW_SIDE_EFFECTING


def _direct_plan(scatter):
    def plan(x, y, c, srcs, lands):
        me = 4 * x + 2 * y + c
        local, remote = [], []
        for src, land in zip(srcs, lands):
            local.append((src.at[me] if scatter else src, land.at[me]))
            for mx, my, mc in MASKS:
                px, py, pc = (x + mx) % 2, (y + my) % 2, (c + mc) % 2
                blk = src.at[4 * px + 2 * py + pc] if scatter else src
                remote.append((blk, land.at[me], (px, py, pc)))
        return local, remote
    return plan


def _split_start(name, srcs, land_shapes, plan, n_local, n_remote, dep=None):
    ns, nl = len(srcs), len(land_shapes)
    deps = [] if dep is None else [dep]
    lands = [lax.empty(s.shape, s.dtype) for s in land_shapes]

    def body(*refs):
        ins, lz = refs[:ns], refs[ns:ns + nl]
        outs = refs[ns + nl + len(deps):]
        send_sems, recv_sems, token, local_sems = outs[0], outs[1], outs[2 + ns + nl], outs[3 + ns + nl]
        local, remote = plan(lax.axis_index("x"), lax.axis_index("y"), lax.axis_index("c"), ins, lz)
        own = [pltpu.make_async_copy(src, dst, local_sems.at[i]) for i, (src, dst) in enumerate(local)]
        for cp in own:
            cp.start()
        for cp in own:
            cp.wait()
        for k, (src, dst, peer) in enumerate(remote):
            pltpu.make_async_remote_copy(src_ref=src, dst_ref=dst, send_sem=send_sems.at[k], recv_sem=recv_sems.at[k],
                                         device_id=peer, device_id_type=MESH).start()
        token[...] = jnp.zeros_like(token)

    hbm = lambda a: pltpu.HBM(a.shape, a.dtype)
    outs = pl.pallas_call(
        body, name=name,
        out_shape=(pltpu.SemaphoreType.DMA((n_remote,)), pltpu.SemaphoreType.DMA((n_remote,)),
                   *[hbm(a) for a in srcs], *[hbm(a) for a in lands], jax.ShapeDtypeStruct((8, 128), F32)),
        in_specs=[ANY_SPEC] * (ns + nl + len(deps)),
        out_specs=(SEM_SPEC, SEM_SPEC, *[ANY_SPEC] * (ns + nl), pl.BlockSpec(memory_space=pltpu.VMEM)),
        scratch_shapes=[pltpu.SemaphoreType.DMA((n_local,))],
        input_output_aliases={i: 2 + i for i in range(ns + nl)},
        compiler_params=pltpu.CompilerParams(has_side_effects=EFFECT),
    )(*[pltpu.with_memory_space_constraint(a, pltpu.HBM) for a in srcs],
      *[pltpu.with_memory_space_constraint(a, pltpu.HBM) for a in lands], *deps)
    return dict(sems=outs[:2], srcs=outs[2:2 + ns], lands=outs[2 + ns:2 + ns + nl], token=outs[-1],
                plan=plan, n_remote=n_remote)


def _split_wait(name, handle, after):
    srcs, lands, plan = handle["srcs"], handle["lands"], handle["plan"]
    ns, nl = len(srcs), len(lands)

    def body(*refs):
        ins, lz = refs[:ns], refs[ns:ns + nl]
        send_sems, recv_sems = refs[ns + nl], refs[ns + nl + 1]
        _, remote = plan(lax.axis_index("x"), lax.axis_index("y"), lax.axis_index("c"), ins, lz)
        for k, (src, dst, peer) in enumerate(remote):
            cp = pltpu.make_async_remote_copy(src_ref=src, dst_ref=dst, send_sem=send_sems.at[k],
                                              recv_sem=recv_sems.at[k], device_id=peer, device_id_type=MESH)
            cp.wait_send()
            cp.wait_recv()

    hbm = lambda a: pltpu.HBM(a.shape, a.dtype)
    outs = pl.pallas_call(
        body, name=name, out_shape=(*[hbm(a) for a in srcs], *[hbm(a) for a in lands]),
        in_specs=[ANY_SPEC] * (ns + nl) + [SEM_SPEC, SEM_SPEC, ANY_SPEC], out_specs=tuple([ANY_SPEC] * (ns + nl)),
        input_output_aliases={i: i for i in range(ns + nl)},
        compiler_params=pltpu.CompilerParams(has_side_effects=EFFECT),
    )(*srcs, *lands, *handle["sems"], after)
    return list(outs[ns:])


def _sc_exchange(name, collective_id, arrays, scatter):
    nt = len(arrays)
    out_type = [jax.ShapeDtypeStruct(a.shape if scatter else (NDEV,) + a.shape, a.dtype) for a in arrays]

    def body(*refs):
        ins, outs = refs[:nt], refs[nt:2 * nt]
        send_sems, recv_sems, local_sems = refs[2 * nt:3 * nt], refs[3 * nt:4 * nt], refs[4 * nt:5 * nt]
        x, y, c = lax.axis_index("x"), lax.axis_index("y"), lax.axis_index("c")
        peers = [(mx + x - 2 * mx * x, my + y - 2 * my * y, mc + c - 2 * mc * c) for mx, my, mc in MASKS]
        barrier = pltpu.get_barrier_semaphore()
        for peer in peers:
            pl.semaphore_signal(barrier, inc=1, device_id=peer, device_id_type=MESH)
        pl.semaphore_wait(barrier, len(peers))
        me = 4 * x + 2 * y + c
        own = []
        for t in range(nt):
            cp = pltpu.make_async_copy(ins[t].at[me] if scatter else ins[t], outs[t].at[me], local_sems[t])
            cp.start()
            own.append(cp)
            for px, py, pc in peers:
                src = ins[t].at[4 * px + 2 * py + pc] if scatter else ins[t]
                pltpu.make_async_remote_copy(src_ref=src, dst_ref=outs[t].at[me], send_sem=send_sems[t],
                                             recv_sem=recv_sems[t], device_id=(px, py, pc), device_id_type=MESH).start()
        for t in range(nt):
            own[t].wait()
            seven = outs[t].at[pl.ds(0, NDEV - 1)]
            drain = pltpu.make_async_remote_copy(src_ref=seven, dst_ref=seven, send_sem=send_sems[t],
                                                 recv_sem=recv_sems[t], device_id=(x, y, c), device_id_type=MESH)
            drain.wait_send()
            drain.wait_recv()

    return pl.kernel(
        body, out_type=out_type, mesh=plsc.ScalarSubcoreMesh(axis_name="sequencer", num_cores=1),
        scratch_types=[pltpu.SemaphoreType.DMA] * (3 * nt),
        compiler_params=pltpu.CompilerParams(collective_id=collective_id), name=name,
    )(*arrays)


SMALL = {
    "e_pre_norm": ((2048,), None), "e_pool_w": ((4, 256, 256), 1), "e_pool_scale": ((1024,), None),
    "e_post_norm": ((2048,), None), "o_pre_norm": ((2048,), 0), "o_sgu_norm_g": ((1024,), 0),
    "o_sgu_norm_b": ((1024,), 0), "o_sgu_w": ((4, 128, 128), None), "o_sgu_b": ((4, 128), None),
    "o_conv_w": ((31, 1024), 1), "o_conv_b": ((1024,), 0), "o_conv_norm_g": ((1024,), 0),
    "o_conv_norm_b": ((1024,), 0), "o_post_norm": ((2048,), 0),
}
SMALL_SHARDED = [n for n, (_, ax) in SMALL.items() if ax is not None]


def _shard_shape(name):
    shape, ax = SMALL[name]
    if ax is None:
        return shape
    return tuple(s // NDEV if i == ax else s for i, s in enumerate(shape))


def _pack(arrs, row_multiple=1):
    flat = jnp.concatenate([a.reshape(-1) for a in arrs])
    pad = -flat.shape[0] % (128 * row_multiple)
    return jnp.concatenate([flat, jnp.zeros((pad,), F32)]).reshape(-1, 128)


def _unpack(buf, shapes):
    flat = buf.reshape(-1)
    out, off = [], 0
    for shp in shapes:
        n = int(np.prod(shp))
        out.append(flat[off:off + n].reshape(shp))
        off += n
    return out


def _take_shard(full, name, me):
    shape, ax = SMALL[name]
    if ax is None:
        return full
    n = shape[ax] // NDEV
    return lax.dynamic_slice_in_dim(full, me * n, n, axis=ax)


BIG = ("e_w_in", "e_w_out", "o_w_in", "o_w_out")
WEIGHTS = ["e_pre_norm", "e_w_in", "e_pool_w", "e_pool_scale", "e_w_out", "e_post_norm", "o_pre_norm", "o_w_in",
           "o_sgu_norm_g", "o_sgu_norm_b", "o_sgu_w", "o_sgu_b", "o_conv_w", "o_conv_b", "o_conv_norm_g",
           "o_conv_norm_b", "o_w_out", "o_post_norm"]


def kernel(x, e_pre_norm, e_w_in, e_pool_w, e_pool_scale, e_w_out, e_post_norm, o_pre_norm, o_w_in, o_sgu_norm_g, o_sgu_norm_b, o_sgu_w, o_sgu_b, o_conv_w, o_conv_b, o_conv_norm_g, o_conv_norm_b, o_w_out, o_post_norm, loss_target, m_e_pre_norm, m_e_w_in, m_e_pool_w, m_e_pool_scale, m_e_w_out, m_e_post_norm, m_o_pre_norm, m_o_w_in, m_o_sgu_norm_g, m_o_sgu_norm_b, m_o_sgu_w, m_o_sgu_b, m_o_conv_w, m_o_conv_b, m_o_conv_norm_g, m_o_conv_norm_b, m_o_w_out, m_o_post_norm, v_e_pre_norm, v_e_w_in, v_e_pool_w, v_e_pool_scale, v_e_w_out, v_e_post_norm, v_o_pre_norm, v_o_w_in, v_o_sgu_norm_g, v_o_sgu_norm_b, v_o_sgu_w, v_o_sgu_b, v_o_conv_w, v_o_conv_b, v_o_conv_norm_g, v_o_conv_norm_b, v_o_w_out, v_o_post_norm):
    given = dict(locals())
    w = {n: given[n][0] for n in WEIGHTS}
    m = {n: given["m_" + n][0] for n in WEIGHTS}
    v = {n: given["v_" + n][0] for n in WEIGHTS}
    me = 4 * lax.axis_index("x") + 2 * lax.axis_index("y") + lax.axis_index("c")
    x, target = x[0], loss_target[0]
    row = lambda a: a.reshape(1, -1)

    bf = {n: _cast_bf16(w[n], "cast_" + n) for n in BIG}
    wg_e_in, small_rows = _sc_exchange("gather_a", 0, [bf["e_w_in"], _pack([w[n] for n in SMALL_SHARDED])], False)
    wg_e_out, wg_o_in, wg_o_out = _sc_exchange("gather_b", 1, [bf["e_w_out"], bf["o_w_in"], bf["o_w_out"]], False)
    h0 = _pre0_fwd(x, row(w["e_pre_norm"]))
    p = {n: w[n] for n in SMALL if SMALL[n][1] is None}
    small_rows = small_rows.reshape(NDEV, -1)
    off = 0
    for n in SMALL_SHARDED:
        shp, ax = _shard_shape(n), SMALL[n][1]
        cnt = int(np.prod(shp))
        blk = small_rows[:, off:off + cnt].reshape((NDEV,) + shp)
        p[n] = jnp.moveaxis(blk, 0, ax).reshape(SMALL[n][0])
        off += cnt
    tabs = _rope_tables()
    pool_w_bf = p["e_pool_w"].astype(BF16)
    sgu_bb = jnp.broadcast_to(p["o_sgu_b"][:, :, None], (4, 128, 128))
    conv_w = jnp.concatenate([p["o_conv_w"], jnp.zeros((HALO - CONV_K, HALF), F32)], axis=0)
    odd_p = (row(p["o_sgu_norm_g"]), row(p["o_sgu_norm_b"]), p["o_sgu_w"], sgu_bb, conv_w,
             row(p["o_conv_b"]), row(p["o_conv_norm_g"]), row(p["o_conv_norm_b"]))

    z0 = _mm_in(h0, wg_e_in, "mm_z0")
    ycat0 = _pool_fwd(z0, pool_w_bf, row(p["e_pool_scale"]))
    ycat0 = _attn_fwd(z0, ycat0, tabs)
    w_out_e, w_out_o = wg_e_out.reshape(2048, D), wg_o_out.reshape(2048, D)
    y0 = _mm_out(ycat0, w_out_e, "mm_y0")
    x1, h1 = _post0_fwd(x, y0, row(p["e_post_norm"]), row(p["o_pre_norm"]))
    z1 = _mm_in(h1, wg_o_in, "mm_z1")
    ycat1 = _odd_fwd(z1, *odd_p)
    y1 = _mm_out(ycat1, w_out_o, "mm_y1")

    g = {}
    loss, dx2, dy1, g["o_post_norm"] = _post1_bwd(y1, x1, target, row(p["o_post_norm"]))
    loss = lax.psum(loss[0, 0], ("x", "y", "c"))
    parts = {}
    parts["o_w_out"], = _sc_exchange("scatter_o_w_out", 2, [_mm_out_dw(ycat1, dy1, "mm_dwout1").reshape(NDEV, 256, D)], True)
    dycat1 = _mm_out_dx(dy1, w_out_o, "mm_dycat1")
    dz1, ddc, g["o_sgu_w"], d_sgu_bb, g["o_sgu_norm_g"], g["o_sgu_norm_b"], g["o_conv_norm_g"], \
        g["o_conv_norm_b"], g["o_conv_b"] = _odd_bwd_a(z1, dycat1, *odd_p)
    dz1, d_conv_w = _odd_bwd_b(z1, ddc, dz1, conv_w)
    g["o_sgu_b"] = d_sgu_bb[:, :, 0]
    g["o_conv_w"] = d_conv_w[:CONV_K]
    parts["o_w_in"], = _sc_exchange("scatter_o_w_in", 3, [_mm_in_dw(h1, dz1, ODD_IN // NDEV, "mm_dwin1")], True)
    dh1 = _mm_in_dx(dz1, wg_o_in, "mm_dh1")
    dx1, dy0, g["o_pre_norm"], g["e_post_norm"] = _mid_bwd(dx2, dh1, x1, y0, row(p["o_pre_norm"]),
                                                           row(p["e_post_norm"]))
    parts["e_w_out"], = _sc_exchange("scatter_e_w_out", 4, [_mm_out_dw(ycat0, dy0, "mm_dwout0").reshape(NDEV, 256, D)], True)
    dycat0 = _mm_out_dx(dy0, w_out_e, "mm_dycat0")
    da_in, da_gate, g["e_pool_w"], g["e_pool_scale"] = _pool_bwd(z0, dycat0, pool_w_bf, row(p["e_pool_scale"]))
    dq, dk, dv, dbg = _attn_bwd(z0, dycat0, tabs)
    dz0 = jnp.concatenate([da_in, da_gate, dq, dk, dv, dbg], axis=1)
    late = [n for n in SMALL if n != "e_pre_norm"]
    parts["e_w_in"], = _sc_exchange("scatter_e_w_in", 5, [_mm_in_dw(h0, dz0, EVEN_IN // NDEV, "mm_dwin0")], True)
    recv_small, = _sc_exchange("gather_small_grads", 6, [_pack([g[n].reshape(SMALL[n][0]) for n in late], 512)], False)
    dh0 = _mm_in_dx(dz0, wg_e_in, "mm_dh0")
    grad_x, g["e_pre_norm"] = _pre0_bwd(dx1, dh0, x, row(p["e_pre_norm"]))
    last, = _sc_exchange("gather_e_pre_norm_grad", 7, [g["e_pre_norm"].reshape(16, 128)], False)

    grads, deltas, new_m, new_v = {}, {}, {}, {}
    for n in ("o_w_out", "o_w_in", "e_w_out", "e_w_in"):
        grads[n], deltas[n], new_m[n], new_v[n] = _adam_reduce(parts[n], w[n], m[n], v[n], "adam_" + n)
    g_small = dict(zip(late, _unpack(_sum_parts(recv_small, "sum_small_grads"), [SMALL[n][0] for n in late])))
    g_small["e_pre_norm"] = _sum_parts(last, "sum_e_pre_norm_grad").reshape(2048)
    for n in SMALL:
        grads[n] = _take_shard(g_small[n], n, me)
    names = list(SMALL)
    shapes = [_shard_shape(n) for n in names]
    d_pack, m_pack, v_pack = _adam_plain(_pack([w[n] for n in names]), _pack([grads[n] for n in names]),
                                         _pack([m[n] for n in names]), _pack([v[n] for n in names]), "adam_small")
    for n, d_, m_, v_ in zip(names, _unpack(d_pack, shapes), _unpack(m_pack, shapes), _unpack(v_pack, shapes)):
        deltas[n], new_m[n], new_v[n] = d_, m_, v_

    lead = lambda a: a[None]
    return (loss, grad_x[None], *[lead(grads[n]) for n in WEIGHTS], *[lead(deltas[n]) for n in WEIGHTS],
            *[lead(new_m[n]) for n in WEIGHTS], *[lead(new_v[n]) for n in WEIGHTS])
```

```python
import functools

import numpy as np
import jax
import jax.numpy as jnp
from jax import lax
from jax.experimental import pallas as pl
from jax.experimental.pallas import tpu as pltpu
from jax.experimental.pallas import tpu_sc as plsc

F32 = jnp.float32
BF16 = jnp.bfloat16

S = 2048
D = 2048
NDEV = 8
EPS = 1e-6
NEG = -1e30
HEAD_DIM = 128
ROT_DIM = 32
ROPE_THETA = 500000.0
PATTERNS = ((128, 1), (512, 4), (2048, 16))
BLK = 128
EVEN_IN = 12288
ODD_IN = 6144
HALF = 1024
CONV_K = 31
HALO = 32
TR = 256
SUB = 32

ADAM_LR = 0.001
ADAM_B1 = 0.9
ADAM_B2 = 0.999
ADAM_EPS = 1e-08
ADAM_WD = 0.01
ADAM_STEP = 10

VMEM_BIG = 56 * 1024 * 1024
MESH = pl.DeviceIdType.MESH

NN = (((1,), (0,)), ((), ()))
NT = (((1,), (1,)), ((), ()))
TN = (((0,), (0,)), ((), ()))


def _dot(a, b, dn=NN):
    return lax.dot_general(a, b, dn, preferred_element_type=F32)


def _sigmoid(x):
    return 1.0 / (1.0 + jnp.exp(-x))


def _silu_and_grad(x):
    sg = _sigmoid(x)
    return x * sg, sg * (1.0 + x * (1.0 - sg))


def _params(sem, vmem=None):
    return pltpu.CompilerParams(dimension_semantics=sem, vmem_limit_bytes=vmem)


ANY_SPEC = pl.BlockSpec(memory_space=pl.ANY)


def _matmul(a, b, *, dn, grid, a_spec, b_spec, o_spec, out_shape, out_dtype, acc_shape, name, dep=None):
    nk = grid[2]
    deps = [] if dep is None else [dep]

    def body(a_ref, b_ref, *rest):
        o_ref, acc = rest[len(deps)], rest[len(deps) + 1:]
        if nk == 1:
            o_ref[...] = _dot(a_ref[...], b_ref[...], dn).astype(o_ref.dtype)
            return
        acc_ref = acc[0]
        k = pl.program_id(2)

        @pl.when(k == 0)
        def _():
            acc_ref[...] = jnp.zeros_like(acc_ref)

        acc_ref[...] += _dot(a_ref[...], b_ref[...], dn)

        @pl.when(k == nk - 1)
        def _():
            o_ref[...] = acc_ref[...].astype(o_ref.dtype)

    return pl.pallas_call(
        body, grid=grid, in_specs=[a_spec, b_spec] + [ANY_SPEC] * len(deps), out_specs=o_spec,
        out_shape=jax.ShapeDtypeStruct(out_shape, out_dtype),
        scratch_shapes=[] if nk == 1 else [pltpu.VMEM(acc_shape, F32)],
        compiler_params=_params(("parallel", "parallel", "arbitrary"), VMEM_BIG), name=name,
    )(a, b, *deps)


TM = 512


def _mm_in(h, wg, name):
    nb = wg.shape[2]
    tn = 512 if nb % 512 == 0 else nb
    per = nb // tn
    return _matmul(
        h, wg, dn=NN, grid=(S // TM, NDEV * per, 1),
        a_spec=pl.BlockSpec((TM, D), lambda i, j, k: (i, 0)),
        b_spec=pl.BlockSpec((None, D, tn), lambda i, j, k: (j // per, 0, j % per)),
        o_spec=pl.BlockSpec((TM, tn), lambda i, j, k: (i, j)),
        out_shape=(S, NDEV * nb), out_dtype=F32, acc_shape=(TM, tn), name=name)


def _mm_in_dx(dz, wg, name, dep=None):
    nb = wg.shape[2]
    return _matmul(
        dz, wg, dn=NT, grid=(S // TM, D // 512, NDEV),
        a_spec=pl.BlockSpec((TM, nb), lambda i, j, k: (i, k)),
        b_spec=pl.BlockSpec((None, 512, nb), lambda i, j, k: (k, j, 0)),
        o_spec=pl.BlockSpec((TM, 512), lambda i, j, k: (i, j)),
        out_shape=(S, D), out_dtype=F32, acc_shape=(TM, 512), name=name, dep=dep)


def _mm_in_dw(h, dz, nb, name):
    tn = 512 if nb % 512 == 0 else nb
    per = nb // tn
    return _matmul(
        h, dz, dn=TN, grid=(D // TM, NDEV * per, 1),
        a_spec=pl.BlockSpec((S, TM), lambda i, j, k: (0, i)),
        b_spec=pl.BlockSpec((S, tn), lambda i, j, k: (0, j)),
        o_spec=pl.BlockSpec((None, TM, tn), lambda i, j, k: (j // per, i, j % per)),
        out_shape=(NDEV, D, nb), out_dtype=BF16, acc_shape=(TM, tn), name=name)


def _mm_out(yc, w, name):
    return _matmul(
        yc, w, dn=NN, grid=(S // TM, D // 512, 1),
        a_spec=pl.BlockSpec((TM, 2048), lambda i, j, k: (i, 0)),
        b_spec=pl.BlockSpec((2048, 512), lambda i, j, k: (0, j)),
        o_spec=pl.BlockSpec((TM, 512), lambda i, j, k: (i, j)),
        out_shape=(S, D), out_dtype=F32, acc_shape=(TM, 512), name=name)


def _mm_out_dx(dy, w, name, dep=None):
    return _matmul(
        dy, w, dn=NT, grid=(S // TM, 2048 // 512, 1),
        a_spec=pl.BlockSpec((TM, D), lambda i, j, k: (i, 0)),
        b_spec=pl.BlockSpec((512, D), lambda i, j, k: (j, 0)),
        o_spec=pl.BlockSpec((TM, 512), lambda i, j, k: (i, j)),
        out_shape=(S, 2048), out_dtype=F32, acc_shape=(TM, 512), name=name, dep=dep)


def _mm_out_dw(yc, dy, name):
    return _matmul(
        yc, dy, dn=TN, grid=(2048 // TM, D // 512, 1),
        a_spec=pl.BlockSpec((S, TM), lambda i, j, k: (0, i)),
        b_spec=pl.BlockSpec((S, 512), lambda i, j, k: (0, j)),
        o_spec=pl.BlockSpec((TM, 512), lambda i, j, k: (i, j)),
        out_shape=(2048, D), out_dtype=BF16, acc_shape=(TM, 512), name=name)


def _row_spec(w=D):
    return pl.BlockSpec((TR, w), lambda i: (i, 0))


def _vec_spec(w=D):
    return pl.BlockSpec((1, w), lambda i: (0, 0))


def _rms_stats(x):
    r = lax.rsqrt(jnp.mean(x * x, axis=-1, keepdims=True) + EPS)
    return x * r, r


def _rms_bwd(dn, xhat, r, g):
    dxh = dn * g
    return r * (dxh - xhat * jnp.mean(dxh * xhat, axis=-1, keepdims=True))


def _acc_rows(ref, val, i):
    s = jnp.sum(val, axis=0, keepdims=True)

    @pl.when(i == 0)
    def _():
        ref[...] = s

    @pl.when(i > 0)
    def _():
        ref[...] += s


def _pre0_fwd(x, g, dep=None):
    deps = [] if dep is None else [dep]

    def body(x_ref, g_ref, *rest):
        xhat, _ = _rms_stats(x_ref[...])
        rest[-1][...] = (xhat * g_ref[...]).astype(BF16)

    return pl.pallas_call(
        body, grid=(S // TR,), in_specs=[_row_spec(), _vec_spec()] + [ANY_SPEC] * len(deps), out_specs=_row_spec(),
        out_shape=jax.ShapeDtypeStruct((S, D), BF16), compiler_params=_params(("parallel",)), name="pre0_fwd",
    )(x, g, *deps)


def _post0_fwd(x, y0, g_post, g_pre1):
    def body(x_ref, y_ref, gp_ref, g1_ref, x1_ref, h1_ref):
        yhat, _ = _rms_stats(y_ref[...])
        x1 = x_ref[...] + yhat * gp_ref[...]
        x1_ref[...] = x1
        xhat, _ = _rms_stats(x1)
        h1_ref[...] = (xhat * g1_ref[...]).astype(BF16)

    return pl.pallas_call(
        body, grid=(S // TR,), in_specs=[_row_spec(), _row_spec(), _vec_spec(), _vec_spec()],
        out_specs=[_row_spec(), _row_spec()],
        out_shape=[jax.ShapeDtypeStruct((S, D), F32), jax.ShapeDtypeStruct((S, D), BF16)],
        compiler_params=_params(("parallel",)), name="post0_fwd",
    )(x, y0, g_post, g_pre1)


def _post1_bwd(y1, x1, target, g_post):
    def body(y_ref, x1_ref, t_ref, g_ref, loss_ref, dx2_ref, dy_ref, dg_ref):
        i = pl.program_id(0)
        yhat, r = _rms_stats(y_ref[...])
        g = g_ref[...]
        err = x1_ref[...] + yhat * g - t_ref[...]
        part = jnp.sum(jnp.sum(err * err, axis=-1, keepdims=True), axis=0, keepdims=True) * (0.5 / D)
        _acc_rows(loss_ref, jnp.broadcast_to(part, (1, 128)), i)
        dx2 = err * (1.0 / D)
        dx2_ref[...] = dx2
        _acc_rows(dg_ref, dx2 * yhat, i)
        dy_ref[...] = _rms_bwd(dx2, yhat, r, g).astype(BF16)

    return pl.pallas_call(
        body, grid=(S // TR,), in_specs=[_row_spec(), _row_spec(), _row_spec(), _vec_spec()],
        out_specs=[_vec_spec(128), _row_spec(), _row_spec(), _vec_spec()],
        out_shape=[jax.ShapeDtypeStruct((1, 128), F32), jax.ShapeDtypeStruct((S, D), F32),
                   jax.ShapeDtypeStruct((S, D), BF16), jax.ShapeDtypeStruct((1, D), F32)],
        compiler_params=_params(("arbitrary",)), name="post1_bwd",
    )(y1, x1, target, g_post)


def _mid_bwd(dx2, dh1, x1, y0, g_pre1, g_post0):
    def body(dx2_ref, dh_ref, x1_ref, y_ref, g1_ref, gp_ref, dx1_ref, dy_ref, dg1_ref, dgp_ref):
        i = pl.program_id(0)
        xhat, r1 = _rms_stats(x1_ref[...])
        dh = dh_ref[...]
        _acc_rows(dg1_ref, dh * xhat, i)
        dx1 = dx2_ref[...] + _rms_bwd(dh, xhat, r1, g1_ref[...])
        dx1_ref[...] = dx1
        yhat, r0 = _rms_stats(y_ref[...])
        _acc_rows(dgp_ref, dx1 * yhat, i)
        dy_ref[...] = _rms_bwd(dx1, yhat, r0, gp_ref[...]).astype(BF16)

    return pl.pallas_call(
        body, grid=(S // TR,),
        in_specs=[_row_spec(), _row_spec(), _row_spec(), _row_spec(), _vec_spec(), _vec_spec()],
        out_specs=[_row_spec(), _row_spec(), _vec_spec(), _vec_spec()],
        out_shape=[jax.ShapeDtypeStruct((S, D), F32), jax.ShapeDtypeStruct((S, D), BF16),
                   jax.ShapeDtypeStruct((1, D), F32), jax.ShapeDtypeStruct((1, D), F32)],
        compiler_params=_params(("arbitrary",)), name="mid_bwd",
    )(dx2, dh1, x1, y0, g_pre1, g_post0)


def _pre0_bwd(dx1, dh0, x, g):
    def body(dx1_ref, dh_ref, x_ref, g_ref, gx_ref, dg_ref):
        i = pl.program_id(0)
        xhat, r = _rms_stats(x_ref[...])
        dh = dh_ref[...]
        _acc_rows(dg_ref, dh * xhat, i)
        gx_ref[...] = dx1_ref[...] + _rms_bwd(dh, xhat, r, g_ref[...])

    return pl.pallas_call(
        body, grid=(S // TR,), in_specs=[_row_spec(), _row_spec(), _row_spec(), _vec_spec()],
        out_specs=[_row_spec(), _vec_spec()],
        out_shape=[jax.ShapeDtypeStruct((S, D), F32), jax.ShapeDtypeStruct((1, D), F32)],
        compiler_params=_params(("arbitrary",)), name="pre0_bwd",
    )(dx1, dh0, x, g)


POOL_CH = 256


def _pool_apply(a, w, transpose):
    n = a.shape[0]
    row = lax.broadcasted_iota(jnp.int32, a.shape, 0)
    cnt = jnp.minimum(row + 1, w).astype(F32)
    s = a / cnt if transpose else a
    for k in (1, 2, 4, 8):
        if transpose:
            sh = jnp.where(row < n - k, pltpu.roll(s, n - k, 0), 0.0)
        else:
            sh = jnp.where(row >= k, pltpu.roll(s, k, 0), 0.0)
        s = jnp.where(w > k, s + sh, s)
    return s - a if transpose else s / cnt - a


def _pool_fwd(z0, pool_w, pool_scale):
    def body(a_ref, gate_ref, w_ref, sc_ref, out_ref):
        win = jnp.left_shift(2, pl.program_id(0))
        pooled = _pool_apply(a_ref[...], win, False)
        mixed = _dot(pooled.astype(BF16), w_ref[...])
        gate = gate_ref[...]
        out_ref[...] = (mixed * sc_ref[...] * (gate * _sigmoid(gate))).astype(BF16)

    return pl.pallas_call(
        body, grid=(4,),
        in_specs=[pl.BlockSpec((S, POOL_CH), lambda g: (0, g)), pl.BlockSpec((S, POOL_CH), lambda g: (0, 4 + g)),
                  pl.BlockSpec((None, POOL_CH, POOL_CH), lambda g: (g, 0, 0)),
                  pl.BlockSpec((1, POOL_CH), lambda g: (0, g))],
        out_specs=pl.BlockSpec((S, POOL_CH), lambda g: (0, g)),
        out_shape=jax.ShapeDtypeStruct((S, 2048), BF16),
        compiler_params=_params(("parallel",), VMEM_BIG), name="pool_fwd",
    )(z0, z0, pool_w, pool_scale)


def _pool_bwd(z0, dycat, pool_w, pool_scale):
    def body(a_ref, gate_ref, dy_ref, w_ref, sc_ref, da_ref, dgate_ref, dw_ref, dsc_ref):
        win = jnp.left_shift(2, pl.program_id(0))
        pooled = _pool_apply(a_ref[...], win, False).astype(BF16)
        w = w_ref[...]
        mixed = _dot(pooled, w)
        silu, dsilu = _silu_and_grad(gate_ref[...])
        dy = dy_ref[...]
        sc = sc_ref[...]
        dgate_ref[...] = (dy * (mixed * sc) * dsilu).astype(BF16)
        dms = dy * silu
        dsc_ref[...] = jnp.sum(dms * mixed, axis=0, keepdims=True)
        dmixed = (dms * sc).astype(BF16)
        dw_ref[...] = _dot(pooled, dmixed, TN)
        dpooled = _dot(dmixed, w, NT)
        da_ref[...] = _pool_apply(dpooled, win, True).astype(BF16)

    slab = lambda off: pl.BlockSpec((S, POOL_CH), lambda g: (0, off + g))
    return pl.pallas_call(
        body, grid=(4,),
        in_specs=[slab(0), slab(4), slab(0), pl.BlockSpec((None, POOL_CH, POOL_CH), lambda g: (g, 0, 0)),
                  pl.BlockSpec((1, POOL_CH), lambda g: (0, g))],
        out_specs=[slab(0), slab(0), pl.BlockSpec((None, POOL_CH, POOL_CH), lambda g: (g, 0, 0)),
                   pl.BlockSpec((1, POOL_CH), lambda g: (0, g))],
        out_shape=[jax.ShapeDtypeStruct((S, HALF), BF16), jax.ShapeDtypeStruct((S, HALF), BF16),
                   jax.ShapeDtypeStruct((4, POOL_CH, POOL_CH), F32), jax.ShapeDtypeStruct((1, HALF), F32)],
        compiler_params=_params(("parallel",), VMEM_BIG), name="pool_bwd",
    )(z0, z0, dycat, pool_w, pool_scale)


Q_COL, K_COL, V_COL, BG_COL = 2048 // 128, 5120 // 128, 8192 // 128, 11264 // 128
SCALE = HEAD_DIM ** -0.5


def _rope_tables():
    pos = jnp.arange(S, dtype=F32)
    inv_freq = jnp.power(ROPE_THETA, -jnp.arange(0, ROT_DIM, 2, dtype=F32) / ROT_DIM)
    ang = pos[:, None] * inv_freq[None, :]
    cos, sin = jnp.cos(ang), jnp.sin(ang)
    half = ROT_DIM // 2
    zeros = jnp.zeros((S, HEAD_DIM - ROT_DIM), F32)
    c = jnp.concatenate([cos, cos, jnp.ones((S, HEAD_DIM - ROT_DIM), F32)], axis=1)
    a = jnp.concatenate([-sin, jnp.zeros((S, half), F32), zeros], axis=1)
    b = jnp.concatenate([jnp.zeros((S, half), F32), sin, zeros], axis=1)
    return c, a, b


def _rope(t, c, a, b):
    half = ROT_DIM // 2
    return t * c + pltpu.roll(t, HEAD_DIM - half, 1) * a + pltpu.roll(t, half, 1) * b


def _rope_t(d, c, a, b):
    half = ROT_DIM // 2
    return d * c + pltpu.roll(d * a, half, 1) + pltpu.roll(d * b, HEAD_DIM - half, 1)


def _deinterleave(dst, src, dil, cast=None):
    length = S // dil
    for r in range(dil):
        v = src[...] if dil == 1 else src[pl.ds(r, length, stride=dil), :]
        dst[r * length:(r + 1) * length, :] = v if cast is None else v.astype(cast)


def _interleave(dst, src, dil):
    length = S // dil
    for r in range(dil):
        if dil == 1:
            dst[...] = src[...]
        else:
            dst[pl.ds(r, length, stride=dil), :] = src[r * length:(r + 1) * length, :]


def _unit_scores(u, nb, qd, kd):
    o0 = pl.multiple_of(u * BLK, BLK)
    p0 = pl.multiple_of(jnp.maximum(u - 1, 0) * BLK, BLK)
    q = qd[pl.ds(o0, BLK), :]
    row = lax.broadcasted_iota(jnp.int32, (BLK, BLK), 0)
    col = lax.broadcasted_iota(jnp.int32, (BLK, BLK), 1)
    s_own = jnp.where(col <= row, _dot(q, kd[pl.ds(o0, BLK), :], NT) * SCALE, NEG)
    if nb == 1:
        return o0, p0, q, s_own, None
    has_prev = (u % nb) != 0
    s_prev = jnp.where((col >= row) & has_prev, _dot(q, kd[pl.ds(p0, BLK), :], NT) * SCALE, NEG)
    return o0, p0, q, s_own, s_prev


def _attn_group_fwd(dil, q_ref, k_ref, v_ref, tabs, tmp, qd, kd, vd, od, ld, og, lg):
    nb = S // dil // BLK
    c, a, b = tabs
    tmp[...] = _rope(q_ref[...], c, a, b)
    _deinterleave(qd, tmp, dil, BF16)
    tmp[...] = _rope(k_ref[...], c, a, b)
    _deinterleave(kd, tmp, dil, BF16)
    _deinterleave(vd, v_ref, dil, BF16)

    def unit(u, carry):
        o0, p0, _, s_own, s_prev = _unit_scores(u, nb, qd, kd)
        m = jnp.max(s_own, axis=1, keepdims=True)
        if s_prev is not None:
            m = jnp.maximum(m, jnp.max(s_prev, axis=1, keepdims=True))
        p_own = jnp.exp(s_own - m)
        den = jnp.sum(p_own, axis=1, keepdims=True)
        acc = _dot(p_own.astype(BF16), vd[pl.ds(o0, BLK), :])
        if s_prev is not None:
            p_prev = jnp.exp(s_prev - m)
            den = den + jnp.sum(p_prev, axis=1, keepdims=True)
            acc = acc + _dot(p_prev.astype(BF16), vd[pl.ds(p0, BLK), :])
        od[pl.ds(o0, BLK), :] = acc / den
        ld[pl.ds(o0, BLK), :] = jnp.broadcast_to(m + jnp.log(den), (BLK, HEAD_DIM))
        return carry

    lax.fori_loop(0, S // BLK, unit, 0)
    _interleave(og, od, dil)
    _interleave(lg, ld, dil)


def _group_weights(lgs):
    l0, l1, l2 = lgs[0][...], lgs[1][...], lgs[2][...]
    mx = jnp.maximum(l0, jnp.maximum(l1, l2))
    e0, e1, e2 = jnp.exp(l0 - mx), jnp.exp(l1 - mx), jnp.exp(l2 - mx)
    den = e0 + e1 + e2
    return e0 / den, e1 / den, e2 / den


def _head_spec(base, ngroups_axis=True):
    return pl.BlockSpec((S, HEAD_DIM), lambda h, p: (0, base + (p % 3) * 8 + h))


ATTN_SCRATCH_FWD = [
    pltpu.VMEM((S, HEAD_DIM), F32),
    pltpu.VMEM((S, HEAD_DIM), BF16), pltpu.VMEM((S, HEAD_DIM), BF16), pltpu.VMEM((S, HEAD_DIM), BF16),
    pltpu.VMEM((S, HEAD_DIM), F32), pltpu.VMEM((S, HEAD_DIM), F32),
    pltpu.VMEM((S, HEAD_DIM), F32), pltpu.VMEM((S, HEAD_DIM), F32), pltpu.VMEM((S, HEAD_DIM), F32),
    pltpu.VMEM((S, HEAD_DIM), F32), pltpu.VMEM((S, HEAD_DIM), F32), pltpu.VMEM((S, HEAD_DIM), F32),
]


def _attn_fwd(z0, ycat, tabs):
    def body(q_ref, k_ref, v_ref, gate_ref, c_ref, a_ref, b_ref, ycat_ref, out_ref,
             tmp, qd, kd, vd, od, ld, og0, og1, og2, lg0, lg1, lg2):
        del ycat_ref
        p = pl.program_id(1)
        ogs, lgs = (og0, og1, og2), (lg0, lg1, lg2)
        tabs_v = (c_ref[...], a_ref[...], b_ref[...])
        for gi, (_, dil) in enumerate(PATTERNS):
            @pl.when(p == gi)
            def _(gi=gi, dil=dil):
                _attn_group_fwd(dil, q_ref, k_ref, v_ref, tabs_v, tmp, qd, kd, vd, od, ld, ogs[gi], lgs[gi])

        @pl.when(p == 2)
        def _():
            w0, w1, w2 = _group_weights(lgs)
            o = w0 * og0[...] + w1 * og1[...] + w2 * og2[...]
            gate = gate_ref[...]
            out_ref[...] = (o * (gate * _sigmoid(gate))).astype(BF16)

    tab = pl.BlockSpec((S, HEAD_DIM), lambda h, p: (0, 0))
    return pl.pallas_call(
        body, grid=(8, 3),
        in_specs=[_head_spec(Q_COL), _head_spec(K_COL), _head_spec(V_COL),
                  pl.BlockSpec((S, HEAD_DIM), lambda h, p: (0, BG_COL + h)), tab, tab, tab,
                  pl.BlockSpec(memory_space=pl.ANY)],
        out_specs=pl.BlockSpec((S, HEAD_DIM), lambda h, p: (0, 8 + h)),
        out_shape=jax.ShapeDtypeStruct((S, 2048), BF16),
        scratch_shapes=ATTN_SCRATCH_FWD, input_output_aliases={7: 0},
        compiler_params=_params(("parallel", "arbitrary"), VMEM_BIG), name="attn_fwd",
    )(z0, z0, z0, z0, *tabs, ycat)


def _attn_bwd(z0, dycat, tabs):
    def body(q_ref, k_ref, v_ref, gate_ref, dy_ref, c_ref, a_ref, b_ref,
             dq_ref, dk_ref, dv_ref, dbg_ref,
             tmp, qd, kd, vd, od, ld, og0, og1, og2, lg0, lg1, lg2, cg0, cg1, cg2, dod, cd, dqd, dkd, dvd):
        p = pl.program_id(1)
        ogs, lgs, cgs = (og0, og1, og2), (lg0, lg1, lg2), (cg0, cg1, cg2)
        tabs_v = (c_ref[...], a_ref[...], b_ref[...])
        for gi, (_, dil) in enumerate(PATTERNS):
            @pl.when(p == gi)
            def _(gi=gi, dil=dil):
                _attn_group_fwd(dil, q_ref, k_ref, v_ref, tabs_v, tmp, qd, kd, vd, od, ld, ogs[gi], lgs[gi])

        @pl.when(p == 2)
        def _():
            w = _group_weights(lgs)
            o = w[0] * og0[...] + w[1] * og1[...] + w[2] * og2[...]
            silu, dsilu = _silu_and_grad(gate_ref[...])
            dy = dy_ref[...]
            dbg_ref[...] = (dy * o * dsilu).astype(BF16)
            do = dy * silu
            dwbar = jnp.sum(do * o, axis=1, keepdims=True)
            for gi in range(3):
                ogs[gi][...] = w[gi] * do
                cgs[gi][...] = -w[gi] * dwbar

        for gi, (_, dil) in enumerate(PATTERNS):
            @pl.when(p == 3 + gi)
            def _(gi=gi, dil=dil):
                nb = S // dil // BLK
                c, a, b = tabs_v
                tmp[...] = _rope(q_ref[...], c, a, b)
                _deinterleave(qd, tmp, dil, BF16)
                tmp[...] = _rope(k_ref[...], c, a, b)
                _deinterleave(kd, tmp, dil, BF16)
                _deinterleave(vd, v_ref, dil, BF16)
                _deinterleave(dod, ogs[gi], dil, BF16)
                _deinterleave(ld, lgs[gi], dil)
                _deinterleave(cd, cgs[gi], dil)
                dkd[...] = jnp.zeros_like(dkd)
                dvd[...] = jnp.zeros_like(dvd)

                def unit(u, carry):
                    o0, p0, q, s_own, s_prev = _unit_scores(u, nb, qd, kd)
                    lse = ld[pl.ds(o0, BLK), :]
                    cv = cd[pl.ds(o0, BLK), :]
                    do = dod[pl.ds(o0, BLK), :]
                    p_own = jnp.exp(s_own - lse)
                    ds_own = (p_own * (_dot(do, vd[pl.ds(o0, BLK), :], NT) + cv) * SCALE).astype(BF16)
                    dq = _dot(ds_own, kd[pl.ds(o0, BLK), :])
                    dkd[pl.ds(o0, BLK), :] += _dot(ds_own, q, TN)
                    dvd[pl.ds(o0, BLK), :] += _dot(p_own.astype(BF16), do, TN)
                    if s_prev is not None:
                        p_prev = jnp.exp(s_prev - lse)
                        ds_prev = (p_prev * (_dot(do, vd[pl.ds(p0, BLK), :], NT) + cv) * SCALE).astype(BF16)
                        dq = dq + _dot(ds_prev, kd[pl.ds(p0, BLK), :])
                        dkd[pl.ds(p0, BLK), :] += _dot(ds_prev, q, TN)
                        dvd[pl.ds(p0, BLK), :] += _dot(p_prev.astype(BF16), do, TN)
                    dqd[pl.ds(o0, BLK), :] = dq
                    return carry

                lax.fori_loop(0, S // BLK, unit, 0)
                _interleave(tmp, dqd, dil)
                dq_ref[...] = _rope_t(tmp[...], c, a, b).astype(BF16)
                _interleave(tmp, dkd, dil)
                dk_ref[...] = _rope_t(tmp[...], c, a, b).astype(BF16)
                _interleave(tmp, dvd, dil)
                dv_ref[...] = tmp[...].astype(BF16)

    tab = pl.BlockSpec((S, HEAD_DIM), lambda h, p: (0, 0))
    hspec = lambda base: pl.BlockSpec((S, HEAD_DIM), lambda h, p: (0, base + h))
    gspec = pl.BlockSpec((S, HEAD_DIM), lambda h, p: (0, jnp.maximum(p - 3, 0) * 8 + h))
    slab = lambda: pltpu.VMEM((S, HEAD_DIM), F32)
    return pl.pallas_call(
        body, grid=(8, 6),
        in_specs=[_head_spec(Q_COL), _head_spec(K_COL), _head_spec(V_COL), hspec(BG_COL), hspec(8), tab, tab, tab],
        out_specs=[gspec, gspec, gspec, hspec(0)],
        out_shape=[jax.ShapeDtypeStruct((S, 3072), BF16)] * 3 + [jax.ShapeDtypeStruct((S, HALF), BF16)],
        scratch_shapes=ATTN_SCRATCH_FWD + [slab(), slab(), slab(), pltpu.VMEM((S, HEAD_DIM), BF16),
                                           slab(), slab(), slab(), slab()],
        compiler_params=_params(("parallel", "arbitrary"), VMEM_BIG), name="attn_bwd",
    )(z0, z0, z0, z0, dycat, *tabs)


SGU_CH = 256
NCHUNK = TR // 128


def _ln_stats(x):
    mu = jnp.mean(x, axis=-1, keepdims=True)
    xc = x - mu
    r = lax.rsqrt(jnp.mean(xc * xc, axis=-1, keepdims=True) + EPS)
    return xc * r, r


def _ln_bwd(dy, xhat, r, g):
    dxh = dy * g
    return r * (dxh - jnp.mean(dxh, axis=-1, keepdims=True) - xhat * jnp.mean(dxh * xhat, axis=-1, keepdims=True))


def _tril_bf16(w):
    row = lax.broadcasted_iota(jnp.int32, w.shape, 0)
    col = lax.broadcasted_iota(jnp.int32, w.shape, 1)
    return jnp.where(row >= col, w, 0.0).astype(BF16)


def _sgu_gate(vn_s, s_s, w_ref, bb_ref):
    for h in range(4):
        wm = _tril_bf16(w_ref[h])
        bias = bb_ref[h]
        for ch in range(NCHUNK):
            rows, cols = slice(ch * 128, (ch + 1) * 128), slice(h * SGU_CH, (h + 1) * SGU_CH)
            s_s[rows, cols] = _dot(wm, vn_s[rows, cols]) + jnp.concatenate([bias, bias], axis=1)


def _conv_fwd(i, dval_ref, dglu_ref, hval_ref, hglu_ref, cw_ref, cb_ref, xw, dcs):
    halo = hval_ref[...] * _sigmoid(hglu_ref[...])
    xw[0:HALO, :] = jnp.where(i > 0, halo, 0.0)
    xw[HALO:HALO + TR, :] = dval_ref[...] * _sigmoid(dglu_ref[...])
    for rb in range(TR // SUB):
        acc = jnp.broadcast_to(cb_ref[...], (SUB, HALF))
        for k in range(CONV_K):
            acc = acc + cw_ref[k:k + 1, :] * xw[pl.ds(rb * SUB + HALO - (CONV_K - 1) + k, SUB), :]
        dcs[rb * SUB:(rb + 1) * SUB, :] = acc


def _odd_in_specs():
    col = lambda j: pl.BlockSpec((TR, HALF), lambda i, *_: (i, j))
    prev = lambda j: pl.BlockSpec((HALO, HALF), lambda i, *_: (jnp.maximum(i * (TR // HALO) - 1, 0), j))
    return [col(0), col(1), col(2), col(3), col(4), col(5), prev(3), prev(4)]


def _full_spec(shape):
    return pl.BlockSpec(shape, lambda i, *_: (0,) * len(shape))


def _odd_fwd(z1, sgu_g, sgu_b, sgu_w, sgu_bb, conv_w, conv_b, cn_g, cn_b):
    def body(u_ref, v_ref, cg_ref, dval_ref, dglu_ref, dgate_ref, hval_ref, hglu_ref,
             g_ref, b_ref, w_ref, bb_ref, cw_ref, cb_ref, cng_ref, cnb_ref, out_ref, vn_s, s_s, xw, dcs):
        i = pl.program_id(0)
        vhat, _ = _ln_stats(v_ref[...])
        vn_s[...] = (vhat * g_ref[...] + b_ref[...]).astype(BF16)
        _sgu_gate(vn_s, s_s, w_ref, bb_ref)
        cg = cg_ref[...]
        out_ref[:, 0:HALF] = (u_ref[...] * s_s[...] * (cg * _sigmoid(cg))).astype(BF16)
        _conv_fwd(i, dval_ref, dglu_ref, hval_ref, hglu_ref, cw_ref, cb_ref, xw, dcs)
        dhat, _ = _ln_stats(dcs[...])
        dn = dhat * cng_ref[...] + cnb_ref[...]
        dgate = dgate_ref[...]
        out_ref[:, HALF:2 * HALF] = ((dn * _sigmoid(dn)) * (dgate * _sigmoid(dgate))).astype(BF16)

    vec = _full_spec((1, HALF))
    return pl.pallas_call(
        body, grid=(S // TR,),
        in_specs=_odd_in_specs() + [vec, vec, _full_spec((4, 128, 128)), _full_spec((4, 128, 128)),
                                    _full_spec((HALO, HALF)), vec, vec, vec],
        out_specs=pl.BlockSpec((TR, 2048), lambda i: (i, 0)),
        out_shape=jax.ShapeDtypeStruct((S, 2048), BF16),
        scratch_shapes=[pltpu.VMEM((TR, HALF), BF16), pltpu.VMEM((TR, HALF), F32),
                        pltpu.VMEM((HALO + TR, HALF), F32), pltpu.VMEM((TR, HALF), F32)],
        compiler_params=_params(("parallel",), VMEM_BIG), name="odd_fwd",
    )(z1, z1, z1, z1, z1, z1, z1, z1, sgu_g, sgu_b, sgu_w, sgu_bb, conv_w, conv_b, cn_g, cn_b)


def _odd_bwd_a(z1, dycat, sgu_g, sgu_b, sgu_w, sgu_bb, conv_w, conv_b, cn_g, cn_b):
    def body(u_ref, v_ref, cg_ref, dval_ref, dglu_ref, dgate_ref, hval_ref, hglu_ref, dy_ref,
             g_ref, b_ref, w_ref, bb_ref, cw_ref, cb_ref, cng_ref, cnb_ref,
             dz_ref, ddc_ref, dw_ref, dbb_ref, dg_ref, db_ref, dcng_ref, dcnb_ref, dcb_ref,
             vn_s, s_s, xw, dcs, ds_s, dvn_s):
        i = pl.program_id(0)
        vhat, rv = _ln_stats(v_ref[...])
        g = g_ref[...]
        vn_s[...] = (vhat * g + b_ref[...]).astype(BF16)
        _sgu_gate(vn_s, s_s, w_ref, bb_ref)
        silu_c, dsilu_c = _silu_and_grad(cg_ref[...])
        dyc = dy_ref[:, 0:HALF]
        u = u_ref[...]
        s = s_s[...]
        dz_ref[:, 0:HALF] = (dyc * s * silu_c).astype(BF16)
        dz_ref[:, 2 * HALF:3 * HALF] = (dyc * u * s * dsilu_c).astype(BF16)
        ds_s[...] = dyc * u * silu_c

        @pl.when(i == 0)
        def _():
            dw_ref[...] = jnp.zeros_like(dw_ref)
            dbb_ref[...] = jnp.zeros_like(dbb_ref)

        tril = lax.broadcasted_iota(jnp.int32, (128, 128), 0) >= lax.broadcasted_iota(jnp.int32, (128, 128), 1)
        for h in range(4):
            wm = _tril_bf16(w_ref[h])
            for ch in range(NCHUNK):
                rows, cols = slice(ch * 128, (ch + 1) * 128), slice(h * SGU_CH, (h + 1) * SGU_CH)
                ds = ds_s[rows, cols]
                dsb = ds.astype(BF16)
                dw_ref[h] += jnp.where(tril, _dot(dsb, vn_s[rows, cols], NT), 0.0)
                dbb_ref[h] += jnp.broadcast_to(jnp.sum(ds, axis=1, keepdims=True), (128, 128))
                dvn_s[rows, cols] = _dot(wm, dsb, TN)
        dvn = dvn_s[...]
        _acc_rows(dg_ref, dvn * vhat, i)
        _acc_rows(db_ref, dvn, i)
        dz_ref[:, HALF:2 * HALF] = _ln_bwd(dvn, vhat, rv, g).astype(BF16)

        _conv_fwd(i, dval_ref, dglu_ref, hval_ref, hglu_ref, cw_ref, cb_ref, xw, dcs)
        dhat, rd = _ln_stats(dcs[...])
        cng = cng_ref[...]
        silu_n, dsilu_n = _silu_and_grad(dhat * cng + cnb_ref[...])
        silu_g, dsilu_g = _silu_and_grad(dgate_ref[...])
        dyd = dy_ref[:, HALF:2 * HALF]
        dz_ref[:, 5 * HALF:6 * HALF] = (dyd * silu_n * dsilu_g).astype(BF16)
        ddn = dyd * silu_g * dsilu_n
        _acc_rows(dcng_ref, ddn * dhat, i)
        _acc_rows(dcnb_ref, ddn, i)
        ddc = _ln_bwd(ddn, dhat, rd, cng)
        ddc_ref[...] = ddc
        _acc_rows(dcb_ref, ddc, i)

    vec = _full_spec((1, HALF))
    sq = _full_spec((4, 128, 128))
    return pl.pallas_call(
        body, grid=(S // TR,),
        in_specs=_odd_in_specs() + [pl.BlockSpec((TR, 2048), lambda i: (i, 0)),
                                    vec, vec, sq, sq, _full_spec((HALO, HALF)), vec, vec, vec],
        out_specs=[pl.BlockSpec((TR, ODD_IN), lambda i: (i, 0)), pl.BlockSpec((TR, HALF), lambda i: (i, 0)),
                   sq, sq, vec, vec, vec, vec, vec],
        out_shape=[jax.ShapeDtypeStruct((S, ODD_IN), BF16), jax.ShapeDtypeStruct((S, HALF), F32),
                   jax.ShapeDtypeStruct((4, 128, 128), F32), jax.ShapeDtypeStruct((4, 128, 128), F32)]
                  + [jax.ShapeDtypeStruct((1, HALF), F32)] * 5,
        scratch_shapes=[pltpu.VMEM((TR, HALF), BF16), pltpu.VMEM((TR, HALF), F32),
                        pltpu.VMEM((HALO + TR, HALF), F32), pltpu.VMEM((TR, HALF), F32),
                        pltpu.VMEM((TR, HALF), F32), pltpu.VMEM((TR, HALF), F32)],
        compiler_params=_params(("arbitrary",), VMEM_BIG), name="odd_bwd_a",
    )(z1, z1, z1, z1, z1, z1, z1, z1, dycat, sgu_g, sgu_b, sgu_w, sgu_bb, conv_w, conv_b, cn_g, cn_b)


def _odd_bwd_b(z1, ddc, dz1, conv_w):
    nt = S // TR

    def body(dval_ref, dglu_ref, hval_ref, hglu_ref, ddc_ref, hddc_ref, cw_ref, dz_in_ref,
             dz_ref, dcw_ref, xw, dwin, dxs):
        del dz_in_ref
        i, j = pl.program_id(0), pl.program_id(1)
        sg = _sigmoid(dglu_ref[...])
        dval = dval_ref[...]

        @pl.when(j == 0)
        def _():
            halo = hval_ref[...] * _sigmoid(hglu_ref[...])
            xw[0:HALO, :] = jnp.where(i > 0, halo, 0.0)
            xw[HALO:HALO + TR, :] = dval * sg
            dwin[0:TR, :] = ddc_ref[...]
            dwin[TR:TR + HALO, :] = jnp.where(i < nt - 1, hddc_ref[...], 0.0)

            @pl.when(i == 0)
            def _():
                dcw_ref[...] = jnp.zeros_like(dcw_ref)

            for rb in range(TR // SUB):
                acc = jnp.zeros((SUB, HALF), F32)
                for k in range(CONV_K):
                    acc = acc + cw_ref[k:k + 1, :] * dwin[pl.ds(rb * SUB + (CONV_K - 1) - k, SUB), :]
                dxs[rb * SUB:(rb + 1) * SUB, :] = acc
            for k in range(CONV_K):
                acc = jnp.zeros((SUB, HALF), F32)
                for rb in range(TR // SUB):
                    acc = acc + dwin[rb * SUB:(rb + 1) * SUB, :] * xw[pl.ds(rb * SUB + HALO - (CONV_K - 1) + k, SUB), :]
                dcw_ref[k:k + 1, :] += jnp.sum(acc, axis=0, keepdims=True)
            dz_ref[...] = (dxs[...] * sg).astype(BF16)

        @pl.when(j == 1)
        def _():
            dz_ref[...] = (dxs[...] * dval * sg * (1.0 - sg)).astype(BF16)

    col = lambda c: pl.BlockSpec((TR, HALF), lambda i, j: (i, c))
    prev = lambda c: pl.BlockSpec((HALO, HALF), lambda i, j: (jnp.maximum(i * (TR // HALO) - 1, 0), c))
    nxt = pl.BlockSpec((HALO, HALF), lambda i, j: (jnp.minimum((i + 1) * (TR // HALO), S // HALO - 1), 0))
    return pl.pallas_call(
        body, grid=(nt, 2),
        in_specs=[col(3), col(4), prev(3), prev(4), pl.BlockSpec((TR, HALF), lambda i, j: (i, 0)), nxt,
                  _full_spec((HALO, HALF)), pl.BlockSpec(memory_space=pl.ANY)],
        out_specs=[pl.BlockSpec((TR, HALF), lambda i, j: (i, 3 + j)), _full_spec((HALO, HALF))],
        out_shape=[jax.ShapeDtypeStruct((S, ODD_IN), BF16), jax.ShapeDtypeStruct((HALO, HALF), F32)],
        scratch_shapes=[pltpu.VMEM((HALO + TR, HALF), F32), pltpu.VMEM((TR + HALO, HALF), F32),
                        pltpu.VMEM((TR, HALF), F32)],
        input_output_aliases={7: 0},
        compiler_params=_params(("arbitrary", "arbitrary"), VMEM_BIG), name="odd_bwd_b",
    )(z1, z1, z1, z1, ddc, ddc, conv_w, dz1)


def _cast_bf16(w, name):
    r, c = w.shape
    tr = min(r, 256)
    def body(i_ref, o_ref):
        o_ref[...] = i_ref[...].astype(BF16)

    return pl.pallas_call(
        body, grid=(r // tr,), in_specs=[pl.BlockSpec((tr, c), lambda i: (i, 0))],
        out_specs=pl.BlockSpec((tr, c), lambda i: (i, 0)), out_shape=jax.ShapeDtypeStruct((r, c), BF16),
        compiler_params=_params(("parallel",)), name=name,
    )(w)


def _adamw(w, g, m, v):
    m = ADAM_B1 * m + (1.0 - ADAM_B1) * g
    v = ADAM_B2 * v + (1.0 - ADAM_B2) * (g * g)
    m_hat = m / (1.0 - ADAM_B1 ** ADAM_STEP)
    v_hat = v / (1.0 - ADAM_B2 ** ADAM_STEP)
    delta = -ADAM_LR * (m_hat / (jnp.sqrt(v_hat) + ADAM_EPS) + ADAM_WD * w)
    return delta, m, v


def _adam_reduce(parts, w, m, v, name):
    r, c = w.shape
    tr = min(r, 128)

    def body(p_ref, w_ref, m_ref, v_ref, g_ref, d_ref, nm_ref, nv_ref):
        g = p_ref[0].astype(F32)
        for d in range(1, NDEV):
            g = g + p_ref[d].astype(F32)
        g_ref[...] = g
        d_ref[...], nm_ref[...], nv_ref[...] = _adamw(w_ref[...], g, m_ref[...], v_ref[...])

    spec = pl.BlockSpec((tr, c), lambda i: (i, 0))
    return pl.pallas_call(
        body, grid=(r // tr,), in_specs=[pl.BlockSpec((NDEV, tr, c), lambda i: (0, i, 0)), spec, spec, spec],
        out_specs=[spec] * 4, out_shape=[jax.ShapeDtypeStruct((r, c), F32)] * 4,
        compiler_params=_params(("parallel",), VMEM_BIG), name=name,
    )(parts, w, m, v)


def _sum_parts(parts, name):
    r = parts.shape[1]
    tr = 8
    for cand in (512, 256, 128, 64, 32, 16, 8):
        if r % cand == 0:
            tr = cand
            break

    def body(p_ref, o_ref):
        g = p_ref[0]
        for d in range(1, NDEV):
            g = g + p_ref[d]
        o_ref[...] = g

    return pl.pallas_call(
        body, grid=(r // tr,), in_specs=[pl.BlockSpec((NDEV, tr, 128), lambda i: (0, i, 0))],
        out_specs=pl.BlockSpec((tr, 128), lambda i: (i, 0)), out_shape=jax.ShapeDtypeStruct((r, 128), F32),
        compiler_params=_params(("parallel",)), name=name,
    )(parts)


def _adam_plain(w, g, m, v, name):
    r, c = w.shape

    def body(w_ref, g_ref, m_ref, v_ref, d_ref, nm_ref, nv_ref):
        d_ref[...], nm_ref[...], nv_ref[...] = _adamw(w_ref[...], g_ref[...], m_ref[...], v_ref[...])

    spec = pl.BlockSpec((r, c), lambda i: (0, 0))
    return pl.pallas_call(
        body, grid=(1,), in_specs=[spec] * 4, out_specs=[spec] * 3,
        out_shape=[jax.ShapeDtypeStruct((r, c), F32)] * 3,
        compiler_params=_params(("arbitrary",)), name=name,
    )(w, g, m, v)


MASKS = [(mx, my, mc) for mx in (0, 1) for my in (0, 1) for mc in (0, 1)][1:]


def _exchange(arrays, scatter, name):
    nt = len(arrays)
    out_shape = [jax.ShapeDtypeStruct(((NDEV,) + a.shape) if not scatter else a.shape, a.dtype) for a in arrays]

    def body(*refs):
        ins, outs = refs[:nt], refs[nt:2 * nt]
        send_sems, recv_sems, local_sems = refs[2 * nt:]
        x, y, c = lax.axis_index("x"), lax.axis_index("y"), lax.axis_index("c")
        me = 4 * x + 2 * y + c
        copies = []
        for t in range(nt):
            src_own = ins[t].at[me] if scatter else ins[t]
            loc = pltpu.make_async_copy(src_own, outs[t].at[me], local_sems.at[t])
            loc.start()
            copies.append(loc)
            for k, (mx, my, mc) in enumerate(MASKS):
                px, py, pc = (x + mx) % 2, (y + my) % 2, (c + mc) % 2
                peer = 4 * px + 2 * py + pc
                src = ins[t].at[peer] if scatter else ins[t]
                rc = pltpu.make_async_remote_copy(
                    src_ref=src, dst_ref=outs[t].at[me], send_sem=send_sems.at[t, k], recv_sem=recv_sems.at[t, k],
                    device_id=(px, py, pc), device_id_type=MESH)
                rc.start()
                copies.append(rc)
        for cp in copies:
            cp.wait()

    hbm = pl.BlockSpec(memory_space=pl.ANY)
    return pl.pallas_call(
        body, in_specs=[hbm] * nt, out_specs=[hbm] * nt, out_shape=out_shape,
        scratch_shapes=[pltpu.SemaphoreType.DMA((nt, 7)), pltpu.SemaphoreType.DMA((nt, 7)),
                        pltpu.SemaphoreType.DMA((nt,))],
        name=name,
    )(*arrays)


SEM_SPEC = pl.BlockSpec(memory_space=pltpu.SEMAPHORE)
EFFECT = pltpu.SideEffectType.DATAFLOW_SIDE_EFFECTING


def _direct_plan(scatter):
    def plan(x, y, c, srcs, lands):
        me = 4 * x + 2 * y + c
        local, remote = [], []
        for src, land in zip(srcs, lands):
            local.append((src.at[me] if scatter else src, land.at[me]))
            for mx, my, mc in MASKS:
                px, py, pc = (x + mx) % 2, (y + my) % 2, (c + mc) % 2
                blk = src.at[4 * px + 2 * py + pc] if scatter else src
                remote.append((blk, land.at[me], (px, py, pc)))
        return local, remote
    return plan


def _split_start(name, srcs, land_shapes, plan, n_local, n_remote, dep=None):
    ns, nl = len(srcs), len(land_shapes)
    deps = [] if dep is None else [dep]
    lands = [lax.empty(s.shape, s.dtype) for s in land_shapes]

    def body(*refs):
        ins, lz = refs[:ns], refs[ns:ns + nl]
        outs = refs[ns + nl + len(deps):]
        send_sems, recv_sems, token, local_sems = outs[0], outs[1], outs[2 + ns + nl], outs[3 + ns + nl]
        local, remote = plan(lax.axis_index("x"), lax.axis_index("y"), lax.axis_index("c"), ins, lz)
        own = [pltpu.make_async_copy(src, dst, local_sems.at[i]) for i, (src, dst) in enumerate(local)]
        for cp in own:
            cp.start()
        for cp in own:
            cp.wait()
        for k, (src, dst, peer) in enumerate(remote):
            pltpu.make_async_remote_copy(src_ref=src, dst_ref=dst, send_sem=send_sems.at[k], recv_sem=recv_sems.at[k],
                                         device_id=peer, device_id_type=MESH).start()
        token[...] = jnp.zeros_like(token)

    hbm = lambda a: pltpu.HBM(a.shape, a.dtype)
    outs = pl.pallas_call(
        body, name=name,
        out_shape=(pltpu.SemaphoreType.DMA((n_remote,)), pltpu.SemaphoreType.DMA((n_remote,)),
                   *[hbm(a) for a in srcs], *[hbm(a) for a in lands], jax.ShapeDtypeStruct((8, 128), F32)),
        in_specs=[ANY_SPEC] * (ns + nl + len(deps)),
        out_specs=(SEM_SPEC, SEM_SPEC, *[ANY_SPEC] * (ns + nl), pl.BlockSpec(memory_space=pltpu.VMEM)),
        scratch_shapes=[pltpu.SemaphoreType.DMA((n_local,))],
        input_output_aliases={i: 2 + i for i in range(ns + nl)},
        compiler_params=pltpu.CompilerParams(has_side_effects=EFFECT),
    )(*[pltpu.with_memory_space_constraint(a, pltpu.HBM) for a in srcs],
      *[pltpu.with_memory_space_constraint(a, pltpu.HBM) for a in lands], *deps)
    return dict(sems=outs[:2], srcs=outs[2:2 + ns], lands=outs[2 + ns:2 + ns + nl], token=outs[-1],
                plan=plan, n_remote=n_remote)


def _split_wait(name, handle, after):
    srcs, lands, plan = handle["srcs"], handle["lands"], handle["plan"]
    ns, nl = len(srcs), len(lands)

    def body(*refs):
        ins, lz = refs[:ns], refs[ns:ns + nl]
        send_sems, recv_sems = refs[ns + nl], refs[ns + nl + 1]
        _, remote = plan(lax.axis_index("x"), lax.axis_index("y"), lax.axis_index("c"), ins, lz)
        for k, (src, dst, peer) in enumerate(remote):
            cp = pltpu.make_async_remote_copy(src_ref=src, dst_ref=dst, send_sem=send_sems.at[k],
                                              recv_sem=recv_sems.at[k], device_id=peer, device_id_type=MESH)
            cp.wait_send()
            cp.wait_recv()

    hbm = lambda a: pltpu.HBM(a.shape, a.dtype)
    outs = pl.pallas_call(
        body, name=name, out_shape=(*[hbm(a) for a in srcs], *[hbm(a) for a in lands]),
        in_specs=[ANY_SPEC] * (ns + nl) + [SEM_SPEC, SEM_SPEC, ANY_SPEC], out_specs=tuple([ANY_SPEC] * (ns + nl)),
        input_output_aliases={i: i for i in range(ns + nl)},
        compiler_params=pltpu.CompilerParams(has_side_effects=EFFECT),
    )(*srcs, *lands, *handle["sems"], after)
    return list(outs[ns:])


def _sc_exchange(name, collective_id, arrays, scatter):
    nt = len(arrays)
    out_type = [jax.ShapeDtypeStruct(a.shape if scatter else (NDEV,) + a.shape, a.dtype) for a in arrays]

    def body(*refs):
        ins, outs = refs[:nt], refs[nt:2 * nt]
        send_sems, recv_sems, local_sems = refs[2 * nt:3 * nt], refs[3 * nt:4 * nt], refs[4 * nt:5 * nt]
        x, y, c = lax.axis_index("x"), lax.axis_index("y"), lax.axis_index("c")
        peers = [(mx + x - 2 * mx * x, my + y - 2 * my * y, mc + c - 2 * mc * c) for mx, my, mc in MASKS]
        barrier = pltpu.get_barrier_semaphore()
        for peer in peers:
            pl.semaphore_signal(barrier, inc=1, device_id=peer, device_id_type=MESH)
        pl.semaphore_wait(barrier, len(peers))
        me = 4 * x + 2 * y + c
        own = []
        for t in range(nt):
            cp = pltpu.make_async_copy(ins[t].at[me] if scatter else ins[t], outs[t].at[me], local_sems[t])
            cp.start()
            own.append(cp)
            for px, py, pc in peers:
                src = ins[t].at[4 * px + 2 * py + pc] if scatter else ins[t]
                pltpu.make_async_remote_copy(src_ref=src, dst_ref=outs[t].at[me], send_sem=send_sems[t],
                                             recv_sem=recv_sems[t], device_id=(px, py, pc), device_id_type=MESH).start()
        for t in range(nt):
            own[t].wait()
            seven = outs[t].at[pl.ds(0, NDEV - 1)]
            drain = pltpu.make_async_remote_copy(src_ref=seven, dst_ref=seven, send_sem=send_sems[t],
                                                 recv_sem=recv_sems[t], device_id=(x, y, c), device_id_type=MESH)
            drain.wait_send()
            drain.wait_recv()

    return pl.kernel(
        body, out_type=out_type, mesh=plsc.ScalarSubcoreMesh(axis_name="sequencer", num_cores=1),
        scratch_types=[pltpu.SemaphoreType.DMA] * (3 * nt),
        compiler_params=pltpu.CompilerParams(collective_id=collective_id), name=name,
    )(*arrays)


SMALL = {
    "e_pre_norm": ((2048,), None), "e_pool_w": ((4, 256, 256), 1), "e_pool_scale": ((1024,), None),
    "e_post_norm": ((2048,), None), "o_pre_norm": ((2048,), 0), "o_sgu_norm_g": ((1024,), 0),
    "o_sgu_norm_b": ((1024,), 0), "o_sgu_w": ((4, 128, 128), None), "o_sgu_b": ((4, 128), None),
    "o_conv_w": ((31, 1024), 1), "o_conv_b": ((1024,), 0), "o_conv_norm_g": ((1024,), 0),
    "o_conv_norm_b": ((1024,), 0), "o_post_norm": ((2048,), 0),
}
SMALL_SHARDED = [n for n, (_, ax) in SMALL.items() if ax is not None]


def _shard_shape(name):
    shape, ax = SMALL[name]
    if ax is None:
        return shape
    return tuple(s // NDEV if i == ax else s for i, s in enumerate(shape))


def _pack(arrs, row_multiple=1):
    flat = jnp.concatenate([a.reshape(-1) for a in arrs])
    pad = -flat.shape[0] % (128 * row_multiple)
    return jnp.concatenate([flat, jnp.zeros((pad,), F32)]).reshape(-1, 128)


def _unpack(buf, shapes):
    flat = buf.reshape(-1)
    out, off = [], 0
    for shp in shapes:
        n = int(np.prod(shp))
        out.append(flat[off:off + n].reshape(shp))
        off += n
    return out


def _take_shard(full, name, me):
    shape, ax = SMALL[name]
    if ax is None:
        return full
    n = shape[ax] // NDEV
    return lax.dynamic_slice_in_dim(full, me * n, n, axis=ax)


BIG = ("e_w_in", "e_w_out", "o_w_in", "o_w_out")
WEIGHTS = ["e_pre_norm", "e_w_in", "e_pool_w", "e_pool_scale", "e_w_out", "e_post_norm", "o_pre_norm", "o_w_in",
           "o_sgu_norm_g", "o_sgu_norm_b", "o_sgu_w", "o_sgu_b", "o_conv_w", "o_conv_b", "o_conv_norm_g",
           "o_conv_norm_b", "o_w_out", "o_post_norm"]


def kernel(x, e_pre_norm, e_w_in, e_pool_w, e_pool_scale, e_w_out, e_post_norm, o_pre_norm, o_w_in, o_sgu_norm_g, o_sgu_norm_b, o_sgu_w, o_sgu_b, o_conv_w, o_conv_b, o_conv_norm_g, o_conv_norm_b, o_w_out, o_post_norm, loss_target, m_e_pre_norm, m_e_w_in, m_e_pool_w, m_e_pool_scale, m_e_w_out, m_e_post_norm, m_o_pre_norm, m_o_w_in, m_o_sgu_norm_g, m_o_sgu_norm_b, m_o_sgu_w, m_o_sgu_b, m_o_conv_w, m_o_conv_b, m_o_conv_norm_g, m_o_conv_norm_b, m_o_w_out, m_o_post_norm, v_e_pre_norm, v_e_w_in, v_e_pool_w, v_e_pool_scale, v_e_w_out, v_e_post_norm, v_o_pre_norm, v_o_w_in, v_o_sgu_norm_g, v_o_sgu_norm_b, v_o_sgu_w, v_o_sgu_b, v_o_conv_w, v_o_conv_b, v_o_conv_norm_g, v_o_conv_norm_b, v_o_w_out, v_o_post_norm):
    given = dict(locals())
    w = {n: given[n][0] for n in WEIGHTS}
    m = {n: given["m_" + n][0] for n in WEIGHTS}
    v = {n: given["v_" + n][0] for n in WEIGHTS}
    me = 4 * lax.axis_index("x") + 2 * lax.axis_index("y") + lax.axis_index("c")
    x, target = x[0], loss_target[0]
    row = lambda a: a.reshape(1, -1)

    bf = {n: _cast_bf16(w[n], "cast_" + n) for n in BIG}
    wg_e_in, small_rows = _sc_exchange("gather_a", 0, [bf["e_w_in"], _pack([w[n] for n in SMALL_SHARDED])], False)
    wg_e_out, wg_o_in, wg_o_out = _sc_exchange("gather_b", 1, [bf["e_w_out"], bf["o_w_in"], bf["o_w_out"]], False)
    h0 = _pre0_fwd(x, row(w["e_pre_norm"]))
    p = {n: w[n] for n in SMALL if SMALL[n][1] is None}
    small_rows = small_rows.reshape(NDEV, -1)
    off = 0
    for n in SMALL_SHARDED:
        shp, ax = _shard_shape(n), SMALL[n][1]
        cnt = int(np.prod(shp))
        blk = small_rows[:, off:off + cnt].reshape((NDEV,) + shp)
        p[n] = jnp.moveaxis(blk, 0, ax).reshape(SMALL[n][0])
        off += cnt
    tabs = _rope_tables()
    pool_w_bf = p["e_pool_w"].astype(BF16)
    sgu_bb = jnp.broadcast_to(p["o_sgu_b"][:, :, None], (4, 128, 128))
    conv_w = jnp.concatenate([p["o_conv_w"], jnp.zeros((HALO - CONV_K, HALF), F32)], axis=0)
    odd_p = (row(p["o_sgu_norm_g"]), row(p["o_sgu_norm_b"]), p["o_sgu_w"], sgu_bb, conv_w,
             row(p["o_conv_b"]), row(p["o_conv_norm_g"]), row(p["o_conv_norm_b"]))

    z0 = _mm_in(h0, wg_e_in, "mm_z0")
    ycat0 = _pool_fwd(z0, pool_w_bf, row(p["e_pool_scale"]))
    ycat0 = _attn_fwd(z0, ycat0, tabs)
    w_out_e, w_out_o = wg_e_out.reshape(2048, D), wg_o_out.reshape(2048, D)
    y0 = _mm_out(ycat0, w_out_e, "mm_y0")
    x1, h1 = _post0_fwd(x, y0, row(p["e_post_norm"]), row(p["o_pre_norm"]))
    z1 = _mm_in(h1, wg_o_in, "mm_z1")
    ycat1 = _odd_fwd(z1, *odd_p)
    y1 = _mm_out(ycat1, w_out_o, "mm_y1")

    g = {}
    loss, dx2, dy1, g["o_post_norm"] = _post1_bwd(y1, x1, target, row(p["o_post_norm"]))
    loss = lax.psum(loss[0, 0], ("x", "y", "c"))
    parts = {}
    dw = _mm_out_dw(ycat1, dy1, "mm_dwout1").reshape(NDEV, 256, D)
    parts["o_w_out"], = _sc_exchange("scatter_o_w_out", 2, [dw], True)
    dycat1 = _mm_out_dx(dy1, w_out_o, "mm_dycat1", dw)
    dz1, ddc, g["o_sgu_w"], d_sgu_bb, g["o_sgu_norm_g"], g["o_sgu_norm_b"], g["o_conv_norm_g"], \
        g["o_conv_norm_b"], g["o_conv_b"] = _odd_bwd_a(z1, dycat1, *odd_p)
    dz1, d_conv_w = _odd_bwd_b(z1, ddc, dz1, conv_w)
    g["o_sgu_b"] = d_sgu_bb[:, :, 0]
    g["o_conv_w"] = d_conv_w[:CONV_K]
    dw = _mm_in_dw(h1, dz1, ODD_IN // NDEV, "mm_dwin1")
    parts["o_w_in"], = _sc_exchange("scatter_o_w_in", 3, [dw], True)
    dh1 = _mm_in_dx(dz1, wg_o_in, "mm_dh1", dw)
    dx1, dy0, g["o_pre_norm"], g["e_post_norm"] = _mid_bwd(dx2, dh1, x1, y0, row(p["o_pre_norm"]),
                                                           row(p["e_post_norm"]))
    dw = _mm_out_dw(ycat0, dy0, "mm_dwout0").reshape(NDEV, 256, D)
    parts["e_w_out"], = _sc_exchange("scatter_e_w_out", 4, [dw], True)
    dycat0 = _mm_out_dx(dy0, w_out_e, "mm_dycat0", dw)
    da_in, da_gate, g["e_pool_w"], g["e_pool_scale"] = _pool_bwd(z0, dycat0, pool_w_bf, row(p["e_pool_scale"]))
    dq, dk, dv, dbg = _attn_bwd(z0, dycat0, tabs)
    dz0 = jnp.concatenate([da_in, da_gate, dq, dk, dv, dbg], axis=1)
    late = [n for n in SMALL if n != "e_pre_norm"]
    dw = _mm_in_dw(h0, dz0, EVEN_IN // NDEV, "mm_dwin0")
    parts["e_w_in"], = _sc_exchange("scatter_e_w_in", 5, [dw], True)
    recv_small, = _sc_exchange("gather_small_grads", 6, [_pack([g[n].reshape(SMALL[n][0]) for n in late], 512)], False)
    dh0 = _mm_in_dx(dz0, wg_e_in, "mm_dh0", dw)
    grad_x, g["e_pre_norm"] = _pre0_bwd(dx1, dh0, x, row(p["e_pre_norm"]))
    last, = _sc_exchange("gather_e_pre_norm_grad", 7, [g["e_pre_norm"].reshape(16, 128)], False)

    grads, deltas, new_m, new_v = {}, {}, {}, {}
    for n in ("o_w_out", "o_w_in", "e_w_out", "e_w_in"):
        grads[n], deltas[n], new_m[n], new_v[n] = _adam_reduce(parts[n], w[n], m[n], v[n], "adam_" + n)
    g_small = dict(zip(late, _unpack(_sum_parts(recv_small, "sum_small_grads"), [SMALL[n][0] for n in late])))
    g_small["e_pre_norm"] = _sum_parts(last, "sum_e_pre_norm_grad").reshape(2048)
    for n in SMALL:
        grads[n] = _take_shard(g_small[n], n, me)
    names = list(SMALL)
    shapes = [_shard_shape(n) for n in names]
    d_pack, m_pack, v_pack = _adam_plain(_pack([w[n] for n in names]), _pack([grads[n] for n in names]),
                                         _pack([m[n] for n in names]), _pack([v[n] for n in names]), "adam_small")
    for n, d_, m_, v_ in zip(names, _unpack(d_pack, shapes), _unpack(m_pack, shapes), _unpack(v_pack, shapes)):
        deltas[n], new_m[n], new_v[n] = d_, m_, v_

    lead = lambda a: a[None]
    return (loss, grad_x[None], *[lead(grads[n]) for n in WEIGHTS], *[lead(deltas[n]) for n in WEIGHTS],
            *[lead(new_m[n]) for n in WEIGHTS], *[lead(new_v[n]) for n in WEIGHTS])
```

```python
import functools

import numpy as np
import jax
import jax.numpy as jnp
from jax import lax
from jax.experimental import pallas as pl
from jax.experimental.pallas import tpu as pltpu
from jax.experimental.pallas import tpu_sc as plsc

F32 = jnp.float32
BF16 = jnp.bfloat16

S = 2048
D = 2048
NDEV = 8
EPS = 1e-6
NEG = -1e30
HEAD_DIM = 128
ROT_DIM = 32
ROPE_THETA = 500000.0
PATTERNS = ((128, 1), (512, 4), (2048, 16))
BLK = 128
EVEN_IN = 12288
ODD_IN = 6144
HALF = 1024
CONV_K = 31
HALO = 32
TR = 256
SUB = 32

ADAM_LR = 0.001
ADAM_B1 = 0.9
ADAM_B2 = 0.999
ADAM_EPS = 1e-08
ADAM_WD = 0.01
ADAM_STEP = 10

VMEM_BIG = 56 * 1024 * 1024
MESH = pl.DeviceIdType.MESH

NN = (((1,), (0,)), ((), ()))
NT = (((1,), (1,)), ((), ()))
TN = (((0,), (0,)), ((), ()))


def _dot(a, b, dn=NN):
    return lax.dot_general(a, b, dn, preferred_element_type=F32)


def _sigmoid(x):
    return 1.0 / (1.0 + jnp.exp(-x))


def _silu_and_grad(x):
    sg = _sigmoid(x)
    return x * sg, sg * (1.0 + x * (1.0 - sg))


def _params(sem, vmem=None):
    return pltpu.CompilerParams(dimension_semantics=sem, vmem_limit_bytes=vmem)


ANY_SPEC = pl.BlockSpec(memory_space=pl.ANY)


def _matmul(a, b, *, dn, grid, a_spec, b_spec, o_spec, out_shape, out_dtype, acc_shape, name, dep=None):
    nk = grid[2]
    deps = [] if dep is None else [dep]

    def body(a_ref, b_ref, *rest):
        o_ref, acc = rest[len(deps)], rest[len(deps) + 1:]
        if nk == 1:
            o_ref[...] = _dot(a_ref[...], b_ref[...], dn).astype(o_ref.dtype)
            return
        acc_ref = acc[0]
        k = pl.program_id(2)

        @pl.when(k == 0)
        def _():
            acc_ref[...] = jnp.zeros_like(acc_ref)

        acc_ref[...] += _dot(a_ref[...], b_ref[...], dn)

        @pl.when(k == nk - 1)
        def _():
            o_ref[...] = acc_ref[...].astype(o_ref.dtype)

    return pl.pallas_call(
        body, grid=grid, in_specs=[a_spec, b_spec] + [ANY_SPEC] * len(deps), out_specs=o_spec,
        out_shape=jax.ShapeDtypeStruct(out_shape, out_dtype),
        scratch_shapes=[] if nk == 1 else [pltpu.VMEM(acc_shape, F32)],
        compiler_params=_params(("parallel", "parallel", "arbitrary"), VMEM_BIG), name=name,
    )(a, b, *deps)


TM = 512


def _mm_in(h, wg, name):
    nb = wg.shape[2]
    tn = 512 if nb % 512 == 0 else nb
    per = nb // tn
    return _matmul(
        h, wg, dn=NN, grid=(S // TM, NDEV * per, 1),
        a_spec=pl.BlockSpec((TM, D), lambda i, j, k: (i, 0)),
        b_spec=pl.BlockSpec((None, D, tn), lambda i, j, k: (j // per, 0, j % per)),
        o_spec=pl.BlockSpec((TM, tn), lambda i, j, k: (i, j)),
        out_shape=(S, NDEV * nb), out_dtype=F32, acc_shape=(TM, tn), name=name)


def _mm_in_dx(dz, wg, name, dep=None):
    nb = wg.shape[2]
    return _matmul(
        dz, wg, dn=NT, grid=(S // TM, D // 512, NDEV),
        a_spec=pl.BlockSpec((TM, nb), lambda i, j, k: (i, k)),
        b_spec=pl.BlockSpec((None, 512, nb), lambda i, j, k: (k, j, 0)),
        o_spec=pl.BlockSpec((TM, 512), lambda i, j, k: (i, j)),
        out_shape=(S, D), out_dtype=F32, acc_shape=(TM, 512), name=name, dep=dep)


def _mm_in_dw(h, dz, nb, name, dep=None):
    tn = 512 if nb % 512 == 0 else nb
    per = nb // tn
    return _matmul(
        h, dz, dn=TN, grid=(D // TM, NDEV * per, 1),
        a_spec=pl.BlockSpec((S, TM), lambda i, j, k: (0, i)),
        b_spec=pl.BlockSpec((S, tn), lambda i, j, k: (0, j)),
        o_spec=pl.BlockSpec((None, TM, tn), lambda i, j, k: (j // per, i, j % per)),
        out_shape=(NDEV, D, nb), out_dtype=BF16, acc_shape=(TM, tn), name=name, dep=dep)


def _mm_out(yc, w, name):
    return _matmul(
        yc, w, dn=NN, grid=(S // TM, D // 512, 1),
        a_spec=pl.BlockSpec((TM, 2048), lambda i, j, k: (i, 0)),
        b_spec=pl.BlockSpec((2048, 512), lambda i, j, k: (0, j)),
        o_spec=pl.BlockSpec((TM, 512), lambda i, j, k: (i, j)),
        out_shape=(S, D), out_dtype=F32, acc_shape=(TM, 512), name=name)


def _mm_out_dx(dy, w, name, dep=None):
    return _matmul(
        dy, w, dn=NT, grid=(S // TM, 2048 // 512, 1),
        a_spec=pl.BlockSpec((TM, D), lambda i, j, k: (i, 0)),
        b_spec=pl.BlockSpec((512, D), lambda i, j, k: (j, 0)),
        o_spec=pl.BlockSpec((TM, 512), lambda i, j, k: (i, j)),
        out_shape=(S, 2048), out_dtype=F32, acc_shape=(TM, 512), name=name, dep=dep)


def _mm_out_dw(yc, dy, name):
    return _matmul(
        yc, dy, dn=TN, grid=(2048 // TM, D // 512, 1),
        a_spec=pl.BlockSpec((S, TM), lambda i, j, k: (0, i)),
        b_spec=pl.BlockSpec((S, 512), lambda i, j, k: (0, j)),
        o_spec=pl.BlockSpec((TM, 512), lambda i, j, k: (i, j)),
        out_shape=(2048, D), out_dtype=BF16, acc_shape=(TM, 512), name=name)


def _row_spec(w=D):
    return pl.BlockSpec((TR, w), lambda i: (i, 0))


def _vec_spec(w=D):
    return pl.BlockSpec((1, w), lambda i: (0, 0))


def _rms_stats(x):
    r = lax.rsqrt(jnp.mean(x * x, axis=-1, keepdims=True) + EPS)
    return x * r, r


def _rms_bwd(dn, xhat, r, g):
    dxh = dn * g
    return r * (dxh - xhat * jnp.mean(dxh * xhat, axis=-1, keepdims=True))


def _acc_rows(ref, val, i):
    s = jnp.sum(val, axis=0, keepdims=True)

    @pl.when(i == 0)
    def _():
        ref[...] = s

    @pl.when(i > 0)
    def _():
        ref[...] += s


def _pre0_fwd(x, g, dep=None):
    deps = [] if dep is None else [dep]

    def body(x_ref, g_ref, *rest):
        xhat, _ = _rms_stats(x_ref[...])
        rest[-1][...] = (xhat * g_ref[...]).astype(BF16)

    return pl.pallas_call(
        body, grid=(S // TR,), in_specs=[_row_spec(), _vec_spec()] + [ANY_SPEC] * len(deps), out_specs=_row_spec(),
        out_shape=jax.ShapeDtypeStruct((S, D), BF16), compiler_params=_params(("parallel",)), name="pre0_fwd",
    )(x, g, *deps)


def _post0_fwd(x, y0, g_post, g_pre1):
    def body(x_ref, y_ref, gp_ref, g1_ref, x1_ref, h1_ref):
        yhat, _ = _rms_stats(y_ref[...])
        x1 = x_ref[...] + yhat * gp_ref[...]
        x1_ref[...] = x1
        xhat, _ = _rms_stats(x1)
        h1_ref[...] = (xhat * g1_ref[...]).astype(BF16)

    return pl.pallas_call(
        body, grid=(S // TR,), in_specs=[_row_spec(), _row_spec(), _vec_spec(), _vec_spec()],
        out_specs=[_row_spec(), _row_spec()],
        out_shape=[jax.ShapeDtypeStruct((S, D), F32), jax.ShapeDtypeStruct((S, D), BF16)],
        compiler_params=_params(("parallel",)), name="post0_fwd",
    )(x, y0, g_post, g_pre1)


def _post1_bwd(y1, x1, target, g_post):
    def body(y_ref, x1_ref, t_ref, g_ref, loss_ref, dx2_ref, dy_ref, dg_ref):
        i = pl.program_id(0)
        yhat, r = _rms_stats(y_ref[...])
        g = g_ref[...]
        err = x1_ref[...] + yhat * g - t_ref[...]
        part = jnp.sum(jnp.sum(err * err, axis=-1, keepdims=True), axis=0, keepdims=True) * (0.5 / D)
        _acc_rows(loss_ref, jnp.broadcast_to(part, (1, 128)), i)
        dx2 = err * (1.0 / D)
        dx2_ref[...] = dx2
        _acc_rows(dg_ref, dx2 * yhat, i)
        dy_ref[...] = _rms_bwd(dx2, yhat, r, g).astype(BF16)

    return pl.pallas_call(
        body, grid=(S // TR,), in_specs=[_row_spec(), _row_spec(), _row_spec(), _vec_spec()],
        out_specs=[_vec_spec(128), _row_spec(), _row_spec(), _vec_spec()],
        out_shape=[jax.ShapeDtypeStruct((1, 128), F32), jax.ShapeDtypeStruct((S, D), F32),
                   jax.ShapeDtypeStruct((S, D), BF16), jax.ShapeDtypeStruct((1, D), F32)],
        compiler_params=_params(("arbitrary",)), name="post1_bwd",
    )(y1, x1, target, g_post)


def _mid_bwd(dx2, dh1, x1, y0, g_pre1, g_post0):
    def body(dx2_ref, dh_ref, x1_ref, y_ref, g1_ref, gp_ref, dx1_ref, dy_ref, dg1_ref, dgp_ref):
        i = pl.program_id(0)
        xhat, r1 = _rms_stats(x1_ref[...])
        dh = dh_ref[...]
        _acc_rows(dg1_ref, dh * xhat, i)
        dx1 = dx2_ref[...] + _rms_bwd(dh, xhat, r1, g1_ref[...])
        dx1_ref[...] = dx1
        yhat, r0 = _rms_stats(y_ref[...])
        _acc_rows(dgp_ref, dx1 * yhat, i)
        dy_ref[...] = _rms_bwd(dx1, yhat, r0, gp_ref[...]).astype(BF16)

    return pl.pallas_call(
        body, grid=(S // TR,),
        in_specs=[_row_spec(), _row_spec(), _row_spec(), _row_spec(), _vec_spec(), _vec_spec()],
        out_specs=[_row_spec(), _row_spec(), _vec_spec(), _vec_spec()],
        out_shape=[jax.ShapeDtypeStruct((S, D), F32), jax.ShapeDtypeStruct((S, D), BF16),
                   jax.ShapeDtypeStruct((1, D), F32), jax.ShapeDtypeStruct((1, D), F32)],
        compiler_params=_params(("arbitrary",)), name="mid_bwd",
    )(dx2, dh1, x1, y0, g_pre1, g_post0)


def _pre0_bwd(dx1, dh0, x, g):
    def body(dx1_ref, dh_ref, x_ref, g_ref, gx_ref, dg_ref):
        i = pl.program_id(0)
        xhat, r = _rms_stats(x_ref[...])
        dh = dh_ref[...]
        _acc_rows(dg_ref, dh * xhat, i)
        gx_ref[...] = dx1_ref[...] + _rms_bwd(dh, xhat, r, g_ref[...])

    return pl.pallas_call(
        body, grid=(S // TR,), in_specs=[_row_spec(), _row_spec(), _row_spec(), _vec_spec()],
        out_specs=[_row_spec(), _vec_spec()],
        out_shape=[jax.ShapeDtypeStruct((S, D), F32), jax.ShapeDtypeStruct((1, D), F32)],
        compiler_params=_params(("arbitrary",)), name="pre0_bwd",
    )(dx1, dh0, x, g)


POOL_CH = 256


def _pool_apply(a, w, transpose):
    n = a.shape[0]
    row = lax.broadcasted_iota(jnp.int32, a.shape, 0)
    cnt = jnp.minimum(row + 1, w).astype(F32)
    s = a / cnt if transpose else a
    for k in (1, 2, 4, 8):
        if transpose:
            sh = jnp.where(row < n - k, pltpu.roll(s, n - k, 0), 0.0)
        else:
            sh = jnp.where(row >= k, pltpu.roll(s, k, 0), 0.0)
        s = jnp.where(w > k, s + sh, s)
    return s - a if transpose else s / cnt - a


def _pool_fwd(z0, pool_w, pool_scale):
    def body(a_ref, gate_ref, w_ref, sc_ref, out_ref):
        win = jnp.left_shift(2, pl.program_id(0))
        pooled = _pool_apply(a_ref[...], win, False)
        mixed = _dot(pooled.astype(BF16), w_ref[...])
        gate = gate_ref[...]
        out_ref[...] = (mixed * sc_ref[...] * (gate * _sigmoid(gate))).astype(BF16)

    return pl.pallas_call(
        body, grid=(4,),
        in_specs=[pl.BlockSpec((S, POOL_CH), lambda g: (0, g)), pl.BlockSpec((S, POOL_CH), lambda g: (0, 4 + g)),
                  pl.BlockSpec((None, POOL_CH, POOL_CH), lambda g: (g, 0, 0)),
                  pl.BlockSpec((1, POOL_CH), lambda g: (0, g))],
        out_specs=pl.BlockSpec((S, POOL_CH), lambda g: (0, g)),
        out_shape=jax.ShapeDtypeStruct((S, 2048), BF16),
        compiler_params=_params(("parallel",), VMEM_BIG), name="pool_fwd",
    )(z0, z0, pool_w, pool_scale)


def _pool_bwd(z0, dycat, pool_w, pool_scale):
    def body(a_ref, gate_ref, dy_ref, w_ref, sc_ref, da_ref, dgate_ref, dw_ref, dsc_ref):
        win = jnp.left_shift(2, pl.program_id(0))
        pooled = _pool_apply(a_ref[...], win, False).astype(BF16)
        w = w_ref[...]
        mixed = _dot(pooled, w)
        silu, dsilu = _silu_and_grad(gate_ref[...])
        dy = dy_ref[...]
        sc = sc_ref[...]
        dgate_ref[...] = (dy * (mixed * sc) * dsilu).astype(BF16)
        dms = dy * silu
        dsc_ref[...] = jnp.sum(dms * mixed, axis=0, keepdims=True)
        dmixed = (dms * sc).astype(BF16)
        dw_ref[...] = _dot(pooled, dmixed, TN)
        dpooled = _dot(dmixed, w, NT)
        da_ref[...] = _pool_apply(dpooled, win, True).astype(BF16)

    slab = lambda off: pl.BlockSpec((S, POOL_CH), lambda g: (0, off + g))
    return pl.pallas_call(
        body, grid=(4,),
        in_specs=[slab(0), slab(4), slab(0), pl.BlockSpec((None, POOL_CH, POOL_CH), lambda g: (g, 0, 0)),
                  pl.BlockSpec((1, POOL_CH), lambda g: (0, g))],
        out_specs=[slab(0), slab(0), pl.BlockSpec((None, POOL_CH, POOL_CH), lambda g: (g, 0, 0)),
                   pl.BlockSpec((1, POOL_CH), lambda g: (0, g))],
        out_shape=[jax.ShapeDtypeStruct((S, HALF), BF16), jax.ShapeDtypeStruct((S, HALF), BF16),
                   jax.ShapeDtypeStruct((4, POOL_CH, POOL_CH), F32), jax.ShapeDtypeStruct((1, HALF), F32)],
        compiler_params=_params(("parallel",), VMEM_BIG), name="pool_bwd",
    )(z0, z0, dycat, pool_w, pool_scale)


Q_COL, K_COL, V_COL, BG_COL = 2048 // 128, 5120 // 128, 8192 // 128, 11264 // 128
SCALE = HEAD_DIM ** -0.5


def _rope_tables():
    pos = jnp.arange(S, dtype=F32)
    inv_freq = jnp.power(ROPE_THETA, -jnp.arange(0, ROT_DIM, 2, dtype=F32) / ROT_DIM)
    ang = pos[:, None] * inv_freq[None, :]
    cos, sin = jnp.cos(ang), jnp.sin(ang)
    half = ROT_DIM // 2
    zeros = jnp.zeros((S, HEAD_DIM - ROT_DIM), F32)
    c = jnp.concatenate([cos, cos, jnp.ones((S, HEAD_DIM - ROT_DIM), F32)], axis=1)
    a = jnp.concatenate([-sin, jnp.zeros((S, half), F32), zeros], axis=1)
    b = jnp.concatenate([jnp.zeros((S, half), F32), sin, zeros], axis=1)
    return c, a, b


def _rope(t, c, a, b):
    half = ROT_DIM // 2
    return t * c + pltpu.roll(t, HEAD_DIM - half, 1) * a + pltpu.roll(t, half, 1) * b


def _rope_t(d, c, a, b):
    half = ROT_DIM // 2
    return d * c + pltpu.roll(d * a, half, 1) + pltpu.roll(d * b, HEAD_DIM - half, 1)


def _deinterleave(dst, src, dil, cast=None):
    length = S // dil
    for r in range(dil):
        v = src[...] if dil == 1 else src[pl.ds(r, length, stride=dil), :]
        dst[r * length:(r + 1) * length, :] = v if cast is None else v.astype(cast)


def _interleave(dst, src, dil):
    length = S // dil
    for r in range(dil):
        if dil == 1:
            dst[...] = src[...]
        else:
            dst[pl.ds(r, length, stride=dil), :] = src[r * length:(r + 1) * length, :]


def _unit_scores(u, nb, qd, kd):
    o0 = pl.multiple_of(u * BLK, BLK)
    p0 = pl.multiple_of(jnp.maximum(u - 1, 0) * BLK, BLK)
    q = qd[pl.ds(o0, BLK), :]
    row = lax.broadcasted_iota(jnp.int32, (BLK, BLK), 0)
    col = lax.broadcasted_iota(jnp.int32, (BLK, BLK), 1)
    s_own = jnp.where(col <= row, _dot(q, kd[pl.ds(o0, BLK), :], NT) * SCALE, NEG)
    if nb == 1:
        return o0, p0, q, s_own, None
    has_prev = (u % nb) != 0
    s_prev = jnp.where((col >= row) & has_prev, _dot(q, kd[pl.ds(p0, BLK), :], NT) * SCALE, NEG)
    return o0, p0, q, s_own, s_prev


def _attn_group_fwd(dil, q_ref, k_ref, v_ref, tabs, tmp, qd, kd, vd, od, ld, og, lg):
    nb = S // dil // BLK
    c, a, b = tabs
    tmp[...] = _rope(q_ref[...], c, a, b)
    _deinterleave(qd, tmp, dil, BF16)
    tmp[...] = _rope(k_ref[...], c, a, b)
    _deinterleave(kd, tmp, dil, BF16)
    _deinterleave(vd, v_ref, dil, BF16)

    def unit(u, carry):
        o0, p0, _, s_own, s_prev = _unit_scores(u, nb, qd, kd)
        m = jnp.max(s_own, axis=1, keepdims=True)
        if s_prev is not None:
            m = jnp.maximum(m, jnp.max(s_prev, axis=1, keepdims=True))
        p_own = jnp.exp(s_own - m)
        den = jnp.sum(p_own, axis=1, keepdims=True)
        acc = _dot(p_own.astype(BF16), vd[pl.ds(o0, BLK), :])
        if s_prev is not None:
            p_prev = jnp.exp(s_prev - m)
            den = den + jnp.sum(p_prev, axis=1, keepdims=True)
            acc = acc + _dot(p_prev.astype(BF16), vd[pl.ds(p0, BLK), :])
        od[pl.ds(o0, BLK), :] = acc / den
        ld[pl.ds(o0, BLK), :] = jnp.broadcast_to(m + jnp.log(den), (BLK, HEAD_DIM))
        return carry

    lax.fori_loop(0, S // BLK, unit, 0)
    _interleave(og, od, dil)
    _interleave(lg, ld, dil)


def _group_weights(lgs):
    l0, l1, l2 = lgs[0][...], lgs[1][...], lgs[2][...]
    mx = jnp.maximum(l0, jnp.maximum(l1, l2))
    e0, e1, e2 = jnp.exp(l0 - mx), jnp.exp(l1 - mx), jnp.exp(l2 - mx)
    den = e0 + e1 + e2
    return e0 / den, e1 / den, e2 / den


def _head_spec(base, ngroups_axis=True):
    return pl.BlockSpec((S, HEAD_DIM), lambda h, p: (0, base + (p % 3) * 8 + h))


ATTN_SCRATCH_FWD = [
    pltpu.VMEM((S, HEAD_DIM), F32),
    pltpu.VMEM((S, HEAD_DIM), BF16), pltpu.VMEM((S, HEAD_DIM), BF16), pltpu.VMEM((S, HEAD_DIM), BF16),
    pltpu.VMEM((S, HEAD_DIM), F32), pltpu.VMEM((S, HEAD_DIM), F32),
    pltpu.VMEM((S, HEAD_DIM), F32), pltpu.VMEM((S, HEAD_DIM), F32), pltpu.VMEM((S, HEAD_DIM), F32),
    pltpu.VMEM((S, HEAD_DIM), F32), pltpu.VMEM((S, HEAD_DIM), F32), pltpu.VMEM((S, HEAD_DIM), F32),
]


def _attn_fwd(z0, ycat, tabs):
    def body(q_ref, k_ref, v_ref, gate_ref, c_ref, a_ref, b_ref, ycat_ref, out_ref,
             tmp, qd, kd, vd, od, ld, og0, og1, og2, lg0, lg1, lg2):
        del ycat_ref
        p = pl.program_id(1)
        ogs, lgs = (og0, og1, og2), (lg0, lg1, lg2)
        tabs_v = (c_ref[...], a_ref[...], b_ref[...])
        for gi, (_, dil) in enumerate(PATTERNS):
            @pl.when(p == gi)
            def _(gi=gi, dil=dil):
                _attn_group_fwd(dil, q_ref, k_ref, v_ref, tabs_v, tmp, qd, kd, vd, od, ld, ogs[gi], lgs[gi])

        @pl.when(p == 2)
        def _():
            w0, w1, w2 = _group_weights(lgs)
            o = w0 * og0[...] + w1 * og1[...] + w2 * og2[...]
            gate = gate_ref[...]
            out_ref[...] = (o * (gate * _sigmoid(gate))).astype(BF16)

    tab = pl.BlockSpec((S, HEAD_DIM), lambda h, p: (0, 0))
    return pl.pallas_call(
        body, grid=(8, 3),
        in_specs=[_head_spec(Q_COL), _head_spec(K_COL), _head_spec(V_COL),
                  pl.BlockSpec((S, HEAD_DIM), lambda h, p: (0, BG_COL + h)), tab, tab, tab,
                  pl.BlockSpec(memory_space=pl.ANY)],
        out_specs=pl.BlockSpec((S, HEAD_DIM), lambda h, p: (0, 8 + h)),
        out_shape=jax.ShapeDtypeStruct((S, 2048), BF16),
        scratch_shapes=ATTN_SCRATCH_FWD, input_output_aliases={7: 0},
        compiler_params=_params(("parallel", "arbitrary"), VMEM_BIG), name="attn_fwd",
    )(z0, z0, z0, z0, *tabs, ycat)


def _attn_bwd(z0, dycat, tabs):
    def body(q_ref, k_ref, v_ref, gate_ref, dy_ref, c_ref, a_ref, b_ref,
             dq_ref, dk_ref, dv_ref, dbg_ref,
             tmp, qd, kd, vd, od, ld, og0, og1, og2, lg0, lg1, lg2, cg0, cg1, cg2, dod, cd, dqd, dkd, dvd):
        p = pl.program_id(1)
        ogs, lgs, cgs = (og0, og1, og2), (lg0, lg1, lg2), (cg0, cg1, cg2)
        tabs_v = (c_ref[...], a_ref[...], b_ref[...])
        for gi, (_, dil) in enumerate(PATTERNS):
            @pl.when(p == gi)
            def _(gi=gi, dil=dil):
                _attn_group_fwd(dil, q_ref, k_ref, v_ref, tabs_v, tmp, qd, kd, vd, od, ld, ogs[gi], lgs[gi])

        @pl.when(p == 2)
        def _():
            w = _group_weights(lgs)
            o = w[0] * og0[...] + w[1] * og1[...] + w[2] * og2[...]
            silu, dsilu = _silu_and_grad(gate_ref[...])
            dy = dy_ref[...]
            dbg_ref[...] = (dy * o * dsilu).astype(BF16)
            do = dy * silu
            dwbar = jnp.sum(do * o, axis=1, keepdims=True)
            for gi in range(3):
                ogs[gi][...] = w[gi] * do
                cgs[gi][...] = -w[gi] * dwbar

        for gi, (_, dil) in enumerate(PATTERNS):
            @pl.when(p == 3 + gi)
            def _(gi=gi, dil=dil):
                nb = S // dil // BLK
                c, a, b = tabs_v
                tmp[...] = _rope(q_ref[...], c, a, b)
                _deinterleave(qd, tmp, dil, BF16)
                tmp[...] = _rope(k_ref[...], c, a, b)
                _deinterleave(kd, tmp, dil, BF16)
                _deinterleave(vd, v_ref, dil, BF16)
                _deinterleave(dod, ogs[gi], dil, BF16)
                _deinterleave(ld, lgs[gi], dil)
                _deinterleave(cd, cgs[gi], dil)
                dkd[...] = jnp.zeros_like(dkd)
                dvd[...] = jnp.zeros_like(dvd)

                def unit(u, carry):
                    o0, p0, q, s_own, s_prev = _unit_scores(u, nb, qd, kd)
                    lse = ld[pl.ds(o0, BLK), :]
                    cv = cd[pl.ds(o0, BLK), :]
                    do = dod[pl.ds(o0, BLK), :]
                    p_own = jnp.exp(s_own - lse)
                    ds_own = (p_own * (_dot(do, vd[pl.ds(o0, BLK), :], NT) + cv) * SCALE).astype(BF16)
                    dq = _dot(ds_own, kd[pl.ds(o0, BLK), :])
                    dkd[pl.ds(o0, BLK), :] += _dot(ds_own, q, TN)
                    dvd[pl.ds(o0, BLK), :] += _dot(p_own.astype(BF16), do, TN)
                    if s_prev is not None:
                        p_prev = jnp.exp(s_prev - lse)
                        ds_prev = (p_prev * (_dot(do, vd[pl.ds(p0, BLK), :], NT) + cv) * SCALE).astype(BF16)
                        dq = dq + _dot(ds_prev, kd[pl.ds(p0, BLK), :])
                        dkd[pl.ds(p0, BLK), :] += _dot(ds_prev, q, TN)
                        dvd[pl.ds(p0, BLK), :] += _dot(p_prev.astype(BF16), do, TN)
                    dqd[pl.ds(o0, BLK), :] = dq
                    return carry

                lax.fori_loop(0, S // BLK, unit, 0)
                _interleave(tmp, dqd, dil)
                dq_ref[...] = _rope_t(tmp[...], c, a, b).astype(BF16)
                _interleave(tmp, dkd, dil)
                dk_ref[...] = _rope_t(tmp[...], c, a, b).astype(BF16)
                _interleave(tmp, dvd, dil)
                dv_ref[...] = tmp[...].astype(BF16)

    tab = pl.BlockSpec((S, HEAD_DIM), lambda h, p: (0, 0))
    hspec = lambda base: pl.BlockSpec((S, HEAD_DIM), lambda h, p: (0, base + h))
    gspec = pl.BlockSpec((S, HEAD_DIM), lambda h, p: (0, jnp.maximum(p - 3, 0) * 8 + h))
    slab = lambda: pltpu.VMEM((S, HEAD_DIM), F32)
    return pl.pallas_call(
        body, grid=(8, 6),
        in_specs=[_head_spec(Q_COL), _head_spec(K_COL), _head_spec(V_COL), hspec(BG_COL), hspec(8), tab, tab, tab],
        out_specs=[gspec, gspec, gspec, hspec(0)],
        out_shape=[jax.ShapeDtypeStruct((S, 3072), BF16)] * 3 + [jax.ShapeDtypeStruct((S, HALF), BF16)],
        scratch_shapes=ATTN_SCRATCH_FWD + [slab(), slab(), slab(), pltpu.VMEM((S, HEAD_DIM), BF16),
                                           slab(), slab(), slab(), slab()],
        compiler_params=_params(("parallel", "arbitrary"), VMEM_BIG), name="attn_bwd",
    )(z0, z0, z0, z0, dycat, *tabs)


SGU_CH = 256
NCHUNK = TR // 128


def _ln_stats(x):
    mu = jnp.mean(x, axis=-1, keepdims=True)
    xc = x - mu
    r = lax.rsqrt(jnp.mean(xc * xc, axis=-1, keepdims=True) + EPS)
    return xc * r, r


def _ln_bwd(dy, xhat, r, g):
    dxh = dy * g
    return r * (dxh - jnp.mean(dxh, axis=-1, keepdims=True) - xhat * jnp.mean(dxh * xhat, axis=-1, keepdims=True))


def _tril_bf16(w):
    row = lax.broadcasted_iota(jnp.int32, w.shape, 0)
    col = lax.broadcasted_iota(jnp.int32, w.shape, 1)
    return jnp.where(row >= col, w, 0.0).astype(BF16)


def _sgu_gate(vn_s, s_s, w_ref, bb_ref):
    for h in range(4):
        wm = _tril_bf16(w_ref[h])
        bias = bb_ref[h]
        for ch in range(NCHUNK):
            rows, cols = slice(ch * 128, (ch + 1) * 128), slice(h * SGU_CH, (h + 1) * SGU_CH)
            s_s[rows, cols] = _dot(wm, vn_s[rows, cols]) + jnp.concatenate([bias, bias], axis=1)


def _conv_fwd(i, dval_ref, dglu_ref, hval_ref, hglu_ref, cw_ref, cb_ref, xw, dcs):
    halo = hval_ref[...] * _sigmoid(hglu_ref[...])
    xw[0:HALO, :] = jnp.where(i > 0, halo, 0.0)
    xw[HALO:HALO + TR, :] = dval_ref[...] * _sigmoid(dglu_ref[...])
    for rb in range(TR // SUB):
        acc = jnp.broadcast_to(cb_ref[...], (SUB, HALF))
        for k in range(CONV_K):
            acc = acc + cw_ref[k:k + 1, :] * xw[pl.ds(rb * SUB + HALO - (CONV_K - 1) + k, SUB), :]
        dcs[rb * SUB:(rb + 1) * SUB, :] = acc


def _odd_in_specs():
    col = lambda j: pl.BlockSpec((TR, HALF), lambda i, *_: (i, j))
    prev = lambda j: pl.BlockSpec((HALO, HALF), lambda i, *_: (jnp.maximum(i * (TR // HALO) - 1, 0), j))
    return [col(0), col(1), col(2), col(3), col(4), col(5), prev(3), prev(4)]


def _full_spec(shape):
    return pl.BlockSpec(shape, lambda i, *_: (0,) * len(shape))


def _odd_fwd(z1, sgu_g, sgu_b, sgu_w, sgu_bb, conv_w, conv_b, cn_g, cn_b):
    def body(u_ref, v_ref, cg_ref, dval_ref, dglu_ref, dgate_ref, hval_ref, hglu_ref,
             g_ref, b_ref, w_ref, bb_ref, cw_ref, cb_ref, cng_ref, cnb_ref, out_ref, vn_s, s_s, xw, dcs):
        i = pl.program_id(0)
        vhat, _ = _ln_stats(v_ref[...])
        vn_s[...] = (vhat * g_ref[...] + b_ref[...]).astype(BF16)
        _sgu_gate(vn_s, s_s, w_ref, bb_ref)
        cg = cg_ref[...]
        out_ref[:, 0:HALF] = (u_ref[...] * s_s[...] * (cg * _sigmoid(cg))).astype(BF16)
        _conv_fwd(i, dval_ref, dglu_ref, hval_ref, hglu_ref, cw_ref, cb_ref, xw, dcs)
        dhat, _ = _ln_stats(dcs[...])
        dn = dhat * cng_ref[...] + cnb_ref[...]
        dgate = dgate_ref[...]
        out_ref[:, HALF:2 * HALF] = ((dn * _sigmoid(dn)) * (dgate * _sigmoid(dgate))).astype(BF16)

    vec = _full_spec((1, HALF))
    return pl.pallas_call(
        body, grid=(S // TR,),
        in_specs=_odd_in_specs() + [vec, vec, _full_spec((4, 128, 128)), _full_spec((4, 128, 128)),
                                    _full_spec((HALO, HALF)), vec, vec, vec],
        out_specs=pl.BlockSpec((TR, 2048), lambda i: (i, 0)),
        out_shape=jax.ShapeDtypeStruct((S, 2048), BF16),
        scratch_shapes=[pltpu.VMEM((TR, HALF), BF16), pltpu.VMEM((TR, HALF), F32),
                        pltpu.VMEM((HALO + TR, HALF), F32), pltpu.VMEM((TR, HALF), F32)],
        compiler_params=_params(("parallel",), VMEM_BIG), name="odd_fwd",
    )(z1, z1, z1, z1, z1, z1, z1, z1, sgu_g, sgu_b, sgu_w, sgu_bb, conv_w, conv_b, cn_g, cn_b)


def _odd_bwd_a(z1, dycat, sgu_g, sgu_b, sgu_w, sgu_bb, conv_w, conv_b, cn_g, cn_b):
    def body(u_ref, v_ref, cg_ref, dval_ref, dglu_ref, dgate_ref, hval_ref, hglu_ref, dy_ref,
             g_ref, b_ref, w_ref, bb_ref, cw_ref, cb_ref, cng_ref, cnb_ref,
             dz_ref, ddc_ref, dw_ref, dbb_ref, dg_ref, db_ref, dcng_ref, dcnb_ref, dcb_ref,
             vn_s, s_s, xw, dcs, ds_s, dvn_s):
        i = pl.program_id(0)
        vhat, rv = _ln_stats(v_ref[...])
        g = g_ref[...]
        vn_s[...] = (vhat * g + b_ref[...]).astype(BF16)
        _sgu_gate(vn_s, s_s, w_ref, bb_ref)
        silu_c, dsilu_c = _silu_and_grad(cg_ref[...])
        dyc = dy_ref[:, 0:HALF]
        u = u_ref[...]
        s = s_s[...]
        dz_ref[:, 0:HALF] = (dyc * s * silu_c).astype(BF16)
        dz_ref[:, 2 * HALF:3 * HALF] = (dyc * u * s * dsilu_c).astype(BF16)
        ds_s[...] = dyc * u * silu_c

        @pl.when(i == 0)
        def _():
            dw_ref[...] = jnp.zeros_like(dw_ref)
            dbb_ref[...] = jnp.zeros_like(dbb_ref)

        tril = lax.broadcasted_iota(jnp.int32, (128, 128), 0) >= lax.broadcasted_iota(jnp.int32, (128, 128), 1)
        for h in range(4):
            wm = _tril_bf16(w_ref[h])
            for ch in range(NCHUNK):
                rows, cols = slice(ch * 128, (ch + 1) * 128), slice(h * SGU_CH, (h + 1) * SGU_CH)
                ds = ds_s[rows, cols]
                dsb = ds.astype(BF16)
                dw_ref[h] += jnp.where(tril, _dot(dsb, vn_s[rows, cols], NT), 0.0)
                dbb_ref[h] += jnp.broadcast_to(jnp.sum(ds, axis=1, keepdims=True), (128, 128))
                dvn_s[rows, cols] = _dot(wm, dsb, TN)
        dvn = dvn_s[...]
        _acc_rows(dg_ref, dvn * vhat, i)
        _acc_rows(db_ref, dvn, i)
        dz_ref[:, HALF:2 * HALF] = _ln_bwd(dvn, vhat, rv, g).astype(BF16)

        _conv_fwd(i, dval_ref, dglu_ref, hval_ref, hglu_ref, cw_ref, cb_ref, xw, dcs)
        dhat, rd = _ln_stats(dcs[...])
        cng = cng_ref[...]
        silu_n, dsilu_n = _silu_and_grad(dhat * cng + cnb_ref[...])
        silu_g, dsilu_g = _silu_and_grad(dgate_ref[...])
        dyd = dy_ref[:, HALF:2 * HALF]
        dz_ref[:, 5 * HALF:6 * HALF] = (dyd * silu_n * dsilu_g).astype(BF16)
        ddn = dyd * silu_g * dsilu_n
        _acc_rows(dcng_ref, ddn * dhat, i)
        _acc_rows(dcnb_ref, ddn, i)
        ddc = _ln_bwd(ddn, dhat, rd, cng)
        ddc_ref[...] = ddc
        _acc_rows(dcb_ref, ddc, i)

    vec = _full_spec((1, HALF))
    sq = _full_spec((4, 128, 128))
    return pl.pallas_call(
        body, grid=(S // TR,),
        in_specs=_odd_in_specs() + [pl.BlockSpec((TR, 2048), lambda i: (i, 0)),
                                    vec, vec, sq, sq, _full_spec((HALO, HALF)), vec, vec, vec],
        out_specs=[pl.BlockSpec((TR, ODD_IN), lambda i: (i, 0)), pl.BlockSpec((TR, HALF), lambda i: (i, 0)),
                   sq, sq, vec, vec, vec, vec, vec],
        out_shape=[jax.ShapeDtypeStruct((S, ODD_IN), BF16), jax.ShapeDtypeStruct((S, HALF), F32),
                   jax.ShapeDtypeStruct((4, 128, 128), F32), jax.ShapeDtypeStruct((4, 128, 128), F32)]
                  + [jax.ShapeDtypeStruct((1, HALF), F32)] * 5,
        scratch_shapes=[pltpu.VMEM((TR, HALF), BF16), pltpu.VMEM((TR, HALF), F32),
                        pltpu.VMEM((HALO + TR, HALF), F32), pltpu.VMEM((TR, HALF), F32),
                        pltpu.VMEM((TR, HALF), F32), pltpu.VMEM((TR, HALF), F32)],
        compiler_params=_params(("arbitrary",), VMEM_BIG), name="odd_bwd_a",
    )(z1, z1, z1, z1, z1, z1, z1, z1, dycat, sgu_g, sgu_b, sgu_w, sgu_bb, conv_w, conv_b, cn_g, cn_b)


def _odd_bwd_b(z1, ddc, dz1, conv_w):
    nt = S // TR

    def body(dval_ref, dglu_ref, hval_ref, hglu_ref, ddc_ref, hddc_ref, cw_ref, dz_in_ref,
             dz_ref, dcw_ref, xw, dwin, dxs):
        del dz_in_ref
        i, j = pl.program_id(0), pl.program_id(1)
        sg = _sigmoid(dglu_ref[...])
        dval = dval_ref[...]

        @pl.when(j == 0)
        def _():
            halo = hval_ref[...] * _sigmoid(hglu_ref[...])
            xw[0:HALO, :] = jnp.where(i > 0, halo, 0.0)
            xw[HALO:HALO + TR, :] = dval * sg
            dwin[0:TR, :] = ddc_ref[...]
            dwin[TR:TR + HALO, :] = jnp.where(i < nt - 1, hddc_ref[...], 0.0)

            @pl.when(i == 0)
            def _():
                dcw_ref[...] = jnp.zeros_like(dcw_ref)

            for rb in range(TR // SUB):
                acc = jnp.zeros((SUB, HALF), F32)
                for k in range(CONV_K):
                    acc = acc + cw_ref[k:k + 1, :] * dwin[pl.ds(rb * SUB + (CONV_K - 1) - k, SUB), :]
                dxs[rb * SUB:(rb + 1) * SUB, :] = acc
            for k in range(CONV_K):
                acc = jnp.zeros((SUB, HALF), F32)
                for rb in range(TR // SUB):
                    acc = acc + dwin[rb * SUB:(rb + 1) * SUB, :] * xw[pl.ds(rb * SUB + HALO - (CONV_K - 1) + k, SUB), :]
                dcw_ref[k:k + 1, :] += jnp.sum(acc, axis=0, keepdims=True)
            dz_ref[...] = (dxs[...] * sg).astype(BF16)

        @pl.when(j == 1)
        def _():
            dz_ref[...] = (dxs[...] * dval * sg * (1.0 - sg)).astype(BF16)

    col = lambda c: pl.BlockSpec((TR, HALF), lambda i, j: (i, c))
    prev = lambda c: pl.BlockSpec((HALO, HALF), lambda i, j: (jnp.maximum(i * (TR // HALO) - 1, 0), c))
    nxt = pl.BlockSpec((HALO, HALF), lambda i, j: (jnp.minimum((i + 1) * (TR // HALO), S // HALO - 1), 0))
    return pl.pallas_call(
        body, grid=(nt, 2),
        in_specs=[col(3), col(4), prev(3), prev(4), pl.BlockSpec((TR, HALF), lambda i, j: (i, 0)), nxt,
                  _full_spec((HALO, HALF)), pl.BlockSpec(memory_space=pl.ANY)],
        out_specs=[pl.BlockSpec((TR, HALF), lambda i, j: (i, 3 + j)), _full_spec((HALO, HALF))],
        out_shape=[jax.ShapeDtypeStruct((S, ODD_IN), BF16), jax.ShapeDtypeStruct((HALO, HALF), F32)],
        scratch_shapes=[pltpu.VMEM((HALO + TR, HALF), F32), pltpu.VMEM((TR + HALO, HALF), F32),
                        pltpu.VMEM((TR, HALF), F32)],
        input_output_aliases={7: 0},
        compiler_params=_params(("arbitrary", "arbitrary"), VMEM_BIG), name="odd_bwd_b",
    )(z1, z1, z1, z1, ddc, ddc, conv_w, dz1)


def _cast_bf16(w, name):
    r, c = w.shape
    tr = min(r, 256)
    def body(i_ref, o_ref):
        o_ref[...] = i_ref[...].astype(BF16)

    return pl.pallas_call(
        body, grid=(r // tr,), in_specs=[pl.BlockSpec((tr, c), lambda i: (i, 0))],
        out_specs=pl.BlockSpec((tr, c), lambda i: (i, 0)), out_shape=jax.ShapeDtypeStruct((r, c), BF16),
        compiler_params=_params(("parallel",)), name=name,
    )(w)


def _adamw(w, g, m, v):
    m = ADAM_B1 * m + (1.0 - ADAM_B1) * g
    v = ADAM_B2 * v + (1.0 - ADAM_B2) * (g * g)
    m_hat = m / (1.0 - ADAM_B1 ** ADAM_STEP)
    v_hat = v / (1.0 - ADAM_B2 ** ADAM_STEP)
    delta = -ADAM_LR * (m_hat / (jnp.sqrt(v_hat) + ADAM_EPS) + ADAM_WD * w)
    return delta, m, v


def _adam_reduce(parts, w, m, v, name, dep=None):
    r, c = w.shape
    tr = min(r, 128)
    deps = [] if dep is None else [dep]

    def body(p_ref, w_ref, m_ref, v_ref, *rest):
        g_ref, d_ref, nm_ref, nv_ref = rest[len(deps):]
        g = p_ref[0].astype(F32)
        for d in range(1, NDEV):
            g = g + p_ref[d].astype(F32)
        g_ref[...] = g
        d_ref[...], nm_ref[...], nv_ref[...] = _adamw(w_ref[...], g, m_ref[...], v_ref[...])

    spec = pl.BlockSpec((tr, c), lambda i: (i, 0))
    return pl.pallas_call(
        body, grid=(r // tr,),
        in_specs=[pl.BlockSpec((NDEV, tr, c), lambda i: (0, i, 0)), spec, spec, spec] + [ANY_SPEC] * len(deps),
        out_specs=[spec] * 4, out_shape=[jax.ShapeDtypeStruct((r, c), F32)] * 4,
        compiler_params=_params(("parallel",), VMEM_BIG), name=name,
    )(parts, w, m, v, *deps)


def _sum_parts(parts, name):
    r = parts.shape[1]
    tr = 8
    for cand in (512, 256, 128, 64, 32, 16, 8):
        if r % cand == 0:
            tr = cand
            break

    def body(p_ref, o_ref):
        g = p_ref[0]
        for d in range(1, NDEV):
            g = g + p_ref[d]
        o_ref[...] = g

    return pl.pallas_call(
        body, grid=(r // tr,), in_specs=[pl.BlockSpec((NDEV, tr, 128), lambda i: (0, i, 0))],
        out_specs=pl.BlockSpec((tr, 128), lambda i: (i, 0)), out_shape=jax.ShapeDtypeStruct((r, 128), F32),
        compiler_params=_params(("parallel",)), name=name,
    )(parts)


def _adam_plain(w, g, m, v, name):
    r, c = w.shape

    def body(w_ref, g_ref, m_ref, v_ref, d_ref, nm_ref, nv_ref):
        d_ref[...], nm_ref[...], nv_ref[...] = _adamw(w_ref[...], g_ref[...], m_ref[...], v_ref[...])

    spec = pl.BlockSpec((r, c), lambda i: (0, 0))
    return pl.pallas_call(
        body, grid=(1,), in_specs=[spec] * 4, out_specs=[spec] * 3,
        out_shape=[jax.ShapeDtypeStruct((r, c), F32)] * 3,
        compiler_params=_params(("arbitrary",)), name=name,
    )(w, g, m, v)


MASKS = [(mx, my, mc) for mx in (0, 1) for my in (0, 1) for mc in (0, 1)][1:]


def _exchange(arrays, scatter, name):
    nt = len(arrays)
    out_shape = [jax.ShapeDtypeStruct(((NDEV,) + a.shape) if not scatter else a.shape, a.dtype) for a in arrays]

    def body(*refs):
        ins, outs = refs[:nt], refs[nt:2 * nt]
        send_sems, recv_sems, local_sems = refs[2 * nt:]
        x, y, c = lax.axis_index("x"), lax.axis_index("y"), lax.axis_index("c")
        me = 4 * x + 2 * y + c
        copies = []
        for t in range(nt):
            src_own = ins[t].at[me] if scatter else ins[t]
            loc = pltpu.make_async_copy(src_own, outs[t].at[me], local_sems.at[t])
            loc.start()
            copies.append(loc)
            for k, (mx, my, mc) in enumerate(MASKS):
                px, py, pc = (x + mx) % 2, (y + my) % 2, (c + mc) % 2
                peer = 4 * px + 2 * py + pc
                src = ins[t].at[peer] if scatter else ins[t]
                rc = pltpu.make_async_remote_copy(
                    src_ref=src, dst_ref=outs[t].at[me], send_sem=send_sems.at[t, k], recv_sem=recv_sems.at[t, k],
                    device_id=(px, py, pc), device_id_type=MESH)
                rc.start()
                copies.append(rc)
        for cp in copies:
            cp.wait()

    hbm = pl.BlockSpec(memory_space=pl.ANY)
    return pl.pallas_call(
        body, in_specs=[hbm] * nt, out_specs=[hbm] * nt, out_shape=out_shape,
        scratch_shapes=[pltpu.SemaphoreType.DMA((nt, 7)), pltpu.SemaphoreType.DMA((nt, 7)),
                        pltpu.SemaphoreType.DMA((nt,))],
        name=name,
    )(*arrays)


SEM_SPEC = pl.BlockSpec(memory_space=pltpu.SEMAPHORE)
EFFECT = pltpu.SideEffectType.DATAFLOW_SIDE_EFFECTING


def _direct_plan(scatter):
    def plan(x, y, c, srcs, lands):
        me = 4 * x + 2 * y + c
        local, remote = [], []
        for src, land in zip(srcs, lands):
            local.append((src.at[me] if scatter else src, land.at[me]))
            for mx, my, mc in MASKS:
                px, py, pc = (x + mx) % 2, (y + my) % 2, (c + mc) % 2
                blk = src.at[4 * px + 2 * py + pc] if scatter else src
                remote.append((blk, land.at[me], (px, py, pc)))
        return local, remote
    return plan


def _split_start(name, srcs, land_shapes, plan, n_local, n_remote, dep=None):
    ns, nl = len(srcs), len(land_shapes)
    deps = [] if dep is None else [dep]
    lands = [lax.empty(s.shape, s.dtype) for s in land_shapes]

    def body(*refs):
        ins, lz = refs[:ns], refs[ns:ns + nl]
        outs = refs[ns + nl + len(deps):]
        send_sems, recv_sems, token, local_sems = outs[0], outs[1], outs[2 + ns + nl], outs[3 + ns + nl]
        local, remote = plan(lax.axis_index("x"), lax.axis_index("y"), lax.axis_index("c"), ins, lz)
        own = [pltpu.make_async_copy(src, dst, local_sems.at[i]) for i, (src, dst) in enumerate(local)]
        for cp in own:
            cp.start()
        for cp in own:
            cp.wait()
        for k, (src, dst, peer) in enumerate(remote):
            pltpu.make_async_remote_copy(src_ref=src, dst_ref=dst, send_sem=send_sems.at[k], recv_sem=recv_sems.at[k],
                                         device_id=peer, device_id_type=MESH).start()
        token[...] = jnp.zeros_like(token)

    hbm = lambda a: pltpu.HBM(a.shape, a.dtype)
    outs = pl.pallas_call(
        body, name=name,
        out_shape=(pltpu.SemaphoreType.DMA((n_remote,)), pltpu.SemaphoreType.DMA((n_remote,)),
                   *[hbm(a) for a in srcs], *[hbm(a) for a in lands], jax.ShapeDtypeStruct((8, 128), F32)),
        in_specs=[ANY_SPEC] * (ns + nl + len(deps)),
        out_specs=(SEM_SPEC, SEM_SPEC, *[ANY_SPEC] * (ns + nl), pl.BlockSpec(memory_space=pltpu.VMEM)),
        scratch_shapes=[pltpu.SemaphoreType.DMA((n_local,))],
        input_output_aliases={i: 2 + i for i in range(ns + nl)},
        compiler_params=pltpu.CompilerParams(has_side_effects=EFFECT),
    )(*[pltpu.with_memory_space_constraint(a, pltpu.HBM) for a in srcs],
      *[pltpu.with_memory_space_constraint(a, pltpu.HBM) for a in lands], *deps)
    return dict(sems=outs[:2], srcs=outs[2:2 + ns], lands=outs[2 + ns:2 + ns + nl], token=outs[-1],
                plan=plan, n_remote=n_remote)


def _split_wait(name, handle, after):
    srcs, lands, plan = handle["srcs"], handle["lands"], handle["plan"]
    ns, nl = len(srcs), len(lands)

    def body(*refs):
        ins, lz = refs[:ns], refs[ns:ns + nl]
        send_sems, recv_sems = refs[ns + nl], refs[ns + nl + 1]
        _, remote = plan(lax.axis_index("x"), lax.axis_index("y"), lax.axis_index("c"), ins, lz)
        for k, (src, dst, peer) in enumerate(remote):
            cp = pltpu.make_async_remote_copy(src_ref=src, dst_ref=dst, send_sem=send_sems.at[k],
                                              recv_sem=recv_sems.at[k], device_id=peer, device_id_type=MESH)
            cp.wait_send()
            cp.wait_recv()

    hbm = lambda a: pltpu.HBM(a.shape, a.dtype)
    outs = pl.pallas_call(
        body, name=name, out_shape=(*[hbm(a) for a in srcs], *[hbm(a) for a in lands]),
        in_specs=[ANY_SPEC] * (ns + nl) + [SEM_SPEC, SEM_SPEC, ANY_SPEC], out_specs=tuple([ANY_SPEC] * (ns + nl)),
        input_output_aliases={i: i for i in range(ns + nl)},
        compiler_params=pltpu.CompilerParams(has_side_effects=EFFECT),
    )(*srcs, *lands, *handle["sems"], after)
    return list(outs[ns:])


def _sc_exchange(name, collective_id, arrays, scatter):
    nt = len(arrays)
    out_type = [jax.ShapeDtypeStruct(a.shape if scatter else (NDEV,) + a.shape, a.dtype) for a in arrays]

    def body(*refs):
        ins, outs = refs[:nt], refs[nt:2 * nt]
        send_sems, recv_sems, local_sems = refs[2 * nt:3 * nt], refs[3 * nt:4 * nt], refs[4 * nt:5 * nt]
        x, y, c = lax.axis_index("x"), lax.axis_index("y"), lax.axis_index("c")
        peers = [(mx + x - 2 * mx * x, my + y - 2 * my * y, mc + c - 2 * mc * c) for mx, my, mc in MASKS]
        barrier = pltpu.get_barrier_semaphore()
        for peer in peers:
            pl.semaphore_signal(barrier, inc=1, device_id=peer, device_id_type=MESH)
        pl.semaphore_wait(barrier, len(peers))
        me = 4 * x + 2 * y + c
        own = []
        for t in range(nt):
            cp = pltpu.make_async_copy(ins[t].at[me] if scatter else ins[t], outs[t].at[me], local_sems[t])
            cp.start()
            own.append(cp)
            for px, py, pc in peers:
                src = ins[t].at[4 * px + 2 * py + pc] if scatter else ins[t]
                pltpu.make_async_remote_copy(src_ref=src, dst_ref=outs[t].at[me], send_sem=send_sems[t],
                                             recv_sem=recv_sems[t], device_id=(px, py, pc), device_id_type=MESH).start()
        for t in range(nt):
            own[t].wait()
            seven = outs[t].at[pl.ds(0, NDEV - 1)]
            drain = pltpu.make_async_remote_copy(src_ref=seven, dst_ref=seven, send_sem=send_sems[t],
                                                 recv_sem=recv_sems[t], device_id=(x, y, c), device_id_type=MESH)
            drain.wait_send()
            drain.wait_recv()

    return pl.kernel(
        body, out_type=out_type, mesh=plsc.ScalarSubcoreMesh(axis_name="sequencer", num_cores=1),
        scratch_types=[pltpu.SemaphoreType.DMA] * (3 * nt),
        compiler_params=pltpu.CompilerParams(collective_id=collective_id), name=name,
    )(*arrays)


SMALL = {
    "e_pre_norm": ((2048,), None), "e_pool_w": ((4, 256, 256), 1), "e_pool_scale": ((1024,), None),
    "e_post_norm": ((2048,), None), "o_pre_norm": ((2048,), 0), "o_sgu_norm_g": ((1024,), 0),
    "o_sgu_norm_b": ((1024,), 0), "o_sgu_w": ((4, 128, 128), None), "o_sgu_b": ((4, 128), None),
    "o_conv_w": ((31, 1024), 1), "o_conv_b": ((1024,), 0), "o_conv_norm_g": ((1024,), 0),
    "o_conv_norm_b": ((1024,), 0), "o_post_norm": ((2048,), 0),
}
SMALL_SHARDED = [n for n, (_, ax) in SMALL.items() if ax is not None]


def _shard_shape(name):
    shape, ax = SMALL[name]
    if ax is None:
        return shape
    return tuple(s // NDEV if i == ax else s for i, s in enumerate(shape))


def _pack(arrs, row_multiple=1):
    flat = jnp.concatenate([a.reshape(-1) for a in arrs])
    pad = -flat.shape[0] % (128 * row_multiple)
    return jnp.concatenate([flat, jnp.zeros((pad,), F32)]).reshape(-1, 128)


def _unpack(buf, shapes):
    flat = buf.reshape(-1)
    out, off = [], 0
    for shp in shapes:
        n = int(np.prod(shp))
        out.append(flat[off:off + n].reshape(shp))
        off += n
    return out


def _take_shard(full, name, me):
    shape, ax = SMALL[name]
    if ax is None:
        return full
    n = shape[ax] // NDEV
    return lax.dynamic_slice_in_dim(full, me * n, n, axis=ax)


BIG = ("e_w_in", "e_w_out", "o_w_in", "o_w_out")
WEIGHTS = ["e_pre_norm", "e_w_in", "e_pool_w", "e_pool_scale", "e_w_out", "e_post_norm", "o_pre_norm", "o_w_in",
           "o_sgu_norm_g", "o_sgu_norm_b", "o_sgu_w", "o_sgu_b", "o_conv_w", "o_conv_b", "o_conv_norm_g",
           "o_conv_norm_b", "o_w_out", "o_post_norm"]


def kernel(x, e_pre_norm, e_w_in, e_pool_w, e_pool_scale, e_w_out, e_post_norm, o_pre_norm, o_w_in, o_sgu_norm_g, o_sgu_norm_b, o_sgu_w, o_sgu_b, o_conv_w, o_conv_b, o_conv_norm_g, o_conv_norm_b, o_w_out, o_post_norm, loss_target, m_e_pre_norm, m_e_w_in, m_e_pool_w, m_e_pool_scale, m_e_w_out, m_e_post_norm, m_o_pre_norm, m_o_w_in, m_o_sgu_norm_g, m_o_sgu_norm_b, m_o_sgu_w, m_o_sgu_b, m_o_conv_w, m_o_conv_b, m_o_conv_norm_g, m_o_conv_norm_b, m_o_w_out, m_o_post_norm, v_e_pre_norm, v_e_w_in, v_e_pool_w, v_e_pool_scale, v_e_w_out, v_e_post_norm, v_o_pre_norm, v_o_w_in, v_o_sgu_norm_g, v_o_sgu_norm_b, v_o_sgu_w, v_o_sgu_b, v_o_conv_w, v_o_conv_b, v_o_conv_norm_g, v_o_conv_norm_b, v_o_w_out, v_o_post_norm):
    given = dict(locals())
    w = {n: given[n][0] for n in WEIGHTS}
    m = {n: given["m_" + n][0] for n in WEIGHTS}
    v = {n: given["v_" + n][0] for n in WEIGHTS}
    me = 4 * lax.axis_index("x") + 2 * lax.axis_index("y") + lax.axis_index("c")
    x, target = x[0], loss_target[0]
    row = lambda a: a.reshape(1, -1)

    bf = {n: _cast_bf16(w[n], "cast_" + n) for n in BIG}
    wg_e_in, small_rows = _sc_exchange("gather_a", 0, [bf["e_w_in"], _pack([w[n] for n in SMALL_SHARDED])], False)
    wg_e_out, wg_o_in, wg_o_out = _sc_exchange("gather_b", 1, [bf["e_w_out"], bf["o_w_in"], bf["o_w_out"]], False)
    h0 = _pre0_fwd(x, row(w["e_pre_norm"]))
    p = {n: w[n] for n in SMALL if SMALL[n][1] is None}
    small_rows = small_rows.reshape(NDEV, -1)
    off = 0
    for n in SMALL_SHARDED:
        shp, ax = _shard_shape(n), SMALL[n][1]
        cnt = int(np.prod(shp))
        blk = small_rows[:, off:off + cnt].reshape((NDEV,) + shp)
        p[n] = jnp.moveaxis(blk, 0, ax).reshape(SMALL[n][0])
        off += cnt
    tabs = _rope_tables()
    pool_w_bf = p["e_pool_w"].astype(BF16)
    sgu_bb = jnp.broadcast_to(p["o_sgu_b"][:, :, None], (4, 128, 128))
    conv_w = jnp.concatenate([p["o_conv_w"], jnp.zeros((HALO - CONV_K, HALF), F32)], axis=0)
    odd_p = (row(p["o_sgu_norm_g"]), row(p["o_sgu_norm_b"]), p["o_sgu_w"], sgu_bb, conv_w,
             row(p["o_conv_b"]), row(p["o_conv_norm_g"]), row(p["o_conv_norm_b"]))

    z0 = _mm_in(h0, wg_e_in, "mm_z0")
    ycat0 = _pool_fwd(z0, pool_w_bf, row(p["e_pool_scale"]))
    ycat0 = _attn_fwd(z0, ycat0, tabs)
    w_out_e, w_out_o = wg_e_out.reshape(2048, D), wg_o_out.reshape(2048, D)
    y0 = _mm_out(ycat0, w_out_e, "mm_y0")
    x1, h1 = _post0_fwd(x, y0, row(p["e_post_norm"]), row(p["o_pre_norm"]))
    z1 = _mm_in(h1, wg_o_in, "mm_z1")
    ycat1 = _odd_fwd(z1, *odd_p)
    y1 = _mm_out(ycat1, w_out_o, "mm_y1")

    g = {}
    loss, dx2, dy1, g["o_post_norm"] = _post1_bwd(y1, x1, target, row(p["o_post_norm"]))
    loss = lax.psum(loss[0, 0], ("x", "y", "c"))
    parts = {}
    dw = _mm_out_dw(ycat1, dy1, "mm_dwout1").reshape(NDEV, 256, D)
    parts["o_w_out"], = _sc_exchange("scatter_o_w_out", 2, [dw], True)
    dycat1 = _mm_out_dx(dy1, w_out_o, "mm_dycat1", dw)
    dz1, ddc, g["o_sgu_w"], d_sgu_bb, g["o_sgu_norm_g"], g["o_sgu_norm_b"], g["o_conv_norm_g"], \
        g["o_conv_norm_b"], g["o_conv_b"] = _odd_bwd_a(z1, dycat1, *odd_p)
    dz1, d_conv_w = _odd_bwd_b(z1, ddc, dz1, conv_w)
    g["o_sgu_b"] = d_sgu_bb[:, :, 0]
    g["o_conv_w"] = d_conv_w[:CONV_K]
    grads, deltas, new_m, new_v = {}, {}, {}, {}

    def adam(n, dep):
        grads[n], deltas[n], new_m[n], new_v[n] = _adam_reduce(parts[n], w[n], m[n], v[n], "adam_" + n, dep)
        return new_v[n]

    pin = adam("o_w_out", d_conv_w)
    dw = _mm_in_dw(h1, dz1, ODD_IN // NDEV, "mm_dwin1", pin)
    parts["o_w_in"], = _sc_exchange("scatter_o_w_in", 3, [dw], True)
    dh1 = _mm_in_dx(dz1, wg_o_in, "mm_dh1", dw)
    dx1, dy0, g["o_pre_norm"], g["e_post_norm"] = _mid_bwd(dx2, dh1, x1, y0, row(p["o_pre_norm"]),
                                                           row(p["e_post_norm"]))
    dw = _mm_out_dw(ycat0, dy0, "mm_dwout0").reshape(NDEV, 256, D)
    parts["e_w_out"], = _sc_exchange("scatter_e_w_out", 4, [dw], True)
    dycat0 = _mm_out_dx(dy0, w_out_e, "mm_dycat0", dw)
    da_in, da_gate, g["e_pool_w"], g["e_pool_scale"] = _pool_bwd(z0, dycat0, pool_w_bf, row(p["e_pool_scale"]))
    dq, dk, dv, dbg = _attn_bwd(z0, dycat0, tabs)
    dz0 = jnp.concatenate([da_in, da_gate, dq, dk, dv, dbg], axis=1)
    late = [n for n in SMALL if n != "e_pre_norm"]
    pin = adam("e_w_out", adam("o_w_in", dbg))
    dw = _mm_in_dw(h0, dz0, EVEN_IN // NDEV, "mm_dwin0", pin)
    parts["e_w_in"], = _sc_exchange("scatter_e_w_in", 5, [dw], True)
    recv_small, = _sc_exchange("gather_small_grads", 6, [_pack([g[n].reshape(SMALL[n][0]) for n in late], 512)], False)
    dh0 = _mm_in_dx(dz0, wg_e_in, "mm_dh0", dw)
    grad_x, g["e_pre_norm"] = _pre0_bwd(dx1, dh0, x, row(p["e_pre_norm"]))
    last, = _sc_exchange("gather_e_pre_norm_grad", 7, [g["e_pre_norm"].reshape(16, 128)], False)

    adam("e_w_in", None)
    g_small = dict(zip(late, _unpack(_sum_parts(recv_small, "sum_small_grads"), [SMALL[n][0] for n in late])))
    g_small["e_pre_norm"] = _sum_parts(last, "sum_e_pre_norm_grad").reshape(2048)
    for n in SMALL:
        grads[n] = _take_shard(g_small[n], n, me)
    names = list(SMALL)
    shapes = [_shard_shape(n) for n in names]
    d_pack, m_pack, v_pack = _adam_plain(_pack([w[n] for n in names]), _pack([grads[n] for n in names]),
                                         _pack([m[n] for n in names]), _pack([v[n] for n in names]), "adam_small")
    for n, d_, m_, v_ in zip(names, _unpack(d_pack, shapes), _unpack(m_pack, shapes), _unpack(v_pack, shapes)):
        deltas[n], new_m[n], new_v[n] = d_, m_, v_

    lead = lambda a: a[None]
    return (loss, grad_x[None], *[lead(grads[n]) for n in WEIGHTS], *[lead(deltas[n]) for n in WEIGHTS],
            *[lead(new_m[n]) for n in WEIGHTS], *[lead(new_v[n]) for n in WEIGHTS])
```

```python
import functools

import numpy as np
import jax
import jax.numpy as jnp
from jax import lax
from jax.experimental import pallas as pl
from jax.experimental.pallas import tpu as pltpu
from jax.experimental.pallas import tpu_sc as plsc

F32 = jnp.float32
BF16 = jnp.bfloat16

S = 2048
D = 2048
NDEV = 8
EPS = 1e-6
NEG = -1e30
HEAD_DIM = 128
ROT_DIM = 32
ROPE_THETA = 500000.0
PATTERNS = ((128, 1), (512, 4), (2048, 16))
BLK = 128
EVEN_IN = 12288
ODD_IN = 6144
HALF = 1024
CONV_K = 31
HALO = 32
TR = 256
SUB = 32

ADAM_LR = 0.001
ADAM_B1 = 0.9
ADAM_B2 = 0.999
ADAM_EPS = 1e-08
ADAM_WD = 0.01
ADAM_STEP = 10

VMEM_BIG = 56 * 1024 * 1024
MESH = pl.DeviceIdType.MESH

NN = (((1,), (0,)), ((), ()))
NT = (((1,), (1,)), ((), ()))
TN = (((0,), (0,)), ((), ()))


def _dot(a, b, dn=NN):
    return lax.dot_general(a, b, dn, preferred_element_type=F32)


def _sigmoid(x):
    return 1.0 / (1.0 + jnp.exp(-x))


def _silu_and_grad(x):
    sg = _sigmoid(x)
    return x * sg, sg * (1.0 + x * (1.0 - sg))


def _params(sem, vmem=None):
    return pltpu.CompilerParams(dimension_semantics=sem, vmem_limit_bytes=vmem)


ANY_SPEC = pl.BlockSpec(memory_space=pl.ANY)


def _matmul(a, b, *, dn, grid, a_spec, b_spec, o_spec, out_shape, out_dtype, acc_shape, name, dep=None):
    nk = grid[2]
    deps = [] if dep is None else [dep]

    def body(a_ref, b_ref, *rest):
        o_ref, acc = rest[len(deps)], rest[len(deps) + 1:]
        if nk == 1:
            o_ref[...] = _dot(a_ref[...], b_ref[...], dn).astype(o_ref.dtype)
            return
        acc_ref = acc[0]
        k = pl.program_id(2)

        @pl.when(k == 0)
        def _():
            acc_ref[...] = jnp.zeros_like(acc_ref)

        acc_ref[...] += _dot(a_ref[...], b_ref[...], dn)

        @pl.when(k == nk - 1)
        def _():
            o_ref[...] = acc_ref[...].astype(o_ref.dtype)

    return pl.pallas_call(
        body, grid=grid, in_specs=[a_spec, b_spec] + [ANY_SPEC] * len(deps), out_specs=o_spec,
        out_shape=jax.ShapeDtypeStruct(out_shape, out_dtype),
        scratch_shapes=[] if nk == 1 else [pltpu.VMEM(acc_shape, F32)],
        compiler_params=_params(("parallel", "parallel", "arbitrary"), VMEM_BIG), name=name,
    )(a, b, *deps)


TM = 512


def _mm_in(h, wg, name):
    nb = wg.shape[2]
    tn = 512 if nb % 512 == 0 else nb
    per = nb // tn
    return _matmul(
        h, wg, dn=NN, grid=(S // TM, NDEV * per, 1),
        a_spec=pl.BlockSpec((TM, D), lambda i, j, k: (i, 0)),
        b_spec=pl.BlockSpec((None, D, tn), lambda i, j, k: (j // per, 0, j % per)),
        o_spec=pl.BlockSpec((TM, tn), lambda i, j, k: (i, j)),
        out_shape=(S, NDEV * nb), out_dtype=F32, acc_shape=(TM, tn), name=name)


def _mm_in_dx(dz, wg, name, dep=None):
    nb = wg.shape[2]
    return _matmul(
        dz, wg, dn=NT, grid=(S // TM, D // 512, NDEV),
        a_spec=pl.BlockSpec((TM, nb), lambda i, j, k: (i, k)),
        b_spec=pl.BlockSpec((None, 512, nb), lambda i, j, k: (k, j, 0)),
        o_spec=pl.BlockSpec((TM, 512), lambda i, j, k: (i, j)),
        out_shape=(S, D), out_dtype=F32, acc_shape=(TM, 512), name=name, dep=dep)


def _mm_in_dw(h, dz, nb, name, dep=None):
    tn = 512 if nb % 512 == 0 else nb
    per = nb // tn
    return _matmul(
        h, dz, dn=TN, grid=(D // TM, NDEV * per, 1),
        a_spec=pl.BlockSpec((S, TM), lambda i, j, k: (0, i)),
        b_spec=pl.BlockSpec((S, tn), lambda i, j, k: (0, j)),
        o_spec=pl.BlockSpec((None, TM, tn), lambda i, j, k: (j // per, i, j % per)),
        out_shape=(NDEV, D, nb), out_dtype=BF16, acc_shape=(TM, tn), name=name, dep=dep)


def _mm_out(yc, w, name):
    return _matmul(
        yc, w, dn=NN, grid=(S // TM, D // 512, 1),
        a_spec=pl.BlockSpec((TM, 2048), lambda i, j, k: (i, 0)),
        b_spec=pl.BlockSpec((2048, 512), lambda i, j, k: (0, j)),
        o_spec=pl.BlockSpec((TM, 512), lambda i, j, k: (i, j)),
        out_shape=(S, D), out_dtype=F32, acc_shape=(TM, 512), name=name)


def _mm_out_dx(dy, w, name, dep=None):
    return _matmul(
        dy, w, dn=NT, grid=(S // TM, 2048 // 512, 1),
        a_spec=pl.BlockSpec((TM, D), lambda i, j, k: (i, 0)),
        b_spec=pl.BlockSpec((512, D), lambda i, j, k: (j, 0)),
        o_spec=pl.BlockSpec((TM, 512), lambda i, j, k: (i, j)),
        out_shape=(S, 2048), out_dtype=F32, acc_shape=(TM, 512), name=name, dep=dep)


def _mm_out_dw(yc, dy, name):
    return _matmul(
        yc, dy, dn=TN, grid=(2048 // TM, D // 512, 1),
        a_spec=pl.BlockSpec((S, TM), lambda i, j, k: (0, i)),
        b_spec=pl.BlockSpec((S, 512), lambda i, j, k: (0, j)),
        o_spec=pl.BlockSpec((TM, 512), lambda i, j, k: (i, j)),
        out_shape=(2048, D), out_dtype=BF16, acc_shape=(TM, 512), name=name)


def _row_spec(w=D):
    return pl.BlockSpec((TR, w), lambda i: (i, 0))


def _vec_spec(w=D):
    return pl.BlockSpec((1, w), lambda i: (0, 0))


def _rms_stats(x):
    r = lax.rsqrt(jnp.mean(x * x, axis=-1, keepdims=True) + EPS)
    return x * r, r


def _rms_bwd(dn, xhat, r, g):
    dxh = dn * g
    return r * (dxh - xhat * jnp.mean(dxh * xhat, axis=-1, keepdims=True))


def _acc_rows(ref, val, i):
    s = jnp.sum(val, axis=0, keepdims=True)

    @pl.when(i == 0)
    def _():
        ref[...] = s

    @pl.when(i > 0)
    def _():
        ref[...] += s


def _pre0_fwd(x, g, dep=None):
    deps = [] if dep is None else [dep]

    def body(x_ref, g_ref, *rest):
        xhat, _ = _rms_stats(x_ref[...])
        rest[-1][...] = (xhat * g_ref[...]).astype(BF16)

    return pl.pallas_call(
        body, grid=(S // TR,), in_specs=[_row_spec(), _vec_spec()] + [ANY_SPEC] * len(deps), out_specs=_row_spec(),
        out_shape=jax.ShapeDtypeStruct((S, D), BF16), compiler_params=_params(("parallel",)), name="pre0_fwd",
    )(x, g, *deps)


def _post0_fwd(x, y0, g_post, g_pre1):
    def body(x_ref, y_ref, gp_ref, g1_ref, x1_ref, h1_ref):
        yhat, _ = _rms_stats(y_ref[...])
        x1 = x_ref[...] + yhat * gp_ref[...]
        x1_ref[...] = x1
        xhat, _ = _rms_stats(x1)
        h1_ref[...] = (xhat * g1_ref[...]).astype(BF16)

    return pl.pallas_call(
        body, grid=(S // TR,), in_specs=[_row_spec(), _row_spec(), _vec_spec(), _vec_spec()],
        out_specs=[_row_spec(), _row_spec()],
        out_shape=[jax.ShapeDtypeStruct((S, D), F32), jax.ShapeDtypeStruct((S, D), BF16)],
        compiler_params=_params(("parallel",)), name="post0_fwd",
    )(x, y0, g_post, g_pre1)


def _post1_bwd(y1, x1, target, g_post):
    def body(y_ref, x1_ref, t_ref, g_ref, loss_ref, dx2_ref, dy_ref, dg_ref):
        i = pl.program_id(0)
        yhat, r = _rms_stats(y_ref[...])
        g = g_ref[...]
        err = x1_ref[...] + yhat * g - t_ref[...]
        part = jnp.sum(jnp.sum(err * err, axis=-1, keepdims=True), axis=0, keepdims=True) * (0.5 / D)
        _acc_rows(loss_ref, jnp.broadcast_to(part, (1, 128)), i)
        dx2 = err * (1.0 / D)
        dx2_ref[...] = dx2
        _acc_rows(dg_ref, dx2 * yhat, i)
        dy_ref[...] = _rms_bwd(dx2, yhat, r, g).astype(BF16)

    return pl.pallas_call(
        body, grid=(S // TR,), in_specs=[_row_spec(), _row_spec(), _row_spec(), _vec_spec()],
        out_specs=[_vec_spec(128), _row_spec(), _row_spec(), _vec_spec()],
        out_shape=[jax.ShapeDtypeStruct((1, 128), F32), jax.ShapeDtypeStruct((S, D), F32),
                   jax.ShapeDtypeStruct((S, D), BF16), jax.ShapeDtypeStruct((1, D), F32)],
        compiler_params=_params(("arbitrary",)), name="post1_bwd",
    )(y1, x1, target, g_post)


def _mid_bwd(dx2, dh1, x1, y0, g_pre1, g_post0):
    def body(dx2_ref, dh_ref, x1_ref, y_ref, g1_ref, gp_ref, dx1_ref, dy_ref, dg1_ref, dgp_ref):
        i = pl.program_id(0)
        xhat, r1 = _rms_stats(x1_ref[...])
        dh = dh_ref[...]
        _acc_rows(dg1_ref, dh * xhat, i)
        dx1 = dx2_ref[...] + _rms_bwd(dh, xhat, r1, g1_ref[...])
        dx1_ref[...] = dx1
        yhat, r0 = _rms_stats(y_ref[...])
        _acc_rows(dgp_ref, dx1 * yhat, i)
        dy_ref[...] = _rms_bwd(dx1, yhat, r0, gp_ref[...]).astype(BF16)

    return pl.pallas_call(
        body, grid=(S // TR,),
        in_specs=[_row_spec(), _row_spec(), _row_spec(), _row_spec(), _vec_spec(), _vec_spec()],
        out_specs=[_row_spec(), _row_spec(), _vec_spec(), _vec_spec()],
        out_shape=[jax.ShapeDtypeStruct((S, D), F32), jax.ShapeDtypeStruct((S, D), BF16),
                   jax.ShapeDtypeStruct((1, D), F32), jax.ShapeDtypeStruct((1, D), F32)],
        compiler_params=_params(("arbitrary",)), name="mid_bwd",
    )(dx2, dh1, x1, y0, g_pre1, g_post0)


def _pre0_bwd(dx1, dh0, x, g):
    def body(dx1_ref, dh_ref, x_ref, g_ref, gx_ref, dg_ref):
        i = pl.program_id(0)
        xhat, r = _rms_stats(x_ref[...])
        dh = dh_ref[...]
        _acc_rows(dg_ref, dh * xhat, i)
        gx_ref[...] = dx1_ref[...] + _rms_bwd(dh, xhat, r, g_ref[...])

    return pl.pallas_call(
        body, grid=(S // TR,), in_specs=[_row_spec(), _row_spec(), _row_spec(), _vec_spec()],
        out_specs=[_row_spec(), _vec_spec()],
        out_shape=[jax.ShapeDtypeStruct((S, D), F32), jax.ShapeDtypeStruct((1, D), F32)],
        compiler_params=_params(("arbitrary",)), name="pre0_bwd",
    )(dx1, dh0, x, g)


POOL_CH = 256


def _pool_apply(a, w, transpose):
    n = a.shape[0]
    row = lax.broadcasted_iota(jnp.int32, a.shape, 0)
    cnt = jnp.minimum(row + 1, w).astype(F32)
    s = a / cnt if transpose else a
    for k in (1, 2, 4, 8):
        if transpose:
            sh = jnp.where(row < n - k, pltpu.roll(s, n - k, 0), 0.0)
        else:
            sh = jnp.where(row >= k, pltpu.roll(s, k, 0), 0.0)
        s = jnp.where(w > k, s + sh, s)
    return s - a if transpose else s / cnt - a


def _pool_fwd(z0, pool_w, pool_scale):
    def body(a_ref, gate_ref, w_ref, sc_ref, out_ref):
        win = jnp.left_shift(2, pl.program_id(0))
        pooled = _pool_apply(a_ref[...], win, False)
        mixed = _dot(pooled.astype(BF16), w_ref[...])
        gate = gate_ref[...]
        out_ref[...] = (mixed * sc_ref[...] * (gate * _sigmoid(gate))).astype(BF16)

    return pl.pallas_call(
        body, grid=(4,),
        in_specs=[pl.BlockSpec((S, POOL_CH), lambda g: (0, g)), pl.BlockSpec((S, POOL_CH), lambda g: (0, 4 + g)),
                  pl.BlockSpec((None, POOL_CH, POOL_CH), lambda g: (g, 0, 0)),
                  pl.BlockSpec((1, POOL_CH), lambda g: (0, g))],
        out_specs=pl.BlockSpec((S, POOL_CH), lambda g: (0, g)),
        out_shape=jax.ShapeDtypeStruct((S, 2048), BF16),
        compiler_params=_params(("parallel",), VMEM_BIG), name="pool_fwd",
    )(z0, z0, pool_w, pool_scale)


def _pool_bwd(z0, dycat, pool_w, pool_scale):
    def body(a_ref, gate_ref, dy_ref, w_ref, sc_ref, da_ref, dgate_ref, dw_ref, dsc_ref):
        win = jnp.left_shift(2, pl.program_id(0))
        pooled = _pool_apply(a_ref[...], win, False).astype(BF16)
        w = w_ref[...]
        mixed = _dot(pooled, w)
        silu, dsilu = _silu_and_grad(gate_ref[...])
        dy = dy_ref[...]
        sc = sc_ref[...]
        dgate_ref[...] = (dy * (mixed * sc) * dsilu).astype(BF16)
        dms = dy * silu
        dsc_ref[...] = jnp.sum(dms * mixed, axis=0, keepdims=True)
        dmixed = (dms * sc).astype(BF16)
        dw_ref[...] = _dot(pooled, dmixed, TN)
        dpooled = _dot(dmixed, w, NT)
        da_ref[...] = _pool_apply(dpooled, win, True).astype(BF16)

    slab = lambda off: pl.BlockSpec((S, POOL_CH), lambda g: (0, off + g))
    return pl.pallas_call(
        body, grid=(4,),
        in_specs=[slab(0), slab(4), slab(0), pl.BlockSpec((None, POOL_CH, POOL_CH), lambda g: (g, 0, 0)),
                  pl.BlockSpec((1, POOL_CH), lambda g: (0, g))],
        out_specs=[slab(0), slab(0), pl.BlockSpec((None, POOL_CH, POOL_CH), lambda g: (g, 0, 0)),
                   pl.BlockSpec((1, POOL_CH), lambda g: (0, g))],
        out_shape=[jax.ShapeDtypeStruct((S, HALF), BF16), jax.ShapeDtypeStruct((S, HALF), BF16),
                   jax.ShapeDtypeStruct((4, POOL_CH, POOL_CH), F32), jax.ShapeDtypeStruct((1, HALF), F32)],
        compiler_params=_params(("parallel",), VMEM_BIG), name="pool_bwd",
    )(z0, z0, dycat, pool_w, pool_scale)


Q_COL, K_COL, V_COL, BG_COL = 2048 // 128, 5120 // 128, 8192 // 128, 11264 // 128
SCALE = HEAD_DIM ** -0.5


def _rope_tables():
    pos = jnp.arange(S, dtype=F32)
    inv_freq = jnp.power(ROPE_THETA, -jnp.arange(0, ROT_DIM, 2, dtype=F32) / ROT_DIM)
    ang = pos[:, None] * inv_freq[None, :]
    cos, sin = jnp.cos(ang), jnp.sin(ang)
    half = ROT_DIM // 2
    zeros = jnp.zeros((S, HEAD_DIM - ROT_DIM), F32)
    c = jnp.concatenate([cos, cos, jnp.ones((S, HEAD_DIM - ROT_DIM), F32)], axis=1)
    a = jnp.concatenate([-sin, jnp.zeros((S, half), F32), zeros], axis=1)
    b = jnp.concatenate([jnp.zeros((S, half), F32), sin, zeros], axis=1)
    return c, a, b


def _rope(t, c, a, b):
    half = ROT_DIM // 2
    return t * c + pltpu.roll(t, HEAD_DIM - half, 1) * a + pltpu.roll(t, half, 1) * b


def _rope_t(d, c, a, b):
    half = ROT_DIM // 2
    return d * c + pltpu.roll(d * a, half, 1) + pltpu.roll(d * b, HEAD_DIM - half, 1)


def _deinterleave(dst, src, dil, cast=None):
    length = S // dil
    for r in range(dil):
        v = src[...] if dil == 1 else src[pl.ds(r, length, stride=dil), :]
        dst[r * length:(r + 1) * length, :] = v if cast is None else v.astype(cast)


def _interleave(dst, src, dil):
    length = S // dil
    for r in range(dil):
        if dil == 1:
            dst[...] = src[...]
        else:
            dst[pl.ds(r, length, stride=dil), :] = src[r * length:(r + 1) * length, :]


def _unit_scores(u, nb, qd, kd):
    o0 = pl.multiple_of(u * BLK, BLK)
    p0 = pl.multiple_of(jnp.maximum(u - 1, 0) * BLK, BLK)
    q = qd[pl.ds(o0, BLK), :]
    row = lax.broadcasted_iota(jnp.int32, (BLK, BLK), 0)
    col = lax.broadcasted_iota(jnp.int32, (BLK, BLK), 1)
    s_own = jnp.where(col <= row, _dot(q, kd[pl.ds(o0, BLK), :], NT) * SCALE, NEG)
    if nb == 1:
        return o0, p0, q, s_own, None
    has_prev = (u % nb) != 0
    s_prev = jnp.where((col >= row) & has_prev, _dot(q, kd[pl.ds(p0, BLK), :], NT) * SCALE, NEG)
    return o0, p0, q, s_own, s_prev


def _attn_group_fwd(dil, q_ref, k_ref, v_ref, tabs, tmp, qd, kd, vd, od, ld, og, lg):
    nb = S // dil // BLK
    c, a, b = tabs
    tmp[...] = _rope(q_ref[...], c, a, b)
    _deinterleave(qd, tmp, dil, BF16)
    tmp[...] = _rope(k_ref[...], c, a, b)
    _deinterleave(kd, tmp, dil, BF16)
    _deinterleave(vd, v_ref, dil, BF16)

    def unit(u, carry):
        o0, p0, _, s_own, s_prev = _unit_scores(u, nb, qd, kd)
        m = jnp.max(s_own, axis=1, keepdims=True)
        if s_prev is not None:
            m = jnp.maximum(m, jnp.max(s_prev, axis=1, keepdims=True))
        p_own = jnp.exp(s_own - m)
        den = jnp.sum(p_own, axis=1, keepdims=True)
        acc = _dot(p_own.astype(BF16), vd[pl.ds(o0, BLK), :])
        if s_prev is not None:
            p_prev = jnp.exp(s_prev - m)
            den = den + jnp.sum(p_prev, axis=1, keepdims=True)
            acc = acc + _dot(p_prev.astype(BF16), vd[pl.ds(p0, BLK), :])
        od[pl.ds(o0, BLK), :] = acc / den
        ld[pl.ds(o0, BLK), :] = jnp.broadcast_to(m + jnp.log(den), (BLK, HEAD_DIM))
        return carry

    lax.fori_loop(0, S // BLK, unit, 0)
    _interleave(og, od, dil)
    _interleave(lg, ld, dil)


def _group_weights(lgs):
    l0, l1, l2 = lgs[0][...], lgs[1][...], lgs[2][...]
    mx = jnp.maximum(l0, jnp.maximum(l1, l2))
    e0, e1, e2 = jnp.exp(l0 - mx), jnp.exp(l1 - mx), jnp.exp(l2 - mx)
    den = e0 + e1 + e2
    return e0 / den, e1 / den, e2 / den


def _head_spec(base, ngroups_axis=True):
    return pl.BlockSpec((S, HEAD_DIM), lambda h, p: (0, base + (p % 3) * 8 + h))


ATTN_SCRATCH_FWD = [
    pltpu.VMEM((S, HEAD_DIM), F32),
    pltpu.VMEM((S, HEAD_DIM), BF16), pltpu.VMEM((S, HEAD_DIM), BF16), pltpu.VMEM((S, HEAD_DIM), BF16),
    pltpu.VMEM((S, HEAD_DIM), F32), pltpu.VMEM((S, HEAD_DIM), F32),
    pltpu.VMEM((S, HEAD_DIM), F32), pltpu.VMEM((S, HEAD_DIM), F32), pltpu.VMEM((S, HEAD_DIM), F32),
    pltpu.VMEM((S, HEAD_DIM), F32), pltpu.VMEM((S, HEAD_DIM), F32), pltpu.VMEM((S, HEAD_DIM), F32),
]


def _attn_fwd(z0, ycat, tabs):
    def body(q_ref, k_ref, v_ref, gate_ref, c_ref, a_ref, b_ref, ycat_ref, out_ref,
             tmp, qd, kd, vd, od, ld, og0, og1, og2, lg0, lg1, lg2):
        del ycat_ref
        p = pl.program_id(1)
        ogs, lgs = (og0, og1, og2), (lg0, lg1, lg2)
        tabs_v = (c_ref[...], a_ref[...], b_ref[...])
        for gi, (_, dil) in enumerate(PATTERNS):
            @pl.when(p == gi)
            def _(gi=gi, dil=dil):
                _attn_group_fwd(dil, q_ref, k_ref, v_ref, tabs_v, tmp, qd, kd, vd, od, ld, ogs[gi], lgs[gi])

        @pl.when(p == 2)
        def _():
            w0, w1, w2 = _group_weights(lgs)
            o = w0 * og0[...] + w1 * og1[...] + w2 * og2[...]
            gate = gate_ref[...]
            out_ref[...] = (o * (gate * _sigmoid(gate))).astype(BF16)

    tab = pl.BlockSpec((S, HEAD_DIM), lambda h, p: (0, 0))
    return pl.pallas_call(
        body, grid=(8, 3),
        in_specs=[_head_spec(Q_COL), _head_spec(K_COL), _head_spec(V_COL),
                  pl.BlockSpec((S, HEAD_DIM), lambda h, p: (0, BG_COL + h)), tab, tab, tab,
                  pl.BlockSpec(memory_space=pl.ANY)],
        out_specs=pl.BlockSpec((S, HEAD_DIM), lambda h, p: (0, 8 + h)),
        out_shape=jax.ShapeDtypeStruct((S, 2048), BF16),
        scratch_shapes=ATTN_SCRATCH_FWD, input_output_aliases={7: 0},
        compiler_params=_params(("parallel", "arbitrary"), VMEM_BIG), name="attn_fwd",
    )(z0, z0, z0, z0, *tabs, ycat)


def _attn_bwd(z0, dycat, tabs):
    def body(q_ref, k_ref, v_ref, gate_ref, dy_ref, c_ref, a_ref, b_ref,
             dq_ref, dk_ref, dv_ref, dbg_ref,
             tmp, qd, kd, vd, od, ld, og0, og1, og2, lg0, lg1, lg2, cg0, cg1, cg2, dod, cd, dqd, dkd, dvd):
        p = pl.program_id(1)
        ogs, lgs, cgs = (og0, og1, og2), (lg0, lg1, lg2), (cg0, cg1, cg2)
        tabs_v = (c_ref[...], a_ref[...], b_ref[...])
        for gi, (_, dil) in enumerate(PATTERNS):
            @pl.when(p == gi)
            def _(gi=gi, dil=dil):
                _attn_group_fwd(dil, q_ref, k_ref, v_ref, tabs_v, tmp, qd, kd, vd, od, ld, ogs[gi], lgs[gi])

        @pl.when(p == 2)
        def _():
            w = _group_weights(lgs)
            o = w[0] * og0[...] + w[1] * og1[...] + w[2] * og2[...]
            silu, dsilu = _silu_and_grad(gate_ref[...])
            dy = dy_ref[...]
            dbg_ref[...] = (dy * o * dsilu).astype(BF16)
            do = dy * silu
            dwbar = jnp.sum(do * o, axis=1, keepdims=True)
            for gi in range(3):
                ogs[gi][...] = w[gi] * do
                cgs[gi][...] = -w[gi] * dwbar

        for gi, (_, dil) in enumerate(PATTERNS):
            @pl.when(p == 3 + gi)
            def _(gi=gi, dil=dil):
                nb = S // dil // BLK
                c, a, b = tabs_v
                tmp[...] = _rope(q_ref[...], c, a, b)
                _deinterleave(qd, tmp, dil, BF16)
                tmp[...] = _rope(k_ref[...], c, a, b)
                _deinterleave(kd, tmp, dil, BF16)
                _deinterleave(vd, v_ref, dil, BF16)
                _deinterleave(dod, ogs[gi], dil, BF16)
                _deinterleave(ld, lgs[gi], dil)
                _deinterleave(cd, cgs[gi], dil)
                dkd[...] = jnp.zeros_like(dkd)
                dvd[...] = jnp.zeros_like(dvd)

                def unit(u, carry):
                    o0, p0, q, s_own, s_prev = _unit_scores(u, nb, qd, kd)
                    lse = ld[pl.ds(o0, BLK), :]
                    cv = cd[pl.ds(o0, BLK), :]
                    do = dod[pl.ds(o0, BLK), :]
                    p_own = jnp.exp(s_own - lse)
                    ds_own = (p_own * (_dot(do, vd[pl.ds(o0, BLK), :], NT) + cv) * SCALE).astype(BF16)
                    dq = _dot(ds_own, kd[pl.ds(o0, BLK), :])
                    dkd[pl.ds(o0, BLK), :] += _dot(ds_own, q, TN)
                    dvd[pl.ds(o0, BLK), :] += _dot(p_own.astype(BF16), do, TN)
                    if s_prev is not None:
                        p_prev = jnp.exp(s_prev - lse)
                        ds_prev = (p_prev * (_dot(do, vd[pl.ds(p0, BLK), :], NT) + cv) * SCALE).astype(BF16)
                        dq = dq + _dot(ds_prev, kd[pl.ds(p0, BLK), :])
                        dkd[pl.ds(p0, BLK), :] += _dot(ds_prev, q, TN)
                        dvd[pl.ds(p0, BLK), :] += _dot(p_prev.astype(BF16), do, TN)
                    dqd[pl.ds(o0, BLK), :] = dq
                    return carry

                lax.fori_loop(0, S // BLK, unit, 0)
                _interleave(tmp, dqd, dil)
                dq_ref[...] = _rope_t(tmp[...], c, a, b).astype(BF16)
                _interleave(tmp, dkd, dil)
                dk_ref[...] = _rope_t(tmp[...], c, a, b).astype(BF16)
                _interleave(tmp, dvd, dil)
                dv_ref[...] = tmp[...].astype(BF16)

    tab = pl.BlockSpec((S, HEAD_DIM), lambda h, p: (0, 0))
    hspec = lambda base: pl.BlockSpec((S, HEAD_DIM), lambda h, p: (0, base + h))
    gspec = pl.BlockSpec((S, HEAD_DIM), lambda h, p: (0, jnp.maximum(p - 3, 0) * 8 + h))
    slab = lambda: pltpu.VMEM((S, HEAD_DIM), F32)
    return pl.pallas_call(
        body, grid=(8, 6),
        in_specs=[_head_spec(Q_COL), _head_spec(K_COL), _head_spec(V_COL), hspec(BG_COL), hspec(8), tab, tab, tab],
        out_specs=[gspec, gspec, gspec, hspec(0)],
        out_shape=[jax.ShapeDtypeStruct((S, 3072), BF16)] * 3 + [jax.ShapeDtypeStruct((S, HALF), BF16)],
        scratch_shapes=ATTN_SCRATCH_FWD + [slab(), slab(), slab(), pltpu.VMEM((S, HEAD_DIM), BF16),
                                           slab(), slab(), slab(), slab()],
        compiler_params=_params(("parallel", "arbitrary"), VMEM_BIG), name="attn_bwd",
    )(z0, z0, z0, z0, dycat, *tabs)


SGU_CH = 256
NCHUNK = TR // 128


def _ln_stats(x):
    mu = jnp.mean(x, axis=-1, keepdims=True)
    xc = x - mu
    r = lax.rsqrt(jnp.mean(xc * xc, axis=-1, keepdims=True) + EPS)
    return xc * r, r


def _ln_bwd(dy, xhat, r, g):
    dxh = dy * g
    return r * (dxh - jnp.mean(dxh, axis=-1, keepdims=True) - xhat * jnp.mean(dxh * xhat, axis=-1, keepdims=True))


def _tril_bf16(w):
    row = lax.broadcasted_iota(jnp.int32, w.shape, 0)
    col = lax.broadcasted_iota(jnp.int32, w.shape, 1)
    return jnp.where(row >= col, w, 0.0).astype(BF16)


def _sgu_gate(vn_s, s_s, w_ref, bb_ref):
    for h in range(4):
        wm = _tril_bf16(w_ref[h])
        bias = bb_ref[h]
        for ch in range(NCHUNK):
            rows, cols = slice(ch * 128, (ch + 1) * 128), slice(h * SGU_CH, (h + 1) * SGU_CH)
            s_s[rows, cols] = _dot(wm, vn_s[rows, cols]) + jnp.concatenate([bias, bias], axis=1)


def _conv_fwd(i, dval_ref, dglu_ref, hval_ref, hglu_ref, cw_ref, cb_ref, xw, dcs):
    halo = hval_ref[...] * _sigmoid(hglu_ref[...])
    xw[0:HALO, :] = jnp.where(i > 0, halo, 0.0)
    xw[HALO:HALO + TR, :] = dval_ref[...] * _sigmoid(dglu_ref[...])
    for rb in range(TR // SUB):
        acc = jnp.broadcast_to(cb_ref[...], (SUB, HALF))
        for k in range(CONV_K):
            acc = acc + cw_ref[k:k + 1, :] * xw[pl.ds(rb * SUB + HALO - (CONV_K - 1) + k, SUB), :]
        dcs[rb * SUB:(rb + 1) * SUB, :] = acc


def _odd_in_specs():
    col = lambda j: pl.BlockSpec((TR, HALF), lambda i, *_: (i, j))
    prev = lambda j: pl.BlockSpec((HALO, HALF), lambda i, *_: (jnp.maximum(i * (TR // HALO) - 1, 0), j))
    return [col(0), col(1), col(2), col(3), col(4), col(5), prev(3), prev(4)]


def _full_spec(shape):
    return pl.BlockSpec(shape, lambda i, *_: (0,) * len(shape))


def _odd_fwd(z1, sgu_g, sgu_b, sgu_w, sgu_bb, conv_w, conv_b, cn_g, cn_b):
    def body(u_ref, v_ref, cg_ref, dval_ref, dglu_ref, dgate_ref, hval_ref, hglu_ref,
             g_ref, b_ref, w_ref, bb_ref, cw_ref, cb_ref, cng_ref, cnb_ref, out_ref, vn_s, s_s, xw, dcs):
        i = pl.program_id(0)
        vhat, _ = _ln_stats(v_ref[...])
        vn_s[...] = (vhat * g_ref[...] + b_ref[...]).astype(BF16)
        _sgu_gate(vn_s, s_s, w_ref, bb_ref)
        cg = cg_ref[...]
        out_ref[:, 0:HALF] = (u_ref[...] * s_s[...] * (cg * _sigmoid(cg))).astype(BF16)
        _conv_fwd(i, dval_ref, dglu_ref, hval_ref, hglu_ref, cw_ref, cb_ref, xw, dcs)
        dhat, _ = _ln_stats(dcs[...])
        dn = dhat * cng_ref[...] + cnb_ref[...]
        dgate = dgate_ref[...]
        out_ref[:, HALF:2 * HALF] = ((dn * _sigmoid(dn)) * (dgate * _sigmoid(dgate))).astype(BF16)

    vec = _full_spec((1, HALF))
    return pl.pallas_call(
        body, grid=(S // TR,),
        in_specs=_odd_in_specs() + [vec, vec, _full_spec((4, 128, 128)), _full_spec((4, 128, 128)),
                                    _full_spec((HALO, HALF)), vec, vec, vec],
        out_specs=pl.BlockSpec((TR, 2048), lambda i: (i, 0)),
        out_shape=jax.ShapeDtypeStruct((S, 2048), BF16),
        scratch_shapes=[pltpu.VMEM((TR, HALF), BF16), pltpu.VMEM((TR, HALF), F32),
                        pltpu.VMEM((HALO + TR, HALF), F32), pltpu.VMEM((TR, HALF), F32)],
        compiler_params=_params(("parallel",), VMEM_BIG), name="odd_fwd",
    )(z1, z1, z1, z1, z1, z1, z1, z1, sgu_g, sgu_b, sgu_w, sgu_bb, conv_w, conv_b, cn_g, cn_b)


def _odd_bwd_a(z1, dycat, sgu_g, sgu_b, sgu_w, sgu_bb, conv_w, conv_b, cn_g, cn_b):
    def body(u_ref, v_ref, cg_ref, dval_ref, dglu_ref, dgate_ref, hval_ref, hglu_ref, dy_ref,
             g_ref, b_ref, w_ref, bb_ref, cw_ref, cb_ref, cng_ref, cnb_ref,
             dz_ref, ddc_ref, dw_ref, dbb_ref, dg_ref, db_ref, dcng_ref, dcnb_ref, dcb_ref,
             vn_s, s_s, xw, dcs, ds_s, dvn_s):
        i = pl.program_id(0)
        vhat, rv = _ln_stats(v_ref[...])
        g = g_ref[...]
        vn_s[...] = (vhat * g + b_ref[...]).astype(BF16)
        _sgu_gate(vn_s, s_s, w_ref, bb_ref)
        silu_c, dsilu_c = _silu_and_grad(cg_ref[...])
        dyc = dy_ref[:, 0:HALF]
        u = u_ref[...]
        s = s_s[...]
        dz_ref[:, 0:HALF] = (dyc * s * silu_c).astype(BF16)
        dz_ref[:, 2 * HALF:3 * HALF] = (dyc * u * s * dsilu_c).astype(BF16)
        ds_s[...] = dyc * u * silu_c

        @pl.when(i == 0)
        def _():
            dw_ref[...] = jnp.zeros_like(dw_ref)
            dbb_ref[...] = jnp.zeros_like(dbb_ref)

        tril = lax.broadcasted_iota(jnp.int32, (128, 128), 0) >= lax.broadcasted_iota(jnp.int32, (128, 128), 1)
        for h in range(4):
            wm = _tril_bf16(w_ref[h])
            for ch in range(NCHUNK):
                rows, cols = slice(ch * 128, (ch + 1) * 128), slice(h * SGU_CH, (h + 1) * SGU_CH)
                ds = ds_s[rows, cols]
                dsb = ds.astype(BF16)
                dw_ref[h] += jnp.where(tril, _dot(dsb, vn_s[rows, cols], NT), 0.0)
                dbb_ref[h] += jnp.broadcast_to(jnp.sum(ds, axis=1, keepdims=True), (128, 128))
                dvn_s[rows, cols] = _dot(wm, dsb, TN)
        dvn = dvn_s[...]
        _acc_rows(dg_ref, dvn * vhat, i)
        _acc_rows(db_ref, dvn, i)
        dz_ref[:, HALF:2 * HALF] = _ln_bwd(dvn, vhat, rv, g).astype(BF16)

        _conv_fwd(i, dval_ref, dglu_ref, hval_ref, hglu_ref, cw_ref, cb_ref, xw, dcs)
        dhat, rd = _ln_stats(dcs[...])
        cng = cng_ref[...]
        silu_n, dsilu_n = _silu_and_grad(dhat * cng + cnb_ref[...])
        silu_g, dsilu_g = _silu_and_grad(dgate_ref[...])
        dyd = dy_ref[:, HALF:2 * HALF]
        dz_ref[:, 5 * HALF:6 * HALF] = (dyd * silu_n * dsilu_g).astype(BF16)
        ddn = dyd * silu_g * dsilu_n
        _acc_rows(dcng_ref, ddn * dhat, i)
        _acc_rows(dcnb_ref, ddn, i)
        ddc = _ln_bwd(ddn, dhat, rd, cng)
        ddc_ref[...] = ddc
        _acc_rows(dcb_ref, ddc, i)

    vec = _full_spec((1, HALF))
    sq = _full_spec((4, 128, 128))
    return pl.pallas_call(
        body, grid=(S // TR,),
        in_specs=_odd_in_specs() + [pl.BlockSpec((TR, 2048), lambda i: (i, 0)),
                                    vec, vec, sq, sq, _full_spec((HALO, HALF)), vec, vec, vec],
        out_specs=[pl.BlockSpec((TR, ODD_IN), lambda i: (i, 0)), pl.BlockSpec((TR, HALF), lambda i: (i, 0)),
                   sq, sq, vec, vec, vec, vec, vec],
        out_shape=[jax.ShapeDtypeStruct((S, ODD_IN), BF16), jax.ShapeDtypeStruct((S, HALF), F32),
                   jax.ShapeDtypeStruct((4, 128, 128), F32), jax.ShapeDtypeStruct((4, 128, 128), F32)]
                  + [jax.ShapeDtypeStruct((1, HALF), F32)] * 5,
        scratch_shapes=[pltpu.VMEM((TR, HALF), BF16), pltpu.VMEM((TR, HALF), F32),
                        pltpu.VMEM((HALO + TR, HALF), F32), pltpu.VMEM((TR, HALF), F32),
                        pltpu.VMEM((TR, HALF), F32), pltpu.VMEM((TR, HALF), F32)],
        compiler_params=_params(("arbitrary",), VMEM_BIG), name="odd_bwd_a",
    )(z1, z1, z1, z1, z1, z1, z1, z1, dycat, sgu_g, sgu_b, sgu_w, sgu_bb, conv_w, conv_b, cn_g, cn_b)


def _odd_bwd_b(z1, ddc, dz1, conv_w):
    nt = S // TR

    def body(dval_ref, dglu_ref, hval_ref, hglu_ref, ddc_ref, hddc_ref, cw_ref, dz_in_ref,
             dz_ref, dcw_ref, xw, dwin, dxs):
        del dz_in_ref
        i, j = pl.program_id(0), pl.program_id(1)
        sg = _sigmoid(dglu_ref[...])
        dval = dval_ref[...]

        @pl.when(j == 0)
        def _():
            halo = hval_ref[...] * _sigmoid(hglu_ref[...])
            xw[0:HALO, :] = jnp.where(i > 0, halo, 0.0)
            xw[HALO:HALO + TR, :] = dval * sg
            dwin[0:TR, :] = ddc_ref[...]
            dwin[TR:TR + HALO, :] = jnp.where(i < nt - 1, hddc_ref[...], 0.0)

            @pl.when(i == 0)
            def _():
                dcw_ref[...] = jnp.zeros_like(dcw_ref)

            for rb in range(TR // SUB):
                acc = jnp.zeros((SUB, HALF), F32)
                for k in range(CONV_K):
                    acc = acc + cw_ref[k:k + 1, :] * dwin[pl.ds(rb * SUB + (CONV_K - 1) - k, SUB), :]
                dxs[rb * SUB:(rb + 1) * SUB, :] = acc
            for k in range(CONV_K):
                acc = jnp.zeros((SUB, HALF), F32)
                for rb in range(TR // SUB):
                    acc = acc + dwin[rb * SUB:(rb + 1) * SUB, :] * xw[pl.ds(rb * SUB + HALO - (CONV_K - 1) + k, SUB), :]
                dcw_ref[k:k + 1, :] += jnp.sum(acc, axis=0, keepdims=True)
            dz_ref[...] = (dxs[...] * sg).astype(BF16)

        @pl.when(j == 1)
        def _():
            dz_ref[...] = (dxs[...] * dval * sg * (1.0 - sg)).astype(BF16)

    col = lambda c: pl.BlockSpec((TR, HALF), lambda i, j: (i, c))
    prev = lambda c: pl.BlockSpec((HALO, HALF), lambda i, j: (jnp.maximum(i * (TR // HALO) - 1, 0), c))
    nxt = pl.BlockSpec((HALO, HALF), lambda i, j: (jnp.minimum((i + 1) * (TR // HALO), S // HALO - 1), 0))
    return pl.pallas_call(
        body, grid=(nt, 2),
        in_specs=[col(3), col(4), prev(3), prev(4), pl.BlockSpec((TR, HALF), lambda i, j: (i, 0)), nxt,
                  _full_spec((HALO, HALF)), pl.BlockSpec(memory_space=pl.ANY)],
        out_specs=[pl.BlockSpec((TR, HALF), lambda i, j: (i, 3 + j)), _full_spec((HALO, HALF))],
        out_shape=[jax.ShapeDtypeStruct((S, ODD_IN), BF16), jax.ShapeDtypeStruct((HALO, HALF), F32)],
        scratch_shapes=[pltpu.VMEM((HALO + TR, HALF), F32), pltpu.VMEM((TR + HALO, HALF), F32),
                        pltpu.VMEM((TR, HALF), F32)],
        input_output_aliases={7: 0},
        compiler_params=_params(("arbitrary", "arbitrary"), VMEM_BIG), name="odd_bwd_b",
    )(z1, z1, z1, z1, ddc, ddc, conv_w, dz1)


def _cast_bf16(w, name):
    r, c = w.shape
    tr = min(r, 256)
    def body(i_ref, o_ref):
        o_ref[...] = i_ref[...].astype(BF16)

    return pl.pallas_call(
        body, grid=(r // tr,), in_specs=[pl.BlockSpec((tr, c), lambda i: (i, 0))],
        out_specs=pl.BlockSpec((tr, c), lambda i: (i, 0)), out_shape=jax.ShapeDtypeStruct((r, c), BF16),
        compiler_params=_params(("parallel",)), name=name,
    )(w)


def _adamw(w, g, m, v):
    m = ADAM_B1 * m + (1.0 - ADAM_B1) * g
    v = ADAM_B2 * v + (1.0 - ADAM_B2) * (g * g)
    m_hat = m / (1.0 - ADAM_B1 ** ADAM_STEP)
    v_hat = v / (1.0 - ADAM_B2 ** ADAM_STEP)
    delta = -ADAM_LR * (m_hat / (jnp.sqrt(v_hat) + ADAM_EPS) + ADAM_WD * w)
    return delta, m, v


def _adam_reduce(parts, w, m, v, name, dep=None):
    r, c = w.shape
    tr = min(r, 128)
    deps = [] if dep is None else [dep]

    def body(p_ref, w_ref, m_ref, v_ref, *rest):
        g_ref, d_ref, nm_ref, nv_ref = rest[len(deps):]
        g = p_ref[0].astype(F32)
        for d in range(1, NDEV):
            g = g + p_ref[d].astype(F32)
        g_ref[...] = g
        d_ref[...], nm_ref[...], nv_ref[...] = _adamw(w_ref[...], g, m_ref[...], v_ref[...])

    spec = pl.BlockSpec((tr, c), lambda i: (i, 0))
    return pl.pallas_call(
        body, grid=(r // tr,),
        in_specs=[pl.BlockSpec((NDEV, tr, c), lambda i: (0, i, 0)), spec, spec, spec] + [ANY_SPEC] * len(deps),
        out_specs=[spec] * 4, out_shape=[jax.ShapeDtypeStruct((r, c), F32)] * 4,
        compiler_params=_params(("parallel",), VMEM_BIG), name=name,
    )(parts, w, m, v, *deps)


def _sum_parts(parts, name, dep=None):
    r = parts.shape[1]
    tr = 8
    for cand in (512, 256, 128, 64, 32, 16, 8):
        if r % cand == 0:
            tr = cand
            break
    deps = [] if dep is None else [dep]

    def body(p_ref, *rest):
        g = p_ref[0]
        for d in range(1, NDEV):
            g = g + p_ref[d]
        rest[-1][...] = g

    return pl.pallas_call(
        body, grid=(r // tr,), in_specs=[pl.BlockSpec((NDEV, tr, 128), lambda i: (0, i, 0))] + [ANY_SPEC] * len(deps),
        out_specs=pl.BlockSpec((tr, 128), lambda i: (i, 0)), out_shape=jax.ShapeDtypeStruct((r, 128), F32),
        compiler_params=_params(("parallel",)), name=name,
    )(parts, *deps)


def _adam_plain(w, g, m, v, name):
    r, c = w.shape

    def body(w_ref, g_ref, m_ref, v_ref, d_ref, nm_ref, nv_ref):
        d_ref[...], nm_ref[...], nv_ref[...] = _adamw(w_ref[...], g_ref[...], m_ref[...], v_ref[...])

    spec = pl.BlockSpec((r, c), lambda i: (0, 0))
    return pl.pallas_call(
        body, grid=(1,), in_specs=[spec] * 4, out_specs=[spec] * 3,
        out_shape=[jax.ShapeDtypeStruct((r, c), F32)] * 3,
        compiler_params=_params(("arbitrary",)), name=name,
    )(w, g, m, v)


MASKS = [(mx, my, mc) for mx in (0, 1) for my in (0, 1) for mc in (0, 1)][1:]


def _exchange(arrays, scatter, name):
    nt = len(arrays)
    out_shape = [jax.ShapeDtypeStruct(((NDEV,) + a.shape) if not scatter else a.shape, a.dtype) for a in arrays]

    def body(*refs):
        ins, outs = refs[:nt], refs[nt:2 * nt]
        send_sems, recv_sems, local_sems = refs[2 * nt:]
        x, y, c = lax.axis_index("x"), lax.axis_index("y"), lax.axis_index("c")
        me = 4 * x + 2 * y + c
        copies = []
        for t in range(nt):
            src_own = ins[t].at[me] if scatter else ins[t]
            loc = pltpu.make_async_copy(src_own, outs[t].at[me], local_sems.at[t])
            loc.start()
            copies.append(loc)
            for k, (mx, my, mc) in enumerate(MASKS):
                px, py, pc = (x + mx) % 2, (y + my) % 2, (c + mc) % 2
                peer = 4 * px + 2 * py + pc
                src = ins[t].at[peer] if scatter else ins[t]
                rc = pltpu.make_async_remote_copy(
                    src_ref=src, dst_ref=outs[t].at[me], send_sem=send_sems.at[t, k], recv_sem=recv_sems.at[t, k],
                    device_id=(px, py, pc), device_id_type=MESH)
                rc.start()
                copies.append(rc)
        for cp in copies:
            cp.wait()

    hbm = pl.BlockSpec(memory_space=pl.ANY)
    return pl.pallas_call(
        body, in_specs=[hbm] * nt, out_specs=[hbm] * nt, out_shape=out_shape,
        scratch_shapes=[pltpu.SemaphoreType.DMA((nt, 7)), pltpu.SemaphoreType.DMA((nt, 7)),
                        pltpu.SemaphoreType.DMA((nt,))],
        name=name,
    )(*arrays)


SEM_SPEC = pl.BlockSpec(memory_space=pltpu.SEMAPHORE)
EFFECT = pltpu.SideEffectType.DATAFLOW_SIDE_EFFECTING


def _direct_plan(scatter):
    def plan(x, y, c, srcs, lands):
        me = 4 * x + 2 * y + c
        local, remote = [], []
        for src, land in zip(srcs, lands):
            local.append((src.at[me] if scatter else src, land.at[me]))
            for mx, my, mc in MASKS:
                px, py, pc = (x + mx) % 2, (y + my) % 2, (c + mc) % 2
                blk = src.at[4 * px + 2 * py + pc] if scatter else src
                remote.append((blk, land.at[me], (px, py, pc)))
        return local, remote
    return plan


def _split_start(name, srcs, land_shapes, plan, n_local, n_remote, dep=None):
    ns, nl = len(srcs), len(land_shapes)
    deps = [] if dep is None else [dep]
    lands = [lax.empty(s.shape, s.dtype) for s in land_shapes]

    def body(*refs):
        ins, lz = refs[:ns], refs[ns:ns + nl]
        outs = refs[ns + nl + len(deps):]
        send_sems, recv_sems, token, local_sems = outs[0], outs[1], outs[2 + ns + nl], outs[3 + ns + nl]
        local, remote = plan(lax.axis_index("x"), lax.axis_index("y"), lax.axis_index("c"), ins, lz)
        own = [pltpu.make_async_copy(src, dst, local_sems.at[i]) for i, (src, dst) in enumerate(local)]
        for cp in own:
            cp.start()
        for cp in own:
            cp.wait()
        for k, (src, dst, peer) in enumerate(remote):
            pltpu.make_async_remote_copy(src_ref=src, dst_ref=dst, send_sem=send_sems.at[k], recv_sem=recv_sems.at[k],
                                         device_id=peer, device_id_type=MESH).start()
        token[...] = jnp.zeros_like(token)

    hbm = lambda a: pltpu.HBM(a.shape, a.dtype)
    outs = pl.pallas_call(
        body, name=name,
        out_shape=(pltpu.SemaphoreType.DMA((n_remote,)), pltpu.SemaphoreType.DMA((n_remote,)),
                   *[hbm(a) for a in srcs], *[hbm(a) for a in lands], jax.ShapeDtypeStruct((8, 128), F32)),
        in_specs=[ANY_SPEC] * (ns + nl + len(deps)),
        out_specs=(SEM_SPEC, SEM_SPEC, *[ANY_SPEC] * (ns + nl), pl.BlockSpec(memory_space=pltpu.VMEM)),
        scratch_shapes=[pltpu.SemaphoreType.DMA((n_local,))],
        input_output_aliases={i: 2 + i for i in range(ns + nl)},
        compiler_params=pltpu.CompilerParams(has_side_effects=EFFECT),
    )(*[pltpu.with_memory_space_constraint(a, pltpu.HBM) for a in srcs],
      *[pltpu.with_memory_space_constraint(a, pltpu.HBM) for a in lands], *deps)
    return dict(sems=outs[:2], srcs=outs[2:2 + ns], lands=outs[2 + ns:2 + ns + nl], token=outs[-1],
                plan=plan, n_remote=n_remote)


def _split_wait(name, handle, after):
    srcs, lands, plan = handle["srcs"], handle["lands"], handle["plan"]
    ns, nl = len(srcs), len(lands)

    def body(*refs):
        ins, lz = refs[:ns], refs[ns:ns + nl]
        send_sems, recv_sems = refs[ns + nl], refs[ns + nl + 1]
        _, remote = plan(lax.axis_index("x"), lax.axis_index("y"), lax.axis_index("c"), ins, lz)
        for k, (src, dst, peer) in enumerate(remote):
            cp = pltpu.make_async_remote_copy(src_ref=src, dst_ref=dst, send_sem=send_sems.at[k],
                                              recv_sem=recv_sems.at[k], device_id=peer, device_id_type=MESH)
            cp.wait_send()
            cp.wait_recv()

    hbm = lambda a: pltpu.HBM(a.shape, a.dtype)
    outs = pl.pallas_call(
        body, name=name, out_shape=(*[hbm(a) for a in srcs], *[hbm(a) for a in lands]),
        in_specs=[ANY_SPEC] * (ns + nl) + [SEM_SPEC, SEM_SPEC, ANY_SPEC], out_specs=tuple([ANY_SPEC] * (ns + nl)),
        input_output_aliases={i: i for i in range(ns + nl)},
        compiler_params=pltpu.CompilerParams(has_side_effects=EFFECT),
    )(*srcs, *lands, *handle["sems"], after)
    return list(outs[ns:])


def _sc_exchange(name, collective_id, arrays, scatter):
    nt = len(arrays)
    out_type = [jax.ShapeDtypeStruct(a.shape if scatter else (NDEV,) + a.shape, a.dtype) for a in arrays]

    def body(*refs):
        ins, outs = refs[:nt], refs[nt:2 * nt]
        send_sems, recv_sems, local_sems = refs[2 * nt:3 * nt], refs[3 * nt:4 * nt], refs[4 * nt:5 * nt]
        x, y, c = lax.axis_index("x"), lax.axis_index("y"), lax.axis_index("c")
        peers = [(mx + x - 2 * mx * x, my + y - 2 * my * y, mc + c - 2 * mc * c) for mx, my, mc in MASKS]
        barrier = pltpu.get_barrier_semaphore()
        for peer in peers:
            pl.semaphore_signal(barrier, inc=1, device_id=peer, device_id_type=MESH)
        pl.semaphore_wait(barrier, len(peers))
        me = 4 * x + 2 * y + c
        own = []
        for t in range(nt):
            cp = pltpu.make_async_copy(ins[t].at[me] if scatter else ins[t], outs[t].at[me], local_sems[t])
            cp.start()
            own.append(cp)
            for px, py, pc in peers:
                src = ins[t].at[4 * px + 2 * py + pc] if scatter else ins[t]
                pltpu.make_async_remote_copy(src_ref=src, dst_ref=outs[t].at[me], send_sem=send_sems[t],
                                             recv_sem=recv_sems[t], device_id=(px, py, pc), device_id_type=MESH).start()
        for t in range(nt):
            own[t].wait()
            seven = outs[t].at[pl.ds(0, NDEV - 1)]
            drain = pltpu.make_async_remote_copy(src_ref=seven, dst_ref=seven, send_sem=send_sems[t],
                                                 recv_sem=recv_sems[t], device_id=(x, y, c), device_id_type=MESH)
            drain.wait_send()
            drain.wait_recv()

    return pl.kernel(
        body, out_type=out_type, mesh=plsc.ScalarSubcoreMesh(axis_name="sequencer", num_cores=1),
        scratch_types=[pltpu.SemaphoreType.DMA] * (3 * nt),
        compiler_params=pltpu.CompilerParams(collective_id=collective_id), name=name,
    )(*arrays)


SMALL = {
    "e_pre_norm": ((2048,), None), "e_pool_w": ((4, 256, 256), 1), "e_pool_scale": ((1024,), None),
    "e_post_norm": ((2048,), None), "o_pre_norm": ((2048,), 0), "o_sgu_norm_g": ((1024,), 0),
    "o_sgu_norm_b": ((1024,), 0), "o_sgu_w": ((4, 128, 128), None), "o_sgu_b": ((4, 128), None),
    "o_conv_w": ((31, 1024), 1), "o_conv_b": ((1024,), 0), "o_conv_norm_g": ((1024,), 0),
    "o_conv_norm_b": ((1024,), 0), "o_post_norm": ((2048,), 0),
}
SMALL_SHARDED = [n for n, (_, ax) in SMALL.items() if ax is not None]


def _shard_shape(name):
    shape, ax = SMALL[name]
    if ax is None:
        return shape
    return tuple(s // NDEV if i == ax else s for i, s in enumerate(shape))


def _pack(arrs, row_multiple=1):
    flat = jnp.concatenate([a.reshape(-1) for a in arrs])
    pad = -flat.shape[0] % (128 * row_multiple)
    return jnp.concatenate([flat, jnp.zeros((pad,), F32)]).reshape(-1, 128)


def _unpack(buf, shapes):
    flat = buf.reshape(-1)
    out, off = [], 0
    for shp in shapes:
        n = int(np.prod(shp))
        out.append(flat[off:off + n].reshape(shp))
        off += n
    return out


def _take_shard(full, name, me):
    shape, ax = SMALL[name]
    if ax is None:
        return full
    n = shape[ax] // NDEV
    return lax.dynamic_slice_in_dim(full, me * n, n, axis=ax)


BIG = ("e_w_in", "e_w_out", "o_w_in", "o_w_out")
WEIGHTS = ["e_pre_norm", "e_w_in", "e_pool_w", "e_pool_scale", "e_w_out", "e_post_norm", "o_pre_norm", "o_w_in",
           "o_sgu_norm_g", "o_sgu_norm_b", "o_sgu_w", "o_sgu_b", "o_conv_w", "o_conv_b", "o_conv_norm_g",
           "o_conv_norm_b", "o_w_out", "o_post_norm"]


def kernel(x, e_pre_norm, e_w_in, e_pool_w, e_pool_scale, e_w_out, e_post_norm, o_pre_norm, o_w_in, o_sgu_norm_g, o_sgu_norm_b, o_sgu_w, o_sgu_b, o_conv_w, o_conv_b, o_conv_norm_g, o_conv_norm_b, o_w_out, o_post_norm, loss_target, m_e_pre_norm, m_e_w_in, m_e_pool_w, m_e_pool_scale, m_e_w_out, m_e_post_norm, m_o_pre_norm, m_o_w_in, m_o_sgu_norm_g, m_o_sgu_norm_b, m_o_sgu_w, m_o_sgu_b, m_o_conv_w, m_o_conv_b, m_o_conv_norm_g, m_o_conv_norm_b, m_o_w_out, m_o_post_norm, v_e_pre_norm, v_e_w_in, v_e_pool_w, v_e_pool_scale, v_e_w_out, v_e_post_norm, v_o_pre_norm, v_o_w_in, v_o_sgu_norm_g, v_o_sgu_norm_b, v_o_sgu_w, v_o_sgu_b, v_o_conv_w, v_o_conv_b, v_o_conv_norm_g, v_o_conv_norm_b, v_o_w_out, v_o_post_norm):
    given = dict(locals())
    w = {n: given[n][0] for n in WEIGHTS}
    m = {n: given["m_" + n][0] for n in WEIGHTS}
    v = {n: given["v_" + n][0] for n in WEIGHTS}
    me = 4 * lax.axis_index("x") + 2 * lax.axis_index("y") + lax.axis_index("c")
    x, target = x[0], loss_target[0]
    row = lambda a: a.reshape(1, -1)

    bf = {n: _cast_bf16(w[n], "cast_" + n) for n in BIG}
    wg_e_in, small_rows = _sc_exchange("gather_a", 0, [bf["e_w_in"], _pack([w[n] for n in SMALL_SHARDED])], False)
    wg_e_out, wg_o_in, wg_o_out = _sc_exchange("gather_b", 1, [bf["e_w_out"], bf["o_w_in"], bf["o_w_out"]], False)
    h0 = _pre0_fwd(x, row(w["e_pre_norm"]))
    p = {n: w[n] for n in SMALL if SMALL[n][1] is None}
    small_rows = small_rows.reshape(NDEV, -1)
    off = 0
    for n in SMALL_SHARDED:
        shp, ax = _shard_shape(n), SMALL[n][1]
        cnt = int(np.prod(shp))
        blk = small_rows[:, off:off + cnt].reshape((NDEV,) + shp)
        p[n] = jnp.moveaxis(blk, 0, ax).reshape(SMALL[n][0])
        off += cnt
    tabs = _rope_tables()
    pool_w_bf = p["e_pool_w"].astype(BF16)
    sgu_bb = jnp.broadcast_to(p["o_sgu_b"][:, :, None], (4, 128, 128))
    conv_w = jnp.concatenate([p["o_conv_w"], jnp.zeros((HALO - CONV_K, HALF), F32)], axis=0)
    odd_p = (row(p["o_sgu_norm_g"]), row(p["o_sgu_norm_b"]), p["o_sgu_w"], sgu_bb, conv_w,
             row(p["o_conv_b"]), row(p["o_conv_norm_g"]), row(p["o_conv_norm_b"]))

    z0 = _mm_in(h0, wg_e_in, "mm_z0")
    ycat0 = _pool_fwd(z0, pool_w_bf, row(p["e_pool_scale"]))
    ycat0 = _attn_fwd(z0, ycat0, tabs)
    w_out_e, w_out_o = wg_e_out.reshape(2048, D), wg_o_out.reshape(2048, D)
    y0 = _mm_out(ycat0, w_out_e, "mm_y0")
    x1, h1 = _post0_fwd(x, y0, row(p["e_post_norm"]), row(p["o_pre_norm"]))
    z1 = _mm_in(h1, wg_o_in, "mm_z1")
    ycat1 = _odd_fwd(z1, *odd_p)
    y1 = _mm_out(ycat1, w_out_o, "mm_y1")

    g = {}
    loss, dx2, dy1, g["o_post_norm"] = _post1_bwd(y1, x1, target, row(p["o_post_norm"]))
    loss = lax.psum(loss[0, 0], ("x", "y", "c"))
    parts = {}
    dw = _mm_out_dw(ycat1, dy1, "mm_dwout1").reshape(NDEV, 256, D)
    parts["o_w_out"], = _sc_exchange("scatter_o_w_out", 2, [dw], True)
    dycat1 = _mm_out_dx(dy1, w_out_o, "mm_dycat1", dw)
    dz1, ddc, g["o_sgu_w"], d_sgu_bb, g["o_sgu_norm_g"], g["o_sgu_norm_b"], g["o_conv_norm_g"], \
        g["o_conv_norm_b"], g["o_conv_b"] = _odd_bwd_a(z1, dycat1, *odd_p)
    dz1, d_conv_w = _odd_bwd_b(z1, ddc, dz1, conv_w)
    g["o_sgu_b"] = d_sgu_bb[:, :, 0]
    g["o_conv_w"] = d_conv_w[:CONV_K]
    grads, deltas, new_m, new_v = {}, {}, {}, {}

    def adam(n, dep):
        grads[n], deltas[n], new_m[n], new_v[n] = _adam_reduce(parts[n], w[n], m[n], v[n], "adam_" + n, dep)
        return new_v[n]

    pin = adam("o_w_out", d_conv_w)
    dw = _mm_in_dw(h1, dz1, ODD_IN // NDEV, "mm_dwin1", pin)
    parts["o_w_in"], = _sc_exchange("scatter_o_w_in", 3, [dw], True)
    dh1 = _mm_in_dx(dz1, wg_o_in, "mm_dh1", dw)
    dx1, dy0, g["o_pre_norm"], g["e_post_norm"] = _mid_bwd(dx2, dh1, x1, y0, row(p["o_pre_norm"]),
                                                           row(p["e_post_norm"]))
    dw = _mm_out_dw(ycat0, dy0, "mm_dwout0").reshape(NDEV, 256, D)
    parts["e_w_out"], = _sc_exchange("scatter_e_w_out", 4, [dw], True)
    dycat0 = _mm_out_dx(dy0, w_out_e, "mm_dycat0", dw)
    da_in, da_gate, g["e_pool_w"], g["e_pool_scale"] = _pool_bwd(z0, dycat0, pool_w_bf, row(p["e_pool_scale"]))
    dq, dk, dv, dbg = _attn_bwd(z0, dycat0, tabs)
    dz0 = jnp.concatenate([da_in, da_gate, dq, dk, dv, dbg], axis=1)
    late = [n for n in SMALL if n != "e_pre_norm"]
    pin = adam("e_w_out", adam("o_w_in", dbg))
    dw = _mm_in_dw(h0, dz0, EVEN_IN // NDEV, "mm_dwin0", pin)
    parts["e_w_in"], = _sc_exchange("scatter_e_w_in", 5, [dw], True)
    recv_small, = _sc_exchange("gather_small_grads", 6, [_pack([g[n].reshape(SMALL[n][0]) for n in late], 512)], False)
    dh0 = _mm_in_dx(dz0, wg_e_in, "mm_dh0", dw)
    grad_x, g["e_pre_norm"] = _pre0_bwd(dx1, dh0, x, row(p["e_pre_norm"]))
    last, = _sc_exchange("gather_e_pre_norm_grad", 7, [g["e_pre_norm"].reshape(16, 128)], False)

    g_small = dict(zip(late, _unpack(_sum_parts(recv_small, "sum_small_grads"), [SMALL[n][0] for n in late])))
    pin = adam("e_w_in", grad_x)
    g_small["e_pre_norm"] = _sum_parts(last, "sum_e_pre_norm_grad", pin).reshape(2048)
    for n in SMALL:
        grads[n] = _take_shard(g_small[n], n, me)
    names = list(SMALL)
    shapes = [_shard_shape(n) for n in names]
    d_pack, m_pack, v_pack = _adam_plain(_pack([w[n] for n in names]), _pack([grads[n] for n in names]),
                                         _pack([m[n] for n in names]), _pack([v[n] for n in names]), "adam_small")
    for n, d_, m_, v_ in zip(names, _unpack(d_pack, shapes), _unpack(m_pack, shapes), _unpack(v_pack, shapes)):
        deltas[n], new_m[n], new_v[n] = d_, m_, v_

    lead = lambda a: a[None]
    return (loss, grad_x[None], *[lead(grads[n]) for n in WEIGHTS], *[lead(deltas[n]) for n in WEIGHTS],
            *[lead(new_m[n]) for n in WEIGHTS], *[lead(new_v[n]) for n in WEIGHTS])
```

```python
import functools

import numpy as np
import jax
import jax.numpy as jnp
from jax import lax
from jax.experimental import pallas as pl
from jax.experimental.pallas import tpu as pltpu
from jax.experimental.pallas import tpu_sc as plsc

F32 = jnp.float32
BF16 = jnp.bfloat16

S = 2048
D = 2048
NDEV = 8
EPS = 1e-6
NEG = -1e30
HEAD_DIM = 128
ROT_DIM = 32
ROPE_THETA = 500000.0
PATTERNS = ((128, 1), (512, 4), (2048, 16))
BLK = 128
EVEN_IN = 12288
ODD_IN = 6144
HALF = 1024
CONV_K = 31
HALO = 32
TR = 256
SUB = 32

ADAM_LR = 0.001
ADAM_B1 = 0.9
ADAM_B2 = 0.999
ADAM_EPS = 1e-08
ADAM_WD = 0.01
ADAM_STEP = 10

VMEM_BIG = 56 * 1024 * 1024
MESH = pl.DeviceIdType.MESH

NN = (((1,), (0,)), ((), ()))
NT = (((1,), (1,)), ((), ()))
TN = (((0,), (0,)), ((), ()))


def _dot(a, b, dn=NN):
    return lax.dot_general(a, b, dn, preferred_element_type=F32)


def _sigmoid(x):
    return 1.0 / (1.0 + jnp.exp(-x))


def _silu_and_grad(x):
    sg = _sigmoid(x)
    return x * sg, sg * (1.0 + x * (1.0 - sg))


def _params(sem, vmem=None):
    return pltpu.CompilerParams(dimension_semantics=sem, vmem_limit_bytes=vmem)


ANY_SPEC = pl.BlockSpec(memory_space=pl.ANY)


def _matmul(a, b, *, dn, grid, a_spec, b_spec, o_spec, out_shape, out_dtype, acc_shape, name, dep=None):
    nk = grid[2]
    deps = [] if dep is None else [dep]

    def body(a_ref, b_ref, *rest):
        o_ref, acc = rest[len(deps)], rest[len(deps) + 1:]
        if nk == 1:
            o_ref[...] = _dot(a_ref[...], b_ref[...], dn).astype(o_ref.dtype)
            return
        acc_ref = acc[0]
        k = pl.program_id(2)

        @pl.when(k == 0)
        def _():
            acc_ref[...] = jnp.zeros_like(acc_ref)

        acc_ref[...] += _dot(a_ref[...], b_ref[...], dn)

        @pl.when(k == nk - 1)
        def _():
            o_ref[...] = acc_ref[...].astype(o_ref.dtype)

    return pl.pallas_call(
        body, grid=grid, in_specs=[a_spec, b_spec] + [ANY_SPEC] * len(deps), out_specs=o_spec,
        out_shape=jax.ShapeDtypeStruct(out_shape, out_dtype),
        scratch_shapes=[] if nk == 1 else [pltpu.VMEM(acc_shape, F32)],
        compiler_params=_params(("parallel", "parallel", "arbitrary"), VMEM_BIG), name=name,
    )(a, b, *deps)


TM = 2048


def _mm_in(h, wg, name):
    nb = wg.shape[2]
    tn = 512 if nb % 512 == 0 else nb
    per = nb // tn
    return _matmul(
        h, wg, dn=NN, grid=(S // TM, NDEV * per, 1),
        a_spec=pl.BlockSpec((TM, D), lambda i, j, k: (i, 0)),
        b_spec=pl.BlockSpec((None, D, tn), lambda i, j, k: (j // per, 0, j % per)),
        o_spec=pl.BlockSpec((TM, tn), lambda i, j, k: (i, j)),
        out_shape=(S, NDEV * nb), out_dtype=F32, acc_shape=(TM, tn), name=name)


def _mm_in_dx(dz, wg, name, dep=None):
    nb = wg.shape[2]
    return _matmul(
        dz, wg, dn=NT, grid=(S // TM, D // 1024, NDEV),
        a_spec=pl.BlockSpec((TM, nb), lambda i, j, k: (i, k)),
        b_spec=pl.BlockSpec((None, 1024, nb), lambda i, j, k: (k, j, 0)),
        o_spec=pl.BlockSpec((TM, 1024), lambda i, j, k: (i, j)),
        out_shape=(S, D), out_dtype=F32, acc_shape=(TM, 1024), name=name, dep=dep)


def _mm_in_dw(h, dz, nb, name, dep=None):
    tn = 512 if nb % 512 == 0 else nb
    per = nb // tn
    return _matmul(
        h, dz, dn=TN, grid=(D // TM, NDEV * per, 1),
        a_spec=pl.BlockSpec((S, TM), lambda i, j, k: (0, i)),
        b_spec=pl.BlockSpec((S, tn), lambda i, j, k: (0, j)),
        o_spec=pl.BlockSpec((None, TM, tn), lambda i, j, k: (j // per, i, j % per)),
        out_shape=(NDEV, D, nb), out_dtype=BF16, acc_shape=(TM, tn), name=name, dep=dep)


def _mm_out(yc, w, name):
    return _matmul(
        yc, w, dn=NN, grid=(S // TM, D // 512, 1),
        a_spec=pl.BlockSpec((TM, 2048), lambda i, j, k: (i, 0)),
        b_spec=pl.BlockSpec((2048, 512), lambda i, j, k: (0, j)),
        o_spec=pl.BlockSpec((TM, 512), lambda i, j, k: (i, j)),
        out_shape=(S, D), out_dtype=F32, acc_shape=(TM, 512), name=name)


def _mm_out_dx(dy, w, name, dep=None):
    return _matmul(
        dy, w, dn=NT, grid=(S // TM, 2048 // 512, 1),
        a_spec=pl.BlockSpec((TM, D), lambda i, j, k: (i, 0)),
        b_spec=pl.BlockSpec((512, D), lambda i, j, k: (j, 0)),
        o_spec=pl.BlockSpec((TM, 512), lambda i, j, k: (i, j)),
        out_shape=(S, 2048), out_dtype=F32, acc_shape=(TM, 512), name=name, dep=dep)


def _mm_out_dw(yc, dy, name):
    return _matmul(
        yc, dy, dn=TN, grid=(2048 // TM, D // 512, 1),
        a_spec=pl.BlockSpec((S, TM), lambda i, j, k: (0, i)),
        b_spec=pl.BlockSpec((S, 512), lambda i, j, k: (0, j)),
        o_spec=pl.BlockSpec((TM, 512), lambda i, j, k: (i, j)),
        out_shape=(2048, D), out_dtype=BF16, acc_shape=(TM, 512), name=name)


def _row_spec(w=D):
    return pl.BlockSpec((TR, w), lambda i: (i, 0))


def _vec_spec(w=D):
    return pl.BlockSpec((1, w), lambda i: (0, 0))


def _rms_stats(x):
    r = lax.rsqrt(jnp.mean(x * x, axis=-1, keepdims=True) + EPS)
    return x * r, r


def _rms_bwd(dn, xhat, r, g):
    dxh = dn * g
    return r * (dxh - xhat * jnp.mean(dxh * xhat, axis=-1, keepdims=True))


def _acc_rows(ref, val, i):
    s = jnp.sum(val, axis=0, keepdims=True)

    @pl.when(i == 0)
    def _():
        ref[...] = s

    @pl.when(i > 0)
    def _():
        ref[...] += s


def _pre0_fwd(x, g, dep=None):
    deps = [] if dep is None else [dep]

    def body(x_ref, g_ref, *rest):
        xhat, _ = _rms_stats(x_ref[...])
        rest[-1][...] = (xhat * g_ref[...]).astype(BF16)

    return pl.pallas_call(
        body, grid=(S // TR,), in_specs=[_row_spec(), _vec_spec()] + [ANY_SPEC] * len(deps), out_specs=_row_spec(),
        out_shape=jax.ShapeDtypeStruct((S, D), BF16), compiler_params=_params(("parallel",)), name="pre0_fwd",
    )(x, g, *deps)


def _post0_fwd(x, y0, g_post, g_pre1):
    def body(x_ref, y_ref, gp_ref, g1_ref, x1_ref, h1_ref):
        yhat, _ = _rms_stats(y_ref[...])
        x1 = x_ref[...] + yhat * gp_ref[...]
        x1_ref[...] = x1
        xhat, _ = _rms_stats(x1)
        h1_ref[...] = (xhat * g1_ref[...]).astype(BF16)

    return pl.pallas_call(
        body, grid=(S // TR,), in_specs=[_row_spec(), _row_spec(), _vec_spec(), _vec_spec()],
        out_specs=[_row_spec(), _row_spec()],
        out_shape=[jax.ShapeDtypeStruct((S, D), F32), jax.ShapeDtypeStruct((S, D), BF16)],
        compiler_params=_params(("parallel",)), name="post0_fwd",
    )(x, y0, g_post, g_pre1)


def _post1_bwd(y1, x1, target, g_post):
    def body(y_ref, x1_ref, t_ref, g_ref, loss_ref, dx2_ref, dy_ref, dg_ref):
        i = pl.program_id(0)
        yhat, r = _rms_stats(y_ref[...])
        g = g_ref[...]
        err = x1_ref[...] + yhat * g - t_ref[...]
        part = jnp.sum(jnp.sum(err * err, axis=-1, keepdims=True), axis=0, keepdims=True) * (0.5 / D)
        _acc_rows(loss_ref, jnp.broadcast_to(part, (1, 128)), i)
        dx2 = err * (1.0 / D)
        dx2_ref[...] = dx2
        _acc_rows(dg_ref, dx2 * yhat, i)
        dy_ref[...] = _rms_bwd(dx2, yhat, r, g).astype(BF16)

    return pl.pallas_call(
        body, grid=(S // TR,), in_specs=[_row_spec(), _row_spec(), _row_spec(), _vec_spec()],
        out_specs=[_vec_spec(128), _row_spec(), _row_spec(), _vec_spec()],
        out_shape=[jax.ShapeDtypeStruct((1, 128), F32), jax.ShapeDtypeStruct((S, D), F32),
                   jax.ShapeDtypeStruct((S, D), BF16), jax.ShapeDtypeStruct((1, D), F32)],
        compiler_params=_params(("arbitrary",)), name="post1_bwd",
    )(y1, x1, target, g_post)


def _mid_bwd(dx2, dh1, x1, y0, g_pre1, g_post0):
    def body(dx2_ref, dh_ref, x1_ref, y_ref, g1_ref, gp_ref, dx1_ref, dy_ref, dg1_ref, dgp_ref):
        i = pl.program_id(0)
        xhat, r1 = _rms_stats(x1_ref[...])
        dh = dh_ref[...]
        _acc_rows(dg1_ref, dh * xhat, i)
        dx1 = dx2_ref[...] + _rms_bwd(dh, xhat, r1, g1_ref[...])
        dx1_ref[...] = dx1
        yhat, r0 = _rms_stats(y_ref[...])
        _acc_rows(dgp_ref, dx1 * yhat, i)
        dy_ref[...] = _rms_bwd(dx1, yhat, r0, gp_ref[...]).astype(BF16)

    return pl.pallas_call(
        body, grid=(S // TR,),
        in_specs=[_row_spec(), _row_spec(), _row_spec(), _row_spec(), _vec_spec(), _vec_spec()],
        out_specs=[_row_spec(), _row_spec(), _vec_spec(), _vec_spec()],
        out_shape=[jax.ShapeDtypeStruct((S, D), F32), jax.ShapeDtypeStruct((S, D), BF16),
                   jax.ShapeDtypeStruct((1, D), F32), jax.ShapeDtypeStruct((1, D), F32)],
        compiler_params=_params(("arbitrary",)), name="mid_bwd",
    )(dx2, dh1, x1, y0, g_pre1, g_post0)


def _pre0_bwd(dx1, dh0, x, g):
    def body(dx1_ref, dh_ref, x_ref, g_ref, gx_ref, dg_ref):
        i = pl.program_id(0)
        xhat, r = _rms_stats(x_ref[...])
        dh = dh_ref[...]
        _acc_rows(dg_ref, dh * xhat, i)
        gx_ref[...] = dx1_ref[...] + _rms_bwd(dh, xhat, r, g_ref[...])

    return pl.pallas_call(
        body, grid=(S // TR,), in_specs=[_row_spec(), _row_spec(), _row_spec(), _vec_spec()],
        out_specs=[_row_spec(), _vec_spec()],
        out_shape=[jax.ShapeDtypeStruct((S, D), F32), jax.ShapeDtypeStruct((1, D), F32)],
        compiler_params=_params(("arbitrary",)), name="pre0_bwd",
    )(dx1, dh0, x, g)


POOL_CH = 256


def _pool_apply(a, w, transpose):
    n = a.shape[0]
    row = lax.broadcasted_iota(jnp.int32, a.shape, 0)
    cnt = jnp.minimum(row + 1, w).astype(F32)
    s = a / cnt if transpose else a
    for k in (1, 2, 4, 8):
        if transpose:
            sh = jnp.where(row < n - k, pltpu.roll(s, n - k, 0), 0.0)
        else:
            sh = jnp.where(row >= k, pltpu.roll(s, k, 0), 0.0)
        s = jnp.where(w > k, s + sh, s)
    return s - a if transpose else s / cnt - a


def _pool_fwd(z0, pool_w, pool_scale):
    def body(a_ref, gate_ref, w_ref, sc_ref, out_ref):
        win = jnp.left_shift(2, pl.program_id(0))
        pooled = _pool_apply(a_ref[...], win, False)
        mixed = _dot(pooled.astype(BF16), w_ref[...])
        gate = gate_ref[...]
        out_ref[...] = (mixed * sc_ref[...] * (gate * _sigmoid(gate))).astype(BF16)

    return pl.pallas_call(
        body, grid=(4,),
        in_specs=[pl.BlockSpec((S, POOL_CH), lambda g: (0, g)), pl.BlockSpec((S, POOL_CH), lambda g: (0, 4 + g)),
                  pl.BlockSpec((None, POOL_CH, POOL_CH), lambda g: (g, 0, 0)),
                  pl.BlockSpec((1, POOL_CH), lambda g: (0, g))],
        out_specs=pl.BlockSpec((S, POOL_CH), lambda g: (0, g)),
        out_shape=jax.ShapeDtypeStruct((S, 2048), BF16),
        compiler_params=_params(("parallel",), VMEM_BIG), name="pool_fwd",
    )(z0, z0, pool_w, pool_scale)


def _pool_bwd(z0, dycat, pool_w, pool_scale):
    def body(a_ref, gate_ref, dy_ref, w_ref, sc_ref, da_ref, dgate_ref, dw_ref, dsc_ref):
        win = jnp.left_shift(2, pl.program_id(0))
        pooled = _pool_apply(a_ref[...], win, False).astype(BF16)
        w = w_ref[...]
        mixed = _dot(pooled, w)
        silu, dsilu = _silu_and_grad(gate_ref[...])
        dy = dy_ref[...]
        sc = sc_ref[...]
        dgate_ref[...] = (dy * (mixed * sc) * dsilu).astype(BF16)
        dms = dy * silu
        dsc_ref[...] = jnp.sum(dms * mixed, axis=0, keepdims=True)
        dmixed = (dms * sc).astype(BF16)
        dw_ref[...] = _dot(pooled, dmixed, TN)
        dpooled = _dot(dmixed, w, NT)
        da_ref[...] = _pool_apply(dpooled, win, True).astype(BF16)

    slab = lambda off: pl.BlockSpec((S, POOL_CH), lambda g: (0, off + g))
    return pl.pallas_call(
        body, grid=(4,),
        in_specs=[slab(0), slab(4), slab(0), pl.BlockSpec((None, POOL_CH, POOL_CH), lambda g: (g, 0, 0)),
                  pl.BlockSpec((1, POOL_CH), lambda g: (0, g))],
        out_specs=[slab(0), slab(0), pl.BlockSpec((None, POOL_CH, POOL_CH), lambda g: (g, 0, 0)),
                   pl.BlockSpec((1, POOL_CH), lambda g: (0, g))],
        out_shape=[jax.ShapeDtypeStruct((S, HALF), BF16), jax.ShapeDtypeStruct((S, HALF), BF16),
                   jax.ShapeDtypeStruct((4, POOL_CH, POOL_CH), F32), jax.ShapeDtypeStruct((1, HALF), F32)],
        compiler_params=_params(("parallel",), VMEM_BIG), name="pool_bwd",
    )(z0, z0, dycat, pool_w, pool_scale)


Q_COL, K_COL, V_COL, BG_COL = 2048 // 128, 5120 // 128, 8192 // 128, 11264 // 128
SCALE = HEAD_DIM ** -0.5


def _rope_tables():
    pos = jnp.arange(S, dtype=F32)
    inv_freq = jnp.power(ROPE_THETA, -jnp.arange(0, ROT_DIM, 2, dtype=F32) / ROT_DIM)
    ang = pos[:, None] * inv_freq[None, :]
    cos, sin = jnp.cos(ang), jnp.sin(ang)
    half = ROT_DIM // 2
    zeros = jnp.zeros((S, HEAD_DIM - ROT_DIM), F32)
    c = jnp.concatenate([cos, cos, jnp.ones((S, HEAD_DIM - ROT_DIM), F32)], axis=1)
    a = jnp.concatenate([-sin, jnp.zeros((S, half), F32), zeros], axis=1)
    b = jnp.concatenate([jnp.zeros((S, half), F32), sin, zeros], axis=1)
    return c, a, b


def _rope(t, c, a, b):
    half = ROT_DIM // 2
    return t * c + pltpu.roll(t, HEAD_DIM - half, 1) * a + pltpu.roll(t, half, 1) * b


def _rope_t(d, c, a, b):
    half = ROT_DIM // 2
    return d * c + pltpu.roll(d * a, half, 1) + pltpu.roll(d * b, HEAD_DIM - half, 1)


def _deinterleave(dst, src, dil, cast=None, dst_off=0):
    length = S // dil
    for r in range(dil):
        v = src[...] if dil == 1 else src[pl.ds(r, length, stride=dil), :]
        dst[dst_off + r * length:dst_off + (r + 1) * length, :] = v if cast is None else v.astype(cast)


def _interleave(dst, src, dil, src_off=0):
    length = S // dil
    for r in range(dil):
        if dil == 1:
            dst[...] = src[src_off:src_off + S, :]
        else:
            dst[pl.ds(r, length, stride=dil), :] = src[src_off + r * length:src_off + (r + 1) * length, :]


CU = 4
NUNITS = S // BLK
B_QK = (((2,), (2,)), ((0,), (0,)))
B_PV = (((2,), (1,)), ((0,), (0,)))
B_TN = (((1,), (1,)), ((0,), (0,)))


def _blocks(ref, first):
    return ref[first * BLK:(first + CU) * BLK, :].reshape(CU, BLK, HEAD_DIM)


def _chunk_scores(u0, nb, qd, kdp):
    q = _blocks(qd, u0)
    row = lax.broadcasted_iota(jnp.int32, (CU, BLK, BLK), 1)
    col = lax.broadcasted_iota(jnp.int32, (CU, BLK, BLK), 2)
    s_own = jnp.where(col <= row, _dot(q, _blocks(kdp, u0 + 1), B_QK) * SCALE, NEG)
    if nb == 1:
        return q, s_own, None
    unit = lax.broadcasted_iota(jnp.int32, (CU, BLK, BLK), 0) + u0
    s_prev = jnp.where((col >= row) & ((unit % nb) != 0), _dot(q, _blocks(kdp, u0), B_QK) * SCALE, NEG)
    return q, s_own, s_prev


def _attn_prepare(dil, q_ref, k_ref, v_ref, tabs, tmp, qd, kdp, vdp):
    c, a, b = tabs
    tmp[...] = _rope(q_ref[...], c, a, b)
    _deinterleave(qd, tmp, dil, BF16)
    tmp[...] = _rope(k_ref[...], c, a, b)
    _deinterleave(kdp, tmp, dil, BF16, BLK)
    _deinterleave(vdp, v_ref, dil, BF16, BLK)
    kdp[0:BLK, :] = jnp.zeros((BLK, HEAD_DIM), BF16)
    vdp[0:BLK, :] = jnp.zeros((BLK, HEAD_DIM), BF16)


def _attn_group_fwd(dil, q_ref, k_ref, v_ref, tabs, tmp, qd, kdp, vdp, od, ld, og, lg):
    nb = S // dil // BLK
    _attn_prepare(dil, q_ref, k_ref, v_ref, tabs, tmp, qd, kdp, vdp)
    for u0 in range(0, NUNITS, CU):
        _, s_own, s_prev = _chunk_scores(u0, nb, qd, kdp)
        m = jnp.max(s_own, axis=2, keepdims=True)
        if s_prev is not None:
            m = jnp.maximum(m, jnp.max(s_prev, axis=2, keepdims=True))
        p_own = jnp.exp(s_own - m)
        den = jnp.sum(p_own, axis=2, keepdims=True)
        acc = _dot(p_own.astype(BF16), _blocks(vdp, u0 + 1), B_PV)
        if s_prev is not None:
            p_prev = jnp.exp(s_prev - m)
            den = den + jnp.sum(p_prev, axis=2, keepdims=True)
            acc = acc + _dot(p_prev.astype(BF16), _blocks(vdp, u0), B_PV)
        rows = slice(u0 * BLK, (u0 + CU) * BLK)
        od[rows, :] = (acc / den).reshape(CU * BLK, HEAD_DIM)
        ld[rows, :] = jnp.broadcast_to(m + jnp.log(den), (CU, BLK, HEAD_DIM)).reshape(CU * BLK, HEAD_DIM)
    _interleave(og, od, dil)
    _interleave(lg, ld, dil)


def _group_weights(lgs):
    l0, l1, l2 = lgs[0][...], lgs[1][...], lgs[2][...]
    mx = jnp.maximum(l0, jnp.maximum(l1, l2))
    e0, e1, e2 = jnp.exp(l0 - mx), jnp.exp(l1 - mx), jnp.exp(l2 - mx)
    den = e0 + e1 + e2
    return e0 / den, e1 / den, e2 / den


def _head_spec(base, ngroups_axis=True):
    return pl.BlockSpec((S, HEAD_DIM), lambda h, p: (0, base + (p % 3) * 8 + h))


ATTN_SCRATCH_FWD = [
    pltpu.VMEM((S, HEAD_DIM), F32),
    pltpu.VMEM((S, HEAD_DIM), BF16), pltpu.VMEM((S + BLK, HEAD_DIM), BF16), pltpu.VMEM((S + BLK, HEAD_DIM), BF16),
    pltpu.VMEM((S, HEAD_DIM), F32), pltpu.VMEM((S, HEAD_DIM), F32),
    pltpu.VMEM((S, HEAD_DIM), F32), pltpu.VMEM((S, HEAD_DIM), F32), pltpu.VMEM((S, HEAD_DIM), F32),
    pltpu.VMEM((S, HEAD_DIM), F32), pltpu.VMEM((S, HEAD_DIM), F32), pltpu.VMEM((S, HEAD_DIM), F32),
]


def _attn_fwd(z0, ycat, tabs):
    def body(q_ref, k_ref, v_ref, gate_ref, c_ref, a_ref, b_ref, ycat_ref, out_ref,
             tmp, qd, kd, vd, od, ld, og0, og1, og2, lg0, lg1, lg2):
        del ycat_ref
        p = pl.program_id(1)
        ogs, lgs = (og0, og1, og2), (lg0, lg1, lg2)
        tabs_v = (c_ref[...], a_ref[...], b_ref[...])
        for gi, (_, dil) in enumerate(PATTERNS):
            @pl.when(p == gi)
            def _(gi=gi, dil=dil):
                _attn_group_fwd(dil, q_ref, k_ref, v_ref, tabs_v, tmp, qd, kd, vd, od, ld, ogs[gi], lgs[gi])

        @pl.when(p == 2)
        def _():
            w0, w1, w2 = _group_weights(lgs)
            o = w0 * og0[...] + w1 * og1[...] + w2 * og2[...]
            gate = gate_ref[...]
            out_ref[...] = (o * (gate * _sigmoid(gate))).astype(BF16)

    tab = pl.BlockSpec((S, HEAD_DIM), lambda h, p: (0, 0))
    return pl.pallas_call(
        body, grid=(8, 3),
        in_specs=[_head_spec(Q_COL), _head_spec(K_COL), _head_spec(V_COL),
                  pl.BlockSpec((S, HEAD_DIM), lambda h, p: (0, BG_COL + h)), tab, tab, tab,
                  pl.BlockSpec(memory_space=pl.ANY)],
        out_specs=pl.BlockSpec((S, HEAD_DIM), lambda h, p: (0, 8 + h)),
        out_shape=jax.ShapeDtypeStruct((S, 2048), BF16),
        scratch_shapes=ATTN_SCRATCH_FWD, input_output_aliases={7: 0},
        compiler_params=_params(("parallel", "arbitrary"), VMEM_BIG), name="attn_fwd",
    )(z0, z0, z0, z0, *tabs, ycat)


def _attn_bwd(z0, dycat, tabs):
    def body(q_ref, k_ref, v_ref, gate_ref, dy_ref, c_ref, a_ref, b_ref,
             dq_ref, dk_ref, dv_ref, dbg_ref,
             tmp, qd, kd, vd, od, ld, og0, og1, og2, lg0, lg1, lg2, cg0, cg1, cg2, dod, cd, dqd, dkd, dvd):
        p = pl.program_id(1)
        ogs, lgs, cgs = (og0, og1, og2), (lg0, lg1, lg2), (cg0, cg1, cg2)
        tabs_v = (c_ref[...], a_ref[...], b_ref[...])
        for gi, (_, dil) in enumerate(PATTERNS):
            @pl.when(p == gi)
            def _(gi=gi, dil=dil):
                _attn_group_fwd(dil, q_ref, k_ref, v_ref, tabs_v, tmp, qd, kd, vd, od, ld, ogs[gi], lgs[gi])

        @pl.when(p == 2)
        def _():
            w = _group_weights(lgs)
            o = w[0] * og0[...] + w[1] * og1[...] + w[2] * og2[...]
            silu, dsilu = _silu_and_grad(gate_ref[...])
            dy = dy_ref[...]
            dbg_ref[...] = (dy * o * dsilu).astype(BF16)
            do = dy * silu
            dwbar = jnp.sum(do * o, axis=1, keepdims=True)
            for gi in range(3):
                ogs[gi][...] = w[gi] * do
                cgs[gi][...] = -w[gi] * dwbar

        for gi, (_, dil) in enumerate(PATTERNS):
            @pl.when(p == 3 + gi)
            def _(gi=gi, dil=dil):
                nb = S // dil // BLK
                c, a, b = tabs_v
                _attn_prepare(dil, q_ref, k_ref, v_ref, tabs_v, tmp, qd, kd, vd)
                _deinterleave(dod, ogs[gi], dil, BF16)
                _deinterleave(ld, lgs[gi], dil)
                _deinterleave(cd, cgs[gi], dil)
                dkd[...] = jnp.zeros_like(dkd)
                dvd[...] = jnp.zeros_like(dvd)
                flat = lambda t: t.reshape(CU * BLK, HEAD_DIM)
                for u0 in range(0, NUNITS, CU):
                    q, s_own, s_prev = _chunk_scores(u0, nb, qd, kd)
                    lse, cv, do = _blocks(ld, u0), _blocks(cd, u0), _blocks(dod, u0)
                    own = slice((u0 + 1) * BLK, (u0 + 1 + CU) * BLK)
                    p_own = jnp.exp(s_own - lse)
                    ds_own = (p_own * (_dot(do, _blocks(vd, u0 + 1), B_QK) + cv) * SCALE).astype(BF16)
                    dq = _dot(ds_own, _blocks(kd, u0 + 1), B_PV)
                    dkd[own, :] += flat(_dot(ds_own, q, B_TN))
                    dvd[own, :] += flat(_dot(p_own.astype(BF16), do, B_TN))
                    if s_prev is not None:
                        prev = slice(u0 * BLK, (u0 + CU) * BLK)
                        p_prev = jnp.exp(s_prev - lse)
                        ds_prev = (p_prev * (_dot(do, _blocks(vd, u0), B_QK) + cv) * SCALE).astype(BF16)
                        dq = dq + _dot(ds_prev, _blocks(kd, u0), B_PV)
                        dkd[prev, :] += flat(_dot(ds_prev, q, B_TN))
                        dvd[prev, :] += flat(_dot(p_prev.astype(BF16), do, B_TN))
                    dqd[u0 * BLK:(u0 + CU) * BLK, :] = flat(dq)
                _interleave(tmp, dqd, dil)
                dq_ref[...] = _rope_t(tmp[...], c, a, b).astype(BF16)
                _interleave(tmp, dkd, dil, BLK)
                dk_ref[...] = _rope_t(tmp[...], c, a, b).astype(BF16)
                _interleave(tmp, dvd, dil, BLK)
                dv_ref[...] = tmp[...].astype(BF16)

    tab = pl.BlockSpec((S, HEAD_DIM), lambda h, p: (0, 0))
    hspec = lambda base: pl.BlockSpec((S, HEAD_DIM), lambda h, p: (0, base + h))
    gspec = pl.BlockSpec((S, HEAD_DIM), lambda h, p: (0, jnp.maximum(p - 3, 0) * 8 + h))
    slab = lambda: pltpu.VMEM((S, HEAD_DIM), F32)
    return pl.pallas_call(
        body, grid=(8, 6),
        in_specs=[_head_spec(Q_COL), _head_spec(K_COL), _head_spec(V_COL), hspec(BG_COL), hspec(8), tab, tab, tab],
        out_specs=[gspec, gspec, gspec, hspec(0)],
        out_shape=[jax.ShapeDtypeStruct((S, 3072), BF16)] * 3 + [jax.ShapeDtypeStruct((S, HALF), BF16)],
        scratch_shapes=ATTN_SCRATCH_FWD + [slab(), slab(), slab(), pltpu.VMEM((S, HEAD_DIM), BF16), slab(), slab(),
                                           pltpu.VMEM((S + BLK, HEAD_DIM), F32), pltpu.VMEM((S + BLK, HEAD_DIM), F32)],
        compiler_params=_params(("parallel", "arbitrary"), VMEM_BIG), name="attn_bwd",
    )(z0, z0, z0, z0, dycat, *tabs)


SGU_CH = 256
NCHUNK = TR // 128


def _ln_stats(x):
    mu = jnp.mean(x, axis=-1, keepdims=True)
    xc = x - mu
    r = lax.rsqrt(jnp.mean(xc * xc, axis=-1, keepdims=True) + EPS)
    return xc * r, r


def _ln_bwd(dy, xhat, r, g):
    dxh = dy * g
    return r * (dxh - jnp.mean(dxh, axis=-1, keepdims=True) - xhat * jnp.mean(dxh * xhat, axis=-1, keepdims=True))


def _tril_bf16(w):
    row = lax.broadcasted_iota(jnp.int32, w.shape, 0)
    col = lax.broadcasted_iota(jnp.int32, w.shape, 1)
    return jnp.where(row >= col, w, 0.0).astype(BF16)


def _sgu_gate(vn_s, s_s, w_ref, bb_ref):
    for h in range(4):
        wm = _tril_bf16(w_ref[h])
        bias = bb_ref[h]
        for ch in range(NCHUNK):
            rows, cols = slice(ch * 128, (ch + 1) * 128), slice(h * SGU_CH, (h + 1) * SGU_CH)
            s_s[rows, cols] = _dot(wm, vn_s[rows, cols]) + jnp.concatenate([bias, bias], axis=1)


def _conv_fwd(i, dval_ref, dglu_ref, hval_ref, hglu_ref, cw_ref, cb_ref, xw, dcs):
    halo = hval_ref[...] * _sigmoid(hglu_ref[...])
    xw[0:HALO, :] = jnp.where(i > 0, halo, 0.0)
    xw[HALO:HALO + TR, :] = dval_ref[...] * _sigmoid(dglu_ref[...])
    for rb in range(TR // SUB):
        acc = jnp.broadcast_to(cb_ref[...], (SUB, HALF))
        for k in range(CONV_K):
            acc = acc + cw_ref[k:k + 1, :] * xw[pl.ds(rb * SUB + HALO - (CONV_K - 1) + k, SUB), :]
        dcs[rb * SUB:(rb + 1) * SUB, :] = acc


def _odd_in_specs():
    col = lambda j: pl.BlockSpec((TR, HALF), lambda i, *_: (i, j))
    prev = lambda j: pl.BlockSpec((HALO, HALF), lambda i, *_: (jnp.maximum(i * (TR // HALO) - 1, 0), j))
    return [col(0), col(1), col(2), col(3), col(4), col(5), prev(3), prev(4)]


def _full_spec(shape):
    return pl.BlockSpec(shape, lambda i, *_: (0,) * len(shape))


def _odd_fwd(z1, sgu_g, sgu_b, sgu_w, sgu_bb, conv_w, conv_b, cn_g, cn_b):
    def body(u_ref, v_ref, cg_ref, dval_ref, dglu_ref, dgate_ref, hval_ref, hglu_ref,
             g_ref, b_ref, w_ref, bb_ref, cw_ref, cb_ref, cng_ref, cnb_ref, out_ref, vn_s, s_s, xw, dcs):
        i = pl.program_id(0)
        vhat, _ = _ln_stats(v_ref[...])
        vn_s[...] = (vhat * g_ref[...] + b_ref[...]).astype(BF16)
        _sgu_gate(vn_s, s_s, w_ref, bb_ref)
        cg = cg_ref[...]
        out_ref[:, 0:HALF] = (u_ref[...] * s_s[...] * (cg * _sigmoid(cg))).astype(BF16)
        _conv_fwd(i, dval_ref, dglu_ref, hval_ref, hglu_ref, cw_ref, cb_ref, xw, dcs)
        dhat, _ = _ln_stats(dcs[...])
        dn = dhat * cng_ref[...] + cnb_ref[...]
        dgate = dgate_ref[...]
        out_ref[:, HALF:2 * HALF] = ((dn * _sigmoid(dn)) * (dgate * _sigmoid(dgate))).astype(BF16)

    vec = _full_spec((1, HALF))
    return pl.pallas_call(
        body, grid=(S // TR,),
        in_specs=_odd_in_specs() + [vec, vec, _full_spec((4, 128, 128)), _full_spec((4, 128, 128)),
                                    _full_spec((HALO, HALF)), vec, vec, vec],
        out_specs=pl.BlockSpec((TR, 2048), lambda i: (i, 0)),
        out_shape=jax.ShapeDtypeStruct((S, 2048), BF16),
        scratch_shapes=[pltpu.VMEM((TR, HALF), BF16), pltpu.VMEM((TR, HALF), F32),
                        pltpu.VMEM((HALO + TR, HALF), F32), pltpu.VMEM((TR, HALF), F32)],
        compiler_params=_params(("parallel",), VMEM_BIG), name="odd_fwd",
    )(z1, z1, z1, z1, z1, z1, z1, z1, sgu_g, sgu_b, sgu_w, sgu_bb, conv_w, conv_b, cn_g, cn_b)


def _odd_bwd_a(z1, dycat, sgu_g, sgu_b, sgu_w, sgu_bb, conv_w, conv_b, cn_g, cn_b):
    def body(u_ref, v_ref, cg_ref, dval_ref, dglu_ref, dgate_ref, hval_ref, hglu_ref, dy_ref,
             g_ref, b_ref, w_ref, bb_ref, cw_ref, cb_ref, cng_ref, cnb_ref,
             dz_ref, ddc_ref, dw_ref, dbb_ref, dg_ref, db_ref, dcng_ref, dcnb_ref, dcb_ref,
             vn_s, s_s, xw, dcs, ds_s, dvn_s):
        i = pl.program_id(0)
        vhat, rv = _ln_stats(v_ref[...])
        g = g_ref[...]
        vn_s[...] = (vhat * g + b_ref[...]).astype(BF16)
        _sgu_gate(vn_s, s_s, w_ref, bb_ref)
        silu_c, dsilu_c = _silu_and_grad(cg_ref[...])
        dyc = dy_ref[:, 0:HALF]
        u = u_ref[...]
        s = s_s[...]
        dz_ref[:, 0:HALF] = (dyc * s * silu_c).astype(BF16)
        dz_ref[:, 2 * HALF:3 * HALF] = (dyc * u * s * dsilu_c).astype(BF16)
        ds_s[...] = dyc * u * silu_c

        @pl.when(i == 0)
        def _():
            dw_ref[...] = jnp.zeros_like(dw_ref)
            dbb_ref[...] = jnp.zeros_like(dbb_ref)

        tril = lax.broadcasted_iota(jnp.int32, (128, 128), 0) >= lax.broadcasted_iota(jnp.int32, (128, 128), 1)
        for h in range(4):
            wm = _tril_bf16(w_ref[h])
            for ch in range(NCHUNK):
                rows, cols = slice(ch * 128, (ch + 1) * 128), slice(h * SGU_CH, (h + 1) * SGU_CH)
                ds = ds_s[rows, cols]
                dsb = ds.astype(BF16)
                dw_ref[h] += jnp.where(tril, _dot(dsb, vn_s[rows, cols], NT), 0.0)
                dbb_ref[h] += jnp.broadcast_to(jnp.sum(ds, axis=1, keepdims=True), (128, 128))
                dvn_s[rows, cols] = _dot(wm, dsb, TN)
        dvn = dvn_s[...]
        _acc_rows(dg_ref, dvn * vhat, i)
        _acc_rows(db_ref, dvn, i)
        dz_ref[:, HALF:2 * HALF] = _ln_bwd(dvn, vhat, rv, g).astype(BF16)

        _conv_fwd(i, dval_ref, dglu_ref, hval_ref, hglu_ref, cw_ref, cb_ref, xw, dcs)
        dhat, rd = _ln_stats(dcs[...])
        cng = cng_ref[...]
        silu_n, dsilu_n = _silu_and_grad(dhat * cng + cnb_ref[...])
        silu_g, dsilu_g = _silu_and_grad(dgate_ref[...])
        dyd = dy_ref[:, HALF:2 * HALF]
        dz_ref[:, 5 * HALF:6 * HALF] = (dyd * silu_n * dsilu_g).astype(BF16)
        ddn = dyd * silu_g * dsilu_n
        _acc_rows(dcng_ref, ddn * dhat, i)
        _acc_rows(dcnb_ref, ddn, i)
        ddc = _ln_bwd(ddn, dhat, rd, cng)
        ddc_ref[...] = ddc
        _acc_rows(dcb_ref, ddc, i)

    vec = _full_spec((1, HALF))
    sq = _full_spec((4, 128, 128))
    return pl.pallas_call(
        body, grid=(S // TR,),
        in_specs=_odd_in_specs() + [pl.BlockSpec((TR, 2048), lambda i: (i, 0)),
                                    vec, vec, sq, sq, _full_spec((HALO, HALF)), vec, vec, vec],
        out_specs=[pl.BlockSpec((TR, ODD_IN), lambda i: (i, 0)), pl.BlockSpec((TR, HALF), lambda i: (i, 0)),
                   sq, sq, vec, vec, vec, vec, vec],
        out_shape=[jax.ShapeDtypeStruct((S, ODD_IN), BF16), jax.ShapeDtypeStruct((S, HALF), F32),
                   jax.ShapeDtypeStruct((4, 128, 128), F32), jax.ShapeDtypeStruct((4, 128, 128), F32)]
                  + [jax.ShapeDtypeStruct((1, HALF), F32)] * 5,
        scratch_shapes=[pltpu.VMEM((TR, HALF), BF16), pltpu.VMEM((TR, HALF), F32),
                        pltpu.VMEM((HALO + TR, HALF), F32), pltpu.VMEM((TR, HALF), F32),
                        pltpu.VMEM((TR, HALF), F32), pltpu.VMEM((TR, HALF), F32)],
        compiler_params=_params(("arbitrary",), VMEM_BIG), name="odd_bwd_a",
    )(z1, z1, z1, z1, z1, z1, z1, z1, dycat, sgu_g, sgu_b, sgu_w, sgu_bb, conv_w, conv_b, cn_g, cn_b)


def _odd_bwd_b(z1, ddc, dz1, conv_w):
    nt = S // TR

    def body(dval_ref, dglu_ref, hval_ref, hglu_ref, ddc_ref, hddc_ref, cw_ref, dz_in_ref,
             dz_ref, dcw_ref, xw, dwin, dxs):
        del dz_in_ref
        i, j = pl.program_id(0), pl.program_id(1)
        sg = _sigmoid(dglu_ref[...])
        dval = dval_ref[...]

        @pl.when(j == 0)
        def _():
            halo = hval_ref[...] * _sigmoid(hglu_ref[...])
            xw[0:HALO, :] = jnp.where(i > 0, halo, 0.0)
            xw[HALO:HALO + TR, :] = dval * sg
            dwin[0:TR, :] = ddc_ref[...]
            dwin[TR:TR + HALO, :] = jnp.where(i < nt - 1, hddc_ref[...], 0.0)

            @pl.when(i == 0)
            def _():
                dcw_ref[...] = jnp.zeros_like(dcw_ref)

            for rb in range(TR // SUB):
                acc = jnp.zeros((SUB, HALF), F32)
                for k in range(CONV_K):
                    acc = acc + cw_ref[k:k + 1, :] * dwin[pl.ds(rb * SUB + (CONV_K - 1) - k, SUB), :]
                dxs[rb * SUB:(rb + 1) * SUB, :] = acc
            for k in range(CONV_K):
                acc = jnp.zeros((SUB, HALF), F32)
                for rb in range(TR // SUB):
                    acc = acc + dwin[rb * SUB:(rb + 1) * SUB, :] * xw[pl.ds(rb * SUB + HALO - (CONV_K - 1) + k, SUB), :]
                dcw_ref[k:k + 1, :] += jnp.sum(acc, axis=0, keepdims=True)
            dz_ref[...] = (dxs[...] * sg).astype(BF16)

        @pl.when(j == 1)
        def _():
            dz_ref[...] = (dxs[...] * dval * sg * (1.0 - sg)).astype(BF16)

    col = lambda c: pl.BlockSpec((TR, HALF), lambda i, j: (i, c))
    prev = lambda c: pl.BlockSpec((HALO, HALF), lambda i, j: (jnp.maximum(i * (TR // HALO) - 1, 0), c))
    nxt = pl.BlockSpec((HALO, HALF), lambda i, j: (jnp.minimum((i + 1) * (TR // HALO), S // HALO - 1), 0))
    return pl.pallas_call(
        body, grid=(nt, 2),
        in_specs=[col(3), col(4), prev(3), prev(4), pl.BlockSpec((TR, HALF), lambda i, j: (i, 0)), nxt,
                  _full_spec((HALO, HALF)), pl.BlockSpec(memory_space=pl.ANY)],
        out_specs=[pl.BlockSpec((TR, HALF), lambda i, j: (i, 3 + j)), _full_spec((HALO, HALF))],
        out_shape=[jax.ShapeDtypeStruct((S, ODD_IN), BF16), jax.ShapeDtypeStruct((HALO, HALF), F32)],
        scratch_shapes=[pltpu.VMEM((HALO + TR, HALF), F32), pltpu.VMEM((TR + HALO, HALF), F32),
                        pltpu.VMEM((TR, HALF), F32)],
        input_output_aliases={7: 0},
        compiler_params=_params(("arbitrary", "arbitrary"), VMEM_BIG), name="odd_bwd_b",
    )(z1, z1, z1, z1, ddc, ddc, conv_w, dz1)


def _cast_bf16(w, name):
    r, c = w.shape
    tr = min(r, 256)
    def body(i_ref, o_ref):
        o_ref[...] = i_ref[...].astype(BF16)

    return pl.pallas_call(
        body, grid=(r // tr,), in_specs=[pl.BlockSpec((tr, c), lambda i: (i, 0))],
        out_specs=pl.BlockSpec((tr, c), lambda i: (i, 0)), out_shape=jax.ShapeDtypeStruct((r, c), BF16),
        compiler_params=_params(("parallel",)), name=name,
    )(w)


def _adamw(w, g, m, v):
    m = ADAM_B1 * m + (1.0 - ADAM_B1) * g
    v = ADAM_B2 * v + (1.0 - ADAM_B2) * (g * g)
    m_hat = m / (1.0 - ADAM_B1 ** ADAM_STEP)
    v_hat = v / (1.0 - ADAM_B2 ** ADAM_STEP)
    delta = -ADAM_LR * (m_hat / (jnp.sqrt(v_hat) + ADAM_EPS) + ADAM_WD * w)
    return delta, m, v


def _adam_reduce(parts, w, m, v, name, dep=None):
    r, c = w.shape
    tr = min(r, 128)
    deps = [] if dep is None else [dep]

    def body(p_ref, w_ref, m_ref, v_ref, *rest):
        g_ref, d_ref, nm_ref, nv_ref = rest[len(deps):]
        g = p_ref[0].astype(F32)
        for d in range(1, NDEV):
            g = g + p_ref[d].astype(F32)
        g_ref[...] = g
        d_ref[...], nm_ref[...], nv_ref[...] = _adamw(w_ref[...], g, m_ref[...], v_ref[...])

    spec = pl.BlockSpec((tr, c), lambda i: (i, 0))
    return pl.pallas_call(
        body, grid=(r // tr,),
        in_specs=[pl.BlockSpec((NDEV, tr, c), lambda i: (0, i, 0)), spec, spec, spec] + [ANY_SPEC] * len(deps),
        out_specs=[spec] * 4, out_shape=[jax.ShapeDtypeStruct((r, c), F32)] * 4,
        compiler_params=_params(("parallel",), VMEM_BIG), name=name,
    )(parts, w, m, v, *deps)


def _sum_parts(parts, name, dep=None):
    r = parts.shape[1]
    tr = 8
    for cand in (512, 256, 128, 64, 32, 16, 8):
        if r % cand == 0:
            tr = cand
            break
    deps = [] if dep is None else [dep]

    def body(p_ref, *rest):
        g = p_ref[0]
        for d in range(1, NDEV):
            g = g + p_ref[d]
        rest[-1][...] = g

    return pl.pallas_call(
        body, grid=(r // tr,), in_specs=[pl.BlockSpec((NDEV, tr, 128), lambda i: (0, i, 0))] + [ANY_SPEC] * len(deps),
        out_specs=pl.BlockSpec((tr, 128), lambda i: (i, 0)), out_shape=jax.ShapeDtypeStruct((r, 128), F32),
        compiler_params=_params(("parallel",)), name=name,
    )(parts, *deps)


def _adam_plain(w, g, m, v, name):
    r, c = w.shape

    def body(w_ref, g_ref, m_ref, v_ref, d_ref, nm_ref, nv_ref):
        d_ref[...], nm_ref[...], nv_ref[...] = _adamw(w_ref[...], g_ref[...], m_ref[...], v_ref[...])

    spec = pl.BlockSpec((r, c), lambda i: (0, 0))
    return pl.pallas_call(
        body, grid=(1,), in_specs=[spec] * 4, out_specs=[spec] * 3,
        out_shape=[jax.ShapeDtypeStruct((r, c), F32)] * 3,
        compiler_params=_params(("arbitrary",)), name=name,
    )(w, g, m, v)


MASKS = [(mx, my, mc) for mx in (0, 1) for my in (0, 1) for mc in (0, 1)][1:]


def _exchange(arrays, scatter, name):
    nt = len(arrays)
    out_shape = [jax.ShapeDtypeStruct(((NDEV,) + a.shape) if not scatter else a.shape, a.dtype) for a in arrays]

    def body(*refs):
        ins, outs = refs[:nt], refs[nt:2 * nt]
        send_sems, recv_sems, local_sems = refs[2 * nt:]
        x, y, c = lax.axis_index("x"), lax.axis_index("y"), lax.axis_index("c")
        me = 4 * x + 2 * y + c
        copies = []
        for t in range(nt):
            src_own = ins[t].at[me] if scatter else ins[t]
            loc = pltpu.make_async_copy(src_own, outs[t].at[me], local_sems.at[t])
            loc.start()
            copies.append(loc)
            for k, (mx, my, mc) in enumerate(MASKS):
                px, py, pc = (x + mx) % 2, (y + my) % 2, (c + mc) % 2
                peer = 4 * px + 2 * py + pc
                src = ins[t].at[peer] if scatter else ins[t]
                rc = pltpu.make_async_remote_copy(
                    src_ref=src, dst_ref=outs[t].at[me], send_sem=send_sems.at[t, k], recv_sem=recv_sems.at[t, k],
                    device_id=(px, py, pc), device_id_type=MESH)
                rc.start()
                copies.append(rc)
        for cp in copies:
            cp.wait()

    hbm = pl.BlockSpec(memory_space=pl.ANY)
    return pl.pallas_call(
        body, in_specs=[hbm] * nt, out_specs=[hbm] * nt, out_shape=out_shape,
        scratch_shapes=[pltpu.SemaphoreType.DMA((nt, 7)), pltpu.SemaphoreType.DMA((nt, 7)),
                        pltpu.SemaphoreType.DMA((nt,))],
        name=name,
    )(*arrays)


SEM_SPEC = pl.BlockSpec(memory_space=pltpu.SEMAPHORE)
EFFECT = pltpu.SideEffectType.DATAFLOW_SIDE_EFFECTING


def _direct_plan(scatter):
    def plan(x, y, c, srcs, lands):
        me = 4 * x + 2 * y + c
        local, remote = [], []
        for src, land in zip(srcs, lands):
            local.append((src.at[me] if scatter else src, land.at[me]))
            for mx, my, mc in MASKS:
                px, py, pc = (x + mx) % 2, (y + my) % 2, (c + mc) % 2
                blk = src.at[4 * px + 2 * py + pc] if scatter else src
                remote.append((blk, land.at[me], (px, py, pc)))
        return local, remote
    return plan


def _split_start(name, srcs, land_shapes, plan, n_local, n_remote, dep=None):
    ns, nl = len(srcs), len(land_shapes)
    deps = [] if dep is None else [dep]
    lands = [lax.empty(s.shape, s.dtype) for s in land_shapes]

    def body(*refs):
        ins, lz = refs[:ns], refs[ns:ns + nl]
        outs = refs[ns + nl + len(deps):]
        send_sems, recv_sems, token, local_sems = outs[0], outs[1], outs[2 + ns + nl], outs[3 + ns + nl]
        local, remote = plan(lax.axis_index("x"), lax.axis_index("y"), lax.axis_index("c"), ins, lz)
        own = [pltpu.make_async_copy(src, dst, local_sems.at[i]) for i, (src, dst) in enumerate(local)]
        for cp in own:
            cp.start()
        for cp in own:
            cp.wait()
        for k, (src, dst, peer) in enumerate(remote):
            pltpu.make_async_remote_copy(src_ref=src, dst_ref=dst, send_sem=send_sems.at[k], recv_sem=recv_sems.at[k],
                                         device_id=peer, device_id_type=MESH).start()
        token[...] = jnp.zeros_like(token)

    hbm = lambda a: pltpu.HBM(a.shape, a.dtype)
    outs = pl.pallas_call(
        body, name=name,
        out_shape=(pltpu.SemaphoreType.DMA((n_remote,)), pltpu.SemaphoreType.DMA((n_remote,)),
                   *[hbm(a) for a in srcs], *[hbm(a) for a in lands], jax.ShapeDtypeStruct((8, 128), F32)),
        in_specs=[ANY_SPEC] * (ns + nl + len(deps)),
        out_specs=(SEM_SPEC, SEM_SPEC, *[ANY_SPEC] * (ns + nl), pl.BlockSpec(memory_space=pltpu.VMEM)),
        scratch_shapes=[pltpu.SemaphoreType.DMA((n_local,))],
        input_output_aliases={i: 2 + i for i in range(ns + nl)},
        compiler_params=pltpu.CompilerParams(has_side_effects=EFFECT),
    )(*[pltpu.with_memory_space_constraint(a, pltpu.HBM) for a in srcs],
      *[pltpu.with_memory_space_constraint(a, pltpu.HBM) for a in lands], *deps)
    return dict(sems=outs[:2], srcs=outs[2:2 + ns], lands=outs[2 + ns:2 + ns + nl], token=outs[-1],
                plan=plan, n_remote=n_remote)


def _split_wait(name, handle, after):
    srcs, lands, plan = handle["srcs"], handle["lands"], handle["plan"]
    ns, nl = len(srcs), len(lands)

    def body(*refs):
        ins, lz = refs[:ns], refs[ns:ns + nl]
        send_sems, recv_sems = refs[ns + nl], refs[ns + nl + 1]
        _, remote = plan(lax.axis_index("x"), lax.axis_index("y"), lax.axis_index("c"), ins, lz)
        for k, (src, dst, peer) in enumerate(remote):
            cp = pltpu.make_async_remote_copy(src_ref=src, dst_ref=dst, send_sem=send_sems.at[k],
                                              recv_sem=recv_sems.at[k], device_id=peer, device_id_type=MESH)
            cp.wait_send()
            cp.wait_recv()

    hbm = lambda a: pltpu.HBM(a.shape, a.dtype)
    outs = pl.pallas_call(
        body, name=name, out_shape=(*[hbm(a) for a in srcs], *[hbm(a) for a in lands]),
        in_specs=[ANY_SPEC] * (ns + nl) + [SEM_SPEC, SEM_SPEC, ANY_SPEC], out_specs=tuple([ANY_SPEC] * (ns + nl)),
        input_output_aliases={i: i for i in range(ns + nl)},
        compiler_params=pltpu.CompilerParams(has_side_effects=EFFECT),
    )(*srcs, *lands, *handle["sems"], after)
    return list(outs[ns:])


def _sc_exchange(name, collective_id, arrays, scatter):
    nt = len(arrays)
    out_type = [jax.ShapeDtypeStruct(a.shape if scatter else (NDEV,) + a.shape, a.dtype) for a in arrays]

    def body(*refs):
        ins, outs = refs[:nt], refs[nt:2 * nt]
        send_sems, recv_sems, local_sems = refs[2 * nt:3 * nt], refs[3 * nt:4 * nt], refs[4 * nt:5 * nt]
        x, y, c = lax.axis_index("x"), lax.axis_index("y"), lax.axis_index("c")
        peers = [(mx + x - 2 * mx * x, my + y - 2 * my * y, mc + c - 2 * mc * c) for mx, my, mc in MASKS]
        barrier = pltpu.get_barrier_semaphore()
        for peer in peers:
            pl.semaphore_signal(barrier, inc=1, device_id=peer, device_id_type=MESH)
        pl.semaphore_wait(barrier, len(peers))
        me = 4 * x + 2 * y + c
        own = []
        for t in range(nt):
            cp = pltpu.make_async_copy(ins[t].at[me] if scatter else ins[t], outs[t].at[me], local_sems[t])
            cp.start()
            own.append(cp)
            for px, py, pc in peers:
                src = ins[t].at[4 * px + 2 * py + pc] if scatter else ins[t]
                pltpu.make_async_remote_copy(src_ref=src, dst_ref=outs[t].at[me], send_sem=send_sems[t],
                                             recv_sem=recv_sems[t], device_id=(px, py, pc), device_id_type=MESH).start()
        for t in range(nt):
            own[t].wait()
            seven = outs[t].at[pl.ds(0, NDEV - 1)]
            drain = pltpu.make_async_remote_copy(src_ref=seven, dst_ref=seven, send_sem=send_sems[t],
                                                 recv_sem=recv_sems[t], device_id=(x, y, c), device_id_type=MESH)
            drain.wait_send()
            drain.wait_recv()

    return pl.kernel(
        body, out_type=out_type, mesh=plsc.ScalarSubcoreMesh(axis_name="sequencer", num_cores=1),
        scratch_types=[pltpu.SemaphoreType.DMA] * (3 * nt),
        compiler_params=pltpu.CompilerParams(collective_id=collective_id), name=name,
    )(*arrays)


SMALL = {
    "e_pre_norm": ((2048,), None), "e_pool_w": ((4, 256, 256), 1), "e_pool_scale": ((1024,), None),
    "e_post_norm": ((2048,), None), "o_pre_norm": ((2048,), 0), "o_sgu_norm_g": ((1024,), 0),
    "o_sgu_norm_b": ((1024,), 0), "o_sgu_w": ((4, 128, 128), None), "o_sgu_b": ((4, 128), None),
    "o_conv_w": ((31, 1024), 1), "o_conv_b": ((1024,), 0), "o_conv_norm_g": ((1024,), 0),
    "o_conv_norm_b": ((1024,), 0), "o_post_norm": ((2048,), 0),
}
SMALL_SHARDED = [n for n, (_, ax) in SMALL.items() if ax is not None]


def _shard_shape(name):
    shape, ax = SMALL[name]
    if ax is None:
        return shape
    return tuple(s // NDEV if i == ax else s for i, s in enumerate(shape))


def _pack(arrs, row_multiple=1):
    flat = jnp.concatenate([a.reshape(-1) for a in arrs])
    pad = -flat.shape[0] % (128 * row_multiple)
    return jnp.concatenate([flat, jnp.zeros((pad,), F32)]).reshape(-1, 128)


def _unpack(buf, shapes):
    flat = buf.reshape(-1)
    out, off = [], 0
    for shp in shapes:
        n = int(np.prod(shp))
        out.append(flat[off:off + n].reshape(shp))
        off += n
    return out


def _take_shard(full, name, me):
    shape, ax = SMALL[name]
    if ax is None:
        return full
    n = shape[ax] // NDEV
    return lax.dynamic_slice_in_dim(full, me * n, n, axis=ax)


BIG = ("e_w_in", "e_w_out", "o_w_in", "o_w_out")
WEIGHTS = ["e_pre_norm", "e_w_in", "e_pool_w", "e_pool_scale", "e_w_out", "e_post_norm", "o_pre_norm", "o_w_in",
           "o_sgu_norm_g", "o_sgu_norm_b", "o_sgu_w", "o_sgu_b", "o_conv_w", "o_conv_b", "o_conv_norm_g",
           "o_conv_norm_b", "o_w_out", "o_post_norm"]


def kernel(x, e_pre_norm, e_w_in, e_pool_w, e_pool_scale, e_w_out, e_post_norm, o_pre_norm, o_w_in, o_sgu_norm_g, o_sgu_norm_b, o_sgu_w, o_sgu_b, o_conv_w, o_conv_b, o_conv_norm_g, o_conv_norm_b, o_w_out, o_post_norm, loss_target, m_e_pre_norm, m_e_w_in, m_e_pool_w, m_e_pool_scale, m_e_w_out, m_e_post_norm, m_o_pre_norm, m_o_w_in, m_o_sgu_norm_g, m_o_sgu_norm_b, m_o_sgu_w, m_o_sgu_b, m_o_conv_w, m_o_conv_b, m_o_conv_norm_g, m_o_conv_norm_b, m_o_w_out, m_o_post_norm, v_e_pre_norm, v_e_w_in, v_e_pool_w, v_e_pool_scale, v_e_w_out, v_e_post_norm, v_o_pre_norm, v_o_w_in, v_o_sgu_norm_g, v_o_sgu_norm_b, v_o_sgu_w, v_o_sgu_b, v_o_conv_w, v_o_conv_b, v_o_conv_norm_g, v_o_conv_norm_b, v_o_w_out, v_o_post_norm):
    given = dict(locals())
    w = {n: given[n][0] for n in WEIGHTS}
    m = {n: given["m_" + n][0] for n in WEIGHTS}
    v = {n: given["v_" + n][0] for n in WEIGHTS}
    me = 4 * lax.axis_index("x") + 2 * lax.axis_index("y") + lax.axis_index("c")
    x, target = x[0], loss_target[0]
    row = lambda a: a.reshape(1, -1)

    bf = {n: _cast_bf16(w[n], "cast_" + n) for n in BIG}
    wg_e_in, small_rows = _sc_exchange("gather_a", 0, [bf["e_w_in"], _pack([w[n] for n in SMALL_SHARDED])], False)
    wg_e_out, wg_o_in, wg_o_out = _sc_exchange("gather_b", 1, [bf["e_w_out"], bf["o_w_in"], bf["o_w_out"]], False)
    h0 = _pre0_fwd(x, row(w["e_pre_norm"]))
    p = {n: w[n] for n in SMALL if SMALL[n][1] is None}
    small_rows = small_rows.reshape(NDEV, -1)
    off = 0
    for n in SMALL_SHARDED:
        shp, ax = _shard_shape(n), SMALL[n][1]
        cnt = int(np.prod(shp))
        blk = small_rows[:, off:off + cnt].reshape((NDEV,) + shp)
        p[n] = jnp.moveaxis(blk, 0, ax).reshape(SMALL[n][0])
        off += cnt
    tabs = _rope_tables()
    pool_w_bf = p["e_pool_w"].astype(BF16)
    sgu_bb = jnp.broadcast_to(p["o_sgu_b"][:, :, None], (4, 128, 128))
    conv_w = jnp.concatenate([p["o_conv_w"], jnp.zeros((HALO - CONV_K, HALF), F32)], axis=0)
    odd_p = (row(p["o_sgu_norm_g"]), row(p["o_sgu_norm_b"]), p["o_sgu_w"], sgu_bb, conv_w,
             row(p["o_conv_b"]), row(p["o_conv_norm_g"]), row(p["o_conv_norm_b"]))

    z0 = _mm_in(h0, wg_e_in, "mm_z0")
    ycat0 = _pool_fwd(z0, pool_w_bf, row(p["e_pool_scale"]))
    ycat0 = _attn_fwd(z0, ycat0, tabs)
    w_out_e, w_out_o = wg_e_out.reshape(2048, D), wg_o_out.reshape(2048, D)
    y0 = _mm_out(ycat0, w_out_e, "mm_y0")
    x1, h1 = _post0_fwd(x, y0, row(p["e_post_norm"]), row(p["o_pre_norm"]))
    z1 = _mm_in(h1, wg_o_in, "mm_z1")
    ycat1 = _odd_fwd(z1, *odd_p)
    y1 = _mm_out(ycat1, w_out_o, "mm_y1")

    g = {}
    loss, dx2, dy1, g["o_post_norm"] = _post1_bwd(y1, x1, target, row(p["o_post_norm"]))
    loss = lax.psum(loss[0, 0], ("x", "y", "c"))
    parts = {}
    dw = _mm_out_dw(ycat1, dy1, "mm_dwout1").reshape(NDEV, 256, D)
    parts["o_w_out"], = _sc_exchange("scatter_o_w_out", 2, [dw], True)
    dycat1 = _mm_out_dx(dy1, w_out_o, "mm_dycat1", dw)
    dz1, ddc, g["o_sgu_w"], d_sgu_bb, g["o_sgu_norm_g"], g["o_sgu_norm_b"], g["o_conv_norm_g"], \
        g["o_conv_norm_b"], g["o_conv_b"] = _odd_bwd_a(z1, dycat1, *odd_p)
    dz1, d_conv_w = _odd_bwd_b(z1, ddc, dz1, conv_w)
    g["o_sgu_b"] = d_sgu_bb[:, :, 0]
    g["o_conv_w"] = d_conv_w[:CONV_K]
    grads, deltas, new_m, new_v = {}, {}, {}, {}

    def adam(n, dep):
        grads[n], deltas[n], new_m[n], new_v[n] = _adam_reduce(parts[n], w[n], m[n], v[n], "adam_" + n, dep)
        return new_v[n]

    pin = adam("o_w_out", d_conv_w)
    dw = _mm_in_dw(h1, dz1, ODD_IN // NDEV, "mm_dwin1", pin)
    parts["o_w_in"], = _sc_exchange("scatter_o_w_in", 3, [dw], True)
    dh1 = _mm_in_dx(dz1, wg_o_in, "mm_dh1", dw)
    dx1, dy0, g["o_pre_norm"], g["e_post_norm"] = _mid_bwd(dx2, dh1, x1, y0, row(p["o_pre_norm"]),
                                                           row(p["e_post_norm"]))
    dw = _mm_out_dw(ycat0, dy0, "mm_dwout0").reshape(NDEV, 256, D)
    parts["e_w_out"], = _sc_exchange("scatter_e_w_out", 4, [dw], True)
    dycat0 = _mm_out_dx(dy0, w_out_e, "mm_dycat0", dw)
    da_in, da_gate, g["e_pool_w"], g["e_pool_scale"] = _pool_bwd(z0, dycat0, pool_w_bf, row(p["e_pool_scale"]))
    dq, dk, dv, dbg = _attn_bwd(z0, dycat0, tabs)
    dz0 = jnp.concatenate([da_in, da_gate, dq, dk, dv, dbg], axis=1)
    late = [n for n in SMALL if n != "e_pre_norm"]
    pin = adam("e_w_out", adam("o_w_in", dbg))
    dw = _mm_in_dw(h0, dz0, EVEN_IN // NDEV, "mm_dwin0", pin)
    parts["e_w_in"], = _sc_exchange("scatter_e_w_in", 5, [dw], True)
    recv_small, = _sc_exchange("gather_small_grads", 6, [_pack([g[n].reshape(SMALL[n][0]) for n in late], 512)], False)
    dh0 = _mm_in_dx(dz0, wg_e_in, "mm_dh0", dw)
    grad_x, g["e_pre_norm"] = _pre0_bwd(dx1, dh0, x, row(p["e_pre_norm"]))
    last, = _sc_exchange("gather_e_pre_norm_grad", 7, [g["e_pre_norm"].reshape(16, 128)], False)

    g_small = dict(zip(late, _unpack(_sum_parts(recv_small, "sum_small_grads"), [SMALL[n][0] for n in late])))
    pin = adam("e_w_in", grad_x)
    g_small["e_pre_norm"] = _sum_parts(last, "sum_e_pre_norm_grad", pin).reshape(2048)
    for n in SMALL:
        grads[n] = _take_shard(g_small[n], n, me)
    names = list(SMALL)
    shapes = [_shard_shape(n) for n in names]
    d_pack, m_pack, v_pack = _adam_plain(_pack([w[n] for n in names]), _pack([grads[n] for n in names]),
                                         _pack([m[n] for n in names]), _pack([v[n] for n in names]), "adam_small")
    for n, d_, m_, v_ in zip(names, _unpack(d_pack, shapes), _unpack(m_pack, shapes), _unpack(v_pack, shapes)):
        deltas[n], new_m[n], new_v[n] = d_, m_, v_

    lead = lambda a: a[None]
    return (loss, grad_x[None], *[lead(grads[n]) for n in WEIGHTS], *[lead(deltas[n]) for n in WEIGHTS],
            *[lead(new_m[n]) for n in WEIGHTS], *[lead(new_v[n]) for n in WEIGHTS])
```

```python
import functools

import numpy as np
import jax
import jax.numpy as jnp
from jax import lax
from jax.experimental import pallas as pl
from jax.experimental.pallas import tpu as pltpu
from jax.experimental.pallas import tpu_sc as plsc

F32 = jnp.float32
BF16 = jnp.bfloat16

S = 2048
D = 2048
NDEV = 8
EPS = 1e-6
NEG = -1e30
HEAD_DIM = 128
ROT_DIM = 32
ROPE_THETA = 500000.0
PATTERNS = ((128, 1), (512, 4), (2048, 16))
BLK = 128
EVEN_IN = 12288
ODD_IN = 6144
HALF = 1024
CONV_K = 31
HALO = 32
TR = 256
SUB = 32

ADAM_LR = 0.001
ADAM_B1 = 0.9
ADAM_B2 = 0.999
ADAM_EPS = 1e-08
ADAM_WD = 0.01
ADAM_STEP = 10

VMEM_BIG = 56 * 1024 * 1024
MESH = pl.DeviceIdType.MESH

NN = (((1,), (0,)), ((), ()))
NT = (((1,), (1,)), ((), ()))
TN = (((0,), (0,)), ((), ()))


def _dot(a, b, dn=NN):
    return lax.dot_general(a, b, dn, preferred_element_type=F32)


def _sigmoid(x):
    return 1.0 / (1.0 + jnp.exp(-x))


def _silu_and_grad(x):
    sg = _sigmoid(x)
    return x * sg, sg * (1.0 + x * (1.0 - sg))


def _params(sem, vmem=None):
    return pltpu.CompilerParams(dimension_semantics=sem, vmem_limit_bytes=vmem)


ANY_SPEC = pl.BlockSpec(memory_space=pl.ANY)


def _matmul(a, b, *, dn, grid, a_spec, b_spec, o_spec, out_shape, out_dtype, acc_shape, name, dep=None):
    nk = grid[2]
    deps = [] if dep is None else [dep]

    def body(a_ref, b_ref, *rest):
        o_ref, acc = rest[len(deps)], rest[len(deps) + 1:]
        if nk == 1:
            o_ref[...] = _dot(a_ref[...], b_ref[...], dn).astype(o_ref.dtype)
            return
        acc_ref = acc[0]
        k = pl.program_id(2)

        @pl.when(k == 0)
        def _():
            acc_ref[...] = jnp.zeros_like(acc_ref)

        acc_ref[...] += _dot(a_ref[...], b_ref[...], dn)

        @pl.when(k == nk - 1)
        def _():
            o_ref[...] = acc_ref[...].astype(o_ref.dtype)

    return pl.pallas_call(
        body, grid=grid, in_specs=[a_spec, b_spec] + [ANY_SPEC] * len(deps), out_specs=o_spec,
        out_shape=jax.ShapeDtypeStruct(out_shape, out_dtype),
        scratch_shapes=[] if nk == 1 else [pltpu.VMEM(acc_shape, F32)],
        compiler_params=_params(("parallel", "parallel", "arbitrary"), VMEM_BIG), name=name,
    )(a, b, *deps)


TM = 2048


def _mm_in(h, wg, name):
    nb = wg.shape[2]
    tn = 512 if nb % 512 == 0 else nb
    per = nb // tn
    return _matmul(
        h, wg, dn=NN, grid=(S // TM, NDEV * per, 1),
        a_spec=pl.BlockSpec((TM, D), lambda i, j, k: (i, 0)),
        b_spec=pl.BlockSpec((None, D, tn), lambda i, j, k: (j // per, 0, j % per)),
        o_spec=pl.BlockSpec((TM, tn), lambda i, j, k: (i, j)),
        out_shape=(S, NDEV * nb), out_dtype=F32, acc_shape=(TM, tn), name=name)


def _mm_in_dx(dz, wg, name, dep=None):
    nb = wg.shape[2]
    return _matmul(
        dz, wg, dn=NT, grid=(S // TM, D // 1024, NDEV),
        a_spec=pl.BlockSpec((TM, nb), lambda i, j, k: (i, k)),
        b_spec=pl.BlockSpec((None, 1024, nb), lambda i, j, k: (k, j, 0)),
        o_spec=pl.BlockSpec((TM, 1024), lambda i, j, k: (i, j)),
        out_shape=(S, D), out_dtype=F32, acc_shape=(TM, 1024), name=name, dep=dep)


def _mm_in_dw(h, dz, nb, name, dep=None):
    tn = 512 if nb % 512 == 0 else nb
    per = nb // tn
    return _matmul(
        h, dz, dn=TN, grid=(D // TM, NDEV * per, 1),
        a_spec=pl.BlockSpec((S, TM), lambda i, j, k: (0, i)),
        b_spec=pl.BlockSpec((S, tn), lambda i, j, k: (0, j)),
        o_spec=pl.BlockSpec((None, TM, tn), lambda i, j, k: (j // per, i, j % per)),
        out_shape=(NDEV, D, nb), out_dtype=BF16, acc_shape=(TM, tn), name=name, dep=dep)


def _mm_out(yc, w, name):
    return _matmul(
        yc, w, dn=NN, grid=(S // TM, D // 512, 1),
        a_spec=pl.BlockSpec((TM, 2048), lambda i, j, k: (i, 0)),
        b_spec=pl.BlockSpec((2048, 512), lambda i, j, k: (0, j)),
        o_spec=pl.BlockSpec((TM, 512), lambda i, j, k: (i, j)),
        out_shape=(S, D), out_dtype=F32, acc_shape=(TM, 512), name=name)


def _mm_out_dx(dy, w, name, dep=None):
    return _matmul(
        dy, w, dn=NT, grid=(S // TM, 2048 // 512, 1),
        a_spec=pl.BlockSpec((TM, D), lambda i, j, k: (i, 0)),
        b_spec=pl.BlockSpec((512, D), lambda i, j, k: (j, 0)),
        o_spec=pl.BlockSpec((TM, 512), lambda i, j, k: (i, j)),
        out_shape=(S, 2048), out_dtype=F32, acc_shape=(TM, 512), name=name, dep=dep)


def _mm_out_dw(yc, dy, name):
    return _matmul(
        yc, dy, dn=TN, grid=(2048 // TM, D // 512, 1),
        a_spec=pl.BlockSpec((S, TM), lambda i, j, k: (0, i)),
        b_spec=pl.BlockSpec((S, 512), lambda i, j, k: (0, j)),
        o_spec=pl.BlockSpec((TM, 512), lambda i, j, k: (i, j)),
        out_shape=(2048, D), out_dtype=BF16, acc_shape=(TM, 512), name=name)


def _row_spec(w=D):
    return pl.BlockSpec((TR, w), lambda i: (i, 0))


def _vec_spec(w=D):
    return pl.BlockSpec((1, w), lambda i: (0, 0))


def _rms_stats(x):
    r = lax.rsqrt(jnp.mean(x * x, axis=-1, keepdims=True) + EPS)
    return x * r, r


def _rms_bwd(dn, xhat, r, g):
    dxh = dn * g
    return r * (dxh - xhat * jnp.mean(dxh * xhat, axis=-1, keepdims=True))


def _acc_rows(ref, val, i):
    s = jnp.sum(val, axis=0, keepdims=True)

    @pl.when(i == 0)
    def _():
        ref[...] = s

    @pl.when(i > 0)
    def _():
        ref[...] += s


def _pre0_fwd(x, g, dep=None):
    deps = [] if dep is None else [dep]

    def body(x_ref, g_ref, *rest):
        xhat, _ = _rms_stats(x_ref[...])
        rest[-1][...] = (xhat * g_ref[...]).astype(BF16)

    return pl.pallas_call(
        body, grid=(S // TR,), in_specs=[_row_spec(), _vec_spec()] + [ANY_SPEC] * len(deps), out_specs=_row_spec(),
        out_shape=jax.ShapeDtypeStruct((S, D), BF16), compiler_params=_params(("parallel",)), name="pre0_fwd",
    )(x, g, *deps)


def _post0_fwd(x, y0, g_post, g_pre1):
    def body(x_ref, y_ref, gp_ref, g1_ref, x1_ref, h1_ref):
        yhat, _ = _rms_stats(y_ref[...])
        x1 = x_ref[...] + yhat * gp_ref[...]
        x1_ref[...] = x1
        xhat, _ = _rms_stats(x1)
        h1_ref[...] = (xhat * g1_ref[...]).astype(BF16)

    return pl.pallas_call(
        body, grid=(S // TR,), in_specs=[_row_spec(), _row_spec(), _vec_spec(), _vec_spec()],
        out_specs=[_row_spec(), _row_spec()],
        out_shape=[jax.ShapeDtypeStruct((S, D), F32), jax.ShapeDtypeStruct((S, D), BF16)],
        compiler_params=_params(("parallel",)), name="post0_fwd",
    )(x, y0, g_post, g_pre1)


def _post1_bwd(y1, x1, target, g_post):
    def body(y_ref, x1_ref, t_ref, g_ref, loss_ref, dx2_ref, dy_ref, dg_ref):
        i = pl.program_id(0)
        yhat, r = _rms_stats(y_ref[...])
        g = g_ref[...]
        err = x1_ref[...] + yhat * g - t_ref[...]
        part = jnp.sum(jnp.sum(err * err, axis=-1, keepdims=True), axis=0, keepdims=True) * (0.5 / D)
        _acc_rows(loss_ref, jnp.broadcast_to(part, (1, 128)), i)
        dx2 = err * (1.0 / D)
        dx2_ref[...] = dx2
        _acc_rows(dg_ref, dx2 * yhat, i)
        dy_ref[...] = _rms_bwd(dx2, yhat, r, g).astype(BF16)

    return pl.pallas_call(
        body, grid=(S // TR,), in_specs=[_row_spec(), _row_spec(), _row_spec(), _vec_spec()],
        out_specs=[_vec_spec(128), _row_spec(), _row_spec(), _vec_spec()],
        out_shape=[jax.ShapeDtypeStruct((1, 128), F32), jax.ShapeDtypeStruct((S, D), F32),
                   jax.ShapeDtypeStruct((S, D), BF16), jax.ShapeDtypeStruct((1, D), F32)],
        compiler_params=_params(("arbitrary",)), name="post1_bwd",
    )(y1, x1, target, g_post)


def _mid_bwd(dx2, dh1, x1, y0, g_pre1, g_post0):
    def body(dx2_ref, dh_ref, x1_ref, y_ref, g1_ref, gp_ref, dx1_ref, dy_ref, dg1_ref, dgp_ref):
        i = pl.program_id(0)
        xhat, r1 = _rms_stats(x1_ref[...])
        dh = dh_ref[...]
        _acc_rows(dg1_ref, dh * xhat, i)
        dx1 = dx2_ref[...] + _rms_bwd(dh, xhat, r1, g1_ref[...])
        dx1_ref[...] = dx1
        yhat, r0 = _rms_stats(y_ref[...])
        _acc_rows(dgp_ref, dx1 * yhat, i)
        dy_ref[...] = _rms_bwd(dx1, yhat, r0, gp_ref[...]).astype(BF16)

    return pl.pallas_call(
        body, grid=(S // TR,),
        in_specs=[_row_spec(), _row_spec(), _row_spec(), _row_spec(), _vec_spec(), _vec_spec()],
        out_specs=[_row_spec(), _row_spec(), _vec_spec(), _vec_spec()],
        out_shape=[jax.ShapeDtypeStruct((S, D), F32), jax.ShapeDtypeStruct((S, D), BF16),
                   jax.ShapeDtypeStruct((1, D), F32), jax.ShapeDtypeStruct((1, D), F32)],
        compiler_params=_params(("arbitrary",)), name="mid_bwd",
    )(dx2, dh1, x1, y0, g_pre1, g_post0)


def _pre0_bwd(dx1, dh0, x, g):
    def body(dx1_ref, dh_ref, x_ref, g_ref, gx_ref, dg_ref):
        i = pl.program_id(0)
        xhat, r = _rms_stats(x_ref[...])
        dh = dh_ref[...]
        _acc_rows(dg_ref, dh * xhat, i)
        gx_ref[...] = dx1_ref[...] + _rms_bwd(dh, xhat, r, g_ref[...])

    return pl.pallas_call(
        body, grid=(S // TR,), in_specs=[_row_spec(), _row_spec(), _row_spec(), _vec_spec()],
        out_specs=[_row_spec(), _vec_spec()],
        out_shape=[jax.ShapeDtypeStruct((S, D), F32), jax.ShapeDtypeStruct((1, D), F32)],
        compiler_params=_params(("arbitrary",)), name="pre0_bwd",
    )(dx1, dh0, x, g)


POOL_CH = 256


def _pool_apply(a, w, transpose):
    n = a.shape[0]
    row = lax.broadcasted_iota(jnp.int32, a.shape, 0)
    cnt = jnp.minimum(row + 1, w).astype(F32)
    s = a / cnt if transpose else a
    for k in (1, 2, 4, 8):
        if transpose:
            sh = jnp.where(row < n - k, pltpu.roll(s, n - k, 0), 0.0)
        else:
            sh = jnp.where(row >= k, pltpu.roll(s, k, 0), 0.0)
        s = jnp.where(w > k, s + sh, s)
    return s - a if transpose else s / cnt - a


def _pool_fwd(z0, pool_w, pool_scale):
    def body(a_ref, gate_ref, w_ref, sc_ref, out_ref):
        win = jnp.left_shift(2, pl.program_id(0))
        pooled = _pool_apply(a_ref[...], win, False)
        mixed = _dot(pooled.astype(BF16), w_ref[...])
        gate = gate_ref[...]
        out_ref[...] = (mixed * sc_ref[...] * (gate * _sigmoid(gate))).astype(BF16)

    return pl.pallas_call(
        body, grid=(4,),
        in_specs=[pl.BlockSpec((S, POOL_CH), lambda g: (0, g)), pl.BlockSpec((S, POOL_CH), lambda g: (0, 4 + g)),
                  pl.BlockSpec((None, POOL_CH, POOL_CH), lambda g: (g, 0, 0)),
                  pl.BlockSpec((1, POOL_CH), lambda g: (0, g))],
        out_specs=pl.BlockSpec((S, POOL_CH), lambda g: (0, g)),
        out_shape=jax.ShapeDtypeStruct((S, 2048), BF16),
        compiler_params=_params(("parallel",), VMEM_BIG), name="pool_fwd",
    )(z0, z0, pool_w, pool_scale)


def _pool_bwd(z0, dycat, pool_w, pool_scale):
    def body(a_ref, gate_ref, dy_ref, w_ref, sc_ref, da_ref, dgate_ref, dw_ref, dsc_ref):
        win = jnp.left_shift(2, pl.program_id(0))
        pooled = _pool_apply(a_ref[...], win, False).astype(BF16)
        w = w_ref[...]
        mixed = _dot(pooled, w)
        silu, dsilu = _silu_and_grad(gate_ref[...])
        dy = dy_ref[...]
        sc = sc_ref[...]
        dgate_ref[...] = (dy * (mixed * sc) * dsilu).astype(BF16)
        dms = dy * silu
        dsc_ref[...] = jnp.sum(dms * mixed, axis=0, keepdims=True)
        dmixed = (dms * sc).astype(BF16)
        dw_ref[...] = _dot(pooled, dmixed, TN)
        dpooled = _dot(dmixed, w, NT)
        da_ref[...] = _pool_apply(dpooled, win, True).astype(BF16)

    slab = lambda off: pl.BlockSpec((S, POOL_CH), lambda g: (0, off + g))
    return pl.pallas_call(
        body, grid=(4,),
        in_specs=[slab(0), slab(4), slab(0), pl.BlockSpec((None, POOL_CH, POOL_CH), lambda g: (g, 0, 0)),
                  pl.BlockSpec((1, POOL_CH), lambda g: (0, g))],
        out_specs=[slab(0), slab(0), pl.BlockSpec((None, POOL_CH, POOL_CH), lambda g: (g, 0, 0)),
                   pl.BlockSpec((1, POOL_CH), lambda g: (0, g))],
        out_shape=[jax.ShapeDtypeStruct((S, HALF), BF16), jax.ShapeDtypeStruct((S, HALF), BF16),
                   jax.ShapeDtypeStruct((4, POOL_CH, POOL_CH), F32), jax.ShapeDtypeStruct((1, HALF), F32)],
        compiler_params=_params(("parallel",), VMEM_BIG), name="pool_bwd",
    )(z0, z0, dycat, pool_w, pool_scale)


Q_COL, K_COL, V_COL, BG_COL = 2048 // 128, 5120 // 128, 8192 // 128, 11264 // 128
SCALE = HEAD_DIM ** -0.5


def _rope_tables():
    pos = jnp.arange(S, dtype=F32)
    inv_freq = jnp.power(ROPE_THETA, -jnp.arange(0, ROT_DIM, 2, dtype=F32) / ROT_DIM)
    ang = pos[:, None] * inv_freq[None, :]
    cos, sin = jnp.cos(ang), jnp.sin(ang)
    half = ROT_DIM // 2
    zeros = jnp.zeros((S, HEAD_DIM - ROT_DIM), F32)
    c = jnp.concatenate([cos, cos, jnp.ones((S, HEAD_DIM - ROT_DIM), F32)], axis=1)
    a = jnp.concatenate([-sin, jnp.zeros((S, half), F32), zeros], axis=1)
    b = jnp.concatenate([jnp.zeros((S, half), F32), sin, zeros], axis=1)
    return c, a, b


def _rope(t, c, a, b):
    half = ROT_DIM // 2
    return t * c + pltpu.roll(t, HEAD_DIM - half, 1) * a + pltpu.roll(t, half, 1) * b


def _rope_t(d, c, a, b):
    half = ROT_DIM // 2
    return d * c + pltpu.roll(d * a, half, 1) + pltpu.roll(d * b, HEAD_DIM - half, 1)


def _deinterleave(dst, src, dil, cast=None, dst_off=0):
    length = S // dil
    for r in range(dil):
        v = src[...] if dil == 1 else src[pl.ds(r, length, stride=dil), :]
        dst[dst_off + r * length:dst_off + (r + 1) * length, :] = v if cast is None else v.astype(cast)


def _interleave(dst, src, dil, src_off=0):
    length = S // dil
    for r in range(dil):
        if dil == 1:
            dst[...] = src[src_off:src_off + S, :]
        else:
            dst[pl.ds(r, length, stride=dil), :] = src[src_off + r * length:src_off + (r + 1) * length, :]


CU = 4
NUNITS = S // BLK
B_QK = (((2,), (2,)), ((0,), (0,)))
B_PV = (((2,), (1,)), ((0,), (0,)))
B_TN = (((1,), (1,)), ((0,), (0,)))


def _blocks(ref, first):
    return ref[first * BLK:(first + CU) * BLK, :].reshape(CU, BLK, HEAD_DIM)


def _chunk_scores(u0, nb, qd, kdp):
    q = _blocks(qd, u0)
    row = lax.broadcasted_iota(jnp.int32, (CU, BLK, BLK), 1)
    col = lax.broadcasted_iota(jnp.int32, (CU, BLK, BLK), 2)
    s_own = jnp.where(col <= row, _dot(q, _blocks(kdp, u0 + 1), B_QK) * SCALE, NEG)
    if nb == 1:
        return q, s_own, None
    unit = lax.broadcasted_iota(jnp.int32, (CU, BLK, BLK), 0) + u0
    s_prev = jnp.where((col >= row) & ((unit % nb) != 0), _dot(q, _blocks(kdp, u0), B_QK) * SCALE, NEG)
    return q, s_own, s_prev


def _attn_prepare(dil, q_ref, k_ref, v_ref, tabs, tmp, qd, kdp, vdp):
    c, a, b = tabs
    tmp[...] = _rope(q_ref[...], c, a, b)
    _deinterleave(qd, tmp, dil, BF16)
    tmp[...] = _rope(k_ref[...], c, a, b)
    _deinterleave(kdp, tmp, dil, BF16, BLK)
    _deinterleave(vdp, v_ref, dil, BF16, BLK)
    kdp[0:BLK, :] = jnp.zeros((BLK, HEAD_DIM), BF16)
    vdp[0:BLK, :] = jnp.zeros((BLK, HEAD_DIM), BF16)


def _attn_group_fwd(dil, q_ref, k_ref, v_ref, tabs, tmp, qd, kdp, vdp, od, ld, og, lg):
    nb = S // dil // BLK
    _attn_prepare(dil, q_ref, k_ref, v_ref, tabs, tmp, qd, kdp, vdp)
    for u0 in range(0, NUNITS, CU):
        _, s_own, s_prev = _chunk_scores(u0, nb, qd, kdp)
        m = jnp.max(s_own, axis=2, keepdims=True)
        if s_prev is not None:
            m = jnp.maximum(m, jnp.max(s_prev, axis=2, keepdims=True))
        p_own = jnp.exp(s_own - m)
        den = jnp.sum(p_own, axis=2, keepdims=True)
        acc = _dot(p_own.astype(BF16), _blocks(vdp, u0 + 1), B_PV)
        if s_prev is not None:
            p_prev = jnp.exp(s_prev - m)
            den = den + jnp.sum(p_prev, axis=2, keepdims=True)
            acc = acc + _dot(p_prev.astype(BF16), _blocks(vdp, u0), B_PV)
        rows = slice(u0 * BLK, (u0 + CU) * BLK)
        od[rows, :] = (acc / den).reshape(CU * BLK, HEAD_DIM)
        ld[rows, :] = jnp.broadcast_to(m + jnp.log(den), (CU, BLK, HEAD_DIM)).reshape(CU * BLK, HEAD_DIM)
    _interleave(og, od, dil)
    _interleave(lg, ld, dil)


def _group_weights(lgs):
    l0, l1, l2 = lgs[0][...], lgs[1][...], lgs[2][...]
    mx = jnp.maximum(l0, jnp.maximum(l1, l2))
    e0, e1, e2 = jnp.exp(l0 - mx), jnp.exp(l1 - mx), jnp.exp(l2 - mx)
    den = e0 + e1 + e2
    return e0 / den, e1 / den, e2 / den


def _head_spec(base, ngroups_axis=True):
    return pl.BlockSpec((S, HEAD_DIM), lambda h, p: (0, base + (p % 3) * 8 + h))


ATTN_SCRATCH_FWD = [
    pltpu.VMEM((S, HEAD_DIM), F32),
    pltpu.VMEM((S, HEAD_DIM), BF16), pltpu.VMEM((S + BLK, HEAD_DIM), BF16), pltpu.VMEM((S + BLK, HEAD_DIM), BF16),
    pltpu.VMEM((S, HEAD_DIM), F32), pltpu.VMEM((S, HEAD_DIM), F32),
    pltpu.VMEM((S, HEAD_DIM), F32), pltpu.VMEM((S, HEAD_DIM), F32), pltpu.VMEM((S, HEAD_DIM), F32),
    pltpu.VMEM((S, HEAD_DIM), F32), pltpu.VMEM((S, HEAD_DIM), F32), pltpu.VMEM((S, HEAD_DIM), F32),
]


def _attn_fwd(z0, ycat, tabs):
    def body(q_ref, k_ref, v_ref, gate_ref, c_ref, a_ref, b_ref, ycat_ref, out_ref,
             tmp, qd, kd, vd, od, ld, og0, og1, og2, lg0, lg1, lg2):
        del ycat_ref
        p = pl.program_id(1)
        ogs, lgs = (og0, og1, og2), (lg0, lg1, lg2)
        tabs_v = (c_ref[...], a_ref[...], b_ref[...])
        for gi, (_, dil) in enumerate(PATTERNS):
            @pl.when(p == gi)
            def _(gi=gi, dil=dil):
                _attn_group_fwd(dil, q_ref, k_ref, v_ref, tabs_v, tmp, qd, kd, vd, od, ld, ogs[gi], lgs[gi])

        @pl.when(p == 2)
        def _():
            w0, w1, w2 = _group_weights(lgs)
            o = w0 * og0[...] + w1 * og1[...] + w2 * og2[...]
            gate = gate_ref[...]
            out_ref[...] = (o * (gate * _sigmoid(gate))).astype(BF16)

    tab = pl.BlockSpec((S, HEAD_DIM), lambda h, p: (0, 0))
    return pl.pallas_call(
        body, grid=(8, 3),
        in_specs=[_head_spec(Q_COL), _head_spec(K_COL), _head_spec(V_COL),
                  pl.BlockSpec((S, HEAD_DIM), lambda h, p: (0, BG_COL + h)), tab, tab, tab,
                  pl.BlockSpec(memory_space=pl.ANY)],
        out_specs=pl.BlockSpec((S, HEAD_DIM), lambda h, p: (0, 8 + h)),
        out_shape=jax.ShapeDtypeStruct((S, 2048), BF16),
        scratch_shapes=ATTN_SCRATCH_FWD, input_output_aliases={7: 0},
        compiler_params=_params(("parallel", "arbitrary"), VMEM_BIG), name="attn_fwd",
    )(z0, z0, z0, z0, *tabs, ycat)


def _attn_bwd(z0, dycat, tabs):
    def body(q_ref, k_ref, v_ref, gate_ref, dy_ref, c_ref, a_ref, b_ref,
             dq_ref, dk_ref, dv_ref, dbg_ref,
             tmp, qd, kd, vd, od, ld, og0, og1, og2, lg0, lg1, lg2, cg0, cg1, cg2, dod, cd, dqd, dkd, dvd):
        p = pl.program_id(1)
        ogs, lgs, cgs = (og0, og1, og2), (lg0, lg1, lg2), (cg0, cg1, cg2)
        tabs_v = (c_ref[...], a_ref[...], b_ref[...])
        for gi, (_, dil) in enumerate(PATTERNS):
            @pl.when(p == gi)
            def _(gi=gi, dil=dil):
                _attn_group_fwd(dil, q_ref, k_ref, v_ref, tabs_v, tmp, qd, kd, vd, od, ld, ogs[gi], lgs[gi])

        @pl.when(p == 2)
        def _():
            w = _group_weights(lgs)
            o = w[0] * og0[...] + w[1] * og1[...] + w[2] * og2[...]
            silu, dsilu = _silu_and_grad(gate_ref[...])
            dy = dy_ref[...]
            dbg_ref[...] = (dy * o * dsilu).astype(BF16)
            do = dy * silu
            dwbar = jnp.sum(do * o, axis=1, keepdims=True)
            for gi in range(3):
                ogs[gi][...] = w[gi] * do
                cgs[gi][...] = -w[gi] * dwbar

        for gi, (_, dil) in enumerate(PATTERNS):
            @pl.when(p == 3 + gi)
            def _(gi=gi, dil=dil):
                nb = S // dil // BLK
                c, a, b = tabs_v
                _attn_prepare(dil, q_ref, k_ref, v_ref, tabs_v, tmp, qd, kd, vd)
                _deinterleave(dod, ogs[gi], dil, BF16)
                _deinterleave(ld, lgs[gi], dil)
                _deinterleave(cd, cgs[gi], dil)
                dkd[...] = jnp.zeros_like(dkd)
                dvd[...] = jnp.zeros_like(dvd)
                flat = lambda t: t.reshape(CU * BLK, HEAD_DIM)
                for u0 in range(0, NUNITS, CU):
                    q, s_own, s_prev = _chunk_scores(u0, nb, qd, kd)
                    lse, cv, do = _blocks(ld, u0), _blocks(cd, u0), _blocks(dod, u0)
                    own = slice((u0 + 1) * BLK, (u0 + 1 + CU) * BLK)
                    p_own = jnp.exp(s_own - lse)
                    ds_own = (p_own * (_dot(do, _blocks(vd, u0 + 1), B_QK) + cv) * SCALE).astype(BF16)
                    dq = _dot(ds_own, _blocks(kd, u0 + 1), B_PV)
                    dkd[own, :] += flat(_dot(ds_own, q, B_TN))
                    dvd[own, :] += flat(_dot(p_own.astype(BF16), do, B_TN))
                    if s_prev is not None:
                        prev = slice(u0 * BLK, (u0 + CU) * BLK)
                        p_prev = jnp.exp(s_prev - lse)
                        ds_prev = (p_prev * (_dot(do, _blocks(vd, u0), B_QK) + cv) * SCALE).astype(BF16)
                        dq = dq + _dot(ds_prev, _blocks(kd, u0), B_PV)
                        dkd[prev, :] += flat(_dot(ds_prev, q, B_TN))
                        dvd[prev, :] += flat(_dot(p_prev.astype(BF16), do, B_TN))
                    dqd[u0 * BLK:(u0 + CU) * BLK, :] = flat(dq)
                _interleave(tmp, dqd, dil)
                dq_ref[...] = _rope_t(tmp[...], c, a, b).astype(BF16)
                _interleave(tmp, dkd, dil, BLK)
                dk_ref[...] = _rope_t(tmp[...], c, a, b).astype(BF16)
                _interleave(tmp, dvd, dil, BLK)
                dv_ref[...] = tmp[...].astype(BF16)

    tab = pl.BlockSpec((S, HEAD_DIM), lambda h, p: (0, 0))
    hspec = lambda base: pl.BlockSpec((S, HEAD_DIM), lambda h, p: (0, base + h))
    gspec = pl.BlockSpec((S, HEAD_DIM), lambda h, p: (0, jnp.maximum(p - 3, 0) * 8 + h))
    slab = lambda: pltpu.VMEM((S, HEAD_DIM), F32)
    return pl.pallas_call(
        body, grid=(8, 6),
        in_specs=[_head_spec(Q_COL), _head_spec(K_COL), _head_spec(V_COL), hspec(BG_COL), hspec(8), tab, tab, tab],
        out_specs=[gspec, gspec, gspec, hspec(0)],
        out_shape=[jax.ShapeDtypeStruct((S, 3072), BF16)] * 3 + [jax.ShapeDtypeStruct((S, HALF), BF16)],
        scratch_shapes=ATTN_SCRATCH_FWD + [slab(), slab(), slab(), pltpu.VMEM((S, HEAD_DIM), BF16), slab(), slab(),
                                           pltpu.VMEM((S + BLK, HEAD_DIM), F32), pltpu.VMEM((S + BLK, HEAD_DIM), F32)],
        compiler_params=_params(("parallel", "arbitrary"), VMEM_BIG), name="attn_bwd",
    )(z0, z0, z0, z0, dycat, *tabs)


SGU_CH = 256
NCHUNK = TR // 128


def _ln_stats(x):
    mu = jnp.mean(x, axis=-1, keepdims=True)
    xc = x - mu
    r = lax.rsqrt(jnp.mean(xc * xc, axis=-1, keepdims=True) + EPS)
    return xc * r, r


def _ln_bwd(dy, xhat, r, g):
    dxh = dy * g
    return r * (dxh - jnp.mean(dxh, axis=-1, keepdims=True) - xhat * jnp.mean(dxh * xhat, axis=-1, keepdims=True))


def _tril_bf16(w):
    row = lax.broadcasted_iota(jnp.int32, w.shape, 0)
    col = lax.broadcasted_iota(jnp.int32, w.shape, 1)
    return jnp.where(row >= col, w, 0.0).astype(BF16)


def _sgu_gate(vn_s, s_s, w_ref, bb_ref):
    for h in range(4):
        wm = _tril_bf16(w_ref[h])
        bias = bb_ref[h]
        for ch in range(NCHUNK):
            rows, cols = slice(ch * 128, (ch + 1) * 128), slice(h * SGU_CH, (h + 1) * SGU_CH)
            s_s[rows, cols] = _dot(wm, vn_s[rows, cols]) + jnp.concatenate([bias, bias], axis=1)


def _conv_fwd(i, dval_ref, dglu_ref, hval_ref, hglu_ref, cw_ref, cb_ref, xw, dcs):
    halo = hval_ref[...] * _sigmoid(hglu_ref[...])
    xw[0:HALO, :] = jnp.where(i > 0, halo, 0.0)
    xw[HALO:HALO + TR, :] = dval_ref[...] * _sigmoid(dglu_ref[...])
    for rb in range(TR // SUB):
        acc = jnp.broadcast_to(cb_ref[...], (SUB, HALF))
        for k in range(CONV_K):
            acc = acc + cw_ref[k:k + 1, :] * xw[pl.ds(rb * SUB + HALO - (CONV_K - 1) + k, SUB), :]
        dcs[rb * SUB:(rb + 1) * SUB, :] = acc


def _odd_in_specs():
    col = lambda j: pl.BlockSpec((TR, HALF), lambda i, *_: (i, j))
    prev = lambda j: pl.BlockSpec((HALO, HALF), lambda i, *_: (jnp.maximum(i * (TR // HALO) - 1, 0), j))
    return [col(0), col(1), col(2), col(3), col(4), col(5), prev(3), prev(4)]


def _full_spec(shape):
    return pl.BlockSpec(shape, lambda i, *_: (0,) * len(shape))


def _odd_fwd(z1, sgu_g, sgu_b, sgu_w, sgu_bb, conv_w, conv_b, cn_g, cn_b):
    def body(u_ref, v_ref, cg_ref, dval_ref, dglu_ref, dgate_ref, hval_ref, hglu_ref,
             g_ref, b_ref, w_ref, bb_ref, cw_ref, cb_ref, cng_ref, cnb_ref, out_ref, vn_s, s_s, xw, dcs):
        i = pl.program_id(0)
        vhat, _ = _ln_stats(v_ref[...])
        vn_s[...] = (vhat * g_ref[...] + b_ref[...]).astype(BF16)
        _sgu_gate(vn_s, s_s, w_ref, bb_ref)
        cg = cg_ref[...]
        out_ref[:, 0:HALF] = (u_ref[...] * s_s[...] * (cg * _sigmoid(cg))).astype(BF16)
        _conv_fwd(i, dval_ref, dglu_ref, hval_ref, hglu_ref, cw_ref, cb_ref, xw, dcs)
        dhat, _ = _ln_stats(dcs[...])
        dn = dhat * cng_ref[...] + cnb_ref[...]
        dgate = dgate_ref[...]
        out_ref[:, HALF:2 * HALF] = ((dn * _sigmoid(dn)) * (dgate * _sigmoid(dgate))).astype(BF16)

    vec = _full_spec((1, HALF))
    return pl.pallas_call(
        body, grid=(S // TR,),
        in_specs=_odd_in_specs() + [vec, vec, _full_spec((4, 128, 128)), _full_spec((4, 128, 128)),
                                    _full_spec((HALO, HALF)), vec, vec, vec],
        out_specs=pl.BlockSpec((TR, 2048), lambda i: (i, 0)),
        out_shape=jax.ShapeDtypeStruct((S, 2048), BF16),
        scratch_shapes=[pltpu.VMEM((TR, HALF), BF16), pltpu.VMEM((TR, HALF), F32),
                        pltpu.VMEM((HALO + TR, HALF), F32), pltpu.VMEM((TR, HALF), F32)],
        compiler_params=_params(("parallel",), VMEM_BIG), name="odd_fwd",
    )(z1, z1, z1, z1, z1, z1, z1, z1, sgu_g, sgu_b, sgu_w, sgu_bb, conv_w, conv_b, cn_g, cn_b)


def _odd_bwd_a(z1, dycat, sgu_g, sgu_b, sgu_w, sgu_bb, conv_w, conv_b, cn_g, cn_b):
    def body(u_ref, v_ref, cg_ref, dval_ref, dglu_ref, dgate_ref, hval_ref, hglu_ref, dy_ref,
             g_ref, b_ref, w_ref, bb_ref, cw_ref, cb_ref, cng_ref, cnb_ref,
             dz_ref, ddc_ref, dw_ref, dbb_ref, dg_ref, db_ref, dcng_ref, dcnb_ref, dcb_ref,
             vn_s, s_s, xw, dcs, ds_s, dvn_s):
        i = pl.program_id(0)
        vhat, rv = _ln_stats(v_ref[...])
        g = g_ref[...]
        vn_s[...] = (vhat * g + b_ref[...]).astype(BF16)
        _sgu_gate(vn_s, s_s, w_ref, bb_ref)
        silu_c, dsilu_c = _silu_and_grad(cg_ref[...])
        dyc = dy_ref[:, 0:HALF]
        u = u_ref[...]
        s = s_s[...]
        dz_ref[:, 0:HALF] = (dyc * s * silu_c).astype(BF16)
        dz_ref[:, 2 * HALF:3 * HALF] = (dyc * u * s * dsilu_c).astype(BF16)
        ds_s[...] = dyc * u * silu_c

        @pl.when(i == 0)
        def _():
            dw_ref[...] = jnp.zeros_like(dw_ref)
            dbb_ref[...] = jnp.zeros_like(dbb_ref)

        tril = lax.broadcasted_iota(jnp.int32, (128, 128), 0) >= lax.broadcasted_iota(jnp.int32, (128, 128), 1)
        for h in range(4):
            wm = _tril_bf16(w_ref[h])
            for ch in range(NCHUNK):
                rows, cols = slice(ch * 128, (ch + 1) * 128), slice(h * SGU_CH, (h + 1) * SGU_CH)
                ds = ds_s[rows, cols]
                dsb = ds.astype(BF16)
                dw_ref[h] += jnp.where(tril, _dot(dsb, vn_s[rows, cols], NT), 0.0)
                dbb_ref[h] += jnp.broadcast_to(jnp.sum(ds, axis=1, keepdims=True), (128, 128))
                dvn_s[rows, cols] = _dot(wm, dsb, TN)
        dvn = dvn_s[...]
        _acc_rows(dg_ref, dvn * vhat, i)
        _acc_rows(db_ref, dvn, i)
        dz_ref[:, HALF:2 * HALF] = _ln_bwd(dvn, vhat, rv, g).astype(BF16)

        _conv_fwd(i, dval_ref, dglu_ref, hval_ref, hglu_ref, cw_ref, cb_ref, xw, dcs)
        dhat, rd = _ln_stats(dcs[...])
        cng = cng_ref[...]
        silu_n, dsilu_n = _silu_and_grad(dhat * cng + cnb_ref[...])
        silu_g, dsilu_g = _silu_and_grad(dgate_ref[...])
        dyd = dy_ref[:, HALF:2 * HALF]
        dz_ref[:, 5 * HALF:6 * HALF] = (dyd * silu_n * dsilu_g).astype(BF16)
        ddn = dyd * silu_g * dsilu_n
        _acc_rows(dcng_ref, ddn * dhat, i)
        _acc_rows(dcnb_ref, ddn, i)
        ddc = _ln_bwd(ddn, dhat, rd, cng)
        ddc_ref[...] = ddc
        _acc_rows(dcb_ref, ddc, i)

    vec = _full_spec((1, HALF))
    sq = _full_spec((4, 128, 128))
    return pl.pallas_call(
        body, grid=(S // TR,),
        in_specs=_odd_in_specs() + [pl.BlockSpec((TR, 2048), lambda i: (i, 0)),
                                    vec, vec, sq, sq, _full_spec((HALO, HALF)), vec, vec, vec],
        out_specs=[pl.BlockSpec((TR, ODD_IN), lambda i: (i, 0)), pl.BlockSpec((TR, HALF), lambda i: (i, 0)),
                   sq, sq, vec, vec, vec, vec, vec],
        out_shape=[jax.ShapeDtypeStruct((S, ODD_IN), BF16), jax.ShapeDtypeStruct((S, HALF), F32),
                   jax.ShapeDtypeStruct((4, 128, 128), F32), jax.ShapeDtypeStruct((4, 128, 128), F32)]
                  + [jax.ShapeDtypeStruct((1, HALF), F32)] * 5,
        scratch_shapes=[pltpu.VMEM((TR, HALF), BF16), pltpu.VMEM((TR, HALF), F32),
                        pltpu.VMEM((HALO + TR, HALF), F32), pltpu.VMEM((TR, HALF), F32),
                        pltpu.VMEM((TR, HALF), F32), pltpu.VMEM((TR, HALF), F32)],
        compiler_params=_params(("arbitrary",), VMEM_BIG), name="odd_bwd_a",
    )(z1, z1, z1, z1, z1, z1, z1, z1, dycat, sgu_g, sgu_b, sgu_w, sgu_bb, conv_w, conv_b, cn_g, cn_b)


def _odd_bwd_b(z1, ddc, dz1, conv_w):
    nt = S // TR

    def body(dval_ref, dglu_ref, hval_ref, hglu_ref, ddc_ref, hddc_ref, cw_ref, dz_in_ref,
             dz_ref, dcw_ref, xw, dwin, dxs):
        del dz_in_ref
        i, j = pl.program_id(0), pl.program_id(1)
        sg = _sigmoid(dglu_ref[...])
        dval = dval_ref[...]

        @pl.when(j == 0)
        def _():
            halo = hval_ref[...] * _sigmoid(hglu_ref[...])
            xw[0:HALO, :] = jnp.where(i > 0, halo, 0.0)
            xw[HALO:HALO + TR, :] = dval * sg
            dwin[0:TR, :] = ddc_ref[...]
            dwin[TR:TR + HALO, :] = jnp.where(i < nt - 1, hddc_ref[...], 0.0)

            @pl.when(i == 0)
            def _():
                dcw_ref[...] = jnp.zeros_like(dcw_ref)

            for rb in range(TR // SUB):
                acc = jnp.zeros((SUB, HALF), F32)
                for k in range(CONV_K):
                    acc = acc + cw_ref[k:k + 1, :] * dwin[pl.ds(rb * SUB + (CONV_K - 1) - k, SUB), :]
                dxs[rb * SUB:(rb + 1) * SUB, :] = acc
            for k in range(CONV_K):
                acc = jnp.zeros((SUB, HALF), F32)
                for rb in range(TR // SUB):
                    acc = acc + dwin[rb * SUB:(rb + 1) * SUB, :] * xw[pl.ds(rb * SUB + HALO - (CONV_K - 1) + k, SUB), :]
                dcw_ref[k:k + 1, :] += jnp.sum(acc, axis=0, keepdims=True)
            dz_ref[...] = (dxs[...] * sg).astype(BF16)

        @pl.when(j == 1)
        def _():
            dz_ref[...] = (dxs[...] * dval * sg * (1.0 - sg)).astype(BF16)

    col = lambda c: pl.BlockSpec((TR, HALF), lambda i, j: (i, c))
    prev = lambda c: pl.BlockSpec((HALO, HALF), lambda i, j: (jnp.maximum(i * (TR // HALO) - 1, 0), c))
    nxt = pl.BlockSpec((HALO, HALF), lambda i, j: (jnp.minimum((i + 1) * (TR // HALO), S // HALO - 1), 0))
    return pl.pallas_call(
        body, grid=(nt, 2),
        in_specs=[col(3), col(4), prev(3), prev(4), pl.BlockSpec((TR, HALF), lambda i, j: (i, 0)), nxt,
                  _full_spec((HALO, HALF)), pl.BlockSpec(memory_space=pl.ANY)],
        out_specs=[pl.BlockSpec((TR, HALF), lambda i, j: (i, 3 + j)), _full_spec((HALO, HALF))],
        out_shape=[jax.ShapeDtypeStruct((S, ODD_IN), BF16), jax.ShapeDtypeStruct((HALO, HALF), F32)],
        scratch_shapes=[pltpu.VMEM((HALO + TR, HALF), F32), pltpu.VMEM((TR + HALO, HALF), F32),
                        pltpu.VMEM((TR, HALF), F32)],
        input_output_aliases={7: 0},
        compiler_params=_params(("arbitrary", "arbitrary"), VMEM_BIG), name="odd_bwd_b",
    )(z1, z1, z1, z1, ddc, ddc, conv_w, dz1)


def _cast_bf16(w, name):
    r, c = w.shape
    tr = min(r, 256)
    def body(i_ref, o_ref):
        o_ref[...] = i_ref[...].astype(BF16)

    return pl.pallas_call(
        body, grid=(r // tr,), in_specs=[pl.BlockSpec((tr, c), lambda i: (i, 0))],
        out_specs=pl.BlockSpec((tr, c), lambda i: (i, 0)), out_shape=jax.ShapeDtypeStruct((r, c), BF16),
        compiler_params=_params(("parallel",)), name=name,
    )(w)


def _adamw(w, g, m, v):
    m = ADAM_B1 * m + (1.0 - ADAM_B1) * g
    v = ADAM_B2 * v + (1.0 - ADAM_B2) * (g * g)
    m_hat = m / (1.0 - ADAM_B1 ** ADAM_STEP)
    v_hat = v / (1.0 - ADAM_B2 ** ADAM_STEP)
    delta = -ADAM_LR * (m_hat / (jnp.sqrt(v_hat) + ADAM_EPS) + ADAM_WD * w)
    return delta, m, v


def _adam_reduce(parts, w, m, v, name, dep=None):
    r, c = w.shape
    tr = min(r, 128)
    deps = [] if dep is None else [dep]

    def body(p_ref, w_ref, m_ref, v_ref, *rest):
        g_ref, d_ref, nm_ref, nv_ref = rest[len(deps):]
        g = p_ref[0].astype(F32)
        for d in range(1, NDEV):
            g = g + p_ref[d].astype(F32)
        g_ref[...] = g
        d_ref[...], nm_ref[...], nv_ref[...] = _adamw(w_ref[...], g, m_ref[...], v_ref[...])

    spec = pl.BlockSpec((tr, c), lambda i: (i, 0))
    return pl.pallas_call(
        body, grid=(r // tr,),
        in_specs=[pl.BlockSpec((NDEV, tr, c), lambda i: (0, i, 0)), spec, spec, spec] + [ANY_SPEC] * len(deps),
        out_specs=[spec] * 4, out_shape=[jax.ShapeDtypeStruct((r, c), F32)] * 4,
        compiler_params=_params(("parallel",), VMEM_BIG), name=name,
    )(parts, w, m, v, *deps)


def _sum_parts(parts, name, dep=None):
    r = parts.shape[1]
    tr = 8
    for cand in (512, 256, 128, 64, 32, 16, 8):
        if r % cand == 0:
            tr = cand
            break
    deps = [] if dep is None else [dep]

    def body(p_ref, *rest):
        g = p_ref[0]
        for d in range(1, NDEV):
            g = g + p_ref[d]
        rest[-1][...] = g

    return pl.pallas_call(
        body, grid=(r // tr,), in_specs=[pl.BlockSpec((NDEV, tr, 128), lambda i: (0, i, 0))] + [ANY_SPEC] * len(deps),
        out_specs=pl.BlockSpec((tr, 128), lambda i: (i, 0)), out_shape=jax.ShapeDtypeStruct((r, 128), F32),
        compiler_params=_params(("parallel",)), name=name,
    )(parts, *deps)


def _adam_plain(w, g, m, v, name):
    r, c = w.shape

    def body(w_ref, g_ref, m_ref, v_ref, d_ref, nm_ref, nv_ref):
        d_ref[...], nm_ref[...], nv_ref[...] = _adamw(w_ref[...], g_ref[...], m_ref[...], v_ref[...])

    spec = pl.BlockSpec((r, c), lambda i: (0, 0))
    return pl.pallas_call(
        body, grid=(1,), in_specs=[spec] * 4, out_specs=[spec] * 3,
        out_shape=[jax.ShapeDtypeStruct((r, c), F32)] * 3,
        compiler_params=_params(("arbitrary",)), name=name,
    )(w, g, m, v)


MASKS = [(mx, my, mc) for mx in (0, 1) for my in (0, 1) for mc in (0, 1)][1:]


def _exchange(arrays, scatter, name):
    nt = len(arrays)
    out_shape = [jax.ShapeDtypeStruct(((NDEV,) + a.shape) if not scatter else a.shape, a.dtype) for a in arrays]

    def body(*refs):
        ins, outs = refs[:nt], refs[nt:2 * nt]
        send_sems, recv_sems, local_sems = refs[2 * nt:]
        x, y, c = lax.axis_index("x"), lax.axis_index("y"), lax.axis_index("c")
        me = 4 * x + 2 * y + c
        copies = []
        for t in range(nt):
            src_own = ins[t].at[me] if scatter else ins[t]
            loc = pltpu.make_async_copy(src_own, outs[t].at[me], local_sems.at[t])
            loc.start()
            copies.append(loc)
            for k, (mx, my, mc) in enumerate(MASKS):
                px, py, pc = (x + mx) % 2, (y + my) % 2, (c + mc) % 2
                peer = 4 * px + 2 * py + pc
                src = ins[t].at[peer] if scatter else ins[t]
                rc = pltpu.make_async_remote_copy(
                    src_ref=src, dst_ref=outs[t].at[me], send_sem=send_sems.at[t, k], recv_sem=recv_sems.at[t, k],
                    device_id=(px, py, pc), device_id_type=MESH)
                rc.start()
                copies.append(rc)
        for cp in copies:
            cp.wait()

    hbm = pl.BlockSpec(memory_space=pl.ANY)
    return pl.pallas_call(
        body, in_specs=[hbm] * nt, out_specs=[hbm] * nt, out_shape=out_shape,
        scratch_shapes=[pltpu.SemaphoreType.DMA((nt, 7)), pltpu.SemaphoreType.DMA((nt, 7)),
                        pltpu.SemaphoreType.DMA((nt,))],
        name=name,
    )(*arrays)


SEM_SPEC = pl.BlockSpec(memory_space=pltpu.SEMAPHORE)
EFFECT = pltpu.SideEffectType.DATAFLOW_SIDE_EFFECTING


def _direct_plan(scatter):
    def plan(x, y, c, srcs, lands):
        me = 4 * x + 2 * y + c
        local, remote = [], []
        for src, land in zip(srcs, lands):
            local.append((src.at[me] if scatter else src, land.at[me]))
            for mx, my, mc in MASKS:
                px, py, pc = (x + mx) % 2, (y + my) % 2, (c + mc) % 2
                blk = src.at[4 * px + 2 * py + pc] if scatter else src
                remote.append((blk, land.at[me], (px, py, pc)))
        return local, remote
    return plan


def _split_start(name, srcs, land_shapes, plan, n_local, n_remote, dep=None):
    ns, nl = len(srcs), len(land_shapes)
    deps = [] if dep is None else [dep]
    lands = [lax.empty(s.shape, s.dtype) for s in land_shapes]

    def body(*refs):
        ins, lz = refs[:ns], refs[ns:ns + nl]
        outs = refs[ns + nl + len(deps):]
        send_sems, recv_sems, token, local_sems = outs[0], outs[1], outs[2 + ns + nl], outs[3 + ns + nl]
        local, remote = plan(lax.axis_index("x"), lax.axis_index("y"), lax.axis_index("c"), ins, lz)
        own = [pltpu.make_async_copy(src, dst, local_sems.at[i]) for i, (src, dst) in enumerate(local)]
        for cp in own:
            cp.start()
        for cp in own:
            cp.wait()
        for k, (src, dst, peer) in enumerate(remote):
            pltpu.make_async_remote_copy(src_ref=src, dst_ref=dst, send_sem=send_sems.at[k], recv_sem=recv_sems.at[k],
                                         device_id=peer, device_id_type=MESH).start()
        token[...] = jnp.zeros_like(token)

    hbm = lambda a: pltpu.HBM(a.shape, a.dtype)
    outs = pl.pallas_call(
        body, name=name,
        out_shape=(pltpu.SemaphoreType.DMA((n_remote,)), pltpu.SemaphoreType.DMA((n_remote,)),
                   *[hbm(a) for a in srcs], *[hbm(a) for a in lands], jax.ShapeDtypeStruct((8, 128), F32)),
        in_specs=[ANY_SPEC] * (ns + nl + len(deps)),
        out_specs=(SEM_SPEC, SEM_SPEC, *[ANY_SPEC] * (ns + nl), pl.BlockSpec(memory_space=pltpu.VMEM)),
        scratch_shapes=[pltpu.SemaphoreType.DMA((n_local,))],
        input_output_aliases={i: 2 + i for i in range(ns + nl)},
        compiler_params=pltpu.CompilerParams(has_side_effects=EFFECT),
    )(*[pltpu.with_memory_space_constraint(a, pltpu.HBM) for a in srcs],
      *[pltpu.with_memory_space_constraint(a, pltpu.HBM) for a in lands], *deps)
    return dict(sems=outs[:2], srcs=outs[2:2 + ns], lands=outs[2 + ns:2 + ns + nl], token=outs[-1],
                plan=plan, n_remote=n_remote)


def _split_wait(name, handle, after):
    srcs, lands, plan = handle["srcs"], handle["lands"], handle["plan"]
    ns, nl = len(srcs), len(lands)

    def body(*refs):
        ins, lz = refs[:ns], refs[ns:ns + nl]
        send_sems, recv_sems = refs[ns + nl], refs[ns + nl + 1]
        _, remote = plan(lax.axis_index("x"), lax.axis_index("y"), lax.axis_index("c"), ins, lz)
        for k, (src, dst, peer) in enumerate(remote):
            cp = pltpu.make_async_remote_copy(src_ref=src, dst_ref=dst, send_sem=send_sems.at[k],
                                              recv_sem=recv_sems.at[k], device_id=peer, device_id_type=MESH)
            cp.wait_send()
            cp.wait_recv()

    hbm = lambda a: pltpu.HBM(a.shape, a.dtype)
    outs = pl.pallas_call(
        body, name=name, out_shape=(*[hbm(a) for a in srcs], *[hbm(a) for a in lands]),
        in_specs=[ANY_SPEC] * (ns + nl) + [SEM_SPEC, SEM_SPEC, ANY_SPEC], out_specs=tuple([ANY_SPEC] * (ns + nl)),
        input_output_aliases={i: i for i in range(ns + nl)},
        compiler_params=pltpu.CompilerParams(has_side_effects=EFFECT),
    )(*srcs, *lands, *handle["sems"], after)
    return list(outs[ns:])


def _sc_exchange(name, collective_id, arrays, scatter):
    nt = len(arrays)
    out_type = [jax.ShapeDtypeStruct(a.shape if scatter else (NDEV,) + a.shape, a.dtype) for a in arrays]

    def body(*refs):
        ins, outs = refs[:nt], refs[nt:2 * nt]
        send_sems, recv_sems, local_sems = refs[2 * nt:3 * nt], refs[3 * nt:4 * nt], refs[4 * nt:5 * nt]
        x, y, c = lax.axis_index("x"), lax.axis_index("y"), lax.axis_index("c")
        peers = [(mx + x - 2 * mx * x, my + y - 2 * my * y, mc + c - 2 * mc * c) for mx, my, mc in MASKS]
        barrier = pltpu.get_barrier_semaphore()
        for peer in peers:
            pl.semaphore_signal(barrier, inc=1, device_id=peer, device_id_type=MESH)
        pl.semaphore_wait(barrier, len(peers))
        me = 4 * x + 2 * y + c
        own = []
        for t in range(nt):
            cp = pltpu.make_async_copy(ins[t].at[me] if scatter else ins[t], outs[t].at[me], local_sems[t])
            cp.start()
            own.append(cp)
            for px, py, pc in peers:
                src = ins[t].at[4 * px + 2 * py + pc] if scatter else ins[t]
                pltpu.make_async_remote_copy(src_ref=src, dst_ref=outs[t].at[me], send_sem=send_sems[t],
                                             recv_sem=recv_sems[t], device_id=(px, py, pc), device_id_type=MESH).start()
        for t in range(nt):
            own[t].wait()
            seven = outs[t].at[pl.ds(0, NDEV - 1)]
            drain = pltpu.make_async_remote_copy(src_ref=seven, dst_ref=seven, send_sem=send_sems[t],
                                                 recv_sem=recv_sems[t], device_id=(x, y, c), device_id_type=MESH)
            drain.wait_send()
            drain.wait_recv()

    return pl.kernel(
        body, out_type=out_type, mesh=plsc.ScalarSubcoreMesh(axis_name="sequencer", num_cores=1),
        scratch_types=[pltpu.SemaphoreType.DMA] * (3 * nt),
        compiler_params=pltpu.CompilerParams(collective_id=collective_id), name=name,
    )(*arrays)


def _sc_gather_two_level(name, collective_id, arrays):
    nt = len(arrays)
    out_type = [jax.ShapeDtypeStruct((NDEV,) + a.shape, a.dtype) for a in arrays]

    def body(*refs):
        ins, outs = refs[:nt], refs[nt:2 * nt]
        sems = refs[2 * nt:]
        send_sems, sib_sems, local_sems = sems[:nt], sems[nt:2 * nt], sems[2 * nt:3 * nt]
        ici_sems = [sems[3 * nt + 3 * t:3 * nt + 3 * t + 3] for t in range(nt)]
        x, y, c = lax.axis_index("x"), lax.axis_index("y"), lax.axis_index("c")
        sibling = (x, y, 1 - c)
        chips = [(1 - x, y), (x, 1 - y), (1 - x, 1 - y)]
        barrier = pltpu.get_barrier_semaphore()
        for peer in [sibling] + [(cx, cy, c) for cx, cy in chips]:
            pl.semaphore_signal(barrier, inc=1, device_id=peer, device_id_type=MESH)
        pl.semaphore_wait(barrier, 4)
        me = 4 * x + 2 * y + c

        def push(t, src, slot, recv_sem, to):
            pltpu.make_async_remote_copy(src_ref=src, dst_ref=outs[t].at[slot], send_sem=send_sems[t],
                                         recv_sem=recv_sem, device_id=to, device_id_type=MESH).start()

        own = []
        for t in range(nt):
            cp = pltpu.make_async_copy(ins[t], outs[t].at[me], local_sems[t])
            cp.start()
            own.append(cp)
            for j, (cx, cy) in enumerate(chips):
                push(t, ins[t], me, ici_sems[t][j], (cx, cy, c))
            push(t, ins[t], me, sib_sems[t], sibling)
        for t in range(nt):
            for j, (cx, cy) in enumerate(chips):
                slot = 4 * cx + 2 * cy + c
                landed = outs[t].at[slot]
                pltpu.make_async_remote_copy(src_ref=landed, dst_ref=landed, send_sem=send_sems[t],
                                             recv_sem=ici_sems[t][j], device_id=(cx, cy, c),
                                             device_id_type=MESH).wait_recv()
                push(t, landed, slot, sib_sems[t], sibling)
        for t in range(nt):
            own[t].wait()
            four, seven = outs[t].at[pl.ds(0, 4)], outs[t].at[pl.ds(0, 7)]
            pltpu.make_async_remote_copy(src_ref=four, dst_ref=four, send_sem=send_sems[t], recv_sem=sib_sems[t],
                                         device_id=sibling, device_id_type=MESH).wait_recv()
            pltpu.make_async_remote_copy(src_ref=seven, dst_ref=seven, send_sem=send_sems[t], recv_sem=sib_sems[t],
                                         device_id=sibling, device_id_type=MESH).wait_send()

    return pl.kernel(
        body, out_type=out_type, mesh=plsc.ScalarSubcoreMesh(axis_name="sequencer", num_cores=1),
        scratch_types=[pltpu.SemaphoreType.DMA] * (6 * nt),
        compiler_params=pltpu.CompilerParams(collective_id=collective_id), name=name,
    )(*arrays)


SMALL = {
    "e_pre_norm": ((2048,), None), "e_pool_w": ((4, 256, 256), 1), "e_pool_scale": ((1024,), None),
    "e_post_norm": ((2048,), None), "o_pre_norm": ((2048,), 0), "o_sgu_norm_g": ((1024,), 0),
    "o_sgu_norm_b": ((1024,), 0), "o_sgu_w": ((4, 128, 128), None), "o_sgu_b": ((4, 128), None),
    "o_conv_w": ((31, 1024), 1), "o_conv_b": ((1024,), 0), "o_conv_norm_g": ((1024,), 0),
    "o_conv_norm_b": ((1024,), 0), "o_post_norm": ((2048,), 0),
}
SMALL_SHARDED = [n for n, (_, ax) in SMALL.items() if ax is not None]


def _shard_shape(name):
    shape, ax = SMALL[name]
    if ax is None:
        return shape
    return tuple(s // NDEV if i == ax else s for i, s in enumerate(shape))


def _pack(arrs, row_multiple=1):
    flat = jnp.concatenate([a.reshape(-1) for a in arrs])
    pad = -flat.shape[0] % (128 * row_multiple)
    return jnp.concatenate([flat, jnp.zeros((pad,), F32)]).reshape(-1, 128)


def _unpack(buf, shapes):
    flat = buf.reshape(-1)
    out, off = [], 0
    for shp in shapes:
        n = int(np.prod(shp))
        out.append(flat[off:off + n].reshape(shp))
        off += n
    return out


def _take_shard(full, name, me):
    shape, ax = SMALL[name]
    if ax is None:
        return full
    n = shape[ax] // NDEV
    return lax.dynamic_slice_in_dim(full, me * n, n, axis=ax)


BIG = ("e_w_in", "e_w_out", "o_w_in", "o_w_out")
WEIGHTS = ["e_pre_norm", "e_w_in", "e_pool_w", "e_pool_scale", "e_w_out", "e_post_norm", "o_pre_norm", "o_w_in",
           "o_sgu_norm_g", "o_sgu_norm_b", "o_sgu_w", "o_sgu_b", "o_conv_w", "o_conv_b", "o_conv_norm_g",
           "o_conv_norm_b", "o_w_out", "o_post_norm"]


def kernel(x, e_pre_norm, e_w_in, e_pool_w, e_pool_scale, e_w_out, e_post_norm, o_pre_norm, o_w_in, o_sgu_norm_g, o_sgu_norm_b, o_sgu_w, o_sgu_b, o_conv_w, o_conv_b, o_conv_norm_g, o_conv_norm_b, o_w_out, o_post_norm, loss_target, m_e_pre_norm, m_e_w_in, m_e_pool_w, m_e_pool_scale, m_e_w_out, m_e_post_norm, m_o_pre_norm, m_o_w_in, m_o_sgu_norm_g, m_o_sgu_norm_b, m_o_sgu_w, m_o_sgu_b, m_o_conv_w, m_o_conv_b, m_o_conv_norm_g, m_o_conv_norm_b, m_o_w_out, m_o_post_norm, v_e_pre_norm, v_e_w_in, v_e_pool_w, v_e_pool_scale, v_e_w_out, v_e_post_norm, v_o_pre_norm, v_o_w_in, v_o_sgu_norm_g, v_o_sgu_norm_b, v_o_sgu_w, v_o_sgu_b, v_o_conv_w, v_o_conv_b, v_o_conv_norm_g, v_o_conv_norm_b, v_o_w_out, v_o_post_norm):
    given = dict(locals())
    w = {n: given[n][0] for n in WEIGHTS}
    m = {n: given["m_" + n][0] for n in WEIGHTS}
    v = {n: given["v_" + n][0] for n in WEIGHTS}
    me = 4 * lax.axis_index("x") + 2 * lax.axis_index("y") + lax.axis_index("c")
    x, target = x[0], loss_target[0]
    row = lambda a: a.reshape(1, -1)

    bf = {n: _cast_bf16(w[n], "cast_" + n) for n in BIG}
    wg_e_in, small_rows = _sc_gather_two_level("gather_a", 0, [bf["e_w_in"], _pack([w[n] for n in SMALL_SHARDED])])
    wg_e_out, wg_o_in, wg_o_out = _sc_gather_two_level("gather_b", 1, [bf["e_w_out"], bf["o_w_in"], bf["o_w_out"]])
    h0 = _pre0_fwd(x, row(w["e_pre_norm"]))
    p = {n: w[n] for n in SMALL if SMALL[n][1] is None}
    small_rows = small_rows.reshape(NDEV, -1)
    off = 0
    for n in SMALL_SHARDED:
        shp, ax = _shard_shape(n), SMALL[n][1]
        cnt = int(np.prod(shp))
        blk = small_rows[:, off:off + cnt].reshape((NDEV,) + shp)
        p[n] = jnp.moveaxis(blk, 0, ax).reshape(SMALL[n][0])
        off += cnt
    tabs = _rope_tables()
    pool_w_bf = p["e_pool_w"].astype(BF16)
    sgu_bb = jnp.broadcast_to(p["o_sgu_b"][:, :, None], (4, 128, 128))
    conv_w = jnp.concatenate([p["o_conv_w"], jnp.zeros((HALO - CONV_K, HALF), F32)], axis=0)
    odd_p = (row(p["o_sgu_norm_g"]), row(p["o_sgu_norm_b"]), p["o_sgu_w"], sgu_bb, conv_w,
             row(p["o_conv_b"]), row(p["o_conv_norm_g"]), row(p["o_conv_norm_b"]))

    z0 = _mm_in(h0, wg_e_in, "mm_z0")
    ycat0 = _pool_fwd(z0, pool_w_bf, row(p["e_pool_scale"]))
    ycat0 = _attn_fwd(z0, ycat0, tabs)
    w_out_e, w_out_o = wg_e_out.reshape(2048, D), wg_o_out.reshape(2048, D)
    y0 = _mm_out(ycat0, w_out_e, "mm_y0")
    x1, h1 = _post0_fwd(x, y0, row(p["e_post_norm"]), row(p["o_pre_norm"]))
    z1 = _mm_in(h1, wg_o_in, "mm_z1")
    ycat1 = _odd_fwd(z1, *odd_p)
    y1 = _mm_out(ycat1, w_out_o, "mm_y1")

    g = {}
    loss, dx2, dy1, g["o_post_norm"] = _post1_bwd(y1, x1, target, row(p["o_post_norm"]))
    loss = lax.psum(loss[0, 0], ("x", "y", "c"))
    parts = {}
    dw = _mm_out_dw(ycat1, dy1, "mm_dwout1").reshape(NDEV, 256, D)
    parts["o_w_out"], = _sc_exchange("scatter_o_w_out", 2, [dw], True)
    dycat1 = _mm_out_dx(dy1, w_out_o, "mm_dycat1", dw)
    dz1, ddc, g["o_sgu_w"], d_sgu_bb, g["o_sgu_norm_g"], g["o_sgu_norm_b"], g["o_conv_norm_g"], \
        g["o_conv_norm_b"], g["o_conv_b"] = _odd_bwd_a(z1, dycat1, *odd_p)
    dz1, d_conv_w = _odd_bwd_b(z1, ddc, dz1, conv_w)
    g["o_sgu_b"] = d_sgu_bb[:, :, 0]
    g["o_conv_w"] = d_conv_w[:CONV_K]
    grads, deltas, new_m, new_v = {}, {}, {}, {}

    def adam(n, dep):
        grads[n], deltas[n], new_m[n], new_v[n] = _adam_reduce(parts[n], w[n], m[n], v[n], "adam_" + n, dep)
        return new_v[n]

    pin = adam("o_w_out", d_conv_w)
    dw = _mm_in_dw(h1, dz1, ODD_IN // NDEV, "mm_dwin1", pin)
    parts["o_w_in"], = _sc_exchange("scatter_o_w_in", 3, [dw], True)
    dh1 = _mm_in_dx(dz1, wg_o_in, "mm_dh1", dw)
    dx1, dy0, g["o_pre_norm"], g["e_post_norm"] = _mid_bwd(dx2, dh1, x1, y0, row(p["o_pre_norm"]),
                                                           row(p["e_post_norm"]))
    dw = _mm_out_dw(ycat0, dy0, "mm_dwout0").reshape(NDEV, 256, D)
    parts["e_w_out"], = _sc_exchange("scatter_e_w_out", 4, [dw], True)
    dycat0 = _mm_out_dx(dy0, w_out_e, "mm_dycat0", dw)
    da_in, da_gate, g["e_pool_w"], g["e_pool_scale"] = _pool_bwd(z0, dycat0, pool_w_bf, row(p["e_pool_scale"]))
    dq, dk, dv, dbg = _attn_bwd(z0, dycat0, tabs)
    dz0 = jnp.concatenate([da_in, da_gate, dq, dk, dv, dbg], axis=1)
    late = [n for n in SMALL if n != "e_pre_norm"]
    pin = adam("e_w_out", adam("o_w_in", dbg))
    dw = _mm_in_dw(h0, dz0, EVEN_IN // NDEV, "mm_dwin0", pin)
    parts["e_w_in"], = _sc_exchange("scatter_e_w_in", 5, [dw], True)
    recv_small, = _sc_exchange("gather_small_grads", 6, [_pack([g[n].reshape(SMALL[n][0]) for n in late], 512)], False)
    dh0 = _mm_in_dx(dz0, wg_e_in, "mm_dh0", dw)
    grad_x, g["e_pre_norm"] = _pre0_bwd(dx1, dh0, x, row(p["e_pre_norm"]))
    last, = _sc_exchange("gather_e_pre_norm_grad", 7, [g["e_pre_norm"].reshape(16, 128)], False)

    g_small = dict(zip(late, _unpack(_sum_parts(recv_small, "sum_small_grads"), [SMALL[n][0] for n in late])))
    pin = adam("e_w_in", grad_x)
    g_small["e_pre_norm"] = _sum_parts(last, "sum_e_pre_norm_grad", pin).reshape(2048)
    for n in SMALL:
        grads[n] = _take_shard(g_small[n], n, me)
    names = list(SMALL)
    shapes = [_shard_shape(n) for n in names]
    d_pack, m_pack, v_pack = _adam_plain(_pack([w[n] for n in names]), _pack([grads[n] for n in names]),
                                         _pack([m[n] for n in names]), _pack([v[n] for n in names]), "adam_small")
    for n, d_, m_, v_ in zip(names, _unpack(d_pack, shapes), _unpack(m_pack, shapes), _unpack(v_pack, shapes)):
        deltas[n], new_m[n], new_v[n] = d_, m_, v_

    lead = lambda a: a[None]
    return (loss, grad_x[None], *[lead(grads[n]) for n in WEIGHTS], *[lead(deltas[n]) for n in WEIGHTS],
            *[lead(new_m[n]) for n in WEIGHTS], *[lead(new_v[n]) for n in WEIGHTS])
```

```python
import functools

import numpy as np
import jax
import jax.numpy as jnp
from jax import lax
from jax.experimental import pallas as pl
from jax.experimental.pallas import tpu as pltpu
from jax.experimental.pallas import tpu_sc as plsc

F32 = jnp.float32
BF16 = jnp.bfloat16

S = 2048
D = 2048
NDEV = 8
EPS = 1e-6
NEG = -1e30
HEAD_DIM = 128
ROT_DIM = 32
ROPE_THETA = 500000.0
PATTERNS = ((128, 1), (512, 4), (2048, 16))
BLK = 128
EVEN_IN = 12288
ODD_IN = 6144
HALF = 1024
CONV_K = 31
HALO = 32
TR = 256
SUB = 32

ADAM_LR = 0.001
ADAM_B1 = 0.9
ADAM_B2 = 0.999
ADAM_EPS = 1e-08
ADAM_WD = 0.01
ADAM_STEP = 10

VMEM_BIG = 56 * 1024 * 1024
MESH = pl.DeviceIdType.MESH

NN = (((1,), (0,)), ((), ()))
NT = (((1,), (1,)), ((), ()))
TN = (((0,), (0,)), ((), ()))


def _dot(a, b, dn=NN):
    return lax.dot_general(a, b, dn, preferred_element_type=F32)


def _sigmoid(x):
    return 1.0 / (1.0 + jnp.exp(-x))


def _silu_and_grad(x):
    sg = _sigmoid(x)
    return x * sg, sg * (1.0 + x * (1.0 - sg))


def _params(sem, vmem=None):
    return pltpu.CompilerParams(dimension_semantics=sem, vmem_limit_bytes=vmem)


ANY_SPEC = pl.BlockSpec(memory_space=pl.ANY)


def _matmul(a, b, *, dn, grid, a_spec, b_spec, o_spec, out_shape, out_dtype, acc_shape, name, dep=None):
    nk = grid[2]
    deps = [] if dep is None else [dep]

    def body(a_ref, b_ref, *rest):
        o_ref, acc = rest[len(deps)], rest[len(deps) + 1:]
        if nk == 1:
            o_ref[...] = _dot(a_ref[...], b_ref[...], dn).astype(o_ref.dtype)
            return
        acc_ref = acc[0]
        k = pl.program_id(2)

        @pl.when(k == 0)
        def _():
            acc_ref[...] = jnp.zeros_like(acc_ref)

        acc_ref[...] += _dot(a_ref[...], b_ref[...], dn)

        @pl.when(k == nk - 1)
        def _():
            o_ref[...] = acc_ref[...].astype(o_ref.dtype)

    return pl.pallas_call(
        body, grid=grid, in_specs=[a_spec, b_spec] + [ANY_SPEC] * len(deps), out_specs=o_spec,
        out_shape=jax.ShapeDtypeStruct(out_shape, out_dtype),
        scratch_shapes=[] if nk == 1 else [pltpu.VMEM(acc_shape, F32)],
        compiler_params=_params(("parallel", "parallel", "arbitrary"), VMEM_BIG), name=name,
    )(a, b, *deps)


TM = 2048


def _mm_in(h, wg, name):
    nb = wg.shape[2]
    tn = 512 if nb % 512 == 0 else nb
    per = nb // tn
    return _matmul(
        h, wg, dn=NN, grid=(S // TM, NDEV * per, 1),
        a_spec=pl.BlockSpec((TM, D), lambda i, j, k: (i, 0)),
        b_spec=pl.BlockSpec((None, D, tn), lambda i, j, k: (j // per, 0, j % per)),
        o_spec=pl.BlockSpec((TM, tn), lambda i, j, k: (i, j)),
        out_shape=(S, NDEV * nb), out_dtype=F32, acc_shape=(TM, tn), name=name)


def _mm_in_dx(dz, wg, name, dep=None):
    nb = wg.shape[2]
    return _matmul(
        dz, wg, dn=NT, grid=(S // TM, D // 1024, NDEV),
        a_spec=pl.BlockSpec((TM, nb), lambda i, j, k: (i, k)),
        b_spec=pl.BlockSpec((None, 1024, nb), lambda i, j, k: (k, j, 0)),
        o_spec=pl.BlockSpec((TM, 1024), lambda i, j, k: (i, j)),
        out_shape=(S, D), out_dtype=F32, acc_shape=(TM, 1024), name=name, dep=dep)


def _mm_in_dw(h, dz, nb, name, dep=None):
    tn = 512 if nb % 512 == 0 else nb
    per = nb // tn
    return _matmul(
        h, dz, dn=TN, grid=(D // TM, NDEV * per, 1),
        a_spec=pl.BlockSpec((S, TM), lambda i, j, k: (0, i)),
        b_spec=pl.BlockSpec((S, tn), lambda i, j, k: (0, j)),
        o_spec=pl.BlockSpec((None, TM, tn), lambda i, j, k: (j // per, i, j % per)),
        out_shape=(NDEV, D, nb), out_dtype=BF16, acc_shape=(TM, tn), name=name, dep=dep)


def _mm_out(yc, w, name, dep=None):
    return _matmul(
        yc, w, dn=NN, grid=(S // TM, D // 512, 1),
        a_spec=pl.BlockSpec((TM, 2048), lambda i, j, k: (i, 0)),
        b_spec=pl.BlockSpec((2048, 512), lambda i, j, k: (0, j)),
        o_spec=pl.BlockSpec((TM, 512), lambda i, j, k: (i, j)),
        out_shape=(S, D), out_dtype=F32, acc_shape=(TM, 512), name=name, dep=dep)


def _mm_out_dx(dy, w, name, dep=None):
    return _matmul(
        dy, w, dn=NT, grid=(S // TM, 2048 // 512, 1),
        a_spec=pl.BlockSpec((TM, D), lambda i, j, k: (i, 0)),
        b_spec=pl.BlockSpec((512, D), lambda i, j, k: (j, 0)),
        o_spec=pl.BlockSpec((TM, 512), lambda i, j, k: (i, j)),
        out_shape=(S, 2048), out_dtype=F32, acc_shape=(TM, 512), name=name, dep=dep)


def _mm_out_dw(yc, dy, name):
    return _matmul(
        yc, dy, dn=TN, grid=(2048 // TM, D // 512, 1),
        a_spec=pl.BlockSpec((S, TM), lambda i, j, k: (0, i)),
        b_spec=pl.BlockSpec((S, 512), lambda i, j, k: (0, j)),
        o_spec=pl.BlockSpec((TM, 512), lambda i, j, k: (i, j)),
        out_shape=(2048, D), out_dtype=BF16, acc_shape=(TM, 512), name=name)


def _row_spec(w=D):
    return pl.BlockSpec((TR, w), lambda i: (i, 0))


def _vec_spec(w=D):
    return pl.BlockSpec((1, w), lambda i: (0, 0))


def _rms_stats(x):
    r = lax.rsqrt(jnp.mean(x * x, axis=-1, keepdims=True) + EPS)
    return x * r, r


def _rms_bwd(dn, xhat, r, g):
    dxh = dn * g
    return r * (dxh - xhat * jnp.mean(dxh * xhat, axis=-1, keepdims=True))


def _acc_rows(ref, val, i):
    s = jnp.sum(val, axis=0, keepdims=True)

    @pl.when(i == 0)
    def _():
        ref[...] = s

    @pl.when(i > 0)
    def _():
        ref[...] += s


def _pre0_fwd(x, g, dep=None):
    deps = [] if dep is None else [dep]

    def body(x_ref, g_ref, *rest):
        xhat, _ = _rms_stats(x_ref[...])
        rest[-1][...] = (xhat * g_ref[...]).astype(BF16)

    return pl.pallas_call(
        body, grid=(S // TR,), in_specs=[_row_spec(), _vec_spec()] + [ANY_SPEC] * len(deps), out_specs=_row_spec(),
        out_shape=jax.ShapeDtypeStruct((S, D), BF16), compiler_params=_params(("parallel",)), name="pre0_fwd",
    )(x, g, *deps)


def _post0_fwd(x, y0, g_post, g_pre1):
    def body(x_ref, y_ref, gp_ref, g1_ref, x1_ref, h1_ref):
        yhat, _ = _rms_stats(y_ref[...])
        x1 = x_ref[...] + yhat * gp_ref[...]
        x1_ref[...] = x1
        xhat, _ = _rms_stats(x1)
        h1_ref[...] = (xhat * g1_ref[...]).astype(BF16)

    return pl.pallas_call(
        body, grid=(S // TR,), in_specs=[_row_spec(), _row_spec(), _vec_spec(), _vec_spec()],
        out_specs=[_row_spec(), _row_spec()],
        out_shape=[jax.ShapeDtypeStruct((S, D), F32), jax.ShapeDtypeStruct((S, D), BF16)],
        compiler_params=_params(("parallel",)), name="post0_fwd",
    )(x, y0, g_post, g_pre1)


def _post1_bwd(y1, x1, target, g_post):
    def body(y_ref, x1_ref, t_ref, g_ref, loss_ref, dx2_ref, dy_ref, dg_ref):
        i = pl.program_id(0)
        yhat, r = _rms_stats(y_ref[...])
        g = g_ref[...]
        err = x1_ref[...] + yhat * g - t_ref[...]
        part = jnp.sum(jnp.sum(err * err, axis=-1, keepdims=True), axis=0, keepdims=True) * (0.5 / D)
        _acc_rows(loss_ref, jnp.broadcast_to(part, (1, 128)), i)
        dx2 = err * (1.0 / D)
        dx2_ref[...] = dx2
        _acc_rows(dg_ref, dx2 * yhat, i)
        dy_ref[...] = _rms_bwd(dx2, yhat, r, g).astype(BF16)

    return pl.pallas_call(
        body, grid=(S // TR,), in_specs=[_row_spec(), _row_spec(), _row_spec(), _vec_spec()],
        out_specs=[_vec_spec(128), _row_spec(), _row_spec(), _vec_spec()],
        out_shape=[jax.ShapeDtypeStruct((1, 128), F32), jax.ShapeDtypeStruct((S, D), F32),
                   jax.ShapeDtypeStruct((S, D), BF16), jax.ShapeDtypeStruct((1, D), F32)],
        compiler_params=_params(("arbitrary",)), name="post1_bwd",
    )(y1, x1, target, g_post)


def _mid_bwd(dx2, dh1, x1, y0, g_pre1, g_post0):
    def body(dx2_ref, dh_ref, x1_ref, y_ref, g1_ref, gp_ref, dx1_ref, dy_ref, dg1_ref, dgp_ref):
        i = pl.program_id(0)
        xhat, r1 = _rms_stats(x1_ref[...])
        dh = dh_ref[...]
        _acc_rows(dg1_ref, dh * xhat, i)
        dx1 = dx2_ref[...] + _rms_bwd(dh, xhat, r1, g1_ref[...])
        dx1_ref[...] = dx1
        yhat, r0 = _rms_stats(y_ref[...])
        _acc_rows(dgp_ref, dx1 * yhat, i)
        dy_ref[...] = _rms_bwd(dx1, yhat, r0, gp_ref[...]).astype(BF16)

    return pl.pallas_call(
        body, grid=(S // TR,),
        in_specs=[_row_spec(), _row_spec(), _row_spec(), _row_spec(), _vec_spec(), _vec_spec()],
        out_specs=[_row_spec(), _row_spec(), _vec_spec(), _vec_spec()],
        out_shape=[jax.ShapeDtypeStruct((S, D), F32), jax.ShapeDtypeStruct((S, D), BF16),
                   jax.ShapeDtypeStruct((1, D), F32), jax.ShapeDtypeStruct((1, D), F32)],
        compiler_params=_params(("arbitrary",)), name="mid_bwd",
    )(dx2, dh1, x1, y0, g_pre1, g_post0)


def _pre0_bwd(dx1, dh0, x, g):
    def body(dx1_ref, dh_ref, x_ref, g_ref, gx_ref, dg_ref):
        i = pl.program_id(0)
        xhat, r = _rms_stats(x_ref[...])
        dh = dh_ref[...]
        _acc_rows(dg_ref, dh * xhat, i)
        gx_ref[...] = dx1_ref[...] + _rms_bwd(dh, xhat, r, g_ref[...])

    return pl.pallas_call(
        body, grid=(S // TR,), in_specs=[_row_spec(), _row_spec(), _row_spec(), _vec_spec()],
        out_specs=[_row_spec(), _vec_spec()],
        out_shape=[jax.ShapeDtypeStruct((S, D), F32), jax.ShapeDtypeStruct((1, D), F32)],
        compiler_params=_params(("arbitrary",)), name="pre0_bwd",
    )(dx1, dh0, x, g)


POOL_CH = 256


def _pool_apply(a, w, transpose):
    n = a.shape[0]
    row = lax.broadcasted_iota(jnp.int32, a.shape, 0)
    cnt = jnp.minimum(row + 1, w).astype(F32)
    s = a / cnt if transpose else a
    for k in (1, 2, 4, 8):
        if transpose:
            sh = jnp.where(row < n - k, pltpu.roll(s, n - k, 0), 0.0)
        else:
            sh = jnp.where(row >= k, pltpu.roll(s, k, 0), 0.0)
        s = jnp.where(w > k, s + sh, s)
    return s - a if transpose else s / cnt - a


def _pool_fwd(z0, pool_w, pool_scale):
    def body(a_ref, gate_ref, w_ref, sc_ref, out_ref):
        win = jnp.left_shift(2, pl.program_id(0))
        pooled = _pool_apply(a_ref[...], win, False)
        mixed = _dot(pooled.astype(BF16), w_ref[...])
        gate = gate_ref[...]
        out_ref[...] = (mixed * sc_ref[...] * (gate * _sigmoid(gate))).astype(BF16)

    return pl.pallas_call(
        body, grid=(4,),
        in_specs=[pl.BlockSpec((S, POOL_CH), lambda g: (0, g)), pl.BlockSpec((S, POOL_CH), lambda g: (0, 4 + g)),
                  pl.BlockSpec((None, POOL_CH, POOL_CH), lambda g: (g, 0, 0)),
                  pl.BlockSpec((1, POOL_CH), lambda g: (0, g))],
        out_specs=pl.BlockSpec((S, POOL_CH), lambda g: (0, g)),
        out_shape=jax.ShapeDtypeStruct((S, 2048), BF16),
        compiler_params=_params(("parallel",), VMEM_BIG), name="pool_fwd",
    )(z0, z0, pool_w, pool_scale)


def _pool_bwd(z0, dycat, pool_w, pool_scale):
    def body(a_ref, gate_ref, dy_ref, w_ref, sc_ref, da_ref, dgate_ref, dw_ref, dsc_ref):
        win = jnp.left_shift(2, pl.program_id(0))
        pooled = _pool_apply(a_ref[...], win, False).astype(BF16)
        w = w_ref[...]
        mixed = _dot(pooled, w)
        silu, dsilu = _silu_and_grad(gate_ref[...])
        dy = dy_ref[...]
        sc = sc_ref[...]
        dgate_ref[...] = (dy * (mixed * sc) * dsilu).astype(BF16)
        dms = dy * silu
        dsc_ref[...] = jnp.sum(dms * mixed, axis=0, keepdims=True)
        dmixed = (dms * sc).astype(BF16)
        dw_ref[...] = _dot(pooled, dmixed, TN)
        dpooled = _dot(dmixed, w, NT)
        da_ref[...] = _pool_apply(dpooled, win, True).astype(BF16)

    slab = lambda off: pl.BlockSpec((S, POOL_CH), lambda g: (0, off + g))
    return pl.pallas_call(
        body, grid=(4,),
        in_specs=[slab(0), slab(4), slab(0), pl.BlockSpec((None, POOL_CH, POOL_CH), lambda g: (g, 0, 0)),
                  pl.BlockSpec((1, POOL_CH), lambda g: (0, g))],
        out_specs=[slab(0), slab(0), pl.BlockSpec((None, POOL_CH, POOL_CH), lambda g: (g, 0, 0)),
                   pl.BlockSpec((1, POOL_CH), lambda g: (0, g))],
        out_shape=[jax.ShapeDtypeStruct((S, HALF), BF16), jax.ShapeDtypeStruct((S, HALF), BF16),
                   jax.ShapeDtypeStruct((4, POOL_CH, POOL_CH), F32), jax.ShapeDtypeStruct((1, HALF), F32)],
        compiler_params=_params(("parallel",), VMEM_BIG), name="pool_bwd",
    )(z0, z0, dycat, pool_w, pool_scale)


Q_COL, K_COL, V_COL, BG_COL = 2048 // 128, 5120 // 128, 8192 // 128, 11264 // 128
SCALE = HEAD_DIM ** -0.5


def _rope_tables():
    pos = jnp.arange(S, dtype=F32)
    inv_freq = jnp.power(ROPE_THETA, -jnp.arange(0, ROT_DIM, 2, dtype=F32) / ROT_DIM)
    ang = pos[:, None] * inv_freq[None, :]
    cos, sin = jnp.cos(ang), jnp.sin(ang)
    half = ROT_DIM // 2
    zeros = jnp.zeros((S, HEAD_DIM - ROT_DIM), F32)
    c = jnp.concatenate([cos, cos, jnp.ones((S, HEAD_DIM - ROT_DIM), F32)], axis=1)
    a = jnp.concatenate([-sin, jnp.zeros((S, half), F32), zeros], axis=1)
    b = jnp.concatenate([jnp.zeros((S, half), F32), sin, zeros], axis=1)
    return c, a, b


def _rope(t, c, a, b):
    half = ROT_DIM // 2
    return t * c + pltpu.roll(t, HEAD_DIM - half, 1) * a + pltpu.roll(t, half, 1) * b


def _rope_t(d, c, a, b):
    half = ROT_DIM // 2
    return d * c + pltpu.roll(d * a, half, 1) + pltpu.roll(d * b, HEAD_DIM - half, 1)


def _deinterleave(dst, src, dil, cast=None, dst_off=0):
    length = S // dil
    for r in range(dil):
        v = src[...] if dil == 1 else src[pl.ds(r, length, stride=dil), :]
        dst[dst_off + r * length:dst_off + (r + 1) * length, :] = v if cast is None else v.astype(cast)


def _interleave(dst, src, dil, src_off=0):
    length = S // dil
    for r in range(dil):
        if dil == 1:
            dst[...] = src[src_off:src_off + S, :]
        else:
            dst[pl.ds(r, length, stride=dil), :] = src[src_off + r * length:src_off + (r + 1) * length, :]


CU = 4
NUNITS = S // BLK
B_QK = (((2,), (2,)), ((0,), (0,)))
B_PV = (((2,), (1,)), ((0,), (0,)))
B_TN = (((1,), (1,)), ((0,), (0,)))


def _blocks(ref, first):
    return ref[first * BLK:(first + CU) * BLK, :].reshape(CU, BLK, HEAD_DIM)


def _chunk_scores(u0, nb, qd, kdp):
    q = _blocks(qd, u0)
    row = lax.broadcasted_iota(jnp.int32, (CU, BLK, BLK), 1)
    col = lax.broadcasted_iota(jnp.int32, (CU, BLK, BLK), 2)
    s_own = jnp.where(col <= row, _dot(q, _blocks(kdp, u0 + 1), B_QK) * SCALE, NEG)
    if nb == 1:
        return q, s_own, None
    unit = lax.broadcasted_iota(jnp.int32, (CU, BLK, BLK), 0) + u0
    s_prev = jnp.where((col >= row) & ((unit % nb) != 0), _dot(q, _blocks(kdp, u0), B_QK) * SCALE, NEG)
    return q, s_own, s_prev


def _attn_prepare(dil, q_ref, k_ref, v_ref, tabs, tmp, qd, kdp, vdp):
    c, a, b = tabs
    tmp[...] = _rope(q_ref[...], c, a, b)
    _deinterleave(qd, tmp, dil, BF16)
    tmp[...] = _rope(k_ref[...], c, a, b)
    _deinterleave(kdp, tmp, dil, BF16, BLK)
    _deinterleave(vdp, v_ref, dil, BF16, BLK)
    kdp[0:BLK, :] = jnp.zeros((BLK, HEAD_DIM), BF16)
    vdp[0:BLK, :] = jnp.zeros((BLK, HEAD_DIM), BF16)


def _attn_group_fwd(dil, q_ref, k_ref, v_ref, tabs, tmp, qd, kdp, vdp, od, ld, og, lg):
    nb = S // dil // BLK
    _attn_prepare(dil, q_ref, k_ref, v_ref, tabs, tmp, qd, kdp, vdp)
    for u0 in range(0, NUNITS, CU):
        _, s_own, s_prev = _chunk_scores(u0, nb, qd, kdp)
        m = jnp.max(s_own, axis=2, keepdims=True)
        if s_prev is not None:
            m = jnp.maximum(m, jnp.max(s_prev, axis=2, keepdims=True))
        p_own = jnp.exp(s_own - m)
        den = jnp.sum(p_own, axis=2, keepdims=True)
        acc = _dot(p_own.astype(BF16), _blocks(vdp, u0 + 1), B_PV)
        if s_prev is not None:
            p_prev = jnp.exp(s_prev - m)
            den = den + jnp.sum(p_prev, axis=2, keepdims=True)
            acc = acc + _dot(p_prev.astype(BF16), _blocks(vdp, u0), B_PV)
        rows = slice(u0 * BLK, (u0 + CU) * BLK)
        od[rows, :] = (acc / den).reshape(CU * BLK, HEAD_DIM)
        ld[rows, :] = jnp.broadcast_to(m + jnp.log(den), (CU, BLK, HEAD_DIM)).reshape(CU * BLK, HEAD_DIM)
    _interleave(og, od, dil)
    _interleave(lg, ld, dil)


def _group_weights(lgs):
    l0, l1, l2 = lgs[0][...], lgs[1][...], lgs[2][...]
    mx = jnp.maximum(l0, jnp.maximum(l1, l2))
    e0, e1, e2 = jnp.exp(l0 - mx), jnp.exp(l1 - mx), jnp.exp(l2 - mx)
    den = e0 + e1 + e2
    return e0 / den, e1 / den, e2 / den


def _head_spec(base, ngroups_axis=True):
    return pl.BlockSpec((S, HEAD_DIM), lambda h, p: (0, base + (p % 3) * 8 + h))


ATTN_SCRATCH_FWD = [
    pltpu.VMEM((S, HEAD_DIM), F32),
    pltpu.VMEM((S, HEAD_DIM), BF16), pltpu.VMEM((S + BLK, HEAD_DIM), BF16), pltpu.VMEM((S + BLK, HEAD_DIM), BF16),
    pltpu.VMEM((S, HEAD_DIM), F32), pltpu.VMEM((S, HEAD_DIM), F32),
    pltpu.VMEM((S, HEAD_DIM), F32), pltpu.VMEM((S, HEAD_DIM), F32), pltpu.VMEM((S, HEAD_DIM), F32),
    pltpu.VMEM((S, HEAD_DIM), F32), pltpu.VMEM((S, HEAD_DIM), F32), pltpu.VMEM((S, HEAD_DIM), F32),
]


def _attn_fwd(z0, ycat, tabs):
    def body(q_ref, k_ref, v_ref, gate_ref, c_ref, a_ref, b_ref, ycat_ref, out_ref,
             tmp, qd, kd, vd, od, ld, og0, og1, og2, lg0, lg1, lg2):
        del ycat_ref
        p = pl.program_id(1)
        ogs, lgs = (og0, og1, og2), (lg0, lg1, lg2)
        tabs_v = (c_ref[...], a_ref[...], b_ref[...])
        for gi, (_, dil) in enumerate(PATTERNS):
            @pl.when(p == gi)
            def _(gi=gi, dil=dil):
                _attn_group_fwd(dil, q_ref, k_ref, v_ref, tabs_v, tmp, qd, kd, vd, od, ld, ogs[gi], lgs[gi])

        @pl.when(p == 2)
        def _():
            w0, w1, w2 = _group_weights(lgs)
            o = w0 * og0[...] + w1 * og1[...] + w2 * og2[...]
            gate = gate_ref[...]
            out_ref[...] = (o * (gate * _sigmoid(gate))).astype(BF16)

    tab = pl.BlockSpec((S, HEAD_DIM), lambda h, p: (0, 0))
    return pl.pallas_call(
        body, grid=(8, 3),
        in_specs=[_head_spec(Q_COL), _head_spec(K_COL), _head_spec(V_COL),
                  pl.BlockSpec((S, HEAD_DIM), lambda h, p: (0, BG_COL + h)), tab, tab, tab,
                  pl.BlockSpec(memory_space=pl.ANY)],
        out_specs=pl.BlockSpec((S, HEAD_DIM), lambda h, p: (0, 8 + h)),
        out_shape=jax.ShapeDtypeStruct((S, 2048), BF16),
        scratch_shapes=ATTN_SCRATCH_FWD, input_output_aliases={7: 0},
        compiler_params=_params(("parallel", "arbitrary"), VMEM_BIG), name="attn_fwd",
    )(z0, z0, z0, z0, *tabs, ycat)


def _attn_bwd(z0, dycat, tabs):
    def body(q_ref, k_ref, v_ref, gate_ref, dy_ref, c_ref, a_ref, b_ref,
             dq_ref, dk_ref, dv_ref, dbg_ref,
             tmp, qd, kd, vd, od, ld, og0, og1, og2, lg0, lg1, lg2, cg0, cg1, cg2, dod, cd, dqd, dkd, dvd):
        p = pl.program_id(1)
        ogs, lgs, cgs = (og0, og1, og2), (lg0, lg1, lg2), (cg0, cg1, cg2)
        tabs_v = (c_ref[...], a_ref[...], b_ref[...])
        for gi, (_, dil) in enumerate(PATTERNS):
            @pl.when(p == gi)
            def _(gi=gi, dil=dil):
                _attn_group_fwd(dil, q_ref, k_ref, v_ref, tabs_v, tmp, qd, kd, vd, od, ld, ogs[gi], lgs[gi])

        @pl.when(p == 2)
        def _():
            w = _group_weights(lgs)
            o = w[0] * og0[...] + w[1] * og1[...] + w[2] * og2[...]
            silu, dsilu = _silu_and_grad(gate_ref[...])
            dy = dy_ref[...]
            dbg_ref[...] = (dy * o * dsilu).astype(BF16)
            do = dy * silu
            dwbar = jnp.sum(do * o, axis=1, keepdims=True)
            for gi in range(3):
                ogs[gi][...] = w[gi] * do
                cgs[gi][...] = -w[gi] * dwbar

        for gi, (_, dil) in enumerate(PATTERNS):
            @pl.when(p == 3 + gi)
            def _(gi=gi, dil=dil):
                nb = S // dil // BLK
                c, a, b = tabs_v
                _attn_prepare(dil, q_ref, k_ref, v_ref, tabs_v, tmp, qd, kd, vd)
                _deinterleave(dod, ogs[gi], dil, BF16)
                _deinterleave(ld, lgs[gi], dil)
                _deinterleave(cd, cgs[gi], dil)
                dkd[...] = jnp.zeros_like(dkd)
                dvd[...] = jnp.zeros_like(dvd)
                flat = lambda t: t.reshape(CU * BLK, HEAD_DIM)
                for u0 in range(0, NUNITS, CU):
                    q, s_own, s_prev = _chunk_scores(u0, nb, qd, kd)
                    lse, cv, do = _blocks(ld, u0), _blocks(cd, u0), _blocks(dod, u0)
                    own = slice((u0 + 1) * BLK, (u0 + 1 + CU) * BLK)
                    p_own = jnp.exp(s_own - lse)
                    ds_own = (p_own * (_dot(do, _blocks(vd, u0 + 1), B_QK) + cv) * SCALE).astype(BF16)
                    dq = _dot(ds_own, _blocks(kd, u0 + 1), B_PV)
                    dkd[own, :] += flat(_dot(ds_own, q, B_TN))
                    dvd[own, :] += flat(_dot(p_own.astype(BF16), do, B_TN))
                    if s_prev is not None:
                        prev = slice(u0 * BLK, (u0 + CU) * BLK)
                        p_prev = jnp.exp(s_prev - lse)
                        ds_prev = (p_prev * (_dot(do, _blocks(vd, u0), B_QK) + cv) * SCALE).astype(BF16)
                        dq = dq + _dot(ds_prev, _blocks(kd, u0), B_PV)
                        dkd[prev, :] += flat(_dot(ds_prev, q, B_TN))
                        dvd[prev, :] += flat(_dot(p_prev.astype(BF16), do, B_TN))
                    dqd[u0 * BLK:(u0 + CU) * BLK, :] = flat(dq)
                _interleave(tmp, dqd, dil)
                dq_ref[...] = _rope_t(tmp[...], c, a, b).astype(BF16)
                _interleave(tmp, dkd, dil, BLK)
                dk_ref[...] = _rope_t(tmp[...], c, a, b).astype(BF16)
                _interleave(tmp, dvd, dil, BLK)
                dv_ref[...] = tmp[...].astype(BF16)

    tab = pl.BlockSpec((S, HEAD_DIM), lambda h, p: (0, 0))
    hspec = lambda base: pl.BlockSpec((S, HEAD_DIM), lambda h, p: (0, base + h))
    gspec = pl.BlockSpec((S, HEAD_DIM), lambda h, p: (0, jnp.maximum(p - 3, 0) * 8 + h))
    slab = lambda: pltpu.VMEM((S, HEAD_DIM), F32)
    return pl.pallas_call(
        body, grid=(8, 6),
        in_specs=[_head_spec(Q_COL), _head_spec(K_COL), _head_spec(V_COL), hspec(BG_COL), hspec(8), tab, tab, tab],
        out_specs=[gspec, gspec, gspec, hspec(0)],
        out_shape=[jax.ShapeDtypeStruct((S, 3072), BF16)] * 3 + [jax.ShapeDtypeStruct((S, HALF), BF16)],
        scratch_shapes=ATTN_SCRATCH_FWD + [slab(), slab(), slab(), pltpu.VMEM((S, HEAD_DIM), BF16), slab(), slab(),
                                           pltpu.VMEM((S + BLK, HEAD_DIM), F32), pltpu.VMEM((S + BLK, HEAD_DIM), F32)],
        compiler_params=_params(("parallel", "arbitrary"), VMEM_BIG), name="attn_bwd",
    )(z0, z0, z0, z0, dycat, *tabs)


SGU_CH = 256
NCHUNK = TR // 128


def _ln_stats(x):
    mu = jnp.mean(x, axis=-1, keepdims=True)
    xc = x - mu
    r = lax.rsqrt(jnp.mean(xc * xc, axis=-1, keepdims=True) + EPS)
    return xc * r, r


def _ln_bwd(dy, xhat, r, g):
    dxh = dy * g
    return r * (dxh - jnp.mean(dxh, axis=-1, keepdims=True) - xhat * jnp.mean(dxh * xhat, axis=-1, keepdims=True))


def _tril_bf16(w):
    row = lax.broadcasted_iota(jnp.int32, w.shape, 0)
    col = lax.broadcasted_iota(jnp.int32, w.shape, 1)
    return jnp.where(row >= col, w, 0.0).astype(BF16)


def _sgu_gate(vn_s, s_s, w_ref, bb_ref):
    for h in range(4):
        wm = _tril_bf16(w_ref[h])
        bias = bb_ref[h]
        for ch in range(NCHUNK):
            rows, cols = slice(ch * 128, (ch + 1) * 128), slice(h * SGU_CH, (h + 1) * SGU_CH)
            s_s[rows, cols] = _dot(wm, vn_s[rows, cols]) + jnp.concatenate([bias, bias], axis=1)


def _conv_fwd(i, dval_ref, dglu_ref, hval_ref, hglu_ref, cw_ref, cb_ref, xw, dcs):
    halo = hval_ref[...] * _sigmoid(hglu_ref[...])
    xw[0:HALO, :] = jnp.where(i > 0, halo, 0.0)
    xw[HALO:HALO + TR, :] = dval_ref[...] * _sigmoid(dglu_ref[...])
    for rb in range(TR // SUB):
        acc = jnp.broadcast_to(cb_ref[...], (SUB, HALF))
        for k in range(CONV_K):
            acc = acc + cw_ref[k:k + 1, :] * xw[pl.ds(rb * SUB + HALO - (CONV_K - 1) + k, SUB), :]
        dcs[rb * SUB:(rb + 1) * SUB, :] = acc


def _odd_in_specs():
    col = lambda j: pl.BlockSpec((TR, HALF), lambda i, *_: (i, j))
    prev = lambda j: pl.BlockSpec((HALO, HALF), lambda i, *_: (jnp.maximum(i * (TR // HALO) - 1, 0), j))
    return [col(0), col(1), col(2), col(3), col(4), col(5), prev(3), prev(4)]


def _full_spec(shape):
    return pl.BlockSpec(shape, lambda i, *_: (0,) * len(shape))


def _odd_fwd(z1, sgu_g, sgu_b, sgu_w, sgu_bb, conv_w, conv_b, cn_g, cn_b):
    def body(u_ref, v_ref, cg_ref, dval_ref, dglu_ref, dgate_ref, hval_ref, hglu_ref,
             g_ref, b_ref, w_ref, bb_ref, cw_ref, cb_ref, cng_ref, cnb_ref, out_ref, vn_s, s_s, xw, dcs):
        i = pl.program_id(0)
        vhat, _ = _ln_stats(v_ref[...])
        vn_s[...] = (vhat * g_ref[...] + b_ref[...]).astype(BF16)
        _sgu_gate(vn_s, s_s, w_ref, bb_ref)
        cg = cg_ref[...]
        out_ref[:, 0:HALF] = (u_ref[...] * s_s[...] * (cg * _sigmoid(cg))).astype(BF16)
        _conv_fwd(i, dval_ref, dglu_ref, hval_ref, hglu_ref, cw_ref, cb_ref, xw, dcs)
        dhat, _ = _ln_stats(dcs[...])
        dn = dhat * cng_ref[...] + cnb_ref[...]
        dgate = dgate_ref[...]
        out_ref[:, HALF:2 * HALF] = ((dn * _sigmoid(dn)) * (dgate * _sigmoid(dgate))).astype(BF16)

    vec = _full_spec((1, HALF))
    return pl.pallas_call(
        body, grid=(S // TR,),
        in_specs=_odd_in_specs() + [vec, vec, _full_spec((4, 128, 128)), _full_spec((4, 128, 128)),
                                    _full_spec((HALO, HALF)), vec, vec, vec],
        out_specs=pl.BlockSpec((TR, 2048), lambda i: (i, 0)),
        out_shape=jax.ShapeDtypeStruct((S, 2048), BF16),
        scratch_shapes=[pltpu.VMEM((TR, HALF), BF16), pltpu.VMEM((TR, HALF), F32),
                        pltpu.VMEM((HALO + TR, HALF), F32), pltpu.VMEM((TR, HALF), F32)],
        compiler_params=_params(("parallel",), VMEM_BIG), name="odd_fwd",
    )(z1, z1, z1, z1, z1, z1, z1, z1, sgu_g, sgu_b, sgu_w, sgu_bb, conv_w, conv_b, cn_g, cn_b)


def _odd_bwd_a(z1, dycat, sgu_g, sgu_b, sgu_w, sgu_bb, conv_w, conv_b, cn_g, cn_b):
    def body(u_ref, v_ref, cg_ref, dval_ref, dglu_ref, dgate_ref, hval_ref, hglu_ref, dy_ref,
             g_ref, b_ref, w_ref, bb_ref, cw_ref, cb_ref, cng_ref, cnb_ref,
             dz_ref, ddc_ref, dw_ref, dbb_ref, dg_ref, db_ref, dcng_ref, dcnb_ref, dcb_ref,
             vn_s, s_s, xw, dcs, ds_s, dvn_s):
        i = pl.program_id(0)
        vhat, rv = _ln_stats(v_ref[...])
        g = g_ref[...]
        vn_s[...] = (vhat * g + b_ref[...]).astype(BF16)
        _sgu_gate(vn_s, s_s, w_ref, bb_ref)
        silu_c, dsilu_c = _silu_and_grad(cg_ref[...])
        dyc = dy_ref[:, 0:HALF]
        u = u_ref[...]
        s = s_s[...]
        dz_ref[:, 0:HALF] = (dyc * s * silu_c).astype(BF16)
        dz_ref[:, 2 * HALF:3 * HALF] = (dyc * u * s * dsilu_c).astype(BF16)
        ds_s[...] = dyc * u * silu_c

        @pl.when(i == 0)
        def _():
            dw_ref[...] = jnp.zeros_like(dw_ref)
            dbb_ref[...] = jnp.zeros_like(dbb_ref)

        tril = lax.broadcasted_iota(jnp.int32, (128, 128), 0) >= lax.broadcasted_iota(jnp.int32, (128, 128), 1)
        for h in range(4):
            wm = _tril_bf16(w_ref[h])
            for ch in range(NCHUNK):
                rows, cols = slice(ch * 128, (ch + 1) * 128), slice(h * SGU_CH, (h + 1) * SGU_CH)
                ds = ds_s[rows, cols]
                dsb = ds.astype(BF16)
                dw_ref[h] += jnp.where(tril, _dot(dsb, vn_s[rows, cols], NT), 0.0)
                dbb_ref[h] += jnp.broadcast_to(jnp.sum(ds, axis=1, keepdims=True), (128, 128))
                dvn_s[rows, cols] = _dot(wm, dsb, TN)
        dvn = dvn_s[...]
        _acc_rows(dg_ref, dvn * vhat, i)
        _acc_rows(db_ref, dvn, i)
        dz_ref[:, HALF:2 * HALF] = _ln_bwd(dvn, vhat, rv, g).astype(BF16)

        _conv_fwd(i, dval_ref, dglu_ref, hval_ref, hglu_ref, cw_ref, cb_ref, xw, dcs)
        dhat, rd = _ln_stats(dcs[...])
        cng = cng_ref[...]
        silu_n, dsilu_n = _silu_and_grad(dhat * cng + cnb_ref[...])
        silu_g, dsilu_g = _silu_and_grad(dgate_ref[...])
        dyd = dy_ref[:, HALF:2 * HALF]
        dz_ref[:, 5 * HALF:6 * HALF] = (dyd * silu_n * dsilu_g).astype(BF16)
        ddn = dyd * silu_g * dsilu_n
        _acc_rows(dcng_ref, ddn * dhat, i)
        _acc_rows(dcnb_ref, ddn, i)
        ddc = _ln_bwd(ddn, dhat, rd, cng)
        ddc_ref[...] = ddc
        _acc_rows(dcb_ref, ddc, i)

    vec = _full_spec((1, HALF))
    sq = _full_spec((4, 128, 128))
    return pl.pallas_call(
        body, grid=(S // TR,),
        in_specs=_odd_in_specs() + [pl.BlockSpec((TR, 2048), lambda i: (i, 0)),
                                    vec, vec, sq, sq, _full_spec((HALO, HALF)), vec, vec, vec],
        out_specs=[pl.BlockSpec((TR, ODD_IN), lambda i: (i, 0)), pl.BlockSpec((TR, HALF), lambda i: (i, 0)),
                   sq, sq, vec, vec, vec, vec, vec],
        out_shape=[jax.ShapeDtypeStruct((S, ODD_IN), BF16), jax.ShapeDtypeStruct((S, HALF), F32),
                   jax.ShapeDtypeStruct((4, 128, 128), F32), jax.ShapeDtypeStruct((4, 128, 128), F32)]
                  + [jax.ShapeDtypeStruct((1, HALF), F32)] * 5,
        scratch_shapes=[pltpu.VMEM((TR, HALF), BF16), pltpu.VMEM((TR, HALF), F32),
                        pltpu.VMEM((HALO + TR, HALF), F32), pltpu.VMEM((TR, HALF), F32),
                        pltpu.VMEM((TR, HALF), F32), pltpu.VMEM((TR, HALF), F32)],
        compiler_params=_params(("arbitrary",), VMEM_BIG), name="odd_bwd_a",
    )(z1, z1, z1, z1, z1, z1, z1, z1, dycat, sgu_g, sgu_b, sgu_w, sgu_bb, conv_w, conv_b, cn_g, cn_b)


def _odd_bwd_b(z1, ddc, dz1, conv_w):
    nt = S // TR

    def body(dval_ref, dglu_ref, hval_ref, hglu_ref, ddc_ref, hddc_ref, cw_ref, dz_in_ref,
             dz_ref, dcw_ref, xw, dwin, dxs):
        del dz_in_ref
        i, j = pl.program_id(0), pl.program_id(1)
        sg = _sigmoid(dglu_ref[...])
        dval = dval_ref[...]

        @pl.when(j == 0)
        def _():
            halo = hval_ref[...] * _sigmoid(hglu_ref[...])
            xw[0:HALO, :] = jnp.where(i > 0, halo, 0.0)
            xw[HALO:HALO + TR, :] = dval * sg
            dwin[0:TR, :] = ddc_ref[...]
            dwin[TR:TR + HALO, :] = jnp.where(i < nt - 1, hddc_ref[...], 0.0)

            @pl.when(i == 0)
            def _():
                dcw_ref[...] = jnp.zeros_like(dcw_ref)

            for rb in range(TR // SUB):
                acc = jnp.zeros((SUB, HALF), F32)
                for k in range(CONV_K):
                    acc = acc + cw_ref[k:k + 1, :] * dwin[pl.ds(rb * SUB + (CONV_K - 1) - k, SUB), :]
                dxs[rb * SUB:(rb + 1) * SUB, :] = acc
            for k in range(CONV_K):
                acc = jnp.zeros((SUB, HALF), F32)
                for rb in range(TR // SUB):
                    acc = acc + dwin[rb * SUB:(rb + 1) * SUB, :] * xw[pl.ds(rb * SUB + HALO - (CONV_K - 1) + k, SUB), :]
                dcw_ref[k:k + 1, :] += jnp.sum(acc, axis=0, keepdims=True)
            dz_ref[...] = (dxs[...] * sg).astype(BF16)

        @pl.when(j == 1)
        def _():
            dz_ref[...] = (dxs[...] * dval * sg * (1.0 - sg)).astype(BF16)

    col = lambda c: pl.BlockSpec((TR, HALF), lambda i, j: (i, c))
    prev = lambda c: pl.BlockSpec((HALO, HALF), lambda i, j: (jnp.maximum(i * (TR // HALO) - 1, 0), c))
    nxt = pl.BlockSpec((HALO, HALF), lambda i, j: (jnp.minimum((i + 1) * (TR // HALO), S // HALO - 1), 0))
    return pl.pallas_call(
        body, grid=(nt, 2),
        in_specs=[col(3), col(4), prev(3), prev(4), pl.BlockSpec((TR, HALF), lambda i, j: (i, 0)), nxt,
                  _full_spec((HALO, HALF)), pl.BlockSpec(memory_space=pl.ANY)],
        out_specs=[pl.BlockSpec((TR, HALF), lambda i, j: (i, 3 + j)), _full_spec((HALO, HALF))],
        out_shape=[jax.ShapeDtypeStruct((S, ODD_IN), BF16), jax.ShapeDtypeStruct((HALO, HALF), F32)],
        scratch_shapes=[pltpu.VMEM((HALO + TR, HALF), F32), pltpu.VMEM((TR + HALO, HALF), F32),
                        pltpu.VMEM((TR, HALF), F32)],
        input_output_aliases={7: 0},
        compiler_params=_params(("arbitrary", "arbitrary"), VMEM_BIG), name="odd_bwd_b",
    )(z1, z1, z1, z1, ddc, ddc, conv_w, dz1)


def _cast_bf16(w, name):
    r, c = w.shape
    tr = min(r, 256)
    def body(i_ref, o_ref):
        o_ref[...] = i_ref[...].astype(BF16)

    return pl.pallas_call(
        body, grid=(r // tr,), in_specs=[pl.BlockSpec((tr, c), lambda i: (i, 0))],
        out_specs=pl.BlockSpec((tr, c), lambda i: (i, 0)), out_shape=jax.ShapeDtypeStruct((r, c), BF16),
        compiler_params=_params(("parallel",)), name=name,
    )(w)


def _adamw(w, g, m, v):
    m = ADAM_B1 * m + (1.0 - ADAM_B1) * g
    v = ADAM_B2 * v + (1.0 - ADAM_B2) * (g * g)
    m_hat = m / (1.0 - ADAM_B1 ** ADAM_STEP)
    v_hat = v / (1.0 - ADAM_B2 ** ADAM_STEP)
    delta = -ADAM_LR * (m_hat / (jnp.sqrt(v_hat) + ADAM_EPS) + ADAM_WD * w)
    return delta, m, v


def _adam_reduce(parts, w, m, v, name, dep=None):
    r, c = w.shape
    tr = min(r, 128)
    deps = [] if dep is None else [dep]
    nparts = parts.shape[0]

    def body(p_ref, w_ref, m_ref, v_ref, *rest):
        g_ref, d_ref, nm_ref, nv_ref = rest[len(deps):]
        g = p_ref[0].astype(F32)
        for d in range(1, nparts):
            g = g + p_ref[d].astype(F32)
        g_ref[...] = g
        d_ref[...], nm_ref[...], nv_ref[...] = _adamw(w_ref[...], g, m_ref[...], v_ref[...])

    spec = pl.BlockSpec((tr, c), lambda i: (i, 0))
    return pl.pallas_call(
        body, grid=(r // tr,),
        in_specs=[pl.BlockSpec((nparts, tr, c), lambda i: (0, i, 0)), spec, spec, spec] + [ANY_SPEC] * len(deps),
        out_specs=[spec] * 4, out_shape=[jax.ShapeDtypeStruct((r, c), F32)] * 4,
        compiler_params=_params(("parallel",), VMEM_BIG), name=name,
    )(parts, w, m, v, *deps)


def _sum_parts(parts, name, dep=None):
    r = parts.shape[1]
    tr = 8
    for cand in (512, 256, 128, 64, 32, 16, 8):
        if r % cand == 0:
            tr = cand
            break
    deps = [] if dep is None else [dep]

    def body(p_ref, *rest):
        g = p_ref[0]
        for d in range(1, NDEV):
            g = g + p_ref[d]
        rest[-1][...] = g

    return pl.pallas_call(
        body, grid=(r // tr,), in_specs=[pl.BlockSpec((NDEV, tr, 128), lambda i: (0, i, 0))] + [ANY_SPEC] * len(deps),
        out_specs=pl.BlockSpec((tr, 128), lambda i: (i, 0)), out_shape=jax.ShapeDtypeStruct((r, 128), F32),
        compiler_params=_params(("parallel",)), name=name,
    )(parts, *deps)


def _adam_plain(w, g, m, v, name):
    r, c = w.shape

    def body(w_ref, g_ref, m_ref, v_ref, d_ref, nm_ref, nv_ref):
        d_ref[...], nm_ref[...], nv_ref[...] = _adamw(w_ref[...], g_ref[...], m_ref[...], v_ref[...])

    spec = pl.BlockSpec((r, c), lambda i: (0, 0))
    return pl.pallas_call(
        body, grid=(1,), in_specs=[spec] * 4, out_specs=[spec] * 3,
        out_shape=[jax.ShapeDtypeStruct((r, c), F32)] * 3,
        compiler_params=_params(("arbitrary",)), name=name,
    )(w, g, m, v)


MASKS = [(mx, my, mc) for mx in (0, 1) for my in (0, 1) for mc in (0, 1)][1:]


def _exchange(arrays, scatter, name):
    nt = len(arrays)
    out_shape = [jax.ShapeDtypeStruct(((NDEV,) + a.shape) if not scatter else a.shape, a.dtype) for a in arrays]

    def body(*refs):
        ins, outs = refs[:nt], refs[nt:2 * nt]
        send_sems, recv_sems, local_sems = refs[2 * nt:]
        x, y, c = lax.axis_index("x"), lax.axis_index("y"), lax.axis_index("c")
        me = 4 * x + 2 * y + c
        copies = []
        for t in range(nt):
            src_own = ins[t].at[me] if scatter else ins[t]
            loc = pltpu.make_async_copy(src_own, outs[t].at[me], local_sems.at[t])
            loc.start()
            copies.append(loc)
            for k, (mx, my, mc) in enumerate(MASKS):
                px, py, pc = (x + mx) % 2, (y + my) % 2, (c + mc) % 2
                peer = 4 * px + 2 * py + pc
                src = ins[t].at[peer] if scatter else ins[t]
                rc = pltpu.make_async_remote_copy(
                    src_ref=src, dst_ref=outs[t].at[me], send_sem=send_sems.at[t, k], recv_sem=recv_sems.at[t, k],
                    device_id=(px, py, pc), device_id_type=MESH)
                rc.start()
                copies.append(rc)
        for cp in copies:
            cp.wait()

    hbm = pl.BlockSpec(memory_space=pl.ANY)
    return pl.pallas_call(
        body, in_specs=[hbm] * nt, out_specs=[hbm] * nt, out_shape=out_shape,
        scratch_shapes=[pltpu.SemaphoreType.DMA((nt, 7)), pltpu.SemaphoreType.DMA((nt, 7)),
                        pltpu.SemaphoreType.DMA((nt,))],
        name=name,
    )(*arrays)


SEM_SPEC = pl.BlockSpec(memory_space=pltpu.SEMAPHORE)
EFFECT = pltpu.SideEffectType.DATAFLOW_SIDE_EFFECTING


def _direct_plan(scatter):
    def plan(x, y, c, srcs, lands):
        me = 4 * x + 2 * y + c
        local, remote = [], []
        for src, land in zip(srcs, lands):
            local.append((src.at[me] if scatter else src, land.at[me]))
            for mx, my, mc in MASKS:
                px, py, pc = (x + mx) % 2, (y + my) % 2, (c + mc) % 2
                blk = src.at[4 * px + 2 * py + pc] if scatter else src
                remote.append((blk, land.at[me], (px, py, pc)))
        return local, remote
    return plan


def _split_start(name, srcs, land_shapes, plan, n_local, n_remote, dep=None):
    ns, nl = len(srcs), len(land_shapes)
    deps = [] if dep is None else [dep]
    lands = [lax.empty(s.shape, s.dtype) for s in land_shapes]

    def body(*refs):
        ins, lz = refs[:ns], refs[ns:ns + nl]
        outs = refs[ns + nl + len(deps):]
        send_sems, recv_sems, token, local_sems = outs[0], outs[1], outs[2 + ns + nl], outs[3 + ns + nl]
        local, remote = plan(lax.axis_index("x"), lax.axis_index("y"), lax.axis_index("c"), ins, lz)
        own = [pltpu.make_async_copy(src, dst, local_sems.at[i]) for i, (src, dst) in enumerate(local)]
        for cp in own:
            cp.start()
        for cp in own:
            cp.wait()
        for k, (src, dst, peer) in enumerate(remote):
            pltpu.make_async_remote_copy(src_ref=src, dst_ref=dst, send_sem=send_sems.at[k], recv_sem=recv_sems.at[k],
                                         device_id=peer, device_id_type=MESH).start()
        token[...] = jnp.zeros_like(token)

    hbm = lambda a: pltpu.HBM(a.shape, a.dtype)
    outs = pl.pallas_call(
        body, name=name,
        out_shape=(pltpu.SemaphoreType.DMA((n_remote,)), pltpu.SemaphoreType.DMA((n_remote,)),
                   *[hbm(a) for a in srcs], *[hbm(a) for a in lands], jax.ShapeDtypeStruct((8, 128), F32)),
        in_specs=[ANY_SPEC] * (ns + nl + len(deps)),
        out_specs=(SEM_SPEC, SEM_SPEC, *[ANY_SPEC] * (ns + nl), pl.BlockSpec(memory_space=pltpu.VMEM)),
        scratch_shapes=[pltpu.SemaphoreType.DMA((n_local,))],
        input_output_aliases={i: 2 + i for i in range(ns + nl)},
        compiler_params=pltpu.CompilerParams(has_side_effects=EFFECT),
    )(*[pltpu.with_memory_space_constraint(a, pltpu.HBM) for a in srcs],
      *[pltpu.with_memory_space_constraint(a, pltpu.HBM) for a in lands], *deps)
    return dict(sems=outs[:2], srcs=outs[2:2 + ns], lands=outs[2 + ns:2 + ns + nl], token=outs[-1],
                plan=plan, n_remote=n_remote)


def _split_wait(name, handle, after):
    srcs, lands, plan = handle["srcs"], handle["lands"], handle["plan"]
    ns, nl = len(srcs), len(lands)

    def body(*refs):
        ins, lz = refs[:ns], refs[ns:ns + nl]
        send_sems, recv_sems = refs[ns + nl], refs[ns + nl + 1]
        _, remote = plan(lax.axis_index("x"), lax.axis_index("y"), lax.axis_index("c"), ins, lz)
        for k, (src, dst, peer) in enumerate(remote):
            cp = pltpu.make_async_remote_copy(src_ref=src, dst_ref=dst, send_sem=send_sems.at[k],
                                              recv_sem=recv_sems.at[k], device_id=peer, device_id_type=MESH)
            cp.wait_send()
            cp.wait_recv()

    hbm = lambda a: pltpu.HBM(a.shape, a.dtype)
    outs = pl.pallas_call(
        body, name=name, out_shape=(*[hbm(a) for a in srcs], *[hbm(a) for a in lands]),
        in_specs=[ANY_SPEC] * (ns + nl) + [SEM_SPEC, SEM_SPEC, ANY_SPEC], out_specs=tuple([ANY_SPEC] * (ns + nl)),
        input_output_aliases={i: i for i in range(ns + nl)},
        compiler_params=pltpu.CompilerParams(has_side_effects=EFFECT),
    )(*srcs, *lands, *handle["sems"], after)
    return list(outs[ns:])


def _sc_exchange(name, collective_id, arrays, scatter):
    nt = len(arrays)
    out_type = [jax.ShapeDtypeStruct(a.shape if scatter else (NDEV,) + a.shape, a.dtype) for a in arrays]

    def body(*refs):
        ins, outs = refs[:nt], refs[nt:2 * nt]
        send_sems, recv_sems, local_sems = refs[2 * nt:3 * nt], refs[3 * nt:4 * nt], refs[4 * nt:5 * nt]
        x, y, c = lax.axis_index("x"), lax.axis_index("y"), lax.axis_index("c")
        peers = [(mx + x - 2 * mx * x, my + y - 2 * my * y, mc + c - 2 * mc * c) for mx, my, mc in MASKS]
        barrier = pltpu.get_barrier_semaphore()
        for peer in peers:
            pl.semaphore_signal(barrier, inc=1, device_id=peer, device_id_type=MESH)
        pl.semaphore_wait(barrier, len(peers))
        me = 4 * x + 2 * y + c
        own = []
        for t in range(nt):
            cp = pltpu.make_async_copy(ins[t].at[me] if scatter else ins[t], outs[t].at[me], local_sems[t])
            cp.start()
            own.append(cp)
            for px, py, pc in peers:
                src = ins[t].at[4 * px + 2 * py + pc] if scatter else ins[t]
                pltpu.make_async_remote_copy(src_ref=src, dst_ref=outs[t].at[me], send_sem=send_sems[t],
                                             recv_sem=recv_sems[t], device_id=(px, py, pc), device_id_type=MESH).start()
        for t in range(nt):
            own[t].wait()
            seven = outs[t].at[pl.ds(0, NDEV - 1)]
            drain = pltpu.make_async_remote_copy(src_ref=seven, dst_ref=seven, send_sem=send_sems[t],
                                                 recv_sem=recv_sems[t], device_id=(x, y, c), device_id_type=MESH)
            drain.wait_send()
            drain.wait_recv()

    return pl.kernel(
        body, out_type=out_type, mesh=plsc.ScalarSubcoreMesh(axis_name="sequencer", num_cores=1),
        scratch_types=[pltpu.SemaphoreType.DMA] * (3 * nt),
        compiler_params=pltpu.CompilerParams(collective_id=collective_id), name=name,
    )(*arrays)


def _sc_gather_two_level(name, collective_id, arrays):
    nt = len(arrays)
    out_type = [jax.ShapeDtypeStruct((NDEV,) + a.shape, a.dtype) for a in arrays]

    def body(*refs):
        ins, outs = refs[:nt], refs[nt:2 * nt]
        sems = refs[2 * nt:]
        send_sems, sib_sems, local_sems = sems[:nt], sems[nt:2 * nt], sems[2 * nt:3 * nt]
        ici_sems = [sems[3 * nt + 3 * t:3 * nt + 3 * t + 3] for t in range(nt)]
        x, y, c = lax.axis_index("x"), lax.axis_index("y"), lax.axis_index("c")
        sibling = (x, y, 1 - c)
        chips = [(1 - x, y), (x, 1 - y), (1 - x, 1 - y)]
        barrier = pltpu.get_barrier_semaphore()
        for peer in [sibling] + [(cx, cy, c) for cx, cy in chips]:
            pl.semaphore_signal(barrier, inc=1, device_id=peer, device_id_type=MESH)
        pl.semaphore_wait(barrier, 4)
        me = 4 * x + 2 * y + c

        def push(t, src, slot, recv_sem, to):
            pltpu.make_async_remote_copy(src_ref=src, dst_ref=outs[t].at[slot], send_sem=send_sems[t],
                                         recv_sem=recv_sem, device_id=to, device_id_type=MESH).start()

        own = []
        for t in range(nt):
            cp = pltpu.make_async_copy(ins[t], outs[t].at[me], local_sems[t])
            cp.start()
            own.append(cp)
            for j, (cx, cy) in enumerate(chips):
                push(t, ins[t], me, ici_sems[t][j], (cx, cy, c))
            push(t, ins[t], me, sib_sems[t], sibling)
        for t in range(nt):
            for j, (cx, cy) in enumerate(chips):
                slot = 4 * cx + 2 * cy + c
                landed = outs[t].at[slot]
                pltpu.make_async_remote_copy(src_ref=landed, dst_ref=landed, send_sem=send_sems[t],
                                             recv_sem=ici_sems[t][j], device_id=(cx, cy, c),
                                             device_id_type=MESH).wait_recv()
                push(t, landed, slot, sib_sems[t], sibling)
        for t in range(nt):
            own[t].wait()
            four, seven = outs[t].at[pl.ds(0, 4)], outs[t].at[pl.ds(0, 7)]
            pltpu.make_async_remote_copy(src_ref=four, dst_ref=four, send_sem=send_sems[t], recv_sem=sib_sems[t],
                                         device_id=sibling, device_id_type=MESH).wait_recv()
            pltpu.make_async_remote_copy(src_ref=seven, dst_ref=seven, send_sem=send_sems[t], recv_sem=sib_sems[t],
                                         device_id=sibling, device_id_type=MESH).wait_send()

    return pl.kernel(
        body, out_type=out_type, mesh=plsc.ScalarSubcoreMesh(axis_name="sequencer", num_cores=1),
        scratch_types=[pltpu.SemaphoreType.DMA] * (6 * nt),
        compiler_params=pltpu.CompilerParams(collective_id=collective_id), name=name,
    )(*arrays)


def _sc_sibling_exchange(name, collective_id, src, out_shape, pieces):
    def body(src_ref, out_ref, send_sem, recv_sem):
        x, y, c = lax.axis_index("x"), lax.axis_index("y"), lax.axis_index("c")
        sibling = (x, y, 1 - c)
        barrier = pltpu.get_barrier_semaphore()
        pl.semaphore_signal(barrier, inc=1, device_id=sibling, device_id_type=MESH)
        pl.semaphore_wait(barrier, 1)
        for piece, lands in pieces(c, src_ref, out_ref):
            pltpu.make_async_remote_copy(src_ref=piece, dst_ref=lands, send_sem=send_sem, recv_sem=recv_sem,
                                         device_id=sibling, device_id_type=MESH).start()
        drain = pltpu.make_async_remote_copy(src_ref=out_ref, dst_ref=out_ref, send_sem=send_sem, recv_sem=recv_sem,
                                             device_id=sibling, device_id_type=MESH)
        drain.wait_send()
        drain.wait_recv()

    return pl.kernel(
        body, out_type=jax.ShapeDtypeStruct(out_shape, src.dtype),
        mesh=plsc.ScalarSubcoreMesh(axis_name="sequencer", num_cores=1), scratch_types=[pltpu.SemaphoreType.DMA] * 2,
        compiler_params=pltpu.CompilerParams(collective_id=collective_id), name=name,
    )(src)


def _sc_chip_scatter(name, collective_id, q):
    def body(q_ref, out_ref, send_sem, recv_sem, local_sem):
        x, y, c = lax.axis_index("x"), lax.axis_index("y"), lax.axis_index("c")
        chips = [(1 - x, y), (x, 1 - y), (1 - x, 1 - y)]
        barrier = pltpu.get_barrier_semaphore()
        for cx, cy in chips:
            pl.semaphore_signal(barrier, inc=1, device_id=(cx, cy, c), device_id_type=MESH)
        pl.semaphore_wait(barrier, 3)
        mine = 2 * x + y
        own = pltpu.make_async_copy(q_ref.at[mine], out_ref.at[mine], local_sem)
        own.start()
        for cx, cy in chips:
            pltpu.make_async_remote_copy(src_ref=q_ref.at[2 * cx + cy], dst_ref=out_ref.at[mine], send_sem=send_sem,
                                         recv_sem=recv_sem, device_id=(cx, cy, c), device_id_type=MESH).start()
        own.wait()
        three = out_ref.at[pl.ds(0, 3)]
        drain = pltpu.make_async_remote_copy(src_ref=three, dst_ref=three, send_sem=send_sem, recv_sem=recv_sem,
                                             device_id=(x, y, c), device_id_type=MESH)
        drain.wait_send()
        drain.wait_recv()

    return pl.kernel(
        body, out_type=jax.ShapeDtypeStruct(q.shape, q.dtype),
        mesh=plsc.ScalarSubcoreMesh(axis_name="sequencer", num_cores=1), scratch_types=[pltpu.SemaphoreType.DMA] * 3,
        compiler_params=pltpu.CompilerParams(collective_id=collective_id), name=name,
    )(q)


def _mm_pair_dw(h_own, dz, h_sib, dz_sib, nb, name, dep=None):
    tn = 512 if nb % 512 == 0 else nb
    per = nb // tn
    o_spec = pl.BlockSpec((None, D, tn), lambda i, j, k: (j // per, 0, j % per))
    part = _matmul(
        h_own, dz, dn=TN, grid=(1, 4 * per, 1),
        a_spec=pl.BlockSpec((S, D), lambda i, j, k: (0, 0)),
        b_spec=pl.BlockSpec((S, tn), lambda i, j, k: (0, (2 * (j // per) + lax.axis_index("c")) * per + j % per)),
        o_spec=o_spec, out_shape=(4, D, nb), out_dtype=F32, acc_shape=(D, tn), name=name + "_own", dep=dep)

    def body(a_ref, b_ref, p_ref, o_ref):
        o_ref[...] = (p_ref[...] + _dot(a_ref[...], b_ref[...], TN)).astype(BF16)

    return pl.pallas_call(
        body, grid=(1, 4 * per, 1),
        in_specs=[pl.BlockSpec((S, D), lambda i, j, k: (0, 0)), pl.BlockSpec((S, tn), lambda i, j, k: (0, j)), o_spec],
        out_specs=o_spec, out_shape=jax.ShapeDtypeStruct((4, D, nb), BF16),
        compiler_params=_params(("parallel", "parallel", "arbitrary"), VMEM_BIG), name=name + "_sibling",
    )(h_sib, dz_sib, part)


SMALL = {
    "e_pre_norm": ((2048,), None), "e_pool_w": ((4, 256, 256), 1), "e_pool_scale": ((1024,), None),
    "e_post_norm": ((2048,), None), "o_pre_norm": ((2048,), 0), "o_sgu_norm_g": ((1024,), 0),
    "o_sgu_norm_b": ((1024,), 0), "o_sgu_w": ((4, 128, 128), None), "o_sgu_b": ((4, 128), None),
    "o_conv_w": ((31, 1024), 1), "o_conv_b": ((1024,), 0), "o_conv_norm_g": ((1024,), 0),
    "o_conv_norm_b": ((1024,), 0), "o_post_norm": ((2048,), 0),
}
SMALL_SHARDED = [n for n, (_, ax) in SMALL.items() if ax is not None]


def _shard_shape(name):
    shape, ax = SMALL[name]
    if ax is None:
        return shape
    return tuple(s // NDEV if i == ax else s for i, s in enumerate(shape))


def _pack(arrs, row_multiple=1):
    flat = jnp.concatenate([a.reshape(-1) for a in arrs])
    pad = -flat.shape[0] % (128 * row_multiple)
    return jnp.concatenate([flat, jnp.zeros((pad,), F32)]).reshape(-1, 128)


def _unpack(buf, shapes):
    flat = buf.reshape(-1)
    out, off = [], 0
    for shp in shapes:
        n = int(np.prod(shp))
        out.append(flat[off:off + n].reshape(shp))
        off += n
    return out


def _take_shard(full, name, me):
    shape, ax = SMALL[name]
    if ax is None:
        return full
    n = shape[ax] // NDEV
    return lax.dynamic_slice_in_dim(full, me * n, n, axis=ax)


BIG = ("e_w_in", "e_w_out", "o_w_in", "o_w_out")
WEIGHTS = ["e_pre_norm", "e_w_in", "e_pool_w", "e_pool_scale", "e_w_out", "e_post_norm", "o_pre_norm", "o_w_in",
           "o_sgu_norm_g", "o_sgu_norm_b", "o_sgu_w", "o_sgu_b", "o_conv_w", "o_conv_b", "o_conv_norm_g",
           "o_conv_norm_b", "o_w_out", "o_post_norm"]


def kernel(x, e_pre_norm, e_w_in, e_pool_w, e_pool_scale, e_w_out, e_post_norm, o_pre_norm, o_w_in, o_sgu_norm_g, o_sgu_norm_b, o_sgu_w, o_sgu_b, o_conv_w, o_conv_b, o_conv_norm_g, o_conv_norm_b, o_w_out, o_post_norm, loss_target, m_e_pre_norm, m_e_w_in, m_e_pool_w, m_e_pool_scale, m_e_w_out, m_e_post_norm, m_o_pre_norm, m_o_w_in, m_o_sgu_norm_g, m_o_sgu_norm_b, m_o_sgu_w, m_o_sgu_b, m_o_conv_w, m_o_conv_b, m_o_conv_norm_g, m_o_conv_norm_b, m_o_w_out, m_o_post_norm, v_e_pre_norm, v_e_w_in, v_e_pool_w, v_e_pool_scale, v_e_w_out, v_e_post_norm, v_o_pre_norm, v_o_w_in, v_o_sgu_norm_g, v_o_sgu_norm_b, v_o_sgu_w, v_o_sgu_b, v_o_conv_w, v_o_conv_b, v_o_conv_norm_g, v_o_conv_norm_b, v_o_w_out, v_o_post_norm):
    given = dict(locals())
    w = {n: given[n][0] for n in WEIGHTS}
    m = {n: given["m_" + n][0] for n in WEIGHTS}
    v = {n: given["v_" + n][0] for n in WEIGHTS}
    me = 4 * lax.axis_index("x") + 2 * lax.axis_index("y") + lax.axis_index("c")
    x, target = x[0], loss_target[0]
    row = lambda a: a.reshape(1, -1)

    bf = {n: _cast_bf16(w[n], "cast_" + n) for n in BIG}
    wg_e_in, small_rows = _sc_gather_two_level("gather_a", 0, [bf["e_w_in"], _pack([w[n] for n in SMALL_SHARDED])])
    wg_e_out, wg_o_in, wg_o_out = _sc_gather_two_level("gather_b", 1, [bf["e_w_out"], bf["o_w_in"], bf["o_w_out"]])
    h0 = _pre0_fwd(x, row(w["e_pre_norm"]))
    h0_sib = _sc_sibling_exchange("swap_h0", 8, h0, h0.shape, lambda c, src, out: [(src, out)])
    p = {n: w[n] for n in SMALL if SMALL[n][1] is None}
    small_rows = small_rows.reshape(NDEV, -1)
    off = 0
    for n in SMALL_SHARDED:
        shp, ax = _shard_shape(n), SMALL[n][1]
        cnt = int(np.prod(shp))
        blk = small_rows[:, off:off + cnt].reshape((NDEV,) + shp)
        p[n] = jnp.moveaxis(blk, 0, ax).reshape(SMALL[n][0])
        off += cnt
    tabs = _rope_tables()
    pool_w_bf = p["e_pool_w"].astype(BF16)
    sgu_bb = jnp.broadcast_to(p["o_sgu_b"][:, :, None], (4, 128, 128))
    conv_w = jnp.concatenate([p["o_conv_w"], jnp.zeros((HALO - CONV_K, HALF), F32)], axis=0)
    odd_p = (row(p["o_sgu_norm_g"]), row(p["o_sgu_norm_b"]), p["o_sgu_w"], sgu_bb, conv_w,
             row(p["o_conv_b"]), row(p["o_conv_norm_g"]), row(p["o_conv_norm_b"]))

    z0 = _mm_in(h0, wg_e_in, "mm_z0")
    ycat0 = _pool_fwd(z0, pool_w_bf, row(p["e_pool_scale"]))
    ycat0 = _attn_fwd(z0, ycat0, tabs)
    w_out_e, w_out_o = wg_e_out.reshape(2048, D), wg_o_out.reshape(2048, D)
    y0 = _mm_out(ycat0, w_out_e, "mm_y0", h0_sib)
    x1, h1 = _post0_fwd(x, y0, row(p["e_post_norm"]), row(p["o_pre_norm"]))
    z1 = _mm_in(h1, wg_o_in, "mm_z1")
    ycat1 = _odd_fwd(z1, *odd_p)
    y1 = _mm_out(ycat1, w_out_o, "mm_y1")

    g = {}
    loss, dx2, dy1, g["o_post_norm"] = _post1_bwd(y1, x1, target, row(p["o_post_norm"]))
    loss = lax.psum(loss[0, 0], ("x", "y", "c"))
    parts = {}
    dw = _mm_out_dw(ycat1, dy1, "mm_dwout1").reshape(NDEV, 256, D)
    parts["o_w_out"], = _sc_exchange("scatter_o_w_out", 2, [dw], True)
    dycat1 = _mm_out_dx(dy1, w_out_o, "mm_dycat1", dw)
    dz1, ddc, g["o_sgu_w"], d_sgu_bb, g["o_sgu_norm_g"], g["o_sgu_norm_b"], g["o_conv_norm_g"], \
        g["o_conv_norm_b"], g["o_conv_b"] = _odd_bwd_a(z1, dycat1, *odd_p)
    dz1, d_conv_w = _odd_bwd_b(z1, ddc, dz1, conv_w)
    g["o_sgu_b"] = d_sgu_bb[:, :, 0]
    g["o_conv_w"] = d_conv_w[:CONV_K]
    grads, deltas, new_m, new_v = {}, {}, {}, {}

    def adam(n, dep):
        grads[n], deltas[n], new_m[n], new_v[n] = _adam_reduce(parts[n], w[n], m[n], v[n], "adam_" + n, dep)
        return new_v[n]

    pin = adam("o_w_out", d_conv_w)
    dw = _mm_in_dw(h1, dz1, ODD_IN // NDEV, "mm_dwin1", pin)
    parts["o_w_in"], = _sc_exchange("scatter_o_w_in", 3, [dw], True)
    dh1 = _mm_in_dx(dz1, wg_o_in, "mm_dh1", dw)
    dx1, dy0, g["o_pre_norm"], g["e_post_norm"] = _mid_bwd(dx2, dh1, x1, y0, row(p["o_pre_norm"]),
                                                           row(p["e_post_norm"]))
    dw = _mm_out_dw(ycat0, dy0, "mm_dwout0").reshape(NDEV, 256, D)
    parts["e_w_out"], = _sc_exchange("scatter_e_w_out", 4, [dw], True)
    dycat0 = _mm_out_dx(dy0, w_out_e, "mm_dycat0", dw)
    da_in, da_gate, g["e_pool_w"], g["e_pool_scale"] = _pool_bwd(z0, dycat0, pool_w_bf, row(p["e_pool_scale"]))
    dq, dk, dv, dbg = _attn_bwd(z0, dycat0, tabs)
    dz0 = jnp.concatenate([da_in, da_gate, dq, dk, dv, dbg], axis=1)
    late = [n for n in SMALL if n != "e_pre_norm"]
    pin = adam("e_w_out", adam("o_w_in", dbg))
    nb = EVEN_IN // NDEV
    dz0_sib = _sc_sibling_exchange(
        "swap_dz0", 9, dz0, (S, 4 * nb),
        lambda c, src, out: [(src.at[:, pl.ds((2 * j + 1 - c) * nb, nb)], out.at[:, pl.ds(j * nb, nb)])
                             for j in range(4)])
    dw = _mm_pair_dw(h0, dz0, h0_sib, dz0_sib, nb, "mm_dwin0", pin)
    parts["e_w_in"] = _sc_chip_scatter("scatter_e_w_in", 5, dw)
    recv_small, = _sc_exchange("gather_small_grads", 6, [_pack([g[n].reshape(SMALL[n][0]) for n in late], 512)], False)
    dh0 = _mm_in_dx(dz0, wg_e_in, "mm_dh0", dw)
    grad_x, g["e_pre_norm"] = _pre0_bwd(dx1, dh0, x, row(p["e_pre_norm"]))
    last, = _sc_exchange("gather_e_pre_norm_grad", 7, [g["e_pre_norm"].reshape(16, 128)], False)

    g_small = dict(zip(late, _unpack(_sum_parts(recv_small, "sum_small_grads"), [SMALL[n][0] for n in late])))
    pin = adam("e_w_in", grad_x)
    g_small["e_pre_norm"] = _sum_parts(last, "sum_e_pre_norm_grad", pin).reshape(2048)
    for n in SMALL:
        grads[n] = _take_shard(g_small[n], n, me)
    names = list(SMALL)
    shapes = [_shard_shape(n) for n in names]
    d_pack, m_pack, v_pack = _adam_plain(_pack([w[n] for n in names]), _pack([grads[n] for n in names]),
                                         _pack([m[n] for n in names]), _pack([v[n] for n in names]), "adam_small")
    for n, d_, m_, v_ in zip(names, _unpack(d_pack, shapes), _unpack(m_pack, shapes), _unpack(v_pack, shapes)):
        deltas[n], new_m[n], new_v[n] = d_, m_, v_

    lead = lambda a: a[None]
    return (loss, grad_x[None], *[lead(grads[n]) for n in WEIGHTS], *[lead(deltas[n]) for n in WEIGHTS],
            *[lead(new_m[n]) for n in WEIGHTS], *[lead(new_v[n]) for n in WEIGHTS])
```

```python
import functools

import numpy as np
import jax
import jax.numpy as jnp
from jax import lax
from jax.experimental import pallas as pl
from jax.experimental.pallas import tpu as pltpu
from jax.experimental.pallas import tpu_sc as plsc

F32 = jnp.float32
BF16 = jnp.bfloat16

S = 2048
D = 2048
NDEV = 8
EPS = 1e-6
NEG = -1e30
HEAD_DIM = 128
ROT_DIM = 32
ROPE_THETA = 500000.0
PATTERNS = ((128, 1), (512, 4), (2048, 16))
BLK = 128
EVEN_IN = 12288
ODD_IN = 6144
HALF = 1024
CONV_K = 31
HALO = 32
TR = 256
SUB = 32

ADAM_LR = 0.001
ADAM_B1 = 0.9
ADAM_B2 = 0.999
ADAM_EPS = 1e-08
ADAM_WD = 0.01
ADAM_STEP = 10

VMEM_BIG = 56 * 1024 * 1024
MESH = pl.DeviceIdType.MESH

NN = (((1,), (0,)), ((), ()))
NT = (((1,), (1,)), ((), ()))
TN = (((0,), (0,)), ((), ()))


def _dot(a, b, dn=NN):
    return lax.dot_general(a, b, dn, preferred_element_type=F32)


def _sigmoid(x):
    return 1.0 / (1.0 + jnp.exp(-x))


def _silu_and_grad(x):
    sg = _sigmoid(x)
    return x * sg, sg * (1.0 + x * (1.0 - sg))


def _params(sem, vmem=None):
    return pltpu.CompilerParams(dimension_semantics=sem, vmem_limit_bytes=vmem)


ANY_SPEC = pl.BlockSpec(memory_space=pl.ANY)


def _matmul(a, b, *, dn, grid, a_spec, b_spec, o_spec, out_shape, out_dtype, acc_shape, name, dep=None):
    nk = grid[2]
    deps = [] if dep is None else list(dep) if isinstance(dep, (tuple, list)) else [dep]

    def body(a_ref, b_ref, *rest):
        o_ref, acc = rest[len(deps)], rest[len(deps) + 1:]
        if nk == 1:
            o_ref[...] = _dot(a_ref[...], b_ref[...], dn).astype(o_ref.dtype)
            return
        acc_ref = acc[0]
        k = pl.program_id(2)

        @pl.when(k == 0)
        def _():
            acc_ref[...] = jnp.zeros_like(acc_ref)

        acc_ref[...] += _dot(a_ref[...], b_ref[...], dn)

        @pl.when(k == nk - 1)
        def _():
            o_ref[...] = acc_ref[...].astype(o_ref.dtype)

    return pl.pallas_call(
        body, grid=grid, in_specs=[a_spec, b_spec] + [ANY_SPEC] * len(deps), out_specs=o_spec,
        out_shape=jax.ShapeDtypeStruct(out_shape, out_dtype),
        scratch_shapes=[] if nk == 1 else [pltpu.VMEM(acc_shape, F32)],
        compiler_params=_params(("parallel", "parallel", "arbitrary"), VMEM_BIG), name=name,
    )(a, b, *deps)


TM = 2048


def _mm_in(h, wg, name):
    nb = wg.shape[2]
    tn = 512 if nb % 512 == 0 else nb
    per = nb // tn
    return _matmul(
        h, wg, dn=NN, grid=(S // TM, NDEV * per, 1),
        a_spec=pl.BlockSpec((TM, D), lambda i, j, k: (i, 0)),
        b_spec=pl.BlockSpec((None, D, tn), lambda i, j, k: (j // per, 0, j % per)),
        o_spec=pl.BlockSpec((TM, tn), lambda i, j, k: (i, j)),
        out_shape=(S, NDEV * nb), out_dtype=F32, acc_shape=(TM, tn), name=name)


def _mm_in_dx(dz, wg, name, dep=None):
    nb = wg.shape[2]
    return _matmul(
        dz, wg, dn=NT, grid=(S // TM, D // 1024, NDEV),
        a_spec=pl.BlockSpec((TM, nb), lambda i, j, k: (i, k)),
        b_spec=pl.BlockSpec((None, 1024, nb), lambda i, j, k: (k, j, 0)),
        o_spec=pl.BlockSpec((TM, 1024), lambda i, j, k: (i, j)),
        out_shape=(S, D), out_dtype=F32, acc_shape=(TM, 1024), name=name, dep=dep)


def _mm_in_dw(h, dz, nb, name, dep=None):
    tn = 512 if nb % 512 == 0 else nb
    per = nb // tn
    return _matmul(
        h, dz, dn=TN, grid=(D // TM, NDEV * per, 1),
        a_spec=pl.BlockSpec((S, TM), lambda i, j, k: (0, i)),
        b_spec=pl.BlockSpec((S, tn), lambda i, j, k: (0, j)),
        o_spec=pl.BlockSpec((None, TM, tn), lambda i, j, k: (j // per, i, j % per)),
        out_shape=(NDEV, D, nb), out_dtype=BF16, acc_shape=(TM, tn), name=name, dep=dep)


def _mm_out(yc, w, name, dep=None):
    return _matmul(
        yc, w, dn=NN, grid=(S // TM, D // 512, 1),
        a_spec=pl.BlockSpec((TM, 2048), lambda i, j, k: (i, 0)),
        b_spec=pl.BlockSpec((2048, 512), lambda i, j, k: (0, j)),
        o_spec=pl.BlockSpec((TM, 512), lambda i, j, k: (i, j)),
        out_shape=(S, D), out_dtype=F32, acc_shape=(TM, 512), name=name, dep=dep)


def _mm_out_dx(dy, w, name, dep=None):
    return _matmul(
        dy, w, dn=NT, grid=(S // TM, 2048 // 512, 1),
        a_spec=pl.BlockSpec((TM, D), lambda i, j, k: (i, 0)),
        b_spec=pl.BlockSpec((512, D), lambda i, j, k: (j, 0)),
        o_spec=pl.BlockSpec((TM, 512), lambda i, j, k: (i, j)),
        out_shape=(S, 2048), out_dtype=F32, acc_shape=(TM, 512), name=name, dep=dep)


def _mm_out_dw(yc, dy, name):
    return _matmul(
        yc, dy, dn=TN, grid=(2048 // TM, D // 512, 1),
        a_spec=pl.BlockSpec((S, TM), lambda i, j, k: (0, i)),
        b_spec=pl.BlockSpec((S, 512), lambda i, j, k: (0, j)),
        o_spec=pl.BlockSpec((TM, 512), lambda i, j, k: (i, j)),
        out_shape=(2048, D), out_dtype=BF16, acc_shape=(TM, 512), name=name)


def _row_spec(w=D):
    return pl.BlockSpec((TR, w), lambda i: (i, 0))


def _vec_spec(w=D):
    return pl.BlockSpec((1, w), lambda i: (0, 0))


def _rms_stats(x):
    r = lax.rsqrt(jnp.mean(x * x, axis=-1, keepdims=True) + EPS)
    return x * r, r


def _rms_bwd(dn, xhat, r, g):
    dxh = dn * g
    return r * (dxh - xhat * jnp.mean(dxh * xhat, axis=-1, keepdims=True))


def _acc_rows(ref, val, i):
    s = jnp.sum(val, axis=0, keepdims=True)

    @pl.when(i == 0)
    def _():
        ref[...] = s

    @pl.when(i > 0)
    def _():
        ref[...] += s


def _pre0_fwd(x, g, dep=None):
    deps = [] if dep is None else [dep]

    def body(x_ref, g_ref, *rest):
        xhat, _ = _rms_stats(x_ref[...])
        rest[-1][...] = (xhat * g_ref[...]).astype(BF16)

    return pl.pallas_call(
        body, grid=(S // TR,), in_specs=[_row_spec(), _vec_spec()] + [ANY_SPEC] * len(deps), out_specs=_row_spec(),
        out_shape=jax.ShapeDtypeStruct((S, D), BF16), compiler_params=_params(("parallel",)), name="pre0_fwd",
    )(x, g, *deps)


def _post0_fwd(x, y0, g_post, g_pre1):
    def body(x_ref, y_ref, gp_ref, g1_ref, x1_ref, h1_ref):
        yhat, _ = _rms_stats(y_ref[...])
        x1 = x_ref[...] + yhat * gp_ref[...]
        x1_ref[...] = x1
        xhat, _ = _rms_stats(x1)
        h1_ref[...] = (xhat * g1_ref[...]).astype(BF16)

    return pl.pallas_call(
        body, grid=(S // TR,), in_specs=[_row_spec(), _row_spec(), _vec_spec(), _vec_spec()],
        out_specs=[_row_spec(), _row_spec()],
        out_shape=[jax.ShapeDtypeStruct((S, D), F32), jax.ShapeDtypeStruct((S, D), BF16)],
        compiler_params=_params(("parallel",)), name="post0_fwd",
    )(x, y0, g_post, g_pre1)


def _post1_bwd(y1, x1, target, g_post):
    def body(y_ref, x1_ref, t_ref, g_ref, loss_ref, dx2_ref, dy_ref, dg_ref):
        i = pl.program_id(0)
        yhat, r = _rms_stats(y_ref[...])
        g = g_ref[...]
        err = x1_ref[...] + yhat * g - t_ref[...]
        part = jnp.sum(jnp.sum(err * err, axis=-1, keepdims=True), axis=0, keepdims=True) * (0.5 / D)
        _acc_rows(loss_ref, jnp.broadcast_to(part, (1, 128)), i)
        dx2 = err * (1.0 / D)
        dx2_ref[...] = dx2
        _acc_rows(dg_ref, dx2 * yhat, i)
        dy_ref[...] = _rms_bwd(dx2, yhat, r, g).astype(BF16)

    return pl.pallas_call(
        body, grid=(S // TR,), in_specs=[_row_spec(), _row_spec(), _row_spec(), _vec_spec()],
        out_specs=[_vec_spec(128), _row_spec(), _row_spec(), _vec_spec()],
        out_shape=[jax.ShapeDtypeStruct((1, 128), F32), jax.ShapeDtypeStruct((S, D), F32),
                   jax.ShapeDtypeStruct((S, D), BF16), jax.ShapeDtypeStruct((1, D), F32)],
        compiler_params=_params(("arbitrary",)), name="post1_bwd",
    )(y1, x1, target, g_post)


def _mid_bwd(dx2, dh1, x1, y0, g_pre1, g_post0):
    def body(dx2_ref, dh_ref, x1_ref, y_ref, g1_ref, gp_ref, dx1_ref, dy_ref, dg1_ref, dgp_ref):
        i = pl.program_id(0)
        xhat, r1 = _rms_stats(x1_ref[...])
        dh = dh_ref[...]
        _acc_rows(dg1_ref, dh * xhat, i)
        dx1 = dx2_ref[...] + _rms_bwd(dh, xhat, r1, g1_ref[...])
        dx1_ref[...] = dx1
        yhat, r0 = _rms_stats(y_ref[...])
        _acc_rows(dgp_ref, dx1 * yhat, i)
        dy_ref[...] = _rms_bwd(dx1, yhat, r0, gp_ref[...]).astype(BF16)

    return pl.pallas_call(
        body, grid=(S // TR,),
        in_specs=[_row_spec(), _row_spec(), _row_spec(), _row_spec(), _vec_spec(), _vec_spec()],
        out_specs=[_row_spec(), _row_spec(), _vec_spec(), _vec_spec()],
        out_shape=[jax.ShapeDtypeStruct((S, D), F32), jax.ShapeDtypeStruct((S, D), BF16),
                   jax.ShapeDtypeStruct((1, D), F32), jax.ShapeDtypeStruct((1, D), F32)],
        compiler_params=_params(("arbitrary",)), name="mid_bwd",
    )(dx2, dh1, x1, y0, g_pre1, g_post0)


def _pre0_bwd(dx1, dh0, x, g):
    def body(dx1_ref, dh_ref, x_ref, g_ref, gx_ref, dg_ref):
        i = pl.program_id(0)
        xhat, r = _rms_stats(x_ref[...])
        dh = dh_ref[...]
        _acc_rows(dg_ref, dh * xhat, i)
        gx_ref[...] = dx1_ref[...] + _rms_bwd(dh, xhat, r, g_ref[...])

    return pl.pallas_call(
        body, grid=(S // TR,), in_specs=[_row_spec(), _row_spec(), _row_spec(), _vec_spec()],
        out_specs=[_row_spec(), _vec_spec()],
        out_shape=[jax.ShapeDtypeStruct((S, D), F32), jax.ShapeDtypeStruct((1, D), F32)],
        compiler_params=_params(("arbitrary",)), name="pre0_bwd",
    )(dx1, dh0, x, g)


POOL_CH = 256


def _pool_apply(a, w, transpose):
    n = a.shape[0]
    row = lax.broadcasted_iota(jnp.int32, a.shape, 0)
    cnt = jnp.minimum(row + 1, w).astype(F32)
    s = a / cnt if transpose else a
    for k in (1, 2, 4, 8):
        if transpose:
            sh = jnp.where(row < n - k, pltpu.roll(s, n - k, 0), 0.0)
        else:
            sh = jnp.where(row >= k, pltpu.roll(s, k, 0), 0.0)
        s = jnp.where(w > k, s + sh, s)
    return s - a if transpose else s / cnt - a


def _pool_fwd(z0, pool_w, pool_scale):
    def body(a_ref, gate_ref, w_ref, sc_ref, out_ref):
        win = jnp.left_shift(2, pl.program_id(0))
        pooled = _pool_apply(a_ref[...], win, False)
        mixed = _dot(pooled.astype(BF16), w_ref[...])
        gate = gate_ref[...]
        out_ref[...] = (mixed * sc_ref[...] * (gate * _sigmoid(gate))).astype(BF16)

    return pl.pallas_call(
        body, grid=(4,),
        in_specs=[pl.BlockSpec((S, POOL_CH), lambda g: (0, g)), pl.BlockSpec((S, POOL_CH), lambda g: (0, 4 + g)),
                  pl.BlockSpec((None, POOL_CH, POOL_CH), lambda g: (g, 0, 0)),
                  pl.BlockSpec((1, POOL_CH), lambda g: (0, g))],
        out_specs=pl.BlockSpec((S, POOL_CH), lambda g: (0, g)),
        out_shape=jax.ShapeDtypeStruct((S, 2048), BF16),
        compiler_params=_params(("parallel",), VMEM_BIG), name="pool_fwd",
    )(z0, z0, pool_w, pool_scale)


def _pool_bwd(z0, dycat, pool_w, pool_scale):
    def body(a_ref, gate_ref, dy_ref, w_ref, sc_ref, da_ref, dgate_ref, dw_ref, dsc_ref):
        win = jnp.left_shift(2, pl.program_id(0))
        pooled = _pool_apply(a_ref[...], win, False).astype(BF16)
        w = w_ref[...]
        mixed = _dot(pooled, w)
        silu, dsilu = _silu_and_grad(gate_ref[...])
        dy = dy_ref[...]
        sc = sc_ref[...]
        dgate_ref[...] = (dy * (mixed * sc) * dsilu).astype(BF16)
        dms = dy * silu
        dsc_ref[...] = jnp.sum(dms * mixed, axis=0, keepdims=True)
        dmixed = (dms * sc).astype(BF16)
        dw_ref[...] = _dot(pooled, dmixed, TN)
        dpooled = _dot(dmixed, w, NT)
        da_ref[...] = _pool_apply(dpooled, win, True).astype(BF16)

    slab = lambda off: pl.BlockSpec((S, POOL_CH), lambda g: (0, off + g))
    return pl.pallas_call(
        body, grid=(4,),
        in_specs=[slab(0), slab(4), slab(0), pl.BlockSpec((None, POOL_CH, POOL_CH), lambda g: (g, 0, 0)),
                  pl.BlockSpec((1, POOL_CH), lambda g: (0, g))],
        out_specs=[slab(0), slab(0), pl.BlockSpec((None, POOL_CH, POOL_CH), lambda g: (g, 0, 0)),
                   pl.BlockSpec((1, POOL_CH), lambda g: (0, g))],
        out_shape=[jax.ShapeDtypeStruct((S, HALF), BF16), jax.ShapeDtypeStruct((S, HALF), BF16),
                   jax.ShapeDtypeStruct((4, POOL_CH, POOL_CH), F32), jax.ShapeDtypeStruct((1, HALF), F32)],
        compiler_params=_params(("parallel",), VMEM_BIG), name="pool_bwd",
    )(z0, z0, dycat, pool_w, pool_scale)


Q_COL, K_COL, V_COL, BG_COL = 2048 // 128, 5120 // 128, 8192 // 128, 11264 // 128
SCALE = HEAD_DIM ** -0.5


def _rope_tables():
    pos = jnp.arange(S, dtype=F32)
    inv_freq = jnp.power(ROPE_THETA, -jnp.arange(0, ROT_DIM, 2, dtype=F32) / ROT_DIM)
    ang = pos[:, None] * inv_freq[None, :]
    cos, sin = jnp.cos(ang), jnp.sin(ang)
    half = ROT_DIM // 2
    zeros = jnp.zeros((S, HEAD_DIM - ROT_DIM), F32)
    c = jnp.concatenate([cos, cos, jnp.ones((S, HEAD_DIM - ROT_DIM), F32)], axis=1)
    a = jnp.concatenate([-sin, jnp.zeros((S, half), F32), zeros], axis=1)
    b = jnp.concatenate([jnp.zeros((S, half), F32), sin, zeros], axis=1)
    return c, a, b


def _rope(t, c, a, b):
    half = ROT_DIM // 2
    return t * c + pltpu.roll(t, HEAD_DIM - half, 1) * a + pltpu.roll(t, half, 1) * b


def _rope_t(d, c, a, b):
    half = ROT_DIM // 2
    return d * c + pltpu.roll(d * a, half, 1) + pltpu.roll(d * b, HEAD_DIM - half, 1)


def _deinterleave(dst, src, dil, cast=None, dst_off=0):
    length = S // dil
    for r in range(dil):
        v = src[...] if dil == 1 else src[pl.ds(r, length, stride=dil), :]
        dst[dst_off + r * length:dst_off + (r + 1) * length, :] = v if cast is None else v.astype(cast)


def _interleave(dst, src, dil, src_off=0):
    length = S // dil
    for r in range(dil):
        if dil == 1:
            dst[...] = src[src_off:src_off + S, :]
        else:
            dst[pl.ds(r, length, stride=dil), :] = src[src_off + r * length:src_off + (r + 1) * length, :]


CU = 4
NUNITS = S // BLK
B_QK = (((2,), (2,)), ((0,), (0,)))
B_PV = (((2,), (1,)), ((0,), (0,)))
B_TN = (((1,), (1,)), ((0,), (0,)))


def _blocks(ref, first):
    return ref[first * BLK:(first + CU) * BLK, :].reshape(CU, BLK, HEAD_DIM)


def _chunk_scores(u0, nb, qd, kdp):
    q = _blocks(qd, u0)
    row = lax.broadcasted_iota(jnp.int32, (CU, BLK, BLK), 1)
    col = lax.broadcasted_iota(jnp.int32, (CU, BLK, BLK), 2)
    s_own = jnp.where(col <= row, _dot(q, _blocks(kdp, u0 + 1), B_QK) * SCALE, NEG)
    if nb == 1:
        return q, s_own, None
    unit = lax.broadcasted_iota(jnp.int32, (CU, BLK, BLK), 0) + u0
    s_prev = jnp.where((col >= row) & ((unit % nb) != 0), _dot(q, _blocks(kdp, u0), B_QK) * SCALE, NEG)
    return q, s_own, s_prev


def _qkv_prep(z0, tabs):
    def body(q_ref, k_ref, v_ref, c_ref, a_ref, b_ref, qo_ref, ko_ref, vo_ref, tmp):
        p = pl.program_id(1)
        for gi, (_, dil) in enumerate(PATTERNS):
            @pl.when(p == gi)
            def _(dil=dil):
                c, a, b = c_ref[...], a_ref[...], b_ref[...]
                tmp[...] = _rope(q_ref[...], c, a, b)
                _deinterleave(qo_ref, tmp, dil, BF16)
                tmp[...] = _rope(k_ref[...], c, a, b)
                _deinterleave(ko_ref, tmp, dil, BF16)
                _deinterleave(vo_ref, v_ref, dil, BF16)

    tab = pl.BlockSpec((S, HEAD_DIM), lambda h, p: (0, 0))
    out = pl.BlockSpec((S, HEAD_DIM), lambda h, p: (0, p * 8 + h))
    return pl.pallas_call(
        body, grid=(8, 3), in_specs=[_head_spec(Q_COL), _head_spec(K_COL), _head_spec(V_COL), tab, tab, tab],
        out_specs=[out, out, out], out_shape=[jax.ShapeDtypeStruct((S, 3072), BF16)] * 3,
        scratch_shapes=[pltpu.VMEM((S, HEAD_DIM), F32)],
        compiler_params=_params(("parallel", "arbitrary"), VMEM_BIG), name="qkv_prep",
    )(z0, z0, z0, *tabs)


def _pad_copy(dst, src):
    dst[0:BLK, :] = jnp.zeros((BLK, HEAD_DIM), dst.dtype)
    dst[BLK:BLK + S, :] = src[...]


def _attn_group_fwd(dil, qd, kd_ref, vd_ref, kdp, vdp, od, ld, og, lg):
    nb = S // dil // BLK
    _pad_copy(kdp, kd_ref)
    _pad_copy(vdp, vd_ref)
    for u0 in range(0, NUNITS, CU):
        _, s_own, s_prev = _chunk_scores(u0, nb, qd, kdp)
        m = jnp.max(s_own, axis=2, keepdims=True)
        if s_prev is not None:
            m = jnp.maximum(m, jnp.max(s_prev, axis=2, keepdims=True))
        p_own = jnp.exp(s_own - m)
        den = jnp.sum(p_own, axis=2, keepdims=True)
        acc = _dot(p_own.astype(BF16), _blocks(vdp, u0 + 1), B_PV)
        if s_prev is not None:
            p_prev = jnp.exp(s_prev - m)
            den = den + jnp.sum(p_prev, axis=2, keepdims=True)
            acc = acc + _dot(p_prev.astype(BF16), _blocks(vdp, u0), B_PV)
        rows = slice(u0 * BLK, (u0 + CU) * BLK)
        od[rows, :] = (acc / den).reshape(CU * BLK, HEAD_DIM)
        ld[rows, :] = jnp.broadcast_to(m + jnp.log(den), (CU, BLK, HEAD_DIM)).reshape(CU * BLK, HEAD_DIM)
    _interleave(og, od, dil)
    _interleave(lg, ld, dil)


def _group_weights(lgs):
    l0, l1, l2 = lgs[0][...], lgs[1][...], lgs[2][...]
    mx = jnp.maximum(l0, jnp.maximum(l1, l2))
    e0, e1, e2 = jnp.exp(l0 - mx), jnp.exp(l1 - mx), jnp.exp(l2 - mx)
    den = e0 + e1 + e2
    return e0 / den, e1 / den, e2 / den


def _head_spec(base, ngroups_axis=True):
    return pl.BlockSpec((S, HEAD_DIM), lambda h, p: (0, base + (p % 3) * 8 + h))


def _slab(dtype=F32, rows=S):
    return pltpu.VMEM((rows, HEAD_DIM), dtype)


def _attn_fwd(z0, qkv, ycat):
    def body(q_ref, k_ref, v_ref, gate_ref, ycat_ref, out_ref, og_ref, lg_ref,
             kdp, vdp, od, ld, og0, og1, og2, lg0, lg1, lg2):
        del ycat_ref
        p = pl.program_id(1)
        ogs, lgs = (og0, og1, og2), (lg0, lg1, lg2)
        for gi, (_, dil) in enumerate(PATTERNS):
            @pl.when(p == gi)
            def _(gi=gi, dil=dil):
                _attn_group_fwd(dil, q_ref, k_ref, v_ref, kdp, vdp, od, ld, ogs[gi], lgs[gi])
                og_ref[...] = ogs[gi][...]
                lg_ref[...] = lgs[gi][...]

        @pl.when(p == 2)
        def _():
            w0, w1, w2 = _group_weights(lgs)
            o = w0 * og0[...] + w1 * og1[...] + w2 * og2[...]
            gate = gate_ref[...]
            out_ref[...] = (o * (gate * _sigmoid(gate))).astype(BF16)

    grp = pl.BlockSpec((S, HEAD_DIM), lambda h, p: (0, p * 8 + h))
    return pl.pallas_call(
        body, grid=(8, 3),
        in_specs=[grp, grp, grp, pl.BlockSpec((S, HEAD_DIM), lambda h, p: (0, BG_COL + h)), ANY_SPEC],
        out_specs=[pl.BlockSpec((S, HEAD_DIM), lambda h, p: (0, 8 + h)), grp, grp],
        out_shape=[jax.ShapeDtypeStruct((S, 2048), BF16), jax.ShapeDtypeStruct((S, 3072), F32),
                   jax.ShapeDtypeStruct((S, 3072), F32)],
        scratch_shapes=[_slab(BF16, S + BLK), _slab(BF16, S + BLK)] + [_slab() for _ in range(8)],
        input_output_aliases={4: 0},
        compiler_params=_params(("parallel", "arbitrary"), VMEM_BIG), name="attn_fwd",
    )(*qkv, z0, ycat)


def _attn_bwd(z0, qkv, og, lg, dycat, tabs):
    def body(q_ref, k_ref, v_ref, gate_ref, dy_ref, c_ref, a_ref, b_ref,
             og0_ref, og1_ref, og2_ref, lg0_ref, lg1_ref, lg2_ref,
             dq_ref, dk_ref, dv_ref, dbg_ref,
             tmp, kd, vd, ld, dg0, dg1, dg2, cg0, cg1, cg2, dod, cd, dqd, dkd, dvd):
        p = pl.program_id(1)
        ogs, lgs, dgs, cgs = (og0_ref, og1_ref, og2_ref), (lg0_ref, lg1_ref, lg2_ref), (dg0, dg1, dg2), (cg0, cg1, cg2)

        @pl.when(p == 0)
        def _():
            w = _group_weights(lgs)
            o = w[0] * ogs[0][...] + w[1] * ogs[1][...] + w[2] * ogs[2][...]
            silu, dsilu = _silu_and_grad(gate_ref[...])
            dy = dy_ref[...]
            dbg_ref[...] = (dy * o * dsilu).astype(BF16)
            do = dy * silu
            dwbar = jnp.sum(do * o, axis=1, keepdims=True)
            for gi in range(3):
                dgs[gi][...] = w[gi] * do
                cgs[gi][...] = -w[gi] * dwbar

        for gi, (_, dil) in enumerate(PATTERNS):
            @pl.when(p == 1 + gi)
            def _(gi=gi, dil=dil):
                nb = S // dil // BLK
                qd = q_ref
                c, a, b = c_ref[...], a_ref[...], b_ref[...]
                _pad_copy(kd, k_ref)
                _pad_copy(vd, v_ref)
                _deinterleave(dod, dgs[gi], dil, BF16)
                _deinterleave(ld, lgs[gi], dil)
                _deinterleave(cd, cgs[gi], dil)
                dkd[...] = jnp.zeros_like(dkd)
                dvd[...] = jnp.zeros_like(dvd)
                flat = lambda t: t.reshape(CU * BLK, HEAD_DIM)
                for u0 in range(0, NUNITS, CU):
                    q, s_own, s_prev = _chunk_scores(u0, nb, qd, kd)
                    lse, cv, do = _blocks(ld, u0), _blocks(cd, u0), _blocks(dod, u0)
                    own = slice((u0 + 1) * BLK, (u0 + 1 + CU) * BLK)
                    p_own = jnp.exp(s_own - lse)
                    ds_own = (p_own * (_dot(do, _blocks(vd, u0 + 1), B_QK) + cv) * SCALE).astype(BF16)
                    dq = _dot(ds_own, _blocks(kd, u0 + 1), B_PV)
                    dkd[own, :] += flat(_dot(ds_own, q, B_TN))
                    dvd[own, :] += flat(_dot(p_own.astype(BF16), do, B_TN))
                    if s_prev is not None:
                        prev = slice(u0 * BLK, (u0 + CU) * BLK)
                        p_prev = jnp.exp(s_prev - lse)
                        ds_prev = (p_prev * (_dot(do, _blocks(vd, u0), B_QK) + cv) * SCALE).astype(BF16)
                        dq = dq + _dot(ds_prev, _blocks(kd, u0), B_PV)
                        dkd[prev, :] += flat(_dot(ds_prev, q, B_TN))
                        dvd[prev, :] += flat(_dot(p_prev.astype(BF16), do, B_TN))
                    dqd[u0 * BLK:(u0 + CU) * BLK, :] = flat(dq)
                _interleave(tmp, dqd, dil)
                dq_ref[...] = _rope_t(tmp[...], c, a, b).astype(BF16)
                _interleave(tmp, dkd, dil, BLK)
                dk_ref[...] = _rope_t(tmp[...], c, a, b).astype(BF16)
                _interleave(tmp, dvd, dil, BLK)
                dv_ref[...] = tmp[...].astype(BF16)

    tab = pl.BlockSpec((S, HEAD_DIM), lambda h, p: (0, 0))
    hspec = lambda base: pl.BlockSpec((S, HEAD_DIM), lambda h, p: (0, base + h))
    gspec = pl.BlockSpec((S, HEAD_DIM), lambda h, p: (0, jnp.maximum(p - 1, 0) * 8 + h))
    return pl.pallas_call(
        body, grid=(8, 4),
        in_specs=[gspec, gspec, gspec, hspec(BG_COL), hspec(8), tab, tab, tab,
                  hspec(0), hspec(8), hspec(16), hspec(0), hspec(8), hspec(16)],
        out_specs=[gspec, gspec, gspec, hspec(0)],
        out_shape=[jax.ShapeDtypeStruct((S, 3072), BF16)] * 3 + [jax.ShapeDtypeStruct((S, HALF), BF16)],
        scratch_shapes=[_slab(), _slab(BF16, S + BLK), _slab(BF16, S + BLK), _slab()] + [_slab() for _ in range(6)]
                       + [_slab(BF16), _slab(), _slab(), _slab(F32, S + BLK), _slab(F32, S + BLK)],
        compiler_params=_params(("parallel", "arbitrary"), VMEM_BIG), name="attn_bwd",
    )(*qkv, z0, dycat, *tabs, og, og, og, lg, lg, lg)


SGU_CH = 256
NCHUNK = TR // 128


def _ln_stats(x):
    mu = jnp.mean(x, axis=-1, keepdims=True)
    xc = x - mu
    r = lax.rsqrt(jnp.mean(xc * xc, axis=-1, keepdims=True) + EPS)
    return xc * r, r


def _ln_bwd(dy, xhat, r, g):
    dxh = dy * g
    return r * (dxh - jnp.mean(dxh, axis=-1, keepdims=True) - xhat * jnp.mean(dxh * xhat, axis=-1, keepdims=True))


def _tril_bf16(w):
    row = lax.broadcasted_iota(jnp.int32, w.shape, 0)
    col = lax.broadcasted_iota(jnp.int32, w.shape, 1)
    return jnp.where(row >= col, w, 0.0).astype(BF16)


def _sgu_gate(vn_s, s_s, w_ref, bb_ref):
    for h in range(4):
        wm = _tril_bf16(w_ref[h])
        bias = bb_ref[h]
        for ch in range(NCHUNK):
            rows, cols = slice(ch * 128, (ch + 1) * 128), slice(h * SGU_CH, (h + 1) * SGU_CH)
            s_s[rows, cols] = _dot(wm, vn_s[rows, cols]) + jnp.concatenate([bias, bias], axis=1)


def _conv_fwd(i, dval_ref, dglu_ref, hval_ref, hglu_ref, cw_ref, cb_ref, xw, dcs):
    halo = hval_ref[...] * _sigmoid(hglu_ref[...])
    xw[0:HALO, :] = jnp.where(i > 0, halo, 0.0)
    xw[HALO:HALO + TR, :] = dval_ref[...] * _sigmoid(dglu_ref[...])
    for rb in range(TR // SUB):
        acc = jnp.broadcast_to(cb_ref[...], (SUB, HALF))
        for k in range(CONV_K):
            acc = acc + cw_ref[k:k + 1, :] * xw[pl.ds(rb * SUB + HALO - (CONV_K - 1) + k, SUB), :]
        dcs[rb * SUB:(rb + 1) * SUB, :] = acc


def _odd_in_specs():
    col = lambda j: pl.BlockSpec((TR, HALF), lambda i, *_: (i, j))
    prev = lambda j: pl.BlockSpec((HALO, HALF), lambda i, *_: (jnp.maximum(i * (TR // HALO) - 1, 0), j))
    return [col(0), col(1), col(2), col(3), col(4), col(5), prev(3), prev(4)]


def _full_spec(shape):
    return pl.BlockSpec(shape, lambda i, *_: (0,) * len(shape))


def _odd_fwd(z1, sgu_g, sgu_b, sgu_w, sgu_bb, conv_w, conv_b, cn_g, cn_b):
    def body(u_ref, v_ref, cg_ref, dval_ref, dglu_ref, dgate_ref, hval_ref, hglu_ref,
             g_ref, b_ref, w_ref, bb_ref, cw_ref, cb_ref, cng_ref, cnb_ref, out_ref, vn_s, s_s, xw, dcs):
        i = pl.program_id(0)
        vhat, _ = _ln_stats(v_ref[...])
        vn_s[...] = (vhat * g_ref[...] + b_ref[...]).astype(BF16)
        _sgu_gate(vn_s, s_s, w_ref, bb_ref)
        cg = cg_ref[...]
        out_ref[:, 0:HALF] = (u_ref[...] * s_s[...] * (cg * _sigmoid(cg))).astype(BF16)
        _conv_fwd(i, dval_ref, dglu_ref, hval_ref, hglu_ref, cw_ref, cb_ref, xw, dcs)
        dhat, _ = _ln_stats(dcs[...])
        dn = dhat * cng_ref[...] + cnb_ref[...]
        dgate = dgate_ref[...]
        out_ref[:, HALF:2 * HALF] = ((dn * _sigmoid(dn)) * (dgate * _sigmoid(dgate))).astype(BF16)

    vec = _full_spec((1, HALF))
    return pl.pallas_call(
        body, grid=(S // TR,),
        in_specs=_odd_in_specs() + [vec, vec, _full_spec((4, 128, 128)), _full_spec((4, 128, 128)),
                                    _full_spec((HALO, HALF)), vec, vec, vec],
        out_specs=pl.BlockSpec((TR, 2048), lambda i: (i, 0)),
        out_shape=jax.ShapeDtypeStruct((S, 2048), BF16),
        scratch_shapes=[pltpu.VMEM((TR, HALF), BF16), pltpu.VMEM((TR, HALF), F32),
                        pltpu.VMEM((HALO + TR, HALF), F32), pltpu.VMEM((TR, HALF), F32)],
        compiler_params=_params(("parallel",), VMEM_BIG), name="odd_fwd",
    )(z1, z1, z1, z1, z1, z1, z1, z1, sgu_g, sgu_b, sgu_w, sgu_bb, conv_w, conv_b, cn_g, cn_b)


def _odd_bwd_a(z1, dycat, sgu_g, sgu_b, sgu_w, sgu_bb, conv_w, conv_b, cn_g, cn_b):
    def body(u_ref, v_ref, cg_ref, dval_ref, dglu_ref, dgate_ref, hval_ref, hglu_ref, dy_ref,
             g_ref, b_ref, w_ref, bb_ref, cw_ref, cb_ref, cng_ref, cnb_ref,
             dz_ref, ddc_ref, dw_ref, dbb_ref, dg_ref, db_ref, dcng_ref, dcnb_ref, dcb_ref,
             vn_s, s_s, xw, dcs, ds_s, dvn_s):
        i = pl.program_id(0)
        vhat, rv = _ln_stats(v_ref[...])
        g = g_ref[...]
        vn_s[...] = (vhat * g + b_ref[...]).astype(BF16)
        _sgu_gate(vn_s, s_s, w_ref, bb_ref)
        silu_c, dsilu_c = _silu_and_grad(cg_ref[...])
        dyc = dy_ref[:, 0:HALF]
        u = u_ref[...]
        s = s_s[...]
        dz_ref[:, 0:HALF] = (dyc * s * silu_c).astype(BF16)
        dz_ref[:, 2 * HALF:3 * HALF] = (dyc * u * s * dsilu_c).astype(BF16)
        ds_s[...] = dyc * u * silu_c

        @pl.when(i == 0)
        def _():
            dw_ref[...] = jnp.zeros_like(dw_ref)
            dbb_ref[...] = jnp.zeros_like(dbb_ref)

        tril = lax.broadcasted_iota(jnp.int32, (128, 128), 0) >= lax.broadcasted_iota(jnp.int32, (128, 128), 1)
        for h in range(4):
            wm = _tril_bf16(w_ref[h])
            for ch in range(NCHUNK):
                rows, cols = slice(ch * 128, (ch + 1) * 128), slice(h * SGU_CH, (h + 1) * SGU_CH)
                ds = ds_s[rows, cols]
                dsb = ds.astype(BF16)
                dw_ref[h] += jnp.where(tril, _dot(dsb, vn_s[rows, cols], NT), 0.0)
                dbb_ref[h] += jnp.broadcast_to(jnp.sum(ds, axis=1, keepdims=True), (128, 128))
                dvn_s[rows, cols] = _dot(wm, dsb, TN)
        dvn = dvn_s[...]
        _acc_rows(dg_ref, dvn * vhat, i)
        _acc_rows(db_ref, dvn, i)
        dz_ref[:, HALF:2 * HALF] = _ln_bwd(dvn, vhat, rv, g).astype(BF16)

        _conv_fwd(i, dval_ref, dglu_ref, hval_ref, hglu_ref, cw_ref, cb_ref, xw, dcs)
        dhat, rd = _ln_stats(dcs[...])
        cng = cng_ref[...]
        silu_n, dsilu_n = _silu_and_grad(dhat * cng + cnb_ref[...])
        silu_g, dsilu_g = _silu_and_grad(dgate_ref[...])
        dyd = dy_ref[:, HALF:2 * HALF]
        dz_ref[:, 5 * HALF:6 * HALF] = (dyd * silu_n * dsilu_g).astype(BF16)
        ddn = dyd * silu_g * dsilu_n
        _acc_rows(dcng_ref, ddn * dhat, i)
        _acc_rows(dcnb_ref, ddn, i)
        ddc = _ln_bwd(ddn, dhat, rd, cng)
        ddc_ref[...] = ddc
        _acc_rows(dcb_ref, ddc, i)

    vec = _full_spec((1, HALF))
    sq = _full_spec((4, 128, 128))
    return pl.pallas_call(
        body, grid=(S // TR,),
        in_specs=_odd_in_specs() + [pl.BlockSpec((TR, 2048), lambda i: (i, 0)),
                                    vec, vec, sq, sq, _full_spec((HALO, HALF)), vec, vec, vec],
        out_specs=[pl.BlockSpec((TR, ODD_IN), lambda i: (i, 0)), pl.BlockSpec((TR, HALF), lambda i: (i, 0)),
                   sq, sq, vec, vec, vec, vec, vec],
        out_shape=[jax.ShapeDtypeStruct((S, ODD_IN), BF16), jax.ShapeDtypeStruct((S, HALF), F32),
                   jax.ShapeDtypeStruct((4, 128, 128), F32), jax.ShapeDtypeStruct((4, 128, 128), F32)]
                  + [jax.ShapeDtypeStruct((1, HALF), F32)] * 5,
        scratch_shapes=[pltpu.VMEM((TR, HALF), BF16), pltpu.VMEM((TR, HALF), F32),
                        pltpu.VMEM((HALO + TR, HALF), F32), pltpu.VMEM((TR, HALF), F32),
                        pltpu.VMEM((TR, HALF), F32), pltpu.VMEM((TR, HALF), F32)],
        compiler_params=_params(("arbitrary",), VMEM_BIG), name="odd_bwd_a",
    )(z1, z1, z1, z1, z1, z1, z1, z1, dycat, sgu_g, sgu_b, sgu_w, sgu_bb, conv_w, conv_b, cn_g, cn_b)


def _odd_bwd_b(z1, ddc, dz1, conv_w):
    nt = S // TR

    def body(dval_ref, dglu_ref, hval_ref, hglu_ref, ddc_ref, hddc_ref, cw_ref, dz_in_ref,
             dz_ref, dcw_ref, xw, dwin, dxs):
        del dz_in_ref
        i, j = pl.program_id(0), pl.program_id(1)
        sg = _sigmoid(dglu_ref[...])
        dval = dval_ref[...]

        @pl.when(j == 0)
        def _():
            halo = hval_ref[...] * _sigmoid(hglu_ref[...])
            xw[0:HALO, :] = jnp.where(i > 0, halo, 0.0)
            xw[HALO:HALO + TR, :] = dval * sg
            dwin[0:TR, :] = ddc_ref[...]
            dwin[TR:TR + HALO, :] = jnp.where(i < nt - 1, hddc_ref[...], 0.0)

            @pl.when(i == 0)
            def _():
                dcw_ref[...] = jnp.zeros_like(dcw_ref)

            for rb in range(TR // SUB):
                acc = jnp.zeros((SUB, HALF), F32)
                for k in range(CONV_K):
                    acc = acc + cw_ref[k:k + 1, :] * dwin[pl.ds(rb * SUB + (CONV_K - 1) - k, SUB), :]
                dxs[rb * SUB:(rb + 1) * SUB, :] = acc
            for k in range(CONV_K):
                acc = jnp.zeros((SUB, HALF), F32)
                for rb in range(TR // SUB):
                    acc = acc + dwin[rb * SUB:(rb + 1) * SUB, :] * xw[pl.ds(rb * SUB + HALO - (CONV_K - 1) + k, SUB), :]
                dcw_ref[k:k + 1, :] += jnp.sum(acc, axis=0, keepdims=True)
            dz_ref[...] = (dxs[...] * sg).astype(BF16)

        @pl.when(j == 1)
        def _():
            dz_ref[...] = (dxs[...] * dval * sg * (1.0 - sg)).astype(BF16)

    col = lambda c: pl.BlockSpec((TR, HALF), lambda i, j: (i, c))
    prev = lambda c: pl.BlockSpec((HALO, HALF), lambda i, j: (jnp.maximum(i * (TR // HALO) - 1, 0), c))
    nxt = pl.BlockSpec((HALO, HALF), lambda i, j: (jnp.minimum((i + 1) * (TR // HALO), S // HALO - 1), 0))
    return pl.pallas_call(
        body, grid=(nt, 2),
        in_specs=[col(3), col(4), prev(3), prev(4), pl.BlockSpec((TR, HALF), lambda i, j: (i, 0)), nxt,
                  _full_spec((HALO, HALF)), pl.BlockSpec(memory_space=pl.ANY)],
        out_specs=[pl.BlockSpec((TR, HALF), lambda i, j: (i, 3 + j)), _full_spec((HALO, HALF))],
        out_shape=[jax.ShapeDtypeStruct((S, ODD_IN), BF16), jax.ShapeDtypeStruct((HALO, HALF), F32)],
        scratch_shapes=[pltpu.VMEM((HALO + TR, HALF), F32), pltpu.VMEM((TR + HALO, HALF), F32),
                        pltpu.VMEM((TR, HALF), F32)],
        input_output_aliases={7: 0},
        compiler_params=_params(("arbitrary", "arbitrary"), VMEM_BIG), name="odd_bwd_b",
    )(z1, z1, z1, z1, ddc, ddc, conv_w, dz1)


def _cast_bf16(w, name):
    r, c = w.shape
    tr = min(r, 256)
    def body(i_ref, o_ref):
        o_ref[...] = i_ref[...].astype(BF16)

    return pl.pallas_call(
        body, grid=(r // tr,), in_specs=[pl.BlockSpec((tr, c), lambda i: (i, 0))],
        out_specs=pl.BlockSpec((tr, c), lambda i: (i, 0)), out_shape=jax.ShapeDtypeStruct((r, c), BF16),
        compiler_params=_params(("parallel",)), name=name,
    )(w)


def _adamw(w, g, m, v):
    m = ADAM_B1 * m + (1.0 - ADAM_B1) * g
    v = ADAM_B2 * v + (1.0 - ADAM_B2) * (g * g)
    m_hat = m / (1.0 - ADAM_B1 ** ADAM_STEP)
    v_hat = v / (1.0 - ADAM_B2 ** ADAM_STEP)
    delta = -ADAM_LR * (m_hat / (jnp.sqrt(v_hat) + ADAM_EPS) + ADAM_WD * w)
    return delta, m, v


def _adam_reduce(parts, w, m, v, name, dep=None):
    r, c = w.shape
    tr = min(r, 128)
    deps = [] if dep is None else [dep]
    nparts = parts.shape[0]

    def body(p_ref, w_ref, m_ref, v_ref, *rest):
        g_ref, d_ref, nm_ref, nv_ref = rest[len(deps):]
        g = p_ref[0].astype(F32)
        for d in range(1, nparts):
            g = g + p_ref[d].astype(F32)
        g_ref[...] = g
        d_ref[...], nm_ref[...], nv_ref[...] = _adamw(w_ref[...], g, m_ref[...], v_ref[...])

    spec = pl.BlockSpec((tr, c), lambda i: (i, 0))
    return pl.pallas_call(
        body, grid=(r // tr,),
        in_specs=[pl.BlockSpec((nparts, tr, c), lambda i: (0, i, 0)), spec, spec, spec] + [ANY_SPEC] * len(deps),
        out_specs=[spec] * 4, out_shape=[jax.ShapeDtypeStruct((r, c), F32)] * 4,
        compiler_params=_params(("parallel",), VMEM_BIG), name=name,
    )(parts, w, m, v, *deps)


def _sum_parts(parts, name, dep=None):
    r = parts.shape[1]
    tr = 8
    for cand in (512, 256, 128, 64, 32, 16, 8):
        if r % cand == 0:
            tr = cand
            break
    deps = [] if dep is None else [dep]

    def body(p_ref, *rest):
        g = p_ref[0]
        for d in range(1, NDEV):
            g = g + p_ref[d]
        rest[-1][...] = g

    return pl.pallas_call(
        body, grid=(r // tr,), in_specs=[pl.BlockSpec((NDEV, tr, 128), lambda i: (0, i, 0))] + [ANY_SPEC] * len(deps),
        out_specs=pl.BlockSpec((tr, 128), lambda i: (i, 0)), out_shape=jax.ShapeDtypeStruct((r, 128), F32),
        compiler_params=_params(("parallel",)), name=name,
    )(parts, *deps)


def _adam_plain(w, g, m, v, name):
    r, c = w.shape

    def body(w_ref, g_ref, m_ref, v_ref, d_ref, nm_ref, nv_ref):
        d_ref[...], nm_ref[...], nv_ref[...] = _adamw(w_ref[...], g_ref[...], m_ref[...], v_ref[...])

    spec = pl.BlockSpec((r, c), lambda i: (0, 0))
    return pl.pallas_call(
        body, grid=(1,), in_specs=[spec] * 4, out_specs=[spec] * 3,
        out_shape=[jax.ShapeDtypeStruct((r, c), F32)] * 3,
        compiler_params=_params(("arbitrary",)), name=name,
    )(w, g, m, v)


MASKS = [(mx, my, mc) for mx in (0, 1) for my in (0, 1) for mc in (0, 1)][1:]


def _exchange(arrays, scatter, name):
    nt = len(arrays)
    out_shape = [jax.ShapeDtypeStruct(((NDEV,) + a.shape) if not scatter else a.shape, a.dtype) for a in arrays]

    def body(*refs):
        ins, outs = refs[:nt], refs[nt:2 * nt]
        send_sems, recv_sems, local_sems = refs[2 * nt:]
        x, y, c = lax.axis_index("x"), lax.axis_index("y"), lax.axis_index("c")
        me = 4 * x + 2 * y + c
        copies = []
        for t in range(nt):
            src_own = ins[t].at[me] if scatter else ins[t]
            loc = pltpu.make_async_copy(src_own, outs[t].at[me], local_sems.at[t])
            loc.start()
            copies.append(loc)
            for k, (mx, my, mc) in enumerate(MASKS):
                px, py, pc = (x + mx) % 2, (y + my) % 2, (c + mc) % 2
                peer = 4 * px + 2 * py + pc
                src = ins[t].at[peer] if scatter else ins[t]
                rc = pltpu.make_async_remote_copy(
                    src_ref=src, dst_ref=outs[t].at[me], send_sem=send_sems.at[t, k], recv_sem=recv_sems.at[t, k],
                    device_id=(px, py, pc), device_id_type=MESH)
                rc.start()
                copies.append(rc)
        for cp in copies:
            cp.wait()

    hbm = pl.BlockSpec(memory_space=pl.ANY)
    return pl.pallas_call(
        body, in_specs=[hbm] * nt, out_specs=[hbm] * nt, out_shape=out_shape,
        scratch_shapes=[pltpu.SemaphoreType.DMA((nt, 7)), pltpu.SemaphoreType.DMA((nt, 7)),
                        pltpu.SemaphoreType.DMA((nt,))],
        name=name,
    )(*arrays)


SEM_SPEC = pl.BlockSpec(memory_space=pltpu.SEMAPHORE)
EFFECT = pltpu.SideEffectType.DATAFLOW_SIDE_EFFECTING


def _direct_plan(scatter):
    def plan(x, y, c, srcs, lands):
        me = 4 * x + 2 * y + c
        local, remote = [], []
        for src, land in zip(srcs, lands):
            local.append((src.at[me] if scatter else src, land.at[me]))
            for mx, my, mc in MASKS:
                px, py, pc = (x + mx) % 2, (y + my) % 2, (c + mc) % 2
                blk = src.at[4 * px + 2 * py + pc] if scatter else src
                remote.append((blk, land.at[me], (px, py, pc)))
        return local, remote
    return plan


def _split_start(name, srcs, land_shapes, plan, n_local, n_remote, dep=None):
    ns, nl = len(srcs), len(land_shapes)
    deps = [] if dep is None else [dep]
    lands = [lax.empty(s.shape, s.dtype) for s in land_shapes]

    def body(*refs):
        ins, lz = refs[:ns], refs[ns:ns + nl]
        outs = refs[ns + nl + len(deps):]
        send_sems, recv_sems, token, local_sems = outs[0], outs[1], outs[2 + ns + nl], outs[3 + ns + nl]
        local, remote = plan(lax.axis_index("x"), lax.axis_index("y"), lax.axis_index("c"), ins, lz)
        own = [pltpu.make_async_copy(src, dst, local_sems.at[i]) for i, (src, dst) in enumerate(local)]
        for cp in own:
            cp.start()
        for cp in own:
            cp.wait()
        for k, (src, dst, peer) in enumerate(remote):
            pltpu.make_async_remote_copy(src_ref=src, dst_ref=dst, send_sem=send_sems.at[k], recv_sem=recv_sems.at[k],
                                         device_id=peer, device_id_type=MESH).start()
        token[...] = jnp.zeros_like(token)

    hbm = lambda a: pltpu.HBM(a.shape, a.dtype)
    outs = pl.pallas_call(
        body, name=name,
        out_shape=(pltpu.SemaphoreType.DMA((n_remote,)), pltpu.SemaphoreType.DMA((n_remote,)),
                   *[hbm(a) for a in srcs], *[hbm(a) for a in lands], jax.ShapeDtypeStruct((8, 128), F32)),
        in_specs=[ANY_SPEC] * (ns + nl + len(deps)),
        out_specs=(SEM_SPEC, SEM_SPEC, *[ANY_SPEC] * (ns + nl), pl.BlockSpec(memory_space=pltpu.VMEM)),
        scratch_shapes=[pltpu.SemaphoreType.DMA((n_local,))],
        input_output_aliases={i: 2 + i for i in range(ns + nl)},
        compiler_params=pltpu.CompilerParams(has_side_effects=EFFECT),
    )(*[pltpu.with_memory_space_constraint(a, pltpu.HBM) for a in srcs],
      *[pltpu.with_memory_space_constraint(a, pltpu.HBM) for a in lands], *deps)
    return dict(sems=outs[:2], srcs=outs[2:2 + ns], lands=outs[2 + ns:2 + ns + nl], token=outs[-1],
                plan=plan, n_remote=n_remote)


def _split_wait(name, handle, after):
    srcs, lands, plan = handle["srcs"], handle["lands"], handle["plan"]
    ns, nl = len(srcs), len(lands)

    def body(*refs):
        ins, lz = refs[:ns], refs[ns:ns + nl]
        send_sems, recv_sems = refs[ns + nl], refs[ns + nl + 1]
        _, remote = plan(lax.axis_index("x"), lax.axis_index("y"), lax.axis_index("c"), ins, lz)
        for k, (src, dst, peer) in enumerate(remote):
            cp = pltpu.make_async_remote_copy(src_ref=src, dst_ref=dst, send_sem=send_sems.at[k],
                                              recv_sem=recv_sems.at[k], device_id=peer, device_id_type=MESH)
            cp.wait_send()
            cp.wait_recv()

    hbm = lambda a: pltpu.HBM(a.shape, a.dtype)
    outs = pl.pallas_call(
        body, name=name, out_shape=(*[hbm(a) for a in srcs], *[hbm(a) for a in lands]),
        in_specs=[ANY_SPEC] * (ns + nl) + [SEM_SPEC, SEM_SPEC, ANY_SPEC], out_specs=tuple([ANY_SPEC] * (ns + nl)),
        input_output_aliases={i: i for i in range(ns + nl)},
        compiler_params=pltpu.CompilerParams(has_side_effects=EFFECT),
    )(*srcs, *lands, *handle["sems"], after)
    return list(outs[ns:])


def _sc_exchange(name, collective_id, arrays, scatter):
    nt = len(arrays)
    out_type = [jax.ShapeDtypeStruct(a.shape if scatter else (NDEV,) + a.shape, a.dtype) for a in arrays]

    def body(*refs):
        ins, outs = refs[:nt], refs[nt:2 * nt]
        send_sems, recv_sems, local_sems = refs[2 * nt:3 * nt], refs[3 * nt:4 * nt], refs[4 * nt:5 * nt]
        x, y, c = lax.axis_index("x"), lax.axis_index("y"), lax.axis_index("c")
        peers = [(mx + x - 2 * mx * x, my + y - 2 * my * y, mc + c - 2 * mc * c) for mx, my, mc in MASKS]
        barrier = pltpu.get_barrier_semaphore()
        for peer in peers:
            pl.semaphore_signal(barrier, inc=1, device_id=peer, device_id_type=MESH)
        pl.semaphore_wait(barrier, len(peers))
        me = 4 * x + 2 * y + c
        own = []
        for t in range(nt):
            cp = pltpu.make_async_copy(ins[t].at[me] if scatter else ins[t], outs[t].at[me], local_sems[t])
            cp.start()
            own.append(cp)
            for px, py, pc in peers:
                src = ins[t].at[4 * px + 2 * py + pc] if scatter else ins[t]
                pltpu.make_async_remote_copy(src_ref=src, dst_ref=outs[t].at[me], send_sem=send_sems[t],
                                             recv_sem=recv_sems[t], device_id=(px, py, pc), device_id_type=MESH).start()
        for t in range(nt):
            own[t].wait()
            seven = outs[t].at[pl.ds(0, NDEV - 1)]
            drain = pltpu.make_async_remote_copy(src_ref=seven, dst_ref=seven, send_sem=send_sems[t],
                                                 recv_sem=recv_sems[t], device_id=(x, y, c), device_id_type=MESH)
            drain.wait_send()
            drain.wait_recv()

    return pl.kernel(
        body, out_type=out_type, mesh=plsc.ScalarSubcoreMesh(axis_name="sequencer", num_cores=1),
        scratch_types=[pltpu.SemaphoreType.DMA] * (3 * nt),
        compiler_params=pltpu.CompilerParams(collective_id=collective_id), name=name,
    )(*arrays)


def _sc_gather_two_level(name, collective_id, arrays):
    nt = len(arrays)
    out_type = [jax.ShapeDtypeStruct((NDEV,) + a.shape, a.dtype) for a in arrays]

    def body(*refs):
        ins, outs = refs[:nt], refs[nt:2 * nt]
        sems = refs[2 * nt:]
        send_sems, sib_sems, local_sems = sems[:nt], sems[nt:2 * nt], sems[2 * nt:3 * nt]
        ici_sems = [sems[3 * nt + 3 * t:3 * nt + 3 * t + 3] for t in range(nt)]
        x, y, c = lax.axis_index("x"), lax.axis_index("y"), lax.axis_index("c")
        sibling = (x, y, 1 - c)
        chips = [(1 - x, y), (x, 1 - y), (1 - x, 1 - y)]
        barrier = pltpu.get_barrier_semaphore()
        for peer in [sibling] + [(cx, cy, c) for cx, cy in chips]:
            pl.semaphore_signal(barrier, inc=1, device_id=peer, device_id_type=MESH)
        pl.semaphore_wait(barrier, 4)
        me = 4 * x + 2 * y + c

        def push(t, src, slot, recv_sem, to):
            pltpu.make_async_remote_copy(src_ref=src, dst_ref=outs[t].at[slot], send_sem=send_sems[t],
                                         recv_sem=recv_sem, device_id=to, device_id_type=MESH).start()

        own = []
        for t in range(nt):
            cp = pltpu.make_async_copy(ins[t], outs[t].at[me], local_sems[t])
            cp.start()
            own.append(cp)
            for j, (cx, cy) in enumerate(chips):
                push(t, ins[t], me, ici_sems[t][j], (cx, cy, c))
            push(t, ins[t], me, sib_sems[t], sibling)
        for t in range(nt):
            for j, (cx, cy) in enumerate(chips):
                slot = 4 * cx + 2 * cy + c
                landed = outs[t].at[slot]
                pltpu.make_async_remote_copy(src_ref=landed, dst_ref=landed, send_sem=send_sems[t],
                                             recv_sem=ici_sems[t][j], device_id=(cx, cy, c),
                                             device_id_type=MESH).wait_recv()
                push(t, landed, slot, sib_sems[t], sibling)
        for t in range(nt):
            own[t].wait()
            four, seven = outs[t].at[pl.ds(0, 4)], outs[t].at[pl.ds(0, 7)]
            pltpu.make_async_remote_copy(src_ref=four, dst_ref=four, send_sem=send_sems[t], recv_sem=sib_sems[t],
                                         device_id=sibling, device_id_type=MESH).wait_recv()
            pltpu.make_async_remote_copy(src_ref=seven, dst_ref=seven, send_sem=send_sems[t], recv_sem=sib_sems[t],
                                         device_id=sibling, device_id_type=MESH).wait_send()

    return pl.kernel(
        body, out_type=out_type, mesh=plsc.ScalarSubcoreMesh(axis_name="sequencer", num_cores=1),
        scratch_types=[pltpu.SemaphoreType.DMA] * (6 * nt),
        compiler_params=pltpu.CompilerParams(collective_id=collective_id), name=name,
    )(*arrays)


def _sc_sibling_exchange(name, collective_id, src, out_shape, pieces):
    def body(src_ref, out_ref, send_sem, recv_sem):
        x, y, c = lax.axis_index("x"), lax.axis_index("y"), lax.axis_index("c")
        sibling = (x, y, 1 - c)
        barrier = pltpu.get_barrier_semaphore()
        pl.semaphore_signal(barrier, inc=1, device_id=sibling, device_id_type=MESH)
        pl.semaphore_wait(barrier, 1)
        for piece, lands in pieces(c, src_ref, out_ref):
            pltpu.make_async_remote_copy(src_ref=piece, dst_ref=lands, send_sem=send_sem, recv_sem=recv_sem,
                                         device_id=sibling, device_id_type=MESH).start()
        drain = pltpu.make_async_remote_copy(src_ref=out_ref, dst_ref=out_ref, send_sem=send_sem, recv_sem=recv_sem,
                                             device_id=sibling, device_id_type=MESH)
        drain.wait_send()
        drain.wait_recv()

    return pl.kernel(
        body, out_type=jax.ShapeDtypeStruct(out_shape, src.dtype),
        mesh=plsc.ScalarSubcoreMesh(axis_name="sequencer", num_cores=1), scratch_types=[pltpu.SemaphoreType.DMA] * 2,
        compiler_params=pltpu.CompilerParams(collective_id=collective_id), name=name,
    )(src)


def _sc_chip_scatter(name, collective_id, q):
    def body(q_ref, out_ref, send_sem, recv_sem, local_sem):
        x, y, c = lax.axis_index("x"), lax.axis_index("y"), lax.axis_index("c")
        chips = [(1 - x, y), (x, 1 - y), (1 - x, 1 - y)]
        barrier = pltpu.get_barrier_semaphore()
        for cx, cy in chips:
            pl.semaphore_signal(barrier, inc=1, device_id=(cx, cy, c), device_id_type=MESH)
        pl.semaphore_wait(barrier, 3)
        mine = 2 * x + y
        own = pltpu.make_async_copy(q_ref.at[mine], out_ref.at[mine], local_sem)
        own.start()
        for cx, cy in chips:
            pltpu.make_async_remote_copy(src_ref=q_ref.at[2 * cx + cy], dst_ref=out_ref.at[mine], send_sem=send_sem,
                                         recv_sem=recv_sem, device_id=(cx, cy, c), device_id_type=MESH).start()
        own.wait()
        three = out_ref.at[pl.ds(0, 3)]
        drain = pltpu.make_async_remote_copy(src_ref=three, dst_ref=three, send_sem=send_sem, recv_sem=recv_sem,
                                             device_id=(x, y, c), device_id_type=MESH)
        drain.wait_send()
        drain.wait_recv()

    return pl.kernel(
        body, out_type=jax.ShapeDtypeStruct(q.shape, q.dtype),
        mesh=plsc.ScalarSubcoreMesh(axis_name="sequencer", num_cores=1), scratch_types=[pltpu.SemaphoreType.DMA] * 3,
        compiler_params=pltpu.CompilerParams(collective_id=collective_id), name=name,
    )(q)


def _mm_pair_dw(h_own, dz, h_sib, dz_sib, nb, name, dep=None):
    tn = 512 if nb % 512 == 0 else nb
    per = nb // tn
    o_spec = pl.BlockSpec((None, D, tn), lambda i, j, k: (j // per, 0, j % per))
    part = _matmul(
        h_own, dz, dn=TN, grid=(1, 4 * per, 1),
        a_spec=pl.BlockSpec((S, D), lambda i, j, k: (0, 0)),
        b_spec=pl.BlockSpec((S, tn), lambda i, j, k: (0, (2 * (j // per) + lax.axis_index("c")) * per + j % per)),
        o_spec=o_spec, out_shape=(4, D, nb), out_dtype=F32, acc_shape=(D, tn), name=name + "_own", dep=dep)

    def body(a_ref, b_ref, p_ref, o_ref):
        o_ref[...] = (p_ref[...] + _dot(a_ref[...], b_ref[...], TN)).astype(BF16)

    return pl.pallas_call(
        body, grid=(1, 4 * per, 1),
        in_specs=[pl.BlockSpec((S, D), lambda i, j, k: (0, 0)), pl.BlockSpec((S, tn), lambda i, j, k: (0, j)), o_spec],
        out_specs=o_spec, out_shape=jax.ShapeDtypeStruct((4, D, nb), BF16),
        compiler_params=_params(("parallel", "parallel", "arbitrary"), VMEM_BIG), name=name + "_sibling",
    )(h_sib, dz_sib, part)


SMALL = {
    "e_pre_norm": ((2048,), None), "e_pool_w": ((4, 256, 256), 1), "e_pool_scale": ((1024,), None),
    "e_post_norm": ((2048,), None), "o_pre_norm": ((2048,), 0), "o_sgu_norm_g": ((1024,), 0),
    "o_sgu_norm_b": ((1024,), 0), "o_sgu_w": ((4, 128, 128), None), "o_sgu_b": ((4, 128), None),
    "o_conv_w": ((31, 1024), 1), "o_conv_b": ((1024,), 0), "o_conv_norm_g": ((1024,), 0),
    "o_conv_norm_b": ((1024,), 0), "o_post_norm": ((2048,), 0),
}
SMALL_SHARDED = [n for n, (_, ax) in SMALL.items() if ax is not None]


def _shard_shape(name):
    shape, ax = SMALL[name]
    if ax is None:
        return shape
    return tuple(s // NDEV if i == ax else s for i, s in enumerate(shape))


def _pack(arrs, row_multiple=1):
    flat = jnp.concatenate([a.reshape(-1) for a in arrs])
    pad = -flat.shape[0] % (128 * row_multiple)
    return jnp.concatenate([flat, jnp.zeros((pad,), F32)]).reshape(-1, 128)


def _unpack(buf, shapes):
    flat = buf.reshape(-1)
    out, off = [], 0
    for shp in shapes:
        n = int(np.prod(shp))
        out.append(flat[off:off + n].reshape(shp))
        off += n
    return out


def _take_shard(full, name, me):
    shape, ax = SMALL[name]
    if ax is None:
        return full
    n = shape[ax] // NDEV
    return lax.dynamic_slice_in_dim(full, me * n, n, axis=ax)


BIG = ("e_w_in", "e_w_out", "o_w_in", "o_w_out")
WEIGHTS = ["e_pre_norm", "e_w_in", "e_pool_w", "e_pool_scale", "e_w_out", "e_post_norm", "o_pre_norm", "o_w_in",
           "o_sgu_norm_g", "o_sgu_norm_b", "o_sgu_w", "o_sgu_b", "o_conv_w", "o_conv_b", "o_conv_norm_g",
           "o_conv_norm_b", "o_w_out", "o_post_norm"]


def kernel(x, e_pre_norm, e_w_in, e_pool_w, e_pool_scale, e_w_out, e_post_norm, o_pre_norm, o_w_in, o_sgu_norm_g, o_sgu_norm_b, o_sgu_w, o_sgu_b, o_conv_w, o_conv_b, o_conv_norm_g, o_conv_norm_b, o_w_out, o_post_norm, loss_target, m_e_pre_norm, m_e_w_in, m_e_pool_w, m_e_pool_scale, m_e_w_out, m_e_post_norm, m_o_pre_norm, m_o_w_in, m_o_sgu_norm_g, m_o_sgu_norm_b, m_o_sgu_w, m_o_sgu_b, m_o_conv_w, m_o_conv_b, m_o_conv_norm_g, m_o_conv_norm_b, m_o_w_out, m_o_post_norm, v_e_pre_norm, v_e_w_in, v_e_pool_w, v_e_pool_scale, v_e_w_out, v_e_post_norm, v_o_pre_norm, v_o_w_in, v_o_sgu_norm_g, v_o_sgu_norm_b, v_o_sgu_w, v_o_sgu_b, v_o_conv_w, v_o_conv_b, v_o_conv_norm_g, v_o_conv_norm_b, v_o_w_out, v_o_post_norm):
    given = dict(locals())
    w = {n: given[n][0] for n in WEIGHTS}
    m = {n: given["m_" + n][0] for n in WEIGHTS}
    v = {n: given["v_" + n][0] for n in WEIGHTS}
    me = 4 * lax.axis_index("x") + 2 * lax.axis_index("y") + lax.axis_index("c")
    x, target = x[0], loss_target[0]
    row = lambda a: a.reshape(1, -1)

    bf = {n: _cast_bf16(w[n], "cast_" + n) for n in BIG}
    wg_e_in, small_rows = _sc_gather_two_level("gather_a", 0, [bf["e_w_in"], _pack([w[n] for n in SMALL_SHARDED])])
    wg_e_out, wg_o_in, wg_o_out = _sc_gather_two_level("gather_b", 1, [bf["e_w_out"], bf["o_w_in"], bf["o_w_out"]])
    h0 = _pre0_fwd(x, row(w["e_pre_norm"]))
    h0_sib = _sc_sibling_exchange("swap_h0", 8, h0, h0.shape, lambda c, src, out: [(src, out)])
    p = {n: w[n] for n in SMALL if SMALL[n][1] is None}
    small_rows = small_rows.reshape(NDEV, -1)
    off = 0
    for n in SMALL_SHARDED:
        shp, ax = _shard_shape(n), SMALL[n][1]
        cnt = int(np.prod(shp))
        blk = small_rows[:, off:off + cnt].reshape((NDEV,) + shp)
        p[n] = jnp.moveaxis(blk, 0, ax).reshape(SMALL[n][0])
        off += cnt
    tabs = _rope_tables()
    pool_w_bf = p["e_pool_w"].astype(BF16)
    sgu_bb = jnp.broadcast_to(p["o_sgu_b"][:, :, None], (4, 128, 128))
    conv_w = jnp.concatenate([p["o_conv_w"], jnp.zeros((HALO - CONV_K, HALF), F32)], axis=0)
    odd_p = (row(p["o_sgu_norm_g"]), row(p["o_sgu_norm_b"]), p["o_sgu_w"], sgu_bb, conv_w,
             row(p["o_conv_b"]), row(p["o_conv_norm_g"]), row(p["o_conv_norm_b"]))

    z0 = _mm_in(h0, wg_e_in, "mm_z0")
    ycat0 = _pool_fwd(z0, pool_w_bf, row(p["e_pool_scale"]))
    qkv = _qkv_prep(z0, tabs)
    ycat0, og, lg = _attn_fwd(z0, qkv, ycat0)
    w_out_e, w_out_o = wg_e_out.reshape(2048, D), wg_o_out.reshape(2048, D)
    y0 = _mm_out(ycat0, w_out_e, "mm_y0", h0_sib)
    x1, h1 = _post0_fwd(x, y0, row(p["e_post_norm"]), row(p["o_pre_norm"]))
    z1 = _mm_in(h1, wg_o_in, "mm_z1")
    ycat1 = _odd_fwd(z1, *odd_p)
    y1 = _mm_out(ycat1, w_out_o, "mm_y1")

    g = {}
    loss, dx2, dy1, g["o_post_norm"] = _post1_bwd(y1, x1, target, row(p["o_post_norm"]))
    loss = lax.psum(loss[0, 0], ("x", "y", "c"))
    parts = {}
    dw = _mm_out_dw(ycat1, dy1, "mm_dwout1").reshape(NDEV, 256, D)
    parts["o_w_out"], = _sc_exchange("scatter_o_w_out", 2, [dw], True)
    dycat1 = _mm_out_dx(dy1, w_out_o, "mm_dycat1", (dw, loss.reshape(1, 1)))
    dz1, ddc, g["o_sgu_w"], d_sgu_bb, g["o_sgu_norm_g"], g["o_sgu_norm_b"], g["o_conv_norm_g"], \
        g["o_conv_norm_b"], g["o_conv_b"] = _odd_bwd_a(z1, dycat1, *odd_p)
    dz1, d_conv_w = _odd_bwd_b(z1, ddc, dz1, conv_w)
    g["o_sgu_b"] = d_sgu_bb[:, :, 0]
    g["o_conv_w"] = d_conv_w[:CONV_K]
    grads, deltas, new_m, new_v = {}, {}, {}, {}

    def adam(n, dep):
        grads[n], deltas[n], new_m[n], new_v[n] = _adam_reduce(parts[n], w[n], m[n], v[n], "adam_" + n, dep)
        return new_v[n]

    pin = adam("o_w_out", d_conv_w)
    dw = _mm_in_dw(h1, dz1, ODD_IN // NDEV, "mm_dwin1", pin)
    parts["o_w_in"], = _sc_exchange("scatter_o_w_in", 3, [dw], True)
    dh1 = _mm_in_dx(dz1, wg_o_in, "mm_dh1", dw)
    dx1, dy0, g["o_pre_norm"], g["e_post_norm"] = _mid_bwd(dx2, dh1, x1, y0, row(p["o_pre_norm"]),
                                                           row(p["e_post_norm"]))
    dw = _mm_out_dw(ycat0, dy0, "mm_dwout0").reshape(NDEV, 256, D)
    parts["e_w_out"], = _sc_exchange("scatter_e_w_out", 4, [dw], True)
    dycat0 = _mm_out_dx(dy0, w_out_e, "mm_dycat0", dw)
    da_in, da_gate, g["e_pool_w"], g["e_pool_scale"] = _pool_bwd(z0, dycat0, pool_w_bf, row(p["e_pool_scale"]))
    dq, dk, dv, dbg = _attn_bwd(z0, qkv, og, lg, dycat0, tabs)
    dz0 = jnp.concatenate([da_in, da_gate, dq, dk, dv, dbg], axis=1)
    late = [n for n in SMALL if n != "e_pre_norm"]
    pin = adam("e_w_out", adam("o_w_in", dbg))
    nb = EVEN_IN // NDEV
    dz0_sib = _sc_sibling_exchange(
        "swap_dz0", 9, dz0, (S, 4 * nb),
        lambda c, src, out: [(src.at[:, pl.ds((2 * j + 1 - c) * nb, nb)], out.at[:, pl.ds(j * nb, nb)])
                             for j in range(4)])
    dw = _mm_pair_dw(h0, dz0, h0_sib, dz0_sib, nb, "mm_dwin0", pin)
    parts["e_w_in"] = _sc_chip_scatter("scatter_e_w_in", 5, dw)
    recv_small, = _sc_exchange("gather_small_grads", 6, [_pack([g[n].reshape(SMALL[n][0]) for n in late], 512)], False)
    dh0 = _mm_in_dx(dz0, wg_e_in, "mm_dh0", dw)
    grad_x, g["e_pre_norm"] = _pre0_bwd(dx1, dh0, x, row(p["e_pre_norm"]))
    last, = _sc_exchange("gather_e_pre_norm_grad", 7, [g["e_pre_norm"].reshape(16, 128)], False)

    g_small = dict(zip(late, _unpack(_sum_parts(recv_small, "sum_small_grads"), [SMALL[n][0] for n in late])))
    pin = adam("e_w_in", grad_x)
    g_small["e_pre_norm"] = _sum_parts(last, "sum_e_pre_norm_grad", pin).reshape(2048)
    for n in SMALL:
        grads[n] = _take_shard(g_small[n], n, me)
    names = list(SMALL)
    shapes = [_shard_shape(n) for n in names]
    d_pack, m_pack, v_pack = _adam_plain(_pack([w[n] for n in names]), _pack([grads[n] for n in names]),
                                         _pack([m[n] for n in names]), _pack([v[n] for n in names]), "adam_small")
    for n, d_, m_, v_ in zip(names, _unpack(d_pack, shapes), _unpack(m_pack, shapes), _unpack(v_pack, shapes)):
        deltas[n], new_m[n], new_v[n] = d_, m_, v_

    lead = lambda a: a[None]
    return (loss, grad_x[None], *[lead(grads[n]) for n in WEIGHTS], *[lead(deltas[n]) for n in WEIGHTS],
            *[lead(new_m[n]) for n in WEIGHTS], *[lead(new_v[n]) for n in WEIGHTS])
```

```python
import functools

import numpy as np
import jax
import jax.numpy as jnp
from jax import lax
from jax.experimental import pallas as pl
from jax.experimental.pallas import tpu as pltpu
from jax.experimental.pallas import tpu_sc as plsc

F32 = jnp.float32
BF16 = jnp.bfloat16

S = 2048
D = 2048
NDEV = 8
EPS = 1e-6
NEG = -1e30
HEAD_DIM = 128
ROT_DIM = 32
ROPE_THETA = 500000.0
PATTERNS = ((128, 1), (512, 4), (2048, 16))
BLK = 128
EVEN_IN = 12288
ODD_IN = 6144
HALF = 1024
CONV_K = 31
HALO = 32
TR = 256
SUB = 32

ADAM_LR = 0.001
ADAM_B1 = 0.9
ADAM_B2 = 0.999
ADAM_EPS = 1e-08
ADAM_WD = 0.01
ADAM_STEP = 10

VMEM_BIG = 56 * 1024 * 1024
MESH = pl.DeviceIdType.MESH

NN = (((1,), (0,)), ((), ()))
NT = (((1,), (1,)), ((), ()))
TN = (((0,), (0,)), ((), ()))


def _dot(a, b, dn=NN):
    return lax.dot_general(a, b, dn, preferred_element_type=F32)


def _sigmoid(x):
    return 1.0 / (1.0 + jnp.exp(-x))


def _silu_and_grad(x):
    sg = _sigmoid(x)
    return x * sg, sg * (1.0 + x * (1.0 - sg))


def _params(sem, vmem=None):
    return pltpu.CompilerParams(dimension_semantics=sem, vmem_limit_bytes=vmem)


ANY_SPEC = pl.BlockSpec(memory_space=pl.ANY)


def _matmul(a, b, *, dn, grid, a_spec, b_spec, o_spec, out_shape, out_dtype, acc_shape, name, dep=None):
    nk = grid[2]
    deps = [] if dep is None else list(dep) if isinstance(dep, (tuple, list)) else [dep]

    def body(a_ref, b_ref, *rest):
        o_ref, acc = rest[len(deps)], rest[len(deps) + 1:]
        if nk == 1:
            o_ref[...] = _dot(a_ref[...], b_ref[...], dn).astype(o_ref.dtype)
            return
        acc_ref = acc[0]
        k = pl.program_id(2)

        @pl.when(k == 0)
        def _():
            acc_ref[...] = jnp.zeros_like(acc_ref)

        acc_ref[...] += _dot(a_ref[...], b_ref[...], dn)

        @pl.when(k == nk - 1)
        def _():
            o_ref[...] = acc_ref[...].astype(o_ref.dtype)

    return pl.pallas_call(
        body, grid=grid, in_specs=[a_spec, b_spec] + [ANY_SPEC] * len(deps), out_specs=o_spec,
        out_shape=jax.ShapeDtypeStruct(out_shape, out_dtype),
        scratch_shapes=[] if nk == 1 else [pltpu.VMEM(acc_shape, F32)],
        compiler_params=_params(("parallel", "parallel", "arbitrary"), VMEM_BIG), name=name,
    )(a, b, *deps)


TM = 2048


def _mm_in(h, wg, name):
    nb = wg.shape[2]
    tn = 512 if nb % 512 == 0 else nb
    per = nb // tn
    return _matmul(
        h, wg, dn=NN, grid=(S // TM, NDEV * per, 1),
        a_spec=pl.BlockSpec((TM, D), lambda i, j, k: (i, 0)),
        b_spec=pl.BlockSpec((None, D, tn), lambda i, j, k: (j // per, 0, j % per)),
        o_spec=pl.BlockSpec((TM, tn), lambda i, j, k: (i, j)),
        out_shape=(S, NDEV * nb), out_dtype=F32, acc_shape=(TM, tn), name=name)


def _mm_in_dx(dz, wg, name, dep=None):
    nb = wg.shape[2]
    return _matmul(
        dz, wg, dn=NT, grid=(S // TM, D // 1024, NDEV),
        a_spec=pl.BlockSpec((TM, nb), lambda i, j, k: (i, k)),
        b_spec=pl.BlockSpec((None, 1024, nb), lambda i, j, k: (k, j, 0)),
        o_spec=pl.BlockSpec((TM, 1024), lambda i, j, k: (i, j)),
        out_shape=(S, D), out_dtype=F32, acc_shape=(TM, 1024), name=name, dep=dep)


def _mm_in_dw(h, dz, nb, name, dep=None):
    tn = 512 if nb % 512 == 0 else nb
    per = nb // tn
    return _matmul(
        h, dz, dn=TN, grid=(D // TM, NDEV * per, 1),
        a_spec=pl.BlockSpec((S, TM), lambda i, j, k: (0, i)),
        b_spec=pl.BlockSpec((S, tn), lambda i, j, k: (0, j)),
        o_spec=pl.BlockSpec((None, TM, tn), lambda i, j, k: (j // per, i, j % per)),
        out_shape=(NDEV, D, nb), out_dtype=BF16, acc_shape=(TM, tn), name=name, dep=dep)


def _mm_out(yc, w, name, dep=None):
    return _matmul(
        yc, w, dn=NN, grid=(S // TM, D // 512, 1),
        a_spec=pl.BlockSpec((TM, 2048), lambda i, j, k: (i, 0)),
        b_spec=pl.BlockSpec((2048, 512), lambda i, j, k: (0, j)),
        o_spec=pl.BlockSpec((TM, 512), lambda i, j, k: (i, j)),
        out_shape=(S, D), out_dtype=F32, acc_shape=(TM, 512), name=name, dep=dep)


def _mm_out_dx(dy, w, name, dep=None):
    return _matmul(
        dy, w, dn=NT, grid=(S // TM, 2048 // 512, 1),
        a_spec=pl.BlockSpec((TM, D), lambda i, j, k: (i, 0)),
        b_spec=pl.BlockSpec((512, D), lambda i, j, k: (j, 0)),
        o_spec=pl.BlockSpec((TM, 512), lambda i, j, k: (i, j)),
        out_shape=(S, 2048), out_dtype=F32, acc_shape=(TM, 512), name=name, dep=dep)


def _mm_out_dw(yc, dy, name):
    return _matmul(
        yc, dy, dn=TN, grid=(2048 // TM, D // 512, 1),
        a_spec=pl.BlockSpec((S, TM), lambda i, j, k: (0, i)),
        b_spec=pl.BlockSpec((S, 512), lambda i, j, k: (0, j)),
        o_spec=pl.BlockSpec((TM, 512), lambda i, j, k: (i, j)),
        out_shape=(2048, D), out_dtype=BF16, acc_shape=(TM, 512), name=name)


def _row_spec(w=D):
    return pl.BlockSpec((TR, w), lambda i: (i, 0))


def _vec_spec(w=D):
    return pl.BlockSpec((1, w), lambda i: (0, 0))


def _rms_stats(x):
    r = lax.rsqrt(jnp.mean(x * x, axis=-1, keepdims=True) + EPS)
    return x * r, r


def _rms_bwd(dn, xhat, r, g):
    dxh = dn * g
    return r * (dxh - xhat * jnp.mean(dxh * xhat, axis=-1, keepdims=True))


def _acc_rows(ref, val, i):
    s = jnp.sum(val, axis=0, keepdims=True)

    @pl.when(i == 0)
    def _():
        ref[...] = s

    @pl.when(i > 0)
    def _():
        ref[...] += s


def _pre0_fwd(x, g, dep=None):
    deps = [] if dep is None else [dep]

    def body(x_ref, g_ref, *rest):
        xhat, _ = _rms_stats(x_ref[...])
        rest[-1][...] = (xhat * g_ref[...]).astype(BF16)

    return pl.pallas_call(
        body, grid=(S // TR,), in_specs=[_row_spec(), _vec_spec()] + [ANY_SPEC] * len(deps), out_specs=_row_spec(),
        out_shape=jax.ShapeDtypeStruct((S, D), BF16), compiler_params=_params(("parallel",)), name="pre0_fwd",
    )(x, g, *deps)


def _post0_fwd(x, y0, g_post, g_pre1):
    def body(x_ref, y_ref, gp_ref, g1_ref, x1_ref, h1_ref):
        yhat, _ = _rms_stats(y_ref[...])
        x1 = x_ref[...] + yhat * gp_ref[...]
        x1_ref[...] = x1
        xhat, _ = _rms_stats(x1)
        h1_ref[...] = (xhat * g1_ref[...]).astype(BF16)

    return pl.pallas_call(
        body, grid=(S // TR,), in_specs=[_row_spec(), _row_spec(), _vec_spec(), _vec_spec()],
        out_specs=[_row_spec(), _row_spec()],
        out_shape=[jax.ShapeDtypeStruct((S, D), F32), jax.ShapeDtypeStruct((S, D), BF16)],
        compiler_params=_params(("parallel",)), name="post0_fwd",
    )(x, y0, g_post, g_pre1)


def _post1_bwd(y1, x1, target, g_post):
    def body(y_ref, x1_ref, t_ref, g_ref, loss_ref, dx2_ref, dy_ref, dg_ref):
        i = pl.program_id(0)
        yhat, r = _rms_stats(y_ref[...])
        g = g_ref[...]
        err = x1_ref[...] + yhat * g - t_ref[...]
        part = jnp.sum(jnp.sum(err * err, axis=-1, keepdims=True), axis=0, keepdims=True) * (0.5 / D)
        _acc_rows(loss_ref, jnp.broadcast_to(part, (1, 128)), i)
        dx2 = err * (1.0 / D)
        dx2_ref[...] = dx2
        _acc_rows(dg_ref, dx2 * yhat, i)
        dy_ref[...] = _rms_bwd(dx2, yhat, r, g).astype(BF16)

    return pl.pallas_call(
        body, grid=(S // TR,), in_specs=[_row_spec(), _row_spec(), _row_spec(), _vec_spec()],
        out_specs=[_vec_spec(128), _row_spec(), _row_spec(), _vec_spec()],
        out_shape=[jax.ShapeDtypeStruct((1, 128), F32), jax.ShapeDtypeStruct((S, D), F32),
                   jax.ShapeDtypeStruct((S, D), BF16), jax.ShapeDtypeStruct((1, D), F32)],
        compiler_params=_params(("arbitrary",)), name="post1_bwd",
    )(y1, x1, target, g_post)


def _mid_bwd(dx2, dh1, x1, y0, g_pre1, g_post0):
    def body(dx2_ref, dh_ref, x1_ref, y_ref, g1_ref, gp_ref, dx1_ref, dy_ref, dg1_ref, dgp_ref):
        i = pl.program_id(0)
        xhat, r1 = _rms_stats(x1_ref[...])
        dh = dh_ref[...]
        _acc_rows(dg1_ref, dh * xhat, i)
        dx1 = dx2_ref[...] + _rms_bwd(dh, xhat, r1, g1_ref[...])
        dx1_ref[...] = dx1
        yhat, r0 = _rms_stats(y_ref[...])
        _acc_rows(dgp_ref, dx1 * yhat, i)
        dy_ref[...] = _rms_bwd(dx1, yhat, r0, gp_ref[...]).astype(BF16)

    return pl.pallas_call(
        body, grid=(S // TR,),
        in_specs=[_row_spec(), _row_spec(), _row_spec(), _row_spec(), _vec_spec(), _vec_spec()],
        out_specs=[_row_spec(), _row_spec(), _vec_spec(), _vec_spec()],
        out_shape=[jax.ShapeDtypeStruct((S, D), F32), jax.ShapeDtypeStruct((S, D), BF16),
                   jax.ShapeDtypeStruct((1, D), F32), jax.ShapeDtypeStruct((1, D), F32)],
        compiler_params=_params(("arbitrary",)), name="mid_bwd",
    )(dx2, dh1, x1, y0, g_pre1, g_post0)


def _pre0_bwd(dx1, dh0, x, g):
    def body(dx1_ref, dh_ref, x_ref, g_ref, gx_ref, dg_ref):
        i = pl.program_id(0)
        xhat, r = _rms_stats(x_ref[...])
        dh = dh_ref[...]
        _acc_rows(dg_ref, dh * xhat, i)
        gx_ref[...] = dx1_ref[...] + _rms_bwd(dh, xhat, r, g_ref[...])

    return pl.pallas_call(
        body, grid=(S // TR,), in_specs=[_row_spec(), _row_spec(), _row_spec(), _vec_spec()],
        out_specs=[_row_spec(), _vec_spec()],
        out_shape=[jax.ShapeDtypeStruct((S, D), F32), jax.ShapeDtypeStruct((1, D), F32)],
        compiler_params=_params(("arbitrary",)), name="pre0_bwd",
    )(dx1, dh0, x, g)


POOL_CH = 256


def _pool_apply(a, w, transpose):
    n = a.shape[0]
    row = lax.broadcasted_iota(jnp.int32, a.shape, 0)
    cnt = jnp.minimum(row + 1, w).astype(F32)
    s = a / cnt if transpose else a
    for k in (1, 2, 4, 8):
        if transpose:
            sh = jnp.where(row < n - k, pltpu.roll(s, n - k, 0), 0.0)
        else:
            sh = jnp.where(row >= k, pltpu.roll(s, k, 0), 0.0)
        s = jnp.where(w > k, s + sh, s)
    return s - a if transpose else s / cnt - a


def _pool_fwd(z0, pool_w, pool_scale):
    def body(a_ref, gate_ref, w_ref, sc_ref, out_ref):
        win = jnp.left_shift(2, pl.program_id(0))
        pooled = _pool_apply(a_ref[...], win, False)
        mixed = _dot(pooled.astype(BF16), w_ref[...])
        gate = gate_ref[...]
        out_ref[...] = (mixed * sc_ref[...] * (gate * _sigmoid(gate))).astype(BF16)

    return pl.pallas_call(
        body, grid=(4,),
        in_specs=[pl.BlockSpec((S, POOL_CH), lambda g: (0, g)), pl.BlockSpec((S, POOL_CH), lambda g: (0, 4 + g)),
                  pl.BlockSpec((None, POOL_CH, POOL_CH), lambda g: (g, 0, 0)),
                  pl.BlockSpec((1, POOL_CH), lambda g: (0, g))],
        out_specs=pl.BlockSpec((S, POOL_CH), lambda g: (0, g)),
        out_shape=jax.ShapeDtypeStruct((S, 2048), BF16),
        compiler_params=_params(("parallel",), VMEM_BIG), name="pool_fwd",
    )(z0, z0, pool_w, pool_scale)


def _pool_bwd(z0, dycat, pool_w, pool_scale):
    def body(a_ref, gate_ref, dy_ref, w_ref, sc_ref, da_ref, dgate_ref, dw_ref, dsc_ref):
        win = jnp.left_shift(2, pl.program_id(0))
        pooled = _pool_apply(a_ref[...], win, False).astype(BF16)
        w = w_ref[...]
        mixed = _dot(pooled, w)
        silu, dsilu = _silu_and_grad(gate_ref[...])
        dy = dy_ref[...]
        sc = sc_ref[...]
        dgate_ref[...] = (dy * (mixed * sc) * dsilu).astype(BF16)
        dms = dy * silu
        dsc_ref[...] = jnp.sum(dms * mixed, axis=0, keepdims=True)
        dmixed = (dms * sc).astype(BF16)
        dw_ref[...] = _dot(pooled, dmixed, TN)
        dpooled = _dot(dmixed, w, NT)
        da_ref[...] = _pool_apply(dpooled, win, True).astype(BF16)

    slab = lambda off: pl.BlockSpec((S, POOL_CH), lambda g: (0, off + g))
    return pl.pallas_call(
        body, grid=(4,),
        in_specs=[slab(0), slab(4), slab(0), pl.BlockSpec((None, POOL_CH, POOL_CH), lambda g: (g, 0, 0)),
                  pl.BlockSpec((1, POOL_CH), lambda g: (0, g))],
        out_specs=[slab(0), slab(0), pl.BlockSpec((None, POOL_CH, POOL_CH), lambda g: (g, 0, 0)),
                   pl.BlockSpec((1, POOL_CH), lambda g: (0, g))],
        out_shape=[jax.ShapeDtypeStruct((S, HALF), BF16), jax.ShapeDtypeStruct((S, HALF), BF16),
                   jax.ShapeDtypeStruct((4, POOL_CH, POOL_CH), F32), jax.ShapeDtypeStruct((1, HALF), F32)],
        compiler_params=_params(("parallel",), VMEM_BIG), name="pool_bwd",
    )(z0, z0, dycat, pool_w, pool_scale)


Q_COL, K_COL, V_COL, BG_COL = 2048 // 128, 5120 // 128, 8192 // 128, 11264 // 128
SCALE = HEAD_DIM ** -0.5


def _rope_tables():
    pos = jnp.arange(S, dtype=F32)
    inv_freq = jnp.power(ROPE_THETA, -jnp.arange(0, ROT_DIM, 2, dtype=F32) / ROT_DIM)
    ang = pos[:, None] * inv_freq[None, :]
    cos, sin = jnp.cos(ang), jnp.sin(ang)
    half = ROT_DIM // 2
    zeros = jnp.zeros((S, HEAD_DIM - ROT_DIM), F32)
    c = jnp.concatenate([cos, cos, jnp.ones((S, HEAD_DIM - ROT_DIM), F32)], axis=1)
    a = jnp.concatenate([-sin, jnp.zeros((S, half), F32), zeros], axis=1)
    b = jnp.concatenate([jnp.zeros((S, half), F32), sin, zeros], axis=1)
    return c, a, b


def _rope(t, c, a, b):
    half = ROT_DIM // 2
    return t * c + pltpu.roll(t, HEAD_DIM - half, 1) * a + pltpu.roll(t, half, 1) * b


def _rope_t(d, c, a, b):
    half = ROT_DIM // 2
    return d * c + pltpu.roll(d * a, half, 1) + pltpu.roll(d * b, HEAD_DIM - half, 1)


def _deinterleave(dst, src, dil, cast=None, dst_off=0):
    length = S // dil
    for r in range(dil):
        v = src[...] if dil == 1 else src[pl.ds(r, length, stride=dil), :]
        dst[dst_off + r * length:dst_off + (r + 1) * length, :] = v if cast is None else v.astype(cast)


def _interleave(dst, src, dil, src_off=0):
    length = S // dil
    for r in range(dil):
        if dil == 1:
            dst[...] = src[src_off:src_off + S, :]
        else:
            dst[pl.ds(r, length, stride=dil), :] = src[src_off + r * length:src_off + (r + 1) * length, :]


CU = 4
NUNITS = S // BLK
B_QK = (((2,), (2,)), ((0,), (0,)))
B_PV = (((2,), (1,)), ((0,), (0,)))
B_TN = (((1,), (1,)), ((0,), (0,)))


def _blocks(ref, first):
    return ref[first * BLK:(first + CU) * BLK, :].reshape(CU, BLK, HEAD_DIM)


def _chunk_scores(u0, nb, qd, kdp):
    q = _blocks(qd, u0)
    row = lax.broadcasted_iota(jnp.int32, (CU, BLK, BLK), 1)
    col = lax.broadcasted_iota(jnp.int32, (CU, BLK, BLK), 2)
    s_own = jnp.where(col <= row, _dot(q, _blocks(kdp, u0 + 1), B_QK) * SCALE, NEG)
    if nb == 1:
        return q, s_own, None
    unit = lax.broadcasted_iota(jnp.int32, (CU, BLK, BLK), 0) + u0
    s_prev = jnp.where((col >= row) & ((unit % nb) != 0), _dot(q, _blocks(kdp, u0), B_QK) * SCALE, NEG)
    return q, s_own, s_prev


def _qkv_prep(z0, tabs):
    def body(q_ref, k_ref, v_ref, c_ref, a_ref, b_ref, qo_ref, ko_ref, vo_ref, tmp):
        p = pl.program_id(1)
        for gi, (_, dil) in enumerate(PATTERNS):
            @pl.when(p == gi)
            def _(dil=dil):
                c, a, b = c_ref[...], a_ref[...], b_ref[...]
                tmp[...] = _rope(q_ref[...], c, a, b)
                _deinterleave(qo_ref, tmp, dil, BF16)
                tmp[...] = _rope(k_ref[...], c, a, b)
                _deinterleave(ko_ref, tmp, dil, BF16)
                _deinterleave(vo_ref, v_ref, dil, BF16)

    tab = pl.BlockSpec((S, HEAD_DIM), lambda h, p: (0, 0))
    out = pl.BlockSpec((S, HEAD_DIM), lambda h, p: (0, p * 8 + h))
    return pl.pallas_call(
        body, grid=(8, 3), in_specs=[_head_spec(Q_COL), _head_spec(K_COL), _head_spec(V_COL), tab, tab, tab],
        out_specs=[out, out, out], out_shape=[jax.ShapeDtypeStruct((S, 3072), BF16)] * 3,
        scratch_shapes=[pltpu.VMEM((S, HEAD_DIM), F32)],
        compiler_params=_params(("parallel", "arbitrary"), VMEM_BIG), name="qkv_prep",
    )(z0, z0, z0, *tabs)


def _pad_copy(dst, src):
    dst[0:BLK, :] = jnp.zeros((BLK, HEAD_DIM), dst.dtype)
    dst[BLK:BLK + S, :] = src[...]


def _attn_group_fwd(dil, qd, kd_ref, vd_ref, kdp, vdp, od, ld, og, lg):
    nb = S // dil // BLK
    _pad_copy(kdp, kd_ref)
    _pad_copy(vdp, vd_ref)
    for u0 in range(0, NUNITS, CU):
        _, s_own, s_prev = _chunk_scores(u0, nb, qd, kdp)
        m = jnp.max(s_own, axis=2, keepdims=True)
        if s_prev is not None:
            m = jnp.maximum(m, jnp.max(s_prev, axis=2, keepdims=True))
        p_own = jnp.exp(s_own - m)
        den = jnp.sum(p_own, axis=2, keepdims=True)
        acc = _dot(p_own.astype(BF16), _blocks(vdp, u0 + 1), B_PV)
        if s_prev is not None:
            p_prev = jnp.exp(s_prev - m)
            den = den + jnp.sum(p_prev, axis=2, keepdims=True)
            acc = acc + _dot(p_prev.astype(BF16), _blocks(vdp, u0), B_PV)
        rows = slice(u0 * BLK, (u0 + CU) * BLK)
        od[rows, :] = (acc / den).reshape(CU * BLK, HEAD_DIM)
        ld[rows, :] = jnp.broadcast_to(m + jnp.log(den), (CU, BLK, HEAD_DIM)).reshape(CU * BLK, HEAD_DIM)
    _interleave(og, od, dil)
    _interleave(lg, ld, dil)


def _group_weights(lgs):
    l0, l1, l2 = lgs[0][...], lgs[1][...], lgs[2][...]
    mx = jnp.maximum(l0, jnp.maximum(l1, l2))
    e0, e1, e2 = jnp.exp(l0 - mx), jnp.exp(l1 - mx), jnp.exp(l2 - mx)
    den = e0 + e1 + e2
    return e0 / den, e1 / den, e2 / den


def _head_spec(base, ngroups_axis=True):
    return pl.BlockSpec((S, HEAD_DIM), lambda h, p: (0, base + (p % 3) * 8 + h))


def _slab(dtype=F32, rows=S):
    return pltpu.VMEM((rows, HEAD_DIM), dtype)


def _attn_fwd(z0, qkv, ycat):
    def body(q_ref, k_ref, v_ref, gate_ref, ycat_ref, out_ref, og_ref, lg_ref,
             kdp, vdp, od, ld, og0, og1, og2, lg0, lg1, lg2):
        del ycat_ref
        p = pl.program_id(1)
        ogs, lgs = (og0, og1, og2), (lg0, lg1, lg2)
        for gi, (_, dil) in enumerate(PATTERNS):
            @pl.when(p == gi)
            def _(gi=gi, dil=dil):
                _attn_group_fwd(dil, q_ref, k_ref, v_ref, kdp, vdp, od, ld, ogs[gi], lgs[gi])
                og_ref[...] = ogs[gi][...]
                lg_ref[...] = lgs[gi][...]

        @pl.when(p == 2)
        def _():
            w0, w1, w2 = _group_weights(lgs)
            o = w0 * og0[...] + w1 * og1[...] + w2 * og2[...]
            gate = gate_ref[...]
            out_ref[...] = (o * (gate * _sigmoid(gate))).astype(BF16)

    grp = pl.BlockSpec((S, HEAD_DIM), lambda h, p: (0, p * 8 + h))
    return pl.pallas_call(
        body, grid=(8, 3),
        in_specs=[grp, grp, grp, pl.BlockSpec((S, HEAD_DIM), lambda h, p: (0, BG_COL + h)), ANY_SPEC],
        out_specs=[pl.BlockSpec((S, HEAD_DIM), lambda h, p: (0, 8 + h)), grp, grp],
        out_shape=[jax.ShapeDtypeStruct((S, 2048), BF16), jax.ShapeDtypeStruct((S, 3072), F32),
                   jax.ShapeDtypeStruct((S, 3072), F32)],
        scratch_shapes=[_slab(BF16, S + BLK), _slab(BF16, S + BLK)] + [_slab() for _ in range(8)],
        input_output_aliases={4: 0},
        compiler_params=_params(("parallel", "arbitrary"), VMEM_BIG), name="attn_fwd",
    )(*qkv, z0, ycat)


def _attn_bwd(z0, qkv, og, lg, dycat, tabs):
    def body(q_ref, k_ref, v_ref, gate_ref, dy_ref, c_ref, a_ref, b_ref,
             og0_ref, og1_ref, og2_ref, lg0_ref, lg1_ref, lg2_ref,
             dq_ref, dk_ref, dv_ref, dbg_ref,
             tmp, kd, vd, ld, dg0, dg1, dg2, cg0, cg1, cg2, dod, cd, dqd, dkd, dvd):
        p = pl.program_id(1)
        ogs, lgs, dgs, cgs = (og0_ref, og1_ref, og2_ref), (lg0_ref, lg1_ref, lg2_ref), (dg0, dg1, dg2), (cg0, cg1, cg2)

        @pl.when(p == 0)
        def _():
            w = _group_weights(lgs)
            o = w[0] * ogs[0][...] + w[1] * ogs[1][...] + w[2] * ogs[2][...]
            silu, dsilu = _silu_and_grad(gate_ref[...])
            dy = dy_ref[...]
            dbg_ref[...] = (dy * o * dsilu).astype(BF16)
            do = dy * silu
            dwbar = jnp.sum(do * o, axis=1, keepdims=True)
            for gi in range(3):
                dgs[gi][...] = w[gi] * do
                cgs[gi][...] = -w[gi] * dwbar

        for gi, (_, dil) in enumerate(PATTERNS):
            @pl.when(p == 1 + gi)
            def _(gi=gi, dil=dil):
                nb = S // dil // BLK
                qd = q_ref
                c, a, b = c_ref[...], a_ref[...], b_ref[...]
                _pad_copy(kd, k_ref)
                _pad_copy(vd, v_ref)
                _deinterleave(dod, dgs[gi], dil, BF16)
                _deinterleave(ld, lgs[gi], dil)
                _deinterleave(cd, cgs[gi], dil)
                dkd[...] = jnp.zeros_like(dkd)
                dvd[...] = jnp.zeros_like(dvd)
                flat = lambda t: t.reshape(CU * BLK, HEAD_DIM)
                for u0 in range(0, NUNITS, CU):
                    q, s_own, s_prev = _chunk_scores(u0, nb, qd, kd)
                    lse, cv, do = _blocks(ld, u0), _blocks(cd, u0), _blocks(dod, u0)
                    own = slice((u0 + 1) * BLK, (u0 + 1 + CU) * BLK)
                    p_own = jnp.exp(s_own - lse)
                    ds_own = (p_own * (_dot(do, _blocks(vd, u0 + 1), B_QK) + cv) * SCALE).astype(BF16)
                    dq = _dot(ds_own, _blocks(kd, u0 + 1), B_PV)
                    dkd[own, :] += flat(_dot(ds_own, q, B_TN))
                    dvd[own, :] += flat(_dot(p_own.astype(BF16), do, B_TN))
                    if s_prev is not None:
                        prev = slice(u0 * BLK, (u0 + CU) * BLK)
                        p_prev = jnp.exp(s_prev - lse)
                        ds_prev = (p_prev * (_dot(do, _blocks(vd, u0), B_QK) + cv) * SCALE).astype(BF16)
                        dq = dq + _dot(ds_prev, _blocks(kd, u0), B_PV)
                        dkd[prev, :] += flat(_dot(ds_prev, q, B_TN))
                        dvd[prev, :] += flat(_dot(p_prev.astype(BF16), do, B_TN))
                    dqd[u0 * BLK:(u0 + CU) * BLK, :] = flat(dq)
                _interleave(tmp, dqd, dil)
                dq_ref[...] = _rope_t(tmp[...], c, a, b).astype(BF16)
                _interleave(tmp, dkd, dil, BLK)
                dk_ref[...] = _rope_t(tmp[...], c, a, b).astype(BF16)
                _interleave(tmp, dvd, dil, BLK)
                dv_ref[...] = tmp[...].astype(BF16)

    tab = pl.BlockSpec((S, HEAD_DIM), lambda h, p: (0, 0))
    hspec = lambda base: pl.BlockSpec((S, HEAD_DIM), lambda h, p: (0, base + h))
    gspec = pl.BlockSpec((S, HEAD_DIM), lambda h, p: (0, jnp.maximum(p - 1, 0) * 8 + h))
    return pl.pallas_call(
        body, grid=(8, 4),
        in_specs=[gspec, gspec, gspec, hspec(BG_COL), hspec(8), tab, tab, tab,
                  hspec(0), hspec(8), hspec(16), hspec(0), hspec(8), hspec(16)],
        out_specs=[gspec, gspec, gspec, hspec(0)],
        out_shape=[jax.ShapeDtypeStruct((S, 3072), BF16)] * 3 + [jax.ShapeDtypeStruct((S, HALF), BF16)],
        scratch_shapes=[_slab(), _slab(BF16, S + BLK), _slab(BF16, S + BLK), _slab()] + [_slab() for _ in range(6)]
                       + [_slab(BF16), _slab(), _slab(), _slab(F32, S + BLK), _slab(F32, S + BLK)],
        compiler_params=_params(("parallel", "arbitrary"), VMEM_BIG), name="attn_bwd",
    )(*qkv, z0, dycat, *tabs, og, og, og, lg, lg, lg)


SGU_CH = 256
NCHUNK = TR // 128


def _ln_stats(x):
    mu = jnp.mean(x, axis=-1, keepdims=True)
    xc = x - mu
    r = lax.rsqrt(jnp.mean(xc * xc, axis=-1, keepdims=True) + EPS)
    return xc * r, r


def _ln_bwd(dy, xhat, r, g):
    dxh = dy * g
    return r * (dxh - jnp.mean(dxh, axis=-1, keepdims=True) - xhat * jnp.mean(dxh * xhat, axis=-1, keepdims=True))


def _tril_bf16(w):
    row = lax.broadcasted_iota(jnp.int32, w.shape, 0)
    col = lax.broadcasted_iota(jnp.int32, w.shape, 1)
    return jnp.where(row >= col, w, 0.0).astype(BF16)


def _sgu_gate(vn_s, s_s, w_ref, bb_ref):
    for h in range(4):
        wm = _tril_bf16(w_ref[h])
        bias = bb_ref[h]
        for ch in range(NCHUNK):
            rows, cols = slice(ch * 128, (ch + 1) * 128), slice(h * SGU_CH, (h + 1) * SGU_CH)
            s_s[rows, cols] = _dot(wm, vn_s[rows, cols]) + jnp.concatenate([bias, bias], axis=1)


WIN = HALO + TR
SUBL = 8


def _shifted_copies(dst, src):
    dst[0] = src[...]
    for b in range(1, SUBL):
        dst[b, 0:WIN - SUBL, :] = src[pl.ds(b, WIN - SUBL), :]


def _rows_at(copies, off, n):
    return copies[off % SUBL, pl.ds(off - off % SUBL, n), :]


def _conv_fwd(i, dval_ref, dglu_ref, hval_ref, hglu_ref, cw_ref, cb_ref, xw, xr, dcs):
    halo = hval_ref[...] * _sigmoid(hglu_ref[...])
    xw[0:HALO, :] = jnp.where(i > 0, halo, 0.0)
    xw[HALO:HALO + TR, :] = dval_ref[...] * _sigmoid(dglu_ref[...])
    _shifted_copies(xr, xw)
    for rb in range(TR // SUB):
        acc = jnp.broadcast_to(cb_ref[...], (SUB, HALF))
        for k in range(CONV_K):
            acc = acc + cw_ref[k:k + 1, :] * _rows_at(xr, rb * SUB + HALO - (CONV_K - 1) + k, SUB)
        dcs[rb * SUB:(rb + 1) * SUB, :] = acc


def _odd_in_specs():
    col = lambda j: pl.BlockSpec((TR, HALF), lambda i, *_: (i, j))
    prev = lambda j: pl.BlockSpec((HALO, HALF), lambda i, *_: (jnp.maximum(i * (TR // HALO) - 1, 0), j))
    return [col(0), col(1), col(2), col(3), col(4), col(5), prev(3), prev(4)]


def _full_spec(shape):
    return pl.BlockSpec(shape, lambda i, *_: (0,) * len(shape))


def _odd_fwd(z1, sgu_g, sgu_b, sgu_w, sgu_bb, conv_w, conv_b, cn_g, cn_b):
    def body(u_ref, v_ref, cg_ref, dval_ref, dglu_ref, dgate_ref, hval_ref, hglu_ref,
             g_ref, b_ref, w_ref, bb_ref, cw_ref, cb_ref, cng_ref, cnb_ref, out_ref, vn_s, s_s, xw, dcs, xr):
        i = pl.program_id(0)
        vhat, _ = _ln_stats(v_ref[...])
        vn_s[...] = (vhat * g_ref[...] + b_ref[...]).astype(BF16)
        _sgu_gate(vn_s, s_s, w_ref, bb_ref)
        cg = cg_ref[...]
        out_ref[:, 0:HALF] = (u_ref[...] * s_s[...] * (cg * _sigmoid(cg))).astype(BF16)
        _conv_fwd(i, dval_ref, dglu_ref, hval_ref, hglu_ref, cw_ref, cb_ref, xw, xr, dcs)
        dhat, _ = _ln_stats(dcs[...])
        dn = dhat * cng_ref[...] + cnb_ref[...]
        dgate = dgate_ref[...]
        out_ref[:, HALF:2 * HALF] = ((dn * _sigmoid(dn)) * (dgate * _sigmoid(dgate))).astype(BF16)

    vec = _full_spec((1, HALF))
    return pl.pallas_call(
        body, grid=(S // TR,),
        in_specs=_odd_in_specs() + [vec, vec, _full_spec((4, 128, 128)), _full_spec((4, 128, 128)),
                                    _full_spec((HALO, HALF)), vec, vec, vec],
        out_specs=pl.BlockSpec((TR, 2048), lambda i: (i, 0)),
        out_shape=jax.ShapeDtypeStruct((S, 2048), BF16),
        scratch_shapes=[pltpu.VMEM((TR, HALF), BF16), pltpu.VMEM((TR, HALF), F32),
                        pltpu.VMEM((WIN, HALF), F32), pltpu.VMEM((TR, HALF), F32), pltpu.VMEM((SUBL, WIN, HALF), F32)],
        compiler_params=_params(("parallel",), VMEM_BIG), name="odd_fwd",
    )(z1, z1, z1, z1, z1, z1, z1, z1, sgu_g, sgu_b, sgu_w, sgu_bb, conv_w, conv_b, cn_g, cn_b)


def _odd_bwd_a(z1, dycat, sgu_g, sgu_b, sgu_w, sgu_bb, conv_w, conv_b, cn_g, cn_b):
    def body(u_ref, v_ref, cg_ref, dval_ref, dglu_ref, dgate_ref, hval_ref, hglu_ref, dy_ref,
             g_ref, b_ref, w_ref, bb_ref, cw_ref, cb_ref, cng_ref, cnb_ref,
             dz_ref, ddc_ref, dw_ref, dbb_ref, dg_ref, db_ref, dcng_ref, dcnb_ref, dcb_ref,
             vn_s, s_s, xw, dcs, ds_s, dvn_s, xr):
        i = pl.program_id(0)
        vhat, rv = _ln_stats(v_ref[...])
        g = g_ref[...]
        vn_s[...] = (vhat * g + b_ref[...]).astype(BF16)
        _sgu_gate(vn_s, s_s, w_ref, bb_ref)
        silu_c, dsilu_c = _silu_and_grad(cg_ref[...])
        dyc = dy_ref[:, 0:HALF]
        u = u_ref[...]
        s = s_s[...]
        dz_ref[:, 0:HALF] = (dyc * s * silu_c).astype(BF16)
        dz_ref[:, 2 * HALF:3 * HALF] = (dyc * u * s * dsilu_c).astype(BF16)
        ds_s[...] = dyc * u * silu_c

        @pl.when(i == 0)
        def _():
            dw_ref[...] = jnp.zeros_like(dw_ref)
            dbb_ref[...] = jnp.zeros_like(dbb_ref)

        tril = lax.broadcasted_iota(jnp.int32, (128, 128), 0) >= lax.broadcasted_iota(jnp.int32, (128, 128), 1)
        for h in range(4):
            wm = _tril_bf16(w_ref[h])
            for ch in range(NCHUNK):
                rows, cols = slice(ch * 128, (ch + 1) * 128), slice(h * SGU_CH, (h + 1) * SGU_CH)
                ds = ds_s[rows, cols]
                dsb = ds.astype(BF16)
                dw_ref[h] += jnp.where(tril, _dot(dsb, vn_s[rows, cols], NT), 0.0)
                dbb_ref[h] += jnp.broadcast_to(jnp.sum(ds, axis=1, keepdims=True), (128, 128))
                dvn_s[rows, cols] = _dot(wm, dsb, TN)
        dvn = dvn_s[...]
        _acc_rows(dg_ref, dvn * vhat, i)
        _acc_rows(db_ref, dvn, i)
        dz_ref[:, HALF:2 * HALF] = _ln_bwd(dvn, vhat, rv, g).astype(BF16)

        _conv_fwd(i, dval_ref, dglu_ref, hval_ref, hglu_ref, cw_ref, cb_ref, xw, xr, dcs)
        dhat, rd = _ln_stats(dcs[...])
        cng = cng_ref[...]
        silu_n, dsilu_n = _silu_and_grad(dhat * cng + cnb_ref[...])
        silu_g, dsilu_g = _silu_and_grad(dgate_ref[...])
        dyd = dy_ref[:, HALF:2 * HALF]
        dz_ref[:, 5 * HALF:6 * HALF] = (dyd * silu_n * dsilu_g).astype(BF16)
        ddn = dyd * silu_g * dsilu_n
        _acc_rows(dcng_ref, ddn * dhat, i)
        _acc_rows(dcnb_ref, ddn, i)
        ddc = _ln_bwd(ddn, dhat, rd, cng)
        ddc_ref[...] = ddc
        _acc_rows(dcb_ref, ddc, i)

    vec = _full_spec((1, HALF))
    sq = _full_spec((4, 128, 128))
    return pl.pallas_call(
        body, grid=(S // TR,),
        in_specs=_odd_in_specs() + [pl.BlockSpec((TR, 2048), lambda i: (i, 0)),
                                    vec, vec, sq, sq, _full_spec((HALO, HALF)), vec, vec, vec],
        out_specs=[pl.BlockSpec((TR, ODD_IN), lambda i: (i, 0)), pl.BlockSpec((TR, HALF), lambda i: (i, 0)),
                   sq, sq, vec, vec, vec, vec, vec],
        out_shape=[jax.ShapeDtypeStruct((S, ODD_IN), BF16), jax.ShapeDtypeStruct((S, HALF), F32),
                   jax.ShapeDtypeStruct((4, 128, 128), F32), jax.ShapeDtypeStruct((4, 128, 128), F32)]
                  + [jax.ShapeDtypeStruct((1, HALF), F32)] * 5,
        scratch_shapes=[pltpu.VMEM((TR, HALF), BF16), pltpu.VMEM((TR, HALF), F32),
                        pltpu.VMEM((WIN, HALF), F32), pltpu.VMEM((TR, HALF), F32),
                        pltpu.VMEM((TR, HALF), F32), pltpu.VMEM((TR, HALF), F32), pltpu.VMEM((SUBL, WIN, HALF), F32)],
        compiler_params=_params(("arbitrary",), VMEM_BIG), name="odd_bwd_a",
    )(z1, z1, z1, z1, z1, z1, z1, z1, dycat, sgu_g, sgu_b, sgu_w, sgu_bb, conv_w, conv_b, cn_g, cn_b)


def _odd_bwd_b(z1, ddc, dz1, conv_w):
    nt = S // TR

    def body(dval_ref, dglu_ref, hval_ref, hglu_ref, ddc_ref, hddc_ref, cw_ref, dz_in_ref,
             dz_ref, dcw_ref, xw, dwin, dxs, xr, dr):
        del dz_in_ref
        i, j = pl.program_id(0), pl.program_id(1)
        sg = _sigmoid(dglu_ref[...])
        dval = dval_ref[...]

        @pl.when(j == 0)
        def _():
            halo = hval_ref[...] * _sigmoid(hglu_ref[...])
            xw[0:HALO, :] = jnp.where(i > 0, halo, 0.0)
            xw[HALO:HALO + TR, :] = dval * sg
            dwin[0:TR, :] = ddc_ref[...]
            dwin[TR:TR + HALO, :] = jnp.where(i < nt - 1, hddc_ref[...], 0.0)
            _shifted_copies(xr, xw)
            _shifted_copies(dr, dwin)

            @pl.when(i == 0)
            def _():
                dcw_ref[...] = jnp.zeros_like(dcw_ref)

            for rb in range(TR // SUB):
                acc = jnp.zeros((SUB, HALF), F32)
                for k in range(CONV_K):
                    acc = acc + cw_ref[k:k + 1, :] * _rows_at(dr, rb * SUB + (CONV_K - 1) - k, SUB)
                dxs[rb * SUB:(rb + 1) * SUB, :] = acc
            for k in range(CONV_K):
                acc = jnp.zeros((SUB, HALF), F32)
                for rb in range(TR // SUB):
                    acc = acc + dwin[rb * SUB:(rb + 1) * SUB, :] * _rows_at(xr, rb * SUB + HALO - (CONV_K - 1) + k, SUB)
                dcw_ref[k:k + 1, :] += jnp.sum(acc, axis=0, keepdims=True)
            dz_ref[...] = (dxs[...] * sg).astype(BF16)

        @pl.when(j == 1)
        def _():
            dz_ref[...] = (dxs[...] * dval * sg * (1.0 - sg)).astype(BF16)

    col = lambda c: pl.BlockSpec((TR, HALF), lambda i, j: (i, c))
    prev = lambda c: pl.BlockSpec((HALO, HALF), lambda i, j: (jnp.maximum(i * (TR // HALO) - 1, 0), c))
    nxt = pl.BlockSpec((HALO, HALF), lambda i, j: (jnp.minimum((i + 1) * (TR // HALO), S // HALO - 1), 0))
    return pl.pallas_call(
        body, grid=(nt, 2),
        in_specs=[col(3), col(4), prev(3), prev(4), pl.BlockSpec((TR, HALF), lambda i, j: (i, 0)), nxt,
                  _full_spec((HALO, HALF)), pl.BlockSpec(memory_space=pl.ANY)],
        out_specs=[pl.BlockSpec((TR, HALF), lambda i, j: (i, 3 + j)), _full_spec((HALO, HALF))],
        out_shape=[jax.ShapeDtypeStruct((S, ODD_IN), BF16), jax.ShapeDtypeStruct((HALO, HALF), F32)],
        scratch_shapes=[pltpu.VMEM((WIN, HALF), F32), pltpu.VMEM((WIN, HALF), F32), pltpu.VMEM((TR, HALF), F32),
                        pltpu.VMEM((SUBL, WIN, HALF), F32), pltpu.VMEM((SUBL, WIN, HALF), F32)],
        input_output_aliases={7: 0},
        compiler_params=_params(("arbitrary", "arbitrary"), VMEM_BIG), name="odd_bwd_b",
    )(z1, z1, z1, z1, ddc, ddc, conv_w, dz1)


def _cast_bf16(w, name):
    r, c = w.shape
    tr = min(r, 256)
    def body(i_ref, o_ref):
        o_ref[...] = i_ref[...].astype(BF16)

    return pl.pallas_call(
        body, grid=(r // tr,), in_specs=[pl.BlockSpec((tr, c), lambda i: (i, 0))],
        out_specs=pl.BlockSpec((tr, c), lambda i: (i, 0)), out_shape=jax.ShapeDtypeStruct((r, c), BF16),
        compiler_params=_params(("parallel",)), name=name,
    )(w)


def _adamw(w, g, m, v):
    m = ADAM_B1 * m + (1.0 - ADAM_B1) * g
    v = ADAM_B2 * v + (1.0 - ADAM_B2) * (g * g)
    m_hat = m / (1.0 - ADAM_B1 ** ADAM_STEP)
    v_hat = v / (1.0 - ADAM_B2 ** ADAM_STEP)
    delta = -ADAM_LR * (m_hat / (jnp.sqrt(v_hat) + ADAM_EPS) + ADAM_WD * w)
    return delta, m, v


def _adam_reduce(parts, w, m, v, name, dep=None):
    r, c = w.shape
    tr = min(r, 128)
    deps = [] if dep is None else [dep]
    nparts = parts.shape[0]

    def body(p_ref, w_ref, m_ref, v_ref, *rest):
        g_ref, d_ref, nm_ref, nv_ref = rest[len(deps):]
        g = p_ref[0].astype(F32)
        for d in range(1, nparts):
            g = g + p_ref[d].astype(F32)
        g_ref[...] = g
        d_ref[...], nm_ref[...], nv_ref[...] = _adamw(w_ref[...], g, m_ref[...], v_ref[...])

    spec = pl.BlockSpec((tr, c), lambda i: (i, 0))
    return pl.pallas_call(
        body, grid=(r // tr,),
        in_specs=[pl.BlockSpec((nparts, tr, c), lambda i: (0, i, 0)), spec, spec, spec] + [ANY_SPEC] * len(deps),
        out_specs=[spec] * 4, out_shape=[jax.ShapeDtypeStruct((r, c), F32)] * 4,
        compiler_params=_params(("parallel",), VMEM_BIG), name=name,
    )(parts, w, m, v, *deps)


def _sum_parts(parts, name, dep=None):
    r = parts.shape[1]
    tr = 8
    for cand in (512, 256, 128, 64, 32, 16, 8):
        if r % cand == 0:
            tr = cand
            break
    deps = [] if dep is None else [dep]

    def body(p_ref, *rest):
        g = p_ref[0]
        for d in range(1, NDEV):
            g = g + p_ref[d]
        rest[-1][...] = g

    return pl.pallas_call(
        body, grid=(r // tr,), in_specs=[pl.BlockSpec((NDEV, tr, 128), lambda i: (0, i, 0))] + [ANY_SPEC] * len(deps),
        out_specs=pl.BlockSpec((tr, 128), lambda i: (i, 0)), out_shape=jax.ShapeDtypeStruct((r, 128), F32),
        compiler_params=_params(("parallel",)), name=name,
    )(parts, *deps)


def _adam_plain(w, g, m, v, name):
    r, c = w.shape

    def body(w_ref, g_ref, m_ref, v_ref, d_ref, nm_ref, nv_ref):
        d_ref[...], nm_ref[...], nv_ref[...] = _adamw(w_ref[...], g_ref[...], m_ref[...], v_ref[...])

    spec = pl.BlockSpec((r, c), lambda i: (0, 0))
    return pl.pallas_call(
        body, grid=(1,), in_specs=[spec] * 4, out_specs=[spec] * 3,
        out_shape=[jax.ShapeDtypeStruct((r, c), F32)] * 3,
        compiler_params=_params(("arbitrary",)), name=name,
    )(w, g, m, v)


MASKS = [(mx, my, mc) for mx in (0, 1) for my in (0, 1) for mc in (0, 1)][1:]


def _exchange(arrays, scatter, name):
    nt = len(arrays)
    out_shape = [jax.ShapeDtypeStruct(((NDEV,) + a.shape) if not scatter else a.shape, a.dtype) for a in arrays]

    def body(*refs):
        ins, outs = refs[:nt], refs[nt:2 * nt]
        send_sems, recv_sems, local_sems = refs[2 * nt:]
        x, y, c = lax.axis_index("x"), lax.axis_index("y"), lax.axis_index("c")
        me = 4 * x + 2 * y + c
        copies = []
        for t in range(nt):
            src_own = ins[t].at[me] if scatter else ins[t]
            loc = pltpu.make_async_copy(src_own, outs[t].at[me], local_sems.at[t])
            loc.start()
            copies.append(loc)
            for k, (mx, my, mc) in enumerate(MASKS):
                px, py, pc = (x + mx) % 2, (y + my) % 2, (c + mc) % 2
                peer = 4 * px + 2 * py + pc
                src = ins[t].at[peer] if scatter else ins[t]
                rc = pltpu.make_async_remote_copy(
                    src_ref=src, dst_ref=outs[t].at[me], send_sem=send_sems.at[t, k], recv_sem=recv_sems.at[t, k],
                    device_id=(px, py, pc), device_id_type=MESH)
                rc.start()
                copies.append(rc)
        for cp in copies:
            cp.wait()

    hbm = pl.BlockSpec(memory_space=pl.ANY)
    return pl.pallas_call(
        body, in_specs=[hbm] * nt, out_specs=[hbm] * nt, out_shape=out_shape,
        scratch_shapes=[pltpu.SemaphoreType.DMA((nt, 7)), pltpu.SemaphoreType.DMA((nt, 7)),
                        pltpu.SemaphoreType.DMA((nt,))],
        name=name,
    )(*arrays)


SEM_SPEC = pl.BlockSpec(memory_space=pltpu.SEMAPHORE)
EFFECT = pltpu.SideEffectType.DATAFLOW_SIDE_EFFECTING


def _direct_plan(scatter):
    def plan(x, y, c, srcs, lands):
        me = 4 * x + 2 * y + c
        local, remote = [], []
        for src, land in zip(srcs, lands):
            local.append((src.at[me] if scatter else src, land.at[me]))
            for mx, my, mc in MASKS:
                px, py, pc = (x + mx) % 2, (y + my) % 2, (c + mc) % 2
                blk = src.at[4 * px + 2 * py + pc] if scatter else src
                remote.append((blk, land.at[me], (px, py, pc)))
        return local, remote
    return plan


def _split_start(name, srcs, land_shapes, plan, n_local, n_remote, dep=None):
    ns, nl = len(srcs), len(land_shapes)
    deps = [] if dep is None else [dep]
    lands = [lax.empty(s.shape, s.dtype) for s in land_shapes]

    def body(*refs):
        ins, lz = refs[:ns], refs[ns:ns + nl]
        outs = refs[ns + nl + len(deps):]
        send_sems, recv_sems, token, local_sems = outs[0], outs[1], outs[2 + ns + nl], outs[3 + ns + nl]
        local, remote = plan(lax.axis_index("x"), lax.axis_index("y"), lax.axis_index("c"), ins, lz)
        own = [pltpu.make_async_copy(src, dst, local_sems.at[i]) for i, (src, dst) in enumerate(local)]
        for cp in own:
            cp.start()
        for cp in own:
            cp.wait()
        for k, (src, dst, peer) in enumerate(remote):
            pltpu.make_async_remote_copy(src_ref=src, dst_ref=dst, send_sem=send_sems.at[k], recv_sem=recv_sems.at[k],
                                         device_id=peer, device_id_type=MESH).start()
        token[...] = jnp.zeros_like(token)

    hbm = lambda a: pltpu.HBM(a.shape, a.dtype)
    outs = pl.pallas_call(
        body, name=name,
        out_shape=(pltpu.SemaphoreType.DMA((n_remote,)), pltpu.SemaphoreType.DMA((n_remote,)),
                   *[hbm(a) for a in srcs], *[hbm(a) for a in lands], jax.ShapeDtypeStruct((8, 128), F32)),
        in_specs=[ANY_SPEC] * (ns + nl + len(deps)),
        out_specs=(SEM_SPEC, SEM_SPEC, *[ANY_SPEC] * (ns + nl), pl.BlockSpec(memory_space=pltpu.VMEM)),
        scratch_shapes=[pltpu.SemaphoreType.DMA((n_local,))],
        input_output_aliases={i: 2 + i for i in range(ns + nl)},
        compiler_params=pltpu.CompilerParams(has_side_effects=EFFECT),
    )(*[pltpu.with_memory_space_constraint(a, pltpu.HBM) for a in srcs],
      *[pltpu.with_memory_space_constraint(a, pltpu.HBM) for a in lands], *deps)
    return dict(sems=outs[:2], srcs=outs[2:2 + ns], lands=outs[2 + ns:2 + ns + nl], token=outs[-1],
                plan=plan, n_remote=n_remote)


def _split_wait(name, handle, after):
    srcs, lands, plan = handle["srcs"], handle["lands"], handle["plan"]
    ns, nl = len(srcs), len(lands)

    def body(*refs):
        ins, lz = refs[:ns], refs[ns:ns + nl]
        send_sems, recv_sems = refs[ns + nl], refs[ns + nl + 1]
        _, remote = plan(lax.axis_index("x"), lax.axis_index("y"), lax.axis_index("c"), ins, lz)
        for k, (src, dst, peer) in enumerate(remote):
            cp = pltpu.make_async_remote_copy(src_ref=src, dst_ref=dst, send_sem=send_sems.at[k],
                                              recv_sem=recv_sems.at[k], device_id=peer, device_id_type=MESH)
            cp.wait_send()
            cp.wait_recv()

    hbm = lambda a: pltpu.HBM(a.shape, a.dtype)
    outs = pl.pallas_call(
        body, name=name, out_shape=(*[hbm(a) for a in srcs], *[hbm(a) for a in lands]),
        in_specs=[ANY_SPEC] * (ns + nl) + [SEM_SPEC, SEM_SPEC, ANY_SPEC], out_specs=tuple([ANY_SPEC] * (ns + nl)),
        input_output_aliases={i: i for i in range(ns + nl)},
        compiler_params=pltpu.CompilerParams(has_side_effects=EFFECT),
    )(*srcs, *lands, *handle["sems"], after)
    return list(outs[ns:])


def _sc_exchange(name, collective_id, arrays, scatter):
    nt = len(arrays)
    out_type = [jax.ShapeDtypeStruct(a.shape if scatter else (NDEV,) + a.shape, a.dtype) for a in arrays]

    def body(*refs):
        ins, outs = refs[:nt], refs[nt:2 * nt]
        send_sems, recv_sems, local_sems = refs[2 * nt:3 * nt], refs[3 * nt:4 * nt], refs[4 * nt:5 * nt]
        x, y, c = lax.axis_index("x"), lax.axis_index("y"), lax.axis_index("c")
        peers = [(mx + x - 2 * mx * x, my + y - 2 * my * y, mc + c - 2 * mc * c) for mx, my, mc in MASKS]
        barrier = pltpu.get_barrier_semaphore()
        for peer in peers:
            pl.semaphore_signal(barrier, inc=1, device_id=peer, device_id_type=MESH)
        pl.semaphore_wait(barrier, len(peers))
        me = 4 * x + 2 * y + c
        own = []
        for t in range(nt):
            cp = pltpu.make_async_copy(ins[t].at[me] if scatter else ins[t], outs[t].at[me], local_sems[t])
            cp.start()
            own.append(cp)
            for px, py, pc in peers:
                src = ins[t].at[4 * px + 2 * py + pc] if scatter else ins[t]
                pltpu.make_async_remote_copy(src_ref=src, dst_ref=outs[t].at[me], send_sem=send_sems[t],
                                             recv_sem=recv_sems[t], device_id=(px, py, pc), device_id_type=MESH).start()
        for t in range(nt):
            own[t].wait()
            seven = outs[t].at[pl.ds(0, NDEV - 1)]
            drain = pltpu.make_async_remote_copy(src_ref=seven, dst_ref=seven, send_sem=send_sems[t],
                                                 recv_sem=recv_sems[t], device_id=(x, y, c), device_id_type=MESH)
            drain.wait_send()
            drain.wait_recv()

    return pl.kernel(
        body, out_type=out_type, mesh=plsc.ScalarSubcoreMesh(axis_name="sequencer", num_cores=1),
        scratch_types=[pltpu.SemaphoreType.DMA] * (3 * nt),
        compiler_params=pltpu.CompilerParams(collective_id=collective_id), name=name,
    )(*arrays)


def _sc_gather_two_level(name, collective_id, arrays):
    nt = len(arrays)
    out_type = [jax.ShapeDtypeStruct((NDEV,) + a.shape, a.dtype) for a in arrays]

    def body(*refs):
        ins, outs = refs[:nt], refs[nt:2 * nt]
        sems = refs[2 * nt:]
        send_sems, sib_sems, local_sems = sems[:nt], sems[nt:2 * nt], sems[2 * nt:3 * nt]
        ici_sems = [sems[3 * nt + 3 * t:3 * nt + 3 * t + 3] for t in range(nt)]
        x, y, c = lax.axis_index("x"), lax.axis_index("y"), lax.axis_index("c")
        sibling = (x, y, 1 - c)
        chips = [(1 - x, y), (x, 1 - y), (1 - x, 1 - y)]
        barrier = pltpu.get_barrier_semaphore()
        for peer in [sibling] + [(cx, cy, c) for cx, cy in chips]:
            pl.semaphore_signal(barrier, inc=1, device_id=peer, device_id_type=MESH)
        pl.semaphore_wait(barrier, 4)
        me = 4 * x + 2 * y + c

        def push(t, src, slot, recv_sem, to):
            pltpu.make_async_remote_copy(src_ref=src, dst_ref=outs[t].at[slot], send_sem=send_sems[t],
                                         recv_sem=recv_sem, device_id=to, device_id_type=MESH).start()

        own = []
        for t in range(nt):
            cp = pltpu.make_async_copy(ins[t], outs[t].at[me], local_sems[t])
            cp.start()
            own.append(cp)
            for j, (cx, cy) in enumerate(chips):
                push(t, ins[t], me, ici_sems[t][j], (cx, cy, c))
            push(t, ins[t], me, sib_sems[t], sibling)
        for t in range(nt):
            for j, (cx, cy) in enumerate(chips):
                slot = 4 * cx + 2 * cy + c
                landed = outs[t].at[slot]
                pltpu.make_async_remote_copy(src_ref=landed, dst_ref=landed, send_sem=send_sems[t],
                                             recv_sem=ici_sems[t][j], device_id=(cx, cy, c),
                                             device_id_type=MESH).wait_recv()
                push(t, landed, slot, sib_sems[t], sibling)
        for t in range(nt):
            own[t].wait()
            four, seven = outs[t].at[pl.ds(0, 4)], outs[t].at[pl.ds(0, 7)]
            pltpu.make_async_remote_copy(src_ref=four, dst_ref=four, send_sem=send_sems[t], recv_sem=sib_sems[t],
                                         device_id=sibling, device_id_type=MESH).wait_recv()
            pltpu.make_async_remote_copy(src_ref=seven, dst_ref=seven, send_sem=send_sems[t], recv_sem=sib_sems[t],
                                         device_id=sibling, device_id_type=MESH).wait_send()

    return pl.kernel(
        body, out_type=out_type, mesh=plsc.ScalarSubcoreMesh(axis_name="sequencer", num_cores=1),
        scratch_types=[pltpu.SemaphoreType.DMA] * (6 * nt),
        compiler_params=pltpu.CompilerParams(collective_id=collective_id), name=name,
    )(*arrays)


def _sc_sibling_exchange(name, collective_id, src, out_shape, pieces):
    def body(src_ref, out_ref, send_sem, recv_sem):
        x, y, c = lax.axis_index("x"), lax.axis_index("y"), lax.axis_index("c")
        sibling = (x, y, 1 - c)
        barrier = pltpu.get_barrier_semaphore()
        pl.semaphore_signal(barrier, inc=1, device_id=sibling, device_id_type=MESH)
        pl.semaphore_wait(barrier, 1)
        for piece, lands in pieces(c, src_ref, out_ref):
            pltpu.make_async_remote_copy(src_ref=piece, dst_ref=lands, send_sem=send_sem, recv_sem=recv_sem,
                                         device_id=sibling, device_id_type=MESH).start()
        drain = pltpu.make_async_remote_copy(src_ref=out_ref, dst_ref=out_ref, send_sem=send_sem, recv_sem=recv_sem,
                                             device_id=sibling, device_id_type=MESH)
        drain.wait_send()
        drain.wait_recv()

    return pl.kernel(
        body, out_type=jax.ShapeDtypeStruct(out_shape, src.dtype),
        mesh=plsc.ScalarSubcoreMesh(axis_name="sequencer", num_cores=1), scratch_types=[pltpu.SemaphoreType.DMA] * 2,
        compiler_params=pltpu.CompilerParams(collective_id=collective_id), name=name,
    )(src)


def _sc_chip_scatter(name, collective_id, q):
    def body(q_ref, out_ref, send_sem, recv_sem, local_sem):
        x, y, c = lax.axis_index("x"), lax.axis_index("y"), lax.axis_index("c")
        chips = [(1 - x, y), (x, 1 - y), (1 - x, 1 - y)]
        barrier = pltpu.get_barrier_semaphore()
        for cx, cy in chips:
            pl.semaphore_signal(barrier, inc=1, device_id=(cx, cy, c), device_id_type=MESH)
        pl.semaphore_wait(barrier, 3)
        mine = 2 * x + y
        own = pltpu.make_async_copy(q_ref.at[mine], out_ref.at[mine], local_sem)
        own.start()
        for cx, cy in chips:
            pltpu.make_async_remote_copy(src_ref=q_ref.at[2 * cx + cy], dst_ref=out_ref.at[mine], send_sem=send_sem,
                                         recv_sem=recv_sem, device_id=(cx, cy, c), device_id_type=MESH).start()
        own.wait()
        three = out_ref.at[pl.ds(0, 3)]
        drain = pltpu.make_async_remote_copy(src_ref=three, dst_ref=three, send_sem=send_sem, recv_sem=recv_sem,
                                             device_id=(x, y, c), device_id_type=MESH)
        drain.wait_send()
        drain.wait_recv()

    return pl.kernel(
        body, out_type=jax.ShapeDtypeStruct(q.shape, q.dtype),
        mesh=plsc.ScalarSubcoreMesh(axis_name="sequencer", num_cores=1), scratch_types=[pltpu.SemaphoreType.DMA] * 3,
        compiler_params=pltpu.CompilerParams(collective_id=collective_id), name=name,
    )(q)


def _mm_pair_dw(h_own, dz, h_sib, dz_sib, nb, name, dep=None):
    tn = 512 if nb % 512 == 0 else nb
    per = nb // tn
    o_spec = pl.BlockSpec((None, D, tn), lambda i, j, k: (j // per, 0, j % per))
    part = _matmul(
        h_own, dz, dn=TN, grid=(1, 4 * per, 1),
        a_spec=pl.BlockSpec((S, D), lambda i, j, k: (0, 0)),
        b_spec=pl.BlockSpec((S, tn), lambda i, j, k: (0, (2 * (j // per) + lax.axis_index("c")) * per + j % per)),
        o_spec=o_spec, out_shape=(4, D, nb), out_dtype=F32, acc_shape=(D, tn), name=name + "_own", dep=dep)

    def body(a_ref, b_ref, p_ref, o_ref):
        o_ref[...] = (p_ref[...] + _dot(a_ref[...], b_ref[...], TN)).astype(BF16)

    return pl.pallas_call(
        body, grid=(1, 4 * per, 1),
        in_specs=[pl.BlockSpec((S, D), lambda i, j, k: (0, 0)), pl.BlockSpec((S, tn), lambda i, j, k: (0, j)), o_spec],
        out_specs=o_spec, out_shape=jax.ShapeDtypeStruct((4, D, nb), BF16),
        compiler_params=_params(("parallel", "parallel", "arbitrary"), VMEM_BIG), name=name + "_sibling",
    )(h_sib, dz_sib, part)


SMALL = {
    "e_pre_norm": ((2048,), None), "e_pool_w": ((4, 256, 256), 1), "e_pool_scale": ((1024,), None),
    "e_post_norm": ((2048,), None), "o_pre_norm": ((2048,), 0), "o_sgu_norm_g": ((1024,), 0),
    "o_sgu_norm_b": ((1024,), 0), "o_sgu_w": ((4, 128, 128), None), "o_sgu_b": ((4, 128), None),
    "o_conv_w": ((31, 1024), 1), "o_conv_b": ((1024,), 0), "o_conv_norm_g": ((1024,), 0),
    "o_conv_norm_b": ((1024,), 0), "o_post_norm": ((2048,), 0),
}
SMALL_SHARDED = [n for n, (_, ax) in SMALL.items() if ax is not None]


def _shard_shape(name):
    shape, ax = SMALL[name]
    if ax is None:
        return shape
    return tuple(s // NDEV if i == ax else s for i, s in enumerate(shape))


def _pack(arrs, row_multiple=1):
    flat = jnp.concatenate([a.reshape(-1) for a in arrs])
    pad = -flat.shape[0] % (128 * row_multiple)
    return jnp.concatenate([flat, jnp.zeros((pad,), F32)]).reshape(-1, 128)


def _unpack(buf, shapes):
    flat = buf.reshape(-1)
    out, off = [], 0
    for shp in shapes:
        n = int(np.prod(shp))
        out.append(flat[off:off + n].reshape(shp))
        off += n
    return out


def _take_shard(full, name, me):
    shape, ax = SMALL[name]
    if ax is None:
        return full
    n = shape[ax] // NDEV
    return lax.dynamic_slice_in_dim(full, me * n, n, axis=ax)


BIG = ("e_w_in", "e_w_out", "o_w_in", "o_w_out")
WEIGHTS = ["e_pre_norm", "e_w_in", "e_pool_w", "e_pool_scale", "e_w_out", "e_post_norm", "o_pre_norm", "o_w_in",
           "o_sgu_norm_g", "o_sgu_norm_b", "o_sgu_w", "o_sgu_b", "o_conv_w", "o_conv_b", "o_conv_norm_g",
           "o_conv_norm_b", "o_w_out", "o_post_norm"]


def kernel(x, e_pre_norm, e_w_in, e_pool_w, e_pool_scale, e_w_out, e_post_norm, o_pre_norm, o_w_in, o_sgu_norm_g, o_sgu_norm_b, o_sgu_w, o_sgu_b, o_conv_w, o_conv_b, o_conv_norm_g, o_conv_norm_b, o_w_out, o_post_norm, loss_target, m_e_pre_norm, m_e_w_in, m_e_pool_w, m_e_pool_scale, m_e_w_out, m_e_post_norm, m_o_pre_norm, m_o_w_in, m_o_sgu_norm_g, m_o_sgu_norm_b, m_o_sgu_w, m_o_sgu_b, m_o_conv_w, m_o_conv_b, m_o_conv_norm_g, m_o_conv_norm_b, m_o_w_out, m_o_post_norm, v_e_pre_norm, v_e_w_in, v_e_pool_w, v_e_pool_scale, v_e_w_out, v_e_post_norm, v_o_pre_norm, v_o_w_in, v_o_sgu_norm_g, v_o_sgu_norm_b, v_o_sgu_w, v_o_sgu_b, v_o_conv_w, v_o_conv_b, v_o_conv_norm_g, v_o_conv_norm_b, v_o_w_out, v_o_post_norm):
    given = dict(locals())
    w = {n: given[n][0] for n in WEIGHTS}
    m = {n: given["m_" + n][0] for n in WEIGHTS}
    v = {n: given["v_" + n][0] for n in WEIGHTS}
    me = 4 * lax.axis_index("x") + 2 * lax.axis_index("y") + lax.axis_index("c")
    x, target = x[0], loss_target[0]
    row = lambda a: a.reshape(1, -1)

    bf = {n: _cast_bf16(w[n], "cast_" + n) for n in BIG}
    wg_e_in, small_rows = _sc_gather_two_level("gather_a", 0, [bf["e_w_in"], _pack([w[n] for n in SMALL_SHARDED])])
    wg_e_out, wg_o_in, wg_o_out = _sc_gather_two_level("gather_b", 1, [bf["e_w_out"], bf["o_w_in"], bf["o_w_out"]])
    h0 = _pre0_fwd(x, row(w["e_pre_norm"]))
    h0_sib = _sc_sibling_exchange("swap_h0", 8, h0, h0.shape, lambda c, src, out: [(src, out)])
    p = {n: w[n] for n in SMALL if SMALL[n][1] is None}
    small_rows = small_rows.reshape(NDEV, -1)
    off = 0
    for n in SMALL_SHARDED:
        shp, ax = _shard_shape(n), SMALL[n][1]
        cnt = int(np.prod(shp))
        blk = small_rows[:, off:off + cnt].reshape((NDEV,) + shp)
        p[n] = jnp.moveaxis(blk, 0, ax).reshape(SMALL[n][0])
        off += cnt
    tabs = _rope_tables()
    pool_w_bf = p["e_pool_w"].astype(BF16)
    sgu_bb = jnp.broadcast_to(p["o_sgu_b"][:, :, None], (4, 128, 128))
    conv_w = jnp.concatenate([p["o_conv_w"], jnp.zeros((HALO - CONV_K, HALF), F32)], axis=0)
    odd_p = (row(p["o_sgu_norm_g"]), row(p["o_sgu_norm_b"]), p["o_sgu_w"], sgu_bb, conv_w,
             row(p["o_conv_b"]), row(p["o_conv_norm_g"]), row(p["o_conv_norm_b"]))

    z0 = _mm_in(h0, wg_e_in, "mm_z0")
    ycat0 = _pool_fwd(z0, pool_w_bf, row(p["e_pool_scale"]))
    qkv = _qkv_prep(z0, tabs)
    ycat0, og, lg = _attn_fwd(z0, qkv, ycat0)
    w_out_e, w_out_o = wg_e_out.reshape(2048, D), wg_o_out.reshape(2048, D)
    y0 = _mm_out(ycat0, w_out_e, "mm_y0", h0_sib)
    x1, h1 = _post0_fwd(x, y0, row(p["e_post_norm"]), row(p["o_pre_norm"]))
    z1 = _mm_in(h1, wg_o_in, "mm_z1")
    ycat1 = _odd_fwd(z1, *odd_p)
    y1 = _mm_out(ycat1, w_out_o, "mm_y1")

    g = {}
    loss, dx2, dy1, g["o_post_norm"] = _post1_bwd(y1, x1, target, row(p["o_post_norm"]))
    loss = lax.psum(loss[0, 0], ("x", "y", "c"))
    parts = {}
    dw = _mm_out_dw(ycat1, dy1, "mm_dwout1").reshape(NDEV, 256, D)
    parts["o_w_out"], = _sc_exchange("scatter_o_w_out", 2, [dw], True)
    dycat1 = _mm_out_dx(dy1, w_out_o, "mm_dycat1", (dw, loss.reshape(1, 1)))
    dz1, ddc, g["o_sgu_w"], d_sgu_bb, g["o_sgu_norm_g"], g["o_sgu_norm_b"], g["o_conv_norm_g"], \
        g["o_conv_norm_b"], g["o_conv_b"] = _odd_bwd_a(z1, dycat1, *odd_p)
    dz1, d_conv_w = _odd_bwd_b(z1, ddc, dz1, conv_w)
    g["o_sgu_b"] = d_sgu_bb[:, :, 0]
    g["o_conv_w"] = d_conv_w[:CONV_K]
    grads, deltas, new_m, new_v = {}, {}, {}, {}

    def adam(n, dep):
        grads[n], deltas[n], new_m[n], new_v[n] = _adam_reduce(parts[n], w[n], m[n], v[n], "adam_" + n, dep)
        return new_v[n]

    pin = adam("o_w_out", d_conv_w)
    dw = _mm_in_dw(h1, dz1, ODD_IN // NDEV, "mm_dwin1", pin)
    parts["o_w_in"], = _sc_exchange("scatter_o_w_in", 3, [dw], True)
    dh1 = _mm_in_dx(dz1, wg_o_in, "mm_dh1", dw)
    dx1, dy0, g["o_pre_norm"], g["e_post_norm"] = _mid_bwd(dx2, dh1, x1, y0, row(p["o_pre_norm"]),
                                                           row(p["e_post_norm"]))
    dw = _mm_out_dw(ycat0, dy0, "mm_dwout0").reshape(NDEV, 256, D)
    parts["e_w_out"], = _sc_exchange("scatter_e_w_out", 4, [dw], True)
    dycat0 = _mm_out_dx(dy0, w_out_e, "mm_dycat0", dw)
    da_in, da_gate, g["e_pool_w"], g["e_pool_scale"] = _pool_bwd(z0, dycat0, pool_w_bf, row(p["e_pool_scale"]))
    dq, dk, dv, dbg = _attn_bwd(z0, qkv, og, lg, dycat0, tabs)
    dz0 = jnp.concatenate([da_in, da_gate, dq, dk, dv, dbg], axis=1)
    late = [n for n in SMALL if n != "e_pre_norm"]
    pin = adam("e_w_out", adam("o_w_in", dbg))
    nb = EVEN_IN // NDEV
    dz0_sib = _sc_sibling_exchange(
        "swap_dz0", 9, dz0, (S, 4 * nb),
        lambda c, src, out: [(src.at[:, pl.ds((2 * j + 1 - c) * nb, nb)], out.at[:, pl.ds(j * nb, nb)])
                             for j in range(4)])
    dw = _mm_pair_dw(h0, dz0, h0_sib, dz0_sib, nb, "mm_dwin0", pin)
    parts["e_w_in"] = _sc_chip_scatter("scatter_e_w_in", 5, dw)
    recv_small, = _sc_exchange("gather_small_grads", 6, [_pack([g[n].reshape(SMALL[n][0]) for n in late], 512)], False)
    dh0 = _mm_in_dx(dz0, wg_e_in, "mm_dh0", dw)
    grad_x, g["e_pre_norm"] = _pre0_bwd(dx1, dh0, x, row(p["e_pre_norm"]))
    last, = _sc_exchange("gather_e_pre_norm_grad", 7, [g["e_pre_norm"].reshape(16, 128)], False)

    g_small = dict(zip(late, _unpack(_sum_parts(recv_small, "sum_small_grads"), [SMALL[n][0] for n in late])))
    pin = adam("e_w_in", grad_x)
    g_small["e_pre_norm"] = _sum_parts(last, "sum_e_pre_norm_grad", pin).reshape(2048)
    for n in SMALL:
        grads[n] = _take_shard(g_small[n], n, me)
    names = list(SMALL)
    shapes = [_shard_shape(n) for n in names]
    d_pack, m_pack, v_pack = _adam_plain(_pack([w[n] for n in names]), _pack([grads[n] for n in names]),
                                         _pack([m[n] for n in names]), _pack([v[n] for n in names]), "adam_small")
    for n, d_, m_, v_ in zip(names, _unpack(d_pack, shapes), _unpack(m_pack, shapes), _unpack(v_pack, shapes)):
        deltas[n], new_m[n], new_v[n] = d_, m_, v_

    lead = lambda a: a[None]
    return (loss, grad_x[None], *[lead(grads[n]) for n in WEIGHTS], *[lead(deltas[n]) for n in WEIGHTS],
            *[lead(new_m[n]) for n in WEIGHTS], *[lead(new_v[n]) for n in WEIGHTS])
```

```python
import functools

import numpy as np
import jax
import jax.numpy as jnp
from jax import lax
from jax.experimental import pallas as pl
from jax.experimental.pallas import tpu as pltpu
from jax.experimental.pallas import tpu_sc as plsc

F32 = jnp.float32
BF16 = jnp.bfloat16

S = 2048
D = 2048
NDEV = 8
EPS = 1e-6
NEG = -1e30
HEAD_DIM = 128
ROT_DIM = 32
ROPE_THETA = 500000.0
PATTERNS = ((128, 1), (512, 4), (2048, 16))
BLK = 128
EVEN_IN = 12288
ODD_IN = 6144
HALF = 1024
CONV_K = 31
HALO = 32
TR = 256
SUB = 32

ADAM_LR = 0.001
ADAM_B1 = 0.9
ADAM_B2 = 0.999
ADAM_EPS = 1e-08
ADAM_WD = 0.01
ADAM_STEP = 10

VMEM_BIG = 56 * 1024 * 1024
MESH = pl.DeviceIdType.MESH

NN = (((1,), (0,)), ((), ()))
NT = (((1,), (1,)), ((), ()))
TN = (((0,), (0,)), ((), ()))


def _dot(a, b, dn=NN):
    return lax.dot_general(a, b, dn, preferred_element_type=F32)


def _sigmoid(x):
    return 1.0 / (1.0 + jnp.exp(-x))


def _silu_and_grad(x):
    sg = _sigmoid(x)
    return x * sg, sg * (1.0 + x * (1.0 - sg))


def _params(sem, vmem=None):
    return pltpu.CompilerParams(dimension_semantics=sem, vmem_limit_bytes=vmem)


ANY_SPEC = pl.BlockSpec(memory_space=pl.ANY)


def _matmul(a, b, *, dn, grid, a_spec, b_spec, o_spec, out_shape, out_dtype, acc_shape, name, dep=None):
    nk = grid[2]
    deps = [] if dep is None else list(dep) if isinstance(dep, (tuple, list)) else [dep]

    def body(a_ref, b_ref, *rest):
        o_ref, acc = rest[len(deps)], rest[len(deps) + 1:]
        if nk == 1:
            o_ref[...] = _dot(a_ref[...], b_ref[...], dn).astype(o_ref.dtype)
            return
        acc_ref = acc[0]
        k = pl.program_id(2)

        @pl.when(k == 0)
        def _():
            acc_ref[...] = jnp.zeros_like(acc_ref)

        acc_ref[...] += _dot(a_ref[...], b_ref[...], dn)

        @pl.when(k == nk - 1)
        def _():
            o_ref[...] = acc_ref[...].astype(o_ref.dtype)

    return pl.pallas_call(
        body, grid=grid, in_specs=[a_spec, b_spec] + [ANY_SPEC] * len(deps), out_specs=o_spec,
        out_shape=jax.ShapeDtypeStruct(out_shape, out_dtype),
        scratch_shapes=[] if nk == 1 else [pltpu.VMEM(acc_shape, F32)],
        compiler_params=_params(("parallel", "parallel", "arbitrary"), VMEM_BIG), name=name,
    )(a, b, *deps)


TM = 2048


def _mm_in(h, wg, name):
    nb = wg.shape[2]
    tn = 512 if nb % 512 == 0 else nb
    per = nb // tn
    return _matmul(
        h, wg, dn=NN, grid=(S // TM, NDEV * per, 1),
        a_spec=pl.BlockSpec((TM, D), lambda i, j, k: (i, 0)),
        b_spec=pl.BlockSpec((None, D, tn), lambda i, j, k: (j // per, 0, j % per)),
        o_spec=pl.BlockSpec((TM, tn), lambda i, j, k: (i, j)),
        out_shape=(S, NDEV * nb), out_dtype=F32, acc_shape=(TM, tn), name=name)


def _mm_in_dx(dz, wg, name, dep=None):
    nb = wg.shape[2]
    return _matmul(
        dz, wg, dn=NT, grid=(S // TM, D // 1024, NDEV),
        a_spec=pl.BlockSpec((TM, nb), lambda i, j, k: (i, k)),
        b_spec=pl.BlockSpec((None, 1024, nb), lambda i, j, k: (k, j, 0)),
        o_spec=pl.BlockSpec((TM, 1024), lambda i, j, k: (i, j)),
        out_shape=(S, D), out_dtype=F32, acc_shape=(TM, 1024), name=name, dep=dep)


def _mm_in_dw(h, dz, nb, name, dep=None):
    tn = 512 if nb % 512 == 0 else nb
    per = nb // tn
    return _matmul(
        h, dz, dn=TN, grid=(D // TM, NDEV * per, 1),
        a_spec=pl.BlockSpec((S, TM), lambda i, j, k: (0, i)),
        b_spec=pl.BlockSpec((S, tn), lambda i, j, k: (0, j)),
        o_spec=pl.BlockSpec((None, TM, tn), lambda i, j, k: (j // per, i, j % per)),
        out_shape=(NDEV, D, nb), out_dtype=BF16, acc_shape=(TM, tn), name=name, dep=dep)


def _mm_out(yc, w, name, dep=None):
    return _matmul(
        yc, w, dn=NN, grid=(S // TM, D // 512, 1),
        a_spec=pl.BlockSpec((TM, 2048), lambda i, j, k: (i, 0)),
        b_spec=pl.BlockSpec((2048, 512), lambda i, j, k: (0, j)),
        o_spec=pl.BlockSpec((TM, 512), lambda i, j, k: (i, j)),
        out_shape=(S, D), out_dtype=F32, acc_shape=(TM, 512), name=name, dep=dep)


def _mm_out_dx(dy, w, name, dep=None):
    return _matmul(
        dy, w, dn=NT, grid=(S // TM, 2048 // 512, 1),
        a_spec=pl.BlockSpec((TM, D), lambda i, j, k: (i, 0)),
        b_spec=pl.BlockSpec((512, D), lambda i, j, k: (j, 0)),
        o_spec=pl.BlockSpec((TM, 512), lambda i, j, k: (i, j)),
        out_shape=(S, 2048), out_dtype=F32, acc_shape=(TM, 512), name=name, dep=dep)


def _mm_out_dw(yc, dy, name):
    return _matmul(
        yc, dy, dn=TN, grid=(2048 // TM, D // 512, 1),
        a_spec=pl.BlockSpec((S, TM), lambda i, j, k: (0, i)),
        b_spec=pl.BlockSpec((S, 512), lambda i, j, k: (0, j)),
        o_spec=pl.BlockSpec((TM, 512), lambda i, j, k: (i, j)),
        out_shape=(2048, D), out_dtype=BF16, acc_shape=(TM, 512), name=name)


def _row_spec(w=D):
    return pl.BlockSpec((TR, w), lambda i: (i, 0))


def _vec_spec(w=D):
    return pl.BlockSpec((1, w), lambda i: (0, 0))


def _rms_stats(x):
    r = lax.rsqrt(jnp.mean(x * x, axis=-1, keepdims=True) + EPS)
    return x * r, r


def _rms_bwd(dn, xhat, r, g):
    dxh = dn * g
    return r * (dxh - xhat * jnp.mean(dxh * xhat, axis=-1, keepdims=True))


def _acc_rows(ref, val, i):
    s = jnp.sum(val, axis=0, keepdims=True)

    @pl.when(i == 0)
    def _():
        ref[...] = s

    @pl.when(i > 0)
    def _():
        ref[...] += s


def _pre0_fwd(x, g, dep=None):
    deps = [] if dep is None else [dep]

    def body(x_ref, g_ref, *rest):
        xhat, _ = _rms_stats(x_ref[...])
        rest[-1][...] = (xhat * g_ref[...]).astype(BF16)

    return pl.pallas_call(
        body, grid=(S // TR,), in_specs=[_row_spec(), _vec_spec()] + [ANY_SPEC] * len(deps), out_specs=_row_spec(),
        out_shape=jax.ShapeDtypeStruct((S, D), BF16), compiler_params=_params(("parallel",)), name="pre0_fwd",
    )(x, g, *deps)


def _post0_fwd(x, y0, g_post, g_pre1):
    def body(x_ref, y_ref, gp_ref, g1_ref, x1_ref, h1_ref):
        yhat, _ = _rms_stats(y_ref[...])
        x1 = x_ref[...] + yhat * gp_ref[...]
        x1_ref[...] = x1
        xhat, _ = _rms_stats(x1)
        h1_ref[...] = (xhat * g1_ref[...]).astype(BF16)

    return pl.pallas_call(
        body, grid=(S // TR,), in_specs=[_row_spec(), _row_spec(), _vec_spec(), _vec_spec()],
        out_specs=[_row_spec(), _row_spec()],
        out_shape=[jax.ShapeDtypeStruct((S, D), F32), jax.ShapeDtypeStruct((S, D), BF16)],
        compiler_params=_params(("parallel",)), name="post0_fwd",
    )(x, y0, g_post, g_pre1)


def _post1_bwd(y1, x1, target, g_post):
    def body(y_ref, x1_ref, t_ref, g_ref, loss_ref, dx2_ref, dy_ref, dg_ref):
        i = pl.program_id(0)
        yhat, r = _rms_stats(y_ref[...])
        g = g_ref[...]
        err = x1_ref[...] + yhat * g - t_ref[...]
        part = jnp.sum(jnp.sum(err * err, axis=-1, keepdims=True), axis=0, keepdims=True) * (0.5 / D)
        _acc_rows(loss_ref, jnp.broadcast_to(part, (1, 128)), i)
        dx2 = err * (1.0 / D)
        dx2_ref[...] = dx2
        _acc_rows(dg_ref, dx2 * yhat, i)
        dy_ref[...] = _rms_bwd(dx2, yhat, r, g).astype(BF16)

    return pl.pallas_call(
        body, grid=(S // TR,), in_specs=[_row_spec(), _row_spec(), _row_spec(), _vec_spec()],
        out_specs=[_vec_spec(128), _row_spec(), _row_spec(), _vec_spec()],
        out_shape=[jax.ShapeDtypeStruct((1, 128), F32), jax.ShapeDtypeStruct((S, D), F32),
                   jax.ShapeDtypeStruct((S, D), BF16), jax.ShapeDtypeStruct((1, D), F32)],
        compiler_params=_params(("arbitrary",)), name="post1_bwd",
    )(y1, x1, target, g_post)


def _mid_bwd(dx2, dh1, x1, y0, g_pre1, g_post0):
    def body(dx2_ref, dh_ref, x1_ref, y_ref, g1_ref, gp_ref, dx1_ref, dy_ref, dg1_ref, dgp_ref):
        i = pl.program_id(0)
        xhat, r1 = _rms_stats(x1_ref[...])
        dh = dh_ref[...]
        _acc_rows(dg1_ref, dh * xhat, i)
        dx1 = dx2_ref[...] + _rms_bwd(dh, xhat, r1, g1_ref[...])
        dx1_ref[...] = dx1
        yhat, r0 = _rms_stats(y_ref[...])
        _acc_rows(dgp_ref, dx1 * yhat, i)
        dy_ref[...] = _rms_bwd(dx1, yhat, r0, gp_ref[...]).astype(BF16)

    return pl.pallas_call(
        body, grid=(S // TR,),
        in_specs=[_row_spec(), _row_spec(), _row_spec(), _row_spec(), _vec_spec(), _vec_spec()],
        out_specs=[_row_spec(), _row_spec(), _vec_spec(), _vec_spec()],
        out_shape=[jax.ShapeDtypeStruct((S, D), F32), jax.ShapeDtypeStruct((S, D), BF16),
                   jax.ShapeDtypeStruct((1, D), F32), jax.ShapeDtypeStruct((1, D), F32)],
        compiler_params=_params(("arbitrary",)), name="mid_bwd",
    )(dx2, dh1, x1, y0, g_pre1, g_post0)


def _pre0_bwd(dx1, dh0, x, g):
    def body(dx1_ref, dh_ref, x_ref, g_ref, gx_ref, dg_ref):
        i = pl.program_id(0)
        xhat, r = _rms_stats(x_ref[...])
        dh = dh_ref[...]
        _acc_rows(dg_ref, dh * xhat, i)
        gx_ref[...] = dx1_ref[...] + _rms_bwd(dh, xhat, r, g_ref[...])

    return pl.pallas_call(
        body, grid=(S // TR,), in_specs=[_row_spec(), _row_spec(), _row_spec(), _vec_spec()],
        out_specs=[_row_spec(), _vec_spec()],
        out_shape=[jax.ShapeDtypeStruct((S, D), F32), jax.ShapeDtypeStruct((1, D), F32)],
        compiler_params=_params(("arbitrary",)), name="pre0_bwd",
    )(dx1, dh0, x, g)


POOL_CH = 256


def _pool_apply(a, w, transpose):
    n = a.shape[0]
    row = lax.broadcasted_iota(jnp.int32, a.shape, 0)
    cnt = jnp.minimum(row + 1, w).astype(F32)
    s = a / cnt if transpose else a
    for k in (1, 2, 4, 8):
        if transpose:
            sh = jnp.where(row < n - k, pltpu.roll(s, n - k, 0), 0.0)
        else:
            sh = jnp.where(row >= k, pltpu.roll(s, k, 0), 0.0)
        s = jnp.where(w > k, s + sh, s)
    return s - a if transpose else s / cnt - a


def _pool_fwd(z0, pool_w, pool_scale):
    def body(a_ref, gate_ref, w_ref, sc_ref, out_ref):
        win = jnp.left_shift(2, pl.program_id(0))
        pooled = _pool_apply(a_ref[...], win, False)
        mixed = _dot(pooled.astype(BF16), w_ref[...])
        gate = gate_ref[...]
        out_ref[...] = (mixed * sc_ref[...] * (gate * _sigmoid(gate))).astype(BF16)

    return pl.pallas_call(
        body, grid=(4,),
        in_specs=[pl.BlockSpec((S, POOL_CH), lambda g: (0, g)), pl.BlockSpec((S, POOL_CH), lambda g: (0, 4 + g)),
                  pl.BlockSpec((None, POOL_CH, POOL_CH), lambda g: (g, 0, 0)),
                  pl.BlockSpec((1, POOL_CH), lambda g: (0, g))],
        out_specs=pl.BlockSpec((S, POOL_CH), lambda g: (0, g)),
        out_shape=jax.ShapeDtypeStruct((S, 2048), BF16),
        compiler_params=_params(("parallel",), VMEM_BIG), name="pool_fwd",
    )(z0, z0, pool_w, pool_scale)


def _pool_bwd(z0, dycat, pool_w, pool_scale):
    def body(a_ref, gate_ref, dy_ref, w_ref, sc_ref, dz_ref, dw_ref, dsc_ref, dgate_s):
        emit = pl.program_id(1)

        @pl.when(emit == 0)
        def _():
            win = jnp.left_shift(2, pl.program_id(0))
            pooled = _pool_apply(a_ref[...], win, False).astype(BF16)
            w = w_ref[...]
            mixed = _dot(pooled, w)
            silu, dsilu = _silu_and_grad(gate_ref[...])
            dy = dy_ref[...]
            sc = sc_ref[...]
            dgate_s[...] = (dy * (mixed * sc) * dsilu).astype(BF16)
            dms = dy * silu
            dsc_ref[...] = jnp.sum(dms * mixed, axis=0, keepdims=True)
            dmixed = (dms * sc).astype(BF16)
            dw_ref[...] = _dot(pooled, dmixed, TN)
            dpooled = _dot(dmixed, w, NT)
            dz_ref[...] = _pool_apply(dpooled, win, True).astype(BF16)

        @pl.when(emit == 1)
        def _():
            dz_ref[...] = dgate_s[...]

    slab = lambda off: pl.BlockSpec((S, POOL_CH), lambda g, e: (0, off + g))
    return pl.pallas_call(
        body, grid=(4, 2),
        in_specs=[slab(0), slab(4), slab(0), pl.BlockSpec((None, POOL_CH, POOL_CH), lambda g, e: (g, 0, 0)),
                  pl.BlockSpec((1, POOL_CH), lambda g, e: (0, g))],
        out_specs=[pl.BlockSpec((S, POOL_CH), lambda g, e: (0, 4 * e + g)),
                   pl.BlockSpec((None, POOL_CH, POOL_CH), lambda g, e: (g, 0, 0)),
                   pl.BlockSpec((1, POOL_CH), lambda g, e: (0, g))],
        out_shape=[jax.ShapeDtypeStruct((S, EVEN_IN), BF16),
                   jax.ShapeDtypeStruct((4, POOL_CH, POOL_CH), F32), jax.ShapeDtypeStruct((1, HALF), F32)],
        scratch_shapes=[pltpu.VMEM((S, POOL_CH), BF16)],
        compiler_params=_params(("parallel", "arbitrary"), VMEM_BIG), name="pool_bwd",
    )(z0, z0, dycat, pool_w, pool_scale)


Q_COL, K_COL, V_COL, BG_COL = 2048 // 128, 5120 // 128, 8192 // 128, 11264 // 128
SCALE = HEAD_DIM ** -0.5


def _rope_tables():
    pos = jnp.arange(S, dtype=F32)
    inv_freq = jnp.power(ROPE_THETA, -jnp.arange(0, ROT_DIM, 2, dtype=F32) / ROT_DIM)
    ang = pos[:, None] * inv_freq[None, :]
    cos, sin = jnp.cos(ang), jnp.sin(ang)
    half = ROT_DIM // 2
    zeros = jnp.zeros((S, HEAD_DIM - ROT_DIM), F32)
    c = jnp.concatenate([cos, cos, jnp.ones((S, HEAD_DIM - ROT_DIM), F32)], axis=1)
    a = jnp.concatenate([-sin, jnp.zeros((S, half), F32), zeros], axis=1)
    b = jnp.concatenate([jnp.zeros((S, half), F32), sin, zeros], axis=1)
    return c, a, b


def _rope(t, c, a, b):
    half = ROT_DIM // 2
    return t * c + pltpu.roll(t, HEAD_DIM - half, 1) * a + pltpu.roll(t, half, 1) * b


def _rope_t(d, c, a, b):
    half = ROT_DIM // 2
    return d * c + pltpu.roll(d * a, half, 1) + pltpu.roll(d * b, HEAD_DIM - half, 1)


def _deinterleave(dst, src, dil, cast=None, dst_off=0):
    length = S // dil
    for r in range(dil):
        v = src[...] if dil == 1 else src[pl.ds(r, length, stride=dil), :]
        dst[dst_off + r * length:dst_off + (r + 1) * length, :] = v if cast is None else v.astype(cast)


def _interleave(dst, src, dil, src_off=0):
    length = S // dil
    for r in range(dil):
        if dil == 1:
            dst[...] = src[src_off:src_off + S, :]
        else:
            dst[pl.ds(r, length, stride=dil), :] = src[src_off + r * length:src_off + (r + 1) * length, :]


CU = 4
NUNITS = S // BLK
B_QK = (((2,), (2,)), ((0,), (0,)))
B_PV = (((2,), (1,)), ((0,), (0,)))
B_TN = (((1,), (1,)), ((0,), (0,)))


def _blocks(ref, first):
    return ref[first * BLK:(first + CU) * BLK, :].reshape(CU, BLK, HEAD_DIM)


def _chunk_scores(u0, nb, qd, kdp):
    q = _blocks(qd, u0)
    row = lax.broadcasted_iota(jnp.int32, (CU, BLK, BLK), 1)
    col = lax.broadcasted_iota(jnp.int32, (CU, BLK, BLK), 2)
    s_own = jnp.where(col <= row, _dot(q, _blocks(kdp, u0 + 1), B_QK) * SCALE, NEG)
    if nb == 1:
        return q, s_own, None
    unit = lax.broadcasted_iota(jnp.int32, (CU, BLK, BLK), 0) + u0
    s_prev = jnp.where((col >= row) & ((unit % nb) != 0), _dot(q, _blocks(kdp, u0), B_QK) * SCALE, NEG)
    return q, s_own, s_prev


def _qkv_prep(z0, tabs):
    def body(q_ref, k_ref, v_ref, c_ref, a_ref, b_ref, qo_ref, ko_ref, vo_ref, tmp):
        p = pl.program_id(1)
        for gi, (_, dil) in enumerate(PATTERNS):
            @pl.when(p == gi)
            def _(dil=dil):
                c, a, b = c_ref[...], a_ref[...], b_ref[...]
                tmp[...] = _rope(q_ref[...], c, a, b)
                _deinterleave(qo_ref, tmp, dil, BF16)
                tmp[...] = _rope(k_ref[...], c, a, b)
                _deinterleave(ko_ref, tmp, dil, BF16)
                _deinterleave(vo_ref, v_ref, dil, BF16)

    tab = pl.BlockSpec((S, HEAD_DIM), lambda h, p: (0, 0))
    out = pl.BlockSpec((S, HEAD_DIM), lambda h, p: (0, p * 8 + h))
    return pl.pallas_call(
        body, grid=(8, 3), in_specs=[_head_spec(Q_COL), _head_spec(K_COL), _head_spec(V_COL), tab, tab, tab],
        out_specs=[out, out, out], out_shape=[jax.ShapeDtypeStruct((S, 3072), BF16)] * 3,
        scratch_shapes=[pltpu.VMEM((S, HEAD_DIM), F32)],
        compiler_params=_params(("parallel", "arbitrary"), VMEM_BIG), name="qkv_prep",
    )(z0, z0, z0, *tabs)


def _pad_copy(dst, src):
    dst[0:BLK, :] = jnp.zeros((BLK, HEAD_DIM), dst.dtype)
    dst[BLK:BLK + S, :] = src[...]


def _attn_group_fwd(dil, qd, kd_ref, vd_ref, kdp, vdp, od, ld, og, lg):
    nb = S // dil // BLK
    _pad_copy(kdp, kd_ref)
    _pad_copy(vdp, vd_ref)
    for u0 in range(0, NUNITS, CU):
        _, s_own, s_prev = _chunk_scores(u0, nb, qd, kdp)
        m = jnp.max(s_own, axis=2, keepdims=True)
        if s_prev is not None:
            m = jnp.maximum(m, jnp.max(s_prev, axis=2, keepdims=True))
        p_own = jnp.exp(s_own - m)
        den = jnp.sum(p_own, axis=2, keepdims=True)
        acc = _dot(p_own.astype(BF16), _blocks(vdp, u0 + 1), B_PV)
        if s_prev is not None:
            p_prev = jnp.exp(s_prev - m)
            den = den + jnp.sum(p_prev, axis=2, keepdims=True)
            acc = acc + _dot(p_prev.astype(BF16), _blocks(vdp, u0), B_PV)
        rows = slice(u0 * BLK, (u0 + CU) * BLK)
        od[rows, :] = (acc / den).reshape(CU * BLK, HEAD_DIM)
        ld[rows, :] = jnp.broadcast_to(m + jnp.log(den), (CU, BLK, HEAD_DIM)).reshape(CU * BLK, HEAD_DIM)
    _interleave(og, od, dil)
    _interleave(lg, ld, dil)


def _group_weights(lgs):
    l0, l1, l2 = lgs[0][...], lgs[1][...], lgs[2][...]
    mx = jnp.maximum(l0, jnp.maximum(l1, l2))
    e0, e1, e2 = jnp.exp(l0 - mx), jnp.exp(l1 - mx), jnp.exp(l2 - mx)
    den = e0 + e1 + e2
    return e0 / den, e1 / den, e2 / den


def _head_spec(base, ngroups_axis=True):
    return pl.BlockSpec((S, HEAD_DIM), lambda h, p: (0, base + (p % 3) * 8 + h))


def _slab(dtype=F32, rows=S):
    return pltpu.VMEM((rows, HEAD_DIM), dtype)


def _attn_fwd(z0, qkv, ycat):
    def body(q_ref, k_ref, v_ref, gate_ref, ycat_ref, out_ref, og_ref, lg_ref,
             kdp, vdp, od, ld, og0, og1, og2, lg0, lg1, lg2):
        del ycat_ref
        p = pl.program_id(1)
        ogs, lgs = (og0, og1, og2), (lg0, lg1, lg2)
        for gi, (_, dil) in enumerate(PATTERNS):
            @pl.when(p == gi)
            def _(gi=gi, dil=dil):
                _attn_group_fwd(dil, q_ref, k_ref, v_ref, kdp, vdp, od, ld, ogs[gi], lgs[gi])
                og_ref[...] = ogs[gi][...]
                lg_ref[...] = lgs[gi][...]

        @pl.when(p == 2)
        def _():
            w0, w1, w2 = _group_weights(lgs)
            o = w0 * og0[...] + w1 * og1[...] + w2 * og2[...]
            gate = gate_ref[...]
            out_ref[...] = (o * (gate * _sigmoid(gate))).astype(BF16)

    grp = pl.BlockSpec((S, HEAD_DIM), lambda h, p: (0, p * 8 + h))
    return pl.pallas_call(
        body, grid=(8, 3),
        in_specs=[grp, grp, grp, pl.BlockSpec((S, HEAD_DIM), lambda h, p: (0, BG_COL + h)), ANY_SPEC],
        out_specs=[pl.BlockSpec((S, HEAD_DIM), lambda h, p: (0, 8 + h)), grp, grp],
        out_shape=[jax.ShapeDtypeStruct((S, 2048), BF16), jax.ShapeDtypeStruct((S, 3072), F32),
                   jax.ShapeDtypeStruct((S, 3072), F32)],
        scratch_shapes=[_slab(BF16, S + BLK), _slab(BF16, S + BLK)] + [_slab() for _ in range(8)],
        input_output_aliases={4: 0},
        compiler_params=_params(("parallel", "arbitrary"), VMEM_BIG), name="attn_fwd",
    )(*qkv, z0, ycat)


def _attn_bwd(z0, qkv, og, lg, dycat, tabs, dz0):
    def body(q_ref, k_ref, v_ref, gate_ref, dy_ref, c_ref, a_ref, b_ref,
             og0_ref, og1_ref, og2_ref, lg0_ref, lg1_ref, lg2_ref, dz_in_ref, dz_ref,
             tmp, kd, vd, ld, dg0, dg1, dg2, cg0, cg1, cg2, dod, cd, dqd, dkd, dvd, dk_s, dv_s):
        del dz_in_ref
        p = pl.program_id(1)
        ogs, lgs, dgs, cgs = (og0_ref, og1_ref, og2_ref), (lg0_ref, lg1_ref, lg2_ref), (dg0, dg1, dg2), (cg0, cg1, cg2)
        dq_ref = dbg_ref = dz_ref

        @pl.when((p > 0) & ((p - 1) % 3 == 1))
        def _():
            dz_ref[...] = dk_s[...]

        @pl.when((p > 0) & ((p - 1) % 3 == 2))
        def _():
            dz_ref[...] = dv_s[...]

        @pl.when(p == 0)
        def _():
            w = _group_weights(lgs)
            o = w[0] * ogs[0][...] + w[1] * ogs[1][...] + w[2] * ogs[2][...]
            silu, dsilu = _silu_and_grad(gate_ref[...])
            dy = dy_ref[...]
            dbg_ref[...] = (dy * o * dsilu).astype(BF16)
            do = dy * silu
            dwbar = jnp.sum(do * o, axis=1, keepdims=True)
            for gi in range(3):
                dgs[gi][...] = w[gi] * do
                cgs[gi][...] = -w[gi] * dwbar

        for gi, (_, dil) in enumerate(PATTERNS):
            @pl.when(p == 1 + 3 * gi)
            def _(gi=gi, dil=dil):
                nb = S // dil // BLK
                qd = q_ref
                c, a, b = c_ref[...], a_ref[...], b_ref[...]
                _pad_copy(kd, k_ref)
                _pad_copy(vd, v_ref)
                _deinterleave(dod, dgs[gi], dil, BF16)
                _deinterleave(ld, lgs[gi], dil)
                _deinterleave(cd, cgs[gi], dil)
                dkd[...] = jnp.zeros_like(dkd)
                dvd[...] = jnp.zeros_like(dvd)
                flat = lambda t: t.reshape(CU * BLK, HEAD_DIM)
                for u0 in range(0, NUNITS, CU):
                    q, s_own, s_prev = _chunk_scores(u0, nb, qd, kd)
                    lse, cv, do = _blocks(ld, u0), _blocks(cd, u0), _blocks(dod, u0)
                    own = slice((u0 + 1) * BLK, (u0 + 1 + CU) * BLK)
                    p_own = jnp.exp(s_own - lse)
                    ds_own = (p_own * (_dot(do, _blocks(vd, u0 + 1), B_QK) + cv) * SCALE).astype(BF16)
                    dq = _dot(ds_own, _blocks(kd, u0 + 1), B_PV)
                    dkd[own, :] += flat(_dot(ds_own, q, B_TN))
                    dvd[own, :] += flat(_dot(p_own.astype(BF16), do, B_TN))
                    if s_prev is not None:
                        prev = slice(u0 * BLK, (u0 + CU) * BLK)
                        p_prev = jnp.exp(s_prev - lse)
                        ds_prev = (p_prev * (_dot(do, _blocks(vd, u0), B_QK) + cv) * SCALE).astype(BF16)
                        dq = dq + _dot(ds_prev, _blocks(kd, u0), B_PV)
                        dkd[prev, :] += flat(_dot(ds_prev, q, B_TN))
                        dvd[prev, :] += flat(_dot(p_prev.astype(BF16), do, B_TN))
                    dqd[u0 * BLK:(u0 + CU) * BLK, :] = flat(dq)
                _interleave(tmp, dqd, dil)
                dq_ref[...] = _rope_t(tmp[...], c, a, b).astype(BF16)
                _interleave(tmp, dkd, dil, BLK)
                dk_s[...] = _rope_t(tmp[...], c, a, b).astype(BF16)
                _interleave(tmp, dvd, dil, BLK)
                dv_s[...] = tmp[...].astype(BF16)

    def out_col(h, p):
        g, r = jnp.maximum(p - 1, 0) // 3, jnp.maximum(p - 1, 0) % 3
        return 0, jnp.where(p == 0, BG_COL + h, Q_COL + r * 24 + g * 8 + h)

    tab = pl.BlockSpec((S, HEAD_DIM), lambda h, p: (0, 0))
    hspec = lambda base: pl.BlockSpec((S, HEAD_DIM), lambda h, p: (0, base + h))
    gspec = pl.BlockSpec((S, HEAD_DIM), lambda h, p: (0, (jnp.maximum(p - 1, 0) // 3) * 8 + h))
    return pl.pallas_call(
        body, grid=(8, 10),
        in_specs=[gspec, gspec, gspec, hspec(BG_COL), hspec(8), tab, tab, tab,
                  hspec(0), hspec(8), hspec(16), hspec(0), hspec(8), hspec(16), ANY_SPEC],
        out_specs=pl.BlockSpec((S, HEAD_DIM), out_col),
        out_shape=jax.ShapeDtypeStruct((S, EVEN_IN), BF16),
        scratch_shapes=[_slab(), _slab(BF16, S + BLK), _slab(BF16, S + BLK), _slab()] + [_slab() for _ in range(6)]
                       + [_slab(BF16), _slab(), _slab(), _slab(F32, S + BLK), _slab(F32, S + BLK),
                          _slab(BF16), _slab(BF16)],
        input_output_aliases={14: 0},
        compiler_params=_params(("parallel", "arbitrary"), VMEM_BIG), name="attn_bwd",
    )(*qkv, z0, dycat, *tabs, og, og, og, lg, lg, lg, dz0)


SGU_CH = 256
NCHUNK = TR // 128


def _ln_stats(x):
    mu = jnp.mean(x, axis=-1, keepdims=True)
    xc = x - mu
    r = lax.rsqrt(jnp.mean(xc * xc, axis=-1, keepdims=True) + EPS)
    return xc * r, r


def _ln_bwd(dy, xhat, r, g):
    dxh = dy * g
    return r * (dxh - jnp.mean(dxh, axis=-1, keepdims=True) - xhat * jnp.mean(dxh * xhat, axis=-1, keepdims=True))


def _tril_bf16(w):
    row = lax.broadcasted_iota(jnp.int32, w.shape, 0)
    col = lax.broadcasted_iota(jnp.int32, w.shape, 1)
    return jnp.where(row >= col, w, 0.0).astype(BF16)


def _sgu_gate(vn_s, s_s, w_ref, bb_ref):
    for h in range(4):
        wm = _tril_bf16(w_ref[h])
        bias = bb_ref[h]
        for ch in range(NCHUNK):
            rows, cols = slice(ch * 128, (ch + 1) * 128), slice(h * SGU_CH, (h + 1) * SGU_CH)
            s_s[rows, cols] = _dot(wm, vn_s[rows, cols]) + jnp.concatenate([bias, bias], axis=1)


WIN = HALO + TR
SUBL = 8


def _shifted_copies(dst, src):
    dst[0] = src[...]
    for b in range(1, SUBL):
        dst[b, 0:WIN - SUBL, :] = src[pl.ds(b, WIN - SUBL), :]


def _rows_at(copies, off, n):
    return copies[off % SUBL, pl.ds(off - off % SUBL, n), :]


def _conv_fwd(i, dval_ref, dglu_ref, hval_ref, hglu_ref, cw_ref, cb_ref, xw, xr, dcs):
    halo = hval_ref[...] * _sigmoid(hglu_ref[...])
    xw[0:HALO, :] = jnp.where(i > 0, halo, 0.0)
    xw[HALO:HALO + TR, :] = dval_ref[...] * _sigmoid(dglu_ref[...])
    _shifted_copies(xr, xw)
    for rb in range(TR // SUB):
        acc = jnp.broadcast_to(cb_ref[...], (SUB, HALF))
        for k in range(CONV_K):
            acc = acc + cw_ref[k:k + 1, :] * _rows_at(xr, rb * SUB + HALO - (CONV_K - 1) + k, SUB)
        dcs[rb * SUB:(rb + 1) * SUB, :] = acc


def _odd_in_specs():
    col = lambda j: pl.BlockSpec((TR, HALF), lambda i, *_: (i, j))
    prev = lambda j: pl.BlockSpec((HALO, HALF), lambda i, *_: (jnp.maximum(i * (TR // HALO) - 1, 0), j))
    return [col(0), col(1), col(2), col(3), col(4), col(5), prev(3), prev(4)]


def _full_spec(shape):
    return pl.BlockSpec(shape, lambda i, *_: (0,) * len(shape))


def _odd_fwd(z1, sgu_g, sgu_b, sgu_w, sgu_bb, conv_w, conv_b, cn_g, cn_b):
    def body(u_ref, v_ref, cg_ref, dval_ref, dglu_ref, dgate_ref, hval_ref, hglu_ref,
             g_ref, b_ref, w_ref, bb_ref, cw_ref, cb_ref, cng_ref, cnb_ref, out_ref, vn_s, s_s, xw, dcs, xr):
        i = pl.program_id(0)
        vhat, _ = _ln_stats(v_ref[...])
        vn_s[...] = (vhat * g_ref[...] + b_ref[...]).astype(BF16)
        _sgu_gate(vn_s, s_s, w_ref, bb_ref)
        cg = cg_ref[...]
        out_ref[:, 0:HALF] = (u_ref[...] * s_s[...] * (cg * _sigmoid(cg))).astype(BF16)
        _conv_fwd(i, dval_ref, dglu_ref, hval_ref, hglu_ref, cw_ref, cb_ref, xw, xr, dcs)
        dhat, _ = _ln_stats(dcs[...])
        dn = dhat * cng_ref[...] + cnb_ref[...]
        dgate = dgate_ref[...]
        out_ref[:, HALF:2 * HALF] = ((dn * _sigmoid(dn)) * (dgate * _sigmoid(dgate))).astype(BF16)

    vec = _full_spec((1, HALF))
    return pl.pallas_call(
        body, grid=(S // TR,),
        in_specs=_odd_in_specs() + [vec, vec, _full_spec((4, 128, 128)), _full_spec((4, 128, 128)),
                                    _full_spec((HALO, HALF)), vec, vec, vec],
        out_specs=pl.BlockSpec((TR, 2048), lambda i: (i, 0)),
        out_shape=jax.ShapeDtypeStruct((S, 2048), BF16),
        scratch_shapes=[pltpu.VMEM((TR, HALF), BF16), pltpu.VMEM((TR, HALF), F32),
                        pltpu.VMEM((WIN, HALF), F32), pltpu.VMEM((TR, HALF), F32), pltpu.VMEM((SUBL, WIN, HALF), F32)],
        compiler_params=_params(("parallel",), VMEM_BIG), name="odd_fwd",
    )(z1, z1, z1, z1, z1, z1, z1, z1, sgu_g, sgu_b, sgu_w, sgu_bb, conv_w, conv_b, cn_g, cn_b)


def _odd_bwd_a(z1, dycat, sgu_g, sgu_b, sgu_w, sgu_bb, conv_w, conv_b, cn_g, cn_b):
    def body(u_ref, v_ref, cg_ref, dval_ref, dglu_ref, dgate_ref, hval_ref, hglu_ref, dy_ref,
             g_ref, b_ref, w_ref, bb_ref, cw_ref, cb_ref, cng_ref, cnb_ref,
             dz_ref, ddc_ref, dw_ref, dbb_ref, dg_ref, db_ref, dcng_ref, dcnb_ref, dcb_ref,
             vn_s, s_s, xw, dcs, ds_s, dvn_s, xr):
        i = pl.program_id(0)
        vhat, rv = _ln_stats(v_ref[...])
        g = g_ref[...]
        vn_s[...] = (vhat * g + b_ref[...]).astype(BF16)
        _sgu_gate(vn_s, s_s, w_ref, bb_ref)
        silu_c, dsilu_c = _silu_and_grad(cg_ref[...])
        dyc = dy_ref[:, 0:HALF]
        u = u_ref[...]
        s = s_s[...]
        dz_ref[:, 0:HALF] = (dyc * s * silu_c).astype(BF16)
        dz_ref[:, 2 * HALF:3 * HALF] = (dyc * u * s * dsilu_c).astype(BF16)
        ds_s[...] = dyc * u * silu_c

        @pl.when(i == 0)
        def _():
            dw_ref[...] = jnp.zeros_like(dw_ref)
            dbb_ref[...] = jnp.zeros_like(dbb_ref)

        tril = lax.broadcasted_iota(jnp.int32, (128, 128), 0) >= lax.broadcasted_iota(jnp.int32, (128, 128), 1)
        for h in range(4):
            wm = _tril_bf16(w_ref[h])
            for ch in range(NCHUNK):
                rows, cols = slice(ch * 128, (ch + 1) * 128), slice(h * SGU_CH, (h + 1) * SGU_CH)
                ds = ds_s[rows, cols]
                dsb = ds.astype(BF16)
                dw_ref[h] += jnp.where(tril, _dot(dsb, vn_s[rows, cols], NT), 0.0)
                dbb_ref[h] += jnp.broadcast_to(jnp.sum(ds, axis=1, keepdims=True), (128, 128))
                dvn_s[rows, cols] = _dot(wm, dsb, TN)
        dvn = dvn_s[...]
        _acc_rows(dg_ref, dvn * vhat, i)
        _acc_rows(db_ref, dvn, i)
        dz_ref[:, HALF:2 * HALF] = _ln_bwd(dvn, vhat, rv, g).astype(BF16)

        _conv_fwd(i, dval_ref, dglu_ref, hval_ref, hglu_ref, cw_ref, cb_ref, xw, xr, dcs)
        dhat, rd = _ln_stats(dcs[...])
        cng = cng_ref[...]
        silu_n, dsilu_n = _silu_and_grad(dhat * cng + cnb_ref[...])
        silu_g, dsilu_g = _silu_and_grad(dgate_ref[...])
        dyd = dy_ref[:, HALF:2 * HALF]
        dz_ref[:, 5 * HALF:6 * HALF] = (dyd * silu_n * dsilu_g).astype(BF16)
        ddn = dyd * silu_g * dsilu_n
        _acc_rows(dcng_ref, ddn * dhat, i)
        _acc_rows(dcnb_ref, ddn, i)
        ddc = _ln_bwd(ddn, dhat, rd, cng)
        ddc_ref[...] = ddc
        _acc_rows(dcb_ref, ddc, i)

    vec = _full_spec((1, HALF))
    sq = _full_spec((4, 128, 128))
    return pl.pallas_call(
        body, grid=(S // TR,),
        in_specs=_odd_in_specs() + [pl.BlockSpec((TR, 2048), lambda i: (i, 0)),
                                    vec, vec, sq, sq, _full_spec((HALO, HALF)), vec, vec, vec],
        out_specs=[pl.BlockSpec((TR, ODD_IN), lambda i: (i, 0)), pl.BlockSpec((TR, HALF), lambda i: (i, 0)),
                   sq, sq, vec, vec, vec, vec, vec],
        out_shape=[jax.ShapeDtypeStruct((S, ODD_IN), BF16), jax.ShapeDtypeStruct((S, HALF), F32),
                   jax.ShapeDtypeStruct((4, 128, 128), F32), jax.ShapeDtypeStruct((4, 128, 128), F32)]
                  + [jax.ShapeDtypeStruct((1, HALF), F32)] * 5,
        scratch_shapes=[pltpu.VMEM((TR, HALF), BF16), pltpu.VMEM((TR, HALF), F32),
                        pltpu.VMEM((WIN, HALF), F32), pltpu.VMEM((TR, HALF), F32),
                        pltpu.VMEM((TR, HALF), F32), pltpu.VMEM((TR, HALF), F32), pltpu.VMEM((SUBL, WIN, HALF), F32)],
        compiler_params=_params(("arbitrary",), VMEM_BIG), name="odd_bwd_a",
    )(z1, z1, z1, z1, z1, z1, z1, z1, dycat, sgu_g, sgu_b, sgu_w, sgu_bb, conv_w, conv_b, cn_g, cn_b)


def _odd_bwd_b(z1, ddc, dz1, conv_w):
    nt = S // TR

    def body(dval_ref, dglu_ref, hval_ref, hglu_ref, ddc_ref, hddc_ref, cw_ref, dz_in_ref,
             dz_ref, dcw_ref, xw, dwin, dxs, xr, dr):
        del dz_in_ref
        i, j = pl.program_id(0), pl.program_id(1)
        sg = _sigmoid(dglu_ref[...])
        dval = dval_ref[...]

        @pl.when(j == 0)
        def _():
            halo = hval_ref[...] * _sigmoid(hglu_ref[...])
            xw[0:HALO, :] = jnp.where(i > 0, halo, 0.0)
            xw[HALO:HALO + TR, :] = dval * sg
            dwin[0:TR, :] = ddc_ref[...]
            dwin[TR:TR + HALO, :] = jnp.where(i < nt - 1, hddc_ref[...], 0.0)
            _shifted_copies(xr, xw)
            _shifted_copies(dr, dwin)

            @pl.when(i == 0)
            def _():
                dcw_ref[...] = jnp.zeros_like(dcw_ref)

            for rb in range(TR // SUB):
                acc = jnp.zeros((SUB, HALF), F32)
                for k in range(CONV_K):
                    acc = acc + cw_ref[k:k + 1, :] * _rows_at(dr, rb * SUB + (CONV_K - 1) - k, SUB)
                dxs[rb * SUB:(rb + 1) * SUB, :] = acc
            for k in range(CONV_K):
                acc = jnp.zeros((SUB, HALF), F32)
                for rb in range(TR // SUB):
                    acc = acc + dwin[rb * SUB:(rb + 1) * SUB, :] * _rows_at(xr, rb * SUB + HALO - (CONV_K - 1) + k, SUB)
                dcw_ref[k:k + 1, :] += jnp.sum(acc, axis=0, keepdims=True)
            dz_ref[...] = (dxs[...] * sg).astype(BF16)

        @pl.when(j == 1)
        def _():
            dz_ref[...] = (dxs[...] * dval * sg * (1.0 - sg)).astype(BF16)

    col = lambda c: pl.BlockSpec((TR, HALF), lambda i, j: (i, c))
    prev = lambda c: pl.BlockSpec((HALO, HALF), lambda i, j: (jnp.maximum(i * (TR // HALO) - 1, 0), c))
    nxt = pl.BlockSpec((HALO, HALF), lambda i, j: (jnp.minimum((i + 1) * (TR // HALO), S // HALO - 1), 0))
    return pl.pallas_call(
        body, grid=(nt, 2),
        in_specs=[col(3), col(4), prev(3), prev(4), pl.BlockSpec((TR, HALF), lambda i, j: (i, 0)), nxt,
                  _full_spec((HALO, HALF)), pl.BlockSpec(memory_space=pl.ANY)],
        out_specs=[pl.BlockSpec((TR, HALF), lambda i, j: (i, 3 + j)), _full_spec((HALO, HALF))],
        out_shape=[jax.ShapeDtypeStruct((S, ODD_IN), BF16), jax.ShapeDtypeStruct((HALO, HALF), F32)],
        scratch_shapes=[pltpu.VMEM((WIN, HALF), F32), pltpu.VMEM((WIN, HALF), F32), pltpu.VMEM((TR, HALF), F32),
                        pltpu.VMEM((SUBL, WIN, HALF), F32), pltpu.VMEM((SUBL, WIN, HALF), F32)],
        input_output_aliases={7: 0},
        compiler_params=_params(("arbitrary", "arbitrary"), VMEM_BIG), name="odd_bwd_b",
    )(z1, z1, z1, z1, ddc, ddc, conv_w, dz1)


def _cast_bf16(w, name):
    r, c = w.shape
    tr = min(r, 256)
    def body(i_ref, o_ref):
        o_ref[...] = i_ref[...].astype(BF16)

    return pl.pallas_call(
        body, grid=(r // tr,), in_specs=[pl.BlockSpec((tr, c), lambda i: (i, 0))],
        out_specs=pl.BlockSpec((tr, c), lambda i: (i, 0)), out_shape=jax.ShapeDtypeStruct((r, c), BF16),
        compiler_params=_params(("parallel",)), name=name,
    )(w)


def _adamw(w, g, m, v):
    m = ADAM_B1 * m + (1.0 - ADAM_B1) * g
    v = ADAM_B2 * v + (1.0 - ADAM_B2) * (g * g)
    m_hat = m / (1.0 - ADAM_B1 ** ADAM_STEP)
    v_hat = v / (1.0 - ADAM_B2 ** ADAM_STEP)
    delta = -ADAM_LR * (m_hat / (jnp.sqrt(v_hat) + ADAM_EPS) + ADAM_WD * w)
    return delta, m, v


def _adam_reduce(parts, w, m, v, name, dep=None):
    r, c = w.shape
    tr = min(r, 128)
    deps = [] if dep is None else [dep]
    nparts = parts.shape[0]

    def body(p_ref, w_ref, m_ref, v_ref, *rest):
        g_ref, d_ref, nm_ref, nv_ref = rest[len(deps):]
        g = p_ref[0].astype(F32)
        for d in range(1, nparts):
            g = g + p_ref[d].astype(F32)
        g_ref[...] = g
        d_ref[...], nm_ref[...], nv_ref[...] = _adamw(w_ref[...], g, m_ref[...], v_ref[...])

    spec = pl.BlockSpec((tr, c), lambda i: (i, 0))
    return pl.pallas_call(
        body, grid=(r // tr,),
        in_specs=[pl.BlockSpec((nparts, tr, c), lambda i: (0, i, 0)), spec, spec, spec] + [ANY_SPEC] * len(deps),
        out_specs=[spec] * 4, out_shape=[jax.ShapeDtypeStruct((r, c), F32)] * 4,
        compiler_params=_params(("parallel",), VMEM_BIG), name=name,
    )(parts, w, m, v, *deps)


def _sum_parts(parts, name, dep=None):
    r = parts.shape[1]
    tr = 8
    for cand in (512, 256, 128, 64, 32, 16, 8):
        if r % cand == 0:
            tr = cand
            break
    deps = [] if dep is None else [dep]

    def body(p_ref, *rest):
        g = p_ref[0]
        for d in range(1, NDEV):
            g = g + p_ref[d]
        rest[-1][...] = g

    return pl.pallas_call(
        body, grid=(r // tr,), in_specs=[pl.BlockSpec((NDEV, tr, 128), lambda i: (0, i, 0))] + [ANY_SPEC] * len(deps),
        out_specs=pl.BlockSpec((tr, 128), lambda i: (i, 0)), out_shape=jax.ShapeDtypeStruct((r, 128), F32),
        compiler_params=_params(("parallel",)), name=name,
    )(parts, *deps)


def _sum_unpack(parts, rows, name):
    def body(p_ref, *outs):
        off = 0
        for o_ref, n in zip(outs, rows):
            acc = p_ref[0, off:off + n, :]
            for d in range(1, NDEV):
                acc = acc + p_ref[d, off:off + n, :]
            o_ref[...] = acc
            off += n

    return pl.pallas_call(
        body, grid=(1,), in_specs=[pl.BlockSpec(parts.shape, lambda i: (0, 0, 0))],
        out_specs=[pl.BlockSpec((n, 128), lambda i: (0, 0)) for n in rows],
        out_shape=[jax.ShapeDtypeStruct((n, 128), F32) for n in rows],
        compiler_params=_params(("arbitrary",), VMEM_BIG), name=name,
    )(parts)


def _adam_small(ws, gs, g_specs, ms, vs, name):
    n = len(ws)

    def body(*refs):
        w_r, g_r, m_r, v_r = refs[:n], refs[n:2 * n], refs[2 * n:3 * n], refs[3 * n:4 * n]
        outs = refs[4 * n:]
        for i in range(n):
            g = g_r[i][...]
            outs[4 * i][...] = g
            outs[4 * i + 1][...], outs[4 * i + 2][...], outs[4 * i + 3][...] = _adamw(
                w_r[i][...], g, m_r[i][...], v_r[i][...])

    whole = lambda a: pl.BlockSpec(a.shape, lambda i, nd=a.ndim: (0,) * nd)
    outs = pl.pallas_call(
        body, grid=(1,),
        in_specs=[whole(a) for a in ws] + list(g_specs) + [whole(a) for a in ms] + [whole(a) for a in vs],
        out_specs=[whole(a) for a in ws for _ in range(4)],
        out_shape=[jax.ShapeDtypeStruct(a.shape, F32) for a in ws for _ in range(4)],
        compiler_params=_params(("arbitrary",), VMEM_BIG), name=name,
    )(*ws, *gs, *ms, *vs)
    return [outs[4 * i:4 * i + 4] for i in range(n)]


MASKS = [(mx, my, mc) for mx in (0, 1) for my in (0, 1) for mc in (0, 1)][1:]


def _exchange(arrays, scatter, name):
    nt = len(arrays)
    out_shape = [jax.ShapeDtypeStruct(((NDEV,) + a.shape) if not scatter else a.shape, a.dtype) for a in arrays]

    def body(*refs):
        ins, outs = refs[:nt], refs[nt:2 * nt]
        send_sems, recv_sems, local_sems = refs[2 * nt:]
        x, y, c = lax.axis_index("x"), lax.axis_index("y"), lax.axis_index("c")
        me = 4 * x + 2 * y + c
        copies = []
        for t in range(nt):
            src_own = ins[t].at[me] if scatter else ins[t]
            loc = pltpu.make_async_copy(src_own, outs[t].at[me], local_sems.at[t])
            loc.start()
            copies.append(loc)
            for k, (mx, my, mc) in enumerate(MASKS):
                px, py, pc = (x + mx) % 2, (y + my) % 2, (c + mc) % 2
                peer = 4 * px + 2 * py + pc
                src = ins[t].at[peer] if scatter else ins[t]
                rc = pltpu.make_async_remote_copy(
                    src_ref=src, dst_ref=outs[t].at[me], send_sem=send_sems.at[t, k], recv_sem=recv_sems.at[t, k],
                    device_id=(px, py, pc), device_id_type=MESH)
                rc.start()
                copies.append(rc)
        for cp in copies:
            cp.wait()

    hbm = pl.BlockSpec(memory_space=pl.ANY)
    return pl.pallas_call(
        body, in_specs=[hbm] * nt, out_specs=[hbm] * nt, out_shape=out_shape,
        scratch_shapes=[pltpu.SemaphoreType.DMA((nt, 7)), pltpu.SemaphoreType.DMA((nt, 7)),
                        pltpu.SemaphoreType.DMA((nt,))],
        name=name,
    )(*arrays)


SEM_SPEC = pl.BlockSpec(memory_space=pltpu.SEMAPHORE)
EFFECT = pltpu.SideEffectType.DATAFLOW_SIDE_EFFECTING


def _direct_plan(scatter):
    def plan(x, y, c, srcs, lands):
        me = 4 * x + 2 * y + c
        local, remote = [], []
        for src, land in zip(srcs, lands):
            local.append((src.at[me] if scatter else src, land.at[me]))
            for mx, my, mc in MASKS:
                px, py, pc = (x + mx) % 2, (y + my) % 2, (c + mc) % 2
                blk = src.at[4 * px + 2 * py + pc] if scatter else src
                remote.append((blk, land.at[me], (px, py, pc)))
        return local, remote
    return plan


def _split_start(name, srcs, land_shapes, plan, n_local, n_remote, dep=None):
    ns, nl = len(srcs), len(land_shapes)
    deps = [] if dep is None else [dep]
    lands = [lax.empty(s.shape, s.dtype) for s in land_shapes]

    def body(*refs):
        ins, lz = refs[:ns], refs[ns:ns + nl]
        outs = refs[ns + nl + len(deps):]
        send_sems, recv_sems, token, local_sems = outs[0], outs[1], outs[2 + ns + nl], outs[3 + ns + nl]
        local, remote = plan(lax.axis_index("x"), lax.axis_index("y"), lax.axis_index("c"), ins, lz)
        own = [pltpu.make_async_copy(src, dst, local_sems.at[i]) for i, (src, dst) in enumerate(local)]
        for cp in own:
            cp.start()
        for cp in own:
            cp.wait()
        for k, (src, dst, peer) in enumerate(remote):
            pltpu.make_async_remote_copy(src_ref=src, dst_ref=dst, send_sem=send_sems.at[k], recv_sem=recv_sems.at[k],
                                         device_id=peer, device_id_type=MESH).start()
        token[...] = jnp.zeros_like(token)

    hbm = lambda a: pltpu.HBM(a.shape, a.dtype)
    outs = pl.pallas_call(
        body, name=name,
        out_shape=(pltpu.SemaphoreType.DMA((n_remote,)), pltpu.SemaphoreType.DMA((n_remote,)),
                   *[hbm(a) for a in srcs], *[hbm(a) for a in lands], jax.ShapeDtypeStruct((8, 128), F32)),
        in_specs=[ANY_SPEC] * (ns + nl + len(deps)),
        out_specs=(SEM_SPEC, SEM_SPEC, *[ANY_SPEC] * (ns + nl), pl.BlockSpec(memory_space=pltpu.VMEM)),
        scratch_shapes=[pltpu.SemaphoreType.DMA((n_local,))],
        input_output_aliases={i: 2 + i for i in range(ns + nl)},
        compiler_params=pltpu.CompilerParams(has_side_effects=EFFECT),
    )(*[pltpu.with_memory_space_constraint(a, pltpu.HBM) for a in srcs],
      *[pltpu.with_memory_space_constraint(a, pltpu.HBM) for a in lands], *deps)
    return dict(sems=outs[:2], srcs=outs[2:2 + ns], lands=outs[2 + ns:2 + ns + nl], token=outs[-1],
                plan=plan, n_remote=n_remote)


def _split_wait(name, handle, after):
    srcs, lands, plan = handle["srcs"], handle["lands"], handle["plan"]
    ns, nl = len(srcs), len(lands)

    def body(*refs):
        ins, lz = refs[:ns], refs[ns:ns + nl]
        send_sems, recv_sems = refs[ns + nl], refs[ns + nl + 1]
        _, remote = plan(lax.axis_index("x"), lax.axis_index("y"), lax.axis_index("c"), ins, lz)
        for k, (src, dst, peer) in enumerate(remote):
            cp = pltpu.make_async_remote_copy(src_ref=src, dst_ref=dst, send_sem=send_sems.at[k],
                                              recv_sem=recv_sems.at[k], device_id=peer, device_id_type=MESH)
            cp.wait_send()
            cp.wait_recv()

    hbm = lambda a: pltpu.HBM(a.shape, a.dtype)
    outs = pl.pallas_call(
        body, name=name, out_shape=(*[hbm(a) for a in srcs], *[hbm(a) for a in lands]),
        in_specs=[ANY_SPEC] * (ns + nl) + [SEM_SPEC, SEM_SPEC, ANY_SPEC], out_specs=tuple([ANY_SPEC] * (ns + nl)),
        input_output_aliases={i: i for i in range(ns + nl)},
        compiler_params=pltpu.CompilerParams(has_side_effects=EFFECT),
    )(*srcs, *lands, *handle["sems"], after)
    return list(outs[ns:])


def _sc_exchange(name, collective_id, arrays, scatter):
    nt = len(arrays)
    out_type = [jax.ShapeDtypeStruct(a.shape if scatter else (NDEV,) + a.shape, a.dtype) for a in arrays]

    def body(*refs):
        ins, outs = refs[:nt], refs[nt:2 * nt]
        send_sems, recv_sems, local_sems = refs[2 * nt:3 * nt], refs[3 * nt:4 * nt], refs[4 * nt:5 * nt]
        x, y, c = lax.axis_index("x"), lax.axis_index("y"), lax.axis_index("c")
        peers = [(mx + x - 2 * mx * x, my + y - 2 * my * y, mc + c - 2 * mc * c) for mx, my, mc in MASKS]
        barrier = pltpu.get_barrier_semaphore()
        for peer in peers:
            pl.semaphore_signal(barrier, inc=1, device_id=peer, device_id_type=MESH)
        pl.semaphore_wait(barrier, len(peers))
        me = 4 * x + 2 * y + c
        own = []
        for t in range(nt):
            cp = pltpu.make_async_copy(ins[t].at[me] if scatter else ins[t], outs[t].at[me], local_sems[t])
            cp.start()
            own.append(cp)
            for px, py, pc in peers:
                src = ins[t].at[4 * px + 2 * py + pc] if scatter else ins[t]
                pltpu.make_async_remote_copy(src_ref=src, dst_ref=outs[t].at[me], send_sem=send_sems[t],
                                             recv_sem=recv_sems[t], device_id=(px, py, pc), device_id_type=MESH).start()
        for t in range(nt):
            own[t].wait()
            seven = outs[t].at[pl.ds(0, NDEV - 1)]
            drain = pltpu.make_async_remote_copy(src_ref=seven, dst_ref=seven, send_sem=send_sems[t],
                                                 recv_sem=recv_sems[t], device_id=(x, y, c), device_id_type=MESH)
            drain.wait_send()
            drain.wait_recv()

    return pl.kernel(
        body, out_type=out_type, mesh=plsc.ScalarSubcoreMesh(axis_name="sequencer", num_cores=1),
        scratch_types=[pltpu.SemaphoreType.DMA] * (3 * nt),
        compiler_params=pltpu.CompilerParams(collective_id=collective_id), name=name,
    )(*arrays)


def _sc_gather_two_level(name, collective_id, arrays):
    nt = len(arrays)
    out_type = [jax.ShapeDtypeStruct((NDEV,) + a.shape, a.dtype) for a in arrays]

    def body(*refs):
        ins, outs = refs[:nt], refs[nt:2 * nt]
        sems = refs[2 * nt:]
        send_sems, sib_sems, local_sems = sems[:nt], sems[nt:2 * nt], sems[2 * nt:3 * nt]
        ici_sems = [sems[3 * nt + 3 * t:3 * nt + 3 * t + 3] for t in range(nt)]
        x, y, c = lax.axis_index("x"), lax.axis_index("y"), lax.axis_index("c")
        sibling = (x, y, 1 - c)
        chips = [(1 - x, y), (x, 1 - y), (1 - x, 1 - y)]
        barrier = pltpu.get_barrier_semaphore()
        for peer in [sibling] + [(cx, cy, c) for cx, cy in chips]:
            pl.semaphore_signal(barrier, inc=1, device_id=peer, device_id_type=MESH)
        pl.semaphore_wait(barrier, 4)
        me = 4 * x + 2 * y + c

        def push(t, src, slot, recv_sem, to):
            pltpu.make_async_remote_copy(src_ref=src, dst_ref=outs[t].at[slot], send_sem=send_sems[t],
                                         recv_sem=recv_sem, device_id=to, device_id_type=MESH).start()

        own = []
        for t in range(nt):
            cp = pltpu.make_async_copy(ins[t], outs[t].at[me], local_sems[t])
            cp.start()
            own.append(cp)
            for j, (cx, cy) in enumerate(chips):
                push(t, ins[t], me, ici_sems[t][j], (cx, cy, c))
            push(t, ins[t], me, sib_sems[t], sibling)
        for t in range(nt):
            for j, (cx, cy) in enumerate(chips):
                slot = 4 * cx + 2 * cy + c
                landed = outs[t].at[slot]
                pltpu.make_async_remote_copy(src_ref=landed, dst_ref=landed, send_sem=send_sems[t],
                                             recv_sem=ici_sems[t][j], device_id=(cx, cy, c),
                                             device_id_type=MESH).wait_recv()
                push(t, landed, slot, sib_sems[t], sibling)
        for t in range(nt):
            own[t].wait()
            four, seven = outs[t].at[pl.ds(0, 4)], outs[t].at[pl.ds(0, 7)]
            pltpu.make_async_remote_copy(src_ref=four, dst_ref=four, send_sem=send_sems[t], recv_sem=sib_sems[t],
                                         device_id=sibling, device_id_type=MESH).wait_recv()
            pltpu.make_async_remote_copy(src_ref=seven, dst_ref=seven, send_sem=send_sems[t], recv_sem=sib_sems[t],
                                         device_id=sibling, device_id_type=MESH).wait_send()

    return pl.kernel(
        body, out_type=out_type, mesh=plsc.ScalarSubcoreMesh(axis_name="sequencer", num_cores=1),
        scratch_types=[pltpu.SemaphoreType.DMA] * (6 * nt),
        compiler_params=pltpu.CompilerParams(collective_id=collective_id), name=name,
    )(*arrays)


def _sc_sibling_exchange(name, collective_id, src, out_shape, pieces):
    def body(src_ref, out_ref, send_sem, recv_sem):
        x, y, c = lax.axis_index("x"), lax.axis_index("y"), lax.axis_index("c")
        sibling = (x, y, 1 - c)
        barrier = pltpu.get_barrier_semaphore()
        pl.semaphore_signal(barrier, inc=1, device_id=sibling, device_id_type=MESH)
        pl.semaphore_wait(barrier, 1)
        for piece, lands in pieces(c, src_ref, out_ref):
            pltpu.make_async_remote_copy(src_ref=piece, dst_ref=lands, send_sem=send_sem, recv_sem=recv_sem,
                                         device_id=sibling, device_id_type=MESH).start()
        drain = pltpu.make_async_remote_copy(src_ref=out_ref, dst_ref=out_ref, send_sem=send_sem, recv_sem=recv_sem,
                                             device_id=sibling, device_id_type=MESH)
        drain.wait_send()
        drain.wait_recv()

    return pl.kernel(
        body, out_type=jax.ShapeDtypeStruct(out_shape, src.dtype),
        mesh=plsc.ScalarSubcoreMesh(axis_name="sequencer", num_cores=1), scratch_types=[pltpu.SemaphoreType.DMA] * 2,
        compiler_params=pltpu.CompilerParams(collective_id=collective_id), name=name,
    )(src)


def _sc_chip_scatter(name, collective_id, q):
    def body(q_ref, out_ref, send_sem, recv_sem, local_sem):
        x, y, c = lax.axis_index("x"), lax.axis_index("y"), lax.axis_index("c")
        chips = [(1 - x, y), (x, 1 - y), (1 - x, 1 - y)]
        barrier = pltpu.get_barrier_semaphore()
        for cx, cy in chips:
            pl.semaphore_signal(barrier, inc=1, device_id=(cx, cy, c), device_id_type=MESH)
        pl.semaphore_wait(barrier, 3)
        mine = 2 * x + y
        own = pltpu.make_async_copy(q_ref.at[mine], out_ref.at[mine], local_sem)
        own.start()
        for cx, cy in chips:
            pltpu.make_async_remote_copy(src_ref=q_ref.at[2 * cx + cy], dst_ref=out_ref.at[mine], send_sem=send_sem,
                                         recv_sem=recv_sem, device_id=(cx, cy, c), device_id_type=MESH).start()
        own.wait()
        three = out_ref.at[pl.ds(0, 3)]
        drain = pltpu.make_async_remote_copy(src_ref=three, dst_ref=three, send_sem=send_sem, recv_sem=recv_sem,
                                             device_id=(x, y, c), device_id_type=MESH)
        drain.wait_send()
        drain.wait_recv()

    return pl.kernel(
        body, out_type=jax.ShapeDtypeStruct(q.shape, q.dtype),
        mesh=plsc.ScalarSubcoreMesh(axis_name="sequencer", num_cores=1), scratch_types=[pltpu.SemaphoreType.DMA] * 3,
        compiler_params=pltpu.CompilerParams(collective_id=collective_id), name=name,
    )(q)


def _mm_pair_dw(h_own, dz, h_sib, dz_sib, nb, name, dep=None):
    tn = 512 if nb % 512 == 0 else nb
    per = nb // tn
    o_spec = pl.BlockSpec((None, D, tn), lambda i, j, k: (j // per, 0, j % per))
    part = _matmul(
        h_own, dz, dn=TN, grid=(1, 4 * per, 1),
        a_spec=pl.BlockSpec((S, D), lambda i, j, k: (0, 0)),
        b_spec=pl.BlockSpec((S, tn), lambda i, j, k: (0, (2 * (j // per) + lax.axis_index("c")) * per + j % per)),
        o_spec=o_spec, out_shape=(4, D, nb), out_dtype=F32, acc_shape=(D, tn), name=name + "_own", dep=dep)

    def body(a_ref, b_ref, p_ref, o_ref):
        o_ref[...] = (p_ref[...] + _dot(a_ref[...], b_ref[...], TN)).astype(BF16)

    return pl.pallas_call(
        body, grid=(1, 4 * per, 1),
        in_specs=[pl.BlockSpec((S, D), lambda i, j, k: (0, 0)), pl.BlockSpec((S, tn), lambda i, j, k: (0, j)), o_spec],
        out_specs=o_spec, out_shape=jax.ShapeDtypeStruct((4, D, nb), BF16),
        compiler_params=_params(("parallel", "parallel", "arbitrary"), VMEM_BIG), name=name + "_sibling",
    )(h_sib, dz_sib, part)


SMALL = {
    "e_pre_norm": ((2048,), None), "e_pool_w": ((4, 256, 256), 1), "e_pool_scale": ((1024,), None),
    "e_post_norm": ((2048,), None), "o_pre_norm": ((2048,), 0), "o_sgu_norm_g": ((1024,), 0),
    "o_sgu_norm_b": ((1024,), 0), "o_sgu_w": ((4, 128, 128), None), "o_sgu_b": ((4, 128), None),
    "o_conv_w": ((31, 1024), 1), "o_conv_b": ((1024,), 0), "o_conv_norm_g": ((1024,), 0),
    "o_conv_norm_b": ((1024,), 0), "o_post_norm": ((2048,), 0),
}
SMALL_SHARDED = [n for n, (_, ax) in SMALL.items() if ax is not None]


def _shard_shape(name):
    shape, ax = SMALL[name]
    if ax is None:
        return shape
    return tuple(s // NDEV if i == ax else s for i, s in enumerate(shape))


def _pack(arrs, row_multiple=1):
    flat = jnp.concatenate([a.reshape(-1) for a in arrs])
    pad = -flat.shape[0] % (128 * row_multiple)
    return jnp.concatenate([flat, jnp.zeros((pad,), F32)]).reshape(-1, 128)


def _small_views(name):
    shape, ax = SMALL[name]
    me = lambda: 4 * lax.axis_index("x") + 2 * lax.axis_index("y") + lax.axis_index("c")
    if ax is None:
        view = (int(np.prod(shape)) // 128, 128)
        return view, view, pl.BlockSpec(view, lambda i: (0, 0))
    if len(shape) == 1:
        n = shape[0] // NDEV
        return (1, n), (NDEV, 1, n), pl.BlockSpec((None, 1, n), lambda i: (me(), 0, 0))
    part = _shard_shape(name)
    return part, shape, pl.BlockSpec(part, lambda i: tuple(me() if d == ax else 0 for d in range(len(shape))))


BIG = ("e_w_in", "e_w_out", "o_w_in", "o_w_out")
WEIGHTS = ["e_pre_norm", "e_w_in", "e_pool_w", "e_pool_scale", "e_w_out", "e_post_norm", "o_pre_norm", "o_w_in",
           "o_sgu_norm_g", "o_sgu_norm_b", "o_sgu_w", "o_sgu_b", "o_conv_w", "o_conv_b", "o_conv_norm_g",
           "o_conv_norm_b", "o_w_out", "o_post_norm"]


def kernel(x, e_pre_norm, e_w_in, e_pool_w, e_pool_scale, e_w_out, e_post_norm, o_pre_norm, o_w_in, o_sgu_norm_g, o_sgu_norm_b, o_sgu_w, o_sgu_b, o_conv_w, o_conv_b, o_conv_norm_g, o_conv_norm_b, o_w_out, o_post_norm, loss_target, m_e_pre_norm, m_e_w_in, m_e_pool_w, m_e_pool_scale, m_e_w_out, m_e_post_norm, m_o_pre_norm, m_o_w_in, m_o_sgu_norm_g, m_o_sgu_norm_b, m_o_sgu_w, m_o_sgu_b, m_o_conv_w, m_o_conv_b, m_o_conv_norm_g, m_o_conv_norm_b, m_o_w_out, m_o_post_norm, v_e_pre_norm, v_e_w_in, v_e_pool_w, v_e_pool_scale, v_e_w_out, v_e_post_norm, v_o_pre_norm, v_o_w_in, v_o_sgu_norm_g, v_o_sgu_norm_b, v_o_sgu_w, v_o_sgu_b, v_o_conv_w, v_o_conv_b, v_o_conv_norm_g, v_o_conv_norm_b, v_o_w_out, v_o_post_norm):
    given = dict(locals())
    w = {n: given[n][0] for n in WEIGHTS}
    m = {n: given["m_" + n][0] for n in WEIGHTS}
    v = {n: given["v_" + n][0] for n in WEIGHTS}
    me = 4 * lax.axis_index("x") + 2 * lax.axis_index("y") + lax.axis_index("c")
    x, target = x[0], loss_target[0]
    row = lambda a: a.reshape(1, -1)

    bf = {n: _cast_bf16(w[n], "cast_" + n) for n in BIG}
    wg_e_in, small_rows = _sc_gather_two_level("gather_a", 0, [bf["e_w_in"], _pack([w[n] for n in SMALL_SHARDED])])
    wg_e_out, wg_o_in, wg_o_out = _sc_gather_two_level("gather_b", 1, [bf["e_w_out"], bf["o_w_in"], bf["o_w_out"]])
    h0 = _pre0_fwd(x, row(w["e_pre_norm"]))
    h0_sib = _sc_sibling_exchange("swap_h0", 8, h0, h0.shape, lambda c, src, out: [(src, out)])
    p = {n: w[n] for n in SMALL if SMALL[n][1] is None}
    small_rows = small_rows.reshape(NDEV, -1)
    off = 0
    for n in SMALL_SHARDED:
        shp, ax = _shard_shape(n), SMALL[n][1]
        cnt = int(np.prod(shp))
        blk = small_rows[:, off:off + cnt].reshape((NDEV,) + shp)
        p[n] = jnp.moveaxis(blk, 0, ax).reshape(SMALL[n][0])
        off += cnt
    tabs = _rope_tables()
    pool_w_bf = p["e_pool_w"].astype(BF16)
    sgu_bb = jnp.broadcast_to(p["o_sgu_b"][:, :, None], (4, 128, 128))
    conv_w = jnp.concatenate([p["o_conv_w"], jnp.zeros((HALO - CONV_K, HALF), F32)], axis=0)
    odd_p = (row(p["o_sgu_norm_g"]), row(p["o_sgu_norm_b"]), p["o_sgu_w"], sgu_bb, conv_w,
             row(p["o_conv_b"]), row(p["o_conv_norm_g"]), row(p["o_conv_norm_b"]))

    z0 = _mm_in(h0, wg_e_in, "mm_z0")
    ycat0 = _pool_fwd(z0, pool_w_bf, row(p["e_pool_scale"]))
    qkv = _qkv_prep(z0, tabs)
    ycat0, og, lg = _attn_fwd(z0, qkv, ycat0)
    w_out_e, w_out_o = wg_e_out.reshape(2048, D), wg_o_out.reshape(2048, D)
    y0 = _mm_out(ycat0, w_out_e, "mm_y0", h0_sib)
    x1, h1 = _post0_fwd(x, y0, row(p["e_post_norm"]), row(p["o_pre_norm"]))
    z1 = _mm_in(h1, wg_o_in, "mm_z1")
    ycat1 = _odd_fwd(z1, *odd_p)
    y1 = _mm_out(ycat1, w_out_o, "mm_y1")

    g = {}
    loss, dx2, dy1, g["o_post_norm"] = _post1_bwd(y1, x1, target, row(p["o_post_norm"]))
    loss = lax.psum(loss[0, 0], ("x", "y", "c"))
    parts = {}
    dw = _mm_out_dw(ycat1, dy1, "mm_dwout1").reshape(NDEV, 256, D)
    parts["o_w_out"], = _sc_exchange("scatter_o_w_out", 2, [dw], True)
    dycat1 = _mm_out_dx(dy1, w_out_o, "mm_dycat1", (dw, loss.reshape(1, 1)))
    dz1, ddc, g["o_sgu_w"], d_sgu_bb, g["o_sgu_norm_g"], g["o_sgu_norm_b"], g["o_conv_norm_g"], \
        g["o_conv_norm_b"], g["o_conv_b"] = _odd_bwd_a(z1, dycat1, *odd_p)
    dz1, d_conv_w = _odd_bwd_b(z1, ddc, dz1, conv_w)
    g["o_sgu_b"] = d_sgu_bb[:, :, 0]
    g["o_conv_w"] = d_conv_w[:CONV_K]
    grads, deltas, new_m, new_v = {}, {}, {}, {}

    def adam(n, dep):
        grads[n], deltas[n], new_m[n], new_v[n] = _adam_reduce(parts[n], w[n], m[n], v[n], "adam_" + n, dep)
        return new_v[n]

    pin = adam("o_w_out", d_conv_w)
    dw = _mm_in_dw(h1, dz1, ODD_IN // NDEV, "mm_dwin1", pin)
    parts["o_w_in"], = _sc_exchange("scatter_o_w_in", 3, [dw], True)
    dh1 = _mm_in_dx(dz1, wg_o_in, "mm_dh1", dw)
    dx1, dy0, g["o_pre_norm"], g["e_post_norm"] = _mid_bwd(dx2, dh1, x1, y0, row(p["o_pre_norm"]),
                                                           row(p["e_post_norm"]))
    dw = _mm_out_dw(ycat0, dy0, "mm_dwout0").reshape(NDEV, 256, D)
    parts["e_w_out"], = _sc_exchange("scatter_e_w_out", 4, [dw], True)
    dycat0 = _mm_out_dx(dy0, w_out_e, "mm_dycat0", dw)
    dz0, g["e_pool_w"], g["e_pool_scale"] = _pool_bwd(z0, dycat0, pool_w_bf, row(p["e_pool_scale"]))
    dz0 = _attn_bwd(z0, qkv, og, lg, dycat0, tabs, dz0)
    late = [n for n in SMALL if n not in ("e_pre_norm", "o_sgu_b")] + ["o_sgu_b"]
    pin = adam("e_w_out", adam("o_w_in", dz0))
    nb = EVEN_IN // NDEV
    dz0_sib = _sc_sibling_exchange(
        "swap_dz0", 9, dz0, (S, 4 * nb),
        lambda c, src, out: [(src.at[:, pl.ds((2 * j + 1 - c) * nb, nb)], out.at[:, pl.ds(j * nb, nb)])
                             for j in range(4)])
    dw = _mm_pair_dw(h0, dz0, h0_sib, dz0_sib, nb, "mm_dwin0", pin)
    parts["e_w_in"] = _sc_chip_scatter("scatter_e_w_in", 5, dw)
    recv_small, = _sc_exchange("gather_small_grads", 6, [_pack([g[n].reshape(SMALL[n][0]) for n in late], 512)], False)
    dh0 = _mm_in_dx(dz0, wg_e_in, "mm_dh0", dw)
    grad_x, g["e_pre_norm"] = _pre0_bwd(dx1, dh0, x, row(p["e_pre_norm"]))
    last, = _sc_exchange("gather_e_pre_norm_grad", 7, [g["e_pre_norm"].reshape(16, 128)], False)

    rows = [int(np.prod(SMALL[n][0])) // 128 for n in late]
    summed = dict(zip(late, _sum_unpack(recv_small, rows, "sum_small_grads")))
    pin = adam("e_w_in", grad_x)
    summed["e_pre_norm"] = _sum_parts(last, "sum_e_pre_norm_grad", pin)
    names = list(SMALL)
    views = [_small_views(n) for n in names]
    mine = lambda src: [src[n].reshape(vw[0]) for n, vw in zip(names, views)]
    res = _adam_small(mine(w), [summed[n].reshape(vw[1]) for n, vw in zip(names, views)], [vw[2] for vw in views],
                      mine(m), mine(v), "adam_small")
    for n, out in zip(names, res):
        grads[n], deltas[n], new_m[n], new_v[n] = [t.reshape(_shard_shape(n)) for t in out]

    lead = lambda a: a[None]
    return (loss, grad_x[None], *[lead(grads[n]) for n in WEIGHTS], *[lead(deltas[n]) for n in WEIGHTS],
            *[lead(new_m[n]) for n in WEIGHTS], *[lead(new_v[n]) for n in WEIGHTS])
```

```python
import functools

import numpy as np
import jax
import jax.numpy as jnp
from jax import lax
from jax.experimental import pallas as pl
from jax.experimental.pallas import tpu as pltpu
from jax.experimental.pallas import tpu_sc as plsc

F32 = jnp.float32
BF16 = jnp.bfloat16

S = 2048
D = 2048
NDEV = 8
EPS = 1e-6
NEG = -1e30
HEAD_DIM = 128
ROT_DIM = 32
ROPE_THETA = 500000.0
PATTERNS = ((128, 1), (512, 4), (2048, 16))
BLK = 128
EVEN_IN = 12288
ODD_IN = 6144
HALF = 1024
CONV_K = 31
HALO = 32
TR = 256
SUB = 32

ADAM_LR = 0.001
ADAM_B1 = 0.9
ADAM_B2 = 0.999
ADAM_EPS = 1e-08
ADAM_WD = 0.01
ADAM_STEP = 10

VMEM_BIG = 56 * 1024 * 1024
MESH = pl.DeviceIdType.MESH

NN = (((1,), (0,)), ((), ()))
NT = (((1,), (1,)), ((), ()))
TN = (((0,), (0,)), ((), ()))


def _dot(a, b, dn=NN):
    return lax.dot_general(a, b, dn, preferred_element_type=F32)


def _sigmoid(x):
    return 1.0 / (1.0 + jnp.exp(-x))


def _silu_and_grad(x):
    sg = _sigmoid(x)
    return x * sg, sg * (1.0 + x * (1.0 - sg))


def _params(sem, vmem=None):
    return pltpu.CompilerParams(dimension_semantics=sem, vmem_limit_bytes=vmem)


ANY_SPEC = pl.BlockSpec(memory_space=pl.ANY)


def _matmul(a, b, *, dn, grid, a_spec, b_spec, o_spec, out_shape, out_dtype, acc_shape, name, dep=None):
    nk = grid[2]
    deps = [] if dep is None else list(dep) if isinstance(dep, (tuple, list)) else [dep]

    def body(a_ref, b_ref, *rest):
        o_ref, acc = rest[len(deps)], rest[len(deps) + 1:]
        if nk == 1:
            o_ref[...] = _dot(a_ref[...], b_ref[...], dn).astype(o_ref.dtype)
            return
        acc_ref = acc[0]
        k = pl.program_id(2)

        @pl.when(k == 0)
        def _():
            acc_ref[...] = jnp.zeros_like(acc_ref)

        acc_ref[...] += _dot(a_ref[...], b_ref[...], dn)

        @pl.when(k == nk - 1)
        def _():
            o_ref[...] = acc_ref[...].astype(o_ref.dtype)

    return pl.pallas_call(
        body, grid=grid, in_specs=[a_spec, b_spec] + [ANY_SPEC] * len(deps), out_specs=o_spec,
        out_shape=jax.ShapeDtypeStruct(out_shape, out_dtype),
        scratch_shapes=[] if nk == 1 else [pltpu.VMEM(acc_shape, F32)],
        compiler_params=_params(("parallel", "parallel", "arbitrary"), VMEM_BIG), name=name,
    )(a, b, *deps)


TM = 2048


def _mm_in(h, wg, name):
    nb = wg.shape[2]
    tn = 512 if nb % 512 == 0 else nb
    per = nb // tn
    return _matmul(
        h, wg, dn=NN, grid=(S // TM, NDEV * per, 1),
        a_spec=pl.BlockSpec((TM, D), lambda i, j, k: (i, 0)),
        b_spec=pl.BlockSpec((None, D, tn), lambda i, j, k: (j // per, 0, j % per)),
        o_spec=pl.BlockSpec((TM, tn), lambda i, j, k: (i, j)),
        out_shape=(S, NDEV * nb), out_dtype=F32, acc_shape=(TM, tn), name=name)


def _mm_in_dx(dz, wg, name, dep=None):
    nb = wg.shape[2]
    return _matmul(
        dz, wg, dn=NT, grid=(S // TM, D // 1024, NDEV),
        a_spec=pl.BlockSpec((TM, nb), lambda i, j, k: (i, k)),
        b_spec=pl.BlockSpec((None, 1024, nb), lambda i, j, k: (k, j, 0)),
        o_spec=pl.BlockSpec((TM, 1024), lambda i, j, k: (i, j)),
        out_shape=(S, D), out_dtype=F32, acc_shape=(TM, 1024), name=name, dep=dep)


def _mm_in_dw(h, dz, nb, name, dep=None):
    tn = 512 if nb % 512 == 0 else nb
    per = nb // tn
    return _matmul(
        h, dz, dn=TN, grid=(D // TM, NDEV * per, 1),
        a_spec=pl.BlockSpec((S, TM), lambda i, j, k: (0, i)),
        b_spec=pl.BlockSpec((S, tn), lambda i, j, k: (0, j)),
        o_spec=pl.BlockSpec((None, TM, tn), lambda i, j, k: (j // per, i, j % per)),
        out_shape=(NDEV, D, nb), out_dtype=BF16, acc_shape=(TM, tn), name=name, dep=dep)


def _mm_out(yc, w, name, dep=None):
    return _matmul(
        yc, w, dn=NN, grid=(S // TM, D // 512, 1),
        a_spec=pl.BlockSpec((TM, 2048), lambda i, j, k: (i, 0)),
        b_spec=pl.BlockSpec((2048, 512), lambda i, j, k: (0, j)),
        o_spec=pl.BlockSpec((TM, 512), lambda i, j, k: (i, j)),
        out_shape=(S, D), out_dtype=F32, acc_shape=(TM, 512), name=name, dep=dep)


def _mm_out_dx(dy, w, name, dep=None):
    return _matmul(
        dy, w, dn=NT, grid=(S // TM, 2048 // 512, 1),
        a_spec=pl.BlockSpec((TM, D), lambda i, j, k: (i, 0)),
        b_spec=pl.BlockSpec((512, D), lambda i, j, k: (j, 0)),
        o_spec=pl.BlockSpec((TM, 512), lambda i, j, k: (i, j)),
        out_shape=(S, 2048), out_dtype=F32, acc_shape=(TM, 512), name=name, dep=dep)


def _mm_out_dw(yc, dy, name):
    return _matmul(
        yc, dy, dn=TN, grid=(2048 // TM, D // 512, 1),
        a_spec=pl.BlockSpec((S, TM), lambda i, j, k: (0, i)),
        b_spec=pl.BlockSpec((S, 512), lambda i, j, k: (0, j)),
        o_spec=pl.BlockSpec((TM, 512), lambda i, j, k: (i, j)),
        out_shape=(2048, D), out_dtype=BF16, acc_shape=(TM, 512), name=name)


def _row_spec(w=D):
    return pl.BlockSpec((TR, w), lambda i: (i, 0))


def _vec_spec(w=D):
    return pl.BlockSpec((1, w), lambda i: (0, 0))


def _rms_stats(x):
    r = lax.rsqrt(jnp.mean(x * x, axis=-1, keepdims=True) + EPS)
    return x * r, r


def _rms_bwd(dn, xhat, r, g):
    dxh = dn * g
    return r * (dxh - xhat * jnp.mean(dxh * xhat, axis=-1, keepdims=True))


def _acc_rows(ref, val, i):
    s = jnp.sum(val, axis=0, keepdims=True)

    @pl.when(i == 0)
    def _():
        ref[...] = s

    @pl.when(i > 0)
    def _():
        ref[...] += s


def _pre0_fwd(x, g, dep=None):
    deps = [] if dep is None else [dep]

    def body(x_ref, g_ref, *rest):
        xhat, _ = _rms_stats(x_ref[...])
        rest[-1][...] = (xhat * g_ref[...]).astype(BF16)

    return pl.pallas_call(
        body, grid=(S // TR,), in_specs=[_row_spec(), _vec_spec()] + [ANY_SPEC] * len(deps), out_specs=_row_spec(),
        out_shape=jax.ShapeDtypeStruct((S, D), BF16), compiler_params=_params(("parallel",)), name="pre0_fwd",
    )(x, g, *deps)


def _post0_fwd(x, y0, g_post, g_pre1):
    def body(x_ref, y_ref, gp_ref, g1_ref, x1_ref, h1_ref):
        yhat, _ = _rms_stats(y_ref[...])
        x1 = x_ref[...] + yhat * gp_ref[...]
        x1_ref[...] = x1
        xhat, _ = _rms_stats(x1)
        h1_ref[...] = (xhat * g1_ref[...]).astype(BF16)

    return pl.pallas_call(
        body, grid=(S // TR,), in_specs=[_row_spec(), _row_spec(), _vec_spec(), _vec_spec()],
        out_specs=[_row_spec(), _row_spec()],
        out_shape=[jax.ShapeDtypeStruct((S, D), F32), jax.ShapeDtypeStruct((S, D), BF16)],
        compiler_params=_params(("parallel",)), name="post0_fwd",
    )(x, y0, g_post, g_pre1)


def _post1_bwd(y1, x1, target, g_post):
    def body(y_ref, x1_ref, t_ref, g_ref, loss_ref, dx2_ref, dy_ref, dg_ref):
        i = pl.program_id(0)
        yhat, r = _rms_stats(y_ref[...])
        g = g_ref[...]
        err = x1_ref[...] + yhat * g - t_ref[...]
        part = jnp.sum(jnp.sum(err * err, axis=-1, keepdims=True), axis=0, keepdims=True) * (0.5 / D)
        _acc_rows(loss_ref, jnp.broadcast_to(part, (1, 128)), i)
        dx2 = err * (1.0 / D)
        dx2_ref[...] = dx2
        _acc_rows(dg_ref, dx2 * yhat, i)
        dy_ref[...] = _rms_bwd(dx2, yhat, r, g).astype(BF16)

    return pl.pallas_call(
        body, grid=(S // TR,), in_specs=[_row_spec(), _row_spec(), _row_spec(), _vec_spec()],
        out_specs=[_vec_spec(128), _row_spec(), _row_spec(), _vec_spec()],
        out_shape=[jax.ShapeDtypeStruct((1, 128), F32), jax.ShapeDtypeStruct((S, D), F32),
                   jax.ShapeDtypeStruct((S, D), BF16), jax.ShapeDtypeStruct((1, D), F32)],
        compiler_params=_params(("arbitrary",)), name="post1_bwd",
    )(y1, x1, target, g_post)


def _mid_bwd(dx2, dh1, x1, y0, g_pre1, g_post0):
    def body(dx2_ref, dh_ref, x1_ref, y_ref, g1_ref, gp_ref, dx1_ref, dy_ref, dg1_ref, dgp_ref):
        i = pl.program_id(0)
        xhat, r1 = _rms_stats(x1_ref[...])
        dh = dh_ref[...]
        _acc_rows(dg1_ref, dh * xhat, i)
        dx1 = dx2_ref[...] + _rms_bwd(dh, xhat, r1, g1_ref[...])
        dx1_ref[...] = dx1
        yhat, r0 = _rms_stats(y_ref[...])
        _acc_rows(dgp_ref, dx1 * yhat, i)
        dy_ref[...] = _rms_bwd(dx1, yhat, r0, gp_ref[...]).astype(BF16)

    return pl.pallas_call(
        body, grid=(S // TR,),
        in_specs=[_row_spec(), _row_spec(), _row_spec(), _row_spec(), _vec_spec(), _vec_spec()],
        out_specs=[_row_spec(), _row_spec(), _vec_spec(), _vec_spec()],
        out_shape=[jax.ShapeDtypeStruct((S, D), F32), jax.ShapeDtypeStruct((S, D), BF16),
                   jax.ShapeDtypeStruct((1, D), F32), jax.ShapeDtypeStruct((1, D), F32)],
        compiler_params=_params(("arbitrary",)), name="mid_bwd",
    )(dx2, dh1, x1, y0, g_pre1, g_post0)


def _pre0_bwd(dx1, dh0, x, g):
    def body(dx1_ref, dh_ref, x_ref, g_ref, gx_ref, dg_ref):
        i = pl.program_id(0)
        xhat, r = _rms_stats(x_ref[...])
        dh = dh_ref[...]
        _acc_rows(dg_ref, dh * xhat, i)
        gx_ref[...] = dx1_ref[...] + _rms_bwd(dh, xhat, r, g_ref[...])

    return pl.pallas_call(
        body, grid=(S // TR,), in_specs=[_row_spec(), _row_spec(), _row_spec(), _vec_spec()],
        out_specs=[_row_spec(), _vec_spec()],
        out_shape=[jax.ShapeDtypeStruct((S, D), F32), jax.ShapeDtypeStruct((1, D), F32)],
        compiler_params=_params(("arbitrary",)), name="pre0_bwd",
    )(dx1, dh0, x, g)


POOL_CH = 256


def _pool_apply(a, w, transpose):
    n = a.shape[0]
    row = lax.broadcasted_iota(jnp.int32, a.shape, 0)
    cnt = jnp.minimum(row + 1, w).astype(F32)
    s = a / cnt if transpose else a
    for k in (1, 2, 4, 8):
        if transpose:
            sh = jnp.where(row < n - k, pltpu.roll(s, n - k, 0), 0.0)
        else:
            sh = jnp.where(row >= k, pltpu.roll(s, k, 0), 0.0)
        s = jnp.where(w > k, s + sh, s)
    return s - a if transpose else s / cnt - a


def _pool_fwd(z0, pool_w, pool_scale):
    def body(a_ref, gate_ref, w_ref, sc_ref, out_ref):
        win = jnp.left_shift(2, pl.program_id(0))
        pooled = _pool_apply(a_ref[...], win, False)
        mixed = _dot(pooled.astype(BF16), w_ref[...])
        gate = gate_ref[...]
        out_ref[...] = (mixed * sc_ref[...] * (gate * _sigmoid(gate))).astype(BF16)

    return pl.pallas_call(
        body, grid=(4,),
        in_specs=[pl.BlockSpec((S, POOL_CH), lambda g: (0, g)), pl.BlockSpec((S, POOL_CH), lambda g: (0, 4 + g)),
                  pl.BlockSpec((None, POOL_CH, POOL_CH), lambda g: (g, 0, 0)),
                  pl.BlockSpec((1, POOL_CH), lambda g: (0, g))],
        out_specs=pl.BlockSpec((S, POOL_CH), lambda g: (0, g)),
        out_shape=jax.ShapeDtypeStruct((S, 2048), BF16),
        compiler_params=_params(("parallel",), VMEM_BIG), name="pool_fwd",
    )(z0, z0, pool_w, pool_scale)


def _pool_bwd(z0, dycat, pool_w, pool_scale):
    def body(a_ref, gate_ref, dy_ref, w_ref, sc_ref, da_ref, dgate_ref, dw_ref, dsc_ref):
        win = jnp.left_shift(2, pl.program_id(0))
        pooled = _pool_apply(a_ref[...], win, False).astype(BF16)
        w = w_ref[...]
        mixed = _dot(pooled, w)
        silu, dsilu = _silu_and_grad(gate_ref[...])
        dy = dy_ref[...]
        sc = sc_ref[...]
        dgate_ref[...] = (dy * (mixed * sc) * dsilu).astype(BF16)
        dms = dy * silu
        dsc_ref[...] = jnp.sum(dms * mixed, axis=0, keepdims=True)
        dmixed = (dms * sc).astype(BF16)
        dw_ref[...] = _dot(pooled, dmixed, TN)
        dpooled = _dot(dmixed, w, NT)
        da_ref[...] = _pool_apply(dpooled, win, True).astype(BF16)

    slab = lambda off: pl.BlockSpec((S, POOL_CH), lambda g: (0, off + g))
    return pl.pallas_call(
        body, grid=(4,),
        in_specs=[slab(0), slab(4), slab(0), pl.BlockSpec((None, POOL_CH, POOL_CH), lambda g: (g, 0, 0)),
                  pl.BlockSpec((1, POOL_CH), lambda g: (0, g))],
        out_specs=[slab(0), slab(0), pl.BlockSpec((None, POOL_CH, POOL_CH), lambda g: (g, 0, 0)),
                   pl.BlockSpec((1, POOL_CH), lambda g: (0, g))],
        out_shape=[jax.ShapeDtypeStruct((S, HALF), BF16), jax.ShapeDtypeStruct((S, HALF), BF16),
                   jax.ShapeDtypeStruct((4, POOL_CH, POOL_CH), F32), jax.ShapeDtypeStruct((1, HALF), F32)],
        compiler_params=_params(("parallel",), VMEM_BIG), name="pool_bwd",
    )(z0, z0, dycat, pool_w, pool_scale)


Q_COL, K_COL, V_COL, BG_COL = 2048 // 128, 5120 // 128, 8192 // 128, 11264 // 128
SCALE = HEAD_DIM ** -0.5


def _rope_tables():
    pos = jnp.arange(S, dtype=F32)
    inv_freq = jnp.power(ROPE_THETA, -jnp.arange(0, ROT_DIM, 2, dtype=F32) / ROT_DIM)
    ang = pos[:, None] * inv_freq[None, :]
    cos, sin = jnp.cos(ang), jnp.sin(ang)
    half = ROT_DIM // 2
    zeros = jnp.zeros((S, HEAD_DIM - ROT_DIM), F32)
    c = jnp.concatenate([cos, cos, jnp.ones((S, HEAD_DIM - ROT_DIM), F32)], axis=1)
    a = jnp.concatenate([-sin, jnp.zeros((S, half), F32), zeros], axis=1)
    b = jnp.concatenate([jnp.zeros((S, half), F32), sin, zeros], axis=1)
    return c, a, b


def _rope(t, c, a, b):
    half = ROT_DIM // 2
    return t * c + pltpu.roll(t, HEAD_DIM - half, 1) * a + pltpu.roll(t, half, 1) * b


def _rope_t(d, c, a, b):
    half = ROT_DIM // 2
    return d * c + pltpu.roll(d * a, half, 1) + pltpu.roll(d * b, HEAD_DIM - half, 1)


def _deinterleave(dst, src, dil, cast=None, dst_off=0):
    length = S // dil
    for r in range(dil):
        v = src[...] if dil == 1 else src[pl.ds(r, length, stride=dil), :]
        dst[dst_off + r * length:dst_off + (r + 1) * length, :] = v if cast is None else v.astype(cast)


def _interleave(dst, src, dil, src_off=0):
    length = S // dil
    for r in range(dil):
        if dil == 1:
            dst[...] = src[src_off:src_off + S, :]
        else:
            dst[pl.ds(r, length, stride=dil), :] = src[src_off + r * length:src_off + (r + 1) * length, :]


CU = 4
NUNITS = S // BLK
B_QK = (((2,), (2,)), ((0,), (0,)))
B_PV = (((2,), (1,)), ((0,), (0,)))
B_TN = (((1,), (1,)), ((0,), (0,)))


def _blocks(ref, first):
    return ref[first * BLK:(first + CU) * BLK, :].reshape(CU, BLK, HEAD_DIM)


def _chunk_scores(u0, nb, qd, kdp):
    q = _blocks(qd, u0)
    row = lax.broadcasted_iota(jnp.int32, (CU, BLK, BLK), 1)
    col = lax.broadcasted_iota(jnp.int32, (CU, BLK, BLK), 2)
    s_own = jnp.where(col <= row, _dot(q, _blocks(kdp, u0 + 1), B_QK) * SCALE, NEG)
    if nb == 1:
        return q, s_own, None
    unit = lax.broadcasted_iota(jnp.int32, (CU, BLK, BLK), 0) + u0
    s_prev = jnp.where((col >= row) & ((unit % nb) != 0), _dot(q, _blocks(kdp, u0), B_QK) * SCALE, NEG)
    return q, s_own, s_prev


def _qkv_prep(z0, tabs):
    def body(q_ref, k_ref, v_ref, c_ref, a_ref, b_ref, qo_ref, ko_ref, vo_ref, tmp):
        p = pl.program_id(1)
        for gi, (_, dil) in enumerate(PATTERNS):
            @pl.when(p == gi)
            def _(dil=dil):
                c, a, b = c_ref[...], a_ref[...], b_ref[...]
                tmp[...] = _rope(q_ref[...], c, a, b)
                _deinterleave(qo_ref, tmp, dil, BF16)
                tmp[...] = _rope(k_ref[...], c, a, b)
                _deinterleave(ko_ref, tmp, dil, BF16)
                _deinterleave(vo_ref, v_ref, dil, BF16)

    tab = pl.BlockSpec((S, HEAD_DIM), lambda h, p: (0, 0))
    out = pl.BlockSpec((S, HEAD_DIM), lambda h, p: (0, p * 8 + h))
    return pl.pallas_call(
        body, grid=(8, 3), in_specs=[_head_spec(Q_COL), _head_spec(K_COL), _head_spec(V_COL), tab, tab, tab],
        out_specs=[out, out, out], out_shape=[jax.ShapeDtypeStruct((S, 3072), BF16)] * 3,
        scratch_shapes=[pltpu.VMEM((S, HEAD_DIM), F32)],
        compiler_params=_params(("parallel", "arbitrary"), VMEM_BIG), name="qkv_prep",
    )(z0, z0, z0, *tabs)


def _pad_copy(dst, src):
    dst[0:BLK, :] = jnp.zeros((BLK, HEAD_DIM), dst.dtype)
    dst[BLK:BLK + S, :] = src[...]


def _attn_group_fwd(dil, qd, kd_ref, vd_ref, kdp, vdp, od, ld, og, lg):
    nb = S // dil // BLK
    _pad_copy(kdp, kd_ref)
    _pad_copy(vdp, vd_ref)
    for u0 in range(0, NUNITS, CU):
        _, s_own, s_prev = _chunk_scores(u0, nb, qd, kdp)
        m = jnp.max(s_own, axis=2, keepdims=True)
        if s_prev is not None:
            m = jnp.maximum(m, jnp.max(s_prev, axis=2, keepdims=True))
        p_own = jnp.exp(s_own - m)
        den = jnp.sum(p_own, axis=2, keepdims=True)
        acc = _dot(p_own.astype(BF16), _blocks(vdp, u0 + 1), B_PV)
        if s_prev is not None:
            p_prev = jnp.exp(s_prev - m)
            den = den + jnp.sum(p_prev, axis=2, keepdims=True)
            acc = acc + _dot(p_prev.astype(BF16), _blocks(vdp, u0), B_PV)
        rows = slice(u0 * BLK, (u0 + CU) * BLK)
        od[rows, :] = (acc / den).reshape(CU * BLK, HEAD_DIM)
        ld[rows, :] = jnp.broadcast_to(m + jnp.log(den), (CU, BLK, HEAD_DIM)).reshape(CU * BLK, HEAD_DIM)
    _interleave(og, od, dil)
    _interleave(lg, ld, dil)


def _group_weights(lgs):
    l0, l1, l2 = lgs[0][...], lgs[1][...], lgs[2][...]
    mx = jnp.maximum(l0, jnp.maximum(l1, l2))
    e0, e1, e2 = jnp.exp(l0 - mx), jnp.exp(l1 - mx), jnp.exp(l2 - mx)
    den = e0 + e1 + e2
    return e0 / den, e1 / den, e2 / den


def _head_spec(base, ngroups_axis=True):
    return pl.BlockSpec((S, HEAD_DIM), lambda h, p: (0, base + (p % 3) * 8 + h))


def _slab(dtype=F32, rows=S):
    return pltpu.VMEM((rows, HEAD_DIM), dtype)


def _attn_fwd(z0, qkv, ycat):
    def body(q_ref, k_ref, v_ref, gate_ref, ycat_ref, out_ref, og_ref, lg_ref,
             kdp, vdp, od, ld, og0, og1, og2, lg0, lg1, lg2):
        del ycat_ref
        p = pl.program_id(1)
        ogs, lgs = (og0, og1, og2), (lg0, lg1, lg2)
        for gi, (_, dil) in enumerate(PATTERNS):
            @pl.when(p == gi)
            def _(gi=gi, dil=dil):
                _attn_group_fwd(dil, q_ref, k_ref, v_ref, kdp, vdp, od, ld, ogs[gi], lgs[gi])
                og_ref[...] = ogs[gi][...]
                lg_ref[...] = lgs[gi][...]

        @pl.when(p == 2)
        def _():
            w0, w1, w2 = _group_weights(lgs)
            o = w0 * og0[...] + w1 * og1[...] + w2 * og2[...]
            gate = gate_ref[...]
            out_ref[...] = (o * (gate * _sigmoid(gate))).astype(BF16)

    grp = pl.BlockSpec((S, HEAD_DIM), lambda h, p: (0, p * 8 + h))
    return pl.pallas_call(
        body, grid=(8, 3),
        in_specs=[grp, grp, grp, pl.BlockSpec((S, HEAD_DIM), lambda h, p: (0, BG_COL + h)), ANY_SPEC],
        out_specs=[pl.BlockSpec((S, HEAD_DIM), lambda h, p: (0, 8 + h)), grp, grp],
        out_shape=[jax.ShapeDtypeStruct((S, 2048), BF16), jax.ShapeDtypeStruct((S, 3072), F32),
                   jax.ShapeDtypeStruct((S, 3072), F32)],
        scratch_shapes=[_slab(BF16, S + BLK), _slab(BF16, S + BLK)] + [_slab() for _ in range(8)],
        input_output_aliases={4: 0},
        compiler_params=_params(("parallel", "arbitrary"), VMEM_BIG), name="attn_fwd",
    )(*qkv, z0, ycat)


def _attn_bwd(z0, qkv, og, lg, dycat, tabs):
    def body(q_ref, k_ref, v_ref, gate_ref, dy_ref, c_ref, a_ref, b_ref,
             og0_ref, og1_ref, og2_ref, lg0_ref, lg1_ref, lg2_ref,
             dq_ref, dk_ref, dv_ref, dbg_ref,
             tmp, kd, vd, ld, dg0, dg1, dg2, cg0, cg1, cg2, dod, cd, dqd, dkd, dvd):
        p = pl.program_id(1)
        ogs, lgs, dgs, cgs = (og0_ref, og1_ref, og2_ref), (lg0_ref, lg1_ref, lg2_ref), (dg0, dg1, dg2), (cg0, cg1, cg2)

        @pl.when(p == 0)
        def _():
            w = _group_weights(lgs)
            o = w[0] * ogs[0][...] + w[1] * ogs[1][...] + w[2] * ogs[2][...]
            silu, dsilu = _silu_and_grad(gate_ref[...])
            dy = dy_ref[...]
            dbg_ref[...] = (dy * o * dsilu).astype(BF16)
            do = dy * silu
            dwbar = jnp.sum(do * o, axis=1, keepdims=True)
            for gi in range(3):
                dgs[gi][...] = w[gi] * do
                cgs[gi][...] = -w[gi] * dwbar

        for gi, (_, dil) in enumerate(PATTERNS):
            @pl.when(p == 1 + gi)
            def _(gi=gi, dil=dil):
                nb = S // dil // BLK
                qd = q_ref
                c, a, b = c_ref[...], a_ref[...], b_ref[...]
                _pad_copy(kd, k_ref)
                _pad_copy(vd, v_ref)
                _deinterleave(dod, dgs[gi], dil, BF16)
                _deinterleave(ld, lgs[gi], dil)
                _deinterleave(cd, cgs[gi], dil)
                dkd[...] = jnp.zeros_like(dkd)
                dvd[...] = jnp.zeros_like(dvd)
                flat = lambda t: t.reshape(CU * BLK, HEAD_DIM)
                for u0 in range(0, NUNITS, CU):
                    q, s_own, s_prev = _chunk_scores(u0, nb, qd, kd)
                    lse, cv, do = _blocks(ld, u0), _blocks(cd, u0), _blocks(dod, u0)
                    own = slice((u0 + 1) * BLK, (u0 + 1 + CU) * BLK)
                    p_own = jnp.exp(s_own - lse)
                    ds_own = (p_own * (_dot(do, _blocks(vd, u0 + 1), B_QK) + cv) * SCALE).astype(BF16)
                    dq = _dot(ds_own, _blocks(kd, u0 + 1), B_PV)
                    dkd[own, :] += flat(_dot(ds_own, q, B_TN))
                    dvd[own, :] += flat(_dot(p_own.astype(BF16), do, B_TN))
                    if s_prev is not None:
                        prev = slice(u0 * BLK, (u0 + CU) * BLK)
                        p_prev = jnp.exp(s_prev - lse)
                        ds_prev = (p_prev * (_dot(do, _blocks(vd, u0), B_QK) + cv) * SCALE).astype(BF16)
                        dq = dq + _dot(ds_prev, _blocks(kd, u0), B_PV)
                        dkd[prev, :] += flat(_dot(ds_prev, q, B_TN))
                        dvd[prev, :] += flat(_dot(p_prev.astype(BF16), do, B_TN))
                    dqd[u0 * BLK:(u0 + CU) * BLK, :] = flat(dq)
                _interleave(tmp, dqd, dil)
                dq_ref[...] = _rope_t(tmp[...], c, a, b).astype(BF16)
                _interleave(tmp, dkd, dil, BLK)
                dk_ref[...] = _rope_t(tmp[...], c, a, b).astype(BF16)
                _interleave(tmp, dvd, dil, BLK)
                dv_ref[...] = tmp[...].astype(BF16)

    tab = pl.BlockSpec((S, HEAD_DIM), lambda h, p: (0, 0))
    hspec = lambda base: pl.BlockSpec((S, HEAD_DIM), lambda h, p: (0, base + h))
    gspec = pl.BlockSpec((S, HEAD_DIM), lambda h, p: (0, jnp.maximum(p - 1, 0) * 8 + h))
    return pl.pallas_call(
        body, grid=(8, 4),
        in_specs=[gspec, gspec, gspec, hspec(BG_COL), hspec(8), tab, tab, tab,
                  hspec(0), hspec(8), hspec(16), hspec(0), hspec(8), hspec(16)],
        out_specs=[gspec, gspec, gspec, hspec(0)],
        out_shape=[jax.ShapeDtypeStruct((S, 3072), BF16)] * 3 + [jax.ShapeDtypeStruct((S, HALF), BF16)],
        scratch_shapes=[_slab(), _slab(BF16, S + BLK), _slab(BF16, S + BLK), _slab()] + [_slab() for _ in range(6)]
                       + [_slab(BF16), _slab(), _slab(), _slab(F32, S + BLK), _slab(F32, S + BLK)],
        compiler_params=_params(("parallel", "arbitrary"), VMEM_BIG), name="attn_bwd",
    )(*qkv, z0, dycat, *tabs, og, og, og, lg, lg, lg)


SGU_CH = 256
NCHUNK = TR // 128


def _ln_stats(x):
    mu = jnp.mean(x, axis=-1, keepdims=True)
    xc = x - mu
    r = lax.rsqrt(jnp.mean(xc * xc, axis=-1, keepdims=True) + EPS)
    return xc * r, r


def _ln_bwd(dy, xhat, r, g):
    dxh = dy * g
    return r * (dxh - jnp.mean(dxh, axis=-1, keepdims=True) - xhat * jnp.mean(dxh * xhat, axis=-1, keepdims=True))


def _tril_bf16(w):
    row = lax.broadcasted_iota(jnp.int32, w.shape, 0)
    col = lax.broadcasted_iota(jnp.int32, w.shape, 1)
    return jnp.where(row >= col, w, 0.0).astype(BF16)


def _sgu_gate(vn_s, s_s, w_ref, bb_ref):
    for h in range(4):
        wm = _tril_bf16(w_ref[h])
        bias = bb_ref[h]
        for ch in range(NCHUNK):
            rows, cols = slice(ch * 128, (ch + 1) * 128), slice(h * SGU_CH, (h + 1) * SGU_CH)
            s_s[rows, cols] = _dot(wm, vn_s[rows, cols]) + jnp.concatenate([bias, bias], axis=1)


WIN = HALO + TR
SUBL = 8


def _shifted_copies(dst, src):
    dst[0] = src[...]
    for b in range(1, SUBL):
        dst[b, 0:WIN - SUBL, :] = src[pl.ds(b, WIN - SUBL), :]


def _rows_at(copies, off, n):
    return copies[off % SUBL, pl.ds(off - off % SUBL, n), :]


def _conv_fwd(i, dval_ref, dglu_ref, hval_ref, hglu_ref, cw_ref, cb_ref, xw, xr, dcs):
    halo = hval_ref[...] * _sigmoid(hglu_ref[...])
    xw[0:HALO, :] = jnp.where(i > 0, halo, 0.0)
    xw[HALO:HALO + TR, :] = dval_ref[...] * _sigmoid(dglu_ref[...])
    _shifted_copies(xr, xw)
    for rb in range(TR // SUB):
        acc = jnp.broadcast_to(cb_ref[...], (SUB, HALF))
        for k in range(CONV_K):
            acc = acc + cw_ref[k:k + 1, :] * _rows_at(xr, rb * SUB + HALO - (CONV_K - 1) + k, SUB)
        dcs[rb * SUB:(rb + 1) * SUB, :] = acc


def _odd_in_specs():
    col = lambda j: pl.BlockSpec((TR, HALF), lambda i, *_: (i, j))
    prev = lambda j: pl.BlockSpec((HALO, HALF), lambda i, *_: (jnp.maximum(i * (TR // HALO) - 1, 0), j))
    return [col(0), col(1), col(2), col(3), col(4), col(5), prev(3), prev(4)]


def _full_spec(shape):
    return pl.BlockSpec(shape, lambda i, *_: (0,) * len(shape))


def _odd_fwd(z1, sgu_g, sgu_b, sgu_w, sgu_bb, conv_w, conv_b, cn_g, cn_b):
    def body(u_ref, v_ref, cg_ref, dval_ref, dglu_ref, dgate_ref, hval_ref, hglu_ref,
             g_ref, b_ref, w_ref, bb_ref, cw_ref, cb_ref, cng_ref, cnb_ref, out_ref, vn_s, s_s, xw, dcs, xr):
        i = pl.program_id(0)
        vhat, _ = _ln_stats(v_ref[...])
        vn_s[...] = (vhat * g_ref[...] + b_ref[...]).astype(BF16)
        _sgu_gate(vn_s, s_s, w_ref, bb_ref)
        cg = cg_ref[...]
        out_ref[:, 0:HALF] = (u_ref[...] * s_s[...] * (cg * _sigmoid(cg))).astype(BF16)
        _conv_fwd(i, dval_ref, dglu_ref, hval_ref, hglu_ref, cw_ref, cb_ref, xw, xr, dcs)
        dhat, _ = _ln_stats(dcs[...])
        dn = dhat * cng_ref[...] + cnb_ref[...]
        dgate = dgate_ref[...]
        out_ref[:, HALF:2 * HALF] = ((dn * _sigmoid(dn)) * (dgate * _sigmoid(dgate))).astype(BF16)

    vec = _full_spec((1, HALF))
    return pl.pallas_call(
        body, grid=(S // TR,),
        in_specs=_odd_in_specs() + [vec, vec, _full_spec((4, 128, 128)), _full_spec((4, 128, 128)),
                                    _full_spec((HALO, HALF)), vec, vec, vec],
        out_specs=pl.BlockSpec((TR, 2048), lambda i: (i, 0)),
        out_shape=jax.ShapeDtypeStruct((S, 2048), BF16),
        scratch_shapes=[pltpu.VMEM((TR, HALF), BF16), pltpu.VMEM((TR, HALF), F32),
                        pltpu.VMEM((WIN, HALF), F32), pltpu.VMEM((TR, HALF), F32), pltpu.VMEM((SUBL, WIN, HALF), F32)],
        compiler_params=_params(("parallel",), VMEM_BIG), name="odd_fwd",
    )(z1, z1, z1, z1, z1, z1, z1, z1, sgu_g, sgu_b, sgu_w, sgu_bb, conv_w, conv_b, cn_g, cn_b)


def _odd_bwd_a(z1, dycat, sgu_g, sgu_b, sgu_w, sgu_bb, conv_w, conv_b, cn_g, cn_b):
    def body(u_ref, v_ref, cg_ref, dval_ref, dglu_ref, dgate_ref, hval_ref, hglu_ref, dy_ref,
             g_ref, b_ref, w_ref, bb_ref, cw_ref, cb_ref, cng_ref, cnb_ref,
             dz_ref, ddc_ref, dw_ref, dbb_ref, dg_ref, db_ref, dcng_ref, dcnb_ref, dcb_ref,
             vn_s, s_s, xw, dcs, ds_s, dvn_s, xr):
        i = pl.program_id(0)
        vhat, rv = _ln_stats(v_ref[...])
        g = g_ref[...]
        vn_s[...] = (vhat * g + b_ref[...]).astype(BF16)
        _sgu_gate(vn_s, s_s, w_ref, bb_ref)
        silu_c, dsilu_c = _silu_and_grad(cg_ref[...])
        dyc = dy_ref[:, 0:HALF]
        u = u_ref[...]
        s = s_s[...]
        dz_ref[:, 0:HALF] = (dyc * s * silu_c).astype(BF16)
        dz_ref[:, 2 * HALF:3 * HALF] = (dyc * u * s * dsilu_c).astype(BF16)
        ds_s[...] = dyc * u * silu_c

        @pl.when(i == 0)
        def _():
            dw_ref[...] = jnp.zeros_like(dw_ref)
            dbb_ref[...] = jnp.zeros_like(dbb_ref)

        tril = lax.broadcasted_iota(jnp.int32, (128, 128), 0) >= lax.broadcasted_iota(jnp.int32, (128, 128), 1)
        for h in range(4):
            wm = _tril_bf16(w_ref[h])
            for ch in range(NCHUNK):
                rows, cols = slice(ch * 128, (ch + 1) * 128), slice(h * SGU_CH, (h + 1) * SGU_CH)
                ds = ds_s[rows, cols]
                dsb = ds.astype(BF16)
                dw_ref[h] += jnp.where(tril, _dot(dsb, vn_s[rows, cols], NT), 0.0)
                dbb_ref[h] += jnp.broadcast_to(jnp.sum(ds, axis=1, keepdims=True), (128, 128))
                dvn_s[rows, cols] = _dot(wm, dsb, TN)
        dvn = dvn_s[...]
        _acc_rows(dg_ref, dvn * vhat, i)
        _acc_rows(db_ref, dvn, i)
        dz_ref[:, HALF:2 * HALF] = _ln_bwd(dvn, vhat, rv, g).astype(BF16)

        _conv_fwd(i, dval_ref, dglu_ref, hval_ref, hglu_ref, cw_ref, cb_ref, xw, xr, dcs)
        dhat, rd = _ln_stats(dcs[...])
        cng = cng_ref[...]
        silu_n, dsilu_n = _silu_and_grad(dhat * cng + cnb_ref[...])
        silu_g, dsilu_g = _silu_and_grad(dgate_ref[...])
        dyd = dy_ref[:, HALF:2 * HALF]
        dz_ref[:, 5 * HALF:6 * HALF] = (dyd * silu_n * dsilu_g).astype(BF16)
        ddn = dyd * silu_g * dsilu_n
        _acc_rows(dcng_ref, ddn * dhat, i)
        _acc_rows(dcnb_ref, ddn, i)
        ddc = _ln_bwd(ddn, dhat, rd, cng)
        ddc_ref[...] = ddc
        _acc_rows(dcb_ref, ddc, i)

    vec = _full_spec((1, HALF))
    sq = _full_spec((4, 128, 128))
    return pl.pallas_call(
        body, grid=(S // TR,),
        in_specs=_odd_in_specs() + [pl.BlockSpec((TR, 2048), lambda i: (i, 0)),
                                    vec, vec, sq, sq, _full_spec((HALO, HALF)), vec, vec, vec],
        out_specs=[pl.BlockSpec((TR, ODD_IN), lambda i: (i, 0)), pl.BlockSpec((TR, HALF), lambda i: (i, 0)),
                   sq, sq, vec, vec, vec, vec, vec],
        out_shape=[jax.ShapeDtypeStruct((S, ODD_IN), BF16), jax.ShapeDtypeStruct((S, HALF), F32),
                   jax.ShapeDtypeStruct((4, 128, 128), F32), jax.ShapeDtypeStruct((4, 128, 128), F32)]
                  + [jax.ShapeDtypeStruct((1, HALF), F32)] * 5,
        scratch_shapes=[pltpu.VMEM((TR, HALF), BF16), pltpu.VMEM((TR, HALF), F32),
                        pltpu.VMEM((WIN, HALF), F32), pltpu.VMEM((TR, HALF), F32),
                        pltpu.VMEM((TR, HALF), F32), pltpu.VMEM((TR, HALF), F32), pltpu.VMEM((SUBL, WIN, HALF), F32)],
        compiler_params=_params(("arbitrary",), VMEM_BIG), name="odd_bwd_a",
    )(z1, z1, z1, z1, z1, z1, z1, z1, dycat, sgu_g, sgu_b, sgu_w, sgu_bb, conv_w, conv_b, cn_g, cn_b)


def _odd_bwd_b(z1, ddc, dz1, conv_w):
    nt = S // TR

    def body(dval_ref, dglu_ref, hval_ref, hglu_ref, ddc_ref, hddc_ref, cw_ref, dz_in_ref,
             dz_ref, dcw_ref, xw, dwin, dxs, xr, dr):
        del dz_in_ref
        i, j = pl.program_id(0), pl.program_id(1)
        sg = _sigmoid(dglu_ref[...])
        dval = dval_ref[...]

        @pl.when(j == 0)
        def _():
            halo = hval_ref[...] * _sigmoid(hglu_ref[...])
            xw[0:HALO, :] = jnp.where(i > 0, halo, 0.0)
            xw[HALO:HALO + TR, :] = dval * sg
            dwin[0:TR, :] = ddc_ref[...]
            dwin[TR:TR + HALO, :] = jnp.where(i < nt - 1, hddc_ref[...], 0.0)
            _shifted_copies(xr, xw)
            _shifted_copies(dr, dwin)

            @pl.when(i == 0)
            def _():
                dcw_ref[...] = jnp.zeros_like(dcw_ref)

            for rb in range(TR // SUB):
                acc = jnp.zeros((SUB, HALF), F32)
                for k in range(CONV_K):
                    acc = acc + cw_ref[k:k + 1, :] * _rows_at(dr, rb * SUB + (CONV_K - 1) - k, SUB)
                dxs[rb * SUB:(rb + 1) * SUB, :] = acc
            for k in range(CONV_K):
                acc = jnp.zeros((SUB, HALF), F32)
                for rb in range(TR // SUB):
                    acc = acc + dwin[rb * SUB:(rb + 1) * SUB, :] * _rows_at(xr, rb * SUB + HALO - (CONV_K - 1) + k, SUB)
                dcw_ref[k:k + 1, :] += jnp.sum(acc, axis=0, keepdims=True)
            dz_ref[...] = (dxs[...] * sg).astype(BF16)

        @pl.when(j == 1)
        def _():
            dz_ref[...] = (dxs[...] * dval * sg * (1.0 - sg)).astype(BF16)

    col = lambda c: pl.BlockSpec((TR, HALF), lambda i, j: (i, c))
    prev = lambda c: pl.BlockSpec((HALO, HALF), lambda i, j: (jnp.maximum(i * (TR // HALO) - 1, 0), c))
    nxt = pl.BlockSpec((HALO, HALF), lambda i, j: (jnp.minimum((i + 1) * (TR // HALO), S // HALO - 1), 0))
    return pl.pallas_call(
        body, grid=(nt, 2),
        in_specs=[col(3), col(4), prev(3), prev(4), pl.BlockSpec((TR, HALF), lambda i, j: (i, 0)), nxt,
                  _full_spec((HALO, HALF)), pl.BlockSpec(memory_space=pl.ANY)],
        out_specs=[pl.BlockSpec((TR, HALF), lambda i, j: (i, 3 + j)), _full_spec((HALO, HALF))],
        out_shape=[jax.ShapeDtypeStruct((S, ODD_IN), BF16), jax.ShapeDtypeStruct((HALO, HALF), F32)],
        scratch_shapes=[pltpu.VMEM((WIN, HALF), F32), pltpu.VMEM((WIN, HALF), F32), pltpu.VMEM((TR, HALF), F32),
                        pltpu.VMEM((SUBL, WIN, HALF), F32), pltpu.VMEM((SUBL, WIN, HALF), F32)],
        input_output_aliases={7: 0},
        compiler_params=_params(("arbitrary", "arbitrary"), VMEM_BIG), name="odd_bwd_b",
    )(z1, z1, z1, z1, ddc, ddc, conv_w, dz1)


def _cast_bf16(w, name):
    r, c = w.shape
    tr = min(r, 256)
    def body(i_ref, o_ref):
        o_ref[...] = i_ref[...].astype(BF16)

    return pl.pallas_call(
        body, grid=(r // tr,), in_specs=[pl.BlockSpec((tr, c), lambda i: (i, 0))],
        out_specs=pl.BlockSpec((tr, c), lambda i: (i, 0)), out_shape=jax.ShapeDtypeStruct((r, c), BF16),
        compiler_params=_params(("parallel",)), name=name,
    )(w)


def _adamw(w, g, m, v):
    m = ADAM_B1 * m + (1.0 - ADAM_B1) * g
    v = ADAM_B2 * v + (1.0 - ADAM_B2) * (g * g)
    m_hat = m / (1.0 - ADAM_B1 ** ADAM_STEP)
    v_hat = v / (1.0 - ADAM_B2 ** ADAM_STEP)
    delta = -ADAM_LR * (m_hat / (jnp.sqrt(v_hat) + ADAM_EPS) + ADAM_WD * w)
    return delta, m, v


def _adam_reduce(parts, w, m, v, name, dep=None):
    r, c = w.shape
    tr = min(r, 128)
    deps = [] if dep is None else [dep]
    nparts = parts.shape[0]

    def body(p_ref, w_ref, m_ref, v_ref, *rest):
        g_ref, d_ref, nm_ref, nv_ref = rest[len(deps):]
        g = p_ref[0].astype(F32)
        for d in range(1, nparts):
            g = g + p_ref[d].astype(F32)
        g_ref[...] = g
        d_ref[...], nm_ref[...], nv_ref[...] = _adamw(w_ref[...], g, m_ref[...], v_ref[...])

    spec = pl.BlockSpec((tr, c), lambda i: (i, 0))
    return pl.pallas_call(
        body, grid=(r // tr,),
        in_specs=[pl.BlockSpec((nparts, tr, c), lambda i: (0, i, 0)), spec, spec, spec] + [ANY_SPEC] * len(deps),
        out_specs=[spec] * 4, out_shape=[jax.ShapeDtypeStruct((r, c), F32)] * 4,
        compiler_params=_params(("parallel",), VMEM_BIG), name=name,
    )(parts, w, m, v, *deps)


def _sum_parts(parts, name, dep=None):
    r = parts.shape[1]
    tr = 8
    for cand in (512, 256, 128, 64, 32, 16, 8):
        if r % cand == 0:
            tr = cand
            break
    deps = [] if dep is None else [dep]

    def body(p_ref, *rest):
        g = p_ref[0]
        for d in range(1, NDEV):
            g = g + p_ref[d]
        rest[-1][...] = g

    return pl.pallas_call(
        body, grid=(r // tr,), in_specs=[pl.BlockSpec((NDEV, tr, 128), lambda i: (0, i, 0))] + [ANY_SPEC] * len(deps),
        out_specs=pl.BlockSpec((tr, 128), lambda i: (i, 0)), out_shape=jax.ShapeDtypeStruct((r, 128), F32),
        compiler_params=_params(("parallel",)), name=name,
    )(parts, *deps)


def _sum_unpack(parts, rows, name):
    def body(p_ref, *outs):
        off = 0
        for o_ref, n in zip(outs, rows):
            acc = p_ref[0, off:off + n, :]
            for d in range(1, NDEV):
                acc = acc + p_ref[d, off:off + n, :]
            o_ref[...] = acc
            off += n

    return pl.pallas_call(
        body, grid=(1,), in_specs=[pl.BlockSpec(parts.shape, lambda i: (0, 0, 0))],
        out_specs=[pl.BlockSpec((n, 128), lambda i: (0, 0)) for n in rows],
        out_shape=[jax.ShapeDtypeStruct((n, 128), F32) for n in rows],
        compiler_params=_params(("arbitrary",), VMEM_BIG), name=name,
    )(parts)


def _adam_small(ws, gs, g_specs, ms, vs, name):
    n = len(ws)

    def body(*refs):
        w_r, g_r, m_r, v_r = refs[:n], refs[n:2 * n], refs[2 * n:3 * n], refs[3 * n:4 * n]
        outs = refs[4 * n:]
        for i in range(n):
            g = g_r[i][...]
            outs[4 * i][...] = g
            outs[4 * i + 1][...], outs[4 * i + 2][...], outs[4 * i + 3][...] = _adamw(
                w_r[i][...], g, m_r[i][...], v_r[i][...])

    whole = lambda a: pl.BlockSpec(a.shape, lambda i, nd=a.ndim: (0,) * nd)
    outs = pl.pallas_call(
        body, grid=(1,),
        in_specs=[whole(a) for a in ws] + list(g_specs) + [whole(a) for a in ms] + [whole(a) for a in vs],
        out_specs=[whole(a) for a in ws for _ in range(4)],
        out_shape=[jax.ShapeDtypeStruct(a.shape, F32) for a in ws for _ in range(4)],
        compiler_params=_params(("arbitrary",), VMEM_BIG), name=name,
    )(*ws, *gs, *ms, *vs)
    return [outs[4 * i:4 * i + 4] for i in range(n)]


MASKS = [(mx, my, mc) for mx in (0, 1) for my in (0, 1) for mc in (0, 1)][1:]


def _exchange(arrays, scatter, name):
    nt = len(arrays)
    out_shape = [jax.ShapeDtypeStruct(((NDEV,) + a.shape) if not scatter else a.shape, a.dtype) for a in arrays]

    def body(*refs):
        ins, outs = refs[:nt], refs[nt:2 * nt]
        send_sems, recv_sems, local_sems = refs[2 * nt:]
        x, y, c = lax.axis_index("x"), lax.axis_index("y"), lax.axis_index("c")
        me = 4 * x + 2 * y + c
        copies = []
        for t in range(nt):
            src_own = ins[t].at[me] if scatter else ins[t]
            loc = pltpu.make_async_copy(src_own, outs[t].at[me], local_sems.at[t])
            loc.start()
            copies.append(loc)
            for k, (mx, my, mc) in enumerate(MASKS):
                px, py, pc = (x + mx) % 2, (y + my) % 2, (c + mc) % 2
                peer = 4 * px + 2 * py + pc
                src = ins[t].at[peer] if scatter else ins[t]
                rc = pltpu.make_async_remote_copy(
                    src_ref=src, dst_ref=outs[t].at[me], send_sem=send_sems.at[t, k], recv_sem=recv_sems.at[t, k],
                    device_id=(px, py, pc), device_id_type=MESH)
                rc.start()
                copies.append(rc)
        for cp in copies:
            cp.wait()

    hbm = pl.BlockSpec(memory_space=pl.ANY)
    return pl.pallas_call(
        body, in_specs=[hbm] * nt, out_specs=[hbm] * nt, out_shape=out_shape,
        scratch_shapes=[pltpu.SemaphoreType.DMA((nt, 7)), pltpu.SemaphoreType.DMA((nt, 7)),
                        pltpu.SemaphoreType.DMA((nt,))],
        name=name,
    )(*arrays)


SEM_SPEC = pl.BlockSpec(memory_space=pltpu.SEMAPHORE)
EFFECT = pltpu.SideEffectType.DATAFLOW_SIDE_EFFECTING


def _direct_plan(scatter):
    def plan(x, y, c, srcs, lands):
        me = 4 * x + 2 * y + c
        local, remote = [], []
        for src, land in zip(srcs, lands):
            local.append((src.at[me] if scatter else src, land.at[me]))
            for mx, my, mc in MASKS:
                px, py, pc = (x + mx) % 2, (y + my) % 2, (c + mc) % 2
                blk = src.at[4 * px + 2 * py + pc] if scatter else src
                remote.append((blk, land.at[me], (px, py, pc)))
        return local, remote
    return plan


def _split_start(name, srcs, land_shapes, plan, n_local, n_remote, dep=None):
    ns, nl = len(srcs), len(land_shapes)
    deps = [] if dep is None else [dep]
    lands = [lax.empty(s.shape, s.dtype) for s in land_shapes]

    def body(*refs):
        ins, lz = refs[:ns], refs[ns:ns + nl]
        outs = refs[ns + nl + len(deps):]
        send_sems, recv_sems, token, local_sems = outs[0], outs[1], outs[2 + ns + nl], outs[3 + ns + nl]
        local, remote = plan(lax.axis_index("x"), lax.axis_index("y"), lax.axis_index("c"), ins, lz)
        own = [pltpu.make_async_copy(src, dst, local_sems.at[i]) for i, (src, dst) in enumerate(local)]
        for cp in own:
            cp.start()
        for cp in own:
            cp.wait()
        for k, (src, dst, peer) in enumerate(remote):
            pltpu.make_async_remote_copy(src_ref=src, dst_ref=dst, send_sem=send_sems.at[k], recv_sem=recv_sems.at[k],
                                         device_id=peer, device_id_type=MESH).start()
        token[...] = jnp.zeros_like(token)

    hbm = lambda a: pltpu.HBM(a.shape, a.dtype)
    outs = pl.pallas_call(
        body, name=name,
        out_shape=(pltpu.SemaphoreType.DMA((n_remote,)), pltpu.SemaphoreType.DMA((n_remote,)),
                   *[hbm(a) for a in srcs], *[hbm(a) for a in lands], jax.ShapeDtypeStruct((8, 128), F32)),
        in_specs=[ANY_SPEC] * (ns + nl + len(deps)),
        out_specs=(SEM_SPEC, SEM_SPEC, *[ANY_SPEC] * (ns + nl), pl.BlockSpec(memory_space=pltpu.VMEM)),
        scratch_shapes=[pltpu.SemaphoreType.DMA((n_local,))],
        input_output_aliases={i: 2 + i for i in range(ns + nl)},
        compiler_params=pltpu.CompilerParams(has_side_effects=EFFECT),
    )(*[pltpu.with_memory_space_constraint(a, pltpu.HBM) for a in srcs],
      *[pltpu.with_memory_space_constraint(a, pltpu.HBM) for a in lands], *deps)
    return dict(sems=outs[:2], srcs=outs[2:2 + ns], lands=outs[2 + ns:2 + ns + nl], token=outs[-1],
                plan=plan, n_remote=n_remote)


def _split_wait(name, handle, after):
    srcs, lands, plan = handle["srcs"], handle["lands"], handle["plan"]
    ns, nl = len(srcs), len(lands)

    def body(*refs):
        ins, lz = refs[:ns], refs[ns:ns + nl]
        send_sems, recv_sems = refs[ns + nl], refs[ns + nl + 1]
        _, remote = plan(lax.axis_index("x"), lax.axis_index("y"), lax.axis_index("c"), ins, lz)
        for k, (src, dst, peer) in enumerate(remote):
            cp = pltpu.make_async_remote_copy(src_ref=src, dst_ref=dst, send_sem=send_sems.at[k],
                                              recv_sem=recv_sems.at[k], device_id=peer, device_id_type=MESH)
            cp.wait_send()
            cp.wait_recv()

    hbm = lambda a: pltpu.HBM(a.shape, a.dtype)
    outs = pl.pallas_call(
        body, name=name, out_shape=(*[hbm(a) for a in srcs], *[hbm(a) for a in lands]),
        in_specs=[ANY_SPEC] * (ns + nl) + [SEM_SPEC, SEM_SPEC, ANY_SPEC], out_specs=tuple([ANY_SPEC] * (ns + nl)),
        input_output_aliases={i: i for i in range(ns + nl)},
        compiler_params=pltpu.CompilerParams(has_side_effects=EFFECT),
    )(*srcs, *lands, *handle["sems"], after)
    return list(outs[ns:])


def _sc_exchange(name, collective_id, arrays, scatter):
    nt = len(arrays)
    out_type = [jax.ShapeDtypeStruct(a.shape if scatter else (NDEV,) + a.shape, a.dtype) for a in arrays]

    def body(*refs):
        ins, outs = refs[:nt], refs[nt:2 * nt]
        send_sems, recv_sems, local_sems = refs[2 * nt:3 * nt], refs[3 * nt:4 * nt], refs[4 * nt:5 * nt]
        x, y, c = lax.axis_index("x"), lax.axis_index("y"), lax.axis_index("c")
        peers = [(mx + x - 2 * mx * x, my + y - 2 * my * y, mc + c - 2 * mc * c) for mx, my, mc in MASKS]
        barrier = pltpu.get_barrier_semaphore()
        for peer in peers:
            pl.semaphore_signal(barrier, inc=1, device_id=peer, device_id_type=MESH)
        pl.semaphore_wait(barrier, len(peers))
        me = 4 * x + 2 * y + c
        own = []
        for t in range(nt):
            cp = pltpu.make_async_copy(ins[t].at[me] if scatter else ins[t], outs[t].at[me], local_sems[t])
            cp.start()
            own.append(cp)
            for px, py, pc in peers:
                src = ins[t].at[4 * px + 2 * py + pc] if scatter else ins[t]
                pltpu.make_async_remote_copy(src_ref=src, dst_ref=outs[t].at[me], send_sem=send_sems[t],
                                             recv_sem=recv_sems[t], device_id=(px, py, pc), device_id_type=MESH).start()
        for t in range(nt):
            own[t].wait()
            seven = outs[t].at[pl.ds(0, NDEV - 1)]
            drain = pltpu.make_async_remote_copy(src_ref=seven, dst_ref=seven, send_sem=send_sems[t],
                                                 recv_sem=recv_sems[t], device_id=(x, y, c), device_id_type=MESH)
            drain.wait_send()
            drain.wait_recv()

    return pl.kernel(
        body, out_type=out_type, mesh=plsc.ScalarSubcoreMesh(axis_name="sequencer", num_cores=1),
        scratch_types=[pltpu.SemaphoreType.DMA] * (3 * nt),
        compiler_params=pltpu.CompilerParams(collective_id=collective_id), name=name,
    )(*arrays)


def _sc_gather_two_level(name, collective_id, arrays):
    nt = len(arrays)
    out_type = [jax.ShapeDtypeStruct((NDEV,) + a.shape, a.dtype) for a in arrays]

    def body(*refs):
        ins, outs = refs[:nt], refs[nt:2 * nt]
        sems = refs[2 * nt:]
        send_sems, sib_sems, local_sems = sems[:nt], sems[nt:2 * nt], sems[2 * nt:3 * nt]
        ici_sems = [sems[3 * nt + 3 * t:3 * nt + 3 * t + 3] for t in range(nt)]
        x, y, c = lax.axis_index("x"), lax.axis_index("y"), lax.axis_index("c")
        sibling = (x, y, 1 - c)
        chips = [(1 - x, y), (x, 1 - y), (1 - x, 1 - y)]
        barrier = pltpu.get_barrier_semaphore()
        for peer in [sibling] + [(cx, cy, c) for cx, cy in chips]:
            pl.semaphore_signal(barrier, inc=1, device_id=peer, device_id_type=MESH)
        pl.semaphore_wait(barrier, 4)
        me = 4 * x + 2 * y + c

        def push(t, src, slot, recv_sem, to):
            pltpu.make_async_remote_copy(src_ref=src, dst_ref=outs[t].at[slot], send_sem=send_sems[t],
                                         recv_sem=recv_sem, device_id=to, device_id_type=MESH).start()

        own = []
        for t in range(nt):
            cp = pltpu.make_async_copy(ins[t], outs[t].at[me], local_sems[t])
            cp.start()
            own.append(cp)
            for j, (cx, cy) in enumerate(chips):
                push(t, ins[t], me, ici_sems[t][j], (cx, cy, c))
            push(t, ins[t], me, sib_sems[t], sibling)
        for t in range(nt):
            for j, (cx, cy) in enumerate(chips):
                slot = 4 * cx + 2 * cy + c
                landed = outs[t].at[slot]
                pltpu.make_async_remote_copy(src_ref=landed, dst_ref=landed, send_sem=send_sems[t],
                                             recv_sem=ici_sems[t][j], device_id=(cx, cy, c),
                                             device_id_type=MESH).wait_recv()
                push(t, landed, slot, sib_sems[t], sibling)
        for t in range(nt):
            own[t].wait()
            four, seven = outs[t].at[pl.ds(0, 4)], outs[t].at[pl.ds(0, 7)]
            pltpu.make_async_remote_copy(src_ref=four, dst_ref=four, send_sem=send_sems[t], recv_sem=sib_sems[t],
                                         device_id=sibling, device_id_type=MESH).wait_recv()
            pltpu.make_async_remote_copy(src_ref=seven, dst_ref=seven, send_sem=send_sems[t], recv_sem=sib_sems[t],
                                         device_id=sibling, device_id_type=MESH).wait_send()

    return pl.kernel(
        body, out_type=out_type, mesh=plsc.ScalarSubcoreMesh(axis_name="sequencer", num_cores=1),
        scratch_types=[pltpu.SemaphoreType.DMA] * (6 * nt),
        compiler_params=pltpu.CompilerParams(collective_id=collective_id), name=name,
    )(*arrays)


def _sc_sibling_exchange(name, collective_id, src, out_shape, pieces):
    def body(src_ref, out_ref, send_sem, recv_sem):
        x, y, c = lax.axis_index("x"), lax.axis_index("y"), lax.axis_index("c")
        sibling = (x, y, 1 - c)
        barrier = pltpu.get_barrier_semaphore()
        pl.semaphore_signal(barrier, inc=1, device_id=sibling, device_id_type=MESH)
        pl.semaphore_wait(barrier, 1)
        for piece, lands in pieces(c, src_ref, out_ref):
            pltpu.make_async_remote_copy(src_ref=piece, dst_ref=lands, send_sem=send_sem, recv_sem=recv_sem,
                                         device_id=sibling, device_id_type=MESH).start()
        drain = pltpu.make_async_remote_copy(src_ref=out_ref, dst_ref=out_ref, send_sem=send_sem, recv_sem=recv_sem,
                                             device_id=sibling, device_id_type=MESH)
        drain.wait_send()
        drain.wait_recv()

    return pl.kernel(
        body, out_type=jax.ShapeDtypeStruct(out_shape, src.dtype),
        mesh=plsc.ScalarSubcoreMesh(axis_name="sequencer", num_cores=1), scratch_types=[pltpu.SemaphoreType.DMA] * 2,
        compiler_params=pltpu.CompilerParams(collective_id=collective_id), name=name,
    )(src)


def _sc_chip_scatter(name, collective_id, q):
    def body(q_ref, out_ref, send_sem, recv_sem, local_sem):
        x, y, c = lax.axis_index("x"), lax.axis_index("y"), lax.axis_index("c")
        chips = [(1 - x, y), (x, 1 - y), (1 - x, 1 - y)]
        barrier = pltpu.get_barrier_semaphore()
        for cx, cy in chips:
            pl.semaphore_signal(barrier, inc=1, device_id=(cx, cy, c), device_id_type=MESH)
        pl.semaphore_wait(barrier, 3)
        mine = 2 * x + y
        own = pltpu.make_async_copy(q_ref.at[mine], out_ref.at[mine], local_sem)
        own.start()
        for cx, cy in chips:
            pltpu.make_async_remote_copy(src_ref=q_ref.at[2 * cx + cy], dst_ref=out_ref.at[mine], send_sem=send_sem,
                                         recv_sem=recv_sem, device_id=(cx, cy, c), device_id_type=MESH).start()
        own.wait()
        three = out_ref.at[pl.ds(0, 3)]
        drain = pltpu.make_async_remote_copy(src_ref=three, dst_ref=three, send_sem=send_sem, recv_sem=recv_sem,
                                             device_id=(x, y, c), device_id_type=MESH)
        drain.wait_send()
        drain.wait_recv()

    return pl.kernel(
        body, out_type=jax.ShapeDtypeStruct(q.shape, q.dtype),
        mesh=plsc.ScalarSubcoreMesh(axis_name="sequencer", num_cores=1), scratch_types=[pltpu.SemaphoreType.DMA] * 3,
        compiler_params=pltpu.CompilerParams(collective_id=collective_id), name=name,
    )(q)


def _mm_pair_dw(h_own, dz, h_sib, dz_sib, nb, name, dep=None):
    tn = 512 if nb % 512 == 0 else nb
    per = nb // tn
    o_spec = pl.BlockSpec((None, D, tn), lambda i, j, k: (j // per, 0, j % per))
    part = _matmul(
        h_own, dz, dn=TN, grid=(1, 4 * per, 1),
        a_spec=pl.BlockSpec((S, D), lambda i, j, k: (0, 0)),
        b_spec=pl.BlockSpec((S, tn), lambda i, j, k: (0, (2 * (j // per) + lax.axis_index("c")) * per + j % per)),
        o_spec=o_spec, out_shape=(4, D, nb), out_dtype=F32, acc_shape=(D, tn), name=name + "_own", dep=dep)

    def body(a_ref, b_ref, p_ref, o_ref):
        o_ref[...] = (p_ref[...] + _dot(a_ref[...], b_ref[...], TN)).astype(BF16)

    return pl.pallas_call(
        body, grid=(1, 4 * per, 1),
        in_specs=[pl.BlockSpec((S, D), lambda i, j, k: (0, 0)), pl.BlockSpec((S, tn), lambda i, j, k: (0, j)), o_spec],
        out_specs=o_spec, out_shape=jax.ShapeDtypeStruct((4, D, nb), BF16),
        compiler_params=_params(("parallel", "parallel", "arbitrary"), VMEM_BIG), name=name + "_sibling",
    )(h_sib, dz_sib, part)


SMALL = {
    "e_pre_norm": ((2048,), None), "e_pool_w": ((4, 256, 256), 1), "e_pool_scale": ((1024,), None),
    "e_post_norm": ((2048,), None), "o_pre_norm": ((2048,), 0), "o_sgu_norm_g": ((1024,), 0),
    "o_sgu_norm_b": ((1024,), 0), "o_sgu_w": ((4, 128, 128), None), "o_sgu_b": ((4, 128), None),
    "o_conv_w": ((31, 1024), 1), "o_conv_b": ((1024,), 0), "o_conv_norm_g": ((1024,), 0),
    "o_conv_norm_b": ((1024,), 0), "o_post_norm": ((2048,), 0),
}
SMALL_SHARDED = [n for n, (_, ax) in SMALL.items() if ax is not None]


def _shard_shape(name):
    shape, ax = SMALL[name]
    if ax is None:
        return shape
    return tuple(s // NDEV if i == ax else s for i, s in enumerate(shape))


def _pack(arrs, row_multiple=1):
    flat = jnp.concatenate([a.reshape(-1) for a in arrs])
    pad = -flat.shape[0] % (128 * row_multiple)
    return jnp.concatenate([flat, jnp.zeros((pad,), F32)]).reshape(-1, 128)


def _small_views(name):
    shape, ax = SMALL[name]
    me = lambda: 4 * lax.axis_index("x") + 2 * lax.axis_index("y") + lax.axis_index("c")
    if ax is None:
        view = (int(np.prod(shape)) // 128, 128)
        return view, view, pl.BlockSpec(view, lambda i: (0, 0))
    if len(shape) == 1:
        n = shape[0] // NDEV
        return (1, n), (NDEV, 1, n), pl.BlockSpec((None, 1, n), lambda i: (me(), 0, 0))
    part = _shard_shape(name)
    return part, shape, pl.BlockSpec(part, lambda i: tuple(me() if d == ax else 0 for d in range(len(shape))))


BIG = ("e_w_in", "e_w_out", "o_w_in", "o_w_out")
WEIGHTS = ["e_pre_norm", "e_w_in", "e_pool_w", "e_pool_scale", "e_w_out", "e_post_norm", "o_pre_norm", "o_w_in",
           "o_sgu_norm_g", "o_sgu_norm_b", "o_sgu_w", "o_sgu_b", "o_conv_w", "o_conv_b", "o_conv_norm_g",
           "o_conv_norm_b", "o_w_out", "o_post_norm"]


def kernel(x, e_pre_norm, e_w_in, e_pool_w, e_pool_scale, e_w_out, e_post_norm, o_pre_norm, o_w_in, o_sgu_norm_g, o_sgu_norm_b, o_sgu_w, o_sgu_b, o_conv_w, o_conv_b, o_conv_norm_g, o_conv_norm_b, o_w_out, o_post_norm, loss_target, m_e_pre_norm, m_e_w_in, m_e_pool_w, m_e_pool_scale, m_e_w_out, m_e_post_norm, m_o_pre_norm, m_o_w_in, m_o_sgu_norm_g, m_o_sgu_norm_b, m_o_sgu_w, m_o_sgu_b, m_o_conv_w, m_o_conv_b, m_o_conv_norm_g, m_o_conv_norm_b, m_o_w_out, m_o_post_norm, v_e_pre_norm, v_e_w_in, v_e_pool_w, v_e_pool_scale, v_e_w_out, v_e_post_norm, v_o_pre_norm, v_o_w_in, v_o_sgu_norm_g, v_o_sgu_norm_b, v_o_sgu_w, v_o_sgu_b, v_o_conv_w, v_o_conv_b, v_o_conv_norm_g, v_o_conv_norm_b, v_o_w_out, v_o_post_norm):
    given = dict(locals())
    w = {n: given[n][0] for n in WEIGHTS}
    m = {n: given["m_" + n][0] for n in WEIGHTS}
    v = {n: given["v_" + n][0] for n in WEIGHTS}
    me = 4 * lax.axis_index("x") + 2 * lax.axis_index("y") + lax.axis_index("c")
    x, target = x[0], loss_target[0]
    row = lambda a: a.reshape(1, -1)

    bf = {n: _cast_bf16(w[n], "cast_" + n) for n in BIG}
    wg_e_in, small_rows = _sc_gather_two_level("gather_a", 0, [bf["e_w_in"], _pack([w[n] for n in SMALL_SHARDED])])
    wg_e_out, wg_o_in, wg_o_out = _sc_gather_two_level("gather_b", 1, [bf["e_w_out"], bf["o_w_in"], bf["o_w_out"]])
    h0 = _pre0_fwd(x, row(w["e_pre_norm"]))
    h0_sib = _sc_sibling_exchange("swap_h0", 8, h0, h0.shape, lambda c, src, out: [(src, out)])
    p = {n: w[n] for n in SMALL if SMALL[n][1] is None}
    small_rows = small_rows.reshape(NDEV, -1)
    off = 0
    for n in SMALL_SHARDED:
        shp, ax = _shard_shape(n), SMALL[n][1]
        cnt = int(np.prod(shp))
        blk = small_rows[:, off:off + cnt].reshape((NDEV,) + shp)
        p[n] = jnp.moveaxis(blk, 0, ax).reshape(SMALL[n][0])
        off += cnt
    tabs = _rope_tables()
    pool_w_bf = p["e_pool_w"].astype(BF16)
    sgu_bb = jnp.broadcast_to(p["o_sgu_b"][:, :, None], (4, 128, 128))
    conv_w = jnp.concatenate([p["o_conv_w"], jnp.zeros((HALO - CONV_K, HALF), F32)], axis=0)
    odd_p = (row(p["o_sgu_norm_g"]), row(p["o_sgu_norm_b"]), p["o_sgu_w"], sgu_bb, conv_w,
             row(p["o_conv_b"]), row(p["o_conv_norm_g"]), row(p["o_conv_norm_b"]))

    z0 = _mm_in(h0, wg_e_in, "mm_z0")
    ycat0 = _pool_fwd(z0, pool_w_bf, row(p["e_pool_scale"]))
    qkv = _qkv_prep(z0, tabs)
    ycat0, og, lg = _attn_fwd(z0, qkv, ycat0)
    w_out_e, w_out_o = wg_e_out.reshape(2048, D), wg_o_out.reshape(2048, D)
    y0 = _mm_out(ycat0, w_out_e, "mm_y0", h0_sib)
    x1, h1 = _post0_fwd(x, y0, row(p["e_post_norm"]), row(p["o_pre_norm"]))
    z1 = _mm_in(h1, wg_o_in, "mm_z1")
    ycat1 = _odd_fwd(z1, *odd_p)
    y1 = _mm_out(ycat1, w_out_o, "mm_y1")

    g = {}
    loss, dx2, dy1, g["o_post_norm"] = _post1_bwd(y1, x1, target, row(p["o_post_norm"]))
    loss = lax.psum(loss[0, 0], ("x", "y", "c"))
    parts = {}
    dw = _mm_out_dw(ycat1, dy1, "mm_dwout1").reshape(NDEV, 256, D)
    parts["o_w_out"], = _sc_exchange("scatter_o_w_out", 2, [dw], True)
    dycat1 = _mm_out_dx(dy1, w_out_o, "mm_dycat1", (dw, loss.reshape(1, 1)))
    dz1, ddc, g["o_sgu_w"], d_sgu_bb, g["o_sgu_norm_g"], g["o_sgu_norm_b"], g["o_conv_norm_g"], \
        g["o_conv_norm_b"], g["o_conv_b"] = _odd_bwd_a(z1, dycat1, *odd_p)
    dz1, d_conv_w = _odd_bwd_b(z1, ddc, dz1, conv_w)
    g["o_sgu_b"] = d_sgu_bb[:, :, 0]
    g["o_conv_w"] = d_conv_w[:CONV_K]
    grads, deltas, new_m, new_v = {}, {}, {}, {}

    def adam(n, dep):
        grads[n], deltas[n], new_m[n], new_v[n] = _adam_reduce(parts[n], w[n], m[n], v[n], "adam_" + n, dep)
        return new_v[n]

    pin = adam("o_w_out", d_conv_w)
    dw = _mm_in_dw(h1, dz1, ODD_IN // NDEV, "mm_dwin1", pin)
    parts["o_w_in"], = _sc_exchange("scatter_o_w_in", 3, [dw], True)
    dh1 = _mm_in_dx(dz1, wg_o_in, "mm_dh1", dw)
    dx1, dy0, g["o_pre_norm"], g["e_post_norm"] = _mid_bwd(dx2, dh1, x1, y0, row(p["o_pre_norm"]),
                                                           row(p["e_post_norm"]))
    dw = _mm_out_dw(ycat0, dy0, "mm_dwout0").reshape(NDEV, 256, D)
    parts["e_w_out"], = _sc_exchange("scatter_e_w_out", 4, [dw], True)
    dycat0 = _mm_out_dx(dy0, w_out_e, "mm_dycat0", dw)
    da_in, da_gate, g["e_pool_w"], g["e_pool_scale"] = _pool_bwd(z0, dycat0, pool_w_bf, row(p["e_pool_scale"]))
    dq, dk, dv, dbg = _attn_bwd(z0, qkv, og, lg, dycat0, tabs)
    dz0 = jnp.concatenate([da_in, da_gate, dq, dk, dv, dbg], axis=1)
    late = [n for n in SMALL if n not in ("e_pre_norm", "o_sgu_b")] + ["o_sgu_b"]
    pin = adam("e_w_out", adam("o_w_in", dz0))
    nb = EVEN_IN // NDEV
    dz0_sib = _sc_sibling_exchange(
        "swap_dz0", 9, dz0, (S, 4 * nb),
        lambda c, src, out: [(src.at[:, pl.ds((2 * j + 1 - c) * nb, nb)], out.at[:, pl.ds(j * nb, nb)])
                             for j in range(4)])
    dw = _mm_pair_dw(h0, dz0, h0_sib, dz0_sib, nb, "mm_dwin0", pin)
    parts["e_w_in"] = _sc_chip_scatter("scatter_e_w_in", 5, dw)
    recv_small, = _sc_exchange("gather_small_grads", 6, [_pack([g[n].reshape(SMALL[n][0]) for n in late], 512)], False)
    dh0 = _mm_in_dx(dz0, wg_e_in, "mm_dh0", dw)
    grad_x, g["e_pre_norm"] = _pre0_bwd(dx1, dh0, x, row(p["e_pre_norm"]))
    last, = _sc_exchange("gather_e_pre_norm_grad", 7, [g["e_pre_norm"].reshape(16, 128)], False)

    rows = [int(np.prod(SMALL[n][0])) // 128 for n in late]
    summed = dict(zip(late, _sum_unpack(recv_small, rows, "sum_small_grads")))
    pin = adam("e_w_in", grad_x)
    summed["e_pre_norm"] = _sum_parts(last, "sum_e_pre_norm_grad", pin)
    names = list(SMALL)
    views = [_small_views(n) for n in names]
    mine = lambda src: [src[n].reshape(vw[0]) for n, vw in zip(names, views)]
    res = _adam_small(mine(w), [summed[n].reshape(vw[1]) for n, vw in zip(names, views)], [vw[2] for vw in views],
                      mine(m), mine(v), "adam_small")
    for n, out in zip(names, res):
        grads[n], deltas[n], new_m[n], new_v[n] = [t.reshape(_shard_shape(n)) for t in out]

    lead = lambda a: a[None]
    return (loss, grad_x[None], *[lead(grads[n]) for n in WEIGHTS], *[lead(deltas[n]) for n in WEIGHTS],
            *[lead(new_m[n]) for n in WEIGHTS], *[lead(new_v[n]) for n in WEIGHTS])
```

```python
import functools

import numpy as np
import jax
import jax.numpy as jnp
from jax import lax
from jax.experimental import pallas as pl
from jax.experimental.pallas import tpu as pltpu
from jax.experimental.pallas import tpu_sc as plsc

F32 = jnp.float32
BF16 = jnp.bfloat16

S = 2048
D = 2048
NDEV = 8
EPS = 1e-6
NEG = -1e30
HEAD_DIM = 128
ROT_DIM = 32
ROPE_THETA = 500000.0
PATTERNS = ((128, 1), (512, 4), (2048, 16))
BLK = 128
EVEN_IN = 12288
ODD_IN = 6144
HALF = 1024
CONV_K = 31
HALO = 32
TR = 256
SUB = 32

ADAM_LR = 0.001
ADAM_B1 = 0.9
ADAM_B2 = 0.999
ADAM_EPS = 1e-08
ADAM_WD = 0.01
ADAM_STEP = 10

VMEM_BIG = 56 * 1024 * 1024
MESH = pl.DeviceIdType.MESH

NN = (((1,), (0,)), ((), ()))
NT = (((1,), (1,)), ((), ()))
TN = (((0,), (0,)), ((), ()))


def _dot(a, b, dn=NN):
    return lax.dot_general(a, b, dn, preferred_element_type=F32)


def _sigmoid(x):
    return 1.0 / (1.0 + jnp.exp(-x))


def _silu_and_grad(x):
    sg = _sigmoid(x)
    return x * sg, sg * (1.0 + x * (1.0 - sg))


def _params(sem, vmem=None):
    return pltpu.CompilerParams(dimension_semantics=sem, vmem_limit_bytes=vmem)


ANY_SPEC = pl.BlockSpec(memory_space=pl.ANY)


def _matmul(a, b, *, dn, grid, a_spec, b_spec, o_spec, out_shape, out_dtype, acc_shape, name, dep=None):
    nk = grid[2]
    deps = [] if dep is None else list(dep) if isinstance(dep, (tuple, list)) else [dep]

    def body(a_ref, b_ref, *rest):
        o_ref, acc = rest[len(deps)], rest[len(deps) + 1:]
        if nk == 1:
            o_ref[...] = _dot(a_ref[...], b_ref[...], dn).astype(o_ref.dtype)
            return
        acc_ref = acc[0]
        k = pl.program_id(2)

        @pl.when(k == 0)
        def _():
            acc_ref[...] = jnp.zeros_like(acc_ref)

        acc_ref[...] += _dot(a_ref[...], b_ref[...], dn)

        @pl.when(k == nk - 1)
        def _():
            o_ref[...] = acc_ref[...].astype(o_ref.dtype)

    return pl.pallas_call(
        body, grid=grid, in_specs=[a_spec, b_spec] + [ANY_SPEC] * len(deps), out_specs=o_spec,
        out_shape=jax.ShapeDtypeStruct(out_shape, out_dtype),
        scratch_shapes=[] if nk == 1 else [pltpu.VMEM(acc_shape, F32)],
        compiler_params=_params(("parallel", "parallel", "arbitrary"), VMEM_BIG), name=name,
    )(a, b, *deps)


TM = 2048


def _mm_in(h, wg, name):
    nb = wg.shape[2]
    tn = 512 if nb % 512 == 0 else nb
    per = nb // tn
    return _matmul(
        h, wg, dn=NN, grid=(S // TM, NDEV * per, 1),
        a_spec=pl.BlockSpec((TM, D), lambda i, j, k: (i, 0)),
        b_spec=pl.BlockSpec((None, D, tn), lambda i, j, k: (j // per, 0, j % per)),
        o_spec=pl.BlockSpec((TM, tn), lambda i, j, k: (i, j)),
        out_shape=(S, NDEV * nb), out_dtype=F32, acc_shape=(TM, tn), name=name)


def _mm_in_dx(dz, wg, name, dep=None):
    nb = wg.shape[2]
    return _matmul(
        dz, wg, dn=NT, grid=(S // TM, D // 1024, NDEV),
        a_spec=pl.BlockSpec((TM, nb), lambda i, j, k: (i, k)),
        b_spec=pl.BlockSpec((None, 1024, nb), lambda i, j, k: (k, j, 0)),
        o_spec=pl.BlockSpec((TM, 1024), lambda i, j, k: (i, j)),
        out_shape=(S, D), out_dtype=F32, acc_shape=(TM, 1024), name=name, dep=dep)


def _mm_in_dw(h, dz, nb, name, dep=None):
    tn = 512 if nb % 512 == 0 else nb
    per = nb // tn
    return _matmul(
        h, dz, dn=TN, grid=(D // TM, NDEV * per, 1),
        a_spec=pl.BlockSpec((S, TM), lambda i, j, k: (0, i)),
        b_spec=pl.BlockSpec((S, tn), lambda i, j, k: (0, j)),
        o_spec=pl.BlockSpec((None, TM, tn), lambda i, j, k: (j // per, i, j % per)),
        out_shape=(NDEV, D, nb), out_dtype=BF16, acc_shape=(TM, tn), name=name, dep=dep)


def _mm_out(yc, w, name, dep=None):
    return _matmul(
        yc, w, dn=NN, grid=(S // TM, D // 512, 1),
        a_spec=pl.BlockSpec((TM, 2048), lambda i, j, k: (i, 0)),
        b_spec=pl.BlockSpec((2048, 512), lambda i, j, k: (0, j)),
        o_spec=pl.BlockSpec((TM, 512), lambda i, j, k: (i, j)),
        out_shape=(S, D), out_dtype=F32, acc_shape=(TM, 512), name=name, dep=dep)


def _mm_out_dx(dy, w, name, dep=None):
    return _matmul(
        dy, w, dn=NT, grid=(S // TM, 2048 // 512, 1),
        a_spec=pl.BlockSpec((TM, D), lambda i, j, k: (i, 0)),
        b_spec=pl.BlockSpec((512, D), lambda i, j, k: (j, 0)),
        o_spec=pl.BlockSpec((TM, 512), lambda i, j, k: (i, j)),
        out_shape=(S, 2048), out_dtype=F32, acc_shape=(TM, 512), name=name, dep=dep)


def _mm_out_dw(yc, dy, name):
    return _matmul(
        yc, dy, dn=TN, grid=(2048 // TM, D // 512, 1),
        a_spec=pl.BlockSpec((S, TM), lambda i, j, k: (0, i)),
        b_spec=pl.BlockSpec((S, 512), lambda i, j, k: (0, j)),
        o_spec=pl.BlockSpec((TM, 512), lambda i, j, k: (i, j)),
        out_shape=(2048, D), out_dtype=BF16, acc_shape=(TM, 512), name=name)


def _row_spec(w=D):
    return pl.BlockSpec((TR, w), lambda i: (i, 0))


def _vec_spec(w=D):
    return pl.BlockSpec((1, w), lambda i: (0, 0))


def _rms_stats(x):
    r = lax.rsqrt(jnp.mean(x * x, axis=-1, keepdims=True) + EPS)
    return x * r, r


def _rms_bwd(dn, xhat, r, g):
    dxh = dn * g
    return r * (dxh - xhat * jnp.mean(dxh * xhat, axis=-1, keepdims=True))


def _acc_rows(ref, val, i):
    s = jnp.sum(val, axis=0, keepdims=True)

    @pl.when(i == 0)
    def _():
        ref[...] = s

    @pl.when(i > 0)
    def _():
        ref[...] += s


def _pre0_fwd(x, g, dep=None):
    deps = [] if dep is None else [dep]

    def body(x_ref, g_ref, *rest):
        xhat, _ = _rms_stats(x_ref[...])
        rest[-1][...] = (xhat * g_ref[...]).astype(BF16)

    return pl.pallas_call(
        body, grid=(S // TR,), in_specs=[_row_spec(), _vec_spec()] + [ANY_SPEC] * len(deps), out_specs=_row_spec(),
        out_shape=jax.ShapeDtypeStruct((S, D), BF16), compiler_params=_params(("parallel",)), name="pre0_fwd",
    )(x, g, *deps)


def _post0_fwd(x, y0, g_post, g_pre1):
    def body(x_ref, y_ref, gp_ref, g1_ref, x1_ref, h1_ref):
        yhat, _ = _rms_stats(y_ref[...])
        x1 = x_ref[...] + yhat * gp_ref[...]
        x1_ref[...] = x1
        xhat, _ = _rms_stats(x1)
        h1_ref[...] = (xhat * g1_ref[...]).astype(BF16)

    return pl.pallas_call(
        body, grid=(S // TR,), in_specs=[_row_spec(), _row_spec(), _vec_spec(), _vec_spec()],
        out_specs=[_row_spec(), _row_spec()],
        out_shape=[jax.ShapeDtypeStruct((S, D), F32), jax.ShapeDtypeStruct((S, D), BF16)],
        compiler_params=_params(("parallel",)), name="post0_fwd",
    )(x, y0, g_post, g_pre1)


def _post1_bwd(y1, x1, target, g_post):
    def body(y_ref, x1_ref, t_ref, g_ref, loss_ref, dx2_ref, dy_ref, dg_ref):
        i = pl.program_id(0)
        yhat, r = _rms_stats(y_ref[...])
        g = g_ref[...]
        err = x1_ref[...] + yhat * g - t_ref[...]
        part = jnp.sum(jnp.sum(err * err, axis=-1, keepdims=True), axis=0, keepdims=True) * (0.5 / D)
        _acc_rows(loss_ref, jnp.broadcast_to(part, (1, 128)), i)
        dx2 = err * (1.0 / D)
        dx2_ref[...] = dx2
        _acc_rows(dg_ref, dx2 * yhat, i)
        dy_ref[...] = _rms_bwd(dx2, yhat, r, g).astype(BF16)

    return pl.pallas_call(
        body, grid=(S // TR,), in_specs=[_row_spec(), _row_spec(), _row_spec(), _vec_spec()],
        out_specs=[_vec_spec(128), _row_spec(), _row_spec(), _vec_spec()],
        out_shape=[jax.ShapeDtypeStruct((1, 128), F32), jax.ShapeDtypeStruct((S, D), F32),
                   jax.ShapeDtypeStruct((S, D), BF16), jax.ShapeDtypeStruct((1, D), F32)],
        compiler_params=_params(("arbitrary",)), name="post1_bwd",
    )(y1, x1, target, g_post)


def _mid_bwd(dx2, dh1, x1, y0, g_pre1, g_post0):
    def body(dx2_ref, dh_ref, x1_ref, y_ref, g1_ref, gp_ref, dx1_ref, dy_ref, dg1_ref, dgp_ref):
        i = pl.program_id(0)
        xhat, r1 = _rms_stats(x1_ref[...])
        dh = dh_ref[...]
        _acc_rows(dg1_ref, dh * xhat, i)
        dx1 = dx2_ref[...] + _rms_bwd(dh, xhat, r1, g1_ref[...])
        dx1_ref[...] = dx1
        yhat, r0 = _rms_stats(y_ref[...])
        _acc_rows(dgp_ref, dx1 * yhat, i)
        dy_ref[...] = _rms_bwd(dx1, yhat, r0, gp_ref[...]).astype(BF16)

    return pl.pallas_call(
        body, grid=(S // TR,),
        in_specs=[_row_spec(), _row_spec(), _row_spec(), _row_spec(), _vec_spec(), _vec_spec()],
        out_specs=[_row_spec(), _row_spec(), _vec_spec(), _vec_spec()],
        out_shape=[jax.ShapeDtypeStruct((S, D), F32), jax.ShapeDtypeStruct((S, D), BF16),
                   jax.ShapeDtypeStruct((1, D), F32), jax.ShapeDtypeStruct((1, D), F32)],
        compiler_params=_params(("arbitrary",)), name="mid_bwd",
    )(dx2, dh1, x1, y0, g_pre1, g_post0)


def _pre0_bwd(dx1, dh0, x, g):
    def body(dx1_ref, dh_ref, x_ref, g_ref, gx_ref, dg_ref):
        i = pl.program_id(0)
        xhat, r = _rms_stats(x_ref[...])
        dh = dh_ref[...]
        _acc_rows(dg_ref, dh * xhat, i)
        gx_ref[...] = dx1_ref[...] + _rms_bwd(dh, xhat, r, g_ref[...])

    return pl.pallas_call(
        body, grid=(S // TR,), in_specs=[_row_spec(), _row_spec(), _row_spec(), _vec_spec()],
        out_specs=[_row_spec(), _vec_spec()],
        out_shape=[jax.ShapeDtypeStruct((S, D), F32), jax.ShapeDtypeStruct((1, D), F32)],
        compiler_params=_params(("arbitrary",)), name="pre0_bwd",
    )(dx1, dh0, x, g)


POOL_CH = 256


def _pool_apply(a, w, transpose):
    n = a.shape[0]
    row = lax.broadcasted_iota(jnp.int32, a.shape, 0)
    cnt = jnp.minimum(row + 1, w).astype(F32)
    s = a / cnt if transpose else a
    for k in (1, 2, 4, 8):
        if transpose:
            sh = jnp.where(row < n - k, pltpu.roll(s, n - k, 0), 0.0)
        else:
            sh = jnp.where(row >= k, pltpu.roll(s, k, 0), 0.0)
        s = jnp.where(w > k, s + sh, s)
    return s - a if transpose else s / cnt - a


def _pool_fwd(z0, pool_w, pool_scale):
    def body(a_ref, gate_ref, w_ref, sc_ref, out_ref):
        win = jnp.left_shift(2, pl.program_id(0))
        pooled = _pool_apply(a_ref[...], win, False)
        mixed = _dot(pooled.astype(BF16), w_ref[...])
        gate = gate_ref[...]
        out_ref[...] = (mixed * sc_ref[...] * (gate * _sigmoid(gate))).astype(BF16)

    return pl.pallas_call(
        body, grid=(4,),
        in_specs=[pl.BlockSpec((S, POOL_CH), lambda g: (0, g)), pl.BlockSpec((S, POOL_CH), lambda g: (0, 4 + g)),
                  pl.BlockSpec((None, POOL_CH, POOL_CH), lambda g: (g, 0, 0)),
                  pl.BlockSpec((1, POOL_CH), lambda g: (0, g))],
        out_specs=pl.BlockSpec((S, POOL_CH), lambda g: (0, g)),
        out_shape=jax.ShapeDtypeStruct((S, 2048), BF16),
        compiler_params=_params(("parallel",), VMEM_BIG), name="pool_fwd",
    )(z0, z0, pool_w, pool_scale)


def _pool_bwd(z0, dycat, pool_w, pool_scale):
    def body(a_ref, gate_ref, dy_ref, w_ref, sc_ref, da_ref, dgate_ref, dw_ref, dsc_ref):
        win = jnp.left_shift(2, pl.program_id(0))
        pooled = _pool_apply(a_ref[...], win, False).astype(BF16)
        w = w_ref[...]
        mixed = _dot(pooled, w)
        silu, dsilu = _silu_and_grad(gate_ref[...])
        dy = dy_ref[...]
        sc = sc_ref[...]
        dgate_ref[...] = (dy * (mixed * sc) * dsilu).astype(BF16)
        dms = dy * silu
        dsc_ref[...] = jnp.sum(dms * mixed, axis=0, keepdims=True)
        dmixed = (dms * sc).astype(BF16)
        dw_ref[...] = _dot(pooled, dmixed, TN)
        dpooled = _dot(dmixed, w, NT)
        da_ref[...] = _pool_apply(dpooled, win, True).astype(BF16)

    slab = lambda off: pl.BlockSpec((S, POOL_CH), lambda g: (0, off + g))
    return pl.pallas_call(
        body, grid=(4,),
        in_specs=[slab(0), slab(4), slab(0), pl.BlockSpec((None, POOL_CH, POOL_CH), lambda g: (g, 0, 0)),
                  pl.BlockSpec((1, POOL_CH), lambda g: (0, g))],
        out_specs=[slab(0), slab(0), pl.BlockSpec((None, POOL_CH, POOL_CH), lambda g: (g, 0, 0)),
                   pl.BlockSpec((1, POOL_CH), lambda g: (0, g))],
        out_shape=[jax.ShapeDtypeStruct((S, HALF), BF16), jax.ShapeDtypeStruct((S, HALF), BF16),
                   jax.ShapeDtypeStruct((4, POOL_CH, POOL_CH), F32), jax.ShapeDtypeStruct((1, HALF), F32)],
        compiler_params=_params(("parallel",), VMEM_BIG), name="pool_bwd",
    )(z0, z0, dycat, pool_w, pool_scale)


Q_COL, K_COL, V_COL, BG_COL = 2048 // 128, 5120 // 128, 8192 // 128, 11264 // 128
SCALE = HEAD_DIM ** -0.5


def _rope_tables():
    pos = jnp.arange(S, dtype=F32)
    inv_freq = jnp.power(ROPE_THETA, -jnp.arange(0, ROT_DIM, 2, dtype=F32) / ROT_DIM)
    ang = pos[:, None] * inv_freq[None, :]
    cos, sin = jnp.cos(ang), jnp.sin(ang)
    half = ROT_DIM // 2
    zeros = jnp.zeros((S, HEAD_DIM - ROT_DIM), F32)
    c = jnp.concatenate([cos, cos, jnp.ones((S, HEAD_DIM - ROT_DIM), F32)], axis=1)
    a = jnp.concatenate([-sin, jnp.zeros((S, half), F32), zeros], axis=1)
    b = jnp.concatenate([jnp.zeros((S, half), F32), sin, zeros], axis=1)
    return c, a, b


def _rope(t, c, a, b):
    half = ROT_DIM // 2
    return t * c + pltpu.roll(t, HEAD_DIM - half, 1) * a + pltpu.roll(t, half, 1) * b


def _rope_t(d, c, a, b):
    half = ROT_DIM // 2
    return d * c + pltpu.roll(d * a, half, 1) + pltpu.roll(d * b, HEAD_DIM - half, 1)


def _deinterleave(dst, src, dil, cast=None, dst_off=0):
    length = S // dil
    for r in range(dil):
        v = src[...] if dil == 1 else src[pl.ds(r, length, stride=dil), :]
        dst[dst_off + r * length:dst_off + (r + 1) * length, :] = v if cast is None else v.astype(cast)


def _interleave(dst, src, dil, src_off=0):
    length = S // dil
    for r in range(dil):
        if dil == 1:
            dst[...] = src[src_off:src_off + S, :]
        else:
            dst[pl.ds(r, length, stride=dil), :] = src[src_off + r * length:src_off + (r + 1) * length, :]


CU = 4
NUNITS = S // BLK
B_QK = (((2,), (2,)), ((0,), (0,)))
B_PV = (((2,), (1,)), ((0,), (0,)))
B_TN = (((1,), (1,)), ((0,), (0,)))


def _blocks(ref, first):
    return ref[first * BLK:(first + CU) * BLK, :].reshape(CU, BLK, HEAD_DIM)


def _chunk_scores(u0, nb, qd, kdp):
    q = _blocks(qd, u0)
    row = lax.broadcasted_iota(jnp.int32, (CU, BLK, BLK), 1)
    col = lax.broadcasted_iota(jnp.int32, (CU, BLK, BLK), 2)
    s_own = jnp.where(col <= row, _dot(q, _blocks(kdp, u0 + 1), B_QK) * SCALE, NEG)
    if nb == 1:
        return q, s_own, None
    unit = lax.broadcasted_iota(jnp.int32, (CU, BLK, BLK), 0) + u0
    s_prev = jnp.where((col >= row) & ((unit % nb) != 0), _dot(q, _blocks(kdp, u0), B_QK) * SCALE, NEG)
    return q, s_own, s_prev


def _qkv_prep(z0, tabs):
    def body(q_ref, k_ref, v_ref, c_ref, a_ref, b_ref, qo_ref, ko_ref, vo_ref, tmp):
        p = pl.program_id(1)
        for gi, (_, dil) in enumerate(PATTERNS):
            @pl.when(p == gi)
            def _(dil=dil):
                c, a, b = c_ref[...], a_ref[...], b_ref[...]
                tmp[...] = _rope(q_ref[...], c, a, b)
                _deinterleave(qo_ref, tmp, dil, BF16)
                tmp[...] = _rope(k_ref[...], c, a, b)
                _deinterleave(ko_ref, tmp, dil, BF16)
                _deinterleave(vo_ref, v_ref, dil, BF16)

    tab = pl.BlockSpec((S, HEAD_DIM), lambda h, p: (0, 0))
    out = pl.BlockSpec((S, HEAD_DIM), lambda h, p: (0, p * 8 + h))
    return pl.pallas_call(
        body, grid=(8, 3), in_specs=[_head_spec(Q_COL), _head_spec(K_COL), _head_spec(V_COL), tab, tab, tab],
        out_specs=[out, out, out], out_shape=[jax.ShapeDtypeStruct((S, 3072), BF16)] * 3,
        scratch_shapes=[pltpu.VMEM((S, HEAD_DIM), F32)],
        compiler_params=_params(("parallel", "arbitrary"), VMEM_BIG), name="qkv_prep",
    )(z0, z0, z0, *tabs)


def _pad_copy(dst, src):
    dst[0:BLK, :] = jnp.zeros((BLK, HEAD_DIM), dst.dtype)
    dst[BLK:BLK + S, :] = src[...]


def _attn_group_fwd(dil, qd, kd_ref, vd_ref, kdp, vdp, od, ld, og, lg):
    nb = S // dil // BLK
    _pad_copy(kdp, kd_ref)
    _pad_copy(vdp, vd_ref)
    for u0 in range(0, NUNITS, CU):
        _, s_own, s_prev = _chunk_scores(u0, nb, qd, kdp)
        m = jnp.max(s_own, axis=2, keepdims=True)
        if s_prev is not None:
            m = jnp.maximum(m, jnp.max(s_prev, axis=2, keepdims=True))
        p_own = jnp.exp(s_own - m)
        den = jnp.sum(p_own, axis=2, keepdims=True)
        acc = _dot(p_own.astype(BF16), _blocks(vdp, u0 + 1), B_PV)
        if s_prev is not None:
            p_prev = jnp.exp(s_prev - m)
            den = den + jnp.sum(p_prev, axis=2, keepdims=True)
            acc = acc + _dot(p_prev.astype(BF16), _blocks(vdp, u0), B_PV)
        rows = slice(u0 * BLK, (u0 + CU) * BLK)
        od[rows, :] = (acc / den).reshape(CU * BLK, HEAD_DIM)
        ld[rows, :] = jnp.broadcast_to(m + jnp.log(den), (CU, BLK, HEAD_DIM)).reshape(CU * BLK, HEAD_DIM)
    _interleave(og, od, dil)
    _interleave(lg, ld, dil)


def _group_weights(lgs):
    l0, l1, l2 = lgs[0][...], lgs[1][...], lgs[2][...]
    mx = jnp.maximum(l0, jnp.maximum(l1, l2))
    e0, e1, e2 = jnp.exp(l0 - mx), jnp.exp(l1 - mx), jnp.exp(l2 - mx)
    den = e0 + e1 + e2
    return e0 / den, e1 / den, e2 / den


def _head_spec(base, ngroups_axis=True):
    return pl.BlockSpec((S, HEAD_DIM), lambda h, p: (0, base + (p % 3) * 8 + h))


def _slab(dtype=F32, rows=S):
    return pltpu.VMEM((rows, HEAD_DIM), dtype)


def _attn_fwd(z0, qkv, ycat):
    def body(q_ref, k_ref, v_ref, gate_ref, ycat_ref, out_ref, og_ref, lg_ref,
             kdp, vdp, od, ld, og0, og1, og2, lg0, lg1, lg2):
        del ycat_ref
        p = pl.program_id(1)
        ogs, lgs = (og0, og1, og2), (lg0, lg1, lg2)
        for gi, (_, dil) in enumerate(PATTERNS):
            @pl.when(p == gi)
            def _(gi=gi, dil=dil):
                _attn_group_fwd(dil, q_ref, k_ref, v_ref, kdp, vdp, od, ld, ogs[gi], lgs[gi])
                og_ref[...] = ogs[gi][...]
                lg_ref[...] = lgs[gi][...]

        @pl.when(p == 2)
        def _():
            w0, w1, w2 = _group_weights(lgs)
            o = w0 * og0[...] + w1 * og1[...] + w2 * og2[...]
            gate = gate_ref[...]
            out_ref[...] = (o * (gate * _sigmoid(gate))).astype(BF16)

    grp = pl.BlockSpec((S, HEAD_DIM), lambda h, p: (0, p * 8 + h))
    return pl.pallas_call(
        body, grid=(8, 3),
        in_specs=[grp, grp, grp, pl.BlockSpec((S, HEAD_DIM), lambda h, p: (0, BG_COL + h)), ANY_SPEC],
        out_specs=[pl.BlockSpec((S, HEAD_DIM), lambda h, p: (0, 8 + h)), grp, grp],
        out_shape=[jax.ShapeDtypeStruct((S, 2048), BF16), jax.ShapeDtypeStruct((S, 3072), F32),
                   jax.ShapeDtypeStruct((S, 3072), F32)],
        scratch_shapes=[_slab(BF16, S + BLK), _slab(BF16, S + BLK)] + [_slab() for _ in range(8)],
        input_output_aliases={4: 0},
        compiler_params=_params(("parallel", "arbitrary"), VMEM_BIG), name="attn_fwd",
    )(*qkv, z0, ycat)


def _attn_bwd(z0, qkv, og, lg, dycat, tabs):
    def body(q_ref, k_ref, v_ref, gate_ref, dy_ref, c_ref, a_ref, b_ref,
             og0_ref, og1_ref, og2_ref, lg0_ref, lg1_ref, lg2_ref,
             dq_ref, dk_ref, dv_ref, dbg_ref,
             tmp, kd, vd, ld, dg0, dg1, dg2, cg0, cg1, cg2, dod, cd, dqd, dkd, dvd):
        p = pl.program_id(1)
        ogs, lgs, dgs, cgs = (og0_ref, og1_ref, og2_ref), (lg0_ref, lg1_ref, lg2_ref), (dg0, dg1, dg2), (cg0, cg1, cg2)

        @pl.when(p == 0)
        def _():
            w = _group_weights(lgs)
            o = w[0] * ogs[0][...] + w[1] * ogs[1][...] + w[2] * ogs[2][...]
            silu, dsilu = _silu_and_grad(gate_ref[...])
            dy = dy_ref[...]
            dbg_ref[...] = (dy * o * dsilu).astype(BF16)
            do = dy * silu
            dwbar = jnp.sum(do * o, axis=1, keepdims=True)
            for gi in range(3):
                dgs[gi][...] = w[gi] * do
                cgs[gi][...] = -w[gi] * dwbar

        for gi, (_, dil) in enumerate(PATTERNS):
            @pl.when(p == 1 + gi)
            def _(gi=gi, dil=dil):
                nb = S // dil // BLK
                qd = q_ref
                c, a, b = c_ref[...], a_ref[...], b_ref[...]
                _pad_copy(kd, k_ref)
                _pad_copy(vd, v_ref)
                _deinterleave(dod, dgs[gi], dil, BF16)
                _deinterleave(ld, lgs[gi], dil)
                _deinterleave(cd, cgs[gi], dil)
                dkd[...] = jnp.zeros_like(dkd)
                dvd[...] = jnp.zeros_like(dvd)
                flat = lambda t: t.reshape(CU * BLK, HEAD_DIM)
                for u0 in range(0, NUNITS, CU):
                    q, s_own, s_prev = _chunk_scores(u0, nb, qd, kd)
                    lse, cv, do = _blocks(ld, u0), _blocks(cd, u0), _blocks(dod, u0)
                    own = slice((u0 + 1) * BLK, (u0 + 1 + CU) * BLK)
                    p_own = jnp.exp(s_own - lse)
                    ds_own = (p_own * (_dot(do, _blocks(vd, u0 + 1), B_QK) + cv) * SCALE).astype(BF16)
                    dq = _dot(ds_own, _blocks(kd, u0 + 1), B_PV)
                    dkd[own, :] += flat(_dot(ds_own, q, B_TN))
                    dvd[own, :] += flat(_dot(p_own.astype(BF16), do, B_TN))
                    if s_prev is not None:
                        prev = slice(u0 * BLK, (u0 + CU) * BLK)
                        p_prev = jnp.exp(s_prev - lse)
                        ds_prev = (p_prev * (_dot(do, _blocks(vd, u0), B_QK) + cv) * SCALE).astype(BF16)
                        dq = dq + _dot(ds_prev, _blocks(kd, u0), B_PV)
                        dkd[prev, :] += flat(_dot(ds_prev, q, B_TN))
                        dvd[prev, :] += flat(_dot(p_prev.astype(BF16), do, B_TN))
                    dqd[u0 * BLK:(u0 + CU) * BLK, :] = flat(dq)
                _interleave(tmp, dqd, dil)
                dq_ref[...] = _rope_t(tmp[...], c, a, b).astype(BF16)
                _interleave(tmp, dkd, dil, BLK)
                dk_ref[...] = _rope_t(tmp[...], c, a, b).astype(BF16)
                _interleave(tmp, dvd, dil, BLK)
                dv_ref[...] = tmp[...].astype(BF16)

    tab = pl.BlockSpec((S, HEAD_DIM), lambda h, p: (0, 0))
    hspec = lambda base: pl.BlockSpec((S, HEAD_DIM), lambda h, p: (0, base + h))
    gspec = pl.BlockSpec((S, HEAD_DIM), lambda h, p: (0, jnp.maximum(p - 1, 0) * 8 + h))
    return pl.pallas_call(
        body, grid=(8, 4),
        in_specs=[gspec, gspec, gspec, hspec(BG_COL), hspec(8), tab, tab, tab,
                  hspec(0), hspec(8), hspec(16), hspec(0), hspec(8), hspec(16)],
        out_specs=[gspec, gspec, gspec, hspec(0)],
        out_shape=[jax.ShapeDtypeStruct((S, 3072), BF16)] * 3 + [jax.ShapeDtypeStruct((S, HALF), BF16)],
        scratch_shapes=[_slab(), _slab(BF16, S + BLK), _slab(BF16, S + BLK), _slab()] + [_slab() for _ in range(6)]
                       + [_slab(BF16), _slab(), _slab(), _slab(F32, S + BLK), _slab(F32, S + BLK)],
        compiler_params=_params(("parallel", "arbitrary"), VMEM_BIG), name="attn_bwd",
    )(*qkv, z0, dycat, *tabs, og, og, og, lg, lg, lg)


SGU_CH = 256
NCHUNK = TR // 128


def _ln_stats(x):
    mu = jnp.mean(x, axis=-1, keepdims=True)
    xc = x - mu
    r = lax.rsqrt(jnp.mean(xc * xc, axis=-1, keepdims=True) + EPS)
    return xc * r, r


def _ln_bwd(dy, xhat, r, g):
    dxh = dy * g
    return r * (dxh - jnp.mean(dxh, axis=-1, keepdims=True) - xhat * jnp.mean(dxh * xhat, axis=-1, keepdims=True))


def _tril_bf16(w):
    row = lax.broadcasted_iota(jnp.int32, w.shape, 0)
    col = lax.broadcasted_iota(jnp.int32, w.shape, 1)
    return jnp.where(row >= col, w, 0.0).astype(BF16)


def _sgu_gate(vn_s, s_s, w_ref, bb_ref):
    for h in range(4):
        wm = _tril_bf16(w_ref[h])
        bias = bb_ref[h]
        for ch in range(NCHUNK):
            rows, cols = slice(ch * 128, (ch + 1) * 128), slice(h * SGU_CH, (h + 1) * SGU_CH)
            s_s[rows, cols] = _dot(wm, vn_s[rows, cols]) + jnp.concatenate([bias, bias], axis=1)


WIN = HALO + TR
SUBL = 8


def _shifted_copies(dst, src):
    dst[0] = src[...]
    for b in range(1, SUBL):
        dst[b, 0:WIN - SUBL, :] = src[pl.ds(b, WIN - SUBL), :]


def _rows_at(copies, off, n):
    return copies[off % SUBL, pl.ds(off - off % SUBL, n), :]


def _conv_fwd(i, dval_ref, dglu_ref, hval_ref, hglu_ref, cw_ref, cb_ref, xw, xr, dcs):
    halo = hval_ref[...] * _sigmoid(hglu_ref[...])
    xw[0:HALO, :] = jnp.where(i > 0, halo, 0.0)
    xw[HALO:HALO + TR, :] = dval_ref[...] * _sigmoid(dglu_ref[...])
    _shifted_copies(xr, xw)
    for rb in range(TR // SUB):
        acc = jnp.broadcast_to(cb_ref[...], (SUB, HALF))
        for k in range(CONV_K):
            acc = acc + cw_ref[k:k + 1, :] * _rows_at(xr, rb * SUB + HALO - (CONV_K - 1) + k, SUB)
        dcs[rb * SUB:(rb + 1) * SUB, :] = acc


def _odd_in_specs():
    col = lambda j: pl.BlockSpec((TR, HALF), lambda i, *_: (i, j))
    prev = lambda j: pl.BlockSpec((HALO, HALF), lambda i, *_: (jnp.maximum(i * (TR // HALO) - 1, 0), j))
    return [col(0), col(1), col(2), col(3), col(4), col(5), prev(3), prev(4)]


def _full_spec(shape):
    return pl.BlockSpec(shape, lambda i, *_: (0,) * len(shape))


def _odd_fwd(z1, sgu_g, sgu_b, sgu_w, sgu_bb, conv_w, conv_b, cn_g, cn_b):
    def body(u_ref, v_ref, cg_ref, dval_ref, dglu_ref, dgate_ref, hval_ref, hglu_ref,
             g_ref, b_ref, w_ref, bb_ref, cw_ref, cb_ref, cng_ref, cnb_ref, out_ref, vn_s, s_s, xw, dcs, xr):
        i = pl.program_id(0)
        vhat, _ = _ln_stats(v_ref[...])
        vn_s[...] = (vhat * g_ref[...] + b_ref[...]).astype(BF16)
        _sgu_gate(vn_s, s_s, w_ref, bb_ref)
        cg = cg_ref[...]
        out_ref[:, 0:HALF] = (u_ref[...] * s_s[...] * (cg * _sigmoid(cg))).astype(BF16)
        _conv_fwd(i, dval_ref, dglu_ref, hval_ref, hglu_ref, cw_ref, cb_ref, xw, xr, dcs)
        dhat, _ = _ln_stats(dcs[...])
        dn = dhat * cng_ref[...] + cnb_ref[...]
        dgate = dgate_ref[...]
        out_ref[:, HALF:2 * HALF] = ((dn * _sigmoid(dn)) * (dgate * _sigmoid(dgate))).astype(BF16)

    vec = _full_spec((1, HALF))
    return pl.pallas_call(
        body, grid=(S // TR,),
        in_specs=_odd_in_specs() + [vec, vec, _full_spec((4, 128, 128)), _full_spec((4, 128, 128)),
                                    _full_spec((HALO, HALF)), vec, vec, vec],
        out_specs=pl.BlockSpec((TR, 2048), lambda i: (i, 0)),
        out_shape=jax.ShapeDtypeStruct((S, 2048), BF16),
        scratch_shapes=[pltpu.VMEM((TR, HALF), BF16), pltpu.VMEM((TR, HALF), F32),
                        pltpu.VMEM((WIN, HALF), F32), pltpu.VMEM((TR, HALF), F32), pltpu.VMEM((SUBL, WIN, HALF), F32)],
        compiler_params=_params(("parallel",), VMEM_BIG), name="odd_fwd",
    )(z1, z1, z1, z1, z1, z1, z1, z1, sgu_g, sgu_b, sgu_w, sgu_bb, conv_w, conv_b, cn_g, cn_b)


def _odd_bwd_a(z1, dycat, sgu_g, sgu_b, sgu_w, sgu_bb, conv_w, conv_b, cn_g, cn_b):
    def body(u_ref, v_ref, cg_ref, dval_ref, dglu_ref, dgate_ref, hval_ref, hglu_ref, dy_ref,
             g_ref, b_ref, w_ref, bb_ref, cw_ref, cb_ref, cng_ref, cnb_ref,
             dz_ref, ddc_ref, dw_ref, dbb_ref, dg_ref, db_ref, dcng_ref, dcnb_ref, dcb_ref,
             vn_s, s_s, xw, dcs, ds_s, dvn_s, xr):
        i = pl.program_id(0)
        vhat, rv = _ln_stats(v_ref[...])
        g = g_ref[...]
        vn_s[...] = (vhat * g + b_ref[...]).astype(BF16)
        _sgu_gate(vn_s, s_s, w_ref, bb_ref)
        silu_c, dsilu_c = _silu_and_grad(cg_ref[...])
        dyc = dy_ref[:, 0:HALF]
        u = u_ref[...]
        s = s_s[...]
        dz_ref[:, 0:HALF] = (dyc * s * silu_c).astype(BF16)
        dz_ref[:, 2 * HALF:3 * HALF] = (dyc * u * s * dsilu_c).astype(BF16)
        ds_s[...] = dyc * u * silu_c

        @pl.when(i == 0)
        def _():
            dw_ref[...] = jnp.zeros_like(dw_ref)
            dbb_ref[...] = jnp.zeros_like(dbb_ref)

        tril = lax.broadcasted_iota(jnp.int32, (128, 128), 0) >= lax.broadcasted_iota(jnp.int32, (128, 128), 1)
        for h in range(4):
            wm = _tril_bf16(w_ref[h])
            for ch in range(NCHUNK):
                rows, cols = slice(ch * 128, (ch + 1) * 128), slice(h * SGU_CH, (h + 1) * SGU_CH)
                ds = ds_s[rows, cols]
                dsb = ds.astype(BF16)
                dw_ref[h] += jnp.where(tril, _dot(dsb, vn_s[rows, cols], NT), 0.0)
                dbb_ref[h] += jnp.broadcast_to(jnp.sum(ds, axis=1, keepdims=True), (128, 128))
                dvn_s[rows, cols] = _dot(wm, dsb, TN)
        dvn = dvn_s[...]
        _acc_rows(dg_ref, dvn * vhat, i)
        _acc_rows(db_ref, dvn, i)
        dz_ref[:, HALF:2 * HALF] = _ln_bwd(dvn, vhat, rv, g).astype(BF16)

        _conv_fwd(i, dval_ref, dglu_ref, hval_ref, hglu_ref, cw_ref, cb_ref, xw, xr, dcs)
        dhat, rd = _ln_stats(dcs[...])
        cng = cng_ref[...]
        silu_n, dsilu_n = _silu_and_grad(dhat * cng + cnb_ref[...])
        silu_g, dsilu_g = _silu_and_grad(dgate_ref[...])
        dyd = dy_ref[:, HALF:2 * HALF]
        dz_ref[:, 5 * HALF:6 * HALF] = (dyd * silu_n * dsilu_g).astype(BF16)
        ddn = dyd * silu_g * dsilu_n
        _acc_rows(dcng_ref, ddn * dhat, i)
        _acc_rows(dcnb_ref, ddn, i)
        ddc = _ln_bwd(ddn, dhat, rd, cng)
        ddc_ref[...] = ddc
        _acc_rows(dcb_ref, ddc, i)

    vec = _full_spec((1, HALF))
    sq = _full_spec((4, 128, 128))
    return pl.pallas_call(
        body, grid=(S // TR,),
        in_specs=_odd_in_specs() + [pl.BlockSpec((TR, 2048), lambda i: (i, 0)),
                                    vec, vec, sq, sq, _full_spec((HALO, HALF)), vec, vec, vec],
        out_specs=[pl.BlockSpec((TR, ODD_IN), lambda i: (i, 0)), pl.BlockSpec((TR, HALF), lambda i: (i, 0)),
                   sq, sq, vec, vec, vec, vec, vec],
        out_shape=[jax.ShapeDtypeStruct((S, ODD_IN), BF16), jax.ShapeDtypeStruct((S, HALF), F32),
                   jax.ShapeDtypeStruct((4, 128, 128), F32), jax.ShapeDtypeStruct((4, 128, 128), F32)]
                  + [jax.ShapeDtypeStruct((1, HALF), F32)] * 5,
        scratch_shapes=[pltpu.VMEM((TR, HALF), BF16), pltpu.VMEM((TR, HALF), F32),
                        pltpu.VMEM((WIN, HALF), F32), pltpu.VMEM((TR, HALF), F32),
                        pltpu.VMEM((TR, HALF), F32), pltpu.VMEM((TR, HALF), F32), pltpu.VMEM((SUBL, WIN, HALF), F32)],
        compiler_params=_params(("arbitrary",), VMEM_BIG), name="odd_bwd_a",
    )(z1, z1, z1, z1, z1, z1, z1, z1, dycat, sgu_g, sgu_b, sgu_w, sgu_bb, conv_w, conv_b, cn_g, cn_b)


def _odd_bwd_b(z1, ddc, dz1, conv_w):
    nt = S // TR

    def body(dval_ref, dglu_ref, hval_ref, hglu_ref, ddc_ref, hddc_ref, cw_ref, dz_in_ref,
             dz_ref, dcw_ref, xw, dwin, dxs, xr, dr):
        del dz_in_ref
        i, j = pl.program_id(0), pl.program_id(1)
        sg = _sigmoid(dglu_ref[...])
        dval = dval_ref[...]

        @pl.when(j == 0)
        def _():
            halo = hval_ref[...] * _sigmoid(hglu_ref[...])
            xw[0:HALO, :] = jnp.where(i > 0, halo, 0.0)
            xw[HALO:HALO + TR, :] = dval * sg
            dwin[0:TR, :] = ddc_ref[...]
            dwin[TR:TR + HALO, :] = jnp.where(i < nt - 1, hddc_ref[...], 0.0)
            _shifted_copies(xr, xw)
            _shifted_copies(dr, dwin)

            @pl.when(i == 0)
            def _():
                dcw_ref[...] = jnp.zeros_like(dcw_ref)

            for rb in range(TR // SUB):
                acc = jnp.zeros((SUB, HALF), F32)
                for k in range(CONV_K):
                    acc = acc + cw_ref[k:k + 1, :] * _rows_at(dr, rb * SUB + (CONV_K - 1) - k, SUB)
                dxs[rb * SUB:(rb + 1) * SUB, :] = acc
            for k in range(CONV_K):
                acc = jnp.zeros((SUB, HALF), F32)
                for rb in range(TR // SUB):
                    acc = acc + dwin[rb * SUB:(rb + 1) * SUB, :] * _rows_at(xr, rb * SUB + HALO - (CONV_K - 1) + k, SUB)
                dcw_ref[k:k + 1, :] += jnp.sum(acc, axis=0, keepdims=True)
            dz_ref[...] = (dxs[...] * sg).astype(BF16)

        @pl.when(j == 1)
        def _():
            dz_ref[...] = (dxs[...] * dval * sg * (1.0 - sg)).astype(BF16)

    col = lambda c: pl.BlockSpec((TR, HALF), lambda i, j: (i, c))
    prev = lambda c: pl.BlockSpec((HALO, HALF), lambda i, j: (jnp.maximum(i * (TR // HALO) - 1, 0), c))
    nxt = pl.BlockSpec((HALO, HALF), lambda i, j: (jnp.minimum((i + 1) * (TR // HALO), S // HALO - 1), 0))
    return pl.pallas_call(
        body, grid=(nt, 2),
        in_specs=[col(3), col(4), prev(3), prev(4), pl.BlockSpec((TR, HALF), lambda i, j: (i, 0)), nxt,
                  _full_spec((HALO, HALF)), pl.BlockSpec(memory_space=pl.ANY)],
        out_specs=[pl.BlockSpec((TR, HALF), lambda i, j: (i, 3 + j)), _full_spec((HALO, HALF))],
        out_shape=[jax.ShapeDtypeStruct((S, ODD_IN), BF16), jax.ShapeDtypeStruct((HALO, HALF), F32)],
        scratch_shapes=[pltpu.VMEM((WIN, HALF), F32), pltpu.VMEM((WIN, HALF), F32), pltpu.VMEM((TR, HALF), F32),
                        pltpu.VMEM((SUBL, WIN, HALF), F32), pltpu.VMEM((SUBL, WIN, HALF), F32)],
        input_output_aliases={7: 0},
        compiler_params=_params(("arbitrary", "arbitrary"), VMEM_BIG), name="odd_bwd_b",
    )(z1, z1, z1, z1, ddc, ddc, conv_w, dz1)


def _cast_bf16(w, name):
    r, c = w.shape
    tr = min(r, 256)
    def body(i_ref, o_ref):
        o_ref[...] = i_ref[...].astype(BF16)

    return pl.pallas_call(
        body, grid=(r // tr,), in_specs=[pl.BlockSpec((tr, c), lambda i: (i, 0))],
        out_specs=pl.BlockSpec((tr, c), lambda i: (i, 0)), out_shape=jax.ShapeDtypeStruct((r, c), BF16),
        compiler_params=_params(("parallel",)), name=name,
    )(w)


def _adamw(w, g, m, v):
    m = ADAM_B1 * m + (1.0 - ADAM_B1) * g
    v = ADAM_B2 * v + (1.0 - ADAM_B2) * (g * g)
    m_hat = m / (1.0 - ADAM_B1 ** ADAM_STEP)
    v_hat = v / (1.0 - ADAM_B2 ** ADAM_STEP)
    delta = -ADAM_LR * (m_hat / (jnp.sqrt(v_hat) + ADAM_EPS) + ADAM_WD * w)
    return delta, m, v


def _adam_reduce(parts, w, m, v, name, dep=None):
    r, c = w.shape
    tr = min(r, 128)
    deps = [] if dep is None else [dep]
    nparts = parts.shape[0]

    def body(p_ref, w_ref, m_ref, v_ref, *rest):
        g_ref, d_ref, nm_ref, nv_ref = rest[len(deps):]
        g = p_ref[0].astype(F32)
        for d in range(1, nparts):
            g = g + p_ref[d].astype(F32)
        g_ref[...] = g
        d_ref[...], nm_ref[...], nv_ref[...] = _adamw(w_ref[...], g, m_ref[...], v_ref[...])

    spec = pl.BlockSpec((tr, c), lambda i: (i, 0))
    return pl.pallas_call(
        body, grid=(r // tr,),
        in_specs=[pl.BlockSpec((nparts, tr, c), lambda i: (0, i, 0)), spec, spec, spec] + [ANY_SPEC] * len(deps),
        out_specs=[spec] * 4, out_shape=[jax.ShapeDtypeStruct((r, c), F32)] * 4,
        compiler_params=_params(("parallel",), VMEM_BIG), name=name,
    )(parts, w, m, v, *deps)


def _sum_parts(parts, name, dep=None):
    r = parts.shape[1]
    tr = 8
    for cand in (512, 256, 128, 64, 32, 16, 8):
        if r % cand == 0:
            tr = cand
            break
    deps = [] if dep is None else [dep]

    def body(p_ref, *rest):
        g = p_ref[0]
        for d in range(1, NDEV):
            g = g + p_ref[d]
        rest[-1][...] = g

    return pl.pallas_call(
        body, grid=(r // tr,), in_specs=[pl.BlockSpec((NDEV, tr, 128), lambda i: (0, i, 0))] + [ANY_SPEC] * len(deps),
        out_specs=pl.BlockSpec((tr, 128), lambda i: (i, 0)), out_shape=jax.ShapeDtypeStruct((r, 128), F32),
        compiler_params=_params(("parallel",)), name=name,
    )(parts, *deps)


def _sum_unpack(parts, rows, name):
    def body(p_ref, *outs):
        off = 0
        for o_ref, n in zip(outs, rows):
            acc = p_ref[0, off:off + n, :]
            for d in range(1, NDEV):
                acc = acc + p_ref[d, off:off + n, :]
            o_ref[...] = acc
            off += n

    return pl.pallas_call(
        body, grid=(1,), in_specs=[pl.BlockSpec(parts.shape, lambda i: (0, 0, 0))],
        out_specs=[pl.BlockSpec((n, 128), lambda i: (0, 0)) for n in rows],
        out_shape=[jax.ShapeDtypeStruct((n, 128), F32) for n in rows],
        compiler_params=_params(("arbitrary",), VMEM_BIG), name=name,
    )(parts)


def _adam_small(ws, gs, g_specs, ms, vs, name):
    n = len(ws)

    def body(*refs):
        w_r, g_r, m_r, v_r = refs[:n], refs[n:2 * n], refs[2 * n:3 * n], refs[3 * n:4 * n]
        outs = refs[4 * n:]
        for i in range(n):
            g = g_r[i][...]
            outs[4 * i][...] = g
            outs[4 * i + 1][...], outs[4 * i + 2][...], outs[4 * i + 3][...] = _adamw(
                w_r[i][...], g, m_r[i][...], v_r[i][...])

    whole = lambda a: pl.BlockSpec(a.shape, lambda i, nd=a.ndim: (0,) * nd)
    outs = pl.pallas_call(
        body, grid=(1,),
        in_specs=[whole(a) for a in ws] + list(g_specs) + [whole(a) for a in ms] + [whole(a) for a in vs],
        out_specs=[whole(a) for a in ws for _ in range(4)],
        out_shape=[jax.ShapeDtypeStruct(a.shape, F32) for a in ws for _ in range(4)],
        compiler_params=_params(("arbitrary",), VMEM_BIG), name=name,
    )(*ws, *gs, *ms, *vs)
    return [outs[4 * i:4 * i + 4] for i in range(n)]


MASKS = [(mx, my, mc) for mx in (0, 1) for my in (0, 1) for mc in (0, 1)][1:]


def _exchange(arrays, scatter, name):
    nt = len(arrays)
    out_shape = [jax.ShapeDtypeStruct(((NDEV,) + a.shape) if not scatter else a.shape, a.dtype) for a in arrays]

    def body(*refs):
        ins, outs = refs[:nt], refs[nt:2 * nt]
        send_sems, recv_sems, local_sems = refs[2 * nt:]
        x, y, c = lax.axis_index("x"), lax.axis_index("y"), lax.axis_index("c")
        me = 4 * x + 2 * y + c
        copies = []
        for t in range(nt):
            src_own = ins[t].at[me] if scatter else ins[t]
            loc = pltpu.make_async_copy(src_own, outs[t].at[me], local_sems.at[t])
            loc.start()
            copies.append(loc)
            for k, (mx, my, mc) in enumerate(MASKS):
                px, py, pc = (x + mx) % 2, (y + my) % 2, (c + mc) % 2
                peer = 4 * px + 2 * py + pc
                src = ins[t].at[peer] if scatter else ins[t]
                rc = pltpu.make_async_remote_copy(
                    src_ref=src, dst_ref=outs[t].at[me], send_sem=send_sems.at[t, k], recv_sem=recv_sems.at[t, k],
                    device_id=(px, py, pc), device_id_type=MESH)
                rc.start()
                copies.append(rc)
        for cp in copies:
            cp.wait()

    hbm = pl.BlockSpec(memory_space=pl.ANY)
    return pl.pallas_call(
        body, in_specs=[hbm] * nt, out_specs=[hbm] * nt, out_shape=out_shape,
        scratch_shapes=[pltpu.SemaphoreType.DMA((nt, 7)), pltpu.SemaphoreType.DMA((nt, 7)),
                        pltpu.SemaphoreType.DMA((nt,))],
        name=name,
    )(*arrays)


SEM_SPEC = pl.BlockSpec(memory_space=pltpu.SEMAPHORE)
EFFECT = pltpu.SideEffectType.DATAFLOW_SIDE_EFFECTING


def _direct_plan(scatter):
    def plan(x, y, c, srcs, lands):
        me = 4 * x + 2 * y + c
        local, remote = [], []
        for src, land in zip(srcs, lands):
            local.append((src.at[me] if scatter else src, land.at[me]))
            for mx, my, mc in MASKS:
                px, py, pc = (x + mx) % 2, (y + my) % 2, (c + mc) % 2
                blk = src.at[4 * px + 2 * py + pc] if scatter else src
                remote.append((blk, land.at[me], (px, py, pc)))
        return local, remote
    return plan


def _split_start(name, srcs, land_shapes, plan, n_local, n_remote, dep=None):
    ns, nl = len(srcs), len(land_shapes)
    deps = [] if dep is None else [dep]
    lands = [lax.empty(s.shape, s.dtype) for s in land_shapes]

    def body(*refs):
        ins, lz = refs[:ns], refs[ns:ns + nl]
        outs = refs[ns + nl + len(deps):]
        send_sems, recv_sems, token, local_sems = outs[0], outs[1], outs[2 + ns + nl], outs[3 + ns + nl]
        local, remote = plan(lax.axis_index("x"), lax.axis_index("y"), lax.axis_index("c"), ins, lz)
        own = [pltpu.make_async_copy(src, dst, local_sems.at[i]) for i, (src, dst) in enumerate(local)]
        for cp in own:
            cp.start()
        for cp in own:
            cp.wait()
        for k, (src, dst, peer) in enumerate(remote):
            pltpu.make_async_remote_copy(src_ref=src, dst_ref=dst, send_sem=send_sems.at[k], recv_sem=recv_sems.at[k],
                                         device_id=peer, device_id_type=MESH).start()
        token[...] = jnp.zeros_like(token)

    hbm = lambda a: pltpu.HBM(a.shape, a.dtype)
    outs = pl.pallas_call(
        body, name=name,
        out_shape=(pltpu.SemaphoreType.DMA((n_remote,)), pltpu.SemaphoreType.DMA((n_remote,)),
                   *[hbm(a) for a in srcs], *[hbm(a) for a in lands], jax.ShapeDtypeStruct((8, 128), F32)),
        in_specs=[ANY_SPEC] * (ns + nl + len(deps)),
        out_specs=(SEM_SPEC, SEM_SPEC, *[ANY_SPEC] * (ns + nl), pl.BlockSpec(memory_space=pltpu.VMEM)),
        scratch_shapes=[pltpu.SemaphoreType.DMA((n_local,))],
        input_output_aliases={i: 2 + i for i in range(ns + nl)},
        compiler_params=pltpu.CompilerParams(has_side_effects=EFFECT),
    )(*[pltpu.with_memory_space_constraint(a, pltpu.HBM) for a in srcs],
      *[pltpu.with_memory_space_constraint(a, pltpu.HBM) for a in lands], *deps)
    return dict(sems=outs[:2], srcs=outs[2:2 + ns], lands=outs[2 + ns:2 + ns + nl], token=outs[-1],
                plan=plan, n_remote=n_remote)


def _split_wait(name, handle, after):
    srcs, lands, plan = handle["srcs"], handle["lands"], handle["plan"]
    ns, nl = len(srcs), len(lands)

    def body(*refs):
        ins, lz = refs[:ns], refs[ns:ns + nl]
        send_sems, recv_sems = refs[ns + nl], refs[ns + nl + 1]
        _, remote = plan(lax.axis_index("x"), lax.axis_index("y"), lax.axis_index("c"), ins, lz)
        for k, (src, dst, peer) in enumerate(remote):
            cp = pltpu.make_async_remote_copy(src_ref=src, dst_ref=dst, send_sem=send_sems.at[k],
                                              recv_sem=recv_sems.at[k], device_id=peer, device_id_type=MESH)
            cp.wait_send()
            cp.wait_recv()

    hbm = lambda a: pltpu.HBM(a.shape, a.dtype)
    outs = pl.pallas_call(
        body, name=name, out_shape=(*[hbm(a) for a in srcs], *[hbm(a) for a in lands]),
        in_specs=[ANY_SPEC] * (ns + nl) + [SEM_SPEC, SEM_SPEC, ANY_SPEC], out_specs=tuple([ANY_SPEC] * (ns + nl)),
        input_output_aliases={i: i for i in range(ns + nl)},
        compiler_params=pltpu.CompilerParams(has_side_effects=EFFECT),
    )(*srcs, *lands, *handle["sems"], after)
    return list(outs[ns:])


def _sc_exchange(name, collective_id, arrays, scatter):
    nt = len(arrays)
    out_type = [jax.ShapeDtypeStruct(a.shape if scatter else (NDEV,) + a.shape, a.dtype) for a in arrays]

    def body(*refs):
        ins, outs = refs[:nt], refs[nt:2 * nt]
        send_sems, recv_sems, local_sems = refs[2 * nt:3 * nt], refs[3 * nt:4 * nt], refs[4 * nt:5 * nt]
        x, y, c = lax.axis_index("x"), lax.axis_index("y"), lax.axis_index("c")
        peers = [(mx + x - 2 * mx * x, my + y - 2 * my * y, mc + c - 2 * mc * c) for mx, my, mc in MASKS]
        barrier = pltpu.get_barrier_semaphore()
        for peer in peers:
            pl.semaphore_signal(barrier, inc=1, device_id=peer, device_id_type=MESH)
        pl.semaphore_wait(barrier, len(peers))
        me = 4 * x + 2 * y + c
        own = []
        for t in range(nt):
            cp = pltpu.make_async_copy(ins[t].at[me] if scatter else ins[t], outs[t].at[me], local_sems[t])
            cp.start()
            own.append(cp)
            for px, py, pc in peers:
                src = ins[t].at[4 * px + 2 * py + pc] if scatter else ins[t]
                pltpu.make_async_remote_copy(src_ref=src, dst_ref=outs[t].at[me], send_sem=send_sems[t],
                                             recv_sem=recv_sems[t], device_id=(px, py, pc), device_id_type=MESH).start()
        for t in range(nt):
            own[t].wait()
            seven = outs[t].at[pl.ds(0, NDEV - 1)]
            drain = pltpu.make_async_remote_copy(src_ref=seven, dst_ref=seven, send_sem=send_sems[t],
                                                 recv_sem=recv_sems[t], device_id=(x, y, c), device_id_type=MESH)
            drain.wait_send()
            drain.wait_recv()

    return pl.kernel(
        body, out_type=out_type, mesh=plsc.ScalarSubcoreMesh(axis_name="sequencer", num_cores=1),
        scratch_types=[pltpu.SemaphoreType.DMA] * (3 * nt),
        compiler_params=pltpu.CompilerParams(collective_id=collective_id), name=name,
    )(*arrays)


def _sc_gather_two_level(name, collective_id, arrays):
    nt = len(arrays)
    out_type = [jax.ShapeDtypeStruct((NDEV,) + a.shape, a.dtype) for a in arrays]

    def body(*refs):
        ins, outs = refs[:nt], refs[nt:2 * nt]
        sems = refs[2 * nt:]
        send_sems, sib_sems, local_sems = sems[:nt], sems[nt:2 * nt], sems[2 * nt:3 * nt]
        ici_sems = [sems[3 * nt + 3 * t:3 * nt + 3 * t + 3] for t in range(nt)]
        x, y, c = lax.axis_index("x"), lax.axis_index("y"), lax.axis_index("c")
        sibling = (x, y, 1 - c)
        chips = [(1 - x, y), (x, 1 - y), (1 - x, 1 - y)]
        barrier = pltpu.get_barrier_semaphore()
        for peer in [sibling] + [(cx, cy, c) for cx, cy in chips]:
            pl.semaphore_signal(barrier, inc=1, device_id=peer, device_id_type=MESH)
        pl.semaphore_wait(barrier, 4)
        me = 4 * x + 2 * y + c

        def push(t, src, slot, recv_sem, to):
            pltpu.make_async_remote_copy(src_ref=src, dst_ref=outs[t].at[slot], send_sem=send_sems[t],
                                         recv_sem=recv_sem, device_id=to, device_id_type=MESH).start()

        own = []
        for t in range(nt):
            cp = pltpu.make_async_copy(ins[t], outs[t].at[me], local_sems[t])
            cp.start()
            own.append(cp)
            for j, (cx, cy) in enumerate(chips):
                push(t, ins[t], me, ici_sems[t][j], (cx, cy, c))
            push(t, ins[t], me, sib_sems[t], sibling)
        for t in range(nt):
            for j, (cx, cy) in enumerate(chips):
                slot = 4 * cx + 2 * cy + c
                landed = outs[t].at[slot]
                pltpu.make_async_remote_copy(src_ref=landed, dst_ref=landed, send_sem=send_sems[t],
                                             recv_sem=ici_sems[t][j], device_id=(cx, cy, c),
                                             device_id_type=MESH).wait_recv()
                push(t, landed, slot, sib_sems[t], sibling)
        for t in range(nt):
            own[t].wait()
            four, seven = outs[t].at[pl.ds(0, 4)], outs[t].at[pl.ds(0, 7)]
            pltpu.make_async_remote_copy(src_ref=four, dst_ref=four, send_sem=send_sems[t], recv_sem=sib_sems[t],
                                         device_id=sibling, device_id_type=MESH).wait_recv()
            pltpu.make_async_remote_copy(src_ref=seven, dst_ref=seven, send_sem=send_sems[t], recv_sem=sib_sems[t],
                                         device_id=sibling, device_id_type=MESH).wait_send()

    return pl.kernel(
        body, out_type=out_type, mesh=plsc.ScalarSubcoreMesh(axis_name="sequencer", num_cores=1),
        scratch_types=[pltpu.SemaphoreType.DMA] * (6 * nt),
        compiler_params=pltpu.CompilerParams(collective_id=collective_id), name=name,
    )(*arrays)


def _sc_sibling_exchange(name, collective_id, src, out_shape, pieces):
    def body(src_ref, out_ref, send_sem, recv_sem):
        x, y, c = lax.axis_index("x"), lax.axis_index("y"), lax.axis_index("c")
        sibling = (x, y, 1 - c)
        barrier = pltpu.get_barrier_semaphore()
        pl.semaphore_signal(barrier, inc=1, device_id=sibling, device_id_type=MESH)
        pl.semaphore_wait(barrier, 1)
        for piece, lands in pieces(c, src_ref, out_ref):
            pltpu.make_async_remote_copy(src_ref=piece, dst_ref=lands, send_sem=send_sem, recv_sem=recv_sem,
                                         device_id=sibling, device_id_type=MESH).start()
        drain = pltpu.make_async_remote_copy(src_ref=out_ref, dst_ref=out_ref, send_sem=send_sem, recv_sem=recv_sem,
                                             device_id=sibling, device_id_type=MESH)
        drain.wait_send()
        drain.wait_recv()

    return pl.kernel(
        body, out_type=jax.ShapeDtypeStruct(out_shape, src.dtype),
        mesh=plsc.ScalarSubcoreMesh(axis_name="sequencer", num_cores=1), scratch_types=[pltpu.SemaphoreType.DMA] * 2,
        compiler_params=pltpu.CompilerParams(collective_id=collective_id), name=name,
    )(src)


def _sc_chip_scatter(name, collective_id, q):
    def body(q_ref, out_ref, send_sem, recv_sem, local_sem):
        x, y, c = lax.axis_index("x"), lax.axis_index("y"), lax.axis_index("c")
        chips = [(1 - x, y), (x, 1 - y), (1 - x, 1 - y)]
        barrier = pltpu.get_barrier_semaphore()
        for cx, cy in chips:
            pl.semaphore_signal(barrier, inc=1, device_id=(cx, cy, c), device_id_type=MESH)
        pl.semaphore_wait(barrier, 3)
        mine = 2 * x + y
        own = pltpu.make_async_copy(q_ref.at[mine], out_ref.at[mine], local_sem)
        own.start()
        for cx, cy in chips:
            pltpu.make_async_remote_copy(src_ref=q_ref.at[2 * cx + cy], dst_ref=out_ref.at[mine], send_sem=send_sem,
                                         recv_sem=recv_sem, device_id=(cx, cy, c), device_id_type=MESH).start()
        own.wait()
        three = out_ref.at[pl.ds(0, 3)]
        drain = pltpu.make_async_remote_copy(src_ref=three, dst_ref=three, send_sem=send_sem, recv_sem=recv_sem,
                                             device_id=(x, y, c), device_id_type=MESH)
        drain.wait_send()
        drain.wait_recv()

    return pl.kernel(
        body, out_type=jax.ShapeDtypeStruct(q.shape, q.dtype),
        mesh=plsc.ScalarSubcoreMesh(axis_name="sequencer", num_cores=1), scratch_types=[pltpu.SemaphoreType.DMA] * 3,
        compiler_params=pltpu.CompilerParams(collective_id=collective_id), name=name,
    )(q)


def _mm_pair_dw(h_own, dz, h_sib, dz_sib, nb, name, dep=None):
    tn = 512 if nb % 512 == 0 else nb
    per = nb // tn
    o_spec = pl.BlockSpec((None, D, tn), lambda i, j, k: (j // per, 0, j % per))
    part = _matmul(
        h_own, dz, dn=TN, grid=(1, 4 * per, 1),
        a_spec=pl.BlockSpec((S, D), lambda i, j, k: (0, 0)),
        b_spec=pl.BlockSpec((S, tn), lambda i, j, k: (0, (2 * (j // per) + lax.axis_index("c")) * per + j % per)),
        o_spec=o_spec, out_shape=(4, D, nb), out_dtype=F32, acc_shape=(D, tn), name=name + "_own", dep=dep)

    def body(a_ref, b_ref, p_ref, o_ref):
        o_ref[...] = (p_ref[...] + _dot(a_ref[...], b_ref[...], TN)).astype(BF16)

    return pl.pallas_call(
        body, grid=(1, 4 * per, 1),
        in_specs=[pl.BlockSpec((S, D), lambda i, j, k: (0, 0)), pl.BlockSpec((S, tn), lambda i, j, k: (0, j)), o_spec],
        out_specs=o_spec, out_shape=jax.ShapeDtypeStruct((4, D, nb), BF16),
        compiler_params=_params(("parallel", "parallel", "arbitrary"), VMEM_BIG), name=name + "_sibling",
    )(h_sib, dz_sib, part)


SMALL = {
    "e_pre_norm": ((2048,), None), "e_pool_w": ((4, 256, 256), 1), "e_pool_scale": ((1024,), None),
    "e_post_norm": ((2048,), None), "o_pre_norm": ((2048,), 0), "o_sgu_norm_g": ((1024,), 0),
    "o_sgu_norm_b": ((1024,), 0), "o_sgu_w": ((4, 128, 128), None), "o_sgu_b": ((4, 128), None),
    "o_conv_w": ((31, 1024), 1), "o_conv_b": ((1024,), 0), "o_conv_norm_g": ((1024,), 0),
    "o_conv_norm_b": ((1024,), 0), "o_post_norm": ((2048,), 0),
}
SMALL_SHARDED = [n for n, (_, ax) in SMALL.items() if ax is not None]


def _shard_shape(name):
    shape, ax = SMALL[name]
    if ax is None:
        return shape
    return tuple(s // NDEV if i == ax else s for i, s in enumerate(shape))


def _pack(arrs, row_multiple=1):
    flat = jnp.concatenate([a.reshape(-1) for a in arrs])
    pad = -flat.shape[0] % (128 * row_multiple)
    return jnp.concatenate([flat, jnp.zeros((pad,), F32)]).reshape(-1, 128)


def _small_views(name):
    shape, ax = SMALL[name]
    me = lambda: 4 * lax.axis_index("x") + 2 * lax.axis_index("y") + lax.axis_index("c")
    if ax is None:
        view = (int(np.prod(shape)) // 128, 128)
        return view, view, pl.BlockSpec(view, lambda i: (0, 0))
    if len(shape) == 1:
        n = shape[0] // NDEV
        return (1, n), (NDEV, 1, n), pl.BlockSpec((None, 1, n), lambda i: (me(), 0, 0))
    part = _shard_shape(name)
    return part, shape, pl.BlockSpec(part, lambda i: tuple(me() if d == ax else 0 for d in range(len(shape))))


BIG = ("e_w_in", "e_w_out", "o_w_in", "o_w_out")
WEIGHTS = ["e_pre_norm", "e_w_in", "e_pool_w", "e_pool_scale", "e_w_out", "e_post_norm", "o_pre_norm", "o_w_in",
           "o_sgu_norm_g", "o_sgu_norm_b", "o_sgu_w", "o_sgu_b", "o_conv_w", "o_conv_b", "o_conv_norm_g",
           "o_conv_norm_b", "o_w_out", "o_post_norm"]


def kernel(x, e_pre_norm, e_w_in, e_pool_w, e_pool_scale, e_w_out, e_post_norm, o_pre_norm, o_w_in, o_sgu_norm_g, o_sgu_norm_b, o_sgu_w, o_sgu_b, o_conv_w, o_conv_b, o_conv_norm_g, o_conv_norm_b, o_w_out, o_post_norm, loss_target, m_e_pre_norm, m_e_w_in, m_e_pool_w, m_e_pool_scale, m_e_w_out, m_e_post_norm, m_o_pre_norm, m_o_w_in, m_o_sgu_norm_g, m_o_sgu_norm_b, m_o_sgu_w, m_o_sgu_b, m_o_conv_w, m_o_conv_b, m_o_conv_norm_g, m_o_conv_norm_b, m_o_w_out, m_o_post_norm, v_e_pre_norm, v_e_w_in, v_e_pool_w, v_e_pool_scale, v_e_w_out, v_e_post_norm, v_o_pre_norm, v_o_w_in, v_o_sgu_norm_g, v_o_sgu_norm_b, v_o_sgu_w, v_o_sgu_b, v_o_conv_w, v_o_conv_b, v_o_conv_norm_g, v_o_conv_norm_b, v_o_w_out, v_o_post_norm):
    given = dict(locals())
    w = {n: given[n][0] for n in WEIGHTS}
    m = {n: given["m_" + n][0] for n in WEIGHTS}
    v = {n: given["v_" + n][0] for n in WEIGHTS}
    me = 4 * lax.axis_index("x") + 2 * lax.axis_index("y") + lax.axis_index("c")
    x, target = x[0], loss_target[0]
    row = lambda a: a.reshape(1, -1)

    bf = {n: _cast_bf16(w[n], "cast_" + n) for n in BIG}
    wg_e_in, small_rows = _sc_gather_two_level("gather_a", 0, [bf["e_w_in"], _pack([w[n] for n in SMALL_SHARDED])])
    wg_e_out, wg_o_in, wg_o_out = _sc_gather_two_level("gather_b", 1, [bf["e_w_out"], bf["o_w_in"], bf["o_w_out"]])
    h0 = _pre0_fwd(x, row(w["e_pre_norm"]))
    h0_sib = _sc_sibling_exchange("swap_h0", 8, h0, h0.shape, lambda c, src, out: [(src, out)])
    p = {n: w[n] for n in SMALL if SMALL[n][1] is None}
    small_rows = small_rows.reshape(NDEV, -1)
    off = 0
    for n in SMALL_SHARDED:
        shp, ax = _shard_shape(n), SMALL[n][1]
        cnt = int(np.prod(shp))
        blk = small_rows[:, off:off + cnt].reshape((NDEV,) + shp)
        p[n] = jnp.moveaxis(blk, 0, ax).reshape(SMALL[n][0])
        off += cnt
    tabs = _rope_tables()
    pool_w_bf = p["e_pool_w"].astype(BF16)
    sgu_bb = jnp.broadcast_to(p["o_sgu_b"][:, :, None], (4, 128, 128))
    conv_w = jnp.concatenate([p["o_conv_w"], jnp.zeros((HALO - CONV_K, HALF), F32)], axis=0)
    odd_p = (row(p["o_sgu_norm_g"]), row(p["o_sgu_norm_b"]), p["o_sgu_w"], sgu_bb, conv_w,
             row(p["o_conv_b"]), row(p["o_conv_norm_g"]), row(p["o_conv_norm_b"]))

    z0 = _mm_in(h0, wg_e_in, "mm_z0")
    ycat0 = _pool_fwd(z0, pool_w_bf, row(p["e_pool_scale"]))
    qkv = _qkv_prep(z0, tabs)
    ycat0, og, lg = _attn_fwd(z0, qkv, ycat0)
    w_out_e, w_out_o = wg_e_out.reshape(2048, D), wg_o_out.reshape(2048, D)
    y0 = _mm_out(ycat0, w_out_e, "mm_y0", h0_sib)
    x1, h1 = _post0_fwd(x, y0, row(p["e_post_norm"]), row(p["o_pre_norm"]))
    z1 = _mm_in(h1, wg_o_in, "mm_z1")
    ycat1 = _odd_fwd(z1, *odd_p)
    y1 = _mm_out(ycat1, w_out_o, "mm_y1")

    g = {}
    loss, dx2, dy1, g["o_post_norm"] = _post1_bwd(y1, x1, target, row(p["o_post_norm"]))
    loss = lax.psum(loss[0, 0], ("x", "y", "c"))
    parts = {}
    dw = _mm_out_dw(ycat1, dy1, "mm_dwout1").reshape(NDEV, 256, D)
    parts["o_w_out"], = _sc_exchange("scatter_o_w_out", 2, [dw], True)
    dycat1 = _mm_out_dx(dy1, w_out_o, "mm_dycat1", (dw, loss.reshape(1, 1)))
    dz1, ddc, g["o_sgu_w"], d_sgu_bb, g["o_sgu_norm_g"], g["o_sgu_norm_b"], g["o_conv_norm_g"], \
        g["o_conv_norm_b"], g["o_conv_b"] = _odd_bwd_a(z1, dycat1, *odd_p)
    dz1, d_conv_w = _odd_bwd_b(z1, ddc, dz1, conv_w)
    g["o_sgu_b"] = d_sgu_bb[:, :, 0]
    g["o_conv_w"] = d_conv_w[:CONV_K]
    grads, deltas, new_m, new_v = {}, {}, {}, {}

    def adam(n, dep):
        grads[n], deltas[n], new_m[n], new_v[n] = _adam_reduce(parts[n], w[n], m[n], v[n], "adam_" + n, dep)
        return new_v[n]

    pin = adam("o_w_out", d_conv_w)
    dw = _mm_in_dw(h1, dz1, ODD_IN // NDEV, "mm_dwin1", pin)
    parts["o_w_in"], = _sc_exchange("scatter_o_w_in", 3, [dw], True)
    dh1 = _mm_in_dx(dz1, wg_o_in, "mm_dh1", dw)
    dx1, dy0, g["o_pre_norm"], g["e_post_norm"] = _mid_bwd(dx2, dh1, x1, y0, row(p["o_pre_norm"]),
                                                           row(p["e_post_norm"]))
    dw = _mm_out_dw(ycat0, dy0, "mm_dwout0").reshape(NDEV, 256, D)
    parts["e_w_out"], = _sc_exchange("scatter_e_w_out", 4, [dw], True)
    dycat0 = _mm_out_dx(dy0, w_out_e, "mm_dycat0", dw)
    da_in, da_gate, g["e_pool_w"], g["e_pool_scale"] = _pool_bwd(z0, dycat0, pool_w_bf, row(p["e_pool_scale"]))
    dq, dk, dv, dbg = _attn_bwd(z0, qkv, og, lg, dycat0, tabs)
    dz0 = jnp.concatenate([da_in, da_gate, dq, dk, dv, dbg], axis=1)
    late = [n for n in SMALL if n not in ("e_pre_norm", "o_sgu_b")] + ["o_sgu_b"]
    nb = EVEN_IN // NDEV
    dz0_sib = _sc_sibling_exchange(
        "swap_dz0", 9, dz0, (S, 4 * nb),
        lambda c, src, out: [(src.at[:, pl.ds((2 * j + 1 - c) * nb, nb)], out.at[:, pl.ds(j * nb, nb)])
                             for j in range(4)])
    dw = _mm_pair_dw(h0, dz0, h0_sib, dz0_sib, nb, "mm_dwin0")
    parts["e_w_in"] = _sc_chip_scatter("scatter_e_w_in", 5, dw)
    recv_small, = _sc_exchange("gather_small_grads", 6, [_pack([g[n].reshape(SMALL[n][0]) for n in late], 512)], False)
    pin = adam("e_w_out", adam("o_w_in", dw))
    dh0 = _mm_in_dx(dz0, wg_e_in, "mm_dh0", (dw, pin))
    grad_x, g["e_pre_norm"] = _pre0_bwd(dx1, dh0, x, row(p["e_pre_norm"]))
    last, = _sc_exchange("gather_e_pre_norm_grad", 7, [g["e_pre_norm"].reshape(16, 128)], False)

    rows = [int(np.prod(SMALL[n][0])) // 128 for n in late]
    summed = dict(zip(late, _sum_unpack(recv_small, rows, "sum_small_grads")))
    pin = adam("e_w_in", grad_x)
    summed["e_pre_norm"] = _sum_parts(last, "sum_e_pre_norm_grad", pin)
    names = list(SMALL)
    views = [_small_views(n) for n in names]
    mine = lambda src: [src[n].reshape(vw[0]) for n, vw in zip(names, views)]
    res = _adam_small(mine(w), [summed[n].reshape(vw[1]) for n, vw in zip(names, views)], [vw[2] for vw in views],
                      mine(m), mine(v), "adam_small")
    for n, out in zip(names, res):
        grads[n], deltas[n], new_m[n], new_v[n] = [t.reshape(_shard_shape(n)) for t in out]

    lead = lambda a: a[None]
    return (loss, grad_x[None], *[lead(grads[n]) for n in WEIGHTS], *[lead(deltas[n]) for n in WEIGHTS],
            *[lead(new_m[n]) for n in WEIGHTS], *[lead(new_v[n]) for n in WEIGHTS])
```

```python
import functools

import numpy as np
import jax
import jax.numpy as jnp
from jax import lax
from jax.experimental import pallas as pl
from jax.experimental.pallas import tpu as pltpu
from jax.experimental.pallas import tpu_sc as plsc

F32 = jnp.float32
BF16 = jnp.bfloat16

S = 2048
D = 2048
NDEV = 8
EPS = 1e-6
NEG = -1e30
HEAD_DIM = 128
ROT_DIM = 32
ROPE_THETA = 500000.0
PATTERNS = ((128, 1), (512, 4), (2048, 16))
BLK = 128
EVEN_IN = 12288
ODD_IN = 6144
HALF = 1024
CONV_K = 31
HALO = 32
TR = 256
SUB = 32

ADAM_LR = 0.001
ADAM_B1 = 0.9
ADAM_B2 = 0.999
ADAM_EPS = 1e-08
ADAM_WD = 0.01
ADAM_STEP = 10

VMEM_BIG = 56 * 1024 * 1024
MESH = pl.DeviceIdType.MESH

NN = (((1,), (0,)), ((), ()))
NT = (((1,), (1,)), ((), ()))
TN = (((0,), (0,)), ((), ()))


def _dot(a, b, dn=NN):
    return lax.dot_general(a, b, dn, preferred_element_type=F32)


def _sigmoid(x):
    return 1.0 / (1.0 + jnp.exp(-x))


def _silu_and_grad(x):
    sg = _sigmoid(x)
    return x * sg, sg * (1.0 + x * (1.0 - sg))


def _params(sem, vmem=None):
    return pltpu.CompilerParams(dimension_semantics=sem, vmem_limit_bytes=vmem)


ANY_SPEC = pl.BlockSpec(memory_space=pl.ANY)


def _matmul(a, b, *, dn, grid, a_spec, b_spec, o_spec, out_shape, out_dtype, acc_shape, name, dep=None):
    nk = grid[2]
    deps = [] if dep is None else list(dep) if isinstance(dep, (tuple, list)) else [dep]

    def body(a_ref, b_ref, *rest):
        o_ref, acc = rest[len(deps)], rest[len(deps) + 1:]
        if nk == 1:
            o_ref[...] = _dot(a_ref[...], b_ref[...], dn).astype(o_ref.dtype)
            return
        acc_ref = acc[0]
        k = pl.program_id(2)

        @pl.when(k == 0)
        def _():
            acc_ref[...] = jnp.zeros_like(acc_ref)

        acc_ref[...] += _dot(a_ref[...], b_ref[...], dn)

        @pl.when(k == nk - 1)
        def _():
            o_ref[...] = acc_ref[...].astype(o_ref.dtype)

    return pl.pallas_call(
        body, grid=grid, in_specs=[a_spec, b_spec] + [ANY_SPEC] * len(deps), out_specs=o_spec,
        out_shape=jax.ShapeDtypeStruct(out_shape, out_dtype),
        scratch_shapes=[] if nk == 1 else [pltpu.VMEM(acc_shape, F32)],
        compiler_params=_params(("parallel", "parallel", "arbitrary"), VMEM_BIG), name=name,
    )(a, b, *deps)


TM = 2048


def _mm_in(h, wg, name):
    nb = wg.shape[2]
    tn = 512 if nb % 512 == 0 else nb
    per = nb // tn
    return _matmul(
        h, wg, dn=NN, grid=(S // TM, NDEV * per, 1),
        a_spec=pl.BlockSpec((TM, D), lambda i, j, k: (i, 0)),
        b_spec=pl.BlockSpec((None, D, tn), lambda i, j, k: (j // per, 0, j % per)),
        o_spec=pl.BlockSpec((TM, tn), lambda i, j, k: (i, j)),
        out_shape=(S, NDEV * nb), out_dtype=F32, acc_shape=(TM, tn), name=name)


def _mm_in_dx(dz, wg, name, dep=None):
    nb = wg.shape[2]
    return _matmul(
        dz, wg, dn=NT, grid=(S // TM, D // 1024, NDEV),
        a_spec=pl.BlockSpec((TM, nb), lambda i, j, k: (i, k)),
        b_spec=pl.BlockSpec((None, 1024, nb), lambda i, j, k: (k, j, 0)),
        o_spec=pl.BlockSpec((TM, 1024), lambda i, j, k: (i, j)),
        out_shape=(S, D), out_dtype=F32, acc_shape=(TM, 1024), name=name, dep=dep)


def _mm_in_dw(h, dz, nb, name, dep=None):
    tn = 512 if nb % 512 == 0 else nb
    per = nb // tn
    return _matmul(
        h, dz, dn=TN, grid=(D // TM, NDEV * per, 1),
        a_spec=pl.BlockSpec((S, TM), lambda i, j, k: (0, i)),
        b_spec=pl.BlockSpec((S, tn), lambda i, j, k: (0, j)),
        o_spec=pl.BlockSpec((None, TM, tn), lambda i, j, k: (j // per, i, j % per)),
        out_shape=(NDEV, D, nb), out_dtype=BF16, acc_shape=(TM, tn), name=name, dep=dep)


def _mm_out(yc, w, name, dep=None):
    return _matmul(
        yc, w, dn=NN, grid=(S // TM, D // 512, 1),
        a_spec=pl.BlockSpec((TM, 2048), lambda i, j, k: (i, 0)),
        b_spec=pl.BlockSpec((2048, 512), lambda i, j, k: (0, j)),
        o_spec=pl.BlockSpec((TM, 512), lambda i, j, k: (i, j)),
        out_shape=(S, D), out_dtype=F32, acc_shape=(TM, 512), name=name, dep=dep)


def _mm_out_dx(dy, w, name, dep=None):
    return _matmul(
        dy, w, dn=NT, grid=(S // TM, 2048 // 512, 1),
        a_spec=pl.BlockSpec((TM, D), lambda i, j, k: (i, 0)),
        b_spec=pl.BlockSpec((512, D), lambda i, j, k: (j, 0)),
        o_spec=pl.BlockSpec((TM, 512), lambda i, j, k: (i, j)),
        out_shape=(S, 2048), out_dtype=F32, acc_shape=(TM, 512), name=name, dep=dep)


def _mm_out_dw(yc, dy, name):
    return _matmul(
        yc, dy, dn=TN, grid=(2048 // TM, D // 512, 1),
        a_spec=pl.BlockSpec((S, TM), lambda i, j, k: (0, i)),
        b_spec=pl.BlockSpec((S, 512), lambda i, j, k: (0, j)),
        o_spec=pl.BlockSpec((TM, 512), lambda i, j, k: (i, j)),
        out_shape=(2048, D), out_dtype=BF16, acc_shape=(TM, 512), name=name)


def _row_spec(w=D):
    return pl.BlockSpec((TR, w), lambda i: (i, 0))


def _vec_spec(w=D):
    return pl.BlockSpec((1, w), lambda i: (0, 0))


def _rms_stats(x):
    r = lax.rsqrt(jnp.mean(x * x, axis=-1, keepdims=True) + EPS)
    return x * r, r


def _rms_bwd(dn, xhat, r, g):
    dxh = dn * g
    return r * (dxh - xhat * jnp.mean(dxh * xhat, axis=-1, keepdims=True))


def _acc_rows(ref, val, i):
    s = jnp.sum(val, axis=0, keepdims=True)

    @pl.when(i == 0)
    def _():
        ref[...] = s

    @pl.when(i > 0)
    def _():
        ref[...] += s


def _pre0_fwd(x, g, dep=None):
    deps = [] if dep is None else [dep]

    def body(x_ref, g_ref, *rest):
        xhat, _ = _rms_stats(x_ref[...])
        rest[-1][...] = (xhat * g_ref[...]).astype(BF16)

    return pl.pallas_call(
        body, grid=(S // TR,), in_specs=[_row_spec(), _vec_spec()] + [ANY_SPEC] * len(deps), out_specs=_row_spec(),
        out_shape=jax.ShapeDtypeStruct((S, D), BF16), compiler_params=_params(("parallel",)), name="pre0_fwd",
    )(x, g, *deps)


def _post0_fwd(x, y0, g_post, g_pre1):
    def body(x_ref, y_ref, gp_ref, g1_ref, x1_ref, h1_ref):
        yhat, _ = _rms_stats(y_ref[...])
        x1 = x_ref[...] + yhat * gp_ref[...]
        x1_ref[...] = x1
        xhat, _ = _rms_stats(x1)
        h1_ref[...] = (xhat * g1_ref[...]).astype(BF16)

    return pl.pallas_call(
        body, grid=(S // TR,), in_specs=[_row_spec(), _row_spec(), _vec_spec(), _vec_spec()],
        out_specs=[_row_spec(), _row_spec()],
        out_shape=[jax.ShapeDtypeStruct((S, D), F32), jax.ShapeDtypeStruct((S, D), BF16)],
        compiler_params=_params(("parallel",)), name="post0_fwd",
    )(x, y0, g_post, g_pre1)


def _post1_bwd(y1, x1, target, g_post):
    def body(y_ref, x1_ref, t_ref, g_ref, loss_ref, dx2_ref, dy_ref, dg_ref):
        i = pl.program_id(0)
        yhat, r = _rms_stats(y_ref[...])
        g = g_ref[...]
        err = x1_ref[...] + yhat * g - t_ref[...]
        part = jnp.sum(jnp.sum(err * err, axis=-1, keepdims=True), axis=0, keepdims=True) * (0.5 / D)
        _acc_rows(loss_ref, jnp.broadcast_to(part, (1, 128)), i)
        dx2 = err * (1.0 / D)
        dx2_ref[...] = dx2
        _acc_rows(dg_ref, dx2 * yhat, i)
        dy_ref[...] = _rms_bwd(dx2, yhat, r, g).astype(BF16)

    return pl.pallas_call(
        body, grid=(S // TR,), in_specs=[_row_spec(), _row_spec(), _row_spec(), _vec_spec()],
        out_specs=[_vec_spec(128), _row_spec(), _row_spec(), _vec_spec()],
        out_shape=[jax.ShapeDtypeStruct((1, 128), F32), jax.ShapeDtypeStruct((S, D), F32),
                   jax.ShapeDtypeStruct((S, D), BF16), jax.ShapeDtypeStruct((1, D), F32)],
        compiler_params=_params(("arbitrary",)), name="post1_bwd",
    )(y1, x1, target, g_post)


def _mid_bwd(dx2, dh1, x1, y0, g_pre1, g_post0):
    def body(dx2_ref, dh_ref, x1_ref, y_ref, g1_ref, gp_ref, dx1_ref, dy_ref, dg1_ref, dgp_ref):
        i = pl.program_id(0)
        xhat, r1 = _rms_stats(x1_ref[...])
        dh = dh_ref[...]
        _acc_rows(dg1_ref, dh * xhat, i)
        dx1 = dx2_ref[...] + _rms_bwd(dh, xhat, r1, g1_ref[...])
        dx1_ref[...] = dx1
        yhat, r0 = _rms_stats(y_ref[...])
        _acc_rows(dgp_ref, dx1 * yhat, i)
        dy_ref[...] = _rms_bwd(dx1, yhat, r0, gp_ref[...]).astype(BF16)

    return pl.pallas_call(
        body, grid=(S // TR,),
        in_specs=[_row_spec(), _row_spec(), _row_spec(), _row_spec(), _vec_spec(), _vec_spec()],
        out_specs=[_row_spec(), _row_spec(), _vec_spec(), _vec_spec()],
        out_shape=[jax.ShapeDtypeStruct((S, D), F32), jax.ShapeDtypeStruct((S, D), BF16),
                   jax.ShapeDtypeStruct((1, D), F32), jax.ShapeDtypeStruct((1, D), F32)],
        compiler_params=_params(("arbitrary",)), name="mid_bwd",
    )(dx2, dh1, x1, y0, g_pre1, g_post0)


def _pre0_bwd(dx1, dh0, x, g):
    def body(dx1_ref, dh_ref, x_ref, g_ref, gx_ref, dg_ref):
        i = pl.program_id(0)
        xhat, r = _rms_stats(x_ref[...])
        dh = dh_ref[...]
        _acc_rows(dg_ref, dh * xhat, i)
        gx_ref[...] = dx1_ref[...] + _rms_bwd(dh, xhat, r, g_ref[...])

    return pl.pallas_call(
        body, grid=(S // TR,), in_specs=[_row_spec(), _row_spec(), _row_spec(), _vec_spec()],
        out_specs=[_row_spec(), _vec_spec()],
        out_shape=[jax.ShapeDtypeStruct((S, D), F32), jax.ShapeDtypeStruct((1, D), F32)],
        compiler_params=_params(("arbitrary",)), name="pre0_bwd",
    )(dx1, dh0, x, g)


POOL_CH = 256


def _pool_apply(a, w, transpose):
    n = a.shape[0]
    row = lax.broadcasted_iota(jnp.int32, a.shape, 0)
    cnt = jnp.minimum(row + 1, w).astype(F32)
    s = a / cnt if transpose else a
    for k in (1, 2, 4, 8):
        if transpose:
            sh = jnp.where(row < n - k, pltpu.roll(s, n - k, 0), 0.0)
        else:
            sh = jnp.where(row >= k, pltpu.roll(s, k, 0), 0.0)
        s = jnp.where(w > k, s + sh, s)
    return s - a if transpose else s / cnt - a


def _pool_fwd(z0, pool_w, pool_scale):
    def body(a_ref, gate_ref, w_ref, sc_ref, out_ref):
        win = jnp.left_shift(2, pl.program_id(0))
        pooled = _pool_apply(a_ref[...], win, False)
        mixed = _dot(pooled.astype(BF16), w_ref[...])
        gate = gate_ref[...]
        out_ref[...] = (mixed * sc_ref[...] * (gate * _sigmoid(gate))).astype(BF16)

    return pl.pallas_call(
        body, grid=(4,),
        in_specs=[pl.BlockSpec((S, POOL_CH), lambda g: (0, g)), pl.BlockSpec((S, POOL_CH), lambda g: (0, 4 + g)),
                  pl.BlockSpec((None, POOL_CH, POOL_CH), lambda g: (g, 0, 0)),
                  pl.BlockSpec((1, POOL_CH), lambda g: (0, g))],
        out_specs=pl.BlockSpec((S, POOL_CH), lambda g: (0, g)),
        out_shape=jax.ShapeDtypeStruct((S, 2048), BF16),
        compiler_params=_params(("parallel",), VMEM_BIG), name="pool_fwd",
    )(z0, z0, pool_w, pool_scale)


def _pool_bwd(z0, dycat, pool_w, pool_scale):
    def body(a_ref, gate_ref, dy_ref, w_ref, sc_ref, da_ref, dgate_ref, dw_ref, dsc_ref):
        win = jnp.left_shift(2, pl.program_id(0))
        pooled = _pool_apply(a_ref[...], win, False).astype(BF16)
        w = w_ref[...]
        mixed = _dot(pooled, w)
        silu, dsilu = _silu_and_grad(gate_ref[...])
        dy = dy_ref[...]
        sc = sc_ref[...]
        dgate_ref[...] = (dy * (mixed * sc) * dsilu).astype(BF16)
        dms = dy * silu
        dsc_ref[...] = jnp.sum(dms * mixed, axis=0, keepdims=True)
        dmixed = (dms * sc).astype(BF16)
        dw_ref[...] = _dot(pooled, dmixed, TN)
        dpooled = _dot(dmixed, w, NT)
        da_ref[...] = _pool_apply(dpooled, win, True).astype(BF16)

    slab = lambda off: pl.BlockSpec((S, POOL_CH), lambda g: (0, off + g))
    return pl.pallas_call(
        body, grid=(4,),
        in_specs=[slab(0), slab(4), slab(0), pl.BlockSpec((None, POOL_CH, POOL_CH), lambda g: (g, 0, 0)),
                  pl.BlockSpec((1, POOL_CH), lambda g: (0, g))],
        out_specs=[slab(0), slab(0), pl.BlockSpec((None, POOL_CH, POOL_CH), lambda g: (g, 0, 0)),
                   pl.BlockSpec((1, POOL_CH), lambda g: (0, g))],
        out_shape=[jax.ShapeDtypeStruct((S, HALF), BF16), jax.ShapeDtypeStruct((S, HALF), BF16),
                   jax.ShapeDtypeStruct((4, POOL_CH, POOL_CH), F32), jax.ShapeDtypeStruct((1, HALF), F32)],
        compiler_params=_params(("parallel",), VMEM_BIG), name="pool_bwd",
    )(z0, z0, dycat, pool_w, pool_scale)


Q_COL, K_COL, V_COL, BG_COL = 2048 // 128, 5120 // 128, 8192 // 128, 11264 // 128
SCALE = HEAD_DIM ** -0.5


def _rope_tables():
    pos = jnp.arange(S, dtype=F32)
    inv_freq = jnp.power(ROPE_THETA, -jnp.arange(0, ROT_DIM, 2, dtype=F32) / ROT_DIM)
    ang = pos[:, None] * inv_freq[None, :]
    cos, sin = jnp.cos(ang), jnp.sin(ang)
    half = ROT_DIM // 2
    zeros = jnp.zeros((S, HEAD_DIM - ROT_DIM), F32)
    c = jnp.concatenate([cos, cos, jnp.ones((S, HEAD_DIM - ROT_DIM), F32)], axis=1)
    a = jnp.concatenate([-sin, jnp.zeros((S, half), F32), zeros], axis=1)
    b = jnp.concatenate([jnp.zeros((S, half), F32), sin, zeros], axis=1)
    return c, a, b


def _rope(t, c, a, b):
    half = ROT_DIM // 2
    return t * c + pltpu.roll(t, HEAD_DIM - half, 1) * a + pltpu.roll(t, half, 1) * b


def _rope_t(d, c, a, b):
    half = ROT_DIM // 2
    return d * c + pltpu.roll(d * a, half, 1) + pltpu.roll(d * b, HEAD_DIM - half, 1)


def _deinterleave(dst, src, dil, cast=None, dst_off=0):
    length = S // dil
    for r in range(dil):
        v = src[...] if dil == 1 else src[pl.ds(r, length, stride=dil), :]
        dst[dst_off + r * length:dst_off + (r + 1) * length, :] = v if cast is None else v.astype(cast)


def _interleave(dst, src, dil, src_off=0):
    length = S // dil
    for r in range(dil):
        if dil == 1:
            dst[...] = src[src_off:src_off + S, :]
        else:
            dst[pl.ds(r, length, stride=dil), :] = src[src_off + r * length:src_off + (r + 1) * length, :]


CU = 4
NUNITS = S // BLK
B_QK = (((2,), (2,)), ((0,), (0,)))
B_PV = (((2,), (1,)), ((0,), (0,)))
B_TN = (((1,), (1,)), ((0,), (0,)))


def _blocks(ref, first):
    return ref[first * BLK:(first + CU) * BLK, :].reshape(CU, BLK, HEAD_DIM)


def _chunk_scores(u0, nb, qd, kdp):
    q = _blocks(qd, u0)
    row = lax.broadcasted_iota(jnp.int32, (CU, BLK, BLK), 1)
    col = lax.broadcasted_iota(jnp.int32, (CU, BLK, BLK), 2)
    s_own = jnp.where(col <= row, _dot(q, _blocks(kdp, u0 + 1), B_QK) * SCALE, NEG)
    if nb == 1:
        return q, s_own, None
    unit = lax.broadcasted_iota(jnp.int32, (CU, BLK, BLK), 0) + u0
    s_prev = jnp.where((col >= row) & ((unit % nb) != 0), _dot(q, _blocks(kdp, u0), B_QK) * SCALE, NEG)
    return q, s_own, s_prev


def _qkv_prep(z0, tabs):
    def body(q_ref, k_ref, v_ref, c_ref, a_ref, b_ref, qo_ref, ko_ref, vo_ref, tmp):
        p = pl.program_id(1)
        for gi, (_, dil) in enumerate(PATTERNS):
            @pl.when(p == gi)
            def _(dil=dil):
                c, a, b = c_ref[...], a_ref[...], b_ref[...]
                tmp[...] = _rope(q_ref[...], c, a, b)
                _deinterleave(qo_ref, tmp, dil, BF16)
                tmp[...] = _rope(k_ref[...], c, a, b)
                _deinterleave(ko_ref, tmp, dil, BF16)
                _deinterleave(vo_ref, v_ref, dil, BF16)

    tab = pl.BlockSpec((S, HEAD_DIM), lambda h, p: (0, 0))
    out = pl.BlockSpec((S, HEAD_DIM), lambda h, p: (0, p * 8 + h))
    return pl.pallas_call(
        body, grid=(8, 3), in_specs=[_head_spec(Q_COL), _head_spec(K_COL), _head_spec(V_COL), tab, tab, tab],
        out_specs=[out, out, out], out_shape=[jax.ShapeDtypeStruct((S, 3072), BF16)] * 3,
        scratch_shapes=[pltpu.VMEM((S, HEAD_DIM), F32)],
        compiler_params=_params(("parallel", "arbitrary"), VMEM_BIG), name="qkv_prep",
    )(z0, z0, z0, *tabs)


def _pad_copy(dst, src):
    dst[0:BLK, :] = jnp.zeros((BLK, HEAD_DIM), dst.dtype)
    dst[BLK:BLK + S, :] = src[...]


def _attn_group_fwd(dil, qd, kd_ref, vd_ref, kdp, vdp, od, ld, og, lg):
    nb = S // dil // BLK
    _pad_copy(kdp, kd_ref)
    _pad_copy(vdp, vd_ref)
    for u0 in range(0, NUNITS, CU):
        _, s_own, s_prev = _chunk_scores(u0, nb, qd, kdp)
        m = jnp.max(s_own, axis=2, keepdims=True)
        if s_prev is not None:
            m = jnp.maximum(m, jnp.max(s_prev, axis=2, keepdims=True))
        p_own = jnp.exp(s_own - m)
        den = jnp.sum(p_own, axis=2, keepdims=True)
        acc = _dot(p_own.astype(BF16), _blocks(vdp, u0 + 1), B_PV)
        if s_prev is not None:
            p_prev = jnp.exp(s_prev - m)
            den = den + jnp.sum(p_prev, axis=2, keepdims=True)
            acc = acc + _dot(p_prev.astype(BF16), _blocks(vdp, u0), B_PV)
        rows = slice(u0 * BLK, (u0 + CU) * BLK)
        od[rows, :] = (acc / den).reshape(CU * BLK, HEAD_DIM)
        ld[rows, :] = jnp.broadcast_to(m + jnp.log(den), (CU, BLK, HEAD_DIM)).reshape(CU * BLK, HEAD_DIM)
    _interleave(og, od, dil)
    _interleave(lg, ld, dil)


def _group_weights(lgs):
    l0, l1, l2 = lgs[0][...], lgs[1][...], lgs[2][...]
    mx = jnp.maximum(l0, jnp.maximum(l1, l2))
    e0, e1, e2 = jnp.exp(l0 - mx), jnp.exp(l1 - mx), jnp.exp(l2 - mx)
    den = e0 + e1 + e2
    return e0 / den, e1 / den, e2 / den


def _head_spec(base, ngroups_axis=True):
    return pl.BlockSpec((S, HEAD_DIM), lambda h, p: (0, base + (p % 3) * 8 + h))


def _slab(dtype=F32, rows=S):
    return pltpu.VMEM((rows, HEAD_DIM), dtype)


def _attn_fwd(z0, qkv, ycat):
    def body(q_ref, k_ref, v_ref, gate_ref, ycat_ref, out_ref, og_ref, lg_ref,
             kdp, vdp, od, ld, og0, og1, og2, lg0, lg1, lg2):
        del ycat_ref
        p = pl.program_id(1)
        ogs, lgs = (og0, og1, og2), (lg0, lg1, lg2)
        for gi, (_, dil) in enumerate(PATTERNS):
            @pl.when(p == gi)
            def _(gi=gi, dil=dil):
                _attn_group_fwd(dil, q_ref, k_ref, v_ref, kdp, vdp, od, ld, ogs[gi], lgs[gi])
                og_ref[...] = ogs[gi][...]
                lg_ref[...] = lgs[gi][...]

        @pl.when(p == 2)
        def _():
            w0, w1, w2 = _group_weights(lgs)
            o = w0 * og0[...] + w1 * og1[...] + w2 * og2[...]
            gate = gate_ref[...]
            out_ref[...] = (o * (gate * _sigmoid(gate))).astype(BF16)

    grp = pl.BlockSpec((S, HEAD_DIM), lambda h, p: (0, p * 8 + h))
    return pl.pallas_call(
        body, grid=(8, 3),
        in_specs=[grp, grp, grp, pl.BlockSpec((S, HEAD_DIM), lambda h, p: (0, BG_COL + h)), ANY_SPEC],
        out_specs=[pl.BlockSpec((S, HEAD_DIM), lambda h, p: (0, 8 + h)), grp, grp],
        out_shape=[jax.ShapeDtypeStruct((S, 2048), BF16), jax.ShapeDtypeStruct((S, 3072), F32),
                   jax.ShapeDtypeStruct((S, 3072), F32)],
        scratch_shapes=[_slab(BF16, S + BLK), _slab(BF16, S + BLK)] + [_slab() for _ in range(8)],
        input_output_aliases={4: 0},
        compiler_params=_params(("parallel", "arbitrary"), VMEM_BIG), name="attn_fwd",
    )(*qkv, z0, ycat)


def _attn_bwd(z0, qkv, og, lg, dycat, tabs):
    def body(q_ref, k_ref, v_ref, gate_ref, dy_ref, c_ref, a_ref, b_ref,
             og0_ref, og1_ref, og2_ref, lg0_ref, lg1_ref, lg2_ref,
             dq_ref, dk_ref, dv_ref, dbg_ref,
             tmp, kd, vd, ld, dg0, dg1, dg2, cg0, cg1, cg2, dod, cd, dqd, dkd, dvd):
        p = pl.program_id(1)
        ogs, lgs, dgs, cgs = (og0_ref, og1_ref, og2_ref), (lg0_ref, lg1_ref, lg2_ref), (dg0, dg1, dg2), (cg0, cg1, cg2)

        @pl.when(p == 0)
        def _():
            w = _group_weights(lgs)
            o = w[0] * ogs[0][...] + w[1] * ogs[1][...] + w[2] * ogs[2][...]
            silu, dsilu = _silu_and_grad(gate_ref[...])
            dy = dy_ref[...]
            dbg_ref[...] = (dy * o * dsilu).astype(BF16)
            do = dy * silu
            dwbar = jnp.sum(do * o, axis=1, keepdims=True)
            for gi in range(3):
                dgs[gi][...] = w[gi] * do
                cgs[gi][...] = -w[gi] * dwbar

        for gi, (_, dil) in enumerate(PATTERNS):
            @pl.when(p == 1 + gi)
            def _(gi=gi, dil=dil):
                nb = S // dil // BLK
                qd = q_ref
                c, a, b = c_ref[...], a_ref[...], b_ref[...]
                _pad_copy(kd, k_ref)
                _pad_copy(vd, v_ref)
                _deinterleave(dod, dgs[gi], dil, BF16)
                _deinterleave(ld, lgs[gi], dil)
                _deinterleave(cd, cgs[gi], dil)
                dkd[...] = jnp.zeros_like(dkd)
                dvd[...] = jnp.zeros_like(dvd)
                flat = lambda t: t.reshape(CU * BLK, HEAD_DIM)
                for u0 in range(0, NUNITS, CU):
                    q, s_own, s_prev = _chunk_scores(u0, nb, qd, kd)
                    lse, cv, do = _blocks(ld, u0), _blocks(cd, u0), _blocks(dod, u0)
                    own = slice((u0 + 1) * BLK, (u0 + 1 + CU) * BLK)
                    p_own = jnp.exp(s_own - lse)
                    ds_own = (p_own * (_dot(do, _blocks(vd, u0 + 1), B_QK) + cv) * SCALE).astype(BF16)
                    dq = _dot(ds_own, _blocks(kd, u0 + 1), B_PV)
                    dkd[own, :] += flat(_dot(ds_own, q, B_TN))
                    dvd[own, :] += flat(_dot(p_own.astype(BF16), do, B_TN))
                    if s_prev is not None:
                        prev = slice(u0 * BLK, (u0 + CU) * BLK)
                        p_prev = jnp.exp(s_prev - lse)
                        ds_prev = (p_prev * (_dot(do, _blocks(vd, u0), B_QK) + cv) * SCALE).astype(BF16)
                        dq = dq + _dot(ds_prev, _blocks(kd, u0), B_PV)
                        dkd[prev, :] += flat(_dot(ds_prev, q, B_TN))
                        dvd[prev, :] += flat(_dot(p_prev.astype(BF16), do, B_TN))
                    dqd[u0 * BLK:(u0 + CU) * BLK, :] = flat(dq)
                _interleave(tmp, dqd, dil)
                dq_ref[...] = _rope_t(tmp[...], c, a, b).astype(BF16)
                _interleave(tmp, dkd, dil, BLK)
                dk_ref[...] = _rope_t(tmp[...], c, a, b).astype(BF16)
                _interleave(tmp, dvd, dil, BLK)
                dv_ref[...] = tmp[...].astype(BF16)

    tab = pl.BlockSpec((S, HEAD_DIM), lambda h, p: (0, 0))
    hspec = lambda base: pl.BlockSpec((S, HEAD_DIM), lambda h, p: (0, base + h))
    gspec = pl.BlockSpec((S, HEAD_DIM), lambda h, p: (0, jnp.maximum(p - 1, 0) * 8 + h))
    return pl.pallas_call(
        body, grid=(8, 4),
        in_specs=[gspec, gspec, gspec, hspec(BG_COL), hspec(8), tab, tab, tab,
                  hspec(0), hspec(8), hspec(16), hspec(0), hspec(8), hspec(16)],
        out_specs=[gspec, gspec, gspec, hspec(0)],
        out_shape=[jax.ShapeDtypeStruct((S, 3072), BF16)] * 3 + [jax.ShapeDtypeStruct((S, HALF), BF16)],
        scratch_shapes=[_slab(), _slab(BF16, S + BLK), _slab(BF16, S + BLK), _slab()] + [_slab() for _ in range(6)]
                       + [_slab(BF16), _slab(), _slab(), _slab(F32, S + BLK), _slab(F32, S + BLK)],
        compiler_params=_params(("parallel", "arbitrary"), VMEM_BIG), name="attn_bwd",
    )(*qkv, z0, dycat, *tabs, og, og, og, lg, lg, lg)


SGU_CH = 256
NCHUNK = TR // 128


def _ln_stats(x):
    mu = jnp.mean(x, axis=-1, keepdims=True)
    xc = x - mu
    r = lax.rsqrt(jnp.mean(xc * xc, axis=-1, keepdims=True) + EPS)
    return xc * r, r


def _ln_bwd(dy, xhat, r, g):
    dxh = dy * g
    return r * (dxh - jnp.mean(dxh, axis=-1, keepdims=True) - xhat * jnp.mean(dxh * xhat, axis=-1, keepdims=True))


def _tril_bf16(w):
    row = lax.broadcasted_iota(jnp.int32, w.shape, 0)
    col = lax.broadcasted_iota(jnp.int32, w.shape, 1)
    return jnp.where(row >= col, w, 0.0).astype(BF16)


def _sgu_gate(vn_s, s_s, w_ref, bb_ref):
    for h in range(4):
        wm = _tril_bf16(w_ref[h])
        bias = bb_ref[h]
        for ch in range(NCHUNK):
            rows, cols = slice(ch * 128, (ch + 1) * 128), slice(h * SGU_CH, (h + 1) * SGU_CH)
            s_s[rows, cols] = _dot(wm, vn_s[rows, cols]) + jnp.concatenate([bias, bias], axis=1)


WIN = HALO + TR
SUBL = 8


def _shifted_copies(dst, src):
    dst[0] = src[...]
    for b in range(1, SUBL):
        dst[b, 0:WIN - SUBL, :] = src[pl.ds(b, WIN - SUBL), :]


def _rows_at(copies, off, n):
    return copies[off % SUBL, pl.ds(off - off % SUBL, n), :]


def _conv_fwd(i, dval_ref, dglu_ref, hval_ref, hglu_ref, cw_ref, cb_ref, xw, xr, dcs):
    halo = hval_ref[...] * _sigmoid(hglu_ref[...])
    xw[0:HALO, :] = jnp.where(i > 0, halo, 0.0)
    xw[HALO:HALO + TR, :] = dval_ref[...] * _sigmoid(dglu_ref[...])
    _shifted_copies(xr, xw)
    for rb in range(TR // SUB):
        acc = jnp.broadcast_to(cb_ref[...], (SUB, HALF))
        for k in range(CONV_K):
            acc = acc + cw_ref[k:k + 1, :] * _rows_at(xr, rb * SUB + HALO - (CONV_K - 1) + k, SUB)
        dcs[rb * SUB:(rb + 1) * SUB, :] = acc


def _odd_in_specs():
    col = lambda j: pl.BlockSpec((TR, HALF), lambda i, *_: (i, j))
    prev = lambda j: pl.BlockSpec((HALO, HALF), lambda i, *_: (jnp.maximum(i * (TR // HALO) - 1, 0), j))
    return [col(0), col(1), col(2), col(3), col(4), col(5), prev(3), prev(4)]


def _full_spec(shape):
    return pl.BlockSpec(shape, lambda i, *_: (0,) * len(shape))


def _odd_fwd(z1, sgu_g, sgu_b, sgu_w, sgu_bb, conv_w, conv_b, cn_g, cn_b):
    def body(u_ref, v_ref, cg_ref, dval_ref, dglu_ref, dgate_ref, hval_ref, hglu_ref,
             g_ref, b_ref, w_ref, bb_ref, cw_ref, cb_ref, cng_ref, cnb_ref, out_ref, vn_s, s_s, xw, dcs, xr):
        i = pl.program_id(0)
        vhat, _ = _ln_stats(v_ref[...])
        vn_s[...] = (vhat * g_ref[...] + b_ref[...]).astype(BF16)
        _sgu_gate(vn_s, s_s, w_ref, bb_ref)
        cg = cg_ref[...]
        out_ref[:, 0:HALF] = (u_ref[...] * s_s[...] * (cg * _sigmoid(cg))).astype(BF16)
        _conv_fwd(i, dval_ref, dglu_ref, hval_ref, hglu_ref, cw_ref, cb_ref, xw, xr, dcs)
        dhat, _ = _ln_stats(dcs[...])
        dn = dhat * cng_ref[...] + cnb_ref[...]
        dgate = dgate_ref[...]
        out_ref[:, HALF:2 * HALF] = ((dn * _sigmoid(dn)) * (dgate * _sigmoid(dgate))).astype(BF16)

    vec = _full_spec((1, HALF))
    return pl.pallas_call(
        body, grid=(S // TR,),
        in_specs=_odd_in_specs() + [vec, vec, _full_spec((4, 128, 128)), _full_spec((4, 128, 128)),
                                    _full_spec((HALO, HALF)), vec, vec, vec],
        out_specs=pl.BlockSpec((TR, 2048), lambda i: (i, 0)),
        out_shape=jax.ShapeDtypeStruct((S, 2048), BF16),
        scratch_shapes=[pltpu.VMEM((TR, HALF), BF16), pltpu.VMEM((TR, HALF), F32),
                        pltpu.VMEM((WIN, HALF), F32), pltpu.VMEM((TR, HALF), F32), pltpu.VMEM((SUBL, WIN, HALF), F32)],
        compiler_params=_params(("parallel",), VMEM_BIG), name="odd_fwd",
    )(z1, z1, z1, z1, z1, z1, z1, z1, sgu_g, sgu_b, sgu_w, sgu_bb, conv_w, conv_b, cn_g, cn_b)


def _odd_bwd_a(z1, dycat, sgu_g, sgu_b, sgu_w, sgu_bb, conv_w, conv_b, cn_g, cn_b):
    def body(u_ref, v_ref, cg_ref, dval_ref, dglu_ref, dgate_ref, hval_ref, hglu_ref, dy_ref,
             g_ref, b_ref, w_ref, bb_ref, cw_ref, cb_ref, cng_ref, cnb_ref,
             dz_ref, ddc_ref, dw_ref, dbb_ref, dg_ref, db_ref, dcng_ref, dcnb_ref, dcb_ref,
             vn_s, s_s, xw, dcs, ds_s, dvn_s, xr):
        i = pl.program_id(0)
        vhat, rv = _ln_stats(v_ref[...])
        g = g_ref[...]
        vn_s[...] = (vhat * g + b_ref[...]).astype(BF16)
        _sgu_gate(vn_s, s_s, w_ref, bb_ref)
        silu_c, dsilu_c = _silu_and_grad(cg_ref[...])
        dyc = dy_ref[:, 0:HALF]
        u = u_ref[...]
        s = s_s[...]
        dz_ref[:, 0:HALF] = (dyc * s * silu_c).astype(BF16)
        dz_ref[:, 2 * HALF:3 * HALF] = (dyc * u * s * dsilu_c).astype(BF16)
        ds_s[...] = dyc * u * silu_c

        @pl.when(i == 0)
        def _():
            dw_ref[...] = jnp.zeros_like(dw_ref)
            dbb_ref[...] = jnp.zeros_like(dbb_ref)

        tril = lax.broadcasted_iota(jnp.int32, (128, 128), 0) >= lax.broadcasted_iota(jnp.int32, (128, 128), 1)
        for h in range(4):
            wm = _tril_bf16(w_ref[h])
            for ch in range(NCHUNK):
                rows, cols = slice(ch * 128, (ch + 1) * 128), slice(h * SGU_CH, (h + 1) * SGU_CH)
                ds = ds_s[rows, cols]
                dsb = ds.astype(BF16)
                dw_ref[h] += jnp.where(tril, _dot(dsb, vn_s[rows, cols], NT), 0.0)
                dbb_ref[h] += jnp.broadcast_to(jnp.sum(ds, axis=1, keepdims=True), (128, 128))
                dvn_s[rows, cols] = _dot(wm, dsb, TN)
        dvn = dvn_s[...]
        _acc_rows(dg_ref, dvn * vhat, i)
        _acc_rows(db_ref, dvn, i)
        dz_ref[:, HALF:2 * HALF] = _ln_bwd(dvn, vhat, rv, g).astype(BF16)

        _conv_fwd(i, dval_ref, dglu_ref, hval_ref, hglu_ref, cw_ref, cb_ref, xw, xr, dcs)
        dhat, rd = _ln_stats(dcs[...])
        cng = cng_ref[...]
        silu_n, dsilu_n = _silu_and_grad(dhat * cng + cnb_ref[...])
        silu_g, dsilu_g = _silu_and_grad(dgate_ref[...])
        dyd = dy_ref[:, HALF:2 * HALF]
        dz_ref[:, 5 * HALF:6 * HALF] = (dyd * silu_n * dsilu_g).astype(BF16)
        ddn = dyd * silu_g * dsilu_n
        _acc_rows(dcng_ref, ddn * dhat, i)
        _acc_rows(dcnb_ref, ddn, i)
        ddc = _ln_bwd(ddn, dhat, rd, cng)
        ddc_ref[...] = ddc
        _acc_rows(dcb_ref, ddc, i)

    vec = _full_spec((1, HALF))
    sq = _full_spec((4, 128, 128))
    return pl.pallas_call(
        body, grid=(S // TR,),
        in_specs=_odd_in_specs() + [pl.BlockSpec((TR, 2048), lambda i: (i, 0)),
                                    vec, vec, sq, sq, _full_spec((HALO, HALF)), vec, vec, vec],
        out_specs=[pl.BlockSpec((TR, ODD_IN), lambda i: (i, 0)), pl.BlockSpec((TR, HALF), lambda i: (i, 0)),
                   sq, sq, vec, vec, vec, vec, vec],
        out_shape=[jax.ShapeDtypeStruct((S, ODD_IN), BF16), jax.ShapeDtypeStruct((S, HALF), F32),
                   jax.ShapeDtypeStruct((4, 128, 128), F32), jax.ShapeDtypeStruct((4, 128, 128), F32)]
                  + [jax.ShapeDtypeStruct((1, HALF), F32)] * 5,
        scratch_shapes=[pltpu.VMEM((TR, HALF), BF16), pltpu.VMEM((TR, HALF), F32),
                        pltpu.VMEM((WIN, HALF), F32), pltpu.VMEM((TR, HALF), F32),
                        pltpu.VMEM((TR, HALF), F32), pltpu.VMEM((TR, HALF), F32), pltpu.VMEM((SUBL, WIN, HALF), F32)],
        compiler_params=_params(("arbitrary",), VMEM_BIG), name="odd_bwd_a",
    )(z1, z1, z1, z1, z1, z1, z1, z1, dycat, sgu_g, sgu_b, sgu_w, sgu_bb, conv_w, conv_b, cn_g, cn_b)


def _odd_bwd_b(z1, ddc, dz1, conv_w):
    nt = S // TR

    def body(dval_ref, dglu_ref, hval_ref, hglu_ref, ddc_ref, hddc_ref, cw_ref, dz_in_ref,
             dz_ref, dcw_ref, xw, dwin, dxs, xr, dr):
        del dz_in_ref
        i, j = pl.program_id(0), pl.program_id(1)
        sg = _sigmoid(dglu_ref[...])
        dval = dval_ref[...]

        @pl.when(j == 0)
        def _():
            halo = hval_ref[...] * _sigmoid(hglu_ref[...])
            xw[0:HALO, :] = jnp.where(i > 0, halo, 0.0)
            xw[HALO:HALO + TR, :] = dval * sg
            dwin[0:TR, :] = ddc_ref[...]
            dwin[TR:TR + HALO, :] = jnp.where(i < nt - 1, hddc_ref[...], 0.0)
            _shifted_copies(xr, xw)
            _shifted_copies(dr, dwin)

            @pl.when(i == 0)
            def _():
                dcw_ref[...] = jnp.zeros_like(dcw_ref)

            for rb in range(TR // SUB):
                acc = jnp.zeros((SUB, HALF), F32)
                for k in range(CONV_K):
                    acc = acc + cw_ref[k:k + 1, :] * _rows_at(dr, rb * SUB + (CONV_K - 1) - k, SUB)
                dxs[rb * SUB:(rb + 1) * SUB, :] = acc
            for k in range(CONV_K):
                acc = jnp.zeros((SUB, HALF), F32)
                for rb in range(TR // SUB):
                    acc = acc + dwin[rb * SUB:(rb + 1) * SUB, :] * _rows_at(xr, rb * SUB + HALO - (CONV_K - 1) + k, SUB)
                dcw_ref[k:k + 1, :] += jnp.sum(acc, axis=0, keepdims=True)
            dz_ref[...] = (dxs[...] * sg).astype(BF16)

        @pl.when(j == 1)
        def _():
            dz_ref[...] = (dxs[...] * dval * sg * (1.0 - sg)).astype(BF16)

    col = lambda c: pl.BlockSpec((TR, HALF), lambda i, j: (i, c))
    prev = lambda c: pl.BlockSpec((HALO, HALF), lambda i, j: (jnp.maximum(i * (TR // HALO) - 1, 0), c))
    nxt = pl.BlockSpec((HALO, HALF), lambda i, j: (jnp.minimum((i + 1) * (TR // HALO), S // HALO - 1), 0))
    return pl.pallas_call(
        body, grid=(nt, 2),
        in_specs=[col(3), col(4), prev(3), prev(4), pl.BlockSpec((TR, HALF), lambda i, j: (i, 0)), nxt,
                  _full_spec((HALO, HALF)), pl.BlockSpec(memory_space=pl.ANY)],
        out_specs=[pl.BlockSpec((TR, HALF), lambda i, j: (i, 3 + j)), _full_spec((HALO, HALF))],
        out_shape=[jax.ShapeDtypeStruct((S, ODD_IN), BF16), jax.ShapeDtypeStruct((HALO, HALF), F32)],
        scratch_shapes=[pltpu.VMEM((WIN, HALF), F32), pltpu.VMEM((WIN, HALF), F32), pltpu.VMEM((TR, HALF), F32),
                        pltpu.VMEM((SUBL, WIN, HALF), F32), pltpu.VMEM((SUBL, WIN, HALF), F32)],
        input_output_aliases={7: 0},
        compiler_params=_params(("arbitrary", "arbitrary"), VMEM_BIG), name="odd_bwd_b",
    )(z1, z1, z1, z1, ddc, ddc, conv_w, dz1)


def _cast_bf16(w, name):
    r, c = w.shape
    tr = min(r, 256)
    def body(i_ref, o_ref):
        o_ref[...] = i_ref[...].astype(BF16)

    return pl.pallas_call(
        body, grid=(r // tr,), in_specs=[pl.BlockSpec((tr, c), lambda i: (i, 0))],
        out_specs=pl.BlockSpec((tr, c), lambda i: (i, 0)), out_shape=jax.ShapeDtypeStruct((r, c), BF16),
        compiler_params=_params(("parallel",)), name=name,
    )(w)


def _adamw(w, g, m, v):
    m = ADAM_B1 * m + (1.0 - ADAM_B1) * g
    v = ADAM_B2 * v + (1.0 - ADAM_B2) * (g * g)
    m_hat = m / (1.0 - ADAM_B1 ** ADAM_STEP)
    v_hat = v / (1.0 - ADAM_B2 ** ADAM_STEP)
    delta = -ADAM_LR * (m_hat / (jnp.sqrt(v_hat) + ADAM_EPS) + ADAM_WD * w)
    return delta, m, v


def _adam_reduce(parts, w, m, v, name, dep=None):
    r, c = w.shape
    tr = min(r, 128)
    deps = [] if dep is None else [dep]
    pieces = list(parts) if isinstance(parts, (list, tuple)) else [parts]
    npieces, nparts = len(pieces), pieces[0].shape[0]

    def body(*refs):
        p_refs = refs[:npieces]
        w_ref, m_ref, v_ref = refs[npieces:npieces + 3]
        g_ref, d_ref, nm_ref, nv_ref = refs[npieces + 3 + len(deps):]
        cols = []
        for p_ref in p_refs:
            acc = p_ref[0].astype(F32)
            for d in range(1, nparts):
                acc = acc + p_ref[d].astype(F32)
            cols.append(acc)
        g = cols[0] if npieces == 1 else jnp.concatenate(cols, axis=1)
        g_ref[...] = g
        d_ref[...], nm_ref[...], nv_ref[...] = _adamw(w_ref[...], g, m_ref[...], v_ref[...])

    spec = pl.BlockSpec((tr, c), lambda i: (i, 0))
    return pl.pallas_call(
        body, grid=(r // tr,),
        in_specs=[pl.BlockSpec((nparts, tr, a.shape[2]), lambda i: (0, i, 0)) for a in pieces] + [spec, spec, spec]
                 + [ANY_SPEC] * len(deps),
        out_specs=[spec] * 4, out_shape=[jax.ShapeDtypeStruct((r, c), F32)] * 4,
        compiler_params=_params(("parallel",), VMEM_BIG), name=name,
    )(*pieces, w, m, v, *deps)


def _sum_parts(parts, name, dep=None):
    r = parts.shape[1]
    tr = 8
    for cand in (512, 256, 128, 64, 32, 16, 8):
        if r % cand == 0:
            tr = cand
            break
    deps = [] if dep is None else [dep]

    def body(p_ref, *rest):
        g = p_ref[0]
        for d in range(1, NDEV):
            g = g + p_ref[d]
        rest[-1][...] = g

    return pl.pallas_call(
        body, grid=(r // tr,), in_specs=[pl.BlockSpec((NDEV, tr, 128), lambda i: (0, i, 0))] + [ANY_SPEC] * len(deps),
        out_specs=pl.BlockSpec((tr, 128), lambda i: (i, 0)), out_shape=jax.ShapeDtypeStruct((r, 128), F32),
        compiler_params=_params(("parallel",)), name=name,
    )(parts, *deps)


def _sum_unpack(parts, rows, name):
    def body(p_ref, *outs):
        off = 0
        for o_ref, n in zip(outs, rows):
            acc = p_ref[0, off:off + n, :]
            for d in range(1, NDEV):
                acc = acc + p_ref[d, off:off + n, :]
            o_ref[...] = acc
            off += n

    return pl.pallas_call(
        body, grid=(1,), in_specs=[pl.BlockSpec(parts.shape, lambda i: (0, 0, 0))],
        out_specs=[pl.BlockSpec((n, 128), lambda i: (0, 0)) for n in rows],
        out_shape=[jax.ShapeDtypeStruct((n, 128), F32) for n in rows],
        compiler_params=_params(("arbitrary",), VMEM_BIG), name=name,
    )(parts)


def _adam_small(ws, gs, g_specs, ms, vs, name):
    n = len(ws)

    def body(*refs):
        w_r, g_r, m_r, v_r = refs[:n], refs[n:2 * n], refs[2 * n:3 * n], refs[3 * n:4 * n]
        outs = refs[4 * n:]
        for i in range(n):
            g = g_r[i][...]
            outs[4 * i][...] = g
            outs[4 * i + 1][...], outs[4 * i + 2][...], outs[4 * i + 3][...] = _adamw(
                w_r[i][...], g, m_r[i][...], v_r[i][...])

    whole = lambda a: pl.BlockSpec(a.shape, lambda i, nd=a.ndim: (0,) * nd)
    outs = pl.pallas_call(
        body, grid=(1,),
        in_specs=[whole(a) for a in ws] + list(g_specs) + [whole(a) for a in ms] + [whole(a) for a in vs],
        out_specs=[whole(a) for a in ws for _ in range(4)],
        out_shape=[jax.ShapeDtypeStruct(a.shape, F32) for a in ws for _ in range(4)],
        compiler_params=_params(("arbitrary",), VMEM_BIG), name=name,
    )(*ws, *gs, *ms, *vs)
    return [outs[4 * i:4 * i + 4] for i in range(n)]


MASKS = [(mx, my, mc) for mx in (0, 1) for my in (0, 1) for mc in (0, 1)][1:]


def _exchange(arrays, scatter, name):
    nt = len(arrays)
    out_shape = [jax.ShapeDtypeStruct(((NDEV,) + a.shape) if not scatter else a.shape, a.dtype) for a in arrays]

    def body(*refs):
        ins, outs = refs[:nt], refs[nt:2 * nt]
        send_sems, recv_sems, local_sems = refs[2 * nt:]
        x, y, c = lax.axis_index("x"), lax.axis_index("y"), lax.axis_index("c")
        me = 4 * x + 2 * y + c
        copies = []
        for t in range(nt):
            src_own = ins[t].at[me] if scatter else ins[t]
            loc = pltpu.make_async_copy(src_own, outs[t].at[me], local_sems.at[t])
            loc.start()
            copies.append(loc)
            for k, (mx, my, mc) in enumerate(MASKS):
                px, py, pc = (x + mx) % 2, (y + my) % 2, (c + mc) % 2
                peer = 4 * px + 2 * py + pc
                src = ins[t].at[peer] if scatter else ins[t]
                rc = pltpu.make_async_remote_copy(
                    src_ref=src, dst_ref=outs[t].at[me], send_sem=send_sems.at[t, k], recv_sem=recv_sems.at[t, k],
                    device_id=(px, py, pc), device_id_type=MESH)
                rc.start()
                copies.append(rc)
        for cp in copies:
            cp.wait()

    hbm = pl.BlockSpec(memory_space=pl.ANY)
    return pl.pallas_call(
        body, in_specs=[hbm] * nt, out_specs=[hbm] * nt, out_shape=out_shape,
        scratch_shapes=[pltpu.SemaphoreType.DMA((nt, 7)), pltpu.SemaphoreType.DMA((nt, 7)),
                        pltpu.SemaphoreType.DMA((nt,))],
        name=name,
    )(*arrays)


SEM_SPEC = pl.BlockSpec(memory_space=pltpu.SEMAPHORE)
EFFECT = pltpu.SideEffectType.DATAFLOW_SIDE_EFFECTING


def _direct_plan(scatter):
    def plan(x, y, c, srcs, lands):
        me = 4 * x + 2 * y + c
        local, remote = [], []
        for src, land in zip(srcs, lands):
            local.append((src.at[me] if scatter else src, land.at[me]))
            for mx, my, mc in MASKS:
                px, py, pc = (x + mx) % 2, (y + my) % 2, (c + mc) % 2
                blk = src.at[4 * px + 2 * py + pc] if scatter else src
                remote.append((blk, land.at[me], (px, py, pc)))
        return local, remote
    return plan


def _split_start(name, srcs, land_shapes, plan, n_local, n_remote, dep=None):
    ns, nl = len(srcs), len(land_shapes)
    deps = [] if dep is None else [dep]
    lands = [lax.empty(s.shape, s.dtype) for s in land_shapes]

    def body(*refs):
        ins, lz = refs[:ns], refs[ns:ns + nl]
        outs = refs[ns + nl + len(deps):]
        send_sems, recv_sems, token, local_sems = outs[0], outs[1], outs[2 + ns + nl], outs[3 + ns + nl]
        local, remote = plan(lax.axis_index("x"), lax.axis_index("y"), lax.axis_index("c"), ins, lz)
        own = [pltpu.make_async_copy(src, dst, local_sems.at[i]) for i, (src, dst) in enumerate(local)]
        for cp in own:
            cp.start()
        for cp in own:
            cp.wait()
        for k, (src, dst, peer) in enumerate(remote):
            pltpu.make_async_remote_copy(src_ref=src, dst_ref=dst, send_sem=send_sems.at[k], recv_sem=recv_sems.at[k],
                                         device_id=peer, device_id_type=MESH).start()
        token[...] = jnp.zeros_like(token)

    hbm = lambda a: pltpu.HBM(a.shape, a.dtype)
    outs = pl.pallas_call(
        body, name=name,
        out_shape=(pltpu.SemaphoreType.DMA((n_remote,)), pltpu.SemaphoreType.DMA((n_remote,)),
                   *[hbm(a) for a in srcs], *[hbm(a) for a in lands], jax.ShapeDtypeStruct((8, 128), F32)),
        in_specs=[ANY_SPEC] * (ns + nl + len(deps)),
        out_specs=(SEM_SPEC, SEM_SPEC, *[ANY_SPEC] * (ns + nl), pl.BlockSpec(memory_space=pltpu.VMEM)),
        scratch_shapes=[pltpu.SemaphoreType.DMA((n_local,))],
        input_output_aliases={i: 2 + i for i in range(ns + nl)},
        compiler_params=pltpu.CompilerParams(has_side_effects=EFFECT),
    )(*[pltpu.with_memory_space_constraint(a, pltpu.HBM) for a in srcs],
      *[pltpu.with_memory_space_constraint(a, pltpu.HBM) for a in lands], *deps)
    return dict(sems=outs[:2], srcs=outs[2:2 + ns], lands=outs[2 + ns:2 + ns + nl], token=outs[-1],
                plan=plan, n_remote=n_remote)


def _split_wait(name, handle, after):
    srcs, lands, plan = handle["srcs"], handle["lands"], handle["plan"]
    ns, nl = len(srcs), len(lands)

    def body(*refs):
        ins, lz = refs[:ns], refs[ns:ns + nl]
        send_sems, recv_sems = refs[ns + nl], refs[ns + nl + 1]
        _, remote = plan(lax.axis_index("x"), lax.axis_index("y"), lax.axis_index("c"), ins, lz)
        for k, (src, dst, peer) in enumerate(remote):
            cp = pltpu.make_async_remote_copy(src_ref=src, dst_ref=dst, send_sem=send_sems.at[k],
                                              recv_sem=recv_sems.at[k], device_id=peer, device_id_type=MESH)
            cp.wait_send()
            cp.wait_recv()

    hbm = lambda a: pltpu.HBM(a.shape, a.dtype)
    outs = pl.pallas_call(
        body, name=name, out_shape=(*[hbm(a) for a in srcs], *[hbm(a) for a in lands]),
        in_specs=[ANY_SPEC] * (ns + nl) + [SEM_SPEC, SEM_SPEC, ANY_SPEC], out_specs=tuple([ANY_SPEC] * (ns + nl)),
        input_output_aliases={i: i for i in range(ns + nl)},
        compiler_params=pltpu.CompilerParams(has_side_effects=EFFECT),
    )(*srcs, *lands, *handle["sems"], after)
    return list(outs[ns:])


def _sc_exchange(name, collective_id, arrays, scatter):
    nt = len(arrays)
    out_type = [jax.ShapeDtypeStruct(a.shape if scatter else (NDEV,) + a.shape, a.dtype) for a in arrays]

    def body(*refs):
        ins, outs = refs[:nt], refs[nt:2 * nt]
        send_sems, recv_sems, local_sems = refs[2 * nt:3 * nt], refs[3 * nt:4 * nt], refs[4 * nt:5 * nt]
        x, y, c = lax.axis_index("x"), lax.axis_index("y"), lax.axis_index("c")
        peers = [(mx + x - 2 * mx * x, my + y - 2 * my * y, mc + c - 2 * mc * c) for mx, my, mc in MASKS]
        barrier = pltpu.get_barrier_semaphore()
        for peer in peers:
            pl.semaphore_signal(barrier, inc=1, device_id=peer, device_id_type=MESH)
        pl.semaphore_wait(barrier, len(peers))
        me = 4 * x + 2 * y + c
        own = []
        for t in range(nt):
            cp = pltpu.make_async_copy(ins[t].at[me] if scatter else ins[t], outs[t].at[me], local_sems[t])
            cp.start()
            own.append(cp)
            for px, py, pc in peers:
                src = ins[t].at[4 * px + 2 * py + pc] if scatter else ins[t]
                pltpu.make_async_remote_copy(src_ref=src, dst_ref=outs[t].at[me], send_sem=send_sems[t],
                                             recv_sem=recv_sems[t], device_id=(px, py, pc), device_id_type=MESH).start()
        for t in range(nt):
            own[t].wait()
            seven = outs[t].at[pl.ds(0, NDEV - 1)]
            drain = pltpu.make_async_remote_copy(src_ref=seven, dst_ref=seven, send_sem=send_sems[t],
                                                 recv_sem=recv_sems[t], device_id=(x, y, c), device_id_type=MESH)
            drain.wait_send()
            drain.wait_recv()

    return pl.kernel(
        body, out_type=out_type, mesh=plsc.ScalarSubcoreMesh(axis_name="sequencer", num_cores=1),
        scratch_types=[pltpu.SemaphoreType.DMA] * (3 * nt),
        compiler_params=pltpu.CompilerParams(collective_id=collective_id), name=name,
    )(*arrays)


def _sc_gather_two_level(name, collective_id, arrays):
    nt = len(arrays)
    out_type = [jax.ShapeDtypeStruct((NDEV,) + a.shape, a.dtype) for a in arrays]

    def body(*refs):
        ins, outs = refs[:nt], refs[nt:2 * nt]
        sems = refs[2 * nt:]
        send_sems, sib_sems, local_sems = sems[:nt], sems[nt:2 * nt], sems[2 * nt:3 * nt]
        ici_sems = [sems[3 * nt + 3 * t:3 * nt + 3 * t + 3] for t in range(nt)]
        x, y, c = lax.axis_index("x"), lax.axis_index("y"), lax.axis_index("c")
        sibling = (x, y, 1 - c)
        chips = [(1 - x, y), (x, 1 - y), (1 - x, 1 - y)]
        barrier = pltpu.get_barrier_semaphore()
        for peer in [sibling] + [(cx, cy, c) for cx, cy in chips]:
            pl.semaphore_signal(barrier, inc=1, device_id=peer, device_id_type=MESH)
        pl.semaphore_wait(barrier, 4)
        me = 4 * x + 2 * y + c

        def push(t, src, slot, recv_sem, to):
            pltpu.make_async_remote_copy(src_ref=src, dst_ref=outs[t].at[slot], send_sem=send_sems[t],
                                         recv_sem=recv_sem, device_id=to, device_id_type=MESH).start()

        own = []
        for t in range(nt):
            cp = pltpu.make_async_copy(ins[t], outs[t].at[me], local_sems[t])
            cp.start()
            own.append(cp)
            for j, (cx, cy) in enumerate(chips):
                push(t, ins[t], me, ici_sems[t][j], (cx, cy, c))
            push(t, ins[t], me, sib_sems[t], sibling)
        for t in range(nt):
            for j, (cx, cy) in enumerate(chips):
                slot = 4 * cx + 2 * cy + c
                landed = outs[t].at[slot]
                pltpu.make_async_remote_copy(src_ref=landed, dst_ref=landed, send_sem=send_sems[t],
                                             recv_sem=ici_sems[t][j], device_id=(cx, cy, c),
                                             device_id_type=MESH).wait_recv()
                push(t, landed, slot, sib_sems[t], sibling)
        for t in range(nt):
            own[t].wait()
            four, seven = outs[t].at[pl.ds(0, 4)], outs[t].at[pl.ds(0, 7)]
            pltpu.make_async_remote_copy(src_ref=four, dst_ref=four, send_sem=send_sems[t], recv_sem=sib_sems[t],
                                         device_id=sibling, device_id_type=MESH).wait_recv()
            pltpu.make_async_remote_copy(src_ref=seven, dst_ref=seven, send_sem=send_sems[t], recv_sem=sib_sems[t],
                                         device_id=sibling, device_id_type=MESH).wait_send()

    return pl.kernel(
        body, out_type=out_type, mesh=plsc.ScalarSubcoreMesh(axis_name="sequencer", num_cores=1),
        scratch_types=[pltpu.SemaphoreType.DMA] * (6 * nt),
        compiler_params=pltpu.CompilerParams(collective_id=collective_id), name=name,
    )(*arrays)


def _sc_sibling_exchange(name, collective_id, src, out_shape, pieces):
    def body(src_ref, out_ref, send_sem, recv_sem):
        x, y, c = lax.axis_index("x"), lax.axis_index("y"), lax.axis_index("c")
        sibling = (x, y, 1 - c)
        barrier = pltpu.get_barrier_semaphore()
        pl.semaphore_signal(barrier, inc=1, device_id=sibling, device_id_type=MESH)
        pl.semaphore_wait(barrier, 1)
        for piece, lands in pieces(c, src_ref, out_ref):
            pltpu.make_async_remote_copy(src_ref=piece, dst_ref=lands, send_sem=send_sem, recv_sem=recv_sem,
                                         device_id=sibling, device_id_type=MESH).start()
        drain = pltpu.make_async_remote_copy(src_ref=out_ref, dst_ref=out_ref, send_sem=send_sem, recv_sem=recv_sem,
                                             device_id=sibling, device_id_type=MESH)
        drain.wait_send()
        drain.wait_recv()

    return pl.kernel(
        body, out_type=jax.ShapeDtypeStruct(out_shape, src.dtype),
        mesh=plsc.ScalarSubcoreMesh(axis_name="sequencer", num_cores=1), scratch_types=[pltpu.SemaphoreType.DMA] * 2,
        compiler_params=pltpu.CompilerParams(collective_id=collective_id), name=name,
    )(src)


def _sc_chip_scatter(name, collective_id, q):
    def body(q_ref, out_ref, send_sem, recv_sem, local_sem):
        x, y, c = lax.axis_index("x"), lax.axis_index("y"), lax.axis_index("c")
        chips = [(1 - x, y), (x, 1 - y), (1 - x, 1 - y)]
        barrier = pltpu.get_barrier_semaphore()
        for cx, cy in chips:
            pl.semaphore_signal(barrier, inc=1, device_id=(cx, cy, c), device_id_type=MESH)
        pl.semaphore_wait(barrier, 3)
        mine = 2 * x + y
        own = pltpu.make_async_copy(q_ref.at[mine], out_ref.at[mine], local_sem)
        own.start()
        for cx, cy in chips:
            pltpu.make_async_remote_copy(src_ref=q_ref.at[2 * cx + cy], dst_ref=out_ref.at[mine], send_sem=send_sem,
                                         recv_sem=recv_sem, device_id=(cx, cy, c), device_id_type=MESH).start()
        own.wait()
        three = out_ref.at[pl.ds(0, 3)]
        drain = pltpu.make_async_remote_copy(src_ref=three, dst_ref=three, send_sem=send_sem, recv_sem=recv_sem,
                                             device_id=(x, y, c), device_id_type=MESH)
        drain.wait_send()
        drain.wait_recv()

    return pl.kernel(
        body, out_type=jax.ShapeDtypeStruct(q.shape, q.dtype),
        mesh=plsc.ScalarSubcoreMesh(axis_name="sequencer", num_cores=1), scratch_types=[pltpu.SemaphoreType.DMA] * 3,
        compiler_params=pltpu.CompilerParams(collective_id=collective_id), name=name,
    )(q)


def _mm_pair_dw(h_own, dz, h_sib, dz_sib, nb, half, name, dep=None):
    tn = nb // 2
    o_spec = pl.BlockSpec((None, D, tn), lambda i, j, k: (j, 0, 0))
    part = _matmul(
        h_own, dz, dn=TN, grid=(1, 4, 1),
        a_spec=pl.BlockSpec((S, D), lambda i, j, k: (0, 0)),
        b_spec=pl.BlockSpec((S, tn), lambda i, j, k: (0, (2 * j + lax.axis_index("c")) * 2 + half)),
        o_spec=o_spec, out_shape=(4, D, tn), out_dtype=F32, acc_shape=(D, tn), name=name + "_own", dep=dep)

    def body(a_ref, b_ref, p_ref, o_ref):
        o_ref[...] = (p_ref[...] + _dot(a_ref[...], b_ref[...], TN)).astype(BF16)

    return pl.pallas_call(
        body, grid=(1, 4, 1),
        in_specs=[pl.BlockSpec((S, D), lambda i, j, k: (0, 0)), pl.BlockSpec((S, tn), lambda i, j, k: (0, j)), o_spec],
        out_specs=o_spec, out_shape=jax.ShapeDtypeStruct((4, D, tn), BF16),
        compiler_params=_params(("parallel", "parallel", "arbitrary"), VMEM_BIG), name=name + "_sibling",
    )(h_sib, dz_sib, part)


SMALL = {
    "e_pre_norm": ((2048,), None), "e_pool_w": ((4, 256, 256), 1), "e_pool_scale": ((1024,), None),
    "e_post_norm": ((2048,), None), "o_pre_norm": ((2048,), 0), "o_sgu_norm_g": ((1024,), 0),
    "o_sgu_norm_b": ((1024,), 0), "o_sgu_w": ((4, 128, 128), None), "o_sgu_b": ((4, 128), None),
    "o_conv_w": ((31, 1024), 1), "o_conv_b": ((1024,), 0), "o_conv_norm_g": ((1024,), 0),
    "o_conv_norm_b": ((1024,), 0), "o_post_norm": ((2048,), 0),
}
SMALL_SHARDED = [n for n, (_, ax) in SMALL.items() if ax is not None]


def _shard_shape(name):
    shape, ax = SMALL[name]
    if ax is None:
        return shape
    return tuple(s // NDEV if i == ax else s for i, s in enumerate(shape))


def _pack(arrs, row_multiple=1):
    flat = jnp.concatenate([a.reshape(-1) for a in arrs])
    pad = -flat.shape[0] % (128 * row_multiple)
    return jnp.concatenate([flat, jnp.zeros((pad,), F32)]).reshape(-1, 128)


def _small_views(name):
    shape, ax = SMALL[name]
    me = lambda: 4 * lax.axis_index("x") + 2 * lax.axis_index("y") + lax.axis_index("c")
    if ax is None:
        view = (int(np.prod(shape)) // 128, 128)
        return view, view, pl.BlockSpec(view, lambda i: (0, 0))
    if len(shape) == 1:
        n = shape[0] // NDEV
        return (1, n), (NDEV, 1, n), pl.BlockSpec((None, 1, n), lambda i: (me(), 0, 0))
    part = _shard_shape(name)
    return part, shape, pl.BlockSpec(part, lambda i: tuple(me() if d == ax else 0 for d in range(len(shape))))


BIG = ("e_w_in", "e_w_out", "o_w_in", "o_w_out")
WEIGHTS = ["e_pre_norm", "e_w_in", "e_pool_w", "e_pool_scale", "e_w_out", "e_post_norm", "o_pre_norm", "o_w_in",
           "o_sgu_norm_g", "o_sgu_norm_b", "o_sgu_w", "o_sgu_b", "o_conv_w", "o_conv_b", "o_conv_norm_g",
           "o_conv_norm_b", "o_w_out", "o_post_norm"]


def kernel(x, e_pre_norm, e_w_in, e_pool_w, e_pool_scale, e_w_out, e_post_norm, o_pre_norm, o_w_in, o_sgu_norm_g, o_sgu_norm_b, o_sgu_w, o_sgu_b, o_conv_w, o_conv_b, o_conv_norm_g, o_conv_norm_b, o_w_out, o_post_norm, loss_target, m_e_pre_norm, m_e_w_in, m_e_pool_w, m_e_pool_scale, m_e_w_out, m_e_post_norm, m_o_pre_norm, m_o_w_in, m_o_sgu_norm_g, m_o_sgu_norm_b, m_o_sgu_w, m_o_sgu_b, m_o_conv_w, m_o_conv_b, m_o_conv_norm_g, m_o_conv_norm_b, m_o_w_out, m_o_post_norm, v_e_pre_norm, v_e_w_in, v_e_pool_w, v_e_pool_scale, v_e_w_out, v_e_post_norm, v_o_pre_norm, v_o_w_in, v_o_sgu_norm_g, v_o_sgu_norm_b, v_o_sgu_w, v_o_sgu_b, v_o_conv_w, v_o_conv_b, v_o_conv_norm_g, v_o_conv_norm_b, v_o_w_out, v_o_post_norm):
    given = dict(locals())
    w = {n: given[n][0] for n in WEIGHTS}
    m = {n: given["m_" + n][0] for n in WEIGHTS}
    v = {n: given["v_" + n][0] for n in WEIGHTS}
    me = 4 * lax.axis_index("x") + 2 * lax.axis_index("y") + lax.axis_index("c")
    x, target = x[0], loss_target[0]
    row = lambda a: a.reshape(1, -1)

    wg_e_in, small_rows = _sc_gather_two_level(
        "gather_a", 0, [_cast_bf16(w["e_w_in"], "cast_e_w_in"), _pack([w[n] for n in SMALL_SHARDED])])
    h0 = _pre0_fwd(x, row(w["e_pre_norm"]))
    wg_e_out, wg_o_in, wg_o_out = _sc_gather_two_level(
        "gather_b", 1, [_cast_bf16(w[n], "cast_" + n) for n in ("e_w_out", "o_w_in", "o_w_out")])
    h0_sib = _sc_sibling_exchange("swap_h0", 8, h0, h0.shape, lambda c, src, out: [(src, out)])
    p = {n: w[n] for n in SMALL if SMALL[n][1] is None}
    small_rows = small_rows.reshape(NDEV, -1)
    off = 0
    for n in SMALL_SHARDED:
        shp, ax = _shard_shape(n), SMALL[n][1]
        cnt = int(np.prod(shp))
        blk = small_rows[:, off:off + cnt].reshape((NDEV,) + shp)
        p[n] = jnp.moveaxis(blk, 0, ax).reshape(SMALL[n][0])
        off += cnt
    tabs = _rope_tables()
    pool_w_bf = p["e_pool_w"].astype(BF16)
    sgu_bb = jnp.broadcast_to(p["o_sgu_b"][:, :, None], (4, 128, 128))
    conv_w = jnp.concatenate([p["o_conv_w"], jnp.zeros((HALO - CONV_K, HALF), F32)], axis=0)
    odd_p = (row(p["o_sgu_norm_g"]), row(p["o_sgu_norm_b"]), p["o_sgu_w"], sgu_bb, conv_w,
             row(p["o_conv_b"]), row(p["o_conv_norm_g"]), row(p["o_conv_norm_b"]))

    z0 = _mm_in(h0, wg_e_in, "mm_z0")
    ycat0 = _pool_fwd(z0, pool_w_bf, row(p["e_pool_scale"]))
    qkv = _qkv_prep(z0, tabs)
    ycat0, og, lg = _attn_fwd(z0, qkv, ycat0)
    w_out_e, w_out_o = wg_e_out.reshape(2048, D), wg_o_out.reshape(2048, D)
    y0 = _mm_out(ycat0, w_out_e, "mm_y0", h0_sib)
    x1, h1 = _post0_fwd(x, y0, row(p["e_post_norm"]), row(p["o_pre_norm"]))
    z1 = _mm_in(h1, wg_o_in, "mm_z1")
    ycat1 = _odd_fwd(z1, *odd_p)
    y1 = _mm_out(ycat1, w_out_o, "mm_y1")

    g = {}
    loss, dx2, dy1, g["o_post_norm"] = _post1_bwd(y1, x1, target, row(p["o_post_norm"]))
    loss = lax.psum(loss[0, 0], ("x", "y", "c"))
    parts = {}
    dw = _mm_out_dw(ycat1, dy1, "mm_dwout1").reshape(NDEV, 256, D)
    parts["o_w_out"], = _sc_exchange("scatter_o_w_out", 2, [dw], True)
    dycat1 = _mm_out_dx(dy1, w_out_o, "mm_dycat1", (dw, loss.reshape(1, 1)))
    dz1, ddc, g["o_sgu_w"], d_sgu_bb, g["o_sgu_norm_g"], g["o_sgu_norm_b"], g["o_conv_norm_g"], \
        g["o_conv_norm_b"], g["o_conv_b"] = _odd_bwd_a(z1, dycat1, *odd_p)
    dz1, d_conv_w = _odd_bwd_b(z1, ddc, dz1, conv_w)
    g["o_sgu_b"] = d_sgu_bb[:, :, 0]
    g["o_conv_w"] = d_conv_w[:CONV_K]
    grads, deltas, new_m, new_v = {}, {}, {}, {}

    def adam(n, dep):
        grads[n], deltas[n], new_m[n], new_v[n] = _adam_reduce(parts[n], w[n], m[n], v[n], "adam_" + n, dep)
        return new_v[n]

    pin = adam("o_w_out", d_conv_w)
    dw = _mm_in_dw(h1, dz1, ODD_IN // NDEV, "mm_dwin1", pin)
    parts["o_w_in"], = _sc_exchange("scatter_o_w_in", 3, [dw], True)
    dh1 = _mm_in_dx(dz1, wg_o_in, "mm_dh1", dw)
    dx1, dy0, g["o_pre_norm"], g["e_post_norm"] = _mid_bwd(dx2, dh1, x1, y0, row(p["o_pre_norm"]),
                                                           row(p["e_post_norm"]))
    dw = _mm_out_dw(ycat0, dy0, "mm_dwout0").reshape(NDEV, 256, D)
    parts["e_w_out"], = _sc_exchange("scatter_e_w_out", 4, [dw], True)
    dycat0 = _mm_out_dx(dy0, w_out_e, "mm_dycat0", dw)
    da_in, da_gate, g["e_pool_w"], g["e_pool_scale"] = _pool_bwd(z0, dycat0, pool_w_bf, row(p["e_pool_scale"]))
    dq, dk, dv, dbg = _attn_bwd(z0, qkv, og, lg, dycat0, tabs)
    dz0 = jnp.concatenate([da_in, da_gate, dq, dk, dv, dbg], axis=1)
    late = [n for n in SMALL if n not in ("e_pre_norm", "o_sgu_b")] + ["o_sgu_b"]
    pin = adam("e_w_out", adam("o_w_in", dz0))
    nb, hb = EVEN_IN // NDEV, EVEN_IN // NDEV // 2
    swapped = [_sc_sibling_exchange(
        "swap_dz0_%d" % half, 9 + half, dz0, (S, 4 * hb),
        lambda c, src, out, half=half: [(src.at[:, pl.ds((2 * j + 1 - c) * nb + half * hb, hb)],
                                         out.at[:, pl.ds(j * hb, hb)]) for j in range(4)]) for half in (0, 1)]
    dw = pin
    parts["e_w_in"] = []
    for half in (0, 1):
        dw = _mm_pair_dw(h0, dz0, h0_sib, swapped[half], nb, half, "mm_dwin0_%d" % half, dw)
        parts["e_w_in"].append(_sc_chip_scatter("scatter_e_w_in_%d" % half, (5, 11)[half], dw))
    recv_small, = _sc_exchange("gather_small_grads", 6, [_pack([g[n].reshape(SMALL[n][0]) for n in late], 512)], False)
    dh0 = _mm_in_dx(dz0, wg_e_in, "mm_dh0", dw)
    grad_x, g["e_pre_norm"] = _pre0_bwd(dx1, dh0, x, row(p["e_pre_norm"]))
    last, = _sc_exchange("gather_e_pre_norm_grad", 7, [g["e_pre_norm"].reshape(16, 128)], False)

    rows = [int(np.prod(SMALL[n][0])) // 128 for n in late]
    summed = dict(zip(late, _sum_unpack(recv_small, rows, "sum_small_grads")))
    pin = adam("e_w_in", grad_x)
    summed["e_pre_norm"] = _sum_parts(last, "sum_e_pre_norm_grad", pin)
    names = list(SMALL)
    views = [_small_views(n) for n in names]
    mine = lambda src: [src[n].reshape(vw[0]) for n, vw in zip(names, views)]
    res = _adam_small(mine(w), [summed[n].reshape(vw[1]) for n, vw in zip(names, views)], [vw[2] for vw in views],
                      mine(m), mine(v), "adam_small")
    for n, out in zip(names, res):
        grads[n], deltas[n], new_m[n], new_v[n] = [t.reshape(_shard_shape(n)) for t in out]

    lead = lambda a: a[None]
    return (loss, grad_x[None], *[lead(grads[n]) for n in WEIGHTS], *[lead(deltas[n]) for n in WEIGHTS],
            *[lead(new_m[n]) for n in WEIGHTS], *[lead(new_v[n]) for n in WEIGHTS])
```

```python
import functools

import numpy as np
import jax
import jax.numpy as jnp
from jax import lax
from jax.experimental import pallas as pl
from jax.experimental.pallas import tpu as pltpu
from jax.experimental.pallas import tpu_sc as plsc

F32 = jnp.float32
BF16 = jnp.bfloat16

S = 2048
D = 2048
NDEV = 8
EPS = 1e-6
NEG = -1e30
HEAD_DIM = 128
ROT_DIM = 32
ROPE_THETA = 500000.0
PATTERNS = ((128, 1), (512, 4), (2048, 16))
BLK = 128
EVEN_IN = 12288
ODD_IN = 6144
HALF = 1024
CONV_K = 31
HALO = 32
TR = 256
SUB = 32

ADAM_LR = 0.001
ADAM_B1 = 0.9
ADAM_B2 = 0.999
ADAM_EPS = 1e-08
ADAM_WD = 0.01
ADAM_STEP = 10

VMEM_BIG = 56 * 1024 * 1024
MESH = pl.DeviceIdType.MESH

NN = (((1,), (0,)), ((), ()))
NT = (((1,), (1,)), ((), ()))
TN = (((0,), (0,)), ((), ()))


def _dot(a, b, dn=NN):
    return lax.dot_general(a, b, dn, preferred_element_type=F32)


def _sigmoid(x):
    return 1.0 / (1.0 + jnp.exp(-x))


def _silu_and_grad(x):
    sg = _sigmoid(x)
    return x * sg, sg * (1.0 + x * (1.0 - sg))


def _params(sem, vmem=None):
    return pltpu.CompilerParams(dimension_semantics=sem, vmem_limit_bytes=vmem)


ANY_SPEC = pl.BlockSpec(memory_space=pl.ANY)


def _matmul(a, b, *, dn, grid, a_spec, b_spec, o_spec, out_shape, out_dtype, acc_shape, name, dep=None):
    nk = grid[2]
    deps = [] if dep is None else list(dep) if isinstance(dep, (tuple, list)) else [dep]

    def body(a_ref, b_ref, *rest):
        o_ref, acc = rest[len(deps)], rest[len(deps) + 1:]
        if nk == 1:
            o_ref[...] = _dot(a_ref[...], b_ref[...], dn).astype(o_ref.dtype)
            return
        acc_ref = acc[0]
        k = pl.program_id(2)

        @pl.when(k == 0)
        def _():
            acc_ref[...] = jnp.zeros_like(acc_ref)

        acc_ref[...] += _dot(a_ref[...], b_ref[...], dn)

        @pl.when(k == nk - 1)
        def _():
            o_ref[...] = acc_ref[...].astype(o_ref.dtype)

    return pl.pallas_call(
        body, grid=grid, in_specs=[a_spec, b_spec] + [ANY_SPEC] * len(deps), out_specs=o_spec,
        out_shape=jax.ShapeDtypeStruct(out_shape, out_dtype),
        scratch_shapes=[] if nk == 1 else [pltpu.VMEM(acc_shape, F32)],
        compiler_params=_params(("parallel", "parallel", "arbitrary"), VMEM_BIG), name=name,
    )(a, b, *deps)


TM = 2048


def _mm_in(h, wg, name):
    nb = wg.shape[2]
    tn = 512 if nb % 512 == 0 else nb
    per = nb // tn
    return _matmul(
        h, wg, dn=NN, grid=(S // TM, NDEV * per, 1),
        a_spec=pl.BlockSpec((TM, D), lambda i, j, k: (i, 0)),
        b_spec=pl.BlockSpec((None, D, tn), lambda i, j, k: (j // per, 0, j % per)),
        o_spec=pl.BlockSpec((TM, tn), lambda i, j, k: (i, j)),
        out_shape=(S, NDEV * nb), out_dtype=F32, acc_shape=(TM, tn), name=name)


def _mm_in_dx(dz, wg, name, dep=None):
    nb = wg.shape[2]
    return _matmul(
        dz, wg, dn=NT, grid=(S // TM, D // 1024, NDEV),
        a_spec=pl.BlockSpec((TM, nb), lambda i, j, k: (i, k)),
        b_spec=pl.BlockSpec((None, 1024, nb), lambda i, j, k: (k, j, 0)),
        o_spec=pl.BlockSpec((TM, 1024), lambda i, j, k: (i, j)),
        out_shape=(S, D), out_dtype=F32, acc_shape=(TM, 1024), name=name, dep=dep)


def _mm_in_dw(h, dz, nb, name, dep=None):
    tn = 512 if nb % 512 == 0 else nb
    per = nb // tn
    return _matmul(
        h, dz, dn=TN, grid=(D // TM, NDEV * per, 1),
        a_spec=pl.BlockSpec((S, TM), lambda i, j, k: (0, i)),
        b_spec=pl.BlockSpec((S, tn), lambda i, j, k: (0, j)),
        o_spec=pl.BlockSpec((None, TM, tn), lambda i, j, k: (j // per, i, j % per)),
        out_shape=(NDEV, D, nb), out_dtype=BF16, acc_shape=(TM, tn), name=name, dep=dep)


def _mm_out(yc, w, name, dep=None):
    return _matmul(
        yc, w, dn=NN, grid=(S // TM, D // 512, 1),
        a_spec=pl.BlockSpec((TM, 2048), lambda i, j, k: (i, 0)),
        b_spec=pl.BlockSpec((2048, 512), lambda i, j, k: (0, j)),
        o_spec=pl.BlockSpec((TM, 512), lambda i, j, k: (i, j)),
        out_shape=(S, D), out_dtype=F32, acc_shape=(TM, 512), name=name, dep=dep)


def _mm_out_dx(dy, w, name, dep=None):
    return _matmul(
        dy, w, dn=NT, grid=(S // TM, 2048 // 512, 1),
        a_spec=pl.BlockSpec((TM, D), lambda i, j, k: (i, 0)),
        b_spec=pl.BlockSpec((512, D), lambda i, j, k: (j, 0)),
        o_spec=pl.BlockSpec((TM, 512), lambda i, j, k: (i, j)),
        out_shape=(S, 2048), out_dtype=F32, acc_shape=(TM, 512), name=name, dep=dep)


def _mm_out_dw(yc, dy, name):
    return _matmul(
        yc, dy, dn=TN, grid=(2048 // TM, D // 512, 1),
        a_spec=pl.BlockSpec((S, TM), lambda i, j, k: (0, i)),
        b_spec=pl.BlockSpec((S, 512), lambda i, j, k: (0, j)),
        o_spec=pl.BlockSpec((TM, 512), lambda i, j, k: (i, j)),
        out_shape=(2048, D), out_dtype=BF16, acc_shape=(TM, 512), name=name)


def _row_spec(w=D):
    return pl.BlockSpec((TR, w), lambda i: (i, 0))


def _vec_spec(w=D):
    return pl.BlockSpec((1, w), lambda i: (0, 0))


def _rms_stats(x):
    r = lax.rsqrt(jnp.mean(x * x, axis=-1, keepdims=True) + EPS)
    return x * r, r


def _rms_bwd(dn, xhat, r, g):
    dxh = dn * g
    return r * (dxh - xhat * jnp.mean(dxh * xhat, axis=-1, keepdims=True))


def _acc_rows(ref, val, i):
    s = jnp.sum(val, axis=0, keepdims=True)

    @pl.when(i == 0)
    def _():
        ref[...] = s

    @pl.when(i > 0)
    def _():
        ref[...] += s


def _pre0_fwd(x, g, dep=None):
    deps = [] if dep is None else [dep]

    def body(x_ref, g_ref, *rest):
        xhat, _ = _rms_stats(x_ref[...])
        rest[-1][...] = (xhat * g_ref[...]).astype(BF16)

    return pl.pallas_call(
        body, grid=(S // TR,), in_specs=[_row_spec(), _vec_spec()] + [ANY_SPEC] * len(deps), out_specs=_row_spec(),
        out_shape=jax.ShapeDtypeStruct((S, D), BF16), compiler_params=_params(("parallel",)), name="pre0_fwd",
    )(x, g, *deps)


def _post0_fwd(x, y0, g_post, g_pre1):
    def body(x_ref, y_ref, gp_ref, g1_ref, x1_ref, h1_ref):
        yhat, _ = _rms_stats(y_ref[...])
        x1 = x_ref[...] + yhat * gp_ref[...]
        x1_ref[...] = x1
        xhat, _ = _rms_stats(x1)
        h1_ref[...] = (xhat * g1_ref[...]).astype(BF16)

    return pl.pallas_call(
        body, grid=(S // TR,), in_specs=[_row_spec(), _row_spec(), _vec_spec(), _vec_spec()],
        out_specs=[_row_spec(), _row_spec()],
        out_shape=[jax.ShapeDtypeStruct((S, D), F32), jax.ShapeDtypeStruct((S, D), BF16)],
        compiler_params=_params(("parallel",)), name="post0_fwd",
    )(x, y0, g_post, g_pre1)


def _post1_bwd(y1, x1, target, g_post):
    def body(y_ref, x1_ref, t_ref, g_ref, loss_ref, dx2_ref, dy_ref, dg_ref):
        i = pl.program_id(0)
        yhat, r = _rms_stats(y_ref[...])
        g = g_ref[...]
        err = x1_ref[...] + yhat * g - t_ref[...]
        part = jnp.sum(jnp.sum(err * err, axis=-1, keepdims=True), axis=0, keepdims=True) * (0.5 / D)
        _acc_rows(loss_ref, jnp.broadcast_to(part, (1, 128)), i)
        dx2 = err * (1.0 / D)
        dx2_ref[...] = dx2
        _acc_rows(dg_ref, dx2 * yhat, i)
        dy_ref[...] = _rms_bwd(dx2, yhat, r, g).astype(BF16)

    return pl.pallas_call(
        body, grid=(S // TR,), in_specs=[_row_spec(), _row_spec(), _row_spec(), _vec_spec()],
        out_specs=[_vec_spec(128), _row_spec(), _row_spec(), _vec_spec()],
        out_shape=[jax.ShapeDtypeStruct((1, 128), F32), jax.ShapeDtypeStruct((S, D), F32),
                   jax.ShapeDtypeStruct((S, D), BF16), jax.ShapeDtypeStruct((1, D), F32)],
        compiler_params=_params(("arbitrary",)), name="post1_bwd",
    )(y1, x1, target, g_post)


def _mid_bwd(dx2, dh1, x1, y0, g_pre1, g_post0):
    def body(dx2_ref, dh_ref, x1_ref, y_ref, g1_ref, gp_ref, dx1_ref, dy_ref, dg1_ref, dgp_ref):
        i = pl.program_id(0)
        xhat, r1 = _rms_stats(x1_ref[...])
        dh = dh_ref[...]
        _acc_rows(dg1_ref, dh * xhat, i)
        dx1 = dx2_ref[...] + _rms_bwd(dh, xhat, r1, g1_ref[...])
        dx1_ref[...] = dx1
        yhat, r0 = _rms_stats(y_ref[...])
        _acc_rows(dgp_ref, dx1 * yhat, i)
        dy_ref[...] = _rms_bwd(dx1, yhat, r0, gp_ref[...]).astype(BF16)

    return pl.pallas_call(
        body, grid=(S // TR,),
        in_specs=[_row_spec(), _row_spec(), _row_spec(), _row_spec(), _vec_spec(), _vec_spec()],
        out_specs=[_row_spec(), _row_spec(), _vec_spec(), _vec_spec()],
        out_shape=[jax.ShapeDtypeStruct((S, D), F32), jax.ShapeDtypeStruct((S, D), BF16),
                   jax.ShapeDtypeStruct((1, D), F32), jax.ShapeDtypeStruct((1, D), F32)],
        compiler_params=_params(("arbitrary",)), name="mid_bwd",
    )(dx2, dh1, x1, y0, g_pre1, g_post0)


def _pre0_bwd(dx1, dh0, x, g):
    def body(dx1_ref, dh_ref, x_ref, g_ref, gx_ref, dg_ref):
        i = pl.program_id(0)
        xhat, r = _rms_stats(x_ref[...])
        dh = dh_ref[...]
        _acc_rows(dg_ref, dh * xhat, i)
        gx_ref[...] = dx1_ref[...] + _rms_bwd(dh, xhat, r, g_ref[...])

    return pl.pallas_call(
        body, grid=(S // TR,), in_specs=[_row_spec(), _row_spec(), _row_spec(), _vec_spec()],
        out_specs=[_row_spec(), _vec_spec()],
        out_shape=[jax.ShapeDtypeStruct((S, D), F32), jax.ShapeDtypeStruct((1, D), F32)],
        compiler_params=_params(("arbitrary",)), name="pre0_bwd",
    )(dx1, dh0, x, g)


POOL_CH = 256


def _pool_apply(a, w, transpose):
    n = a.shape[0]
    row = lax.broadcasted_iota(jnp.int32, a.shape, 0)
    cnt = jnp.minimum(row + 1, w).astype(F32)
    s = a / cnt if transpose else a
    for k in (1, 2, 4, 8):
        if transpose:
            sh = jnp.where(row < n - k, pltpu.roll(s, n - k, 0), 0.0)
        else:
            sh = jnp.where(row >= k, pltpu.roll(s, k, 0), 0.0)
        s = jnp.where(w > k, s + sh, s)
    return s - a if transpose else s / cnt - a


def _pool_fwd(z0, pool_w, pool_scale):
    def body(a_ref, gate_ref, w_ref, sc_ref, out_ref):
        win = jnp.left_shift(2, pl.program_id(0))
        pooled = _pool_apply(a_ref[...], win, False)
        mixed = _dot(pooled.astype(BF16), w_ref[...])
        gate = gate_ref[...]
        out_ref[...] = (mixed * sc_ref[...] * (gate * _sigmoid(gate))).astype(BF16)

    return pl.pallas_call(
        body, grid=(4,),
        in_specs=[pl.BlockSpec((S, POOL_CH), lambda g: (0, g)), pl.BlockSpec((S, POOL_CH), lambda g: (0, 4 + g)),
                  pl.BlockSpec((None, POOL_CH, POOL_CH), lambda g: (g, 0, 0)),
                  pl.BlockSpec((1, POOL_CH), lambda g: (0, g))],
        out_specs=pl.BlockSpec((S, POOL_CH), lambda g: (0, g)),
        out_shape=jax.ShapeDtypeStruct((S, 2048), BF16),
        compiler_params=_params(("parallel",), VMEM_BIG), name="pool_fwd",
    )(z0, z0, pool_w, pool_scale)


def _pool_bwd(z0, dycat, pool_w, pool_scale):
    def body(a_ref, gate_ref, dy_ref, w_ref, sc_ref, da_ref, dgate_ref, dw_ref, dsc_ref):
        win = jnp.left_shift(2, pl.program_id(0))
        pooled = _pool_apply(a_ref[...], win, False).astype(BF16)
        w = w_ref[...]
        mixed = _dot(pooled, w)
        silu, dsilu = _silu_and_grad(gate_ref[...])
        dy = dy_ref[...]
        sc = sc_ref[...]
        dgate_ref[...] = (dy * (mixed * sc) * dsilu).astype(BF16)
        dms = dy * silu
        dsc_ref[...] = jnp.sum(dms * mixed, axis=0, keepdims=True)
        dmixed = (dms * sc).astype(BF16)
        dw_ref[...] = _dot(pooled, dmixed, TN)
        dpooled = _dot(dmixed, w, NT)
        da_ref[...] = _pool_apply(dpooled, win, True).astype(BF16)

    slab = lambda off: pl.BlockSpec((S, POOL_CH), lambda g: (0, off + g))
    return pl.pallas_call(
        body, grid=(4,),
        in_specs=[slab(0), slab(4), slab(0), pl.BlockSpec((None, POOL_CH, POOL_CH), lambda g: (g, 0, 0)),
                  pl.BlockSpec((1, POOL_CH), lambda g: (0, g))],
        out_specs=[slab(0), slab(0), pl.BlockSpec((None, POOL_CH, POOL_CH), lambda g: (g, 0, 0)),
                   pl.BlockSpec((1, POOL_CH), lambda g: (0, g))],
        out_shape=[jax.ShapeDtypeStruct((S, HALF), BF16), jax.ShapeDtypeStruct((S, HALF), BF16),
                   jax.ShapeDtypeStruct((4, POOL_CH, POOL_CH), F32), jax.ShapeDtypeStruct((1, HALF), F32)],
        compiler_params=_params(("parallel",), VMEM_BIG), name="pool_bwd",
    )(z0, z0, dycat, pool_w, pool_scale)


Q_COL, K_COL, V_COL, BG_COL = 2048 // 128, 5120 // 128, 8192 // 128, 11264 // 128
SCALE = HEAD_DIM ** -0.5


def _rope_tables():
    pos = jnp.arange(S, dtype=F32)
    inv_freq = jnp.power(ROPE_THETA, -jnp.arange(0, ROT_DIM, 2, dtype=F32) / ROT_DIM)
    ang = pos[:, None] * inv_freq[None, :]
    cos, sin = jnp.cos(ang), jnp.sin(ang)
    half = ROT_DIM // 2
    zeros = jnp.zeros((S, HEAD_DIM - ROT_DIM), F32)
    c = jnp.concatenate([cos, cos, jnp.ones((S, HEAD_DIM - ROT_DIM), F32)], axis=1)
    a = jnp.concatenate([-sin, jnp.zeros((S, half), F32), zeros], axis=1)
    b = jnp.concatenate([jnp.zeros((S, half), F32), sin, zeros], axis=1)
    return c, a, b


def _rope(t, c, a, b):
    half = ROT_DIM // 2
    return t * c + pltpu.roll(t, HEAD_DIM - half, 1) * a + pltpu.roll(t, half, 1) * b


def _rope_t(d, c, a, b):
    half = ROT_DIM // 2
    return d * c + pltpu.roll(d * a, half, 1) + pltpu.roll(d * b, HEAD_DIM - half, 1)


def _deinterleave(dst, src, dil, cast=None, dst_off=0):
    length = S // dil
    for r in range(dil):
        v = src[...] if dil == 1 else src[pl.ds(r, length, stride=dil), :]
        dst[dst_off + r * length:dst_off + (r + 1) * length, :] = v if cast is None else v.astype(cast)


def _interleave(dst, src, dil, src_off=0):
    length = S // dil
    for r in range(dil):
        if dil == 1:
            dst[...] = src[src_off:src_off + S, :]
        else:
            dst[pl.ds(r, length, stride=dil), :] = src[src_off + r * length:src_off + (r + 1) * length, :]


CU = 4
NUNITS = S // BLK
B_QK = (((2,), (2,)), ((0,), (0,)))
B_PV = (((2,), (1,)), ((0,), (0,)))
B_TN = (((1,), (1,)), ((0,), (0,)))


def _blocks(ref, first):
    return ref[first * BLK:(first + CU) * BLK, :].reshape(CU, BLK, HEAD_DIM)


def _chunk_scores(u0, nb, qd, kdp):
    q = _blocks(qd, u0)
    row = lax.broadcasted_iota(jnp.int32, (CU, BLK, BLK), 1)
    col = lax.broadcasted_iota(jnp.int32, (CU, BLK, BLK), 2)
    s_own = jnp.where(col <= row, _dot(q, _blocks(kdp, u0 + 1), B_QK) * SCALE, NEG)
    if nb == 1:
        return q, s_own, None
    unit = lax.broadcasted_iota(jnp.int32, (CU, BLK, BLK), 0) + u0
    s_prev = jnp.where((col >= row) & ((unit % nb) != 0), _dot(q, _blocks(kdp, u0), B_QK) * SCALE, NEG)
    return q, s_own, s_prev


def _qkv_prep(z0, tabs):
    def body(q_ref, k_ref, v_ref, c_ref, a_ref, b_ref, qo_ref, ko_ref, vo_ref, tmp):
        p = pl.program_id(1)
        for gi, (_, dil) in enumerate(PATTERNS):
            @pl.when(p == gi)
            def _(dil=dil):
                c, a, b = c_ref[...], a_ref[...], b_ref[...]
                tmp[...] = _rope(q_ref[...], c, a, b)
                _deinterleave(qo_ref, tmp, dil, BF16)
                tmp[...] = _rope(k_ref[...], c, a, b)
                _deinterleave(ko_ref, tmp, dil, BF16)
                _deinterleave(vo_ref, v_ref, dil, BF16)

    tab = pl.BlockSpec((S, HEAD_DIM), lambda h, p: (0, 0))
    out = pl.BlockSpec((S, HEAD_DIM), lambda h, p: (0, p * 8 + h))
    return pl.pallas_call(
        body, grid=(8, 3), in_specs=[_head_spec(Q_COL), _head_spec(K_COL), _head_spec(V_COL), tab, tab, tab],
        out_specs=[out, out, out], out_shape=[jax.ShapeDtypeStruct((S, 3072), BF16)] * 3,
        scratch_shapes=[pltpu.VMEM((S, HEAD_DIM), F32)],
        compiler_params=_params(("parallel", "arbitrary"), VMEM_BIG), name="qkv_prep",
    )(z0, z0, z0, *tabs)


def _pad_copy(dst, src):
    dst[0:BLK, :] = jnp.zeros((BLK, HEAD_DIM), dst.dtype)
    dst[BLK:BLK + S, :] = src[...]


def _attn_group_fwd(dil, qd, kd_ref, vd_ref, kdp, vdp, od, ld, og, lg):
    nb = S // dil // BLK
    _pad_copy(kdp, kd_ref)
    _pad_copy(vdp, vd_ref)
    for u0 in range(0, NUNITS, CU):
        _, s_own, s_prev = _chunk_scores(u0, nb, qd, kdp)
        m = jnp.max(s_own, axis=2, keepdims=True)
        if s_prev is not None:
            m = jnp.maximum(m, jnp.max(s_prev, axis=2, keepdims=True))
        p_own = jnp.exp(s_own - m)
        den = jnp.sum(p_own, axis=2, keepdims=True)
        acc = _dot(p_own.astype(BF16), _blocks(vdp, u0 + 1), B_PV)
        if s_prev is not None:
            p_prev = jnp.exp(s_prev - m)
            den = den + jnp.sum(p_prev, axis=2, keepdims=True)
            acc = acc + _dot(p_prev.astype(BF16), _blocks(vdp, u0), B_PV)
        rows = slice(u0 * BLK, (u0 + CU) * BLK)
        od[rows, :] = (acc / den).reshape(CU * BLK, HEAD_DIM)
        ld[rows, :] = jnp.broadcast_to(m + jnp.log(den), (CU, BLK, HEAD_DIM)).reshape(CU * BLK, HEAD_DIM)
    _interleave(og, od, dil)
    _interleave(lg, ld, dil)


def _group_weights(lgs):
    l0, l1, l2 = lgs[0][...], lgs[1][...], lgs[2][...]
    mx = jnp.maximum(l0, jnp.maximum(l1, l2))
    e0, e1, e2 = jnp.exp(l0 - mx), jnp.exp(l1 - mx), jnp.exp(l2 - mx)
    den = e0 + e1 + e2
    return e0 / den, e1 / den, e2 / den


def _head_spec(base, ngroups_axis=True):
    return pl.BlockSpec((S, HEAD_DIM), lambda h, p: (0, base + (p % 3) * 8 + h))


def _slab(dtype=F32, rows=S):
    return pltpu.VMEM((rows, HEAD_DIM), dtype)


def _attn_fwd(z0, qkv, ycat):
    def body(q_ref, k_ref, v_ref, gate_ref, ycat_ref, out_ref, og_ref, lg_ref,
             kdp, vdp, od, ld, og0, og1, og2, lg0, lg1, lg2):
        del ycat_ref
        p = pl.program_id(1)
        ogs, lgs = (og0, og1, og2), (lg0, lg1, lg2)
        for gi, (_, dil) in enumerate(PATTERNS):
            @pl.when(p == gi)
            def _(gi=gi, dil=dil):
                _attn_group_fwd(dil, q_ref, k_ref, v_ref, kdp, vdp, od, ld, ogs[gi], lgs[gi])
                og_ref[...] = ogs[gi][...]
                lg_ref[...] = lgs[gi][...]

        @pl.when(p == 2)
        def _():
            w0, w1, w2 = _group_weights(lgs)
            o = w0 * og0[...] + w1 * og1[...] + w2 * og2[...]
            gate = gate_ref[...]
            out_ref[...] = (o * (gate * _sigmoid(gate))).astype(BF16)

    grp = pl.BlockSpec((S, HEAD_DIM), lambda h, p: (0, p * 8 + h))
    return pl.pallas_call(
        body, grid=(8, 3),
        in_specs=[grp, grp, grp, pl.BlockSpec((S, HEAD_DIM), lambda h, p: (0, BG_COL + h)), ANY_SPEC],
        out_specs=[pl.BlockSpec((S, HEAD_DIM), lambda h, p: (0, 8 + h)), grp, grp],
        out_shape=[jax.ShapeDtypeStruct((S, 2048), BF16), jax.ShapeDtypeStruct((S, 3072), F32),
                   jax.ShapeDtypeStruct((S, 3072), F32)],
        scratch_shapes=[_slab(BF16, S + BLK), _slab(BF16, S + BLK)] + [_slab() for _ in range(8)],
        input_output_aliases={4: 0},
        compiler_params=_params(("parallel", "arbitrary"), VMEM_BIG), name="attn_fwd",
    )(*qkv, z0, ycat)


def _attn_bwd(z0, qkv, og, lg, dycat, tabs):
    def body(q_ref, k_ref, v_ref, gate_ref, dy_ref, c_ref, a_ref, b_ref,
             og0_ref, og1_ref, og2_ref, lg0_ref, lg1_ref, lg2_ref,
             dq_ref, dk_ref, dv_ref, dbg_ref,
             tmp, kd, vd, ld, dg0, dg1, dg2, cg0, cg1, cg2, dod, cd, dqd, dkd, dvd):
        p = pl.program_id(1)
        ogs, lgs, dgs, cgs = (og0_ref, og1_ref, og2_ref), (lg0_ref, lg1_ref, lg2_ref), (dg0, dg1, dg2), (cg0, cg1, cg2)

        @pl.when(p == 0)
        def _():
            w = _group_weights(lgs)
            o = w[0] * ogs[0][...] + w[1] * ogs[1][...] + w[2] * ogs[2][...]
            silu, dsilu = _silu_and_grad(gate_ref[...])
            dy = dy_ref[...]
            dbg_ref[...] = (dy * o * dsilu).astype(BF16)
            do = dy * silu
            dwbar = jnp.sum(do * o, axis=1, keepdims=True)
            for gi in range(3):
                dgs[gi][...] = w[gi] * do
                cgs[gi][...] = -w[gi] * dwbar

        for gi, (_, dil) in enumerate(PATTERNS):
            @pl.when(p == 1 + gi)
            def _(gi=gi, dil=dil):
                nb = S // dil // BLK
                qd = q_ref
                c, a, b = c_ref[...], a_ref[...], b_ref[...]
                _pad_copy(kd, k_ref)
                _pad_copy(vd, v_ref)
                _deinterleave(dod, dgs[gi], dil, BF16)
                _deinterleave(ld, lgs[gi], dil)
                _deinterleave(cd, cgs[gi], dil)
                dkd[...] = jnp.zeros_like(dkd)
                dvd[...] = jnp.zeros_like(dvd)
                flat = lambda t: t.reshape(CU * BLK, HEAD_DIM)
                for u0 in range(0, NUNITS, CU):
                    q, s_own, s_prev = _chunk_scores(u0, nb, qd, kd)
                    lse, cv, do = _blocks(ld, u0), _blocks(cd, u0), _blocks(dod, u0)
                    own = slice((u0 + 1) * BLK, (u0 + 1 + CU) * BLK)
                    p_own = jnp.exp(s_own - lse)
                    ds_own = (p_own * (_dot(do, _blocks(vd, u0 + 1), B_QK) + cv) * SCALE).astype(BF16)
                    dq = _dot(ds_own, _blocks(kd, u0 + 1), B_PV)
                    dkd[own, :] += flat(_dot(ds_own, q, B_TN))
                    dvd[own, :] += flat(_dot(p_own.astype(BF16), do, B_TN))
                    if s_prev is not None:
                        prev = slice(u0 * BLK, (u0 + CU) * BLK)
                        p_prev = jnp.exp(s_prev - lse)
                        ds_prev = (p_prev * (_dot(do, _blocks(vd, u0), B_QK) + cv) * SCALE).astype(BF16)
                        dq = dq + _dot(ds_prev, _blocks(kd, u0), B_PV)
                        dkd[prev, :] += flat(_dot(ds_prev, q, B_TN))
                        dvd[prev, :] += flat(_dot(p_prev.astype(BF16), do, B_TN))
                    dqd[u0 * BLK:(u0 + CU) * BLK, :] = flat(dq)
                _interleave(tmp, dqd, dil)
                dq_ref[...] = _rope_t(tmp[...], c, a, b).astype(BF16)
                _interleave(tmp, dkd, dil, BLK)
                dk_ref[...] = _rope_t(tmp[...], c, a, b).astype(BF16)
                _interleave(tmp, dvd, dil, BLK)
                dv_ref[...] = tmp[...].astype(BF16)

    tab = pl.BlockSpec((S, HEAD_DIM), lambda h, p: (0, 0))
    hspec = lambda base: pl.BlockSpec((S, HEAD_DIM), lambda h, p: (0, base + h))
    gspec = pl.BlockSpec((S, HEAD_DIM), lambda h, p: (0, jnp.maximum(p - 1, 0) * 8 + h))
    return pl.pallas_call(
        body, grid=(8, 4),
        in_specs=[gspec, gspec, gspec, hspec(BG_COL), hspec(8), tab, tab, tab,
                  hspec(0), hspec(8), hspec(16), hspec(0), hspec(8), hspec(16)],
        out_specs=[gspec, gspec, gspec, hspec(0)],
        out_shape=[jax.ShapeDtypeStruct((S, 3072), BF16)] * 3 + [jax.ShapeDtypeStruct((S, HALF), BF16)],
        scratch_shapes=[_slab(), _slab(BF16, S + BLK), _slab(BF16, S + BLK), _slab()] + [_slab() for _ in range(6)]
                       + [_slab(BF16), _slab(), _slab(), _slab(F32, S + BLK), _slab(F32, S + BLK)],
        compiler_params=_params(("parallel", "arbitrary"), VMEM_BIG), name="attn_bwd",
    )(*qkv, z0, dycat, *tabs, og, og, og, lg, lg, lg)


SGU_CH = 256
NCHUNK = TR // 128


def _ln_stats(x):
    mu = jnp.mean(x, axis=-1, keepdims=True)
    xc = x - mu
    r = lax.rsqrt(jnp.mean(xc * xc, axis=-1, keepdims=True) + EPS)
    return xc * r, r


def _ln_bwd(dy, xhat, r, g):
    dxh = dy * g
    return r * (dxh - jnp.mean(dxh, axis=-1, keepdims=True) - xhat * jnp.mean(dxh * xhat, axis=-1, keepdims=True))


def _tril_bf16(w):
    row = lax.broadcasted_iota(jnp.int32, w.shape, 0)
    col = lax.broadcasted_iota(jnp.int32, w.shape, 1)
    return jnp.where(row >= col, w, 0.0).astype(BF16)


def _sgu_gate(vn_s, s_s, w_ref, bb_ref):
    for h in range(4):
        wm = _tril_bf16(w_ref[h])
        bias = bb_ref[h]
        for ch in range(NCHUNK):
            rows, cols = slice(ch * 128, (ch + 1) * 128), slice(h * SGU_CH, (h + 1) * SGU_CH)
            s_s[rows, cols] = _dot(wm, vn_s[rows, cols]) + jnp.concatenate([bias, bias], axis=1)


WIN = HALO + TR
SUBL = 8


def _shifted_copies(dst, src):
    dst[0] = src[...]
    for b in range(1, SUBL):
        dst[b, 0:WIN - SUBL, :] = src[pl.ds(b, WIN - SUBL), :]


def _rows_at(copies, off, n):
    return copies[off % SUBL, pl.ds(off - off % SUBL, n), :]


def _conv_fwd(i, dval_ref, dglu_ref, hval_ref, hglu_ref, cw_ref, cb_ref, xw, xr, dcs):
    halo = hval_ref[...] * _sigmoid(hglu_ref[...])
    xw[0:HALO, :] = jnp.where(i > 0, halo, 0.0)
    xw[HALO:HALO + TR, :] = dval_ref[...] * _sigmoid(dglu_ref[...])
    _shifted_copies(xr, xw)
    for rb in range(TR // SUB):
        acc = jnp.broadcast_to(cb_ref[...], (SUB, HALF))
        for k in range(CONV_K):
            acc = acc + cw_ref[k:k + 1, :] * _rows_at(xr, rb * SUB + HALO - (CONV_K - 1) + k, SUB)
        dcs[rb * SUB:(rb + 1) * SUB, :] = acc


def _odd_in_specs():
    col = lambda j: pl.BlockSpec((TR, HALF), lambda i, *_: (i, j))
    prev = lambda j: pl.BlockSpec((HALO, HALF), lambda i, *_: (jnp.maximum(i * (TR // HALO) - 1, 0), j))
    return [col(0), col(1), col(2), col(3), col(4), col(5), prev(3), prev(4)]


def _full_spec(shape):
    return pl.BlockSpec(shape, lambda i, *_: (0,) * len(shape))


def _odd_fwd(z1, sgu_g, sgu_b, sgu_w, sgu_bb, conv_w, conv_b, cn_g, cn_b):
    def body(u_ref, v_ref, cg_ref, dval_ref, dglu_ref, dgate_ref, hval_ref, hglu_ref,
             g_ref, b_ref, w_ref, bb_ref, cw_ref, cb_ref, cng_ref, cnb_ref, out_ref, vn_s, s_s, xw, dcs, xr):
        i = pl.program_id(0)
        vhat, _ = _ln_stats(v_ref[...])
        vn_s[...] = (vhat * g_ref[...] + b_ref[...]).astype(BF16)
        _sgu_gate(vn_s, s_s, w_ref, bb_ref)
        cg = cg_ref[...]
        out_ref[:, 0:HALF] = (u_ref[...] * s_s[...] * (cg * _sigmoid(cg))).astype(BF16)
        _conv_fwd(i, dval_ref, dglu_ref, hval_ref, hglu_ref, cw_ref, cb_ref, xw, xr, dcs)
        dhat, _ = _ln_stats(dcs[...])
        dn = dhat * cng_ref[...] + cnb_ref[...]
        dgate = dgate_ref[...]
        out_ref[:, HALF:2 * HALF] = ((dn * _sigmoid(dn)) * (dgate * _sigmoid(dgate))).astype(BF16)

    vec = _full_spec((1, HALF))
    return pl.pallas_call(
        body, grid=(S // TR,),
        in_specs=_odd_in_specs() + [vec, vec, _full_spec((4, 128, 128)), _full_spec((4, 128, 128)),
                                    _full_spec((HALO, HALF)), vec, vec, vec],
        out_specs=pl.BlockSpec((TR, 2048), lambda i: (i, 0)),
        out_shape=jax.ShapeDtypeStruct((S, 2048), BF16),
        scratch_shapes=[pltpu.VMEM((TR, HALF), BF16), pltpu.VMEM((TR, HALF), F32),
                        pltpu.VMEM((WIN, HALF), F32), pltpu.VMEM((TR, HALF), F32), pltpu.VMEM((SUBL, WIN, HALF), F32)],
        compiler_params=_params(("parallel",), VMEM_BIG), name="odd_fwd",
    )(z1, z1, z1, z1, z1, z1, z1, z1, sgu_g, sgu_b, sgu_w, sgu_bb, conv_w, conv_b, cn_g, cn_b)


def _odd_bwd_a(z1, dycat, sgu_g, sgu_b, sgu_w, sgu_bb, conv_w, conv_b, cn_g, cn_b):
    def body(u_ref, v_ref, cg_ref, dval_ref, dglu_ref, dgate_ref, hval_ref, hglu_ref, dy_ref,
             g_ref, b_ref, w_ref, bb_ref, cw_ref, cb_ref, cng_ref, cnb_ref,
             dz_ref, ddc_ref, dw_ref, dbb_ref, dg_ref, db_ref, dcng_ref, dcnb_ref, dcb_ref,
             vn_s, s_s, xw, dcs, ds_s, dvn_s, xr):
        i = pl.program_id(0)
        vhat, rv = _ln_stats(v_ref[...])
        g = g_ref[...]
        vn_s[...] = (vhat * g + b_ref[...]).astype(BF16)
        _sgu_gate(vn_s, s_s, w_ref, bb_ref)
        silu_c, dsilu_c = _silu_and_grad(cg_ref[...])
        dyc = dy_ref[:, 0:HALF]
        u = u_ref[...]
        s = s_s[...]
        dz_ref[:, 0:HALF] = (dyc * s * silu_c).astype(BF16)
        dz_ref[:, 2 * HALF:3 * HALF] = (dyc * u * s * dsilu_c).astype(BF16)
        ds_s[...] = dyc * u * silu_c

        @pl.when(i == 0)
        def _():
            dw_ref[...] = jnp.zeros_like(dw_ref)
            dbb_ref[...] = jnp.zeros_like(dbb_ref)

        tril = lax.broadcasted_iota(jnp.int32, (128, 128), 0) >= lax.broadcasted_iota(jnp.int32, (128, 128), 1)
        for h in range(4):
            wm = _tril_bf16(w_ref[h])
            for ch in range(NCHUNK):
                rows, cols = slice(ch * 128, (ch + 1) * 128), slice(h * SGU_CH, (h + 1) * SGU_CH)
                ds = ds_s[rows, cols]
                dsb = ds.astype(BF16)
                dw_ref[h] += jnp.where(tril, _dot(dsb, vn_s[rows, cols], NT), 0.0)
                dbb_ref[h] += jnp.broadcast_to(jnp.sum(ds, axis=1, keepdims=True), (128, 128))
                dvn_s[rows, cols] = _dot(wm, dsb, TN)
        dvn = dvn_s[...]
        _acc_rows(dg_ref, dvn * vhat, i)
        _acc_rows(db_ref, dvn, i)
        dz_ref[:, HALF:2 * HALF] = _ln_bwd(dvn, vhat, rv, g).astype(BF16)

        _conv_fwd(i, dval_ref, dglu_ref, hval_ref, hglu_ref, cw_ref, cb_ref, xw, xr, dcs)
        dhat, rd = _ln_stats(dcs[...])
        cng = cng_ref[...]
        silu_n, dsilu_n = _silu_and_grad(dhat * cng + cnb_ref[...])
        silu_g, dsilu_g = _silu_and_grad(dgate_ref[...])
        dyd = dy_ref[:, HALF:2 * HALF]
        dz_ref[:, 5 * HALF:6 * HALF] = (dyd * silu_n * dsilu_g).astype(BF16)
        ddn = dyd * silu_g * dsilu_n
        _acc_rows(dcng_ref, ddn * dhat, i)
        _acc_rows(dcnb_ref, ddn, i)
        ddc = _ln_bwd(ddn, dhat, rd, cng)
        ddc_ref[...] = ddc
        _acc_rows(dcb_ref, ddc, i)

    vec = _full_spec((1, HALF))
    sq = _full_spec((4, 128, 128))
    return pl.pallas_call(
        body, grid=(S // TR,),
        in_specs=_odd_in_specs() + [pl.BlockSpec((TR, 2048), lambda i: (i, 0)),
                                    vec, vec, sq, sq, _full_spec((HALO, HALF)), vec, vec, vec],
        out_specs=[pl.BlockSpec((TR, ODD_IN), lambda i: (i, 0)), pl.BlockSpec((TR, HALF), lambda i: (i, 0)),
                   sq, sq, vec, vec, vec, vec, vec],
        out_shape=[jax.ShapeDtypeStruct((S, ODD_IN), BF16), jax.ShapeDtypeStruct((S, HALF), F32),
                   jax.ShapeDtypeStruct((4, 128, 128), F32), jax.ShapeDtypeStruct((4, 128, 128), F32)]
                  + [jax.ShapeDtypeStruct((1, HALF), F32)] * 5,
        scratch_shapes=[pltpu.VMEM((TR, HALF), BF16), pltpu.VMEM((TR, HALF), F32),
                        pltpu.VMEM((WIN, HALF), F32), pltpu.VMEM((TR, HALF), F32),
                        pltpu.VMEM((TR, HALF), F32), pltpu.VMEM((TR, HALF), F32), pltpu.VMEM((SUBL, WIN, HALF), F32)],
        compiler_params=_params(("arbitrary",), VMEM_BIG), name="odd_bwd_a",
    )(z1, z1, z1, z1, z1, z1, z1, z1, dycat, sgu_g, sgu_b, sgu_w, sgu_bb, conv_w, conv_b, cn_g, cn_b)


def _odd_bwd_b(z1, ddc, dz1, conv_w):
    nt = S // TR

    def body(dval_ref, dglu_ref, hval_ref, hglu_ref, ddc_ref, hddc_ref, cw_ref, dz_in_ref,
             dz_ref, dcw_ref, xw, dwin, dxs, xr, dr):
        del dz_in_ref
        i, j = pl.program_id(0), pl.program_id(1)
        sg = _sigmoid(dglu_ref[...])
        dval = dval_ref[...]

        @pl.when(j == 0)
        def _():
            halo = hval_ref[...] * _sigmoid(hglu_ref[...])
            xw[0:HALO, :] = jnp.where(i > 0, halo, 0.0)
            xw[HALO:HALO + TR, :] = dval * sg
            dwin[0:TR, :] = ddc_ref[...]
            dwin[TR:TR + HALO, :] = jnp.where(i < nt - 1, hddc_ref[...], 0.0)
            _shifted_copies(xr, xw)
            _shifted_copies(dr, dwin)

            @pl.when(i == 0)
            def _():
                dcw_ref[...] = jnp.zeros_like(dcw_ref)

            for rb in range(TR // SUB):
                acc = jnp.zeros((SUB, HALF), F32)
                for k in range(CONV_K):
                    acc = acc + cw_ref[k:k + 1, :] * _rows_at(dr, rb * SUB + (CONV_K - 1) - k, SUB)
                dxs[rb * SUB:(rb + 1) * SUB, :] = acc
            for k in range(CONV_K):
                acc = jnp.zeros((SUB, HALF), F32)
                for rb in range(TR // SUB):
                    acc = acc + dwin[rb * SUB:(rb + 1) * SUB, :] * _rows_at(xr, rb * SUB + HALO - (CONV_K - 1) + k, SUB)
                dcw_ref[k:k + 1, :] += jnp.sum(acc, axis=0, keepdims=True)
            dz_ref[...] = (dxs[...] * sg).astype(BF16)

        @pl.when(j == 1)
        def _():
            dz_ref[...] = (dxs[...] * dval * sg * (1.0 - sg)).astype(BF16)

    col = lambda c: pl.BlockSpec((TR, HALF), lambda i, j: (i, c))
    prev = lambda c: pl.BlockSpec((HALO, HALF), lambda i, j: (jnp.maximum(i * (TR // HALO) - 1, 0), c))
    nxt = pl.BlockSpec((HALO, HALF), lambda i, j: (jnp.minimum((i + 1) * (TR // HALO), S // HALO - 1), 0))
    return pl.pallas_call(
        body, grid=(nt, 2),
        in_specs=[col(3), col(4), prev(3), prev(4), pl.BlockSpec((TR, HALF), lambda i, j: (i, 0)), nxt,
                  _full_spec((HALO, HALF)), pl.BlockSpec(memory_space=pl.ANY)],
        out_specs=[pl.BlockSpec((TR, HALF), lambda i, j: (i, 3 + j)), _full_spec((HALO, HALF))],
        out_shape=[jax.ShapeDtypeStruct((S, ODD_IN), BF16), jax.ShapeDtypeStruct((HALO, HALF), F32)],
        scratch_shapes=[pltpu.VMEM((WIN, HALF), F32), pltpu.VMEM((WIN, HALF), F32), pltpu.VMEM((TR, HALF), F32),
                        pltpu.VMEM((SUBL, WIN, HALF), F32), pltpu.VMEM((SUBL, WIN, HALF), F32)],
        input_output_aliases={7: 0},
        compiler_params=_params(("arbitrary", "arbitrary"), VMEM_BIG), name="odd_bwd_b",
    )(z1, z1, z1, z1, ddc, ddc, conv_w, dz1)


def _cast_bf16(w, name):
    r, c = w.shape
    tr = min(r, 256)
    def body(i_ref, o_ref):
        o_ref[...] = i_ref[...].astype(BF16)

    return pl.pallas_call(
        body, grid=(r // tr,), in_specs=[pl.BlockSpec((tr, c), lambda i: (i, 0))],
        out_specs=pl.BlockSpec((tr, c), lambda i: (i, 0)), out_shape=jax.ShapeDtypeStruct((r, c), BF16),
        compiler_params=_params(("parallel",)), name=name,
    )(w)


def _adamw(w, g, m, v):
    m = ADAM_B1 * m + (1.0 - ADAM_B1) * g
    v = ADAM_B2 * v + (1.0 - ADAM_B2) * (g * g)
    m_hat = m / (1.0 - ADAM_B1 ** ADAM_STEP)
    v_hat = v / (1.0 - ADAM_B2 ** ADAM_STEP)
    delta = -ADAM_LR * (m_hat / (jnp.sqrt(v_hat) + ADAM_EPS) + ADAM_WD * w)
    return delta, m, v


def _adam_reduce(parts, w, m, v, name, dep=None):
    r, c = w.shape
    tr = min(r, 128)
    deps = [] if dep is None else [dep]
    pieces = list(parts) if isinstance(parts, (list, tuple)) else [parts]
    npieces, nparts = len(pieces), pieces[0].shape[0]

    def body(*refs):
        p_refs = refs[:npieces]
        w_ref, m_ref, v_ref = refs[npieces:npieces + 3]
        g_ref, d_ref, nm_ref, nv_ref = refs[npieces + 3 + len(deps):]
        cols = []
        for p_ref in p_refs:
            acc = p_ref[0].astype(F32)
            for d in range(1, nparts):
                acc = acc + p_ref[d].astype(F32)
            cols.append(acc)
        g = cols[0] if npieces == 1 else jnp.concatenate(cols, axis=1)
        g_ref[...] = g
        d_ref[...], nm_ref[...], nv_ref[...] = _adamw(w_ref[...], g, m_ref[...], v_ref[...])

    spec = pl.BlockSpec((tr, c), lambda i: (i, 0))
    return pl.pallas_call(
        body, grid=(r // tr,),
        in_specs=[pl.BlockSpec((nparts, tr, a.shape[2]), lambda i: (0, i, 0)) for a in pieces] + [spec, spec, spec]
                 + [ANY_SPEC] * len(deps),
        out_specs=[spec] * 4, out_shape=[jax.ShapeDtypeStruct((r, c), F32)] * 4,
        compiler_params=_params(("parallel",), VMEM_BIG), name=name,
    )(*pieces, w, m, v, *deps)


def _sum_parts(parts, name, dep=None):
    r = parts.shape[1]
    tr = 8
    for cand in (512, 256, 128, 64, 32, 16, 8):
        if r % cand == 0:
            tr = cand
            break
    deps = [] if dep is None else [dep]

    def body(p_ref, *rest):
        g = p_ref[0]
        for d in range(1, NDEV):
            g = g + p_ref[d]
        rest[-1][...] = g

    return pl.pallas_call(
        body, grid=(r // tr,), in_specs=[pl.BlockSpec((NDEV, tr, 128), lambda i: (0, i, 0))] + [ANY_SPEC] * len(deps),
        out_specs=pl.BlockSpec((tr, 128), lambda i: (i, 0)), out_shape=jax.ShapeDtypeStruct((r, 128), F32),
        compiler_params=_params(("parallel",)), name=name,
    )(parts, *deps)


def _sum_unpack(parts, rows, name, dep=None):
    deps = [] if dep is None else [dep]

    def body(p_ref, *outs):
        outs = outs[len(deps):]
        off = 0
        for o_ref, n in zip(outs, rows):
            acc = p_ref[0, off:off + n, :]
            for d in range(1, NDEV):
                acc = acc + p_ref[d, off:off + n, :]
            o_ref[...] = acc
            off += n

    return pl.pallas_call(
        body, grid=(1,), in_specs=[pl.BlockSpec(parts.shape, lambda i: (0, 0, 0))] + [ANY_SPEC] * len(deps),
        out_specs=[pl.BlockSpec((n, 128), lambda i: (0, 0)) for n in rows],
        out_shape=[jax.ShapeDtypeStruct((n, 128), F32) for n in rows],
        compiler_params=_params(("arbitrary",), VMEM_BIG), name=name,
    )(parts, *deps)


def _adam_small(ws, gs, g_specs, ms, vs, name):
    n = len(ws)

    def body(*refs):
        w_r, g_r, m_r, v_r = refs[:n], refs[n:2 * n], refs[2 * n:3 * n], refs[3 * n:4 * n]
        outs = refs[4 * n:]
        for i in range(n):
            g = g_r[i][...]
            outs[4 * i][...] = g
            outs[4 * i + 1][...], outs[4 * i + 2][...], outs[4 * i + 3][...] = _adamw(
                w_r[i][...], g, m_r[i][...], v_r[i][...])

    whole = lambda a: pl.BlockSpec(a.shape, lambda i, nd=a.ndim: (0,) * nd)
    outs = pl.pallas_call(
        body, grid=(1,),
        in_specs=[whole(a) for a in ws] + list(g_specs) + [whole(a) for a in ms] + [whole(a) for a in vs],
        out_specs=[whole(a) for a in ws for _ in range(4)],
        out_shape=[jax.ShapeDtypeStruct(a.shape, F32) for a in ws for _ in range(4)],
        compiler_params=_params(("arbitrary",), VMEM_BIG), name=name,
    )(*ws, *gs, *ms, *vs)
    return [outs[4 * i:4 * i + 4] for i in range(n)]


MASKS = [(mx, my, mc) for mx in (0, 1) for my in (0, 1) for mc in (0, 1)][1:]


def _exchange(arrays, scatter, name):
    nt = len(arrays)
    out_shape = [jax.ShapeDtypeStruct(((NDEV,) + a.shape) if not scatter else a.shape, a.dtype) for a in arrays]

    def body(*refs):
        ins, outs = refs[:nt], refs[nt:2 * nt]
        send_sems, recv_sems, local_sems = refs[2 * nt:]
        x, y, c = lax.axis_index("x"), lax.axis_index("y"), lax.axis_index("c")
        me = 4 * x + 2 * y + c
        copies = []
        for t in range(nt):
            src_own = ins[t].at[me] if scatter else ins[t]
            loc = pltpu.make_async_copy(src_own, outs[t].at[me], local_sems.at[t])
            loc.start()
            copies.append(loc)
            for k, (mx, my, mc) in enumerate(MASKS):
                px, py, pc = (x + mx) % 2, (y + my) % 2, (c + mc) % 2
                peer = 4 * px + 2 * py + pc
                src = ins[t].at[peer] if scatter else ins[t]
                rc = pltpu.make_async_remote_copy(
                    src_ref=src, dst_ref=outs[t].at[me], send_sem=send_sems.at[t, k], recv_sem=recv_sems.at[t, k],
                    device_id=(px, py, pc), device_id_type=MESH)
                rc.start()
                copies.append(rc)
        for cp in copies:
            cp.wait()

    hbm = pl.BlockSpec(memory_space=pl.ANY)
    return pl.pallas_call(
        body, in_specs=[hbm] * nt, out_specs=[hbm] * nt, out_shape=out_shape,
        scratch_shapes=[pltpu.SemaphoreType.DMA((nt, 7)), pltpu.SemaphoreType.DMA((nt, 7)),
                        pltpu.SemaphoreType.DMA((nt,))],
        name=name,
    )(*arrays)


SEM_SPEC = pl.BlockSpec(memory_space=pltpu.SEMAPHORE)
EFFECT = pltpu.SideEffectType.DATAFLOW_SIDE_EFFECTING


def _direct_plan(scatter):
    def plan(x, y, c, srcs, lands):
        me = 4 * x + 2 * y + c
        local, remote = [], []
        for src, land in zip(srcs, lands):
            local.append((src.at[me] if scatter else src, land.at[me]))
            for mx, my, mc in MASKS:
                px, py, pc = (x + mx) % 2, (y + my) % 2, (c + mc) % 2
                blk = src.at[4 * px + 2 * py + pc] if scatter else src
                remote.append((blk, land.at[me], (px, py, pc)))
        return local, remote
    return plan


def _split_start(name, srcs, land_shapes, plan, n_local, n_remote, dep=None):
    ns, nl = len(srcs), len(land_shapes)
    deps = [] if dep is None else [dep]
    lands = [lax.empty(s.shape, s.dtype) for s in land_shapes]

    def body(*refs):
        ins, lz = refs[:ns], refs[ns:ns + nl]
        outs = refs[ns + nl + len(deps):]
        send_sems, recv_sems, token, local_sems = outs[0], outs[1], outs[2 + ns + nl], outs[3 + ns + nl]
        local, remote = plan(lax.axis_index("x"), lax.axis_index("y"), lax.axis_index("c"), ins, lz)
        own = [pltpu.make_async_copy(src, dst, local_sems.at[i]) for i, (src, dst) in enumerate(local)]
        for cp in own:
            cp.start()
        for cp in own:
            cp.wait()
        for k, (src, dst, peer) in enumerate(remote):
            pltpu.make_async_remote_copy(src_ref=src, dst_ref=dst, send_sem=send_sems.at[k], recv_sem=recv_sems.at[k],
                                         device_id=peer, device_id_type=MESH).start()
        token[...] = jnp.zeros_like(token)

    hbm = lambda a: pltpu.HBM(a.shape, a.dtype)
    outs = pl.pallas_call(
        body, name=name,
        out_shape=(pltpu.SemaphoreType.DMA((n_remote,)), pltpu.SemaphoreType.DMA((n_remote,)),
                   *[hbm(a) for a in srcs], *[hbm(a) for a in lands], jax.ShapeDtypeStruct((8, 128), F32)),
        in_specs=[ANY_SPEC] * (ns + nl + len(deps)),
        out_specs=(SEM_SPEC, SEM_SPEC, *[ANY_SPEC] * (ns + nl), pl.BlockSpec(memory_space=pltpu.VMEM)),
        scratch_shapes=[pltpu.SemaphoreType.DMA((n_local,))],
        input_output_aliases={i: 2 + i for i in range(ns + nl)},
        compiler_params=pltpu.CompilerParams(has_side_effects=EFFECT),
    )(*[pltpu.with_memory_space_constraint(a, pltpu.HBM) for a in srcs],
      *[pltpu.with_memory_space_constraint(a, pltpu.HBM) for a in lands], *deps)
    return dict(sems=outs[:2], srcs=outs[2:2 + ns], lands=outs[2 + ns:2 + ns + nl], token=outs[-1],
                plan=plan, n_remote=n_remote)


def _split_wait(name, handle, after):
    srcs, lands, plan = handle["srcs"], handle["lands"], handle["plan"]
    ns, nl = len(srcs), len(lands)

    def body(*refs):
        ins, lz = refs[:ns], refs[ns:ns + nl]
        send_sems, recv_sems = refs[ns + nl], refs[ns + nl + 1]
        _, remote = plan(lax.axis_index("x"), lax.axis_index("y"), lax.axis_index("c"), ins, lz)
        for k, (src, dst, peer) in enumerate(remote):
            cp = pltpu.make_async_remote_copy(src_ref=src, dst_ref=dst, send_sem=send_sems.at[k],
                                              recv_sem=recv_sems.at[k], device_id=peer, device_id_type=MESH)
            cp.wait_send()
            cp.wait_recv()

    hbm = lambda a: pltpu.HBM(a.shape, a.dtype)
    outs = pl.pallas_call(
        body, name=name, out_shape=(*[hbm(a) for a in srcs], *[hbm(a) for a in lands]),
        in_specs=[ANY_SPEC] * (ns + nl) + [SEM_SPEC, SEM_SPEC, ANY_SPEC], out_specs=tuple([ANY_SPEC] * (ns + nl)),
        input_output_aliases={i: i for i in range(ns + nl)},
        compiler_params=pltpu.CompilerParams(has_side_effects=EFFECT),
    )(*srcs, *lands, *handle["sems"], after)
    return list(outs[ns:])


def _sc_exchange(name, collective_id, arrays, scatter):
    nt = len(arrays)
    out_type = [jax.ShapeDtypeStruct(a.shape if scatter else (NDEV,) + a.shape, a.dtype) for a in arrays]

    def body(*refs):
        ins, outs = refs[:nt], refs[nt:2 * nt]
        send_sems, recv_sems, local_sems = refs[2 * nt:3 * nt], refs[3 * nt:4 * nt], refs[4 * nt:5 * nt]
        x, y, c = lax.axis_index("x"), lax.axis_index("y"), lax.axis_index("c")
        peers = [(mx + x - 2 * mx * x, my + y - 2 * my * y, mc + c - 2 * mc * c) for mx, my, mc in MASKS]
        barrier = pltpu.get_barrier_semaphore()
        for peer in peers:
            pl.semaphore_signal(barrier, inc=1, device_id=peer, device_id_type=MESH)
        pl.semaphore_wait(barrier, len(peers))
        me = 4 * x + 2 * y + c
        own = []
        for t in range(nt):
            cp = pltpu.make_async_copy(ins[t].at[me] if scatter else ins[t], outs[t].at[me], local_sems[t])
            cp.start()
            own.append(cp)
            for px, py, pc in peers:
                src = ins[t].at[4 * px + 2 * py + pc] if scatter else ins[t]
                pltpu.make_async_remote_copy(src_ref=src, dst_ref=outs[t].at[me], send_sem=send_sems[t],
                                             recv_sem=recv_sems[t], device_id=(px, py, pc), device_id_type=MESH).start()
        for t in range(nt):
            own[t].wait()
            seven = outs[t].at[pl.ds(0, NDEV - 1)]
            drain = pltpu.make_async_remote_copy(src_ref=seven, dst_ref=seven, send_sem=send_sems[t],
                                                 recv_sem=recv_sems[t], device_id=(x, y, c), device_id_type=MESH)
            drain.wait_send()
            drain.wait_recv()

    return pl.kernel(
        body, out_type=out_type, mesh=plsc.ScalarSubcoreMesh(axis_name="sequencer", num_cores=1),
        scratch_types=[pltpu.SemaphoreType.DMA] * (3 * nt),
        compiler_params=pltpu.CompilerParams(collective_id=collective_id), name=name,
    )(*arrays)


def _sc_gather_two_level(name, collective_id, arrays):
    nt = len(arrays)
    out_type = [jax.ShapeDtypeStruct((NDEV,) + a.shape, a.dtype) for a in arrays]

    def body(*refs):
        ins, outs = refs[:nt], refs[nt:2 * nt]
        sems = refs[2 * nt:]
        send_sems, sib_sems, local_sems = sems[:nt], sems[nt:2 * nt], sems[2 * nt:3 * nt]
        ici_sems = [sems[3 * nt + 3 * t:3 * nt + 3 * t + 3] for t in range(nt)]
        x, y, c = lax.axis_index("x"), lax.axis_index("y"), lax.axis_index("c")
        sibling = (x, y, 1 - c)
        chips = [(1 - x, y), (x, 1 - y), (1 - x, 1 - y)]
        barrier = pltpu.get_barrier_semaphore()
        for peer in [sibling] + [(cx, cy, c) for cx, cy in chips]:
            pl.semaphore_signal(barrier, inc=1, device_id=peer, device_id_type=MESH)
        pl.semaphore_wait(barrier, 4)
        me = 4 * x + 2 * y + c

        def push(t, src, slot, recv_sem, to):
            pltpu.make_async_remote_copy(src_ref=src, dst_ref=outs[t].at[slot], send_sem=send_sems[t],
                                         recv_sem=recv_sem, device_id=to, device_id_type=MESH).start()

        own = []
        for t in range(nt):
            cp = pltpu.make_async_copy(ins[t], outs[t].at[me], local_sems[t])
            cp.start()
            own.append(cp)
            for j, (cx, cy) in enumerate(chips):
                push(t, ins[t], me, ici_sems[t][j], (cx, cy, c))
            push(t, ins[t], me, sib_sems[t], sibling)
        for t in range(nt):
            for j, (cx, cy) in enumerate(chips):
                slot = 4 * cx + 2 * cy + c
                landed = outs[t].at[slot]
                pltpu.make_async_remote_copy(src_ref=landed, dst_ref=landed, send_sem=send_sems[t],
                                             recv_sem=ici_sems[t][j], device_id=(cx, cy, c),
                                             device_id_type=MESH).wait_recv()
                push(t, landed, slot, sib_sems[t], sibling)
        for t in range(nt):
            own[t].wait()
            four, seven = outs[t].at[pl.ds(0, 4)], outs[t].at[pl.ds(0, 7)]
            pltpu.make_async_remote_copy(src_ref=four, dst_ref=four, send_sem=send_sems[t], recv_sem=sib_sems[t],
                                         device_id=sibling, device_id_type=MESH).wait_recv()
            pltpu.make_async_remote_copy(src_ref=seven, dst_ref=seven, send_sem=send_sems[t], recv_sem=sib_sems[t],
                                         device_id=sibling, device_id_type=MESH).wait_send()

    return pl.kernel(
        body, out_type=out_type, mesh=plsc.ScalarSubcoreMesh(axis_name="sequencer", num_cores=1),
        scratch_types=[pltpu.SemaphoreType.DMA] * (6 * nt),
        compiler_params=pltpu.CompilerParams(collective_id=collective_id), name=name,
    )(*arrays)


def _sc_sibling_exchange(name, collective_id, src, out_shape, pieces):
    def body(src_ref, out_ref, send_sem, recv_sem):
        x, y, c = lax.axis_index("x"), lax.axis_index("y"), lax.axis_index("c")
        sibling = (x, y, 1 - c)
        barrier = pltpu.get_barrier_semaphore()
        pl.semaphore_signal(barrier, inc=1, device_id=sibling, device_id_type=MESH)
        pl.semaphore_wait(barrier, 1)
        for piece, lands in pieces(c, src_ref, out_ref):
            pltpu.make_async_remote_copy(src_ref=piece, dst_ref=lands, send_sem=send_sem, recv_sem=recv_sem,
                                         device_id=sibling, device_id_type=MESH).start()
        drain = pltpu.make_async_remote_copy(src_ref=out_ref, dst_ref=out_ref, send_sem=send_sem, recv_sem=recv_sem,
                                             device_id=sibling, device_id_type=MESH)
        drain.wait_send()
        drain.wait_recv()

    return pl.kernel(
        body, out_type=jax.ShapeDtypeStruct(out_shape, src.dtype),
        mesh=plsc.ScalarSubcoreMesh(axis_name="sequencer", num_cores=1), scratch_types=[pltpu.SemaphoreType.DMA] * 2,
        compiler_params=pltpu.CompilerParams(collective_id=collective_id), name=name,
    )(src)


def _swap_class_columns(name, collective_id, dz, nb):
    return _sc_sibling_exchange(
        name, collective_id, dz, (S, 4 * nb),
        lambda c, src, out: [(src.at[:, pl.ds((2 * j + 1 - c) * nb, nb)], out.at[:, pl.ds(j * nb, nb)])
                             for j in range(4)])


def _sc_chip_scatter(name, collective_id, q):
    def body(q_ref, out_ref, send_sem, recv_sem, local_sem):
        x, y, c = lax.axis_index("x"), lax.axis_index("y"), lax.axis_index("c")
        chips = [(1 - x, y), (x, 1 - y), (1 - x, 1 - y)]
        barrier = pltpu.get_barrier_semaphore()
        for cx, cy in chips:
            pl.semaphore_signal(barrier, inc=1, device_id=(cx, cy, c), device_id_type=MESH)
        pl.semaphore_wait(barrier, 3)
        mine = 2 * x + y
        own = pltpu.make_async_copy(q_ref.at[mine], out_ref.at[mine], local_sem)
        own.start()
        for cx, cy in chips:
            pltpu.make_async_remote_copy(src_ref=q_ref.at[2 * cx + cy], dst_ref=out_ref.at[mine], send_sem=send_sem,
                                         recv_sem=recv_sem, device_id=(cx, cy, c), device_id_type=MESH).start()
        own.wait()
        three = out_ref.at[pl.ds(0, 3)]
        drain = pltpu.make_async_remote_copy(src_ref=three, dst_ref=three, send_sem=send_sem, recv_sem=recv_sem,
                                             device_id=(x, y, c), device_id_type=MESH)
        drain.wait_send()
        drain.wait_recv()

    return pl.kernel(
        body, out_type=jax.ShapeDtypeStruct(q.shape, q.dtype),
        mesh=plsc.ScalarSubcoreMesh(axis_name="sequencer", num_cores=1), scratch_types=[pltpu.SemaphoreType.DMA] * 3,
        compiler_params=pltpu.CompilerParams(collective_id=collective_id), name=name,
    )(q)


def _mm_pair_dw(h_own, dz, h_sib, dz_sib, nb, name, dep=None):
    tn = 512 if nb % 512 == 0 else nb
    per = nb // tn
    o_spec = pl.BlockSpec((None, D, tn), lambda i, j, k: (j // per, 0, j % per))
    part = _matmul(
        h_own, dz, dn=TN, grid=(1, 4 * per, 1),
        a_spec=pl.BlockSpec((S, D), lambda i, j, k: (0, 0)),
        b_spec=pl.BlockSpec((S, tn), lambda i, j, k: (0, (2 * (j // per) + lax.axis_index("c")) * per + j % per)),
        o_spec=o_spec, out_shape=(4, D, nb), out_dtype=F32, acc_shape=(D, tn), name=name + "_own", dep=dep)

    def body(a_ref, b_ref, p_ref, o_ref):
        o_ref[...] = (p_ref[...] + _dot(a_ref[...], b_ref[...], TN)).astype(BF16)

    return pl.pallas_call(
        body, grid=(1, 4 * per, 1),
        in_specs=[pl.BlockSpec((S, D), lambda i, j, k: (0, 0)), pl.BlockSpec((S, tn), lambda i, j, k: (0, j)), o_spec],
        out_specs=o_spec, out_shape=jax.ShapeDtypeStruct((4, D, nb), BF16),
        compiler_params=_params(("parallel", "parallel", "arbitrary"), VMEM_BIG), name=name + "_sibling",
    )(h_sib, dz_sib, part)


SMALL = {
    "e_pre_norm": ((2048,), None), "e_pool_w": ((4, 256, 256), 1), "e_pool_scale": ((1024,), None),
    "e_post_norm": ((2048,), None), "o_pre_norm": ((2048,), 0), "o_sgu_norm_g": ((1024,), 0),
    "o_sgu_norm_b": ((1024,), 0), "o_sgu_w": ((4, 128, 128), None), "o_sgu_b": ((4, 128), None),
    "o_conv_w": ((31, 1024), 1), "o_conv_b": ((1024,), 0), "o_conv_norm_g": ((1024,), 0),
    "o_conv_norm_b": ((1024,), 0), "o_post_norm": ((2048,), 0),
}
SMALL_SHARDED = [n for n, (_, ax) in SMALL.items() if ax is not None]


def _shard_shape(name):
    shape, ax = SMALL[name]
    if ax is None:
        return shape
    return tuple(s // NDEV if i == ax else s for i, s in enumerate(shape))


def _pack(arrs, row_multiple=1):
    flat = jnp.concatenate([a.reshape(-1) for a in arrs])
    pad = -flat.shape[0] % (128 * row_multiple)
    return jnp.concatenate([flat, jnp.zeros((pad,), F32)]).reshape(-1, 128)


def _small_views(name):
    shape, ax = SMALL[name]
    me = lambda: 4 * lax.axis_index("x") + 2 * lax.axis_index("y") + lax.axis_index("c")
    if ax is None:
        view = (int(np.prod(shape)) // 128, 128)
        return view, view, pl.BlockSpec(view, lambda i: (0, 0))
    if len(shape) == 1:
        n = shape[0] // NDEV
        return (1, n), (NDEV, 1, n), pl.BlockSpec((None, 1, n), lambda i: (me(), 0, 0))
    part = _shard_shape(name)
    return part, shape, pl.BlockSpec(part, lambda i: tuple(me() if d == ax else 0 for d in range(len(shape))))


BIG = ("e_w_in", "e_w_out", "o_w_in", "o_w_out")
WEIGHTS = ["e_pre_norm", "e_w_in", "e_pool_w", "e_pool_scale", "e_w_out", "e_post_norm", "o_pre_norm", "o_w_in",
           "o_sgu_norm_g", "o_sgu_norm_b", "o_sgu_w", "o_sgu_b", "o_conv_w", "o_conv_b", "o_conv_norm_g",
           "o_conv_norm_b", "o_w_out", "o_post_norm"]


def kernel(x, e_pre_norm, e_w_in, e_pool_w, e_pool_scale, e_w_out, e_post_norm, o_pre_norm, o_w_in, o_sgu_norm_g, o_sgu_norm_b, o_sgu_w, o_sgu_b, o_conv_w, o_conv_b, o_conv_norm_g, o_conv_norm_b, o_w_out, o_post_norm, loss_target, m_e_pre_norm, m_e_w_in, m_e_pool_w, m_e_pool_scale, m_e_w_out, m_e_post_norm, m_o_pre_norm, m_o_w_in, m_o_sgu_norm_g, m_o_sgu_norm_b, m_o_sgu_w, m_o_sgu_b, m_o_conv_w, m_o_conv_b, m_o_conv_norm_g, m_o_conv_norm_b, m_o_w_out, m_o_post_norm, v_e_pre_norm, v_e_w_in, v_e_pool_w, v_e_pool_scale, v_e_w_out, v_e_post_norm, v_o_pre_norm, v_o_w_in, v_o_sgu_norm_g, v_o_sgu_norm_b, v_o_sgu_w, v_o_sgu_b, v_o_conv_w, v_o_conv_b, v_o_conv_norm_g, v_o_conv_norm_b, v_o_w_out, v_o_post_norm):
    given = dict(locals())
    w = {n: given[n][0] for n in WEIGHTS}
    m = {n: given["m_" + n][0] for n in WEIGHTS}
    v = {n: given["v_" + n][0] for n in WEIGHTS}
    me = 4 * lax.axis_index("x") + 2 * lax.axis_index("y") + lax.axis_index("c")
    x, target = x[0], loss_target[0]
    row = lambda a: a.reshape(1, -1)

    wg_e_in, small_rows = _sc_gather_two_level(
        "gather_a", 0, [_cast_bf16(w["e_w_in"], "cast_e_w_in"), _pack([w[n] for n in SMALL_SHARDED])])
    h0 = _pre0_fwd(x, row(w["e_pre_norm"]))
    wg_e_out, wg_o_in, wg_o_out = _sc_gather_two_level(
        "gather_b", 1, [_cast_bf16(w[n], "cast_" + n) for n in ("e_w_out", "o_w_in", "o_w_out")])
    h0_sib = _sc_sibling_exchange("swap_h0", 8, h0, h0.shape, lambda c, src, out: [(src, out)])
    p = {n: w[n] for n in SMALL if SMALL[n][1] is None}
    small_rows = small_rows.reshape(NDEV, -1)
    off = 0
    for n in SMALL_SHARDED:
        shp, ax = _shard_shape(n), SMALL[n][1]
        cnt = int(np.prod(shp))
        blk = small_rows[:, off:off + cnt].reshape((NDEV,) + shp)
        p[n] = jnp.moveaxis(blk, 0, ax).reshape(SMALL[n][0])
        off += cnt
    tabs = _rope_tables()
    pool_w_bf = p["e_pool_w"].astype(BF16)
    sgu_bb = jnp.broadcast_to(p["o_sgu_b"][:, :, None], (4, 128, 128))
    conv_w = jnp.concatenate([p["o_conv_w"], jnp.zeros((HALO - CONV_K, HALF), F32)], axis=0)
    odd_p = (row(p["o_sgu_norm_g"]), row(p["o_sgu_norm_b"]), p["o_sgu_w"], sgu_bb, conv_w,
             row(p["o_conv_b"]), row(p["o_conv_norm_g"]), row(p["o_conv_norm_b"]))

    z0 = _mm_in(h0, wg_e_in, "mm_z0")
    ycat0 = _pool_fwd(z0, pool_w_bf, row(p["e_pool_scale"]))
    qkv = _qkv_prep(z0, tabs)
    ycat0, og, lg = _attn_fwd(z0, qkv, ycat0)
    w_out_e, w_out_o = wg_e_out.reshape(2048, D), wg_o_out.reshape(2048, D)
    y0 = _mm_out(ycat0, w_out_e, "mm_y0", h0_sib)
    x1, h1 = _post0_fwd(x, y0, row(p["e_post_norm"]), row(p["o_pre_norm"]))
    h1_sib = _sc_sibling_exchange("swap_h1", 11, h1, h1.shape, lambda c, src, out: [(src, out)])
    z1 = _mm_in(h1, wg_o_in, "mm_z1")
    ycat1 = _odd_fwd(z1, *odd_p)
    y1 = _mm_out(ycat1, w_out_o, "mm_y1", h1_sib)

    g = {}
    loss, dx2, dy1, g["o_post_norm"] = _post1_bwd(y1, x1, target, row(p["o_post_norm"]))
    loss = lax.psum(loss[0, 0], ("x", "y", "c"))
    parts = {}
    dw = _mm_out_dw(ycat1, dy1, "mm_dwout1").reshape(NDEV, 256, D)
    parts["o_w_out"], = _sc_exchange("scatter_o_w_out", 2, [dw], True)
    dycat1 = _mm_out_dx(dy1, w_out_o, "mm_dycat1", (dw, loss.reshape(1, 1)))
    dz1, ddc, g["o_sgu_w"], d_sgu_bb, g["o_sgu_norm_g"], g["o_sgu_norm_b"], g["o_conv_norm_g"], \
        g["o_conv_norm_b"], g["o_conv_b"] = _odd_bwd_a(z1, dycat1, *odd_p)
    dz1, d_conv_w = _odd_bwd_b(z1, ddc, dz1, conv_w)
    g["o_sgu_b"] = d_sgu_bb[:, :, 0]
    g["o_conv_w"] = d_conv_w[:CONV_K]
    grads, deltas, new_m, new_v = {}, {}, {}, {}

    def adam(n, dep):
        grads[n], deltas[n], new_m[n], new_v[n] = _adam_reduce(parts[n], w[n], m[n], v[n], "adam_" + n, dep)
        return new_v[n]

    pin = adam("o_w_out", d_conv_w)
    dz1_sib = _swap_class_columns("swap_dz1", 10, dz1, ODD_IN // NDEV)
    dw = _mm_pair_dw(h1, dz1, h1_sib, dz1_sib, ODD_IN // NDEV, "mm_dwin1", pin)
    parts["o_w_in"] = _sc_chip_scatter("scatter_o_w_in", 3, dw)
    dh1 = _mm_in_dx(dz1, wg_o_in, "mm_dh1", dw)
    dx1, dy0, g["o_pre_norm"], g["e_post_norm"] = _mid_bwd(dx2, dh1, x1, y0, row(p["o_pre_norm"]),
                                                           row(p["e_post_norm"]))
    dw = _mm_out_dw(ycat0, dy0, "mm_dwout0").reshape(NDEV, 256, D)
    parts["e_w_out"], = _sc_exchange("scatter_e_w_out", 4, [dw], True)
    dycat0 = _mm_out_dx(dy0, w_out_e, "mm_dycat0", dw)
    da_in, da_gate, g["e_pool_w"], g["e_pool_scale"] = _pool_bwd(z0, dycat0, pool_w_bf, row(p["e_pool_scale"]))
    late = [n for n in SMALL if n not in ("e_pre_norm", "o_sgu_b")] + ["o_sgu_b"]
    recv_small, = _sc_gather_two_level("gather_small_grads", 6,
                                       [_pack([g[n].reshape(SMALL[n][0]) for n in late], 512)])
    dq, dk, dv, dbg = _attn_bwd(z0, qkv, og, lg, dycat0, tabs)
    dz0 = jnp.concatenate([da_in, da_gate, dq, dk, dv, dbg], axis=1)
    pin = adam("e_w_out", adam("o_w_in", dz0))
    rows = [int(np.prod(SMALL[n][0])) // 128 for n in late]
    summed = dict(zip(late, _sum_unpack(recv_small, rows, "sum_small_grads", pin)))
    dz0_sib = _swap_class_columns("swap_dz0", 9, dz0, EVEN_IN // NDEV)
    dw = _mm_pair_dw(h0, dz0, h0_sib, dz0_sib, EVEN_IN // NDEV, "mm_dwin0", summed[late[0]])
    parts["e_w_in"] = _sc_chip_scatter("scatter_e_w_in", 5, dw)
    dh0 = _mm_in_dx(dz0, wg_e_in, "mm_dh0", dw)
    grad_x, g["e_pre_norm"] = _pre0_bwd(dx1, dh0, x, row(p["e_pre_norm"]))
    last, = _sc_exchange("gather_e_pre_norm_grad", 7, [g["e_pre_norm"].reshape(16, 128)], False)

    pin = adam("e_w_in", grad_x)
    summed["e_pre_norm"] = _sum_parts(last, "sum_e_pre_norm_grad", pin)
    names = list(SMALL)
    views = [_small_views(n) for n in names]
    mine = lambda src: [src[n].reshape(vw[0]) for n, vw in zip(names, views)]
    res = _adam_small(mine(w), [summed[n].reshape(vw[1]) for n, vw in zip(names, views)], [vw[2] for vw in views],
                      mine(m), mine(v), "adam_small")
    for n, out in zip(names, res):
        grads[n], deltas[n], new_m[n], new_v[n] = [t.reshape(_shard_shape(n)) for t in out]

    lead = lambda a: a[None]
    return (loss, grad_x[None], *[lead(grads[n]) for n in WEIGHTS], *[lead(deltas[n]) for n in WEIGHTS],
            *[lead(new_m[n]) for n in WEIGHTS], *[lead(new_v[n]) for n in WEIGHTS])
```

```python
import functools

import numpy as np
import jax
import jax.numpy as jnp
from jax import lax
from jax.experimental import pallas as pl
from jax.experimental.pallas import tpu as pltpu
from jax.experimental.pallas import tpu_sc as plsc

F32 = jnp.float32
BF16 = jnp.bfloat16

S = 2048
D = 2048
NDEV = 8
EPS = 1e-6
NEG = -1e30
HEAD_DIM = 128
ROT_DIM = 32
ROPE_THETA = 500000.0
PATTERNS = ((128, 1), (512, 4), (2048, 16))
BLK = 128
EVEN_IN = 12288
ODD_IN = 6144
HALF = 1024
CONV_K = 31
HALO = 32
TR = 256
SUB = 32

ADAM_LR = 0.001
ADAM_B1 = 0.9
ADAM_B2 = 0.999
ADAM_EPS = 1e-08
ADAM_WD = 0.01
ADAM_STEP = 10

VMEM_BIG = 56 * 1024 * 1024
MESH = pl.DeviceIdType.MESH

NN = (((1,), (0,)), ((), ()))
NT = (((1,), (1,)), ((), ()))
TN = (((0,), (0,)), ((), ()))


def _dot(a, b, dn=NN):
    return lax.dot_general(a, b, dn, preferred_element_type=F32)


def _sigmoid(x):
    return 1.0 / (1.0 + jnp.exp(-x))


def _silu_and_grad(x):
    sg = _sigmoid(x)
    return x * sg, sg * (1.0 + x * (1.0 - sg))


def _params(sem, vmem=None):
    return pltpu.CompilerParams(dimension_semantics=sem, vmem_limit_bytes=vmem)


ANY_SPEC = pl.BlockSpec(memory_space=pl.ANY)


def _matmul(a, b, *, dn, grid, a_spec, b_spec, o_spec, out_shape, out_dtype, acc_shape, name, dep=None):
    nk = grid[2]
    deps = [] if dep is None else list(dep) if isinstance(dep, (tuple, list)) else [dep]

    def body(a_ref, b_ref, *rest):
        o_ref, acc = rest[len(deps)], rest[len(deps) + 1:]
        if nk == 1:
            o_ref[...] = _dot(a_ref[...], b_ref[...], dn).astype(o_ref.dtype)
            return
        acc_ref = acc[0]
        k = pl.program_id(2)

        @pl.when(k == 0)
        def _():
            acc_ref[...] = jnp.zeros_like(acc_ref)

        acc_ref[...] += _dot(a_ref[...], b_ref[...], dn)

        @pl.when(k == nk - 1)
        def _():
            o_ref[...] = acc_ref[...].astype(o_ref.dtype)

    return pl.pallas_call(
        body, grid=grid, in_specs=[a_spec, b_spec] + [ANY_SPEC] * len(deps), out_specs=o_spec,
        out_shape=jax.ShapeDtypeStruct(out_shape, out_dtype),
        scratch_shapes=[] if nk == 1 else [pltpu.VMEM(acc_shape, F32)],
        compiler_params=_params(("parallel", "parallel", "arbitrary"), VMEM_BIG), name=name,
    )(a, b, *deps)


TM = 2048


def _mm_in(h, wg, name):
    nb = wg.shape[2]
    tn = 512 if nb % 512 == 0 else nb
    per = nb // tn
    return _matmul(
        h, wg, dn=NN, grid=(S // TM, NDEV * per, 1),
        a_spec=pl.BlockSpec((TM, D), lambda i, j, k: (i, 0)),
        b_spec=pl.BlockSpec((None, D, tn), lambda i, j, k: (j // per, 0, j % per)),
        o_spec=pl.BlockSpec((TM, tn), lambda i, j, k: (i, j)),
        out_shape=(S, NDEV * nb), out_dtype=F32, acc_shape=(TM, tn), name=name)


def _mm_in_dx(dz, wg, name, dep=None):
    nb = wg.shape[2]
    return _matmul(
        dz, wg, dn=NT, grid=(S // TM, D // 1024, NDEV),
        a_spec=pl.BlockSpec((TM, nb), lambda i, j, k: (i, k)),
        b_spec=pl.BlockSpec((None, 1024, nb), lambda i, j, k: (k, j, 0)),
        o_spec=pl.BlockSpec((TM, 1024), lambda i, j, k: (i, j)),
        out_shape=(S, D), out_dtype=F32, acc_shape=(TM, 1024), name=name, dep=dep)


def _mm_out(yc, w, name, dep=None):
    return _matmul(
        yc, w, dn=NN, grid=(S // TM, D // 512, 1),
        a_spec=pl.BlockSpec((TM, 2048), lambda i, j, k: (i, 0)),
        b_spec=pl.BlockSpec((2048, 512), lambda i, j, k: (0, j)),
        o_spec=pl.BlockSpec((TM, 512), lambda i, j, k: (i, j)),
        out_shape=(S, D), out_dtype=F32, acc_shape=(TM, 512), name=name, dep=dep)


def _mm_out_dx(dy, w, name, dep=None):
    return _matmul(
        dy, w, dn=NT, grid=(S // TM, 2048 // 512, 1),
        a_spec=pl.BlockSpec((TM, D), lambda i, j, k: (i, 0)),
        b_spec=pl.BlockSpec((512, D), lambda i, j, k: (j, 0)),
        o_spec=pl.BlockSpec((TM, 512), lambda i, j, k: (i, j)),
        out_shape=(S, 2048), out_dtype=F32, acc_shape=(TM, 512), name=name, dep=dep)


def _mm_out_dw(yc, dy, name):
    return _matmul(
        yc, dy, dn=TN, grid=(2048 // TM, D // 512, 1),
        a_spec=pl.BlockSpec((S, TM), lambda i, j, k: (0, i)),
        b_spec=pl.BlockSpec((S, 512), lambda i, j, k: (0, j)),
        o_spec=pl.BlockSpec((TM, 512), lambda i, j, k: (i, j)),
        out_shape=(2048, D), out_dtype=BF16, acc_shape=(TM, 512), name=name)


def _row_spec(w=D):
    return pl.BlockSpec((TR, w), lambda i: (i, 0))


def _vec_spec(w=D):
    return pl.BlockSpec((1, w), lambda i: (0, 0))


def _rms_stats(x):
    r = lax.rsqrt(jnp.mean(x * x, axis=-1, keepdims=True) + EPS)
    return x * r, r


def _rms_bwd(dn, xhat, r, g):
    dxh = dn * g
    return r * (dxh - xhat * jnp.mean(dxh * xhat, axis=-1, keepdims=True))


def _acc_rows(ref, val, i):
    s = jnp.sum(val, axis=0, keepdims=True)

    @pl.when(i == 0)
    def _():
        ref[...] = s

    @pl.when(i > 0)
    def _():
        ref[...] += s


def _pre0_fwd(x, g, dep=None):
    deps = [] if dep is None else [dep]

    def body(x_ref, g_ref, *rest):
        xhat, _ = _rms_stats(x_ref[...])
        rest[-1][...] = (xhat * g_ref[...]).astype(BF16)

    return pl.pallas_call(
        body, grid=(S // TR,), in_specs=[_row_spec(), _vec_spec()] + [ANY_SPEC] * len(deps), out_specs=_row_spec(),
        out_shape=jax.ShapeDtypeStruct((S, D), BF16), compiler_params=_params(("parallel",)), name="pre0_fwd",
    )(x, g, *deps)


def _post0_fwd(x, y0, g_post, g_pre1):
    def body(x_ref, y_ref, gp_ref, g1_ref, x1_ref, h1_ref):
        yhat, _ = _rms_stats(y_ref[...])
        x1 = x_ref[...] + yhat * gp_ref[...]
        x1_ref[...] = x1
        xhat, _ = _rms_stats(x1)
        h1_ref[...] = (xhat * g1_ref[...]).astype(BF16)

    return pl.pallas_call(
        body, grid=(S // TR,), in_specs=[_row_spec(), _row_spec(), _vec_spec(), _vec_spec()],
        out_specs=[_row_spec(), _row_spec()],
        out_shape=[jax.ShapeDtypeStruct((S, D), F32), jax.ShapeDtypeStruct((S, D), BF16)],
        compiler_params=_params(("parallel",)), name="post0_fwd",
    )(x, y0, g_post, g_pre1)


def _post1_bwd(y1, x1, target, g_post):
    def body(y_ref, x1_ref, t_ref, g_ref, loss_ref, dx2_ref, dy_ref, dg_ref):
        i = pl.program_id(0)
        yhat, r = _rms_stats(y_ref[...])
        g = g_ref[...]
        err = x1_ref[...] + yhat * g - t_ref[...]
        part = jnp.sum(jnp.sum(err * err, axis=-1, keepdims=True), axis=0, keepdims=True) * (0.5 / D)
        _acc_rows(loss_ref, jnp.broadcast_to(part, (1, 128)), i)
        dx2 = err * (1.0 / D)
        dx2_ref[...] = dx2
        _acc_rows(dg_ref, dx2 * yhat, i)
        dy_ref[...] = _rms_bwd(dx2, yhat, r, g).astype(BF16)

    return pl.pallas_call(
        body, grid=(S // TR,), in_specs=[_row_spec(), _row_spec(), _row_spec(), _vec_spec()],
        out_specs=[_vec_spec(128), _row_spec(), _row_spec(), _vec_spec()],
        out_shape=[jax.ShapeDtypeStruct((1, 128), F32), jax.ShapeDtypeStruct((S, D), F32),
                   jax.ShapeDtypeStruct((S, D), BF16), jax.ShapeDtypeStruct((1, D), F32)],
        compiler_params=_params(("arbitrary",)), name="post1_bwd",
    )(y1, x1, target, g_post)


def _mid_bwd(dx2, dh1, x1, y0, g_pre1, g_post0):
    def body(dx2_ref, dh_ref, x1_ref, y_ref, g1_ref, gp_ref, dx1_ref, dy_ref, dg1_ref, dgp_ref):
        i = pl.program_id(0)
        xhat, r1 = _rms_stats(x1_ref[...])
        dh = dh_ref[...]
        _acc_rows(dg1_ref, dh * xhat, i)
        dx1 = dx2_ref[...] + _rms_bwd(dh, xhat, r1, g1_ref[...])
        dx1_ref[...] = dx1
        yhat, r0 = _rms_stats(y_ref[...])
        _acc_rows(dgp_ref, dx1 * yhat, i)
        dy_ref[...] = _rms_bwd(dx1, yhat, r0, gp_ref[...]).astype(BF16)

    return pl.pallas_call(
        body, grid=(S // TR,),
        in_specs=[_row_spec(), _row_spec(), _row_spec(), _row_spec(), _vec_spec(), _vec_spec()],
        out_specs=[_row_spec(), _row_spec(), _vec_spec(), _vec_spec()],
        out_shape=[jax.ShapeDtypeStruct((S, D), F32), jax.ShapeDtypeStruct((S, D), BF16),
                   jax.ShapeDtypeStruct((1, D), F32), jax.ShapeDtypeStruct((1, D), F32)],
        compiler_params=_params(("arbitrary",)), name="mid_bwd",
    )(dx2, dh1, x1, y0, g_pre1, g_post0)


def _pre0_bwd(dx1, dh0, x, g):
    def body(dx1_ref, dh_ref, x_ref, g_ref, gx_ref, dg_ref):
        i = pl.program_id(0)
        xhat, r = _rms_stats(x_ref[...])
        dh = dh_ref[...]
        _acc_rows(dg_ref, dh * xhat, i)
        gx_ref[...] = dx1_ref[...] + _rms_bwd(dh, xhat, r, g_ref[...])

    return pl.pallas_call(
        body, grid=(S // TR,), in_specs=[_row_spec(), _row_spec(), _row_spec(), _vec_spec()],
        out_specs=[_row_spec(), _vec_spec()],
        out_shape=[jax.ShapeDtypeStruct((S, D), F32), jax.ShapeDtypeStruct((1, D), F32)],
        compiler_params=_params(("arbitrary",)), name="pre0_bwd",
    )(dx1, dh0, x, g)


POOL_CH = 256


def _pool_apply(a, w, transpose):
    n = a.shape[0]
    row = lax.broadcasted_iota(jnp.int32, a.shape, 0)
    cnt = jnp.minimum(row + 1, w).astype(F32)
    s = a / cnt if transpose else a
    for k in (1, 2, 4, 8):
        if transpose:
            sh = jnp.where(row < n - k, pltpu.roll(s, n - k, 0), 0.0)
        else:
            sh = jnp.where(row >= k, pltpu.roll(s, k, 0), 0.0)
        s = jnp.where(w > k, s + sh, s)
    return s - a if transpose else s / cnt - a


def _pool_fwd(z0, pool_w, pool_scale):
    def body(a_ref, gate_ref, w_ref, sc_ref, out_ref):
        win = jnp.left_shift(2, pl.program_id(0))
        pooled = _pool_apply(a_ref[...], win, False)
        mixed = _dot(pooled.astype(BF16), w_ref[...])
        gate = gate_ref[...]
        out_ref[...] = (mixed * sc_ref[...] * (gate * _sigmoid(gate))).astype(BF16)

    return pl.pallas_call(
        body, grid=(4,),
        in_specs=[pl.BlockSpec((S, POOL_CH), lambda g: (0, g)), pl.BlockSpec((S, POOL_CH), lambda g: (0, 4 + g)),
                  pl.BlockSpec((None, POOL_CH, POOL_CH), lambda g: (g, 0, 0)),
                  pl.BlockSpec((1, POOL_CH), lambda g: (0, g))],
        out_specs=pl.BlockSpec((S, POOL_CH), lambda g: (0, g)),
        out_shape=jax.ShapeDtypeStruct((S, 2048), BF16),
        compiler_params=_params(("parallel",), VMEM_BIG), name="pool_fwd",
    )(z0, z0, pool_w, pool_scale)


def _pool_bwd(z0, dycat, pool_w, pool_scale):
    def body(a_ref, gate_ref, dy_ref, w_ref, sc_ref, da_ref, dgate_ref, dw_ref, dsc_ref):
        win = jnp.left_shift(2, pl.program_id(0))
        pooled = _pool_apply(a_ref[...], win, False).astype(BF16)
        w = w_ref[...]
        mixed = _dot(pooled, w)
        silu, dsilu = _silu_and_grad(gate_ref[...])
        dy = dy_ref[...]
        sc = sc_ref[...]
        dgate_ref[...] = (dy * (mixed * sc) * dsilu).astype(BF16)
        dms = dy * silu
        dsc_ref[...] = jnp.sum(dms * mixed, axis=0, keepdims=True)
        dmixed = (dms * sc).astype(BF16)
        dw_ref[...] = _dot(pooled, dmixed, TN)
        dpooled = _dot(dmixed, w, NT)
        da_ref[...] = _pool_apply(dpooled, win, True).astype(BF16)

    slab = lambda off: pl.BlockSpec((S, POOL_CH), lambda g: (0, off + g))
    return pl.pallas_call(
        body, grid=(4,),
        in_specs=[slab(0), slab(4), slab(0), pl.BlockSpec((None, POOL_CH, POOL_CH), lambda g: (g, 0, 0)),
                  pl.BlockSpec((1, POOL_CH), lambda g: (0, g))],
        out_specs=[slab(0), slab(0), pl.BlockSpec((None, POOL_CH, POOL_CH), lambda g: (g, 0, 0)),
                   pl.BlockSpec((1, POOL_CH), lambda g: (0, g))],
        out_shape=[jax.ShapeDtypeStruct((S, HALF), BF16), jax.ShapeDtypeStruct((S, HALF), BF16),
                   jax.ShapeDtypeStruct((4, POOL_CH, POOL_CH), F32), jax.ShapeDtypeStruct((1, HALF), F32)],
        compiler_params=_params(("parallel",), VMEM_BIG), name="pool_bwd",
    )(z0, z0, dycat, pool_w, pool_scale)


Q_COL, K_COL, V_COL, BG_COL = 2048 // 128, 5120 // 128, 8192 // 128, 11264 // 128
SCALE = HEAD_DIM ** -0.5


def _rope_tables():
    pos = jnp.arange(S, dtype=F32)
    inv_freq = jnp.power(ROPE_THETA, -jnp.arange(0, ROT_DIM, 2, dtype=F32) / ROT_DIM)
    ang = pos[:, None] * inv_freq[None, :]
    cos, sin = jnp.cos(ang), jnp.sin(ang)
    half = ROT_DIM // 2
    zeros = jnp.zeros((S, HEAD_DIM - ROT_DIM), F32)
    c = jnp.concatenate([cos, cos, jnp.ones((S, HEAD_DIM - ROT_DIM), F32)], axis=1)
    a = jnp.concatenate([-sin, jnp.zeros((S, half), F32), zeros], axis=1)
    b = jnp.concatenate([jnp.zeros((S, half), F32), sin, zeros], axis=1)
    return c, a, b


def _rope(t, c, a, b):
    half = ROT_DIM // 2
    return t * c + pltpu.roll(t, HEAD_DIM - half, 1) * a + pltpu.roll(t, half, 1) * b


def _rope_t(d, c, a, b):
    half = ROT_DIM // 2
    return d * c + pltpu.roll(d * a, half, 1) + pltpu.roll(d * b, HEAD_DIM - half, 1)


def _deinterleave(dst, src, dil, cast=None, dst_off=0):
    length = S // dil
    for r in range(dil):
        v = src[...] if dil == 1 else src[pl.ds(r, length, stride=dil), :]
        dst[dst_off + r * length:dst_off + (r + 1) * length, :] = v if cast is None else v.astype(cast)


def _interleave(dst, src, dil, src_off=0):
    length = S // dil
    for r in range(dil):
        if dil == 1:
            dst[...] = src[src_off:src_off + S, :]
        else:
            dst[pl.ds(r, length, stride=dil), :] = src[src_off + r * length:src_off + (r + 1) * length, :]


CU = 4
NUNITS = S // BLK
B_QK = (((2,), (2,)), ((0,), (0,)))
B_PV = (((2,), (1,)), ((0,), (0,)))
B_TN = (((1,), (1,)), ((0,), (0,)))


def _blocks(ref, first):
    return ref[first * BLK:(first + CU) * BLK, :].reshape(CU, BLK, HEAD_DIM)


def _chunk_scores(u0, nb, qd, kdp):
    q = _blocks(qd, u0)
    row = lax.broadcasted_iota(jnp.int32, (CU, BLK, BLK), 1)
    col = lax.broadcasted_iota(jnp.int32, (CU, BLK, BLK), 2)
    s_own = jnp.where(col <= row, _dot(q, _blocks(kdp, u0 + 1), B_QK) * SCALE, NEG)
    if nb == 1:
        return q, s_own, None
    unit = lax.broadcasted_iota(jnp.int32, (CU, BLK, BLK), 0) + u0
    s_prev = jnp.where((col >= row) & ((unit % nb) != 0), _dot(q, _blocks(kdp, u0), B_QK) * SCALE, NEG)
    return q, s_own, s_prev


def _qkv_prep(z0, tabs):
    def body(q_ref, k_ref, v_ref, c_ref, a_ref, b_ref, qo_ref, ko_ref, vo_ref, tmp):
        p = pl.program_id(1)
        for gi, (_, dil) in enumerate(PATTERNS):
            @pl.when(p == gi)
            def _(dil=dil):
                c, a, b = c_ref[...], a_ref[...], b_ref[...]
                tmp[...] = _rope(q_ref[...], c, a, b)
                _deinterleave(qo_ref, tmp, dil, BF16)
                tmp[...] = _rope(k_ref[...], c, a, b)
                _deinterleave(ko_ref, tmp, dil, BF16)
                _deinterleave(vo_ref, v_ref, dil, BF16)

    tab = pl.BlockSpec((S, HEAD_DIM), lambda h, p: (0, 0))
    out = pl.BlockSpec((S, HEAD_DIM), lambda h, p: (0, p * 8 + h))
    return pl.pallas_call(
        body, grid=(8, 3), in_specs=[_head_spec(Q_COL), _head_spec(K_COL), _head_spec(V_COL), tab, tab, tab],
        out_specs=[out, out, out], out_shape=[jax.ShapeDtypeStruct((S, 3072), BF16)] * 3,
        scratch_shapes=[pltpu.VMEM((S, HEAD_DIM), F32)],
        compiler_params=_params(("parallel", "arbitrary"), VMEM_BIG), name="qkv_prep",
    )(z0, z0, z0, *tabs)


def _pad_copy(dst, src):
    dst[0:BLK, :] = jnp.zeros((BLK, HEAD_DIM), dst.dtype)
    dst[BLK:BLK + S, :] = src[...]


def _attn_group_fwd(dil, qd, kd_ref, vd_ref, kdp, vdp, od, ld, og, lg):
    nb = S // dil // BLK
    _pad_copy(kdp, kd_ref)
    _pad_copy(vdp, vd_ref)
    for u0 in range(0, NUNITS, CU):
        _, s_own, s_prev = _chunk_scores(u0, nb, qd, kdp)
        m = jnp.max(s_own, axis=2, keepdims=True)
        if s_prev is not None:
            m = jnp.maximum(m, jnp.max(s_prev, axis=2, keepdims=True))
        p_own = jnp.exp(s_own - m)
        den = jnp.sum(p_own, axis=2, keepdims=True)
        acc = _dot(p_own.astype(BF16), _blocks(vdp, u0 + 1), B_PV)
        if s_prev is not None:
            p_prev = jnp.exp(s_prev - m)
            den = den + jnp.sum(p_prev, axis=2, keepdims=True)
            acc = acc + _dot(p_prev.astype(BF16), _blocks(vdp, u0), B_PV)
        rows = slice(u0 * BLK, (u0 + CU) * BLK)
        od[rows, :] = (acc / den).reshape(CU * BLK, HEAD_DIM)
        ld[rows, :] = jnp.broadcast_to(m + jnp.log(den), (CU, BLK, HEAD_DIM)).reshape(CU * BLK, HEAD_DIM)
    _interleave(og, od, dil)
    _interleave(lg, ld, dil)


def _group_weights(lgs):
    l0, l1, l2 = lgs[0][...], lgs[1][...], lgs[2][...]
    mx = jnp.maximum(l0, jnp.maximum(l1, l2))
    e0, e1, e2 = jnp.exp(l0 - mx), jnp.exp(l1 - mx), jnp.exp(l2 - mx)
    den = e0 + e1 + e2
    return e0 / den, e1 / den, e2 / den


def _head_spec(base, ngroups_axis=True):
    return pl.BlockSpec((S, HEAD_DIM), lambda h, p: (0, base + (p % 3) * 8 + h))


def _slab(dtype=F32, rows=S):
    return pltpu.VMEM((rows, HEAD_DIM), dtype)


def _attn_fwd(z0, qkv, ycat):
    def body(q_ref, k_ref, v_ref, gate_ref, ycat_ref, out_ref, og_ref, lg_ref,
             kdp, vdp, od, ld, og0, og1, og2, lg0, lg1, lg2):
        del ycat_ref
        p = pl.program_id(1)
        ogs, lgs = (og0, og1, og2), (lg0, lg1, lg2)
        for gi, (_, dil) in enumerate(PATTERNS):
            @pl.when(p == gi)
            def _(gi=gi, dil=dil):
                _attn_group_fwd(dil, q_ref, k_ref, v_ref, kdp, vdp, od, ld, ogs[gi], lgs[gi])
                og_ref[...] = ogs[gi][...]
                lg_ref[...] = lgs[gi][...]

        @pl.when(p == 2)
        def _():
            w0, w1, w2 = _group_weights(lgs)
            o = w0 * og0[...] + w1 * og1[...] + w2 * og2[...]
            gate = gate_ref[...]
            out_ref[...] = (o * (gate * _sigmoid(gate))).astype(BF16)

    grp = pl.BlockSpec((S, HEAD_DIM), lambda h, p: (0, p * 8 + h))
    return pl.pallas_call(
        body, grid=(8, 3),
        in_specs=[grp, grp, grp, pl.BlockSpec((S, HEAD_DIM), lambda h, p: (0, BG_COL + h)), ANY_SPEC],
        out_specs=[pl.BlockSpec((S, HEAD_DIM), lambda h, p: (0, 8 + h)), grp, grp],
        out_shape=[jax.ShapeDtypeStruct((S, 2048), BF16), jax.ShapeDtypeStruct((S, 3072), F32),
                   jax.ShapeDtypeStruct((S, 3072), F32)],
        scratch_shapes=[_slab(BF16, S + BLK), _slab(BF16, S + BLK)] + [_slab() for _ in range(8)],
        input_output_aliases={4: 0},
        compiler_params=_params(("parallel", "arbitrary"), VMEM_BIG), name="attn_fwd",
    )(*qkv, z0, ycat)


def _attn_bwd(z0, qkv, og, lg, dycat, tabs, dep):
    def body(q_ref, k_ref, v_ref, gate_ref, dy_ref, c_ref, a_ref, b_ref,
             og0_ref, og1_ref, og2_ref, lg0_ref, lg1_ref, lg2_ref, dep_ref,
             dq_ref, dk_ref, dv_ref, dbg_ref,
             tmp, kd, vd, ld, dg0, dg1, dg2, cg0, cg1, cg2, dod, cd, dqd, dkd, dvd):
        p = pl.program_id(1)
        ogs, lgs, dgs, cgs = (og0_ref, og1_ref, og2_ref), (lg0_ref, lg1_ref, lg2_ref), (dg0, dg1, dg2), (cg0, cg1, cg2)

        @pl.when(p == 0)
        def _():
            w = _group_weights(lgs)
            o = w[0] * ogs[0][...] + w[1] * ogs[1][...] + w[2] * ogs[2][...]
            silu, dsilu = _silu_and_grad(gate_ref[...])
            dy = dy_ref[...]
            dbg_ref[...] = (dy * o * dsilu).astype(BF16)
            do = dy * silu
            dwbar = jnp.sum(do * o, axis=1, keepdims=True)
            for gi in range(3):
                dgs[gi][...] = w[gi] * do
                cgs[gi][...] = -w[gi] * dwbar

        for gi, (_, dil) in enumerate(PATTERNS):
            @pl.when(p == 1 + gi)
            def _(gi=gi, dil=dil):
                nb = S // dil // BLK
                qd = q_ref
                c, a, b = c_ref[...], a_ref[...], b_ref[...]
                _pad_copy(kd, k_ref)
                _pad_copy(vd, v_ref)
                _deinterleave(dod, dgs[gi], dil, BF16)
                _deinterleave(ld, lgs[gi], dil)
                _deinterleave(cd, cgs[gi], dil)
                dkd[...] = jnp.zeros_like(dkd)
                dvd[...] = jnp.zeros_like(dvd)
                flat = lambda t: t.reshape(CU * BLK, HEAD_DIM)
                for u0 in range(0, NUNITS, CU):
                    q, s_own, s_prev = _chunk_scores(u0, nb, qd, kd)
                    lse, cv, do = _blocks(ld, u0), _blocks(cd, u0), _blocks(dod, u0)
                    own = slice((u0 + 1) * BLK, (u0 + 1 + CU) * BLK)
                    p_own = jnp.exp(s_own - lse)
                    ds_own = (p_own * (_dot(do, _blocks(vd, u0 + 1), B_QK) + cv) * SCALE).astype(BF16)
                    dq = _dot(ds_own, _blocks(kd, u0 + 1), B_PV)
                    dkd[own, :] += flat(_dot(ds_own, q, B_TN))
                    dvd[own, :] += flat(_dot(p_own.astype(BF16), do, B_TN))
                    if s_prev is not None:
                        prev = slice(u0 * BLK, (u0 + CU) * BLK)
                        p_prev = jnp.exp(s_prev - lse)
                        ds_prev = (p_prev * (_dot(do, _blocks(vd, u0), B_QK) + cv) * SCALE).astype(BF16)
                        dq = dq + _dot(ds_prev, _blocks(kd, u0), B_PV)
                        dkd[prev, :] += flat(_dot(ds_prev, q, B_TN))
                        dvd[prev, :] += flat(_dot(p_prev.astype(BF16), do, B_TN))
                    dqd[u0 * BLK:(u0 + CU) * BLK, :] = flat(dq)
                _interleave(tmp, dqd, dil)
                dq_ref[...] = _rope_t(tmp[...], c, a, b).astype(BF16)
                _interleave(tmp, dkd, dil, BLK)
                dk_ref[...] = _rope_t(tmp[...], c, a, b).astype(BF16)
                _interleave(tmp, dvd, dil, BLK)
                dv_ref[...] = tmp[...].astype(BF16)

    tab = pl.BlockSpec((S, HEAD_DIM), lambda h, p: (0, 0))
    hspec = lambda base: pl.BlockSpec((S, HEAD_DIM), lambda h, p: (0, base + h))
    gspec = pl.BlockSpec((S, HEAD_DIM), lambda h, p: (0, jnp.maximum(p - 1, 0) * 8 + h))
    return pl.pallas_call(
        body, grid=(8, 4),
        in_specs=[gspec, gspec, gspec, hspec(BG_COL), hspec(8), tab, tab, tab,
                  hspec(0), hspec(8), hspec(16), hspec(0), hspec(8), hspec(16), ANY_SPEC],
        out_specs=[gspec, gspec, gspec, hspec(0)],
        out_shape=[jax.ShapeDtypeStruct((S, 3072), BF16)] * 3 + [jax.ShapeDtypeStruct((S, HALF), BF16)],
        scratch_shapes=[_slab(), _slab(BF16, S + BLK), _slab(BF16, S + BLK), _slab()] + [_slab() for _ in range(6)]
                       + [_slab(BF16), _slab(), _slab(), _slab(F32, S + BLK), _slab(F32, S + BLK)],
        compiler_params=_params(("parallel", "arbitrary"), VMEM_BIG), name="attn_bwd",
    )(*qkv, z0, dycat, *tabs, og, og, og, lg, lg, lg, dep)


SGU_CH = 256
NCHUNK = TR // 128


def _ln_stats(x):
    mu = jnp.mean(x, axis=-1, keepdims=True)
    xc = x - mu
    r = lax.rsqrt(jnp.mean(xc * xc, axis=-1, keepdims=True) + EPS)
    return xc * r, r


def _ln_bwd(dy, xhat, r, g):
    dxh = dy * g
    return r * (dxh - jnp.mean(dxh, axis=-1, keepdims=True) - xhat * jnp.mean(dxh * xhat, axis=-1, keepdims=True))


def _tril_bf16(w):
    row = lax.broadcasted_iota(jnp.int32, w.shape, 0)
    col = lax.broadcasted_iota(jnp.int32, w.shape, 1)
    return jnp.where(row >= col, w, 0.0).astype(BF16)


def _sgu_gate(vn_s, s_s, w_ref, bb_ref):
    for h in range(4):
        wm = _tril_bf16(w_ref[h])
        bias = bb_ref[h]
        for ch in range(NCHUNK):
            rows, cols = slice(ch * 128, (ch + 1) * 128), slice(h * SGU_CH, (h + 1) * SGU_CH)
            s_s[rows, cols] = _dot(wm, vn_s[rows, cols]) + jnp.concatenate([bias, bias], axis=1)


WIN = HALO + TR
SUBL = 8


def _shifted_copies(dst, src):
    dst[0] = src[...]
    for b in range(1, SUBL):
        dst[b, 0:WIN - SUBL, :] = src[pl.ds(b, WIN - SUBL), :]


def _rows_at(copies, off, n):
    return copies[off % SUBL, pl.ds(off - off % SUBL, n), :]


def _conv_fwd(i, dval_ref, dglu_ref, hval_ref, hglu_ref, cw_ref, cb_ref, xw, xr, dcs):
    halo = hval_ref[...] * _sigmoid(hglu_ref[...])
    xw[0:HALO, :] = jnp.where(i > 0, halo, 0.0)
    xw[HALO:HALO + TR, :] = dval_ref[...] * _sigmoid(dglu_ref[...])
    _shifted_copies(xr, xw)
    for rb in range(TR // SUB):
        acc = jnp.broadcast_to(cb_ref[...], (SUB, HALF))
        for k in range(CONV_K):
            acc = acc + cw_ref[k:k + 1, :] * _rows_at(xr, rb * SUB + HALO - (CONV_K - 1) + k, SUB)
        dcs[rb * SUB:(rb + 1) * SUB, :] = acc


def _odd_in_specs():
    col = lambda j: pl.BlockSpec((TR, HALF), lambda i, *_: (i, j))
    prev = lambda j: pl.BlockSpec((HALO, HALF), lambda i, *_: (jnp.maximum(i * (TR // HALO) - 1, 0), j))
    return [col(0), col(1), col(2), col(3), col(4), col(5), prev(3), prev(4)]


def _full_spec(shape):
    return pl.BlockSpec(shape, lambda i, *_: (0,) * len(shape))


def _odd_fwd(z1, sgu_g, sgu_b, sgu_w, sgu_bb, conv_w, conv_b, cn_g, cn_b):
    def body(u_ref, v_ref, cg_ref, dval_ref, dglu_ref, dgate_ref, hval_ref, hglu_ref,
             g_ref, b_ref, w_ref, bb_ref, cw_ref, cb_ref, cng_ref, cnb_ref, out_ref, vn_s, s_s, xw, dcs, xr):
        i = pl.program_id(0)
        vhat, _ = _ln_stats(v_ref[...])
        vn_s[...] = (vhat * g_ref[...] + b_ref[...]).astype(BF16)
        _sgu_gate(vn_s, s_s, w_ref, bb_ref)
        cg = cg_ref[...]
        out_ref[:, 0:HALF] = (u_ref[...] * s_s[...] * (cg * _sigmoid(cg))).astype(BF16)
        _conv_fwd(i, dval_ref, dglu_ref, hval_ref, hglu_ref, cw_ref, cb_ref, xw, xr, dcs)
        dhat, _ = _ln_stats(dcs[...])
        dn = dhat * cng_ref[...] + cnb_ref[...]
        dgate = dgate_ref[...]
        out_ref[:, HALF:2 * HALF] = ((dn * _sigmoid(dn)) * (dgate * _sigmoid(dgate))).astype(BF16)

    vec = _full_spec((1, HALF))
    return pl.pallas_call(
        body, grid=(S // TR,),
        in_specs=_odd_in_specs() + [vec, vec, _full_spec((4, 128, 128)), _full_spec((4, 128, 128)),
                                    _full_spec((HALO, HALF)), vec, vec, vec],
        out_specs=pl.BlockSpec((TR, 2048), lambda i: (i, 0)),
        out_shape=jax.ShapeDtypeStruct((S, 2048), BF16),
        scratch_shapes=[pltpu.VMEM((TR, HALF), BF16), pltpu.VMEM((TR, HALF), F32),
                        pltpu.VMEM((WIN, HALF), F32), pltpu.VMEM((TR, HALF), F32), pltpu.VMEM((SUBL, WIN, HALF), F32)],
        compiler_params=_params(("parallel",), VMEM_BIG), name="odd_fwd",
    )(z1, z1, z1, z1, z1, z1, z1, z1, sgu_g, sgu_b, sgu_w, sgu_bb, conv_w, conv_b, cn_g, cn_b)


def _odd_bwd_a(z1, dycat, sgu_g, sgu_b, sgu_w, sgu_bb, conv_w, conv_b, cn_g, cn_b):
    def body(u_ref, v_ref, cg_ref, dval_ref, dglu_ref, dgate_ref, hval_ref, hglu_ref, dy_ref,
             g_ref, b_ref, w_ref, bb_ref, cw_ref, cb_ref, cng_ref, cnb_ref,
             dz_ref, ddc_ref, dw_ref, dbb_ref, dg_ref, db_ref, dcng_ref, dcnb_ref, dcb_ref,
             vn_s, s_s, xw, dcs, ds_s, dvn_s, xr):
        i = pl.program_id(0)
        vhat, rv = _ln_stats(v_ref[...])
        g = g_ref[...]
        vn_s[...] = (vhat * g + b_ref[...]).astype(BF16)
        _sgu_gate(vn_s, s_s, w_ref, bb_ref)
        silu_c, dsilu_c = _silu_and_grad(cg_ref[...])
        dyc = dy_ref[:, 0:HALF]
        u = u_ref[...]
        s = s_s[...]
        dz_ref[:, 0:HALF] = (dyc * s * silu_c).astype(BF16)
        dz_ref[:, 2 * HALF:3 * HALF] = (dyc * u * s * dsilu_c).astype(BF16)
        ds_s[...] = dyc * u * silu_c

        @pl.when(i == 0)
        def _():
            dw_ref[...] = jnp.zeros_like(dw_ref)
            dbb_ref[...] = jnp.zeros_like(dbb_ref)

        tril = lax.broadcasted_iota(jnp.int32, (128, 128), 0) >= lax.broadcasted_iota(jnp.int32, (128, 128), 1)
        for h in range(4):
            wm = _tril_bf16(w_ref[h])
            for ch in range(NCHUNK):
                rows, cols = slice(ch * 128, (ch + 1) * 128), slice(h * SGU_CH, (h + 1) * SGU_CH)
                ds = ds_s[rows, cols]
                dsb = ds.astype(BF16)
                dw_ref[h] += jnp.where(tril, _dot(dsb, vn_s[rows, cols], NT), 0.0)
                dbb_ref[h] += jnp.broadcast_to(jnp.sum(ds, axis=1, keepdims=True), (128, 128))
                dvn_s[rows, cols] = _dot(wm, dsb, TN)
        dvn = dvn_s[...]
        _acc_rows(dg_ref, dvn * vhat, i)
        _acc_rows(db_ref, dvn, i)
        dz_ref[:, HALF:2 * HALF] = _ln_bwd(dvn, vhat, rv, g).astype(BF16)

        _conv_fwd(i, dval_ref, dglu_ref, hval_ref, hglu_ref, cw_ref, cb_ref, xw, xr, dcs)
        dhat, rd = _ln_stats(dcs[...])
        cng = cng_ref[...]
        silu_n, dsilu_n = _silu_and_grad(dhat * cng + cnb_ref[...])
        silu_g, dsilu_g = _silu_and_grad(dgate_ref[...])
        dyd = dy_ref[:, HALF:2 * HALF]
        dz_ref[:, 5 * HALF:6 * HALF] = (dyd * silu_n * dsilu_g).astype(BF16)
        ddn = dyd * silu_g * dsilu_n
        _acc_rows(dcng_ref, ddn * dhat, i)
        _acc_rows(dcnb_ref, ddn, i)
        ddc = _ln_bwd(ddn, dhat, rd, cng)
        ddc_ref[...] = ddc
        _acc_rows(dcb_ref, ddc, i)

    vec = _full_spec((1, HALF))
    sq = _full_spec((4, 128, 128))
    return pl.pallas_call(
        body, grid=(S // TR,),
        in_specs=_odd_in_specs() + [pl.BlockSpec((TR, 2048), lambda i: (i, 0)),
                                    vec, vec, sq, sq, _full_spec((HALO, HALF)), vec, vec, vec],
        out_specs=[pl.BlockSpec((TR, ODD_IN), lambda i: (i, 0)), pl.BlockSpec((TR, HALF), lambda i: (i, 0)),
                   sq, sq, vec, vec, vec, vec, vec],
        out_shape=[jax.ShapeDtypeStruct((S, ODD_IN), BF16), jax.ShapeDtypeStruct((S, HALF), F32),
                   jax.ShapeDtypeStruct((4, 128, 128), F32), jax.ShapeDtypeStruct((4, 128, 128), F32)]
                  + [jax.ShapeDtypeStruct((1, HALF), F32)] * 5,
        scratch_shapes=[pltpu.VMEM((TR, HALF), BF16), pltpu.VMEM((TR, HALF), F32),
                        pltpu.VMEM((WIN, HALF), F32), pltpu.VMEM((TR, HALF), F32),
                        pltpu.VMEM((TR, HALF), F32), pltpu.VMEM((TR, HALF), F32), pltpu.VMEM((SUBL, WIN, HALF), F32)],
        compiler_params=_params(("arbitrary",), VMEM_BIG), name="odd_bwd_a",
    )(z1, z1, z1, z1, z1, z1, z1, z1, dycat, sgu_g, sgu_b, sgu_w, sgu_bb, conv_w, conv_b, cn_g, cn_b)


def _odd_bwd_b(z1, ddc, dz1, conv_w):
    nt = S // TR

    def body(dval_ref, dglu_ref, hval_ref, hglu_ref, ddc_ref, hddc_ref, cw_ref, dz_in_ref,
             dz_ref, dcw_ref, xw, dwin, dxs, xr, dr):
        del dz_in_ref
        i, j = pl.program_id(0), pl.program_id(1)
        sg = _sigmoid(dglu_ref[...])
        dval = dval_ref[...]

        @pl.when(j == 0)
        def _():
            halo = hval_ref[...] * _sigmoid(hglu_ref[...])
            xw[0:HALO, :] = jnp.where(i > 0, halo, 0.0)
            xw[HALO:HALO + TR, :] = dval * sg
            dwin[0:TR, :] = ddc_ref[...]
            dwin[TR:TR + HALO, :] = jnp.where(i < nt - 1, hddc_ref[...], 0.0)
            _shifted_copies(xr, xw)
            _shifted_copies(dr, dwin)

            @pl.when(i == 0)
            def _():
                dcw_ref[...] = jnp.zeros_like(dcw_ref)

            for rb in range(TR // SUB):
                acc = jnp.zeros((SUB, HALF), F32)
                for k in range(CONV_K):
                    acc = acc + cw_ref[k:k + 1, :] * _rows_at(dr, rb * SUB + (CONV_K - 1) - k, SUB)
                dxs[rb * SUB:(rb + 1) * SUB, :] = acc
            for k in range(CONV_K):
                acc = jnp.zeros((SUB, HALF), F32)
                for rb in range(TR // SUB):
                    acc = acc + dwin[rb * SUB:(rb + 1) * SUB, :] * _rows_at(xr, rb * SUB + HALO - (CONV_K - 1) + k, SUB)
                dcw_ref[k:k + 1, :] += jnp.sum(acc, axis=0, keepdims=True)
            dz_ref[...] = (dxs[...] * sg).astype(BF16)

        @pl.when(j == 1)
        def _():
            dz_ref[...] = (dxs[...] * dval * sg * (1.0 - sg)).astype(BF16)

    col = lambda c: pl.BlockSpec((TR, HALF), lambda i, j: (i, c))
    prev = lambda c: pl.BlockSpec((HALO, HALF), lambda i, j: (jnp.maximum(i * (TR // HALO) - 1, 0), c))
    nxt = pl.BlockSpec((HALO, HALF), lambda i, j: (jnp.minimum((i + 1) * (TR // HALO), S // HALO - 1), 0))
    return pl.pallas_call(
        body, grid=(nt, 2),
        in_specs=[col(3), col(4), prev(3), prev(4), pl.BlockSpec((TR, HALF), lambda i, j: (i, 0)), nxt,
                  _full_spec((HALO, HALF)), pl.BlockSpec(memory_space=pl.ANY)],
        out_specs=[pl.BlockSpec((TR, HALF), lambda i, j: (i, 3 + j)), _full_spec((HALO, HALF))],
        out_shape=[jax.ShapeDtypeStruct((S, ODD_IN), BF16), jax.ShapeDtypeStruct((HALO, HALF), F32)],
        scratch_shapes=[pltpu.VMEM((WIN, HALF), F32), pltpu.VMEM((WIN, HALF), F32), pltpu.VMEM((TR, HALF), F32),
                        pltpu.VMEM((SUBL, WIN, HALF), F32), pltpu.VMEM((SUBL, WIN, HALF), F32)],
        input_output_aliases={7: 0},
        compiler_params=_params(("arbitrary", "arbitrary"), VMEM_BIG), name="odd_bwd_b",
    )(z1, z1, z1, z1, ddc, ddc, conv_w, dz1)


def _cast_bf16(w, name):
    r, c = w.shape
    tr = min(r, 256)
    def body(i_ref, o_ref):
        o_ref[...] = i_ref[...].astype(BF16)

    return pl.pallas_call(
        body, grid=(r // tr,), in_specs=[pl.BlockSpec((tr, c), lambda i: (i, 0))],
        out_specs=pl.BlockSpec((tr, c), lambda i: (i, 0)), out_shape=jax.ShapeDtypeStruct((r, c), BF16),
        compiler_params=_params(("parallel",)), name=name,
    )(w)


def _adamw(w, g, m, v):
    m = ADAM_B1 * m + (1.0 - ADAM_B1) * g
    v = ADAM_B2 * v + (1.0 - ADAM_B2) * (g * g)
    m_hat = m / (1.0 - ADAM_B1 ** ADAM_STEP)
    v_hat = v / (1.0 - ADAM_B2 ** ADAM_STEP)
    delta = -ADAM_LR * (m_hat / (jnp.sqrt(v_hat) + ADAM_EPS) + ADAM_WD * w)
    return delta, m, v


def _adam_reduce(parts, w, m, v, name, dep=None):
    r, c = w.shape
    tr = min(r, 128)
    deps = [] if dep is None else [dep]
    pieces = list(parts) if isinstance(parts, (list, tuple)) else [parts]
    npieces, nparts = len(pieces), pieces[0].shape[0]

    def body(*refs):
        p_refs = refs[:npieces]
        w_ref, m_ref, v_ref = refs[npieces:npieces + 3]
        g_ref, d_ref, nm_ref, nv_ref = refs[npieces + 3 + len(deps):]
        cols = []
        for p_ref in p_refs:
            acc = p_ref[0].astype(F32)
            for d in range(1, nparts):
                acc = acc + p_ref[d].astype(F32)
            cols.append(acc)
        g = cols[0] if npieces == 1 else jnp.concatenate(cols, axis=1)
        g_ref[...] = g
        d_ref[...], nm_ref[...], nv_ref[...] = _adamw(w_ref[...], g, m_ref[...], v_ref[...])

    spec = pl.BlockSpec((tr, c), lambda i: (i, 0))
    return pl.pallas_call(
        body, grid=(r // tr,),
        in_specs=[pl.BlockSpec((nparts, tr, a.shape[2]), lambda i: (0, i, 0)) for a in pieces] + [spec, spec, spec]
                 + [ANY_SPEC] * len(deps),
        out_specs=[spec] * 4, out_shape=[jax.ShapeDtypeStruct((r, c), F32)] * 4,
        compiler_params=_params(("parallel",), VMEM_BIG), name=name,
    )(*pieces, w, m, v, *deps)


def _arrived(x, name, dep=None):
    deps = [] if dep is None else [dep]

    def body(*refs):
        refs[-1][...] = jnp.zeros_like(refs[-1])

    return pl.pallas_call(
        body, in_specs=[ANY_SPEC] * (1 + len(deps)), out_specs=pl.BlockSpec(memory_space=pltpu.VMEM),
        out_shape=jax.ShapeDtypeStruct((8, 128), F32), name=name,
    )(x, *deps)


def _sum_parts(parts, name, dep=None):
    r = parts.shape[1]
    tr = 8
    for cand in (512, 256, 128, 64, 32, 16, 8):
        if r % cand == 0:
            tr = cand
            break
    deps = [] if dep is None else [dep]

    def body(p_ref, *rest):
        g = p_ref[0]
        for d in range(1, NDEV):
            g = g + p_ref[d]
        rest[-1][...] = g

    return pl.pallas_call(
        body, grid=(r // tr,), in_specs=[pl.BlockSpec((NDEV, tr, 128), lambda i: (0, i, 0))] + [ANY_SPEC] * len(deps),
        out_specs=pl.BlockSpec((tr, 128), lambda i: (i, 0)), out_shape=jax.ShapeDtypeStruct((r, 128), F32),
        compiler_params=_params(("parallel",)), name=name,
    )(parts, *deps)


def _sum_unpack(parts, rows, name, dep=None):
    deps = [] if dep is None else [dep]

    def body(p_ref, *outs):
        outs = outs[len(deps):]
        off = 0
        for o_ref, n in zip(outs, rows):
            acc = p_ref[0, off:off + n, :]
            for d in range(1, NDEV):
                acc = acc + p_ref[d, off:off + n, :]
            o_ref[...] = acc
            off += n

    return pl.pallas_call(
        body, grid=(1,), in_specs=[pl.BlockSpec(parts.shape, lambda i: (0, 0, 0))] + [ANY_SPEC] * len(deps),
        out_specs=[pl.BlockSpec((n, 128), lambda i: (0, 0)) for n in rows],
        out_shape=[jax.ShapeDtypeStruct((n, 128), F32) for n in rows],
        compiler_params=_params(("arbitrary",), VMEM_BIG), name=name,
    )(parts, *deps)


def _adam_small(ws, gs, g_specs, ms, vs, name):
    n = len(ws)

    def body(*refs):
        w_r, g_r, m_r, v_r = refs[:n], refs[n:2 * n], refs[2 * n:3 * n], refs[3 * n:4 * n]
        outs = refs[4 * n:]
        for i in range(n):
            g = g_r[i][...]
            outs[4 * i][...] = g
            outs[4 * i + 1][...], outs[4 * i + 2][...], outs[4 * i + 3][...] = _adamw(
                w_r[i][...], g, m_r[i][...], v_r[i][...])

    whole = lambda a: pl.BlockSpec(a.shape, lambda i, nd=a.ndim: (0,) * nd)
    outs = pl.pallas_call(
        body, grid=(1,),
        in_specs=[whole(a) for a in ws] + list(g_specs) + [whole(a) for a in ms] + [whole(a) for a in vs],
        out_specs=[whole(a) for a in ws for _ in range(4)],
        out_shape=[jax.ShapeDtypeStruct(a.shape, F32) for a in ws for _ in range(4)],
        compiler_params=_params(("arbitrary",), VMEM_BIG), name=name,
    )(*ws, *gs, *ms, *vs)
    return [outs[4 * i:4 * i + 4] for i in range(n)]


MASKS = [(mx, my, mc) for mx in (0, 1) for my in (0, 1) for mc in (0, 1)][1:]


def _sc_exchange(name, collective_id, arrays, scatter):
    nt = len(arrays)
    out_type = [jax.ShapeDtypeStruct(a.shape if scatter else (NDEV,) + a.shape, a.dtype) for a in arrays]

    def body(*refs):
        ins, outs = refs[:nt], refs[nt:2 * nt]
        send_sems, recv_sems, local_sems = refs[2 * nt:3 * nt], refs[3 * nt:4 * nt], refs[4 * nt:5 * nt]
        x, y, c = lax.axis_index("x"), lax.axis_index("y"), lax.axis_index("c")
        peers = [(mx + x - 2 * mx * x, my + y - 2 * my * y, mc + c - 2 * mc * c) for mx, my, mc in MASKS]
        barrier = pltpu.get_barrier_semaphore()
        for peer in peers:
            pl.semaphore_signal(barrier, inc=1, device_id=peer, device_id_type=MESH)
        pl.semaphore_wait(barrier, len(peers))
        me = 4 * x + 2 * y + c
        own = []
        for t in range(nt):
            cp = pltpu.make_async_copy(ins[t].at[me] if scatter else ins[t], outs[t].at[me], local_sems[t])
            cp.start()
            own.append(cp)
            for px, py, pc in peers:
                src = ins[t].at[4 * px + 2 * py + pc] if scatter else ins[t]
                pltpu.make_async_remote_copy(src_ref=src, dst_ref=outs[t].at[me], send_sem=send_sems[t],
                                             recv_sem=recv_sems[t], device_id=(px, py, pc), device_id_type=MESH).start()
        for t in range(nt):
            own[t].wait()
            seven = outs[t].at[pl.ds(0, NDEV - 1)]
            drain = pltpu.make_async_remote_copy(src_ref=seven, dst_ref=seven, send_sem=send_sems[t],
                                                 recv_sem=recv_sems[t], device_id=(x, y, c), device_id_type=MESH)
            drain.wait_send()
            drain.wait_recv()

    return pl.kernel(
        body, out_type=out_type, mesh=plsc.ScalarSubcoreMesh(axis_name="sequencer", num_cores=1),
        scratch_types=[pltpu.SemaphoreType.DMA] * (3 * nt),
        compiler_params=pltpu.CompilerParams(collective_id=collective_id), name=name,
    )(*arrays)


def _sc_gather_two_level(name, collective_id, arrays):
    nt = len(arrays)
    out_type = [jax.ShapeDtypeStruct((NDEV,) + a.shape, a.dtype) for a in arrays]

    def body(*refs):
        ins, outs = refs[:nt], refs[nt:2 * nt]
        sems = refs[2 * nt:]
        send_sems, sib_sems, local_sems = sems[:nt], sems[nt:2 * nt], sems[2 * nt:3 * nt]
        ici_sems = [sems[3 * nt + 3 * t:3 * nt + 3 * t + 3] for t in range(nt)]
        x, y, c = lax.axis_index("x"), lax.axis_index("y"), lax.axis_index("c")
        sibling = (x, y, 1 - c)
        chips = [(1 - x, y), (x, 1 - y), (1 - x, 1 - y)]
        barrier = pltpu.get_barrier_semaphore()
        for peer in [sibling] + [(cx, cy, c) for cx, cy in chips]:
            pl.semaphore_signal(barrier, inc=1, device_id=peer, device_id_type=MESH)
        pl.semaphore_wait(barrier, 4)
        me = 4 * x + 2 * y + c

        def push(t, src, slot, recv_sem, to):
            pltpu.make_async_remote_copy(src_ref=src, dst_ref=outs[t].at[slot], send_sem=send_sems[t],
                                         recv_sem=recv_sem, device_id=to, device_id_type=MESH).start()

        own = []
        for t in range(nt):
            cp = pltpu.make_async_copy(ins[t], outs[t].at[me], local_sems[t])
            cp.start()
            own.append(cp)
            for j, (cx, cy) in enumerate(chips):
                push(t, ins[t], me, ici_sems[t][j], (cx, cy, c))
            push(t, ins[t], me, sib_sems[t], sibling)
        for t in range(nt):
            for j, (cx, cy) in enumerate(chips):
                slot = 4 * cx + 2 * cy + c
                landed = outs[t].at[slot]
                pltpu.make_async_remote_copy(src_ref=landed, dst_ref=landed, send_sem=send_sems[t],
                                             recv_sem=ici_sems[t][j], device_id=(cx, cy, c),
                                             device_id_type=MESH).wait_recv()
                push(t, landed, slot, sib_sems[t], sibling)
        for t in range(nt):
            own[t].wait()
            four, seven = outs[t].at[pl.ds(0, 4)], outs[t].at[pl.ds(0, 7)]
            pltpu.make_async_remote_copy(src_ref=four, dst_ref=four, send_sem=send_sems[t], recv_sem=sib_sems[t],
                                         device_id=sibling, device_id_type=MESH).wait_recv()
            pltpu.make_async_remote_copy(src_ref=seven, dst_ref=seven, send_sem=send_sems[t], recv_sem=sib_sems[t],
                                         device_id=sibling, device_id_type=MESH).wait_send()

    return pl.kernel(
        body, out_type=out_type, mesh=plsc.ScalarSubcoreMesh(axis_name="sequencer", num_cores=1),
        scratch_types=[pltpu.SemaphoreType.DMA] * (6 * nt),
        compiler_params=pltpu.CompilerParams(collective_id=collective_id), name=name,
    )(*arrays)


def _sc_sibling_exchange(name, collective_id, src, out_shape, pieces):
    def body(src_ref, out_ref, send_sem, recv_sem):
        x, y, c = lax.axis_index("x"), lax.axis_index("y"), lax.axis_index("c")
        sibling = (x, y, 1 - c)
        barrier = pltpu.get_barrier_semaphore()
        pl.semaphore_signal(barrier, inc=1, device_id=sibling, device_id_type=MESH)
        pl.semaphore_wait(barrier, 1)
        for piece, lands in pieces(c, src_ref, out_ref):
            pltpu.make_async_remote_copy(src_ref=piece, dst_ref=lands, send_sem=send_sem, recv_sem=recv_sem,
                                         device_id=sibling, device_id_type=MESH).start()
        drain = pltpu.make_async_remote_copy(src_ref=out_ref, dst_ref=out_ref, send_sem=send_sem, recv_sem=recv_sem,
                                             device_id=sibling, device_id_type=MESH)
        drain.wait_send()
        drain.wait_recv()

    return pl.kernel(
        body, out_type=jax.ShapeDtypeStruct(out_shape, src.dtype),
        mesh=plsc.ScalarSubcoreMesh(axis_name="sequencer", num_cores=1), scratch_types=[pltpu.SemaphoreType.DMA] * 2,
        compiler_params=pltpu.CompilerParams(collective_id=collective_id), name=name,
    )(src)


def _swap_class_columns(name, collective_id, dz, nb):
    return _sc_sibling_exchange(
        name, collective_id, dz, (S, 4 * nb),
        lambda c, src, out: [(src.at[:, pl.ds((2 * j + 1 - c) * nb, nb)], out.at[:, pl.ds(j * nb, nb)])
                             for j in range(4)])


def _sc_chip_scatter(name, collective_id, q):
    def body(q_ref, out_ref, send_sem, recv_sem, local_sem):
        x, y, c = lax.axis_index("x"), lax.axis_index("y"), lax.axis_index("c")
        chips = [(1 - x, y), (x, 1 - y), (1 - x, 1 - y)]
        barrier = pltpu.get_barrier_semaphore()
        for cx, cy in chips:
            pl.semaphore_signal(barrier, inc=1, device_id=(cx, cy, c), device_id_type=MESH)
        pl.semaphore_wait(barrier, 3)
        mine = 2 * x + y
        own = pltpu.make_async_copy(q_ref.at[mine], out_ref.at[mine], local_sem)
        own.start()
        for cx, cy in chips:
            pltpu.make_async_remote_copy(src_ref=q_ref.at[2 * cx + cy], dst_ref=out_ref.at[mine], send_sem=send_sem,
                                         recv_sem=recv_sem, device_id=(cx, cy, c), device_id_type=MESH).start()
        own.wait()
        three = out_ref.at[pl.ds(0, 3)]
        drain = pltpu.make_async_remote_copy(src_ref=three, dst_ref=three, send_sem=send_sem, recv_sem=recv_sem,
                                             device_id=(x, y, c), device_id_type=MESH)
        drain.wait_send()
        drain.wait_recv()

    return pl.kernel(
        body, out_type=jax.ShapeDtypeStruct(q.shape, q.dtype),
        mesh=plsc.ScalarSubcoreMesh(axis_name="sequencer", num_cores=1), scratch_types=[pltpu.SemaphoreType.DMA] * 3,
        compiler_params=pltpu.CompilerParams(collective_id=collective_id), name=name,
    )(q)


def _mm_pair_dw(h_own, dz, h_sib, dz_sib, nb, name, dep=None):
    tn = 512 if nb % 512 == 0 else nb
    per = nb // tn
    o_spec = pl.BlockSpec((None, D, tn), lambda i, j, k: (j // per, 0, j % per))
    part = _matmul(
        h_own, dz, dn=TN, grid=(1, 4 * per, 1),
        a_spec=pl.BlockSpec((S, D), lambda i, j, k: (0, 0)),
        b_spec=pl.BlockSpec((S, tn), lambda i, j, k: (0, (2 * (j // per) + lax.axis_index("c")) * per + j % per)),
        o_spec=o_spec, out_shape=(4, D, nb), out_dtype=F32, acc_shape=(D, tn), name=name + "_own", dep=dep)

    def body(a_ref, b_ref, p_ref, o_ref):
        o_ref[...] = (p_ref[...] + _dot(a_ref[...], b_ref[...], TN)).astype(BF16)

    return pl.pallas_call(
        body, grid=(1, 4 * per, 1),
        in_specs=[pl.BlockSpec((S, D), lambda i, j, k: (0, 0)), pl.BlockSpec((S, tn), lambda i, j, k: (0, j)), o_spec],
        out_specs=o_spec, out_shape=jax.ShapeDtypeStruct((4, D, nb), BF16),
        compiler_params=_params(("parallel", "parallel", "arbitrary"), VMEM_BIG), name=name + "_sibling",
    )(h_sib, dz_sib, part)


SMALL = {
    "e_pre_norm": ((2048,), None), "e_pool_w": ((4, 256, 256), 1), "e_pool_scale": ((1024,), None),
    "e_post_norm": ((2048,), None), "o_pre_norm": ((2048,), 0), "o_sgu_norm_g": ((1024,), 0),
    "o_sgu_norm_b": ((1024,), 0), "o_sgu_w": ((4, 128, 128), None), "o_sgu_b": ((4, 128), None),
    "o_conv_w": ((31, 1024), 1), "o_conv_b": ((1024,), 0), "o_conv_norm_g": ((1024,), 0),
    "o_conv_norm_b": ((1024,), 0), "o_post_norm": ((2048,), 0),
}
SMALL_SHARDED = [n for n, (_, ax) in SMALL.items() if ax is not None]


def _shard_shape(name):
    shape, ax = SMALL[name]
    if ax is None:
        return shape
    return tuple(s // NDEV if i == ax else s for i, s in enumerate(shape))


def _pack(arrs, row_multiple=1):
    flat = jnp.concatenate([a.reshape(-1) for a in arrs])
    pad = -flat.shape[0] % (128 * row_multiple)
    return jnp.concatenate([flat, jnp.zeros((pad,), F32)]).reshape(-1, 128)


def _small_views(name):
    shape, ax = SMALL[name]
    me = lambda: 4 * lax.axis_index("x") + 2 * lax.axis_index("y") + lax.axis_index("c")
    if ax is None:
        view = (int(np.prod(shape)) // 128, 128)
        return view, view, pl.BlockSpec(view, lambda i: (0, 0))
    if len(shape) == 1:
        n = shape[0] // NDEV
        return (1, n), (NDEV, 1, n), pl.BlockSpec((None, 1, n), lambda i: (me(), 0, 0))
    part = _shard_shape(name)
    return part, shape, pl.BlockSpec(part, lambda i: tuple(me() if d == ax else 0 for d in range(len(shape))))


BIG = ("e_w_in", "e_w_out", "o_w_in", "o_w_out")
WEIGHTS = ["e_pre_norm", "e_w_in", "e_pool_w", "e_pool_scale", "e_w_out", "e_post_norm", "o_pre_norm", "o_w_in",
           "o_sgu_norm_g", "o_sgu_norm_b", "o_sgu_w", "o_sgu_b", "o_conv_w", "o_conv_b", "o_conv_norm_g",
           "o_conv_norm_b", "o_w_out", "o_post_norm"]


def kernel(x, e_pre_norm, e_w_in, e_pool_w, e_pool_scale, e_w_out, e_post_norm, o_pre_norm, o_w_in, o_sgu_norm_g, o_sgu_norm_b, o_sgu_w, o_sgu_b, o_conv_w, o_conv_b, o_conv_norm_g, o_conv_norm_b, o_w_out, o_post_norm, loss_target, m_e_pre_norm, m_e_w_in, m_e_pool_w, m_e_pool_scale, m_e_w_out, m_e_post_norm, m_o_pre_norm, m_o_w_in, m_o_sgu_norm_g, m_o_sgu_norm_b, m_o_sgu_w, m_o_sgu_b, m_o_conv_w, m_o_conv_b, m_o_conv_norm_g, m_o_conv_norm_b, m_o_w_out, m_o_post_norm, v_e_pre_norm, v_e_w_in, v_e_pool_w, v_e_pool_scale, v_e_w_out, v_e_post_norm, v_o_pre_norm, v_o_w_in, v_o_sgu_norm_g, v_o_sgu_norm_b, v_o_sgu_w, v_o_sgu_b, v_o_conv_w, v_o_conv_b, v_o_conv_norm_g, v_o_conv_norm_b, v_o_w_out, v_o_post_norm):
    given = dict(locals())
    w = {n: given[n][0] for n in WEIGHTS}
    m = {n: given["m_" + n][0] for n in WEIGHTS}
    v = {n: given["v_" + n][0] for n in WEIGHTS}
    me = 4 * lax.axis_index("x") + 2 * lax.axis_index("y") + lax.axis_index("c")
    x, target = x[0], loss_target[0]
    row = lambda a: a.reshape(1, -1)

    wg_e_in, small_rows = _sc_gather_two_level(
        "gather_a", 0, [_cast_bf16(w["e_w_in"], "cast_e_w_in"), _pack([w[n] for n in SMALL_SHARDED])])
    h0 = _pre0_fwd(x, row(w["e_pre_norm"]))
    wg_e_out, wg_o_in, wg_o_out = _sc_gather_two_level(
        "gather_b", 1, [_cast_bf16(w[n], "cast_" + n) for n in ("e_w_out", "o_w_in", "o_w_out")])
    h0_sib = _sc_sibling_exchange("swap_h0", 8, h0, h0.shape, lambda c, src, out: [(src, out)])
    p = {n: w[n] for n in SMALL if SMALL[n][1] is None}
    small_rows = small_rows.reshape(NDEV, -1)
    off = 0
    for n in SMALL_SHARDED:
        shp, ax = _shard_shape(n), SMALL[n][1]
        cnt = int(np.prod(shp))
        blk = small_rows[:, off:off + cnt].reshape((NDEV,) + shp)
        p[n] = jnp.moveaxis(blk, 0, ax).reshape(SMALL[n][0])
        off += cnt
    tabs = _rope_tables()
    pool_w_bf = p["e_pool_w"].astype(BF16)
    sgu_bb = jnp.broadcast_to(p["o_sgu_b"][:, :, None], (4, 128, 128))
    conv_w = jnp.concatenate([p["o_conv_w"], jnp.zeros((HALO - CONV_K, HALF), F32)], axis=0)
    odd_p = (row(p["o_sgu_norm_g"]), row(p["o_sgu_norm_b"]), p["o_sgu_w"], sgu_bb, conv_w,
             row(p["o_conv_b"]), row(p["o_conv_norm_g"]), row(p["o_conv_norm_b"]))

    z0 = _mm_in(h0, wg_e_in, "mm_z0")
    ycat0 = _pool_fwd(z0, pool_w_bf, row(p["e_pool_scale"]))
    qkv = _qkv_prep(z0, tabs)
    ycat0, og, lg = _attn_fwd(z0, qkv, ycat0)
    w_out_e, w_out_o = wg_e_out.reshape(2048, D), wg_o_out.reshape(2048, D)
    y0 = _mm_out(ycat0, w_out_e, "mm_y0", h0_sib)
    x1, h1 = _post0_fwd(x, y0, row(p["e_post_norm"]), row(p["o_pre_norm"]))
    h1_sib = _sc_sibling_exchange("swap_h1", 11, h1, h1.shape, lambda c, src, out: [(src, out)])
    z1 = _mm_in(h1, wg_o_in, "mm_z1")
    ycat1 = _odd_fwd(z1, *odd_p)
    y1 = _mm_out(ycat1, w_out_o, "mm_y1", h1_sib)

    g = {}
    loss, dx2, dy1, g["o_post_norm"] = _post1_bwd(y1, x1, target, row(p["o_post_norm"]))
    loss = lax.psum(loss[0, 0], ("x", "y", "c"))
    parts = {}
    dw = _mm_out_dw(ycat1, dy1, "mm_dwout1").reshape(NDEV, 256, D)
    parts["o_w_out"], = _sc_exchange("scatter_o_w_out", 2, [dw], True)
    dycat1 = _mm_out_dx(dy1, w_out_o, "mm_dycat1", (dw, loss.reshape(1, 1)))
    dz1, ddc, g["o_sgu_w"], d_sgu_bb, g["o_sgu_norm_g"], g["o_sgu_norm_b"], g["o_conv_norm_g"], \
        g["o_conv_norm_b"], g["o_conv_b"] = _odd_bwd_a(z1, dycat1, *odd_p)
    dz1, d_conv_w = _odd_bwd_b(z1, ddc, dz1, conv_w)
    g["o_sgu_b"] = d_sgu_bb[:, :, 0]
    g["o_conv_w"] = d_conv_w[:CONV_K]
    grads, deltas, new_m, new_v = {}, {}, {}, {}

    def adam(n, dep):
        grads[n], deltas[n], new_m[n], new_v[n] = _adam_reduce(parts[n], w[n], m[n], v[n], "adam_" + n, dep)
        return new_v[n]

    pin = adam("o_w_out", d_conv_w)
    dz1_sib = _swap_class_columns("swap_dz1", 10, dz1, ODD_IN // NDEV)
    dw = _mm_pair_dw(h1, dz1, h1_sib, dz1_sib, ODD_IN // NDEV, "mm_dwin1", pin)
    parts["o_w_in"] = _sc_chip_scatter("scatter_o_w_in", 3, dw)
    dh1 = _mm_in_dx(dz1, wg_o_in, "mm_dh1", dw)
    dx1, dy0, g["o_pre_norm"], g["e_post_norm"] = _mid_bwd(dx2, dh1, x1, y0, row(p["o_pre_norm"]),
                                                           row(p["e_post_norm"]))
    dw = _mm_out_dw(ycat0, dy0, "mm_dwout0").reshape(NDEV, 256, D)
    parts["e_w_out"], = _sc_exchange("scatter_e_w_out", 4, [dw], True)
    dycat0 = _mm_out_dx(dy0, w_out_e, "mm_dycat0", dw)
    da_in, da_gate, g["e_pool_w"], g["e_pool_scale"] = _pool_bwd(z0, dycat0, pool_w_bf, row(p["e_pool_scale"]))
    late = [n for n in SMALL if n not in ("e_pre_norm", "o_sgu_b")] + ["o_sgu_b"]
    recv_small, = _sc_gather_two_level("gather_small_grads", 6,
                                       [_pack([g[n].reshape(SMALL[n][0]) for n in late], 512)])
    took = _arrived(parts["o_w_in"], "arrived_o_w_in")
    dq, dk, dv, dbg = _attn_bwd(z0, qkv, og, lg, dycat0, tabs, took)
    dz0 = jnp.concatenate([da_in, da_gate, dq, dk, dv, dbg], axis=1)
    pin = adam("e_w_out", dz0)
    rows = [int(np.prod(SMALL[n][0])) // 128 for n in late]
    summed = dict(zip(late, _sum_unpack(recv_small, rows, "sum_small_grads", pin)))
    dz0_sib = _swap_class_columns("swap_dz0", 9, dz0, EVEN_IN // NDEV)
    dw = _mm_pair_dw(h0, dz0, h0_sib, dz0_sib, EVEN_IN // NDEV, "mm_dwin0", summed[late[0]])
    parts["e_w_in"] = _sc_chip_scatter("scatter_e_w_in", 5, dw)
    pin = adam("o_w_in", dw)
    dh0 = _mm_in_dx(dz0, wg_e_in, "mm_dh0", (dw, pin))
    grad_x, g["e_pre_norm"] = _pre0_bwd(dx1, dh0, x, row(p["e_pre_norm"]))
    last, = _sc_exchange("gather_e_pre_norm_grad", 7, [g["e_pre_norm"].reshape(16, 128)], False)

    pin = adam("e_w_in", grad_x)
    summed["e_pre_norm"] = _sum_parts(last, "sum_e_pre_norm_grad", pin)
    names = list(SMALL)
    views = [_small_views(n) for n in names]
    mine = lambda src: [src[n].reshape(vw[0]) for n, vw in zip(names, views)]
    res = _adam_small(mine(w), [summed[n].reshape(vw[1]) for n, vw in zip(names, views)], [vw[2] for vw in views],
                      mine(m), mine(v), "adam_small")
    for n, out in zip(names, res):
        grads[n], deltas[n], new_m[n], new_v[n] = [t.reshape(_shard_shape(n)) for t in out]

    lead = lambda a: a[None]
    return (loss, grad_x[None], *[lead(grads[n]) for n in WEIGHTS], *[lead(deltas[n]) for n in WEIGHTS],
            *[lead(new_m[n]) for n in WEIGHTS], *[lead(new_v[n]) for n in WEIGHTS])
```

```python
import functools

import numpy as np
import jax
import jax.numpy as jnp
from jax import lax
from jax.experimental import pallas as pl
from jax.experimental.pallas import tpu as pltpu
from jax.experimental.pallas import tpu_sc as plsc

F32 = jnp.float32
BF16 = jnp.bfloat16

S = 2048
D = 2048
NDEV = 8
EPS = 1e-6
NEG = -1e30
HEAD_DIM = 128
ROT_DIM = 32
ROPE_THETA = 500000.0
PATTERNS = ((128, 1), (512, 4), (2048, 16))
BLK = 128
EVEN_IN = 12288
ODD_IN = 6144
HALF = 1024
CONV_K = 31
HALO = 32
TR = 256
SUB = 32

ADAM_LR = 0.001
ADAM_B1 = 0.9
ADAM_B2 = 0.999
ADAM_EPS = 1e-08
ADAM_WD = 0.01
ADAM_STEP = 10

VMEM_BIG = 56 * 1024 * 1024
MESH = pl.DeviceIdType.MESH

NN = (((1,), (0,)), ((), ()))
NT = (((1,), (1,)), ((), ()))
TN = (((0,), (0,)), ((), ()))


def _dot(a, b, dn=NN):
    return lax.dot_general(a, b, dn, preferred_element_type=F32)


def _sigmoid(x):
    return 1.0 / (1.0 + jnp.exp(-x))


def _silu_and_grad(x):
    sg = _sigmoid(x)
    return x * sg, sg * (1.0 + x * (1.0 - sg))


def _params(sem, vmem=None):
    return pltpu.CompilerParams(dimension_semantics=sem, vmem_limit_bytes=vmem)


ANY_SPEC = pl.BlockSpec(memory_space=pl.ANY)


def _matmul(a, b, *, dn, grid, a_spec, b_spec, o_spec, out_shape, out_dtype, acc_shape, name, dep=None):
    nk = grid[2]
    deps = [] if dep is None else list(dep) if isinstance(dep, (tuple, list)) else [dep]

    def body(a_ref, b_ref, *rest):
        o_ref, acc = rest[len(deps)], rest[len(deps) + 1:]
        if nk == 1:
            o_ref[...] = _dot(a_ref[...], b_ref[...], dn).astype(o_ref.dtype)
            return
        acc_ref = acc[0]
        k = pl.program_id(2)

        @pl.when(k == 0)
        def _():
            acc_ref[...] = jnp.zeros_like(acc_ref)

        acc_ref[...] += _dot(a_ref[...], b_ref[...], dn)

        @pl.when(k == nk - 1)
        def _():
            o_ref[...] = acc_ref[...].astype(o_ref.dtype)

    return pl.pallas_call(
        body, grid=grid, in_specs=[a_spec, b_spec] + [ANY_SPEC] * len(deps), out_specs=o_spec,
        out_shape=jax.ShapeDtypeStruct(out_shape, out_dtype),
        scratch_shapes=[] if nk == 1 else [pltpu.VMEM(acc_shape, F32)],
        compiler_params=_params(("parallel", "parallel", "arbitrary"), VMEM_BIG), name=name,
    )(a, b, *deps)


TM = 2048


def _mm_in(h, wg, name):
    nb = wg.shape[2]
    tn = 512 if nb % 512 == 0 else nb
    per = nb // tn
    return _matmul(
        h, wg, dn=NN, grid=(S // TM, NDEV * per, 1),
        a_spec=pl.BlockSpec((TM, D), lambda i, j, k: (i, 0)),
        b_spec=pl.BlockSpec((None, D, tn), lambda i, j, k: (j // per, 0, j % per)),
        o_spec=pl.BlockSpec((TM, tn), lambda i, j, k: (i, j)),
        out_shape=(S, NDEV * nb), out_dtype=F32, acc_shape=(TM, tn), name=name)


def _mm_in_halves(h, wg_halves, name):
    hb = wg_halves[0].shape[2]
    z = None
    for half, wg in enumerate(wg_halves):
        prev = [] if z is None else [z]

        def body(a_ref, b_ref, *rest):
            rest[-1][...] = _dot(a_ref[...], b_ref[...])

        z = pl.pallas_call(
            body, grid=(NDEV,),
            in_specs=[pl.BlockSpec((S, D), lambda j: (0, 0)), pl.BlockSpec((None, D, hb), lambda j: (j, 0, 0))]
                     + [ANY_SPEC] * len(prev),
            out_specs=pl.BlockSpec((S, hb), lambda j, half=half: (0, 2 * j + half)),
            out_shape=jax.ShapeDtypeStruct((S, 2 * NDEV * hb), F32),
            input_output_aliases={2: 0} if prev else {},
            compiler_params=_params(("parallel",), VMEM_BIG), name="%s_%d" % (name, half),
        )(h, wg, *prev)
    return z


def _mm_in_dx_halves(dz, wg_halves, name, dep):
    hb = wg_halves[0].shape[2]
    nk = 2 * NDEV

    def body(a_ref, b0_ref, b1_ref, dep_ref, o_ref, acc_ref):
        k = pl.program_id(2)

        @pl.when(k == 0)
        def _():
            acc_ref[...] = jnp.zeros_like(acc_ref)

        @pl.when(k % 2 == 0)
        def _():
            acc_ref[...] += _dot(a_ref[...], b0_ref[...], NT)

        @pl.when(k % 2 == 1)
        def _():
            acc_ref[...] += _dot(a_ref[...], b1_ref[...], NT)

        @pl.when(k == nk - 1)
        def _():
            o_ref[...] = acc_ref[...]

    b_spec = pl.BlockSpec((None, 1024, hb), lambda i, j, k: (k // 2, j, 0))
    return pl.pallas_call(
        body, grid=(1, D // 1024, nk),
        in_specs=[pl.BlockSpec((S, hb), lambda i, j, k: (0, k)), b_spec, b_spec, ANY_SPEC],
        out_specs=pl.BlockSpec((S, 1024), lambda i, j, k: (0, j)), out_shape=jax.ShapeDtypeStruct((S, D), F32),
        scratch_shapes=[pltpu.VMEM((S, 1024), F32)],
        compiler_params=_params(("parallel", "parallel", "arbitrary"), VMEM_BIG), name=name,
    )(dz, *wg_halves, dep)


def _mm_in_dx(dz, wg, name, dep=None):
    nb = wg.shape[2]
    return _matmul(
        dz, wg, dn=NT, grid=(S // TM, D // 1024, NDEV),
        a_spec=pl.BlockSpec((TM, nb), lambda i, j, k: (i, k)),
        b_spec=pl.BlockSpec((None, 1024, nb), lambda i, j, k: (k, j, 0)),
        o_spec=pl.BlockSpec((TM, 1024), lambda i, j, k: (i, j)),
        out_shape=(S, D), out_dtype=F32, acc_shape=(TM, 1024), name=name, dep=dep)


def _mm_out(yc, w, name, dep=None):
    return _matmul(
        yc, w, dn=NN, grid=(S // TM, D // 512, 1),
        a_spec=pl.BlockSpec((TM, 2048), lambda i, j, k: (i, 0)),
        b_spec=pl.BlockSpec((2048, 512), lambda i, j, k: (0, j)),
        o_spec=pl.BlockSpec((TM, 512), lambda i, j, k: (i, j)),
        out_shape=(S, D), out_dtype=F32, acc_shape=(TM, 512), name=name, dep=dep)


def _mm_out_dx(dy, w, name, dep=None):
    return _matmul(
        dy, w, dn=NT, grid=(S // TM, 2048 // 512, 1),
        a_spec=pl.BlockSpec((TM, D), lambda i, j, k: (i, 0)),
        b_spec=pl.BlockSpec((512, D), lambda i, j, k: (j, 0)),
        o_spec=pl.BlockSpec((TM, 512), lambda i, j, k: (i, j)),
        out_shape=(S, 2048), out_dtype=F32, acc_shape=(TM, 512), name=name, dep=dep)


def _mm_out_dw(yc, dy, name):
    return _matmul(
        yc, dy, dn=TN, grid=(2048 // TM, D // 512, 1),
        a_spec=pl.BlockSpec((S, TM), lambda i, j, k: (0, i)),
        b_spec=pl.BlockSpec((S, 512), lambda i, j, k: (0, j)),
        o_spec=pl.BlockSpec((TM, 512), lambda i, j, k: (i, j)),
        out_shape=(2048, D), out_dtype=BF16, acc_shape=(TM, 512), name=name)


def _row_spec(w=D):
    return pl.BlockSpec((TR, w), lambda i: (i, 0))


def _vec_spec(w=D):
    return pl.BlockSpec((1, w), lambda i: (0, 0))


def _rms_stats(x):
    r = lax.rsqrt(jnp.mean(x * x, axis=-1, keepdims=True) + EPS)
    return x * r, r


def _rms_bwd(dn, xhat, r, g):
    dxh = dn * g
    return r * (dxh - xhat * jnp.mean(dxh * xhat, axis=-1, keepdims=True))


def _acc_rows(ref, val, i):
    s = jnp.sum(val, axis=0, keepdims=True)

    @pl.when(i == 0)
    def _():
        ref[...] = s

    @pl.when(i > 0)
    def _():
        ref[...] += s


def _pre0_fwd(x, g, dep=None):
    deps = [] if dep is None else [dep]

    def body(x_ref, g_ref, *rest):
        xhat, _ = _rms_stats(x_ref[...])
        rest[-1][...] = (xhat * g_ref[...]).astype(BF16)

    return pl.pallas_call(
        body, grid=(S // TR,), in_specs=[_row_spec(), _vec_spec()] + [ANY_SPEC] * len(deps), out_specs=_row_spec(),
        out_shape=jax.ShapeDtypeStruct((S, D), BF16), compiler_params=_params(("parallel",)), name="pre0_fwd",
    )(x, g, *deps)


def _post0_fwd(x, y0, g_post, g_pre1):
    def body(x_ref, y_ref, gp_ref, g1_ref, x1_ref, h1_ref):
        yhat, _ = _rms_stats(y_ref[...])
        x1 = x_ref[...] + yhat * gp_ref[...]
        x1_ref[...] = x1
        xhat, _ = _rms_stats(x1)
        h1_ref[...] = (xhat * g1_ref[...]).astype(BF16)

    return pl.pallas_call(
        body, grid=(S // TR,), in_specs=[_row_spec(), _row_spec(), _vec_spec(), _vec_spec()],
        out_specs=[_row_spec(), _row_spec()],
        out_shape=[jax.ShapeDtypeStruct((S, D), F32), jax.ShapeDtypeStruct((S, D), BF16)],
        compiler_params=_params(("parallel",)), name="post0_fwd",
    )(x, y0, g_post, g_pre1)


def _post1_bwd(y1, x1, target, g_post):
    def body(y_ref, x1_ref, t_ref, g_ref, loss_ref, dx2_ref, dy_ref, dg_ref):
        i = pl.program_id(0)
        yhat, r = _rms_stats(y_ref[...])
        g = g_ref[...]
        err = x1_ref[...] + yhat * g - t_ref[...]
        part = jnp.sum(jnp.sum(err * err, axis=-1, keepdims=True), axis=0, keepdims=True) * (0.5 / D)
        _acc_rows(loss_ref, jnp.broadcast_to(part, (1, 128)), i)
        dx2 = err * (1.0 / D)
        dx2_ref[...] = dx2
        _acc_rows(dg_ref, dx2 * yhat, i)
        dy_ref[...] = _rms_bwd(dx2, yhat, r, g).astype(BF16)

    return pl.pallas_call(
        body, grid=(S // TR,), in_specs=[_row_spec(), _row_spec(), _row_spec(), _vec_spec()],
        out_specs=[_vec_spec(128), _row_spec(), _row_spec(), _vec_spec()],
        out_shape=[jax.ShapeDtypeStruct((1, 128), F32), jax.ShapeDtypeStruct((S, D), F32),
                   jax.ShapeDtypeStruct((S, D), BF16), jax.ShapeDtypeStruct((1, D), F32)],
        compiler_params=_params(("arbitrary",)), name="post1_bwd",
    )(y1, x1, target, g_post)


def _mid_bwd(dx2, dh1, x1, y0, g_pre1, g_post0):
    def body(dx2_ref, dh_ref, x1_ref, y_ref, g1_ref, gp_ref, dx1_ref, dy_ref, dg1_ref, dgp_ref):
        i = pl.program_id(0)
        xhat, r1 = _rms_stats(x1_ref[...])
        dh = dh_ref[...]
        _acc_rows(dg1_ref, dh * xhat, i)
        dx1 = dx2_ref[...] + _rms_bwd(dh, xhat, r1, g1_ref[...])
        dx1_ref[...] = dx1
        yhat, r0 = _rms_stats(y_ref[...])
        _acc_rows(dgp_ref, dx1 * yhat, i)
        dy_ref[...] = _rms_bwd(dx1, yhat, r0, gp_ref[...]).astype(BF16)

    return pl.pallas_call(
        body, grid=(S // TR,),
        in_specs=[_row_spec(), _row_spec(), _row_spec(), _row_spec(), _vec_spec(), _vec_spec()],
        out_specs=[_row_spec(), _row_spec(), _vec_spec(), _vec_spec()],
        out_shape=[jax.ShapeDtypeStruct((S, D), F32), jax.ShapeDtypeStruct((S, D), BF16),
                   jax.ShapeDtypeStruct((1, D), F32), jax.ShapeDtypeStruct((1, D), F32)],
        compiler_params=_params(("arbitrary",)), name="mid_bwd",
    )(dx2, dh1, x1, y0, g_pre1, g_post0)


def _pre0_bwd(dx1, dh0, x, g):
    def body(dx1_ref, dh_ref, x_ref, g_ref, gx_ref, dg_ref):
        i = pl.program_id(0)
        xhat, r = _rms_stats(x_ref[...])
        dh = dh_ref[...]
        _acc_rows(dg_ref, dh * xhat, i)
        gx_ref[...] = dx1_ref[...] + _rms_bwd(dh, xhat, r, g_ref[...])

    return pl.pallas_call(
        body, grid=(S // TR,), in_specs=[_row_spec(), _row_spec(), _row_spec(), _vec_spec()],
        out_specs=[_row_spec(), _vec_spec()],
        out_shape=[jax.ShapeDtypeStruct((S, D), F32), jax.ShapeDtypeStruct((1, D), F32)],
        compiler_params=_params(("arbitrary",)), name="pre0_bwd",
    )(dx1, dh0, x, g)


POOL_CH = 256


def _pool_apply(a, w, transpose):
    n = a.shape[0]
    row = lax.broadcasted_iota(jnp.int32, a.shape, 0)
    cnt = jnp.minimum(row + 1, w).astype(F32)
    s = a / cnt if transpose else a
    for k in (1, 2, 4, 8):
        if transpose:
            sh = jnp.where(row < n - k, pltpu.roll(s, n - k, 0), 0.0)
        else:
            sh = jnp.where(row >= k, pltpu.roll(s, k, 0), 0.0)
        s = jnp.where(w > k, s + sh, s)
    return s - a if transpose else s / cnt - a


def _pool_fwd(z0, pool_w, pool_scale):
    def body(a_ref, gate_ref, w_ref, sc_ref, out_ref):
        win = jnp.left_shift(2, pl.program_id(0))
        pooled = _pool_apply(a_ref[...], win, False)
        mixed = _dot(pooled.astype(BF16), w_ref[...])
        gate = gate_ref[...]
        out_ref[...] = (mixed * sc_ref[...] * (gate * _sigmoid(gate))).astype(BF16)

    return pl.pallas_call(
        body, grid=(4,),
        in_specs=[pl.BlockSpec((S, POOL_CH), lambda g: (0, g)), pl.BlockSpec((S, POOL_CH), lambda g: (0, 4 + g)),
                  pl.BlockSpec((None, POOL_CH, POOL_CH), lambda g: (g, 0, 0)),
                  pl.BlockSpec((1, POOL_CH), lambda g: (0, g))],
        out_specs=pl.BlockSpec((S, POOL_CH), lambda g: (0, g)),
        out_shape=jax.ShapeDtypeStruct((S, 2048), BF16),
        compiler_params=_params(("parallel",), VMEM_BIG), name="pool_fwd",
    )(z0, z0, pool_w, pool_scale)


def _pool_bwd(z0, dycat, pool_w, pool_scale):
    def body(a_ref, gate_ref, dy_ref, w_ref, sc_ref, da_ref, dgate_ref, dw_ref, dsc_ref):
        win = jnp.left_shift(2, pl.program_id(0))
        pooled = _pool_apply(a_ref[...], win, False).astype(BF16)
        w = w_ref[...]
        mixed = _dot(pooled, w)
        silu, dsilu = _silu_and_grad(gate_ref[...])
        dy = dy_ref[...]
        sc = sc_ref[...]
        dgate_ref[...] = (dy * (mixed * sc) * dsilu).astype(BF16)
        dms = dy * silu
        dsc_ref[...] = jnp.sum(dms * mixed, axis=0, keepdims=True)
        dmixed = (dms * sc).astype(BF16)
        dw_ref[...] = _dot(pooled, dmixed, TN)
        dpooled = _dot(dmixed, w, NT)
        da_ref[...] = _pool_apply(dpooled, win, True).astype(BF16)

    slab = lambda off: pl.BlockSpec((S, POOL_CH), lambda g: (0, off + g))
    return pl.pallas_call(
        body, grid=(4,),
        in_specs=[slab(0), slab(4), slab(0), pl.BlockSpec((None, POOL_CH, POOL_CH), lambda g: (g, 0, 0)),
                  pl.BlockSpec((1, POOL_CH), lambda g: (0, g))],
        out_specs=[slab(0), slab(0), pl.BlockSpec((None, POOL_CH, POOL_CH), lambda g: (g, 0, 0)),
                   pl.BlockSpec((1, POOL_CH), lambda g: (0, g))],
        out_shape=[jax.ShapeDtypeStruct((S, HALF), BF16), jax.ShapeDtypeStruct((S, HALF), BF16),
                   jax.ShapeDtypeStruct((4, POOL_CH, POOL_CH), F32), jax.ShapeDtypeStruct((1, HALF), F32)],
        compiler_params=_params(("parallel",), VMEM_BIG), name="pool_bwd",
    )(z0, z0, dycat, pool_w, pool_scale)


Q_COL, K_COL, V_COL, BG_COL = 2048 // 128, 5120 // 128, 8192 // 128, 11264 // 128
SCALE = HEAD_DIM ** -0.5


def _rope_tables():
    pos = jnp.arange(S, dtype=F32)
    inv_freq = jnp.power(ROPE_THETA, -jnp.arange(0, ROT_DIM, 2, dtype=F32) / ROT_DIM)
    ang = pos[:, None] * inv_freq[None, :]
    cos, sin = jnp.cos(ang), jnp.sin(ang)
    half = ROT_DIM // 2
    zeros = jnp.zeros((S, HEAD_DIM - ROT_DIM), F32)
    c = jnp.concatenate([cos, cos, jnp.ones((S, HEAD_DIM - ROT_DIM), F32)], axis=1)
    a = jnp.concatenate([-sin, jnp.zeros((S, half), F32), zeros], axis=1)
    b = jnp.concatenate([jnp.zeros((S, half), F32), sin, zeros], axis=1)
    return c, a, b


def _rope(t, c, a, b):
    half = ROT_DIM // 2
    return t * c + pltpu.roll(t, HEAD_DIM - half, 1) * a + pltpu.roll(t, half, 1) * b


def _rope_t(d, c, a, b):
    half = ROT_DIM // 2
    return d * c + pltpu.roll(d * a, half, 1) + pltpu.roll(d * b, HEAD_DIM - half, 1)


def _deinterleave(dst, src, dil, cast=None, dst_off=0):
    length = S // dil
    for r in range(dil):
        v = src[...] if dil == 1 else src[pl.ds(r, length, stride=dil), :]
        dst[dst_off + r * length:dst_off + (r + 1) * length, :] = v if cast is None else v.astype(cast)


def _interleave(dst, src, dil, src_off=0):
    length = S // dil
    for r in range(dil):
        if dil == 1:
            dst[...] = src[src_off:src_off + S, :]
        else:
            dst[pl.ds(r, length, stride=dil), :] = src[src_off + r * length:src_off + (r + 1) * length, :]


CU = 4
NUNITS = S // BLK
B_QK = (((2,), (2,)), ((0,), (0,)))
B_PV = (((2,), (1,)), ((0,), (0,)))
B_TN = (((1,), (1,)), ((0,), (0,)))


def _blocks(ref, first):
    return ref[first * BLK:(first + CU) * BLK, :].reshape(CU, BLK, HEAD_DIM)


def _chunk_scores(u0, nb, qd, kdp):
    q = _blocks(qd, u0)
    row = lax.broadcasted_iota(jnp.int32, (CU, BLK, BLK), 1)
    col = lax.broadcasted_iota(jnp.int32, (CU, BLK, BLK), 2)
    s_own = jnp.where(col <= row, _dot(q, _blocks(kdp, u0 + 1), B_QK) * SCALE, NEG)
    if nb == 1:
        return q, s_own, None
    unit = lax.broadcasted_iota(jnp.int32, (CU, BLK, BLK), 0) + u0
    s_prev = jnp.where((col >= row) & ((unit % nb) != 0), _dot(q, _blocks(kdp, u0), B_QK) * SCALE, NEG)
    return q, s_own, s_prev


def _qkv_prep(z0, tabs):
    def body(q_ref, k_ref, v_ref, c_ref, a_ref, b_ref, qo_ref, ko_ref, vo_ref, tmp):
        p = pl.program_id(1)
        for gi, (_, dil) in enumerate(PATTERNS):
            @pl.when(p == gi)
            def _(dil=dil):
                c, a, b = c_ref[...], a_ref[...], b_ref[...]
                tmp[...] = _rope(q_ref[...], c, a, b)
                _deinterleave(qo_ref, tmp, dil, BF16)
                tmp[...] = _rope(k_ref[...], c, a, b)
                _deinterleave(ko_ref, tmp, dil, BF16)
                _deinterleave(vo_ref, v_ref, dil, BF16)

    tab = pl.BlockSpec((S, HEAD_DIM), lambda h, p: (0, 0))
    out = pl.BlockSpec((S, HEAD_DIM), lambda h, p: (0, p * 8 + h))
    return pl.pallas_call(
        body, grid=(8, 3), in_specs=[_head_spec(Q_COL), _head_spec(K_COL), _head_spec(V_COL), tab, tab, tab],
        out_specs=[out, out, out], out_shape=[jax.ShapeDtypeStruct((S, 3072), BF16)] * 3,
        scratch_shapes=[pltpu.VMEM((S, HEAD_DIM), F32)],
        compiler_params=_params(("parallel", "arbitrary"), VMEM_BIG), name="qkv_prep",
    )(z0, z0, z0, *tabs)


def _pad_copy(dst, src):
    dst[0:BLK, :] = jnp.zeros((BLK, HEAD_DIM), dst.dtype)
    dst[BLK:BLK + S, :] = src[...]


def _attn_group_fwd(dil, qd, kd_ref, vd_ref, kdp, vdp, od, ld, og, lg):
    nb = S // dil // BLK
    _pad_copy(kdp, kd_ref)
    _pad_copy(vdp, vd_ref)
    for u0 in range(0, NUNITS, CU):
        _, s_own, s_prev = _chunk_scores(u0, nb, qd, kdp)
        m = jnp.max(s_own, axis=2, keepdims=True)
        if s_prev is not None:
            m = jnp.maximum(m, jnp.max(s_prev, axis=2, keepdims=True))
        p_own = jnp.exp(s_own - m)
        den = jnp.sum(p_own, axis=2, keepdims=True)
        acc = _dot(p_own.astype(BF16), _blocks(vdp, u0 + 1), B_PV)
        if s_prev is not None:
            p_prev = jnp.exp(s_prev - m)
            den = den + jnp.sum(p_prev, axis=2, keepdims=True)
            acc = acc + _dot(p_prev.astype(BF16), _blocks(vdp, u0), B_PV)
        rows = slice(u0 * BLK, (u0 + CU) * BLK)
        od[rows, :] = (acc / den).reshape(CU * BLK, HEAD_DIM)
        ld[rows, :] = jnp.broadcast_to(m + jnp.log(den), (CU, BLK, HEAD_DIM)).reshape(CU * BLK, HEAD_DIM)
    _interleave(og, od, dil)
    _interleave(lg, ld, dil)


def _group_weights(lgs):
    l0, l1, l2 = lgs[0][...], lgs[1][...], lgs[2][...]
    mx = jnp.maximum(l0, jnp.maximum(l1, l2))
    e0, e1, e2 = jnp.exp(l0 - mx), jnp.exp(l1 - mx), jnp.exp(l2 - mx)
    den = e0 + e1 + e2
    return e0 / den, e1 / den, e2 / den


def _head_spec(base, ngroups_axis=True):
    return pl.BlockSpec((S, HEAD_DIM), lambda h, p: (0, base + (p % 3) * 8 + h))


def _slab(dtype=F32, rows=S):
    return pltpu.VMEM((rows, HEAD_DIM), dtype)


def _attn_fwd(z0, qkv, ycat):
    def body(q_ref, k_ref, v_ref, gate_ref, ycat_ref, out_ref, og_ref, lg_ref,
             kdp, vdp, od, ld, og0, og1, og2, lg0, lg1, lg2):
        del ycat_ref
        p = pl.program_id(1)
        ogs, lgs = (og0, og1, og2), (lg0, lg1, lg2)
        for gi, (_, dil) in enumerate(PATTERNS):
            @pl.when(p == gi)
            def _(gi=gi, dil=dil):
                _attn_group_fwd(dil, q_ref, k_ref, v_ref, kdp, vdp, od, ld, ogs[gi], lgs[gi])
                og_ref[...] = ogs[gi][...]
                lg_ref[...] = lgs[gi][...]

        @pl.when(p == 2)
        def _():
            w0, w1, w2 = _group_weights(lgs)
            o = w0 * og0[...] + w1 * og1[...] + w2 * og2[...]
            gate = gate_ref[...]
            out_ref[...] = (o * (gate * _sigmoid(gate))).astype(BF16)

    grp = pl.BlockSpec((S, HEAD_DIM), lambda h, p: (0, p * 8 + h))
    return pl.pallas_call(
        body, grid=(8, 3),
        in_specs=[grp, grp, grp, pl.BlockSpec((S, HEAD_DIM), lambda h, p: (0, BG_COL + h)), ANY_SPEC],
        out_specs=[pl.BlockSpec((S, HEAD_DIM), lambda h, p: (0, 8 + h)), grp, grp],
        out_shape=[jax.ShapeDtypeStruct((S, 2048), BF16), jax.ShapeDtypeStruct((S, 3072), F32),
                   jax.ShapeDtypeStruct((S, 3072), F32)],
        scratch_shapes=[_slab(BF16, S + BLK), _slab(BF16, S + BLK)] + [_slab() for _ in range(8)],
        input_output_aliases={4: 0},
        compiler_params=_params(("parallel", "arbitrary"), VMEM_BIG), name="attn_fwd",
    )(*qkv, z0, ycat)


def _attn_bwd(z0, qkv, og, lg, dycat, tabs, dep):
    def body(q_ref, k_ref, v_ref, gate_ref, dy_ref, c_ref, a_ref, b_ref,
             og0_ref, og1_ref, og2_ref, lg0_ref, lg1_ref, lg2_ref, dep_ref,
             dq_ref, dk_ref, dv_ref, dbg_ref,
             tmp, kd, vd, ld, dg0, dg1, dg2, cg0, cg1, cg2, dod, cd, dqd, dkd, dvd):
        p = pl.program_id(1)
        ogs, lgs, dgs, cgs = (og0_ref, og1_ref, og2_ref), (lg0_ref, lg1_ref, lg2_ref), (dg0, dg1, dg2), (cg0, cg1, cg2)

        @pl.when(p == 0)
        def _():
            w = _group_weights(lgs)
            o = w[0] * ogs[0][...] + w[1] * ogs[1][...] + w[2] * ogs[2][...]
            silu, dsilu = _silu_and_grad(gate_ref[...])
            dy = dy_ref[...]
            dbg_ref[...] = (dy * o * dsilu).astype(BF16)
            do = dy * silu
            dwbar = jnp.sum(do * o, axis=1, keepdims=True)
            for gi in range(3):
                dgs[gi][...] = w[gi] * do
                cgs[gi][...] = -w[gi] * dwbar

        for gi, (_, dil) in enumerate(PATTERNS):
            @pl.when(p == 1 + gi)
            def _(gi=gi, dil=dil):
                nb = S // dil // BLK
                qd = q_ref
                c, a, b = c_ref[...], a_ref[...], b_ref[...]
                _pad_copy(kd, k_ref)
                _pad_copy(vd, v_ref)
                _deinterleave(dod, dgs[gi], dil, BF16)
                _deinterleave(ld, lgs[gi], dil)
                _deinterleave(cd, cgs[gi], dil)
                dkd[...] = jnp.zeros_like(dkd)
                dvd[...] = jnp.zeros_like(dvd)
                flat = lambda t: t.reshape(CU * BLK, HEAD_DIM)
                for u0 in range(0, NUNITS, CU):
                    q, s_own, s_prev = _chunk_scores(u0, nb, qd, kd)
                    lse, cv, do = _blocks(ld, u0), _blocks(cd, u0), _blocks(dod, u0)
                    own = slice((u0 + 1) * BLK, (u0 + 1 + CU) * BLK)
                    p_own = jnp.exp(s_own - lse)
                    ds_own = (p_own * (_dot(do, _blocks(vd, u0 + 1), B_QK) + cv) * SCALE).astype(BF16)
                    dq = _dot(ds_own, _blocks(kd, u0 + 1), B_PV)
                    dkd[own, :] += flat(_dot(ds_own, q, B_TN))
                    dvd[own, :] += flat(_dot(p_own.astype(BF16), do, B_TN))
                    if s_prev is not None:
                        prev = slice(u0 * BLK, (u0 + CU) * BLK)
                        p_prev = jnp.exp(s_prev - lse)
                        ds_prev = (p_prev * (_dot(do, _blocks(vd, u0), B_QK) + cv) * SCALE).astype(BF16)
                        dq = dq + _dot(ds_prev, _blocks(kd, u0), B_PV)
                        dkd[prev, :] += flat(_dot(ds_prev, q, B_TN))
                        dvd[prev, :] += flat(_dot(p_prev.astype(BF16), do, B_TN))
                    dqd[u0 * BLK:(u0 + CU) * BLK, :] = flat(dq)
                _interleave(tmp, dqd, dil)
                dq_ref[...] = _rope_t(tmp[...], c, a, b).astype(BF16)
                _interleave(tmp, dkd, dil, BLK)
                dk_ref[...] = _rope_t(tmp[...], c, a, b).astype(BF16)
                _interleave(tmp, dvd, dil, BLK)
                dv_ref[...] = tmp[...].astype(BF16)

    tab = pl.BlockSpec((S, HEAD_DIM), lambda h, p: (0, 0))
    hspec = lambda base: pl.BlockSpec((S, HEAD_DIM), lambda h, p: (0, base + h))
    gspec = pl.BlockSpec((S, HEAD_DIM), lambda h, p: (0, jnp.maximum(p - 1, 0) * 8 + h))
    return pl.pallas_call(
        body, grid=(8, 4),
        in_specs=[gspec, gspec, gspec, hspec(BG_COL), hspec(8), tab, tab, tab,
                  hspec(0), hspec(8), hspec(16), hspec(0), hspec(8), hspec(16), ANY_SPEC],
        out_specs=[gspec, gspec, gspec, hspec(0)],
        out_shape=[jax.ShapeDtypeStruct((S, 3072), BF16)] * 3 + [jax.ShapeDtypeStruct((S, HALF), BF16)],
        scratch_shapes=[_slab(), _slab(BF16, S + BLK), _slab(BF16, S + BLK), _slab()] + [_slab() for _ in range(6)]
                       + [_slab(BF16), _slab(), _slab(), _slab(F32, S + BLK), _slab(F32, S + BLK)],
        compiler_params=_params(("parallel", "arbitrary"), VMEM_BIG), name="attn_bwd",
    )(*qkv, z0, dycat, *tabs, og, og, og, lg, lg, lg, dep)


SGU_CH = 256
NCHUNK = TR // 128


def _ln_stats(x):
    mu = jnp.mean(x, axis=-1, keepdims=True)
    xc = x - mu
    r = lax.rsqrt(jnp.mean(xc * xc, axis=-1, keepdims=True) + EPS)
    return xc * r, r


def _ln_bwd(dy, xhat, r, g):
    dxh = dy * g
    return r * (dxh - jnp.mean(dxh, axis=-1, keepdims=True) - xhat * jnp.mean(dxh * xhat, axis=-1, keepdims=True))


def _tril_bf16(w):
    row = lax.broadcasted_iota(jnp.int32, w.shape, 0)
    col = lax.broadcasted_iota(jnp.int32, w.shape, 1)
    return jnp.where(row >= col, w, 0.0).astype(BF16)


def _sgu_gate(vn_s, s_s, w_ref, bb_ref):
    for h in range(4):
        wm = _tril_bf16(w_ref[h])
        bias = bb_ref[h]
        for ch in range(NCHUNK):
            rows, cols = slice(ch * 128, (ch + 1) * 128), slice(h * SGU_CH, (h + 1) * SGU_CH)
            s_s[rows, cols] = _dot(wm, vn_s[rows, cols]) + jnp.concatenate([bias, bias], axis=1)


WIN = HALO + TR
SUBL = 8


def _shifted_copies(dst, src):
    dst[0] = src[...]
    for b in range(1, SUBL):
        dst[b, 0:WIN - SUBL, :] = src[pl.ds(b, WIN - SUBL), :]


def _rows_at(copies, off, n):
    return copies[off % SUBL, pl.ds(off - off % SUBL, n), :]


def _conv_fwd(i, dval_ref, dglu_ref, hval_ref, hglu_ref, cw_ref, cb_ref, xw, xr, dcs):
    halo = hval_ref[...] * _sigmoid(hglu_ref[...])
    xw[0:HALO, :] = jnp.where(i > 0, halo, 0.0)
    xw[HALO:HALO + TR, :] = dval_ref[...] * _sigmoid(dglu_ref[...])
    _shifted_copies(xr, xw)
    for rb in range(TR // SUB):
        acc = jnp.broadcast_to(cb_ref[...], (SUB, HALF))
        for k in range(CONV_K):
            acc = acc + cw_ref[k:k + 1, :] * _rows_at(xr, rb * SUB + HALO - (CONV_K - 1) + k, SUB)
        dcs[rb * SUB:(rb + 1) * SUB, :] = acc


def _odd_in_specs():
    col = lambda j: pl.BlockSpec((TR, HALF), lambda i, *_: (i, j))
    prev = lambda j: pl.BlockSpec((HALO, HALF), lambda i, *_: (jnp.maximum(i * (TR // HALO) - 1, 0), j))
    return [col(0), col(1), col(2), col(3), col(4), col(5), prev(3), prev(4)]


def _full_spec(shape):
    return pl.BlockSpec(shape, lambda i, *_: (0,) * len(shape))


def _odd_fwd(z1, sgu_g, sgu_b, sgu_w, sgu_bb, conv_w, conv_b, cn_g, cn_b):
    def body(u_ref, v_ref, cg_ref, dval_ref, dglu_ref, dgate_ref, hval_ref, hglu_ref,
             g_ref, b_ref, w_ref, bb_ref, cw_ref, cb_ref, cng_ref, cnb_ref, out_ref, vn_s, s_s, xw, dcs, xr):
        i = pl.program_id(0)
        vhat, _ = _ln_stats(v_ref[...])
        vn_s[...] = (vhat * g_ref[...] + b_ref[...]).astype(BF16)
        _sgu_gate(vn_s, s_s, w_ref, bb_ref)
        cg = cg_ref[...]
        out_ref[:, 0:HALF] = (u_ref[...] * s_s[...] * (cg * _sigmoid(cg))).astype(BF16)
        _conv_fwd(i, dval_ref, dglu_ref, hval_ref, hglu_ref, cw_ref, cb_ref, xw, xr, dcs)
        dhat, _ = _ln_stats(dcs[...])
        dn = dhat * cng_ref[...] + cnb_ref[...]
        dgate = dgate_ref[...]
        out_ref[:, HALF:2 * HALF] = ((dn * _sigmoid(dn)) * (dgate * _sigmoid(dgate))).astype(BF16)

    vec = _full_spec((1, HALF))
    return pl.pallas_call(
        body, grid=(S // TR,),
        in_specs=_odd_in_specs() + [vec, vec, _full_spec((4, 128, 128)), _full_spec((4, 128, 128)),
                                    _full_spec((HALO, HALF)), vec, vec, vec],
        out_specs=pl.BlockSpec((TR, 2048), lambda i: (i, 0)),
        out_shape=jax.ShapeDtypeStruct((S, 2048), BF16),
        scratch_shapes=[pltpu.VMEM((TR, HALF), BF16), pltpu.VMEM((TR, HALF), F32),
                        pltpu.VMEM((WIN, HALF), F32), pltpu.VMEM((TR, HALF), F32), pltpu.VMEM((SUBL, WIN, HALF), F32)],
        compiler_params=_params(("parallel",), VMEM_BIG), name="odd_fwd",
    )(z1, z1, z1, z1, z1, z1, z1, z1, sgu_g, sgu_b, sgu_w, sgu_bb, conv_w, conv_b, cn_g, cn_b)


def _odd_bwd_a(z1, dycat, sgu_g, sgu_b, sgu_w, sgu_bb, conv_w, conv_b, cn_g, cn_b):
    def body(u_ref, v_ref, cg_ref, dval_ref, dglu_ref, dgate_ref, hval_ref, hglu_ref, dy_ref,
             g_ref, b_ref, w_ref, bb_ref, cw_ref, cb_ref, cng_ref, cnb_ref,
             dz_ref, ddc_ref, dw_ref, dbb_ref, dg_ref, db_ref, dcng_ref, dcnb_ref, dcb_ref,
             vn_s, s_s, xw, dcs, ds_s, dvn_s, xr):
        i = pl.program_id(0)
        vhat, rv = _ln_stats(v_ref[...])
        g = g_ref[...]
        vn_s[...] = (vhat * g + b_ref[...]).astype(BF16)
        _sgu_gate(vn_s, s_s, w_ref, bb_ref)
        silu_c, dsilu_c = _silu_and_grad(cg_ref[...])
        dyc = dy_ref[:, 0:HALF]
        u = u_ref[...]
        s = s_s[...]
        dz_ref[:, 0:HALF] = (dyc * s * silu_c).astype(BF16)
        dz_ref[:, 2 * HALF:3 * HALF] = (dyc * u * s * dsilu_c).astype(BF16)
        ds_s[...] = dyc * u * silu_c

        @pl.when(i == 0)
        def _():
            dw_ref[...] = jnp.zeros_like(dw_ref)
            dbb_ref[...] = jnp.zeros_like(dbb_ref)

        tril = lax.broadcasted_iota(jnp.int32, (128, 128), 0) >= lax.broadcasted_iota(jnp.int32, (128, 128), 1)
        for h in range(4):
            wm = _tril_bf16(w_ref[h])
            for ch in range(NCHUNK):
                rows, cols = slice(ch * 128, (ch + 1) * 128), slice(h * SGU_CH, (h + 1) * SGU_CH)
                ds = ds_s[rows, cols]
                dsb = ds.astype(BF16)
                dw_ref[h] += jnp.where(tril, _dot(dsb, vn_s[rows, cols], NT), 0.0)
                dbb_ref[h] += jnp.broadcast_to(jnp.sum(ds, axis=1, keepdims=True), (128, 128))
                dvn_s[rows, cols] = _dot(wm, dsb, TN)
        dvn = dvn_s[...]
        _acc_rows(dg_ref, dvn * vhat, i)
        _acc_rows(db_ref, dvn, i)
        dz_ref[:, HALF:2 * HALF] = _ln_bwd(dvn, vhat, rv, g).astype(BF16)

        _conv_fwd(i, dval_ref, dglu_ref, hval_ref, hglu_ref, cw_ref, cb_ref, xw, xr, dcs)
        dhat, rd = _ln_stats(dcs[...])
        cng = cng_ref[...]
        silu_n, dsilu_n = _silu_and_grad(dhat * cng + cnb_ref[...])
        silu_g, dsilu_g = _silu_and_grad(dgate_ref[...])
        dyd = dy_ref[:, HALF:2 * HALF]
        dz_ref[:, 5 * HALF:6 * HALF] = (dyd * silu_n * dsilu_g).astype(BF16)
        ddn = dyd * silu_g * dsilu_n
        _acc_rows(dcng_ref, ddn * dhat, i)
        _acc_rows(dcnb_ref, ddn, i)
        ddc = _ln_bwd(ddn, dhat, rd, cng)
        ddc_ref[...] = ddc
        _acc_rows(dcb_ref, ddc, i)

    vec = _full_spec((1, HALF))
    sq = _full_spec((4, 128, 128))
    return pl.pallas_call(
        body, grid=(S // TR,),
        in_specs=_odd_in_specs() + [pl.BlockSpec((TR, 2048), lambda i: (i, 0)),
                                    vec, vec, sq, sq, _full_spec((HALO, HALF)), vec, vec, vec],
        out_specs=[pl.BlockSpec((TR, ODD_IN), lambda i: (i, 0)), pl.BlockSpec((TR, HALF), lambda i: (i, 0)),
                   sq, sq, vec, vec, vec, vec, vec],
        out_shape=[jax.ShapeDtypeStruct((S, ODD_IN), BF16), jax.ShapeDtypeStruct((S, HALF), F32),
                   jax.ShapeDtypeStruct((4, 128, 128), F32), jax.ShapeDtypeStruct((4, 128, 128), F32)]
                  + [jax.ShapeDtypeStruct((1, HALF), F32)] * 5,
        scratch_shapes=[pltpu.VMEM((TR, HALF), BF16), pltpu.VMEM((TR, HALF), F32),
                        pltpu.VMEM((WIN, HALF), F32), pltpu.VMEM((TR, HALF), F32),
                        pltpu.VMEM((TR, HALF), F32), pltpu.VMEM((TR, HALF), F32), pltpu.VMEM((SUBL, WIN, HALF), F32)],
        compiler_params=_params(("arbitrary",), VMEM_BIG), name="odd_bwd_a",
    )(z1, z1, z1, z1, z1, z1, z1, z1, dycat, sgu_g, sgu_b, sgu_w, sgu_bb, conv_w, conv_b, cn_g, cn_b)


def _odd_bwd_b(z1, ddc, dz1, conv_w):
    nt = S // TR

    def body(dval_ref, dglu_ref, hval_ref, hglu_ref, ddc_ref, hddc_ref, cw_ref, dz_in_ref,
             dz_ref, dcw_ref, xw, dwin, dxs, xr, dr):
        del dz_in_ref
        i, j = pl.program_id(0), pl.program_id(1)
        sg = _sigmoid(dglu_ref[...])
        dval = dval_ref[...]

        @pl.when(j == 0)
        def _():
            halo = hval_ref[...] * _sigmoid(hglu_ref[...])
            xw[0:HALO, :] = jnp.where(i > 0, halo, 0.0)
            xw[HALO:HALO + TR, :] = dval * sg
            dwin[0:TR, :] = ddc_ref[...]
            dwin[TR:TR + HALO, :] = jnp.where(i < nt - 1, hddc_ref[...], 0.0)
            _shifted_copies(xr, xw)
            _shifted_copies(dr, dwin)

            @pl.when(i == 0)
            def _():
                dcw_ref[...] = jnp.zeros_like(dcw_ref)

            for rb in range(TR // SUB):
                acc = jnp.zeros((SUB, HALF), F32)
                for k in range(CONV_K):
                    acc = acc + cw_ref[k:k + 1, :] * _rows_at(dr, rb * SUB + (CONV_K - 1) - k, SUB)
                dxs[rb * SUB:(rb + 1) * SUB, :] = acc
            for k in range(CONV_K):
                acc = jnp.zeros((SUB, HALF), F32)
                for rb in range(TR // SUB):
                    acc = acc + dwin[rb * SUB:(rb + 1) * SUB, :] * _rows_at(xr, rb * SUB + HALO - (CONV_K - 1) + k, SUB)
                dcw_ref[k:k + 1, :] += jnp.sum(acc, axis=0, keepdims=True)
            dz_ref[...] = (dxs[...] * sg).astype(BF16)

        @pl.when(j == 1)
        def _():
            dz_ref[...] = (dxs[...] * dval * sg * (1.0 - sg)).astype(BF16)

    col = lambda c: pl.BlockSpec((TR, HALF), lambda i, j: (i, c))
    prev = lambda c: pl.BlockSpec((HALO, HALF), lambda i, j: (jnp.maximum(i * (TR // HALO) - 1, 0), c))
    nxt = pl.BlockSpec((HALO, HALF), lambda i, j: (jnp.minimum((i + 1) * (TR // HALO), S // HALO - 1), 0))
    return pl.pallas_call(
        body, grid=(nt, 2),
        in_specs=[col(3), col(4), prev(3), prev(4), pl.BlockSpec((TR, HALF), lambda i, j: (i, 0)), nxt,
                  _full_spec((HALO, HALF)), pl.BlockSpec(memory_space=pl.ANY)],
        out_specs=[pl.BlockSpec((TR, HALF), lambda i, j: (i, 3 + j)), _full_spec((HALO, HALF))],
        out_shape=[jax.ShapeDtypeStruct((S, ODD_IN), BF16), jax.ShapeDtypeStruct((HALO, HALF), F32)],
        scratch_shapes=[pltpu.VMEM((WIN, HALF), F32), pltpu.VMEM((WIN, HALF), F32), pltpu.VMEM((TR, HALF), F32),
                        pltpu.VMEM((SUBL, WIN, HALF), F32), pltpu.VMEM((SUBL, WIN, HALF), F32)],
        input_output_aliases={7: 0},
        compiler_params=_params(("arbitrary", "arbitrary"), VMEM_BIG), name="odd_bwd_b",
    )(z1, z1, z1, z1, ddc, ddc, conv_w, dz1)


def _cast_bf16(w, name, piece=0, npieces=1):
    r, c = w.shape[0], w.shape[1] // npieces
    tr = min(r, 256)

    def body(i_ref, o_ref):
        o_ref[...] = i_ref[...].astype(BF16)

    return pl.pallas_call(
        body, grid=(r // tr,), in_specs=[pl.BlockSpec((tr, c), lambda i: (i, piece))],
        out_specs=pl.BlockSpec((tr, c), lambda i: (i, 0)), out_shape=jax.ShapeDtypeStruct((r, c), BF16),
        compiler_params=_params(("parallel",)), name=name,
    )(w)


def _adamw(w, g, m, v):
    m = ADAM_B1 * m + (1.0 - ADAM_B1) * g
    v = ADAM_B2 * v + (1.0 - ADAM_B2) * (g * g)
    m_hat = m / (1.0 - ADAM_B1 ** ADAM_STEP)
    v_hat = v / (1.0 - ADAM_B2 ** ADAM_STEP)
    delta = -ADAM_LR * (m_hat / (jnp.sqrt(v_hat) + ADAM_EPS) + ADAM_WD * w)
    return delta, m, v


def _adam_reduce(parts, w, m, v, name, dep=None):
    r, c = w.shape
    tr = min(r, 128)
    deps = [] if dep is None else [dep]
    pieces = list(parts) if isinstance(parts, (list, tuple)) else [parts]
    npieces, nparts = len(pieces), pieces[0].shape[0]

    def body(*refs):
        p_refs = refs[:npieces]
        w_ref, m_ref, v_ref = refs[npieces:npieces + 3]
        g_ref, d_ref, nm_ref, nv_ref = refs[npieces + 3 + len(deps):]
        cols = []
        for p_ref in p_refs:
            acc = p_ref[0].astype(F32)
            for d in range(1, nparts):
                acc = acc + p_ref[d].astype(F32)
            cols.append(acc)
        g = cols[0] if npieces == 1 else jnp.concatenate(cols, axis=1)
        g_ref[...] = g
        d_ref[...], nm_ref[...], nv_ref[...] = _adamw(w_ref[...], g, m_ref[...], v_ref[...])

    spec = pl.BlockSpec((tr, c), lambda i: (i, 0))
    return pl.pallas_call(
        body, grid=(r // tr,),
        in_specs=[pl.BlockSpec((nparts, tr, a.shape[2]), lambda i: (0, i, 0)) for a in pieces] + [spec, spec, spec]
                 + [ANY_SPEC] * len(deps),
        out_specs=[spec] * 4, out_shape=[jax.ShapeDtypeStruct((r, c), F32)] * 4,
        compiler_params=_params(("parallel",), VMEM_BIG), name=name,
    )(*pieces, w, m, v, *deps)


def _arrived(x, name, dep=None):
    deps = [] if dep is None else [dep]

    def body(*refs):
        refs[-1][...] = jnp.zeros_like(refs[-1])

    return pl.pallas_call(
        body, in_specs=[ANY_SPEC] * (1 + len(deps)), out_specs=pl.BlockSpec(memory_space=pltpu.VMEM),
        out_shape=jax.ShapeDtypeStruct((8, 128), F32), name=name,
    )(x, *deps)


def _sum_parts(parts, name, dep=None):
    r = parts.shape[1]
    tr = 8
    for cand in (512, 256, 128, 64, 32, 16, 8):
        if r % cand == 0:
            tr = cand
            break
    deps = [] if dep is None else [dep]

    def body(p_ref, *rest):
        g = p_ref[0]
        for d in range(1, NDEV):
            g = g + p_ref[d]
        rest[-1][...] = g

    return pl.pallas_call(
        body, grid=(r // tr,), in_specs=[pl.BlockSpec((NDEV, tr, 128), lambda i: (0, i, 0))] + [ANY_SPEC] * len(deps),
        out_specs=pl.BlockSpec((tr, 128), lambda i: (i, 0)), out_shape=jax.ShapeDtypeStruct((r, 128), F32),
        compiler_params=_params(("parallel",)), name=name,
    )(parts, *deps)


def _sum_unpack(parts, rows, name, dep=None):
    deps = [] if dep is None else [dep]

    def body(p_ref, *outs):
        outs = outs[len(deps):]
        off = 0
        for o_ref, n in zip(outs, rows):
            acc = p_ref[0, off:off + n, :]
            for d in range(1, NDEV):
                acc = acc + p_ref[d, off:off + n, :]
            o_ref[...] = acc
            off += n

    return pl.pallas_call(
        body, grid=(1,), in_specs=[pl.BlockSpec(parts.shape, lambda i: (0, 0, 0))] + [ANY_SPEC] * len(deps),
        out_specs=[pl.BlockSpec((n, 128), lambda i: (0, 0)) for n in rows],
        out_shape=[jax.ShapeDtypeStruct((n, 128), F32) for n in rows],
        compiler_params=_params(("arbitrary",), VMEM_BIG), name=name,
    )(parts, *deps)


def _adam_small(ws, gs, g_specs, ms, vs, name):
    n = len(ws)

    def body(*refs):
        w_r, g_r, m_r, v_r = refs[:n], refs[n:2 * n], refs[2 * n:3 * n], refs[3 * n:4 * n]
        outs = refs[4 * n:]
        for i in range(n):
            g = g_r[i][...]
            outs[4 * i][...] = g
            outs[4 * i + 1][...], outs[4 * i + 2][...], outs[4 * i + 3][...] = _adamw(
                w_r[i][...], g, m_r[i][...], v_r[i][...])

    whole = lambda a: pl.BlockSpec(a.shape, lambda i, nd=a.ndim: (0,) * nd)
    outs = pl.pallas_call(
        body, grid=(1,),
        in_specs=[whole(a) for a in ws] + list(g_specs) + [whole(a) for a in ms] + [whole(a) for a in vs],
        out_specs=[whole(a) for a in ws for _ in range(4)],
        out_shape=[jax.ShapeDtypeStruct(a.shape, F32) for a in ws for _ in range(4)],
        compiler_params=_params(("arbitrary",), VMEM_BIG), name=name,
    )(*ws, *gs, *ms, *vs)
    return [outs[4 * i:4 * i + 4] for i in range(n)]


MASKS = [(mx, my, mc) for mx in (0, 1) for my in (0, 1) for mc in (0, 1)][1:]


def _sc_exchange(name, collective_id, arrays, scatter):
    nt = len(arrays)
    out_type = [jax.ShapeDtypeStruct(a.shape if scatter else (NDEV,) + a.shape, a.dtype) for a in arrays]

    def body(*refs):
        ins, outs = refs[:nt], refs[nt:2 * nt]
        send_sems, recv_sems, local_sems = refs[2 * nt:3 * nt], refs[3 * nt:4 * nt], refs[4 * nt:5 * nt]
        x, y, c = lax.axis_index("x"), lax.axis_index("y"), lax.axis_index("c")
        peers = [(mx + x - 2 * mx * x, my + y - 2 * my * y, mc + c - 2 * mc * c) for mx, my, mc in MASKS]
        barrier = pltpu.get_barrier_semaphore()
        for peer in peers:
            pl.semaphore_signal(barrier, inc=1, device_id=peer, device_id_type=MESH)
        pl.semaphore_wait(barrier, len(peers))
        me = 4 * x + 2 * y + c
        own = []
        for t in range(nt):
            cp = pltpu.make_async_copy(ins[t].at[me] if scatter else ins[t], outs[t].at[me], local_sems[t])
            cp.start()
            own.append(cp)
            for px, py, pc in peers:
                src = ins[t].at[4 * px + 2 * py + pc] if scatter else ins[t]
                pltpu.make_async_remote_copy(src_ref=src, dst_ref=outs[t].at[me], send_sem=send_sems[t],
                                             recv_sem=recv_sems[t], device_id=(px, py, pc), device_id_type=MESH).start()
        for t in range(nt):
            own[t].wait()
            seven = outs[t].at[pl.ds(0, NDEV - 1)]
            drain = pltpu.make_async_remote_copy(src_ref=seven, dst_ref=seven, send_sem=send_sems[t],
                                                 recv_sem=recv_sems[t], device_id=(x, y, c), device_id_type=MESH)
            drain.wait_send()
            drain.wait_recv()

    return pl.kernel(
        body, out_type=out_type, mesh=plsc.ScalarSubcoreMesh(axis_name="sequencer", num_cores=1),
        scratch_types=[pltpu.SemaphoreType.DMA] * (3 * nt),
        compiler_params=pltpu.CompilerParams(collective_id=collective_id), name=name,
    )(*arrays)


def _sc_gather_two_level(name, collective_id, arrays):
    nt = len(arrays)
    out_type = [jax.ShapeDtypeStruct((NDEV,) + a.shape, a.dtype) for a in arrays]

    def body(*refs):
        ins, outs = refs[:nt], refs[nt:2 * nt]
        sems = refs[2 * nt:]
        send_sems, sib_sems, local_sems = sems[:nt], sems[nt:2 * nt], sems[2 * nt:3 * nt]
        ici_sems = [sems[3 * nt + 3 * t:3 * nt + 3 * t + 3] for t in range(nt)]
        x, y, c = lax.axis_index("x"), lax.axis_index("y"), lax.axis_index("c")
        sibling = (x, y, 1 - c)
        chips = [(1 - x, y), (x, 1 - y), (1 - x, 1 - y)]
        barrier = pltpu.get_barrier_semaphore()
        for peer in [sibling] + [(cx, cy, c) for cx, cy in chips]:
            pl.semaphore_signal(barrier, inc=1, device_id=peer, device_id_type=MESH)
        pl.semaphore_wait(barrier, 4)
        me = 4 * x + 2 * y + c

        def push(t, src, slot, recv_sem, to):
            pltpu.make_async_remote_copy(src_ref=src, dst_ref=outs[t].at[slot], send_sem=send_sems[t],
                                         recv_sem=recv_sem, device_id=to, device_id_type=MESH).start()

        own = []
        for t in range(nt):
            cp = pltpu.make_async_copy(ins[t], outs[t].at[me], local_sems[t])
            cp.start()
            own.append(cp)
            for j, (cx, cy) in enumerate(chips):
                push(t, ins[t], me, ici_sems[t][j], (cx, cy, c))
            push(t, ins[t], me, sib_sems[t], sibling)
        for t in range(nt):
            for j, (cx, cy) in enumerate(chips):
                slot = 4 * cx + 2 * cy + c
                landed = outs[t].at[slot]
                pltpu.make_async_remote_copy(src_ref=landed, dst_ref=landed, send_sem=send_sems[t],
                                             recv_sem=ici_sems[t][j], device_id=(cx, cy, c),
                                             device_id_type=MESH).wait_recv()
                push(t, landed, slot, sib_sems[t], sibling)
        for t in range(nt):
            own[t].wait()
            four, seven = outs[t].at[pl.ds(0, 4)], outs[t].at[pl.ds(0, 7)]
            pltpu.make_async_remote_copy(src_ref=four, dst_ref=four, send_sem=send_sems[t], recv_sem=sib_sems[t],
                                         device_id=sibling, device_id_type=MESH).wait_recv()
            pltpu.make_async_remote_copy(src_ref=seven, dst_ref=seven, send_sem=send_sems[t], recv_sem=sib_sems[t],
                                         device_id=sibling, device_id_type=MESH).wait_send()

    return pl.kernel(
        body, out_type=out_type, mesh=plsc.ScalarSubcoreMesh(axis_name="sequencer", num_cores=1),
        scratch_types=[pltpu.SemaphoreType.DMA] * (6 * nt),
        compiler_params=pltpu.CompilerParams(collective_id=collective_id), name=name,
    )(*arrays)


def _sc_sibling_exchange(name, collective_id, src, out_shape, pieces):
    def body(src_ref, out_ref, send_sem, recv_sem):
        x, y, c = lax.axis_index("x"), lax.axis_index("y"), lax.axis_index("c")
        sibling = (x, y, 1 - c)
        barrier = pltpu.get_barrier_semaphore()
        pl.semaphore_signal(barrier, inc=1, device_id=sibling, device_id_type=MESH)
        pl.semaphore_wait(barrier, 1)
        for piece, lands in pieces(c, src_ref, out_ref):
            pltpu.make_async_remote_copy(src_ref=piece, dst_ref=lands, send_sem=send_sem, recv_sem=recv_sem,
                                         device_id=sibling, device_id_type=MESH).start()
        drain = pltpu.make_async_remote_copy(src_ref=out_ref, dst_ref=out_ref, send_sem=send_sem, recv_sem=recv_sem,
                                             device_id=sibling, device_id_type=MESH)
        drain.wait_send()
        drain.wait_recv()

    return pl.kernel(
        body, out_type=jax.ShapeDtypeStruct(out_shape, src.dtype),
        mesh=plsc.ScalarSubcoreMesh(axis_name="sequencer", num_cores=1), scratch_types=[pltpu.SemaphoreType.DMA] * 2,
        compiler_params=pltpu.CompilerParams(collective_id=collective_id), name=name,
    )(src)


def _swap_class_columns(name, collective_id, dz, nb):
    return _sc_sibling_exchange(
        name, collective_id, dz, (S, 4 * nb),
        lambda c, src, out: [(src.at[:, pl.ds((2 * j + 1 - c) * nb, nb)], out.at[:, pl.ds(j * nb, nb)])
                             for j in range(4)])


def _sc_chip_scatter(name, collective_id, q):
    def body(q_ref, out_ref, send_sem, recv_sem, local_sem):
        x, y, c = lax.axis_index("x"), lax.axis_index("y"), lax.axis_index("c")
        chips = [(1 - x, y), (x, 1 - y), (1 - x, 1 - y)]
        barrier = pltpu.get_barrier_semaphore()
        for cx, cy in chips:
            pl.semaphore_signal(barrier, inc=1, device_id=(cx, cy, c), device_id_type=MESH)
        pl.semaphore_wait(barrier, 3)
        mine = 2 * x + y
        own = pltpu.make_async_copy(q_ref.at[mine], out_ref.at[mine], local_sem)
        own.start()
        for cx, cy in chips:
            pltpu.make_async_remote_copy(src_ref=q_ref.at[2 * cx + cy], dst_ref=out_ref.at[mine], send_sem=send_sem,
                                         recv_sem=recv_sem, device_id=(cx, cy, c), device_id_type=MESH).start()
        own.wait()
        three = out_ref.at[pl.ds(0, 3)]
        drain = pltpu.make_async_remote_copy(src_ref=three, dst_ref=three, send_sem=send_sem, recv_sem=recv_sem,
                                             device_id=(x, y, c), device_id_type=MESH)
        drain.wait_send()
        drain.wait_recv()

    return pl.kernel(
        body, out_type=jax.ShapeDtypeStruct(q.shape, q.dtype),
        mesh=plsc.ScalarSubcoreMesh(axis_name="sequencer", num_cores=1), scratch_types=[pltpu.SemaphoreType.DMA] * 3,
        compiler_params=pltpu.CompilerParams(collective_id=collective_id), name=name,
    )(q)


def _mm_pair_dw(h_own, dz, h_sib, dz_sib, nb, name, dep=None):
    tn = 512 if nb % 512 == 0 else nb
    per = nb // tn
    o_spec = pl.BlockSpec((None, D, tn), lambda i, j, k: (j // per, 0, j % per))
    part = _matmul(
        h_own, dz, dn=TN, grid=(1, 4 * per, 1),
        a_spec=pl.BlockSpec((S, D), lambda i, j, k: (0, 0)),
        b_spec=pl.BlockSpec((S, tn), lambda i, j, k: (0, (2 * (j // per) + lax.axis_index("c")) * per + j % per)),
        o_spec=o_spec, out_shape=(4, D, nb), out_dtype=F32, acc_shape=(D, tn), name=name + "_own", dep=dep)

    def body(a_ref, b_ref, p_ref, o_ref):
        o_ref[...] = (p_ref[...] + _dot(a_ref[...], b_ref[...], TN)).astype(BF16)

    return pl.pallas_call(
        body, grid=(1, 4 * per, 1),
        in_specs=[pl.BlockSpec((S, D), lambda i, j, k: (0, 0)), pl.BlockSpec((S, tn), lambda i, j, k: (0, j)), o_spec],
        out_specs=o_spec, out_shape=jax.ShapeDtypeStruct((4, D, nb), BF16),
        compiler_params=_params(("parallel", "parallel", "arbitrary"), VMEM_BIG), name=name + "_sibling",
    )(h_sib, dz_sib, part)


SMALL = {
    "e_pre_norm": ((2048,), None), "e_pool_w": ((4, 256, 256), 1), "e_pool_scale": ((1024,), None),
    "e_post_norm": ((2048,), None), "o_pre_norm": ((2048,), 0), "o_sgu_norm_g": ((1024,), 0),
    "o_sgu_norm_b": ((1024,), 0), "o_sgu_w": ((4, 128, 128), None), "o_sgu_b": ((4, 128), None),
    "o_conv_w": ((31, 1024), 1), "o_conv_b": ((1024,), 0), "o_conv_norm_g": ((1024,), 0),
    "o_conv_norm_b": ((1024,), 0), "o_post_norm": ((2048,), 0),
}
SMALL_SHARDED = [n for n, (_, ax) in SMALL.items() if ax is not None]


def _shard_shape(name):
    shape, ax = SMALL[name]
    if ax is None:
        return shape
    return tuple(s // NDEV if i == ax else s for i, s in enumerate(shape))


def _pack(arrs, row_multiple=1):
    flat = jnp.concatenate([a.reshape(-1) for a in arrs])
    pad = -flat.shape[0] % (128 * row_multiple)
    return jnp.concatenate([flat, jnp.zeros((pad,), F32)]).reshape(-1, 128)


def _small_views(name):
    shape, ax = SMALL[name]
    me = lambda: 4 * lax.axis_index("x") + 2 * lax.axis_index("y") + lax.axis_index("c")
    if ax is None:
        view = (int(np.prod(shape)) // 128, 128)
        return view, view, pl.BlockSpec(view, lambda i: (0, 0))
    if len(shape) == 1:
        n = shape[0] // NDEV
        return (1, n), (NDEV, 1, n), pl.BlockSpec((None, 1, n), lambda i: (me(), 0, 0))
    part = _shard_shape(name)
    return part, shape, pl.BlockSpec(part, lambda i: tuple(me() if d == ax else 0 for d in range(len(shape))))


BIG = ("e_w_in", "e_w_out", "o_w_in", "o_w_out")
WEIGHTS = ["e_pre_norm", "e_w_in", "e_pool_w", "e_pool_scale", "e_w_out", "e_post_norm", "o_pre_norm", "o_w_in",
           "o_sgu_norm_g", "o_sgu_norm_b", "o_sgu_w", "o_sgu_b", "o_conv_w", "o_conv_b", "o_conv_norm_g",
           "o_conv_norm_b", "o_w_out", "o_post_norm"]


def kernel(x, e_pre_norm, e_w_in, e_pool_w, e_pool_scale, e_w_out, e_post_norm, o_pre_norm, o_w_in, o_sgu_norm_g, o_sgu_norm_b, o_sgu_w, o_sgu_b, o_conv_w, o_conv_b, o_conv_norm_g, o_conv_norm_b, o_w_out, o_post_norm, loss_target, m_e_pre_norm, m_e_w_in, m_e_pool_w, m_e_pool_scale, m_e_w_out, m_e_post_norm, m_o_pre_norm, m_o_w_in, m_o_sgu_norm_g, m_o_sgu_norm_b, m_o_sgu_w, m_o_sgu_b, m_o_conv_w, m_o_conv_b, m_o_conv_norm_g, m_o_conv_norm_b, m_o_w_out, m_o_post_norm, v_e_pre_norm, v_e_w_in, v_e_pool_w, v_e_pool_scale, v_e_w_out, v_e_post_norm, v_o_pre_norm, v_o_w_in, v_o_sgu_norm_g, v_o_sgu_norm_b, v_o_sgu_w, v_o_sgu_b, v_o_conv_w, v_o_conv_b, v_o_conv_norm_g, v_o_conv_norm_b, v_o_w_out, v_o_post_norm):
    given = dict(locals())
    w = {n: given[n][0] for n in WEIGHTS}
    m = {n: given["m_" + n][0] for n in WEIGHTS}
    v = {n: given["v_" + n][0] for n in WEIGHTS}
    me = 4 * lax.axis_index("x") + 2 * lax.axis_index("y") + lax.axis_index("c")
    x, target = x[0], loss_target[0]
    row = lambda a: a.reshape(1, -1)

    lo, small_rows = _sc_gather_two_level(
        "gather_a0", 0, [_cast_bf16(w["e_w_in"], "cast_e_w_in_0", 0, 2), _pack([w[n] for n in SMALL_SHARDED])])
    hi, = _sc_gather_two_level("gather_a1", 12, [_cast_bf16(w["e_w_in"], "cast_e_w_in_1", 1, 2)])
    wg_e_in = (lo, hi)
    h0 = _pre0_fwd(x, row(w["e_pre_norm"]))
    wg_e_out, wg_o_in, wg_o_out = _sc_gather_two_level(
        "gather_b", 1, [_cast_bf16(w[n], "cast_" + n) for n in ("e_w_out", "o_w_in", "o_w_out")])
    h0_sib = _sc_sibling_exchange("swap_h0", 8, h0, h0.shape, lambda c, src, out: [(src, out)])
    p = {n: w[n] for n in SMALL if SMALL[n][1] is None}
    small_rows = small_rows.reshape(NDEV, -1)
    off = 0
    for n in SMALL_SHARDED:
        shp, ax = _shard_shape(n), SMALL[n][1]
        cnt = int(np.prod(shp))
        blk = small_rows[:, off:off + cnt].reshape((NDEV,) + shp)
        p[n] = jnp.moveaxis(blk, 0, ax).reshape(SMALL[n][0])
        off += cnt
    tabs = _rope_tables()
    pool_w_bf = p["e_pool_w"].astype(BF16)
    sgu_bb = jnp.broadcast_to(p["o_sgu_b"][:, :, None], (4, 128, 128))
    conv_w = jnp.concatenate([p["o_conv_w"], jnp.zeros((HALO - CONV_K, HALF), F32)], axis=0)
    odd_p = (row(p["o_sgu_norm_g"]), row(p["o_sgu_norm_b"]), p["o_sgu_w"], sgu_bb, conv_w,
             row(p["o_conv_b"]), row(p["o_conv_norm_g"]), row(p["o_conv_norm_b"]))

    z0 = _mm_in_halves(h0, wg_e_in, "mm_z0")
    ycat0 = _pool_fwd(z0, pool_w_bf, row(p["e_pool_scale"]))
    qkv = _qkv_prep(z0, tabs)
    ycat0, og, lg = _attn_fwd(z0, qkv, ycat0)
    w_out_e, w_out_o = wg_e_out.reshape(2048, D), wg_o_out.reshape(2048, D)
    y0 = _mm_out(ycat0, w_out_e, "mm_y0", h0_sib)
    x1, h1 = _post0_fwd(x, y0, row(p["e_post_norm"]), row(p["o_pre_norm"]))
    h1_sib = _sc_sibling_exchange("swap_h1", 11, h1, h1.shape, lambda c, src, out: [(src, out)])
    z1 = _mm_in(h1, wg_o_in, "mm_z1")
    ycat1 = _odd_fwd(z1, *odd_p)
    y1 = _mm_out(ycat1, w_out_o, "mm_y1", h1_sib)

    g = {}
    loss, dx2, dy1, g["o_post_norm"] = _post1_bwd(y1, x1, target, row(p["o_post_norm"]))
    loss = lax.psum(loss[0, 0], ("x", "y", "c"))
    parts = {}
    dw = _mm_out_dw(ycat1, dy1, "mm_dwout1").reshape(NDEV, 256, D)
    parts["o_w_out"], = _sc_exchange("scatter_o_w_out", 2, [dw], True)
    dycat1 = _mm_out_dx(dy1, w_out_o, "mm_dycat1", (dw, loss.reshape(1, 1)))
    dz1, ddc, g["o_sgu_w"], d_sgu_bb, g["o_sgu_norm_g"], g["o_sgu_norm_b"], g["o_conv_norm_g"], \
        g["o_conv_norm_b"], g["o_conv_b"] = _odd_bwd_a(z1, dycat1, *odd_p)
    dz1, d_conv_w = _odd_bwd_b(z1, ddc, dz1, conv_w)
    g["o_sgu_b"] = d_sgu_bb[:, :, 0]
    g["o_conv_w"] = d_conv_w[:CONV_K]
    grads, deltas, new_m, new_v = {}, {}, {}, {}

    def adam(n, dep):
        grads[n], deltas[n], new_m[n], new_v[n] = _adam_reduce(parts[n], w[n], m[n], v[n], "adam_" + n, dep)
        return new_v[n]

    pin = _arrived(parts["o_w_out"], "arrived_o_w_out", d_conv_w)
    dz1_sib = _swap_class_columns("swap_dz1", 10, dz1, ODD_IN // NDEV)
    dw = _mm_pair_dw(h1, dz1, h1_sib, dz1_sib, ODD_IN // NDEV, "mm_dwin1", pin)
    parts["o_w_in"] = _sc_chip_scatter("scatter_o_w_in", 3, dw)
    dh1 = _mm_in_dx(dz1, wg_o_in, "mm_dh1", dw)
    dx1, dy0, g["o_pre_norm"], g["e_post_norm"] = _mid_bwd(dx2, dh1, x1, y0, row(p["o_pre_norm"]),
                                                           row(p["e_post_norm"]))
    dw = _mm_out_dw(ycat0, dy0, "mm_dwout0").reshape(NDEV, 256, D)
    parts["e_w_out"], = _sc_exchange("scatter_e_w_out", 4, [dw], True)
    dycat0 = _mm_out_dx(dy0, w_out_e, "mm_dycat0", dw)
    da_in, da_gate, g["e_pool_w"], g["e_pool_scale"] = _pool_bwd(z0, dycat0, pool_w_bf, row(p["e_pool_scale"]))
    late = [n for n in SMALL if n not in ("e_pre_norm", "o_sgu_b")] + ["o_sgu_b"]
    recv_small, = _sc_gather_two_level("gather_small_grads", 6,
                                       [_pack([g[n].reshape(SMALL[n][0]) for n in late], 512)])
    took = _arrived(parts["o_w_in"], "arrived_o_w_in")
    dq, dk, dv, dbg = _attn_bwd(z0, qkv, og, lg, dycat0, tabs, took)
    dz0 = jnp.concatenate([da_in, da_gate, dq, dk, dv, dbg], axis=1)
    took = _arrived(recv_small, "arrived_small_grads", _arrived(parts["e_w_out"], "arrived_e_w_out", dz0))
    dz0_sib = _swap_class_columns("swap_dz0", 9, dz0, EVEN_IN // NDEV)
    dw = _mm_pair_dw(h0, dz0, h0_sib, dz0_sib, EVEN_IN // NDEV, "mm_dwin0", took)
    parts["e_w_in"] = _sc_chip_scatter("scatter_e_w_in", 5, dw)
    pin = adam("e_w_out", adam("o_w_out", adam("o_w_in", dw)))
    rows = [int(np.prod(SMALL[n][0])) // 128 for n in late]
    summed = dict(zip(late, _sum_unpack(recv_small, rows, "sum_small_grads", pin)))
    dh0 = _mm_in_dx_halves(dz0, wg_e_in, "mm_dh0", summed[late[0]])
    grad_x, g["e_pre_norm"] = _pre0_bwd(dx1, dh0, x, row(p["e_pre_norm"]))
    last, = _sc_exchange("gather_e_pre_norm_grad", 7, [g["e_pre_norm"].reshape(16, 128)], False)

    pin = adam("e_w_in", grad_x)
    summed["e_pre_norm"] = _sum_parts(last, "sum_e_pre_norm_grad", pin)
    names = list(SMALL)
    views = [_small_views(n) for n in names]
    mine = lambda src: [src[n].reshape(vw[0]) for n, vw in zip(names, views)]
    res = _adam_small(mine(w), [summed[n].reshape(vw[1]) for n, vw in zip(names, views)], [vw[2] for vw in views],
                      mine(m), mine(v), "adam_small")
    for n, out in zip(names, res):
        grads[n], deltas[n], new_m[n], new_v[n] = [t.reshape(_shard_shape(n)) for t in out]

    lead = lambda a: a[None]
    return (loss, grad_x[None], *[lead(grads[n]) for n in WEIGHTS], *[lead(deltas[n]) for n in WEIGHTS],
            *[lead(new_m[n]) for n in WEIGHTS], *[lead(new_v[n]) for n in WEIGHTS])
```

```python
import functools

import numpy as np
import jax
import jax.numpy as jnp
from jax import lax
from jax.experimental import pallas as pl
from jax.experimental.pallas import tpu as pltpu
from jax.experimental.pallas import tpu_sc as plsc

F32 = jnp.float32
BF16 = jnp.bfloat16

S = 2048
D = 2048
NDEV = 8
EPS = 1e-6
NEG = -1e30
HEAD_DIM = 128
ROT_DIM = 32
ROPE_THETA = 500000.0
PATTERNS = ((128, 1), (512, 4), (2048, 16))
BLK = 128
EVEN_IN = 12288
ODD_IN = 6144
HALF = 1024
CONV_K = 31
HALO = 32
TR = 256
SUB = 32

ADAM_LR = 0.001
ADAM_B1 = 0.9
ADAM_B2 = 0.999
ADAM_EPS = 1e-08
ADAM_WD = 0.01
ADAM_STEP = 10

VMEM_BIG = 56 * 1024 * 1024
MESH = pl.DeviceIdType.MESH

NN = (((1,), (0,)), ((), ()))
NT = (((1,), (1,)), ((), ()))
TN = (((0,), (0,)), ((), ()))


def _dot(a, b, dn=NN):
    return lax.dot_general(a, b, dn, preferred_element_type=F32)


def _sigmoid(x):
    return 1.0 / (1.0 + jnp.exp(-x))


def _silu_and_grad(x):
    sg = _sigmoid(x)
    return x * sg, sg * (1.0 + x * (1.0 - sg))


def _params(sem, vmem=None):
    return pltpu.CompilerParams(dimension_semantics=sem, vmem_limit_bytes=vmem)


ANY_SPEC = pl.BlockSpec(memory_space=pl.ANY)


def _matmul(a, b, *, dn, grid, a_spec, b_spec, o_spec, out_shape, out_dtype, acc_shape, name, dep=None):
    nk = grid[2]
    deps = [] if dep is None else list(dep) if isinstance(dep, (tuple, list)) else [dep]

    def body(a_ref, b_ref, *rest):
        o_ref, acc = rest[len(deps)], rest[len(deps) + 1:]
        if nk == 1:
            o_ref[...] = _dot(a_ref[...], b_ref[...], dn).astype(o_ref.dtype)
            return
        acc_ref = acc[0]
        k = pl.program_id(2)

        @pl.when(k == 0)
        def _():
            acc_ref[...] = jnp.zeros_like(acc_ref)

        acc_ref[...] += _dot(a_ref[...], b_ref[...], dn)

        @pl.when(k == nk - 1)
        def _():
            o_ref[...] = acc_ref[...].astype(o_ref.dtype)

    return pl.pallas_call(
        body, grid=grid, in_specs=[a_spec, b_spec] + [ANY_SPEC] * len(deps), out_specs=o_spec,
        out_shape=jax.ShapeDtypeStruct(out_shape, out_dtype),
        scratch_shapes=[] if nk == 1 else [pltpu.VMEM(acc_shape, F32)],
        compiler_params=_params(("parallel", "parallel", "arbitrary"), VMEM_BIG), name=name,
    )(a, b, *deps)


TM = 2048


def _mm_in(h, wg, name):
    nb = wg.shape[2]
    tn = 512 if nb % 512 == 0 else nb
    per = nb // tn
    return _matmul(
        h, wg, dn=NN, grid=(S // TM, NDEV * per, 1),
        a_spec=pl.BlockSpec((TM, D), lambda i, j, k: (i, 0)),
        b_spec=pl.BlockSpec((None, D, tn), lambda i, j, k: (j // per, 0, j % per)),
        o_spec=pl.BlockSpec((TM, tn), lambda i, j, k: (i, j)),
        out_shape=(S, NDEV * nb), out_dtype=F32, acc_shape=(TM, tn), name=name)


def _mm_in_halves(h, wg_halves, name):
    hb = wg_halves[0].shape[2]
    z = None
    for half, wg in enumerate(wg_halves):
        prev = [] if z is None else [z]

        def body(a_ref, b_ref, *rest):
            rest[-1][...] = _dot(a_ref[...], b_ref[...])

        z = pl.pallas_call(
            body, grid=(NDEV,),
            in_specs=[pl.BlockSpec((S, D), lambda j: (0, 0)), pl.BlockSpec((None, D, hb), lambda j: (j, 0, 0))]
                     + [ANY_SPEC] * len(prev),
            out_specs=pl.BlockSpec((S, hb), lambda j, half=half: (0, 2 * j + half)),
            out_shape=jax.ShapeDtypeStruct((S, 2 * NDEV * hb), F32),
            input_output_aliases={2: 0} if prev else {},
            compiler_params=_params(("parallel",), VMEM_BIG), name="%s_%d" % (name, half),
        )(h, wg, *prev)
    return z


def _mm_in_dx_halves(dz, wg_halves, name, dep):
    hb = wg_halves[0].shape[2]
    nk = 2 * NDEV

    def body(a_ref, b0_ref, b1_ref, dep_ref, o_ref, acc_ref):
        k = pl.program_id(2)

        @pl.when(k == 0)
        def _():
            acc_ref[...] = jnp.zeros_like(acc_ref)

        @pl.when(k % 2 == 0)
        def _():
            acc_ref[...] += _dot(a_ref[...], b0_ref[...], NT)

        @pl.when(k % 2 == 1)
        def _():
            acc_ref[...] += _dot(a_ref[...], b1_ref[...], NT)

        @pl.when(k == nk - 1)
        def _():
            o_ref[...] = acc_ref[...]

    b_spec = pl.BlockSpec((None, 1024, hb), lambda i, j, k: (k // 2, j, 0))
    return pl.pallas_call(
        body, grid=(1, D // 1024, nk),
        in_specs=[pl.BlockSpec((S, hb), lambda i, j, k: (0, k)), b_spec, b_spec, ANY_SPEC],
        out_specs=pl.BlockSpec((S, 1024), lambda i, j, k: (0, j)), out_shape=jax.ShapeDtypeStruct((S, D), F32),
        scratch_shapes=[pltpu.VMEM((S, 1024), F32)],
        compiler_params=_params(("parallel", "parallel", "arbitrary"), VMEM_BIG), name=name,
    )(dz, *wg_halves, dep)


def _mm_in_dx(dz, wg, name, dep=None):
    nb = wg.shape[2]
    return _matmul(
        dz, wg, dn=NT, grid=(S // TM, D // 1024, NDEV),
        a_spec=pl.BlockSpec((TM, nb), lambda i, j, k: (i, k)),
        b_spec=pl.BlockSpec((None, 1024, nb), lambda i, j, k: (k, j, 0)),
        o_spec=pl.BlockSpec((TM, 1024), lambda i, j, k: (i, j)),
        out_shape=(S, D), out_dtype=F32, acc_shape=(TM, 1024), name=name, dep=dep)


def _mm_out(yc, w, name, dep=None):
    return _matmul(
        yc, w, dn=NN, grid=(S // TM, D // 512, 1),
        a_spec=pl.BlockSpec((TM, 2048), lambda i, j, k: (i, 0)),
        b_spec=pl.BlockSpec((2048, 512), lambda i, j, k: (0, j)),
        o_spec=pl.BlockSpec((TM, 512), lambda i, j, k: (i, j)),
        out_shape=(S, D), out_dtype=F32, acc_shape=(TM, 512), name=name, dep=dep)


def _mm_out_dx(dy, w, name, dep=None):
    return _matmul(
        dy, w, dn=NT, grid=(S // TM, 2048 // 512, 1),
        a_spec=pl.BlockSpec((TM, D), lambda i, j, k: (i, 0)),
        b_spec=pl.BlockSpec((512, D), lambda i, j, k: (j, 0)),
        o_spec=pl.BlockSpec((TM, 512), lambda i, j, k: (i, j)),
        out_shape=(S, 2048), out_dtype=F32, acc_shape=(TM, 512), name=name, dep=dep)


def _mm_out_dw(yc, dy, name):
    return _matmul(
        yc, dy, dn=TN, grid=(2048 // TM, D // 512, 1),
        a_spec=pl.BlockSpec((S, TM), lambda i, j, k: (0, i)),
        b_spec=pl.BlockSpec((S, 512), lambda i, j, k: (0, j)),
        o_spec=pl.BlockSpec((TM, 512), lambda i, j, k: (i, j)),
        out_shape=(2048, D), out_dtype=BF16, acc_shape=(TM, 512), name=name)


def _row_spec(w=D):
    return pl.BlockSpec((TR, w), lambda i: (i, 0))


def _vec_spec(w=D):
    return pl.BlockSpec((1, w), lambda i: (0, 0))


def _rms_stats(x):
    r = lax.rsqrt(jnp.mean(x * x, axis=-1, keepdims=True) + EPS)
    return x * r, r


def _rms_bwd(dn, xhat, r, g):
    dxh = dn * g
    return r * (dxh - xhat * jnp.mean(dxh * xhat, axis=-1, keepdims=True))


def _acc_rows(ref, val, i):
    s = jnp.sum(val, axis=0, keepdims=True)

    @pl.when(i == 0)
    def _():
        ref[...] = s

    @pl.when(i > 0)
    def _():
        ref[...] += s


def _pre0_fwd(x, g, dep=None):
    deps = [] if dep is None else [dep]

    def body(x_ref, g_ref, *rest):
        xhat, _ = _rms_stats(x_ref[...])
        rest[-1][...] = (xhat * g_ref[...]).astype(BF16)

    return pl.pallas_call(
        body, grid=(S // TR,), in_specs=[_row_spec(), _vec_spec()] + [ANY_SPEC] * len(deps), out_specs=_row_spec(),
        out_shape=jax.ShapeDtypeStruct((S, D), BF16), compiler_params=_params(("parallel",)), name="pre0_fwd",
    )(x, g, *deps)


def _post0_fwd(x, y0, g_post, g_pre1):
    def body(x_ref, y_ref, gp_ref, g1_ref, x1_ref, h1_ref):
        yhat, _ = _rms_stats(y_ref[...])
        x1 = x_ref[...] + yhat * gp_ref[...]
        x1_ref[...] = x1
        xhat, _ = _rms_stats(x1)
        h1_ref[...] = (xhat * g1_ref[...]).astype(BF16)

    return pl.pallas_call(
        body, grid=(S // TR,), in_specs=[_row_spec(), _row_spec(), _vec_spec(), _vec_spec()],
        out_specs=[_row_spec(), _row_spec()],
        out_shape=[jax.ShapeDtypeStruct((S, D), F32), jax.ShapeDtypeStruct((S, D), BF16)],
        compiler_params=_params(("parallel",)), name="post0_fwd",
    )(x, y0, g_post, g_pre1)


def _post1_bwd(y1, x1, target, g_post):
    def body(y_ref, x1_ref, t_ref, g_ref, loss_ref, dx2_ref, dy_ref, dg_ref):
        i = pl.program_id(0)
        yhat, r = _rms_stats(y_ref[...])
        g = g_ref[...]
        err = x1_ref[...] + yhat * g - t_ref[...]
        part = jnp.sum(jnp.sum(err * err, axis=-1, keepdims=True), axis=0, keepdims=True) * (0.5 / D)
        _acc_rows(loss_ref, jnp.broadcast_to(part, (1, 128)), i)
        dx2 = err * (1.0 / D)
        dx2_ref[...] = dx2
        _acc_rows(dg_ref, dx2 * yhat, i)
        dy_ref[...] = _rms_bwd(dx2, yhat, r, g).astype(BF16)

    return pl.pallas_call(
        body, grid=(S // TR,), in_specs=[_row_spec(), _row_spec(), _row_spec(), _vec_spec()],
        out_specs=[_vec_spec(128), _row_spec(), _row_spec(), _vec_spec()],
        out_shape=[jax.ShapeDtypeStruct((1, 128), F32), jax.ShapeDtypeStruct((S, D), F32),
                   jax.ShapeDtypeStruct((S, D), BF16), jax.ShapeDtypeStruct((1, D), F32)],
        compiler_params=_params(("arbitrary",)), name="post1_bwd",
    )(y1, x1, target, g_post)


def _mid_bwd(dx2, dh1, x1, y0, g_pre1, g_post0):
    def body(dx2_ref, dh_ref, x1_ref, y_ref, g1_ref, gp_ref, dx1_ref, dy_ref, dg1_ref, dgp_ref):
        i = pl.program_id(0)
        xhat, r1 = _rms_stats(x1_ref[...])
        dh = dh_ref[...]
        _acc_rows(dg1_ref, dh * xhat, i)
        dx1 = dx2_ref[...] + _rms_bwd(dh, xhat, r1, g1_ref[...])
        dx1_ref[...] = dx1
        yhat, r0 = _rms_stats(y_ref[...])
        _acc_rows(dgp_ref, dx1 * yhat, i)
        dy_ref[...] = _rms_bwd(dx1, yhat, r0, gp_ref[...]).astype(BF16)

    return pl.pallas_call(
        body, grid=(S // TR,),
        in_specs=[_row_spec(), _row_spec(), _row_spec(), _row_spec(), _vec_spec(), _vec_spec()],
        out_specs=[_row_spec(), _row_spec(), _vec_spec(), _vec_spec()],
        out_shape=[jax.ShapeDtypeStruct((S, D), F32), jax.ShapeDtypeStruct((S, D), BF16),
                   jax.ShapeDtypeStruct((1, D), F32), jax.ShapeDtypeStruct((1, D), F32)],
        compiler_params=_params(("arbitrary",)), name="mid_bwd",
    )(dx2, dh1, x1, y0, g_pre1, g_post0)


def _pre0_bwd(dx1, dh0, x, g):
    def body(dx1_ref, dh_ref, x_ref, g_ref, gx_ref, dg_ref):
        i = pl.program_id(0)
        xhat, r = _rms_stats(x_ref[...])
        dh = dh_ref[...]
        _acc_rows(dg_ref, dh * xhat, i)
        gx_ref[...] = dx1_ref[...] + _rms_bwd(dh, xhat, r, g_ref[...])

    return pl.pallas_call(
        body, grid=(S // TR,), in_specs=[_row_spec(), _row_spec(), _row_spec(), _vec_spec()],
        out_specs=[_row_spec(), _vec_spec()],
        out_shape=[jax.ShapeDtypeStruct((S, D), F32), jax.ShapeDtypeStruct((1, D), F32)],
        compiler_params=_params(("arbitrary",)), name="pre0_bwd",
    )(dx1, dh0, x, g)


POOL_CH = 256


def _pool_apply(a, w, transpose):
    n = a.shape[0]
    row = lax.broadcasted_iota(jnp.int32, a.shape, 0)
    cnt = jnp.minimum(row + 1, w).astype(F32)
    s = a / cnt if transpose else a
    for k in (1, 2, 4, 8):
        if transpose:
            sh = jnp.where(row < n - k, pltpu.roll(s, n - k, 0), 0.0)
        else:
            sh = jnp.where(row >= k, pltpu.roll(s, k, 0), 0.0)
        s = jnp.where(w > k, s + sh, s)
    return s - a if transpose else s / cnt - a


def _pool_fwd(z0, pool_w, pool_scale):
    def body(a_ref, gate_ref, w_ref, sc_ref, out_ref):
        win = jnp.left_shift(2, pl.program_id(0))
        pooled = _pool_apply(a_ref[...], win, False)
        mixed = _dot(pooled.astype(BF16), w_ref[...])
        gate = gate_ref[...]
        out_ref[...] = (mixed * sc_ref[...] * (gate * _sigmoid(gate))).astype(BF16)

    return pl.pallas_call(
        body, grid=(4,),
        in_specs=[pl.BlockSpec((S, POOL_CH), lambda g: (0, g)), pl.BlockSpec((S, POOL_CH), lambda g: (0, 4 + g)),
                  pl.BlockSpec((None, POOL_CH, POOL_CH), lambda g: (g, 0, 0)),
                  pl.BlockSpec((1, POOL_CH), lambda g: (0, g))],
        out_specs=pl.BlockSpec((S, POOL_CH), lambda g: (0, g)),
        out_shape=jax.ShapeDtypeStruct((S, 2048), BF16),
        compiler_params=_params(("parallel",), VMEM_BIG), name="pool_fwd",
    )(z0, z0, pool_w, pool_scale)


def _pool_bwd(z0, dycat, pool_w, pool_scale):
    def body(a_ref, gate_ref, dy_ref, w_ref, sc_ref, da_ref, dgate_ref, dw_ref, dsc_ref):
        win = jnp.left_shift(2, pl.program_id(0))
        pooled = _pool_apply(a_ref[...], win, False).astype(BF16)
        w = w_ref[...]
        mixed = _dot(pooled, w)
        silu, dsilu = _silu_and_grad(gate_ref[...])
        dy = dy_ref[...]
        sc = sc_ref[...]
        dgate_ref[...] = (dy * (mixed * sc) * dsilu).astype(BF16)
        dms = dy * silu
        dsc_ref[...] = jnp.sum(dms * mixed, axis=0, keepdims=True)
        dmixed = (dms * sc).astype(BF16)
        dw_ref[...] = _dot(pooled, dmixed, TN)
        dpooled = _dot(dmixed, w, NT)
        da_ref[...] = _pool_apply(dpooled, win, True).astype(BF16)

    slab = lambda off: pl.BlockSpec((S, POOL_CH), lambda g: (0, off + g))
    return pl.pallas_call(
        body, grid=(4,),
        in_specs=[slab(0), slab(4), slab(0), pl.BlockSpec((None, POOL_CH, POOL_CH), lambda g: (g, 0, 0)),
                  pl.BlockSpec((1, POOL_CH), lambda g: (0, g))],
        out_specs=[slab(0), slab(0), pl.BlockSpec((None, POOL_CH, POOL_CH), lambda g: (g, 0, 0)),
                   pl.BlockSpec((1, POOL_CH), lambda g: (0, g))],
        out_shape=[jax.ShapeDtypeStruct((S, HALF), BF16), jax.ShapeDtypeStruct((S, HALF), BF16),
                   jax.ShapeDtypeStruct((4, POOL_CH, POOL_CH), F32), jax.ShapeDtypeStruct((1, HALF), F32)],
        compiler_params=_params(("parallel",), VMEM_BIG), name="pool_bwd",
    )(z0, z0, dycat, pool_w, pool_scale)


Q_COL, K_COL, V_COL, BG_COL = 2048 // 128, 5120 // 128, 8192 // 128, 11264 // 128
SCALE = HEAD_DIM ** -0.5


def _rope_tables():
    pos = jnp.arange(S, dtype=F32)
    inv_freq = jnp.power(ROPE_THETA, -jnp.arange(0, ROT_DIM, 2, dtype=F32) / ROT_DIM)
    ang = pos[:, None] * inv_freq[None, :]
    cos, sin = jnp.cos(ang), jnp.sin(ang)
    half = ROT_DIM // 2
    zeros = jnp.zeros((S, HEAD_DIM - ROT_DIM), F32)
    c = jnp.concatenate([cos, cos, jnp.ones((S, HEAD_DIM - ROT_DIM), F32)], axis=1)
    a = jnp.concatenate([-sin, jnp.zeros((S, half), F32), zeros], axis=1)
    b = jnp.concatenate([jnp.zeros((S, half), F32), sin, zeros], axis=1)
    return c, a, b


def _rope(t, c, a, b):
    half = ROT_DIM // 2
    return t * c + pltpu.roll(t, HEAD_DIM - half, 1) * a + pltpu.roll(t, half, 1) * b


def _rope_t(d, c, a, b):
    half = ROT_DIM // 2
    return d * c + pltpu.roll(d * a, half, 1) + pltpu.roll(d * b, HEAD_DIM - half, 1)


def _deinterleave(dst, src, dil, cast=None, dst_off=0):
    length = S // dil
    for r in range(dil):
        v = src[...] if dil == 1 else src[pl.ds(r, length, stride=dil), :]
        dst[dst_off + r * length:dst_off + (r + 1) * length, :] = v if cast is None else v.astype(cast)


def _interleave(dst, src, dil, src_off=0):
    length = S // dil
    for r in range(dil):
        if dil == 1:
            dst[...] = src[src_off:src_off + S, :]
        else:
            dst[pl.ds(r, length, stride=dil), :] = src[src_off + r * length:src_off + (r + 1) * length, :]


CU = 4
NUNITS = S // BLK
B_QK = (((2,), (2,)), ((0,), (0,)))
B_PV = (((2,), (1,)), ((0,), (0,)))
B_TN = (((1,), (1,)), ((0,), (0,)))


def _blocks(ref, first):
    return ref[first * BLK:(first + CU) * BLK, :].reshape(CU, BLK, HEAD_DIM)


def _chunk_scores(u0, nb, qd, kdp):
    q = _blocks(qd, u0)
    row = lax.broadcasted_iota(jnp.int32, (CU, BLK, BLK), 1)
    col = lax.broadcasted_iota(jnp.int32, (CU, BLK, BLK), 2)
    s_own = jnp.where(col <= row, _dot(q, _blocks(kdp, u0 + 1), B_QK) * SCALE, NEG)
    if nb == 1:
        return q, s_own, None
    unit = lax.broadcasted_iota(jnp.int32, (CU, BLK, BLK), 0) + u0
    s_prev = jnp.where((col >= row) & ((unit % nb) != 0), _dot(q, _blocks(kdp, u0), B_QK) * SCALE, NEG)
    return q, s_own, s_prev


def _qkv_prep(z0, tabs):
    def body(q_ref, k_ref, v_ref, c_ref, a_ref, b_ref, qo_ref, ko_ref, vo_ref, tmp):
        p = pl.program_id(1)
        for gi, (_, dil) in enumerate(PATTERNS):
            @pl.when(p == gi)
            def _(dil=dil):
                c, a, b = c_ref[...], a_ref[...], b_ref[...]
                tmp[...] = _rope(q_ref[...], c, a, b)
                _deinterleave(qo_ref, tmp, dil, BF16)
                tmp[...] = _rope(k_ref[...], c, a, b)
                _deinterleave(ko_ref, tmp, dil, BF16)
                _deinterleave(vo_ref, v_ref, dil, BF16)

    tab = pl.BlockSpec((S, HEAD_DIM), lambda h, p: (0, 0))
    out = pl.BlockSpec((S, HEAD_DIM), lambda h, p: (0, p * 8 + h))
    return pl.pallas_call(
        body, grid=(8, 3), in_specs=[_head_spec(Q_COL), _head_spec(K_COL), _head_spec(V_COL), tab, tab, tab],
        out_specs=[out, out, out], out_shape=[jax.ShapeDtypeStruct((S, 3072), BF16)] * 3,
        scratch_shapes=[pltpu.VMEM((S, HEAD_DIM), F32)],
        compiler_params=_params(("parallel", "arbitrary"), VMEM_BIG), name="qkv_prep",
    )(z0, z0, z0, *tabs)


def _pad_copy(dst, src):
    dst[0:BLK, :] = jnp.zeros((BLK, HEAD_DIM), dst.dtype)
    dst[BLK:BLK + S, :] = src[...]


def _attn_group_fwd(dil, qd, kd_ref, vd_ref, kdp, vdp, od, ld, og, lg):
    nb = S // dil // BLK
    _pad_copy(kdp, kd_ref)
    _pad_copy(vdp, vd_ref)
    for u0 in range(0, NUNITS, CU):
        _, s_own, s_prev = _chunk_scores(u0, nb, qd, kdp)
        m = jnp.max(s_own, axis=2, keepdims=True)
        if s_prev is not None:
            m = jnp.maximum(m, jnp.max(s_prev, axis=2, keepdims=True))
        p_own = jnp.exp(s_own - m)
        den = jnp.sum(p_own, axis=2, keepdims=True)
        acc = _dot(p_own.astype(BF16), _blocks(vdp, u0 + 1), B_PV)
        if s_prev is not None:
            p_prev = jnp.exp(s_prev - m)
            den = den + jnp.sum(p_prev, axis=2, keepdims=True)
            acc = acc + _dot(p_prev.astype(BF16), _blocks(vdp, u0), B_PV)
        rows = slice(u0 * BLK, (u0 + CU) * BLK)
        od[rows, :] = (acc / den).reshape(CU * BLK, HEAD_DIM)
        ld[rows, :] = jnp.broadcast_to(m + jnp.log(den), (CU, BLK, HEAD_DIM)).reshape(CU * BLK, HEAD_DIM)
    _interleave(og, od, dil)
    _interleave(lg, ld, dil)


def _group_weights(lgs):
    l0, l1, l2 = lgs[0][...], lgs[1][...], lgs[2][...]
    mx = jnp.maximum(l0, jnp.maximum(l1, l2))
    e0, e1, e2 = jnp.exp(l0 - mx), jnp.exp(l1 - mx), jnp.exp(l2 - mx)
    den = e0 + e1 + e2
    return e0 / den, e1 / den, e2 / den


def _head_spec(base, ngroups_axis=True):
    return pl.BlockSpec((S, HEAD_DIM), lambda h, p: (0, base + (p % 3) * 8 + h))


def _slab(dtype=F32, rows=S):
    return pltpu.VMEM((rows, HEAD_DIM), dtype)


def _attn_fwd(z0, qkv, ycat):
    def body(q_ref, k_ref, v_ref, gate_ref, ycat_ref, out_ref, og_ref, lg_ref,
             kdp, vdp, od, ld, og0, og1, og2, lg0, lg1, lg2):
        del ycat_ref
        p = pl.program_id(1)
        ogs, lgs = (og0, og1, og2), (lg0, lg1, lg2)
        for gi, (_, dil) in enumerate(PATTERNS):
            @pl.when(p == gi)
            def _(gi=gi, dil=dil):
                _attn_group_fwd(dil, q_ref, k_ref, v_ref, kdp, vdp, od, ld, ogs[gi], lgs[gi])
                og_ref[...] = ogs[gi][...]
                lg_ref[...] = lgs[gi][...]

        @pl.when(p == 2)
        def _():
            w0, w1, w2 = _group_weights(lgs)
            o = w0 * og0[...] + w1 * og1[...] + w2 * og2[...]
            gate = gate_ref[...]
            out_ref[...] = (o * (gate * _sigmoid(gate))).astype(BF16)

    grp = pl.BlockSpec((S, HEAD_DIM), lambda h, p: (0, p * 8 + h))
    return pl.pallas_call(
        body, grid=(8, 3),
        in_specs=[grp, grp, grp, pl.BlockSpec((S, HEAD_DIM), lambda h, p: (0, BG_COL + h)), ANY_SPEC],
        out_specs=[pl.BlockSpec((S, HEAD_DIM), lambda h, p: (0, 8 + h)), grp, grp],
        out_shape=[jax.ShapeDtypeStruct((S, 2048), BF16), jax.ShapeDtypeStruct((S, 3072), F32),
                   jax.ShapeDtypeStruct((S, 3072), F32)],
        scratch_shapes=[_slab(BF16, S + BLK), _slab(BF16, S + BLK)] + [_slab() for _ in range(8)],
        input_output_aliases={4: 0},
        compiler_params=_params(("parallel", "arbitrary"), VMEM_BIG), name="attn_fwd",
    )(*qkv, z0, ycat)


def _attn_bwd(z0, qkv, og, lg, dycat, tabs, dep):
    def body(q_ref, k_ref, v_ref, gate_ref, dy_ref, c_ref, a_ref, b_ref,
             og0_ref, og1_ref, og2_ref, lg0_ref, lg1_ref, lg2_ref, dep_ref,
             dq_ref, dk_ref, dv_ref, dbg_ref,
             tmp, kd, vd, ld, dg0, dg1, dg2, cg0, cg1, cg2, dod, cd, dqd, dkd, dvd):
        p = pl.program_id(1)
        ogs, lgs, dgs, cgs = (og0_ref, og1_ref, og2_ref), (lg0_ref, lg1_ref, lg2_ref), (dg0, dg1, dg2), (cg0, cg1, cg2)

        @pl.when(p == 0)
        def _():
            w = _group_weights(lgs)
            o = w[0] * ogs[0][...] + w[1] * ogs[1][...] + w[2] * ogs[2][...]
            silu, dsilu = _silu_and_grad(gate_ref[...])
            dy = dy_ref[...]
            dbg_ref[...] = (dy * o * dsilu).astype(BF16)
            do = dy * silu
            dwbar = jnp.sum(do * o, axis=1, keepdims=True)
            for gi in range(3):
                dgs[gi][...] = w[gi] * do
                cgs[gi][...] = -w[gi] * dwbar

        for gi, (_, dil) in enumerate(PATTERNS):
            @pl.when(p == 1 + gi)
            def _(gi=gi, dil=dil):
                nb = S // dil // BLK
                qd = q_ref
                c, a, b = c_ref[...], a_ref[...], b_ref[...]
                _pad_copy(kd, k_ref)
                _pad_copy(vd, v_ref)
                _deinterleave(dod, dgs[gi], dil, BF16)
                _deinterleave(ld, lgs[gi], dil)
                _deinterleave(cd, cgs[gi], dil)
                dkd[...] = jnp.zeros_like(dkd)
                dvd[...] = jnp.zeros_like(dvd)
                flat = lambda t: t.reshape(CU * BLK, HEAD_DIM)
                for u0 in range(0, NUNITS, CU):
                    q, s_own, s_prev = _chunk_scores(u0, nb, qd, kd)
                    lse, cv, do = _blocks(ld, u0), _blocks(cd, u0), _blocks(dod, u0)
                    own = slice((u0 + 1) * BLK, (u0 + 1 + CU) * BLK)
                    p_own = jnp.exp(s_own - lse)
                    ds_own = (p_own * (_dot(do, _blocks(vd, u0 + 1), B_QK) + cv) * SCALE).astype(BF16)
                    dq = _dot(ds_own, _blocks(kd, u0 + 1), B_PV)
                    dkd[own, :] += flat(_dot(ds_own, q, B_TN))
                    dvd[own, :] += flat(_dot(p_own.astype(BF16), do, B_TN))
                    if s_prev is not None:
                        prev = slice(u0 * BLK, (u0 + CU) * BLK)
                        p_prev = jnp.exp(s_prev - lse)
                        ds_prev = (p_prev * (_dot(do, _blocks(vd, u0), B_QK) + cv) * SCALE).astype(BF16)
                        dq = dq + _dot(ds_prev, _blocks(kd, u0), B_PV)
                        dkd[prev, :] += flat(_dot(ds_prev, q, B_TN))
                        dvd[prev, :] += flat(_dot(p_prev.astype(BF16), do, B_TN))
                    dqd[u0 * BLK:(u0 + CU) * BLK, :] = flat(dq)
                _interleave(tmp, dqd, dil)
                dq_ref[...] = _rope_t(tmp[...], c, a, b).astype(BF16)
                _interleave(tmp, dkd, dil, BLK)
                dk_ref[...] = _rope_t(tmp[...], c, a, b).astype(BF16)
                _interleave(tmp, dvd, dil, BLK)
                dv_ref[...] = tmp[...].astype(BF16)

    tab = pl.BlockSpec((S, HEAD_DIM), lambda h, p: (0, 0))
    hspec = lambda base: pl.BlockSpec((S, HEAD_DIM), lambda h, p: (0, base + h))
    gspec = pl.BlockSpec((S, HEAD_DIM), lambda h, p: (0, jnp.maximum(p - 1, 0) * 8 + h))
    return pl.pallas_call(
        body, grid=(8, 4),
        in_specs=[gspec, gspec, gspec, hspec(BG_COL), hspec(8), tab, tab, tab,
                  hspec(0), hspec(8), hspec(16), hspec(0), hspec(8), hspec(16), ANY_SPEC],
        out_specs=[gspec, gspec, gspec, hspec(0)],
        out_shape=[jax.ShapeDtypeStruct((S, 3072), BF16)] * 3 + [jax.ShapeDtypeStruct((S, HALF), BF16)],
        scratch_shapes=[_slab(), _slab(BF16, S + BLK), _slab(BF16, S + BLK), _slab()] + [_slab() for _ in range(6)]
                       + [_slab(BF16), _slab(), _slab(), _slab(F32, S + BLK), _slab(F32, S + BLK)],
        compiler_params=_params(("parallel", "arbitrary"), VMEM_BIG), name="attn_bwd",
    )(*qkv, z0, dycat, *tabs, og, og, og, lg, lg, lg, dep)


SGU_CH = 256
NCHUNK = TR // 128


def _ln_stats(x):
    mu = jnp.mean(x, axis=-1, keepdims=True)
    xc = x - mu
    r = lax.rsqrt(jnp.mean(xc * xc, axis=-1, keepdims=True) + EPS)
    return xc * r, r


def _ln_bwd(dy, xhat, r, g):
    dxh = dy * g
    return r * (dxh - jnp.mean(dxh, axis=-1, keepdims=True) - xhat * jnp.mean(dxh * xhat, axis=-1, keepdims=True))


def _tril_bf16(w):
    row = lax.broadcasted_iota(jnp.int32, w.shape, 0)
    col = lax.broadcasted_iota(jnp.int32, w.shape, 1)
    return jnp.where(row >= col, w, 0.0).astype(BF16)


def _sgu_gate(vn_s, s_s, w_ref, bb_ref):
    for h in range(4):
        wm = _tril_bf16(w_ref[h])
        bias = bb_ref[h]
        for ch in range(NCHUNK):
            rows, cols = slice(ch * 128, (ch + 1) * 128), slice(h * SGU_CH, (h + 1) * SGU_CH)
            s_s[rows, cols] = _dot(wm, vn_s[rows, cols]) + jnp.concatenate([bias, bias], axis=1)


WIN = HALO + TR
SUBL = 8


def _shifted_copies(dst, src):
    dst[0] = src[...]
    for b in range(1, SUBL):
        dst[b, 0:WIN - SUBL, :] = src[pl.ds(b, WIN - SUBL), :]


def _rows_at(copies, off, n):
    return copies[off % SUBL, pl.ds(off - off % SUBL, n), :]


def _conv_fwd(i, dval_ref, dglu_ref, hval_ref, hglu_ref, cw_ref, cb_ref, xw, xr, dcs):
    halo = hval_ref[...] * _sigmoid(hglu_ref[...])
    xw[0:HALO, :] = jnp.where(i > 0, halo, 0.0)
    xw[HALO:HALO + TR, :] = dval_ref[...] * _sigmoid(dglu_ref[...])
    _shifted_copies(xr, xw)
    for rb in range(TR // SUB):
        acc = jnp.broadcast_to(cb_ref[...], (SUB, HALF))
        for k in range(CONV_K):
            acc = acc + cw_ref[k:k + 1, :] * _rows_at(xr, rb * SUB + HALO - (CONV_K - 1) + k, SUB)
        dcs[rb * SUB:(rb + 1) * SUB, :] = acc


def _odd_in_specs():
    col = lambda j: pl.BlockSpec((TR, HALF), lambda i, *_: (i, j))
    prev = lambda j: pl.BlockSpec((HALO, HALF), lambda i, *_: (jnp.maximum(i * (TR // HALO) - 1, 0), j))
    return [col(0), col(1), col(2), col(3), col(4), col(5), prev(3), prev(4)]


def _full_spec(shape):
    return pl.BlockSpec(shape, lambda i, *_: (0,) * len(shape))


def _odd_fwd(z1, sgu_g, sgu_b, sgu_w, sgu_bb, conv_w, conv_b, cn_g, cn_b):
    def body(u_ref, v_ref, cg_ref, dval_ref, dglu_ref, dgate_ref, hval_ref, hglu_ref,
             g_ref, b_ref, w_ref, bb_ref, cw_ref, cb_ref, cng_ref, cnb_ref, out_ref, vn_s, s_s, xw, dcs, xr):
        i = pl.program_id(0)
        vhat, _ = _ln_stats(v_ref[...])
        vn_s[...] = (vhat * g_ref[...] + b_ref[...]).astype(BF16)
        _sgu_gate(vn_s, s_s, w_ref, bb_ref)
        cg = cg_ref[...]
        out_ref[:, 0:HALF] = (u_ref[...] * s_s[...] * (cg * _sigmoid(cg))).astype(BF16)
        _conv_fwd(i, dval_ref, dglu_ref, hval_ref, hglu_ref, cw_ref, cb_ref, xw, xr, dcs)
        dhat, _ = _ln_stats(dcs[...])
        dn = dhat * cng_ref[...] + cnb_ref[...]
        dgate = dgate_ref[...]
        out_ref[:, HALF:2 * HALF] = ((dn * _sigmoid(dn)) * (dgate * _sigmoid(dgate))).astype(BF16)

    vec = _full_spec((1, HALF))
    return pl.pallas_call(
        body, grid=(S // TR,),
        in_specs=_odd_in_specs() + [vec, vec, _full_spec((4, 128, 128)), _full_spec((4, 128, 128)),
                                    _full_spec((HALO, HALF)), vec, vec, vec],
        out_specs=pl.BlockSpec((TR, 2048), lambda i: (i, 0)),
        out_shape=jax.ShapeDtypeStruct((S, 2048), BF16),
        scratch_shapes=[pltpu.VMEM((TR, HALF), BF16), pltpu.VMEM((TR, HALF), F32),
                        pltpu.VMEM((WIN, HALF), F32), pltpu.VMEM((TR, HALF), F32), pltpu.VMEM((SUBL, WIN, HALF), F32)],
        compiler_params=_params(("parallel",), VMEM_BIG), name="odd_fwd",
    )(z1, z1, z1, z1, z1, z1, z1, z1, sgu_g, sgu_b, sgu_w, sgu_bb, conv_w, conv_b, cn_g, cn_b)


def _odd_bwd_a(z1, dycat, sgu_g, sgu_b, sgu_w, sgu_bb, conv_w, conv_b, cn_g, cn_b):
    def body(u_ref, v_ref, cg_ref, dval_ref, dglu_ref, dgate_ref, hval_ref, hglu_ref, dy_ref,
             g_ref, b_ref, w_ref, bb_ref, cw_ref, cb_ref, cng_ref, cnb_ref,
             dz_ref, ddc_ref, dw_ref, dbb_ref, dg_ref, db_ref, dcng_ref, dcnb_ref, dcb_ref,
             vn_s, s_s, xw, dcs, ds_s, dvn_s, xr):
        i = pl.program_id(0)
        vhat, rv = _ln_stats(v_ref[...])
        g = g_ref[...]
        vn_s[...] = (vhat * g + b_ref[...]).astype(BF16)
        _sgu_gate(vn_s, s_s, w_ref, bb_ref)
        silu_c, dsilu_c = _silu_and_grad(cg_ref[...])
        dyc = dy_ref[:, 0:HALF]
        u = u_ref[...]
        s = s_s[...]
        dz_ref[:, 0:HALF] = (dyc * s * silu_c).astype(BF16)
        dz_ref[:, 2 * HALF:3 * HALF] = (dyc * u * s * dsilu_c).astype(BF16)
        ds_s[...] = dyc * u * silu_c

        @pl.when(i == 0)
        def _():
            dw_ref[...] = jnp.zeros_like(dw_ref)
            dbb_ref[...] = jnp.zeros_like(dbb_ref)

        tril = lax.broadcasted_iota(jnp.int32, (128, 128), 0) >= lax.broadcasted_iota(jnp.int32, (128, 128), 1)
        for h in range(4):
            wm = _tril_bf16(w_ref[h])
            for ch in range(NCHUNK):
                rows, cols = slice(ch * 128, (ch + 1) * 128), slice(h * SGU_CH, (h + 1) * SGU_CH)
                ds = ds_s[rows, cols]
                dsb = ds.astype(BF16)
                dw_ref[h] += jnp.where(tril, _dot(dsb, vn_s[rows, cols], NT), 0.0)
                dbb_ref[h] += jnp.broadcast_to(jnp.sum(ds, axis=1, keepdims=True), (128, 128))
                dvn_s[rows, cols] = _dot(wm, dsb, TN)
        dvn = dvn_s[...]
        _acc_rows(dg_ref, dvn * vhat, i)
        _acc_rows(db_ref, dvn, i)
        dz_ref[:, HALF:2 * HALF] = _ln_bwd(dvn, vhat, rv, g).astype(BF16)

        _conv_fwd(i, dval_ref, dglu_ref, hval_ref, hglu_ref, cw_ref, cb_ref, xw, xr, dcs)
        dhat, rd = _ln_stats(dcs[...])
        cng = cng_ref[...]
        silu_n, dsilu_n = _silu_and_grad(dhat * cng + cnb_ref[...])
        silu_g, dsilu_g = _silu_and_grad(dgate_ref[...])
        dyd = dy_ref[:, HALF:2 * HALF]
        dz_ref[:, 5 * HALF:6 * HALF] = (dyd * silu_n * dsilu_g).astype(BF16)
        ddn = dyd * silu_g * dsilu_n
        _acc_rows(dcng_ref, ddn * dhat, i)
        _acc_rows(dcnb_ref, ddn, i)
        ddc = _ln_bwd(ddn, dhat, rd, cng)
        ddc_ref[...] = ddc
        _acc_rows(dcb_ref, ddc, i)

    vec = _full_spec((1, HALF))
    sq = _full_spec((4, 128, 128))
    return pl.pallas_call(
        body, grid=(S // TR,),
        in_specs=_odd_in_specs() + [pl.BlockSpec((TR, 2048), lambda i: (i, 0)),
                                    vec, vec, sq, sq, _full_spec((HALO, HALF)), vec, vec, vec],
        out_specs=[pl.BlockSpec((TR, ODD_IN), lambda i: (i, 0)), pl.BlockSpec((TR, HALF), lambda i: (i, 0)),
                   sq, sq, vec, vec, vec, vec, vec],
        out_shape=[jax.ShapeDtypeStruct((S, ODD_IN), BF16), jax.ShapeDtypeStruct((S, HALF), F32),
                   jax.ShapeDtypeStruct((4, 128, 128), F32), jax.ShapeDtypeStruct((4, 128, 128), F32)]
                  + [jax.ShapeDtypeStruct((1, HALF), F32)] * 5,
        scratch_shapes=[pltpu.VMEM((TR, HALF), BF16), pltpu.VMEM((TR, HALF), F32),
                        pltpu.VMEM((WIN, HALF), F32), pltpu.VMEM((TR, HALF), F32),
                        pltpu.VMEM((TR, HALF), F32), pltpu.VMEM((TR, HALF), F32), pltpu.VMEM((SUBL, WIN, HALF), F32)],
        compiler_params=_params(("arbitrary",), VMEM_BIG), name="odd_bwd_a",
    )(z1, z1, z1, z1, z1, z1, z1, z1, dycat, sgu_g, sgu_b, sgu_w, sgu_bb, conv_w, conv_b, cn_g, cn_b)


def _odd_bwd_b(z1, ddc, dz1, conv_w):
    nt = S // TR

    def body(dval_ref, dglu_ref, hval_ref, hglu_ref, ddc_ref, hddc_ref, cw_ref, dz_in_ref,
             dz_ref, dcw_ref, xw, dwin, dxs, xr, dr):
        del dz_in_ref
        i, j = pl.program_id(0), pl.program_id(1)
        sg = _sigmoid(dglu_ref[...])
        dval = dval_ref[...]

        @pl.when(j == 0)
        def _():
            halo = hval_ref[...] * _sigmoid(hglu_ref[...])
            xw[0:HALO, :] = jnp.where(i > 0, halo, 0.0)
            xw[HALO:HALO + TR, :] = dval * sg
            dwin[0:TR, :] = ddc_ref[...]
            dwin[TR:TR + HALO, :] = jnp.where(i < nt - 1, hddc_ref[...], 0.0)
            _shifted_copies(xr, xw)
            _shifted_copies(dr, dwin)

            @pl.when(i == 0)
            def _():
                dcw_ref[...] = jnp.zeros_like(dcw_ref)

            for rb in range(TR // SUB):
                acc = jnp.zeros((SUB, HALF), F32)
                for k in range(CONV_K):
                    acc = acc + cw_ref[k:k + 1, :] * _rows_at(dr, rb * SUB + (CONV_K - 1) - k, SUB)
                dxs[rb * SUB:(rb + 1) * SUB, :] = acc
            for k in range(CONV_K):
                acc = jnp.zeros((SUB, HALF), F32)
                for rb in range(TR // SUB):
                    acc = acc + dwin[rb * SUB:(rb + 1) * SUB, :] * _rows_at(xr, rb * SUB + HALO - (CONV_K - 1) + k, SUB)
                dcw_ref[k:k + 1, :] += jnp.sum(acc, axis=0, keepdims=True)
            dz_ref[...] = (dxs[...] * sg).astype(BF16)

        @pl.when(j == 1)
        def _():
            dz_ref[...] = (dxs[...] * dval * sg * (1.0 - sg)).astype(BF16)

    col = lambda c: pl.BlockSpec((TR, HALF), lambda i, j: (i, c))
    prev = lambda c: pl.BlockSpec((HALO, HALF), lambda i, j: (jnp.maximum(i * (TR // HALO) - 1, 0), c))
    nxt = pl.BlockSpec((HALO, HALF), lambda i, j: (jnp.minimum((i + 1) * (TR // HALO), S // HALO - 1), 0))
    return pl.pallas_call(
        body, grid=(nt, 2),
        in_specs=[col(3), col(4), prev(3), prev(4), pl.BlockSpec((TR, HALF), lambda i, j: (i, 0)), nxt,
                  _full_spec((HALO, HALF)), pl.BlockSpec(memory_space=pl.ANY)],
        out_specs=[pl.BlockSpec((TR, HALF), lambda i, j: (i, 3 + j)), _full_spec((HALO, HALF))],
        out_shape=[jax.ShapeDtypeStruct((S, ODD_IN), BF16), jax.ShapeDtypeStruct((HALO, HALF), F32)],
        scratch_shapes=[pltpu.VMEM((WIN, HALF), F32), pltpu.VMEM((WIN, HALF), F32), pltpu.VMEM((TR, HALF), F32),
                        pltpu.VMEM((SUBL, WIN, HALF), F32), pltpu.VMEM((SUBL, WIN, HALF), F32)],
        input_output_aliases={7: 0},
        compiler_params=_params(("arbitrary", "arbitrary"), VMEM_BIG), name="odd_bwd_b",
    )(z1, z1, z1, z1, ddc, ddc, conv_w, dz1)


def _cast_bf16(w, name, piece=0, npieces=1):
    r, c = w.shape[0], w.shape[1] // npieces
    tr = min(r, 256)

    def body(i_ref, o_ref):
        o_ref[...] = i_ref[...].astype(BF16)

    return pl.pallas_call(
        body, grid=(r // tr,), in_specs=[pl.BlockSpec((tr, c), lambda i: (i, piece))],
        out_specs=pl.BlockSpec((tr, c), lambda i: (i, 0)), out_shape=jax.ShapeDtypeStruct((r, c), BF16),
        compiler_params=_params(("parallel",)), name=name,
    )(w)


def _adamw(w, g, m, v):
    m = ADAM_B1 * m + (1.0 - ADAM_B1) * g
    v = ADAM_B2 * v + (1.0 - ADAM_B2) * (g * g)
    m_hat = m / (1.0 - ADAM_B1 ** ADAM_STEP)
    v_hat = v / (1.0 - ADAM_B2 ** ADAM_STEP)
    delta = -ADAM_LR * (m_hat / (jnp.sqrt(v_hat) + ADAM_EPS) + ADAM_WD * w)
    return delta, m, v


def _adam_reduce(parts, w, m, v, name, dep=None, piece=0, npieces=1, prev=None):
    r, c = w.shape
    cp = c // npieces
    tr = min(r, 128)
    extra = ([] if dep is None else [dep]) + ([] if prev is None else list(prev))
    nparts = parts.shape[0]

    def body(p_ref, w_ref, m_ref, v_ref, *rest):
        g_ref, d_ref, nm_ref, nv_ref = rest[len(extra):]
        g = p_ref[0].astype(F32)
        for d in range(1, nparts):
            g = g + p_ref[d].astype(F32)
        g_ref[...] = g
        d_ref[...], nm_ref[...], nv_ref[...] = _adamw(w_ref[...], g, m_ref[...], v_ref[...])

    spec = pl.BlockSpec((tr, cp), lambda i: (i, piece))
    first = 4 + (0 if dep is None else 1)
    return pl.pallas_call(
        body, grid=(r // tr,),
        in_specs=[pl.BlockSpec((nparts, tr, cp), lambda i: (0, i, 0)), spec, spec, spec] + [ANY_SPEC] * len(extra),
        out_specs=[spec] * 4, out_shape=[jax.ShapeDtypeStruct((r, c), F32)] * 4,
        input_output_aliases={} if prev is None else {first + k: k for k in range(4)},
        compiler_params=_params(("parallel",), VMEM_BIG), name=name,
    )(parts, w, m, v, *extra)


def _arrived(x, name, dep=None):
    deps = [] if dep is None else [dep]

    def body(*refs):
        refs[-1][...] = jnp.zeros_like(refs[-1])

    return pl.pallas_call(
        body, in_specs=[ANY_SPEC] * (1 + len(deps)), out_specs=pl.BlockSpec(memory_space=pltpu.VMEM),
        out_shape=jax.ShapeDtypeStruct((8, 128), F32), name=name,
    )(x, *deps)


def _sum_parts(parts, name, dep=None):
    r = parts.shape[1]
    tr = 8
    for cand in (512, 256, 128, 64, 32, 16, 8):
        if r % cand == 0:
            tr = cand
            break
    deps = [] if dep is None else [dep]

    def body(p_ref, *rest):
        g = p_ref[0]
        for d in range(1, NDEV):
            g = g + p_ref[d]
        rest[-1][...] = g

    return pl.pallas_call(
        body, grid=(r // tr,), in_specs=[pl.BlockSpec((NDEV, tr, 128), lambda i: (0, i, 0))] + [ANY_SPEC] * len(deps),
        out_specs=pl.BlockSpec((tr, 128), lambda i: (i, 0)), out_shape=jax.ShapeDtypeStruct((r, 128), F32),
        compiler_params=_params(("parallel",)), name=name,
    )(parts, *deps)


def _sum_unpack(parts, rows, name, dep=None):
    deps = [] if dep is None else [dep]

    def body(p_ref, *outs):
        outs = outs[len(deps):]
        off = 0
        for o_ref, n in zip(outs, rows):
            acc = p_ref[0, off:off + n, :]
            for d in range(1, NDEV):
                acc = acc + p_ref[d, off:off + n, :]
            o_ref[...] = acc
            off += n

    return pl.pallas_call(
        body, grid=(1,), in_specs=[pl.BlockSpec(parts.shape, lambda i: (0, 0, 0))] + [ANY_SPEC] * len(deps),
        out_specs=[pl.BlockSpec((n, 128), lambda i: (0, 0)) for n in rows],
        out_shape=[jax.ShapeDtypeStruct((n, 128), F32) for n in rows],
        compiler_params=_params(("arbitrary",), VMEM_BIG), name=name,
    )(parts, *deps)


def _adam_small(ws, gs, g_specs, ms, vs, name):
    n = len(ws)

    def body(*refs):
        w_r, g_r, m_r, v_r = refs[:n], refs[n:2 * n], refs[2 * n:3 * n], refs[3 * n:4 * n]
        outs = refs[4 * n:]
        for i in range(n):
            g = g_r[i][...]
            outs[4 * i][...] = g
            outs[4 * i + 1][...], outs[4 * i + 2][...], outs[4 * i + 3][...] = _adamw(
                w_r[i][...], g, m_r[i][...], v_r[i][...])

    whole = lambda a: pl.BlockSpec(a.shape, lambda i, nd=a.ndim: (0,) * nd)
    outs = pl.pallas_call(
        body, grid=(1,),
        in_specs=[whole(a) for a in ws] + list(g_specs) + [whole(a) for a in ms] + [whole(a) for a in vs],
        out_specs=[whole(a) for a in ws for _ in range(4)],
        out_shape=[jax.ShapeDtypeStruct(a.shape, F32) for a in ws for _ in range(4)],
        compiler_params=_params(("arbitrary",), VMEM_BIG), name=name,
    )(*ws, *gs, *ms, *vs)
    return [outs[4 * i:4 * i + 4] for i in range(n)]


MASKS = [(mx, my, mc) for mx in (0, 1) for my in (0, 1) for mc in (0, 1)][1:]


def _sc_exchange(name, collective_id, arrays, scatter):
    nt = len(arrays)
    out_type = [jax.ShapeDtypeStruct(a.shape if scatter else (NDEV,) + a.shape, a.dtype) for a in arrays]

    def body(*refs):
        ins, outs = refs[:nt], refs[nt:2 * nt]
        send_sems, recv_sems, local_sems = refs[2 * nt:3 * nt], refs[3 * nt:4 * nt], refs[4 * nt:5 * nt]
        x, y, c = lax.axis_index("x"), lax.axis_index("y"), lax.axis_index("c")
        peers = [(mx + x - 2 * mx * x, my + y - 2 * my * y, mc + c - 2 * mc * c) for mx, my, mc in MASKS]
        barrier = pltpu.get_barrier_semaphore()
        for peer in peers:
            pl.semaphore_signal(barrier, inc=1, device_id=peer, device_id_type=MESH)
        pl.semaphore_wait(barrier, len(peers))
        me = 4 * x + 2 * y + c
        own = []
        for t in range(nt):
            cp = pltpu.make_async_copy(ins[t].at[me] if scatter else ins[t], outs[t].at[me], local_sems[t])
            cp.start()
            own.append(cp)
            for px, py, pc in peers:
                src = ins[t].at[4 * px + 2 * py + pc] if scatter else ins[t]
                pltpu.make_async_remote_copy(src_ref=src, dst_ref=outs[t].at[me], send_sem=send_sems[t],
                                             recv_sem=recv_sems[t], device_id=(px, py, pc), device_id_type=MESH).start()
        for t in range(nt):
            own[t].wait()
            seven = outs[t].at[pl.ds(0, NDEV - 1)]
            drain = pltpu.make_async_remote_copy(src_ref=seven, dst_ref=seven, send_sem=send_sems[t],
                                                 recv_sem=recv_sems[t], device_id=(x, y, c), device_id_type=MESH)
            drain.wait_send()
            drain.wait_recv()

    return pl.kernel(
        body, out_type=out_type, mesh=plsc.ScalarSubcoreMesh(axis_name="sequencer", num_cores=1),
        scratch_types=[pltpu.SemaphoreType.DMA] * (3 * nt),
        compiler_params=pltpu.CompilerParams(collective_id=collective_id), name=name,
    )(*arrays)


def _sc_gather_two_level(name, collective_id, arrays):
    nt = len(arrays)
    out_type = [jax.ShapeDtypeStruct((NDEV,) + a.shape, a.dtype) for a in arrays]

    def body(*refs):
        ins, outs = refs[:nt], refs[nt:2 * nt]
        sems = refs[2 * nt:]
        send_sems, sib_sems, local_sems = sems[:nt], sems[nt:2 * nt], sems[2 * nt:3 * nt]
        ici_sems = [sems[3 * nt + 3 * t:3 * nt + 3 * t + 3] for t in range(nt)]
        x, y, c = lax.axis_index("x"), lax.axis_index("y"), lax.axis_index("c")
        sibling = (x, y, 1 - c)
        chips = [(1 - x, y), (x, 1 - y), (1 - x, 1 - y)]
        barrier = pltpu.get_barrier_semaphore()
        for peer in [sibling] + [(cx, cy, c) for cx, cy in chips]:
            pl.semaphore_signal(barrier, inc=1, device_id=peer, device_id_type=MESH)
        pl.semaphore_wait(barrier, 4)
        me = 4 * x + 2 * y + c

        def push(t, src, slot, recv_sem, to):
            pltpu.make_async_remote_copy(src_ref=src, dst_ref=outs[t].at[slot], send_sem=send_sems[t],
                                         recv_sem=recv_sem, device_id=to, device_id_type=MESH).start()

        own = []
        for t in range(nt):
            cp = pltpu.make_async_copy(ins[t], outs[t].at[me], local_sems[t])
            cp.start()
            own.append(cp)
            for j, (cx, cy) in enumerate(chips):
                push(t, ins[t], me, ici_sems[t][j], (cx, cy, c))
            push(t, ins[t], me, sib_sems[t], sibling)
        for t in range(nt):
            for j, (cx, cy) in enumerate(chips):
                slot = 4 * cx + 2 * cy + c
                landed = outs[t].at[slot]
                pltpu.make_async_remote_copy(src_ref=landed, dst_ref=landed, send_sem=send_sems[t],
                                             recv_sem=ici_sems[t][j], device_id=(cx, cy, c),
                                             device_id_type=MESH).wait_recv()
                push(t, landed, slot, sib_sems[t], sibling)
        for t in range(nt):
            own[t].wait()
            four, seven = outs[t].at[pl.ds(0, 4)], outs[t].at[pl.ds(0, 7)]
            pltpu.make_async_remote_copy(src_ref=four, dst_ref=four, send_sem=send_sems[t], recv_sem=sib_sems[t],
                                         device_id=sibling, device_id_type=MESH).wait_recv()
            pltpu.make_async_remote_copy(src_ref=seven, dst_ref=seven, send_sem=send_sems[t], recv_sem=sib_sems[t],
                                         device_id=sibling, device_id_type=MESH).wait_send()

    return pl.kernel(
        body, out_type=out_type, mesh=plsc.ScalarSubcoreMesh(axis_name="sequencer", num_cores=1),
        scratch_types=[pltpu.SemaphoreType.DMA] * (6 * nt),
        compiler_params=pltpu.CompilerParams(collective_id=collective_id), name=name,
    )(*arrays)


def _sc_sibling_exchange(name, collective_id, src, out_shape, pieces):
    def body(src_ref, out_ref, send_sem, recv_sem):
        x, y, c = lax.axis_index("x"), lax.axis_index("y"), lax.axis_index("c")
        sibling = (x, y, 1 - c)
        barrier = pltpu.get_barrier_semaphore()
        pl.semaphore_signal(barrier, inc=1, device_id=sibling, device_id_type=MESH)
        pl.semaphore_wait(barrier, 1)
        for piece, lands in pieces(c, src_ref, out_ref):
            pltpu.make_async_remote_copy(src_ref=piece, dst_ref=lands, send_sem=send_sem, recv_sem=recv_sem,
                                         device_id=sibling, device_id_type=MESH).start()
        drain = pltpu.make_async_remote_copy(src_ref=out_ref, dst_ref=out_ref, send_sem=send_sem, recv_sem=recv_sem,
                                             device_id=sibling, device_id_type=MESH)
        drain.wait_send()
        drain.wait_recv()

    return pl.kernel(
        body, out_type=jax.ShapeDtypeStruct(out_shape, src.dtype),
        mesh=plsc.ScalarSubcoreMesh(axis_name="sequencer", num_cores=1), scratch_types=[pltpu.SemaphoreType.DMA] * 2,
        compiler_params=pltpu.CompilerParams(collective_id=collective_id), name=name,
    )(src)


def _swap_class_columns(name, collective_id, dz, nb, piece=0, npieces=1):
    w = nb // npieces
    return _sc_sibling_exchange(
        name, collective_id, dz, (S, 4 * w),
        lambda c, src, out: [(src.at[:, pl.ds((2 * j + 1 - c) * nb + piece * w, w)], out.at[:, pl.ds(j * w, w)])
                             for j in range(4)])


def _sc_chip_scatter(name, collective_id, q):
    def body(q_ref, out_ref, send_sem, recv_sem, local_sem):
        x, y, c = lax.axis_index("x"), lax.axis_index("y"), lax.axis_index("c")
        chips = [(1 - x, y), (x, 1 - y), (1 - x, 1 - y)]
        barrier = pltpu.get_barrier_semaphore()
        for cx, cy in chips:
            pl.semaphore_signal(barrier, inc=1, device_id=(cx, cy, c), device_id_type=MESH)
        pl.semaphore_wait(barrier, 3)
        mine = 2 * x + y
        own = pltpu.make_async_copy(q_ref.at[mine], out_ref.at[mine], local_sem)
        own.start()
        for cx, cy in chips:
            pltpu.make_async_remote_copy(src_ref=q_ref.at[2 * cx + cy], dst_ref=out_ref.at[mine], send_sem=send_sem,
                                         recv_sem=recv_sem, device_id=(cx, cy, c), device_id_type=MESH).start()
        own.wait()
        three = out_ref.at[pl.ds(0, 3)]
        drain = pltpu.make_async_remote_copy(src_ref=three, dst_ref=three, send_sem=send_sem, recv_sem=recv_sem,
                                             device_id=(x, y, c), device_id_type=MESH)
        drain.wait_send()
        drain.wait_recv()

    return pl.kernel(
        body, out_type=jax.ShapeDtypeStruct(q.shape, q.dtype),
        mesh=plsc.ScalarSubcoreMesh(axis_name="sequencer", num_cores=1), scratch_types=[pltpu.SemaphoreType.DMA] * 3,
        compiler_params=pltpu.CompilerParams(collective_id=collective_id), name=name,
    )(q)


def _mm_pair_dw(h_own, dz, h_sib, dz_sib, nb, name, dep=None, piece=0, npieces=1):
    nb = nb // npieces
    tn = 512 if nb % 512 == 0 else nb
    per = nb // tn
    o_spec = pl.BlockSpec((None, D, tn), lambda i, j, k: (j // per, 0, j % per))
    own_col = lambda i, j, k: (0, ((2 * (j // per) + lax.axis_index("c")) * npieces + piece) * per + j % per)
    part = _matmul(
        h_own, dz, dn=TN, grid=(1, 4 * per, 1),
        a_spec=pl.BlockSpec((S, D), lambda i, j, k: (0, 0)), b_spec=pl.BlockSpec((S, tn), own_col),
        o_spec=o_spec, out_shape=(4, D, nb), out_dtype=F32, acc_shape=(D, tn), name=name + "_own", dep=dep)

    def body(a_ref, b_ref, p_ref, o_ref):
        o_ref[...] = (p_ref[...] + _dot(a_ref[...], b_ref[...], TN)).astype(BF16)

    return pl.pallas_call(
        body, grid=(1, 4 * per, 1),
        in_specs=[pl.BlockSpec((S, D), lambda i, j, k: (0, 0)), pl.BlockSpec((S, tn), lambda i, j, k: (0, j)), o_spec],
        out_specs=o_spec, out_shape=jax.ShapeDtypeStruct((4, D, nb), BF16),
        compiler_params=_params(("parallel", "parallel", "arbitrary"), VMEM_BIG), name=name + "_sibling",
    )(h_sib, dz_sib, part)


SMALL = {
    "e_pre_norm": ((2048,), None), "e_pool_w": ((4, 256, 256), 1), "e_pool_scale": ((1024,), None),
    "e_post_norm": ((2048,), None), "o_pre_norm": ((2048,), 0), "o_sgu_norm_g": ((1024,), 0),
    "o_sgu_norm_b": ((1024,), 0), "o_sgu_w": ((4, 128, 128), None), "o_sgu_b": ((4, 128), None),
    "o_conv_w": ((31, 1024), 1), "o_conv_b": ((1024,), 0), "o_conv_norm_g": ((1024,), 0),
    "o_conv_norm_b": ((1024,), 0), "o_post_norm": ((2048,), 0),
}
SMALL_SHARDED = [n for n, (_, ax) in SMALL.items() if ax is not None]


def _shard_shape(name):
    shape, ax = SMALL[name]
    if ax is None:
        return shape
    return tuple(s // NDEV if i == ax else s for i, s in enumerate(shape))


def _pack(arrs, row_multiple=1):
    flat = jnp.concatenate([a.reshape(-1) for a in arrs])
    pad = -flat.shape[0] % (128 * row_multiple)
    return jnp.concatenate([flat, jnp.zeros((pad,), F32)]).reshape(-1, 128)


def _small_views(name):
    shape, ax = SMALL[name]
    me = lambda: 4 * lax.axis_index("x") + 2 * lax.axis_index("y") + lax.axis_index("c")
    if ax is None:
        view = (int(np.prod(shape)) // 128, 128)
        return view, view, pl.BlockSpec(view, lambda i: (0, 0))
    if len(shape) == 1:
        n = shape[0] // NDEV
        return (1, n), (NDEV, 1, n), pl.BlockSpec((None, 1, n), lambda i: (me(), 0, 0))
    part = _shard_shape(name)
    return part, shape, pl.BlockSpec(part, lambda i: tuple(me() if d == ax else 0 for d in range(len(shape))))


BIG = ("e_w_in", "e_w_out", "o_w_in", "o_w_out")
WEIGHTS = ["e_pre_norm", "e_w_in", "e_pool_w", "e_pool_scale", "e_w_out", "e_post_norm", "o_pre_norm", "o_w_in",
           "o_sgu_norm_g", "o_sgu_norm_b", "o_sgu_w", "o_sgu_b", "o_conv_w", "o_conv_b", "o_conv_norm_g",
           "o_conv_norm_b", "o_w_out", "o_post_norm"]


def kernel(x, e_pre_norm, e_w_in, e_pool_w, e_pool_scale, e_w_out, e_post_norm, o_pre_norm, o_w_in, o_sgu_norm_g, o_sgu_norm_b, o_sgu_w, o_sgu_b, o_conv_w, o_conv_b, o_conv_norm_g, o_conv_norm_b, o_w_out, o_post_norm, loss_target, m_e_pre_norm, m_e_w_in, m_e_pool_w, m_e_pool_scale, m_e_w_out, m_e_post_norm, m_o_pre_norm, m_o_w_in, m_o_sgu_norm_g, m_o_sgu_norm_b, m_o_sgu_w, m_o_sgu_b, m_o_conv_w, m_o_conv_b, m_o_conv_norm_g, m_o_conv_norm_b, m_o_w_out, m_o_post_norm, v_e_pre_norm, v_e_w_in, v_e_pool_w, v_e_pool_scale, v_e_w_out, v_e_post_norm, v_o_pre_norm, v_o_w_in, v_o_sgu_norm_g, v_o_sgu_norm_b, v_o_sgu_w, v_o_sgu_b, v_o_conv_w, v_o_conv_b, v_o_conv_norm_g, v_o_conv_norm_b, v_o_w_out, v_o_post_norm):
    given = dict(locals())
    w = {n: given[n][0] for n in WEIGHTS}
    m = {n: given["m_" + n][0] for n in WEIGHTS}
    v = {n: given["v_" + n][0] for n in WEIGHTS}
    me = 4 * lax.axis_index("x") + 2 * lax.axis_index("y") + lax.axis_index("c")
    x, target = x[0], loss_target[0]
    row = lambda a: a.reshape(1, -1)

    lo, small_rows = _sc_gather_two_level(
        "gather_a0", 0, [_cast_bf16(w["e_w_in"], "cast_e_w_in_0", 0, 2), _pack([w[n] for n in SMALL_SHARDED])])
    hi, = _sc_gather_two_level("gather_a1", 12, [_cast_bf16(w["e_w_in"], "cast_e_w_in_1", 1, 2)])
    wg_e_in = (lo, hi)
    h0 = _pre0_fwd(x, row(w["e_pre_norm"]))
    wg_e_out, = _sc_gather_two_level("gather_b", 1, [_cast_bf16(w["e_w_out"], "cast_e_w_out")])
    wg_o_in, wg_o_out = _sc_gather_two_level(
        "gather_c", 13, [_cast_bf16(w[n], "cast_" + n) for n in ("o_w_in", "o_w_out")])
    h0_sib = _sc_sibling_exchange("swap_h0", 8, h0, h0.shape, lambda c, src, out: [(src, out)])
    p = {n: w[n] for n in SMALL if SMALL[n][1] is None}
    small_rows = small_rows.reshape(NDEV, -1)
    off = 0
    for n in SMALL_SHARDED:
        shp, ax = _shard_shape(n), SMALL[n][1]
        cnt = int(np.prod(shp))
        blk = small_rows[:, off:off + cnt].reshape((NDEV,) + shp)
        p[n] = jnp.moveaxis(blk, 0, ax).reshape(SMALL[n][0])
        off += cnt
    tabs = _rope_tables()
    pool_w_bf = p["e_pool_w"].astype(BF16)
    sgu_bb = jnp.broadcast_to(p["o_sgu_b"][:, :, None], (4, 128, 128))
    conv_w = jnp.concatenate([p["o_conv_w"], jnp.zeros((HALO - CONV_K, HALF), F32)], axis=0)
    odd_p = (row(p["o_sgu_norm_g"]), row(p["o_sgu_norm_b"]), p["o_sgu_w"], sgu_bb, conv_w,
             row(p["o_conv_b"]), row(p["o_conv_norm_g"]), row(p["o_conv_norm_b"]))

    z0 = _mm_in_halves(h0, wg_e_in, "mm_z0")
    ycat0 = _pool_fwd(z0, pool_w_bf, row(p["e_pool_scale"]))
    qkv = _qkv_prep(z0, tabs)
    ycat0, og, lg = _attn_fwd(z0, qkv, ycat0)
    w_out_e, w_out_o = wg_e_out.reshape(2048, D), wg_o_out.reshape(2048, D)
    y0 = _mm_out(ycat0, w_out_e, "mm_y0", h0_sib)
    x1, h1 = _post0_fwd(x, y0, row(p["e_post_norm"]), row(p["o_pre_norm"]))
    h1_sib = _sc_sibling_exchange("swap_h1", 11, h1, h1.shape, lambda c, src, out: [(src, out)])
    z1 = _mm_in(h1, wg_o_in, "mm_z1")
    ycat1 = _odd_fwd(z1, *odd_p)
    y1 = _mm_out(ycat1, w_out_o, "mm_y1", h1_sib)

    g = {}
    loss, dx2, dy1, g["o_post_norm"] = _post1_bwd(y1, x1, target, row(p["o_post_norm"]))
    loss = lax.psum(loss[0, 0], ("x", "y", "c"))
    parts = {}
    dw = _mm_out_dw(ycat1, dy1, "mm_dwout1").reshape(NDEV, 256, D)
    parts["o_w_out"], = _sc_exchange("scatter_o_w_out", 2, [dw], True)
    dycat1 = _mm_out_dx(dy1, w_out_o, "mm_dycat1", (dw, loss.reshape(1, 1)))
    dz1, ddc, g["o_sgu_w"], d_sgu_bb, g["o_sgu_norm_g"], g["o_sgu_norm_b"], g["o_conv_norm_g"], \
        g["o_conv_norm_b"], g["o_conv_b"] = _odd_bwd_a(z1, dycat1, *odd_p)
    dz1, d_conv_w = _odd_bwd_b(z1, ddc, dz1, conv_w)
    g["o_sgu_b"] = d_sgu_bb[:, :, 0]
    g["o_conv_w"] = d_conv_w[:CONV_K]
    grads, deltas, new_m, new_v = {}, {}, {}, {}

    def adam(n, dep):
        grads[n], deltas[n], new_m[n], new_v[n] = _adam_reduce(parts[n], w[n], m[n], v[n], "adam_" + n, dep)
        return new_v[n]

    pin = _arrived(parts["o_w_out"], "arrived_o_w_out", d_conv_w)
    dz1_sib = _swap_class_columns("swap_dz1", 10, dz1, ODD_IN // NDEV)
    dw = _mm_pair_dw(h1, dz1, h1_sib, dz1_sib, ODD_IN // NDEV, "mm_dwin1", pin)
    parts["o_w_in"] = _sc_chip_scatter("scatter_o_w_in", 3, dw)
    dh1 = _mm_in_dx(dz1, wg_o_in, "mm_dh1", dw)
    dx1, dy0, g["o_pre_norm"], g["e_post_norm"] = _mid_bwd(dx2, dh1, x1, y0, row(p["o_pre_norm"]),
                                                           row(p["e_post_norm"]))
    dw = _mm_out_dw(ycat0, dy0, "mm_dwout0").reshape(NDEV, 256, D)
    parts["e_w_out"], = _sc_exchange("scatter_e_w_out", 4, [dw], True)
    dycat0 = _mm_out_dx(dy0, w_out_e, "mm_dycat0", dw)
    da_in, da_gate, g["e_pool_w"], g["e_pool_scale"] = _pool_bwd(z0, dycat0, pool_w_bf, row(p["e_pool_scale"]))
    late = [n for n in SMALL if n not in ("e_pre_norm", "o_sgu_b")] + ["o_sgu_b"]
    recv_small, = _sc_gather_two_level("gather_small_grads", 6,
                                       [_pack([g[n].reshape(SMALL[n][0]) for n in late], 512)])
    took = _arrived(parts["o_w_in"], "arrived_o_w_in")
    dq, dk, dv, dbg = _attn_bwd(z0, qkv, og, lg, dycat0, tabs, took)
    dz0 = jnp.concatenate([da_in, da_gate, dq, dk, dv, dbg], axis=1)
    took = _arrived(recv_small, "arrived_small_grads", _arrived(parts["e_w_out"], "arrived_e_w_out", dz0))
    nb = EVEN_IN // NDEV
    swapped = [_swap_class_columns("swap_dz0_%d" % half, (9, 14)[half], dz0, nb, half, 2) for half in (0, 1)]
    dw, e_w_in_parts = took, []
    for half in (0, 1):
        dw = _mm_pair_dw(h0, dz0, h0_sib, swapped[half], nb, "mm_dwin0_%d" % half, dw, half, 2)
        e_w_in_parts.append(_sc_chip_scatter("scatter_e_w_in_%d" % half, (5, 15)[half], dw))
    pin = adam("e_w_out", adam("o_w_out", adam("o_w_in", dw)))
    rows = [int(np.prod(SMALL[n][0])) // 128 for n in late]
    summed = dict(zip(late, _sum_unpack(recv_small, rows, "sum_small_grads", pin)))
    dh0 = _mm_in_dx_halves(dz0, wg_e_in, "mm_dh0", summed[late[0]])
    grad_x, g["e_pre_norm"] = _pre0_bwd(dx1, dh0, x, row(p["e_pre_norm"]))
    last, = _sc_exchange("gather_e_pre_norm_grad", 7, [g["e_pre_norm"].reshape(16, 128)], False)

    n = "e_w_in"
    out = _adam_reduce(e_w_in_parts[0], w[n], m[n], v[n], "adam_e_w_in_0", grad_x, 0, 2)
    out = _adam_reduce(e_w_in_parts[1], w[n], m[n], v[n], "adam_e_w_in_1", None, 1, 2, out)
    grads[n], deltas[n], new_m[n], new_v[n] = out
    summed["e_pre_norm"] = _sum_parts(last, "sum_e_pre_norm_grad", out[3])
    names = list(SMALL)
    views = [_small_views(n) for n in names]
    mine = lambda src: [src[n].reshape(vw[0]) for n, vw in zip(names, views)]
    res = _adam_small(mine(w), [summed[n].reshape(vw[1]) for n, vw in zip(names, views)], [vw[2] for vw in views],
                      mine(m), mine(v), "adam_small")
    for n, out in zip(names, res):
        grads[n], deltas[n], new_m[n], new_v[n] = [t.reshape(_shard_shape(n)) for t in out]

    lead = lambda a: a[None]
    return (loss, grad_x[None], *[lead(grads[n]) for n in WEIGHTS], *[lead(deltas[n]) for n in WEIGHTS],
            *[lead(new_m[n]) for n in WEIGHTS], *[lead(new_v[n]) for n in WEIGHTS])
```

```python
import functools

import numpy as np
import jax
import jax.numpy as jnp
from jax import lax
from jax.experimental import pallas as pl
from jax.experimental.pallas import tpu as pltpu
from jax.experimental.pallas import tpu_sc as plsc

F32 = jnp.float32
BF16 = jnp.bfloat16

S = 2048
D = 2048
NDEV = 8
EPS = 1e-6
NEG = -1e30
HEAD_DIM = 128
ROT_DIM = 32
ROPE_THETA = 500000.0
PATTERNS = ((128, 1), (512, 4), (2048, 16))
BLK = 128
EVEN_IN = 12288
ODD_IN = 6144
HALF = 1024
CONV_K = 31
HALO = 32
TR = 256
SUB = 32

ADAM_LR = 0.001
ADAM_B1 = 0.9
ADAM_B2 = 0.999
ADAM_EPS = 1e-08
ADAM_WD = 0.01
ADAM_STEP = 10

VMEM_BIG = 56 * 1024 * 1024
MESH = pl.DeviceIdType.MESH

NN = (((1,), (0,)), ((), ()))
NT = (((1,), (1,)), ((), ()))
TN = (((0,), (0,)), ((), ()))


def _dot(a, b, dn=NN):
    return lax.dot_general(a, b, dn, preferred_element_type=F32)


def _sigmoid(x):
    return 1.0 / (1.0 + jnp.exp(-x))


def _silu_and_grad(x):
    sg = _sigmoid(x)
    return x * sg, sg * (1.0 + x * (1.0 - sg))


def _params(sem, vmem=None):
    return pltpu.CompilerParams(dimension_semantics=sem, vmem_limit_bytes=vmem)


ANY_SPEC = pl.BlockSpec(memory_space=pl.ANY)


def _matmul(a, b, *, dn, grid, a_spec, b_spec, o_spec, out_shape, out_dtype, acc_shape, name, dep=None):
    nk = grid[2]
    deps = [] if dep is None else list(dep) if isinstance(dep, (tuple, list)) else [dep]

    def body(a_ref, b_ref, *rest):
        o_ref, acc = rest[len(deps)], rest[len(deps) + 1:]
        if nk == 1:
            o_ref[...] = _dot(a_ref[...], b_ref[...], dn).astype(o_ref.dtype)
            return
        acc_ref = acc[0]
        k = pl.program_id(2)

        @pl.when(k == 0)
        def _():
            acc_ref[...] = jnp.zeros_like(acc_ref)

        acc_ref[...] += _dot(a_ref[...], b_ref[...], dn)

        @pl.when(k == nk - 1)
        def _():
            o_ref[...] = acc_ref[...].astype(o_ref.dtype)

    return pl.pallas_call(
        body, grid=grid, in_specs=[a_spec, b_spec] + [ANY_SPEC] * len(deps), out_specs=o_spec,
        out_shape=jax.ShapeDtypeStruct(out_shape, out_dtype),
        scratch_shapes=[] if nk == 1 else [pltpu.VMEM(acc_shape, F32)],
        compiler_params=_params(("parallel", "parallel", "arbitrary"), VMEM_BIG), name=name,
    )(a, b, *deps)


TM = 2048


def _mm_in(h, wg, name):
    nb = wg.shape[2]
    tn = 512 if nb % 512 == 0 else nb
    per = nb // tn
    return _matmul(
        h, wg, dn=NN, grid=(S // TM, NDEV * per, 1),
        a_spec=pl.BlockSpec((TM, D), lambda i, j, k: (i, 0)),
        b_spec=pl.BlockSpec((None, D, tn), lambda i, j, k: (j // per, 0, j % per)),
        o_spec=pl.BlockSpec((TM, tn), lambda i, j, k: (i, j)),
        out_shape=(S, NDEV * nb), out_dtype=F32, acc_shape=(TM, tn), name=name)


def _mm_in_halves(h, wg_halves, name):
    hb = wg_halves[0].shape[2]
    z = None
    for half, wg in enumerate(wg_halves):
        prev = [] if z is None else [z]

        def body(a_ref, b_ref, *rest):
            rest[-1][...] = _dot(a_ref[...], b_ref[...])

        z = pl.pallas_call(
            body, grid=(NDEV,),
            in_specs=[pl.BlockSpec((S, D), lambda j: (0, 0)), pl.BlockSpec((None, D, hb), lambda j: (j, 0, 0))]
                     + [ANY_SPEC] * len(prev),
            out_specs=pl.BlockSpec((S, hb), lambda j, half=half: (0, 2 * j + half)),
            out_shape=jax.ShapeDtypeStruct((S, 2 * NDEV * hb), F32),
            input_output_aliases={2: 0} if prev else {},
            compiler_params=_params(("parallel",), VMEM_BIG), name="%s_%d" % (name, half),
        )(h, wg, *prev)
    return z


def _mm_in_dx_halves(dz, wg_halves, name, dep):
    hb = wg_halves[0].shape[2]
    nk = 2 * NDEV

    def body(a_ref, b0_ref, b1_ref, dep_ref, o_ref, acc_ref):
        k = pl.program_id(2)

        @pl.when(k == 0)
        def _():
            acc_ref[...] = jnp.zeros_like(acc_ref)

        @pl.when(k % 2 == 0)
        def _():
            acc_ref[...] += _dot(a_ref[...], b0_ref[...], NT)

        @pl.when(k % 2 == 1)
        def _():
            acc_ref[...] += _dot(a_ref[...], b1_ref[...], NT)

        @pl.when(k == nk - 1)
        def _():
            o_ref[...] = acc_ref[...]

    b_spec = pl.BlockSpec((None, 1024, hb), lambda i, j, k: (k // 2, j, 0))
    return pl.pallas_call(
        body, grid=(1, D // 1024, nk),
        in_specs=[pl.BlockSpec((S, hb), lambda i, j, k: (0, k)), b_spec, b_spec, ANY_SPEC],
        out_specs=pl.BlockSpec((S, 1024), lambda i, j, k: (0, j)), out_shape=jax.ShapeDtypeStruct((S, D), F32),
        scratch_shapes=[pltpu.VMEM((S, 1024), F32)],
        compiler_params=_params(("parallel", "parallel", "arbitrary"), VMEM_BIG), name=name,
    )(dz, *wg_halves, dep)


def _mm_in_dx(dz, wg, name, dep=None):
    nb = wg.shape[2]
    return _matmul(
        dz, wg, dn=NT, grid=(S // TM, D // 1024, NDEV),
        a_spec=pl.BlockSpec((TM, nb), lambda i, j, k: (i, k)),
        b_spec=pl.BlockSpec((None, 1024, nb), lambda i, j, k: (k, j, 0)),
        o_spec=pl.BlockSpec((TM, 1024), lambda i, j, k: (i, j)),
        out_shape=(S, D), out_dtype=F32, acc_shape=(TM, 1024), name=name, dep=dep)


def _mm_out(yc, w, name, dep=None):
    return _matmul(
        yc, w, dn=NN, grid=(S // TM, D // 512, 1),
        a_spec=pl.BlockSpec((TM, 2048), lambda i, j, k: (i, 0)),
        b_spec=pl.BlockSpec((2048, 512), lambda i, j, k: (0, j)),
        o_spec=pl.BlockSpec((TM, 512), lambda i, j, k: (i, j)),
        out_shape=(S, D), out_dtype=F32, acc_shape=(TM, 512), name=name, dep=dep)


def _mm_out_dx(dy, w, name, dep=None):
    return _matmul(
        dy, w, dn=NT, grid=(S // TM, 2048 // 512, 1),
        a_spec=pl.BlockSpec((TM, D), lambda i, j, k: (i, 0)),
        b_spec=pl.BlockSpec((512, D), lambda i, j, k: (j, 0)),
        o_spec=pl.BlockSpec((TM, 512), lambda i, j, k: (i, j)),
        out_shape=(S, 2048), out_dtype=F32, acc_shape=(TM, 512), name=name, dep=dep)


def _mm_out_dw(yc, dy, name):
    return _matmul(
        yc, dy, dn=TN, grid=(2048 // TM, D // 512, 1),
        a_spec=pl.BlockSpec((S, TM), lambda i, j, k: (0, i)),
        b_spec=pl.BlockSpec((S, 512), lambda i, j, k: (0, j)),
        o_spec=pl.BlockSpec((TM, 512), lambda i, j, k: (i, j)),
        out_shape=(2048, D), out_dtype=BF16, acc_shape=(TM, 512), name=name)


def _row_spec(w=D):
    return pl.BlockSpec((TR, w), lambda i: (i, 0))


def _vec_spec(w=D):
    return pl.BlockSpec((1, w), lambda i: (0, 0))


def _rms_stats(x):
    r = lax.rsqrt(jnp.mean(x * x, axis=-1, keepdims=True) + EPS)
    return x * r, r


def _rms_bwd(dn, xhat, r, g):
    dxh = dn * g
    return r * (dxh - xhat * jnp.mean(dxh * xhat, axis=-1, keepdims=True))


def _acc_rows(ref, val, i):
    s = jnp.sum(val, axis=0, keepdims=True)

    @pl.when(i == 0)
    def _():
        ref[...] = s

    @pl.when(i > 0)
    def _():
        ref[...] += s


def _pre0_fwd(x, g, dep=None):
    deps = [] if dep is None else [dep]

    def body(x_ref, g_ref, *rest):
        xhat, _ = _rms_stats(x_ref[...])
        rest[-1][...] = (xhat * g_ref[...]).astype(BF16)

    return pl.pallas_call(
        body, grid=(S // TR,), in_specs=[_row_spec(), _vec_spec()] + [ANY_SPEC] * len(deps), out_specs=_row_spec(),
        out_shape=jax.ShapeDtypeStruct((S, D), BF16), compiler_params=_params(("parallel",)), name="pre0_fwd",
    )(x, g, *deps)


def _post0_fwd(x, y0, g_post, g_pre1):
    def body(x_ref, y_ref, gp_ref, g1_ref, x1_ref, h1_ref):
        yhat, _ = _rms_stats(y_ref[...])
        x1 = x_ref[...] + yhat * gp_ref[...]
        x1_ref[...] = x1
        xhat, _ = _rms_stats(x1)
        h1_ref[...] = (xhat * g1_ref[...]).astype(BF16)

    return pl.pallas_call(
        body, grid=(S // TR,), in_specs=[_row_spec(), _row_spec(), _vec_spec(), _vec_spec()],
        out_specs=[_row_spec(), _row_spec()],
        out_shape=[jax.ShapeDtypeStruct((S, D), F32), jax.ShapeDtypeStruct((S, D), BF16)],
        compiler_params=_params(("parallel",)), name="post0_fwd",
    )(x, y0, g_post, g_pre1)


def _post1_bwd(ycat, w_out, x1, target, g_post, dep):
    def body(yc_ref, w_ref, x1_ref, t_ref, g_ref, dep_ref, loss_ref, dx2_ref, dy_ref, dg_ref):
        i = pl.program_id(0)
        yhat, r = _rms_stats(_dot(yc_ref[...], w_ref[...]))
        g = g_ref[...]
        err = x1_ref[...] + yhat * g - t_ref[...]
        part = jnp.sum(jnp.sum(err * err, axis=-1, keepdims=True), axis=0, keepdims=True) * (0.5 / D)
        _acc_rows(loss_ref, jnp.broadcast_to(part, (1, 128)), i)
        dx2 = err * (1.0 / D)
        dx2_ref[...] = dx2
        _acc_rows(dg_ref, dx2 * yhat, i)
        dy_ref[...] = _rms_bwd(dx2, yhat, r, g).astype(BF16)

    return pl.pallas_call(
        body, grid=(S // TR,),
        in_specs=[_row_spec(), pl.BlockSpec((2048, D), lambda i: (0, 0)), _row_spec(), _row_spec(), _vec_spec(),
                  ANY_SPEC],
        out_specs=[_vec_spec(128), _row_spec(), _row_spec(), _vec_spec()],
        out_shape=[jax.ShapeDtypeStruct((1, 128), F32), jax.ShapeDtypeStruct((S, D), F32),
                   jax.ShapeDtypeStruct((S, D), BF16), jax.ShapeDtypeStruct((1, D), F32)],
        compiler_params=_params(("arbitrary",), VMEM_BIG), name="post1_bwd",
    )(ycat, w_out, x1, target, g_post, dep)


def _mid_bwd(dx2, dh1, x1, y0, g_pre1, g_post0):
    def body(dx2_ref, dh_ref, x1_ref, y_ref, g1_ref, gp_ref, dx1_ref, dy_ref, dg1_ref, dgp_ref):
        i = pl.program_id(0)
        xhat, r1 = _rms_stats(x1_ref[...])
        dh = dh_ref[...]
        _acc_rows(dg1_ref, dh * xhat, i)
        dx1 = dx2_ref[...] + _rms_bwd(dh, xhat, r1, g1_ref[...])
        dx1_ref[...] = dx1
        yhat, r0 = _rms_stats(y_ref[...])
        _acc_rows(dgp_ref, dx1 * yhat, i)
        dy_ref[...] = _rms_bwd(dx1, yhat, r0, gp_ref[...]).astype(BF16)

    return pl.pallas_call(
        body, grid=(S // TR,),
        in_specs=[_row_spec(), _row_spec(), _row_spec(), _row_spec(), _vec_spec(), _vec_spec()],
        out_specs=[_row_spec(), _row_spec(), _vec_spec(), _vec_spec()],
        out_shape=[jax.ShapeDtypeStruct((S, D), F32), jax.ShapeDtypeStruct((S, D), BF16),
                   jax.ShapeDtypeStruct((1, D), F32), jax.ShapeDtypeStruct((1, D), F32)],
        compiler_params=_params(("arbitrary",)), name="mid_bwd",
    )(dx2, dh1, x1, y0, g_pre1, g_post0)


def _pre0_bwd(dx1, dh0, x, g):
    def body(dx1_ref, dh_ref, x_ref, g_ref, gx_ref, dg_ref):
        i = pl.program_id(0)
        xhat, r = _rms_stats(x_ref[...])
        dh = dh_ref[...]
        _acc_rows(dg_ref, dh * xhat, i)
        gx_ref[...] = dx1_ref[...] + _rms_bwd(dh, xhat, r, g_ref[...])

    return pl.pallas_call(
        body, grid=(S // TR,), in_specs=[_row_spec(), _row_spec(), _row_spec(), _vec_spec()],
        out_specs=[_row_spec(), _vec_spec()],
        out_shape=[jax.ShapeDtypeStruct((S, D), F32), jax.ShapeDtypeStruct((1, D), F32)],
        compiler_params=_params(("arbitrary",)), name="pre0_bwd",
    )(dx1, dh0, x, g)


POOL_CH = 256


def _pool_apply(a, w, transpose):
    n = a.shape[0]
    row = lax.broadcasted_iota(jnp.int32, a.shape, 0)
    cnt = jnp.minimum(row + 1, w).astype(F32)
    s = a / cnt if transpose else a
    for k in (1, 2, 4, 8):
        if transpose:
            sh = jnp.where(row < n - k, pltpu.roll(s, n - k, 0), 0.0)
        else:
            sh = jnp.where(row >= k, pltpu.roll(s, k, 0), 0.0)
        s = jnp.where(w > k, s + sh, s)
    return s - a if transpose else s / cnt - a


def _pool_fwd(z0, pool_w, pool_scale):
    def body(a_ref, gate_ref, w_ref, sc_ref, out_ref):
        win = jnp.left_shift(2, pl.program_id(0))
        pooled = _pool_apply(a_ref[...], win, False)
        mixed = _dot(pooled.astype(BF16), w_ref[...])
        gate = gate_ref[...]
        out_ref[...] = (mixed * sc_ref[...] * (gate * _sigmoid(gate))).astype(BF16)

    return pl.pallas_call(
        body, grid=(4,),
        in_specs=[pl.BlockSpec((S, POOL_CH), lambda g: (0, g)), pl.BlockSpec((S, POOL_CH), lambda g: (0, 4 + g)),
                  pl.BlockSpec((None, POOL_CH, POOL_CH), lambda g: (g, 0, 0)),
                  pl.BlockSpec((1, POOL_CH), lambda g: (0, g))],
        out_specs=pl.BlockSpec((S, POOL_CH), lambda g: (0, g)),
        out_shape=jax.ShapeDtypeStruct((S, 2048), BF16),
        compiler_params=_params(("parallel",), VMEM_BIG), name="pool_fwd",
    )(z0, z0, pool_w, pool_scale)


def _pool_bwd(z0, dycat, pool_w, pool_scale):
    def body(a_ref, gate_ref, dy_ref, w_ref, sc_ref, da_ref, dgate_ref, dw_ref, dsc_ref):
        win = jnp.left_shift(2, pl.program_id(0))
        pooled = _pool_apply(a_ref[...], win, False).astype(BF16)
        w = w_ref[...]
        mixed = _dot(pooled, w)
        silu, dsilu = _silu_and_grad(gate_ref[...])
        dy = dy_ref[...]
        sc = sc_ref[...]
        dgate_ref[...] = (dy * (mixed * sc) * dsilu).astype(BF16)
        dms = dy * silu
        dsc_ref[...] = jnp.sum(dms * mixed, axis=0, keepdims=True)
        dmixed = (dms * sc).astype(BF16)
        dw_ref[...] = _dot(pooled, dmixed, TN)
        dpooled = _dot(dmixed, w, NT)
        da_ref[...] = _pool_apply(dpooled, win, True).astype(BF16)

    slab = lambda off: pl.BlockSpec((S, POOL_CH), lambda g: (0, off + g))
    return pl.pallas_call(
        body, grid=(4,),
        in_specs=[slab(0), slab(4), slab(0), pl.BlockSpec((None, POOL_CH, POOL_CH), lambda g: (g, 0, 0)),
                  pl.BlockSpec((1, POOL_CH), lambda g: (0, g))],
        out_specs=[slab(0), slab(0), pl.BlockSpec((None, POOL_CH, POOL_CH), lambda g: (g, 0, 0)),
                   pl.BlockSpec((1, POOL_CH), lambda g: (0, g))],
        out_shape=[jax.ShapeDtypeStruct((S, HALF), BF16), jax.ShapeDtypeStruct((S, HALF), BF16),
                   jax.ShapeDtypeStruct((4, POOL_CH, POOL_CH), F32), jax.ShapeDtypeStruct((1, HALF), F32)],
        compiler_params=_params(("parallel",), VMEM_BIG), name="pool_bwd",
    )(z0, z0, dycat, pool_w, pool_scale)


Q_COL, K_COL, V_COL, BG_COL = 2048 // 128, 5120 // 128, 8192 // 128, 11264 // 128
SCALE = HEAD_DIM ** -0.5


def _rope_tables():
    pos = jnp.arange(S, dtype=F32)
    inv_freq = jnp.power(ROPE_THETA, -jnp.arange(0, ROT_DIM, 2, dtype=F32) / ROT_DIM)
    ang = pos[:, None] * inv_freq[None, :]
    cos, sin = jnp.cos(ang), jnp.sin(ang)
    half = ROT_DIM // 2
    zeros = jnp.zeros((S, HEAD_DIM - ROT_DIM), F32)
    c = jnp.concatenate([cos, cos, jnp.ones((S, HEAD_DIM - ROT_DIM), F32)], axis=1)
    a = jnp.concatenate([-sin, jnp.zeros((S, half), F32), zeros], axis=1)
    b = jnp.concatenate([jnp.zeros((S, half), F32), sin, zeros], axis=1)
    return c, a, b


def _rope(t, c, a, b):
    half = ROT_DIM // 2
    return t * c + pltpu.roll(t, HEAD_DIM - half, 1) * a + pltpu.roll(t, half, 1) * b


def _rope_t(d, c, a, b):
    half = ROT_DIM // 2
    return d * c + pltpu.roll(d * a, half, 1) + pltpu.roll(d * b, HEAD_DIM - half, 1)


def _deinterleave(dst, src, dil, cast=None, dst_off=0):
    length = S // dil
    for r in range(dil):
        v = src[...] if dil == 1 else src[pl.ds(r, length, stride=dil), :]
        dst[dst_off + r * length:dst_off + (r + 1) * length, :] = v if cast is None else v.astype(cast)


def _interleave(dst, src, dil, src_off=0):
    length = S // dil
    for r in range(dil):
        if dil == 1:
            dst[...] = src[src_off:src_off + S, :]
        else:
            dst[pl.ds(r, length, stride=dil), :] = src[src_off + r * length:src_off + (r + 1) * length, :]


CU = 4
NUNITS = S // BLK
B_QK = (((2,), (2,)), ((0,), (0,)))
B_PV = (((2,), (1,)), ((0,), (0,)))
B_TN = (((1,), (1,)), ((0,), (0,)))


def _blocks(ref, first):
    return ref[first * BLK:(first + CU) * BLK, :].reshape(CU, BLK, HEAD_DIM)


def _chunk_scores(u0, nb, qd, kdp):
    q = _blocks(qd, u0)
    row = lax.broadcasted_iota(jnp.int32, (CU, BLK, BLK), 1)
    col = lax.broadcasted_iota(jnp.int32, (CU, BLK, BLK), 2)
    s_own = jnp.where(col <= row, _dot(q, _blocks(kdp, u0 + 1), B_QK) * SCALE, NEG)
    if nb == 1:
        return q, s_own, None
    unit = lax.broadcasted_iota(jnp.int32, (CU, BLK, BLK), 0) + u0
    s_prev = jnp.where((col >= row) & ((unit % nb) != 0), _dot(q, _blocks(kdp, u0), B_QK) * SCALE, NEG)
    return q, s_own, s_prev


def _qkv_prep(z0, tabs):
    def body(q_ref, k_ref, v_ref, c_ref, a_ref, b_ref, qo_ref, ko_ref, vo_ref, tmp):
        p = pl.program_id(1)
        for gi, (_, dil) in enumerate(PATTERNS):
            @pl.when(p == gi)
            def _(dil=dil):
                c, a, b = c_ref[...], a_ref[...], b_ref[...]
                tmp[...] = _rope(q_ref[...], c, a, b)
                _deinterleave(qo_ref, tmp, dil, BF16)
                tmp[...] = _rope(k_ref[...], c, a, b)
                _deinterleave(ko_ref, tmp, dil, BF16)
                _deinterleave(vo_ref, v_ref, dil, BF16)

    tab = pl.BlockSpec((S, HEAD_DIM), lambda h, p: (0, 0))
    out = pl.BlockSpec((S, HEAD_DIM), lambda h, p: (0, p * 8 + h))
    return pl.pallas_call(
        body, grid=(8, 3), in_specs=[_head_spec(Q_COL), _head_spec(K_COL), _head_spec(V_COL), tab, tab, tab],
        out_specs=[out, out, out], out_shape=[jax.ShapeDtypeStruct((S, 3072), BF16)] * 3,
        scratch_shapes=[pltpu.VMEM((S, HEAD_DIM), F32)],
        compiler_params=_params(("parallel", "arbitrary"), VMEM_BIG), name="qkv_prep",
    )(z0, z0, z0, *tabs)


def _pad_copy(dst, src):
    dst[0:BLK, :] = jnp.zeros((BLK, HEAD_DIM), dst.dtype)
    dst[BLK:BLK + S, :] = src[...]


def _attn_group_fwd(dil, qd, kd_ref, vd_ref, kdp, vdp, od, ld, og, lg):
    nb = S // dil // BLK
    _pad_copy(kdp, kd_ref)
    _pad_copy(vdp, vd_ref)
    for u0 in range(0, NUNITS, CU):
        _, s_own, s_prev = _chunk_scores(u0, nb, qd, kdp)
        m = jnp.max(s_own, axis=2, keepdims=True)
        if s_prev is not None:
            m = jnp.maximum(m, jnp.max(s_prev, axis=2, keepdims=True))
        p_own = jnp.exp(s_own - m)
        den = jnp.sum(p_own, axis=2, keepdims=True)
        acc = _dot(p_own.astype(BF16), _blocks(vdp, u0 + 1), B_PV)
        if s_prev is not None:
            p_prev = jnp.exp(s_prev - m)
            den = den + jnp.sum(p_prev, axis=2, keepdims=True)
            acc = acc + _dot(p_prev.astype(BF16), _blocks(vdp, u0), B_PV)
        rows = slice(u0 * BLK, (u0 + CU) * BLK)
        od[rows, :] = (acc / den).reshape(CU * BLK, HEAD_DIM)
        ld[rows, :] = jnp.broadcast_to(m + jnp.log(den), (CU, BLK, HEAD_DIM)).reshape(CU * BLK, HEAD_DIM)
    _interleave(og, od, dil)
    _interleave(lg, ld, dil)


def _group_weights(lgs):
    l0, l1, l2 = lgs[0][...], lgs[1][...], lgs[2][...]
    mx = jnp.maximum(l0, jnp.maximum(l1, l2))
    e0, e1, e2 = jnp.exp(l0 - mx), jnp.exp(l1 - mx), jnp.exp(l2 - mx)
    den = e0 + e1 + e2
    return e0 / den, e1 / den, e2 / den


def _head_spec(base, ngroups_axis=True):
    return pl.BlockSpec((S, HEAD_DIM), lambda h, p: (0, base + (p % 3) * 8 + h))


def _slab(dtype=F32, rows=S):
    return pltpu.VMEM((rows, HEAD_DIM), dtype)


def _attn_fwd(z0, qkv, ycat):
    def body(q_ref, k_ref, v_ref, gate_ref, ycat_ref, out_ref, og_ref, lg_ref,
             kdp, vdp, od, ld, og0, og1, og2, lg0, lg1, lg2):
        del ycat_ref
        p = pl.program_id(1)
        ogs, lgs = (og0, og1, og2), (lg0, lg1, lg2)
        for gi, (_, dil) in enumerate(PATTERNS):
            @pl.when(p == gi)
            def _(gi=gi, dil=dil):
                _attn_group_fwd(dil, q_ref, k_ref, v_ref, kdp, vdp, od, ld, ogs[gi], lgs[gi])
                og_ref[...] = ogs[gi][...]
                lg_ref[...] = lgs[gi][...]

        @pl.when(p == 2)
        def _():
            w0, w1, w2 = _group_weights(lgs)
            o = w0 * og0[...] + w1 * og1[...] + w2 * og2[...]
            gate = gate_ref[...]
            out_ref[...] = (o * (gate * _sigmoid(gate))).astype(BF16)

    grp = pl.BlockSpec((S, HEAD_DIM), lambda h, p: (0, p * 8 + h))
    return pl.pallas_call(
        body, grid=(8, 3),
        in_specs=[grp, grp, grp, pl.BlockSpec((S, HEAD_DIM), lambda h, p: (0, BG_COL + h)), ANY_SPEC],
        out_specs=[pl.BlockSpec((S, HEAD_DIM), lambda h, p: (0, 8 + h)), grp, grp],
        out_shape=[jax.ShapeDtypeStruct((S, 2048), BF16), jax.ShapeDtypeStruct((S, 3072), F32),
                   jax.ShapeDtypeStruct((S, 3072), F32)],
        scratch_shapes=[_slab(BF16, S + BLK), _slab(BF16, S + BLK)] + [_slab() for _ in range(8)],
        input_output_aliases={4: 0},
        compiler_params=_params(("parallel", "arbitrary"), VMEM_BIG), name="attn_fwd",
    )(*qkv, z0, ycat)


def _attn_bwd(z0, qkv, og, lg, dycat, tabs, dep):
    def body(q_ref, k_ref, v_ref, gate_ref, dy_ref, c_ref, a_ref, b_ref,
             og0_ref, og1_ref, og2_ref, lg0_ref, lg1_ref, lg2_ref, dep_ref,
             dq_ref, dk_ref, dv_ref, dbg_ref,
             tmp, kd, vd, ld, dg0, dg1, dg2, cg0, cg1, cg2, dod, cd, dqd, dkd, dvd):
        p = pl.program_id(1)
        ogs, lgs, dgs, cgs = (og0_ref, og1_ref, og2_ref), (lg0_ref, lg1_ref, lg2_ref), (dg0, dg1, dg2), (cg0, cg1, cg2)

        @pl.when(p == 0)
        def _():
            w = _group_weights(lgs)
            o = w[0] * ogs[0][...] + w[1] * ogs[1][...] + w[2] * ogs[2][...]
            silu, dsilu = _silu_and_grad(gate_ref[...])
            dy = dy_ref[...]
            dbg_ref[...] = (dy * o * dsilu).astype(BF16)
            do = dy * silu
            dwbar = jnp.sum(do * o, axis=1, keepdims=True)
            for gi in range(3):
                dgs[gi][...] = w[gi] * do
                cgs[gi][...] = -w[gi] * dwbar

        for gi, (_, dil) in enumerate(PATTERNS):
            @pl.when(p == 1 + gi)
            def _(gi=gi, dil=dil):
                nb = S // dil // BLK
                qd = q_ref
                c, a, b = c_ref[...], a_ref[...], b_ref[...]
                _pad_copy(kd, k_ref)
                _pad_copy(vd, v_ref)
                _deinterleave(dod, dgs[gi], dil, BF16)
                _deinterleave(ld, lgs[gi], dil)
                _deinterleave(cd, cgs[gi], dil)
                dkd[...] = jnp.zeros_like(dkd)
                dvd[...] = jnp.zeros_like(dvd)
                flat = lambda t: t.reshape(CU * BLK, HEAD_DIM)
                for u0 in range(0, NUNITS, CU):
                    q, s_own, s_prev = _chunk_scores(u0, nb, qd, kd)
                    lse, cv, do = _blocks(ld, u0), _blocks(cd, u0), _blocks(dod, u0)
                    own = slice((u0 + 1) * BLK, (u0 + 1 + CU) * BLK)
                    p_own = jnp.exp(s_own - lse)
                    ds_own = (p_own * (_dot(do, _blocks(vd, u0 + 1), B_QK) + cv) * SCALE).astype(BF16)
                    dq = _dot(ds_own, _blocks(kd, u0 + 1), B_PV)
                    dkd[own, :] += flat(_dot(ds_own, q, B_TN))
                    dvd[own, :] += flat(_dot(p_own.astype(BF16), do, B_TN))
                    if s_prev is not None:
                        prev = slice(u0 * BLK, (u0 + CU) * BLK)
                        p_prev = jnp.exp(s_prev - lse)
                        ds_prev = (p_prev * (_dot(do, _blocks(vd, u0), B_QK) + cv) * SCALE).astype(BF16)
                        dq = dq + _dot(ds_prev, _blocks(kd, u0), B_PV)
                        dkd[prev, :] += flat(_dot(ds_prev, q, B_TN))
                        dvd[prev, :] += flat(_dot(p_prev.astype(BF16), do, B_TN))
                    dqd[u0 * BLK:(u0 + CU) * BLK, :] = flat(dq)
                _interleave(tmp, dqd, dil)
                dq_ref[...] = _rope_t(tmp[...], c, a, b).astype(BF16)
                _interleave(tmp, dkd, dil, BLK)
                dk_ref[...] = _rope_t(tmp[...], c, a, b).astype(BF16)
                _interleave(tmp, dvd, dil, BLK)
                dv_ref[...] = tmp[...].astype(BF16)

    tab = pl.BlockSpec((S, HEAD_DIM), lambda h, p: (0, 0))
    hspec = lambda base: pl.BlockSpec((S, HEAD_DIM), lambda h, p: (0, base + h))
    gspec = pl.BlockSpec((S, HEAD_DIM), lambda h, p: (0, jnp.maximum(p - 1, 0) * 8 + h))
    return pl.pallas_call(
        body, grid=(8, 4),
        in_specs=[gspec, gspec, gspec, hspec(BG_COL), hspec(8), tab, tab, tab,
                  hspec(0), hspec(8), hspec(16), hspec(0), hspec(8), hspec(16), ANY_SPEC],
        out_specs=[gspec, gspec, gspec, hspec(0)],
        out_shape=[jax.ShapeDtypeStruct((S, 3072), BF16)] * 3 + [jax.ShapeDtypeStruct((S, HALF), BF16)],
        scratch_shapes=[_slab(), _slab(BF16, S + BLK), _slab(BF16, S + BLK), _slab()] + [_slab() for _ in range(6)]
                       + [_slab(BF16), _slab(), _slab(), _slab(F32, S + BLK), _slab(F32, S + BLK)],
        compiler_params=_params(("parallel", "arbitrary"), VMEM_BIG), name="attn_bwd",
    )(*qkv, z0, dycat, *tabs, og, og, og, lg, lg, lg, dep)


SGU_CH = 256
NCHUNK = TR // 128


def _ln_stats(x):
    mu = jnp.mean(x, axis=-1, keepdims=True)
    xc = x - mu
    r = lax.rsqrt(jnp.mean(xc * xc, axis=-1, keepdims=True) + EPS)
    return xc * r, r


def _ln_bwd(dy, xhat, r, g):
    dxh = dy * g
    return r * (dxh - jnp.mean(dxh, axis=-1, keepdims=True) - xhat * jnp.mean(dxh * xhat, axis=-1, keepdims=True))


def _tril_bf16(w):
    row = lax.broadcasted_iota(jnp.int32, w.shape, 0)
    col = lax.broadcasted_iota(jnp.int32, w.shape, 1)
    return jnp.where(row >= col, w, 0.0).astype(BF16)


def _sgu_gate(vn_s, s_s, w_ref, bb_ref):
    for h in range(4):
        wm = _tril_bf16(w_ref[h])
        bias = bb_ref[h]
        for ch in range(NCHUNK):
            rows, cols = slice(ch * 128, (ch + 1) * 128), slice(h * SGU_CH, (h + 1) * SGU_CH)
            s_s[rows, cols] = _dot(wm, vn_s[rows, cols]) + jnp.concatenate([bias, bias], axis=1)


WIN = HALO + TR
SUBL = 8


def _shifted_copies(dst, src):
    dst[0] = src[...]
    for b in range(1, SUBL):
        dst[b, 0:WIN - SUBL, :] = src[pl.ds(b, WIN - SUBL), :]


def _rows_at(copies, off, n):
    return copies[off % SUBL, pl.ds(off - off % SUBL, n), :]


def _conv_fwd(i, dval_ref, dglu_ref, hval_ref, hglu_ref, cw_ref, cb_ref, xw, xr, dcs):
    halo = hval_ref[...] * _sigmoid(hglu_ref[...])
    xw[0:HALO, :] = jnp.where(i > 0, halo, 0.0)
    xw[HALO:HALO + TR, :] = dval_ref[...] * _sigmoid(dglu_ref[...])
    _shifted_copies(xr, xw)
    for rb in range(TR // SUB):
        acc = jnp.broadcast_to(cb_ref[...], (SUB, HALF))
        for k in range(CONV_K):
            acc = acc + cw_ref[k:k + 1, :] * _rows_at(xr, rb * SUB + HALO - (CONV_K - 1) + k, SUB)
        dcs[rb * SUB:(rb + 1) * SUB, :] = acc


def _odd_in_specs():
    col = lambda j: pl.BlockSpec((TR, HALF), lambda i, *_: (i, j))
    prev = lambda j: pl.BlockSpec((HALO, HALF), lambda i, *_: (jnp.maximum(i * (TR // HALO) - 1, 0), j))
    return [col(0), col(1), col(2), col(3), col(4), col(5), prev(3), prev(4)]


def _full_spec(shape):
    return pl.BlockSpec(shape, lambda i, *_: (0,) * len(shape))


def _odd_fwd(z1, sgu_g, sgu_b, sgu_w, sgu_bb, conv_w, conv_b, cn_g, cn_b):
    def body(u_ref, v_ref, cg_ref, dval_ref, dglu_ref, dgate_ref, hval_ref, hglu_ref,
             g_ref, b_ref, w_ref, bb_ref, cw_ref, cb_ref, cng_ref, cnb_ref, out_ref, dcs, vn_s, s_s, xw, xr):
        i = pl.program_id(0)
        vhat, _ = _ln_stats(v_ref[...])
        vn_s[...] = (vhat * g_ref[...] + b_ref[...]).astype(BF16)
        _sgu_gate(vn_s, s_s, w_ref, bb_ref)
        cg = cg_ref[...]
        out_ref[:, 0:HALF] = (u_ref[...] * s_s[...] * (cg * _sigmoid(cg))).astype(BF16)
        _conv_fwd(i, dval_ref, dglu_ref, hval_ref, hglu_ref, cw_ref, cb_ref, xw, xr, dcs)
        dhat, _ = _ln_stats(dcs[...])
        dn = dhat * cng_ref[...] + cnb_ref[...]
        dgate = dgate_ref[...]
        out_ref[:, HALF:2 * HALF] = ((dn * _sigmoid(dn)) * (dgate * _sigmoid(dgate))).astype(BF16)

    vec = _full_spec((1, HALF))
    return pl.pallas_call(
        body, grid=(S // TR,),
        in_specs=_odd_in_specs() + [vec, vec, _full_spec((4, 128, 128)), _full_spec((4, 128, 128)),
                                    _full_spec((HALO, HALF)), vec, vec, vec],
        out_specs=[pl.BlockSpec((TR, 2048), lambda i: (i, 0)), pl.BlockSpec((TR, HALF), lambda i: (i, 0))],
        out_shape=[jax.ShapeDtypeStruct((S, 2048), BF16), jax.ShapeDtypeStruct((S, HALF), F32)],
        scratch_shapes=[pltpu.VMEM((TR, HALF), BF16), pltpu.VMEM((TR, HALF), F32),
                        pltpu.VMEM((WIN, HALF), F32), pltpu.VMEM((SUBL, WIN, HALF), F32)],
        compiler_params=_params(("parallel",), VMEM_BIG), name="odd_fwd",
    )(z1, z1, z1, z1, z1, z1, z1, z1, sgu_g, sgu_b, sgu_w, sgu_bb, conv_w, conv_b, cn_g, cn_b)


def _odd_bwd_a(z1, dc, dycat, sgu_g, sgu_b, sgu_w, sgu_bb, cn_g, cn_b):
    def body(u_ref, v_ref, cg_ref, dgate_ref, dcs, dy_ref, g_ref, b_ref, w_ref, bb_ref, cng_ref, cnb_ref,
             dz_ref, ddc_ref, dw_ref, dbb_ref, dg_ref, db_ref, dcng_ref, dcnb_ref, dcb_ref,
             vn_s, s_s, ds_s, dvn_s):
        i = pl.program_id(0)
        vhat, rv = _ln_stats(v_ref[...])
        g = g_ref[...]
        vn_s[...] = (vhat * g + b_ref[...]).astype(BF16)
        _sgu_gate(vn_s, s_s, w_ref, bb_ref)
        silu_c, dsilu_c = _silu_and_grad(cg_ref[...])
        dyc = dy_ref[:, 0:HALF]
        u = u_ref[...]
        s = s_s[...]
        dz_ref[:, 0:HALF] = (dyc * s * silu_c).astype(BF16)
        dz_ref[:, 2 * HALF:3 * HALF] = (dyc * u * s * dsilu_c).astype(BF16)
        ds_s[...] = dyc * u * silu_c

        @pl.when(i == 0)
        def _():
            dw_ref[...] = jnp.zeros_like(dw_ref)
            dbb_ref[...] = jnp.zeros_like(dbb_ref)

        tril = lax.broadcasted_iota(jnp.int32, (128, 128), 0) >= lax.broadcasted_iota(jnp.int32, (128, 128), 1)
        for h in range(4):
            wm = _tril_bf16(w_ref[h])
            for ch in range(NCHUNK):
                rows, cols = slice(ch * 128, (ch + 1) * 128), slice(h * SGU_CH, (h + 1) * SGU_CH)
                ds = ds_s[rows, cols]
                dsb = ds.astype(BF16)
                dw_ref[h] += jnp.where(tril, _dot(dsb, vn_s[rows, cols], NT), 0.0)
                dbb_ref[h] += jnp.broadcast_to(jnp.sum(ds, axis=1, keepdims=True), (128, 128))
                dvn_s[rows, cols] = _dot(wm, dsb, TN)
        dvn = dvn_s[...]
        _acc_rows(dg_ref, dvn * vhat, i)
        _acc_rows(db_ref, dvn, i)
        dz_ref[:, HALF:2 * HALF] = _ln_bwd(dvn, vhat, rv, g).astype(BF16)

        dhat, rd = _ln_stats(dcs[...])
        cng = cng_ref[...]
        silu_n, dsilu_n = _silu_and_grad(dhat * cng + cnb_ref[...])
        silu_g, dsilu_g = _silu_and_grad(dgate_ref[...])
        dyd = dy_ref[:, HALF:2 * HALF]
        dz_ref[:, 5 * HALF:6 * HALF] = (dyd * silu_n * dsilu_g).astype(BF16)
        ddn = dyd * silu_g * dsilu_n
        _acc_rows(dcng_ref, ddn * dhat, i)
        _acc_rows(dcnb_ref, ddn, i)
        ddc = _ln_bwd(ddn, dhat, rd, cng)
        ddc_ref[...] = ddc
        _acc_rows(dcb_ref, ddc, i)

    vec = _full_spec((1, HALF))
    sq = _full_spec((4, 128, 128))
    col = lambda j: pl.BlockSpec((TR, HALF), lambda i: (i, j))
    return pl.pallas_call(
        body, grid=(S // TR,),
        in_specs=[col(0), col(1), col(2), col(5), col(0), pl.BlockSpec((TR, 2048), lambda i: (i, 0)),
                  vec, vec, sq, sq, vec, vec],
        out_specs=[pl.BlockSpec((TR, ODD_IN), lambda i: (i, 0)), pl.BlockSpec((TR, HALF), lambda i: (i, 0)),
                   sq, sq, vec, vec, vec, vec, vec],
        out_shape=[jax.ShapeDtypeStruct((S, ODD_IN), BF16), jax.ShapeDtypeStruct((S, HALF), F32),
                   jax.ShapeDtypeStruct((4, 128, 128), F32), jax.ShapeDtypeStruct((4, 128, 128), F32)]
                  + [jax.ShapeDtypeStruct((1, HALF), F32)] * 5,
        scratch_shapes=[pltpu.VMEM((TR, HALF), BF16), pltpu.VMEM((TR, HALF), F32),
                        pltpu.VMEM((TR, HALF), F32), pltpu.VMEM((TR, HALF), F32)],
        compiler_params=_params(("arbitrary",), VMEM_BIG), name="odd_bwd_a",
    )(z1, z1, z1, z1, dc, dycat, sgu_g, sgu_b, sgu_w, sgu_bb, cn_g, cn_b)


def _odd_bwd_b(z1, ddc, dz1, conv_w):
    nt = S // TR

    def body(dval_ref, dglu_ref, hval_ref, hglu_ref, ddc_ref, hddc_ref, cw_ref, dz_in_ref,
             dz_ref, dcw_ref, xw, dwin, dxs, xr, dr):
        del dz_in_ref
        i, j = pl.program_id(0), pl.program_id(1)
        sg = _sigmoid(dglu_ref[...])
        dval = dval_ref[...]

        @pl.when(j == 0)
        def _():
            halo = hval_ref[...] * _sigmoid(hglu_ref[...])
            xw[0:HALO, :] = jnp.where(i > 0, halo, 0.0)
            xw[HALO:HALO + TR, :] = dval * sg
            dwin[0:TR, :] = ddc_ref[...]
            dwin[TR:TR + HALO, :] = jnp.where(i < nt - 1, hddc_ref[...], 0.0)
            _shifted_copies(xr, xw)
            _shifted_copies(dr, dwin)

            @pl.when(i == 0)
            def _():
                dcw_ref[...] = jnp.zeros_like(dcw_ref)

            for rb in range(TR // SUB):
                acc = jnp.zeros((SUB, HALF), F32)
                for k in range(CONV_K):
                    acc = acc + cw_ref[k:k + 1, :] * _rows_at(dr, rb * SUB + (CONV_K - 1) - k, SUB)
                dxs[rb * SUB:(rb + 1) * SUB, :] = acc
            for k in range(CONV_K):
                acc = jnp.zeros((SUB, HALF), F32)
                for rb in range(TR // SUB):
                    acc = acc + dwin[rb * SUB:(rb + 1) * SUB, :] * _rows_at(xr, rb * SUB + HALO - (CONV_K - 1) + k, SUB)
                dcw_ref[k:k + 1, :] += jnp.sum(acc, axis=0, keepdims=True)
            dz_ref[...] = (dxs[...] * sg).astype(BF16)

        @pl.when(j == 1)
        def _():
            dz_ref[...] = (dxs[...] * dval * sg * (1.0 - sg)).astype(BF16)

    col = lambda c: pl.BlockSpec((TR, HALF), lambda i, j: (i, c))
    prev = lambda c: pl.BlockSpec((HALO, HALF), lambda i, j: (jnp.maximum(i * (TR // HALO) - 1, 0), c))
    nxt = pl.BlockSpec((HALO, HALF), lambda i, j: (jnp.minimum((i + 1) * (TR // HALO), S // HALO - 1), 0))
    return pl.pallas_call(
        body, grid=(nt, 2),
        in_specs=[col(3), col(4), prev(3), prev(4), pl.BlockSpec((TR, HALF), lambda i, j: (i, 0)), nxt,
                  _full_spec((HALO, HALF)), pl.BlockSpec(memory_space=pl.ANY)],
        out_specs=[pl.BlockSpec((TR, HALF), lambda i, j: (i, 3 + j)), _full_spec((HALO, HALF))],
        out_shape=[jax.ShapeDtypeStruct((S, ODD_IN), BF16), jax.ShapeDtypeStruct((HALO, HALF), F32)],
        scratch_shapes=[pltpu.VMEM((WIN, HALF), F32), pltpu.VMEM((WIN, HALF), F32), pltpu.VMEM((TR, HALF), F32),
                        pltpu.VMEM((SUBL, WIN, HALF), F32), pltpu.VMEM((SUBL, WIN, HALF), F32)],
        input_output_aliases={7: 0},
        compiler_params=_params(("arbitrary", "arbitrary"), VMEM_BIG), name="odd_bwd_b",
    )(z1, z1, z1, z1, ddc, ddc, conv_w, dz1)


def _cast_bf16(w, name, piece=0, npieces=1):
    r, c = w.shape[0], w.shape[1] // npieces
    tr = min(r, 256)

    def body(i_ref, o_ref):
        o_ref[...] = i_ref[...].astype(BF16)

    return pl.pallas_call(
        body, grid=(r // tr,), in_specs=[pl.BlockSpec((tr, c), lambda i: (i, piece))],
        out_specs=pl.BlockSpec((tr, c), lambda i: (i, 0)), out_shape=jax.ShapeDtypeStruct((r, c), BF16),
        compiler_params=_params(("parallel",)), name=name,
    )(w)


def _adamw(w, g, m, v):
    m = ADAM_B1 * m + (1.0 - ADAM_B1) * g
    v = ADAM_B2 * v + (1.0 - ADAM_B2) * (g * g)
    m_hat = m / (1.0 - ADAM_B1 ** ADAM_STEP)
    v_hat = v / (1.0 - ADAM_B2 ** ADAM_STEP)
    delta = -ADAM_LR * (m_hat / (jnp.sqrt(v_hat) + ADAM_EPS) + ADAM_WD * w)
    return delta, m, v


def _adam_reduce(parts, w, m, v, name, dep=None, piece=0, npieces=1, prev=None):
    r, c = w.shape
    cp = c // npieces
    tr = min(r, 128)
    extra = ([] if dep is None else [dep]) + ([] if prev is None else list(prev))
    nparts = parts.shape[0]

    def body(p_ref, w_ref, m_ref, v_ref, *rest):
        g_ref, d_ref, nm_ref, nv_ref = rest[len(extra):]
        g = p_ref[0].astype(F32)
        for d in range(1, nparts):
            g = g + p_ref[d].astype(F32)
        g_ref[...] = g
        d_ref[...], nm_ref[...], nv_ref[...] = _adamw(w_ref[...], g, m_ref[...], v_ref[...])

    spec = pl.BlockSpec((tr, cp), lambda i: (i, piece))
    first = 4 + (0 if dep is None else 1)
    return pl.pallas_call(
        body, grid=(r // tr,),
        in_specs=[pl.BlockSpec((nparts, tr, cp), lambda i: (0, i, 0)), spec, spec, spec] + [ANY_SPEC] * len(extra),
        out_specs=[spec] * 4, out_shape=[jax.ShapeDtypeStruct((r, c), F32)] * 4,
        input_output_aliases={} if prev is None else {first + k: k for k in range(4)},
        compiler_params=_params(("parallel",), VMEM_BIG), name=name,
    )(parts, w, m, v, *extra)


def _arrived(x, name, dep=None):
    deps = [] if dep is None else [dep]

    def body(*refs):
        refs[-1][...] = jnp.zeros_like(refs[-1])

    return pl.pallas_call(
        body, in_specs=[ANY_SPEC] * (1 + len(deps)), out_specs=pl.BlockSpec(memory_space=pltpu.VMEM),
        out_shape=jax.ShapeDtypeStruct((8, 128), F32), name=name,
    )(x, *deps)


def _sum_parts(parts, name, dep=None):
    r = parts.shape[1]
    tr = 8
    for cand in (512, 256, 128, 64, 32, 16, 8):
        if r % cand == 0:
            tr = cand
            break
    deps = [] if dep is None else [dep]

    def body(p_ref, *rest):
        g = p_ref[0]
        for d in range(1, NDEV):
            g = g + p_ref[d]
        rest[-1][...] = g

    return pl.pallas_call(
        body, grid=(r // tr,), in_specs=[pl.BlockSpec((NDEV, tr, 128), lambda i: (0, i, 0))] + [ANY_SPEC] * len(deps),
        out_specs=pl.BlockSpec((tr, 128), lambda i: (i, 0)), out_shape=jax.ShapeDtypeStruct((r, 128), F32),
        compiler_params=_params(("parallel",)), name=name,
    )(parts, *deps)


def _sum_unpack(parts, rows, name, dep=None):
    deps = [] if dep is None else [dep]

    def body(p_ref, *outs):
        outs = outs[len(deps):]
        off = 0
        for o_ref, n in zip(outs, rows):
            acc = p_ref[0, off:off + n, :]
            for d in range(1, NDEV):
                acc = acc + p_ref[d, off:off + n, :]
            o_ref[...] = acc
            off += n

    return pl.pallas_call(
        body, grid=(1,), in_specs=[pl.BlockSpec(parts.shape, lambda i: (0, 0, 0))] + [ANY_SPEC] * len(deps),
        out_specs=[pl.BlockSpec((n, 128), lambda i: (0, 0)) for n in rows],
        out_shape=[jax.ShapeDtypeStruct((n, 128), F32) for n in rows],
        compiler_params=_params(("arbitrary",), VMEM_BIG), name=name,
    )(parts, *deps)


def _adam_small(ws, gs, g_specs, ms, vs, name):
    n = len(ws)

    def body(*refs):
        w_r, g_r, m_r, v_r = refs[:n], refs[n:2 * n], refs[2 * n:3 * n], refs[3 * n:4 * n]
        outs = refs[4 * n:]
        for i in range(n):
            g = g_r[i][...]
            outs[4 * i][...] = g
            outs[4 * i + 1][...], outs[4 * i + 2][...], outs[4 * i + 3][...] = _adamw(
                w_r[i][...], g, m_r[i][...], v_r[i][...])

    whole = lambda a: pl.BlockSpec(a.shape, lambda i, nd=a.ndim: (0,) * nd)
    outs = pl.pallas_call(
        body, grid=(1,),
        in_specs=[whole(a) for a in ws] + list(g_specs) + [whole(a) for a in ms] + [whole(a) for a in vs],
        out_specs=[whole(a) for a in ws for _ in range(4)],
        out_shape=[jax.ShapeDtypeStruct(a.shape, F32) for a in ws for _ in range(4)],
        compiler_params=_params(("arbitrary",), VMEM_BIG), name=name,
    )(*ws, *gs, *ms, *vs)
    return [outs[4 * i:4 * i + 4] for i in range(n)]


MASKS = [(mx, my, mc) for mx in (0, 1) for my in (0, 1) for mc in (0, 1)][1:]


def _sc_exchange(name, collective_id, arrays, scatter):
    nt = len(arrays)
    out_type = [jax.ShapeDtypeStruct(a.shape if scatter else (NDEV,) + a.shape, a.dtype) for a in arrays]

    def body(*refs):
        ins, outs = refs[:nt], refs[nt:2 * nt]
        send_sems, recv_sems, local_sems = refs[2 * nt:3 * nt], refs[3 * nt:4 * nt], refs[4 * nt:5 * nt]
        x, y, c = lax.axis_index("x"), lax.axis_index("y"), lax.axis_index("c")
        peers = [(mx + x - 2 * mx * x, my + y - 2 * my * y, mc + c - 2 * mc * c) for mx, my, mc in MASKS]
        barrier = pltpu.get_barrier_semaphore()
        for peer in peers:
            pl.semaphore_signal(barrier, inc=1, device_id=peer, device_id_type=MESH)
        pl.semaphore_wait(barrier, len(peers))
        me = 4 * x + 2 * y + c
        own = []
        for t in range(nt):
            cp = pltpu.make_async_copy(ins[t].at[me] if scatter else ins[t], outs[t].at[me], local_sems[t])
            cp.start()
            own.append(cp)
            for px, py, pc in peers:
                src = ins[t].at[4 * px + 2 * py + pc] if scatter else ins[t]
                pltpu.make_async_remote_copy(src_ref=src, dst_ref=outs[t].at[me], send_sem=send_sems[t],
                                             recv_sem=recv_sems[t], device_id=(px, py, pc), device_id_type=MESH).start()
        for t in range(nt):
            own[t].wait()
            seven = outs[t].at[pl.ds(0, NDEV - 1)]
            drain = pltpu.make_async_remote_copy(src_ref=seven, dst_ref=seven, send_sem=send_sems[t],
                                                 recv_sem=recv_sems[t], device_id=(x, y, c), device_id_type=MESH)
            drain.wait_send()
            drain.wait_recv()

    return pl.kernel(
        body, out_type=out_type, mesh=plsc.ScalarSubcoreMesh(axis_name="sequencer", num_cores=1),
        scratch_types=[pltpu.SemaphoreType.DMA] * (3 * nt),
        compiler_params=pltpu.CompilerParams(collective_id=collective_id), name=name,
    )(*arrays)


def _sc_gather_two_level(name, collective_id, arrays):
    nt = len(arrays)
    out_type = [jax.ShapeDtypeStruct((NDEV,) + a.shape, a.dtype) for a in arrays]

    def body(*refs):
        ins, outs = refs[:nt], refs[nt:2 * nt]
        sems = refs[2 * nt:]
        send_sems, sib_sems, local_sems = sems[:nt], sems[nt:2 * nt], sems[2 * nt:3 * nt]
        ici_sems = [sems[3 * nt + 3 * t:3 * nt + 3 * t + 3] for t in range(nt)]
        x, y, c = lax.axis_index("x"), lax.axis_index("y"), lax.axis_index("c")
        sibling = (x, y, 1 - c)
        chips = [(1 - x, y), (x, 1 - y), (1 - x, 1 - y)]
        barrier = pltpu.get_barrier_semaphore()
        for peer in [sibling] + [(cx, cy, c) for cx, cy in chips]:
            pl.semaphore_signal(barrier, inc=1, device_id=peer, device_id_type=MESH)
        pl.semaphore_wait(barrier, 4)
        me = 4 * x + 2 * y + c

        def push(t, src, slot, recv_sem, to):
            pltpu.make_async_remote_copy(src_ref=src, dst_ref=outs[t].at[slot], send_sem=send_sems[t],
                                         recv_sem=recv_sem, device_id=to, device_id_type=MESH).start()

        own = []
        for t in range(nt):
            cp = pltpu.make_async_copy(ins[t], outs[t].at[me], local_sems[t])
            cp.start()
            own.append(cp)
            for j, (cx, cy) in enumerate(chips):
                push(t, ins[t], me, ici_sems[t][j], (cx, cy, c))
            push(t, ins[t], me, sib_sems[t], sibling)
        for t in range(nt):
            for j, (cx, cy) in enumerate(chips):
                slot = 4 * cx + 2 * cy + c
                landed = outs[t].at[slot]
                pltpu.make_async_remote_copy(src_ref=landed, dst_ref=landed, send_sem=send_sems[t],
                                             recv_sem=ici_sems[t][j], device_id=(cx, cy, c),
                                             device_id_type=MESH).wait_recv()
                push(t, landed, slot, sib_sems[t], sibling)
        for t in range(nt):
            own[t].wait()
            four, seven = outs[t].at[pl.ds(0, 4)], outs[t].at[pl.ds(0, 7)]
            pltpu.make_async_remote_copy(src_ref=four, dst_ref=four, send_sem=send_sems[t], recv_sem=sib_sems[t],
                                         device_id=sibling, device_id_type=MESH).wait_recv()
            pltpu.make_async_remote_copy(src_ref=seven, dst_ref=seven, send_sem=send_sems[t], recv_sem=sib_sems[t],
                                         device_id=sibling, device_id_type=MESH).wait_send()

    return pl.kernel(
        body, out_type=out_type, mesh=plsc.ScalarSubcoreMesh(axis_name="sequencer", num_cores=1),
        scratch_types=[pltpu.SemaphoreType.DMA] * (6 * nt),
        compiler_params=pltpu.CompilerParams(collective_id=collective_id), name=name,
    )(*arrays)


def _sc_sibling_exchange(name, collective_id, src, out_shape, pieces):
    def body(src_ref, out_ref, send_sem, recv_sem):
        x, y, c = lax.axis_index("x"), lax.axis_index("y"), lax.axis_index("c")
        sibling = (x, y, 1 - c)
        barrier = pltpu.get_barrier_semaphore()
        pl.semaphore_signal(barrier, inc=1, device_id=sibling, device_id_type=MESH)
        pl.semaphore_wait(barrier, 1)
        for piece, lands in pieces(c, src_ref, out_ref):
            pltpu.make_async_remote_copy(src_ref=piece, dst_ref=lands, send_sem=send_sem, recv_sem=recv_sem,
                                         device_id=sibling, device_id_type=MESH).start()
        drain = pltpu.make_async_remote_copy(src_ref=out_ref, dst_ref=out_ref, send_sem=send_sem, recv_sem=recv_sem,
                                             device_id=sibling, device_id_type=MESH)
        drain.wait_send()
        drain.wait_recv()

    return pl.kernel(
        body, out_type=jax.ShapeDtypeStruct(out_shape, src.dtype),
        mesh=plsc.ScalarSubcoreMesh(axis_name="sequencer", num_cores=1), scratch_types=[pltpu.SemaphoreType.DMA] * 2,
        compiler_params=pltpu.CompilerParams(collective_id=collective_id), name=name,
    )(src)


def _swap_class_columns(name, collective_id, dz, nb, piece=0, npieces=1):
    w = nb // npieces
    return _sc_sibling_exchange(
        name, collective_id, dz, (S, 4 * w),
        lambda c, src, out: [(src.at[:, pl.ds((2 * j + 1 - c) * nb + piece * w, w)], out.at[:, pl.ds(j * w, w)])
                             for j in range(4)])


def _sc_chip_scatter(name, collective_id, q):
    def body(q_ref, out_ref, send_sem, recv_sem, local_sem):
        x, y, c = lax.axis_index("x"), lax.axis_index("y"), lax.axis_index("c")
        chips = [(1 - x, y), (x, 1 - y), (1 - x, 1 - y)]
        barrier = pltpu.get_barrier_semaphore()
        for cx, cy in chips:
            pl.semaphore_signal(barrier, inc=1, device_id=(cx, cy, c), device_id_type=MESH)
        pl.semaphore_wait(barrier, 3)
        mine = 2 * x + y
        own = pltpu.make_async_copy(q_ref.at[mine], out_ref.at[mine], local_sem)
        own.start()
        for cx, cy in chips:
            pltpu.make_async_remote_copy(src_ref=q_ref.at[2 * cx + cy], dst_ref=out_ref.at[mine], send_sem=send_sem,
                                         recv_sem=recv_sem, device_id=(cx, cy, c), device_id_type=MESH).start()
        own.wait()
        three = out_ref.at[pl.ds(0, 3)]
        drain = pltpu.make_async_remote_copy(src_ref=three, dst_ref=three, send_sem=send_sem, recv_sem=recv_sem,
                                             device_id=(x, y, c), device_id_type=MESH)
        drain.wait_send()
        drain.wait_recv()

    return pl.kernel(
        body, out_type=jax.ShapeDtypeStruct(q.shape, q.dtype),
        mesh=plsc.ScalarSubcoreMesh(axis_name="sequencer", num_cores=1), scratch_types=[pltpu.SemaphoreType.DMA] * 3,
        compiler_params=pltpu.CompilerParams(collective_id=collective_id), name=name,
    )(q)


def _mm_pair_dw(h_own, dz, h_sib, dz_sib, nb, name, dep=None, piece=0, npieces=1):
    nb = nb // npieces
    tn = 512 if nb % 512 == 0 else nb
    per = nb // tn
    o_spec = pl.BlockSpec((None, D, tn), lambda i, j, k: (j // per, 0, j % per))
    own_col = lambda i, j, k: (0, ((2 * (j // per) + lax.axis_index("c")) * npieces + piece) * per + j % per)
    part = _matmul(
        h_own, dz, dn=TN, grid=(1, 4 * per, 1),
        a_spec=pl.BlockSpec((S, D), lambda i, j, k: (0, 0)), b_spec=pl.BlockSpec((S, tn), own_col),
        o_spec=o_spec, out_shape=(4, D, nb), out_dtype=F32, acc_shape=(D, tn), name=name + "_own", dep=dep)

    def body(a_ref, b_ref, p_ref, o_ref):
        o_ref[...] = (p_ref[...] + _dot(a_ref[...], b_ref[...], TN)).astype(BF16)

    return pl.pallas_call(
        body, grid=(1, 4 * per, 1),
        in_specs=[pl.BlockSpec((S, D), lambda i, j, k: (0, 0)), pl.BlockSpec((S, tn), lambda i, j, k: (0, j)), o_spec],
        out_specs=o_spec, out_shape=jax.ShapeDtypeStruct((4, D, nb), BF16),
        compiler_params=_params(("parallel", "parallel", "arbitrary"), VMEM_BIG), name=name + "_sibling",
    )(h_sib, dz_sib, part)


SMALL = {
    "e_pre_norm": ((2048,), None), "e_pool_w": ((4, 256, 256), 1), "e_pool_scale": ((1024,), None),
    "e_post_norm": ((2048,), None), "o_pre_norm": ((2048,), 0), "o_sgu_norm_g": ((1024,), 0),
    "o_sgu_norm_b": ((1024,), 0), "o_sgu_w": ((4, 128, 128), None), "o_sgu_b": ((4, 128), None),
    "o_conv_w": ((31, 1024), 1), "o_conv_b": ((1024,), 0), "o_conv_norm_g": ((1024,), 0),
    "o_conv_norm_b": ((1024,), 0), "o_post_norm": ((2048,), 0),
}
SMALL_SHARDED = [n for n, (_, ax) in SMALL.items() if ax is not None]


def _shard_shape(name):
    shape, ax = SMALL[name]
    if ax is None:
        return shape
    return tuple(s // NDEV if i == ax else s for i, s in enumerate(shape))


def _pack(arrs, row_multiple=1):
    flat = jnp.concatenate([a.reshape(-1) for a in arrs])
    pad = -flat.shape[0] % (128 * row_multiple)
    return jnp.concatenate([flat, jnp.zeros((pad,), F32)]).reshape(-1, 128)


def _small_views(name):
    shape, ax = SMALL[name]
    me = lambda: 4 * lax.axis_index("x") + 2 * lax.axis_index("y") + lax.axis_index("c")
    if ax is None:
        view = (int(np.prod(shape)) // 128, 128)
        return view, view, pl.BlockSpec(view, lambda i: (0, 0))
    if len(shape) == 1:
        n = shape[0] // NDEV
        return (1, n), (NDEV, 1, n), pl.BlockSpec((None, 1, n), lambda i: (me(), 0, 0))
    part = _shard_shape(name)
    return part, shape, pl.BlockSpec(part, lambda i: tuple(me() if d == ax else 0 for d in range(len(shape))))


BIG = ("e_w_in", "e_w_out", "o_w_in", "o_w_out")
WEIGHTS = ["e_pre_norm", "e_w_in", "e_pool_w", "e_pool_scale", "e_w_out", "e_post_norm", "o_pre_norm", "o_w_in",
           "o_sgu_norm_g", "o_sgu_norm_b", "o_sgu_w", "o_sgu_b", "o_conv_w", "o_conv_b", "o_conv_norm_g",
           "o_conv_norm_b", "o_w_out", "o_post_norm"]


def kernel(x, e_pre_norm, e_w_in, e_pool_w, e_pool_scale, e_w_out, e_post_norm, o_pre_norm, o_w_in, o_sgu_norm_g, o_sgu_norm_b, o_sgu_w, o_sgu_b, o_conv_w, o_conv_b, o_conv_norm_g, o_conv_norm_b, o_w_out, o_post_norm, loss_target, m_e_pre_norm, m_e_w_in, m_e_pool_w, m_e_pool_scale, m_e_w_out, m_e_post_norm, m_o_pre_norm, m_o_w_in, m_o_sgu_norm_g, m_o_sgu_norm_b, m_o_sgu_w, m_o_sgu_b, m_o_conv_w, m_o_conv_b, m_o_conv_norm_g, m_o_conv_norm_b, m_o_w_out, m_o_post_norm, v_e_pre_norm, v_e_w_in, v_e_pool_w, v_e_pool_scale, v_e_w_out, v_e_post_norm, v_o_pre_norm, v_o_w_in, v_o_sgu_norm_g, v_o_sgu_norm_b, v_o_sgu_w, v_o_sgu_b, v_o_conv_w, v_o_conv_b, v_o_conv_norm_g, v_o_conv_norm_b, v_o_w_out, v_o_post_norm):
    given = dict(locals())
    w = {n: given[n][0] for n in WEIGHTS}
    m = {n: given["m_" + n][0] for n in WEIGHTS}
    v = {n: given["v_" + n][0] for n in WEIGHTS}
    me = 4 * lax.axis_index("x") + 2 * lax.axis_index("y") + lax.axis_index("c")
    x, target = x[0], loss_target[0]
    row = lambda a: a.reshape(1, -1)

    lo, small_rows = _sc_gather_two_level(
        "gather_a0", 0, [_cast_bf16(w["e_w_in"], "cast_e_w_in_0", 0, 2), _pack([w[n] for n in SMALL_SHARDED])])
    hi, = _sc_gather_two_level("gather_a1", 12, [_cast_bf16(w["e_w_in"], "cast_e_w_in_1", 1, 2)])
    wg_e_in = (lo, hi)
    h0 = _pre0_fwd(x, row(w["e_pre_norm"]))
    wg_e_out, = _sc_gather_two_level("gather_b", 1, [_cast_bf16(w["e_w_out"], "cast_e_w_out")])
    wg_o_in, wg_o_out = _sc_gather_two_level(
        "gather_c", 13, [_cast_bf16(w[n], "cast_" + n) for n in ("o_w_in", "o_w_out")])
    h0_sib = _sc_sibling_exchange("swap_h0", 8, h0, h0.shape, lambda c, src, out: [(src, out)])
    p = {n: w[n] for n in SMALL if SMALL[n][1] is None}
    small_rows = small_rows.reshape(NDEV, -1)
    off = 0
    for n in SMALL_SHARDED:
        shp, ax = _shard_shape(n), SMALL[n][1]
        cnt = int(np.prod(shp))
        blk = small_rows[:, off:off + cnt].reshape((NDEV,) + shp)
        p[n] = jnp.moveaxis(blk, 0, ax).reshape(SMALL[n][0])
        off += cnt
    tabs = _rope_tables()
    pool_w_bf = p["e_pool_w"].astype(BF16)
    sgu_bb = jnp.broadcast_to(p["o_sgu_b"][:, :, None], (4, 128, 128))
    conv_w = jnp.concatenate([p["o_conv_w"], jnp.zeros((HALO - CONV_K, HALF), F32)], axis=0)
    odd_p = (row(p["o_sgu_norm_g"]), row(p["o_sgu_norm_b"]), p["o_sgu_w"], sgu_bb, conv_w,
             row(p["o_conv_b"]), row(p["o_conv_norm_g"]), row(p["o_conv_norm_b"]))

    z0 = _mm_in_halves(h0, wg_e_in, "mm_z0")
    ycat0 = _pool_fwd(z0, pool_w_bf, row(p["e_pool_scale"]))
    qkv = _qkv_prep(z0, tabs)
    ycat0, og, lg = _attn_fwd(z0, qkv, ycat0)
    w_out_e, w_out_o = wg_e_out.reshape(2048, D), wg_o_out.reshape(2048, D)
    y0 = _mm_out(ycat0, w_out_e, "mm_y0", h0_sib)
    x1, h1 = _post0_fwd(x, y0, row(p["e_post_norm"]), row(p["o_pre_norm"]))
    h1_sib = _sc_sibling_exchange("swap_h1", 11, h1, h1.shape, lambda c, src, out: [(src, out)])
    z1 = _mm_in(h1, wg_o_in, "mm_z1")
    ycat1, conv_out = _odd_fwd(z1, *odd_p)

    g = {}
    loss, dx2, dy1, g["o_post_norm"] = _post1_bwd(ycat1, w_out_o, x1, target, row(p["o_post_norm"]), h1_sib)
    loss = lax.psum(loss[0, 0], ("x", "y", "c"))
    parts = {}
    dw = _mm_out_dw(ycat1, dy1, "mm_dwout1").reshape(NDEV, 256, D)
    parts["o_w_out"], = _sc_exchange("scatter_o_w_out", 2, [dw], True)
    dycat1 = _mm_out_dx(dy1, w_out_o, "mm_dycat1", (dw, loss.reshape(1, 1)))
    dz1, ddc, g["o_sgu_w"], d_sgu_bb, g["o_sgu_norm_g"], g["o_sgu_norm_b"], g["o_conv_norm_g"], \
        g["o_conv_norm_b"], g["o_conv_b"] = _odd_bwd_a(z1, conv_out, dycat1, *odd_p[:4], *odd_p[6:])
    dz1, d_conv_w = _odd_bwd_b(z1, ddc, dz1, conv_w)
    g["o_sgu_b"] = d_sgu_bb[:, :, 0]
    g["o_conv_w"] = d_conv_w[:CONV_K]
    grads, deltas, new_m, new_v = {}, {}, {}, {}

    def adam(n, dep):
        grads[n], deltas[n], new_m[n], new_v[n] = _adam_reduce(parts[n], w[n], m[n], v[n], "adam_" + n, dep)
        return new_v[n]

    pin = _arrived(parts["o_w_out"], "arrived_o_w_out", d_conv_w)
    dz1_sib = _swap_class_columns("swap_dz1", 10, dz1, ODD_IN // NDEV)
    dw = _mm_pair_dw(h1, dz1, h1_sib, dz1_sib, ODD_IN // NDEV, "mm_dwin1", pin)
    parts["o_w_in"] = _sc_chip_scatter("scatter_o_w_in", 3, dw)
    dh1 = _mm_in_dx(dz1, wg_o_in, "mm_dh1", dw)
    dx1, dy0, g["o_pre_norm"], g["e_post_norm"] = _mid_bwd(dx2, dh1, x1, y0, row(p["o_pre_norm"]),
                                                           row(p["e_post_norm"]))
    dw = _mm_out_dw(ycat0, dy0, "mm_dwout0").reshape(NDEV, 256, D)
    parts["e_w_out"], = _sc_exchange("scatter_e_w_out", 4, [dw], True)
    dycat0 = _mm_out_dx(dy0, w_out_e, "mm_dycat0", dw)
    da_in, da_gate, g["e_pool_w"], g["e_pool_scale"] = _pool_bwd(z0, dycat0, pool_w_bf, row(p["e_pool_scale"]))
    late = [n for n in SMALL if n not in ("e_pre_norm", "o_sgu_b")] + ["o_sgu_b"]
    recv_small, = _sc_gather_two_level("gather_small_grads", 6,
                                       [_pack([g[n].reshape(SMALL[n][0]) for n in late], 512)])
    took = _arrived(parts["o_w_in"], "arrived_o_w_in")
    dq, dk, dv, dbg = _attn_bwd(z0, qkv, og, lg, dycat0, tabs, took)
    dz0 = jnp.concatenate([da_in, da_gate, dq, dk, dv, dbg], axis=1)
    took = _arrived(recv_small, "arrived_small_grads", _arrived(parts["e_w_out"], "arrived_e_w_out", dz0))
    nb = EVEN_IN // NDEV
    swapped = [_swap_class_columns("swap_dz0_%d" % half, (9, 14)[half], dz0, nb, half, 2) for half in (0, 1)]
    dw, e_w_in_parts = took, []
    for half in (0, 1):
        dw = _mm_pair_dw(h0, dz0, h0_sib, swapped[half], nb, "mm_dwin0_%d" % half, dw, half, 2)
        e_w_in_parts.append(_sc_chip_scatter("scatter_e_w_in_%d" % half, (5, 15)[half], dw))
    pin = adam("e_w_out", adam("o_w_out", adam("o_w_in", dw)))
    rows = [int(np.prod(SMALL[n][0])) // 128 for n in late]
    summed = dict(zip(late, _sum_unpack(recv_small, rows, "sum_small_grads", pin)))
    dh0 = _mm_in_dx_halves(dz0, wg_e_in, "mm_dh0", summed[late[0]])
    grad_x, g["e_pre_norm"] = _pre0_bwd(dx1, dh0, x, row(p["e_pre_norm"]))
    last, = _sc_exchange("gather_e_pre_norm_grad", 7, [g["e_pre_norm"].reshape(16, 128)], False)

    n = "e_w_in"
    out = _adam_reduce(e_w_in_parts[0], w[n], m[n], v[n], "adam_e_w_in_0", grad_x, 0, 2)
    out = _adam_reduce(e_w_in_parts[1], w[n], m[n], v[n], "adam_e_w_in_1", None, 1, 2, out)
    grads[n], deltas[n], new_m[n], new_v[n] = out
    summed["e_pre_norm"] = _sum_parts(last, "sum_e_pre_norm_grad", out[3])
    names = list(SMALL)
    views = [_small_views(n) for n in names]
    mine = lambda src: [src[n].reshape(vw[0]) for n, vw in zip(names, views)]
    res = _adam_small(mine(w), [summed[n].reshape(vw[1]) for n, vw in zip(names, views)], [vw[2] for vw in views],
                      mine(m), mine(v), "adam_small")
    for n, out in zip(names, res):
        grads[n], deltas[n], new_m[n], new_v[n] = [t.reshape(_shard_shape(n)) for t in out]

    lead = lambda a: a[None]
    return (loss, grad_x[None], *[lead(grads[n]) for n in WEIGHTS], *[lead(deltas[n]) for n in WEIGHTS],
            *[lead(new_m[n]) for n in WEIGHTS], *[lead(new_v[n]) for n in WEIGHTS])
```

```python
import functools

import numpy as np
import jax
import jax.numpy as jnp
from jax import lax
from jax.experimental import pallas as pl
from jax.experimental.pallas import tpu as pltpu
from jax.experimental.pallas import tpu_sc as plsc

F32 = jnp.float32
BF16 = jnp.bfloat16

S = 2048
D = 2048
NDEV = 8
EPS = 1e-6
NEG = -1e30
HEAD_DIM = 128
ROT_DIM = 32
ROPE_THETA = 500000.0
PATTERNS = ((128, 1), (512, 4), (2048, 16))
BLK = 128
EVEN_IN = 12288
ODD_IN = 6144
HALF = 1024
CONV_K = 31
HALO = 32
TR = 256
SUB = 32

ADAM_LR = 0.001
ADAM_B1 = 0.9
ADAM_B2 = 0.999
ADAM_EPS = 1e-08
ADAM_WD = 0.01
ADAM_STEP = 10

VMEM_BIG = 56 * 1024 * 1024
MESH = pl.DeviceIdType.MESH

NN = (((1,), (0,)), ((), ()))
NT = (((1,), (1,)), ((), ()))
TN = (((0,), (0,)), ((), ()))


def _dot(a, b, dn=NN):
    return lax.dot_general(a, b, dn, preferred_element_type=F32)


def _sigmoid(x):
    return 1.0 / (1.0 + jnp.exp(-x))


def _silu_and_grad(x):
    sg = _sigmoid(x)
    return x * sg, sg * (1.0 + x * (1.0 - sg))


def _params(sem, vmem=None):
    return pltpu.CompilerParams(dimension_semantics=sem, vmem_limit_bytes=vmem)


ANY_SPEC = pl.BlockSpec(memory_space=pl.ANY)


def _matmul(a, b, *, dn, grid, a_spec, b_spec, o_spec, out_shape, out_dtype, acc_shape, name, dep=None):
    nk = grid[2]
    deps = [] if dep is None else list(dep) if isinstance(dep, (tuple, list)) else [dep]

    def body(a_ref, b_ref, *rest):
        o_ref, acc = rest[len(deps)], rest[len(deps) + 1:]
        if nk == 1:
            o_ref[...] = _dot(a_ref[...], b_ref[...], dn).astype(o_ref.dtype)
            return
        acc_ref = acc[0]
        k = pl.program_id(2)

        @pl.when(k == 0)
        def _():
            acc_ref[...] = jnp.zeros_like(acc_ref)

        acc_ref[...] += _dot(a_ref[...], b_ref[...], dn)

        @pl.when(k == nk - 1)
        def _():
            o_ref[...] = acc_ref[...].astype(o_ref.dtype)

    return pl.pallas_call(
        body, grid=grid, in_specs=[a_spec, b_spec] + [ANY_SPEC] * len(deps), out_specs=o_spec,
        out_shape=jax.ShapeDtypeStruct(out_shape, out_dtype),
        scratch_shapes=[] if nk == 1 else [pltpu.VMEM(acc_shape, F32)],
        compiler_params=_params(("parallel", "parallel", "arbitrary"), VMEM_BIG), name=name,
    )(a, b, *deps)


TM = 2048


def _mm_in(h, wg, name):
    nb = wg.shape[2]
    tn = 512 if nb % 512 == 0 else nb
    per = nb // tn
    return _matmul(
        h, wg, dn=NN, grid=(S // TM, NDEV * per, 1),
        a_spec=pl.BlockSpec((TM, D), lambda i, j, k: (i, 0)),
        b_spec=pl.BlockSpec((None, D, tn), lambda i, j, k: (j // per, 0, j % per)),
        o_spec=pl.BlockSpec((TM, tn), lambda i, j, k: (i, j)),
        out_shape=(S, NDEV * nb), out_dtype=F32, acc_shape=(TM, tn), name=name)


def _mm_in_halves(h, wg_halves, name):
    hb = wg_halves[0].shape[2]
    z = None
    for half, wg in enumerate(wg_halves):
        prev = [] if z is None else [z]

        def body(a_ref, b_ref, *rest):
            rest[-1][...] = _dot(a_ref[...], b_ref[...])

        z = pl.pallas_call(
            body, grid=(NDEV,),
            in_specs=[pl.BlockSpec((S, D), lambda j: (0, 0)), pl.BlockSpec((None, D, hb), lambda j: (j, 0, 0))]
                     + [ANY_SPEC] * len(prev),
            out_specs=pl.BlockSpec((S, hb), lambda j, half=half: (0, 2 * j + half)),
            out_shape=jax.ShapeDtypeStruct((S, 2 * NDEV * hb), F32),
            input_output_aliases={2: 0} if prev else {},
            compiler_params=_params(("parallel",), VMEM_BIG), name="%s_%d" % (name, half),
        )(h, wg, *prev)
    return z


def _mm_out_dx(dy, w, name, dep=None):
    return _matmul(
        dy, w, dn=NT, grid=(S // TM, 2048 // 512, 1),
        a_spec=pl.BlockSpec((TM, D), lambda i, j, k: (i, 0)),
        b_spec=pl.BlockSpec((512, D), lambda i, j, k: (j, 0)),
        o_spec=pl.BlockSpec((TM, 512), lambda i, j, k: (i, j)),
        out_shape=(S, 2048), out_dtype=F32, acc_shape=(TM, 512), name=name, dep=dep)


def _mm_out_dw(yc, dy, name):
    return _matmul(
        yc, dy, dn=TN, grid=(2048 // TM, D // 512, 1),
        a_spec=pl.BlockSpec((S, TM), lambda i, j, k: (0, i)),
        b_spec=pl.BlockSpec((S, 512), lambda i, j, k: (0, j)),
        o_spec=pl.BlockSpec((TM, 512), lambda i, j, k: (i, j)),
        out_shape=(2048, D), out_dtype=BF16, acc_shape=(TM, 512), name=name)


def _row_spec(w=D):
    return pl.BlockSpec((TR, w), lambda i: (i, 0))


def _vec_spec(w=D):
    return pl.BlockSpec((1, w), lambda i: (0, 0))


def _rms_stats(x):
    r = lax.rsqrt(jnp.mean(x * x, axis=-1, keepdims=True) + EPS)
    return x * r, r


def _rms_bwd(dn, xhat, r, g):
    dxh = dn * g
    return r * (dxh - xhat * jnp.mean(dxh * xhat, axis=-1, keepdims=True))


def _acc_rows(ref, val, i):
    s = jnp.sum(val, axis=0, keepdims=True)

    @pl.when(i == 0)
    def _():
        ref[...] = s

    @pl.when(i > 0)
    def _():
        ref[...] += s


def _pre0_fwd(x, g, dep=None):
    deps = [] if dep is None else [dep]

    def body(x_ref, g_ref, *rest):
        xhat, _ = _rms_stats(x_ref[...])
        rest[-1][...] = (xhat * g_ref[...]).astype(BF16)

    return pl.pallas_call(
        body, grid=(S // TR,), in_specs=[_row_spec(), _vec_spec()] + [ANY_SPEC] * len(deps), out_specs=_row_spec(),
        out_shape=jax.ShapeDtypeStruct((S, D), BF16), compiler_params=_params(("parallel",)), name="pre0_fwd",
    )(x, g, *deps)


def _post0_fwd(ycat, w_out, x, g_post, g_pre1, dep):
    def body(yc_ref, w_ref, x_ref, gp_ref, g1_ref, dep_ref, y_ref, x1_ref, h1_ref):
        y = _dot(yc_ref[...], w_ref[...])
        y_ref[...] = y
        yhat, _ = _rms_stats(y)
        x1 = x_ref[...] + yhat * gp_ref[...]
        x1_ref[...] = x1
        xhat, _ = _rms_stats(x1)
        h1_ref[...] = (xhat * g1_ref[...]).astype(BF16)

    return pl.pallas_call(
        body, grid=(S // TR,),
        in_specs=[_row_spec(), pl.BlockSpec((2048, D), lambda i: (0, 0)), _row_spec(), _vec_spec(), _vec_spec(),
                  ANY_SPEC],
        out_specs=[_row_spec(), _row_spec(), _row_spec()],
        out_shape=[jax.ShapeDtypeStruct((S, D), F32), jax.ShapeDtypeStruct((S, D), F32),
                   jax.ShapeDtypeStruct((S, D), BF16)],
        compiler_params=_params(("parallel",), VMEM_BIG), name="post0_fwd",
    )(ycat, w_out, x, g_post, g_pre1, dep)


def _post1_bwd(ycat, w_out, x1, target, g_post, dep):
    def body(yc_ref, w_ref, x1_ref, t_ref, g_ref, dep_ref, loss_ref, dx2_ref, dy_ref, dg_ref):
        i = pl.program_id(0)
        yhat, r = _rms_stats(_dot(yc_ref[...], w_ref[...]))
        g = g_ref[...]
        err = x1_ref[...] + yhat * g - t_ref[...]
        part = jnp.sum(jnp.sum(err * err, axis=-1, keepdims=True), axis=0, keepdims=True) * (0.5 / D)
        _acc_rows(loss_ref, jnp.broadcast_to(part, (1, 128)), i)
        dx2 = err * (1.0 / D)
        dx2_ref[...] = dx2
        _acc_rows(dg_ref, dx2 * yhat, i)
        dy_ref[...] = _rms_bwd(dx2, yhat, r, g).astype(BF16)

    return pl.pallas_call(
        body, grid=(S // TR,),
        in_specs=[_row_spec(), pl.BlockSpec((2048, D), lambda i: (0, 0)), _row_spec(), _row_spec(), _vec_spec(),
                  ANY_SPEC],
        out_specs=[_vec_spec(128), _row_spec(), _row_spec(), _vec_spec()],
        out_shape=[jax.ShapeDtypeStruct((1, 128), F32), jax.ShapeDtypeStruct((S, D), F32),
                   jax.ShapeDtypeStruct((S, D), BF16), jax.ShapeDtypeStruct((1, D), F32)],
        compiler_params=_params(("arbitrary",), VMEM_BIG), name="post1_bwd",
    )(ycat, w_out, x1, target, g_post, dep)


TRB = 512


def _big_rows(w=D):
    return pl.BlockSpec((TRB, w), lambda i, k: (i, 0))


def _vec2(w=D):
    return pl.BlockSpec((1, w), lambda i, k: (0, 0))


def _mid_bwd(dz, wg, dx2, x1, y0, g_pre1, g_post0, dep):
    nb = wg.shape[2]

    def body(dz_ref, w_ref, dx2_ref, x1_ref, y_ref, g1_ref, gp_ref, dep_ref, dx1_ref, dy_ref, dg1_ref, dgp_ref, acc):
        i, k = pl.program_id(0), pl.program_id(1)

        @pl.when(k == 0)
        def _():
            acc[...] = jnp.zeros_like(acc)

        acc[...] += _dot(dz_ref[...], w_ref[...], NT)

        @pl.when(k == NDEV - 1)
        def _():
            xhat, r1 = _rms_stats(x1_ref[...])
            dh = acc[...]
            _acc_rows(dg1_ref, dh * xhat, i)
            dx1 = dx2_ref[...] + _rms_bwd(dh, xhat, r1, g1_ref[...])
            dx1_ref[...] = dx1
            yhat, r0 = _rms_stats(y_ref[...])
            _acc_rows(dgp_ref, dx1 * yhat, i)
            dy_ref[...] = _rms_bwd(dx1, yhat, r0, gp_ref[...]).astype(BF16)

    return pl.pallas_call(
        body, grid=(S // TRB, NDEV),
        in_specs=[pl.BlockSpec((TRB, nb), lambda i, k: (i, k)), pl.BlockSpec((None, D, nb), lambda i, k: (k, 0, 0)),
                  _big_rows(), _big_rows(), _big_rows(), _vec2(), _vec2(), ANY_SPEC],
        out_specs=[_big_rows(), _big_rows(), _vec2(), _vec2()],
        out_shape=[jax.ShapeDtypeStruct((S, D), F32), jax.ShapeDtypeStruct((S, D), BF16),
                   jax.ShapeDtypeStruct((1, D), F32), jax.ShapeDtypeStruct((1, D), F32)],
        scratch_shapes=[pltpu.VMEM((TRB, D), F32)],
        compiler_params=_params(("arbitrary", "arbitrary"), VMEM_BIG), name="mid_bwd",
    )(dz, wg, dx2, x1, y0, g_pre1, g_post0, dep)


def _pre0_bwd(dz, wg_halves, dx1, x, g, dep):
    hb = wg_halves[0].shape[2]
    nk = 2 * NDEV

    def body(dz_ref, w0_ref, w1_ref, dx1_ref, x_ref, g_ref, dep_ref, gx_ref, dg_ref, acc):
        i, k = pl.program_id(0), pl.program_id(1)

        @pl.when(k == 0)
        def _():
            acc[...] = jnp.zeros_like(acc)

        @pl.when(k % 2 == 0)
        def _():
            acc[...] += _dot(dz_ref[...], w0_ref[...], NT)

        @pl.when(k % 2 == 1)
        def _():
            acc[...] += _dot(dz_ref[...], w1_ref[...], NT)

        @pl.when(k == nk - 1)
        def _():
            xhat, r = _rms_stats(x_ref[...])
            dh = acc[...]
            _acc_rows(dg_ref, dh * xhat, i)
            gx_ref[...] = dx1_ref[...] + _rms_bwd(dh, xhat, r, g_ref[...])

    w_spec = pl.BlockSpec((None, D, hb), lambda i, k: (k // 2, 0, 0))
    return pl.pallas_call(
        body, grid=(S // TRB, nk),
        in_specs=[pl.BlockSpec((TRB, hb), lambda i, k: (i, k)), w_spec, w_spec, _big_rows(), _big_rows(), _vec2(),
                  ANY_SPEC],
        out_specs=[_big_rows(), _vec2()],
        out_shape=[jax.ShapeDtypeStruct((S, D), F32), jax.ShapeDtypeStruct((1, D), F32)],
        scratch_shapes=[pltpu.VMEM((TRB, D), F32)],
        compiler_params=_params(("arbitrary", "arbitrary"), VMEM_BIG), name="pre0_bwd",
    )(dz, *wg_halves, dx1, x, g, dep)


POOL_CH = 256


def _pool_apply(a, w, transpose):
    n = a.shape[0]
    row = lax.broadcasted_iota(jnp.int32, a.shape, 0)
    cnt = jnp.minimum(row + 1, w).astype(F32)
    s = a / cnt if transpose else a
    for k in (1, 2, 4, 8):
        if transpose:
            sh = jnp.where(row < n - k, pltpu.roll(s, n - k, 0), 0.0)
        else:
            sh = jnp.where(row >= k, pltpu.roll(s, k, 0), 0.0)
        s = jnp.where(w > k, s + sh, s)
    return s - a if transpose else s / cnt - a


def _pool_fwd(z0, pool_w, pool_scale):
    def body(a_ref, gate_ref, w_ref, sc_ref, out_ref):
        win = jnp.left_shift(2, pl.program_id(0))
        pooled = _pool_apply(a_ref[...], win, False)
        mixed = _dot(pooled.astype(BF16), w_ref[...])
        gate = gate_ref[...]
        out_ref[...] = (mixed * sc_ref[...] * (gate * _sigmoid(gate))).astype(BF16)

    return pl.pallas_call(
        body, grid=(4,),
        in_specs=[pl.BlockSpec((S, POOL_CH), lambda g: (0, g)), pl.BlockSpec((S, POOL_CH), lambda g: (0, 4 + g)),
                  pl.BlockSpec((None, POOL_CH, POOL_CH), lambda g: (g, 0, 0)),
                  pl.BlockSpec((1, POOL_CH), lambda g: (0, g))],
        out_specs=pl.BlockSpec((S, POOL_CH), lambda g: (0, g)),
        out_shape=jax.ShapeDtypeStruct((S, 2048), BF16),
        compiler_params=_params(("parallel",), VMEM_BIG), name="pool_fwd",
    )(z0, z0, pool_w, pool_scale)


def _pool_bwd(z0, dycat, pool_w, pool_scale):
    def body(a_ref, gate_ref, dy_ref, w_ref, sc_ref, da_ref, dgate_ref, dw_ref, dsc_ref):
        win = jnp.left_shift(2, pl.program_id(0))
        pooled = _pool_apply(a_ref[...], win, False).astype(BF16)
        w = w_ref[...]
        mixed = _dot(pooled, w)
        silu, dsilu = _silu_and_grad(gate_ref[...])
        dy = dy_ref[...]
        sc = sc_ref[...]
        dgate_ref[...] = (dy * (mixed * sc) * dsilu).astype(BF16)
        dms = dy * silu
        dsc_ref[...] = jnp.sum(dms * mixed, axis=0, keepdims=True)
        dmixed = (dms * sc).astype(BF16)
        dw_ref[...] = _dot(pooled, dmixed, TN)
        dpooled = _dot(dmixed, w, NT)
        da_ref[...] = _pool_apply(dpooled, win, True).astype(BF16)

    slab = lambda off: pl.BlockSpec((S, POOL_CH), lambda g: (0, off + g))
    return pl.pallas_call(
        body, grid=(4,),
        in_specs=[slab(0), slab(4), slab(0), pl.BlockSpec((None, POOL_CH, POOL_CH), lambda g: (g, 0, 0)),
                  pl.BlockSpec((1, POOL_CH), lambda g: (0, g))],
        out_specs=[slab(0), slab(0), pl.BlockSpec((None, POOL_CH, POOL_CH), lambda g: (g, 0, 0)),
                   pl.BlockSpec((1, POOL_CH), lambda g: (0, g))],
        out_shape=[jax.ShapeDtypeStruct((S, HALF), BF16), jax.ShapeDtypeStruct((S, HALF), BF16),
                   jax.ShapeDtypeStruct((4, POOL_CH, POOL_CH), F32), jax.ShapeDtypeStruct((1, HALF), F32)],
        compiler_params=_params(("parallel",), VMEM_BIG), name="pool_bwd",
    )(z0, z0, dycat, pool_w, pool_scale)


Q_COL, K_COL, V_COL, BG_COL = 2048 // 128, 5120 // 128, 8192 // 128, 11264 // 128
SCALE = HEAD_DIM ** -0.5


def _rope_tables():
    pos = jnp.arange(S, dtype=F32)
    inv_freq = jnp.power(ROPE_THETA, -jnp.arange(0, ROT_DIM, 2, dtype=F32) / ROT_DIM)
    ang = pos[:, None] * inv_freq[None, :]
    cos, sin = jnp.cos(ang), jnp.sin(ang)
    half = ROT_DIM // 2
    zeros = jnp.zeros((S, HEAD_DIM - ROT_DIM), F32)
    c = jnp.concatenate([cos, cos, jnp.ones((S, HEAD_DIM - ROT_DIM), F32)], axis=1)
    a = jnp.concatenate([-sin, jnp.zeros((S, half), F32), zeros], axis=1)
    b = jnp.concatenate([jnp.zeros((S, half), F32), sin, zeros], axis=1)
    return c, a, b


def _rope(t, c, a, b):
    half = ROT_DIM // 2
    return t * c + pltpu.roll(t, HEAD_DIM - half, 1) * a + pltpu.roll(t, half, 1) * b


def _rope_t(d, c, a, b):
    half = ROT_DIM // 2
    return d * c + pltpu.roll(d * a, half, 1) + pltpu.roll(d * b, HEAD_DIM - half, 1)


def _deinterleave(dst, src, dil, cast=None, dst_off=0):
    length = S // dil
    for r in range(dil):
        v = src[...] if dil == 1 else src[pl.ds(r, length, stride=dil), :]
        dst[dst_off + r * length:dst_off + (r + 1) * length, :] = v if cast is None else v.astype(cast)


def _interleave(dst, src, dil, src_off=0):
    length = S // dil
    for r in range(dil):
        if dil == 1:
            dst[...] = src[src_off:src_off + S, :]
        else:
            dst[pl.ds(r, length, stride=dil), :] = src[src_off + r * length:src_off + (r + 1) * length, :]


CU = 4
NUNITS = S // BLK
B_QK = (((2,), (2,)), ((0,), (0,)))
B_PV = (((2,), (1,)), ((0,), (0,)))
B_TN = (((1,), (1,)), ((0,), (0,)))


def _blocks(ref, first):
    return ref[first * BLK:(first + CU) * BLK, :].reshape(CU, BLK, HEAD_DIM)


def _chunk_scores(u0, nb, qd, kdp):
    q = _blocks(qd, u0)
    row = lax.broadcasted_iota(jnp.int32, (CU, BLK, BLK), 1)
    col = lax.broadcasted_iota(jnp.int32, (CU, BLK, BLK), 2)
    s_own = jnp.where(col <= row, _dot(q, _blocks(kdp, u0 + 1), B_QK) * SCALE, NEG)
    if nb == 1:
        return q, s_own, None
    unit = lax.broadcasted_iota(jnp.int32, (CU, BLK, BLK), 0) + u0
    s_prev = jnp.where((col >= row) & ((unit % nb) != 0), _dot(q, _blocks(kdp, u0), B_QK) * SCALE, NEG)
    return q, s_own, s_prev


def _qkv_prep(z0, tabs):
    def body(q_ref, k_ref, v_ref, c_ref, a_ref, b_ref, qo_ref, ko_ref, vo_ref, tmp):
        p = pl.program_id(1)
        for gi, (_, dil) in enumerate(PATTERNS):
            @pl.when(p == gi)
            def _(dil=dil):
                c, a, b = c_ref[...], a_ref[...], b_ref[...]
                tmp[...] = _rope(q_ref[...], c, a, b)
                _deinterleave(qo_ref, tmp, dil, BF16)
                tmp[...] = _rope(k_ref[...], c, a, b)
                _deinterleave(ko_ref, tmp, dil, BF16)
                _deinterleave(vo_ref, v_ref, dil, BF16)

    tab = pl.BlockSpec((S, HEAD_DIM), lambda h, p: (0, 0))
    out = pl.BlockSpec((S, HEAD_DIM), lambda h, p: (0, p * 8 + h))
    return pl.pallas_call(
        body, grid=(8, 3), in_specs=[_head_spec(Q_COL), _head_spec(K_COL), _head_spec(V_COL), tab, tab, tab],
        out_specs=[out, out, out], out_shape=[jax.ShapeDtypeStruct((S, 3072), BF16)] * 3,
        scratch_shapes=[pltpu.VMEM((S, HEAD_DIM), F32)],
        compiler_params=_params(("parallel", "arbitrary"), VMEM_BIG), name="qkv_prep",
    )(z0, z0, z0, *tabs)


def _pad_copy(dst, src):
    dst[0:BLK, :] = jnp.zeros((BLK, HEAD_DIM), dst.dtype)
    dst[BLK:BLK + S, :] = src[...]


def _attn_group_fwd(dil, qd, kd_ref, vd_ref, kdp, vdp, od, ld, og, lg):
    nb = S // dil // BLK
    _pad_copy(kdp, kd_ref)
    _pad_copy(vdp, vd_ref)
    for u0 in range(0, NUNITS, CU):
        _, s_own, s_prev = _chunk_scores(u0, nb, qd, kdp)
        m = jnp.max(s_own, axis=2, keepdims=True)
        if s_prev is not None:
            m = jnp.maximum(m, jnp.max(s_prev, axis=2, keepdims=True))
        p_own = jnp.exp(s_own - m)
        den = jnp.sum(p_own, axis=2, keepdims=True)
        acc = _dot(p_own.astype(BF16), _blocks(vdp, u0 + 1), B_PV)
        if s_prev is not None:
            p_prev = jnp.exp(s_prev - m)
            den = den + jnp.sum(p_prev, axis=2, keepdims=True)
            acc = acc + _dot(p_prev.astype(BF16), _blocks(vdp, u0), B_PV)
        rows = slice(u0 * BLK, (u0 + CU) * BLK)
        od[rows, :] = (acc / den).reshape(CU * BLK, HEAD_DIM)
        ld[rows, :] = jnp.broadcast_to(m + jnp.log(den), (CU, BLK, HEAD_DIM)).reshape(CU * BLK, HEAD_DIM)
    _interleave(og, od, dil)
    _interleave(lg, ld, dil)


def _group_weights(lgs):
    l0, l1, l2 = lgs[0][...], lgs[1][...], lgs[2][...]
    mx = jnp.maximum(l0, jnp.maximum(l1, l2))
    e0, e1, e2 = jnp.exp(l0 - mx), jnp.exp(l1 - mx), jnp.exp(l2 - mx)
    den = e0 + e1 + e2
    return e0 / den, e1 / den, e2 / den


def _head_spec(base, ngroups_axis=True):
    return pl.BlockSpec((S, HEAD_DIM), lambda h, p: (0, base + (p % 3) * 8 + h))


def _slab(dtype=F32, rows=S):
    return pltpu.VMEM((rows, HEAD_DIM), dtype)


def _attn_fwd(z0, qkv, ycat):
    def body(q_ref, k_ref, v_ref, gate_ref, ycat_ref, out_ref, og_ref, lg_ref,
             kdp, vdp, od, ld, og0, og1, og2, lg0, lg1, lg2):
        del ycat_ref
        p = pl.program_id(1)
        ogs, lgs = (og0, og1, og2), (lg0, lg1, lg2)
        for gi, (_, dil) in enumerate(PATTERNS):
            @pl.when(p == gi)
            def _(gi=gi, dil=dil):
                _attn_group_fwd(dil, q_ref, k_ref, v_ref, kdp, vdp, od, ld, ogs[gi], lgs[gi])
                og_ref[...] = ogs[gi][...]
                lg_ref[...] = lgs[gi][...]

        @pl.when(p == 2)
        def _():
            w0, w1, w2 = _group_weights(lgs)
            o = w0 * og0[...] + w1 * og1[...] + w2 * og2[...]
            gate = gate_ref[...]
            out_ref[...] = (o * (gate * _sigmoid(gate))).astype(BF16)

    grp = pl.BlockSpec((S, HEAD_DIM), lambda h, p: (0, p * 8 + h))
    return pl.pallas_call(
        body, grid=(8, 3),
        in_specs=[grp, grp, grp, pl.BlockSpec((S, HEAD_DIM), lambda h, p: (0, BG_COL + h)), ANY_SPEC],
        out_specs=[pl.BlockSpec((S, HEAD_DIM), lambda h, p: (0, 8 + h)), grp, grp],
        out_shape=[jax.ShapeDtypeStruct((S, 2048), BF16), jax.ShapeDtypeStruct((S, 3072), F32),
                   jax.ShapeDtypeStruct((S, 3072), F32)],
        scratch_shapes=[_slab(BF16, S + BLK), _slab(BF16, S + BLK)] + [_slab() for _ in range(8)],
        input_output_aliases={4: 0},
        compiler_params=_params(("parallel", "arbitrary"), VMEM_BIG), name="attn_fwd",
    )(*qkv, z0, ycat)


def _attn_bwd(z0, qkv, og, lg, dycat, tabs, dep):
    def body(q_ref, k_ref, v_ref, gate_ref, dy_ref, c_ref, a_ref, b_ref,
             og0_ref, og1_ref, og2_ref, lg0_ref, lg1_ref, lg2_ref, dep_ref,
             dq_ref, dk_ref, dv_ref, dbg_ref,
             tmp, kd, vd, ld, dg0, dg1, dg2, cg0, cg1, cg2, dod, cd, dqd, dkd, dvd):
        p = pl.program_id(1)
        ogs, lgs, dgs, cgs = (og0_ref, og1_ref, og2_ref), (lg0_ref, lg1_ref, lg2_ref), (dg0, dg1, dg2), (cg0, cg1, cg2)

        @pl.when(p == 0)
        def _():
            w = _group_weights(lgs)
            o = w[0] * ogs[0][...] + w[1] * ogs[1][...] + w[2] * ogs[2][...]
            silu, dsilu = _silu_and_grad(gate_ref[...])
            dy = dy_ref[...]
            dbg_ref[...] = (dy * o * dsilu).astype(BF16)
            do = dy * silu
            dwbar = jnp.sum(do * o, axis=1, keepdims=True)
            for gi in range(3):
                dgs[gi][...] = w[gi] * do
                cgs[gi][...] = -w[gi] * dwbar

        for gi, (_, dil) in enumerate(PATTERNS):
            @pl.when(p == 1 + gi)
            def _(gi=gi, dil=dil):
                nb = S // dil // BLK
                qd = q_ref
                c, a, b = c_ref[...], a_ref[...], b_ref[...]
                _pad_copy(kd, k_ref)
                _pad_copy(vd, v_ref)
                _deinterleave(dod, dgs[gi], dil, BF16)
                _deinterleave(ld, lgs[gi], dil)
                _deinterleave(cd, cgs[gi], dil)
                dkd[...] = jnp.zeros_like(dkd)
                dvd[...] = jnp.zeros_like(dvd)
                flat = lambda t: t.reshape(CU * BLK, HEAD_DIM)
                for u0 in range(0, NUNITS, CU):
                    q, s_own, s_prev = _chunk_scores(u0, nb, qd, kd)
                    lse, cv, do = _blocks(ld, u0), _blocks(cd, u0), _blocks(dod, u0)
                    own = slice((u0 + 1) * BLK, (u0 + 1 + CU) * BLK)
                    p_own = jnp.exp(s_own - lse)
                    ds_own = (p_own * (_dot(do, _blocks(vd, u0 + 1), B_QK) + cv) * SCALE).astype(BF16)
                    dq = _dot(ds_own, _blocks(kd, u0 + 1), B_PV)
                    dkd[own, :] += flat(_dot(ds_own, q, B_TN))
                    dvd[own, :] += flat(_dot(p_own.astype(BF16), do, B_TN))
                    if s_prev is not None:
                        prev = slice(u0 * BLK, (u0 + CU) * BLK)
                        p_prev = jnp.exp(s_prev - lse)
                        ds_prev = (p_prev * (_dot(do, _blocks(vd, u0), B_QK) + cv) * SCALE).astype(BF16)
                        dq = dq + _dot(ds_prev, _blocks(kd, u0), B_PV)
                        dkd[prev, :] += flat(_dot(ds_prev, q, B_TN))
                        dvd[prev, :] += flat(_dot(p_prev.astype(BF16), do, B_TN))
                    dqd[u0 * BLK:(u0 + CU) * BLK, :] = flat(dq)
                _interleave(tmp, dqd, dil)
                dq_ref[...] = _rope_t(tmp[...], c, a, b).astype(BF16)
                _interleave(tmp, dkd, dil, BLK)
                dk_ref[...] = _rope_t(tmp[...], c, a, b).astype(BF16)
                _interleave(tmp, dvd, dil, BLK)
                dv_ref[...] = tmp[...].astype(BF16)

    tab = pl.BlockSpec((S, HEAD_DIM), lambda h, p: (0, 0))
    hspec = lambda base: pl.BlockSpec((S, HEAD_DIM), lambda h, p: (0, base + h))
    gspec = pl.BlockSpec((S, HEAD_DIM), lambda h, p: (0, jnp.maximum(p - 1, 0) * 8 + h))
    return pl.pallas_call(
        body, grid=(8, 4),
        in_specs=[gspec, gspec, gspec, hspec(BG_COL), hspec(8), tab, tab, tab,
                  hspec(0), hspec(8), hspec(16), hspec(0), hspec(8), hspec(16), ANY_SPEC],
        out_specs=[gspec, gspec, gspec, hspec(0)],
        out_shape=[jax.ShapeDtypeStruct((S, 3072), BF16)] * 3 + [jax.ShapeDtypeStruct((S, HALF), BF16)],
        scratch_shapes=[_slab(), _slab(BF16, S + BLK), _slab(BF16, S + BLK), _slab()] + [_slab() for _ in range(6)]
                       + [_slab(BF16), _slab(), _slab(), _slab(F32, S + BLK), _slab(F32, S + BLK)],
        compiler_params=_params(("parallel", "arbitrary"), VMEM_BIG), name="attn_bwd",
    )(*qkv, z0, dycat, *tabs, og, og, og, lg, lg, lg, dep)


SGU_CH = 256
NCHUNK = TR // 128


def _ln_stats(x):
    mu = jnp.mean(x, axis=-1, keepdims=True)
    xc = x - mu
    r = lax.rsqrt(jnp.mean(xc * xc, axis=-1, keepdims=True) + EPS)
    return xc * r, r


def _ln_bwd(dy, xhat, r, g):
    dxh = dy * g
    return r * (dxh - jnp.mean(dxh, axis=-1, keepdims=True) - xhat * jnp.mean(dxh * xhat, axis=-1, keepdims=True))


def _tril_bf16(w):
    row = lax.broadcasted_iota(jnp.int32, w.shape, 0)
    col = lax.broadcasted_iota(jnp.int32, w.shape, 1)
    return jnp.where(row >= col, w, 0.0).astype(BF16)


def _sgu_gate(vn_s, s_s, w_ref, bb_ref):
    for h in range(4):
        wm = _tril_bf16(w_ref[h])
        bias = bb_ref[h]
        for ch in range(NCHUNK):
            rows, cols = slice(ch * 128, (ch + 1) * 128), slice(h * SGU_CH, (h + 1) * SGU_CH)
            s_s[rows, cols] = _dot(wm, vn_s[rows, cols]) + jnp.concatenate([bias, bias], axis=1)


WIN = HALO + TR
SUBL = 8


def _shifted_copies(dst, src):
    dst[0] = src[...]
    for b in range(1, SUBL):
        dst[b, 0:WIN - SUBL, :] = src[pl.ds(b, WIN - SUBL), :]


def _rows_at(copies, off, n):
    return copies[off % SUBL, pl.ds(off - off % SUBL, n), :]


def _conv_fwd(i, dval_ref, dglu_ref, hval_ref, hglu_ref, cw_ref, cb_ref, xw, xr, dcs):
    halo = hval_ref[...] * _sigmoid(hglu_ref[...])
    xw[0:HALO, :] = jnp.where(i > 0, halo, 0.0)
    xw[HALO:HALO + TR, :] = dval_ref[...] * _sigmoid(dglu_ref[...])
    _shifted_copies(xr, xw)
    for rb in range(TR // SUB):
        acc = jnp.broadcast_to(cb_ref[...], (SUB, HALF))
        for k in range(CONV_K):
            acc = acc + cw_ref[k:k + 1, :] * _rows_at(xr, rb * SUB + HALO - (CONV_K - 1) + k, SUB)
        dcs[rb * SUB:(rb + 1) * SUB, :] = acc


def _odd_in_specs():
    col = lambda j: pl.BlockSpec((TR, HALF), lambda i, *_: (i, j))
    prev = lambda j: pl.BlockSpec((HALO, HALF), lambda i, *_: (jnp.maximum(i * (TR // HALO) - 1, 0), j))
    return [col(0), col(1), col(2), col(3), col(4), col(5), prev(3), prev(4)]


def _full_spec(shape):
    return pl.BlockSpec(shape, lambda i, *_: (0,) * len(shape))


def _odd_fwd(z1, sgu_g, sgu_b, sgu_w, sgu_bb, conv_w, conv_b, cn_g, cn_b):
    def body(u_ref, v_ref, cg_ref, dval_ref, dglu_ref, dgate_ref, hval_ref, hglu_ref,
             g_ref, b_ref, w_ref, bb_ref, cw_ref, cb_ref, cng_ref, cnb_ref, out_ref, dcs, vn_s, s_s, xw, xr):
        i = pl.program_id(0)
        vhat, _ = _ln_stats(v_ref[...])
        vn_s[...] = (vhat * g_ref[...] + b_ref[...]).astype(BF16)
        _sgu_gate(vn_s, s_s, w_ref, bb_ref)
        cg = cg_ref[...]
        out_ref[:, 0:HALF] = (u_ref[...] * s_s[...] * (cg * _sigmoid(cg))).astype(BF16)
        _conv_fwd(i, dval_ref, dglu_ref, hval_ref, hglu_ref, cw_ref, cb_ref, xw, xr, dcs)
        dhat, _ = _ln_stats(dcs[...])
        dn = dhat * cng_ref[...] + cnb_ref[...]
        dgate = dgate_ref[...]
        out_ref[:, HALF:2 * HALF] = ((dn * _sigmoid(dn)) * (dgate * _sigmoid(dgate))).astype(BF16)

    vec = _full_spec((1, HALF))
    return pl.pallas_call(
        body, grid=(S // TR,),
        in_specs=_odd_in_specs() + [vec, vec, _full_spec((4, 128, 128)), _full_spec((4, 128, 128)),
                                    _full_spec((HALO, HALF)), vec, vec, vec],
        out_specs=[pl.BlockSpec((TR, 2048), lambda i: (i, 0)), pl.BlockSpec((TR, HALF), lambda i: (i, 0))],
        out_shape=[jax.ShapeDtypeStruct((S, 2048), BF16), jax.ShapeDtypeStruct((S, HALF), F32)],
        scratch_shapes=[pltpu.VMEM((TR, HALF), BF16), pltpu.VMEM((TR, HALF), F32),
                        pltpu.VMEM((WIN, HALF), F32), pltpu.VMEM((SUBL, WIN, HALF), F32)],
        compiler_params=_params(("parallel",), VMEM_BIG), name="odd_fwd",
    )(z1, z1, z1, z1, z1, z1, z1, z1, sgu_g, sgu_b, sgu_w, sgu_bb, conv_w, conv_b, cn_g, cn_b)


def _odd_bwd_a(z1, dc, dycat, sgu_g, sgu_b, sgu_w, sgu_bb, cn_g, cn_b):
    def body(u_ref, v_ref, cg_ref, dgate_ref, dcs, dy_ref, g_ref, b_ref, w_ref, bb_ref, cng_ref, cnb_ref,
             dz_ref, ddc_ref, dw_ref, dbb_ref, dg_ref, db_ref, dcng_ref, dcnb_ref, dcb_ref,
             vn_s, s_s, ds_s, dvn_s):
        i = pl.program_id(0)
        vhat, rv = _ln_stats(v_ref[...])
        g = g_ref[...]
        vn_s[...] = (vhat * g + b_ref[...]).astype(BF16)
        _sgu_gate(vn_s, s_s, w_ref, bb_ref)
        silu_c, dsilu_c = _silu_and_grad(cg_ref[...])
        dyc = dy_ref[:, 0:HALF]
        u = u_ref[...]
        s = s_s[...]
        dz_ref[:, 0:HALF] = (dyc * s * silu_c).astype(BF16)
        dz_ref[:, 2 * HALF:3 * HALF] = (dyc * u * s * dsilu_c).astype(BF16)
        ds_s[...] = dyc * u * silu_c

        @pl.when(i == 0)
        def _():
            dw_ref[...] = jnp.zeros_like(dw_ref)
            dbb_ref[...] = jnp.zeros_like(dbb_ref)

        tril = lax.broadcasted_iota(jnp.int32, (128, 128), 0) >= lax.broadcasted_iota(jnp.int32, (128, 128), 1)
        for h in range(4):
            wm = _tril_bf16(w_ref[h])
            for ch in range(NCHUNK):
                rows, cols = slice(ch * 128, (ch + 1) * 128), slice(h * SGU_CH, (h + 1) * SGU_CH)
                ds = ds_s[rows, cols]
                dsb = ds.astype(BF16)
                dw_ref[h] += jnp.where(tril, _dot(dsb, vn_s[rows, cols], NT), 0.0)
                dbb_ref[h] += jnp.broadcast_to(jnp.sum(ds, axis=1, keepdims=True), (128, 128))
                dvn_s[rows, cols] = _dot(wm, dsb, TN)
        dvn = dvn_s[...]
        _acc_rows(dg_ref, dvn * vhat, i)
        _acc_rows(db_ref, dvn, i)
        dz_ref[:, HALF:2 * HALF] = _ln_bwd(dvn, vhat, rv, g).astype(BF16)

        dhat, rd = _ln_stats(dcs[...])
        cng = cng_ref[...]
        silu_n, dsilu_n = _silu_and_grad(dhat * cng + cnb_ref[...])
        silu_g, dsilu_g = _silu_and_grad(dgate_ref[...])
        dyd = dy_ref[:, HALF:2 * HALF]
        dz_ref[:, 5 * HALF:6 * HALF] = (dyd * silu_n * dsilu_g).astype(BF16)
        ddn = dyd * silu_g * dsilu_n
        _acc_rows(dcng_ref, ddn * dhat, i)
        _acc_rows(dcnb_ref, ddn, i)
        ddc = _ln_bwd(ddn, dhat, rd, cng)
        ddc_ref[...] = ddc
        _acc_rows(dcb_ref, ddc, i)

    vec = _full_spec((1, HALF))
    sq = _full_spec((4, 128, 128))
    col = lambda j: pl.BlockSpec((TR, HALF), lambda i: (i, j))
    return pl.pallas_call(
        body, grid=(S // TR,),
        in_specs=[col(0), col(1), col(2), col(5), col(0), pl.BlockSpec((TR, 2048), lambda i: (i, 0)),
                  vec, vec, sq, sq, vec, vec],
        out_specs=[pl.BlockSpec((TR, ODD_IN), lambda i: (i, 0)), pl.BlockSpec((TR, HALF), lambda i: (i, 0)),
                   sq, sq, vec, vec, vec, vec, vec],
        out_shape=[jax.ShapeDtypeStruct((S, ODD_IN), BF16), jax.ShapeDtypeStruct((S, HALF), F32),
                   jax.ShapeDtypeStruct((4, 128, 128), F32), jax.ShapeDtypeStruct((4, 128, 128), F32)]
                  + [jax.ShapeDtypeStruct((1, HALF), F32)] * 5,
        scratch_shapes=[pltpu.VMEM((TR, HALF), BF16), pltpu.VMEM((TR, HALF), F32),
                        pltpu.VMEM((TR, HALF), F32), pltpu.VMEM((TR, HALF), F32)],
        compiler_params=_params(("arbitrary",), VMEM_BIG), name="odd_bwd_a",
    )(z1, z1, z1, z1, dc, dycat, sgu_g, sgu_b, sgu_w, sgu_bb, cn_g, cn_b)


def _odd_bwd_b(z1, ddc, dz1, conv_w):
    nt = S // TR

    def body(dval_ref, dglu_ref, hval_ref, hglu_ref, ddc_ref, hddc_ref, cw_ref, dz_in_ref,
             dz_ref, dcw_ref, xw, dwin, dxs, xr, dr):
        del dz_in_ref
        i, j = pl.program_id(0), pl.program_id(1)
        sg = _sigmoid(dglu_ref[...])
        dval = dval_ref[...]

        @pl.when(j == 0)
        def _():
            halo = hval_ref[...] * _sigmoid(hglu_ref[...])
            xw[0:HALO, :] = jnp.where(i > 0, halo, 0.0)
            xw[HALO:HALO + TR, :] = dval * sg
            dwin[0:TR, :] = ddc_ref[...]
            dwin[TR:TR + HALO, :] = jnp.where(i < nt - 1, hddc_ref[...], 0.0)
            _shifted_copies(xr, xw)
            _shifted_copies(dr, dwin)

            @pl.when(i == 0)
            def _():
                dcw_ref[...] = jnp.zeros_like(dcw_ref)

            for rb in range(TR // SUB):
                acc = jnp.zeros((SUB, HALF), F32)
                for k in range(CONV_K):
                    acc = acc + cw_ref[k:k + 1, :] * _rows_at(dr, rb * SUB + (CONV_K - 1) - k, SUB)
                dxs[rb * SUB:(rb + 1) * SUB, :] = acc
            for k in range(CONV_K):
                acc = jnp.zeros((SUB, HALF), F32)
                for rb in range(TR // SUB):
                    acc = acc + dwin[rb * SUB:(rb + 1) * SUB, :] * _rows_at(xr, rb * SUB + HALO - (CONV_K - 1) + k, SUB)
                dcw_ref[k:k + 1, :] += jnp.sum(acc, axis=0, keepdims=True)
            dz_ref[...] = (dxs[...] * sg).astype(BF16)

        @pl.when(j == 1)
        def _():
            dz_ref[...] = (dxs[...] * dval * sg * (1.0 - sg)).astype(BF16)

    col = lambda c: pl.BlockSpec((TR, HALF), lambda i, j: (i, c))
    prev = lambda c: pl.BlockSpec((HALO, HALF), lambda i, j: (jnp.maximum(i * (TR // HALO) - 1, 0), c))
    nxt = pl.BlockSpec((HALO, HALF), lambda i, j: (jnp.minimum((i + 1) * (TR // HALO), S // HALO - 1), 0))
    return pl.pallas_call(
        body, grid=(nt, 2),
        in_specs=[col(3), col(4), prev(3), prev(4), pl.BlockSpec((TR, HALF), lambda i, j: (i, 0)), nxt,
                  _full_spec((HALO, HALF)), pl.BlockSpec(memory_space=pl.ANY)],
        out_specs=[pl.BlockSpec((TR, HALF), lambda i, j: (i, 3 + j)), _full_spec((HALO, HALF))],
        out_shape=[jax.ShapeDtypeStruct((S, ODD_IN), BF16), jax.ShapeDtypeStruct((HALO, HALF), F32)],
        scratch_shapes=[pltpu.VMEM((WIN, HALF), F32), pltpu.VMEM((WIN, HALF), F32), pltpu.VMEM((TR, HALF), F32),
                        pltpu.VMEM((SUBL, WIN, HALF), F32), pltpu.VMEM((SUBL, WIN, HALF), F32)],
        input_output_aliases={7: 0},
        compiler_params=_params(("arbitrary", "arbitrary"), VMEM_BIG), name="odd_bwd_b",
    )(z1, z1, z1, z1, ddc, ddc, conv_w, dz1)


def _cast_bf16(w, name, piece=0, npieces=1):
    r, c = w.shape[0], w.shape[1] // npieces
    tr = min(r, 256)

    def body(i_ref, o_ref):
        o_ref[...] = i_ref[...].astype(BF16)

    return pl.pallas_call(
        body, grid=(r // tr,), in_specs=[pl.BlockSpec((tr, c), lambda i: (i, piece))],
        out_specs=pl.BlockSpec((tr, c), lambda i: (i, 0)), out_shape=jax.ShapeDtypeStruct((r, c), BF16),
        compiler_params=_params(("parallel",)), name=name,
    )(w)


def _adamw(w, g, m, v):
    m = ADAM_B1 * m + (1.0 - ADAM_B1) * g
    v = ADAM_B2 * v + (1.0 - ADAM_B2) * (g * g)
    m_hat = m / (1.0 - ADAM_B1 ** ADAM_STEP)
    v_hat = v / (1.0 - ADAM_B2 ** ADAM_STEP)
    delta = -ADAM_LR * (m_hat / (jnp.sqrt(v_hat) + ADAM_EPS) + ADAM_WD * w)
    return delta, m, v


def _adam_reduce(parts, w, m, v, name, dep=None, piece=0, npieces=1, prev=None):
    r, c = w.shape
    cp = c // npieces
    tr = min(r, 128)
    extra = ([] if dep is None else [dep]) + ([] if prev is None else list(prev))
    nparts = parts.shape[0]

    def body(p_ref, w_ref, m_ref, v_ref, *rest):
        g_ref, d_ref, nm_ref, nv_ref = rest[len(extra):]
        g = p_ref[0].astype(F32)
        for d in range(1, nparts):
            g = g + p_ref[d].astype(F32)
        g_ref[...] = g
        d_ref[...], nm_ref[...], nv_ref[...] = _adamw(w_ref[...], g, m_ref[...], v_ref[...])

    spec = pl.BlockSpec((tr, cp), lambda i: (i, piece))
    first = 4 + (0 if dep is None else 1)
    return pl.pallas_call(
        body, grid=(r // tr,),
        in_specs=[pl.BlockSpec((nparts, tr, cp), lambda i: (0, i, 0)), spec, spec, spec] + [ANY_SPEC] * len(extra),
        out_specs=[spec] * 4, out_shape=[jax.ShapeDtypeStruct((r, c), F32)] * 4,
        input_output_aliases={} if prev is None else {first + k: k for k in range(4)},
        compiler_params=_params(("parallel",), VMEM_BIG), name=name,
    )(parts, w, m, v, *extra)


def _arrived(x, name, dep=None):
    deps = [] if dep is None else [dep]

    def body(*refs):
        refs[-1][...] = jnp.zeros_like(refs[-1])

    return pl.pallas_call(
        body, in_specs=[ANY_SPEC] * (1 + len(deps)), out_specs=pl.BlockSpec(memory_space=pltpu.VMEM),
        out_shape=jax.ShapeDtypeStruct((8, 128), F32), name=name,
    )(x, *deps)


def _sum_parts(parts, name, dep=None):
    r = parts.shape[1]
    tr = 8
    for cand in (512, 256, 128, 64, 32, 16, 8):
        if r % cand == 0:
            tr = cand
            break
    deps = [] if dep is None else [dep]

    def body(p_ref, *rest):
        g = p_ref[0]
        for d in range(1, NDEV):
            g = g + p_ref[d]
        rest[-1][...] = g

    return pl.pallas_call(
        body, grid=(r // tr,), in_specs=[pl.BlockSpec((NDEV, tr, 128), lambda i: (0, i, 0))] + [ANY_SPEC] * len(deps),
        out_specs=pl.BlockSpec((tr, 128), lambda i: (i, 0)), out_shape=jax.ShapeDtypeStruct((r, 128), F32),
        compiler_params=_params(("parallel",)), name=name,
    )(parts, *deps)


def _sum_unpack(parts, rows, name, dep=None):
    deps = [] if dep is None else [dep]

    def body(p_ref, *outs):
        outs = outs[len(deps):]
        off = 0
        for o_ref, n in zip(outs, rows):
            acc = p_ref[0, off:off + n, :]
            for d in range(1, NDEV):
                acc = acc + p_ref[d, off:off + n, :]
            o_ref[...] = acc
            off += n

    return pl.pallas_call(
        body, grid=(1,), in_specs=[pl.BlockSpec(parts.shape, lambda i: (0, 0, 0))] + [ANY_SPEC] * len(deps),
        out_specs=[pl.BlockSpec((n, 128), lambda i: (0, 0)) for n in rows],
        out_shape=[jax.ShapeDtypeStruct((n, 128), F32) for n in rows],
        compiler_params=_params(("arbitrary",), VMEM_BIG), name=name,
    )(parts, *deps)


def _adam_small(ws, gs, g_specs, ms, vs, name):
    n = len(ws)

    def body(*refs):
        w_r, g_r, m_r, v_r = refs[:n], refs[n:2 * n], refs[2 * n:3 * n], refs[3 * n:4 * n]
        outs = refs[4 * n:]
        for i in range(n):
            g = g_r[i][...]
            outs[4 * i][...] = g
            outs[4 * i + 1][...], outs[4 * i + 2][...], outs[4 * i + 3][...] = _adamw(
                w_r[i][...], g, m_r[i][...], v_r[i][...])

    whole = lambda a: pl.BlockSpec(a.shape, lambda i, nd=a.ndim: (0,) * nd)
    outs = pl.pallas_call(
        body, grid=(1,),
        in_specs=[whole(a) for a in ws] + list(g_specs) + [whole(a) for a in ms] + [whole(a) for a in vs],
        out_specs=[whole(a) for a in ws for _ in range(4)],
        out_shape=[jax.ShapeDtypeStruct(a.shape, F32) for a in ws for _ in range(4)],
        compiler_params=_params(("arbitrary",), VMEM_BIG), name=name,
    )(*ws, *gs, *ms, *vs)
    return [outs[4 * i:4 * i + 4] for i in range(n)]


MASKS = [(mx, my, mc) for mx in (0, 1) for my in (0, 1) for mc in (0, 1)][1:]


def _sc_exchange(name, collective_id, arrays, scatter):
    nt = len(arrays)
    out_type = [jax.ShapeDtypeStruct(a.shape if scatter else (NDEV,) + a.shape, a.dtype) for a in arrays]

    def body(*refs):
        ins, outs = refs[:nt], refs[nt:2 * nt]
        send_sems, recv_sems, local_sems = refs[2 * nt:3 * nt], refs[3 * nt:4 * nt], refs[4 * nt:5 * nt]
        x, y, c = lax.axis_index("x"), lax.axis_index("y"), lax.axis_index("c")
        peers = [(mx + x - 2 * mx * x, my + y - 2 * my * y, mc + c - 2 * mc * c) for mx, my, mc in MASKS]
        barrier = pltpu.get_barrier_semaphore()
        for peer in peers:
            pl.semaphore_signal(barrier, inc=1, device_id=peer, device_id_type=MESH)
        pl.semaphore_wait(barrier, len(peers))
        me = 4 * x + 2 * y + c
        own = []
        for t in range(nt):
            cp = pltpu.make_async_copy(ins[t].at[me] if scatter else ins[t], outs[t].at[me], local_sems[t])
            cp.start()
            own.append(cp)
            for px, py, pc in peers:
                src = ins[t].at[4 * px + 2 * py + pc] if scatter else ins[t]
                pltpu.make_async_remote_copy(src_ref=src, dst_ref=outs[t].at[me], send_sem=send_sems[t],
                                             recv_sem=recv_sems[t], device_id=(px, py, pc), device_id_type=MESH).start()
        for t in range(nt):
            own[t].wait()
            seven = outs[t].at[pl.ds(0, NDEV - 1)]
            drain = pltpu.make_async_remote_copy(src_ref=seven, dst_ref=seven, send_sem=send_sems[t],
                                                 recv_sem=recv_sems[t], device_id=(x, y, c), device_id_type=MESH)
            drain.wait_send()
            drain.wait_recv()

    return pl.kernel(
        body, out_type=out_type, mesh=plsc.ScalarSubcoreMesh(axis_name="sequencer", num_cores=1),
        scratch_types=[pltpu.SemaphoreType.DMA] * (3 * nt),
        compiler_params=pltpu.CompilerParams(collective_id=collective_id), name=name,
    )(*arrays)


def _sc_gather_two_level(name, collective_id, arrays):
    nt = len(arrays)
    out_type = [jax.ShapeDtypeStruct((NDEV,) + a.shape, a.dtype) for a in arrays]

    def body(*refs):
        ins, outs = refs[:nt], refs[nt:2 * nt]
        sems = refs[2 * nt:]
        send_sems, sib_sems, local_sems = sems[:nt], sems[nt:2 * nt], sems[2 * nt:3 * nt]
        ici_sems = [sems[3 * nt + 3 * t:3 * nt + 3 * t + 3] for t in range(nt)]
        x, y, c = lax.axis_index("x"), lax.axis_index("y"), lax.axis_index("c")
        sibling = (x, y, 1 - c)
        chips = [(1 - x, y), (x, 1 - y), (1 - x, 1 - y)]
        barrier = pltpu.get_barrier_semaphore()
        for peer in [sibling] + [(cx, cy, c) for cx, cy in chips]:
            pl.semaphore_signal(barrier, inc=1, device_id=peer, device_id_type=MESH)
        pl.semaphore_wait(barrier, 4)
        me = 4 * x + 2 * y + c

        def push(t, src, slot, recv_sem, to):
            pltpu.make_async_remote_copy(src_ref=src, dst_ref=outs[t].at[slot], send_sem=send_sems[t],
                                         recv_sem=recv_sem, device_id=to, device_id_type=MESH).start()

        own = []
        for t in range(nt):
            cp = pltpu.make_async_copy(ins[t], outs[t].at[me], local_sems[t])
            cp.start()
            own.append(cp)
            for j, (cx, cy) in enumerate(chips):
                push(t, ins[t], me, ici_sems[t][j], (cx, cy, c))
            push(t, ins[t], me, sib_sems[t], sibling)
        for t in range(nt):
            for j, (cx, cy) in enumerate(chips):
                slot = 4 * cx + 2 * cy + c
                landed = outs[t].at[slot]
                pltpu.make_async_remote_copy(src_ref=landed, dst_ref=landed, send_sem=send_sems[t],
                                             recv_sem=ici_sems[t][j], device_id=(cx, cy, c),
                                             device_id_type=MESH).wait_recv()
                push(t, landed, slot, sib_sems[t], sibling)
        for t in range(nt):
            own[t].wait()
            four, seven = outs[t].at[pl.ds(0, 4)], outs[t].at[pl.ds(0, 7)]
            pltpu.make_async_remote_copy(src_ref=four, dst_ref=four, send_sem=send_sems[t], recv_sem=sib_sems[t],
                                         device_id=sibling, device_id_type=MESH).wait_recv()
            pltpu.make_async_remote_copy(src_ref=seven, dst_ref=seven, send_sem=send_sems[t], recv_sem=sib_sems[t],
                                         device_id=sibling, device_id_type=MESH).wait_send()

    return pl.kernel(
        body, out_type=out_type, mesh=plsc.ScalarSubcoreMesh(axis_name="sequencer", num_cores=1),
        scratch_types=[pltpu.SemaphoreType.DMA] * (6 * nt),
        compiler_params=pltpu.CompilerParams(collective_id=collective_id), name=name,
    )(*arrays)


def _sc_sibling_exchange(name, collective_id, src, out_shape, pieces):
    def body(src_ref, out_ref, send_sem, recv_sem):
        x, y, c = lax.axis_index("x"), lax.axis_index("y"), lax.axis_index("c")
        sibling = (x, y, 1 - c)
        barrier = pltpu.get_barrier_semaphore()
        pl.semaphore_signal(barrier, inc=1, device_id=sibling, device_id_type=MESH)
        pl.semaphore_wait(barrier, 1)
        for piece, lands in pieces(c, src_ref, out_ref):
            pltpu.make_async_remote_copy(src_ref=piece, dst_ref=lands, send_sem=send_sem, recv_sem=recv_sem,
                                         device_id=sibling, device_id_type=MESH).start()
        drain = pltpu.make_async_remote_copy(src_ref=out_ref, dst_ref=out_ref, send_sem=send_sem, recv_sem=recv_sem,
                                             device_id=sibling, device_id_type=MESH)
        drain.wait_send()
        drain.wait_recv()

    return pl.kernel(
        body, out_type=jax.ShapeDtypeStruct(out_shape, src.dtype),
        mesh=plsc.ScalarSubcoreMesh(axis_name="sequencer", num_cores=1), scratch_types=[pltpu.SemaphoreType.DMA] * 2,
        compiler_params=pltpu.CompilerParams(collective_id=collective_id), name=name,
    )(src)


def _swap_class_columns(name, collective_id, dz, nb, piece=0, npieces=1):
    w = nb // npieces
    return _sc_sibling_exchange(
        name, collective_id, dz, (S, 4 * w),
        lambda c, src, out: [(src.at[:, pl.ds((2 * j + 1 - c) * nb + piece * w, w)], out.at[:, pl.ds(j * w, w)])
                             for j in range(4)])


def _sc_chip_scatter(name, collective_id, q):
    def body(q_ref, out_ref, send_sem, recv_sem, local_sem):
        x, y, c = lax.axis_index("x"), lax.axis_index("y"), lax.axis_index("c")
        chips = [(1 - x, y), (x, 1 - y), (1 - x, 1 - y)]
        barrier = pltpu.get_barrier_semaphore()
        for cx, cy in chips:
            pl.semaphore_signal(barrier, inc=1, device_id=(cx, cy, c), device_id_type=MESH)
        pl.semaphore_wait(barrier, 3)
        mine = 2 * x + y
        own = pltpu.make_async_copy(q_ref.at[mine], out_ref.at[mine], local_sem)
        own.start()
        for cx, cy in chips:
            pltpu.make_async_remote_copy(src_ref=q_ref.at[2 * cx + cy], dst_ref=out_ref.at[mine], send_sem=send_sem,
                                         recv_sem=recv_sem, device_id=(cx, cy, c), device_id_type=MESH).start()
        own.wait()
        three = out_ref.at[pl.ds(0, 3)]
        drain = pltpu.make_async_remote_copy(src_ref=three, dst_ref=three, send_sem=send_sem, recv_sem=recv_sem,
                                             device_id=(x, y, c), device_id_type=MESH)
        drain.wait_send()
        drain.wait_recv()

    return pl.kernel(
        body, out_type=jax.ShapeDtypeStruct(q.shape, q.dtype),
        mesh=plsc.ScalarSubcoreMesh(axis_name="sequencer", num_cores=1), scratch_types=[pltpu.SemaphoreType.DMA] * 3,
        compiler_params=pltpu.CompilerParams(collective_id=collective_id), name=name,
    )(q)


def _mm_pair_dw(h_own, dz, h_sib, dz_sib, nb, name, dep=None, piece=0, npieces=1):
    nb = nb // npieces
    tn = 512 if nb % 512 == 0 else nb
    per = nb // tn
    o_spec = pl.BlockSpec((None, D, tn), lambda i, j, k: (j // per, 0, j % per))
    own_col = lambda i, j, k: (0, ((2 * (j // per) + lax.axis_index("c")) * npieces + piece) * per + j % per)
    part = _matmul(
        h_own, dz, dn=TN, grid=(1, 4 * per, 1),
        a_spec=pl.BlockSpec((S, D), lambda i, j, k: (0, 0)), b_spec=pl.BlockSpec((S, tn), own_col),
        o_spec=o_spec, out_shape=(4, D, nb), out_dtype=F32, acc_shape=(D, tn), name=name + "_own", dep=dep)

    def body(a_ref, b_ref, p_ref, o_ref):
        o_ref[...] = (p_ref[...] + _dot(a_ref[...], b_ref[...], TN)).astype(BF16)

    return pl.pallas_call(
        body, grid=(1, 4 * per, 1),
        in_specs=[pl.BlockSpec((S, D), lambda i, j, k: (0, 0)), pl.BlockSpec((S, tn), lambda i, j, k: (0, j)), o_spec],
        out_specs=o_spec, out_shape=jax.ShapeDtypeStruct((4, D, nb), BF16),
        compiler_params=_params(("parallel", "parallel", "arbitrary"), VMEM_BIG), name=name + "_sibling",
    )(h_sib, dz_sib, part)


SMALL = {
    "e_pre_norm": ((2048,), None), "e_pool_w": ((4, 256, 256), 1), "e_pool_scale": ((1024,), None),
    "e_post_norm": ((2048,), None), "o_pre_norm": ((2048,), 0), "o_sgu_norm_g": ((1024,), 0),
    "o_sgu_norm_b": ((1024,), 0), "o_sgu_w": ((4, 128, 128), None), "o_sgu_b": ((4, 128), None),
    "o_conv_w": ((31, 1024), 1), "o_conv_b": ((1024,), 0), "o_conv_norm_g": ((1024,), 0),
    "o_conv_norm_b": ((1024,), 0), "o_post_norm": ((2048,), 0),
}
SMALL_SHARDED = [n for n, (_, ax) in SMALL.items() if ax is not None]


def _shard_shape(name):
    shape, ax = SMALL[name]
    if ax is None:
        return shape
    return tuple(s // NDEV if i == ax else s for i, s in enumerate(shape))


def _pack(arrs, row_multiple=1):
    flat = jnp.concatenate([a.reshape(-1) for a in arrs])
    pad = -flat.shape[0] % (128 * row_multiple)
    return jnp.concatenate([flat, jnp.zeros((pad,), F32)]).reshape(-1, 128)


def _small_views(name):
    shape, ax = SMALL[name]
    me = lambda: 4 * lax.axis_index("x") + 2 * lax.axis_index("y") + lax.axis_index("c")
    if ax is None:
        view = (int(np.prod(shape)) // 128, 128)
        return view, view, pl.BlockSpec(view, lambda i: (0, 0))
    if len(shape) == 1:
        n = shape[0] // NDEV
        return (1, n), (NDEV, 1, n), pl.BlockSpec((None, 1, n), lambda i: (me(), 0, 0))
    part = _shard_shape(name)
    return part, shape, pl.BlockSpec(part, lambda i: tuple(me() if d == ax else 0 for d in range(len(shape))))


BIG = ("e_w_in", "e_w_out", "o_w_in", "o_w_out")
WEIGHTS = ["e_pre_norm", "e_w_in", "e_pool_w", "e_pool_scale", "e_w_out", "e_post_norm", "o_pre_norm", "o_w_in",
           "o_sgu_norm_g", "o_sgu_norm_b", "o_sgu_w", "o_sgu_b", "o_conv_w", "o_conv_b", "o_conv_norm_g",
           "o_conv_norm_b", "o_w_out", "o_post_norm"]


def kernel(x, e_pre_norm, e_w_in, e_pool_w, e_pool_scale, e_w_out, e_post_norm, o_pre_norm, o_w_in, o_sgu_norm_g, o_sgu_norm_b, o_sgu_w, o_sgu_b, o_conv_w, o_conv_b, o_conv_norm_g, o_conv_norm_b, o_w_out, o_post_norm, loss_target, m_e_pre_norm, m_e_w_in, m_e_pool_w, m_e_pool_scale, m_e_w_out, m_e_post_norm, m_o_pre_norm, m_o_w_in, m_o_sgu_norm_g, m_o_sgu_norm_b, m_o_sgu_w, m_o_sgu_b, m_o_conv_w, m_o_conv_b, m_o_conv_norm_g, m_o_conv_norm_b, m_o_w_out, m_o_post_norm, v_e_pre_norm, v_e_w_in, v_e_pool_w, v_e_pool_scale, v_e_w_out, v_e_post_norm, v_o_pre_norm, v_o_w_in, v_o_sgu_norm_g, v_o_sgu_norm_b, v_o_sgu_w, v_o_sgu_b, v_o_conv_w, v_o_conv_b, v_o_conv_norm_g, v_o_conv_norm_b, v_o_w_out, v_o_post_norm):
    given = dict(locals())
    w = {n: given[n][0] for n in WEIGHTS}
    m = {n: given["m_" + n][0] for n in WEIGHTS}
    v = {n: given["v_" + n][0] for n in WEIGHTS}
    me = 4 * lax.axis_index("x") + 2 * lax.axis_index("y") + lax.axis_index("c")
    x, target = x[0], loss_target[0]
    row = lambda a: a.reshape(1, -1)

    lo, small_rows = _sc_gather_two_level(
        "gather_a0", 0, [_cast_bf16(w["e_w_in"], "cast_e_w_in_0", 0, 2), _pack([w[n] for n in SMALL_SHARDED])])
    hi, = _sc_gather_two_level("gather_a1", 12, [_cast_bf16(w["e_w_in"], "cast_e_w_in_1", 1, 2)])
    wg_e_in = (lo, hi)
    h0 = _pre0_fwd(x, row(w["e_pre_norm"]))
    wg_e_out, = _sc_gather_two_level("gather_b", 1, [_cast_bf16(w["e_w_out"], "cast_e_w_out")])
    wg_o_in, wg_o_out = _sc_gather_two_level(
        "gather_c", 13, [_cast_bf16(w[n], "cast_" + n) for n in ("o_w_in", "o_w_out")])
    h0_sib = _sc_sibling_exchange("swap_h0", 8, h0, h0.shape, lambda c, src, out: [(src, out)])
    p = {n: w[n] for n in SMALL if SMALL[n][1] is None}
    small_rows = small_rows.reshape(NDEV, -1)
    off = 0
    for n in SMALL_SHARDED:
        shp, ax = _shard_shape(n), SMALL[n][1]
        cnt = int(np.prod(shp))
        blk = small_rows[:, off:off + cnt].reshape((NDEV,) + shp)
        p[n] = jnp.moveaxis(blk, 0, ax).reshape(SMALL[n][0])
        off += cnt
    tabs = _rope_tables()
    pool_w_bf = p["e_pool_w"].astype(BF16)
    sgu_bb = jnp.broadcast_to(p["o_sgu_b"][:, :, None], (4, 128, 128))
    conv_w = jnp.concatenate([p["o_conv_w"], jnp.zeros((HALO - CONV_K, HALF), F32)], axis=0)
    odd_p = (row(p["o_sgu_norm_g"]), row(p["o_sgu_norm_b"]), p["o_sgu_w"], sgu_bb, conv_w,
             row(p["o_conv_b"]), row(p["o_conv_norm_g"]), row(p["o_conv_norm_b"]))

    z0 = _mm_in_halves(h0, wg_e_in, "mm_z0")
    ycat0 = _pool_fwd(z0, pool_w_bf, row(p["e_pool_scale"]))
    qkv = _qkv_prep(z0, tabs)
    ycat0, og, lg = _attn_fwd(z0, qkv, ycat0)
    w_out_e, w_out_o = wg_e_out.reshape(2048, D), wg_o_out.reshape(2048, D)
    y0, x1, h1 = _post0_fwd(ycat0, w_out_e, x, row(p["e_post_norm"]), row(p["o_pre_norm"]), h0_sib)
    h1_sib = _sc_sibling_exchange("swap_h1", 11, h1, h1.shape, lambda c, src, out: [(src, out)])
    z1 = _mm_in(h1, wg_o_in, "mm_z1")
    ycat1, conv_out = _odd_fwd(z1, *odd_p)

    g = {}
    loss, dx2, dy1, g["o_post_norm"] = _post1_bwd(ycat1, w_out_o, x1, target, row(p["o_post_norm"]), h1_sib)
    loss = lax.psum(loss[0, 0], ("x", "y", "c"))
    parts = {}
    dw = _mm_out_dw(ycat1, dy1, "mm_dwout1").reshape(NDEV, 256, D)
    parts["o_w_out"], = _sc_exchange("scatter_o_w_out", 2, [dw], True)
    dycat1 = _mm_out_dx(dy1, w_out_o, "mm_dycat1", (dw, loss.reshape(1, 1)))
    dz1, ddc, g["o_sgu_w"], d_sgu_bb, g["o_sgu_norm_g"], g["o_sgu_norm_b"], g["o_conv_norm_g"], \
        g["o_conv_norm_b"], g["o_conv_b"] = _odd_bwd_a(z1, conv_out, dycat1, *odd_p[:4], *odd_p[6:])
    dz1, d_conv_w = _odd_bwd_b(z1, ddc, dz1, conv_w)
    g["o_sgu_b"] = d_sgu_bb[:, :, 0]
    g["o_conv_w"] = d_conv_w[:CONV_K]
    grads, deltas, new_m, new_v = {}, {}, {}, {}

    def adam(n, dep):
        grads[n], deltas[n], new_m[n], new_v[n] = _adam_reduce(parts[n], w[n], m[n], v[n], "adam_" + n, dep)
        return new_v[n]

    pin = _arrived(parts["o_w_out"], "arrived_o_w_out", d_conv_w)
    dz1_sib = _swap_class_columns("swap_dz1", 10, dz1, ODD_IN // NDEV)
    dw = _mm_pair_dw(h1, dz1, h1_sib, dz1_sib, ODD_IN // NDEV, "mm_dwin1", pin)
    parts["o_w_in"] = _sc_chip_scatter("scatter_o_w_in", 3, dw)
    dx1, dy0, g["o_pre_norm"], g["e_post_norm"] = _mid_bwd(dz1, wg_o_in, dx2, x1, y0, row(p["o_pre_norm"]),
                                                           row(p["e_post_norm"]), dw)
    dw = _mm_out_dw(ycat0, dy0, "mm_dwout0").reshape(NDEV, 256, D)
    parts["e_w_out"], = _sc_exchange("scatter_e_w_out", 4, [dw], True)
    dycat0 = _mm_out_dx(dy0, w_out_e, "mm_dycat0", dw)
    da_in, da_gate, g["e_pool_w"], g["e_pool_scale"] = _pool_bwd(z0, dycat0, pool_w_bf, row(p["e_pool_scale"]))
    late = [n for n in SMALL if n not in ("e_pre_norm", "o_sgu_b")] + ["o_sgu_b"]
    recv_small, = _sc_gather_two_level("gather_small_grads", 6,
                                       [_pack([g[n].reshape(SMALL[n][0]) for n in late], 512)])
    took = _arrived(parts["o_w_in"], "arrived_o_w_in")
    dq, dk, dv, dbg = _attn_bwd(z0, qkv, og, lg, dycat0, tabs, took)
    dz0 = jnp.concatenate([da_in, da_gate, dq, dk, dv, dbg], axis=1)
    took = _arrived(recv_small, "arrived_small_grads", _arrived(parts["e_w_out"], "arrived_e_w_out", dz0))
    nb = EVEN_IN // NDEV
    swapped = [_swap_class_columns("swap_dz0_%d" % half, (9, 14)[half], dz0, nb, half, 2) for half in (0, 1)]
    dw, e_w_in_parts = took, []
    for half in (0, 1):
        dw = _mm_pair_dw(h0, dz0, h0_sib, swapped[half], nb, "mm_dwin0_%d" % half, dw, half, 2)
        e_w_in_parts.append(_sc_chip_scatter("scatter_e_w_in_%d" % half, (5, 15)[half], dw))
    pin = adam("e_w_out", adam("o_w_out", adam("o_w_in", dw)))
    rows = [int(np.prod(SMALL[n][0])) // 128 for n in late]
    summed = dict(zip(late, _sum_unpack(recv_small, rows, "sum_small_grads", pin)))
    grad_x, g["e_pre_norm"] = _pre0_bwd(dz0, wg_e_in, dx1, x, row(p["e_pre_norm"]), summed[late[0]])
    last, = _sc_exchange("gather_e_pre_norm_grad", 7, [g["e_pre_norm"].reshape(16, 128)], False)

    n = "e_w_in"
    out = _adam_reduce(e_w_in_parts[0], w[n], m[n], v[n], "adam_e_w_in_0", grad_x, 0, 2)
    out = _adam_reduce(e_w_in_parts[1], w[n], m[n], v[n], "adam_e_w_in_1", None, 1, 2, out)
    grads[n], deltas[n], new_m[n], new_v[n] = out
    summed["e_pre_norm"] = _sum_parts(last, "sum_e_pre_norm_grad", out[3])
    names = list(SMALL)
    views = [_small_views(n) for n in names]
    mine = lambda src: [src[n].reshape(vw[0]) for n, vw in zip(names, views)]
    res = _adam_small(mine(w), [summed[n].reshape(vw[1]) for n, vw in zip(names, views)], [vw[2] for vw in views],
                      mine(m), mine(v), "adam_small")
    for n, out in zip(names, res):
        grads[n], deltas[n], new_m[n], new_v[n] = [t.reshape(_shard_shape(n)) for t in out]

    lead = lambda a: a[None]
    return (loss, grad_x[None], *[lead(grads[n]) for n in WEIGHTS], *[lead(deltas[n]) for n in WEIGHTS],
            *[lead(new_m[n]) for n in WEIGHTS], *[lead(new_v[n]) for n in WEIGHTS])
```

```python
import functools

import numpy as np
import jax
import jax.numpy as jnp
from jax import lax
from jax.experimental import pallas as pl
from jax.experimental.pallas import tpu as pltpu
from jax.experimental.pallas import tpu_sc as plsc

F32 = jnp.float32
BF16 = jnp.bfloat16

S = 2048
D = 2048
NDEV = 8
EPS = 1e-6
NEG = -1e30
HEAD_DIM = 128
ROT_DIM = 32
ROPE_THETA = 500000.0
PATTERNS = ((128, 1), (512, 4), (2048, 16))
BLK = 128
EVEN_IN = 12288
ODD_IN = 6144
HALF = 1024
CONV_K = 31
HALO = 32
TR = 256
SUB = 32

ADAM_LR = 0.001
ADAM_B1 = 0.9
ADAM_B2 = 0.999
ADAM_EPS = 1e-08
ADAM_WD = 0.01
ADAM_STEP = 10

VMEM_BIG = 56 * 1024 * 1024
MESH = pl.DeviceIdType.MESH

NN = (((1,), (0,)), ((), ()))
NT = (((1,), (1,)), ((), ()))
TN = (((0,), (0,)), ((), ()))


def _dot(a, b, dn=NN):
    return lax.dot_general(a, b, dn, preferred_element_type=F32)


def _sigmoid(x):
    return 1.0 / (1.0 + jnp.exp(-x))


def _silu_and_grad(x):
    sg = _sigmoid(x)
    return x * sg, sg * (1.0 + x * (1.0 - sg))


def _params(sem, vmem=None):
    return pltpu.CompilerParams(dimension_semantics=sem, vmem_limit_bytes=vmem)


ANY_SPEC = pl.BlockSpec(memory_space=pl.ANY)


def _matmul(a, b, *, dn, grid, a_spec, b_spec, o_spec, out_shape, out_dtype, acc_shape, name, dep=None):
    nk = grid[2]
    deps = [] if dep is None else list(dep) if isinstance(dep, (tuple, list)) else [dep]

    def body(a_ref, b_ref, *rest):
        o_ref, acc = rest[len(deps)], rest[len(deps) + 1:]
        if nk == 1:
            o_ref[...] = _dot(a_ref[...], b_ref[...], dn).astype(o_ref.dtype)
            return
        acc_ref = acc[0]
        k = pl.program_id(2)

        @pl.when(k == 0)
        def _():
            acc_ref[...] = jnp.zeros_like(acc_ref)

        acc_ref[...] += _dot(a_ref[...], b_ref[...], dn)

        @pl.when(k == nk - 1)
        def _():
            o_ref[...] = acc_ref[...].astype(o_ref.dtype)

    return pl.pallas_call(
        body, grid=grid, in_specs=[a_spec, b_spec] + [ANY_SPEC] * len(deps), out_specs=o_spec,
        out_shape=jax.ShapeDtypeStruct(out_shape, out_dtype),
        scratch_shapes=[] if nk == 1 else [pltpu.VMEM(acc_shape, F32)],
        compiler_params=_params(("parallel", "parallel", "arbitrary"), VMEM_BIG), name=name,
    )(a, b, *deps)


TM = 2048


def _mm_in(h, wg, name):
    nb = wg.shape[2]
    tn = 512 if nb % 512 == 0 else nb
    per = nb // tn
    return _matmul(
        h, wg, dn=NN, grid=(S // TM, NDEV * per, 1),
        a_spec=pl.BlockSpec((TM, D), lambda i, j, k: (i, 0)),
        b_spec=pl.BlockSpec((None, D, tn), lambda i, j, k: (j // per, 0, j % per)),
        o_spec=pl.BlockSpec((TM, tn), lambda i, j, k: (i, j)),
        out_shape=(S, NDEV * nb), out_dtype=F32, acc_shape=(TM, tn), name=name)


def _mm_in_halves(h, wg_halves, name):
    hb = wg_halves[0].shape[2]
    z = None
    for half, wg in enumerate(wg_halves):
        prev = [] if z is None else [z]

        def body(a_ref, b_ref, *rest):
            rest[-1][...] = _dot(a_ref[...], b_ref[...])

        z = pl.pallas_call(
            body, grid=(NDEV,),
            in_specs=[pl.BlockSpec((S, D), lambda j: (0, 0)), pl.BlockSpec((None, D, hb), lambda j: (j, 0, 0))]
                     + [ANY_SPEC] * len(prev),
            out_specs=pl.BlockSpec((S, hb), lambda j, half=half: (0, 2 * j + half)),
            out_shape=jax.ShapeDtypeStruct((S, 2 * NDEV * hb), F32),
            input_output_aliases={2: 0} if prev else {},
            compiler_params=_params(("parallel",), VMEM_BIG), name="%s_%d" % (name, half),
        )(h, wg, *prev)
    return z


def _mm_in_dx_halves(dz, wg_halves, name, dep):
    hb = wg_halves[0].shape[2]
    nk = 2 * NDEV

    def body(a_ref, b0_ref, b1_ref, dep_ref, o_ref, acc_ref):
        k = pl.program_id(2)

        @pl.when(k == 0)
        def _():
            acc_ref[...] = jnp.zeros_like(acc_ref)

        @pl.when(k % 2 == 0)
        def _():
            acc_ref[...] += _dot(a_ref[...], b0_ref[...], NT)

        @pl.when(k % 2 == 1)
        def _():
            acc_ref[...] += _dot(a_ref[...], b1_ref[...], NT)

        @pl.when(k == nk - 1)
        def _():
            o_ref[...] = acc_ref[...]

    b_spec = pl.BlockSpec((None, 1024, hb), lambda i, j, k: (k // 2, j, 0))
    return pl.pallas_call(
        body, grid=(1, D // 1024, nk),
        in_specs=[pl.BlockSpec((S, hb), lambda i, j, k: (0, k)), b_spec, b_spec, ANY_SPEC],
        out_specs=pl.BlockSpec((S, 1024), lambda i, j, k: (0, j)), out_shape=jax.ShapeDtypeStruct((S, D), F32),
        scratch_shapes=[pltpu.VMEM((S, 1024), F32)],
        compiler_params=_params(("parallel", "parallel", "arbitrary"), VMEM_BIG), name=name,
    )(dz, *wg_halves, dep)


def _mm_in_dx(dz, wg, name, dep=None):
    nb = wg.shape[2]
    return _matmul(
        dz, wg, dn=NT, grid=(S // TM, D // 1024, NDEV),
        a_spec=pl.BlockSpec((TM, nb), lambda i, j, k: (i, k)),
        b_spec=pl.BlockSpec((None, 1024, nb), lambda i, j, k: (k, j, 0)),
        o_spec=pl.BlockSpec((TM, 1024), lambda i, j, k: (i, j)),
        out_shape=(S, D), out_dtype=F32, acc_shape=(TM, 1024), name=name, dep=dep)


def _mm_out_dx(dy, w, name, dep=None):
    return _matmul(
        dy, w, dn=NT, grid=(S // TM, 2048 // 512, 1),
        a_spec=pl.BlockSpec((TM, D), lambda i, j, k: (i, 0)),
        b_spec=pl.BlockSpec((512, D), lambda i, j, k: (j, 0)),
        o_spec=pl.BlockSpec((TM, 512), lambda i, j, k: (i, j)),
        out_shape=(S, 2048), out_dtype=F32, acc_shape=(TM, 512), name=name, dep=dep)


def _mm_out_dw(yc, dy, name):
    return _matmul(
        yc, dy, dn=TN, grid=(2048 // TM, D // 512, 1),
        a_spec=pl.BlockSpec((S, TM), lambda i, j, k: (0, i)),
        b_spec=pl.BlockSpec((S, 512), lambda i, j, k: (0, j)),
        o_spec=pl.BlockSpec((TM, 512), lambda i, j, k: (i, j)),
        out_shape=(2048, D), out_dtype=BF16, acc_shape=(TM, 512), name=name)


def _row_spec(w=D):
    return pl.BlockSpec((TR, w), lambda i: (i, 0))


def _vec_spec(w=D):
    return pl.BlockSpec((1, w), lambda i: (0, 0))


def _rms_stats(x):
    r = lax.rsqrt(jnp.mean(x * x, axis=-1, keepdims=True) + EPS)
    return x * r, r


def _rms_bwd(dn, xhat, r, g):
    dxh = dn * g
    return r * (dxh - xhat * jnp.mean(dxh * xhat, axis=-1, keepdims=True))


def _acc_rows(ref, val, i):
    s = jnp.sum(val, axis=0, keepdims=True)

    @pl.when(i == 0)
    def _():
        ref[...] = s

    @pl.when(i > 0)
    def _():
        ref[...] += s


def _pre0_fwd(x, g, dep=None):
    deps = [] if dep is None else [dep]

    def body(x_ref, g_ref, *rest):
        xhat, _ = _rms_stats(x_ref[...])
        rest[-1][...] = (xhat * g_ref[...]).astype(BF16)

    return pl.pallas_call(
        body, grid=(S // TR,), in_specs=[_row_spec(), _vec_spec()] + [ANY_SPEC] * len(deps), out_specs=_row_spec(),
        out_shape=jax.ShapeDtypeStruct((S, D), BF16), compiler_params=_params(("parallel",)), name="pre0_fwd",
    )(x, g, *deps)


def _post0_fwd(ycat, w_out, x, g_post, g_pre1, dep):
    def body(yc_ref, w_ref, x_ref, gp_ref, g1_ref, dep_ref, y_ref, x1_ref, h1_ref):
        y = _dot(yc_ref[...], w_ref[...])
        y_ref[...] = y
        yhat, _ = _rms_stats(y)
        x1 = x_ref[...] + yhat * gp_ref[...]
        x1_ref[...] = x1
        xhat, _ = _rms_stats(x1)
        h1_ref[...] = (xhat * g1_ref[...]).astype(BF16)

    return pl.pallas_call(
        body, grid=(S // TR,),
        in_specs=[_row_spec(), pl.BlockSpec((2048, D), lambda i: (0, 0)), _row_spec(), _vec_spec(), _vec_spec(),
                  ANY_SPEC],
        out_specs=[_row_spec(), _row_spec(), _row_spec()],
        out_shape=[jax.ShapeDtypeStruct((S, D), F32), jax.ShapeDtypeStruct((S, D), F32),
                   jax.ShapeDtypeStruct((S, D), BF16)],
        compiler_params=_params(("parallel",), VMEM_BIG), name="post0_fwd",
    )(ycat, w_out, x, g_post, g_pre1, dep)


def _post1_bwd(ycat, w_out, x1, target, g_post, dep):
    def body(yc_ref, w_ref, x1_ref, t_ref, g_ref, dep_ref, loss_ref, dx2_ref, dy_ref, dg_ref):
        i = pl.program_id(0)
        yhat, r = _rms_stats(_dot(yc_ref[...], w_ref[...]))
        g = g_ref[...]
        err = x1_ref[...] + yhat * g - t_ref[...]
        part = jnp.sum(jnp.sum(err * err, axis=-1, keepdims=True), axis=0, keepdims=True) * (0.5 / D)
        _acc_rows(loss_ref, jnp.broadcast_to(part, (1, 128)), i)
        dx2 = err * (1.0 / D)
        dx2_ref[...] = dx2
        _acc_rows(dg_ref, dx2 * yhat, i)
        dy_ref[...] = _rms_bwd(dx2, yhat, r, g).astype(BF16)

    return pl.pallas_call(
        body, grid=(S // TR,),
        in_specs=[_row_spec(), pl.BlockSpec((2048, D), lambda i: (0, 0)), _row_spec(), _row_spec(), _vec_spec(),
                  ANY_SPEC],
        out_specs=[_vec_spec(128), _row_spec(), _row_spec(), _vec_spec()],
        out_shape=[jax.ShapeDtypeStruct((1, 128), F32), jax.ShapeDtypeStruct((S, D), F32),
                   jax.ShapeDtypeStruct((S, D), BF16), jax.ShapeDtypeStruct((1, D), F32)],
        compiler_params=_params(("arbitrary",), VMEM_BIG), name="post1_bwd",
    )(ycat, w_out, x1, target, g_post, dep)


def _mid_bwd(dx2, dh1, x1, y0, g_pre1, g_post0):
    def body(dx2_ref, dh_ref, x1_ref, y_ref, g1_ref, gp_ref, dx1_ref, dy_ref, dg1_ref, dgp_ref):
        i = pl.program_id(0)
        xhat, r1 = _rms_stats(x1_ref[...])
        dh = dh_ref[...]
        _acc_rows(dg1_ref, dh * xhat, i)
        dx1 = dx2_ref[...] + _rms_bwd(dh, xhat, r1, g1_ref[...])
        dx1_ref[...] = dx1
        yhat, r0 = _rms_stats(y_ref[...])
        _acc_rows(dgp_ref, dx1 * yhat, i)
        dy_ref[...] = _rms_bwd(dx1, yhat, r0, gp_ref[...]).astype(BF16)

    return pl.pallas_call(
        body, grid=(S // TR,),
        in_specs=[_row_spec(), _row_spec(), _row_spec(), _row_spec(), _vec_spec(), _vec_spec()],
        out_specs=[_row_spec(), _row_spec(), _vec_spec(), _vec_spec()],
        out_shape=[jax.ShapeDtypeStruct((S, D), F32), jax.ShapeDtypeStruct((S, D), BF16),
                   jax.ShapeDtypeStruct((1, D), F32), jax.ShapeDtypeStruct((1, D), F32)],
        compiler_params=_params(("arbitrary",)), name="mid_bwd",
    )(dx2, dh1, x1, y0, g_pre1, g_post0)


def _pre0_bwd(dx1, dh0, x, g):
    def body(dx1_ref, dh_ref, x_ref, g_ref, gx_ref, dg_ref):
        i = pl.program_id(0)
        xhat, r = _rms_stats(x_ref[...])
        dh = dh_ref[...]
        _acc_rows(dg_ref, dh * xhat, i)
        gx_ref[...] = dx1_ref[...] + _rms_bwd(dh, xhat, r, g_ref[...])

    return pl.pallas_call(
        body, grid=(S // TR,), in_specs=[_row_spec(), _row_spec(), _row_spec(), _vec_spec()],
        out_specs=[_row_spec(), _vec_spec()],
        out_shape=[jax.ShapeDtypeStruct((S, D), F32), jax.ShapeDtypeStruct((1, D), F32)],
        compiler_params=_params(("arbitrary",)), name="pre0_bwd",
    )(dx1, dh0, x, g)


POOL_CH = 256


def _pool_apply(a, w, transpose):
    n = a.shape[0]
    row = lax.broadcasted_iota(jnp.int32, a.shape, 0)
    cnt = jnp.minimum(row + 1, w).astype(F32)
    s = a / cnt if transpose else a
    for k in (1, 2, 4, 8):
        if transpose:
            sh = jnp.where(row < n - k, pltpu.roll(s, n - k, 0), 0.0)
        else:
            sh = jnp.where(row >= k, pltpu.roll(s, k, 0), 0.0)
        s = jnp.where(w > k, s + sh, s)
    return s - a if transpose else s / cnt - a


def _pool_fwd(z0, pool_w, pool_scale):
    def body(a_ref, gate_ref, w_ref, sc_ref, out_ref):
        win = jnp.left_shift(2, pl.program_id(0))
        pooled = _pool_apply(a_ref[...], win, False)
        mixed = _dot(pooled.astype(BF16), w_ref[...])
        gate = gate_ref[...]
        out_ref[...] = (mixed * sc_ref[...] * (gate * _sigmoid(gate))).astype(BF16)

    return pl.pallas_call(
        body, grid=(4,),
        in_specs=[pl.BlockSpec((S, POOL_CH), lambda g: (0, g)), pl.BlockSpec((S, POOL_CH), lambda g: (0, 4 + g)),
                  pl.BlockSpec((None, POOL_CH, POOL_CH), lambda g: (g, 0, 0)),
                  pl.BlockSpec((1, POOL_CH), lambda g: (0, g))],
        out_specs=pl.BlockSpec((S, POOL_CH), lambda g: (0, g)),
        out_shape=jax.ShapeDtypeStruct((S, 2048), BF16),
        compiler_params=_params(("parallel",), VMEM_BIG), name="pool_fwd",
    )(z0, z0, pool_w, pool_scale)


def _pool_bwd(z0, dycat, pool_w, pool_scale):
    def body(a_ref, gate_ref, dy_ref, w_ref, sc_ref, da_ref, dgate_ref, dw_ref, dsc_ref):
        win = jnp.left_shift(2, pl.program_id(0))
        pooled = _pool_apply(a_ref[...], win, False).astype(BF16)
        w = w_ref[...]
        mixed = _dot(pooled, w)
        silu, dsilu = _silu_and_grad(gate_ref[...])
        dy = dy_ref[...]
        sc = sc_ref[...]
        dgate_ref[...] = (dy * (mixed * sc) * dsilu).astype(BF16)
        dms = dy * silu
        dsc_ref[...] = jnp.sum(dms * mixed, axis=0, keepdims=True)
        dmixed = (dms * sc).astype(BF16)
        dw_ref[...] = _dot(pooled, dmixed, TN)
        dpooled = _dot(dmixed, w, NT)
        da_ref[...] = _pool_apply(dpooled, win, True).astype(BF16)

    slab = lambda off: pl.BlockSpec((S, POOL_CH), lambda g: (0, off + g))
    return pl.pallas_call(
        body, grid=(4,),
        in_specs=[slab(0), slab(4), slab(0), pl.BlockSpec((None, POOL_CH, POOL_CH), lambda g: (g, 0, 0)),
                  pl.BlockSpec((1, POOL_CH), lambda g: (0, g))],
        out_specs=[slab(0), slab(0), pl.BlockSpec((None, POOL_CH, POOL_CH), lambda g: (g, 0, 0)),
                   pl.BlockSpec((1, POOL_CH), lambda g: (0, g))],
        out_shape=[jax.ShapeDtypeStruct((S, HALF), BF16), jax.ShapeDtypeStruct((S, HALF), BF16),
                   jax.ShapeDtypeStruct((4, POOL_CH, POOL_CH), F32), jax.ShapeDtypeStruct((1, HALF), F32)],
        compiler_params=_params(("parallel",), VMEM_BIG), name="pool_bwd",
    )(z0, z0, dycat, pool_w, pool_scale)


Q_COL, K_COL, V_COL, BG_COL = 2048 // 128, 5120 // 128, 8192 // 128, 11264 // 128
SCALE = HEAD_DIM ** -0.5


def _rope_tables():
    pos = jnp.arange(S, dtype=F32)
    inv_freq = jnp.power(ROPE_THETA, -jnp.arange(0, ROT_DIM, 2, dtype=F32) / ROT_DIM)
    ang = pos[:, None] * inv_freq[None, :]
    cos, sin = jnp.cos(ang), jnp.sin(ang)
    half = ROT_DIM // 2
    zeros = jnp.zeros((S, HEAD_DIM - ROT_DIM), F32)
    c = jnp.concatenate([cos, cos, jnp.ones((S, HEAD_DIM - ROT_DIM), F32)], axis=1)
    a = jnp.concatenate([-sin, jnp.zeros((S, half), F32), zeros], axis=1)
    b = jnp.concatenate([jnp.zeros((S, half), F32), sin, zeros], axis=1)
    return c, a, b


def _rope(t, c, a, b):
    half = ROT_DIM // 2
    return t * c + pltpu.roll(t, HEAD_DIM - half, 1) * a + pltpu.roll(t, half, 1) * b


def _rope_t(d, c, a, b):
    half = ROT_DIM // 2
    return d * c + pltpu.roll(d * a, half, 1) + pltpu.roll(d * b, HEAD_DIM - half, 1)


def _deinterleave(dst, src, dil, cast=None, dst_off=0):
    length = S // dil
    for r in range(dil):
        v = src[...] if dil == 1 else src[pl.ds(r, length, stride=dil), :]
        dst[dst_off + r * length:dst_off + (r + 1) * length, :] = v if cast is None else v.astype(cast)


def _interleave(dst, src, dil, src_off=0):
    length = S // dil
    for r in range(dil):
        if dil == 1:
            dst[...] = src[src_off:src_off + S, :]
        else:
            dst[pl.ds(r, length, stride=dil), :] = src[src_off + r * length:src_off + (r + 1) * length, :]


CU = 4
NUNITS = S // BLK
B_QK = (((2,), (2,)), ((0,), (0,)))
B_PV = (((2,), (1,)), ((0,), (0,)))
B_TN = (((1,), (1,)), ((0,), (0,)))


def _blocks(ref, first):
    return ref[first * BLK:(first + CU) * BLK, :].reshape(CU, BLK, HEAD_DIM)


def _chunk_scores(u0, nb, qd, kdp):
    q = _blocks(qd, u0)
    row = lax.broadcasted_iota(jnp.int32, (CU, BLK, BLK), 1)
    col = lax.broadcasted_iota(jnp.int32, (CU, BLK, BLK), 2)
    s_own = jnp.where(col <= row, _dot(q, _blocks(kdp, u0 + 1), B_QK) * SCALE, NEG)
    if nb == 1:
        return q, s_own, None
    unit = lax.broadcasted_iota(jnp.int32, (CU, BLK, BLK), 0) + u0
    s_prev = jnp.where((col >= row) & ((unit % nb) != 0), _dot(q, _blocks(kdp, u0), B_QK) * SCALE, NEG)
    return q, s_own, s_prev


def _qkv_prep(z0, tabs):
    def body(q_ref, k_ref, v_ref, c_ref, a_ref, b_ref, qo_ref, ko_ref, vo_ref, tmp):
        p = pl.program_id(1)
        for gi, (_, dil) in enumerate(PATTERNS):
            @pl.when(p == gi)
            def _(dil=dil):
                c, a, b = c_ref[...], a_ref[...], b_ref[...]
                tmp[...] = _rope(q_ref[...], c, a, b)
                _deinterleave(qo_ref, tmp, dil, BF16)
                tmp[...] = _rope(k_ref[...], c, a, b)
                _deinterleave(ko_ref, tmp, dil, BF16)
                _deinterleave(vo_ref, v_ref, dil, BF16)

    tab = pl.BlockSpec((S, HEAD_DIM), lambda h, p: (0, 0))
    out = pl.BlockSpec((S, HEAD_DIM), lambda h, p: (0, p * 8 + h))
    return pl.pallas_call(
        body, grid=(8, 3), in_specs=[_head_spec(Q_COL), _head_spec(K_COL), _head_spec(V_COL), tab, tab, tab],
        out_specs=[out, out, out], out_shape=[jax.ShapeDtypeStruct((S, 3072), BF16)] * 3,
        scratch_shapes=[pltpu.VMEM((S, HEAD_DIM), F32)],
        compiler_params=_params(("parallel", "arbitrary"), VMEM_BIG), name="qkv_prep",
    )(z0, z0, z0, *tabs)


def _pad_copy(dst, src):
    dst[0:BLK, :] = jnp.zeros((BLK, HEAD_DIM), dst.dtype)
    dst[BLK:BLK + S, :] = src[...]


def _attn_group_fwd(dil, qd, kd_ref, vd_ref, kdp, vdp, od, ld, og, lg):
    nb = S // dil // BLK
    _pad_copy(kdp, kd_ref)
    _pad_copy(vdp, vd_ref)
    for u0 in range(0, NUNITS, CU):
        _, s_own, s_prev = _chunk_scores(u0, nb, qd, kdp)
        m = jnp.max(s_own, axis=2, keepdims=True)
        if s_prev is not None:
            m = jnp.maximum(m, jnp.max(s_prev, axis=2, keepdims=True))
        p_own = jnp.exp(s_own - m)
        den = jnp.sum(p_own, axis=2, keepdims=True)
        acc = _dot(p_own.astype(BF16), _blocks(vdp, u0 + 1), B_PV)
        if s_prev is not None:
            p_prev = jnp.exp(s_prev - m)
            den = den + jnp.sum(p_prev, axis=2, keepdims=True)
            acc = acc + _dot(p_prev.astype(BF16), _blocks(vdp, u0), B_PV)
        rows = slice(u0 * BLK, (u0 + CU) * BLK)
        od[rows, :] = (acc / den).reshape(CU * BLK, HEAD_DIM)
        ld[rows, :] = jnp.broadcast_to(m + jnp.log(den), (CU, BLK, HEAD_DIM)).reshape(CU * BLK, HEAD_DIM)
    _interleave(og, od, dil)
    _interleave(lg, ld, dil)


def _group_weights(lgs):
    l0, l1, l2 = lgs[0][...], lgs[1][...], lgs[2][...]
    mx = jnp.maximum(l0, jnp.maximum(l1, l2))
    e0, e1, e2 = jnp.exp(l0 - mx), jnp.exp(l1 - mx), jnp.exp(l2 - mx)
    den = e0 + e1 + e2
    return e0 / den, e1 / den, e2 / den


def _head_spec(base, ngroups_axis=True):
    return pl.BlockSpec((S, HEAD_DIM), lambda h, p: (0, base + (p % 3) * 8 + h))


def _slab(dtype=F32, rows=S):
    return pltpu.VMEM((rows, HEAD_DIM), dtype)


def _attn_fwd(z0, qkv, ycat):
    def body(q_ref, k_ref, v_ref, gate_ref, ycat_ref, out_ref, og_ref, lg_ref,
             kdp, vdp, od, ld, og0, og1, og2, lg0, lg1, lg2):
        del ycat_ref
        p = pl.program_id(1)
        ogs, lgs = (og0, og1, og2), (lg0, lg1, lg2)
        for gi, (_, dil) in enumerate(PATTERNS):
            @pl.when(p == gi)
            def _(gi=gi, dil=dil):
                _attn_group_fwd(dil, q_ref, k_ref, v_ref, kdp, vdp, od, ld, ogs[gi], lgs[gi])
                og_ref[...] = ogs[gi][...]
                lg_ref[...] = lgs[gi][...]

        @pl.when(p == 2)
        def _():
            w0, w1, w2 = _group_weights(lgs)
            o = w0 * og0[...] + w1 * og1[...] + w2 * og2[...]
            gate = gate_ref[...]
            out_ref[...] = (o * (gate * _sigmoid(gate))).astype(BF16)

    grp = pl.BlockSpec((S, HEAD_DIM), lambda h, p: (0, p * 8 + h))
    return pl.pallas_call(
        body, grid=(8, 3),
        in_specs=[grp, grp, grp, pl.BlockSpec((S, HEAD_DIM), lambda h, p: (0, BG_COL + h)), ANY_SPEC],
        out_specs=[pl.BlockSpec((S, HEAD_DIM), lambda h, p: (0, 8 + h)), grp, grp],
        out_shape=[jax.ShapeDtypeStruct((S, 2048), BF16), jax.ShapeDtypeStruct((S, 3072), F32),
                   jax.ShapeDtypeStruct((S, 3072), F32)],
        scratch_shapes=[_slab(BF16, S + BLK), _slab(BF16, S + BLK)] + [_slab() for _ in range(8)],
        input_output_aliases={4: 0},
        compiler_params=_params(("parallel", "arbitrary"), VMEM_BIG), name="attn_fwd",
    )(*qkv, z0, ycat)


def _attn_bwd(z0, qkv, og, lg, dycat, tabs, dep):
    def body(q_ref, k_ref, v_ref, gate_ref, dy_ref, c_ref, a_ref, b_ref,
             og0_ref, og1_ref, og2_ref, lg0_ref, lg1_ref, lg2_ref, dep_ref,
             dq_ref, dk_ref, dv_ref, dbg_ref,
             tmp, kd, vd, ld, dg0, dg1, dg2, cg0, cg1, cg2, dod, cd, dqd, dkd, dvd):
        p = pl.program_id(1)
        ogs, lgs, dgs, cgs = (og0_ref, og1_ref, og2_ref), (lg0_ref, lg1_ref, lg2_ref), (dg0, dg1, dg2), (cg0, cg1, cg2)

        @pl.when(p == 0)
        def _():
            w = _group_weights(lgs)
            o = w[0] * ogs[0][...] + w[1] * ogs[1][...] + w[2] * ogs[2][...]
            silu, dsilu = _silu_and_grad(gate_ref[...])
            dy = dy_ref[...]
            dbg_ref[...] = (dy * o * dsilu).astype(BF16)
            do = dy * silu
            dwbar = jnp.sum(do * o, axis=1, keepdims=True)
            for gi in range(3):
                dgs[gi][...] = w[gi] * do
                cgs[gi][...] = -w[gi] * dwbar

        for gi, (_, dil) in enumerate(PATTERNS):
            @pl.when(p == 1 + gi)
            def _(gi=gi, dil=dil):
                nb = S // dil // BLK
                qd = q_ref
                c, a, b = c_ref[...], a_ref[...], b_ref[...]
                _pad_copy(kd, k_ref)
                _pad_copy(vd, v_ref)
                _deinterleave(dod, dgs[gi], dil, BF16)
                _deinterleave(ld, lgs[gi], dil)
                _deinterleave(cd, cgs[gi], dil)
                dkd[...] = jnp.zeros_like(dkd)
                dvd[...] = jnp.zeros_like(dvd)
                flat = lambda t: t.reshape(CU * BLK, HEAD_DIM)
                for u0 in range(0, NUNITS, CU):
                    q, s_own, s_prev = _chunk_scores(u0, nb, qd, kd)
                    lse, cv, do = _blocks(ld, u0), _blocks(cd, u0), _blocks(dod, u0)
                    own = slice((u0 + 1) * BLK, (u0 + 1 + CU) * BLK)
                    p_own = jnp.exp(s_own - lse)
                    ds_own = (p_own * (_dot(do, _blocks(vd, u0 + 1), B_QK) + cv) * SCALE).astype(BF16)
                    dq = _dot(ds_own, _blocks(kd, u0 + 1), B_PV)
                    dkd[own, :] += flat(_dot(ds_own, q, B_TN))
                    dvd[own, :] += flat(_dot(p_own.astype(BF16), do, B_TN))
                    if s_prev is not None:
                        prev = slice(u0 * BLK, (u0 + CU) * BLK)
                        p_prev = jnp.exp(s_prev - lse)
                        ds_prev = (p_prev * (_dot(do, _blocks(vd, u0), B_QK) + cv) * SCALE).astype(BF16)
                        dq = dq + _dot(ds_prev, _blocks(kd, u0), B_PV)
                        dkd[prev, :] += flat(_dot(ds_prev, q, B_TN))
                        dvd[prev, :] += flat(_dot(p_prev.astype(BF16), do, B_TN))
                    dqd[u0 * BLK:(u0 + CU) * BLK, :] = flat(dq)
                _interleave(tmp, dqd, dil)
                dq_ref[...] = _rope_t(tmp[...], c, a, b).astype(BF16)
                _interleave(tmp, dkd, dil, BLK)
                dk_ref[...] = _rope_t(tmp[...], c, a, b).astype(BF16)
                _interleave(tmp, dvd, dil, BLK)
                dv_ref[...] = tmp[...].astype(BF16)

    tab = pl.BlockSpec((S, HEAD_DIM), lambda h, p: (0, 0))
    hspec = lambda base: pl.BlockSpec((S, HEAD_DIM), lambda h, p: (0, base + h))
    gspec = pl.BlockSpec((S, HEAD_DIM), lambda h, p: (0, jnp.maximum(p - 1, 0) * 8 + h))
    return pl.pallas_call(
        body, grid=(8, 4),
        in_specs=[gspec, gspec, gspec, hspec(BG_COL), hspec(8), tab, tab, tab,
                  hspec(0), hspec(8), hspec(16), hspec(0), hspec(8), hspec(16), ANY_SPEC],
        out_specs=[gspec, gspec, gspec, hspec(0)],
        out_shape=[jax.ShapeDtypeStruct((S, 3072), BF16)] * 3 + [jax.ShapeDtypeStruct((S, HALF), BF16)],
        scratch_shapes=[_slab(), _slab(BF16, S + BLK), _slab(BF16, S + BLK), _slab()] + [_slab() for _ in range(6)]
                       + [_slab(BF16), _slab(), _slab(), _slab(F32, S + BLK), _slab(F32, S + BLK)],
        compiler_params=_params(("parallel", "arbitrary"), VMEM_BIG), name="attn_bwd",
    )(*qkv, z0, dycat, *tabs, og, og, og, lg, lg, lg, dep)


SGU_CH = 256
NCHUNK = TR // 128


def _ln_stats(x):
    mu = jnp.mean(x, axis=-1, keepdims=True)
    xc = x - mu
    r = lax.rsqrt(jnp.mean(xc * xc, axis=-1, keepdims=True) + EPS)
    return xc * r, r


def _ln_bwd(dy, xhat, r, g):
    dxh = dy * g
    return r * (dxh - jnp.mean(dxh, axis=-1, keepdims=True) - xhat * jnp.mean(dxh * xhat, axis=-1, keepdims=True))


def _tril_bf16(w):
    row = lax.broadcasted_iota(jnp.int32, w.shape, 0)
    col = lax.broadcasted_iota(jnp.int32, w.shape, 1)
    return jnp.where(row >= col, w, 0.0).astype(BF16)


def _sgu_gate(vn_s, s_s, w_ref, bb_ref):
    for h in range(4):
        wm = _tril_bf16(w_ref[h])
        bias = bb_ref[h]
        for ch in range(NCHUNK):
            rows, cols = slice(ch * 128, (ch + 1) * 128), slice(h * SGU_CH, (h + 1) * SGU_CH)
            s_s[rows, cols] = _dot(wm, vn_s[rows, cols]) + jnp.concatenate([bias, bias], axis=1)


WIN = HALO + TR
SUBL = 8


def _shifted_copies(dst, src):
    dst[0] = src[...]
    for b in range(1, SUBL):
        dst[b, 0:WIN - SUBL, :] = src[pl.ds(b, WIN - SUBL), :]


def _rows_at(copies, off, n):
    return copies[off % SUBL, pl.ds(off - off % SUBL, n), :]


def _conv_fwd(i, dval_ref, dglu_ref, hval_ref, hglu_ref, cw_ref, cb_ref, xw, xr, dcs):
    halo = hval_ref[...] * _sigmoid(hglu_ref[...])
    xw[0:HALO, :] = jnp.where(i > 0, halo, 0.0)
    xw[HALO:HALO + TR, :] = dval_ref[...] * _sigmoid(dglu_ref[...])
    _shifted_copies(xr, xw)
    for rb in range(TR // SUB):
        acc = jnp.broadcast_to(cb_ref[...], (SUB, HALF))
        for k in range(CONV_K):
            acc = acc + cw_ref[k:k + 1, :] * _rows_at(xr, rb * SUB + HALO - (CONV_K - 1) + k, SUB)
        dcs[rb * SUB:(rb + 1) * SUB, :] = acc


def _odd_in_specs():
    col = lambda j: pl.BlockSpec((TR, HALF), lambda i, *_: (i, j))
    prev = lambda j: pl.BlockSpec((HALO, HALF), lambda i, *_: (jnp.maximum(i * (TR // HALO) - 1, 0), j))
    return [col(0), col(1), col(2), col(3), col(4), col(5), prev(3), prev(4)]


def _full_spec(shape):
    return pl.BlockSpec(shape, lambda i, *_: (0,) * len(shape))


def _odd_fwd(z1, sgu_g, sgu_b, sgu_w, sgu_bb, conv_w, conv_b, cn_g, cn_b):
    def body(u_ref, v_ref, cg_ref, dval_ref, dglu_ref, dgate_ref, hval_ref, hglu_ref,
             g_ref, b_ref, w_ref, bb_ref, cw_ref, cb_ref, cng_ref, cnb_ref, out_ref, dcs, vn_s, s_s, xw, xr):
        i = pl.program_id(0)
        vhat, _ = _ln_stats(v_ref[...])
        vn_s[...] = (vhat * g_ref[...] + b_ref[...]).astype(BF16)
        _sgu_gate(vn_s, s_s, w_ref, bb_ref)
        cg = cg_ref[...]
        out_ref[:, 0:HALF] = (u_ref[...] * s_s[...] * (cg * _sigmoid(cg))).astype(BF16)
        _conv_fwd(i, dval_ref, dglu_ref, hval_ref, hglu_ref, cw_ref, cb_ref, xw, xr, dcs)
        dhat, _ = _ln_stats(dcs[...])
        dn = dhat * cng_ref[...] + cnb_ref[...]
        dgate = dgate_ref[...]
        out_ref[:, HALF:2 * HALF] = ((dn * _sigmoid(dn)) * (dgate * _sigmoid(dgate))).astype(BF16)

    vec = _full_spec((1, HALF))
    return pl.pallas_call(
        body, grid=(S // TR,),
        in_specs=_odd_in_specs() + [vec, vec, _full_spec((4, 128, 128)), _full_spec((4, 128, 128)),
                                    _full_spec((HALO, HALF)), vec, vec, vec],
        out_specs=[pl.BlockSpec((TR, 2048), lambda i: (i, 0)), pl.BlockSpec((TR, HALF), lambda i: (i, 0))],
        out_shape=[jax.ShapeDtypeStruct((S, 2048), BF16), jax.ShapeDtypeStruct((S, HALF), F32)],
        scratch_shapes=[pltpu.VMEM((TR, HALF), BF16), pltpu.VMEM((TR, HALF), F32),
                        pltpu.VMEM((WIN, HALF), F32), pltpu.VMEM((SUBL, WIN, HALF), F32)],
        compiler_params=_params(("parallel",), VMEM_BIG), name="odd_fwd",
    )(z1, z1, z1, z1, z1, z1, z1, z1, sgu_g, sgu_b, sgu_w, sgu_bb, conv_w, conv_b, cn_g, cn_b)


def _odd_bwd_a(z1, dc, dycat, sgu_g, sgu_b, sgu_w, sgu_bb, cn_g, cn_b):
    def body(u_ref, v_ref, cg_ref, dgate_ref, dcs, dy_ref, g_ref, b_ref, w_ref, bb_ref, cng_ref, cnb_ref,
             dz_ref, ddc_ref, dw_ref, dbb_ref, dg_ref, db_ref, dcng_ref, dcnb_ref, dcb_ref,
             vn_s, s_s, ds_s, dvn_s):
        i = pl.program_id(0)
        vhat, rv = _ln_stats(v_ref[...])
        g = g_ref[...]
        vn_s[...] = (vhat * g + b_ref[...]).astype(BF16)
        _sgu_gate(vn_s, s_s, w_ref, bb_ref)
        silu_c, dsilu_c = _silu_and_grad(cg_ref[...])
        dyc = dy_ref[:, 0:HALF]
        u = u_ref[...]
        s = s_s[...]
        dz_ref[:, 0:HALF] = (dyc * s * silu_c).astype(BF16)
        dz_ref[:, 2 * HALF:3 * HALF] = (dyc * u * s * dsilu_c).astype(BF16)
        ds_s[...] = dyc * u * silu_c

        @pl.when(i == 0)
        def _():
            dw_ref[...] = jnp.zeros_like(dw_ref)
            dbb_ref[...] = jnp.zeros_like(dbb_ref)

        tril = lax.broadcasted_iota(jnp.int32, (128, 128), 0) >= lax.broadcasted_iota(jnp.int32, (128, 128), 1)
        for h in range(4):
            wm = _tril_bf16(w_ref[h])
            for ch in range(NCHUNK):
                rows, cols = slice(ch * 128, (ch + 1) * 128), slice(h * SGU_CH, (h + 1) * SGU_CH)
                ds = ds_s[rows, cols]
                dsb = ds.astype(BF16)
                dw_ref[h] += jnp.where(tril, _dot(dsb, vn_s[rows, cols], NT), 0.0)
                dbb_ref[h] += jnp.broadcast_to(jnp.sum(ds, axis=1, keepdims=True), (128, 128))
                dvn_s[rows, cols] = _dot(wm, dsb, TN)
        dvn = dvn_s[...]
        _acc_rows(dg_ref, dvn * vhat, i)
        _acc_rows(db_ref, dvn, i)
        dz_ref[:, HALF:2 * HALF] = _ln_bwd(dvn, vhat, rv, g).astype(BF16)

        dhat, rd = _ln_stats(dcs[...])
        cng = cng_ref[...]
        silu_n, dsilu_n = _silu_and_grad(dhat * cng + cnb_ref[...])
        silu_g, dsilu_g = _silu_and_grad(dgate_ref[...])
        dyd = dy_ref[:, HALF:2 * HALF]
        dz_ref[:, 5 * HALF:6 * HALF] = (dyd * silu_n * dsilu_g).astype(BF16)
        ddn = dyd * silu_g * dsilu_n
        _acc_rows(dcng_ref, ddn * dhat, i)
        _acc_rows(dcnb_ref, ddn, i)
        ddc = _ln_bwd(ddn, dhat, rd, cng)
        ddc_ref[...] = ddc
        _acc_rows(dcb_ref, ddc, i)

    vec = _full_spec((1, HALF))
    sq = _full_spec((4, 128, 128))
    col = lambda j: pl.BlockSpec((TR, HALF), lambda i: (i, j))
    return pl.pallas_call(
        body, grid=(S // TR,),
        in_specs=[col(0), col(1), col(2), col(5), col(0), pl.BlockSpec((TR, 2048), lambda i: (i, 0)),
                  vec, vec, sq, sq, vec, vec],
        out_specs=[pl.BlockSpec((TR, ODD_IN), lambda i: (i, 0)), pl.BlockSpec((TR, HALF), lambda i: (i, 0)),
                   sq, sq, vec, vec, vec, vec, vec],
        out_shape=[jax.ShapeDtypeStruct((S, ODD_IN), BF16), jax.ShapeDtypeStruct((S, HALF), F32),
                   jax.ShapeDtypeStruct((4, 128, 128), F32), jax.ShapeDtypeStruct((4, 128, 128), F32)]
                  + [jax.ShapeDtypeStruct((1, HALF), F32)] * 5,
        scratch_shapes=[pltpu.VMEM((TR, HALF), BF16), pltpu.VMEM((TR, HALF), F32),
                        pltpu.VMEM((TR, HALF), F32), pltpu.VMEM((TR, HALF), F32)],
        compiler_params=_params(("arbitrary",), VMEM_BIG), name="odd_bwd_a",
    )(z1, z1, z1, z1, dc, dycat, sgu_g, sgu_b, sgu_w, sgu_bb, cn_g, cn_b)


def _odd_bwd_b(z1, ddc, dz1, conv_w):
    nt = S // TR

    def body(dval_ref, dglu_ref, hval_ref, hglu_ref, ddc_ref, hddc_ref, cw_ref, dz_in_ref,
             dz_ref, dcw_ref, xw, dwin, dxs, xr, dr):
        del dz_in_ref
        i, j = pl.program_id(0), pl.program_id(1)
        sg = _sigmoid(dglu_ref[...])
        dval = dval_ref[...]

        @pl.when(j == 0)
        def _():
            halo = hval_ref[...] * _sigmoid(hglu_ref[...])
            xw[0:HALO, :] = jnp.where(i > 0, halo, 0.0)
            xw[HALO:HALO + TR, :] = dval * sg
            dwin[0:TR, :] = ddc_ref[...]
            dwin[TR:TR + HALO, :] = jnp.where(i < nt - 1, hddc_ref[...], 0.0)
            _shifted_copies(xr, xw)
            _shifted_copies(dr, dwin)

            @pl.when(i == 0)
            def _():
                dcw_ref[...] = jnp.zeros_like(dcw_ref)

            for rb in range(TR // SUB):
                acc = jnp.zeros((SUB, HALF), F32)
                for k in range(CONV_K):
                    acc = acc + cw_ref[k:k + 1, :] * _rows_at(dr, rb * SUB + (CONV_K - 1) - k, SUB)
                dxs[rb * SUB:(rb + 1) * SUB, :] = acc
            for k in range(CONV_K):
                acc = jnp.zeros((SUB, HALF), F32)
                for rb in range(TR // SUB):
                    acc = acc + dwin[rb * SUB:(rb + 1) * SUB, :] * _rows_at(xr, rb * SUB + HALO - (CONV_K - 1) + k, SUB)
                dcw_ref[k:k + 1, :] += jnp.sum(acc, axis=0, keepdims=True)
            dz_ref[...] = (dxs[...] * sg).astype(BF16)

        @pl.when(j == 1)
        def _():
            dz_ref[...] = (dxs[...] * dval * sg * (1.0 - sg)).astype(BF16)

    col = lambda c: pl.BlockSpec((TR, HALF), lambda i, j: (i, c))
    prev = lambda c: pl.BlockSpec((HALO, HALF), lambda i, j: (jnp.maximum(i * (TR // HALO) - 1, 0), c))
    nxt = pl.BlockSpec((HALO, HALF), lambda i, j: (jnp.minimum((i + 1) * (TR // HALO), S // HALO - 1), 0))
    return pl.pallas_call(
        body, grid=(nt, 2),
        in_specs=[col(3), col(4), prev(3), prev(4), pl.BlockSpec((TR, HALF), lambda i, j: (i, 0)), nxt,
                  _full_spec((HALO, HALF)), pl.BlockSpec(memory_space=pl.ANY)],
        out_specs=[pl.BlockSpec((TR, HALF), lambda i, j: (i, 3 + j)), _full_spec((HALO, HALF))],
        out_shape=[jax.ShapeDtypeStruct((S, ODD_IN), BF16), jax.ShapeDtypeStruct((HALO, HALF), F32)],
        scratch_shapes=[pltpu.VMEM((WIN, HALF), F32), pltpu.VMEM((WIN, HALF), F32), pltpu.VMEM((TR, HALF), F32),
                        pltpu.VMEM((SUBL, WIN, HALF), F32), pltpu.VMEM((SUBL, WIN, HALF), F32)],
        input_output_aliases={7: 0},
        compiler_params=_params(("arbitrary", "arbitrary"), VMEM_BIG), name="odd_bwd_b",
    )(z1, z1, z1, z1, ddc, ddc, conv_w, dz1)


def _cast_bf16(w, name, piece=0, npieces=1):
    r, c = w.shape[0], w.shape[1] // npieces
    tr = min(r, 256)

    def body(i_ref, o_ref):
        o_ref[...] = i_ref[...].astype(BF16)

    return pl.pallas_call(
        body, grid=(r // tr,), in_specs=[pl.BlockSpec((tr, c), lambda i: (i, piece))],
        out_specs=pl.BlockSpec((tr, c), lambda i: (i, 0)), out_shape=jax.ShapeDtypeStruct((r, c), BF16),
        compiler_params=_params(("parallel",)), name=name,
    )(w)


def _adamw(w, g, m, v):
    m = ADAM_B1 * m + (1.0 - ADAM_B1) * g
    v = ADAM_B2 * v + (1.0 - ADAM_B2) * (g * g)
    m_hat = m / (1.0 - ADAM_B1 ** ADAM_STEP)
    v_hat = v / (1.0 - ADAM_B2 ** ADAM_STEP)
    delta = -ADAM_LR * (m_hat / (jnp.sqrt(v_hat) + ADAM_EPS) + ADAM_WD * w)
    return delta, m, v


def _adam_reduce(parts, w, m, v, name, dep=None, piece=0, npieces=1, prev=None):
    r, c = w.shape
    cp = c // npieces
    tr = min(r, 128)
    extra = ([] if dep is None else [dep]) + ([] if prev is None else list(prev))
    nparts = parts.shape[0]

    def body(p_ref, w_ref, m_ref, v_ref, *rest):
        g_ref, d_ref, nm_ref, nv_ref = rest[len(extra):]
        g = p_ref[0].astype(F32)
        for d in range(1, nparts):
            g = g + p_ref[d].astype(F32)
        g_ref[...] = g
        d_ref[...], nm_ref[...], nv_ref[...] = _adamw(w_ref[...], g, m_ref[...], v_ref[...])

    spec = pl.BlockSpec((tr, cp), lambda i: (i, piece))
    first = 4 + (0 if dep is None else 1)
    return pl.pallas_call(
        body, grid=(r // tr,),
        in_specs=[pl.BlockSpec((nparts, tr, cp), lambda i: (0, i, 0)), spec, spec, spec] + [ANY_SPEC] * len(extra),
        out_specs=[spec] * 4, out_shape=[jax.ShapeDtypeStruct((r, c), F32)] * 4,
        input_output_aliases={} if prev is None else {first + k: k for k in range(4)},
        compiler_params=_params(("parallel",), VMEM_BIG), name=name,
    )(parts, w, m, v, *extra)


def _arrived(x, name, dep=None):
    deps = [] if dep is None else [dep]

    def body(*refs):
        refs[-1][...] = jnp.zeros_like(refs[-1])

    return pl.pallas_call(
        body, in_specs=[ANY_SPEC] * (1 + len(deps)), out_specs=pl.BlockSpec(memory_space=pltpu.VMEM),
        out_shape=jax.ShapeDtypeStruct((8, 128), F32), name=name,
    )(x, *deps)


def _sum_parts(parts, name, dep=None):
    r = parts.shape[1]
    tr = 8
    for cand in (512, 256, 128, 64, 32, 16, 8):
        if r % cand == 0:
            tr = cand
            break
    deps = [] if dep is None else [dep]

    def body(p_ref, *rest):
        g = p_ref[0]
        for d in range(1, NDEV):
            g = g + p_ref[d]
        rest[-1][...] = g

    return pl.pallas_call(
        body, grid=(r // tr,), in_specs=[pl.BlockSpec((NDEV, tr, 128), lambda i: (0, i, 0))] + [ANY_SPEC] * len(deps),
        out_specs=pl.BlockSpec((tr, 128), lambda i: (i, 0)), out_shape=jax.ShapeDtypeStruct((r, 128), F32),
        compiler_params=_params(("parallel",)), name=name,
    )(parts, *deps)


def _sum_unpack(parts, rows, name, dep=None):
    deps = [] if dep is None else [dep]

    def body(p_ref, *outs):
        outs = outs[len(deps):]
        off = 0
        for o_ref, n in zip(outs, rows):
            acc = p_ref[0, off:off + n, :]
            for d in range(1, NDEV):
                acc = acc + p_ref[d, off:off + n, :]
            o_ref[...] = acc
            off += n

    return pl.pallas_call(
        body, grid=(1,), in_specs=[pl.BlockSpec(parts.shape, lambda i: (0, 0, 0))] + [ANY_SPEC] * len(deps),
        out_specs=[pl.BlockSpec((n, 128), lambda i: (0, 0)) for n in rows],
        out_shape=[jax.ShapeDtypeStruct((n, 128), F32) for n in rows],
        compiler_params=_params(("arbitrary",), VMEM_BIG), name=name,
    )(parts, *deps)


def _adam_small(ws, gs, g_specs, ms, vs, name):
    n = len(ws)

    def body(*refs):
        w_r, g_r, m_r, v_r = refs[:n], refs[n:2 * n], refs[2 * n:3 * n], refs[3 * n:4 * n]
        outs = refs[4 * n:]
        for i in range(n):
            g = g_r[i][...]
            outs[4 * i][...] = g
            outs[4 * i + 1][...], outs[4 * i + 2][...], outs[4 * i + 3][...] = _adamw(
                w_r[i][...], g, m_r[i][...], v_r[i][...])

    whole = lambda a: pl.BlockSpec(a.shape, lambda i, nd=a.ndim: (0,) * nd)
    outs = pl.pallas_call(
        body, grid=(1,),
        in_specs=[whole(a) for a in ws] + list(g_specs) + [whole(a) for a in ms] + [whole(a) for a in vs],
        out_specs=[whole(a) for a in ws for _ in range(4)],
        out_shape=[jax.ShapeDtypeStruct(a.shape, F32) for a in ws for _ in range(4)],
        compiler_params=_params(("arbitrary",), VMEM_BIG), name=name,
    )(*ws, *gs, *ms, *vs)
    return [outs[4 * i:4 * i + 4] for i in range(n)]


MASKS = [(mx, my, mc) for mx in (0, 1) for my in (0, 1) for mc in (0, 1)][1:]


def _sc_exchange(name, collective_id, arrays, scatter):
    nt = len(arrays)
    out_type = [jax.ShapeDtypeStruct(a.shape if scatter else (NDEV,) + a.shape, a.dtype) for a in arrays]

    def body(*refs):
        ins, outs = refs[:nt], refs[nt:2 * nt]
        send_sems, recv_sems, local_sems = refs[2 * nt:3 * nt], refs[3 * nt:4 * nt], refs[4 * nt:5 * nt]
        x, y, c = lax.axis_index("x"), lax.axis_index("y"), lax.axis_index("c")
        peers = [(mx + x - 2 * mx * x, my + y - 2 * my * y, mc + c - 2 * mc * c) for mx, my, mc in MASKS]
        barrier = pltpu.get_barrier_semaphore()
        for peer in peers:
            pl.semaphore_signal(barrier, inc=1, device_id=peer, device_id_type=MESH)
        pl.semaphore_wait(barrier, len(peers))
        me = 4 * x + 2 * y + c
        own = []
        for t in range(nt):
            cp = pltpu.make_async_copy(ins[t].at[me] if scatter else ins[t], outs[t].at[me], local_sems[t])
            cp.start()
            own.append(cp)
            for px, py, pc in peers:
                src = ins[t].at[4 * px + 2 * py + pc] if scatter else ins[t]
                pltpu.make_async_remote_copy(src_ref=src, dst_ref=outs[t].at[me], send_sem=send_sems[t],
                                             recv_sem=recv_sems[t], device_id=(px, py, pc), device_id_type=MESH).start()
        for t in range(nt):
            own[t].wait()
            seven = outs[t].at[pl.ds(0, NDEV - 1)]
            drain = pltpu.make_async_remote_copy(src_ref=seven, dst_ref=seven, send_sem=send_sems[t],
                                                 recv_sem=recv_sems[t], device_id=(x, y, c), device_id_type=MESH)
            drain.wait_send()
            drain.wait_recv()

    return pl.kernel(
        body, out_type=out_type, mesh=plsc.ScalarSubcoreMesh(axis_name="sequencer", num_cores=1),
        scratch_types=[pltpu.SemaphoreType.DMA] * (3 * nt),
        compiler_params=pltpu.CompilerParams(collective_id=collective_id), name=name,
    )(*arrays)


def _sc_gather_two_level(name, collective_id, arrays):
    nt = len(arrays)
    out_type = [jax.ShapeDtypeStruct((NDEV,) + a.shape, a.dtype) for a in arrays]

    def body(*refs):
        ins, outs = refs[:nt], refs[nt:2 * nt]
        sems = refs[2 * nt:]
        send_sems, sib_sems, local_sems = sems[:nt], sems[nt:2 * nt], sems[2 * nt:3 * nt]
        ici_sems = [sems[3 * nt + 3 * t:3 * nt + 3 * t + 3] for t in range(nt)]
        x, y, c = lax.axis_index("x"), lax.axis_index("y"), lax.axis_index("c")
        sibling = (x, y, 1 - c)
        chips = [(1 - x, y), (x, 1 - y), (1 - x, 1 - y)]
        barrier = pltpu.get_barrier_semaphore()
        for peer in [sibling] + [(cx, cy, c) for cx, cy in chips]:
            pl.semaphore_signal(barrier, inc=1, device_id=peer, device_id_type=MESH)
        pl.semaphore_wait(barrier, 4)
        me = 4 * x + 2 * y + c

        def push(t, src, slot, recv_sem, to):
            pltpu.make_async_remote_copy(src_ref=src, dst_ref=outs[t].at[slot], send_sem=send_sems[t],
                                         recv_sem=recv_sem, device_id=to, device_id_type=MESH).start()

        own = []
        for t in range(nt):
            cp = pltpu.make_async_copy(ins[t], outs[t].at[me], local_sems[t])
            cp.start()
            own.append(cp)
            for j, (cx, cy) in enumerate(chips):
                push(t, ins[t], me, ici_sems[t][j], (cx, cy, c))
            push(t, ins[t], me, sib_sems[t], sibling)
        for t in range(nt):
            for j, (cx, cy) in enumerate(chips):
                slot = 4 * cx + 2 * cy + c
                landed = outs[t].at[slot]
                pltpu.make_async_remote_copy(src_ref=landed, dst_ref=landed, send_sem=send_sems[t],
                                             recv_sem=ici_sems[t][j], device_id=(cx, cy, c),
                                             device_id_type=MESH).wait_recv()
                push(t, landed, slot, sib_sems[t], sibling)
        for t in range(nt):
            own[t].wait()
            four, seven = outs[t].at[pl.ds(0, 4)], outs[t].at[pl.ds(0, 7)]
            pltpu.make_async_remote_copy(src_ref=four, dst_ref=four, send_sem=send_sems[t], recv_sem=sib_sems[t],
                                         device_id=sibling, device_id_type=MESH).wait_recv()
            pltpu.make_async_remote_copy(src_ref=seven, dst_ref=seven, send_sem=send_sems[t], recv_sem=sib_sems[t],
                                         device_id=sibling, device_id_type=MESH).wait_send()

    return pl.kernel(
        body, out_type=out_type, mesh=plsc.ScalarSubcoreMesh(axis_name="sequencer", num_cores=1),
        scratch_types=[pltpu.SemaphoreType.DMA] * (6 * nt),
        compiler_params=pltpu.CompilerParams(collective_id=collective_id), name=name,
    )(*arrays)


def _sc_sibling_exchange(name, collective_id, src, out_shape, pieces):
    def body(src_ref, out_ref, send_sem, recv_sem):
        x, y, c = lax.axis_index("x"), lax.axis_index("y"), lax.axis_index("c")
        sibling = (x, y, 1 - c)
        barrier = pltpu.get_barrier_semaphore()
        pl.semaphore_signal(barrier, inc=1, device_id=sibling, device_id_type=MESH)
        pl.semaphore_wait(barrier, 1)
        for piece, lands in pieces(c, src_ref, out_ref):
            pltpu.make_async_remote_copy(src_ref=piece, dst_ref=lands, send_sem=send_sem, recv_sem=recv_sem,
                                         device_id=sibling, device_id_type=MESH).start()
        drain = pltpu.make_async_remote_copy(src_ref=out_ref, dst_ref=out_ref, send_sem=send_sem, recv_sem=recv_sem,
                                             device_id=sibling, device_id_type=MESH)
        drain.wait_send()
        drain.wait_recv()

    return pl.kernel(
        body, out_type=jax.ShapeDtypeStruct(out_shape, src.dtype),
        mesh=plsc.ScalarSubcoreMesh(axis_name="sequencer", num_cores=1), scratch_types=[pltpu.SemaphoreType.DMA] * 2,
        compiler_params=pltpu.CompilerParams(collective_id=collective_id), name=name,
    )(src)


def _swap_class_columns(name, collective_id, dz, nb, piece=0, npieces=1):
    w = nb // npieces
    return _sc_sibling_exchange(
        name, collective_id, dz, (S, 4 * w),
        lambda c, src, out: [(src.at[:, pl.ds((2 * j + 1 - c) * nb + piece * w, w)], out.at[:, pl.ds(j * w, w)])
                             for j in range(4)])


def _sc_chip_scatter(name, collective_id, q):
    def body(q_ref, out_ref, send_sem, recv_sem, local_sem):
        x, y, c = lax.axis_index("x"), lax.axis_index("y"), lax.axis_index("c")
        chips = [(1 - x, y), (x, 1 - y), (1 - x, 1 - y)]
        barrier = pltpu.get_barrier_semaphore()
        for cx, cy in chips:
            pl.semaphore_signal(barrier, inc=1, device_id=(cx, cy, c), device_id_type=MESH)
        pl.semaphore_wait(barrier, 3)
        mine = 2 * x + y
        own = pltpu.make_async_copy(q_ref.at[mine], out_ref.at[mine], local_sem)
        own.start()
        for cx, cy in chips:
            pltpu.make_async_remote_copy(src_ref=q_ref.at[2 * cx + cy], dst_ref=out_ref.at[mine], send_sem=send_sem,
                                         recv_sem=recv_sem, device_id=(cx, cy, c), device_id_type=MESH).start()
        own.wait()
        three = out_ref.at[pl.ds(0, 3)]
        drain = pltpu.make_async_remote_copy(src_ref=three, dst_ref=three, send_sem=send_sem, recv_sem=recv_sem,
                                             device_id=(x, y, c), device_id_type=MESH)
        drain.wait_send()
        drain.wait_recv()

    return pl.kernel(
        body, out_type=jax.ShapeDtypeStruct(q.shape, q.dtype),
        mesh=plsc.ScalarSubcoreMesh(axis_name="sequencer", num_cores=1), scratch_types=[pltpu.SemaphoreType.DMA] * 3,
        compiler_params=pltpu.CompilerParams(collective_id=collective_id), name=name,
    )(q)


def _mm_pair_dw(h_own, dz, h_sib, dz_sib, nb, name, dep=None, piece=0, npieces=1):
    nb = nb // npieces
    tn = 512 if nb % 512 == 0 else nb
    per = nb // tn
    o_spec = pl.BlockSpec((None, D, tn), lambda i, j, k: (j // per, 0, j % per))
    own_col = lambda i, j, k: (0, ((2 * (j // per) + lax.axis_index("c")) * npieces + piece) * per + j % per)
    part = _matmul(
        h_own, dz, dn=TN, grid=(1, 4 * per, 1),
        a_spec=pl.BlockSpec((S, D), lambda i, j, k: (0, 0)), b_spec=pl.BlockSpec((S, tn), own_col),
        o_spec=o_spec, out_shape=(4, D, nb), out_dtype=F32, acc_shape=(D, tn), name=name + "_own", dep=dep)

    def body(a_ref, b_ref, p_ref, o_ref):
        o_ref[...] = (p_ref[...] + _dot(a_ref[...], b_ref[...], TN)).astype(BF16)

    return pl.pallas_call(
        body, grid=(1, 4 * per, 1),
        in_specs=[pl.BlockSpec((S, D), lambda i, j, k: (0, 0)), pl.BlockSpec((S, tn), lambda i, j, k: (0, j)), o_spec],
        out_specs=o_spec, out_shape=jax.ShapeDtypeStruct((4, D, nb), BF16),
        compiler_params=_params(("parallel", "parallel", "arbitrary"), VMEM_BIG), name=name + "_sibling",
    )(h_sib, dz_sib, part)


SMALL = {
    "e_pre_norm": ((2048,), None), "e_pool_w": ((4, 256, 256), 1), "e_pool_scale": ((1024,), None),
    "e_post_norm": ((2048,), None), "o_pre_norm": ((2048,), 0), "o_sgu_norm_g": ((1024,), 0),
    "o_sgu_norm_b": ((1024,), 0), "o_sgu_w": ((4, 128, 128), None), "o_sgu_b": ((4, 128), None),
    "o_conv_w": ((31, 1024), 1), "o_conv_b": ((1024,), 0), "o_conv_norm_g": ((1024,), 0),
    "o_conv_norm_b": ((1024,), 0), "o_post_norm": ((2048,), 0),
}
SMALL_SHARDED = [n for n, (_, ax) in SMALL.items() if ax is not None]


def _shard_shape(name):
    shape, ax = SMALL[name]
    if ax is None:
        return shape
    return tuple(s // NDEV if i == ax else s for i, s in enumerate(shape))


def _pack(arrs, row_multiple=1):
    flat = jnp.concatenate([a.reshape(-1) for a in arrs])
    pad = -flat.shape[0] % (128 * row_multiple)
    return jnp.concatenate([flat, jnp.zeros((pad,), F32)]).reshape(-1, 128)


def _small_views(name):
    shape, ax = SMALL[name]
    me = lambda: 4 * lax.axis_index("x") + 2 * lax.axis_index("y") + lax.axis_index("c")
    if ax is None:
        view = (int(np.prod(shape)) // 128, 128)
        return view, view, pl.BlockSpec(view, lambda i: (0, 0))
    if len(shape) == 1:
        n = shape[0] // NDEV
        return (1, n), (NDEV, 1, n), pl.BlockSpec((None, 1, n), lambda i: (me(), 0, 0))
    part = _shard_shape(name)
    return part, shape, pl.BlockSpec(part, lambda i: tuple(me() if d == ax else 0 for d in range(len(shape))))


BIG = ("e_w_in", "e_w_out", "o_w_in", "o_w_out")
WEIGHTS = ["e_pre_norm", "e_w_in", "e_pool_w", "e_pool_scale", "e_w_out", "e_post_norm", "o_pre_norm", "o_w_in",
           "o_sgu_norm_g", "o_sgu_norm_b", "o_sgu_w", "o_sgu_b", "o_conv_w", "o_conv_b", "o_conv_norm_g",
           "o_conv_norm_b", "o_w_out", "o_post_norm"]


def kernel(x, e_pre_norm, e_w_in, e_pool_w, e_pool_scale, e_w_out, e_post_norm, o_pre_norm, o_w_in, o_sgu_norm_g, o_sgu_norm_b, o_sgu_w, o_sgu_b, o_conv_w, o_conv_b, o_conv_norm_g, o_conv_norm_b, o_w_out, o_post_norm, loss_target, m_e_pre_norm, m_e_w_in, m_e_pool_w, m_e_pool_scale, m_e_w_out, m_e_post_norm, m_o_pre_norm, m_o_w_in, m_o_sgu_norm_g, m_o_sgu_norm_b, m_o_sgu_w, m_o_sgu_b, m_o_conv_w, m_o_conv_b, m_o_conv_norm_g, m_o_conv_norm_b, m_o_w_out, m_o_post_norm, v_e_pre_norm, v_e_w_in, v_e_pool_w, v_e_pool_scale, v_e_w_out, v_e_post_norm, v_o_pre_norm, v_o_w_in, v_o_sgu_norm_g, v_o_sgu_norm_b, v_o_sgu_w, v_o_sgu_b, v_o_conv_w, v_o_conv_b, v_o_conv_norm_g, v_o_conv_norm_b, v_o_w_out, v_o_post_norm):
    given = dict(locals())
    w = {n: given[n][0] for n in WEIGHTS}
    m = {n: given["m_" + n][0] for n in WEIGHTS}
    v = {n: given["v_" + n][0] for n in WEIGHTS}
    me = 4 * lax.axis_index("x") + 2 * lax.axis_index("y") + lax.axis_index("c")
    x, target = x[0], loss_target[0]
    row = lambda a: a.reshape(1, -1)

    lo, small_rows = _sc_gather_two_level(
        "gather_a0", 0, [_cast_bf16(w["e_w_in"], "cast_e_w_in_0", 0, 2), _pack([w[n] for n in SMALL_SHARDED])])
    hi, = _sc_gather_two_level("gather_a1", 12, [_cast_bf16(w["e_w_in"], "cast_e_w_in_1", 1, 2)])
    wg_e_in = (lo, hi)
    h0 = _pre0_fwd(x, row(w["e_pre_norm"]))
    wg_e_out, = _sc_gather_two_level("gather_b", 1, [_cast_bf16(w["e_w_out"], "cast_e_w_out")])
    wg_o_in, wg_o_out = _sc_gather_two_level(
        "gather_c", 13, [_cast_bf16(w[n], "cast_" + n) for n in ("o_w_in", "o_w_out")])
    h0_sib = _sc_sibling_exchange("swap_h0", 8, h0, h0.shape, lambda c, src, out: [(src, out)])
    p = {n: w[n] for n in SMALL if SMALL[n][1] is None}
    small_rows = small_rows.reshape(NDEV, -1)
    off = 0
    for n in SMALL_SHARDED:
        shp, ax = _shard_shape(n), SMALL[n][1]
        cnt = int(np.prod(shp))
        blk = small_rows[:, off:off + cnt].reshape((NDEV,) + shp)
        p[n] = jnp.moveaxis(blk, 0, ax).reshape(SMALL[n][0])
        off += cnt
    tabs = _rope_tables()
    pool_w_bf = p["e_pool_w"].astype(BF16)
    sgu_bb = jnp.broadcast_to(p["o_sgu_b"][:, :, None], (4, 128, 128))
    conv_w = jnp.concatenate([p["o_conv_w"], jnp.zeros((HALO - CONV_K, HALF), F32)], axis=0)
    odd_p = (row(p["o_sgu_norm_g"]), row(p["o_sgu_norm_b"]), p["o_sgu_w"], sgu_bb, conv_w,
             row(p["o_conv_b"]), row(p["o_conv_norm_g"]), row(p["o_conv_norm_b"]))

    z0 = _mm_in_halves(h0, wg_e_in, "mm_z0")
    ycat0 = _pool_fwd(z0, pool_w_bf, row(p["e_pool_scale"]))
    qkv = _qkv_prep(z0, tabs)
    ycat0, og, lg = _attn_fwd(z0, qkv, ycat0)
    w_out_e, w_out_o = wg_e_out.reshape(2048, D), wg_o_out.reshape(2048, D)
    y0, x1, h1 = _post0_fwd(ycat0, w_out_e, x, row(p["e_post_norm"]), row(p["o_pre_norm"]), h0_sib)
    h1_sib = _sc_sibling_exchange("swap_h1", 11, h1, h1.shape, lambda c, src, out: [(src, out)])
    z1 = _mm_in(h1, wg_o_in, "mm_z1")
    ycat1, conv_out = _odd_fwd(z1, *odd_p)

    g = {}
    loss, dx2, dy1, g["o_post_norm"] = _post1_bwd(ycat1, w_out_o, x1, target, row(p["o_post_norm"]), h1_sib)
    loss = lax.psum(loss[0, 0], ("x", "y", "c"))
    parts = {}
    dw = _mm_out_dw(ycat1, dy1, "mm_dwout1").reshape(NDEV, 256, D)
    parts["o_w_out"], = _sc_exchange("scatter_o_w_out", 2, [dw], True)
    dycat1 = _mm_out_dx(dy1, w_out_o, "mm_dycat1", (dw, loss.reshape(1, 1)))
    dz1, ddc, g["o_sgu_w"], d_sgu_bb, g["o_sgu_norm_g"], g["o_sgu_norm_b"], g["o_conv_norm_g"], \
        g["o_conv_norm_b"], g["o_conv_b"] = _odd_bwd_a(z1, conv_out, dycat1, *odd_p[:4], *odd_p[6:])
    dz1, d_conv_w = _odd_bwd_b(z1, ddc, dz1, conv_w)
    g["o_sgu_b"] = d_sgu_bb[:, :, 0]
    g["o_conv_w"] = d_conv_w[:CONV_K]
    grads, deltas, new_m, new_v = {}, {}, {}, {}

    def adam(n, dep):
        grads[n], deltas[n], new_m[n], new_v[n] = _adam_reduce(parts[n], w[n], m[n], v[n], "adam_" + n, dep)
        return new_v[n]

    pin = _arrived(parts["o_w_out"], "arrived_o_w_out", d_conv_w)
    dz1_sib = _swap_class_columns("swap_dz1", 10, dz1, ODD_IN // NDEV)
    dw = _mm_pair_dw(h1, dz1, h1_sib, dz1_sib, ODD_IN // NDEV, "mm_dwin1", pin)
    parts["o_w_in"] = _sc_chip_scatter("scatter_o_w_in", 3, dw)
    dh1 = _mm_in_dx(dz1, wg_o_in, "mm_dh1", dw)
    dx1, dy0, g["o_pre_norm"], g["e_post_norm"] = _mid_bwd(dx2, dh1, x1, y0, row(p["o_pre_norm"]),
                                                           row(p["e_post_norm"]))
    dw = _mm_out_dw(ycat0, dy0, "mm_dwout0").reshape(NDEV, 256, D)
    parts["e_w_out"], = _sc_exchange("scatter_e_w_out", 4, [dw], True)
    dycat0 = _mm_out_dx(dy0, w_out_e, "mm_dycat0", dw)
    da_in, da_gate, g["e_pool_w"], g["e_pool_scale"] = _pool_bwd(z0, dycat0, pool_w_bf, row(p["e_pool_scale"]))
    late = [n for n in SMALL if n not in ("e_pre_norm", "o_sgu_b")] + ["o_sgu_b"]
    recv_small, = _sc_gather_two_level("gather_small_grads", 6,
                                       [_pack([g[n].reshape(SMALL[n][0]) for n in late], 512)])
    took = _arrived(parts["o_w_in"], "arrived_o_w_in")
    dq, dk, dv, dbg = _attn_bwd(z0, qkv, og, lg, dycat0, tabs, took)
    dz0 = jnp.concatenate([da_in, da_gate, dq, dk, dv, dbg], axis=1)
    took = _arrived(recv_small, "arrived_small_grads", _arrived(parts["e_w_out"], "arrived_e_w_out", dz0))
    nb = EVEN_IN // NDEV
    swapped = [_swap_class_columns("swap_dz0_%d" % half, (9, 14)[half], dz0, nb, half, 2) for half in (0, 1)]
    dw, e_w_in_parts = took, []
    for half in (0, 1):
        dw = _mm_pair_dw(h0, dz0, h0_sib, swapped[half], nb, "mm_dwin0_%d" % half, dw, half, 2)
        e_w_in_parts.append(_sc_chip_scatter("scatter_e_w_in_%d" % half, (5, 15)[half], dw))
    pin = adam("e_w_out", adam("o_w_out", adam("o_w_in", dw)))
    rows = [int(np.prod(SMALL[n][0])) // 128 for n in late]
    summed = dict(zip(late, _sum_unpack(recv_small, rows, "sum_small_grads", pin)))
    dh0 = _mm_in_dx_halves(dz0, wg_e_in, "mm_dh0", summed[late[0]])
    grad_x, g["e_pre_norm"] = _pre0_bwd(dx1, dh0, x, row(p["e_pre_norm"]))
    last, = _sc_exchange("gather_e_pre_norm_grad", 7, [g["e_pre_norm"].reshape(16, 128)], False)

    n = "e_w_in"
    out = _adam_reduce(e_w_in_parts[0], w[n], m[n], v[n], "adam_e_w_in_0", grad_x, 0, 2)
    out = _adam_reduce(e_w_in_parts[1], w[n], m[n], v[n], "adam_e_w_in_1", None, 1, 2, out)
    grads[n], deltas[n], new_m[n], new_v[n] = out
    summed["e_pre_norm"] = _sum_parts(last, "sum_e_pre_norm_grad", out[3])
    names = list(SMALL)
    views = [_small_views(n) for n in names]
    mine = lambda src: [src[n].reshape(vw[0]) for n, vw in zip(names, views)]
    res = _adam_small(mine(w), [summed[n].reshape(vw[1]) for n, vw in zip(names, views)], [vw[2] for vw in views],
                      mine(m), mine(v), "adam_small")
    for n, out in zip(names, res):
        grads[n], deltas[n], new_m[n], new_v[n] = [t.reshape(_shard_shape(n)) for t in out]

    lead = lambda a: a[None]
    return (loss, grad_x[None], *[lead(grads[n]) for n in WEIGHTS], *[lead(deltas[n]) for n in WEIGHTS],
            *[lead(new_m[n]) for n in WEIGHTS], *[lead(new_v[n]) for n in WEIGHTS])
```

```python
import functools

import numpy as np
import jax
import jax.numpy as jnp
from jax import lax
from jax.experimental import pallas as pl
from jax.experimental.pallas import tpu as pltpu
from jax.experimental.pallas import tpu_sc as plsc

F32 = jnp.float32
BF16 = jnp.bfloat16

S = 2048
D = 2048
NDEV = 8
EPS = 1e-6
NEG = -1e30
HEAD_DIM = 128
ROT_DIM = 32
ROPE_THETA = 500000.0
PATTERNS = ((128, 1), (512, 4), (2048, 16))
BLK = 128
EVEN_IN = 12288
ODD_IN = 6144
HALF = 1024
CONV_K = 31
HALO = 32
TR = 256
SUB = 32

ADAM_LR = 0.001
ADAM_B1 = 0.9
ADAM_B2 = 0.999
ADAM_EPS = 1e-08
ADAM_WD = 0.01
ADAM_STEP = 10

VMEM_BIG = 56 * 1024 * 1024
MESH = pl.DeviceIdType.MESH

NN = (((1,), (0,)), ((), ()))
NT = (((1,), (1,)), ((), ()))
TN = (((0,), (0,)), ((), ()))


def _dot(a, b, dn=NN):
    return lax.dot_general(a, b, dn, preferred_element_type=F32)


def _sigmoid(x):
    return 1.0 / (1.0 + jnp.exp(-x))


def _silu_and_grad(x):
    sg = _sigmoid(x)
    return x * sg, sg * (1.0 + x * (1.0 - sg))


def _params(sem, vmem=None):
    return pltpu.CompilerParams(dimension_semantics=sem, vmem_limit_bytes=vmem)


ANY_SPEC = pl.BlockSpec(memory_space=pl.ANY)


def _matmul(a, b, *, dn, grid, a_spec, b_spec, o_spec, out_shape, out_dtype, acc_shape, name, dep=None):
    nk = grid[2]
    deps = [] if dep is None else list(dep) if isinstance(dep, (tuple, list)) else [dep]

    def body(a_ref, b_ref, *rest):
        o_ref, acc = rest[len(deps)], rest[len(deps) + 1:]
        if nk == 1:
            o_ref[...] = _dot(a_ref[...], b_ref[...], dn).astype(o_ref.dtype)
            return
        acc_ref = acc[0]
        k = pl.program_id(2)

        @pl.when(k == 0)
        def _():
            acc_ref[...] = jnp.zeros_like(acc_ref)

        acc_ref[...] += _dot(a_ref[...], b_ref[...], dn)

        @pl.when(k == nk - 1)
        def _():
            o_ref[...] = acc_ref[...].astype(o_ref.dtype)

    return pl.pallas_call(
        body, grid=grid, in_specs=[a_spec, b_spec] + [ANY_SPEC] * len(deps), out_specs=o_spec,
        out_shape=jax.ShapeDtypeStruct(out_shape, out_dtype),
        scratch_shapes=[] if nk == 1 else [pltpu.VMEM(acc_shape, F32)],
        compiler_params=_params(("parallel", "parallel", "arbitrary"), VMEM_BIG), name=name,
    )(a, b, *deps)


TM = 2048


def _mm_in(h, wg, name):
    nb = wg.shape[2]
    tn = 512 if nb % 512 == 0 else nb
    per = nb // tn
    return _matmul(
        h, wg, dn=NN, grid=(S // TM, NDEV * per, 1),
        a_spec=pl.BlockSpec((TM, D), lambda i, j, k: (i, 0)),
        b_spec=pl.BlockSpec((None, D, tn), lambda i, j, k: (j // per, 0, j % per)),
        o_spec=pl.BlockSpec((TM, tn), lambda i, j, k: (i, j)),
        out_shape=(S, NDEV * nb), out_dtype=F32, acc_shape=(TM, tn), name=name)


def _mm_in_halves(h, wg_halves, name):
    hb = wg_halves[0].shape[2]
    z = None
    for half, wg in enumerate(wg_halves):
        prev = [] if z is None else [z]

        def body(a_ref, b_ref, *rest):
            rest[-1][...] = _dot(a_ref[...], b_ref[...])

        z = pl.pallas_call(
            body, grid=(NDEV,),
            in_specs=[pl.BlockSpec((S, D), lambda j: (0, 0)), pl.BlockSpec((None, D, hb), lambda j: (j, 0, 0))]
                     + [ANY_SPEC] * len(prev),
            out_specs=pl.BlockSpec((S, hb), lambda j, half=half: (0, 2 * j + half)),
            out_shape=jax.ShapeDtypeStruct((S, 2 * NDEV * hb), F32),
            input_output_aliases={2: 0} if prev else {},
            compiler_params=_params(("parallel",), VMEM_BIG), name="%s_%d" % (name, half),
        )(h, wg, *prev)
    return z


def _mm_in_dx_halves(dz, wg_halves, name, dep):
    hb = wg_halves[0].shape[2]
    nk = 2 * NDEV

    def body(a_ref, b0_ref, b1_ref, dep_ref, o_ref, acc_ref):
        k = pl.program_id(2)

        @pl.when(k == 0)
        def _():
            acc_ref[...] = jnp.zeros_like(acc_ref)

        @pl.when(k % 2 == 0)
        def _():
            acc_ref[...] += _dot(a_ref[...], b0_ref[...], NT)

        @pl.when(k % 2 == 1)
        def _():
            acc_ref[...] += _dot(a_ref[...], b1_ref[...], NT)

        @pl.when(k == nk - 1)
        def _():
            o_ref[...] = acc_ref[...]

    b_spec = pl.BlockSpec((None, 1024, hb), lambda i, j, k: (k // 2, j, 0))
    return pl.pallas_call(
        body, grid=(1, D // 1024, nk),
        in_specs=[pl.BlockSpec((S, hb), lambda i, j, k: (0, k)), b_spec, b_spec, ANY_SPEC],
        out_specs=pl.BlockSpec((S, 1024), lambda i, j, k: (0, j)), out_shape=jax.ShapeDtypeStruct((S, D), F32),
        scratch_shapes=[pltpu.VMEM((S, 1024), F32)],
        compiler_params=_params(("parallel", "parallel", "arbitrary"), VMEM_BIG), name=name,
    )(dz, *wg_halves, dep)


def _mm_in_dx(dz, wg, name, dep=None):
    nb = wg.shape[2]
    return _matmul(
        dz, wg, dn=NT, grid=(S // TM, D // 1024, NDEV),
        a_spec=pl.BlockSpec((TM, nb), lambda i, j, k: (i, k)),
        b_spec=pl.BlockSpec((None, 1024, nb), lambda i, j, k: (k, j, 0)),
        o_spec=pl.BlockSpec((TM, 1024), lambda i, j, k: (i, j)),
        out_shape=(S, D), out_dtype=F32, acc_shape=(TM, 1024), name=name, dep=dep)


def _mm_out_dx(dy, w, name, dep=None):
    return _matmul(
        dy, w, dn=NT, grid=(S // TM, 2048 // 512, 1),
        a_spec=pl.BlockSpec((TM, D), lambda i, j, k: (i, 0)),
        b_spec=pl.BlockSpec((512, D), lambda i, j, k: (j, 0)),
        o_spec=pl.BlockSpec((TM, 512), lambda i, j, k: (i, j)),
        out_shape=(S, 2048), out_dtype=F32, acc_shape=(TM, 512), name=name, dep=dep)


def _mm_out_dw(yc, dy, name):
    return _matmul(
        yc, dy, dn=TN, grid=(2048 // TM, D // 512, 1),
        a_spec=pl.BlockSpec((S, TM), lambda i, j, k: (0, i)),
        b_spec=pl.BlockSpec((S, 512), lambda i, j, k: (0, j)),
        o_spec=pl.BlockSpec((TM, 512), lambda i, j, k: (i, j)),
        out_shape=(2048, D), out_dtype=BF16, acc_shape=(TM, 512), name=name)


def _row_spec(w=D):
    return pl.BlockSpec((TR, w), lambda i: (i, 0))


def _vec_spec(w=D):
    return pl.BlockSpec((1, w), lambda i: (0, 0))


def _rms_stats(x):
    r = lax.rsqrt(jnp.mean(x * x, axis=-1, keepdims=True) + EPS)
    return x * r, r


def _rms_bwd(dn, xhat, r, g):
    dxh = dn * g
    return r * (dxh - xhat * jnp.mean(dxh * xhat, axis=-1, keepdims=True))


def _acc_rows(ref, val, i):
    s = jnp.sum(val, axis=0, keepdims=True)

    @pl.when(i == 0)
    def _():
        ref[...] = s

    @pl.when(i > 0)
    def _():
        ref[...] += s


def _pre0_fwd(x, g, dep=None):
    deps = [] if dep is None else [dep]

    def body(x_ref, g_ref, *rest):
        xhat, _ = _rms_stats(x_ref[...])
        rest[-1][...] = (xhat * g_ref[...]).astype(BF16)

    return pl.pallas_call(
        body, grid=(S // TR,), in_specs=[_row_spec(), _vec_spec()] + [ANY_SPEC] * len(deps), out_specs=_row_spec(),
        out_shape=jax.ShapeDtypeStruct((S, D), BF16), compiler_params=_params(("parallel",)), name="pre0_fwd",
    )(x, g, *deps)


def _post0_fwd(ycat, w_out, x, g_post, g_pre1, dep):
    def body(yc_ref, w_ref, x_ref, gp_ref, g1_ref, dep_ref, y_ref, x1_ref, h1_ref):
        y = _dot(yc_ref[...], w_ref[...])
        y_ref[...] = y
        yhat, _ = _rms_stats(y)
        x1 = x_ref[...] + yhat * gp_ref[...]
        x1_ref[...] = x1
        xhat, _ = _rms_stats(x1)
        h1_ref[...] = (xhat * g1_ref[...]).astype(BF16)

    return pl.pallas_call(
        body, grid=(S // TR,),
        in_specs=[_row_spec(), pl.BlockSpec((2048, D), lambda i: (0, 0)), _row_spec(), _vec_spec(), _vec_spec(),
                  ANY_SPEC],
        out_specs=[_row_spec(), _row_spec(), _row_spec()],
        out_shape=[jax.ShapeDtypeStruct((S, D), F32), jax.ShapeDtypeStruct((S, D), F32),
                   jax.ShapeDtypeStruct((S, D), BF16)],
        compiler_params=_params(("parallel",), VMEM_BIG), name="post0_fwd",
    )(ycat, w_out, x, g_post, g_pre1, dep)


def _post1_bwd(ycat, w_out, x1, target, g_post, dep):
    def body(yc_ref, w_ref, x1_ref, t_ref, g_ref, dep_ref, loss_ref, dx2_ref, dy_ref, dg_ref):
        i = pl.program_id(0)
        yhat, r = _rms_stats(_dot(yc_ref[...], w_ref[...]))
        g = g_ref[...]
        err = x1_ref[...] + yhat * g - t_ref[...]
        part = jnp.sum(jnp.sum(err * err, axis=-1, keepdims=True), axis=0, keepdims=True) * (0.5 / D)
        _acc_rows(loss_ref, jnp.broadcast_to(part, (1, 128)), i)
        dx2 = err * (1.0 / D)
        dx2_ref[...] = dx2
        _acc_rows(dg_ref, dx2 * yhat, i)
        dy_ref[...] = _rms_bwd(dx2, yhat, r, g).astype(BF16)

    return pl.pallas_call(
        body, grid=(S // TR,),
        in_specs=[_row_spec(), pl.BlockSpec((2048, D), lambda i: (0, 0)), _row_spec(), _row_spec(), _vec_spec(),
                  ANY_SPEC],
        out_specs=[_vec_spec(128), _row_spec(), _row_spec(), _vec_spec()],
        out_shape=[jax.ShapeDtypeStruct((1, 128), F32), jax.ShapeDtypeStruct((S, D), F32),
                   jax.ShapeDtypeStruct((S, D), BF16), jax.ShapeDtypeStruct((1, D), F32)],
        compiler_params=_params(("arbitrary",), VMEM_BIG), name="post1_bwd",
    )(ycat, w_out, x1, target, g_post, dep)


def _mid_bwd(dx2, dh1, x1, y0, g_pre1, g_post0):
    def body(dx2_ref, dh_ref, x1_ref, y_ref, g1_ref, gp_ref, dx1_ref, dy_ref, dg1_ref, dgp_ref):
        i = pl.program_id(0)
        xhat, r1 = _rms_stats(x1_ref[...])
        dh = dh_ref[...]
        _acc_rows(dg1_ref, dh * xhat, i)
        dx1 = dx2_ref[...] + _rms_bwd(dh, xhat, r1, g1_ref[...])
        dx1_ref[...] = dx1
        yhat, r0 = _rms_stats(y_ref[...])
        _acc_rows(dgp_ref, dx1 * yhat, i)
        dy_ref[...] = _rms_bwd(dx1, yhat, r0, gp_ref[...]).astype(BF16)

    return pl.pallas_call(
        body, grid=(S // TR,),
        in_specs=[_row_spec(), _row_spec(), _row_spec(), _row_spec(), _vec_spec(), _vec_spec()],
        out_specs=[_row_spec(), _row_spec(), _vec_spec(), _vec_spec()],
        out_shape=[jax.ShapeDtypeStruct((S, D), F32), jax.ShapeDtypeStruct((S, D), BF16),
                   jax.ShapeDtypeStruct((1, D), F32), jax.ShapeDtypeStruct((1, D), F32)],
        compiler_params=_params(("arbitrary",)), name="mid_bwd",
    )(dx2, dh1, x1, y0, g_pre1, g_post0)


def _pre0_bwd(dx1, dh0, x, g):
    def body(dx1_ref, dh_ref, x_ref, g_ref, gx_ref, dg_ref):
        i = pl.program_id(0)
        xhat, r = _rms_stats(x_ref[...])
        dh = dh_ref[...]
        _acc_rows(dg_ref, dh * xhat, i)
        gx_ref[...] = dx1_ref[...] + _rms_bwd(dh, xhat, r, g_ref[...])

    return pl.pallas_call(
        body, grid=(S // TR,), in_specs=[_row_spec(), _row_spec(), _row_spec(), _vec_spec()],
        out_specs=[_row_spec(), _vec_spec()],
        out_shape=[jax.ShapeDtypeStruct((S, D), F32), jax.ShapeDtypeStruct((1, D), F32)],
        compiler_params=_params(("arbitrary",)), name="pre0_bwd",
    )(dx1, dh0, x, g)


POOL_CH = 256


def _pool_apply(a, w, transpose):
    n = a.shape[0]
    row = lax.broadcasted_iota(jnp.int32, a.shape, 0)
    cnt = jnp.minimum(row + 1, w).astype(F32)
    s = a / cnt if transpose else a
    for k in (1, 2, 4, 8):
        if transpose:
            sh = jnp.where(row < n - k, pltpu.roll(s, n - k, 0), 0.0)
        else:
            sh = jnp.where(row >= k, pltpu.roll(s, k, 0), 0.0)
        s = jnp.where(w > k, s + sh, s)
    return s - a if transpose else s / cnt - a


def _pool_fwd(z0, pool_w, pool_scale):
    def body(a_ref, gate_ref, w_ref, sc_ref, out_ref):
        win = jnp.left_shift(2, pl.program_id(0))
        pooled = _pool_apply(a_ref[...], win, False)
        mixed = _dot(pooled.astype(BF16), w_ref[...])
        gate = gate_ref[...]
        out_ref[...] = (mixed * sc_ref[...] * (gate * _sigmoid(gate))).astype(BF16)

    return pl.pallas_call(
        body, grid=(4,),
        in_specs=[pl.BlockSpec((S, POOL_CH), lambda g: (0, g)), pl.BlockSpec((S, POOL_CH), lambda g: (0, 4 + g)),
                  pl.BlockSpec((None, POOL_CH, POOL_CH), lambda g: (g, 0, 0)),
                  pl.BlockSpec((1, POOL_CH), lambda g: (0, g))],
        out_specs=pl.BlockSpec((S, POOL_CH), lambda g: (0, g)),
        out_shape=jax.ShapeDtypeStruct((S, 2048), BF16),
        compiler_params=_params(("parallel",), VMEM_BIG), name="pool_fwd",
    )(z0, z0, pool_w, pool_scale)


def _pool_bwd(z0, dycat, pool_w, pool_scale):
    def body(a_ref, gate_ref, dy_ref, w_ref, sc_ref, da_ref, dgate_ref, dw_ref, dsc_ref):
        win = jnp.left_shift(2, pl.program_id(0))
        pooled = _pool_apply(a_ref[...], win, False).astype(BF16)
        w = w_ref[...]
        mixed = _dot(pooled, w)
        silu, dsilu = _silu_and_grad(gate_ref[...])
        dy = dy_ref[...]
        sc = sc_ref[...]
        dgate_ref[...] = (dy * (mixed * sc) * dsilu).astype(BF16)
        dms = dy * silu
        dsc_ref[...] = jnp.sum(dms * mixed, axis=0, keepdims=True)
        dmixed = (dms * sc).astype(BF16)
        dw_ref[...] = _dot(pooled, dmixed, TN)
        dpooled = _dot(dmixed, w, NT)
        da_ref[...] = _pool_apply(dpooled, win, True).astype(BF16)

    slab = lambda off: pl.BlockSpec((S, POOL_CH), lambda g: (0, off + g))
    return pl.pallas_call(
        body, grid=(4,),
        in_specs=[slab(0), slab(4), slab(0), pl.BlockSpec((None, POOL_CH, POOL_CH), lambda g: (g, 0, 0)),
                  pl.BlockSpec((1, POOL_CH), lambda g: (0, g))],
        out_specs=[slab(0), slab(0), pl.BlockSpec((None, POOL_CH, POOL_CH), lambda g: (g, 0, 0)),
                   pl.BlockSpec((1, POOL_CH), lambda g: (0, g))],
        out_shape=[jax.ShapeDtypeStruct((S, HALF), BF16), jax.ShapeDtypeStruct((S, HALF), BF16),
                   jax.ShapeDtypeStruct((4, POOL_CH, POOL_CH), F32), jax.ShapeDtypeStruct((1, HALF), F32)],
        compiler_params=_params(("parallel",), VMEM_BIG), name="pool_bwd",
    )(z0, z0, dycat, pool_w, pool_scale)


Q_COL, K_COL, V_COL, BG_COL = 2048 // 128, 5120 // 128, 8192 // 128, 11264 // 128
SCALE = HEAD_DIM ** -0.5


def _rope_tables():
    pos = jnp.arange(S, dtype=F32)
    inv_freq = jnp.power(ROPE_THETA, -jnp.arange(0, ROT_DIM, 2, dtype=F32) / ROT_DIM)
    ang = pos[:, None] * inv_freq[None, :]
    cos, sin = jnp.cos(ang), jnp.sin(ang)
    half = ROT_DIM // 2
    zeros = jnp.zeros((S, HEAD_DIM - ROT_DIM), F32)
    c = jnp.concatenate([cos, cos, jnp.ones((S, HEAD_DIM - ROT_DIM), F32)], axis=1)
    a = jnp.concatenate([-sin, jnp.zeros((S, half), F32), zeros], axis=1)
    b = jnp.concatenate([jnp.zeros((S, half), F32), sin, zeros], axis=1)
    return c, a, b


def _rope(t, c, a, b):
    half = ROT_DIM // 2
    return t * c + pltpu.roll(t, HEAD_DIM - half, 1) * a + pltpu.roll(t, half, 1) * b


def _rope_t(d, c, a, b):
    half = ROT_DIM // 2
    return d * c + pltpu.roll(d * a, half, 1) + pltpu.roll(d * b, HEAD_DIM - half, 1)


def _deinterleave(dst, src, dil, cast=None, dst_off=0):
    length = S // dil
    for r in range(dil):
        v = src[...] if dil == 1 else src[pl.ds(r, length, stride=dil), :]
        dst[dst_off + r * length:dst_off + (r + 1) * length, :] = v if cast is None else v.astype(cast)


def _interleave(dst, src, dil, src_off=0):
    length = S // dil
    for r in range(dil):
        if dil == 1:
            dst[...] = src[src_off:src_off + S, :]
        else:
            dst[pl.ds(r, length, stride=dil), :] = src[src_off + r * length:src_off + (r + 1) * length, :]


CU = 8
NUNITS = S // BLK
B_QK = (((2,), (2,)), ((0,), (0,)))
B_PV = (((2,), (1,)), ((0,), (0,)))
B_TN = (((1,), (1,)), ((0,), (0,)))


def _blocks(ref, first):
    return ref[first * BLK:(first + CU) * BLK, :].reshape(CU, BLK, HEAD_DIM)


def _chunk_scores(u0, nb, qd, kdp):
    q = _blocks(qd, u0)
    row = lax.broadcasted_iota(jnp.int32, (CU, BLK, BLK), 1)
    col = lax.broadcasted_iota(jnp.int32, (CU, BLK, BLK), 2)
    s_own = jnp.where(col <= row, _dot(q, _blocks(kdp, u0 + 1), B_QK) * SCALE, NEG)
    if nb == 1:
        return q, s_own, None
    unit = lax.broadcasted_iota(jnp.int32, (CU, BLK, BLK), 0) + u0
    s_prev = jnp.where((col >= row) & ((unit % nb) != 0), _dot(q, _blocks(kdp, u0), B_QK) * SCALE, NEG)
    return q, s_own, s_prev


def _qkv_prep(z0, tabs):
    def body(q_ref, k_ref, v_ref, c_ref, a_ref, b_ref, qo_ref, ko_ref, vo_ref, tmp):
        p = pl.program_id(1)
        for gi, (_, dil) in enumerate(PATTERNS):
            @pl.when(p == gi)
            def _(dil=dil):
                c, a, b = c_ref[...], a_ref[...], b_ref[...]
                tmp[...] = _rope(q_ref[...], c, a, b)
                _deinterleave(qo_ref, tmp, dil, BF16)
                tmp[...] = _rope(k_ref[...], c, a, b)
                _deinterleave(ko_ref, tmp, dil, BF16)
                _deinterleave(vo_ref, v_ref, dil, BF16)

    tab = pl.BlockSpec((S, HEAD_DIM), lambda h, p: (0, 0))
    out = pl.BlockSpec((S, HEAD_DIM), lambda h, p: (0, p * 8 + h))
    return pl.pallas_call(
        body, grid=(8, 3), in_specs=[_head_spec(Q_COL), _head_spec(K_COL), _head_spec(V_COL), tab, tab, tab],
        out_specs=[out, out, out], out_shape=[jax.ShapeDtypeStruct((S, 3072), BF16)] * 3,
        scratch_shapes=[pltpu.VMEM((S, HEAD_DIM), F32)],
        compiler_params=_params(("parallel", "arbitrary"), VMEM_BIG), name="qkv_prep",
    )(z0, z0, z0, *tabs)


def _pad_copy(dst, src):
    dst[0:BLK, :] = jnp.zeros((BLK, HEAD_DIM), dst.dtype)
    dst[BLK:BLK + S, :] = src[...]


def _attn_group_fwd(dil, qd, kd_ref, vd_ref, kdp, vdp, od, ld, og, lg):
    nb = S // dil // BLK
    _pad_copy(kdp, kd_ref)
    _pad_copy(vdp, vd_ref)
    for u0 in range(0, NUNITS, CU):
        _, s_own, s_prev = _chunk_scores(u0, nb, qd, kdp)
        m = jnp.max(s_own, axis=2, keepdims=True)
        if s_prev is not None:
            m = jnp.maximum(m, jnp.max(s_prev, axis=2, keepdims=True))
        p_own = jnp.exp(s_own - m)
        den = jnp.sum(p_own, axis=2, keepdims=True)
        acc = _dot(p_own.astype(BF16), _blocks(vdp, u0 + 1), B_PV)
        if s_prev is not None:
            p_prev = jnp.exp(s_prev - m)
            den = den + jnp.sum(p_prev, axis=2, keepdims=True)
            acc = acc + _dot(p_prev.astype(BF16), _blocks(vdp, u0), B_PV)
        rows = slice(u0 * BLK, (u0 + CU) * BLK)
        od[rows, :] = (acc / den).reshape(CU * BLK, HEAD_DIM)
        ld[rows, :] = jnp.broadcast_to(m + jnp.log(den), (CU, BLK, HEAD_DIM)).reshape(CU * BLK, HEAD_DIM)
    _interleave(og, od, dil)
    _interleave(lg, ld, dil)


def _group_weights(lgs):
    l0, l1, l2 = lgs[0][...], lgs[1][...], lgs[2][...]
    mx = jnp.maximum(l0, jnp.maximum(l1, l2))
    e0, e1, e2 = jnp.exp(l0 - mx), jnp.exp(l1 - mx), jnp.exp(l2 - mx)
    den = e0 + e1 + e2
    return e0 / den, e1 / den, e2 / den


def _head_spec(base, ngroups_axis=True):
    return pl.BlockSpec((S, HEAD_DIM), lambda h, p: (0, base + (p % 3) * 8 + h))


def _slab(dtype=F32, rows=S):
    return pltpu.VMEM((rows, HEAD_DIM), dtype)


def _attn_fwd(z0, qkv, ycat):
    def body(q_ref, k_ref, v_ref, gate_ref, ycat_ref, out_ref, og_ref, lg_ref,
             kdp, vdp, od, ld, og0, og1, og2, lg0, lg1, lg2):
        del ycat_ref
        p = pl.program_id(1)
        ogs, lgs = (og0, og1, og2), (lg0, lg1, lg2)
        for gi, (_, dil) in enumerate(PATTERNS):
            @pl.when(p == gi)
            def _(gi=gi, dil=dil):
                _attn_group_fwd(dil, q_ref, k_ref, v_ref, kdp, vdp, od, ld, ogs[gi], lgs[gi])
                og_ref[...] = ogs[gi][...]
                lg_ref[...] = lgs[gi][...]

        @pl.when(p == 2)
        def _():
            w0, w1, w2 = _group_weights(lgs)
            o = w0 * og0[...] + w1 * og1[...] + w2 * og2[...]
            gate = gate_ref[...]
            out_ref[...] = (o * (gate * _sigmoid(gate))).astype(BF16)

    grp = pl.BlockSpec((S, HEAD_DIM), lambda h, p: (0, p * 8 + h))
    return pl.pallas_call(
        body, grid=(8, 3),
        in_specs=[grp, grp, grp, pl.BlockSpec((S, HEAD_DIM), lambda h, p: (0, BG_COL + h)), ANY_SPEC],
        out_specs=[pl.BlockSpec((S, HEAD_DIM), lambda h, p: (0, 8 + h)), grp, grp],
        out_shape=[jax.ShapeDtypeStruct((S, 2048), BF16), jax.ShapeDtypeStruct((S, 3072), F32),
                   jax.ShapeDtypeStruct((S, 3072), F32)],
        scratch_shapes=[_slab(BF16, S + BLK), _slab(BF16, S + BLK)] + [_slab() for _ in range(8)],
        input_output_aliases={4: 0},
        compiler_params=_params(("parallel", "arbitrary"), VMEM_BIG), name="attn_fwd",
    )(*qkv, z0, ycat)


def _attn_bwd(z0, qkv, og, lg, dycat, tabs, dep):
    def body(q_ref, k_ref, v_ref, gate_ref, dy_ref, c_ref, a_ref, b_ref,
             og0_ref, og1_ref, og2_ref, lg0_ref, lg1_ref, lg2_ref, dep_ref,
             dq_ref, dk_ref, dv_ref, dbg_ref,
             tmp, kd, vd, ld, dg0, dg1, dg2, cg0, cg1, cg2, dod, cd, dqd, dkd, dvd):
        p = pl.program_id(1)
        ogs, lgs, dgs, cgs = (og0_ref, og1_ref, og2_ref), (lg0_ref, lg1_ref, lg2_ref), (dg0, dg1, dg2), (cg0, cg1, cg2)

        @pl.when(p == 0)
        def _():
            w = _group_weights(lgs)
            o = w[0] * ogs[0][...] + w[1] * ogs[1][...] + w[2] * ogs[2][...]
            silu, dsilu = _silu_and_grad(gate_ref[...])
            dy = dy_ref[...]
            dbg_ref[...] = (dy * o * dsilu).astype(BF16)
            do = dy * silu
            dwbar = jnp.sum(do * o, axis=1, keepdims=True)
            for gi in range(3):
                dgs[gi][...] = w[gi] * do
                cgs[gi][...] = -w[gi] * dwbar

        for gi, (_, dil) in enumerate(PATTERNS):
            @pl.when(p == 1 + gi)
            def _(gi=gi, dil=dil):
                nb = S // dil // BLK
                qd = q_ref
                c, a, b = c_ref[...], a_ref[...], b_ref[...]
                _pad_copy(kd, k_ref)
                _pad_copy(vd, v_ref)
                _deinterleave(dod, dgs[gi], dil, BF16)
                _deinterleave(ld, lgs[gi], dil)
                _deinterleave(cd, cgs[gi], dil)
                dkd[...] = jnp.zeros_like(dkd)
                dvd[...] = jnp.zeros_like(dvd)
                flat = lambda t: t.reshape(CU * BLK, HEAD_DIM)
                for u0 in range(0, NUNITS, CU):
                    q, s_own, s_prev = _chunk_scores(u0, nb, qd, kd)
                    lse, cv, do = _blocks(ld, u0), _blocks(cd, u0), _blocks(dod, u0)
                    own = slice((u0 + 1) * BLK, (u0 + 1 + CU) * BLK)
                    p_own = jnp.exp(s_own - lse)
                    ds_own = (p_own * (_dot(do, _blocks(vd, u0 + 1), B_QK) + cv) * SCALE).astype(BF16)
                    dq = _dot(ds_own, _blocks(kd, u0 + 1), B_PV)
                    dkd[own, :] += flat(_dot(ds_own, q, B_TN))
                    dvd[own, :] += flat(_dot(p_own.astype(BF16), do, B_TN))
                    if s_prev is not None:
                        prev = slice(u0 * BLK, (u0 + CU) * BLK)
                        p_prev = jnp.exp(s_prev - lse)
                        ds_prev = (p_prev * (_dot(do, _blocks(vd, u0), B_QK) + cv) * SCALE).astype(BF16)
                        dq = dq + _dot(ds_prev, _blocks(kd, u0), B_PV)
                        dkd[prev, :] += flat(_dot(ds_prev, q, B_TN))
                        dvd[prev, :] += flat(_dot(p_prev.astype(BF16), do, B_TN))
                    dqd[u0 * BLK:(u0 + CU) * BLK, :] = flat(dq)
                _interleave(tmp, dqd, dil)
                dq_ref[...] = _rope_t(tmp[...], c, a, b).astype(BF16)
                _interleave(tmp, dkd, dil, BLK)
                dk_ref[...] = _rope_t(tmp[...], c, a, b).astype(BF16)
                _interleave(tmp, dvd, dil, BLK)
                dv_ref[...] = tmp[...].astype(BF16)

    tab = pl.BlockSpec((S, HEAD_DIM), lambda h, p: (0, 0))
    hspec = lambda base: pl.BlockSpec((S, HEAD_DIM), lambda h, p: (0, base + h))
    gspec = pl.BlockSpec((S, HEAD_DIM), lambda h, p: (0, jnp.maximum(p - 1, 0) * 8 + h))
    return pl.pallas_call(
        body, grid=(8, 4),
        in_specs=[gspec, gspec, gspec, hspec(BG_COL), hspec(8), tab, tab, tab,
                  hspec(0), hspec(8), hspec(16), hspec(0), hspec(8), hspec(16), ANY_SPEC],
        out_specs=[gspec, gspec, gspec, hspec(0)],
        out_shape=[jax.ShapeDtypeStruct((S, 3072), BF16)] * 3 + [jax.ShapeDtypeStruct((S, HALF), BF16)],
        scratch_shapes=[_slab(), _slab(BF16, S + BLK), _slab(BF16, S + BLK), _slab()] + [_slab() for _ in range(6)]
                       + [_slab(BF16), _slab(), _slab(), _slab(F32, S + BLK), _slab(F32, S + BLK)],
        compiler_params=_params(("parallel", "arbitrary"), VMEM_BIG), name="attn_bwd",
    )(*qkv, z0, dycat, *tabs, og, og, og, lg, lg, lg, dep)


SGU_CH = 256
NCHUNK = TR // 128


def _ln_stats(x):
    mu = jnp.mean(x, axis=-1, keepdims=True)
    xc = x - mu
    r = lax.rsqrt(jnp.mean(xc * xc, axis=-1, keepdims=True) + EPS)
    return xc * r, r


def _ln_bwd(dy, xhat, r, g):
    dxh = dy * g
    return r * (dxh - jnp.mean(dxh, axis=-1, keepdims=True) - xhat * jnp.mean(dxh * xhat, axis=-1, keepdims=True))


def _tril_bf16(w):
    row = lax.broadcasted_iota(jnp.int32, w.shape, 0)
    col = lax.broadcasted_iota(jnp.int32, w.shape, 1)
    return jnp.where(row >= col, w, 0.0).astype(BF16)


def _sgu_gate(vn_s, s_s, w_ref, bb_ref):
    for h in range(4):
        wm = _tril_bf16(w_ref[h])
        bias = bb_ref[h]
        for ch in range(NCHUNK):
            rows, cols = slice(ch * 128, (ch + 1) * 128), slice(h * SGU_CH, (h + 1) * SGU_CH)
            s_s[rows, cols] = _dot(wm, vn_s[rows, cols]) + jnp.concatenate([bias, bias], axis=1)


WIN = HALO + TR
SUBL = 8


def _shifted_copies(dst, src):
    dst[0] = src[...]
    for b in range(1, SUBL):
        dst[b, 0:WIN - SUBL, :] = src[pl.ds(b, WIN - SUBL), :]


def _rows_at(copies, off, n):
    return copies[off % SUBL, pl.ds(off - off % SUBL, n), :]


def _conv_fwd(i, dval_ref, dglu_ref, hval_ref, hglu_ref, cw_ref, cb_ref, xw, xr, dcs):
    halo = hval_ref[...] * _sigmoid(hglu_ref[...])
    xw[0:HALO, :] = jnp.where(i > 0, halo, 0.0)
    xw[HALO:HALO + TR, :] = dval_ref[...] * _sigmoid(dglu_ref[...])
    _shifted_copies(xr, xw)
    for rb in range(TR // SUB):
        acc = jnp.broadcast_to(cb_ref[...], (SUB, HALF))
        for k in range(CONV_K):
            acc = acc + cw_ref[k:k + 1, :] * _rows_at(xr, rb * SUB + HALO - (CONV_K - 1) + k, SUB)
        dcs[rb * SUB:(rb + 1) * SUB, :] = acc


def _odd_in_specs():
    col = lambda j: pl.BlockSpec((TR, HALF), lambda i, *_: (i, j))
    prev = lambda j: pl.BlockSpec((HALO, HALF), lambda i, *_: (jnp.maximum(i * (TR // HALO) - 1, 0), j))
    return [col(0), col(1), col(2), col(3), col(4), col(5), prev(3), prev(4)]


def _full_spec(shape):
    return pl.BlockSpec(shape, lambda i, *_: (0,) * len(shape))


def _odd_fwd(z1, sgu_g, sgu_b, sgu_w, sgu_bb, conv_w, conv_b, cn_g, cn_b):
    def body(u_ref, v_ref, cg_ref, dval_ref, dglu_ref, dgate_ref, hval_ref, hglu_ref,
             g_ref, b_ref, w_ref, bb_ref, cw_ref, cb_ref, cng_ref, cnb_ref, out_ref, dcs, vn_s, s_s, xw, xr):
        i = pl.program_id(0)
        vhat, _ = _ln_stats(v_ref[...])
        vn_s[...] = (vhat * g_ref[...] + b_ref[...]).astype(BF16)
        _sgu_gate(vn_s, s_s, w_ref, bb_ref)
        cg = cg_ref[...]
        out_ref[:, 0:HALF] = (u_ref[...] * s_s[...] * (cg * _sigmoid(cg))).astype(BF16)
        _conv_fwd(i, dval_ref, dglu_ref, hval_ref, hglu_ref, cw_ref, cb_ref, xw, xr, dcs)
        dhat, _ = _ln_stats(dcs[...])
        dn = dhat * cng_ref[...] + cnb_ref[...]
        dgate = dgate_ref[...]
        out_ref[:, HALF:2 * HALF] = ((dn * _sigmoid(dn)) * (dgate * _sigmoid(dgate))).astype(BF16)

    vec = _full_spec((1, HALF))
    return pl.pallas_call(
        body, grid=(S // TR,),
        in_specs=_odd_in_specs() + [vec, vec, _full_spec((4, 128, 128)), _full_spec((4, 128, 128)),
                                    _full_spec((HALO, HALF)), vec, vec, vec],
        out_specs=[pl.BlockSpec((TR, 2048), lambda i: (i, 0)), pl.BlockSpec((TR, HALF), lambda i: (i, 0))],
        out_shape=[jax.ShapeDtypeStruct((S, 2048), BF16), jax.ShapeDtypeStruct((S, HALF), F32)],
        scratch_shapes=[pltpu.VMEM((TR, HALF), BF16), pltpu.VMEM((TR, HALF), F32),
                        pltpu.VMEM((WIN, HALF), F32), pltpu.VMEM((SUBL, WIN, HALF), F32)],
        compiler_params=_params(("parallel",), VMEM_BIG), name="odd_fwd",
    )(z1, z1, z1, z1, z1, z1, z1, z1, sgu_g, sgu_b, sgu_w, sgu_bb, conv_w, conv_b, cn_g, cn_b)


def _odd_bwd_a(z1, dc, dycat, sgu_g, sgu_b, sgu_w, sgu_bb, cn_g, cn_b):
    def body(u_ref, v_ref, cg_ref, dgate_ref, dcs, dy_ref, g_ref, b_ref, w_ref, bb_ref, cng_ref, cnb_ref,
             dz_ref, ddc_ref, dw_ref, dbb_ref, dg_ref, db_ref, dcng_ref, dcnb_ref, dcb_ref,
             vn_s, s_s, ds_s, dvn_s):
        i = pl.program_id(0)
        vhat, rv = _ln_stats(v_ref[...])
        g = g_ref[...]
        vn_s[...] = (vhat * g + b_ref[...]).astype(BF16)
        _sgu_gate(vn_s, s_s, w_ref, bb_ref)
        silu_c, dsilu_c = _silu_and_grad(cg_ref[...])
        dyc = dy_ref[:, 0:HALF]
        u = u_ref[...]
        s = s_s[...]
        dz_ref[:, 0:HALF] = (dyc * s * silu_c).astype(BF16)
        dz_ref[:, 2 * HALF:3 * HALF] = (dyc * u * s * dsilu_c).astype(BF16)
        ds_s[...] = dyc * u * silu_c

        @pl.when(i == 0)
        def _():
            dw_ref[...] = jnp.zeros_like(dw_ref)
            dbb_ref[...] = jnp.zeros_like(dbb_ref)

        tril = lax.broadcasted_iota(jnp.int32, (128, 128), 0) >= lax.broadcasted_iota(jnp.int32, (128, 128), 1)
        for h in range(4):
            wm = _tril_bf16(w_ref[h])
            for ch in range(NCHUNK):
                rows, cols = slice(ch * 128, (ch + 1) * 128), slice(h * SGU_CH, (h + 1) * SGU_CH)
                ds = ds_s[rows, cols]
                dsb = ds.astype(BF16)
                dw_ref[h] += jnp.where(tril, _dot(dsb, vn_s[rows, cols], NT), 0.0)
                dbb_ref[h] += jnp.broadcast_to(jnp.sum(ds, axis=1, keepdims=True), (128, 128))
                dvn_s[rows, cols] = _dot(wm, dsb, TN)
        dvn = dvn_s[...]
        _acc_rows(dg_ref, dvn * vhat, i)
        _acc_rows(db_ref, dvn, i)
        dz_ref[:, HALF:2 * HALF] = _ln_bwd(dvn, vhat, rv, g).astype(BF16)

        dhat, rd = _ln_stats(dcs[...])
        cng = cng_ref[...]
        silu_n, dsilu_n = _silu_and_grad(dhat * cng + cnb_ref[...])
        silu_g, dsilu_g = _silu_and_grad(dgate_ref[...])
        dyd = dy_ref[:, HALF:2 * HALF]
        dz_ref[:, 5 * HALF:6 * HALF] = (dyd * silu_n * dsilu_g).astype(BF16)
        ddn = dyd * silu_g * dsilu_n
        _acc_rows(dcng_ref, ddn * dhat, i)
        _acc_rows(dcnb_ref, ddn, i)
        ddc = _ln_bwd(ddn, dhat, rd, cng)
        ddc_ref[...] = ddc
        _acc_rows(dcb_ref, ddc, i)

    vec = _full_spec((1, HALF))
    sq = _full_spec((4, 128, 128))
    col = lambda j: pl.BlockSpec((TR, HALF), lambda i: (i, j))
    return pl.pallas_call(
        body, grid=(S // TR,),
        in_specs=[col(0), col(1), col(2), col(5), col(0), pl.BlockSpec((TR, 2048), lambda i: (i, 0)),
                  vec, vec, sq, sq, vec, vec],
        out_specs=[pl.BlockSpec((TR, ODD_IN), lambda i: (i, 0)), pl.BlockSpec((TR, HALF), lambda i: (i, 0)),
                   sq, sq, vec, vec, vec, vec, vec],
        out_shape=[jax.ShapeDtypeStruct((S, ODD_IN), BF16), jax.ShapeDtypeStruct((S, HALF), F32),
                   jax.ShapeDtypeStruct((4, 128, 128), F32), jax.ShapeDtypeStruct((4, 128, 128), F32)]
                  + [jax.ShapeDtypeStruct((1, HALF), F32)] * 5,
        scratch_shapes=[pltpu.VMEM((TR, HALF), BF16), pltpu.VMEM((TR, HALF), F32),
                        pltpu.VMEM((TR, HALF), F32), pltpu.VMEM((TR, HALF), F32)],
        compiler_params=_params(("arbitrary",), VMEM_BIG), name="odd_bwd_a",
    )(z1, z1, z1, z1, dc, dycat, sgu_g, sgu_b, sgu_w, sgu_bb, cn_g, cn_b)


def _odd_bwd_b(z1, ddc, dz1, conv_w):
    nt = S // TR

    def body(dval_ref, dglu_ref, hval_ref, hglu_ref, ddc_ref, hddc_ref, cw_ref, dz_in_ref,
             dz_ref, dcw_ref, xw, dwin, dxs, xr, dr):
        del dz_in_ref
        i, j = pl.program_id(0), pl.program_id(1)
        sg = _sigmoid(dglu_ref[...])
        dval = dval_ref[...]

        @pl.when(j == 0)
        def _():
            halo = hval_ref[...] * _sigmoid(hglu_ref[...])
            xw[0:HALO, :] = jnp.where(i > 0, halo, 0.0)
            xw[HALO:HALO + TR, :] = dval * sg
            dwin[0:TR, :] = ddc_ref[...]
            dwin[TR:TR + HALO, :] = jnp.where(i < nt - 1, hddc_ref[...], 0.0)
            _shifted_copies(xr, xw)
            _shifted_copies(dr, dwin)

            @pl.when(i == 0)
            def _():
                dcw_ref[...] = jnp.zeros_like(dcw_ref)

            for rb in range(TR // SUB):
                acc = jnp.zeros((SUB, HALF), F32)
                for k in range(CONV_K):
                    acc = acc + cw_ref[k:k + 1, :] * _rows_at(dr, rb * SUB + (CONV_K - 1) - k, SUB)
                dxs[rb * SUB:(rb + 1) * SUB, :] = acc
            for k in range(CONV_K):
                acc = jnp.zeros((SUB, HALF), F32)
                for rb in range(TR // SUB):
                    acc = acc + dwin[rb * SUB:(rb + 1) * SUB, :] * _rows_at(xr, rb * SUB + HALO - (CONV_K - 1) + k, SUB)
                dcw_ref[k:k + 1, :] += jnp.sum(acc, axis=0, keepdims=True)
            dz_ref[...] = (dxs[...] * sg).astype(BF16)

        @pl.when(j == 1)
        def _():
            dz_ref[...] = (dxs[...] * dval * sg * (1.0 - sg)).astype(BF16)

    col = lambda c: pl.BlockSpec((TR, HALF), lambda i, j: (i, c))
    prev = lambda c: pl.BlockSpec((HALO, HALF), lambda i, j: (jnp.maximum(i * (TR // HALO) - 1, 0), c))
    nxt = pl.BlockSpec((HALO, HALF), lambda i, j: (jnp.minimum((i + 1) * (TR // HALO), S // HALO - 1), 0))
    return pl.pallas_call(
        body, grid=(nt, 2),
        in_specs=[col(3), col(4), prev(3), prev(4), pl.BlockSpec((TR, HALF), lambda i, j: (i, 0)), nxt,
                  _full_spec((HALO, HALF)), pl.BlockSpec(memory_space=pl.ANY)],
        out_specs=[pl.BlockSpec((TR, HALF), lambda i, j: (i, 3 + j)), _full_spec((HALO, HALF))],
        out_shape=[jax.ShapeDtypeStruct((S, ODD_IN), BF16), jax.ShapeDtypeStruct((HALO, HALF), F32)],
        scratch_shapes=[pltpu.VMEM((WIN, HALF), F32), pltpu.VMEM((WIN, HALF), F32), pltpu.VMEM((TR, HALF), F32),
                        pltpu.VMEM((SUBL, WIN, HALF), F32), pltpu.VMEM((SUBL, WIN, HALF), F32)],
        input_output_aliases={7: 0},
        compiler_params=_params(("arbitrary", "arbitrary"), VMEM_BIG), name="odd_bwd_b",
    )(z1, z1, z1, z1, ddc, ddc, conv_w, dz1)


def _cast_bf16(w, name, piece=0, npieces=1):
    r, c = w.shape[0], w.shape[1] // npieces
    tr = min(r, 256)

    def body(i_ref, o_ref):
        o_ref[...] = i_ref[...].astype(BF16)

    return pl.pallas_call(
        body, grid=(r // tr,), in_specs=[pl.BlockSpec((tr, c), lambda i: (i, piece))],
        out_specs=pl.BlockSpec((tr, c), lambda i: (i, 0)), out_shape=jax.ShapeDtypeStruct((r, c), BF16),
        compiler_params=_params(("parallel",)), name=name,
    )(w)


def _adamw(w, g, m, v):
    m = ADAM_B1 * m + (1.0 - ADAM_B1) * g
    v = ADAM_B2 * v + (1.0 - ADAM_B2) * (g * g)
    m_hat = m / (1.0 - ADAM_B1 ** ADAM_STEP)
    v_hat = v / (1.0 - ADAM_B2 ** ADAM_STEP)
    delta = -ADAM_LR * (m_hat / (jnp.sqrt(v_hat) + ADAM_EPS) + ADAM_WD * w)
    return delta, m, v


def _adam_reduce(parts, w, m, v, name, dep=None, piece=0, npieces=1, prev=None):
    r, c = w.shape
    cp = c // npieces
    tr = min(r, 128)
    extra = ([] if dep is None else [dep]) + ([] if prev is None else list(prev))
    nparts = parts.shape[0]

    def body(p_ref, w_ref, m_ref, v_ref, *rest):
        g_ref, d_ref, nm_ref, nv_ref = rest[len(extra):]
        g = p_ref[0].astype(F32)
        for d in range(1, nparts):
            g = g + p_ref[d].astype(F32)
        g_ref[...] = g
        d_ref[...], nm_ref[...], nv_ref[...] = _adamw(w_ref[...], g, m_ref[...], v_ref[...])

    spec = pl.BlockSpec((tr, cp), lambda i: (i, piece))
    first = 4 + (0 if dep is None else 1)
    return pl.pallas_call(
        body, grid=(r // tr,),
        in_specs=[pl.BlockSpec((nparts, tr, cp), lambda i: (0, i, 0)), spec, spec, spec] + [ANY_SPEC] * len(extra),
        out_specs=[spec] * 4, out_shape=[jax.ShapeDtypeStruct((r, c), F32)] * 4,
        input_output_aliases={} if prev is None else {first + k: k for k in range(4)},
        compiler_params=_params(("parallel",), VMEM_BIG), name=name,
    )(parts, w, m, v, *extra)


def _arrived(x, name, dep=None):
    deps = [] if dep is None else [dep]

    def body(*refs):
        refs[-1][...] = jnp.zeros_like(refs[-1])

    return pl.pallas_call(
        body, in_specs=[ANY_SPEC] * (1 + len(deps)), out_specs=pl.BlockSpec(memory_space=pltpu.VMEM),
        out_shape=jax.ShapeDtypeStruct((8, 128), F32), name=name,
    )(x, *deps)


def _sum_parts(parts, name, dep=None):
    r = parts.shape[1]
    tr = 8
    for cand in (512, 256, 128, 64, 32, 16, 8):
        if r % cand == 0:
            tr = cand
            break
    deps = [] if dep is None else [dep]

    def body(p_ref, *rest):
        g = p_ref[0]
        for d in range(1, NDEV):
            g = g + p_ref[d]
        rest[-1][...] = g

    return pl.pallas_call(
        body, grid=(r // tr,), in_specs=[pl.BlockSpec((NDEV, tr, 128), lambda i: (0, i, 0))] + [ANY_SPEC] * len(deps),
        out_specs=pl.BlockSpec((tr, 128), lambda i: (i, 0)), out_shape=jax.ShapeDtypeStruct((r, 128), F32),
        compiler_params=_params(("parallel",)), name=name,
    )(parts, *deps)


def _sum_unpack(parts, rows, name, dep=None):
    deps = [] if dep is None else [dep]

    def body(p_ref, *outs):
        outs = outs[len(deps):]
        off = 0
        for o_ref, n in zip(outs, rows):
            acc = p_ref[0, off:off + n, :]
            for d in range(1, NDEV):
                acc = acc + p_ref[d, off:off + n, :]
            o_ref[...] = acc
            off += n

    return pl.pallas_call(
        body, grid=(1,), in_specs=[pl.BlockSpec(parts.shape, lambda i: (0, 0, 0))] + [ANY_SPEC] * len(deps),
        out_specs=[pl.BlockSpec((n, 128), lambda i: (0, 0)) for n in rows],
        out_shape=[jax.ShapeDtypeStruct((n, 128), F32) for n in rows],
        compiler_params=_params(("arbitrary",), VMEM_BIG), name=name,
    )(parts, *deps)


def _adam_small(ws, gs, g_specs, ms, vs, name):
    n = len(ws)

    def body(*refs):
        w_r, g_r, m_r, v_r = refs[:n], refs[n:2 * n], refs[2 * n:3 * n], refs[3 * n:4 * n]
        outs = refs[4 * n:]
        for i in range(n):
            g = g_r[i][...]
            outs[4 * i][...] = g
            outs[4 * i + 1][...], outs[4 * i + 2][...], outs[4 * i + 3][...] = _adamw(
                w_r[i][...], g, m_r[i][...], v_r[i][...])

    whole = lambda a: pl.BlockSpec(a.shape, lambda i, nd=a.ndim: (0,) * nd)
    outs = pl.pallas_call(
        body, grid=(1,),
        in_specs=[whole(a) for a in ws] + list(g_specs) + [whole(a) for a in ms] + [whole(a) for a in vs],
        out_specs=[whole(a) for a in ws for _ in range(4)],
        out_shape=[jax.ShapeDtypeStruct(a.shape, F32) for a in ws for _ in range(4)],
        compiler_params=_params(("arbitrary",), VMEM_BIG), name=name,
    )(*ws, *gs, *ms, *vs)
    return [outs[4 * i:4 * i + 4] for i in range(n)]


MASKS = [(mx, my, mc) for mx in (0, 1) for my in (0, 1) for mc in (0, 1)][1:]


def _sc_exchange(name, collective_id, arrays, scatter):
    nt = len(arrays)
    out_type = [jax.ShapeDtypeStruct(a.shape if scatter else (NDEV,) + a.shape, a.dtype) for a in arrays]

    def body(*refs):
        ins, outs = refs[:nt], refs[nt:2 * nt]
        send_sems, recv_sems, local_sems = refs[2 * nt:3 * nt], refs[3 * nt:4 * nt], refs[4 * nt:5 * nt]
        x, y, c = lax.axis_index("x"), lax.axis_index("y"), lax.axis_index("c")
        peers = [(mx + x - 2 * mx * x, my + y - 2 * my * y, mc + c - 2 * mc * c) for mx, my, mc in MASKS]
        barrier = pltpu.get_barrier_semaphore()
        for peer in peers:
            pl.semaphore_signal(barrier, inc=1, device_id=peer, device_id_type=MESH)
        pl.semaphore_wait(barrier, len(peers))
        me = 4 * x + 2 * y + c
        own = []
        for t in range(nt):
            cp = pltpu.make_async_copy(ins[t].at[me] if scatter else ins[t], outs[t].at[me], local_sems[t])
            cp.start()
            own.append(cp)
            for px, py, pc in peers:
                src = ins[t].at[4 * px + 2 * py + pc] if scatter else ins[t]
                pltpu.make_async_remote_copy(src_ref=src, dst_ref=outs[t].at[me], send_sem=send_sems[t],
                                             recv_sem=recv_sems[t], device_id=(px, py, pc), device_id_type=MESH).start()
        for t in range(nt):
            own[t].wait()
            seven = outs[t].at[pl.ds(0, NDEV - 1)]
            drain = pltpu.make_async_remote_copy(src_ref=seven, dst_ref=seven, send_sem=send_sems[t],
                                                 recv_sem=recv_sems[t], device_id=(x, y, c), device_id_type=MESH)
            drain.wait_send()
            drain.wait_recv()

    return pl.kernel(
        body, out_type=out_type, mesh=plsc.ScalarSubcoreMesh(axis_name="sequencer", num_cores=1),
        scratch_types=[pltpu.SemaphoreType.DMA] * (3 * nt),
        compiler_params=pltpu.CompilerParams(collective_id=collective_id), name=name,
    )(*arrays)


def _sc_gather_two_level(name, collective_id, arrays):
    nt = len(arrays)
    out_type = [jax.ShapeDtypeStruct((NDEV,) + a.shape, a.dtype) for a in arrays]

    def body(*refs):
        ins, outs = refs[:nt], refs[nt:2 * nt]
        sems = refs[2 * nt:]
        send_sems, sib_sems, local_sems = sems[:nt], sems[nt:2 * nt], sems[2 * nt:3 * nt]
        ici_sems = [sems[3 * nt + 3 * t:3 * nt + 3 * t + 3] for t in range(nt)]
        x, y, c = lax.axis_index("x"), lax.axis_index("y"), lax.axis_index("c")
        sibling = (x, y, 1 - c)
        chips = [(1 - x, y), (x, 1 - y), (1 - x, 1 - y)]
        barrier = pltpu.get_barrier_semaphore()
        for peer in [sibling] + [(cx, cy, c) for cx, cy in chips]:
            pl.semaphore_signal(barrier, inc=1, device_id=peer, device_id_type=MESH)
        pl.semaphore_wait(barrier, 4)
        me = 4 * x + 2 * y + c

        def push(t, src, slot, recv_sem, to):
            pltpu.make_async_remote_copy(src_ref=src, dst_ref=outs[t].at[slot], send_sem=send_sems[t],
                                         recv_sem=recv_sem, device_id=to, device_id_type=MESH).start()

        own = []
        for t in range(nt):
            cp = pltpu.make_async_copy(ins[t], outs[t].at[me], local_sems[t])
            cp.start()
            own.append(cp)
            for j, (cx, cy) in enumerate(chips):
                push(t, ins[t], me, ici_sems[t][j], (cx, cy, c))
            push(t, ins[t], me, sib_sems[t], sibling)
        for t in range(nt):
            for j, (cx, cy) in enumerate(chips):
                slot = 4 * cx + 2 * cy + c
                landed = outs[t].at[slot]
                pltpu.make_async_remote_copy(src_ref=landed, dst_ref=landed, send_sem=send_sems[t],
                                             recv_sem=ici_sems[t][j], device_id=(cx, cy, c),
                                             device_id_type=MESH).wait_recv()
                push(t, landed, slot, sib_sems[t], sibling)
        for t in range(nt):
            own[t].wait()
            four, seven = outs[t].at[pl.ds(0, 4)], outs[t].at[pl.ds(0, 7)]
            pltpu.make_async_remote_copy(src_ref=four, dst_ref=four, send_sem=send_sems[t], recv_sem=sib_sems[t],
                                         device_id=sibling, device_id_type=MESH).wait_recv()
            pltpu.make_async_remote_copy(src_ref=seven, dst_ref=seven, send_sem=send_sems[t], recv_sem=sib_sems[t],
                                         device_id=sibling, device_id_type=MESH).wait_send()

    return pl.kernel(
        body, out_type=out_type, mesh=plsc.ScalarSubcoreMesh(axis_name="sequencer", num_cores=1),
        scratch_types=[pltpu.SemaphoreType.DMA] * (6 * nt),
        compiler_params=pltpu.CompilerParams(collective_id=collective_id), name=name,
    )(*arrays)


def _sc_sibling_exchange(name, collective_id, src, out_shape, pieces):
    def body(src_ref, out_ref, send_sem, recv_sem):
        x, y, c = lax.axis_index("x"), lax.axis_index("y"), lax.axis_index("c")
        sibling = (x, y, 1 - c)
        barrier = pltpu.get_barrier_semaphore()
        pl.semaphore_signal(barrier, inc=1, device_id=sibling, device_id_type=MESH)
        pl.semaphore_wait(barrier, 1)
        for piece, lands in pieces(c, src_ref, out_ref):
            pltpu.make_async_remote_copy(src_ref=piece, dst_ref=lands, send_sem=send_sem, recv_sem=recv_sem,
                                         device_id=sibling, device_id_type=MESH).start()
        drain = pltpu.make_async_remote_copy(src_ref=out_ref, dst_ref=out_ref, send_sem=send_sem, recv_sem=recv_sem,
                                             device_id=sibling, device_id_type=MESH)
        drain.wait_send()
        drain.wait_recv()

    return pl.kernel(
        body, out_type=jax.ShapeDtypeStruct(out_shape, src.dtype),
        mesh=plsc.ScalarSubcoreMesh(axis_name="sequencer", num_cores=1), scratch_types=[pltpu.SemaphoreType.DMA] * 2,
        compiler_params=pltpu.CompilerParams(collective_id=collective_id), name=name,
    )(src)


def _swap_class_columns(name, collective_id, dz, nb, piece=0, npieces=1):
    w = nb // npieces
    return _sc_sibling_exchange(
        name, collective_id, dz, (S, 4 * w),
        lambda c, src, out: [(src.at[:, pl.ds((2 * j + 1 - c) * nb + piece * w, w)], out.at[:, pl.ds(j * w, w)])
                             for j in range(4)])


def _sc_chip_scatter(name, collective_id, q):
    def body(q_ref, out_ref, send_sem, recv_sem, local_sem):
        x, y, c = lax.axis_index("x"), lax.axis_index("y"), lax.axis_index("c")
        chips = [(1 - x, y), (x, 1 - y), (1 - x, 1 - y)]
        barrier = pltpu.get_barrier_semaphore()
        for cx, cy in chips:
            pl.semaphore_signal(barrier, inc=1, device_id=(cx, cy, c), device_id_type=MESH)
        pl.semaphore_wait(barrier, 3)
        mine = 2 * x + y
        own = pltpu.make_async_copy(q_ref.at[mine], out_ref.at[mine], local_sem)
        own.start()
        for cx, cy in chips:
            pltpu.make_async_remote_copy(src_ref=q_ref.at[2 * cx + cy], dst_ref=out_ref.at[mine], send_sem=send_sem,
                                         recv_sem=recv_sem, device_id=(cx, cy, c), device_id_type=MESH).start()
        own.wait()
        three = out_ref.at[pl.ds(0, 3)]
        drain = pltpu.make_async_remote_copy(src_ref=three, dst_ref=three, send_sem=send_sem, recv_sem=recv_sem,
                                             device_id=(x, y, c), device_id_type=MESH)
        drain.wait_send()
        drain.wait_recv()

    return pl.kernel(
        body, out_type=jax.ShapeDtypeStruct(q.shape, q.dtype),
        mesh=plsc.ScalarSubcoreMesh(axis_name="sequencer", num_cores=1), scratch_types=[pltpu.SemaphoreType.DMA] * 3,
        compiler_params=pltpu.CompilerParams(collective_id=collective_id), name=name,
    )(q)


def _mm_pair_dw(h_own, dz, h_sib, dz_sib, nb, name, dep=None, piece=0, npieces=1):
    nb = nb // npieces
    tn = 512 if nb % 512 == 0 else nb
    per = nb // tn
    o_spec = pl.BlockSpec((None, D, tn), lambda i, j, k: (j // per, 0, j % per))
    own_col = lambda i, j, k: (0, ((2 * (j // per) + lax.axis_index("c")) * npieces + piece) * per + j % per)
    part = _matmul(
        h_own, dz, dn=TN, grid=(1, 4 * per, 1),
        a_spec=pl.BlockSpec((S, D), lambda i, j, k: (0, 0)), b_spec=pl.BlockSpec((S, tn), own_col),
        o_spec=o_spec, out_shape=(4, D, nb), out_dtype=F32, acc_shape=(D, tn), name=name + "_own", dep=dep)

    def body(a_ref, b_ref, p_ref, o_ref):
        o_ref[...] = (p_ref[...] + _dot(a_ref[...], b_ref[...], TN)).astype(BF16)

    return pl.pallas_call(
        body, grid=(1, 4 * per, 1),
        in_specs=[pl.BlockSpec((S, D), lambda i, j, k: (0, 0)), pl.BlockSpec((S, tn), lambda i, j, k: (0, j)), o_spec],
        out_specs=o_spec, out_shape=jax.ShapeDtypeStruct((4, D, nb), BF16),
        compiler_params=_params(("parallel", "parallel", "arbitrary"), VMEM_BIG), name=name + "_sibling",
    )(h_sib, dz_sib, part)


SMALL = {
    "e_pre_norm": ((2048,), None), "e_pool_w": ((4, 256, 256), 1), "e_pool_scale": ((1024,), None),
    "e_post_norm": ((2048,), None), "o_pre_norm": ((2048,), 0), "o_sgu_norm_g": ((1024,), 0),
    "o_sgu_norm_b": ((1024,), 0), "o_sgu_w": ((4, 128, 128), None), "o_sgu_b": ((4, 128), None),
    "o_conv_w": ((31, 1024), 1), "o_conv_b": ((1024,), 0), "o_conv_norm_g": ((1024,), 0),
    "o_conv_norm_b": ((1024,), 0), "o_post_norm": ((2048,), 0),
}
SMALL_SHARDED = [n for n, (_, ax) in SMALL.items() if ax is not None]


def _shard_shape(name):
    shape, ax = SMALL[name]
    if ax is None:
        return shape
    return tuple(s // NDEV if i == ax else s for i, s in enumerate(shape))


def _pack(arrs, row_multiple=1):
    flat = jnp.concatenate([a.reshape(-1) for a in arrs])
    pad = -flat.shape[0] % (128 * row_multiple)
    return jnp.concatenate([flat, jnp.zeros((pad,), F32)]).reshape(-1, 128)


def _small_views(name):
    shape, ax = SMALL[name]
    me = lambda: 4 * lax.axis_index("x") + 2 * lax.axis_index("y") + lax.axis_index("c")
    if ax is None:
        view = (int(np.prod(shape)) // 128, 128)
        return view, view, pl.BlockSpec(view, lambda i: (0, 0))
    if len(shape) == 1:
        n = shape[0] // NDEV
        return (1, n), (NDEV, 1, n), pl.BlockSpec((None, 1, n), lambda i: (me(), 0, 0))
    part = _shard_shape(name)
    return part, shape, pl.BlockSpec(part, lambda i: tuple(me() if d == ax else 0 for d in range(len(shape))))


BIG = ("e_w_in", "e_w_out", "o_w_in", "o_w_out")
WEIGHTS = ["e_pre_norm", "e_w_in", "e_pool_w", "e_pool_scale", "e_w_out", "e_post_norm", "o_pre_norm", "o_w_in",
           "o_sgu_norm_g", "o_sgu_norm_b", "o_sgu_w", "o_sgu_b", "o_conv_w", "o_conv_b", "o_conv_norm_g",
           "o_conv_norm_b", "o_w_out", "o_post_norm"]


def kernel(x, e_pre_norm, e_w_in, e_pool_w, e_pool_scale, e_w_out, e_post_norm, o_pre_norm, o_w_in, o_sgu_norm_g, o_sgu_norm_b, o_sgu_w, o_sgu_b, o_conv_w, o_conv_b, o_conv_norm_g, o_conv_norm_b, o_w_out, o_post_norm, loss_target, m_e_pre_norm, m_e_w_in, m_e_pool_w, m_e_pool_scale, m_e_w_out, m_e_post_norm, m_o_pre_norm, m_o_w_in, m_o_sgu_norm_g, m_o_sgu_norm_b, m_o_sgu_w, m_o_sgu_b, m_o_conv_w, m_o_conv_b, m_o_conv_norm_g, m_o_conv_norm_b, m_o_w_out, m_o_post_norm, v_e_pre_norm, v_e_w_in, v_e_pool_w, v_e_pool_scale, v_e_w_out, v_e_post_norm, v_o_pre_norm, v_o_w_in, v_o_sgu_norm_g, v_o_sgu_norm_b, v_o_sgu_w, v_o_sgu_b, v_o_conv_w, v_o_conv_b, v_o_conv_norm_g, v_o_conv_norm_b, v_o_w_out, v_o_post_norm):
    given = dict(locals())
    w = {n: given[n][0] for n in WEIGHTS}
    m = {n: given["m_" + n][0] for n in WEIGHTS}
    v = {n: given["v_" + n][0] for n in WEIGHTS}
    me = 4 * lax.axis_index("x") + 2 * lax.axis_index("y") + lax.axis_index("c")
    x, target = x[0], loss_target[0]
    row = lambda a: a.reshape(1, -1)

    lo, small_rows = _sc_gather_two_level(
        "gather_a0", 0, [_cast_bf16(w["e_w_in"], "cast_e_w_in_0", 0, 2), _pack([w[n] for n in SMALL_SHARDED])])
    hi, = _sc_gather_two_level("gather_a1", 12, [_cast_bf16(w["e_w_in"], "cast_e_w_in_1", 1, 2)])
    wg_e_in = (lo, hi)
    h0 = _pre0_fwd(x, row(w["e_pre_norm"]))
    wg_e_out, = _sc_gather_two_level("gather_b", 1, [_cast_bf16(w["e_w_out"], "cast_e_w_out")])
    wg_o_in, wg_o_out = _sc_gather_two_level(
        "gather_c", 13, [_cast_bf16(w[n], "cast_" + n) for n in ("o_w_in", "o_w_out")])
    h0_sib = _sc_sibling_exchange("swap_h0", 8, h0, h0.shape, lambda c, src, out: [(src, out)])
    p = {n: w[n] for n in SMALL if SMALL[n][1] is None}
    small_rows = small_rows.reshape(NDEV, -1)
    off = 0
    for n in SMALL_SHARDED:
        shp, ax = _shard_shape(n), SMALL[n][1]
        cnt = int(np.prod(shp))
        blk = small_rows[:, off:off + cnt].reshape((NDEV,) + shp)
        p[n] = jnp.moveaxis(blk, 0, ax).reshape(SMALL[n][0])
        off += cnt
    tabs = _rope_tables()
    pool_w_bf = p["e_pool_w"].astype(BF16)
    sgu_bb = jnp.broadcast_to(p["o_sgu_b"][:, :, None], (4, 128, 128))
    conv_w = jnp.concatenate([p["o_conv_w"], jnp.zeros((HALO - CONV_K, HALF), F32)], axis=0)
    odd_p = (row(p["o_sgu_norm_g"]), row(p["o_sgu_norm_b"]), p["o_sgu_w"], sgu_bb, conv_w,
             row(p["o_conv_b"]), row(p["o_conv_norm_g"]), row(p["o_conv_norm_b"]))

    z0 = _mm_in_halves(h0, wg_e_in, "mm_z0")
    ycat0 = _pool_fwd(z0, pool_w_bf, row(p["e_pool_scale"]))
    qkv = _qkv_prep(z0, tabs)
    ycat0, og, lg = _attn_fwd(z0, qkv, ycat0)
    w_out_e, w_out_o = wg_e_out.reshape(2048, D), wg_o_out.reshape(2048, D)
    y0, x1, h1 = _post0_fwd(ycat0, w_out_e, x, row(p["e_post_norm"]), row(p["o_pre_norm"]), h0_sib)
    h1_sib = _sc_sibling_exchange("swap_h1", 11, h1, h1.shape, lambda c, src, out: [(src, out)])
    z1 = _mm_in(h1, wg_o_in, "mm_z1")
    ycat1, conv_out = _odd_fwd(z1, *odd_p)

    g = {}
    loss, dx2, dy1, g["o_post_norm"] = _post1_bwd(ycat1, w_out_o, x1, target, row(p["o_post_norm"]), h1_sib)
    loss = lax.psum(loss[0, 0], ("x", "y", "c"))
    parts = {}
    dw = _mm_out_dw(ycat1, dy1, "mm_dwout1").reshape(NDEV, 256, D)
    parts["o_w_out"], = _sc_exchange("scatter_o_w_out", 2, [dw], True)
    dycat1 = _mm_out_dx(dy1, w_out_o, "mm_dycat1", (dw, loss.reshape(1, 1)))
    dz1, ddc, g["o_sgu_w"], d_sgu_bb, g["o_sgu_norm_g"], g["o_sgu_norm_b"], g["o_conv_norm_g"], \
        g["o_conv_norm_b"], g["o_conv_b"] = _odd_bwd_a(z1, conv_out, dycat1, *odd_p[:4], *odd_p[6:])
    dz1, d_conv_w = _odd_bwd_b(z1, ddc, dz1, conv_w)
    g["o_sgu_b"] = d_sgu_bb[:, :, 0]
    g["o_conv_w"] = d_conv_w[:CONV_K]
    grads, deltas, new_m, new_v = {}, {}, {}, {}

    def adam(n, dep):
        grads[n], deltas[n], new_m[n], new_v[n] = _adam_reduce(parts[n], w[n], m[n], v[n], "adam_" + n, dep)
        return new_v[n]

    pin = _arrived(parts["o_w_out"], "arrived_o_w_out", d_conv_w)
    dz1_sib = _swap_class_columns("swap_dz1", 10, dz1, ODD_IN // NDEV)
    dw = _mm_pair_dw(h1, dz1, h1_sib, dz1_sib, ODD_IN // NDEV, "mm_dwin1", pin)
    parts["o_w_in"] = _sc_chip_scatter("scatter_o_w_in", 3, dw)
    dh1 = _mm_in_dx(dz1, wg_o_in, "mm_dh1", dw)
    dx1, dy0, g["o_pre_norm"], g["e_post_norm"] = _mid_bwd(dx2, dh1, x1, y0, row(p["o_pre_norm"]),
                                                           row(p["e_post_norm"]))
    dw = _mm_out_dw(ycat0, dy0, "mm_dwout0").reshape(NDEV, 256, D)
    parts["e_w_out"], = _sc_exchange("scatter_e_w_out", 4, [dw], True)
    dycat0 = _mm_out_dx(dy0, w_out_e, "mm_dycat0", dw)
    da_in, da_gate, g["e_pool_w"], g["e_pool_scale"] = _pool_bwd(z0, dycat0, pool_w_bf, row(p["e_pool_scale"]))
    late = [n for n in SMALL if n not in ("e_pre_norm", "o_sgu_b")] + ["o_sgu_b"]
    recv_small, = _sc_gather_two_level("gather_small_grads", 6,
                                       [_pack([g[n].reshape(SMALL[n][0]) for n in late], 512)])
    took = _arrived(parts["o_w_in"], "arrived_o_w_in")
    dq, dk, dv, dbg = _attn_bwd(z0, qkv, og, lg, dycat0, tabs, took)
    dz0 = jnp.concatenate([da_in, da_gate, dq, dk, dv, dbg], axis=1)
    took = _arrived(recv_small, "arrived_small_grads", _arrived(parts["e_w_out"], "arrived_e_w_out", dz0))
    nb = EVEN_IN // NDEV
    swapped = [_swap_class_columns("swap_dz0_%d" % half, (9, 14)[half], dz0, nb, half, 2) for half in (0, 1)]
    dw, e_w_in_parts = took, []
    for half in (0, 1):
        dw = _mm_pair_dw(h0, dz0, h0_sib, swapped[half], nb, "mm_dwin0_%d" % half, dw, half, 2)
        e_w_in_parts.append(_sc_chip_scatter("scatter_e_w_in_%d" % half, (5, 15)[half], dw))
    pin = adam("e_w_out", adam("o_w_out", adam("o_w_in", dw)))
    rows = [int(np.prod(SMALL[n][0])) // 128 for n in late]
    summed = dict(zip(late, _sum_unpack(recv_small, rows, "sum_small_grads", pin)))
    dh0 = _mm_in_dx_halves(dz0, wg_e_in, "mm_dh0", summed[late[0]])
    grad_x, g["e_pre_norm"] = _pre0_bwd(dx1, dh0, x, row(p["e_pre_norm"]))
    last, = _sc_exchange("gather_e_pre_norm_grad", 7, [g["e_pre_norm"].reshape(16, 128)], False)

    n = "e_w_in"
    out = _adam_reduce(e_w_in_parts[0], w[n], m[n], v[n], "adam_e_w_in_0", grad_x, 0, 2)
    out = _adam_reduce(e_w_in_parts[1], w[n], m[n], v[n], "adam_e_w_in_1", None, 1, 2, out)
    grads[n], deltas[n], new_m[n], new_v[n] = out
    summed["e_pre_norm"] = _sum_parts(last, "sum_e_pre_norm_grad", out[3])
    names = list(SMALL)
    views = [_small_views(n) for n in names]
    mine = lambda src: [src[n].reshape(vw[0]) for n, vw in zip(names, views)]
    res = _adam_small(mine(w), [summed[n].reshape(vw[1]) for n, vw in zip(names, views)], [vw[2] for vw in views],
                      mine(m), mine(v), "adam_small")
    for n, out in zip(names, res):
        grads[n], deltas[n], new_m[n], new_v[n] = [t.reshape(_shard_shape(n)) for t in out]

    lead = lambda a: a[None]
    return (loss, grad_x[None], *[lead(grads[n]) for n in WEIGHTS], *[lead(deltas[n]) for n in WEIGHTS],
            *[lead(new_m[n]) for n in WEIGHTS], *[lead(new_v[n]) for n in WEIGHTS])
```

```python
import functools

import numpy as np
import jax
import jax.numpy as jnp
from jax import lax
from jax.experimental import pallas as pl
from jax.experimental.pallas import tpu as pltpu
from jax.experimental.pallas import tpu_sc as plsc

F32 = jnp.float32
BF16 = jnp.bfloat16

S = 2048
D = 2048
NDEV = 8
EPS = 1e-6
NEG = -1e30
HEAD_DIM = 128
ROT_DIM = 32
ROPE_THETA = 500000.0
PATTERNS = ((128, 1), (512, 4), (2048, 16))
BLK = 128
EVEN_IN = 12288
ODD_IN = 6144
HALF = 1024
CONV_K = 31
HALO = 32
TR = 256
SUB = 32

ADAM_LR = 0.001
ADAM_B1 = 0.9
ADAM_B2 = 0.999
ADAM_EPS = 1e-08
ADAM_WD = 0.01
ADAM_STEP = 10

VMEM_BIG = 56 * 1024 * 1024
MESH = pl.DeviceIdType.MESH

NN = (((1,), (0,)), ((), ()))
NT = (((1,), (1,)), ((), ()))
TN = (((0,), (0,)), ((), ()))


def _dot(a, b, dn=NN):
    return lax.dot_general(a, b, dn, preferred_element_type=F32)


def _sigmoid(x):
    return 1.0 / (1.0 + jnp.exp(-x))


def _silu_and_grad(x):
    sg = _sigmoid(x)
    return x * sg, sg * (1.0 + x * (1.0 - sg))


def _params(sem, vmem=None):
    return pltpu.CompilerParams(dimension_semantics=sem, vmem_limit_bytes=vmem)


ANY_SPEC = pl.BlockSpec(memory_space=pl.ANY)


def _matmul(a, b, *, dn, grid, a_spec, b_spec, o_spec, out_shape, out_dtype, acc_shape, name, dep=None):
    nk = grid[2]
    deps = [] if dep is None else list(dep) if isinstance(dep, (tuple, list)) else [dep]

    def body(a_ref, b_ref, *rest):
        o_ref, acc = rest[len(deps)], rest[len(deps) + 1:]
        if nk == 1:
            o_ref[...] = _dot(a_ref[...], b_ref[...], dn).astype(o_ref.dtype)
            return
        acc_ref = acc[0]
        k = pl.program_id(2)

        @pl.when(k == 0)
        def _():
            acc_ref[...] = jnp.zeros_like(acc_ref)

        acc_ref[...] += _dot(a_ref[...], b_ref[...], dn)

        @pl.when(k == nk - 1)
        def _():
            o_ref[...] = acc_ref[...].astype(o_ref.dtype)

    return pl.pallas_call(
        body, grid=grid, in_specs=[a_spec, b_spec] + [ANY_SPEC] * len(deps), out_specs=o_spec,
        out_shape=jax.ShapeDtypeStruct(out_shape, out_dtype),
        scratch_shapes=[] if nk == 1 else [pltpu.VMEM(acc_shape, F32)],
        compiler_params=_params(("parallel", "parallel", "arbitrary"), VMEM_BIG), name=name,
    )(a, b, *deps)


TM = 2048


def _mm_in(h, wg, name):
    nb = wg.shape[2]
    tn = 512 if nb % 512 == 0 else nb
    per = nb // tn
    return _matmul(
        h, wg, dn=NN, grid=(S // TM, NDEV * per, 1),
        a_spec=pl.BlockSpec((TM, D), lambda i, j, k: (i, 0)),
        b_spec=pl.BlockSpec((None, D, tn), lambda i, j, k: (j // per, 0, j % per)),
        o_spec=pl.BlockSpec((TM, tn), lambda i, j, k: (i, j)),
        out_shape=(S, NDEV * nb), out_dtype=F32, acc_shape=(TM, tn), name=name)


def _mm_in_halves(h, wg_halves, name):
    hb = wg_halves[0].shape[2]
    z = None
    for half, wg in enumerate(wg_halves):
        prev = [] if z is None else [z]

        def body(a_ref, b_ref, *rest):
            rest[-1][...] = _dot(a_ref[...], b_ref[...])

        z = pl.pallas_call(
            body, grid=(NDEV,),
            in_specs=[pl.BlockSpec((S, D), lambda j: (0, 0)), pl.BlockSpec((None, D, hb), lambda j: (j, 0, 0))]
                     + [ANY_SPEC] * len(prev),
            out_specs=pl.BlockSpec((S, hb), lambda j, half=half: (0, 2 * j + half)),
            out_shape=jax.ShapeDtypeStruct((S, 2 * NDEV * hb), F32),
            input_output_aliases={2: 0} if prev else {},
            compiler_params=_params(("parallel",), VMEM_BIG), name="%s_%d" % (name, half),
        )(h, wg, *prev)
    return z


def _mm_in_dx_halves(dz, wg_halves, name, dep):
    hb = wg_halves[0].shape[2]
    nk = 2 * NDEV

    def body(a_ref, b0_ref, b1_ref, dep_ref, o_ref, acc_ref):
        k = pl.program_id(2)

        @pl.when(k == 0)
        def _():
            acc_ref[...] = jnp.zeros_like(acc_ref)

        @pl.when(k % 2 == 0)
        def _():
            acc_ref[...] += _dot(a_ref[...], b0_ref[...], NT)

        @pl.when(k % 2 == 1)
        def _():
            acc_ref[...] += _dot(a_ref[...], b1_ref[...], NT)

        @pl.when(k == nk - 1)
        def _():
            o_ref[...] = acc_ref[...]

    b_spec = pl.BlockSpec((None, 1024, hb), lambda i, j, k: (k // 2, j, 0))
    return pl.pallas_call(
        body, grid=(1, D // 1024, nk),
        in_specs=[pl.BlockSpec((S, hb), lambda i, j, k: (0, k)), b_spec, b_spec, ANY_SPEC],
        out_specs=pl.BlockSpec((S, 1024), lambda i, j, k: (0, j)), out_shape=jax.ShapeDtypeStruct((S, D), F32),
        scratch_shapes=[pltpu.VMEM((S, 1024), F32)],
        compiler_params=_params(("parallel", "parallel", "arbitrary"), VMEM_BIG), name=name,
    )(dz, *wg_halves, dep)


def _mm_in_dx(dz, wg, name, dep=None):
    nb = wg.shape[2]
    return _matmul(
        dz, wg, dn=NT, grid=(S // TM, D // 1024, NDEV),
        a_spec=pl.BlockSpec((TM, nb), lambda i, j, k: (i, k)),
        b_spec=pl.BlockSpec((None, 1024, nb), lambda i, j, k: (k, j, 0)),
        o_spec=pl.BlockSpec((TM, 1024), lambda i, j, k: (i, j)),
        out_shape=(S, D), out_dtype=F32, acc_shape=(TM, 1024), name=name, dep=dep)


def _mm_out_dx(dy, w, name, dep=None):
    return _matmul(
        dy, w, dn=NT, grid=(S // TM, 2048 // 512, 1),
        a_spec=pl.BlockSpec((TM, D), lambda i, j, k: (i, 0)),
        b_spec=pl.BlockSpec((512, D), lambda i, j, k: (j, 0)),
        o_spec=pl.BlockSpec((TM, 512), lambda i, j, k: (i, j)),
        out_shape=(S, 2048), out_dtype=F32, acc_shape=(TM, 512), name=name, dep=dep)


def _mm_out_dw(yc, dy, name):
    return _matmul(
        yc, dy, dn=TN, grid=(2048 // TM, D // 512, 1),
        a_spec=pl.BlockSpec((S, TM), lambda i, j, k: (0, i)),
        b_spec=pl.BlockSpec((S, 512), lambda i, j, k: (0, j)),
        o_spec=pl.BlockSpec((TM, 512), lambda i, j, k: (i, j)),
        out_shape=(2048, D), out_dtype=BF16, acc_shape=(TM, 512), name=name)


def _row_spec(w=D):
    return pl.BlockSpec((TR, w), lambda i: (i, 0))


def _vec_spec(w=D):
    return pl.BlockSpec((1, w), lambda i: (0, 0))


def _rms_stats(x):
    r = lax.rsqrt(jnp.mean(x * x, axis=-1, keepdims=True) + EPS)
    return x * r, r


def _rms_bwd(dn, xhat, r, g):
    dxh = dn * g
    return r * (dxh - xhat * jnp.mean(dxh * xhat, axis=-1, keepdims=True))


def _acc_rows(ref, val, i):
    s = jnp.sum(val, axis=0, keepdims=True)

    @pl.when(i == 0)
    def _():
        ref[...] = s

    @pl.when(i > 0)
    def _():
        ref[...] += s


def _pre0_fwd(x, g, dep=None):
    deps = [] if dep is None else [dep]

    def body(x_ref, g_ref, *rest):
        xhat, _ = _rms_stats(x_ref[...])
        rest[-1][...] = (xhat * g_ref[...]).astype(BF16)

    return pl.pallas_call(
        body, grid=(S // TR,), in_specs=[_row_spec(), _vec_spec()] + [ANY_SPEC] * len(deps), out_specs=_row_spec(),
        out_shape=jax.ShapeDtypeStruct((S, D), BF16), compiler_params=_params(("parallel",)), name="pre0_fwd",
    )(x, g, *deps)


def _post0_fwd(ycat, w_out, x, g_post, g_pre1, dep):
    def body(yc_ref, w_ref, x_ref, gp_ref, g1_ref, dep_ref, y_ref, x1_ref, h1_ref):
        y = _dot(yc_ref[...], w_ref[...])
        y_ref[...] = y
        yhat, _ = _rms_stats(y)
        x1 = x_ref[...] + yhat * gp_ref[...]
        x1_ref[...] = x1
        xhat, _ = _rms_stats(x1)
        h1_ref[...] = (xhat * g1_ref[...]).astype(BF16)

    return pl.pallas_call(
        body, grid=(S // TR,),
        in_specs=[_row_spec(), pl.BlockSpec((2048, D), lambda i: (0, 0)), _row_spec(), _vec_spec(), _vec_spec(),
                  ANY_SPEC],
        out_specs=[_row_spec(), _row_spec(), _row_spec()],
        out_shape=[jax.ShapeDtypeStruct((S, D), F32), jax.ShapeDtypeStruct((S, D), F32),
                   jax.ShapeDtypeStruct((S, D), BF16)],
        compiler_params=_params(("parallel",), VMEM_BIG), name="post0_fwd",
    )(ycat, w_out, x, g_post, g_pre1, dep)


def _post1_bwd(ycat, w_out, x1, target, g_post, dep):
    def body(yc_ref, w_ref, x1_ref, t_ref, g_ref, dep_ref, loss_ref, dx2_ref, dy_ref, dg_ref):
        i = pl.program_id(0)
        yhat, r = _rms_stats(_dot(yc_ref[...], w_ref[...]))
        g = g_ref[...]
        err = x1_ref[...] + yhat * g - t_ref[...]
        part = jnp.sum(jnp.sum(err * err, axis=-1, keepdims=True), axis=0, keepdims=True) * (0.5 / D)
        _acc_rows(loss_ref, jnp.broadcast_to(part, (1, 128)), i)
        dx2 = err * (1.0 / D)
        dx2_ref[...] = dx2
        _acc_rows(dg_ref, dx2 * yhat, i)
        dy_ref[...] = _rms_bwd(dx2, yhat, r, g).astype(BF16)

    return pl.pallas_call(
        body, grid=(S // TR,),
        in_specs=[_row_spec(), pl.BlockSpec((2048, D), lambda i: (0, 0)), _row_spec(), _row_spec(), _vec_spec(),
                  ANY_SPEC],
        out_specs=[_vec_spec(128), _row_spec(), _row_spec(), _vec_spec()],
        out_shape=[jax.ShapeDtypeStruct((1, 128), F32), jax.ShapeDtypeStruct((S, D), F32),
                   jax.ShapeDtypeStruct((S, D), BF16), jax.ShapeDtypeStruct((1, D), F32)],
        compiler_params=_params(("arbitrary",), VMEM_BIG), name="post1_bwd",
    )(ycat, w_out, x1, target, g_post, dep)


def _mid_bwd(dx2, dh1, x1, y0, g_pre1, g_post0):
    def body(dx2_ref, dh_ref, x1_ref, y_ref, g1_ref, gp_ref, dx1_ref, dy_ref, dg1_ref, dgp_ref):
        i = pl.program_id(0)
        xhat, r1 = _rms_stats(x1_ref[...])
        dh = dh_ref[...]
        _acc_rows(dg1_ref, dh * xhat, i)
        dx1 = dx2_ref[...] + _rms_bwd(dh, xhat, r1, g1_ref[...])
        dx1_ref[...] = dx1
        yhat, r0 = _rms_stats(y_ref[...])
        _acc_rows(dgp_ref, dx1 * yhat, i)
        dy_ref[...] = _rms_bwd(dx1, yhat, r0, gp_ref[...]).astype(BF16)

    return pl.pallas_call(
        body, grid=(S // TR,),
        in_specs=[_row_spec(), _row_spec(), _row_spec(), _row_spec(), _vec_spec(), _vec_spec()],
        out_specs=[_row_spec(), _row_spec(), _vec_spec(), _vec_spec()],
        out_shape=[jax.ShapeDtypeStruct((S, D), F32), jax.ShapeDtypeStruct((S, D), BF16),
                   jax.ShapeDtypeStruct((1, D), F32), jax.ShapeDtypeStruct((1, D), F32)],
        compiler_params=_params(("arbitrary",)), name="mid_bwd",
    )(dx2, dh1, x1, y0, g_pre1, g_post0)


def _pre0_bwd(dx1, dh0, x, g):
    def body(dx1_ref, dh_ref, x_ref, g_ref, gx_ref, dg_ref):
        i = pl.program_id(0)
        xhat, r = _rms_stats(x_ref[...])
        dh = dh_ref[...]
        _acc_rows(dg_ref, dh * xhat, i)
        gx_ref[...] = dx1_ref[...] + _rms_bwd(dh, xhat, r, g_ref[...])

    return pl.pallas_call(
        body, grid=(S // TR,), in_specs=[_row_spec(), _row_spec(), _row_spec(), _vec_spec()],
        out_specs=[_row_spec(), _vec_spec()],
        out_shape=[jax.ShapeDtypeStruct((S, D), F32), jax.ShapeDtypeStruct((1, D), F32)],
        compiler_params=_params(("arbitrary",)), name="pre0_bwd",
    )(dx1, dh0, x, g)


POOL_CH = 256


def _pool_apply(a, w, transpose):
    n = a.shape[0]
    row = lax.broadcasted_iota(jnp.int32, a.shape, 0)
    cnt = jnp.minimum(row + 1, w).astype(F32)
    s = a / cnt if transpose else a
    for k in (1, 2, 4, 8):
        if transpose:
            sh = jnp.where(row < n - k, pltpu.roll(s, n - k, 0), 0.0)
        else:
            sh = jnp.where(row >= k, pltpu.roll(s, k, 0), 0.0)
        s = jnp.where(w > k, s + sh, s)
    return s - a if transpose else s / cnt - a


def _pool_fwd(z0, pool_w, pool_scale):
    def body(a_ref, gate_ref, w_ref, sc_ref, out_ref):
        win = jnp.left_shift(2, pl.program_id(0))
        pooled = _pool_apply(a_ref[...], win, False)
        mixed = _dot(pooled.astype(BF16), w_ref[...])
        gate = gate_ref[...]
        out_ref[...] = (mixed * sc_ref[...] * (gate * _sigmoid(gate))).astype(BF16)

    return pl.pallas_call(
        body, grid=(4,),
        in_specs=[pl.BlockSpec((S, POOL_CH), lambda g: (0, g)), pl.BlockSpec((S, POOL_CH), lambda g: (0, 4 + g)),
                  pl.BlockSpec((None, POOL_CH, POOL_CH), lambda g: (g, 0, 0)),
                  pl.BlockSpec((1, POOL_CH), lambda g: (0, g))],
        out_specs=pl.BlockSpec((S, POOL_CH), lambda g: (0, g)),
        out_shape=jax.ShapeDtypeStruct((S, 2048), BF16),
        compiler_params=_params(("parallel",), VMEM_BIG), name="pool_fwd",
    )(z0, z0, pool_w, pool_scale)


def _pool_bwd(z0, dycat, pool_w, pool_scale):
    def body(a_ref, gate_ref, dy_ref, w_ref, sc_ref, da_ref, dgate_ref, dw_ref, dsc_ref):
        win = jnp.left_shift(2, pl.program_id(0))
        pooled = _pool_apply(a_ref[...], win, False).astype(BF16)
        w = w_ref[...]
        mixed = _dot(pooled, w)
        silu, dsilu = _silu_and_grad(gate_ref[...])
        dy = dy_ref[...]
        sc = sc_ref[...]
        dgate_ref[...] = (dy * (mixed * sc) * dsilu).astype(BF16)
        dms = dy * silu
        dsc_ref[...] = jnp.sum(dms * mixed, axis=0, keepdims=True)
        dmixed = (dms * sc).astype(BF16)
        dw_ref[...] = _dot(pooled, dmixed, TN)
        dpooled = _dot(dmixed, w, NT)
        da_ref[...] = _pool_apply(dpooled, win, True).astype(BF16)

    slab = lambda off: pl.BlockSpec((S, POOL_CH), lambda g: (0, off + g))
    return pl.pallas_call(
        body, grid=(4,),
        in_specs=[slab(0), slab(4), slab(0), pl.BlockSpec((None, POOL_CH, POOL_CH), lambda g: (g, 0, 0)),
                  pl.BlockSpec((1, POOL_CH), lambda g: (0, g))],
        out_specs=[slab(0), slab(0), pl.BlockSpec((None, POOL_CH, POOL_CH), lambda g: (g, 0, 0)),
                   pl.BlockSpec((1, POOL_CH), lambda g: (0, g))],
        out_shape=[jax.ShapeDtypeStruct((S, HALF), BF16), jax.ShapeDtypeStruct((S, HALF), BF16),
                   jax.ShapeDtypeStruct((4, POOL_CH, POOL_CH), F32), jax.ShapeDtypeStruct((1, HALF), F32)],
        compiler_params=_params(("parallel",), VMEM_BIG), name="pool_bwd",
    )(z0, z0, dycat, pool_w, pool_scale)


Q_COL, K_COL, V_COL, BG_COL = 2048 // 128, 5120 // 128, 8192 // 128, 11264 // 128
SCALE = HEAD_DIM ** -0.5


def _rope_tables():
    pos = jnp.arange(S, dtype=F32)
    inv_freq = jnp.power(ROPE_THETA, -jnp.arange(0, ROT_DIM, 2, dtype=F32) / ROT_DIM)
    ang = pos[:, None] * inv_freq[None, :]
    cos, sin = jnp.cos(ang), jnp.sin(ang)
    half = ROT_DIM // 2
    zeros = jnp.zeros((S, HEAD_DIM - ROT_DIM), F32)
    c = jnp.concatenate([cos, cos, jnp.ones((S, HEAD_DIM - ROT_DIM), F32)], axis=1)
    a = jnp.concatenate([-sin, jnp.zeros((S, half), F32), zeros], axis=1)
    b = jnp.concatenate([jnp.zeros((S, half), F32), sin, zeros], axis=1)
    return c, a, b


def _rope(t, c, a, b):
    half = ROT_DIM // 2
    return t * c + pltpu.roll(t, HEAD_DIM - half, 1) * a + pltpu.roll(t, half, 1) * b


def _rope_t(d, c, a, b):
    half = ROT_DIM // 2
    return d * c + pltpu.roll(d * a, half, 1) + pltpu.roll(d * b, HEAD_DIM - half, 1)


def _deinterleave(dst, src, dil, cast=None, dst_off=0):
    length = S // dil
    for r in range(dil):
        v = src[...] if dil == 1 else src[pl.ds(r, length, stride=dil), :]
        dst[dst_off + r * length:dst_off + (r + 1) * length, :] = v if cast is None else v.astype(cast)


def _interleave(dst, src, dil, src_off=0):
    length = S // dil
    for r in range(dil):
        if dil == 1:
            dst[...] = src[src_off:src_off + S, :]
        else:
            dst[pl.ds(r, length, stride=dil), :] = src[src_off + r * length:src_off + (r + 1) * length, :]


CU = 16
NUNITS = S // BLK
B_QK = (((2,), (2,)), ((0,), (0,)))
B_PV = (((2,), (1,)), ((0,), (0,)))
B_TN = (((1,), (1,)), ((0,), (0,)))


def _blocks(ref, first):
    return ref[first * BLK:(first + CU) * BLK, :].reshape(CU, BLK, HEAD_DIM)


def _chunk_scores(u0, nb, qd, kdp):
    q = _blocks(qd, u0)
    row = lax.broadcasted_iota(jnp.int32, (CU, BLK, BLK), 1)
    col = lax.broadcasted_iota(jnp.int32, (CU, BLK, BLK), 2)
    s_own = jnp.where(col <= row, _dot(q, _blocks(kdp, u0 + 1), B_QK) * SCALE, NEG)
    if nb == 1:
        return q, s_own, None
    unit = lax.broadcasted_iota(jnp.int32, (CU, BLK, BLK), 0) + u0
    s_prev = jnp.where((col >= row) & ((unit % nb) != 0), _dot(q, _blocks(kdp, u0), B_QK) * SCALE, NEG)
    return q, s_own, s_prev


def _qkv_prep(z0, tabs):
    def body(q_ref, k_ref, v_ref, c_ref, a_ref, b_ref, qo_ref, ko_ref, vo_ref, tmp):
        p = pl.program_id(1)
        for gi, (_, dil) in enumerate(PATTERNS):
            @pl.when(p == gi)
            def _(dil=dil):
                c, a, b = c_ref[...], a_ref[...], b_ref[...]
                tmp[...] = _rope(q_ref[...], c, a, b)
                _deinterleave(qo_ref, tmp, dil, BF16)
                tmp[...] = _rope(k_ref[...], c, a, b)
                _deinterleave(ko_ref, tmp, dil, BF16)
                _deinterleave(vo_ref, v_ref, dil, BF16)

    tab = pl.BlockSpec((S, HEAD_DIM), lambda h, p: (0, 0))
    out = pl.BlockSpec((S, HEAD_DIM), lambda h, p: (0, p * 8 + h))
    return pl.pallas_call(
        body, grid=(8, 3), in_specs=[_head_spec(Q_COL), _head_spec(K_COL), _head_spec(V_COL), tab, tab, tab],
        out_specs=[out, out, out], out_shape=[jax.ShapeDtypeStruct((S, 3072), BF16)] * 3,
        scratch_shapes=[pltpu.VMEM((S, HEAD_DIM), F32)],
        compiler_params=_params(("parallel", "arbitrary"), VMEM_BIG), name="qkv_prep",
    )(z0, z0, z0, *tabs)


def _pad_copy(dst, src):
    dst[0:BLK, :] = jnp.zeros((BLK, HEAD_DIM), dst.dtype)
    dst[BLK:BLK + S, :] = src[...]


def _attn_group_fwd(dil, qd, kd_ref, vd_ref, kdp, vdp, od, ld, og, lg):
    nb = S // dil // BLK
    _pad_copy(kdp, kd_ref)
    _pad_copy(vdp, vd_ref)
    for u0 in range(0, NUNITS, CU):
        _, s_own, s_prev = _chunk_scores(u0, nb, qd, kdp)
        m = jnp.max(s_own, axis=2, keepdims=True)
        if s_prev is not None:
            m = jnp.maximum(m, jnp.max(s_prev, axis=2, keepdims=True))
        p_own = jnp.exp(s_own - m)
        den = jnp.sum(p_own, axis=2, keepdims=True)
        acc = _dot(p_own.astype(BF16), _blocks(vdp, u0 + 1), B_PV)
        if s_prev is not None:
            p_prev = jnp.exp(s_prev - m)
            den = den + jnp.sum(p_prev, axis=2, keepdims=True)
            acc = acc + _dot(p_prev.astype(BF16), _blocks(vdp, u0), B_PV)
        rows = slice(u0 * BLK, (u0 + CU) * BLK)
        od[rows, :] = (acc / den).reshape(CU * BLK, HEAD_DIM)
        ld[rows, :] = jnp.broadcast_to(m + jnp.log(den), (CU, BLK, HEAD_DIM)).reshape(CU * BLK, HEAD_DIM)
    _interleave(og, od, dil)
    _interleave(lg, ld, dil)


def _group_weights(lgs):
    l0, l1, l2 = lgs[0][...], lgs[1][...], lgs[2][...]
    mx = jnp.maximum(l0, jnp.maximum(l1, l2))
    e0, e1, e2 = jnp.exp(l0 - mx), jnp.exp(l1 - mx), jnp.exp(l2 - mx)
    den = e0 + e1 + e2
    return e0 / den, e1 / den, e2 / den


def _head_spec(base, ngroups_axis=True):
    return pl.BlockSpec((S, HEAD_DIM), lambda h, p: (0, base + (p % 3) * 8 + h))


def _slab(dtype=F32, rows=S):
    return pltpu.VMEM((rows, HEAD_DIM), dtype)


def _attn_fwd(z0, qkv, ycat):
    def body(q_ref, k_ref, v_ref, gate_ref, ycat_ref, out_ref, og_ref, lg_ref,
             kdp, vdp, od, ld, og0, og1, og2, lg0, lg1, lg2):
        del ycat_ref
        p = pl.program_id(1)
        ogs, lgs = (og0, og1, og2), (lg0, lg1, lg2)
        for gi, (_, dil) in enumerate(PATTERNS):
            @pl.when(p == gi)
            def _(gi=gi, dil=dil):
                _attn_group_fwd(dil, q_ref, k_ref, v_ref, kdp, vdp, od, ld, ogs[gi], lgs[gi])
                og_ref[...] = ogs[gi][...]
                lg_ref[...] = lgs[gi][...]

        @pl.when(p == 2)
        def _():
            w0, w1, w2 = _group_weights(lgs)
            o = w0 * og0[...] + w1 * og1[...] + w2 * og2[...]
            gate = gate_ref[...]
            out_ref[...] = (o * (gate * _sigmoid(gate))).astype(BF16)

    grp = pl.BlockSpec((S, HEAD_DIM), lambda h, p: (0, p * 8 + h))
    return pl.pallas_call(
        body, grid=(8, 3),
        in_specs=[grp, grp, grp, pl.BlockSpec((S, HEAD_DIM), lambda h, p: (0, BG_COL + h)), ANY_SPEC],
        out_specs=[pl.BlockSpec((S, HEAD_DIM), lambda h, p: (0, 8 + h)), grp, grp],
        out_shape=[jax.ShapeDtypeStruct((S, 2048), BF16), jax.ShapeDtypeStruct((S, 3072), F32),
                   jax.ShapeDtypeStruct((S, 3072), F32)],
        scratch_shapes=[_slab(BF16, S + BLK), _slab(BF16, S + BLK)] + [_slab() for _ in range(8)],
        input_output_aliases={4: 0},
        compiler_params=_params(("parallel", "arbitrary"), VMEM_BIG), name="attn_fwd",
    )(*qkv, z0, ycat)


def _attn_bwd(z0, qkv, og, lg, dycat, tabs, dep):
    def body(q_ref, k_ref, v_ref, gate_ref, dy_ref, c_ref, a_ref, b_ref,
             og0_ref, og1_ref, og2_ref, lg0_ref, lg1_ref, lg2_ref, dep_ref,
             dq_ref, dk_ref, dv_ref, dbg_ref,
             tmp, kd, vd, ld, dg0, dg1, dg2, cg0, cg1, cg2, dod, cd, dqd, dkd, dvd):
        p = pl.program_id(1)
        ogs, lgs, dgs, cgs = (og0_ref, og1_ref, og2_ref), (lg0_ref, lg1_ref, lg2_ref), (dg0, dg1, dg2), (cg0, cg1, cg2)

        @pl.when(p == 0)
        def _():
            w = _group_weights(lgs)
            o = w[0] * ogs[0][...] + w[1] * ogs[1][...] + w[2] * ogs[2][...]
            silu, dsilu = _silu_and_grad(gate_ref[...])
            dy = dy_ref[...]
            dbg_ref[...] = (dy * o * dsilu).astype(BF16)
            do = dy * silu
            dwbar = jnp.sum(do * o, axis=1, keepdims=True)
            for gi in range(3):
                dgs[gi][...] = w[gi] * do
                cgs[gi][...] = -w[gi] * dwbar

        for gi, (_, dil) in enumerate(PATTERNS):
            @pl.when(p == 1 + gi)
            def _(gi=gi, dil=dil):
                nb = S // dil // BLK
                qd = q_ref
                c, a, b = c_ref[...], a_ref[...], b_ref[...]
                _pad_copy(kd, k_ref)
                _pad_copy(vd, v_ref)
                _deinterleave(dod, dgs[gi], dil, BF16)
                _deinterleave(ld, lgs[gi], dil)
                _deinterleave(cd, cgs[gi], dil)
                dkd[...] = jnp.zeros_like(dkd)
                dvd[...] = jnp.zeros_like(dvd)
                flat = lambda t: t.reshape(CU * BLK, HEAD_DIM)
                for u0 in range(0, NUNITS, CU):
                    q, s_own, s_prev = _chunk_scores(u0, nb, qd, kd)
                    lse, cv, do = _blocks(ld, u0), _blocks(cd, u0), _blocks(dod, u0)
                    own = slice((u0 + 1) * BLK, (u0 + 1 + CU) * BLK)
                    p_own = jnp.exp(s_own - lse)
                    ds_own = (p_own * (_dot(do, _blocks(vd, u0 + 1), B_QK) + cv) * SCALE).astype(BF16)
                    dq = _dot(ds_own, _blocks(kd, u0 + 1), B_PV)
                    dkd[own, :] += flat(_dot(ds_own, q, B_TN))
                    dvd[own, :] += flat(_dot(p_own.astype(BF16), do, B_TN))
                    if s_prev is not None:
                        prev = slice(u0 * BLK, (u0 + CU) * BLK)
                        p_prev = jnp.exp(s_prev - lse)
                        ds_prev = (p_prev * (_dot(do, _blocks(vd, u0), B_QK) + cv) * SCALE).astype(BF16)
                        dq = dq + _dot(ds_prev, _blocks(kd, u0), B_PV)
                        dkd[prev, :] += flat(_dot(ds_prev, q, B_TN))
                        dvd[prev, :] += flat(_dot(p_prev.astype(BF16), do, B_TN))
                    dqd[u0 * BLK:(u0 + CU) * BLK, :] = flat(dq)
                _interleave(tmp, dqd, dil)
                dq_ref[...] = _rope_t(tmp[...], c, a, b).astype(BF16)
                _interleave(tmp, dkd, dil, BLK)
                dk_ref[...] = _rope_t(tmp[...], c, a, b).astype(BF16)
                _interleave(tmp, dvd, dil, BLK)
                dv_ref[...] = tmp[...].astype(BF16)

    tab = pl.BlockSpec((S, HEAD_DIM), lambda h, p: (0, 0))
    hspec = lambda base: pl.BlockSpec((S, HEAD_DIM), lambda h, p: (0, base + h))
    gspec = pl.BlockSpec((S, HEAD_DIM), lambda h, p: (0, jnp.maximum(p - 1, 0) * 8 + h))
    return pl.pallas_call(
        body, grid=(8, 4),
        in_specs=[gspec, gspec, gspec, hspec(BG_COL), hspec(8), tab, tab, tab,
                  hspec(0), hspec(8), hspec(16), hspec(0), hspec(8), hspec(16), ANY_SPEC],
        out_specs=[gspec, gspec, gspec, hspec(0)],
        out_shape=[jax.ShapeDtypeStruct((S, 3072), BF16)] * 3 + [jax.ShapeDtypeStruct((S, HALF), BF16)],
        scratch_shapes=[_slab(), _slab(BF16, S + BLK), _slab(BF16, S + BLK), _slab()] + [_slab() for _ in range(6)]
                       + [_slab(BF16), _slab(), _slab(), _slab(F32, S + BLK), _slab(F32, S + BLK)],
        compiler_params=_params(("parallel", "arbitrary"), VMEM_BIG), name="attn_bwd",
    )(*qkv, z0, dycat, *tabs, og, og, og, lg, lg, lg, dep)


SGU_CH = 256
NCHUNK = TR // 128


def _ln_stats(x):
    mu = jnp.mean(x, axis=-1, keepdims=True)
    xc = x - mu
    r = lax.rsqrt(jnp.mean(xc * xc, axis=-1, keepdims=True) + EPS)
    return xc * r, r


def _ln_bwd(dy, xhat, r, g):
    dxh = dy * g
    return r * (dxh - jnp.mean(dxh, axis=-1, keepdims=True) - xhat * jnp.mean(dxh * xhat, axis=-1, keepdims=True))


def _tril_bf16(w):
    row = lax.broadcasted_iota(jnp.int32, w.shape, 0)
    col = lax.broadcasted_iota(jnp.int32, w.shape, 1)
    return jnp.where(row >= col, w, 0.0).astype(BF16)


def _sgu_gate(vn_s, s_s, w_ref, bb_ref):
    for h in range(4):
        wm = _tril_bf16(w_ref[h])
        bias = bb_ref[h]
        for ch in range(NCHUNK):
            rows, cols = slice(ch * 128, (ch + 1) * 128), slice(h * SGU_CH, (h + 1) * SGU_CH)
            s_s[rows, cols] = _dot(wm, vn_s[rows, cols]) + jnp.concatenate([bias, bias], axis=1)


WIN = HALO + TR
SUBL = 8


def _shifted_copies(dst, src):
    dst[0] = src[...]
    for b in range(1, SUBL):
        dst[b, 0:WIN - SUBL, :] = src[pl.ds(b, WIN - SUBL), :]


def _rows_at(copies, off, n):
    return copies[off % SUBL, pl.ds(off - off % SUBL, n), :]


def _conv_fwd(i, dval_ref, dglu_ref, hval_ref, hglu_ref, cw_ref, cb_ref, xw, xr, dcs):
    halo = hval_ref[...] * _sigmoid(hglu_ref[...])
    xw[0:HALO, :] = jnp.where(i > 0, halo, 0.0)
    xw[HALO:HALO + TR, :] = dval_ref[...] * _sigmoid(dglu_ref[...])
    _shifted_copies(xr, xw)
    for rb in range(TR // SUB):
        acc = jnp.broadcast_to(cb_ref[...], (SUB, HALF))
        for k in range(CONV_K):
            acc = acc + cw_ref[k:k + 1, :] * _rows_at(xr, rb * SUB + HALO - (CONV_K - 1) + k, SUB)
        dcs[rb * SUB:(rb + 1) * SUB, :] = acc


def _odd_in_specs():
    col = lambda j: pl.BlockSpec((TR, HALF), lambda i, *_: (i, j))
    prev = lambda j: pl.BlockSpec((HALO, HALF), lambda i, *_: (jnp.maximum(i * (TR // HALO) - 1, 0), j))
    return [col(0), col(1), col(2), col(3), col(4), col(5), prev(3), prev(4)]


def _full_spec(shape):
    return pl.BlockSpec(shape, lambda i, *_: (0,) * len(shape))


def _odd_fwd(z1, sgu_g, sgu_b, sgu_w, sgu_bb, conv_w, conv_b, cn_g, cn_b):
    def body(u_ref, v_ref, cg_ref, dval_ref, dglu_ref, dgate_ref, hval_ref, hglu_ref,
             g_ref, b_ref, w_ref, bb_ref, cw_ref, cb_ref, cng_ref, cnb_ref, out_ref, dcs, vn_s, s_s, xw, xr):
        i = pl.program_id(0)
        vhat, _ = _ln_stats(v_ref[...])
        vn_s[...] = (vhat * g_ref[...] + b_ref[...]).astype(BF16)
        _sgu_gate(vn_s, s_s, w_ref, bb_ref)
        cg = cg_ref[...]
        out_ref[:, 0:HALF] = (u_ref[...] * s_s[...] * (cg * _sigmoid(cg))).astype(BF16)
        _conv_fwd(i, dval_ref, dglu_ref, hval_ref, hglu_ref, cw_ref, cb_ref, xw, xr, dcs)
        dhat, _ = _ln_stats(dcs[...])
        dn = dhat * cng_ref[...] + cnb_ref[...]
        dgate = dgate_ref[...]
        out_ref[:, HALF:2 * HALF] = ((dn * _sigmoid(dn)) * (dgate * _sigmoid(dgate))).astype(BF16)

    vec = _full_spec((1, HALF))
    return pl.pallas_call(
        body, grid=(S // TR,),
        in_specs=_odd_in_specs() + [vec, vec, _full_spec((4, 128, 128)), _full_spec((4, 128, 128)),
                                    _full_spec((HALO, HALF)), vec, vec, vec],
        out_specs=[pl.BlockSpec((TR, 2048), lambda i: (i, 0)), pl.BlockSpec((TR, HALF), lambda i: (i, 0))],
        out_shape=[jax.ShapeDtypeStruct((S, 2048), BF16), jax.ShapeDtypeStruct((S, HALF), F32)],
        scratch_shapes=[pltpu.VMEM((TR, HALF), BF16), pltpu.VMEM((TR, HALF), F32),
                        pltpu.VMEM((WIN, HALF), F32), pltpu.VMEM((SUBL, WIN, HALF), F32)],
        compiler_params=_params(("parallel",), VMEM_BIG), name="odd_fwd",
    )(z1, z1, z1, z1, z1, z1, z1, z1, sgu_g, sgu_b, sgu_w, sgu_bb, conv_w, conv_b, cn_g, cn_b)


def _odd_bwd_a(z1, dc, dycat, sgu_g, sgu_b, sgu_w, sgu_bb, cn_g, cn_b):
    def body(u_ref, v_ref, cg_ref, dgate_ref, dcs, dy_ref, g_ref, b_ref, w_ref, bb_ref, cng_ref, cnb_ref,
             dz_ref, ddc_ref, dw_ref, dbb_ref, dg_ref, db_ref, dcng_ref, dcnb_ref, dcb_ref,
             vn_s, s_s, ds_s, dvn_s):
        i = pl.program_id(0)
        vhat, rv = _ln_stats(v_ref[...])
        g = g_ref[...]
        vn_s[...] = (vhat * g + b_ref[...]).astype(BF16)
        _sgu_gate(vn_s, s_s, w_ref, bb_ref)
        silu_c, dsilu_c = _silu_and_grad(cg_ref[...])
        dyc = dy_ref[:, 0:HALF]
        u = u_ref[...]
        s = s_s[...]
        dz_ref[:, 0:HALF] = (dyc * s * silu_c).astype(BF16)
        dz_ref[:, 2 * HALF:3 * HALF] = (dyc * u * s * dsilu_c).astype(BF16)
        ds_s[...] = dyc * u * silu_c

        @pl.when(i == 0)
        def _():
            dw_ref[...] = jnp.zeros_like(dw_ref)
            dbb_ref[...] = jnp.zeros_like(dbb_ref)

        tril = lax.broadcasted_iota(jnp.int32, (128, 128), 0) >= lax.broadcasted_iota(jnp.int32, (128, 128), 1)
        for h in range(4):
            wm = _tril_bf16(w_ref[h])
            for ch in range(NCHUNK):
                rows, cols = slice(ch * 128, (ch + 1) * 128), slice(h * SGU_CH, (h + 1) * SGU_CH)
                ds = ds_s[rows, cols]
                dsb = ds.astype(BF16)
                dw_ref[h] += jnp.where(tril, _dot(dsb, vn_s[rows, cols], NT), 0.0)
                dbb_ref[h] += jnp.broadcast_to(jnp.sum(ds, axis=1, keepdims=True), (128, 128))
                dvn_s[rows, cols] = _dot(wm, dsb, TN)
        dvn = dvn_s[...]
        _acc_rows(dg_ref, dvn * vhat, i)
        _acc_rows(db_ref, dvn, i)
        dz_ref[:, HALF:2 * HALF] = _ln_bwd(dvn, vhat, rv, g).astype(BF16)

        dhat, rd = _ln_stats(dcs[...])
        cng = cng_ref[...]
        silu_n, dsilu_n = _silu_and_grad(dhat * cng + cnb_ref[...])
        silu_g, dsilu_g = _silu_and_grad(dgate_ref[...])
        dyd = dy_ref[:, HALF:2 * HALF]
        dz_ref[:, 5 * HALF:6 * HALF] = (dyd * silu_n * dsilu_g).astype(BF16)
        ddn = dyd * silu_g * dsilu_n
        _acc_rows(dcng_ref, ddn * dhat, i)
        _acc_rows(dcnb_ref, ddn, i)
        ddc = _ln_bwd(ddn, dhat, rd, cng)
        ddc_ref[...] = ddc
        _acc_rows(dcb_ref, ddc, i)

    vec = _full_spec((1, HALF))
    sq = _full_spec((4, 128, 128))
    col = lambda j: pl.BlockSpec((TR, HALF), lambda i: (i, j))
    return pl.pallas_call(
        body, grid=(S // TR,),
        in_specs=[col(0), col(1), col(2), col(5), col(0), pl.BlockSpec((TR, 2048), lambda i: (i, 0)),
                  vec, vec, sq, sq, vec, vec],
        out_specs=[pl.BlockSpec((TR, ODD_IN), lambda i: (i, 0)), pl.BlockSpec((TR, HALF), lambda i: (i, 0)),
                   sq, sq, vec, vec, vec, vec, vec],
        out_shape=[jax.ShapeDtypeStruct((S, ODD_IN), BF16), jax.ShapeDtypeStruct((S, HALF), F32),
                   jax.ShapeDtypeStruct((4, 128, 128), F32), jax.ShapeDtypeStruct((4, 128, 128), F32)]
                  + [jax.ShapeDtypeStruct((1, HALF), F32)] * 5,
        scratch_shapes=[pltpu.VMEM((TR, HALF), BF16), pltpu.VMEM((TR, HALF), F32),
                        pltpu.VMEM((TR, HALF), F32), pltpu.VMEM((TR, HALF), F32)],
        compiler_params=_params(("arbitrary",), VMEM_BIG), name="odd_bwd_a",
    )(z1, z1, z1, z1, dc, dycat, sgu_g, sgu_b, sgu_w, sgu_bb, cn_g, cn_b)


def _odd_bwd_b(z1, ddc, dz1, conv_w):
    nt = S // TR

    def body(dval_ref, dglu_ref, hval_ref, hglu_ref, ddc_ref, hddc_ref, cw_ref, dz_in_ref,
             dz_ref, dcw_ref, xw, dwin, dxs, xr, dr):
        del dz_in_ref
        i, j = pl.program_id(0), pl.program_id(1)
        sg = _sigmoid(dglu_ref[...])
        dval = dval_ref[...]

        @pl.when(j == 0)
        def _():
            halo = hval_ref[...] * _sigmoid(hglu_ref[...])
            xw[0:HALO, :] = jnp.where(i > 0, halo, 0.0)
            xw[HALO:HALO + TR, :] = dval * sg
            dwin[0:TR, :] = ddc_ref[...]
            dwin[TR:TR + HALO, :] = jnp.where(i < nt - 1, hddc_ref[...], 0.0)
            _shifted_copies(xr, xw)
            _shifted_copies(dr, dwin)

            @pl.when(i == 0)
            def _():
                dcw_ref[...] = jnp.zeros_like(dcw_ref)

            for rb in range(TR // SUB):
                acc = jnp.zeros((SUB, HALF), F32)
                for k in range(CONV_K):
                    acc = acc + cw_ref[k:k + 1, :] * _rows_at(dr, rb * SUB + (CONV_K - 1) - k, SUB)
                dxs[rb * SUB:(rb + 1) * SUB, :] = acc
            for k in range(CONV_K):
                acc = jnp.zeros((SUB, HALF), F32)
                for rb in range(TR // SUB):
                    acc = acc + dwin[rb * SUB:(rb + 1) * SUB, :] * _rows_at(xr, rb * SUB + HALO - (CONV_K - 1) + k, SUB)
                dcw_ref[k:k + 1, :] += jnp.sum(acc, axis=0, keepdims=True)
            dz_ref[...] = (dxs[...] * sg).astype(BF16)

        @pl.when(j == 1)
        def _():
            dz_ref[...] = (dxs[...] * dval * sg * (1.0 - sg)).astype(BF16)

    col = lambda c: pl.BlockSpec((TR, HALF), lambda i, j: (i, c))
    prev = lambda c: pl.BlockSpec((HALO, HALF), lambda i, j: (jnp.maximum(i * (TR // HALO) - 1, 0), c))
    nxt = pl.BlockSpec((HALO, HALF), lambda i, j: (jnp.minimum((i + 1) * (TR // HALO), S // HALO - 1), 0))
    return pl.pallas_call(
        body, grid=(nt, 2),
        in_specs=[col(3), col(4), prev(3), prev(4), pl.BlockSpec((TR, HALF), lambda i, j: (i, 0)), nxt,
                  _full_spec((HALO, HALF)), pl.BlockSpec(memory_space=pl.ANY)],
        out_specs=[pl.BlockSpec((TR, HALF), lambda i, j: (i, 3 + j)), _full_spec((HALO, HALF))],
        out_shape=[jax.ShapeDtypeStruct((S, ODD_IN), BF16), jax.ShapeDtypeStruct((HALO, HALF), F32)],
        scratch_shapes=[pltpu.VMEM((WIN, HALF), F32), pltpu.VMEM((WIN, HALF), F32), pltpu.VMEM((TR, HALF), F32),
                        pltpu.VMEM((SUBL, WIN, HALF), F32), pltpu.VMEM((SUBL, WIN, HALF), F32)],
        input_output_aliases={7: 0},
        compiler_params=_params(("arbitrary", "arbitrary"), VMEM_BIG), name="odd_bwd_b",
    )(z1, z1, z1, z1, ddc, ddc, conv_w, dz1)


def _cast_bf16(w, name, piece=0, npieces=1):
    r, c = w.shape[0], w.shape[1] // npieces
    tr = min(r, 256)

    def body(i_ref, o_ref):
        o_ref[...] = i_ref[...].astype(BF16)

    return pl.pallas_call(
        body, grid=(r // tr,), in_specs=[pl.BlockSpec((tr, c), lambda i: (i, piece))],
        out_specs=pl.BlockSpec((tr, c), lambda i: (i, 0)), out_shape=jax.ShapeDtypeStruct((r, c), BF16),
        compiler_params=_params(("parallel",)), name=name,
    )(w)


def _adamw(w, g, m, v):
    m = ADAM_B1 * m + (1.0 - ADAM_B1) * g
    v = ADAM_B2 * v + (1.0 - ADAM_B2) * (g * g)
    m_hat = m / (1.0 - ADAM_B1 ** ADAM_STEP)
    v_hat = v / (1.0 - ADAM_B2 ** ADAM_STEP)
    delta = -ADAM_LR * (m_hat / (jnp.sqrt(v_hat) + ADAM_EPS) + ADAM_WD * w)
    return delta, m, v


def _adam_reduce(parts, w, m, v, name, dep=None, piece=0, npieces=1, prev=None):
    r, c = w.shape
    cp = c // npieces
    tr = min(r, 128)
    extra = ([] if dep is None else [dep]) + ([] if prev is None else list(prev))
    nparts = parts.shape[0]

    def body(p_ref, w_ref, m_ref, v_ref, *rest):
        g_ref, d_ref, nm_ref, nv_ref = rest[len(extra):]
        g = p_ref[0].astype(F32)
        for d in range(1, nparts):
            g = g + p_ref[d].astype(F32)
        g_ref[...] = g
        d_ref[...], nm_ref[...], nv_ref[...] = _adamw(w_ref[...], g, m_ref[...], v_ref[...])

    spec = pl.BlockSpec((tr, cp), lambda i: (i, piece))
    first = 4 + (0 if dep is None else 1)
    return pl.pallas_call(
        body, grid=(r // tr,),
        in_specs=[pl.BlockSpec((nparts, tr, cp), lambda i: (0, i, 0)), spec, spec, spec] + [ANY_SPEC] * len(extra),
        out_specs=[spec] * 4, out_shape=[jax.ShapeDtypeStruct((r, c), F32)] * 4,
        input_output_aliases={} if prev is None else {first + k: k for k in range(4)},
        compiler_params=_params(("parallel",), VMEM_BIG), name=name,
    )(parts, w, m, v, *extra)


def _arrived(x, name, dep=None):
    deps = [] if dep is None else [dep]

    def body(*refs):
        refs[-1][...] = jnp.zeros_like(refs[-1])

    return pl.pallas_call(
        body, in_specs=[ANY_SPEC] * (1 + len(deps)), out_specs=pl.BlockSpec(memory_space=pltpu.VMEM),
        out_shape=jax.ShapeDtypeStruct((8, 128), F32), name=name,
    )(x, *deps)


def _sum_parts(parts, name, dep=None):
    r = parts.shape[1]
    tr = 8
    for cand in (512, 256, 128, 64, 32, 16, 8):
        if r % cand == 0:
            tr = cand
            break
    deps = [] if dep is None else [dep]

    def body(p_ref, *rest):
        g = p_ref[0]
        for d in range(1, NDEV):
            g = g + p_ref[d]
        rest[-1][...] = g

    return pl.pallas_call(
        body, grid=(r // tr,), in_specs=[pl.BlockSpec((NDEV, tr, 128), lambda i: (0, i, 0))] + [ANY_SPEC] * len(deps),
        out_specs=pl.BlockSpec((tr, 128), lambda i: (i, 0)), out_shape=jax.ShapeDtypeStruct((r, 128), F32),
        compiler_params=_params(("parallel",)), name=name,
    )(parts, *deps)


def _sum_unpack(parts, rows, name, dep=None):
    deps = [] if dep is None else [dep]

    def body(p_ref, *outs):
        outs = outs[len(deps):]
        off = 0
        for o_ref, n in zip(outs, rows):
            acc = p_ref[0, off:off + n, :]
            for d in range(1, NDEV):
                acc = acc + p_ref[d, off:off + n, :]
            o_ref[...] = acc
            off += n

    return pl.pallas_call(
        body, grid=(1,), in_specs=[pl.BlockSpec(parts.shape, lambda i: (0, 0, 0))] + [ANY_SPEC] * len(deps),
        out_specs=[pl.BlockSpec((n, 128), lambda i: (0, 0)) for n in rows],
        out_shape=[jax.ShapeDtypeStruct((n, 128), F32) for n in rows],
        compiler_params=_params(("arbitrary",), VMEM_BIG), name=name,
    )(parts, *deps)


def _adam_small(ws, gs, g_specs, ms, vs, name):
    n = len(ws)

    def body(*refs):
        w_r, g_r, m_r, v_r = refs[:n], refs[n:2 * n], refs[2 * n:3 * n], refs[3 * n:4 * n]
        outs = refs[4 * n:]
        for i in range(n):
            g = g_r[i][...]
            outs[4 * i][...] = g
            outs[4 * i + 1][...], outs[4 * i + 2][...], outs[4 * i + 3][...] = _adamw(
                w_r[i][...], g, m_r[i][...], v_r[i][...])

    whole = lambda a: pl.BlockSpec(a.shape, lambda i, nd=a.ndim: (0,) * nd)
    outs = pl.pallas_call(
        body, grid=(1,),
        in_specs=[whole(a) for a in ws] + list(g_specs) + [whole(a) for a in ms] + [whole(a) for a in vs],
        out_specs=[whole(a) for a in ws for _ in range(4)],
        out_shape=[jax.ShapeDtypeStruct(a.shape, F32) for a in ws for _ in range(4)],
        compiler_params=_params(("arbitrary",), VMEM_BIG), name=name,
    )(*ws, *gs, *ms, *vs)
    return [outs[4 * i:4 * i + 4] for i in range(n)]


MASKS = [(mx, my, mc) for mx in (0, 1) for my in (0, 1) for mc in (0, 1)][1:]


def _sc_exchange(name, collective_id, arrays, scatter):
    nt = len(arrays)
    out_type = [jax.ShapeDtypeStruct(a.shape if scatter else (NDEV,) + a.shape, a.dtype) for a in arrays]

    def body(*refs):
        ins, outs = refs[:nt], refs[nt:2 * nt]
        send_sems, recv_sems, local_sems = refs[2 * nt:3 * nt], refs[3 * nt:4 * nt], refs[4 * nt:5 * nt]
        x, y, c = lax.axis_index("x"), lax.axis_index("y"), lax.axis_index("c")
        peers = [(mx + x - 2 * mx * x, my + y - 2 * my * y, mc + c - 2 * mc * c) for mx, my, mc in MASKS]
        barrier = pltpu.get_barrier_semaphore()
        for peer in peers:
            pl.semaphore_signal(barrier, inc=1, device_id=peer, device_id_type=MESH)
        pl.semaphore_wait(barrier, len(peers))
        me = 4 * x + 2 * y + c
        own = []
        for t in range(nt):
            cp = pltpu.make_async_copy(ins[t].at[me] if scatter else ins[t], outs[t].at[me], local_sems[t])
            cp.start()
            own.append(cp)
            for px, py, pc in peers:
                src = ins[t].at[4 * px + 2 * py + pc] if scatter else ins[t]
                pltpu.make_async_remote_copy(src_ref=src, dst_ref=outs[t].at[me], send_sem=send_sems[t],
                                             recv_sem=recv_sems[t], device_id=(px, py, pc), device_id_type=MESH).start()
        for t in range(nt):
            own[t].wait()
            seven = outs[t].at[pl.ds(0, NDEV - 1)]
            drain = pltpu.make_async_remote_copy(src_ref=seven, dst_ref=seven, send_sem=send_sems[t],
                                                 recv_sem=recv_sems[t], device_id=(x, y, c), device_id_type=MESH)
            drain.wait_send()
            drain.wait_recv()

    return pl.kernel(
        body, out_type=out_type, mesh=plsc.ScalarSubcoreMesh(axis_name="sequencer", num_cores=1),
        scratch_types=[pltpu.SemaphoreType.DMA] * (3 * nt),
        compiler_params=pltpu.CompilerParams(collective_id=collective_id), name=name,
    )(*arrays)


def _sc_gather_two_level(name, collective_id, arrays):
    nt = len(arrays)
    out_type = [jax.ShapeDtypeStruct((NDEV,) + a.shape, a.dtype) for a in arrays]

    def body(*refs):
        ins, outs = refs[:nt], refs[nt:2 * nt]
        sems = refs[2 * nt:]
        send_sems, sib_sems, local_sems = sems[:nt], sems[nt:2 * nt], sems[2 * nt:3 * nt]
        ici_sems = [sems[3 * nt + 3 * t:3 * nt + 3 * t + 3] for t in range(nt)]
        x, y, c = lax.axis_index("x"), lax.axis_index("y"), lax.axis_index("c")
        sibling = (x, y, 1 - c)
        chips = [(1 - x, y), (x, 1 - y), (1 - x, 1 - y)]
        barrier = pltpu.get_barrier_semaphore()
        for peer in [sibling] + [(cx, cy, c) for cx, cy in chips]:
            pl.semaphore_signal(barrier, inc=1, device_id=peer, device_id_type=MESH)
        pl.semaphore_wait(barrier, 4)
        me = 4 * x + 2 * y + c

        def push(t, src, slot, recv_sem, to):
            pltpu.make_async_remote_copy(src_ref=src, dst_ref=outs[t].at[slot], send_sem=send_sems[t],
                                         recv_sem=recv_sem, device_id=to, device_id_type=MESH).start()

        own = []
        for t in range(nt):
            cp = pltpu.make_async_copy(ins[t], outs[t].at[me], local_sems[t])
            cp.start()
            own.append(cp)
            for j, (cx, cy) in enumerate(chips):
                push(t, ins[t], me, ici_sems[t][j], (cx, cy, c))
            push(t, ins[t], me, sib_sems[t], sibling)
        for t in range(nt):
            for j, (cx, cy) in enumerate(chips):
                slot = 4 * cx + 2 * cy + c
                landed = outs[t].at[slot]
                pltpu.make_async_remote_copy(src_ref=landed, dst_ref=landed, send_sem=send_sems[t],
                                             recv_sem=ici_sems[t][j], device_id=(cx, cy, c),
                                             device_id_type=MESH).wait_recv()
                push(t, landed, slot, sib_sems[t], sibling)
        for t in range(nt):
            own[t].wait()
            four, seven = outs[t].at[pl.ds(0, 4)], outs[t].at[pl.ds(0, 7)]
            pltpu.make_async_remote_copy(src_ref=four, dst_ref=four, send_sem=send_sems[t], recv_sem=sib_sems[t],
                                         device_id=sibling, device_id_type=MESH).wait_recv()
            pltpu.make_async_remote_copy(src_ref=seven, dst_ref=seven, send_sem=send_sems[t], recv_sem=sib_sems[t],
                                         device_id=sibling, device_id_type=MESH).wait_send()

    return pl.kernel(
        body, out_type=out_type, mesh=plsc.ScalarSubcoreMesh(axis_name="sequencer", num_cores=1),
        scratch_types=[pltpu.SemaphoreType.DMA] * (6 * nt),
        compiler_params=pltpu.CompilerParams(collective_id=collective_id), name=name,
    )(*arrays)


def _sc_sibling_exchange(name, collective_id, src, out_shape, pieces):
    def body(src_ref, out_ref, send_sem, recv_sem):
        x, y, c = lax.axis_index("x"), lax.axis_index("y"), lax.axis_index("c")
        sibling = (x, y, 1 - c)
        barrier = pltpu.get_barrier_semaphore()
        pl.semaphore_signal(barrier, inc=1, device_id=sibling, device_id_type=MESH)
        pl.semaphore_wait(barrier, 1)
        for piece, lands in pieces(c, src_ref, out_ref):
            pltpu.make_async_remote_copy(src_ref=piece, dst_ref=lands, send_sem=send_sem, recv_sem=recv_sem,
                                         device_id=sibling, device_id_type=MESH).start()
        drain = pltpu.make_async_remote_copy(src_ref=out_ref, dst_ref=out_ref, send_sem=send_sem, recv_sem=recv_sem,
                                             device_id=sibling, device_id_type=MESH)
        drain.wait_send()
        drain.wait_recv()

    return pl.kernel(
        body, out_type=jax.ShapeDtypeStruct(out_shape, src.dtype),
        mesh=plsc.ScalarSubcoreMesh(axis_name="sequencer", num_cores=1), scratch_types=[pltpu.SemaphoreType.DMA] * 2,
        compiler_params=pltpu.CompilerParams(collective_id=collective_id), name=name,
    )(src)


def _swap_class_columns(name, collective_id, dz, nb, piece=0, npieces=1):
    w = nb // npieces
    return _sc_sibling_exchange(
        name, collective_id, dz, (S, 4 * w),
        lambda c, src, out: [(src.at[:, pl.ds((2 * j + 1 - c) * nb + piece * w, w)], out.at[:, pl.ds(j * w, w)])
                             for j in range(4)])


def _sc_chip_scatter(name, collective_id, q):
    def body(q_ref, out_ref, send_sem, recv_sem, local_sem):
        x, y, c = lax.axis_index("x"), lax.axis_index("y"), lax.axis_index("c")
        chips = [(1 - x, y), (x, 1 - y), (1 - x, 1 - y)]
        barrier = pltpu.get_barrier_semaphore()
        for cx, cy in chips:
            pl.semaphore_signal(barrier, inc=1, device_id=(cx, cy, c), device_id_type=MESH)
        pl.semaphore_wait(barrier, 3)
        mine = 2 * x + y
        own = pltpu.make_async_copy(q_ref.at[mine], out_ref.at[mine], local_sem)
        own.start()
        for cx, cy in chips:
            pltpu.make_async_remote_copy(src_ref=q_ref.at[2 * cx + cy], dst_ref=out_ref.at[mine], send_sem=send_sem,
                                         recv_sem=recv_sem, device_id=(cx, cy, c), device_id_type=MESH).start()
        own.wait()
        three = out_ref.at[pl.ds(0, 3)]
        drain = pltpu.make_async_remote_copy(src_ref=three, dst_ref=three, send_sem=send_sem, recv_sem=recv_sem,
                                             device_id=(x, y, c), device_id_type=MESH)
        drain.wait_send()
        drain.wait_recv()

    return pl.kernel(
        body, out_type=jax.ShapeDtypeStruct(q.shape, q.dtype),
        mesh=plsc.ScalarSubcoreMesh(axis_name="sequencer", num_cores=1), scratch_types=[pltpu.SemaphoreType.DMA] * 3,
        compiler_params=pltpu.CompilerParams(collective_id=collective_id), name=name,
    )(q)


def _mm_pair_dw(h_own, dz, h_sib, dz_sib, nb, name, dep=None, piece=0, npieces=1):
    nb = nb // npieces
    tn = 512 if nb % 512 == 0 else nb
    per = nb // tn
    o_spec = pl.BlockSpec((None, D, tn), lambda i, j, k: (j // per, 0, j % per))
    own_col = lambda i, j, k: (0, ((2 * (j // per) + lax.axis_index("c")) * npieces + piece) * per + j % per)
    part = _matmul(
        h_own, dz, dn=TN, grid=(1, 4 * per, 1),
        a_spec=pl.BlockSpec((S, D), lambda i, j, k: (0, 0)), b_spec=pl.BlockSpec((S, tn), own_col),
        o_spec=o_spec, out_shape=(4, D, nb), out_dtype=F32, acc_shape=(D, tn), name=name + "_own", dep=dep)

    def body(a_ref, b_ref, p_ref, o_ref):
        o_ref[...] = (p_ref[...] + _dot(a_ref[...], b_ref[...], TN)).astype(BF16)

    return pl.pallas_call(
        body, grid=(1, 4 * per, 1),
        in_specs=[pl.BlockSpec((S, D), lambda i, j, k: (0, 0)), pl.BlockSpec((S, tn), lambda i, j, k: (0, j)), o_spec],
        out_specs=o_spec, out_shape=jax.ShapeDtypeStruct((4, D, nb), BF16),
        compiler_params=_params(("parallel", "parallel", "arbitrary"), VMEM_BIG), name=name + "_sibling",
    )(h_sib, dz_sib, part)


SMALL = {
    "e_pre_norm": ((2048,), None), "e_pool_w": ((4, 256, 256), 1), "e_pool_scale": ((1024,), None),
    "e_post_norm": ((2048,), None), "o_pre_norm": ((2048,), 0), "o_sgu_norm_g": ((1024,), 0),
    "o_sgu_norm_b": ((1024,), 0), "o_sgu_w": ((4, 128, 128), None), "o_sgu_b": ((4, 128), None),
    "o_conv_w": ((31, 1024), 1), "o_conv_b": ((1024,), 0), "o_conv_norm_g": ((1024,), 0),
    "o_conv_norm_b": ((1024,), 0), "o_post_norm": ((2048,), 0),
}
SMALL_SHARDED = [n for n, (_, ax) in SMALL.items() if ax is not None]


def _shard_shape(name):
    shape, ax = SMALL[name]
    if ax is None:
        return shape
    return tuple(s // NDEV if i == ax else s for i, s in enumerate(shape))


def _pack(arrs, row_multiple=1):
    flat = jnp.concatenate([a.reshape(-1) for a in arrs])
    pad = -flat.shape[0] % (128 * row_multiple)
    return jnp.concatenate([flat, jnp.zeros((pad,), F32)]).reshape(-1, 128)


def _small_views(name):
    shape, ax = SMALL[name]
    me = lambda: 4 * lax.axis_index("x") + 2 * lax.axis_index("y") + lax.axis_index("c")
    if ax is None:
        view = (int(np.prod(shape)) // 128, 128)
        return view, view, pl.BlockSpec(view, lambda i: (0, 0))
    if len(shape) == 1:
        n = shape[0] // NDEV
        return (1, n), (NDEV, 1, n), pl.BlockSpec((None, 1, n), lambda i: (me(), 0, 0))
    part = _shard_shape(name)
    return part, shape, pl.BlockSpec(part, lambda i: tuple(me() if d == ax else 0 for d in range(len(shape))))


BIG = ("e_w_in", "e_w_out", "o_w_in", "o_w_out")
WEIGHTS = ["e_pre_norm", "e_w_in", "e_pool_w", "e_pool_scale", "e_w_out", "e_post_norm", "o_pre_norm", "o_w_in",
           "o_sgu_norm_g", "o_sgu_norm_b", "o_sgu_w", "o_sgu_b", "o_conv_w", "o_conv_b", "o_conv_norm_g",
           "o_conv_norm_b", "o_w_out", "o_post_norm"]


def kernel(x, e_pre_norm, e_w_in, e_pool_w, e_pool_scale, e_w_out, e_post_norm, o_pre_norm, o_w_in, o_sgu_norm_g, o_sgu_norm_b, o_sgu_w, o_sgu_b, o_conv_w, o_conv_b, o_conv_norm_g, o_conv_norm_b, o_w_out, o_post_norm, loss_target, m_e_pre_norm, m_e_w_in, m_e_pool_w, m_e_pool_scale, m_e_w_out, m_e_post_norm, m_o_pre_norm, m_o_w_in, m_o_sgu_norm_g, m_o_sgu_norm_b, m_o_sgu_w, m_o_sgu_b, m_o_conv_w, m_o_conv_b, m_o_conv_norm_g, m_o_conv_norm_b, m_o_w_out, m_o_post_norm, v_e_pre_norm, v_e_w_in, v_e_pool_w, v_e_pool_scale, v_e_w_out, v_e_post_norm, v_o_pre_norm, v_o_w_in, v_o_sgu_norm_g, v_o_sgu_norm_b, v_o_sgu_w, v_o_sgu_b, v_o_conv_w, v_o_conv_b, v_o_conv_norm_g, v_o_conv_norm_b, v_o_w_out, v_o_post_norm):
    given = dict(locals())
    w = {n: given[n][0] for n in WEIGHTS}
    m = {n: given["m_" + n][0] for n in WEIGHTS}
    v = {n: given["v_" + n][0] for n in WEIGHTS}
    me = 4 * lax.axis_index("x") + 2 * lax.axis_index("y") + lax.axis_index("c")
    x, target = x[0], loss_target[0]
    row = lambda a: a.reshape(1, -1)

    lo, small_rows = _sc_gather_two_level(
        "gather_a0", 0, [_cast_bf16(w["e_w_in"], "cast_e_w_in_0", 0, 2), _pack([w[n] for n in SMALL_SHARDED])])
    hi, = _sc_gather_two_level("gather_a1", 12, [_cast_bf16(w["e_w_in"], "cast_e_w_in_1", 1, 2)])
    wg_e_in = (lo, hi)
    h0 = _pre0_fwd(x, row(w["e_pre_norm"]))
    wg_e_out, = _sc_gather_two_level("gather_b", 1, [_cast_bf16(w["e_w_out"], "cast_e_w_out")])
    wg_o_in, wg_o_out = _sc_gather_two_level(
        "gather_c", 13, [_cast_bf16(w[n], "cast_" + n) for n in ("o_w_in", "o_w_out")])
    h0_sib = _sc_sibling_exchange("swap_h0", 8, h0, h0.shape, lambda c, src, out: [(src, out)])
    p = {n: w[n] for n in SMALL if SMALL[n][1] is None}
    small_rows = small_rows.reshape(NDEV, -1)
    off = 0
    for n in SMALL_SHARDED:
        shp, ax = _shard_shape(n), SMALL[n][1]
        cnt = int(np.prod(shp))
        blk = small_rows[:, off:off + cnt].reshape((NDEV,) + shp)
        p[n] = jnp.moveaxis(blk, 0, ax).reshape(SMALL[n][0])
        off += cnt
    tabs = _rope_tables()
    pool_w_bf = p["e_pool_w"].astype(BF16)
    sgu_bb = jnp.broadcast_to(p["o_sgu_b"][:, :, None], (4, 128, 128))
    conv_w = jnp.concatenate([p["o_conv_w"], jnp.zeros((HALO - CONV_K, HALF), F32)], axis=0)
    odd_p = (row(p["o_sgu_norm_g"]), row(p["o_sgu_norm_b"]), p["o_sgu_w"], sgu_bb, conv_w,
             row(p["o_conv_b"]), row(p["o_conv_norm_g"]), row(p["o_conv_norm_b"]))

    z0 = _mm_in_halves(h0, wg_e_in, "mm_z0")
    ycat0 = _pool_fwd(z0, pool_w_bf, row(p["e_pool_scale"]))
    qkv = _qkv_prep(z0, tabs)
    ycat0, og, lg = _attn_fwd(z0, qkv, ycat0)
    w_out_e, w_out_o = wg_e_out.reshape(2048, D), wg_o_out.reshape(2048, D)
    y0, x1, h1 = _post0_fwd(ycat0, w_out_e, x, row(p["e_post_norm"]), row(p["o_pre_norm"]), h0_sib)
    h1_sib = _sc_sibling_exchange("swap_h1", 11, h1, h1.shape, lambda c, src, out: [(src, out)])
    z1 = _mm_in(h1, wg_o_in, "mm_z1")
    ycat1, conv_out = _odd_fwd(z1, *odd_p)

    g = {}
    loss, dx2, dy1, g["o_post_norm"] = _post1_bwd(ycat1, w_out_o, x1, target, row(p["o_post_norm"]), h1_sib)
    loss = lax.psum(loss[0, 0], ("x", "y", "c"))
    parts = {}
    dw = _mm_out_dw(ycat1, dy1, "mm_dwout1").reshape(NDEV, 256, D)
    parts["o_w_out"], = _sc_exchange("scatter_o_w_out", 2, [dw], True)
    dycat1 = _mm_out_dx(dy1, w_out_o, "mm_dycat1", (dw, loss.reshape(1, 1)))
    dz1, ddc, g["o_sgu_w"], d_sgu_bb, g["o_sgu_norm_g"], g["o_sgu_norm_b"], g["o_conv_norm_g"], \
        g["o_conv_norm_b"], g["o_conv_b"] = _odd_bwd_a(z1, conv_out, dycat1, *odd_p[:4], *odd_p[6:])
    dz1, d_conv_w = _odd_bwd_b(z1, ddc, dz1, conv_w)
    g["o_sgu_b"] = d_sgu_bb[:, :, 0]
    g["o_conv_w"] = d_conv_w[:CONV_K]
    grads, deltas, new_m, new_v = {}, {}, {}, {}

    def adam(n, dep):
        grads[n], deltas[n], new_m[n], new_v[n] = _adam_reduce(parts[n], w[n], m[n], v[n], "adam_" + n, dep)
        return new_v[n]

    pin = _arrived(parts["o_w_out"], "arrived_o_w_out", d_conv_w)
    dz1_sib = _swap_class_columns("swap_dz1", 10, dz1, ODD_IN // NDEV)
    dw = _mm_pair_dw(h1, dz1, h1_sib, dz1_sib, ODD_IN // NDEV, "mm_dwin1", pin)
    parts["o_w_in"] = _sc_chip_scatter("scatter_o_w_in", 3, dw)
    dh1 = _mm_in_dx(dz1, wg_o_in, "mm_dh1", dw)
    dx1, dy0, g["o_pre_norm"], g["e_post_norm"] = _mid_bwd(dx2, dh1, x1, y0, row(p["o_pre_norm"]),
                                                           row(p["e_post_norm"]))
    dw = _mm_out_dw(ycat0, dy0, "mm_dwout0").reshape(NDEV, 256, D)
    parts["e_w_out"], = _sc_exchange("scatter_e_w_out", 4, [dw], True)
    dycat0 = _mm_out_dx(dy0, w_out_e, "mm_dycat0", dw)
    da_in, da_gate, g["e_pool_w"], g["e_pool_scale"] = _pool_bwd(z0, dycat0, pool_w_bf, row(p["e_pool_scale"]))
    late = [n for n in SMALL if n not in ("e_pre_norm", "o_sgu_b")] + ["o_sgu_b"]
    recv_small, = _sc_gather_two_level("gather_small_grads", 6,
                                       [_pack([g[n].reshape(SMALL[n][0]) for n in late], 512)])
    took = _arrived(parts["o_w_in"], "arrived_o_w_in")
    dq, dk, dv, dbg = _attn_bwd(z0, qkv, og, lg, dycat0, tabs, took)
    dz0 = jnp.concatenate([da_in, da_gate, dq, dk, dv, dbg], axis=1)
    took = _arrived(recv_small, "arrived_small_grads", _arrived(parts["e_w_out"], "arrived_e_w_out", dz0))
    nb = EVEN_IN // NDEV
    swapped = [_swap_class_columns("swap_dz0_%d" % half, (9, 14)[half], dz0, nb, half, 2) for half in (0, 1)]
    dw, e_w_in_parts = took, []
    for half in (0, 1):
        dw = _mm_pair_dw(h0, dz0, h0_sib, swapped[half], nb, "mm_dwin0_%d" % half, dw, half, 2)
        e_w_in_parts.append(_sc_chip_scatter("scatter_e_w_in_%d" % half, (5, 15)[half], dw))
    pin = adam("e_w_out", adam("o_w_out", adam("o_w_in", dw)))
    rows = [int(np.prod(SMALL[n][0])) // 128 for n in late]
    summed = dict(zip(late, _sum_unpack(recv_small, rows, "sum_small_grads", pin)))
    dh0 = _mm_in_dx_halves(dz0, wg_e_in, "mm_dh0", summed[late[0]])
    grad_x, g["e_pre_norm"] = _pre0_bwd(dx1, dh0, x, row(p["e_pre_norm"]))
    last, = _sc_exchange("gather_e_pre_norm_grad", 7, [g["e_pre_norm"].reshape(16, 128)], False)

    n = "e_w_in"
    out = _adam_reduce(e_w_in_parts[0], w[n], m[n], v[n], "adam_e_w_in_0", grad_x, 0, 2)
    out = _adam_reduce(e_w_in_parts[1], w[n], m[n], v[n], "adam_e_w_in_1", None, 1, 2, out)
    grads[n], deltas[n], new_m[n], new_v[n] = out
    summed["e_pre_norm"] = _sum_parts(last, "sum_e_pre_norm_grad", out[3])
    names = list(SMALL)
    views = [_small_views(n) for n in names]
    mine = lambda src: [src[n].reshape(vw[0]) for n, vw in zip(names, views)]
    res = _adam_small(mine(w), [summed[n].reshape(vw[1]) for n, vw in zip(names, views)], [vw[2] for vw in views],
                      mine(m), mine(v), "adam_small")
    for n, out in zip(names, res):
        grads[n], deltas[n], new_m[n], new_v[n] = [t.reshape(_shard_shape(n)) for t in out]

    lead = lambda a: a[None]
    return (loss, grad_x[None], *[lead(grads[n]) for n in WEIGHTS], *[lead(deltas[n]) for n in WEIGHTS],
            *[lead(new_m[n]) for n in WEIGHTS], *[lead(new_v[n]) for n in WEIGHTS])
```

```python
import numpy as np
import jax
import jax.numpy as jnp
from jax import lax
from jax.experimental import pallas as pl
from jax.experimental.pallas import tpu as pltpu
from jax.experimental.pallas import tpu_sc as plsc

F32 = jnp.float32
BF16 = jnp.bfloat16

S = 2048
D = 2048
NDEV = 8
EPS = 1e-6
NEG = -1e30
HEAD_DIM = 128
ROT_DIM = 32
ROPE_THETA = 500000.0
PATTERNS = ((128, 1), (512, 4), (2048, 16))
BLK = 128
EVEN_IN = 12288
ODD_IN = 6144
HALF = 1024
CONV_K = 31
HALO = 32
TR = 256
SUB = 32

ADAM_LR = 0.001
ADAM_B1 = 0.9
ADAM_B2 = 0.999
ADAM_EPS = 1e-08
ADAM_WD = 0.01
ADAM_STEP = 10

VMEM_BIG = 56 * 1024 * 1024
MESH = pl.DeviceIdType.MESH

NN = (((1,), (0,)), ((), ()))
NT = (((1,), (1,)), ((), ()))
TN = (((0,), (0,)), ((), ()))


def _dot(a, b, dn=NN):
    return lax.dot_general(a, b, dn, preferred_element_type=F32)


def _sigmoid(x):
    return 1.0 / (1.0 + jnp.exp(-x))


def _silu_and_grad(x):
    sg = _sigmoid(x)
    return x * sg, sg * (1.0 + x * (1.0 - sg))


def _params(sem, vmem=None):
    return pltpu.CompilerParams(dimension_semantics=sem, vmem_limit_bytes=vmem)


ANY_SPEC = pl.BlockSpec(memory_space=pl.ANY)


def _matmul(a, b, *, dn, grid, a_spec, b_spec, o_spec, out_shape, out_dtype, acc_shape, name, dep=None):
    nk = grid[2]
    deps = [] if dep is None else list(dep) if isinstance(dep, (tuple, list)) else [dep]

    def body(a_ref, b_ref, *rest):
        o_ref, acc = rest[len(deps)], rest[len(deps) + 1:]
        if nk == 1:
            o_ref[...] = _dot(a_ref[...], b_ref[...], dn).astype(o_ref.dtype)
            return
        acc_ref = acc[0]
        k = pl.program_id(2)

        @pl.when(k == 0)
        def _():
            acc_ref[...] = jnp.zeros_like(acc_ref)

        acc_ref[...] += _dot(a_ref[...], b_ref[...], dn)

        @pl.when(k == nk - 1)
        def _():
            o_ref[...] = acc_ref[...].astype(o_ref.dtype)

    return pl.pallas_call(
        body, grid=grid, in_specs=[a_spec, b_spec] + [ANY_SPEC] * len(deps), out_specs=o_spec,
        out_shape=jax.ShapeDtypeStruct(out_shape, out_dtype),
        scratch_shapes=[] if nk == 1 else [pltpu.VMEM(acc_shape, F32)],
        compiler_params=_params(("parallel", "parallel", "arbitrary"), VMEM_BIG), name=name,
    )(a, b, *deps)


TM = 2048


def _mm_in(h, wg, name):
    nb = wg.shape[2]
    tn = 512 if nb % 512 == 0 else nb
    per = nb // tn
    return _matmul(
        h, wg, dn=NN, grid=(S // TM, NDEV * per, 1),
        a_spec=pl.BlockSpec((TM, D), lambda i, j, k: (i, 0)),
        b_spec=pl.BlockSpec((None, D, tn), lambda i, j, k: (j // per, 0, j % per)),
        o_spec=pl.BlockSpec((TM, tn), lambda i, j, k: (i, j)),
        out_shape=(S, NDEV * nb), out_dtype=F32, acc_shape=(TM, tn), name=name)


def _mm_in_halves(h, wg_halves, name):
    hb = wg_halves[0].shape[2]
    z = None
    for half, wg in enumerate(wg_halves):
        prev = [] if z is None else [z]

        def body(a_ref, b_ref, *rest):
            rest[-1][...] = _dot(a_ref[...], b_ref[...])

        z = pl.pallas_call(
            body, grid=(NDEV,),
            in_specs=[pl.BlockSpec((S, D), lambda j: (0, 0)), pl.BlockSpec((None, D, hb), lambda j: (j, 0, 0))]
                     + [ANY_SPEC] * len(prev),
            out_specs=pl.BlockSpec((S, hb), lambda j, half=half: (0, 2 * j + half)),
            out_shape=jax.ShapeDtypeStruct((S, 2 * NDEV * hb), F32),
            input_output_aliases={2: 0} if prev else {},
            compiler_params=_params(("parallel",), VMEM_BIG), name="%s_%d" % (name, half),
        )(h, wg, *prev)
    return z


def _mm_in_dx_halves(dz, wg_halves, name, dep):
    hb = wg_halves[0].shape[2]
    nk = 2 * NDEV

    def body(a_ref, b0_ref, b1_ref, dep_ref, o_ref, acc_ref):
        k = pl.program_id(2)

        @pl.when(k == 0)
        def _():
            acc_ref[...] = jnp.zeros_like(acc_ref)

        @pl.when(k % 2 == 0)
        def _():
            acc_ref[...] += _dot(a_ref[...], b0_ref[...], NT)

        @pl.when(k % 2 == 1)
        def _():
            acc_ref[...] += _dot(a_ref[...], b1_ref[...], NT)

        @pl.when(k == nk - 1)
        def _():
            o_ref[...] = acc_ref[...]

    b_spec = pl.BlockSpec((None, 1024, hb), lambda i, j, k: (k // 2, j, 0))
    return pl.pallas_call(
        body, grid=(1, D // 1024, nk),
        in_specs=[pl.BlockSpec((S, hb), lambda i, j, k: (0, k)), b_spec, b_spec, ANY_SPEC],
        out_specs=pl.BlockSpec((S, 1024), lambda i, j, k: (0, j)), out_shape=jax.ShapeDtypeStruct((S, D), F32),
        scratch_shapes=[pltpu.VMEM((S, 1024), F32)],
        compiler_params=_params(("parallel", "parallel", "arbitrary"), VMEM_BIG), name=name,
    )(dz, *wg_halves, dep)


def _mm_in_dx(dz, wg, name, dep=None):
    nb = wg.shape[2]
    return _matmul(
        dz, wg, dn=NT, grid=(S // TM, D // 1024, NDEV),
        a_spec=pl.BlockSpec((TM, nb), lambda i, j, k: (i, k)),
        b_spec=pl.BlockSpec((None, 1024, nb), lambda i, j, k: (k, j, 0)),
        o_spec=pl.BlockSpec((TM, 1024), lambda i, j, k: (i, j)),
        out_shape=(S, D), out_dtype=F32, acc_shape=(TM, 1024), name=name, dep=dep)


def _mm_out_dx(dy, w, name, dep=None):
    return _matmul(
        dy, w, dn=NT, grid=(S // TM, 2048 // 512, 1),
        a_spec=pl.BlockSpec((TM, D), lambda i, j, k: (i, 0)),
        b_spec=pl.BlockSpec((512, D), lambda i, j, k: (j, 0)),
        o_spec=pl.BlockSpec((TM, 512), lambda i, j, k: (i, j)),
        out_shape=(S, 2048), out_dtype=F32, acc_shape=(TM, 512), name=name, dep=dep)


def _mm_out_dw(yc, dy, name):
    return _matmul(
        yc, dy, dn=TN, grid=(2048 // TM, D // 512, 1),
        a_spec=pl.BlockSpec((S, TM), lambda i, j, k: (0, i)),
        b_spec=pl.BlockSpec((S, 512), lambda i, j, k: (0, j)),
        o_spec=pl.BlockSpec((TM, 512), lambda i, j, k: (i, j)),
        out_shape=(2048, D), out_dtype=BF16, acc_shape=(TM, 512), name=name)


def _row_spec(w=D):
    return pl.BlockSpec((TR, w), lambda i: (i, 0))


def _vec_spec(w=D):
    return pl.BlockSpec((1, w), lambda i: (0, 0))


def _rms_stats(x):
    r = lax.rsqrt(jnp.mean(x * x, axis=-1, keepdims=True) + EPS)
    return x * r, r


def _rms_bwd(dn, xhat, r, g):
    dxh = dn * g
    return r * (dxh - xhat * jnp.mean(dxh * xhat, axis=-1, keepdims=True))


def _acc_rows(ref, val, i):
    s = jnp.sum(val, axis=0, keepdims=True)

    @pl.when(i == 0)
    def _():
        ref[...] = s

    @pl.when(i > 0)
    def _():
        ref[...] += s


def _pre0_fwd(x, g):
    def body(x_ref, g_ref, h_ref, ht_ref):
        xhat, _ = _rms_stats(x_ref[...])
        h = xhat * g_ref[...]
        h_ref[...] = h.astype(BF16)
        ht_ref[...] = h.T.astype(BF16)

    return pl.pallas_call(
        body, grid=(S // TR,), in_specs=[_row_spec(), _vec_spec()],
        out_specs=[_row_spec(), pl.BlockSpec((D, TR), lambda i: (0, i))],
        out_shape=[jax.ShapeDtypeStruct((S, D), BF16), jax.ShapeDtypeStruct((D, S), BF16)],
        compiler_params=_params(("parallel",)), name="pre0_fwd",
    )(x, g)


def _post0_fwd(ycat, w_out, x, g_post, g_pre1, dep):
    def body(yc_ref, w_ref, x_ref, gp_ref, g1_ref, dep_ref, y_ref, x1_ref, h1_ref):
        y = _dot(yc_ref[...], w_ref[...])
        y_ref[...] = y
        yhat, _ = _rms_stats(y)
        x1 = x_ref[...] + yhat * gp_ref[...]
        x1_ref[...] = x1
        xhat, _ = _rms_stats(x1)
        h1_ref[...] = (xhat * g1_ref[...]).astype(BF16)

    return pl.pallas_call(
        body, grid=(S // TR,),
        in_specs=[_row_spec(), pl.BlockSpec((2048, D), lambda i: (0, 0)), _row_spec(), _vec_spec(), _vec_spec(),
                  ANY_SPEC],
        out_specs=[_row_spec(), _row_spec(), _row_spec()],
        out_shape=[jax.ShapeDtypeStruct((S, D), F32), jax.ShapeDtypeStruct((S, D), F32),
                   jax.ShapeDtypeStruct((S, D), BF16)],
        compiler_params=_params(("parallel",), VMEM_BIG), name="post0_fwd",
    )(ycat, w_out, x, g_post, g_pre1, dep)


def _post1_bwd(ycat, w_out, x1, target, g_post, dep):
    def body(yc_ref, w_ref, x1_ref, t_ref, g_ref, dep_ref, loss_ref, dx2_ref, dy_ref, dg_ref):
        i = pl.program_id(0)
        yhat, r = _rms_stats(_dot(yc_ref[...], w_ref[...]))
        g = g_ref[...]
        err = x1_ref[...] + yhat * g - t_ref[...]
        part = jnp.sum(jnp.sum(err * err, axis=-1, keepdims=True), axis=0, keepdims=True) * (0.5 / D)
        _acc_rows(loss_ref, jnp.broadcast_to(part, (1, 128)), i)
        dx2 = err * (1.0 / D)
        dx2_ref[...] = dx2
        _acc_rows(dg_ref, dx2 * yhat, i)
        dy_ref[...] = _rms_bwd(dx2, yhat, r, g).astype(BF16)

    return pl.pallas_call(
        body, grid=(S // TR,),
        in_specs=[_row_spec(), pl.BlockSpec((2048, D), lambda i: (0, 0)), _row_spec(), _row_spec(), _vec_spec(),
                  ANY_SPEC],
        out_specs=[_vec_spec(128), _row_spec(), _row_spec(), _vec_spec()],
        out_shape=[jax.ShapeDtypeStruct((1, 128), F32), jax.ShapeDtypeStruct((S, D), F32),
                   jax.ShapeDtypeStruct((S, D), BF16), jax.ShapeDtypeStruct((1, D), F32)],
        compiler_params=_params(("arbitrary",), VMEM_BIG), name="post1_bwd",
    )(ycat, w_out, x1, target, g_post, dep)


def _mid_bwd(dx2, dh1, x1, y0, g_pre1, g_post0):
    def body(dx2_ref, dh_ref, x1_ref, y_ref, g1_ref, gp_ref, dx1_ref, dy_ref, dg1_ref, dgp_ref):
        i = pl.program_id(0)
        xhat, r1 = _rms_stats(x1_ref[...])
        dh = dh_ref[...]
        _acc_rows(dg1_ref, dh * xhat, i)
        dx1 = dx2_ref[...] + _rms_bwd(dh, xhat, r1, g1_ref[...])
        dx1_ref[...] = dx1
        yhat, r0 = _rms_stats(y_ref[...])
        _acc_rows(dgp_ref, dx1 * yhat, i)
        dy_ref[...] = _rms_bwd(dx1, yhat, r0, gp_ref[...]).astype(BF16)

    return pl.pallas_call(
        body, grid=(S // TR,),
        in_specs=[_row_spec(), _row_spec(), _row_spec(), _row_spec(), _vec_spec(), _vec_spec()],
        out_specs=[_row_spec(), _row_spec(), _vec_spec(), _vec_spec()],
        out_shape=[jax.ShapeDtypeStruct((S, D), F32), jax.ShapeDtypeStruct((S, D), BF16),
                   jax.ShapeDtypeStruct((1, D), F32), jax.ShapeDtypeStruct((1, D), F32)],
        compiler_params=_params(("arbitrary",)), name="mid_bwd",
    )(dx2, dh1, x1, y0, g_pre1, g_post0)


def _pre0_bwd(dx1, dh0, x, g):
    def body(dx1_ref, dh_ref, x_ref, g_ref, gx_ref, dg_ref):
        i = pl.program_id(0)
        xhat, r = _rms_stats(x_ref[...])
        dh = dh_ref[...]
        _acc_rows(dg_ref, dh * xhat, i)
        gx_ref[...] = dx1_ref[...] + _rms_bwd(dh, xhat, r, g_ref[...])

    return pl.pallas_call(
        body, grid=(S // TR,), in_specs=[_row_spec(), _row_spec(), _row_spec(), _vec_spec()],
        out_specs=[_row_spec(), _vec_spec()],
        out_shape=[jax.ShapeDtypeStruct((S, D), F32), jax.ShapeDtypeStruct((1, D), F32)],
        compiler_params=_params(("arbitrary",)), name="pre0_bwd",
    )(dx1, dh0, x, g)


POOL_CH = 256


def _pool_apply(a, w, transpose):
    n = a.shape[0]
    row = lax.broadcasted_iota(jnp.int32, a.shape, 0)
    cnt = jnp.minimum(row + 1, w).astype(F32)
    s = a / cnt if transpose else a
    for k in (1, 2, 4, 8):
        if transpose:
            sh = jnp.where(row < n - k, pltpu.roll(s, n - k, 0), 0.0)
        else:
            sh = jnp.where(row >= k, pltpu.roll(s, k, 0), 0.0)
        s = jnp.where(w > k, s + sh, s)
    return s - a if transpose else s / cnt - a


def _pool_fwd(z0, pool_w, pool_scale):
    def body(a_ref, gate_ref, w_ref, sc_ref, out_ref):
        win = jnp.left_shift(2, pl.program_id(0))
        pooled = _pool_apply(a_ref[...], win, False)
        mixed = _dot(pooled.astype(BF16), w_ref[...])
        gate = gate_ref[...]
        out_ref[...] = (mixed * sc_ref[...] * (gate * _sigmoid(gate))).astype(BF16)

    return pl.pallas_call(
        body, grid=(4,),
        in_specs=[pl.BlockSpec((S, POOL_CH), lambda g: (0, g)), pl.BlockSpec((S, POOL_CH), lambda g: (0, 4 + g)),
                  pl.BlockSpec((None, POOL_CH, POOL_CH), lambda g: (g, 0, 0)),
                  pl.BlockSpec((1, POOL_CH), lambda g: (0, g))],
        out_specs=pl.BlockSpec((S, POOL_CH), lambda g: (0, g)),
        out_shape=jax.ShapeDtypeStruct((S, 2048), BF16),
        compiler_params=_params(("parallel",), VMEM_BIG), name="pool_fwd",
    )(z0, z0, pool_w, pool_scale)


def _pool_bwd(z0, dycat, pool_w, pool_scale):
    def body(a_ref, gate_ref, dy_ref, w_ref, sc_ref, da_ref, dgate_ref, dw_ref, dsc_ref):
        win = jnp.left_shift(2, pl.program_id(0))
        pooled = _pool_apply(a_ref[...], win, False).astype(BF16)
        w = w_ref[...]
        mixed = _dot(pooled, w)
        silu, dsilu = _silu_and_grad(gate_ref[...])
        dy = dy_ref[...]
        sc = sc_ref[...]
        dgate_ref[...] = (dy * (mixed * sc) * dsilu).astype(BF16)
        dms = dy * silu
        dsc_ref[...] = jnp.sum(dms * mixed, axis=0, keepdims=True)
        dmixed = (dms * sc).astype(BF16)
        dw_ref[...] = _dot(pooled, dmixed, TN)
        dpooled = _dot(dmixed, w, NT)
        da_ref[...] = _pool_apply(dpooled, win, True).astype(BF16)

    slab = lambda off: pl.BlockSpec((S, POOL_CH), lambda g: (0, off + g))
    return pl.pallas_call(
        body, grid=(4,),
        in_specs=[slab(0), slab(4), slab(0), pl.BlockSpec((None, POOL_CH, POOL_CH), lambda g: (g, 0, 0)),
                  pl.BlockSpec((1, POOL_CH), lambda g: (0, g))],
        out_specs=[slab(0), slab(0), pl.BlockSpec((None, POOL_CH, POOL_CH), lambda g: (g, 0, 0)),
                   pl.BlockSpec((1, POOL_CH), lambda g: (0, g))],
        out_shape=[jax.ShapeDtypeStruct((S, HALF), BF16), jax.ShapeDtypeStruct((S, HALF), BF16),
                   jax.ShapeDtypeStruct((4, POOL_CH, POOL_CH), F32), jax.ShapeDtypeStruct((1, HALF), F32)],
        compiler_params=_params(("parallel",), VMEM_BIG), name="pool_bwd",
    )(z0, z0, dycat, pool_w, pool_scale)


Q_COL, K_COL, V_COL, BG_COL = 2048 // 128, 5120 // 128, 8192 // 128, 11264 // 128
SCALE = HEAD_DIM ** -0.5


def _rope_tables():
    pos = jnp.arange(S, dtype=F32)
    inv_freq = jnp.power(ROPE_THETA, -jnp.arange(0, ROT_DIM, 2, dtype=F32) / ROT_DIM)
    ang = pos[:, None] * inv_freq[None, :]
    cos, sin = jnp.cos(ang), jnp.sin(ang)
    half = ROT_DIM // 2
    zeros = jnp.zeros((S, HEAD_DIM - ROT_DIM), F32)
    c = jnp.concatenate([cos, cos, jnp.ones((S, HEAD_DIM - ROT_DIM), F32)], axis=1)
    a = jnp.concatenate([-sin, jnp.zeros((S, half), F32), zeros], axis=1)
    b = jnp.concatenate([jnp.zeros((S, half), F32), sin, zeros], axis=1)
    return c, a, b


def _rope(t, c, a, b):
    half = ROT_DIM // 2
    return t * c + pltpu.roll(t, HEAD_DIM - half, 1) * a + pltpu.roll(t, half, 1) * b


def _rope_t(d, c, a, b):
    half = ROT_DIM // 2
    return d * c + pltpu.roll(d * a, half, 1) + pltpu.roll(d * b, HEAD_DIM - half, 1)


def _deinterleave(dst, src, dil, cast=None, dst_off=0):
    length = S // dil
    for r in range(dil):
        v = src[...] if dil == 1 else src[pl.ds(r, length, stride=dil), :]
        dst[dst_off + r * length:dst_off + (r + 1) * length, :] = v if cast is None else v.astype(cast)


def _interleave(dst, src, dil, src_off=0):
    length = S // dil
    for r in range(dil):
        if dil == 1:
            dst[...] = src[src_off:src_off + S, :]
        else:
            dst[pl.ds(r, length, stride=dil), :] = src[src_off + r * length:src_off + (r + 1) * length, :]


CU = 8
NUNITS = S // BLK
B_QK = (((2,), (2,)), ((0,), (0,)))
B_PV = (((2,), (1,)), ((0,), (0,)))
B_TN = (((1,), (1,)), ((0,), (0,)))


def _blocks(ref, first):
    return ref[first * BLK:(first + CU) * BLK, :].reshape(CU, BLK, HEAD_DIM)


def _chunk_scores(u0, nb, qd, kdp):
    q = _blocks(qd, u0)
    row = lax.broadcasted_iota(jnp.int32, (CU, BLK, BLK), 1)
    col = lax.broadcasted_iota(jnp.int32, (CU, BLK, BLK), 2)
    s_own = jnp.where(col <= row, _dot(q, _blocks(kdp, u0 + 1), B_QK) * SCALE, NEG)
    if nb == 1:
        return q, s_own, None
    unit = lax.broadcasted_iota(jnp.int32, (CU, BLK, BLK), 0) + u0
    s_prev = jnp.where((col >= row) & ((unit % nb) != 0), _dot(q, _blocks(kdp, u0), B_QK) * SCALE, NEG)
    return q, s_own, s_prev


def _qkv_prep(z0, tabs):
    def body(q_ref, k_ref, v_ref, c_ref, a_ref, b_ref, qo_ref, ko_ref, vo_ref, tmp):
        p = pl.program_id(1)
        for gi, (_, dil) in enumerate(PATTERNS):
            @pl.when(p == gi)
            def _(dil=dil):
                c, a, b = c_ref[...], a_ref[...], b_ref[...]
                tmp[...] = _rope(q_ref[...], c, a, b)
                _deinterleave(qo_ref, tmp, dil, BF16)
                tmp[...] = _rope(k_ref[...], c, a, b)
                _deinterleave(ko_ref, tmp, dil, BF16)
                _deinterleave(vo_ref, v_ref, dil, BF16)

    tab = pl.BlockSpec((S, HEAD_DIM), lambda h, p: (0, 0))
    out = pl.BlockSpec((S, HEAD_DIM), lambda h, p: (0, p * 8 + h))
    return pl.pallas_call(
        body, grid=(8, 3), in_specs=[_head_spec(Q_COL), _head_spec(K_COL), _head_spec(V_COL), tab, tab, tab],
        out_specs=[out, out, out], out_shape=[jax.ShapeDtypeStruct((S, 3072), BF16)] * 3,
        scratch_shapes=[pltpu.VMEM((S, HEAD_DIM), F32)],
        compiler_params=_params(("parallel", "arbitrary"), VMEM_BIG), name="qkv_prep",
    )(z0, z0, z0, *tabs)


def _pad_copy(dst, src):
    dst[0:BLK, :] = jnp.zeros((BLK, HEAD_DIM), dst.dtype)
    dst[BLK:BLK + S, :] = src[...]


def _attn_group_fwd(dil, qd, kd_ref, vd_ref, kdp, vdp, od, ld, og, lg):
    nb = S // dil // BLK
    _pad_copy(kdp, kd_ref)
    _pad_copy(vdp, vd_ref)
    for u0 in range(0, NUNITS, CU):
        _, s_own, s_prev = _chunk_scores(u0, nb, qd, kdp)
        m = jnp.max(s_own, axis=2, keepdims=True)
        if s_prev is not None:
            m = jnp.maximum(m, jnp.max(s_prev, axis=2, keepdims=True))
        p_own = jnp.exp(s_own - m)
        den = jnp.sum(p_own, axis=2, keepdims=True)
        acc = _dot(p_own.astype(BF16), _blocks(vdp, u0 + 1), B_PV)
        if s_prev is not None:
            p_prev = jnp.exp(s_prev - m)
            den = den + jnp.sum(p_prev, axis=2, keepdims=True)
            acc = acc + _dot(p_prev.astype(BF16), _blocks(vdp, u0), B_PV)
        rows = slice(u0 * BLK, (u0 + CU) * BLK)
        od[rows, :] = (acc / den).reshape(CU * BLK, HEAD_DIM)
        ld[rows, :] = jnp.broadcast_to(m + jnp.log(den), (CU, BLK, HEAD_DIM)).reshape(CU * BLK, HEAD_DIM)
    _interleave(og, od, dil)
    _interleave(lg, ld, dil)


def _group_weights(lgs):
    l0, l1, l2 = lgs[0][...], lgs[1][...], lgs[2][...]
    mx = jnp.maximum(l0, jnp.maximum(l1, l2))
    e0, e1, e2 = jnp.exp(l0 - mx), jnp.exp(l1 - mx), jnp.exp(l2 - mx)
    den = e0 + e1 + e2
    return e0 / den, e1 / den, e2 / den


def _head_spec(base):
    return pl.BlockSpec((S, HEAD_DIM), lambda h, p: (0, base + (p % 3) * 8 + h))


def _slab(dtype=F32, rows=S):
    return pltpu.VMEM((rows, HEAD_DIM), dtype)


def _attn_fwd(z0, qkv, ycat):
    def body(q_ref, k_ref, v_ref, gate_ref, ycat_ref, out_ref, og_ref, lg_ref,
             kdp, vdp, od, ld, og0, og1, og2, lg0, lg1, lg2):
        del ycat_ref
        p = pl.program_id(1)
        ogs, lgs = (og0, og1, og2), (lg0, lg1, lg2)
        for gi, (_, dil) in enumerate(PATTERNS):
            @pl.when(p == gi)
            def _(gi=gi, dil=dil):
                _attn_group_fwd(dil, q_ref, k_ref, v_ref, kdp, vdp, od, ld, ogs[gi], lgs[gi])
                og_ref[...] = ogs[gi][...]
                lg_ref[...] = lgs[gi][...]

        @pl.when(p == 2)
        def _():
            w0, w1, w2 = _group_weights(lgs)
            o = w0 * og0[...] + w1 * og1[...] + w2 * og2[...]
            gate = gate_ref[...]
            out_ref[...] = (o * (gate * _sigmoid(gate))).astype(BF16)

    grp = pl.BlockSpec((S, HEAD_DIM), lambda h, p: (0, p * 8 + h))
    return pl.pallas_call(
        body, grid=(8, 3),
        in_specs=[grp, grp, grp, pl.BlockSpec((S, HEAD_DIM), lambda h, p: (0, BG_COL + h)), ANY_SPEC],
        out_specs=[pl.BlockSpec((S, HEAD_DIM), lambda h, p: (0, 8 + h)), grp, grp],
        out_shape=[jax.ShapeDtypeStruct((S, 2048), BF16), jax.ShapeDtypeStruct((S, 3072), F32),
                   jax.ShapeDtypeStruct((S, 3072), F32)],
        scratch_shapes=[_slab(BF16, S + BLK), _slab(BF16, S + BLK)] + [_slab() for _ in range(8)],
        input_output_aliases={4: 0},
        compiler_params=_params(("parallel", "arbitrary"), VMEM_BIG), name="attn_fwd",
    )(*qkv, z0, ycat)


def _attn_bwd(z0, qkv, og, lg, dycat, tabs, dep):
    def body(q_ref, k_ref, v_ref, gate_ref, dy_ref, c_ref, a_ref, b_ref,
             og0_ref, og1_ref, og2_ref, lg0_ref, lg1_ref, lg2_ref, dep_ref,
             dq_ref, dk_ref, dv_ref, dbg_ref,
             tmp, kd, vd, ld, dg0, dg1, dg2, cg0, cg1, cg2, dod, cd, dqd, dkd, dvd):
        p = pl.program_id(1)
        ogs, lgs, dgs, cgs = (og0_ref, og1_ref, og2_ref), (lg0_ref, lg1_ref, lg2_ref), (dg0, dg1, dg2), (cg0, cg1, cg2)

        @pl.when(p == 0)
        def _():
            w = _group_weights(lgs)
            o = w[0] * ogs[0][...] + w[1] * ogs[1][...] + w[2] * ogs[2][...]
            silu, dsilu = _silu_and_grad(gate_ref[...])
            dy = dy_ref[...]
            dbg_ref[...] = (dy * o * dsilu).astype(BF16)
            do = dy * silu
            dwbar = jnp.sum(do * o, axis=1, keepdims=True)
            for gi in range(3):
                dgs[gi][...] = w[gi] * do
                cgs[gi][...] = -w[gi] * dwbar

        for gi, (_, dil) in enumerate(PATTERNS):
            @pl.when(p == 1 + gi)
            def _(gi=gi, dil=dil):
                nb = S // dil // BLK
                qd = q_ref
                c, a, b = c_ref[...], a_ref[...], b_ref[...]
                _pad_copy(kd, k_ref)
                _pad_copy(vd, v_ref)
                _deinterleave(dod, dgs[gi], dil, BF16)
                _deinterleave(ld, lgs[gi], dil)
                _deinterleave(cd, cgs[gi], dil)
                dkd[...] = jnp.zeros_like(dkd)
                dvd[...] = jnp.zeros_like(dvd)
                flat = lambda t: t.reshape(CU * BLK, HEAD_DIM)
                for u0 in range(0, NUNITS, CU):
                    q, s_own, s_prev = _chunk_scores(u0, nb, qd, kd)
                    lse, cv, do = _blocks(ld, u0), _blocks(cd, u0), _blocks(dod, u0)
                    own = slice((u0 + 1) * BLK, (u0 + 1 + CU) * BLK)
                    p_own = jnp.exp(s_own - lse)
                    ds_own = (p_own * (_dot(do, _blocks(vd, u0 + 1), B_QK) + cv) * SCALE).astype(BF16)
                    dq = _dot(ds_own, _blocks(kd, u0 + 1), B_PV)
                    dkd[own, :] += flat(_dot(ds_own, q, B_TN))
                    dvd[own, :] += flat(_dot(p_own.astype(BF16), do, B_TN))
                    if s_prev is not None:
                        prev = slice(u0 * BLK, (u0 + CU) * BLK)
                        p_prev = jnp.exp(s_prev - lse)
                        ds_prev = (p_prev * (_dot(do, _blocks(vd, u0), B_QK) + cv) * SCALE).astype(BF16)
                        dq = dq + _dot(ds_prev, _blocks(kd, u0), B_PV)
                        dkd[prev, :] += flat(_dot(ds_prev, q, B_TN))
                        dvd[prev, :] += flat(_dot(p_prev.astype(BF16), do, B_TN))
                    dqd[u0 * BLK:(u0 + CU) * BLK, :] = flat(dq)
                _interleave(tmp, dqd, dil)
                dq_ref[...] = _rope_t(tmp[...], c, a, b).astype(BF16)
                _interleave(tmp, dkd, dil, BLK)
                dk_ref[...] = _rope_t(tmp[...], c, a, b).astype(BF16)
                _interleave(tmp, dvd, dil, BLK)
                dv_ref[...] = tmp[...].astype(BF16)

    tab = pl.BlockSpec((S, HEAD_DIM), lambda h, p: (0, 0))
    hspec = lambda base: pl.BlockSpec((S, HEAD_DIM), lambda h, p: (0, base + h))
    gspec = pl.BlockSpec((S, HEAD_DIM), lambda h, p: (0, jnp.maximum(p - 1, 0) * 8 + h))
    return pl.pallas_call(
        body, grid=(8, 4),
        in_specs=[gspec, gspec, gspec, hspec(BG_COL), hspec(8), tab, tab, tab,
                  hspec(0), hspec(8), hspec(16), hspec(0), hspec(8), hspec(16), ANY_SPEC],
        out_specs=[gspec, gspec, gspec, hspec(0)],
        out_shape=[jax.ShapeDtypeStruct((S, 3072), BF16)] * 3 + [jax.ShapeDtypeStruct((S, HALF), BF16)],
        scratch_shapes=[_slab(), _slab(BF16, S + BLK), _slab(BF16, S + BLK), _slab()] + [_slab() for _ in range(6)]
                       + [_slab(BF16), _slab(), _slab(), _slab(F32, S + BLK), _slab(F32, S + BLK)],
        compiler_params=_params(("parallel", "arbitrary"), VMEM_BIG), name="attn_bwd",
    )(*qkv, z0, dycat, *tabs, og, og, og, lg, lg, lg, dep)


SGU_CH = 256
NCHUNK = TR // 128


def _ln_stats(x):
    mu = jnp.mean(x, axis=-1, keepdims=True)
    xc = x - mu
    r = lax.rsqrt(jnp.mean(xc * xc, axis=-1, keepdims=True) + EPS)
    return xc * r, r


def _ln_bwd(dy, xhat, r, g):
    dxh = dy * g
    return r * (dxh - jnp.mean(dxh, axis=-1, keepdims=True) - xhat * jnp.mean(dxh * xhat, axis=-1, keepdims=True))


def _tril_bf16(w):
    row = lax.broadcasted_iota(jnp.int32, w.shape, 0)
    col = lax.broadcasted_iota(jnp.int32, w.shape, 1)
    return jnp.where(row >= col, w, 0.0).astype(BF16)


def _sgu_gate(vn_s, s_s, w_ref, bb_ref):
    for h in range(4):
        wm = _tril_bf16(w_ref[h])
        bias = bb_ref[h]
        for ch in range(NCHUNK):
            rows, cols = slice(ch * 128, (ch + 1) * 128), slice(h * SGU_CH, (h + 1) * SGU_CH)
            s_s[rows, cols] = _dot(wm, vn_s[rows, cols]) + jnp.concatenate([bias, bias], axis=1)


WIN = HALO + TR
SUBL = 8


def _shifted_copies(dst, src):
    dst[0] = src[...]
    for b in range(1, SUBL):
        dst[b, 0:WIN - SUBL, :] = src[pl.ds(b, WIN - SUBL), :]


def _rows_at(copies, off, n):
    return copies[off % SUBL, pl.ds(off - off % SUBL, n), :]


def _conv_fwd(i, dval_ref, dglu_ref, hval_ref, hglu_ref, cw_ref, cb_ref, xw, xr, dcs):
    halo = hval_ref[...] * _sigmoid(hglu_ref[...])
    xw[0:HALO, :] = jnp.where(i > 0, halo, 0.0)
    xw[HALO:HALO + TR, :] = dval_ref[...] * _sigmoid(dglu_ref[...])
    _shifted_copies(xr, xw)
    for rb in range(TR // SUB):
        acc = jnp.broadcast_to(cb_ref[...], (SUB, HALF))
        for k in range(CONV_K):
            acc = acc + cw_ref[k:k + 1, :] * _rows_at(xr, rb * SUB + HALO - (CONV_K - 1) + k, SUB)
        dcs[rb * SUB:(rb + 1) * SUB, :] = acc


def _odd_in_specs():
    col = lambda j: pl.BlockSpec((TR, HALF), lambda i, *_: (i, j))
    prev = lambda j: pl.BlockSpec((HALO, HALF), lambda i, *_: (jnp.maximum(i * (TR // HALO) - 1, 0), j))
    return [col(0), col(1), col(2), col(3), col(4), col(5), prev(3), prev(4)]


def _full_spec(shape):
    return pl.BlockSpec(shape, lambda i, *_: (0,) * len(shape))


def _odd_fwd(z1, sgu_g, sgu_b, sgu_w, sgu_bb, conv_w, conv_b, cn_g, cn_b):
    def body(u_ref, v_ref, cg_ref, dval_ref, dglu_ref, dgate_ref, hval_ref, hglu_ref,
             g_ref, b_ref, w_ref, bb_ref, cw_ref, cb_ref, cng_ref, cnb_ref, out_ref, dcs, vn_s, s_s, xw, xr):
        i = pl.program_id(0)
        vhat, _ = _ln_stats(v_ref[...])
        vn_s[...] = (vhat * g_ref[...] + b_ref[...]).astype(BF16)
        _sgu_gate(vn_s, s_s, w_ref, bb_ref)
        cg = cg_ref[...]
        out_ref[:, 0:HALF] = (u_ref[...] * s_s[...] * (cg * _sigmoid(cg))).astype(BF16)
        _conv_fwd(i, dval_ref, dglu_ref, hval_ref, hglu_ref, cw_ref, cb_ref, xw, xr, dcs)
        dhat, _ = _ln_stats(dcs[...])
        dn = dhat * cng_ref[...] + cnb_ref[...]
        dgate = dgate_ref[...]
        out_ref[:, HALF:2 * HALF] = ((dn * _sigmoid(dn)) * (dgate * _sigmoid(dgate))).astype(BF16)

    vec = _full_spec((1, HALF))
    return pl.pallas_call(
        body, grid=(S // TR,),
        in_specs=_odd_in_specs() + [vec, vec, _full_spec((4, 128, 128)), _full_spec((4, 128, 128)),
                                    _full_spec((HALO, HALF)), vec, vec, vec],
        out_specs=[pl.BlockSpec((TR, 2048), lambda i: (i, 0)), pl.BlockSpec((TR, HALF), lambda i: (i, 0))],
        out_shape=[jax.ShapeDtypeStruct((S, 2048), BF16), jax.ShapeDtypeStruct((S, HALF), F32)],
        scratch_shapes=[pltpu.VMEM((TR, HALF), BF16), pltpu.VMEM((TR, HALF), F32),
                        pltpu.VMEM((WIN, HALF), F32), pltpu.VMEM((SUBL, WIN, HALF), F32)],
        compiler_params=_params(("parallel",), VMEM_BIG), name="odd_fwd",
    )(z1, z1, z1, z1, z1, z1, z1, z1, sgu_g, sgu_b, sgu_w, sgu_bb, conv_w, conv_b, cn_g, cn_b)


def _odd_bwd_a(z1, dc, dycat, sgu_g, sgu_b, sgu_w, sgu_bb, cn_g, cn_b):
    def body(u_ref, v_ref, cg_ref, dgate_ref, dcs, dy_ref, g_ref, b_ref, w_ref, bb_ref, cng_ref, cnb_ref,
             dz_ref, ddc_ref, dw_ref, dbb_ref, dg_ref, db_ref, dcng_ref, dcnb_ref, dcb_ref,
             vn_s, s_s, ds_s, dvn_s):
        i = pl.program_id(0)
        vhat, rv = _ln_stats(v_ref[...])
        g = g_ref[...]
        vn_s[...] = (vhat * g + b_ref[...]).astype(BF16)
        _sgu_gate(vn_s, s_s, w_ref, bb_ref)
        silu_c, dsilu_c = _silu_and_grad(cg_ref[...])
        dyc = dy_ref[:, 0:HALF]
        u = u_ref[...]
        s = s_s[...]
        dz_ref[:, 0:HALF] = (dyc * s * silu_c).astype(BF16)
        dz_ref[:, 2 * HALF:3 * HALF] = (dyc * u * s * dsilu_c).astype(BF16)
        ds_s[...] = dyc * u * silu_c

        @pl.when(i == 0)
        def _():
            dw_ref[...] = jnp.zeros_like(dw_ref)
            dbb_ref[...] = jnp.zeros_like(dbb_ref)

        tril = lax.broadcasted_iota(jnp.int32, (128, 128), 0) >= lax.broadcasted_iota(jnp.int32, (128, 128), 1)
        for h in range(4):
            wm = _tril_bf16(w_ref[h])
            for ch in range(NCHUNK):
                rows, cols = slice(ch * 128, (ch + 1) * 128), slice(h * SGU_CH, (h + 1) * SGU_CH)
                ds = ds_s[rows, cols]
                dsb = ds.astype(BF16)
                dw_ref[h] += jnp.where(tril, _dot(dsb, vn_s[rows, cols], NT), 0.0)
                dbb_ref[h] += jnp.broadcast_to(jnp.sum(ds, axis=1, keepdims=True), (128, 128))
                dvn_s[rows, cols] = _dot(wm, dsb, TN)
        dvn = dvn_s[...]
        _acc_rows(dg_ref, dvn * vhat, i)
        _acc_rows(db_ref, dvn, i)
        dz_ref[:, HALF:2 * HALF] = _ln_bwd(dvn, vhat, rv, g).astype(BF16)

        dhat, rd = _ln_stats(dcs[...])
        cng = cng_ref[...]
        silu_n, dsilu_n = _silu_and_grad(dhat * cng + cnb_ref[...])
        silu_g, dsilu_g = _silu_and_grad(dgate_ref[...])
        dyd = dy_ref[:, HALF:2 * HALF]
        dz_ref[:, 5 * HALF:6 * HALF] = (dyd * silu_n * dsilu_g).astype(BF16)
        ddn = dyd * silu_g * dsilu_n
        _acc_rows(dcng_ref, ddn * dhat, i)
        _acc_rows(dcnb_ref, ddn, i)
        ddc = _ln_bwd(ddn, dhat, rd, cng)
        ddc_ref[...] = ddc
        _acc_rows(dcb_ref, ddc, i)

    vec = _full_spec((1, HALF))
    sq = _full_spec((4, 128, 128))
    col = lambda j: pl.BlockSpec((TR, HALF), lambda i: (i, j))
    return pl.pallas_call(
        body, grid=(S // TR,),
        in_specs=[col(0), col(1), col(2), col(5), col(0), pl.BlockSpec((TR, 2048), lambda i: (i, 0)),
                  vec, vec, sq, sq, vec, vec],
        out_specs=[pl.BlockSpec((TR, ODD_IN), lambda i: (i, 0)), pl.BlockSpec((TR, HALF), lambda i: (i, 0)),
                   sq, sq, vec, vec, vec, vec, vec],
        out_shape=[jax.ShapeDtypeStruct((S, ODD_IN), BF16), jax.ShapeDtypeStruct((S, HALF), F32),
                   jax.ShapeDtypeStruct((4, 128, 128), F32), jax.ShapeDtypeStruct((4, 128, 128), F32)]
                  + [jax.ShapeDtypeStruct((1, HALF), F32)] * 5,
        scratch_shapes=[pltpu.VMEM((TR, HALF), BF16), pltpu.VMEM((TR, HALF), F32),
                        pltpu.VMEM((TR, HALF), F32), pltpu.VMEM((TR, HALF), F32)],
        compiler_params=_params(("arbitrary",), VMEM_BIG), name="odd_bwd_a",
    )(z1, z1, z1, z1, dc, dycat, sgu_g, sgu_b, sgu_w, sgu_bb, cn_g, cn_b)


def _odd_bwd_b(z1, ddc, dz1, conv_w):
    nt = S // TR

    def body(dval_ref, dglu_ref, hval_ref, hglu_ref, ddc_ref, hddc_ref, cw_ref, dz_in_ref,
             dz_ref, dcw_ref, xw, dwin, dxs, xr, dr):
        del dz_in_ref
        i, j = pl.program_id(0), pl.program_id(1)
        sg = _sigmoid(dglu_ref[...])
        dval = dval_ref[...]

        @pl.when(j == 0)
        def _():
            halo = hval_ref[...] * _sigmoid(hglu_ref[...])
            xw[0:HALO, :] = jnp.where(i > 0, halo, 0.0)
            xw[HALO:HALO + TR, :] = dval * sg
            dwin[0:TR, :] = ddc_ref[...]
            dwin[TR:TR + HALO, :] = jnp.where(i < nt - 1, hddc_ref[...], 0.0)
            _shifted_copies(xr, xw)
            _shifted_copies(dr, dwin)

            @pl.when(i == 0)
            def _():
                dcw_ref[...] = jnp.zeros_like(dcw_ref)

            for rb in range(TR // SUB):
                acc = jnp.zeros((SUB, HALF), F32)
                for k in range(CONV_K):
                    acc = acc + cw_ref[k:k + 1, :] * _rows_at(dr, rb * SUB + (CONV_K - 1) - k, SUB)
                dxs[rb * SUB:(rb + 1) * SUB, :] = acc
            for k in range(CONV_K):
                acc = jnp.zeros((SUB, HALF), F32)
                for rb in range(TR // SUB):
                    acc = acc + dwin[rb * SUB:(rb + 1) * SUB, :] * _rows_at(xr, rb * SUB + HALO - (CONV_K - 1) + k, SUB)
                dcw_ref[k:k + 1, :] += jnp.sum(acc, axis=0, keepdims=True)
            dz_ref[...] = (dxs[...] * sg).astype(BF16)

        @pl.when(j == 1)
        def _():
            dz_ref[...] = (dxs[...] * dval * sg * (1.0 - sg)).astype(BF16)

    col = lambda c: pl.BlockSpec((TR, HALF), lambda i, j: (i, c))
    prev = lambda c: pl.BlockSpec((HALO, HALF), lambda i, j: (jnp.maximum(i * (TR // HALO) - 1, 0), c))
    nxt = pl.BlockSpec((HALO, HALF), lambda i, j: (jnp.minimum((i + 1) * (TR // HALO), S // HALO - 1), 0))
    return pl.pallas_call(
        body, grid=(nt, 2),
        in_specs=[col(3), col(4), prev(3), prev(4), pl.BlockSpec((TR, HALF), lambda i, j: (i, 0)), nxt,
                  _full_spec((HALO, HALF)), pl.BlockSpec(memory_space=pl.ANY)],
        out_specs=[pl.BlockSpec((TR, HALF), lambda i, j: (i, 3 + j)), _full_spec((HALO, HALF))],
        out_shape=[jax.ShapeDtypeStruct((S, ODD_IN), BF16), jax.ShapeDtypeStruct((HALO, HALF), F32)],
        scratch_shapes=[pltpu.VMEM((WIN, HALF), F32), pltpu.VMEM((WIN, HALF), F32), pltpu.VMEM((TR, HALF), F32),
                        pltpu.VMEM((SUBL, WIN, HALF), F32), pltpu.VMEM((SUBL, WIN, HALF), F32)],
        input_output_aliases={7: 0},
        compiler_params=_params(("arbitrary", "arbitrary"), VMEM_BIG), name="odd_bwd_b",
    )(z1, z1, z1, z1, ddc, ddc, conv_w, dz1)


def _cast_bf16(w, name, piece=0, npieces=1):
    r, c = w.shape[0], w.shape[1] // npieces
    tr = min(r, 256)

    def body(i_ref, o_ref):
        o_ref[...] = i_ref[...].astype(BF16)

    return pl.pallas_call(
        body, grid=(r // tr,), in_specs=[pl.BlockSpec((tr, c), lambda i: (i, piece))],
        out_specs=pl.BlockSpec((tr, c), lambda i: (i, 0)), out_shape=jax.ShapeDtypeStruct((r, c), BF16),
        compiler_params=_params(("parallel",)), name=name,
    )(w)


def _adamw(w, g, m, v):
    m = ADAM_B1 * m + (1.0 - ADAM_B1) * g
    v = ADAM_B2 * v + (1.0 - ADAM_B2) * (g * g)
    m_hat = m / (1.0 - ADAM_B1 ** ADAM_STEP)
    v_hat = v / (1.0 - ADAM_B2 ** ADAM_STEP)
    delta = -ADAM_LR * (m_hat / (jnp.sqrt(v_hat) + ADAM_EPS) + ADAM_WD * w)
    return delta, m, v


def _adam_reduce(parts, w, m, v, name, dep=None, piece=0, npieces=1, prev=None):
    r, c = w.shape
    cp = c // npieces
    tr = min(r, 128)
    extra = ([] if dep is None else [dep]) + ([] if prev is None else list(prev))
    nparts = parts.shape[0]

    def body(p_ref, w_ref, m_ref, v_ref, *rest):
        g_ref, d_ref, nm_ref, nv_ref = rest[len(extra):]
        g = p_ref[0].astype(F32)
        for d in range(1, nparts):
            g = g + p_ref[d].astype(F32)
        g_ref[...] = g
        d_ref[...], nm_ref[...], nv_ref[...] = _adamw(w_ref[...], g, m_ref[...], v_ref[...])

    spec = pl.BlockSpec((tr, cp), lambda i: (i, piece))
    first = 4 + (0 if dep is None else 1)
    return pl.pallas_call(
        body, grid=(r // tr,),
        in_specs=[pl.BlockSpec((nparts, tr, cp), lambda i: (0, i, 0)), spec, spec, spec] + [ANY_SPEC] * len(extra),
        out_specs=[spec] * 4, out_shape=[jax.ShapeDtypeStruct((r, c), F32)] * 4,
        input_output_aliases={} if prev is None else {first + k: k for k in range(4)},
        compiler_params=_params(("parallel",), VMEM_BIG), name=name,
    )(parts, w, m, v, *extra)


def _arrived(x, name, dep=None):
    deps = [] if dep is None else [dep]

    def body(*refs):
        refs[-1][...] = jnp.zeros_like(refs[-1])

    return pl.pallas_call(
        body, in_specs=[ANY_SPEC] * (1 + len(deps)), out_specs=pl.BlockSpec(memory_space=pltpu.VMEM),
        out_shape=jax.ShapeDtypeStruct((8, 128), F32), name=name,
    )(x, *deps)


def _sum_parts(parts, name, dep=None):
    r = parts.shape[1]
    tr = 8
    for cand in (512, 256, 128, 64, 32, 16, 8):
        if r % cand == 0:
            tr = cand
            break
    deps = [] if dep is None else [dep]

    def body(p_ref, *rest):
        g = p_ref[0]
        for d in range(1, NDEV):
            g = g + p_ref[d]
        rest[-1][...] = g

    return pl.pallas_call(
        body, grid=(r // tr,), in_specs=[pl.BlockSpec((NDEV, tr, 128), lambda i: (0, i, 0))] + [ANY_SPEC] * len(deps),
        out_specs=pl.BlockSpec((tr, 128), lambda i: (i, 0)), out_shape=jax.ShapeDtypeStruct((r, 128), F32),
        compiler_params=_params(("parallel",)), name=name,
    )(parts, *deps)


def _sum_unpack(parts, rows, name, dep=None):
    deps = [] if dep is None else [dep]

    def body(p_ref, *outs):
        outs = outs[len(deps):]
        off = 0
        for o_ref, n in zip(outs, rows):
            acc = p_ref[0, off:off + n, :]
            for d in range(1, NDEV):
                acc = acc + p_ref[d, off:off + n, :]
            o_ref[...] = acc
            off += n

    return pl.pallas_call(
        body, grid=(1,), in_specs=[pl.BlockSpec(parts.shape, lambda i: (0, 0, 0))] + [ANY_SPEC] * len(deps),
        out_specs=[pl.BlockSpec((n, 128), lambda i: (0, 0)) for n in rows],
        out_shape=[jax.ShapeDtypeStruct((n, 128), F32) for n in rows],
        compiler_params=_params(("arbitrary",), VMEM_BIG), name=name,
    )(parts, *deps)


def _adam_small(ws, gs, g_specs, ms, vs, name):
    n = len(ws)

    def body(*refs):
        w_r, g_r, m_r, v_r = refs[:n], refs[n:2 * n], refs[2 * n:3 * n], refs[3 * n:4 * n]
        outs = refs[4 * n:]
        for i in range(n):
            g = g_r[i][...]
            outs[4 * i][...] = g
            outs[4 * i + 1][...], outs[4 * i + 2][...], outs[4 * i + 3][...] = _adamw(
                w_r[i][...], g, m_r[i][...], v_r[i][...])

    whole = lambda a: pl.BlockSpec(a.shape, lambda i, nd=a.ndim: (0,) * nd)
    outs = pl.pallas_call(
        body, grid=(1,),
        in_specs=[whole(a) for a in ws] + list(g_specs) + [whole(a) for a in ms] + [whole(a) for a in vs],
        out_specs=[whole(a) for a in ws for _ in range(4)],
        out_shape=[jax.ShapeDtypeStruct(a.shape, F32) for a in ws for _ in range(4)],
        compiler_params=_params(("arbitrary",), VMEM_BIG), name=name,
    )(*ws, *gs, *ms, *vs)
    return [outs[4 * i:4 * i + 4] for i in range(n)]


MASKS = [(mx, my, mc) for mx in (0, 1) for my in (0, 1) for mc in (0, 1)][1:]


def _sc_exchange(name, collective_id, arrays, scatter):
    nt = len(arrays)
    out_type = [jax.ShapeDtypeStruct(a.shape if scatter else (NDEV,) + a.shape, a.dtype) for a in arrays]

    def body(*refs):
        ins, outs = refs[:nt], refs[nt:2 * nt]
        send_sems, recv_sems, local_sems = refs[2 * nt:3 * nt], refs[3 * nt:4 * nt], refs[4 * nt:5 * nt]
        x, y, c = lax.axis_index("x"), lax.axis_index("y"), lax.axis_index("c")
        peers = [(mx + x - 2 * mx * x, my + y - 2 * my * y, mc + c - 2 * mc * c) for mx, my, mc in MASKS]
        barrier = pltpu.get_barrier_semaphore()
        for peer in peers:
            pl.semaphore_signal(barrier, inc=1, device_id=peer, device_id_type=MESH)
        pl.semaphore_wait(barrier, len(peers))
        me = 4 * x + 2 * y + c
        own = []
        for t in range(nt):
            cp = pltpu.make_async_copy(ins[t].at[me] if scatter else ins[t], outs[t].at[me], local_sems[t])
            cp.start()
            own.append(cp)
            for px, py, pc in peers:
                src = ins[t].at[4 * px + 2 * py + pc] if scatter else ins[t]
                pltpu.make_async_remote_copy(src_ref=src, dst_ref=outs[t].at[me], send_sem=send_sems[t],
                                             recv_sem=recv_sems[t], device_id=(px, py, pc), device_id_type=MESH).start()
        for t in range(nt):
            own[t].wait()
            seven = outs[t].at[pl.ds(0, NDEV - 1)]
            drain = pltpu.make_async_remote_copy(src_ref=seven, dst_ref=seven, send_sem=send_sems[t],
                                                 recv_sem=recv_sems[t], device_id=(x, y, c), device_id_type=MESH)
            drain.wait_send()
            drain.wait_recv()

    return pl.kernel(
        body, out_type=out_type, mesh=plsc.ScalarSubcoreMesh(axis_name="sequencer", num_cores=1),
        scratch_types=[pltpu.SemaphoreType.DMA] * (3 * nt),
        compiler_params=pltpu.CompilerParams(collective_id=collective_id), name=name,
    )(*arrays)


def _sc_gather_two_level(name, collective_id, arrays):
    nt = len(arrays)
    out_type = [jax.ShapeDtypeStruct((NDEV,) + a.shape, a.dtype) for a in arrays]

    def body(*refs):
        ins, outs = refs[:nt], refs[nt:2 * nt]
        sems = refs[2 * nt:]
        send_sems, sib_sems, local_sems = sems[:nt], sems[nt:2 * nt], sems[2 * nt:3 * nt]
        ici_sems = [sems[3 * nt + 3 * t:3 * nt + 3 * t + 3] for t in range(nt)]
        x, y, c = lax.axis_index("x"), lax.axis_index("y"), lax.axis_index("c")
        sibling = (x, y, 1 - c)
        chips = [(1 - x, y), (x, 1 - y), (1 - x, 1 - y)]
        barrier = pltpu.get_barrier_semaphore()
        for peer in [sibling] + [(cx, cy, c) for cx, cy in chips]:
            pl.semaphore_signal(barrier, inc=1, device_id=peer, device_id_type=MESH)
        pl.semaphore_wait(barrier, 4)
        me = 4 * x + 2 * y + c

        def push(t, src, slot, recv_sem, to):
            pltpu.make_async_remote_copy(src_ref=src, dst_ref=outs[t].at[slot], send_sem=send_sems[t],
                                         recv_sem=recv_sem, device_id=to, device_id_type=MESH).start()

        own = []
        for t in range(nt):
            cp = pltpu.make_async_copy(ins[t], outs[t].at[me], local_sems[t])
            cp.start()
            own.append(cp)
            for j, (cx, cy) in enumerate(chips):
                push(t, ins[t], me, ici_sems[t][j], (cx, cy, c))
            push(t, ins[t], me, sib_sems[t], sibling)
        for t in range(nt):
            for j, (cx, cy) in enumerate(chips):
                slot = 4 * cx + 2 * cy + c
                landed = outs[t].at[slot]
                pltpu.make_async_remote_copy(src_ref=landed, dst_ref=landed, send_sem=send_sems[t],
                                             recv_sem=ici_sems[t][j], device_id=(cx, cy, c),
                                             device_id_type=MESH).wait_recv()
                push(t, landed, slot, sib_sems[t], sibling)
        for t in range(nt):
            own[t].wait()
            four, seven = outs[t].at[pl.ds(0, 4)], outs[t].at[pl.ds(0, 7)]
            pltpu.make_async_remote_copy(src_ref=four, dst_ref=four, send_sem=send_sems[t], recv_sem=sib_sems[t],
                                         device_id=sibling, device_id_type=MESH).wait_recv()
            pltpu.make_async_remote_copy(src_ref=seven, dst_ref=seven, send_sem=send_sems[t], recv_sem=sib_sems[t],
                                         device_id=sibling, device_id_type=MESH).wait_send()

    return pl.kernel(
        body, out_type=out_type, mesh=plsc.ScalarSubcoreMesh(axis_name="sequencer", num_cores=1),
        scratch_types=[pltpu.SemaphoreType.DMA] * (6 * nt),
        compiler_params=pltpu.CompilerParams(collective_id=collective_id), name=name,
    )(*arrays)


def _sc_sibling_exchange(name, collective_id, src, out_shape, pieces):
    def body(src_ref, out_ref, send_sem, recv_sem):
        x, y, c = lax.axis_index("x"), lax.axis_index("y"), lax.axis_index("c")
        sibling = (x, y, 1 - c)
        barrier = pltpu.get_barrier_semaphore()
        pl.semaphore_signal(barrier, inc=1, device_id=sibling, device_id_type=MESH)
        pl.semaphore_wait(barrier, 1)
        for piece, lands in pieces(c, src_ref, out_ref):
            pltpu.make_async_remote_copy(src_ref=piece, dst_ref=lands, send_sem=send_sem, recv_sem=recv_sem,
                                         device_id=sibling, device_id_type=MESH).start()
        drain = pltpu.make_async_remote_copy(src_ref=out_ref, dst_ref=out_ref, send_sem=send_sem, recv_sem=recv_sem,
                                             device_id=sibling, device_id_type=MESH)
        drain.wait_send()
        drain.wait_recv()

    return pl.kernel(
        body, out_type=jax.ShapeDtypeStruct(out_shape, src.dtype),
        mesh=plsc.ScalarSubcoreMesh(axis_name="sequencer", num_cores=1), scratch_types=[pltpu.SemaphoreType.DMA] * 2,
        compiler_params=pltpu.CompilerParams(collective_id=collective_id), name=name,
    )(src)


def _swap_class_columns(name, collective_id, dz, nb, piece=0, npieces=1):
    w = nb // npieces
    return _sc_sibling_exchange(
        name, collective_id, dz, (S, 4 * w),
        lambda c, src, out: [(src.at[:, pl.ds((2 * j + 1 - c) * nb + piece * w, w)], out.at[:, pl.ds(j * w, w)])
                             for j in range(4)])


def _sc_chip_scatter(name, collective_id, q):
    def body(q_ref, out_ref, send_sem, recv_sem, local_sem):
        x, y, c = lax.axis_index("x"), lax.axis_index("y"), lax.axis_index("c")
        chips = [(1 - x, y), (x, 1 - y), (1 - x, 1 - y)]
        barrier = pltpu.get_barrier_semaphore()
        for cx, cy in chips:
            pl.semaphore_signal(barrier, inc=1, device_id=(cx, cy, c), device_id_type=MESH)
        pl.semaphore_wait(barrier, 3)
        mine = 2 * x + y
        own = pltpu.make_async_copy(q_ref.at[mine], out_ref.at[mine], local_sem)
        own.start()
        for cx, cy in chips:
            pltpu.make_async_remote_copy(src_ref=q_ref.at[2 * cx + cy], dst_ref=out_ref.at[mine], send_sem=send_sem,
                                         recv_sem=recv_sem, device_id=(cx, cy, c), device_id_type=MESH).start()
        own.wait()
        three = out_ref.at[pl.ds(0, 3)]
        drain = pltpu.make_async_remote_copy(src_ref=three, dst_ref=three, send_sem=send_sem, recv_sem=recv_sem,
                                             device_id=(x, y, c), device_id_type=MESH)
        drain.wait_send()
        drain.wait_recv()

    return pl.kernel(
        body, out_type=jax.ShapeDtypeStruct(q.shape, q.dtype),
        mesh=plsc.ScalarSubcoreMesh(axis_name="sequencer", num_cores=1), scratch_types=[pltpu.SemaphoreType.DMA] * 3,
        compiler_params=pltpu.CompilerParams(collective_id=collective_id), name=name,
    )(q)


def _mm_pair_dw(h_own, dz, h_sib, dz_sib, nb, name, dep=None, piece=0, npieces=1, h_transposed=False):
    nb = nb // npieces
    tn = 512 if nb % 512 == 0 else nb
    per = nb // tn
    dn = NN if h_transposed else TN
    o_spec = pl.BlockSpec((None, D, tn), lambda i, j, k: (j // per, 0, j % per))
    own_col = lambda i, j, k: (0, ((2 * (j // per) + lax.axis_index("c")) * npieces + piece) * per + j % per)
    part = _matmul(
        h_own, dz, dn=dn, grid=(1, 4 * per, 1),
        a_spec=pl.BlockSpec((S, D), lambda i, j, k: (0, 0)), b_spec=pl.BlockSpec((S, tn), own_col),
        o_spec=o_spec, out_shape=(4, D, nb), out_dtype=F32, acc_shape=(D, tn), name=name + "_own", dep=dep)

    def body(a_ref, b_ref, p_ref, o_ref):
        o_ref[...] = (p_ref[...] + _dot(a_ref[...], b_ref[...], dn)).astype(BF16)

    return pl.pallas_call(
        body, grid=(1, 4 * per, 1),
        in_specs=[pl.BlockSpec((S, D), lambda i, j, k: (0, 0)), pl.BlockSpec((S, tn), lambda i, j, k: (0, j)), o_spec],
        out_specs=o_spec, out_shape=jax.ShapeDtypeStruct((4, D, nb), BF16),
        compiler_params=_params(("parallel", "parallel", "arbitrary"), VMEM_BIG), name=name + "_sibling",
    )(h_sib, dz_sib, part)


SMALL = {
    "e_pre_norm": ((2048,), None), "e_pool_w": ((4, 256, 256), 1), "e_pool_scale": ((1024,), None),
    "e_post_norm": ((2048,), None), "o_pre_norm": ((2048,), 0), "o_sgu_norm_g": ((1024,), 0),
    "o_sgu_norm_b": ((1024,), 0), "o_sgu_w": ((4, 128, 128), None), "o_sgu_b": ((4, 128), None),
    "o_conv_w": ((31, 1024), 1), "o_conv_b": ((1024,), 0), "o_conv_norm_g": ((1024,), 0),
    "o_conv_norm_b": ((1024,), 0), "o_post_norm": ((2048,), 0),
}
SMALL_SHARDED = [n for n, (_, ax) in SMALL.items() if ax is not None]


def _shard_shape(name):
    shape, ax = SMALL[name]
    if ax is None:
        return shape
    return tuple(s // NDEV if i == ax else s for i, s in enumerate(shape))


def _pack(arrs, row_multiple=1):
    flat = jnp.concatenate([a.reshape(-1) for a in arrs])
    pad = -flat.shape[0] % (128 * row_multiple)
    return jnp.concatenate([flat, jnp.zeros((pad,), F32)]).reshape(-1, 128)


def _small_views(name):
    shape, ax = SMALL[name]
    me = lambda: 4 * lax.axis_index("x") + 2 * lax.axis_index("y") + lax.axis_index("c")
    if ax is None:
        view = (int(np.prod(shape)) // 128, 128)
        return view, view, pl.BlockSpec(view, lambda i: (0, 0))
    if len(shape) == 1:
        n = shape[0] // NDEV
        return (1, n), (NDEV, 1, n), pl.BlockSpec((None, 1, n), lambda i: (me(), 0, 0))
    part = _shard_shape(name)
    return part, shape, pl.BlockSpec(part, lambda i: tuple(me() if d == ax else 0 for d in range(len(shape))))


WEIGHTS = ["e_pre_norm", "e_w_in", "e_pool_w", "e_pool_scale", "e_w_out", "e_post_norm", "o_pre_norm", "o_w_in",
           "o_sgu_norm_g", "o_sgu_norm_b", "o_sgu_w", "o_sgu_b", "o_conv_w", "o_conv_b", "o_conv_norm_g",
           "o_conv_norm_b", "o_w_out", "o_post_norm"]


def kernel(x, e_pre_norm, e_w_in, e_pool_w, e_pool_scale, e_w_out, e_post_norm, o_pre_norm, o_w_in, o_sgu_norm_g, o_sgu_norm_b, o_sgu_w, o_sgu_b, o_conv_w, o_conv_b, o_conv_norm_g, o_conv_norm_b, o_w_out, o_post_norm, loss_target, m_e_pre_norm, m_e_w_in, m_e_pool_w, m_e_pool_scale, m_e_w_out, m_e_post_norm, m_o_pre_norm, m_o_w_in, m_o_sgu_norm_g, m_o_sgu_norm_b, m_o_sgu_w, m_o_sgu_b, m_o_conv_w, m_o_conv_b, m_o_conv_norm_g, m_o_conv_norm_b, m_o_w_out, m_o_post_norm, v_e_pre_norm, v_e_w_in, v_e_pool_w, v_e_pool_scale, v_e_w_out, v_e_post_norm, v_o_pre_norm, v_o_w_in, v_o_sgu_norm_g, v_o_sgu_norm_b, v_o_sgu_w, v_o_sgu_b, v_o_conv_w, v_o_conv_b, v_o_conv_norm_g, v_o_conv_norm_b, v_o_w_out, v_o_post_norm):
    given = dict(locals())
    w = {n: given[n][0] for n in WEIGHTS}
    m = {n: given["m_" + n][0] for n in WEIGHTS}
    v = {n: given["v_" + n][0] for n in WEIGHTS}
    me = 4 * lax.axis_index("x") + 2 * lax.axis_index("y") + lax.axis_index("c")
    x, target = x[0], loss_target[0]
    row = lambda a: a.reshape(1, -1)

    lo, small_rows = _sc_gather_two_level(
        "gather_a0", 0, [_cast_bf16(w["e_w_in"], "cast_e_w_in_0", 0, 2), _pack([w[n] for n in SMALL_SHARDED])])
    hi, = _sc_gather_two_level("gather_a1", 12, [_cast_bf16(w["e_w_in"], "cast_e_w_in_1", 1, 2)])
    wg_e_in = (lo, hi)
    h0, h0t = _pre0_fwd(x, row(w["e_pre_norm"]))
    wg_e_out, = _sc_gather_two_level("gather_b", 1, [_cast_bf16(w["e_w_out"], "cast_e_w_out")])
    wg_o_in, wg_o_out = _sc_gather_two_level(
        "gather_c", 13, [_cast_bf16(w[n], "cast_" + n) for n in ("o_w_in", "o_w_out")])
    h0t_sib = _sc_sibling_exchange("swap_h0", 8, h0t, h0t.shape, lambda c, src, out: [(src, out)])
    p = {n: w[n] for n in SMALL if SMALL[n][1] is None}
    small_rows = small_rows.reshape(NDEV, -1)
    off = 0
    for n in SMALL_SHARDED:
        shp, ax = _shard_shape(n), SMALL[n][1]
        cnt = int(np.prod(shp))
        blk = small_rows[:, off:off + cnt].reshape((NDEV,) + shp)
        p[n] = jnp.moveaxis(blk, 0, ax).reshape(SMALL[n][0])
        off += cnt
    tabs = _rope_tables()
    pool_w_bf = p["e_pool_w"].astype(BF16)
    sgu_bb = jnp.broadcast_to(p["o_sgu_b"][:, :, None], (4, 128, 128))
    conv_w = jnp.concatenate([p["o_conv_w"], jnp.zeros((HALO - CONV_K, HALF), F32)], axis=0)
    odd_p = (row(p["o_sgu_norm_g"]), row(p["o_sgu_norm_b"]), p["o_sgu_w"], sgu_bb, conv_w,
             row(p["o_conv_b"]), row(p["o_conv_norm_g"]), row(p["o_conv_norm_b"]))

    z0 = _mm_in_halves(h0, wg_e_in, "mm_z0")
    ycat0 = _pool_fwd(z0, pool_w_bf, row(p["e_pool_scale"]))
    qkv = _qkv_prep(z0, tabs)
    ycat0, og, lg = _attn_fwd(z0, qkv, ycat0)
    w_out_e, w_out_o = wg_e_out.reshape(2048, D), wg_o_out.reshape(2048, D)
    y0, x1, h1 = _post0_fwd(ycat0, w_out_e, x, row(p["e_post_norm"]), row(p["o_pre_norm"]), h0t_sib)
    h1_sib = _sc_sibling_exchange("swap_h1", 11, h1, h1.shape, lambda c, src, out: [(src, out)])
    z1 = _mm_in(h1, wg_o_in, "mm_z1")
    ycat1, conv_out = _odd_fwd(z1, *odd_p)

    g = {}
    loss, dx2, dy1, g["o_post_norm"] = _post1_bwd(ycat1, w_out_o, x1, target, row(p["o_post_norm"]), h1_sib)
    loss = lax.psum(loss[0, 0], ("x", "y", "c"))
    parts = {}
    dw = _mm_out_dw(ycat1, dy1, "mm_dwout1").reshape(NDEV, 256, D)
    parts["o_w_out"], = _sc_exchange("scatter_o_w_out", 2, [dw], True)
    dycat1 = _mm_out_dx(dy1, w_out_o, "mm_dycat1", (dw, loss.reshape(1, 1)))
    dz1, ddc, g["o_sgu_w"], d_sgu_bb, g["o_sgu_norm_g"], g["o_sgu_norm_b"], g["o_conv_norm_g"], \
        g["o_conv_norm_b"], g["o_conv_b"] = _odd_bwd_a(z1, conv_out, dycat1, *odd_p[:4], *odd_p[6:])
    dz1, d_conv_w = _odd_bwd_b(z1, ddc, dz1, conv_w)
    g["o_sgu_b"] = d_sgu_bb[:, :, 0]
    g["o_conv_w"] = d_conv_w[:CONV_K]
    grads, deltas, new_m, new_v = {}, {}, {}, {}

    def adam(n, dep):
        grads[n], deltas[n], new_m[n], new_v[n] = _adam_reduce(parts[n], w[n], m[n], v[n], "adam_" + n, dep)
        return new_v[n]

    pin = _arrived(parts["o_w_out"], "arrived_o_w_out", d_conv_w)
    dz1_sib = _swap_class_columns("swap_dz1", 10, dz1, ODD_IN // NDEV)
    dw = _mm_pair_dw(h1, dz1, h1_sib, dz1_sib, ODD_IN // NDEV, "mm_dwin1", pin)
    parts["o_w_in"] = _sc_chip_scatter("scatter_o_w_in", 3, dw)
    dh1 = _mm_in_dx(dz1, wg_o_in, "mm_dh1", dw)
    dx1, dy0, g["o_pre_norm"], g["e_post_norm"] = _mid_bwd(dx2, dh1, x1, y0, row(p["o_pre_norm"]),
                                                           row(p["e_post_norm"]))
    dw = _mm_out_dw(ycat0, dy0, "mm_dwout0").reshape(NDEV, 256, D)
    parts["e_w_out"], = _sc_exchange("scatter_e_w_out", 4, [dw], True)
    dycat0 = _mm_out_dx(dy0, w_out_e, "mm_dycat0", dw)
    da_in, da_gate, g["e_pool_w"], g["e_pool_scale"] = _pool_bwd(z0, dycat0, pool_w_bf, row(p["e_pool_scale"]))
    late = [n for n in SMALL if n not in ("e_pre_norm", "o_sgu_b")] + ["o_sgu_b"]
    recv_small, = _sc_gather_two_level("gather_small_grads", 6,
                                       [_pack([g[n].reshape(SMALL[n][0]) for n in late], 512)])
    took = _arrived(parts["o_w_in"], "arrived_o_w_in")
    dq, dk, dv, dbg = _attn_bwd(z0, qkv, og, lg, dycat0, tabs, took)
    dz0 = jnp.concatenate([da_in, da_gate, dq, dk, dv, dbg], axis=1)
    took = _arrived(recv_small, "arrived_small_grads", _arrived(parts["e_w_out"], "arrived_e_w_out", dz0))
    nb = EVEN_IN // NDEV
    swapped = [_swap_class_columns("swap_dz0_%d" % half, (9, 14)[half], dz0, nb, half, 2) for half in (0, 1)]
    dw, e_w_in_parts = took, []
    for half in (0, 1):
        dw = _mm_pair_dw(h0t, dz0, h0t_sib, swapped[half], nb, "mm_dwin0_%d" % half, dw, half, 2, True)
        e_w_in_parts.append(_sc_chip_scatter("scatter_e_w_in_%d" % half, (5, 15)[half], dw))
    pin = adam("e_w_out", adam("o_w_out", adam("o_w_in", dw)))
    rows = [int(np.prod(SMALL[n][0])) // 128 for n in late]
    summed = dict(zip(late, _sum_unpack(recv_small, rows, "sum_small_grads", pin)))
    dh0 = _mm_in_dx_halves(dz0, wg_e_in, "mm_dh0", summed[late[0]])
    grad_x, g["e_pre_norm"] = _pre0_bwd(dx1, dh0, x, row(p["e_pre_norm"]))
    last, = _sc_exchange("gather_e_pre_norm_grad", 7, [g["e_pre_norm"].reshape(16, 128)], False)

    n = "e_w_in"
    out = _adam_reduce(e_w_in_parts[0], w[n], m[n], v[n], "adam_e_w_in_0", grad_x, 0, 2)
    out = _adam_reduce(e_w_in_parts[1], w[n], m[n], v[n], "adam_e_w_in_1", None, 1, 2, out)
    grads[n], deltas[n], new_m[n], new_v[n] = out
    summed["e_pre_norm"] = _sum_parts(last, "sum_e_pre_norm_grad", out[3])
    names = list(SMALL)
    views = [_small_views(n) for n in names]
    mine = lambda src: [src[n].reshape(vw[0]) for n, vw in zip(names, views)]
    res = _adam_small(mine(w), [summed[n].reshape(vw[1]) for n, vw in zip(names, views)], [vw[2] for vw in views],
                      mine(m), mine(v), "adam_small")
    for n, out in zip(names, res):
        grads[n], deltas[n], new_m[n], new_v[n] = [t.reshape(_shard_shape(n)) for t in out]

    lead = lambda a: a[None]
    return (loss, grad_x[None], *[lead(grads[n]) for n in WEIGHTS], *[lead(deltas[n]) for n in WEIGHTS],
            *[lead(new_m[n]) for n in WEIGHTS], *[lead(new_v[n]) for n in WEIGHTS])
```

```python
import numpy as np
import jax
import jax.numpy as jnp
from jax import lax
from jax.experimental import pallas as pl
from jax.experimental.pallas import tpu as pltpu
from jax.experimental.pallas import tpu_sc as plsc

F32 = jnp.float32
BF16 = jnp.bfloat16

S = 2048
D = 2048
NDEV = 8
EPS = 1e-6
NEG = -1e30
HEAD_DIM = 128
ROT_DIM = 32
ROPE_THETA = 500000.0
PATTERNS = ((128, 1), (512, 4), (2048, 16))
BLK = 128
EVEN_IN = 12288
ODD_IN = 6144
HALF = 1024
CONV_K = 31
HALO = 32
TR = 256
SUB = 32

ADAM_LR = 0.001
ADAM_B1 = 0.9
ADAM_B2 = 0.999
ADAM_EPS = 1e-08
ADAM_WD = 0.01
ADAM_STEP = 10

VMEM_BIG = 56 * 1024 * 1024
MESH = pl.DeviceIdType.MESH

NN = (((1,), (0,)), ((), ()))
NT = (((1,), (1,)), ((), ()))
TN = (((0,), (0,)), ((), ()))


def _dot(a, b, dn=NN):
    return lax.dot_general(a, b, dn, preferred_element_type=F32)


def _sigmoid(x):
    return 1.0 / (1.0 + jnp.exp(-x))


def _silu_and_grad(x):
    sg = _sigmoid(x)
    return x * sg, sg * (1.0 + x * (1.0 - sg))


def _params(sem, vmem=None):
    return pltpu.CompilerParams(dimension_semantics=sem, vmem_limit_bytes=vmem)


ANY_SPEC = pl.BlockSpec(memory_space=pl.ANY)


def _matmul(a, b, *, dn, grid, a_spec, b_spec, o_spec, out_shape, out_dtype, acc_shape, name, dep=None):
    nk = grid[2]
    deps = [] if dep is None else list(dep) if isinstance(dep, (tuple, list)) else [dep]

    def body(a_ref, b_ref, *rest):
        o_ref, acc = rest[len(deps)], rest[len(deps) + 1:]
        if nk == 1:
            o_ref[...] = _dot(a_ref[...], b_ref[...], dn).astype(o_ref.dtype)
            return
        acc_ref = acc[0]
        k = pl.program_id(2)

        @pl.when(k == 0)
        def _():
            acc_ref[...] = jnp.zeros_like(acc_ref)

        acc_ref[...] += _dot(a_ref[...], b_ref[...], dn)

        @pl.when(k == nk - 1)
        def _():
            o_ref[...] = acc_ref[...].astype(o_ref.dtype)

    return pl.pallas_call(
        body, grid=grid, in_specs=[a_spec, b_spec] + [ANY_SPEC] * len(deps), out_specs=o_spec,
        out_shape=jax.ShapeDtypeStruct(out_shape, out_dtype),
        scratch_shapes=[] if nk == 1 else [pltpu.VMEM(acc_shape, F32)],
        compiler_params=_params(("parallel", "parallel", "arbitrary"), VMEM_BIG), name=name,
    )(a, b, *deps)


TM = 2048


def _mm_in(h, wg, name):
    nb = wg.shape[2]
    tn = 512 if nb % 512 == 0 else nb
    per = nb // tn
    return _matmul(
        h, wg, dn=NN, grid=(S // TM, NDEV * per, 1),
        a_spec=pl.BlockSpec((TM, D), lambda i, j, k: (i, 0)),
        b_spec=pl.BlockSpec((None, D, tn), lambda i, j, k: (j // per, 0, j % per)),
        o_spec=pl.BlockSpec((TM, tn), lambda i, j, k: (i, j)),
        out_shape=(S, NDEV * nb), out_dtype=F32, acc_shape=(TM, tn), name=name)


def _mm_in_halves(h, wg_halves, name):
    hb = wg_halves[0].shape[2]
    z = None
    for half, wg in enumerate(wg_halves):
        prev = [] if z is None else [z]

        def body(a_ref, b_ref, *rest):
            rest[-1][...] = _dot(a_ref[...], b_ref[...])

        z = pl.pallas_call(
            body, grid=(NDEV,),
            in_specs=[pl.BlockSpec((S, D), lambda j: (0, 0)), pl.BlockSpec((None, D, hb), lambda j: (j, 0, 0))]
                     + [ANY_SPEC] * len(prev),
            out_specs=pl.BlockSpec((S, hb), lambda j, half=half: (0, 2 * j + half)),
            out_shape=jax.ShapeDtypeStruct((S, 2 * NDEV * hb), F32),
            input_output_aliases={2: 0} if prev else {},
            compiler_params=_params(("parallel",), VMEM_BIG), name="%s_%d" % (name, half),
        )(h, wg, *prev)
    return z


def _mm_in_dx_halves(dz, wg_halves, name, dep):
    hb = wg_halves[0].shape[2]
    nk = 2 * NDEV

    def body(a_ref, b0_ref, b1_ref, dep_ref, o_ref, acc_ref):
        k = pl.program_id(2)

        @pl.when(k == 0)
        def _():
            acc_ref[...] = jnp.zeros_like(acc_ref)

        @pl.when(k % 2 == 0)
        def _():
            acc_ref[...] += _dot(a_ref[...], b0_ref[...], NT)

        @pl.when(k % 2 == 1)
        def _():
            acc_ref[...] += _dot(a_ref[...], b1_ref[...], NT)

        @pl.when(k == nk - 1)
        def _():
            o_ref[...] = acc_ref[...]

    b_spec = pl.BlockSpec((None, 1024, hb), lambda i, j, k: (k // 2, j, 0))
    return pl.pallas_call(
        body, grid=(1, D // 1024, nk),
        in_specs=[pl.BlockSpec((S, hb), lambda i, j, k: (0, k)), b_spec, b_spec, ANY_SPEC],
        out_specs=pl.BlockSpec((S, 1024), lambda i, j, k: (0, j)), out_shape=jax.ShapeDtypeStruct((S, D), F32),
        scratch_shapes=[pltpu.VMEM((S, 1024), F32)],
        compiler_params=_params(("parallel", "parallel", "arbitrary"), VMEM_BIG), name=name,
    )(dz, *wg_halves, dep)


def _mm_in_dx(dz, wg, name, dep=None):
    nb = wg.shape[2]
    return _matmul(
        dz, wg, dn=NT, grid=(S // TM, D // 1024, NDEV),
        a_spec=pl.BlockSpec((TM, nb), lambda i, j, k: (i, k)),
        b_spec=pl.BlockSpec((None, 1024, nb), lambda i, j, k: (k, j, 0)),
        o_spec=pl.BlockSpec((TM, 1024), lambda i, j, k: (i, j)),
        out_shape=(S, D), out_dtype=F32, acc_shape=(TM, 1024), name=name, dep=dep)


def _mm_out_dx(dy, w, name, dep=None):
    return _matmul(
        dy, w, dn=NT, grid=(S // TM, 2048 // 512, 1),
        a_spec=pl.BlockSpec((TM, D), lambda i, j, k: (i, 0)),
        b_spec=pl.BlockSpec((512, D), lambda i, j, k: (j, 0)),
        o_spec=pl.BlockSpec((TM, 512), lambda i, j, k: (i, j)),
        out_shape=(S, 2048), out_dtype=F32, acc_shape=(TM, 512), name=name, dep=dep)


def _mm_out_dw(yc, dy, name):
    return _matmul(
        yc, dy, dn=TN, grid=(2048 // TM, D // 512, 1),
        a_spec=pl.BlockSpec((S, TM), lambda i, j, k: (0, i)),
        b_spec=pl.BlockSpec((S, 512), lambda i, j, k: (0, j)),
        o_spec=pl.BlockSpec((TM, 512), lambda i, j, k: (i, j)),
        out_shape=(2048, D), out_dtype=BF16, acc_shape=(TM, 512), name=name)


def _row_spec(w=D):
    return pl.BlockSpec((TR, w), lambda i: (i, 0))


def _vec_spec(w=D):
    return pl.BlockSpec((1, w), lambda i: (0, 0))


def _rms_stats(x):
    r = lax.rsqrt(jnp.mean(x * x, axis=-1, keepdims=True) + EPS)
    return x * r, r


def _rms_bwd(dn, xhat, r, g):
    dxh = dn * g
    return r * (dxh - xhat * jnp.mean(dxh * xhat, axis=-1, keepdims=True))


def _acc_rows(ref, val, i):
    s = jnp.sum(val, axis=0, keepdims=True)

    @pl.when(i == 0)
    def _():
        ref[...] = s

    @pl.when(i > 0)
    def _():
        ref[...] += s


def _pre0_fwd(x, g):
    def body(x_ref, g_ref, h_ref, ht_ref):
        xhat, _ = _rms_stats(x_ref[...])
        h = xhat * g_ref[...]
        h_ref[...] = h.astype(BF16)
        ht_ref[...] = h.T.astype(BF16)

    return pl.pallas_call(
        body, grid=(S // TR,), in_specs=[_row_spec(), _vec_spec()],
        out_specs=[_row_spec(), pl.BlockSpec((D, TR), lambda i: (0, i))],
        out_shape=[jax.ShapeDtypeStruct((S, D), BF16), jax.ShapeDtypeStruct((D, S), BF16)],
        compiler_params=_params(("parallel",)), name="pre0_fwd",
    )(x, g)


def _post0_fwd(ycat, w_out, x, g_post, g_pre1, dep):
    def body(yc_ref, w_ref, x_ref, gp_ref, g1_ref, dep_ref, y_ref, x1_ref, h1_ref):
        y = _dot(yc_ref[...], w_ref[...])
        y_ref[...] = y
        yhat, _ = _rms_stats(y)
        x1 = x_ref[...] + yhat * gp_ref[...]
        x1_ref[...] = x1
        xhat, _ = _rms_stats(x1)
        h1_ref[...] = (xhat * g1_ref[...]).astype(BF16)

    return pl.pallas_call(
        body, grid=(S // TR,),
        in_specs=[_row_spec(), pl.BlockSpec((2048, D), lambda i: (0, 0)), _row_spec(), _vec_spec(), _vec_spec(),
                  ANY_SPEC],
        out_specs=[_row_spec(), _row_spec(), _row_spec()],
        out_shape=[jax.ShapeDtypeStruct((S, D), F32), jax.ShapeDtypeStruct((S, D), F32),
                   jax.ShapeDtypeStruct((S, D), BF16)],
        compiler_params=_params(("parallel",), VMEM_BIG), name="post0_fwd",
    )(ycat, w_out, x, g_post, g_pre1, dep)


def _post1_bwd(ycat, w_out, x1, target, g_post, dep):
    def body(yc_ref, w_ref, x1_ref, t_ref, g_ref, dep_ref, loss_ref, dx2_ref, dy_ref, dg_ref):
        i = pl.program_id(0)
        yhat, r = _rms_stats(_dot(yc_ref[...], w_ref[...]))
        g = g_ref[...]
        err = x1_ref[...] + yhat * g - t_ref[...]
        part = jnp.sum(jnp.sum(err * err, axis=-1, keepdims=True), axis=0, keepdims=True) * (0.5 / D)
        _acc_rows(loss_ref, jnp.broadcast_to(part, (1, 128)), i)
        dx2 = err * (1.0 / D)
        dx2_ref[...] = dx2
        _acc_rows(dg_ref, dx2 * yhat, i)
        dy_ref[...] = _rms_bwd(dx2, yhat, r, g).astype(BF16)

    return pl.pallas_call(
        body, grid=(S // TR,),
        in_specs=[_row_spec(), pl.BlockSpec((2048, D), lambda i: (0, 0)), _row_spec(), _row_spec(), _vec_spec(),
                  ANY_SPEC],
        out_specs=[_vec_spec(128), _row_spec(), _row_spec(), _vec_spec()],
        out_shape=[jax.ShapeDtypeStruct((1, 128), F32), jax.ShapeDtypeStruct((S, D), F32),
                   jax.ShapeDtypeStruct((S, D), BF16), jax.ShapeDtypeStruct((1, D), F32)],
        compiler_params=_params(("arbitrary",), VMEM_BIG), name="post1_bwd",
    )(ycat, w_out, x1, target, g_post, dep)


def _mid_bwd(dx2, dh1, x1, y0, g_pre1, g_post0):
    def body(dx2_ref, dh_ref, x1_ref, y_ref, g1_ref, gp_ref, dx1_ref, dy_ref, dg1_ref, dgp_ref):
        i = pl.program_id(0)
        xhat, r1 = _rms_stats(x1_ref[...])
        dh = dh_ref[...]
        _acc_rows(dg1_ref, dh * xhat, i)
        dx1 = dx2_ref[...] + _rms_bwd(dh, xhat, r1, g1_ref[...])
        dx1_ref[...] = dx1
        yhat, r0 = _rms_stats(y_ref[...])
        _acc_rows(dgp_ref, dx1 * yhat, i)
        dy_ref[...] = _rms_bwd(dx1, yhat, r0, gp_ref[...]).astype(BF16)

    return pl.pallas_call(
        body, grid=(S // TR,),
        in_specs=[_row_spec(), _row_spec(), _row_spec(), _row_spec(), _vec_spec(), _vec_spec()],
        out_specs=[_row_spec(), _row_spec(), _vec_spec(), _vec_spec()],
        out_shape=[jax.ShapeDtypeStruct((S, D), F32), jax.ShapeDtypeStruct((S, D), BF16),
                   jax.ShapeDtypeStruct((1, D), F32), jax.ShapeDtypeStruct((1, D), F32)],
        compiler_params=_params(("arbitrary",)), name="mid_bwd",
    )(dx2, dh1, x1, y0, g_pre1, g_post0)


def _pre0_bwd(dx1, dh0, x, g):
    def body(dx1_ref, dh_ref, x_ref, g_ref, gx_ref, dg_ref):
        i = pl.program_id(0)
        xhat, r = _rms_stats(x_ref[...])
        dh = dh_ref[...]
        _acc_rows(dg_ref, dh * xhat, i)
        gx_ref[...] = dx1_ref[...] + _rms_bwd(dh, xhat, r, g_ref[...])

    return pl.pallas_call(
        body, grid=(S // TR,), in_specs=[_row_spec(), _row_spec(), _row_spec(), _vec_spec()],
        out_specs=[_row_spec(), _vec_spec()],
        out_shape=[jax.ShapeDtypeStruct((S, D), F32), jax.ShapeDtypeStruct((1, D), F32)],
        compiler_params=_params(("arbitrary",)), name="pre0_bwd",
    )(dx1, dh0, x, g)


POOL_CH = 256


def _pool_apply(a, w, transpose):
    n = a.shape[0]
    row = lax.broadcasted_iota(jnp.int32, a.shape, 0)
    cnt = jnp.minimum(row + 1, w).astype(F32)
    s = a / cnt if transpose else a
    for k in (1, 2, 4, 8):
        if transpose:
            sh = jnp.where(row < n - k, pltpu.roll(s, n - k, 0), 0.0)
        else:
            sh = jnp.where(row >= k, pltpu.roll(s, k, 0), 0.0)
        s = jnp.where(w > k, s + sh, s)
    return s - a if transpose else s / cnt - a


def _pool_fwd(z0, pool_w, pool_scale):
    def body(a_ref, gate_ref, w_ref, sc_ref, out_ref):
        win = jnp.left_shift(2, pl.program_id(0))
        pooled = _pool_apply(a_ref[...], win, False)
        mixed = _dot(pooled.astype(BF16), w_ref[...])
        gate = gate_ref[...]
        out_ref[...] = (mixed * sc_ref[...] * (gate * _sigmoid(gate))).astype(BF16)

    return pl.pallas_call(
        body, grid=(4,),
        in_specs=[pl.BlockSpec((S, POOL_CH), lambda g: (0, g)), pl.BlockSpec((S, POOL_CH), lambda g: (0, 4 + g)),
                  pl.BlockSpec((None, POOL_CH, POOL_CH), lambda g: (g, 0, 0)),
                  pl.BlockSpec((1, POOL_CH), lambda g: (0, g))],
        out_specs=pl.BlockSpec((S, POOL_CH), lambda g: (0, g)),
        out_shape=jax.ShapeDtypeStruct((S, 2048), BF16),
        compiler_params=_params(("parallel",), VMEM_BIG), name="pool_fwd",
    )(z0, z0, pool_w, pool_scale)


def _pool_bwd(z0, dycat, pool_w, pool_scale):
    def body(a_ref, gate_ref, dy_ref, w_ref, sc_ref, da_ref, dgate_ref, dw_ref, dsc_ref):
        win = jnp.left_shift(2, pl.program_id(0))
        pooled = _pool_apply(a_ref[...], win, False).astype(BF16)
        w = w_ref[...]
        mixed = _dot(pooled, w)
        silu, dsilu = _silu_and_grad(gate_ref[...])
        dy = dy_ref[...]
        sc = sc_ref[...]
        dgate_ref[...] = (dy * (mixed * sc) * dsilu).astype(BF16)
        dms = dy * silu
        dsc_ref[...] = jnp.sum(dms * mixed, axis=0, keepdims=True)
        dmixed = (dms * sc).astype(BF16)
        dw_ref[...] = _dot(pooled, dmixed, TN)
        dpooled = _dot(dmixed, w, NT)
        da_ref[...] = _pool_apply(dpooled, win, True).astype(BF16)

    slab = lambda off: pl.BlockSpec((S, POOL_CH), lambda g: (0, off + g))
    return pl.pallas_call(
        body, grid=(4,),
        in_specs=[slab(0), slab(4), slab(0), pl.BlockSpec((None, POOL_CH, POOL_CH), lambda g: (g, 0, 0)),
                  pl.BlockSpec((1, POOL_CH), lambda g: (0, g))],
        out_specs=[slab(0), slab(0), pl.BlockSpec((None, POOL_CH, POOL_CH), lambda g: (g, 0, 0)),
                   pl.BlockSpec((1, POOL_CH), lambda g: (0, g))],
        out_shape=[jax.ShapeDtypeStruct((S, HALF), BF16), jax.ShapeDtypeStruct((S, HALF), BF16),
                   jax.ShapeDtypeStruct((4, POOL_CH, POOL_CH), F32), jax.ShapeDtypeStruct((1, HALF), F32)],
        compiler_params=_params(("parallel",), VMEM_BIG), name="pool_bwd",
    )(z0, z0, dycat, pool_w, pool_scale)


Q_COL, K_COL, V_COL, BG_COL = 2048 // 128, 5120 // 128, 8192 // 128, 11264 // 128
SCALE = HEAD_DIM ** -0.5


def _rope_tables():
    pos = jnp.arange(S, dtype=F32)
    inv_freq = jnp.power(ROPE_THETA, -jnp.arange(0, ROT_DIM, 2, dtype=F32) / ROT_DIM)
    ang = pos[:, None] * inv_freq[None, :]
    cos, sin = jnp.cos(ang), jnp.sin(ang)
    half = ROT_DIM // 2
    zeros = jnp.zeros((S, HEAD_DIM - ROT_DIM), F32)
    c = jnp.concatenate([cos, cos, jnp.ones((S, HEAD_DIM - ROT_DIM), F32)], axis=1)
    a = jnp.concatenate([-sin, jnp.zeros((S, half), F32), zeros], axis=1)
    b = jnp.concatenate([jnp.zeros((S, half), F32), sin, zeros], axis=1)
    return c, a, b


def _rope(t, c, a, b):
    half = ROT_DIM // 2
    return t * c + pltpu.roll(t, HEAD_DIM - half, 1) * a + pltpu.roll(t, half, 1) * b


def _rope_t(d, c, a, b):
    half = ROT_DIM // 2
    return d * c + pltpu.roll(d * a, half, 1) + pltpu.roll(d * b, HEAD_DIM - half, 1)


def _deinterleave(dst, src, dil, cast=None, dst_off=0):
    length = S // dil
    for r in range(dil):
        v = src[...] if dil == 1 else src[pl.ds(r, length, stride=dil), :]
        dst[dst_off + r * length:dst_off + (r + 1) * length, :] = v if cast is None else v.astype(cast)


def _interleave(dst, src, dil, src_off=0):
    length = S // dil
    for r in range(dil):
        if dil == 1:
            dst[...] = src[src_off:src_off + S, :]
        else:
            dst[pl.ds(r, length, stride=dil), :] = src[src_off + r * length:src_off + (r + 1) * length, :]


CU = 8
NUNITS = S // BLK
B_QK = (((2,), (2,)), ((0,), (0,)))
B_PV = (((2,), (1,)), ((0,), (0,)))
B_TN = (((1,), (1,)), ((0,), (0,)))


def _blocks(ref, first):
    return ref[first * BLK:(first + CU) * BLK, :].reshape(CU, BLK, HEAD_DIM)


def _chunk_scores(u0, nb, qd, kdp):
    q = _blocks(qd, u0)
    row = lax.broadcasted_iota(jnp.int32, (CU, BLK, BLK), 1)
    col = lax.broadcasted_iota(jnp.int32, (CU, BLK, BLK), 2)
    s_own = jnp.where(col <= row, _dot(q, _blocks(kdp, u0 + 1), B_QK) * SCALE, NEG)
    if nb == 1:
        return q, s_own, None
    unit = lax.broadcasted_iota(jnp.int32, (CU, BLK, BLK), 0) + u0
    s_prev = jnp.where((col >= row) & ((unit % nb) != 0), _dot(q, _blocks(kdp, u0), B_QK) * SCALE, NEG)
    return q, s_own, s_prev


def _qkv_prep(z0, tabs):
    def body(q_ref, k_ref, v_ref, c_ref, a_ref, b_ref, qo_ref, ko_ref, vo_ref, tmp):
        p = pl.program_id(1)
        for gi, (_, dil) in enumerate(PATTERNS):
            @pl.when(p == gi)
            def _(dil=dil):
                c, a, b = c_ref[...], a_ref[...], b_ref[...]
                tmp[...] = _rope(q_ref[...], c, a, b)
                _deinterleave(qo_ref, tmp, dil, BF16)
                tmp[...] = _rope(k_ref[...], c, a, b)
                _deinterleave(ko_ref, tmp, dil, BF16)
                _deinterleave(vo_ref, v_ref, dil, BF16)

    tab = pl.BlockSpec((S, HEAD_DIM), lambda h, p: (0, 0))
    out = pl.BlockSpec((S, HEAD_DIM), lambda h, p: (0, p * 8 + h))
    return pl.pallas_call(
        body, grid=(8, 3), in_specs=[_head_spec(Q_COL), _head_spec(K_COL), _head_spec(V_COL), tab, tab, tab],
        out_specs=[out, out, out], out_shape=[jax.ShapeDtypeStruct((S, 3072), BF16)] * 3,
        scratch_shapes=[pltpu.VMEM((S, HEAD_DIM), F32)],
        compiler_params=_params(("parallel", "arbitrary"), VMEM_BIG), name="qkv_prep",
    )(z0, z0, z0, *tabs)


def _pad_copy(dst, src):
    dst[0:BLK, :] = jnp.zeros((BLK, HEAD_DIM), dst.dtype)
    dst[BLK:BLK + S, :] = src[...]


def _attn_group_fwd(dil, qd, kd_ref, vd_ref, kdp, vdp, od, ld, og, lg):
    nb = S // dil // BLK
    _pad_copy(kdp, kd_ref)
    _pad_copy(vdp, vd_ref)
    for u0 in range(0, NUNITS, CU):
        _, s_own, s_prev = _chunk_scores(u0, nb, qd, kdp)
        m = jnp.max(s_own, axis=2, keepdims=True)
        if s_prev is not None:
            m = jnp.maximum(m, jnp.max(s_prev, axis=2, keepdims=True))
        p_own = jnp.exp(s_own - m)
        den = jnp.sum(p_own, axis=2, keepdims=True)
        acc = _dot(p_own.astype(BF16), _blocks(vdp, u0 + 1), B_PV)
        if s_prev is not None:
            p_prev = jnp.exp(s_prev - m)
            den = den + jnp.sum(p_prev, axis=2, keepdims=True)
            acc = acc + _dot(p_prev.astype(BF16), _blocks(vdp, u0), B_PV)
        rows = slice(u0 * BLK, (u0 + CU) * BLK)
        od[rows, :] = (acc / den).reshape(CU * BLK, HEAD_DIM)
        ld[rows, :] = jnp.broadcast_to(m + jnp.log(den), (CU, BLK, HEAD_DIM)).reshape(CU * BLK, HEAD_DIM)
    _interleave(og, od, dil)
    _interleave(lg, ld, dil)


def _group_weights(lgs):
    l0, l1, l2 = lgs[0][...], lgs[1][...], lgs[2][...]
    mx = jnp.maximum(l0, jnp.maximum(l1, l2))
    e0, e1, e2 = jnp.exp(l0 - mx), jnp.exp(l1 - mx), jnp.exp(l2 - mx)
    den = e0 + e1 + e2
    return e0 / den, e1 / den, e2 / den


def _head_spec(base):
    return pl.BlockSpec((S, HEAD_DIM), lambda h, p: (0, base + (p % 3) * 8 + h))


def _slab(dtype=F32, rows=S):
    return pltpu.VMEM((rows, HEAD_DIM), dtype)


def _attn_fwd(z0, qkv, ycat):
    def body(q_ref, k_ref, v_ref, gate_ref, ycat_ref, out_ref, og_ref, lg_ref,
             kdp, vdp, od, ld, og0, og1, og2, lg0, lg1, lg2):
        del ycat_ref
        p = pl.program_id(1)
        ogs, lgs = (og0, og1, og2), (lg0, lg1, lg2)
        for gi, (_, dil) in enumerate(PATTERNS):
            @pl.when(p == gi)
            def _(gi=gi, dil=dil):
                _attn_group_fwd(dil, q_ref, k_ref, v_ref, kdp, vdp, od, ld, ogs[gi], lgs[gi])
                og_ref[...] = ogs[gi][...]
                lg_ref[...] = lgs[gi][...]

        @pl.when(p == 2)
        def _():
            w0, w1, w2 = _group_weights(lgs)
            o = w0 * og0[...] + w1 * og1[...] + w2 * og2[...]
            gate = gate_ref[...]
            out_ref[...] = (o * (gate * _sigmoid(gate))).astype(BF16)

    grp = pl.BlockSpec((S, HEAD_DIM), lambda h, p: (0, p * 8 + h))
    return pl.pallas_call(
        body, grid=(8, 3),
        in_specs=[grp, grp, grp, pl.BlockSpec((S, HEAD_DIM), lambda h, p: (0, BG_COL + h)), ANY_SPEC],
        out_specs=[pl.BlockSpec((S, HEAD_DIM), lambda h, p: (0, 8 + h)), grp, grp],
        out_shape=[jax.ShapeDtypeStruct((S, 2048), BF16), jax.ShapeDtypeStruct((S, 3072), F32),
                   jax.ShapeDtypeStruct((S, 3072), F32)],
        scratch_shapes=[_slab(BF16, S + BLK), _slab(BF16, S + BLK)] + [_slab() for _ in range(8)],
        input_output_aliases={4: 0},
        compiler_params=_params(("parallel", "arbitrary"), VMEM_BIG), name="attn_fwd",
    )(*qkv, z0, ycat)


def _attn_bwd(z0, qkv, og, lg, dycat, tabs, dep):
    def body(q_ref, k_ref, v_ref, gate_ref, dy_ref, c_ref, a_ref, b_ref,
             og0_ref, og1_ref, og2_ref, lg0_ref, lg1_ref, lg2_ref, dep_ref,
             dq_ref, dk_ref, dv_ref, dbg_ref,
             tmp, kd, vd, ld, dg0, dg1, dg2, cg0, cg1, cg2, dod, cd, dqd, dkd, dvd):
        p = pl.program_id(1)
        ogs, lgs, dgs, cgs = (og0_ref, og1_ref, og2_ref), (lg0_ref, lg1_ref, lg2_ref), (dg0, dg1, dg2), (cg0, cg1, cg2)

        @pl.when(p == 0)
        def _():
            w = _group_weights(lgs)
            o = w[0] * ogs[0][...] + w[1] * ogs[1][...] + w[2] * ogs[2][...]
            silu, dsilu = _silu_and_grad(gate_ref[...])
            dy = dy_ref[...]
            dbg_ref[...] = (dy * o * dsilu).astype(BF16)
            do = dy * silu
            dwbar = jnp.sum(do * o, axis=1, keepdims=True)
            for gi in range(3):
                dgs[gi][...] = w[gi] * do
                cgs[gi][...] = -w[gi] * dwbar

        for gi, (_, dil) in enumerate(PATTERNS):
            @pl.when(p == 1 + gi)
            def _(gi=gi, dil=dil):
                nb = S // dil // BLK
                qd = q_ref
                c, a, b = c_ref[...], a_ref[...], b_ref[...]
                _pad_copy(kd, k_ref)
                _pad_copy(vd, v_ref)
                _deinterleave(dod, dgs[gi], dil, BF16)
                _deinterleave(ld, lgs[gi], dil)
                _deinterleave(cd, cgs[gi], dil)
                dkd[...] = jnp.zeros_like(dkd)
                dvd[...] = jnp.zeros_like(dvd)
                flat = lambda t: t.reshape(CU * BLK, HEAD_DIM)
                for u0 in range(0, NUNITS, CU):
                    q, s_own, s_prev = _chunk_scores(u0, nb, qd, kd)
                    lse, cv, do = _blocks(ld, u0), _blocks(cd, u0), _blocks(dod, u0)
                    own = slice((u0 + 1) * BLK, (u0 + 1 + CU) * BLK)
                    p_own = jnp.exp(s_own - lse)
                    ds_own = (p_own * (_dot(do, _blocks(vd, u0 + 1), B_QK) + cv) * SCALE).astype(BF16)
                    dq = _dot(ds_own, _blocks(kd, u0 + 1), B_PV)
                    dkd[own, :] += flat(_dot(ds_own, q, B_TN))
                    dvd[own, :] += flat(_dot(p_own.astype(BF16), do, B_TN))
                    if s_prev is not None:
                        prev = slice(u0 * BLK, (u0 + CU) * BLK)
                        p_prev = jnp.exp(s_prev - lse)
                        ds_prev = (p_prev * (_dot(do, _blocks(vd, u0), B_QK) + cv) * SCALE).astype(BF16)
                        dq = dq + _dot(ds_prev, _blocks(kd, u0), B_PV)
                        dkd[prev, :] += flat(_dot(ds_prev, q, B_TN))
                        dvd[prev, :] += flat(_dot(p_prev.astype(BF16), do, B_TN))
                    dqd[u0 * BLK:(u0 + CU) * BLK, :] = flat(dq)
                _interleave(tmp, dqd, dil)
                dq_ref[...] = _rope_t(tmp[...], c, a, b).astype(BF16)
                _interleave(tmp, dkd, dil, BLK)
                dk_ref[...] = _rope_t(tmp[...], c, a, b).astype(BF16)
                _interleave(tmp, dvd, dil, BLK)
                dv_ref[...] = tmp[...].astype(BF16)

    tab = pl.BlockSpec((S, HEAD_DIM), lambda h, p: (0, 0))
    hspec = lambda base: pl.BlockSpec((S, HEAD_DIM), lambda h, p: (0, base + h))
    gspec = pl.BlockSpec((S, HEAD_DIM), lambda h, p: (0, jnp.maximum(p - 1, 0) * 8 + h))
    return pl.pallas_call(
        body, grid=(8, 4),
        in_specs=[gspec, gspec, gspec, hspec(BG_COL), hspec(8), tab, tab, tab,
                  hspec(0), hspec(8), hspec(16), hspec(0), hspec(8), hspec(16), ANY_SPEC],
        out_specs=[gspec, gspec, gspec, hspec(0)],
        out_shape=[jax.ShapeDtypeStruct((S, 3072), BF16)] * 3 + [jax.ShapeDtypeStruct((S, HALF), BF16)],
        scratch_shapes=[_slab(), _slab(BF16, S + BLK), _slab(BF16, S + BLK), _slab()] + [_slab() for _ in range(6)]
                       + [_slab(BF16), _slab(), _slab(), _slab(F32, S + BLK), _slab(F32, S + BLK)],
        compiler_params=_params(("parallel", "arbitrary"), VMEM_BIG), name="attn_bwd",
    )(*qkv, z0, dycat, *tabs, og, og, og, lg, lg, lg, dep)


SGU_CH = 256
NCHUNK = TR // 128


def _ln_stats(x):
    mu = jnp.mean(x, axis=-1, keepdims=True)
    xc = x - mu
    r = lax.rsqrt(jnp.mean(xc * xc, axis=-1, keepdims=True) + EPS)
    return xc * r, r


def _ln_bwd(dy, xhat, r, g):
    dxh = dy * g
    return r * (dxh - jnp.mean(dxh, axis=-1, keepdims=True) - xhat * jnp.mean(dxh * xhat, axis=-1, keepdims=True))


def _tril_bf16(w):
    row = lax.broadcasted_iota(jnp.int32, w.shape, 0)
    col = lax.broadcasted_iota(jnp.int32, w.shape, 1)
    return jnp.where(row >= col, w, 0.0).astype(BF16)


def _sgu_gate(vn_s, s_s, w_ref, bb_ref):
    for h in range(4):
        wm = _tril_bf16(w_ref[h])
        bias = bb_ref[h]
        for ch in range(NCHUNK):
            rows, cols = slice(ch * 128, (ch + 1) * 128), slice(h * SGU_CH, (h + 1) * SGU_CH)
            s_s[rows, cols] = _dot(wm, vn_s[rows, cols]) + jnp.concatenate([bias, bias], axis=1)


WIN = HALO + TR
SUBL = 8


def _shifted_copies(dst, src):
    dst[0] = src[...]
    for b in range(1, SUBL):
        dst[b, 0:WIN - SUBL, :] = src[pl.ds(b, WIN - SUBL), :]


def _rows_at(copies, off, n):
    return copies[off % SUBL, pl.ds(off - off % SUBL, n), :]


def _conv_fwd(i, dval_ref, dglu_ref, hval_ref, hglu_ref, cw_ref, cb_ref, xw, xr, dcs):
    halo = hval_ref[...] * _sigmoid(hglu_ref[...])
    xw[0:HALO, :] = jnp.where(i > 0, halo, 0.0)
    xw[HALO:HALO + TR, :] = dval_ref[...] * _sigmoid(dglu_ref[...])
    _shifted_copies(xr, xw)
    for rb in range(TR // SUB):
        acc = jnp.broadcast_to(cb_ref[...], (SUB, HALF))
        for k in range(CONV_K):
            acc = acc + cw_ref[k:k + 1, :] * _rows_at(xr, rb * SUB + HALO - (CONV_K - 1) + k, SUB)
        dcs[rb * SUB:(rb + 1) * SUB, :] = acc


def _odd_in_specs():
    col = lambda j: pl.BlockSpec((TR, HALF), lambda i, *_: (i, j))
    prev = lambda j: pl.BlockSpec((HALO, HALF), lambda i, *_: (jnp.maximum(i * (TR // HALO) - 1, 0), j))
    return [col(0), col(1), col(2), col(3), col(4), col(5), prev(3), prev(4)]


def _full_spec(shape):
    return pl.BlockSpec(shape, lambda i, *_: (0,) * len(shape))


def _odd_fwd(z1, sgu_g, sgu_b, sgu_w, sgu_bb, conv_w, conv_b, cn_g, cn_b):
    def body(u_ref, v_ref, cg_ref, dval_ref, dglu_ref, dgate_ref, hval_ref, hglu_ref,
             g_ref, b_ref, w_ref, bb_ref, cw_ref, cb_ref, cng_ref, cnb_ref, out_ref, dcs, vn_s, s_s, xw, xr):
        i = pl.program_id(0)
        vhat, _ = _ln_stats(v_ref[...])
        vn_s[...] = (vhat * g_ref[...] + b_ref[...]).astype(BF16)
        _sgu_gate(vn_s, s_s, w_ref, bb_ref)
        cg = cg_ref[...]
        out_ref[:, 0:HALF] = (u_ref[...] * s_s[...] * (cg * _sigmoid(cg))).astype(BF16)
        _conv_fwd(i, dval_ref, dglu_ref, hval_ref, hglu_ref, cw_ref, cb_ref, xw, xr, dcs)
        dhat, _ = _ln_stats(dcs[...])
        dn = dhat * cng_ref[...] + cnb_ref[...]
        dgate = dgate_ref[...]
        out_ref[:, HALF:2 * HALF] = ((dn * _sigmoid(dn)) * (dgate * _sigmoid(dgate))).astype(BF16)

    vec = _full_spec((1, HALF))
    return pl.pallas_call(
        body, grid=(S // TR,),
        in_specs=_odd_in_specs() + [vec, vec, _full_spec((4, 128, 128)), _full_spec((4, 128, 128)),
                                    _full_spec((HALO, HALF)), vec, vec, vec],
        out_specs=[pl.BlockSpec((TR, 2048), lambda i: (i, 0)), pl.BlockSpec((TR, HALF), lambda i: (i, 0))],
        out_shape=[jax.ShapeDtypeStruct((S, 2048), BF16), jax.ShapeDtypeStruct((S, HALF), F32)],
        scratch_shapes=[pltpu.VMEM((TR, HALF), BF16), pltpu.VMEM((TR, HALF), F32),
                        pltpu.VMEM((WIN, HALF), F32), pltpu.VMEM((SUBL, WIN, HALF), F32)],
        compiler_params=_params(("parallel",), VMEM_BIG), name="odd_fwd",
    )(z1, z1, z1, z1, z1, z1, z1, z1, sgu_g, sgu_b, sgu_w, sgu_bb, conv_w, conv_b, cn_g, cn_b)


def _odd_bwd_a(z1, dc, dycat, sgu_g, sgu_b, sgu_w, sgu_bb, cn_g, cn_b):
    def body(u_ref, v_ref, cg_ref, dgate_ref, dcs, dy_ref, g_ref, b_ref, w_ref, bb_ref, cng_ref, cnb_ref,
             dz_ref, ddc_ref, dw_ref, dbb_ref, dg_ref, db_ref, dcng_ref, dcnb_ref, dcb_ref,
             vn_s, s_s, ds_s, dvn_s):
        i = pl.program_id(0)
        vhat, rv = _ln_stats(v_ref[...])
        g = g_ref[...]
        vn_s[...] = (vhat * g + b_ref[...]).astype(BF16)
        _sgu_gate(vn_s, s_s, w_ref, bb_ref)
        silu_c, dsilu_c = _silu_and_grad(cg_ref[...])
        dyc = dy_ref[:, 0:HALF]
        u = u_ref[...]
        s = s_s[...]
        dz_ref[:, 0:HALF] = (dyc * s * silu_c).astype(BF16)
        dz_ref[:, 2 * HALF:3 * HALF] = (dyc * u * s * dsilu_c).astype(BF16)
        ds_s[...] = dyc * u * silu_c

        @pl.when(i == 0)
        def _():
            dw_ref[...] = jnp.zeros_like(dw_ref)
            dbb_ref[...] = jnp.zeros_like(dbb_ref)

        tril = lax.broadcasted_iota(jnp.int32, (128, 128), 0) >= lax.broadcasted_iota(jnp.int32, (128, 128), 1)
        for h in range(4):
            wm = _tril_bf16(w_ref[h])
            for ch in range(NCHUNK):
                rows, cols = slice(ch * 128, (ch + 1) * 128), slice(h * SGU_CH, (h + 1) * SGU_CH)
                ds = ds_s[rows, cols]
                dsb = ds.astype(BF16)
                dw_ref[h] += jnp.where(tril, _dot(dsb, vn_s[rows, cols], NT), 0.0)
                dbb_ref[h] += jnp.broadcast_to(jnp.sum(ds, axis=1, keepdims=True), (128, 128))
                dvn_s[rows, cols] = _dot(wm, dsb, TN)
        dvn = dvn_s[...]
        _acc_rows(dg_ref, dvn * vhat, i)
        _acc_rows(db_ref, dvn, i)
        dz_ref[:, HALF:2 * HALF] = _ln_bwd(dvn, vhat, rv, g).astype(BF16)

        dhat, rd = _ln_stats(dcs[...])
        cng = cng_ref[...]
        silu_n, dsilu_n = _silu_and_grad(dhat * cng + cnb_ref[...])
        silu_g, dsilu_g = _silu_and_grad(dgate_ref[...])
        dyd = dy_ref[:, HALF:2 * HALF]
        dz_ref[:, 5 * HALF:6 * HALF] = (dyd * silu_n * dsilu_g).astype(BF16)
        ddn = dyd * silu_g * dsilu_n
        _acc_rows(dcng_ref, ddn * dhat, i)
        _acc_rows(dcnb_ref, ddn, i)
        ddc = _ln_bwd(ddn, dhat, rd, cng)
        ddc_ref[...] = ddc
        _acc_rows(dcb_ref, ddc, i)

    vec = _full_spec((1, HALF))
    sq = _full_spec((4, 128, 128))
    col = lambda j: pl.BlockSpec((TR, HALF), lambda i: (i, j))
    return pl.pallas_call(
        body, grid=(S // TR,),
        in_specs=[col(0), col(1), col(2), col(5), col(0), pl.BlockSpec((TR, 2048), lambda i: (i, 0)),
                  vec, vec, sq, sq, vec, vec],
        out_specs=[pl.BlockSpec((TR, ODD_IN), lambda i: (i, 0)), pl.BlockSpec((TR, HALF), lambda i: (i, 0)),
                   sq, sq, vec, vec, vec, vec, vec],
        out_shape=[jax.ShapeDtypeStruct((S, ODD_IN), BF16), jax.ShapeDtypeStruct((S, HALF), F32),
                   jax.ShapeDtypeStruct((4, 128, 128), F32), jax.ShapeDtypeStruct((4, 128, 128), F32)]
                  + [jax.ShapeDtypeStruct((1, HALF), F32)] * 5,
        scratch_shapes=[pltpu.VMEM((TR, HALF), BF16), pltpu.VMEM((TR, HALF), F32),
                        pltpu.VMEM((TR, HALF), F32), pltpu.VMEM((TR, HALF), F32)],
        compiler_params=_params(("arbitrary",), VMEM_BIG), name="odd_bwd_a",
    )(z1, z1, z1, z1, dc, dycat, sgu_g, sgu_b, sgu_w, sgu_bb, cn_g, cn_b)


def _odd_bwd_b(z1, ddc, dz1, conv_w):
    nt = S // TR

    def body(dval_ref, dglu_ref, hval_ref, hglu_ref, ddc_ref, hddc_ref, cw_ref, dz_in_ref,
             dz_ref, dcw_ref, xw, dwin, dxs, xr, dr):
        del dz_in_ref
        i, j = pl.program_id(0), pl.program_id(1)
        sg = _sigmoid(dglu_ref[...])
        dval = dval_ref[...]

        @pl.when(j == 0)
        def _():
            halo = hval_ref[...] * _sigmoid(hglu_ref[...])
            xw[0:HALO, :] = jnp.where(i > 0, halo, 0.0)
            xw[HALO:HALO + TR, :] = dval * sg
            dwin[0:TR, :] = ddc_ref[...]
            dwin[TR:TR + HALO, :] = jnp.where(i < nt - 1, hddc_ref[...], 0.0)
            _shifted_copies(xr, xw)
            _shifted_copies(dr, dwin)

            @pl.when(i == 0)
            def _():
                dcw_ref[...] = jnp.zeros_like(dcw_ref)

            for rb in range(TR // SUB):
                acc = jnp.zeros((SUB, HALF), F32)
                for k in range(CONV_K):
                    acc = acc + cw_ref[k:k + 1, :] * _rows_at(dr, rb * SUB + (CONV_K - 1) - k, SUB)
                dxs[rb * SUB:(rb + 1) * SUB, :] = acc
            for k in range(CONV_K):
                acc = jnp.zeros((SUB, HALF), F32)
                for rb in range(TR // SUB):
                    acc = acc + dwin[rb * SUB:(rb + 1) * SUB, :] * _rows_at(xr, rb * SUB + HALO - (CONV_K - 1) + k, SUB)
                dcw_ref[k:k + 1, :] += jnp.sum(acc, axis=0, keepdims=True)
            dz_ref[...] = (dxs[...] * sg).astype(BF16)

        @pl.when(j == 1)
        def _():
            dz_ref[...] = (dxs[...] * dval * sg * (1.0 - sg)).astype(BF16)

    col = lambda c: pl.BlockSpec((TR, HALF), lambda i, j: (i, c))
    prev = lambda c: pl.BlockSpec((HALO, HALF), lambda i, j: (jnp.maximum(i * (TR // HALO) - 1, 0), c))
    nxt = pl.BlockSpec((HALO, HALF), lambda i, j: (jnp.minimum((i + 1) * (TR // HALO), S // HALO - 1), 0))
    return pl.pallas_call(
        body, grid=(nt, 2),
        in_specs=[col(3), col(4), prev(3), prev(4), pl.BlockSpec((TR, HALF), lambda i, j: (i, 0)), nxt,
                  _full_spec((HALO, HALF)), pl.BlockSpec(memory_space=pl.ANY)],
        out_specs=[pl.BlockSpec((TR, HALF), lambda i, j: (i, 3 + j)), _full_spec((HALO, HALF))],
        out_shape=[jax.ShapeDtypeStruct((S, ODD_IN), BF16), jax.ShapeDtypeStruct((HALO, HALF), F32)],
        scratch_shapes=[pltpu.VMEM((WIN, HALF), F32), pltpu.VMEM((WIN, HALF), F32), pltpu.VMEM((TR, HALF), F32),
                        pltpu.VMEM((SUBL, WIN, HALF), F32), pltpu.VMEM((SUBL, WIN, HALF), F32)],
        input_output_aliases={7: 0},
        compiler_params=_params(("arbitrary", "arbitrary"), VMEM_BIG), name="odd_bwd_b",
    )(z1, z1, z1, z1, ddc, ddc, conv_w, dz1)


def _cast_bf16(w, name, piece=0, npieces=1):
    r, c = w.shape[0], w.shape[1] // npieces
    tr = min(r, 256)

    def body(i_ref, o_ref):
        o_ref[...] = i_ref[...].astype(BF16)

    return pl.pallas_call(
        body, grid=(r // tr,), in_specs=[pl.BlockSpec((tr, c), lambda i: (i, piece))],
        out_specs=pl.BlockSpec((tr, c), lambda i: (i, 0)), out_shape=jax.ShapeDtypeStruct((r, c), BF16),
        compiler_params=_params(("parallel",)), name=name,
    )(w)


def _adamw(w, g, m, v):
    m = ADAM_B1 * m + (1.0 - ADAM_B1) * g
    v = ADAM_B2 * v + (1.0 - ADAM_B2) * (g * g)
    m_hat = m / (1.0 - ADAM_B1 ** ADAM_STEP)
    v_hat = v / (1.0 - ADAM_B2 ** ADAM_STEP)
    delta = -ADAM_LR * (m_hat / (jnp.sqrt(v_hat) + ADAM_EPS) + ADAM_WD * w)
    return delta, m, v


def _adam_reduce(parts, w, m, v, name, dep=None, piece=0, npieces=1, prev=None):
    r, c = w.shape
    cp = c // npieces
    tr = min(r, 128)
    extra = ([] if dep is None else [dep]) + ([] if prev is None else list(prev))
    nparts = parts.shape[0]

    def body(p_ref, w_ref, m_ref, v_ref, *rest):
        g_ref, d_ref, nm_ref, nv_ref = rest[len(extra):]
        g = p_ref[0].astype(F32)
        for d in range(1, nparts):
            g = g + p_ref[d].astype(F32)
        g_ref[...] = g
        d_ref[...], nm_ref[...], nv_ref[...] = _adamw(w_ref[...], g, m_ref[...], v_ref[...])

    spec = pl.BlockSpec((tr, cp), lambda i: (i, piece))
    first = 4 + (0 if dep is None else 1)
    return pl.pallas_call(
        body, grid=(r // tr,),
        in_specs=[pl.BlockSpec((nparts, tr, cp), lambda i: (0, i, 0)), spec, spec, spec] + [ANY_SPEC] * len(extra),
        out_specs=[spec] * 4, out_shape=[jax.ShapeDtypeStruct((r, c), F32)] * 4,
        input_output_aliases={} if prev is None else {first + k: k for k in range(4)},
        compiler_params=_params(("parallel",), VMEM_BIG), name=name,
    )(parts, w, m, v, *extra)


def _arrived(x, name, dep=None):
    deps = [] if dep is None else [dep]

    def body(*refs):
        refs[-1][...] = jnp.zeros_like(refs[-1])

    return pl.pallas_call(
        body, in_specs=[ANY_SPEC] * (1 + len(deps)), out_specs=pl.BlockSpec(memory_space=pltpu.VMEM),
        out_shape=jax.ShapeDtypeStruct((8, 128), F32), name=name,
    )(x, *deps)


def _sum_parts(parts, name, dep=None):
    r = parts.shape[1]
    tr = 8
    for cand in (512, 256, 128, 64, 32, 16, 8):
        if r % cand == 0:
            tr = cand
            break
    deps = [] if dep is None else [dep]

    def body(p_ref, *rest):
        g = p_ref[0]
        for d in range(1, NDEV):
            g = g + p_ref[d]
        rest[-1][...] = g

    return pl.pallas_call(
        body, grid=(r // tr,), in_specs=[pl.BlockSpec((NDEV, tr, 128), lambda i: (0, i, 0))] + [ANY_SPEC] * len(deps),
        out_specs=pl.BlockSpec((tr, 128), lambda i: (i, 0)), out_shape=jax.ShapeDtypeStruct((r, 128), F32),
        compiler_params=_params(("parallel",)), name=name,
    )(parts, *deps)


def _sum_unpack(parts, rows, name, dep=None):
    deps = [] if dep is None else [dep]

    def body(p_ref, *outs):
        outs = outs[len(deps):]
        off = 0
        for o_ref, n in zip(outs, rows):
            acc = p_ref[0, off:off + n, :]
            for d in range(1, NDEV):
                acc = acc + p_ref[d, off:off + n, :]
            o_ref[...] = acc
            off += n

    return pl.pallas_call(
        body, grid=(1,), in_specs=[pl.BlockSpec(parts.shape, lambda i: (0, 0, 0))] + [ANY_SPEC] * len(deps),
        out_specs=[pl.BlockSpec((n, 128), lambda i: (0, 0)) for n in rows],
        out_shape=[jax.ShapeDtypeStruct((n, 128), F32) for n in rows],
        compiler_params=_params(("arbitrary",), VMEM_BIG), name=name,
    )(parts, *deps)


def _adam_small(ws, gs, g_specs, ms, vs, name):
    n = len(ws)

    def body(*refs):
        w_r, g_r, m_r, v_r = refs[:n], refs[n:2 * n], refs[2 * n:3 * n], refs[3 * n:4 * n]
        outs = refs[4 * n:]
        for i in range(n):
            g = g_r[i][...]
            outs[4 * i][...] = g
            outs[4 * i + 1][...], outs[4 * i + 2][...], outs[4 * i + 3][...] = _adamw(
                w_r[i][...], g, m_r[i][...], v_r[i][...])

    whole = lambda a: pl.BlockSpec(a.shape, lambda i, nd=a.ndim: (0,) * nd)
    outs = pl.pallas_call(
        body, grid=(1,),
        in_specs=[whole(a) for a in ws] + list(g_specs) + [whole(a) for a in ms] + [whole(a) for a in vs],
        out_specs=[whole(a) for a in ws for _ in range(4)],
        out_shape=[jax.ShapeDtypeStruct(a.shape, F32) for a in ws for _ in range(4)],
        compiler_params=_params(("arbitrary",), VMEM_BIG), name=name,
    )(*ws, *gs, *ms, *vs)
    return [outs[4 * i:4 * i + 4] for i in range(n)]


MASKS = [(mx, my, mc) for mx in (0, 1) for my in (0, 1) for mc in (0, 1)][1:]


def _sc_exchange(name, collective_id, arrays, scatter):
    nt = len(arrays)
    out_type = [jax.ShapeDtypeStruct(a.shape if scatter else (NDEV,) + a.shape, a.dtype) for a in arrays]

    def body(*refs):
        ins, outs = refs[:nt], refs[nt:2 * nt]
        send_sems, recv_sems, local_sems = refs[2 * nt:3 * nt], refs[3 * nt:4 * nt], refs[4 * nt:5 * nt]
        x, y, c = lax.axis_index("x"), lax.axis_index("y"), lax.axis_index("c")
        peers = [(mx + x - 2 * mx * x, my + y - 2 * my * y, mc + c - 2 * mc * c) for mx, my, mc in MASKS]
        barrier = pltpu.get_barrier_semaphore()
        for peer in peers:
            pl.semaphore_signal(barrier, inc=1, device_id=peer, device_id_type=MESH)
        pl.semaphore_wait(barrier, len(peers))
        me = 4 * x + 2 * y + c
        own = []
        for t in range(nt):
            cp = pltpu.make_async_copy(ins[t].at[me] if scatter else ins[t], outs[t].at[me], local_sems[t])
            cp.start()
            own.append(cp)
            for px, py, pc in peers:
                src = ins[t].at[4 * px + 2 * py + pc] if scatter else ins[t]
                pltpu.make_async_remote_copy(src_ref=src, dst_ref=outs[t].at[me], send_sem=send_sems[t],
                                             recv_sem=recv_sems[t], device_id=(px, py, pc), device_id_type=MESH).start()
        for t in range(nt):
            own[t].wait()
            seven = outs[t].at[pl.ds(0, NDEV - 1)]
            drain = pltpu.make_async_remote_copy(src_ref=seven, dst_ref=seven, send_sem=send_sems[t],
                                                 recv_sem=recv_sems[t], device_id=(x, y, c), device_id_type=MESH)
            drain.wait_send()
            drain.wait_recv()

    return pl.kernel(
        body, out_type=out_type, mesh=plsc.ScalarSubcoreMesh(axis_name="sequencer", num_cores=1),
        scratch_types=[pltpu.SemaphoreType.DMA] * (3 * nt),
        compiler_params=pltpu.CompilerParams(collective_id=collective_id), name=name,
    )(*arrays)


def _sc_gather_two_level(name, collective_id, arrays):
    nt = len(arrays)
    out_type = [jax.ShapeDtypeStruct((NDEV,) + a.shape, a.dtype) for a in arrays]

    def body(*refs):
        ins, outs = refs[:nt], refs[nt:2 * nt]
        sems = refs[2 * nt:]
        send_sems, sib_sems, local_sems = sems[:nt], sems[nt:2 * nt], sems[2 * nt:3 * nt]
        ici_sems = [sems[3 * nt + 3 * t:3 * nt + 3 * t + 3] for t in range(nt)]
        x, y, c = lax.axis_index("x"), lax.axis_index("y"), lax.axis_index("c")
        sibling = (x, y, 1 - c)
        chips = [(1 - x, y), (x, 1 - y), (1 - x, 1 - y)]
        barrier = pltpu.get_barrier_semaphore()
        for peer in [sibling] + [(cx, cy, c) for cx, cy in chips]:
            pl.semaphore_signal(barrier, inc=1, device_id=peer, device_id_type=MESH)
        pl.semaphore_wait(barrier, 4)
        me = 4 * x + 2 * y + c

        def push(t, src, slot, recv_sem, to):
            pltpu.make_async_remote_copy(src_ref=src, dst_ref=outs[t].at[slot], send_sem=send_sems[t],
                                         recv_sem=recv_sem, device_id=to, device_id_type=MESH).start()

        own = []
        for t in range(nt):
            cp = pltpu.make_async_copy(ins[t], outs[t].at[me], local_sems[t])
            cp.start()
            own.append(cp)
            for j, (cx, cy) in enumerate(chips):
                push(t, ins[t], me, ici_sems[t][j], (cx, cy, c))
            push(t, ins[t], me, sib_sems[t], sibling)
        for t in range(nt):
            for j, (cx, cy) in enumerate(chips):
                slot = 4 * cx + 2 * cy + c
                landed = outs[t].at[slot]
                pltpu.make_async_remote_copy(src_ref=landed, dst_ref=landed, send_sem=send_sems[t],
                                             recv_sem=ici_sems[t][j], device_id=(cx, cy, c),
                                             device_id_type=MESH).wait_recv()
                push(t, landed, slot, sib_sems[t], sibling)
        for t in range(nt):
            own[t].wait()
            four, seven = outs[t].at[pl.ds(0, 4)], outs[t].at[pl.ds(0, 7)]
            pltpu.make_async_remote_copy(src_ref=four, dst_ref=four, send_sem=send_sems[t], recv_sem=sib_sems[t],
                                         device_id=sibling, device_id_type=MESH).wait_recv()
            pltpu.make_async_remote_copy(src_ref=seven, dst_ref=seven, send_sem=send_sems[t], recv_sem=sib_sems[t],
                                         device_id=sibling, device_id_type=MESH).wait_send()

    return pl.kernel(
        body, out_type=out_type, mesh=plsc.ScalarSubcoreMesh(axis_name="sequencer", num_cores=1),
        scratch_types=[pltpu.SemaphoreType.DMA] * (6 * nt),
        compiler_params=pltpu.CompilerParams(collective_id=collective_id), name=name,
    )(*arrays)


def _sc_sibling_exchange(name, collective_id, src, out_shape, pieces):
    def body(src_ref, out_ref, send_sem, recv_sem):
        x, y, c = lax.axis_index("x"), lax.axis_index("y"), lax.axis_index("c")
        sibling = (x, y, 1 - c)
        barrier = pltpu.get_barrier_semaphore()
        pl.semaphore_signal(barrier, inc=1, device_id=sibling, device_id_type=MESH)
        pl.semaphore_wait(barrier, 1)
        for piece, lands in pieces(c, src_ref, out_ref):
            pltpu.make_async_remote_copy(src_ref=piece, dst_ref=lands, send_sem=send_sem, recv_sem=recv_sem,
                                         device_id=sibling, device_id_type=MESH).start()
        drain = pltpu.make_async_remote_copy(src_ref=out_ref, dst_ref=out_ref, send_sem=send_sem, recv_sem=recv_sem,
                                             device_id=sibling, device_id_type=MESH)
        drain.wait_send()
        drain.wait_recv()

    return pl.kernel(
        body, out_type=jax.ShapeDtypeStruct(out_shape, src.dtype),
        mesh=plsc.ScalarSubcoreMesh(axis_name="sequencer", num_cores=1), scratch_types=[pltpu.SemaphoreType.DMA] * 2,
        compiler_params=pltpu.CompilerParams(collective_id=collective_id), name=name,
    )(src)


def _swap_class_columns(name, collective_id, dz, nb, piece=0, npieces=1):
    w = nb // npieces
    return _sc_sibling_exchange(
        name, collective_id, dz, (S, 4 * w),
        lambda c, src, out: [(src.at[:, pl.ds((2 * j + 1 - c) * nb + piece * w, w)], out.at[:, pl.ds(j * w, w)])
                             for j in range(4)])


def _sc_chip_scatter(name, collective_id, q):
    def body(q_ref, out_ref, send_sem, recv_sem, local_sem):
        x, y, c = lax.axis_index("x"), lax.axis_index("y"), lax.axis_index("c")
        chips = [(1 - x, y), (x, 1 - y), (1 - x, 1 - y)]
        barrier = pltpu.get_barrier_semaphore()
        for cx, cy in chips:
            pl.semaphore_signal(barrier, inc=1, device_id=(cx, cy, c), device_id_type=MESH)
        pl.semaphore_wait(barrier, 3)
        mine = 2 * x + y
        own = pltpu.make_async_copy(q_ref.at[mine], out_ref.at[mine], local_sem)
        own.start()
        for cx, cy in chips:
            pltpu.make_async_remote_copy(src_ref=q_ref.at[2 * cx + cy], dst_ref=out_ref.at[mine], send_sem=send_sem,
                                         recv_sem=recv_sem, device_id=(cx, cy, c), device_id_type=MESH).start()
        own.wait()
        three = out_ref.at[pl.ds(0, 3)]
        drain = pltpu.make_async_remote_copy(src_ref=three, dst_ref=three, send_sem=send_sem, recv_sem=recv_sem,
                                             device_id=(x, y, c), device_id_type=MESH)
        drain.wait_send()
        drain.wait_recv()

    return pl.kernel(
        body, out_type=jax.ShapeDtypeStruct(q.shape, q.dtype),
        mesh=plsc.ScalarSubcoreMesh(axis_name="sequencer", num_cores=1), scratch_types=[pltpu.SemaphoreType.DMA] * 3,
        compiler_params=pltpu.CompilerParams(collective_id=collective_id), name=name,
    )(q)


def _mm_pair_dw(h_own, dz, h_sib, dz_sib, nb, name, dep=None, piece=0, npieces=1, h_transposed=False,
                one_call=False):
    nb = nb // npieces
    tn = 512 if nb % 512 == 0 else nb
    per = nb // tn
    dn = NN if h_transposed else TN
    o_spec = pl.BlockSpec((None, D, tn), lambda i, j, k: (j // per, 0, j % per))
    own_col = lambda i, j, k: (0, ((2 * (j // per) + lax.axis_index("c")) * npieces + piece) * per + j % per)
    if one_call:
        def fused(a0_ref, b0_ref, a1_ref, b1_ref, dep_ref, o_ref):
            acc = _dot(a0_ref[...], b0_ref[...], dn) + _dot(a1_ref[...], b1_ref[...], dn)
            o_ref[...] = acc.astype(BF16)

        whole = pl.BlockSpec((S, D), lambda i, j, k: (0, 0), pipeline_mode=pl.Buffered(1))
        return pl.pallas_call(
            fused, grid=(1, 4 * per, 1),
            in_specs=[whole, pl.BlockSpec((S, tn), own_col), whole, pl.BlockSpec((S, tn), lambda i, j, k: (0, j)),
                      ANY_SPEC],
            out_specs=o_spec, out_shape=jax.ShapeDtypeStruct((4, D, nb), BF16),
            compiler_params=_params(("parallel", "parallel", "arbitrary"), VMEM_BIG), name=name,
        )(h_own, dz, h_sib, dz_sib, dep)
    part = _matmul(
        h_own, dz, dn=dn, grid=(1, 4 * per, 1),
        a_spec=pl.BlockSpec((S, D), lambda i, j, k: (0, 0)), b_spec=pl.BlockSpec((S, tn), own_col),
        o_spec=o_spec, out_shape=(4, D, nb), out_dtype=F32, acc_shape=(D, tn), name=name + "_own", dep=dep)

    def body(a_ref, b_ref, p_ref, o_ref):
        o_ref[...] = (p_ref[...] + _dot(a_ref[...], b_ref[...], dn)).astype(BF16)

    return pl.pallas_call(
        body, grid=(1, 4 * per, 1),
        in_specs=[pl.BlockSpec((S, D), lambda i, j, k: (0, 0)), pl.BlockSpec((S, tn), lambda i, j, k: (0, j)), o_spec],
        out_specs=o_spec, out_shape=jax.ShapeDtypeStruct((4, D, nb), BF16),
        compiler_params=_params(("parallel", "parallel", "arbitrary"), VMEM_BIG), name=name + "_sibling",
    )(h_sib, dz_sib, part)


SMALL = {
    "e_pre_norm": ((2048,), None), "e_pool_w": ((4, 256, 256), 1), "e_pool_scale": ((1024,), None),
    "e_post_norm": ((2048,), None), "o_pre_norm": ((2048,), 0), "o_sgu_norm_g": ((1024,), 0),
    "o_sgu_norm_b": ((1024,), 0), "o_sgu_w": ((4, 128, 128), None), "o_sgu_b": ((4, 128), None),
    "o_conv_w": ((31, 1024), 1), "o_conv_b": ((1024,), 0), "o_conv_norm_g": ((1024,), 0),
    "o_conv_norm_b": ((1024,), 0), "o_post_norm": ((2048,), 0),
}
SMALL_SHARDED = [n for n, (_, ax) in SMALL.items() if ax is not None]


def _shard_shape(name):
    shape, ax = SMALL[name]
    if ax is None:
        return shape
    return tuple(s // NDEV if i == ax else s for i, s in enumerate(shape))


def _pack(arrs, row_multiple=1):
    flat = jnp.concatenate([a.reshape(-1) for a in arrs])
    pad = -flat.shape[0] % (128 * row_multiple)
    return jnp.concatenate([flat, jnp.zeros((pad,), F32)]).reshape(-1, 128)


def _small_views(name):
    shape, ax = SMALL[name]
    me = lambda: 4 * lax.axis_index("x") + 2 * lax.axis_index("y") + lax.axis_index("c")
    if ax is None:
        view = (int(np.prod(shape)) // 128, 128)
        return view, view, pl.BlockSpec(view, lambda i: (0, 0))
    if len(shape) == 1:
        n = shape[0] // NDEV
        return (1, n), (NDEV, 1, n), pl.BlockSpec((None, 1, n), lambda i: (me(), 0, 0))
    part = _shard_shape(name)
    return part, shape, pl.BlockSpec(part, lambda i: tuple(me() if d == ax else 0 for d in range(len(shape))))


WEIGHTS = ["e_pre_norm", "e_w_in", "e_pool_w", "e_pool_scale", "e_w_out", "e_post_norm", "o_pre_norm", "o_w_in",
           "o_sgu_norm_g", "o_sgu_norm_b", "o_sgu_w", "o_sgu_b", "o_conv_w", "o_conv_b", "o_conv_norm_g",
           "o_conv_norm_b", "o_w_out", "o_post_norm"]


def kernel(x, e_pre_norm, e_w_in, e_pool_w, e_pool_scale, e_w_out, e_post_norm, o_pre_norm, o_w_in, o_sgu_norm_g, o_sgu_norm_b, o_sgu_w, o_sgu_b, o_conv_w, o_conv_b, o_conv_norm_g, o_conv_norm_b, o_w_out, o_post_norm, loss_target, m_e_pre_norm, m_e_w_in, m_e_pool_w, m_e_pool_scale, m_e_w_out, m_e_post_norm, m_o_pre_norm, m_o_w_in, m_o_sgu_norm_g, m_o_sgu_norm_b, m_o_sgu_w, m_o_sgu_b, m_o_conv_w, m_o_conv_b, m_o_conv_norm_g, m_o_conv_norm_b, m_o_w_out, m_o_post_norm, v_e_pre_norm, v_e_w_in, v_e_pool_w, v_e_pool_scale, v_e_w_out, v_e_post_norm, v_o_pre_norm, v_o_w_in, v_o_sgu_norm_g, v_o_sgu_norm_b, v_o_sgu_w, v_o_sgu_b, v_o_conv_w, v_o_conv_b, v_o_conv_norm_g, v_o_conv_norm_b, v_o_w_out, v_o_post_norm):
    given = dict(locals())
    w = {n: given[n][0] for n in WEIGHTS}
    m = {n: given["m_" + n][0] for n in WEIGHTS}
    v = {n: given["v_" + n][0] for n in WEIGHTS}
    me = 4 * lax.axis_index("x") + 2 * lax.axis_index("y") + lax.axis_index("c")
    x, target = x[0], loss_target[0]
    row = lambda a: a.reshape(1, -1)

    lo, small_rows = _sc_gather_two_level(
        "gather_a0", 0, [_cast_bf16(w["e_w_in"], "cast_e_w_in_0", 0, 2), _pack([w[n] for n in SMALL_SHARDED])])
    hi, = _sc_gather_two_level("gather_a1", 12, [_cast_bf16(w["e_w_in"], "cast_e_w_in_1", 1, 2)])
    wg_e_in = (lo, hi)
    h0, h0t = _pre0_fwd(x, row(w["e_pre_norm"]))
    wg_e_out, = _sc_gather_two_level("gather_b", 1, [_cast_bf16(w["e_w_out"], "cast_e_w_out")])
    wg_o_in, wg_o_out = _sc_gather_two_level(
        "gather_c", 13, [_cast_bf16(w[n], "cast_" + n) for n in ("o_w_in", "o_w_out")])
    h0t_sib = _sc_sibling_exchange("swap_h0", 8, h0t, h0t.shape, lambda c, src, out: [(src, out)])
    p = {n: w[n] for n in SMALL if SMALL[n][1] is None}
    small_rows = small_rows.reshape(NDEV, -1)
    off = 0
    for n in SMALL_SHARDED:
        shp, ax = _shard_shape(n), SMALL[n][1]
        cnt = int(np.prod(shp))
        blk = small_rows[:, off:off + cnt].reshape((NDEV,) + shp)
        p[n] = jnp.moveaxis(blk, 0, ax).reshape(SMALL[n][0])
        off += cnt
    tabs = _rope_tables()
    pool_w_bf = p["e_pool_w"].astype(BF16)
    sgu_bb = jnp.broadcast_to(p["o_sgu_b"][:, :, None], (4, 128, 128))
    conv_w = jnp.concatenate([p["o_conv_w"], jnp.zeros((HALO - CONV_K, HALF), F32)], axis=0)
    odd_p = (row(p["o_sgu_norm_g"]), row(p["o_sgu_norm_b"]), p["o_sgu_w"], sgu_bb, conv_w,
             row(p["o_conv_b"]), row(p["o_conv_norm_g"]), row(p["o_conv_norm_b"]))

    z0 = _mm_in_halves(h0, wg_e_in, "mm_z0")
    ycat0 = _pool_fwd(z0, pool_w_bf, row(p["e_pool_scale"]))
    qkv = _qkv_prep(z0, tabs)
    ycat0, og, lg = _attn_fwd(z0, qkv, ycat0)
    w_out_e, w_out_o = wg_e_out.reshape(2048, D), wg_o_out.reshape(2048, D)
    y0, x1, h1 = _post0_fwd(ycat0, w_out_e, x, row(p["e_post_norm"]), row(p["o_pre_norm"]), h0t_sib)
    h1_sib = _sc_sibling_exchange("swap_h1", 11, h1, h1.shape, lambda c, src, out: [(src, out)])
    z1 = _mm_in(h1, wg_o_in, "mm_z1")
    ycat1, conv_out = _odd_fwd(z1, *odd_p)

    g = {}
    loss, dx2, dy1, g["o_post_norm"] = _post1_bwd(ycat1, w_out_o, x1, target, row(p["o_post_norm"]), h1_sib)
    loss = lax.psum(loss[0, 0], ("x", "y", "c"))
    parts = {}
    dw = _mm_out_dw(ycat1, dy1, "mm_dwout1").reshape(NDEV, 256, D)
    parts["o_w_out"], = _sc_exchange("scatter_o_w_out", 2, [dw], True)
    dycat1 = _mm_out_dx(dy1, w_out_o, "mm_dycat1", (dw, loss.reshape(1, 1)))
    dz1, ddc, g["o_sgu_w"], d_sgu_bb, g["o_sgu_norm_g"], g["o_sgu_norm_b"], g["o_conv_norm_g"], \
        g["o_conv_norm_b"], g["o_conv_b"] = _odd_bwd_a(z1, conv_out, dycat1, *odd_p[:4], *odd_p[6:])
    dz1, d_conv_w = _odd_bwd_b(z1, ddc, dz1, conv_w)
    g["o_sgu_b"] = d_sgu_bb[:, :, 0]
    g["o_conv_w"] = d_conv_w[:CONV_K]
    grads, deltas, new_m, new_v = {}, {}, {}, {}

    def adam(n, dep):
        grads[n], deltas[n], new_m[n], new_v[n] = _adam_reduce(parts[n], w[n], m[n], v[n], "adam_" + n, dep)
        return new_v[n]

    pin = _arrived(parts["o_w_out"], "arrived_o_w_out", d_conv_w)
    dz1_sib = _swap_class_columns("swap_dz1", 10, dz1, ODD_IN // NDEV)
    dw = _mm_pair_dw(h1, dz1, h1_sib, dz1_sib, ODD_IN // NDEV, "mm_dwin1", pin)
    parts["o_w_in"] = _sc_chip_scatter("scatter_o_w_in", 3, dw)
    dh1 = _mm_in_dx(dz1, wg_o_in, "mm_dh1", dw)
    dx1, dy0, g["o_pre_norm"], g["e_post_norm"] = _mid_bwd(dx2, dh1, x1, y0, row(p["o_pre_norm"]),
                                                           row(p["e_post_norm"]))
    dw = _mm_out_dw(ycat0, dy0, "mm_dwout0").reshape(NDEV, 256, D)
    parts["e_w_out"], = _sc_exchange("scatter_e_w_out", 4, [dw], True)
    dycat0 = _mm_out_dx(dy0, w_out_e, "mm_dycat0", dw)
    da_in, da_gate, g["e_pool_w"], g["e_pool_scale"] = _pool_bwd(z0, dycat0, pool_w_bf, row(p["e_pool_scale"]))
    late = [n for n in SMALL if n not in ("e_pre_norm", "o_sgu_b")] + ["o_sgu_b"]
    recv_small, = _sc_gather_two_level("gather_small_grads", 6,
                                       [_pack([g[n].reshape(SMALL[n][0]) for n in late], 512)])
    took = _arrived(parts["o_w_in"], "arrived_o_w_in")
    dq, dk, dv, dbg = _attn_bwd(z0, qkv, og, lg, dycat0, tabs, took)
    dz0 = jnp.concatenate([da_in, da_gate, dq, dk, dv, dbg], axis=1)
    took = _arrived(recv_small, "arrived_small_grads", _arrived(parts["e_w_out"], "arrived_e_w_out", dz0))
    nb = EVEN_IN // NDEV
    swapped = [_swap_class_columns("swap_dz0_%d" % half, (9, 14)[half], dz0, nb, half, 2) for half in (0, 1)]
    dw, e_w_in_parts = took, []
    for half in (0, 1):
        dw = _mm_pair_dw(h0t, dz0, h0t_sib, swapped[half], nb, "mm_dwin0_%d" % half, dw, half, 2, True, half == 1)
        e_w_in_parts.append(_sc_chip_scatter("scatter_e_w_in_%d" % half, (5, 15)[half], dw))
    pin = adam("e_w_out", adam("o_w_out", adam("o_w_in", dw)))
    rows = [int(np.prod(SMALL[n][0])) // 128 for n in late]
    summed = dict(zip(late, _sum_unpack(recv_small, rows, "sum_small_grads", pin)))
    dh0 = _mm_in_dx_halves(dz0, wg_e_in, "mm_dh0", summed[late[0]])
    grad_x, g["e_pre_norm"] = _pre0_bwd(dx1, dh0, x, row(p["e_pre_norm"]))
    last, = _sc_exchange("gather_e_pre_norm_grad", 7, [g["e_pre_norm"].reshape(16, 128)], False)

    n = "e_w_in"
    out = _adam_reduce(e_w_in_parts[0], w[n], m[n], v[n], "adam_e_w_in_0", grad_x, 0, 2)
    out = _adam_reduce(e_w_in_parts[1], w[n], m[n], v[n], "adam_e_w_in_1", None, 1, 2, out)
    grads[n], deltas[n], new_m[n], new_v[n] = out
    summed["e_pre_norm"] = _sum_parts(last, "sum_e_pre_norm_grad", out[3])
    names = list(SMALL)
    views = [_small_views(n) for n in names]
    mine = lambda src: [src[n].reshape(vw[0]) for n, vw in zip(names, views)]
    res = _adam_small(mine(w), [summed[n].reshape(vw[1]) for n, vw in zip(names, views)], [vw[2] for vw in views],
                      mine(m), mine(v), "adam_small")
    for n, out in zip(names, res):
        grads[n], deltas[n], new_m[n], new_v[n] = [t.reshape(_shard_shape(n)) for t in out]

    lead = lambda a: a[None]
    return (loss, grad_x[None], *[lead(grads[n]) for n in WEIGHTS], *[lead(deltas[n]) for n in WEIGHTS],
            *[lead(new_m[n]) for n in WEIGHTS], *[lead(new_v[n]) for n in WEIGHTS])
```

```python
import numpy as np
import jax
import jax.numpy as jnp
from jax import lax
from jax.experimental import pallas as pl
from jax.experimental.pallas import tpu as pltpu
from jax.experimental.pallas import tpu_sc as plsc

F32 = jnp.float32
BF16 = jnp.bfloat16

S = 2048
D = 2048
NDEV = 8
EPS = 1e-6
NEG = -1e30
HEAD_DIM = 128
ROT_DIM = 32
ROPE_THETA = 500000.0
PATTERNS = ((128, 1), (512, 4), (2048, 16))
BLK = 128
EVEN_IN = 12288
ODD_IN = 6144
HALF = 1024
CONV_K = 31
HALO = 32
TR = 256
SUB = 32

ADAM_LR = 0.001
ADAM_B1 = 0.9
ADAM_B2 = 0.999
ADAM_EPS = 1e-08
ADAM_WD = 0.01
ADAM_STEP = 10

VMEM_BIG = 56 * 1024 * 1024
MESH = pl.DeviceIdType.MESH

NN = (((1,), (0,)), ((), ()))
NT = (((1,), (1,)), ((), ()))
TN = (((0,), (0,)), ((), ()))


def _dot(a, b, dn=NN):
    return lax.dot_general(a, b, dn, preferred_element_type=F32)


def _sigmoid(x):
    return 1.0 / (1.0 + jnp.exp(-x))


def _silu_and_grad(x):
    sg = _sigmoid(x)
    return x * sg, sg * (1.0 + x * (1.0 - sg))


def _params(sem, vmem=None):
    return pltpu.CompilerParams(dimension_semantics=sem, vmem_limit_bytes=vmem)


ANY_SPEC = pl.BlockSpec(memory_space=pl.ANY)


def _matmul(a, b, *, dn, grid, a_spec, b_spec, o_spec, out_shape, out_dtype, acc_shape, name, dep=None):
    nk = grid[2]
    deps = [] if dep is None else list(dep) if isinstance(dep, (tuple, list)) else [dep]

    def body(a_ref, b_ref, *rest):
        o_ref, acc = rest[len(deps)], rest[len(deps) + 1:]
        if nk == 1:
            o_ref[...] = _dot(a_ref[...], b_ref[...], dn).astype(o_ref.dtype)
            return
        acc_ref = acc[0]
        k = pl.program_id(2)

        @pl.when(k == 0)
        def _():
            acc_ref[...] = jnp.zeros_like(acc_ref)

        acc_ref[...] += _dot(a_ref[...], b_ref[...], dn)

        @pl.when(k == nk - 1)
        def _():
            o_ref[...] = acc_ref[...].astype(o_ref.dtype)

    return pl.pallas_call(
        body, grid=grid, in_specs=[a_spec, b_spec] + [ANY_SPEC] * len(deps), out_specs=o_spec,
        out_shape=jax.ShapeDtypeStruct(out_shape, out_dtype),
        scratch_shapes=[] if nk == 1 else [pltpu.VMEM(acc_shape, F32)],
        compiler_params=_params(("parallel", "parallel", "arbitrary"), VMEM_BIG), name=name,
    )(a, b, *deps)


TM = 2048


def _mm_in(h, wg, name):
    nb = wg.shape[2]
    tn = 512 if nb % 512 == 0 else nb
    per = nb // tn
    return _matmul(
        h, wg, dn=NN, grid=(S // TM, NDEV * per, 1),
        a_spec=pl.BlockSpec((TM, D), lambda i, j, k: (i, 0)),
        b_spec=pl.BlockSpec((None, D, tn), lambda i, j, k: (j // per, 0, j % per)),
        o_spec=pl.BlockSpec((TM, tn), lambda i, j, k: (i, j)),
        out_shape=(S, NDEV * nb), out_dtype=F32, acc_shape=(TM, tn), name=name)


def _mm_in_halves(h, wg_halves, name):
    hb = wg_halves[0].shape[2]
    z = None
    for half, wg in enumerate(wg_halves):
        prev = [] if z is None else [z]

        def body(a_ref, b_ref, *rest):
            rest[-1][...] = _dot(a_ref[...], b_ref[...])

        z = pl.pallas_call(
            body, grid=(NDEV,),
            in_specs=[pl.BlockSpec((S, D), lambda j: (0, 0)), pl.BlockSpec((None, D, hb), lambda j: (j, 0, 0))]
                     + [ANY_SPEC] * len(prev),
            out_specs=pl.BlockSpec((S, hb), lambda j, half=half: (0, 2 * j + half)),
            out_shape=jax.ShapeDtypeStruct((S, 2 * NDEV * hb), F32),
            input_output_aliases={2: 0} if prev else {},
            compiler_params=_params(("parallel",), VMEM_BIG), name="%s_%d" % (name, half),
        )(h, wg, *prev)
    return z


def _mm_in_dx_halves(dz, wg_halves, name, dep):
    hb = wg_halves[0].shape[2]
    nk = 2 * NDEV

    def body(a_ref, b0_ref, b1_ref, dep_ref, o_ref, acc_ref):
        k = pl.program_id(2)

        @pl.when(k == 0)
        def _():
            acc_ref[...] = jnp.zeros_like(acc_ref)

        @pl.when(k % 2 == 0)
        def _():
            acc_ref[...] += _dot(a_ref[...], b0_ref[...], NT)

        @pl.when(k % 2 == 1)
        def _():
            acc_ref[...] += _dot(a_ref[...], b1_ref[...], NT)

        @pl.when(k == nk - 1)
        def _():
            o_ref[...] = acc_ref[...]

    b_spec = pl.BlockSpec((None, 1024, hb), lambda i, j, k: (k // 2, j, 0))
    return pl.pallas_call(
        body, grid=(1, D // 1024, nk),
        in_specs=[pl.BlockSpec((S, hb), lambda i, j, k: (0, k)), b_spec, b_spec, ANY_SPEC],
        out_specs=pl.BlockSpec((S, 1024), lambda i, j, k: (0, j)), out_shape=jax.ShapeDtypeStruct((S, D), F32),
        scratch_shapes=[pltpu.VMEM((S, 1024), F32)],
        compiler_params=_params(("parallel", "parallel", "arbitrary"), VMEM_BIG), name=name,
    )(dz, *wg_halves, dep)


def _mm_in_dx(dz, wg, name, dep=None):
    nb = wg.shape[2]
    return _matmul(
        dz, wg, dn=NT, grid=(S // TM, D // 1024, NDEV),
        a_spec=pl.BlockSpec((TM, nb), lambda i, j, k: (i, k)),
        b_spec=pl.BlockSpec((None, 1024, nb), lambda i, j, k: (k, j, 0)),
        o_spec=pl.BlockSpec((TM, 1024), lambda i, j, k: (i, j)),
        out_shape=(S, D), out_dtype=F32, acc_shape=(TM, 1024), name=name, dep=dep)


def _mm_out_dx(dy, w, name, dep=None):
    return _matmul(
        dy, w, dn=NT, grid=(S // TM, 2048 // 512, 1),
        a_spec=pl.BlockSpec((TM, D), lambda i, j, k: (i, 0)),
        b_spec=pl.BlockSpec((512, D), lambda i, j, k: (j, 0)),
        o_spec=pl.BlockSpec((TM, 512), lambda i, j, k: (i, j)),
        out_shape=(S, 2048), out_dtype=F32, acc_shape=(TM, 512), name=name, dep=dep)


def _mm_out_dw(yc, dy, name):
    return _matmul(
        yc, dy, dn=TN, grid=(2048 // TM, D // 512, 1),
        a_spec=pl.BlockSpec((S, TM), lambda i, j, k: (0, i)),
        b_spec=pl.BlockSpec((S, 512), lambda i, j, k: (0, j)),
        o_spec=pl.BlockSpec((TM, 512), lambda i, j, k: (i, j)),
        out_shape=(2048, D), out_dtype=BF16, acc_shape=(TM, 512), name=name)


def _row_spec(w=D):
    return pl.BlockSpec((TR, w), lambda i: (i, 0))


def _vec_spec(w=D):
    return pl.BlockSpec((1, w), lambda i: (0, 0))


def _rms_stats(x):
    r = lax.rsqrt(jnp.mean(x * x, axis=-1, keepdims=True) + EPS)
    return x * r, r


def _rms_bwd(dn, xhat, r, g):
    dxh = dn * g
    return r * (dxh - xhat * jnp.mean(dxh * xhat, axis=-1, keepdims=True))


def _acc_rows(ref, val, i):
    s = jnp.sum(val, axis=0, keepdims=True)

    @pl.when(i == 0)
    def _():
        ref[...] = s

    @pl.when(i > 0)
    def _():
        ref[...] += s


def _pre0_fwd(x, g):
    def body(x_ref, g_ref, h_ref, ht_ref):
        xhat, _ = _rms_stats(x_ref[...])
        h = xhat * g_ref[...]
        h_ref[...] = h.astype(BF16)
        ht_ref[...] = h.T.astype(BF16)

    return pl.pallas_call(
        body, grid=(S // TR,), in_specs=[_row_spec(), _vec_spec()],
        out_specs=[_row_spec(), pl.BlockSpec((D, TR), lambda i: (0, i))],
        out_shape=[jax.ShapeDtypeStruct((S, D), BF16), jax.ShapeDtypeStruct((D, S), BF16)],
        compiler_params=_params(("parallel",)), name="pre0_fwd",
    )(x, g)


def _post0_fwd(ycat, w_out, x, g_post, g_pre1, dep):
    def body(yc_ref, w_ref, x_ref, gp_ref, g1_ref, dep_ref, y_ref, x1_ref, h1_ref):
        y = _dot(yc_ref[...], w_ref[...])
        y_ref[...] = y
        yhat, _ = _rms_stats(y)
        x1 = x_ref[...] + yhat * gp_ref[...]
        x1_ref[...] = x1
        xhat, _ = _rms_stats(x1)
        h1_ref[...] = (xhat * g1_ref[...]).astype(BF16)

    return pl.pallas_call(
        body, grid=(S // TR,),
        in_specs=[_row_spec(), pl.BlockSpec((2048, D), lambda i: (0, 0)), _row_spec(), _vec_spec(), _vec_spec(),
                  ANY_SPEC],
        out_specs=[_row_spec(), _row_spec(), _row_spec()],
        out_shape=[jax.ShapeDtypeStruct((S, D), F32), jax.ShapeDtypeStruct((S, D), F32),
                   jax.ShapeDtypeStruct((S, D), BF16)],
        compiler_params=_params(("parallel",), VMEM_BIG), name="post0_fwd",
    )(ycat, w_out, x, g_post, g_pre1, dep)


def _post1_bwd(ycat, w_out, x1, target, g_post, dep):
    def body(yc_ref, w_ref, x1_ref, t_ref, g_ref, dep_ref, loss_ref, dx2_ref, dy_ref, dg_ref):
        i = pl.program_id(0)
        yhat, r = _rms_stats(_dot(yc_ref[...], w_ref[...]))
        g = g_ref[...]
        err = x1_ref[...] + yhat * g - t_ref[...]
        part = jnp.sum(jnp.sum(err * err, axis=-1, keepdims=True), axis=0, keepdims=True) * (0.5 / D)
        _acc_rows(loss_ref, jnp.broadcast_to(part, (1, 128)), i)
        dx2 = err * (1.0 / D)
        dx2_ref[...] = dx2
        _acc_rows(dg_ref, dx2 * yhat, i)
        dy_ref[...] = _rms_bwd(dx2, yhat, r, g).astype(BF16)

    return pl.pallas_call(
        body, grid=(S // TR,),
        in_specs=[_row_spec(), pl.BlockSpec((2048, D), lambda i: (0, 0)), _row_spec(), _row_spec(), _vec_spec(),
                  ANY_SPEC],
        out_specs=[_vec_spec(128), _row_spec(), _row_spec(), _vec_spec()],
        out_shape=[jax.ShapeDtypeStruct((1, 128), F32), jax.ShapeDtypeStruct((S, D), F32),
                   jax.ShapeDtypeStruct((S, D), BF16), jax.ShapeDtypeStruct((1, D), F32)],
        compiler_params=_params(("arbitrary",), VMEM_BIG), name="post1_bwd",
    )(ycat, w_out, x1, target, g_post, dep)


def _mid_bwd(dx2, dh1, x1, y0, g_pre1, g_post0):
    def body(dx2_ref, dh_ref, x1_ref, y_ref, g1_ref, gp_ref, dx1_ref, dy_ref, dg1_ref, dgp_ref):
        i = pl.program_id(0)
        xhat, r1 = _rms_stats(x1_ref[...])
        dh = dh_ref[...]
        _acc_rows(dg1_ref, dh * xhat, i)
        dx1 = dx2_ref[...] + _rms_bwd(dh, xhat, r1, g1_ref[...])
        dx1_ref[...] = dx1
        yhat, r0 = _rms_stats(y_ref[...])
        _acc_rows(dgp_ref, dx1 * yhat, i)
        dy_ref[...] = _rms_bwd(dx1, yhat, r0, gp_ref[...]).astype(BF16)

    return pl.pallas_call(
        body, grid=(S // TR,),
        in_specs=[_row_spec(), _row_spec(), _row_spec(), _row_spec(), _vec_spec(), _vec_spec()],
        out_specs=[_row_spec(), _row_spec(), _vec_spec(), _vec_spec()],
        out_shape=[jax.ShapeDtypeStruct((S, D), F32), jax.ShapeDtypeStruct((S, D), BF16),
                   jax.ShapeDtypeStruct((1, D), F32), jax.ShapeDtypeStruct((1, D), F32)],
        compiler_params=_params(("arbitrary",)), name="mid_bwd",
    )(dx2, dh1, x1, y0, g_pre1, g_post0)


def _pre0_bwd(dx1, dh0, x, g):
    def body(dx1_ref, dh_ref, x_ref, g_ref, gx_ref, dg_ref):
        i = pl.program_id(0)
        xhat, r = _rms_stats(x_ref[...])
        dh = dh_ref[...]
        _acc_rows(dg_ref, dh * xhat, i)
        gx_ref[...] = dx1_ref[...] + _rms_bwd(dh, xhat, r, g_ref[...])

    return pl.pallas_call(
        body, grid=(S // TR,), in_specs=[_row_spec(), _row_spec(), _row_spec(), _vec_spec()],
        out_specs=[_row_spec(), _vec_spec()],
        out_shape=[jax.ShapeDtypeStruct((S, D), F32), jax.ShapeDtypeStruct((1, D), F32)],
        compiler_params=_params(("arbitrary",)), name="pre0_bwd",
    )(dx1, dh0, x, g)


POOL_CH = 256


def _pool_apply(a, w, transpose):
    n = a.shape[0]
    row = lax.broadcasted_iota(jnp.int32, a.shape, 0)
    cnt = jnp.minimum(row + 1, w).astype(F32)
    s = a / cnt if transpose else a
    for k in (1, 2, 4, 8):
        if transpose:
            sh = jnp.where(row < n - k, pltpu.roll(s, n - k, 0), 0.0)
        else:
            sh = jnp.where(row >= k, pltpu.roll(s, k, 0), 0.0)
        s = jnp.where(w > k, s + sh, s)
    return s - a if transpose else s / cnt - a


def _pool_fwd(z0, pool_w, pool_scale):
    def body(a_ref, gate_ref, w_ref, sc_ref, out_ref):
        win = jnp.left_shift(2, pl.program_id(0))
        pooled = _pool_apply(a_ref[...], win, False)
        mixed = _dot(pooled.astype(BF16), w_ref[...])
        gate = gate_ref[...]
        out_ref[...] = (mixed * sc_ref[...] * (gate * _sigmoid(gate))).astype(BF16)

    return pl.pallas_call(
        body, grid=(4,),
        in_specs=[pl.BlockSpec((S, POOL_CH), lambda g: (0, g)), pl.BlockSpec((S, POOL_CH), lambda g: (0, 4 + g)),
                  pl.BlockSpec((None, POOL_CH, POOL_CH), lambda g: (g, 0, 0)),
                  pl.BlockSpec((1, POOL_CH), lambda g: (0, g))],
        out_specs=pl.BlockSpec((S, POOL_CH), lambda g: (0, g)),
        out_shape=jax.ShapeDtypeStruct((S, 2048), BF16),
        compiler_params=_params(("parallel",), VMEM_BIG), name="pool_fwd",
    )(z0, z0, pool_w, pool_scale)


def _pool_bwd(z0, dycat, pool_w, pool_scale):
    def body(a_ref, gate_ref, dy_ref, w_ref, sc_ref, da_ref, dgate_ref, dw_ref, dsc_ref):
        win = jnp.left_shift(2, pl.program_id(0))
        pooled = _pool_apply(a_ref[...], win, False).astype(BF16)
        w = w_ref[...]
        mixed = _dot(pooled, w)
        silu, dsilu = _silu_and_grad(gate_ref[...])
        dy = dy_ref[...]
        sc = sc_ref[...]
        dgate_ref[...] = (dy * (mixed * sc) * dsilu).astype(BF16)
        dms = dy * silu
        dsc_ref[...] = jnp.sum(dms * mixed, axis=0, keepdims=True)
        dmixed = (dms * sc).astype(BF16)
        dw_ref[...] = _dot(pooled, dmixed, TN)
        dpooled = _dot(dmixed, w, NT)
        da_ref[...] = _pool_apply(dpooled, win, True).astype(BF16)

    slab = lambda off: pl.BlockSpec((S, POOL_CH), lambda g: (0, off + g))
    return pl.pallas_call(
        body, grid=(4,),
        in_specs=[slab(0), slab(4), slab(0), pl.BlockSpec((None, POOL_CH, POOL_CH), lambda g: (g, 0, 0)),
                  pl.BlockSpec((1, POOL_CH), lambda g: (0, g))],
        out_specs=[slab(0), slab(0), pl.BlockSpec((None, POOL_CH, POOL_CH), lambda g: (g, 0, 0)),
                   pl.BlockSpec((1, POOL_CH), lambda g: (0, g))],
        out_shape=[jax.ShapeDtypeStruct((S, HALF), BF16), jax.ShapeDtypeStruct((S, HALF), BF16),
                   jax.ShapeDtypeStruct((4, POOL_CH, POOL_CH), F32), jax.ShapeDtypeStruct((1, HALF), F32)],
        compiler_params=_params(("parallel",), VMEM_BIG), name="pool_bwd",
    )(z0, z0, dycat, pool_w, pool_scale)


Q_COL, K_COL, V_COL, BG_COL = 2048 // 128, 5120 // 128, 8192 // 128, 11264 // 128
SCALE = HEAD_DIM ** -0.5


def _rope_tables():
    pos = jnp.arange(S, dtype=F32)
    inv_freq = jnp.power(ROPE_THETA, -jnp.arange(0, ROT_DIM, 2, dtype=F32) / ROT_DIM)
    ang = pos[:, None] * inv_freq[None, :]
    cos, sin = jnp.cos(ang), jnp.sin(ang)
    half = ROT_DIM // 2
    zeros = jnp.zeros((S, HEAD_DIM - ROT_DIM), F32)
    c = jnp.concatenate([cos, cos, jnp.ones((S, HEAD_DIM - ROT_DIM), F32)], axis=1)
    a = jnp.concatenate([-sin, jnp.zeros((S, half), F32), zeros], axis=1)
    b = jnp.concatenate([jnp.zeros((S, half), F32), sin, zeros], axis=1)
    return c, a, b


def _rope(t, c, a, b):
    half = ROT_DIM // 2
    return t * c + pltpu.roll(t, HEAD_DIM - half, 1) * a + pltpu.roll(t, half, 1) * b


def _rope_t(d, c, a, b):
    half = ROT_DIM // 2
    return d * c + pltpu.roll(d * a, half, 1) + pltpu.roll(d * b, HEAD_DIM - half, 1)


def _deinterleave(dst, src, dil, cast=None, dst_off=0):
    length = S // dil
    for r in range(dil):
        v = src[...] if dil == 1 else src[pl.ds(r, length, stride=dil), :]
        dst[dst_off + r * length:dst_off + (r + 1) * length, :] = v if cast is None else v.astype(cast)


def _interleave(dst, src, dil, src_off=0):
    length = S // dil
    for r in range(dil):
        if dil == 1:
            dst[...] = src[src_off:src_off + S, :]
        else:
            dst[pl.ds(r, length, stride=dil), :] = src[src_off + r * length:src_off + (r + 1) * length, :]


CU = 8
NUNITS = S // BLK
B_QK = (((2,), (2,)), ((0,), (0,)))
B_PV = (((2,), (1,)), ((0,), (0,)))
B_TN = (((1,), (1,)), ((0,), (0,)))


def _blocks(ref, first):
    return ref[first * BLK:(first + CU) * BLK, :].reshape(CU, BLK, HEAD_DIM)


def _chunk_scores(u0, nb, qd, kdp):
    q = _blocks(qd, u0)
    row = lax.broadcasted_iota(jnp.int32, (CU, BLK, BLK), 1)
    col = lax.broadcasted_iota(jnp.int32, (CU, BLK, BLK), 2)
    s_own = jnp.where(col <= row, _dot(q, _blocks(kdp, u0 + 1), B_QK) * SCALE, NEG)
    if nb == 1:
        return q, s_own, None
    unit = lax.broadcasted_iota(jnp.int32, (CU, BLK, BLK), 0) + u0
    s_prev = jnp.where((col >= row) & ((unit % nb) != 0), _dot(q, _blocks(kdp, u0), B_QK) * SCALE, NEG)
    return q, s_own, s_prev


def _qkv_prep(z0, tabs):
    def body(q_ref, k_ref, v_ref, c_ref, a_ref, b_ref, qo_ref, ko_ref, vo_ref, tmp):
        p = pl.program_id(1)
        for gi, (_, dil) in enumerate(PATTERNS):
            @pl.when(p == gi)
            def _(dil=dil):
                c, a, b = c_ref[...], a_ref[...], b_ref[...]
                tmp[...] = _rope(q_ref[...], c, a, b)
                _deinterleave(qo_ref, tmp, dil, BF16)
                tmp[...] = _rope(k_ref[...], c, a, b)
                _deinterleave(ko_ref, tmp, dil, BF16)
                _deinterleave(vo_ref, v_ref, dil, BF16)

    tab = pl.BlockSpec((S, HEAD_DIM), lambda h, p: (0, 0))
    out = pl.BlockSpec((S, HEAD_DIM), lambda h, p: (0, p * 8 + h))
    return pl.pallas_call(
        body, grid=(8, 3), in_specs=[_head_spec(Q_COL), _head_spec(K_COL), _head_spec(V_COL), tab, tab, tab],
        out_specs=[out, out, out], out_shape=[jax.ShapeDtypeStruct((S, 3072), BF16)] * 3,
        scratch_shapes=[pltpu.VMEM((S, HEAD_DIM), F32)],
        compiler_params=_params(("parallel", "arbitrary"), VMEM_BIG), name="qkv_prep",
    )(z0, z0, z0, *tabs)


def _pad_copy(dst, src):
    dst[0:BLK, :] = jnp.zeros((BLK, HEAD_DIM), dst.dtype)
    dst[BLK:BLK + S, :] = src[...]


def _attn_group_fwd(dil, qd, kd_ref, vd_ref, kdp, vdp, od, ld, og, lg):
    nb = S // dil // BLK
    _pad_copy(kdp, kd_ref)
    _pad_copy(vdp, vd_ref)
    for u0 in range(0, NUNITS, CU):
        _, s_own, s_prev = _chunk_scores(u0, nb, qd, kdp)
        m = jnp.max(s_own, axis=2, keepdims=True)
        if s_prev is not None:
            m = jnp.maximum(m, jnp.max(s_prev, axis=2, keepdims=True))
        p_own = jnp.exp(s_own - m)
        den = jnp.sum(p_own, axis=2, keepdims=True)
        acc = _dot(p_own.astype(BF16), _blocks(vdp, u0 + 1), B_PV)
        if s_prev is not None:
            p_prev = jnp.exp(s_prev - m)
            den = den + jnp.sum(p_prev, axis=2, keepdims=True)
            acc = acc + _dot(p_prev.astype(BF16), _blocks(vdp, u0), B_PV)
        rows = slice(u0 * BLK, (u0 + CU) * BLK)
        od[rows, :] = (acc / den).reshape(CU * BLK, HEAD_DIM)
        ld[rows, :] = jnp.broadcast_to(m + jnp.log(den), (CU, BLK, HEAD_DIM)).reshape(CU * BLK, HEAD_DIM)
    _interleave(og, od, dil)
    _interleave(lg, ld, dil)


def _group_weights(lgs):
    l0, l1, l2 = lgs[0][...], lgs[1][...], lgs[2][...]
    mx = jnp.maximum(l0, jnp.maximum(l1, l2))
    e0, e1, e2 = jnp.exp(l0 - mx), jnp.exp(l1 - mx), jnp.exp(l2 - mx)
    den = e0 + e1 + e2
    return e0 / den, e1 / den, e2 / den


def _head_spec(base):
    return pl.BlockSpec((S, HEAD_DIM), lambda h, p: (0, base + (p % 3) * 8 + h))


def _slab(dtype=F32, rows=S):
    return pltpu.VMEM((rows, HEAD_DIM), dtype)


def _attn_fwd(z0, qkv, ycat):
    def body(q_ref, k_ref, v_ref, gate_ref, ycat_ref, out_ref, og_ref, lg_ref,
             kdp, vdp, od, ld, og0, og1, og2, lg0, lg1, lg2):
        del ycat_ref
        p = pl.program_id(1)
        ogs, lgs = (og0, og1, og2), (lg0, lg1, lg2)
        for gi, (_, dil) in enumerate(PATTERNS):
            @pl.when(p == gi)
            def _(gi=gi, dil=dil):
                _attn_group_fwd(dil, q_ref, k_ref, v_ref, kdp, vdp, od, ld, ogs[gi], lgs[gi])
                og_ref[...] = ogs[gi][...]
                lg_ref[...] = lgs[gi][...]

        @pl.when(p == 2)
        def _():
            w0, w1, w2 = _group_weights(lgs)
            o = w0 * og0[...] + w1 * og1[...] + w2 * og2[...]
            gate = gate_ref[...]
            out_ref[...] = (o * (gate * _sigmoid(gate))).astype(BF16)

    grp = pl.BlockSpec((S, HEAD_DIM), lambda h, p: (0, p * 8 + h))
    return pl.pallas_call(
        body, grid=(8, 3),
        in_specs=[grp, grp, grp, pl.BlockSpec((S, HEAD_DIM), lambda h, p: (0, BG_COL + h)), ANY_SPEC],
        out_specs=[pl.BlockSpec((S, HEAD_DIM), lambda h, p: (0, 8 + h)), grp, grp],
        out_shape=[jax.ShapeDtypeStruct((S, 2048), BF16), jax.ShapeDtypeStruct((S, 3072), F32),
                   jax.ShapeDtypeStruct((S, 3072), F32)],
        scratch_shapes=[_slab(BF16, S + BLK), _slab(BF16, S + BLK)] + [_slab() for _ in range(8)],
        input_output_aliases={4: 0},
        compiler_params=_params(("parallel", "arbitrary"), VMEM_BIG), name="attn_fwd",
    )(*qkv, z0, ycat)


def _attn_bwd(z0, qkv, og, lg, dycat, tabs, dep):
    def body(q_ref, k_ref, v_ref, gate_ref, dy_ref, c_ref, a_ref, b_ref,
             og0_ref, og1_ref, og2_ref, lg0_ref, lg1_ref, lg2_ref, dep_ref,
             dq_ref, dk_ref, dv_ref, dbg_ref,
             tmp, kd, vd, ld, dg0, dg1, dg2, cg0, cg1, cg2, dod, cd, dqd, dkd, dvd):
        p = pl.program_id(1)
        ogs, lgs, dgs, cgs = (og0_ref, og1_ref, og2_ref), (lg0_ref, lg1_ref, lg2_ref), (dg0, dg1, dg2), (cg0, cg1, cg2)

        @pl.when(p == 0)
        def _():
            w = _group_weights(lgs)
            o = w[0] * ogs[0][...] + w[1] * ogs[1][...] + w[2] * ogs[2][...]
            silu, dsilu = _silu_and_grad(gate_ref[...])
            dy = dy_ref[...]
            dbg_ref[...] = (dy * o * dsilu).astype(BF16)
            do = dy * silu
            dwbar = jnp.sum(do * o, axis=1, keepdims=True)
            for gi in range(3):
                dgs[gi][...] = w[gi] * do
                cgs[gi][...] = -w[gi] * dwbar

        for gi, (_, dil) in enumerate(PATTERNS):
            @pl.when(p == 1 + gi)
            def _(gi=gi, dil=dil):
                nb = S // dil // BLK
                qd = q_ref
                c, a, b = c_ref[...], a_ref[...], b_ref[...]
                _pad_copy(kd, k_ref)
                _pad_copy(vd, v_ref)
                _deinterleave(dod, dgs[gi], dil, BF16)
                _deinterleave(ld, lgs[gi], dil)
                _deinterleave(cd, cgs[gi], dil)
                dkd[...] = jnp.zeros_like(dkd)
                dvd[...] = jnp.zeros_like(dvd)
                flat = lambda t: t.reshape(CU * BLK, HEAD_DIM)
                for u0 in range(0, NUNITS, CU):
                    q, s_own, s_prev = _chunk_scores(u0, nb, qd, kd)
                    lse, cv, do = _blocks(ld, u0), _blocks(cd, u0), _blocks(dod, u0)
                    own = slice((u0 + 1) * BLK, (u0 + 1 + CU) * BLK)
                    p_own = jnp.exp(s_own - lse)
                    ds_own = (p_own * (_dot(do, _blocks(vd, u0 + 1), B_QK) + cv) * SCALE).astype(BF16)
                    dq = _dot(ds_own, _blocks(kd, u0 + 1), B_PV)
                    dkd[own, :] += flat(_dot(ds_own, q, B_TN))
                    dvd[own, :] += flat(_dot(p_own.astype(BF16), do, B_TN))
                    if s_prev is not None:
                        prev = slice(u0 * BLK, (u0 + CU) * BLK)
                        p_prev = jnp.exp(s_prev - lse)
                        ds_prev = (p_prev * (_dot(do, _blocks(vd, u0), B_QK) + cv) * SCALE).astype(BF16)
                        dq = dq + _dot(ds_prev, _blocks(kd, u0), B_PV)
                        dkd[prev, :] += flat(_dot(ds_prev, q, B_TN))
                        dvd[prev, :] += flat(_dot(p_prev.astype(BF16), do, B_TN))
                    dqd[u0 * BLK:(u0 + CU) * BLK, :] = flat(dq)
                _interleave(tmp, dqd, dil)
                dq_ref[...] = _rope_t(tmp[...], c, a, b).astype(BF16)
                _interleave(tmp, dkd, dil, BLK)
                dk_ref[...] = _rope_t(tmp[...], c, a, b).astype(BF16)
                _interleave(tmp, dvd, dil, BLK)
                dv_ref[...] = tmp[...].astype(BF16)

    tab = pl.BlockSpec((S, HEAD_DIM), lambda h, p: (0, 0))
    hspec = lambda base: pl.BlockSpec((S, HEAD_DIM), lambda h, p: (0, base + h))
    gspec = pl.BlockSpec((S, HEAD_DIM), lambda h, p: (0, jnp.maximum(p - 1, 0) * 8 + h))
    return pl.pallas_call(
        body, grid=(8, 4),
        in_specs=[gspec, gspec, gspec, hspec(BG_COL), hspec(8), tab, tab, tab,
                  hspec(0), hspec(8), hspec(16), hspec(0), hspec(8), hspec(16), ANY_SPEC],
        out_specs=[gspec, gspec, gspec, hspec(0)],
        out_shape=[jax.ShapeDtypeStruct((S, 3072), BF16)] * 3 + [jax.ShapeDtypeStruct((S, HALF), BF16)],
        scratch_shapes=[_slab(), _slab(BF16, S + BLK), _slab(BF16, S + BLK), _slab()] + [_slab() for _ in range(6)]
                       + [_slab(BF16), _slab(), _slab(), _slab(F32, S + BLK), _slab(F32, S + BLK)],
        compiler_params=_params(("parallel", "arbitrary"), VMEM_BIG), name="attn_bwd",
    )(*qkv, z0, dycat, *tabs, og, og, og, lg, lg, lg, dep)


SGU_CH = 256
NCHUNK = TR // 128


def _ln_stats(x):
    mu = jnp.mean(x, axis=-1, keepdims=True)
    xc = x - mu
    r = lax.rsqrt(jnp.mean(xc * xc, axis=-1, keepdims=True) + EPS)
    return xc * r, r


def _ln_bwd(dy, xhat, r, g):
    dxh = dy * g
    return r * (dxh - jnp.mean(dxh, axis=-1, keepdims=True) - xhat * jnp.mean(dxh * xhat, axis=-1, keepdims=True))


def _tril_bf16(w):
    row = lax.broadcasted_iota(jnp.int32, w.shape, 0)
    col = lax.broadcasted_iota(jnp.int32, w.shape, 1)
    return jnp.where(row >= col, w, 0.0).astype(BF16)


def _sgu_gate(vn_s, s_s, w_ref, bb_ref):
    for h in range(4):
        wm = _tril_bf16(w_ref[h])
        bias = bb_ref[h]
        for ch in range(NCHUNK):
            rows, cols = slice(ch * 128, (ch + 1) * 128), slice(h * SGU_CH, (h + 1) * SGU_CH)
            s_s[rows, cols] = _dot(wm, vn_s[rows, cols]) + jnp.concatenate([bias, bias], axis=1)


WIN = HALO + TR
SUBL = 8


def _shifted_copies(dst, src):
    dst[0] = src[...]
    for b in range(1, SUBL):
        dst[b, 0:WIN - SUBL, :] = src[pl.ds(b, WIN - SUBL), :]


def _rows_at(copies, off, n):
    return copies[off % SUBL, pl.ds(off - off % SUBL, n), :]


def _conv_fwd(i, dval_ref, dglu_ref, hval_ref, hglu_ref, cw_ref, cb_ref, xw, xr, dcs):
    halo = hval_ref[...] * _sigmoid(hglu_ref[...])
    xw[0:HALO, :] = jnp.where(i > 0, halo, 0.0)
    xw[HALO:HALO + TR, :] = dval_ref[...] * _sigmoid(dglu_ref[...])
    _shifted_copies(xr, xw)
    for rb in range(TR // SUB):
        acc = jnp.broadcast_to(cb_ref[...], (SUB, HALF))
        for k in range(CONV_K):
            acc = acc + cw_ref[k:k + 1, :] * _rows_at(xr, rb * SUB + HALO - (CONV_K - 1) + k, SUB)
        dcs[rb * SUB:(rb + 1) * SUB, :] = acc


def _odd_in_specs():
    col = lambda j: pl.BlockSpec((TR, HALF), lambda i, *_: (i, j))
    prev = lambda j: pl.BlockSpec((HALO, HALF), lambda i, *_: (jnp.maximum(i * (TR // HALO) - 1, 0), j))
    return [col(0), col(1), col(2), col(3), col(4), col(5), prev(3), prev(4)]


def _full_spec(shape):
    return pl.BlockSpec(shape, lambda i, *_: (0,) * len(shape))


def _odd_fwd(z1, sgu_g, sgu_b, sgu_w, sgu_bb, conv_w, conv_b, cn_g, cn_b):
    def body(u_ref, v_ref, cg_ref, dval_ref, dglu_ref, dgate_ref, hval_ref, hglu_ref,
             g_ref, b_ref, w_ref, bb_ref, cw_ref, cb_ref, cng_ref, cnb_ref, out_ref, dcs, vn_s, s_s, xw, xr):
        i = pl.program_id(0)
        vhat, _ = _ln_stats(v_ref[...])
        vn_s[...] = (vhat * g_ref[...] + b_ref[...]).astype(BF16)
        _sgu_gate(vn_s, s_s, w_ref, bb_ref)
        cg = cg_ref[...]
        out_ref[:, 0:HALF] = (u_ref[...] * s_s[...] * (cg * _sigmoid(cg))).astype(BF16)
        _conv_fwd(i, dval_ref, dglu_ref, hval_ref, hglu_ref, cw_ref, cb_ref, xw, xr, dcs)
        dhat, _ = _ln_stats(dcs[...])
        dn = dhat * cng_ref[...] + cnb_ref[...]
        dgate = dgate_ref[...]
        out_ref[:, HALF:2 * HALF] = ((dn * _sigmoid(dn)) * (dgate * _sigmoid(dgate))).astype(BF16)

    vec = _full_spec((1, HALF))
    return pl.pallas_call(
        body, grid=(S // TR,),
        in_specs=_odd_in_specs() + [vec, vec, _full_spec((4, 128, 128)), _full_spec((4, 128, 128)),
                                    _full_spec((HALO, HALF)), vec, vec, vec],
        out_specs=[pl.BlockSpec((TR, 2048), lambda i: (i, 0)), pl.BlockSpec((TR, HALF), lambda i: (i, 0))],
        out_shape=[jax.ShapeDtypeStruct((S, 2048), BF16), jax.ShapeDtypeStruct((S, HALF), F32)],
        scratch_shapes=[pltpu.VMEM((TR, HALF), BF16), pltpu.VMEM((TR, HALF), F32),
                        pltpu.VMEM((WIN, HALF), F32), pltpu.VMEM((SUBL, WIN, HALF), F32)],
        compiler_params=_params(("parallel",), VMEM_BIG), name="odd_fwd",
    )(z1, z1, z1, z1, z1, z1, z1, z1, sgu_g, sgu_b, sgu_w, sgu_bb, conv_w, conv_b, cn_g, cn_b)


def _odd_bwd_a(z1, dc, dycat, sgu_g, sgu_b, sgu_w, sgu_bb, cn_g, cn_b):
    def body(u_ref, v_ref, cg_ref, dgate_ref, dcs, dy_ref, g_ref, b_ref, w_ref, bb_ref, cng_ref, cnb_ref,
             dz_ref, ddc_ref, dw_ref, dbb_ref, dg_ref, db_ref, dcng_ref, dcnb_ref, dcb_ref,
             vn_s, s_s, ds_s, dvn_s):
        i = pl.program_id(0)
        vhat, rv = _ln_stats(v_ref[...])
        g = g_ref[...]
        vn_s[...] = (vhat * g + b_ref[...]).astype(BF16)
        _sgu_gate(vn_s, s_s, w_ref, bb_ref)
        silu_c, dsilu_c = _silu_and_grad(cg_ref[...])
        dyc = dy_ref[:, 0:HALF]
        u = u_ref[...]
        s = s_s[...]
        dz_ref[:, 0:HALF] = (dyc * s * silu_c).astype(BF16)
        dz_ref[:, 2 * HALF:3 * HALF] = (dyc * u * s * dsilu_c).astype(BF16)
        ds_s[...] = dyc * u * silu_c

        @pl.when(i == 0)
        def _():
            dw_ref[...] = jnp.zeros_like(dw_ref)
            dbb_ref[...] = jnp.zeros_like(dbb_ref)

        tril = lax.broadcasted_iota(jnp.int32, (128, 128), 0) >= lax.broadcasted_iota(jnp.int32, (128, 128), 1)
        for h in range(4):
            wm = _tril_bf16(w_ref[h])
            for ch in range(NCHUNK):
                rows, cols = slice(ch * 128, (ch + 1) * 128), slice(h * SGU_CH, (h + 1) * SGU_CH)
                ds = ds_s[rows, cols]
                dsb = ds.astype(BF16)
                dw_ref[h] += jnp.where(tril, _dot(dsb, vn_s[rows, cols], NT), 0.0)
                dbb_ref[h] += jnp.broadcast_to(jnp.sum(ds, axis=1, keepdims=True), (128, 128))
                dvn_s[rows, cols] = _dot(wm, dsb, TN)
        dvn = dvn_s[...]
        _acc_rows(dg_ref, dvn * vhat, i)
        _acc_rows(db_ref, dvn, i)
        dz_ref[:, HALF:2 * HALF] = _ln_bwd(dvn, vhat, rv, g).astype(BF16)

        dhat, rd = _ln_stats(dcs[...])
        cng = cng_ref[...]
        silu_n, dsilu_n = _silu_and_grad(dhat * cng + cnb_ref[...])
        silu_g, dsilu_g = _silu_and_grad(dgate_ref[...])
        dyd = dy_ref[:, HALF:2 * HALF]
        dz_ref[:, 5 * HALF:6 * HALF] = (dyd * silu_n * dsilu_g).astype(BF16)
        ddn = dyd * silu_g * dsilu_n
        _acc_rows(dcng_ref, ddn * dhat, i)
        _acc_rows(dcnb_ref, ddn, i)
        ddc = _ln_bwd(ddn, dhat, rd, cng)
        ddc_ref[...] = ddc
        _acc_rows(dcb_ref, ddc, i)

    vec = _full_spec((1, HALF))
    sq = _full_spec((4, 128, 128))
    col = lambda j: pl.BlockSpec((TR, HALF), lambda i: (i, j))
    return pl.pallas_call(
        body, grid=(S // TR,),
        in_specs=[col(0), col(1), col(2), col(5), col(0), pl.BlockSpec((TR, 2048), lambda i: (i, 0)),
                  vec, vec, sq, sq, vec, vec],
        out_specs=[pl.BlockSpec((TR, ODD_IN), lambda i: (i, 0)), pl.BlockSpec((TR, HALF), lambda i: (i, 0)),
                   sq, sq, vec, vec, vec, vec, vec],
        out_shape=[jax.ShapeDtypeStruct((S, ODD_IN), BF16), jax.ShapeDtypeStruct((S, HALF), F32),
                   jax.ShapeDtypeStruct((4, 128, 128), F32), jax.ShapeDtypeStruct((4, 128, 128), F32)]
                  + [jax.ShapeDtypeStruct((1, HALF), F32)] * 5,
        scratch_shapes=[pltpu.VMEM((TR, HALF), BF16), pltpu.VMEM((TR, HALF), F32),
                        pltpu.VMEM((TR, HALF), F32), pltpu.VMEM((TR, HALF), F32)],
        compiler_params=_params(("arbitrary",), VMEM_BIG), name="odd_bwd_a",
    )(z1, z1, z1, z1, dc, dycat, sgu_g, sgu_b, sgu_w, sgu_bb, cn_g, cn_b)


def _odd_bwd_b(z1, ddc, dz1, conv_w):
    nt = S // TR

    def body(dval_ref, dglu_ref, hval_ref, hglu_ref, ddc_ref, hddc_ref, cw_ref, dz_in_ref,
             dz_ref, dcw_ref, xw, dwin, dxs, xr, dr):
        del dz_in_ref
        i, j = pl.program_id(0), pl.program_id(1)
        sg = _sigmoid(dglu_ref[...])
        dval = dval_ref[...]

        @pl.when(j == 0)
        def _():
            halo = hval_ref[...] * _sigmoid(hglu_ref[...])
            xw[0:HALO, :] = jnp.where(i > 0, halo, 0.0)
            xw[HALO:HALO + TR, :] = dval * sg
            dwin[0:TR, :] = ddc_ref[...]
            dwin[TR:TR + HALO, :] = jnp.where(i < nt - 1, hddc_ref[...], 0.0)
            _shifted_copies(xr, xw)
            _shifted_copies(dr, dwin)

            @pl.when(i == 0)
            def _():
                dcw_ref[...] = jnp.zeros_like(dcw_ref)

            for rb in range(TR // SUB):
                acc = jnp.zeros((SUB, HALF), F32)
                for k in range(CONV_K):
                    acc = acc + cw_ref[k:k + 1, :] * _rows_at(dr, rb * SUB + (CONV_K - 1) - k, SUB)
                dxs[rb * SUB:(rb + 1) * SUB, :] = acc
            for k in range(CONV_K):
                acc = jnp.zeros((SUB, HALF), F32)
                for rb in range(TR // SUB):
                    acc = acc + dwin[rb * SUB:(rb + 1) * SUB, :] * _rows_at(xr, rb * SUB + HALO - (CONV_K - 1) + k, SUB)
                dcw_ref[k:k + 1, :] += jnp.sum(acc, axis=0, keepdims=True)
            dz_ref[...] = (dxs[...] * sg).astype(BF16)

        @pl.when(j == 1)
        def _():
            dz_ref[...] = (dxs[...] * dval * sg * (1.0 - sg)).astype(BF16)

    col = lambda c: pl.BlockSpec((TR, HALF), lambda i, j: (i, c))
    prev = lambda c: pl.BlockSpec((HALO, HALF), lambda i, j: (jnp.maximum(i * (TR // HALO) - 1, 0), c))
    nxt = pl.BlockSpec((HALO, HALF), lambda i, j: (jnp.minimum((i + 1) * (TR // HALO), S // HALO - 1), 0))
    return pl.pallas_call(
        body, grid=(nt, 2),
        in_specs=[col(3), col(4), prev(3), prev(4), pl.BlockSpec((TR, HALF), lambda i, j: (i, 0)), nxt,
                  _full_spec((HALO, HALF)), pl.BlockSpec(memory_space=pl.ANY)],
        out_specs=[pl.BlockSpec((TR, HALF), lambda i, j: (i, 3 + j)), _full_spec((HALO, HALF))],
        out_shape=[jax.ShapeDtypeStruct((S, ODD_IN), BF16), jax.ShapeDtypeStruct((HALO, HALF), F32)],
        scratch_shapes=[pltpu.VMEM((WIN, HALF), F32), pltpu.VMEM((WIN, HALF), F32), pltpu.VMEM((TR, HALF), F32),
                        pltpu.VMEM((SUBL, WIN, HALF), F32), pltpu.VMEM((SUBL, WIN, HALF), F32)],
        input_output_aliases={7: 0},
        compiler_params=_params(("arbitrary", "arbitrary"), VMEM_BIG), name="odd_bwd_b",
    )(z1, z1, z1, z1, ddc, ddc, conv_w, dz1)


def _cast_bf16(w, name, piece=0, npieces=1):
    r, c = w.shape[0], w.shape[1] // npieces
    tr = min(r, 256)

    def body(i_ref, o_ref):
        o_ref[...] = i_ref[...].astype(BF16)

    return pl.pallas_call(
        body, grid=(r // tr,), in_specs=[pl.BlockSpec((tr, c), lambda i: (i, piece))],
        out_specs=pl.BlockSpec((tr, c), lambda i: (i, 0)), out_shape=jax.ShapeDtypeStruct((r, c), BF16),
        compiler_params=_params(("parallel",)), name=name,
    )(w)


def _adamw(w, g, m, v):
    m = ADAM_B1 * m + (1.0 - ADAM_B1) * g
    v = ADAM_B2 * v + (1.0 - ADAM_B2) * (g * g)
    m_hat = m / (1.0 - ADAM_B1 ** ADAM_STEP)
    v_hat = v / (1.0 - ADAM_B2 ** ADAM_STEP)
    delta = -ADAM_LR * (m_hat / (jnp.sqrt(v_hat) + ADAM_EPS) + ADAM_WD * w)
    return delta, m, v


def _adam_reduce(parts, w, m, v, name, dep=None, piece=0, npieces=1, prev=None):
    r, c = w.shape
    cp = c // npieces
    tr = min(r, 128)
    extra = ([] if dep is None else [dep]) + ([] if prev is None else list(prev))
    nparts = parts.shape[0]

    def body(p_ref, w_ref, m_ref, v_ref, *rest):
        g_ref, d_ref, nm_ref, nv_ref = rest[len(extra):]
        g = p_ref[0].astype(F32)
        for d in range(1, nparts):
            g = g + p_ref[d].astype(F32)
        g_ref[...] = g
        d_ref[...], nm_ref[...], nv_ref[...] = _adamw(w_ref[...], g, m_ref[...], v_ref[...])

    spec = pl.BlockSpec((tr, cp), lambda i: (i, piece))
    first = 4 + (0 if dep is None else 1)
    return pl.pallas_call(
        body, grid=(r // tr,),
        in_specs=[pl.BlockSpec((nparts, tr, cp), lambda i: (0, i, 0)), spec, spec, spec] + [ANY_SPEC] * len(extra),
        out_specs=[spec] * 4, out_shape=[jax.ShapeDtypeStruct((r, c), F32)] * 4,
        input_output_aliases={} if prev is None else {first + k: k for k in range(4)},
        compiler_params=_params(("parallel",), VMEM_BIG), name=name,
    )(parts, w, m, v, *extra)


def _arrived(x, name, dep=None):
    deps = [] if dep is None else [dep]

    def body(*refs):
        refs[-1][...] = jnp.zeros_like(refs[-1])

    return pl.pallas_call(
        body, in_specs=[ANY_SPEC] * (1 + len(deps)), out_specs=pl.BlockSpec(memory_space=pltpu.VMEM),
        out_shape=jax.ShapeDtypeStruct((8, 128), F32), name=name,
    )(x, *deps)


def _sum_parts(parts, name, dep=None):
    r = parts.shape[1]
    tr = 8
    for cand in (512, 256, 128, 64, 32, 16, 8):
        if r % cand == 0:
            tr = cand
            break
    deps = [] if dep is None else [dep]

    def body(p_ref, *rest):
        g = p_ref[0]
        for d in range(1, NDEV):
            g = g + p_ref[d]
        rest[-1][...] = g

    return pl.pallas_call(
        body, grid=(r // tr,), in_specs=[pl.BlockSpec((NDEV, tr, 128), lambda i: (0, i, 0))] + [ANY_SPEC] * len(deps),
        out_specs=pl.BlockSpec((tr, 128), lambda i: (i, 0)), out_shape=jax.ShapeDtypeStruct((r, 128), F32),
        compiler_params=_params(("parallel",)), name=name,
    )(parts, *deps)


def _sum_unpack(parts, rows, name, dep=None):
    deps = [] if dep is None else [dep]

    def body(p_ref, *outs):
        outs = outs[len(deps):]
        off = 0
        for o_ref, n in zip(outs, rows):
            acc = p_ref[0, off:off + n, :]
            for d in range(1, NDEV):
                acc = acc + p_ref[d, off:off + n, :]
            o_ref[...] = acc
            off += n

    return pl.pallas_call(
        body, grid=(1,), in_specs=[pl.BlockSpec(parts.shape, lambda i: (0, 0, 0))] + [ANY_SPEC] * len(deps),
        out_specs=[pl.BlockSpec((n, 128), lambda i: (0, 0)) for n in rows],
        out_shape=[jax.ShapeDtypeStruct((n, 128), F32) for n in rows],
        compiler_params=_params(("arbitrary",), VMEM_BIG), name=name,
    )(parts, *deps)


def _adam_small(ws, gs, g_specs, ms, vs, name):
    n = len(ws)

    def body(*refs):
        w_r, g_r, m_r, v_r = refs[:n], refs[n:2 * n], refs[2 * n:3 * n], refs[3 * n:4 * n]
        outs = refs[4 * n:]
        for i in range(n):
            g = g_r[i][...]
            outs[4 * i][...] = g
            outs[4 * i + 1][...], outs[4 * i + 2][...], outs[4 * i + 3][...] = _adamw(
                w_r[i][...], g, m_r[i][...], v_r[i][...])

    whole = lambda a: pl.BlockSpec(a.shape, lambda i, nd=a.ndim: (0,) * nd)
    outs = pl.pallas_call(
        body, grid=(1,),
        in_specs=[whole(a) for a in ws] + list(g_specs) + [whole(a) for a in ms] + [whole(a) for a in vs],
        out_specs=[whole(a) for a in ws for _ in range(4)],
        out_shape=[jax.ShapeDtypeStruct(a.shape, F32) for a in ws for _ in range(4)],
        compiler_params=_params(("arbitrary",), VMEM_BIG), name=name,
    )(*ws, *gs, *ms, *vs)
    return [outs[4 * i:4 * i + 4] for i in range(n)]


MASKS = [(mx, my, mc) for mx in (0, 1) for my in (0, 1) for mc in (0, 1)][1:]


def _sc_exchange(name, collective_id, arrays, scatter):
    nt = len(arrays)
    out_type = [jax.ShapeDtypeStruct(a.shape if scatter else (NDEV,) + a.shape, a.dtype) for a in arrays]

    def body(*refs):
        ins, outs = refs[:nt], refs[nt:2 * nt]
        send_sems, recv_sems, local_sems = refs[2 * nt:3 * nt], refs[3 * nt:4 * nt], refs[4 * nt:5 * nt]
        x, y, c = lax.axis_index("x"), lax.axis_index("y"), lax.axis_index("c")
        peers = [(mx + x - 2 * mx * x, my + y - 2 * my * y, mc + c - 2 * mc * c) for mx, my, mc in MASKS]
        barrier = pltpu.get_barrier_semaphore()
        for peer in peers:
            pl.semaphore_signal(barrier, inc=1, device_id=peer, device_id_type=MESH)
        pl.semaphore_wait(barrier, len(peers))
        me = 4 * x + 2 * y + c
        own = []
        for t in range(nt):
            cp = pltpu.make_async_copy(ins[t].at[me] if scatter else ins[t], outs[t].at[me], local_sems[t])
            cp.start()
            own.append(cp)
            for px, py, pc in peers:
                src = ins[t].at[4 * px + 2 * py + pc] if scatter else ins[t]
                pltpu.make_async_remote_copy(src_ref=src, dst_ref=outs[t].at[me], send_sem=send_sems[t],
                                             recv_sem=recv_sems[t], device_id=(px, py, pc), device_id_type=MESH).start()
        for t in range(nt):
            own[t].wait()
            seven = outs[t].at[pl.ds(0, NDEV - 1)]
            drain = pltpu.make_async_remote_copy(src_ref=seven, dst_ref=seven, send_sem=send_sems[t],
                                                 recv_sem=recv_sems[t], device_id=(x, y, c), device_id_type=MESH)
            drain.wait_send()
            drain.wait_recv()

    return pl.kernel(
        body, out_type=out_type, mesh=plsc.ScalarSubcoreMesh(axis_name="sequencer", num_cores=1),
        scratch_types=[pltpu.SemaphoreType.DMA] * (3 * nt),
        compiler_params=pltpu.CompilerParams(collective_id=collective_id), name=name,
    )(*arrays)


def _sc_gather_two_level(name, collective_id, arrays):
    nt = len(arrays)
    out_type = [jax.ShapeDtypeStruct((NDEV,) + a.shape, a.dtype) for a in arrays]

    def body(*refs):
        ins, outs = refs[:nt], refs[nt:2 * nt]
        sems = refs[2 * nt:]
        send_sems, sib_sems, local_sems = sems[:nt], sems[nt:2 * nt], sems[2 * nt:3 * nt]
        ici_sems = [sems[3 * nt + 3 * t:3 * nt + 3 * t + 3] for t in range(nt)]
        x, y, c = lax.axis_index("x"), lax.axis_index("y"), lax.axis_index("c")
        sibling = (x, y, 1 - c)
        chips = [(1 - x, y), (x, 1 - y), (1 - x, 1 - y)]
        barrier = pltpu.get_barrier_semaphore()
        for peer in [sibling] + [(cx, cy, c) for cx, cy in chips]:
            pl.semaphore_signal(barrier, inc=1, device_id=peer, device_id_type=MESH)
        pl.semaphore_wait(barrier, 4)
        me = 4 * x + 2 * y + c

        def push(t, src, slot, recv_sem, to):
            pltpu.make_async_remote_copy(src_ref=src, dst_ref=outs[t].at[slot], send_sem=send_sems[t],
                                         recv_sem=recv_sem, device_id=to, device_id_type=MESH).start()

        own = []
        for t in range(nt):
            cp = pltpu.make_async_copy(ins[t], outs[t].at[me], local_sems[t])
            cp.start()
            own.append(cp)
            for j, (cx, cy) in enumerate(chips):
                push(t, ins[t], me, ici_sems[t][j], (cx, cy, c))
            push(t, ins[t], me, sib_sems[t], sibling)
        for t in range(nt):
            for j, (cx, cy) in enumerate(chips):
                slot = 4 * cx + 2 * cy + c
                landed = outs[t].at[slot]
                pltpu.make_async_remote_copy(src_ref=landed, dst_ref=landed, send_sem=send_sems[t],
                                             recv_sem=ici_sems[t][j], device_id=(cx, cy, c),
                                             device_id_type=MESH).wait_recv()
                push(t, landed, slot, sib_sems[t], sibling)
        for t in range(nt):
            own[t].wait()
            four, seven = outs[t].at[pl.ds(0, 4)], outs[t].at[pl.ds(0, 7)]
            pltpu.make_async_remote_copy(src_ref=four, dst_ref=four, send_sem=send_sems[t], recv_sem=sib_sems[t],
                                         device_id=sibling, device_id_type=MESH).wait_recv()
            pltpu.make_async_remote_copy(src_ref=seven, dst_ref=seven, send_sem=send_sems[t], recv_sem=sib_sems[t],
                                         device_id=sibling, device_id_type=MESH).wait_send()

    return pl.kernel(
        body, out_type=out_type, mesh=plsc.ScalarSubcoreMesh(axis_name="sequencer", num_cores=1),
        scratch_types=[pltpu.SemaphoreType.DMA] * (6 * nt),
        compiler_params=pltpu.CompilerParams(collective_id=collective_id), name=name,
    )(*arrays)


def _sc_sibling_exchange(name, collective_id, src, out_shape, pieces):
    def body(src_ref, out_ref, send_sem, recv_sem):
        x, y, c = lax.axis_index("x"), lax.axis_index("y"), lax.axis_index("c")
        sibling = (x, y, 1 - c)
        barrier = pltpu.get_barrier_semaphore()
        pl.semaphore_signal(barrier, inc=1, device_id=sibling, device_id_type=MESH)
        pl.semaphore_wait(barrier, 1)
        for piece, lands in pieces(c, src_ref, out_ref):
            pltpu.make_async_remote_copy(src_ref=piece, dst_ref=lands, send_sem=send_sem, recv_sem=recv_sem,
                                         device_id=sibling, device_id_type=MESH).start()
        drain = pltpu.make_async_remote_copy(src_ref=out_ref, dst_ref=out_ref, send_sem=send_sem, recv_sem=recv_sem,
                                             device_id=sibling, device_id_type=MESH)
        drain.wait_send()
        drain.wait_recv()

    return pl.kernel(
        body, out_type=jax.ShapeDtypeStruct(out_shape, src.dtype),
        mesh=plsc.ScalarSubcoreMesh(axis_name="sequencer", num_cores=1), scratch_types=[pltpu.SemaphoreType.DMA] * 2,
        compiler_params=pltpu.CompilerParams(collective_id=collective_id), name=name,
    )(src)


def _swap_class_columns(name, collective_id, dz, nb, piece=0, npieces=1):
    w = nb // npieces
    return _sc_sibling_exchange(
        name, collective_id, dz, (S, 4 * w),
        lambda c, src, out: [(src.at[:, pl.ds((2 * j + 1 - c) * nb + piece * w, w)], out.at[:, pl.ds(j * w, w)])
                             for j in range(4)])


def _sc_chip_scatter(name, collective_id, q):
    def body(q_ref, out_ref, send_sem, recv_sem, local_sem):
        x, y, c = lax.axis_index("x"), lax.axis_index("y"), lax.axis_index("c")
        chips = [(1 - x, y), (x, 1 - y), (1 - x, 1 - y)]
        barrier = pltpu.get_barrier_semaphore()
        for cx, cy in chips:
            pl.semaphore_signal(barrier, inc=1, device_id=(cx, cy, c), device_id_type=MESH)
        pl.semaphore_wait(barrier, 3)
        mine = 2 * x + y
        own = pltpu.make_async_copy(q_ref.at[mine], out_ref.at[mine], local_sem)
        own.start()
        for cx, cy in chips:
            pltpu.make_async_remote_copy(src_ref=q_ref.at[2 * cx + cy], dst_ref=out_ref.at[mine], send_sem=send_sem,
                                         recv_sem=recv_sem, device_id=(cx, cy, c), device_id_type=MESH).start()
        own.wait()
        three = out_ref.at[pl.ds(0, 3)]
        drain = pltpu.make_async_remote_copy(src_ref=three, dst_ref=three, send_sem=send_sem, recv_sem=recv_sem,
                                             device_id=(x, y, c), device_id_type=MESH)
        drain.wait_send()
        drain.wait_recv()

    return pl.kernel(
        body, out_type=jax.ShapeDtypeStruct(q.shape, q.dtype),
        mesh=plsc.ScalarSubcoreMesh(axis_name="sequencer", num_cores=1), scratch_types=[pltpu.SemaphoreType.DMA] * 3,
        compiler_params=pltpu.CompilerParams(collective_id=collective_id), name=name,
    )(q)


def _mm_pair_dw(h_own, dz, h_sib, dz_sib, nb, name, dep=None, piece=0, npieces=1, h_transposed=False,
                one_call=False):
    nb = nb // npieces
    tn = 512 if nb % 512 == 0 else nb
    per = nb // tn
    dn = NN if h_transposed else TN
    o_spec = pl.BlockSpec((None, D, tn), lambda i, j, k: (j // per, 0, j % per))
    own_col = lambda i, j, k: (0, ((2 * (j // per) + lax.axis_index("c")) * npieces + piece) * per + j % per)
    if one_call:
        def fused(a0_ref, b0_ref, a1_ref, b1_ref, dep_ref, o_ref):
            acc = _dot(a0_ref[...], b0_ref[...], dn) + _dot(a1_ref[...], b1_ref[...], dn)
            o_ref[...] = acc.astype(BF16)

        whole = pl.BlockSpec((S, D), lambda i, j, k: (0, 0), pipeline_mode=pl.Buffered(1))
        return pl.pallas_call(
            fused, grid=(1, 4 * per, 1),
            in_specs=[whole, pl.BlockSpec((S, tn), own_col), whole, pl.BlockSpec((S, tn), lambda i, j, k: (0, j)),
                      ANY_SPEC],
            out_specs=o_spec, out_shape=jax.ShapeDtypeStruct((4, D, nb), BF16),
            compiler_params=_params(("parallel", "parallel", "arbitrary"), VMEM_BIG), name=name,
        )(h_own, dz, h_sib, dz_sib, dep)
    part = _matmul(
        h_own, dz, dn=dn, grid=(1, 4 * per, 1),
        a_spec=pl.BlockSpec((S, D), lambda i, j, k: (0, 0)), b_spec=pl.BlockSpec((S, tn), own_col),
        o_spec=o_spec, out_shape=(4, D, nb), out_dtype=F32, acc_shape=(D, tn), name=name + "_own", dep=dep)

    def body(a_ref, b_ref, p_ref, o_ref):
        o_ref[...] = (p_ref[...] + _dot(a_ref[...], b_ref[...], dn)).astype(BF16)

    return pl.pallas_call(
        body, grid=(1, 4 * per, 1),
        in_specs=[pl.BlockSpec((S, D), lambda i, j, k: (0, 0)), pl.BlockSpec((S, tn), lambda i, j, k: (0, j)), o_spec],
        out_specs=o_spec, out_shape=jax.ShapeDtypeStruct((4, D, nb), BF16),
        compiler_params=_params(("parallel", "parallel", "arbitrary"), VMEM_BIG), name=name + "_sibling",
    )(h_sib, dz_sib, part)


SMALL = {
    "e_pre_norm": ((2048,), None), "e_pool_w": ((4, 256, 256), 1), "e_pool_scale": ((1024,), None),
    "e_post_norm": ((2048,), None), "o_pre_norm": ((2048,), 0), "o_sgu_norm_g": ((1024,), 0),
    "o_sgu_norm_b": ((1024,), 0), "o_sgu_w": ((4, 128, 128), None), "o_sgu_b": ((4, 128), None),
    "o_conv_w": ((31, 1024), 1), "o_conv_b": ((1024,), 0), "o_conv_norm_g": ((1024,), 0),
    "o_conv_norm_b": ((1024,), 0), "o_post_norm": ((2048,), 0),
}
SMALL_SHARDED = [n for n, (_, ax) in SMALL.items() if ax is not None]


def _shard_shape(name):
    shape, ax = SMALL[name]
    if ax is None:
        return shape
    return tuple(s // NDEV if i == ax else s for i, s in enumerate(shape))


def _pack(arrs, row_multiple=1):
    flat = jnp.concatenate([a.reshape(-1) for a in arrs])
    pad = -flat.shape[0] % (128 * row_multiple)
    return jnp.concatenate([flat, jnp.zeros((pad,), F32)]).reshape(-1, 128)


def _small_views(name):
    shape, ax = SMALL[name]
    me = lambda: 4 * lax.axis_index("x") + 2 * lax.axis_index("y") + lax.axis_index("c")
    if ax is None:
        view = (int(np.prod(shape)) // 128, 128)
        return view, view, pl.BlockSpec(view, lambda i: (0, 0))
    if len(shape) == 1:
        n = shape[0] // NDEV
        return (1, n), (NDEV, 1, n), pl.BlockSpec((None, 1, n), lambda i: (me(), 0, 0))
    part = _shard_shape(name)
    return part, shape, pl.BlockSpec(part, lambda i: tuple(me() if d == ax else 0 for d in range(len(shape))))


WEIGHTS = ["e_pre_norm", "e_w_in", "e_pool_w", "e_pool_scale", "e_w_out", "e_post_norm", "o_pre_norm", "o_w_in",
           "o_sgu_norm_g", "o_sgu_norm_b", "o_sgu_w", "o_sgu_b", "o_conv_w", "o_conv_b", "o_conv_norm_g",
           "o_conv_norm_b", "o_w_out", "o_post_norm"]


def kernel(x, e_pre_norm, e_w_in, e_pool_w, e_pool_scale, e_w_out, e_post_norm, o_pre_norm, o_w_in, o_sgu_norm_g, o_sgu_norm_b, o_sgu_w, o_sgu_b, o_conv_w, o_conv_b, o_conv_norm_g, o_conv_norm_b, o_w_out, o_post_norm, loss_target, m_e_pre_norm, m_e_w_in, m_e_pool_w, m_e_pool_scale, m_e_w_out, m_e_post_norm, m_o_pre_norm, m_o_w_in, m_o_sgu_norm_g, m_o_sgu_norm_b, m_o_sgu_w, m_o_sgu_b, m_o_conv_w, m_o_conv_b, m_o_conv_norm_g, m_o_conv_norm_b, m_o_w_out, m_o_post_norm, v_e_pre_norm, v_e_w_in, v_e_pool_w, v_e_pool_scale, v_e_w_out, v_e_post_norm, v_o_pre_norm, v_o_w_in, v_o_sgu_norm_g, v_o_sgu_norm_b, v_o_sgu_w, v_o_sgu_b, v_o_conv_w, v_o_conv_b, v_o_conv_norm_g, v_o_conv_norm_b, v_o_w_out, v_o_post_norm):
    given = dict(locals())
    w = {n: given[n][0] for n in WEIGHTS}
    m = {n: given["m_" + n][0] for n in WEIGHTS}
    v = {n: given["v_" + n][0] for n in WEIGHTS}
    me = 4 * lax.axis_index("x") + 2 * lax.axis_index("y") + lax.axis_index("c")
    x, target = x[0], loss_target[0]
    row = lambda a: a.reshape(1, -1)

    lo, small_rows = _sc_gather_two_level(
        "gather_a0", 0, [_cast_bf16(w["e_w_in"], "cast_e_w_in_0", 0, 2), _pack([w[n] for n in SMALL_SHARDED])])
    hi, = _sc_gather_two_level("gather_a1", 12, [_cast_bf16(w["e_w_in"], "cast_e_w_in_1", 1, 2)])
    wg_e_in = (lo, hi)
    h0, h0t = _pre0_fwd(x, row(w["e_pre_norm"]))
    wg_e_out, = _sc_gather_two_level("gather_b", 1, [_cast_bf16(w["e_w_out"], "cast_e_w_out")])
    wg_o_in, wg_o_out = _sc_gather_two_level(
        "gather_c", 13, [_cast_bf16(w[n], "cast_" + n) for n in ("o_w_in", "o_w_out")])
    h0t_sib = _sc_sibling_exchange("swap_h0", 8, h0t, h0t.shape, lambda c, src, out: [(src, out)])
    p = {n: w[n] for n in SMALL if SMALL[n][1] is None}
    small_rows = small_rows.reshape(NDEV, -1)
    off = 0
    for n in SMALL_SHARDED:
        shp, ax = _shard_shape(n), SMALL[n][1]
        cnt = int(np.prod(shp))
        blk = small_rows[:, off:off + cnt].reshape((NDEV,) + shp)
        p[n] = jnp.moveaxis(blk, 0, ax).reshape(SMALL[n][0])
        off += cnt
    tabs = _rope_tables()
    pool_w_bf = p["e_pool_w"].astype(BF16)
    sgu_bb = jnp.broadcast_to(p["o_sgu_b"][:, :, None], (4, 128, 128))
    conv_w = jnp.concatenate([p["o_conv_w"], jnp.zeros((HALO - CONV_K, HALF), F32)], axis=0)
    odd_p = (row(p["o_sgu_norm_g"]), row(p["o_sgu_norm_b"]), p["o_sgu_w"], sgu_bb, conv_w,
             row(p["o_conv_b"]), row(p["o_conv_norm_g"]), row(p["o_conv_norm_b"]))

    z0 = _mm_in_halves(h0, wg_e_in, "mm_z0")
    ycat0 = _pool_fwd(z0, pool_w_bf, row(p["e_pool_scale"]))
    qkv = _qkv_prep(z0, tabs)
    ycat0, og, lg = _attn_fwd(z0, qkv, ycat0)
    w_out_e, w_out_o = wg_e_out.reshape(2048, D), wg_o_out.reshape(2048, D)
    y0, x1, h1 = _post0_fwd(ycat0, w_out_e, x, row(p["e_post_norm"]), row(p["o_pre_norm"]), h0t_sib)
    h1_sib = _sc_sibling_exchange("swap_h1", 11, h1, h1.shape, lambda c, src, out: [(src, out)])
    z1 = _mm_in(h1, wg_o_in, "mm_z1")
    ycat1, conv_out = _odd_fwd(z1, *odd_p)

    g = {}
    loss_part, dx2, dy1, g["o_post_norm"] = _post1_bwd(ycat1, w_out_o, x1, target, row(p["o_post_norm"]), h1_sib)
    parts = {}
    dw = _mm_out_dw(ycat1, dy1, "mm_dwout1").reshape(NDEV, 256, D)
    parts["o_w_out"], = _sc_exchange("scatter_o_w_out", 2, [dw], True)
    dycat1 = _mm_out_dx(dy1, w_out_o, "mm_dycat1", dw)
    dz1, ddc, g["o_sgu_w"], d_sgu_bb, g["o_sgu_norm_g"], g["o_sgu_norm_b"], g["o_conv_norm_g"], \
        g["o_conv_norm_b"], g["o_conv_b"] = _odd_bwd_a(z1, conv_out, dycat1, *odd_p[:4], *odd_p[6:])
    dz1, d_conv_w = _odd_bwd_b(z1, ddc, dz1, conv_w)
    g["o_sgu_b"] = d_sgu_bb[:, :, 0]
    g["o_conv_w"] = d_conv_w[:CONV_K]
    grads, deltas, new_m, new_v = {}, {}, {}, {}

    def adam(n, dep):
        grads[n], deltas[n], new_m[n], new_v[n] = _adam_reduce(parts[n], w[n], m[n], v[n], "adam_" + n, dep)
        return new_v[n]

    pin = _arrived(parts["o_w_out"], "arrived_o_w_out", d_conv_w)
    dz1_sib = _swap_class_columns("swap_dz1", 10, dz1, ODD_IN // NDEV)
    dw = _mm_pair_dw(h1, dz1, h1_sib, dz1_sib, ODD_IN // NDEV, "mm_dwin1", pin)
    parts["o_w_in"] = _sc_chip_scatter("scatter_o_w_in", 3, dw)
    dh1 = _mm_in_dx(dz1, wg_o_in, "mm_dh1", dw)
    dx1, dy0, g["o_pre_norm"], g["e_post_norm"] = _mid_bwd(dx2, dh1, x1, y0, row(p["o_pre_norm"]),
                                                           row(p["e_post_norm"]))
    dw = _mm_out_dw(ycat0, dy0, "mm_dwout0").reshape(NDEV, 256, D)
    parts["e_w_out"], = _sc_exchange("scatter_e_w_out", 4, [dw], True)
    dycat0 = _mm_out_dx(dy0, w_out_e, "mm_dycat0", dw)
    da_in, da_gate, g["e_pool_w"], g["e_pool_scale"] = _pool_bwd(z0, dycat0, pool_w_bf, row(p["e_pool_scale"]))
    late = [n for n in SMALL if n not in ("e_pre_norm", "o_sgu_b")] + ["o_sgu_b"]
    pieces = [g[n].reshape(SMALL[n][0]) for n in late[:-1]] + [jnp.broadcast_to(loss_part, (8, 128)), g[late[-1]]]
    recv_small, = _sc_gather_two_level("gather_small_grads", 6, [_pack(pieces, 512)])
    took = _arrived(parts["o_w_in"], "arrived_o_w_in")
    dq, dk, dv, dbg = _attn_bwd(z0, qkv, og, lg, dycat0, tabs, took)
    dz0 = jnp.concatenate([da_in, da_gate, dq, dk, dv, dbg], axis=1)
    took = _arrived(recv_small, "arrived_small_grads", _arrived(parts["e_w_out"], "arrived_e_w_out", dz0))
    nb = EVEN_IN // NDEV
    swapped = [_swap_class_columns("swap_dz0_%d" % half, (9, 14)[half], dz0, nb, half, 2) for half in (0, 1)]
    dw, e_w_in_parts = took, []
    for half in (0, 1):
        dw = _mm_pair_dw(h0t, dz0, h0t_sib, swapped[half], nb, "mm_dwin0_%d" % half, dw, half, 2, True, half == 1)
        e_w_in_parts.append(_sc_chip_scatter("scatter_e_w_in_%d" % half, (5, 15)[half], dw))
    pin = adam("e_w_out", adam("o_w_out", adam("o_w_in", dw)))
    rows = [int(np.prod(SMALL[n][0])) // 128 for n in late]
    sums = _sum_unpack(recv_small, rows[:-1] + [8, rows[-1]], "sum_small_grads", pin)
    summed = dict(zip(late, sums[:-2] + sums[-1:]))
    loss = sums[-2][0, 0]
    dh0 = _mm_in_dx_halves(dz0, wg_e_in, "mm_dh0", summed[late[0]])
    grad_x, g["e_pre_norm"] = _pre0_bwd(dx1, dh0, x, row(p["e_pre_norm"]))
    last, = _sc_exchange("gather_e_pre_norm_grad", 7, [g["e_pre_norm"].reshape(16, 128)], False)

    n = "e_w_in"
    out = _adam_reduce(e_w_in_parts[0], w[n], m[n], v[n], "adam_e_w_in_0", grad_x, 0, 2)
    out = _adam_reduce(e_w_in_parts[1], w[n], m[n], v[n], "adam_e_w_in_1", None, 1, 2, out)
    grads[n], deltas[n], new_m[n], new_v[n] = out
    summed["e_pre_norm"] = _sum_parts(last, "sum_e_pre_norm_grad", out[3])
    names = list(SMALL)
    views = [_small_views(n) for n in names]
    mine = lambda src: [src[n].reshape(vw[0]) for n, vw in zip(names, views)]
    res = _adam_small(mine(w), [summed[n].reshape(vw[1]) for n, vw in zip(names, views)], [vw[2] for vw in views],
                      mine(m), mine(v), "adam_small")
    for n, out in zip(names, res):
        grads[n], deltas[n], new_m[n], new_v[n] = [t.reshape(_shard_shape(n)) for t in out]

    lead = lambda a: a[None]
    return (loss, grad_x[None], *[lead(grads[n]) for n in WEIGHTS], *[lead(deltas[n]) for n in WEIGHTS],
            *[lead(new_m[n]) for n in WEIGHTS], *[lead(new_v[n]) for n in WEIGHTS])
```

```python
import numpy as np
import jax
import jax.numpy as jnp
from jax import lax
from jax.experimental import pallas as pl
from jax.experimental.pallas import tpu as pltpu
from jax.experimental.pallas import tpu_sc as plsc

F32 = jnp.float32
BF16 = jnp.bfloat16

S = 2048
D = 2048
NDEV = 8
EPS = 1e-6
NEG = -1e30
HEAD_DIM = 128
ROT_DIM = 32
ROPE_THETA = 500000.0
PATTERNS = ((128, 1), (512, 4), (2048, 16))
BLK = 128
EVEN_IN = 12288
ODD_IN = 6144
HALF = 1024
CONV_K = 31
HALO = 32
TR = 256
SUB = 32

ADAM_LR = 0.001
ADAM_B1 = 0.9
ADAM_B2 = 0.999
ADAM_EPS = 1e-08
ADAM_WD = 0.01
ADAM_STEP = 10

VMEM_BIG = 56 * 1024 * 1024
MESH = pl.DeviceIdType.MESH

NN = (((1,), (0,)), ((), ()))
NT = (((1,), (1,)), ((), ()))
TN = (((0,), (0,)), ((), ()))


def _dot(a, b, dn=NN):
    return lax.dot_general(a, b, dn, preferred_element_type=F32)


def _sigmoid(x):
    return 1.0 / (1.0 + jnp.exp(-x))


def _silu_and_grad(x):
    sg = _sigmoid(x)
    return x * sg, sg * (1.0 + x * (1.0 - sg))


def _params(sem, vmem=None):
    return pltpu.CompilerParams(dimension_semantics=sem, vmem_limit_bytes=vmem)


ANY_SPEC = pl.BlockSpec(memory_space=pl.ANY)


def _matmul(a, b, *, dn, grid, a_spec, b_spec, o_spec, out_shape, out_dtype, acc_shape, name, dep=None):
    nk = grid[2]
    deps = [] if dep is None else list(dep) if isinstance(dep, (tuple, list)) else [dep]

    def body(a_ref, b_ref, *rest):
        o_ref, acc = rest[len(deps)], rest[len(deps) + 1:]
        if nk == 1:
            o_ref[...] = _dot(a_ref[...], b_ref[...], dn).astype(o_ref.dtype)
            return
        acc_ref = acc[0]
        k = pl.program_id(2)

        @pl.when(k == 0)
        def _():
            acc_ref[...] = jnp.zeros_like(acc_ref)

        acc_ref[...] += _dot(a_ref[...], b_ref[...], dn)

        @pl.when(k == nk - 1)
        def _():
            o_ref[...] = acc_ref[...].astype(o_ref.dtype)

    return pl.pallas_call(
        body, grid=grid, in_specs=[a_spec, b_spec] + [ANY_SPEC] * len(deps), out_specs=o_spec,
        out_shape=jax.ShapeDtypeStruct(out_shape, out_dtype),
        scratch_shapes=[] if nk == 1 else [pltpu.VMEM(acc_shape, F32)],
        compiler_params=_params(("parallel", "parallel", "arbitrary"), VMEM_BIG), name=name,
    )(a, b, *deps)


TM = 2048


def _mm_in(h, wg, name):
    nb = wg.shape[2]
    tn = 512 if nb % 512 == 0 else nb
    per = nb // tn
    return _matmul(
        h, wg, dn=NN, grid=(S // TM, NDEV * per, 1),
        a_spec=pl.BlockSpec((TM, D), lambda i, j, k: (i, 0)),
        b_spec=pl.BlockSpec((None, D, tn), lambda i, j, k: (j // per, 0, j % per)),
        o_spec=pl.BlockSpec((TM, tn), lambda i, j, k: (i, j)),
        out_shape=(S, NDEV * nb), out_dtype=F32, acc_shape=(TM, tn), name=name)


def _mm_in_halves(h, wg_halves, name):
    hb = wg_halves[0].shape[2]
    z = None
    for half, wg in enumerate(wg_halves):
        prev = [] if z is None else [z]

        def body(a_ref, b_ref, *rest):
            rest[-1][...] = _dot(a_ref[...], b_ref[...])

        z = pl.pallas_call(
            body, grid=(NDEV,),
            in_specs=[pl.BlockSpec((S, D), lambda j: (0, 0)), pl.BlockSpec((None, D, hb), lambda j: (j, 0, 0))]
                     + [ANY_SPEC] * len(prev),
            out_specs=pl.BlockSpec((S, hb), lambda j, half=half: (0, 2 * j + half)),
            out_shape=jax.ShapeDtypeStruct((S, 2 * NDEV * hb), F32),
            input_output_aliases={2: 0} if prev else {},
            compiler_params=_params(("parallel",), VMEM_BIG), name="%s_%d" % (name, half),
        )(h, wg, *prev)
    return z


def _mm_in_dx_halves(dz, wg_halves, name, dep):
    hb = wg_halves[0].shape[2]
    nk = 2 * NDEV

    def body(a_ref, b0_ref, b1_ref, dep_ref, o_ref, acc_ref):
        k = pl.program_id(2)

        @pl.when(k == 0)
        def _():
            acc_ref[...] = jnp.zeros_like(acc_ref)

        @pl.when(k % 2 == 0)
        def _():
            acc_ref[...] += _dot(a_ref[...], b0_ref[...], NT)

        @pl.when(k % 2 == 1)
        def _():
            acc_ref[...] += _dot(a_ref[...], b1_ref[...], NT)

        @pl.when(k == nk - 1)
        def _():
            o_ref[...] = acc_ref[...]

    b_spec = pl.BlockSpec((None, 1024, hb), lambda i, j, k: (k // 2, j, 0))
    return pl.pallas_call(
        body, grid=(1, D // 1024, nk),
        in_specs=[pl.BlockSpec((S, hb), lambda i, j, k: (0, k)), b_spec, b_spec, ANY_SPEC],
        out_specs=pl.BlockSpec((S, 1024), lambda i, j, k: (0, j)), out_shape=jax.ShapeDtypeStruct((S, D), F32),
        scratch_shapes=[pltpu.VMEM((S, 1024), F32)],
        compiler_params=_params(("parallel", "parallel", "arbitrary"), VMEM_BIG), name=name,
    )(dz, *wg_halves, dep)


def _mm_in_dx(dz, wg, name, dep=None):
    nb = wg.shape[2]
    return _matmul(
        dz, wg, dn=NT, grid=(S // TM, D // 1024, NDEV),
        a_spec=pl.BlockSpec((TM, nb), lambda i, j, k: (i, k)),
        b_spec=pl.BlockSpec((None, 1024, nb), lambda i, j, k: (k, j, 0)),
        o_spec=pl.BlockSpec((TM, 1024), lambda i, j, k: (i, j)),
        out_shape=(S, D), out_dtype=F32, acc_shape=(TM, 1024), name=name, dep=dep)


def _mm_out_dx(dy, w, name, dep=None):
    return _matmul(
        dy, w, dn=NT, grid=(S // TM, 2048 // 512, 1),
        a_spec=pl.BlockSpec((TM, D), lambda i, j, k: (i, 0)),
        b_spec=pl.BlockSpec((512, D), lambda i, j, k: (j, 0)),
        o_spec=pl.BlockSpec((TM, 512), lambda i, j, k: (i, j)),
        out_shape=(S, 2048), out_dtype=F32, acc_shape=(TM, 512), name=name, dep=dep)


def _mm_out_dw(yc, dy, name):
    return _matmul(
        yc, dy, dn=TN, grid=(2048 // TM, D // 512, 1),
        a_spec=pl.BlockSpec((S, TM), lambda i, j, k: (0, i)),
        b_spec=pl.BlockSpec((S, 512), lambda i, j, k: (0, j)),
        o_spec=pl.BlockSpec((TM, 512), lambda i, j, k: (i, j)),
        out_shape=(2048, D), out_dtype=BF16, acc_shape=(TM, 512), name=name)


def _row_spec(w=D):
    return pl.BlockSpec((TR, w), lambda i: (i, 0))


def _vec_spec(w=D):
    return pl.BlockSpec((1, w), lambda i: (0, 0))


def _rms_stats(x):
    r = lax.rsqrt(jnp.mean(x * x, axis=-1, keepdims=True) + EPS)
    return x * r, r


def _rms_bwd(dn, xhat, r, g):
    dxh = dn * g
    return r * (dxh - xhat * jnp.mean(dxh * xhat, axis=-1, keepdims=True))


def _acc_rows(ref, val, i):
    s = jnp.sum(val, axis=0, keepdims=True)

    @pl.when(i == 0)
    def _():
        ref[...] = s

    @pl.when(i > 0)
    def _():
        ref[...] += s


def _pre0_fwd(x, g):
    def body(x_ref, g_ref, h_ref, ht_ref):
        xhat, _ = _rms_stats(x_ref[...])
        h = xhat * g_ref[...]
        h_ref[...] = h.astype(BF16)
        ht_ref[...] = h.T.astype(BF16)

    return pl.pallas_call(
        body, grid=(S // TR,), in_specs=[_row_spec(), _vec_spec()],
        out_specs=[_row_spec(), pl.BlockSpec((D, TR), lambda i: (0, i))],
        out_shape=[jax.ShapeDtypeStruct((S, D), BF16), jax.ShapeDtypeStruct((D, S), BF16)],
        compiler_params=_params(("parallel",)), name="pre0_fwd",
    )(x, g)


def _post0_fwd(ycat, w_out, x, g_post, g_pre1, dep):
    def body(yc_ref, w_ref, x_ref, gp_ref, g1_ref, dep_ref, y_ref, x1_ref, h1_ref):
        y = _dot(yc_ref[...], w_ref[...])
        y_ref[...] = y
        yhat, _ = _rms_stats(y)
        x1 = x_ref[...] + yhat * gp_ref[...]
        x1_ref[...] = x1
        xhat, _ = _rms_stats(x1)
        h1_ref[...] = (xhat * g1_ref[...]).astype(BF16)

    return pl.pallas_call(
        body, grid=(S // TR,),
        in_specs=[_row_spec(), pl.BlockSpec((2048, D), lambda i: (0, 0)), _row_spec(), _vec_spec(), _vec_spec(),
                  ANY_SPEC],
        out_specs=[_row_spec(), _row_spec(), _row_spec()],
        out_shape=[jax.ShapeDtypeStruct((S, D), F32), jax.ShapeDtypeStruct((S, D), F32),
                   jax.ShapeDtypeStruct((S, D), BF16)],
        compiler_params=_params(("parallel",), VMEM_BIG), name="post0_fwd",
    )(ycat, w_out, x, g_post, g_pre1, dep)


def _post1_bwd(ycat, w_out, x1, target, g_post, dep):
    def body(yc_ref, w_ref, x1_ref, t_ref, g_ref, dep_ref, loss_ref, dx2_ref, dy_ref, dg_ref):
        i = pl.program_id(0)
        yhat, r = _rms_stats(_dot(yc_ref[...], w_ref[...]))
        g = g_ref[...]
        err = x1_ref[...] + yhat * g - t_ref[...]
        part = jnp.sum(jnp.sum(err * err, axis=-1, keepdims=True), axis=0, keepdims=True) * (0.5 / D)
        _acc_rows(loss_ref, jnp.broadcast_to(part, (1, 128)), i)
        dx2 = err * (1.0 / D)
        dx2_ref[...] = dx2
        _acc_rows(dg_ref, dx2 * yhat, i)
        dy_ref[...] = _rms_bwd(dx2, yhat, r, g).astype(BF16)

    return pl.pallas_call(
        body, grid=(S // TR,),
        in_specs=[_row_spec(), pl.BlockSpec((2048, D), lambda i: (0, 0)), _row_spec(), _row_spec(), _vec_spec(),
                  ANY_SPEC],
        out_specs=[_vec_spec(128), _row_spec(), _row_spec(), _vec_spec()],
        out_shape=[jax.ShapeDtypeStruct((1, 128), F32), jax.ShapeDtypeStruct((S, D), F32),
                   jax.ShapeDtypeStruct((S, D), BF16), jax.ShapeDtypeStruct((1, D), F32)],
        compiler_params=_params(("arbitrary",), VMEM_BIG), name="post1_bwd",
    )(ycat, w_out, x1, target, g_post, dep)


def _mid_bwd(dx2, dh1, x1, y0, g_pre1, g_post0):
    def body(dx2_ref, dh_ref, x1_ref, y_ref, g1_ref, gp_ref, dx1_ref, dy_ref, dg1_ref, dgp_ref):
        i = pl.program_id(0)
        xhat, r1 = _rms_stats(x1_ref[...])
        dh = dh_ref[...]
        _acc_rows(dg1_ref, dh * xhat, i)
        dx1 = dx2_ref[...] + _rms_bwd(dh, xhat, r1, g1_ref[...])
        dx1_ref[...] = dx1
        yhat, r0 = _rms_stats(y_ref[...])
        _acc_rows(dgp_ref, dx1 * yhat, i)
        dy_ref[...] = _rms_bwd(dx1, yhat, r0, gp_ref[...]).astype(BF16)

    return pl.pallas_call(
        body, grid=(S // TR,),
        in_specs=[_row_spec(), _row_spec(), _row_spec(), _row_spec(), _vec_spec(), _vec_spec()],
        out_specs=[_row_spec(), _row_spec(), _vec_spec(), _vec_spec()],
        out_shape=[jax.ShapeDtypeStruct((S, D), F32), jax.ShapeDtypeStruct((S, D), BF16),
                   jax.ShapeDtypeStruct((1, D), F32), jax.ShapeDtypeStruct((1, D), F32)],
        compiler_params=_params(("arbitrary",)), name="mid_bwd",
    )(dx2, dh1, x1, y0, g_pre1, g_post0)


def _pre0_bwd(dx1, dh0, x, g):
    def body(dx1_ref, dh_ref, x_ref, g_ref, gx_ref, dg_ref):
        i = pl.program_id(0)
        xhat, r = _rms_stats(x_ref[...])
        dh = dh_ref[...]
        _acc_rows(dg_ref, dh * xhat, i)
        gx_ref[...] = dx1_ref[...] + _rms_bwd(dh, xhat, r, g_ref[...])

    return pl.pallas_call(
        body, grid=(S // TR,), in_specs=[_row_spec(), _row_spec(), _row_spec(), _vec_spec()],
        out_specs=[_row_spec(), _vec_spec()],
        out_shape=[jax.ShapeDtypeStruct((S, D), F32), jax.ShapeDtypeStruct((1, D), F32)],
        compiler_params=_params(("arbitrary",)), name="pre0_bwd",
    )(dx1, dh0, x, g)


POOL_CH = 256


def _pool_apply(a, w, transpose):
    n = a.shape[0]
    row = lax.broadcasted_iota(jnp.int32, a.shape, 0)
    cnt = jnp.minimum(row + 1, w).astype(F32)
    s = a / cnt if transpose else a
    for k in (1, 2, 4, 8):
        if transpose:
            sh = jnp.where(row < n - k, pltpu.roll(s, n - k, 0), 0.0)
        else:
            sh = jnp.where(row >= k, pltpu.roll(s, k, 0), 0.0)
        s = jnp.where(w > k, s + sh, s)
    return s - a if transpose else s / cnt - a


def _pool_fwd(z0, pool_w, pool_scale):
    def body(a_ref, gate_ref, w_ref, sc_ref, out_ref):
        win = jnp.left_shift(2, pl.program_id(0))
        pooled = _pool_apply(a_ref[...], win, False)
        mixed = _dot(pooled.astype(BF16), w_ref[...])
        gate = gate_ref[...]
        out_ref[...] = (mixed * sc_ref[...] * (gate * _sigmoid(gate))).astype(BF16)

    return pl.pallas_call(
        body, grid=(4,),
        in_specs=[pl.BlockSpec((S, POOL_CH), lambda g: (0, g)), pl.BlockSpec((S, POOL_CH), lambda g: (0, 4 + g)),
                  pl.BlockSpec((None, POOL_CH, POOL_CH), lambda g: (g, 0, 0)),
                  pl.BlockSpec((1, POOL_CH), lambda g: (0, g))],
        out_specs=pl.BlockSpec((S, POOL_CH), lambda g: (0, g)),
        out_shape=jax.ShapeDtypeStruct((S, 2048), BF16),
        compiler_params=_params(("parallel",), VMEM_BIG), name="pool_fwd",
    )(z0, z0, pool_w, pool_scale)


def _pool_bwd(z0, dycat, pool_w, pool_scale):
    def body(a_ref, gate_ref, dy_ref, w_ref, sc_ref, da_ref, dgate_ref, dw_ref, dsc_ref):
        win = jnp.left_shift(2, pl.program_id(0))
        pooled = _pool_apply(a_ref[...], win, False).astype(BF16)
        w = w_ref[...]
        mixed = _dot(pooled, w)
        silu, dsilu = _silu_and_grad(gate_ref[...])
        dy = dy_ref[...]
        sc = sc_ref[...]
        dgate_ref[...] = (dy * (mixed * sc) * dsilu).astype(BF16)
        dms = dy * silu
        dsc_ref[...] = jnp.sum(dms * mixed, axis=0, keepdims=True)
        dmixed = (dms * sc).astype(BF16)
        dw_ref[...] = _dot(pooled, dmixed, TN)
        dpooled = _dot(dmixed, w, NT)
        da_ref[...] = _pool_apply(dpooled, win, True).astype(BF16)

    slab = lambda off: pl.BlockSpec((S, POOL_CH), lambda g: (0, off + g))
    return pl.pallas_call(
        body, grid=(4,),
        in_specs=[slab(0), slab(4), slab(0), pl.BlockSpec((None, POOL_CH, POOL_CH), lambda g: (g, 0, 0)),
                  pl.BlockSpec((1, POOL_CH), lambda g: (0, g))],
        out_specs=[slab(0), slab(0), pl.BlockSpec((None, POOL_CH, POOL_CH), lambda g: (g, 0, 0)),
                   pl.BlockSpec((1, POOL_CH), lambda g: (0, g))],
        out_shape=[jax.ShapeDtypeStruct((S, HALF), BF16), jax.ShapeDtypeStruct((S, HALF), BF16),
                   jax.ShapeDtypeStruct((4, POOL_CH, POOL_CH), F32), jax.ShapeDtypeStruct((1, HALF), F32)],
        compiler_params=_params(("parallel",), VMEM_BIG), name="pool_bwd",
    )(z0, z0, dycat, pool_w, pool_scale)


Q_COL, K_COL, V_COL, BG_COL = 2048 // 128, 5120 // 128, 8192 // 128, 11264 // 128
SCALE = HEAD_DIM ** -0.5


def _rope_tables():
    pos = jnp.arange(S, dtype=F32)
    inv_freq = jnp.power(ROPE_THETA, -jnp.arange(0, ROT_DIM, 2, dtype=F32) / ROT_DIM)
    ang = pos[:, None] * inv_freq[None, :]
    cos, sin = jnp.cos(ang), jnp.sin(ang)
    half = ROT_DIM // 2
    zeros = jnp.zeros((S, HEAD_DIM - ROT_DIM), F32)
    c = jnp.concatenate([cos, cos, jnp.ones((S, HEAD_DIM - ROT_DIM), F32)], axis=1)
    a = jnp.concatenate([-sin, jnp.zeros((S, half), F32), zeros], axis=1)
    b = jnp.concatenate([jnp.zeros((S, half), F32), sin, zeros], axis=1)
    return c, a, b


def _rope(t, c, a, b):
    half = ROT_DIM // 2
    return t * c + pltpu.roll(t, HEAD_DIM - half, 1) * a + pltpu.roll(t, half, 1) * b


def _rope_t(d, c, a, b):
    half = ROT_DIM // 2
    return d * c + pltpu.roll(d * a, half, 1) + pltpu.roll(d * b, HEAD_DIM - half, 1)


def _deinterleave(dst, src, dil, cast=None, dst_off=0):
    length = S // dil
    for r in range(dil):
        v = src[...] if dil == 1 else src[pl.ds(r, length, stride=dil), :]
        dst[dst_off + r * length:dst_off + (r + 1) * length, :] = v if cast is None else v.astype(cast)


def _interleave(dst, src, dil, src_off=0):
    length = S // dil
    for r in range(dil):
        if dil == 1:
            dst[...] = src[src_off:src_off + S, :]
        else:
            dst[pl.ds(r, length, stride=dil), :] = src[src_off + r * length:src_off + (r + 1) * length, :]


CU = 8
NUNITS = S // BLK
B_QK = (((2,), (2,)), ((0,), (0,)))
B_PV = (((2,), (1,)), ((0,), (0,)))
B_TN = (((1,), (1,)), ((0,), (0,)))


def _blocks(ref, first):
    return ref[first * BLK:(first + CU) * BLK, :].reshape(CU, BLK, HEAD_DIM)


def _chunk_scores(u0, nb, qd, kdp):
    q = _blocks(qd, u0)
    row = lax.broadcasted_iota(jnp.int32, (CU, BLK, BLK), 1)
    col = lax.broadcasted_iota(jnp.int32, (CU, BLK, BLK), 2)
    s_own = jnp.where(col <= row, _dot(q, _blocks(kdp, u0 + 1), B_QK) * SCALE, NEG)
    if nb == 1:
        return q, s_own, None
    unit = lax.broadcasted_iota(jnp.int32, (CU, BLK, BLK), 0) + u0
    s_prev = jnp.where((col >= row) & ((unit % nb) != 0), _dot(q, _blocks(kdp, u0), B_QK) * SCALE, NEG)
    return q, s_own, s_prev


def _qkv_prep(z0, tabs):
    def body(q_ref, k_ref, v_ref, c_ref, a_ref, b_ref, qo_ref, ko_ref, vo_ref, tmp):
        p = pl.program_id(1)
        ko_ref[0:BLK, :] = jnp.zeros((BLK, HEAD_DIM), BF16)
        vo_ref[0:BLK, :] = jnp.zeros((BLK, HEAD_DIM), BF16)
        for gi, (_, dil) in enumerate(PATTERNS):
            @pl.when(p == gi)
            def _(dil=dil):
                c, a, b = c_ref[...], a_ref[...], b_ref[...]
                tmp[...] = _rope(q_ref[...], c, a, b)
                _deinterleave(qo_ref, tmp, dil, BF16)
                tmp[...] = _rope(k_ref[...], c, a, b)
                _deinterleave(ko_ref, tmp, dil, BF16, BLK)
                _deinterleave(vo_ref, v_ref, dil, BF16, BLK)

    tab = pl.BlockSpec((S, HEAD_DIM), lambda h, p: (0, 0))
    out = pl.BlockSpec((S, HEAD_DIM), lambda h, p: (0, p * 8 + h))
    outp = pl.BlockSpec((S + BLK, HEAD_DIM), lambda h, p: (0, p * 8 + h))
    return pl.pallas_call(
        body, grid=(8, 3), in_specs=[_head_spec(Q_COL), _head_spec(K_COL), _head_spec(V_COL), tab, tab, tab],
        out_specs=[out, outp, outp],
        out_shape=[jax.ShapeDtypeStruct((S, 3072), BF16)] + [jax.ShapeDtypeStruct((S + BLK, 3072), BF16)] * 2,
        scratch_shapes=[pltpu.VMEM((S, HEAD_DIM), F32)],
        compiler_params=_params(("parallel", "arbitrary"), VMEM_BIG), name="qkv_prep",
    )(z0, z0, z0, *tabs)


def _attn_group_fwd(dil, qd, kdp, vdp, od, ld, og, lg):
    nb = S // dil // BLK
    for u0 in range(0, NUNITS, CU):
        _, s_own, s_prev = _chunk_scores(u0, nb, qd, kdp)
        m = jnp.max(s_own, axis=2, keepdims=True)
        if s_prev is not None:
            m = jnp.maximum(m, jnp.max(s_prev, axis=2, keepdims=True))
        p_own = jnp.exp(s_own - m)
        den = jnp.sum(p_own, axis=2, keepdims=True)
        acc = _dot(p_own.astype(BF16), _blocks(vdp, u0 + 1), B_PV)
        if s_prev is not None:
            p_prev = jnp.exp(s_prev - m)
            den = den + jnp.sum(p_prev, axis=2, keepdims=True)
            acc = acc + _dot(p_prev.astype(BF16), _blocks(vdp, u0), B_PV)
        rows = slice(u0 * BLK, (u0 + CU) * BLK)
        od[rows, :] = (acc / den).reshape(CU * BLK, HEAD_DIM)
        ld[rows, :] = jnp.broadcast_to(m + jnp.log(den), (CU, BLK, HEAD_DIM)).reshape(CU * BLK, HEAD_DIM)
    _interleave(og, od, dil)
    _interleave(lg, ld, dil)


def _group_weights(lgs):
    l0, l1, l2 = lgs[0][...], lgs[1][...], lgs[2][...]
    mx = jnp.maximum(l0, jnp.maximum(l1, l2))
    e0, e1, e2 = jnp.exp(l0 - mx), jnp.exp(l1 - mx), jnp.exp(l2 - mx)
    den = e0 + e1 + e2
    return e0 / den, e1 / den, e2 / den


def _head_spec(base):
    return pl.BlockSpec((S, HEAD_DIM), lambda h, p: (0, base + (p % 3) * 8 + h))


def _slab(dtype=F32, rows=S):
    return pltpu.VMEM((rows, HEAD_DIM), dtype)


def _attn_fwd(z0, qkv, ycat):
    def body(q_ref, k_ref, v_ref, gate_ref, ycat_ref, out_ref, og_ref, lg_ref,
             od, ld, og0, og1, og2, lg0, lg1, lg2):
        del ycat_ref
        p = pl.program_id(1)
        ogs, lgs = (og0, og1, og2), (lg0, lg1, lg2)
        for gi, (_, dil) in enumerate(PATTERNS):
            @pl.when(p == gi)
            def _(gi=gi, dil=dil):
                _attn_group_fwd(dil, q_ref, k_ref, v_ref, od, ld, ogs[gi], lgs[gi])
                og_ref[...] = ogs[gi][...]
                lg_ref[...] = lgs[gi][...]

        @pl.when(p == 2)
        def _():
            w0, w1, w2 = _group_weights(lgs)
            o = w0 * og0[...] + w1 * og1[...] + w2 * og2[...]
            gate = gate_ref[...]
            out_ref[...] = (o * (gate * _sigmoid(gate))).astype(BF16)

    grp = pl.BlockSpec((S, HEAD_DIM), lambda h, p: (0, p * 8 + h))
    grp_pad = pl.BlockSpec((S + BLK, HEAD_DIM), lambda h, p: (0, p * 8 + h))
    return pl.pallas_call(
        body, grid=(8, 3),
        in_specs=[grp, grp_pad, grp_pad, pl.BlockSpec((S, HEAD_DIM), lambda h, p: (0, BG_COL + h)), ANY_SPEC],
        out_specs=[pl.BlockSpec((S, HEAD_DIM), lambda h, p: (0, 8 + h)), grp, grp],
        out_shape=[jax.ShapeDtypeStruct((S, 2048), BF16), jax.ShapeDtypeStruct((S, 3072), F32),
                   jax.ShapeDtypeStruct((S, 3072), F32)],
        scratch_shapes=[_slab() for _ in range(8)],
        input_output_aliases={4: 0},
        compiler_params=_params(("parallel", "arbitrary"), VMEM_BIG), name="attn_fwd",
    )(*qkv, z0, ycat)


def _attn_bwd(z0, qkv, og, lg, dycat, tabs, dep):
    def body(q_ref, k_ref, v_ref, gate_ref, dy_ref, c_ref, a_ref, b_ref,
             og0_ref, og1_ref, og2_ref, lg0_ref, lg1_ref, lg2_ref, dep_ref,
             dq_ref, dk_ref, dv_ref, dbg_ref,
             tmp, ld, dg0, dg1, dg2, cg0, cg1, cg2, dod, cd, dqd, dkd, dvd):
        kd, vd = k_ref, v_ref
        p = pl.program_id(1)
        ogs, lgs, dgs, cgs = (og0_ref, og1_ref, og2_ref), (lg0_ref, lg1_ref, lg2_ref), (dg0, dg1, dg2), (cg0, cg1, cg2)

        @pl.when(p == 0)
        def _():
            w = _group_weights(lgs)
            o = w[0] * ogs[0][...] + w[1] * ogs[1][...] + w[2] * ogs[2][...]
            silu, dsilu = _silu_and_grad(gate_ref[...])
            dy = dy_ref[...]
            dbg_ref[...] = (dy * o * dsilu).astype(BF16)
            do = dy * silu
            dwbar = jnp.sum(do * o, axis=1, keepdims=True)
            for gi in range(3):
                dgs[gi][...] = w[gi] * do
                cgs[gi][...] = -w[gi] * dwbar

        for gi, (_, dil) in enumerate(PATTERNS):
            @pl.when(p == 1 + gi)
            def _(gi=gi, dil=dil):
                nb = S // dil // BLK
                qd = q_ref
                c, a, b = c_ref[...], a_ref[...], b_ref[...]
                _deinterleave(dod, dgs[gi], dil, BF16)
                _deinterleave(ld, lgs[gi], dil)
                _deinterleave(cd, cgs[gi], dil)
                dkd[...] = jnp.zeros_like(dkd)
                dvd[...] = jnp.zeros_like(dvd)
                flat = lambda t: t.reshape(CU * BLK, HEAD_DIM)
                for u0 in range(0, NUNITS, CU):
                    q, s_own, s_prev = _chunk_scores(u0, nb, qd, kd)
                    lse, cv, do = _blocks(ld, u0), _blocks(cd, u0), _blocks(dod, u0)
                    own = slice((u0 + 1) * BLK, (u0 + 1 + CU) * BLK)
                    p_own = jnp.exp(s_own - lse)
                    ds_own = (p_own * (_dot(do, _blocks(vd, u0 + 1), B_QK) + cv) * SCALE).astype(BF16)
                    dq = _dot(ds_own, _blocks(kd, u0 + 1), B_PV)
                    dkd[own, :] += flat(_dot(ds_own, q, B_TN))
                    dvd[own, :] += flat(_dot(p_own.astype(BF16), do, B_TN))
                    if s_prev is not None:
                        prev = slice(u0 * BLK, (u0 + CU) * BLK)
                        p_prev = jnp.exp(s_prev - lse)
                        ds_prev = (p_prev * (_dot(do, _blocks(vd, u0), B_QK) + cv) * SCALE).astype(BF16)
                        dq = dq + _dot(ds_prev, _blocks(kd, u0), B_PV)
                        dkd[prev, :] += flat(_dot(ds_prev, q, B_TN))
                        dvd[prev, :] += flat(_dot(p_prev.astype(BF16), do, B_TN))
                    dqd[u0 * BLK:(u0 + CU) * BLK, :] = flat(dq)
                _interleave(tmp, dqd, dil)
                dq_ref[...] = _rope_t(tmp[...], c, a, b).astype(BF16)
                _interleave(tmp, dkd, dil, BLK)
                dk_ref[...] = _rope_t(tmp[...], c, a, b).astype(BF16)
                _interleave(tmp, dvd, dil, BLK)
                dv_ref[...] = tmp[...].astype(BF16)

    tab = pl.BlockSpec((S, HEAD_DIM), lambda h, p: (0, 0))
    hspec = lambda base: pl.BlockSpec((S, HEAD_DIM), lambda h, p: (0, base + h))
    gspec = pl.BlockSpec((S, HEAD_DIM), lambda h, p: (0, jnp.maximum(p - 1, 0) * 8 + h))
    gspec_pad = pl.BlockSpec((S + BLK, HEAD_DIM), lambda h, p: (0, jnp.maximum(p - 1, 0) * 8 + h))
    return pl.pallas_call(
        body, grid=(8, 4),
        in_specs=[gspec, gspec_pad, gspec_pad, hspec(BG_COL), hspec(8), tab, tab, tab,
                  hspec(0), hspec(8), hspec(16), hspec(0), hspec(8), hspec(16), ANY_SPEC],
        out_specs=[gspec, gspec, gspec, hspec(0)],
        out_shape=[jax.ShapeDtypeStruct((S, 3072), BF16)] * 3 + [jax.ShapeDtypeStruct((S, HALF), BF16)],
        scratch_shapes=[_slab(), _slab()] + [_slab() for _ in range(6)]
                       + [_slab(BF16), _slab(), _slab(), _slab(F32, S + BLK), _slab(F32, S + BLK)],
        compiler_params=_params(("parallel", "arbitrary"), VMEM_BIG), name="attn_bwd",
    )(*qkv, z0, dycat, *tabs, og, og, og, lg, lg, lg, dep)


SGU_CH = 256
NCHUNK = TR // 128


def _ln_stats(x):
    mu = jnp.mean(x, axis=-1, keepdims=True)
    xc = x - mu
    r = lax.rsqrt(jnp.mean(xc * xc, axis=-1, keepdims=True) + EPS)
    return xc * r, r


def _ln_bwd(dy, xhat, r, g):
    dxh = dy * g
    return r * (dxh - jnp.mean(dxh, axis=-1, keepdims=True) - xhat * jnp.mean(dxh * xhat, axis=-1, keepdims=True))


def _tril_bf16(w):
    row = lax.broadcasted_iota(jnp.int32, w.shape, 0)
    col = lax.broadcasted_iota(jnp.int32, w.shape, 1)
    return jnp.where(row >= col, w, 0.0).astype(BF16)


def _sgu_gate(vn_s, s_s, w_ref, bb_ref):
    for h in range(4):
        wm = _tril_bf16(w_ref[h])
        bias = bb_ref[h]
        for ch in range(NCHUNK):
            rows, cols = slice(ch * 128, (ch + 1) * 128), slice(h * SGU_CH, (h + 1) * SGU_CH)
            s_s[rows, cols] = _dot(wm, vn_s[rows, cols]) + jnp.concatenate([bias, bias], axis=1)


WIN = HALO + TR
SUBL = 8


def _shifted_copies(dst, src):
    dst[0] = src[...]
    for b in range(1, SUBL):
        dst[b, 0:WIN - SUBL, :] = src[pl.ds(b, WIN - SUBL), :]


def _rows_at(copies, off, n):
    return copies[off % SUBL, pl.ds(off - off % SUBL, n), :]


def _conv_fwd(i, dval_ref, dglu_ref, hval_ref, hglu_ref, cw_ref, cb_ref, xw, xr, dcs):
    halo = hval_ref[...] * _sigmoid(hglu_ref[...])
    xw[0:HALO, :] = jnp.where(i > 0, halo, 0.0)
    xw[HALO:HALO + TR, :] = dval_ref[...] * _sigmoid(dglu_ref[...])
    _shifted_copies(xr, xw)
    for rb in range(TR // SUB):
        acc = jnp.broadcast_to(cb_ref[...], (SUB, HALF))
        for k in range(CONV_K):
            acc = acc + cw_ref[k:k + 1, :] * _rows_at(xr, rb * SUB + HALO - (CONV_K - 1) + k, SUB)
        dcs[rb * SUB:(rb + 1) * SUB, :] = acc


def _odd_in_specs():
    col = lambda j: pl.BlockSpec((TR, HALF), lambda i, *_: (i, j))
    prev = lambda j: pl.BlockSpec((HALO, HALF), lambda i, *_: (jnp.maximum(i * (TR // HALO) - 1, 0), j))
    return [col(0), col(1), col(2), col(3), col(4), col(5), prev(3), prev(4)]


def _full_spec(shape):
    return pl.BlockSpec(shape, lambda i, *_: (0,) * len(shape))


def _odd_fwd(z1, sgu_g, sgu_b, sgu_w, sgu_bb, conv_w, conv_b, cn_g, cn_b):
    def body(u_ref, v_ref, cg_ref, dval_ref, dglu_ref, dgate_ref, hval_ref, hglu_ref,
             g_ref, b_ref, w_ref, bb_ref, cw_ref, cb_ref, cng_ref, cnb_ref, out_ref, dcs, vn_s, s_s, xw, xr):
        i = pl.program_id(0)
        vhat, _ = _ln_stats(v_ref[...])
        vn_s[...] = (vhat * g_ref[...] + b_ref[...]).astype(BF16)
        _sgu_gate(vn_s, s_s, w_ref, bb_ref)
        cg = cg_ref[...]
        out_ref[:, 0:HALF] = (u_ref[...] * s_s[...] * (cg * _sigmoid(cg))).astype(BF16)
        _conv_fwd(i, dval_ref, dglu_ref, hval_ref, hglu_ref, cw_ref, cb_ref, xw, xr, dcs)
        dhat, _ = _ln_stats(dcs[...])
        dn = dhat * cng_ref[...] + cnb_ref[...]
        dgate = dgate_ref[...]
        out_ref[:, HALF:2 * HALF] = ((dn * _sigmoid(dn)) * (dgate * _sigmoid(dgate))).astype(BF16)

    vec = _full_spec((1, HALF))
    return pl.pallas_call(
        body, grid=(S // TR,),
        in_specs=_odd_in_specs() + [vec, vec, _full_spec((4, 128, 128)), _full_spec((4, 128, 128)),
                                    _full_spec((HALO, HALF)), vec, vec, vec],
        out_specs=[pl.BlockSpec((TR, 2048), lambda i: (i, 0)), pl.BlockSpec((TR, HALF), lambda i: (i, 0))],
        out_shape=[jax.ShapeDtypeStruct((S, 2048), BF16), jax.ShapeDtypeStruct((S, HALF), F32)],
        scratch_shapes=[pltpu.VMEM((TR, HALF), BF16), pltpu.VMEM((TR, HALF), F32),
                        pltpu.VMEM((WIN, HALF), F32), pltpu.VMEM((SUBL, WIN, HALF), F32)],
        compiler_params=_params(("parallel",), VMEM_BIG), name="odd_fwd",
    )(z1, z1, z1, z1, z1, z1, z1, z1, sgu_g, sgu_b, sgu_w, sgu_bb, conv_w, conv_b, cn_g, cn_b)


def _odd_bwd_a(z1, dc, dycat, sgu_g, sgu_b, sgu_w, sgu_bb, cn_g, cn_b):
    def body(u_ref, v_ref, cg_ref, dgate_ref, dcs, dy_ref, g_ref, b_ref, w_ref, bb_ref, cng_ref, cnb_ref,
             dz_ref, ddc_ref, dw_ref, dbb_ref, dg_ref, db_ref, dcng_ref, dcnb_ref, dcb_ref,
             vn_s, s_s, ds_s, dvn_s):
        i = pl.program_id(0)
        vhat, rv = _ln_stats(v_ref[...])
        g = g_ref[...]
        vn_s[...] = (vhat * g + b_ref[...]).astype(BF16)
        _sgu_gate(vn_s, s_s, w_ref, bb_ref)
        silu_c, dsilu_c = _silu_and_grad(cg_ref[...])
        dyc = dy_ref[:, 0:HALF]
        u = u_ref[...]
        s = s_s[...]
        dz_ref[:, 0:HALF] = (dyc * s * silu_c).astype(BF16)
        dz_ref[:, 2 * HALF:3 * HALF] = (dyc * u * s * dsilu_c).astype(BF16)
        ds_s[...] = dyc * u * silu_c

        @pl.when(i == 0)
        def _():
            dw_ref[...] = jnp.zeros_like(dw_ref)
            dbb_ref[...] = jnp.zeros_like(dbb_ref)

        tril = lax.broadcasted_iota(jnp.int32, (128, 128), 0) >= lax.broadcasted_iota(jnp.int32, (128, 128), 1)
        for h in range(4):
            wm = _tril_bf16(w_ref[h])
            for ch in range(NCHUNK):
                rows, cols = slice(ch * 128, (ch + 1) * 128), slice(h * SGU_CH, (h + 1) * SGU_CH)
                ds = ds_s[rows, cols]
                dsb = ds.astype(BF16)
                dw_ref[h] += jnp.where(tril, _dot(dsb, vn_s[rows, cols], NT), 0.0)
                dbb_ref[h] += jnp.broadcast_to(jnp.sum(ds, axis=1, keepdims=True), (128, 128))
                dvn_s[rows, cols] = _dot(wm, dsb, TN)
        dvn = dvn_s[...]
        _acc_rows(dg_ref, dvn * vhat, i)
        _acc_rows(db_ref, dvn, i)
        dz_ref[:, HALF:2 * HALF] = _ln_bwd(dvn, vhat, rv, g).astype(BF16)

        dhat, rd = _ln_stats(dcs[...])
        cng = cng_ref[...]
        silu_n, dsilu_n = _silu_and_grad(dhat * cng + cnb_ref[...])
        silu_g, dsilu_g = _silu_and_grad(dgate_ref[...])
        dyd = dy_ref[:, HALF:2 * HALF]
        dz_ref[:, 5 * HALF:6 * HALF] = (dyd * silu_n * dsilu_g).astype(BF16)
        ddn = dyd * silu_g * dsilu_n
        _acc_rows(dcng_ref, ddn * dhat, i)
        _acc_rows(dcnb_ref, ddn, i)
        ddc = _ln_bwd(ddn, dhat, rd, cng)
        ddc_ref[...] = ddc
        _acc_rows(dcb_ref, ddc, i)

    vec = _full_spec((1, HALF))
    sq = _full_spec((4, 128, 128))
    col = lambda j: pl.BlockSpec((TR, HALF), lambda i: (i, j))
    return pl.pallas_call(
        body, grid=(S // TR,),
        in_specs=[col(0), col(1), col(2), col(5), col(0), pl.BlockSpec((TR, 2048), lambda i: (i, 0)),
                  vec, vec, sq, sq, vec, vec],
        out_specs=[pl.BlockSpec((TR, ODD_IN), lambda i: (i, 0)), pl.BlockSpec((TR, HALF), lambda i: (i, 0)),
                   sq, sq, vec, vec, vec, vec, vec],
        out_shape=[jax.ShapeDtypeStruct((S, ODD_IN), BF16), jax.ShapeDtypeStruct((S, HALF), F32),
                   jax.ShapeDtypeStruct((4, 128, 128), F32), jax.ShapeDtypeStruct((4, 128, 128), F32)]
                  + [jax.ShapeDtypeStruct((1, HALF), F32)] * 5,
        scratch_shapes=[pltpu.VMEM((TR, HALF), BF16), pltpu.VMEM((TR, HALF), F32),
                        pltpu.VMEM((TR, HALF), F32), pltpu.VMEM((TR, HALF), F32)],
        compiler_params=_params(("arbitrary",), VMEM_BIG), name="odd_bwd_a",
    )(z1, z1, z1, z1, dc, dycat, sgu_g, sgu_b, sgu_w, sgu_bb, cn_g, cn_b)


def _odd_bwd_b(z1, ddc, dz1, conv_w):
    nt = S // TR

    def body(dval_ref, dglu_ref, hval_ref, hglu_ref, ddc_ref, hddc_ref, cw_ref, dz_in_ref,
             dz_ref, dcw_ref, xw, dwin, dxs, xr, dr):
        del dz_in_ref
        i, j = pl.program_id(0), pl.program_id(1)
        sg = _sigmoid(dglu_ref[...])
        dval = dval_ref[...]

        @pl.when(j == 0)
        def _():
            halo = hval_ref[...] * _sigmoid(hglu_ref[...])
            xw[0:HALO, :] = jnp.where(i > 0, halo, 0.0)
            xw[HALO:HALO + TR, :] = dval * sg
            dwin[0:TR, :] = ddc_ref[...]
            dwin[TR:TR + HALO, :] = jnp.where(i < nt - 1, hddc_ref[...], 0.0)
            _shifted_copies(xr, xw)
            _shifted_copies(dr, dwin)

            @pl.when(i == 0)
            def _():
                dcw_ref[...] = jnp.zeros_like(dcw_ref)

            for rb in range(TR // SUB):
                acc = jnp.zeros((SUB, HALF), F32)
                for k in range(CONV_K):
                    acc = acc + cw_ref[k:k + 1, :] * _rows_at(dr, rb * SUB + (CONV_K - 1) - k, SUB)
                dxs[rb * SUB:(rb + 1) * SUB, :] = acc
            for k in range(CONV_K):
                acc = jnp.zeros((SUB, HALF), F32)
                for rb in range(TR // SUB):
                    acc = acc + dwin[rb * SUB:(rb + 1) * SUB, :] * _rows_at(xr, rb * SUB + HALO - (CONV_K - 1) + k, SUB)
                dcw_ref[k:k + 1, :] += jnp.sum(acc, axis=0, keepdims=True)
            dz_ref[...] = (dxs[...] * sg).astype(BF16)

        @pl.when(j == 1)
        def _():
            dz_ref[...] = (dxs[...] * dval * sg * (1.0 - sg)).astype(BF16)

    col = lambda c: pl.BlockSpec((TR, HALF), lambda i, j: (i, c))
    prev = lambda c: pl.BlockSpec((HALO, HALF), lambda i, j: (jnp.maximum(i * (TR // HALO) - 1, 0), c))
    nxt = pl.BlockSpec((HALO, HALF), lambda i, j: (jnp.minimum((i + 1) * (TR // HALO), S // HALO - 1), 0))
    return pl.pallas_call(
        body, grid=(nt, 2),
        in_specs=[col(3), col(4), prev(3), prev(4), pl.BlockSpec((TR, HALF), lambda i, j: (i, 0)), nxt,
                  _full_spec((HALO, HALF)), pl.BlockSpec(memory_space=pl.ANY)],
        out_specs=[pl.BlockSpec((TR, HALF), lambda i, j: (i, 3 + j)), _full_spec((HALO, HALF))],
        out_shape=[jax.ShapeDtypeStruct((S, ODD_IN), BF16), jax.ShapeDtypeStruct((HALO, HALF), F32)],
        scratch_shapes=[pltpu.VMEM((WIN, HALF), F32), pltpu.VMEM((WIN, HALF), F32), pltpu.VMEM((TR, HALF), F32),
                        pltpu.VMEM((SUBL, WIN, HALF), F32), pltpu.VMEM((SUBL, WIN, HALF), F32)],
        input_output_aliases={7: 0},
        compiler_params=_params(("arbitrary", "arbitrary"), VMEM_BIG), name="odd_bwd_b",
    )(z1, z1, z1, z1, ddc, ddc, conv_w, dz1)


def _cast_bf16(w, name, piece=0, npieces=1):
    r, c = w.shape[0], w.shape[1] // npieces
    tr = min(r, 256)

    def body(i_ref, o_ref):
        o_ref[...] = i_ref[...].astype(BF16)

    return pl.pallas_call(
        body, grid=(r // tr,), in_specs=[pl.BlockSpec((tr, c), lambda i: (i, piece))],
        out_specs=pl.BlockSpec((tr, c), lambda i: (i, 0)), out_shape=jax.ShapeDtypeStruct((r, c), BF16),
        compiler_params=_params(("parallel",)), name=name,
    )(w)


def _adamw(w, g, m, v):
    m = ADAM_B1 * m + (1.0 - ADAM_B1) * g
    v = ADAM_B2 * v + (1.0 - ADAM_B2) * (g * g)
    m_hat = m / (1.0 - ADAM_B1 ** ADAM_STEP)
    v_hat = v / (1.0 - ADAM_B2 ** ADAM_STEP)
    delta = -ADAM_LR * (m_hat / (jnp.sqrt(v_hat) + ADAM_EPS) + ADAM_WD * w)
    return delta, m, v


def _adam_reduce(parts, w, m, v, name, dep=None, piece=0, npieces=1, prev=None):
    r, c = w.shape
    cp = c // npieces
    tr = min(r, 128)
    extra = ([] if dep is None else [dep]) + ([] if prev is None else list(prev))
    nparts = parts.shape[0]

    def body(p_ref, w_ref, m_ref, v_ref, *rest):
        g_ref, d_ref, nm_ref, nv_ref = rest[len(extra):]
        g = p_ref[0].astype(F32)
        for d in range(1, nparts):
            g = g + p_ref[d].astype(F32)
        g_ref[...] = g
        d_ref[...], nm_ref[...], nv_ref[...] = _adamw(w_ref[...], g, m_ref[...], v_ref[...])

    spec = pl.BlockSpec((tr, cp), lambda i: (i, piece))
    first = 4 + (0 if dep is None else 1)
    return pl.pallas_call(
        body, grid=(r // tr,),
        in_specs=[pl.BlockSpec((nparts, tr, cp), lambda i: (0, i, 0)), spec, spec, spec] + [ANY_SPEC] * len(extra),
        out_specs=[spec] * 4, out_shape=[jax.ShapeDtypeStruct((r, c), F32)] * 4,
        input_output_aliases={} if prev is None else {first + k: k for k in range(4)},
        compiler_params=_params(("parallel",), VMEM_BIG), name=name,
    )(parts, w, m, v, *extra)


def _arrived(x, name, dep=None):
    deps = [] if dep is None else [dep]

    def body(*refs):
        refs[-1][...] = jnp.zeros_like(refs[-1])

    return pl.pallas_call(
        body, in_specs=[ANY_SPEC] * (1 + len(deps)), out_specs=pl.BlockSpec(memory_space=pltpu.VMEM),
        out_shape=jax.ShapeDtypeStruct((8, 128), F32), name=name,
    )(x, *deps)


def _sum_parts(parts, name, dep=None):
    r = parts.shape[1]
    tr = 8
    for cand in (512, 256, 128, 64, 32, 16, 8):
        if r % cand == 0:
            tr = cand
            break
    deps = [] if dep is None else [dep]

    def body(p_ref, *rest):
        g = p_ref[0]
        for d in range(1, NDEV):
            g = g + p_ref[d]
        rest[-1][...] = g

    return pl.pallas_call(
        body, grid=(r // tr,), in_specs=[pl.BlockSpec((NDEV, tr, 128), lambda i: (0, i, 0))] + [ANY_SPEC] * len(deps),
        out_specs=pl.BlockSpec((tr, 128), lambda i: (i, 0)), out_shape=jax.ShapeDtypeStruct((r, 128), F32),
        compiler_params=_params(("parallel",)), name=name,
    )(parts, *deps)


def _sum_unpack(parts, rows, name, dep=None):
    deps = [] if dep is None else [dep]

    def body(p_ref, *outs):
        outs = outs[len(deps):]
        off = 0
        for o_ref, n in zip(outs, rows):
            acc = p_ref[0, off:off + n, :]
            for d in range(1, NDEV):
                acc = acc + p_ref[d, off:off + n, :]
            o_ref[...] = acc
            off += n

    return pl.pallas_call(
        body, grid=(1,), in_specs=[pl.BlockSpec(parts.shape, lambda i: (0, 0, 0))] + [ANY_SPEC] * len(deps),
        out_specs=[pl.BlockSpec((n, 128), lambda i: (0, 0)) for n in rows],
        out_shape=[jax.ShapeDtypeStruct((n, 128), F32) for n in rows],
        compiler_params=_params(("arbitrary",), VMEM_BIG), name=name,
    )(parts, *deps)


def _adam_small(ws, gs, g_specs, ms, vs, name):
    n = len(ws)

    def body(*refs):
        w_r, g_r, m_r, v_r = refs[:n], refs[n:2 * n], refs[2 * n:3 * n], refs[3 * n:4 * n]
        outs = refs[4 * n:]
        for i in range(n):
            g = g_r[i][...]
            outs[4 * i][...] = g
            outs[4 * i + 1][...], outs[4 * i + 2][...], outs[4 * i + 3][...] = _adamw(
                w_r[i][...], g, m_r[i][...], v_r[i][...])

    whole = lambda a: pl.BlockSpec(a.shape, lambda i, nd=a.ndim: (0,) * nd)
    outs = pl.pallas_call(
        body, grid=(1,),
        in_specs=[whole(a) for a in ws] + list(g_specs) + [whole(a) for a in ms] + [whole(a) for a in vs],
        out_specs=[whole(a) for a in ws for _ in range(4)],
        out_shape=[jax.ShapeDtypeStruct(a.shape, F32) for a in ws for _ in range(4)],
        compiler_params=_params(("arbitrary",), VMEM_BIG), name=name,
    )(*ws, *gs, *ms, *vs)
    return [outs[4 * i:4 * i + 4] for i in range(n)]


MASKS = [(mx, my, mc) for mx in (0, 1) for my in (0, 1) for mc in (0, 1)][1:]


def _sc_exchange(name, collective_id, arrays, scatter):
    nt = len(arrays)
    out_type = [jax.ShapeDtypeStruct(a.shape if scatter else (NDEV,) + a.shape, a.dtype) for a in arrays]

    def body(*refs):
        ins, outs = refs[:nt], refs[nt:2 * nt]
        send_sems, recv_sems, local_sems = refs[2 * nt:3 * nt], refs[3 * nt:4 * nt], refs[4 * nt:5 * nt]
        x, y, c = lax.axis_index("x"), lax.axis_index("y"), lax.axis_index("c")
        peers = [(mx + x - 2 * mx * x, my + y - 2 * my * y, mc + c - 2 * mc * c) for mx, my, mc in MASKS]
        barrier = pltpu.get_barrier_semaphore()
        for peer in peers:
            pl.semaphore_signal(barrier, inc=1, device_id=peer, device_id_type=MESH)
        pl.semaphore_wait(barrier, len(peers))
        me = 4 * x + 2 * y + c
        own = []
        for t in range(nt):
            cp = pltpu.make_async_copy(ins[t].at[me] if scatter else ins[t], outs[t].at[me], local_sems[t])
            cp.start()
            own.append(cp)
            for px, py, pc in peers:
                src = ins[t].at[4 * px + 2 * py + pc] if scatter else ins[t]
                pltpu.make_async_remote_copy(src_ref=src, dst_ref=outs[t].at[me], send_sem=send_sems[t],
                                             recv_sem=recv_sems[t], device_id=(px, py, pc), device_id_type=MESH).start()
        for t in range(nt):
            own[t].wait()
            seven = outs[t].at[pl.ds(0, NDEV - 1)]
            drain = pltpu.make_async_remote_copy(src_ref=seven, dst_ref=seven, send_sem=send_sems[t],
                                                 recv_sem=recv_sems[t], device_id=(x, y, c), device_id_type=MESH)
            drain.wait_send()
            drain.wait_recv()

    return pl.kernel(
        body, out_type=out_type, mesh=plsc.ScalarSubcoreMesh(axis_name="sequencer", num_cores=1),
        scratch_types=[pltpu.SemaphoreType.DMA] * (3 * nt),
        compiler_params=pltpu.CompilerParams(collective_id=collective_id), name=name,
    )(*arrays)


def _sc_gather_two_level(name, collective_id, arrays):
    nt = len(arrays)
    out_type = [jax.ShapeDtypeStruct((NDEV,) + a.shape, a.dtype) for a in arrays]

    def body(*refs):
        ins, outs = refs[:nt], refs[nt:2 * nt]
        sems = refs[2 * nt:]
        send_sems, sib_sems, local_sems = sems[:nt], sems[nt:2 * nt], sems[2 * nt:3 * nt]
        ici_sems = [sems[3 * nt + 3 * t:3 * nt + 3 * t + 3] for t in range(nt)]
        x, y, c = lax.axis_index("x"), lax.axis_index("y"), lax.axis_index("c")
        sibling = (x, y, 1 - c)
        chips = [(1 - x, y), (x, 1 - y), (1 - x, 1 - y)]
        barrier = pltpu.get_barrier_semaphore()
        for peer in [sibling] + [(cx, cy, c) for cx, cy in chips]:
            pl.semaphore_signal(barrier, inc=1, device_id=peer, device_id_type=MESH)
        pl.semaphore_wait(barrier, 4)
        me = 4 * x + 2 * y + c

        def push(t, src, slot, recv_sem, to):
            pltpu.make_async_remote_copy(src_ref=src, dst_ref=outs[t].at[slot], send_sem=send_sems[t],
                                         recv_sem=recv_sem, device_id=to, device_id_type=MESH).start()

        own = []
        for t in range(nt):
            cp = pltpu.make_async_copy(ins[t], outs[t].at[me], local_sems[t])
            cp.start()
            own.append(cp)
            for j, (cx, cy) in enumerate(chips):
                push(t, ins[t], me, ici_sems[t][j], (cx, cy, c))
            push(t, ins[t], me, sib_sems[t], sibling)
        for t in range(nt):
            for j, (cx, cy) in enumerate(chips):
                slot = 4 * cx + 2 * cy + c
                landed = outs[t].at[slot]
                pltpu.make_async_remote_copy(src_ref=landed, dst_ref=landed, send_sem=send_sems[t],
                                             recv_sem=ici_sems[t][j], device_id=(cx, cy, c),
                                             device_id_type=MESH).wait_recv()
                push(t, landed, slot, sib_sems[t], sibling)
        for t in range(nt):
            own[t].wait()
            four, seven = outs[t].at[pl.ds(0, 4)], outs[t].at[pl.ds(0, 7)]
            pltpu.make_async_remote_copy(src_ref=four, dst_ref=four, send_sem=send_sems[t], recv_sem=sib_sems[t],
                                         device_id=sibling, device_id_type=MESH).wait_recv()
            pltpu.make_async_remote_copy(src_ref=seven, dst_ref=seven, send_sem=send_sems[t], recv_sem=sib_sems[t],
                                         device_id=sibling, device_id_type=MESH).wait_send()

    return pl.kernel(
        body, out_type=out_type, mesh=plsc.ScalarSubcoreMesh(axis_name="sequencer", num_cores=1),
        scratch_types=[pltpu.SemaphoreType.DMA] * (6 * nt),
        compiler_params=pltpu.CompilerParams(collective_id=collective_id), name=name,
    )(*arrays)


def _sc_sibling_exchange(name, collective_id, src, out_shape, pieces):
    def body(src_ref, out_ref, send_sem, recv_sem):
        x, y, c = lax.axis_index("x"), lax.axis_index("y"), lax.axis_index("c")
        sibling = (x, y, 1 - c)
        barrier = pltpu.get_barrier_semaphore()
        pl.semaphore_signal(barrier, inc=1, device_id=sibling, device_id_type=MESH)
        pl.semaphore_wait(barrier, 1)
        for piece, lands in pieces(c, src_ref, out_ref):
            pltpu.make_async_remote_copy(src_ref=piece, dst_ref=lands, send_sem=send_sem, recv_sem=recv_sem,
                                         device_id=sibling, device_id_type=MESH).start()
        drain = pltpu.make_async_remote_copy(src_ref=out_ref, dst_ref=out_ref, send_sem=send_sem, recv_sem=recv_sem,
                                             device_id=sibling, device_id_type=MESH)
        drain.wait_send()
        drain.wait_recv()

    return pl.kernel(
        body, out_type=jax.ShapeDtypeStruct(out_shape, src.dtype),
        mesh=plsc.ScalarSubcoreMesh(axis_name="sequencer", num_cores=1), scratch_types=[pltpu.SemaphoreType.DMA] * 2,
        compiler_params=pltpu.CompilerParams(collective_id=collective_id), name=name,
    )(src)


def _swap_class_columns(name, collective_id, dz, nb, piece=0, npieces=1):
    w = nb // npieces
    return _sc_sibling_exchange(
        name, collective_id, dz, (S, 4 * w),
        lambda c, src, out: [(src.at[:, pl.ds((2 * j + 1 - c) * nb + piece * w, w)], out.at[:, pl.ds(j * w, w)])
                             for j in range(4)])


def _sc_chip_scatter(name, collective_id, q):
    def body(q_ref, out_ref, send_sem, recv_sem, local_sem):
        x, y, c = lax.axis_index("x"), lax.axis_index("y"), lax.axis_index("c")
        chips = [(1 - x, y), (x, 1 - y), (1 - x, 1 - y)]
        barrier = pltpu.get_barrier_semaphore()
        for cx, cy in chips:
            pl.semaphore_signal(barrier, inc=1, device_id=(cx, cy, c), device_id_type=MESH)
        pl.semaphore_wait(barrier, 3)
        mine = 2 * x + y
        own = pltpu.make_async_copy(q_ref.at[mine], out_ref.at[mine], local_sem)
        own.start()
        for cx, cy in chips:
            pltpu.make_async_remote_copy(src_ref=q_ref.at[2 * cx + cy], dst_ref=out_ref.at[mine], send_sem=send_sem,
                                         recv_sem=recv_sem, device_id=(cx, cy, c), device_id_type=MESH).start()
        own.wait()
        three = out_ref.at[pl.ds(0, 3)]
        drain = pltpu.make_async_remote_copy(src_ref=three, dst_ref=three, send_sem=send_sem, recv_sem=recv_sem,
                                             device_id=(x, y, c), device_id_type=MESH)
        drain.wait_send()
        drain.wait_recv()

    return pl.kernel(
        body, out_type=jax.ShapeDtypeStruct(q.shape, q.dtype),
        mesh=plsc.ScalarSubcoreMesh(axis_name="sequencer", num_cores=1), scratch_types=[pltpu.SemaphoreType.DMA] * 3,
        compiler_params=pltpu.CompilerParams(collective_id=collective_id), name=name,
    )(q)


def _mm_pair_dw(h_own, dz, h_sib, dz_sib, nb, name, dep=None, piece=0, npieces=1, h_transposed=False,
                one_call=False):
    nb = nb // npieces
    tn = 512 if nb % 512 == 0 else nb
    per = nb // tn
    dn = NN if h_transposed else TN
    o_spec = pl.BlockSpec((None, D, tn), lambda i, j, k: (j // per, 0, j % per))
    own_col = lambda i, j, k: (0, ((2 * (j // per) + lax.axis_index("c")) * npieces + piece) * per + j % per)
    if one_call:
        def fused(a0_ref, b0_ref, a1_ref, b1_ref, dep_ref, o_ref):
            acc = _dot(a0_ref[...], b0_ref[...], dn) + _dot(a1_ref[...], b1_ref[...], dn)
            o_ref[...] = acc.astype(BF16)

        whole = pl.BlockSpec((S, D), lambda i, j, k: (0, 0), pipeline_mode=pl.Buffered(1))
        return pl.pallas_call(
            fused, grid=(1, 4 * per, 1),
            in_specs=[whole, pl.BlockSpec((S, tn), own_col), whole, pl.BlockSpec((S, tn), lambda i, j, k: (0, j)),
                      ANY_SPEC],
            out_specs=o_spec, out_shape=jax.ShapeDtypeStruct((4, D, nb), BF16),
            compiler_params=_params(("parallel", "parallel", "arbitrary"), VMEM_BIG), name=name,
        )(h_own, dz, h_sib, dz_sib, dep)
    part = _matmul(
        h_own, dz, dn=dn, grid=(1, 4 * per, 1),
        a_spec=pl.BlockSpec((S, D), lambda i, j, k: (0, 0)), b_spec=pl.BlockSpec((S, tn), own_col),
        o_spec=o_spec, out_shape=(4, D, nb), out_dtype=F32, acc_shape=(D, tn), name=name + "_own", dep=dep)

    def body(a_ref, b_ref, p_ref, o_ref):
        o_ref[...] = (p_ref[...] + _dot(a_ref[...], b_ref[...], dn)).astype(BF16)

    return pl.pallas_call(
        body, grid=(1, 4 * per, 1),
        in_specs=[pl.BlockSpec((S, D), lambda i, j, k: (0, 0)), pl.BlockSpec((S, tn), lambda i, j, k: (0, j)), o_spec],
        out_specs=o_spec, out_shape=jax.ShapeDtypeStruct((4, D, nb), BF16),
        compiler_params=_params(("parallel", "parallel", "arbitrary"), VMEM_BIG), name=name + "_sibling",
    )(h_sib, dz_sib, part)


SMALL = {
    "e_pre_norm": ((2048,), None), "e_pool_w": ((4, 256, 256), 1), "e_pool_scale": ((1024,), None),
    "e_post_norm": ((2048,), None), "o_pre_norm": ((2048,), 0), "o_sgu_norm_g": ((1024,), 0),
    "o_sgu_norm_b": ((1024,), 0), "o_sgu_w": ((4, 128, 128), None), "o_sgu_b": ((4, 128), None),
    "o_conv_w": ((31, 1024), 1), "o_conv_b": ((1024,), 0), "o_conv_norm_g": ((1024,), 0),
    "o_conv_norm_b": ((1024,), 0), "o_post_norm": ((2048,), 0),
}
SMALL_SHARDED = [n for n, (_, ax) in SMALL.items() if ax is not None]


def _shard_shape(name):
    shape, ax = SMALL[name]
    if ax is None:
        return shape
    return tuple(s // NDEV if i == ax else s for i, s in enumerate(shape))


def _pack(arrs, row_multiple=1):
    flat = jnp.concatenate([a.reshape(-1) for a in arrs])
    pad = -flat.shape[0] % (128 * row_multiple)
    return jnp.concatenate([flat, jnp.zeros((pad,), F32)]).reshape(-1, 128)


def _small_views(name):
    shape, ax = SMALL[name]
    me = lambda: 4 * lax.axis_index("x") + 2 * lax.axis_index("y") + lax.axis_index("c")
    if ax is None:
        view = (int(np.prod(shape)) // 128, 128)
        return view, view, pl.BlockSpec(view, lambda i: (0, 0))
    if len(shape) == 1:
        n = shape[0] // NDEV
        return (1, n), (NDEV, 1, n), pl.BlockSpec((None, 1, n), lambda i: (me(), 0, 0))
    part = _shard_shape(name)
    return part, shape, pl.BlockSpec(part, lambda i: tuple(me() if d == ax else 0 for d in range(len(shape))))


WEIGHTS = ["e_pre_norm", "e_w_in", "e_pool_w", "e_pool_scale", "e_w_out", "e_post_norm", "o_pre_norm", "o_w_in",
           "o_sgu_norm_g", "o_sgu_norm_b", "o_sgu_w", "o_sgu_b", "o_conv_w", "o_conv_b", "o_conv_norm_g",
           "o_conv_norm_b", "o_w_out", "o_post_norm"]


def kernel(x, e_pre_norm, e_w_in, e_pool_w, e_pool_scale, e_w_out, e_post_norm, o_pre_norm, o_w_in, o_sgu_norm_g, o_sgu_norm_b, o_sgu_w, o_sgu_b, o_conv_w, o_conv_b, o_conv_norm_g, o_conv_norm_b, o_w_out, o_post_norm, loss_target, m_e_pre_norm, m_e_w_in, m_e_pool_w, m_e_pool_scale, m_e_w_out, m_e_post_norm, m_o_pre_norm, m_o_w_in, m_o_sgu_norm_g, m_o_sgu_norm_b, m_o_sgu_w, m_o_sgu_b, m_o_conv_w, m_o_conv_b, m_o_conv_norm_g, m_o_conv_norm_b, m_o_w_out, m_o_post_norm, v_e_pre_norm, v_e_w_in, v_e_pool_w, v_e_pool_scale, v_e_w_out, v_e_post_norm, v_o_pre_norm, v_o_w_in, v_o_sgu_norm_g, v_o_sgu_norm_b, v_o_sgu_w, v_o_sgu_b, v_o_conv_w, v_o_conv_b, v_o_conv_norm_g, v_o_conv_norm_b, v_o_w_out, v_o_post_norm):
    given = dict(locals())
    w = {n: given[n][0] for n in WEIGHTS}
    m = {n: given["m_" + n][0] for n in WEIGHTS}
    v = {n: given["v_" + n][0] for n in WEIGHTS}
    me = 4 * lax.axis_index("x") + 2 * lax.axis_index("y") + lax.axis_index("c")
    x, target = x[0], loss_target[0]
    row = lambda a: a.reshape(1, -1)

    lo, small_rows = _sc_gather_two_level(
        "gather_a0", 0, [_cast_bf16(w["e_w_in"], "cast_e_w_in_0", 0, 2), _pack([w[n] for n in SMALL_SHARDED])])
    hi, = _sc_gather_two_level("gather_a1", 12, [_cast_bf16(w["e_w_in"], "cast_e_w_in_1", 1, 2)])
    wg_e_in = (lo, hi)
    h0, h0t = _pre0_fwd(x, row(w["e_pre_norm"]))
    wg_e_out, = _sc_gather_two_level("gather_b", 1, [_cast_bf16(w["e_w_out"], "cast_e_w_out")])
    wg_o_in, wg_o_out = _sc_gather_two_level(
        "gather_c", 13, [_cast_bf16(w[n], "cast_" + n) for n in ("o_w_in", "o_w_out")])
    h0t_sib = _sc_sibling_exchange("swap_h0", 8, h0t, h0t.shape, lambda c, src, out: [(src, out)])
    p = {n: w[n] for n in SMALL if SMALL[n][1] is None}
    small_rows = small_rows.reshape(NDEV, -1)
    off = 0
    for n in SMALL_SHARDED:
        shp, ax = _shard_shape(n), SMALL[n][1]
        cnt = int(np.prod(shp))
        blk = small_rows[:, off:off + cnt].reshape((NDEV,) + shp)
        p[n] = jnp.moveaxis(blk, 0, ax).reshape(SMALL[n][0])
        off += cnt
    tabs = _rope_tables()
    pool_w_bf = p["e_pool_w"].astype(BF16)
    sgu_bb = jnp.broadcast_to(p["o_sgu_b"][:, :, None], (4, 128, 128))
    conv_w = jnp.concatenate([p["o_conv_w"], jnp.zeros((HALO - CONV_K, HALF), F32)], axis=0)
    odd_p = (row(p["o_sgu_norm_g"]), row(p["o_sgu_norm_b"]), p["o_sgu_w"], sgu_bb, conv_w,
             row(p["o_conv_b"]), row(p["o_conv_norm_g"]), row(p["o_conv_norm_b"]))

    z0 = _mm_in_halves(h0, wg_e_in, "mm_z0")
    ycat0 = _pool_fwd(z0, pool_w_bf, row(p["e_pool_scale"]))
    qkv = _qkv_prep(z0, tabs)
    ycat0, og, lg = _attn_fwd(z0, qkv, ycat0)
    w_out_e, w_out_o = wg_e_out.reshape(2048, D), wg_o_out.reshape(2048, D)
    y0, x1, h1 = _post0_fwd(ycat0, w_out_e, x, row(p["e_post_norm"]), row(p["o_pre_norm"]), h0t_sib)
    h1_sib = _sc_sibling_exchange("swap_h1", 11, h1, h1.shape, lambda c, src, out: [(src, out)])
    z1 = _mm_in(h1, wg_o_in, "mm_z1")
    ycat1, conv_out = _odd_fwd(z1, *odd_p)

    g = {}
    loss_part, dx2, dy1, g["o_post_norm"] = _post1_bwd(ycat1, w_out_o, x1, target, row(p["o_post_norm"]), h1_sib)
    parts = {}
    dw = _mm_out_dw(ycat1, dy1, "mm_dwout1").reshape(NDEV, 256, D)
    parts["o_w_out"], = _sc_exchange("scatter_o_w_out", 2, [dw], True)
    dycat1 = _mm_out_dx(dy1, w_out_o, "mm_dycat1", dw)
    dz1, ddc, g["o_sgu_w"], d_sgu_bb, g["o_sgu_norm_g"], g["o_sgu_norm_b"], g["o_conv_norm_g"], \
        g["o_conv_norm_b"], g["o_conv_b"] = _odd_bwd_a(z1, conv_out, dycat1, *odd_p[:4], *odd_p[6:])
    dz1, d_conv_w = _odd_bwd_b(z1, ddc, dz1, conv_w)
    g["o_sgu_b"] = d_sgu_bb[:, :, 0]
    g["o_conv_w"] = d_conv_w[:CONV_K]
    grads, deltas, new_m, new_v = {}, {}, {}, {}

    def adam(n, dep):
        grads[n], deltas[n], new_m[n], new_v[n] = _adam_reduce(parts[n], w[n], m[n], v[n], "adam_" + n, dep)
        return new_v[n]

    pin = _arrived(parts["o_w_out"], "arrived_o_w_out", d_conv_w)
    dz1_sib = _swap_class_columns("swap_dz1", 10, dz1, ODD_IN // NDEV)
    dw = _mm_pair_dw(h1, dz1, h1_sib, dz1_sib, ODD_IN // NDEV, "mm_dwin1", pin)
    parts["o_w_in"] = _sc_chip_scatter("scatter_o_w_in", 3, dw)
    dh1 = _mm_in_dx(dz1, wg_o_in, "mm_dh1", dw)
    dx1, dy0, g["o_pre_norm"], g["e_post_norm"] = _mid_bwd(dx2, dh1, x1, y0, row(p["o_pre_norm"]),
                                                           row(p["e_post_norm"]))
    dw = _mm_out_dw(ycat0, dy0, "mm_dwout0").reshape(NDEV, 256, D)
    parts["e_w_out"], = _sc_exchange("scatter_e_w_out", 4, [dw], True)
    dycat0 = _mm_out_dx(dy0, w_out_e, "mm_dycat0", dw)
    da_in, da_gate, g["e_pool_w"], g["e_pool_scale"] = _pool_bwd(z0, dycat0, pool_w_bf, row(p["e_pool_scale"]))
    late = [n for n in SMALL if n not in ("e_pre_norm", "o_sgu_b")] + ["o_sgu_b"]
    pieces = [g[n].reshape(SMALL[n][0]) for n in late[:-1]] + [jnp.broadcast_to(loss_part, (8, 128)), g[late[-1]]]
    recv_small, = _sc_gather_two_level("gather_small_grads", 6, [_pack(pieces, 512)])
    took = _arrived(parts["o_w_in"], "arrived_o_w_in")
    dq, dk, dv, dbg = _attn_bwd(z0, qkv, og, lg, dycat0, tabs, took)
    dz0 = jnp.concatenate([da_in, da_gate, dq, dk, dv, dbg], axis=1)
    took = _arrived(recv_small, "arrived_small_grads", _arrived(parts["e_w_out"], "arrived_e_w_out", dz0))
    nb = EVEN_IN // NDEV
    swapped = [_swap_class_columns("swap_dz0_%d" % half, (9, 14)[half], dz0, nb, half, 2) for half in (0, 1)]
    dw, e_w_in_parts = took, []
    for half in (0, 1):
        dw = _mm_pair_dw(h0t, dz0, h0t_sib, swapped[half], nb, "mm_dwin0_%d" % half, dw, half, 2, True, half == 1)
        e_w_in_parts.append(_sc_chip_scatter("scatter_e_w_in_%d" % half, (5, 15)[half], dw))
    pin = adam("e_w_out", adam("o_w_out", adam("o_w_in", dw)))
    rows = [int(np.prod(SMALL[n][0])) // 128 for n in late]
    sums = _sum_unpack(recv_small, rows[:-1] + [8, rows[-1]], "sum_small_grads", pin)
    summed = dict(zip(late, sums[:-2] + sums[-1:]))
    loss = sums[-2][0, 0]
    dh0 = _mm_in_dx_halves(dz0, wg_e_in, "mm_dh0", summed[late[0]])
    grad_x, g["e_pre_norm"] = _pre0_bwd(dx1, dh0, x, row(p["e_pre_norm"]))
    last, = _sc_exchange("gather_e_pre_norm_grad", 7, [g["e_pre_norm"].reshape(16, 128)], False)

    n = "e_w_in"
    out = _adam_reduce(e_w_in_parts[0], w[n], m[n], v[n], "adam_e_w_in_0", grad_x, 0, 2)
    out = _adam_reduce(e_w_in_parts[1], w[n], m[n], v[n], "adam_e_w_in_1", None, 1, 2, out)
    grads[n], deltas[n], new_m[n], new_v[n] = out
    summed["e_pre_norm"] = _sum_parts(last, "sum_e_pre_norm_grad", out[3])
    names = list(SMALL)
    views = [_small_views(n) for n in names]
    mine = lambda src: [src[n].reshape(vw[0]) for n, vw in zip(names, views)]
    res = _adam_small(mine(w), [summed[n].reshape(vw[1]) for n, vw in zip(names, views)], [vw[2] for vw in views],
                      mine(m), mine(v), "adam_small")
    for n, out in zip(names, res):
        grads[n], deltas[n], new_m[n], new_v[n] = [t.reshape(_shard_shape(n)) for t in out]

    lead = lambda a: a[None]
    return (loss, grad_x[None], *[lead(grads[n]) for n in WEIGHTS], *[lead(deltas[n]) for n in WEIGHTS],
            *[lead(new_m[n]) for n in WEIGHTS], *[lead(new_v[n]) for n in WEIGHTS])
```

```python
import numpy as np
import jax
import jax.numpy as jnp
from jax import lax
from jax.experimental import pallas as pl
from jax.experimental.pallas import tpu as pltpu
from jax.experimental.pallas import tpu_sc as plsc

F32 = jnp.float32
BF16 = jnp.bfloat16

S = 2048
D = 2048
NDEV = 8
EPS = 1e-6
NEG = -1e30
HEAD_DIM = 128
ROT_DIM = 32
ROPE_THETA = 500000.0
PATTERNS = ((128, 1), (512, 4), (2048, 16))
BLK = 128
EVEN_IN = 12288
ODD_IN = 6144
HALF = 1024
CONV_K = 31
HALO = 32
TR = 256
SUB = 16

ADAM_LR = 0.001
ADAM_B1 = 0.9
ADAM_B2 = 0.999
ADAM_EPS = 1e-08
ADAM_WD = 0.01
ADAM_STEP = 10

VMEM_BIG = 56 * 1024 * 1024
MESH = pl.DeviceIdType.MESH

NN = (((1,), (0,)), ((), ()))
NT = (((1,), (1,)), ((), ()))
TN = (((0,), (0,)), ((), ()))


def _dot(a, b, dn=NN):
    return lax.dot_general(a, b, dn, preferred_element_type=F32)


def _sigmoid(x):
    return 1.0 / (1.0 + jnp.exp(-x))


def _silu_and_grad(x):
    sg = _sigmoid(x)
    return x * sg, sg * (1.0 + x * (1.0 - sg))


def _params(sem, vmem=None):
    return pltpu.CompilerParams(dimension_semantics=sem, vmem_limit_bytes=vmem)


ANY_SPEC = pl.BlockSpec(memory_space=pl.ANY)


def _matmul(a, b, *, dn, grid, a_spec, b_spec, o_spec, out_shape, out_dtype, acc_shape, name, dep=None):
    nk = grid[2]
    deps = [] if dep is None else list(dep) if isinstance(dep, (tuple, list)) else [dep]

    def body(a_ref, b_ref, *rest):
        o_ref, acc = rest[len(deps)], rest[len(deps) + 1:]
        if nk == 1:
            o_ref[...] = _dot(a_ref[...], b_ref[...], dn).astype(o_ref.dtype)
            return
        acc_ref = acc[0]
        k = pl.program_id(2)

        @pl.when(k == 0)
        def _():
            acc_ref[...] = jnp.zeros_like(acc_ref)

        acc_ref[...] += _dot(a_ref[...], b_ref[...], dn)

        @pl.when(k == nk - 1)
        def _():
            o_ref[...] = acc_ref[...].astype(o_ref.dtype)

    return pl.pallas_call(
        body, grid=grid, in_specs=[a_spec, b_spec] + [ANY_SPEC] * len(deps), out_specs=o_spec,
        out_shape=jax.ShapeDtypeStruct(out_shape, out_dtype),
        scratch_shapes=[] if nk == 1 else [pltpu.VMEM(acc_shape, F32)],
        compiler_params=_params(("parallel", "parallel", "arbitrary"), VMEM_BIG), name=name,
    )(a, b, *deps)


TM = 2048


def _mm_in(h, wg, name):
    nb = wg.shape[2]
    tn = 512 if nb % 512 == 0 else nb
    per = nb // tn
    return _matmul(
        h, wg, dn=NN, grid=(S // TM, NDEV * per, 1),
        a_spec=pl.BlockSpec((TM, D), lambda i, j, k: (i, 0)),
        b_spec=pl.BlockSpec((None, D, tn), lambda i, j, k: (j // per, 0, j % per)),
        o_spec=pl.BlockSpec((TM, tn), lambda i, j, k: (i, j)),
        out_shape=(S, NDEV * nb), out_dtype=F32, acc_shape=(TM, tn), name=name)


def _mm_in_halves(h, wg_halves, name):
    hb = wg_halves[0].shape[2]
    z = None
    for half, wg in enumerate(wg_halves):
        prev = [] if z is None else [z]

        def body(a_ref, b_ref, *rest):
            rest[-1][...] = _dot(a_ref[...], b_ref[...])

        z = pl.pallas_call(
            body, grid=(NDEV,),
            in_specs=[pl.BlockSpec((S, D), lambda j: (0, 0)), pl.BlockSpec((None, D, hb), lambda j: (j, 0, 0))]
                     + [ANY_SPEC] * len(prev),
            out_specs=pl.BlockSpec((S, hb), lambda j, half=half: (0, 2 * j + half)),
            out_shape=jax.ShapeDtypeStruct((S, 2 * NDEV * hb), F32),
            input_output_aliases={2: 0} if prev else {},
            compiler_params=_params(("parallel",), VMEM_BIG), name="%s_%d" % (name, half),
        )(h, wg, *prev)
    return z


def _mm_in_dx_halves(dz, wg_halves, name, dep):
    hb = wg_halves[0].shape[2]
    nk = 2 * NDEV

    def body(a_ref, b0_ref, b1_ref, dep_ref, o_ref, acc_ref):
        k = pl.program_id(2)

        @pl.when(k == 0)
        def _():
            acc_ref[...] = jnp.zeros_like(acc_ref)

        @pl.when(k % 2 == 0)
        def _():
            acc_ref[...] += _dot(a_ref[...], b0_ref[...], NT)

        @pl.when(k % 2 == 1)
        def _():
            acc_ref[...] += _dot(a_ref[...], b1_ref[...], NT)

        @pl.when(k == nk - 1)
        def _():
            o_ref[...] = acc_ref[...]

    b_spec = pl.BlockSpec((None, 1024, hb), lambda i, j, k: (k // 2, j, 0))
    return pl.pallas_call(
        body, grid=(1, D // 1024, nk),
        in_specs=[pl.BlockSpec((S, hb), lambda i, j, k: (0, k)), b_spec, b_spec, ANY_SPEC],
        out_specs=pl.BlockSpec((S, 1024), lambda i, j, k: (0, j)), out_shape=jax.ShapeDtypeStruct((S, D), F32),
        scratch_shapes=[pltpu.VMEM((S, 1024), F32)],
        compiler_params=_params(("parallel", "parallel", "arbitrary"), VMEM_BIG), name=name,
    )(dz, *wg_halves, dep)


def _mm_in_dx(dz, wg, name, dep=None):
    nb = wg.shape[2]
    return _matmul(
        dz, wg, dn=NT, grid=(S // TM, D // 1024, NDEV),
        a_spec=pl.BlockSpec((TM, nb), lambda i, j, k: (i, k)),
        b_spec=pl.BlockSpec((None, 1024, nb), lambda i, j, k: (k, j, 0)),
        o_spec=pl.BlockSpec((TM, 1024), lambda i, j, k: (i, j)),
        out_shape=(S, D), out_dtype=F32, acc_shape=(TM, 1024), name=name, dep=dep)


def _mm_out_dx(dy, w, name, dep=None):
    return _matmul(
        dy, w, dn=NT, grid=(S // TM, 2048 // 512, 1),
        a_spec=pl.BlockSpec((TM, D), lambda i, j, k: (i, 0)),
        b_spec=pl.BlockSpec((512, D), lambda i, j, k: (j, 0)),
        o_spec=pl.BlockSpec((TM, 512), lambda i, j, k: (i, j)),
        out_shape=(S, 2048), out_dtype=F32, acc_shape=(TM, 512), name=name, dep=dep)


def _mm_out_dw(yc, dy, name):
    return _matmul(
        yc, dy, dn=TN, grid=(2048 // TM, D // 512, 1),
        a_spec=pl.BlockSpec((S, TM), lambda i, j, k: (0, i)),
        b_spec=pl.BlockSpec((S, 512), lambda i, j, k: (0, j)),
        o_spec=pl.BlockSpec((TM, 512), lambda i, j, k: (i, j)),
        out_shape=(2048, D), out_dtype=BF16, acc_shape=(TM, 512), name=name)


def _row_spec(w=D):
    return pl.BlockSpec((TR, w), lambda i: (i, 0))


def _vec_spec(w=D):
    return pl.BlockSpec((1, w), lambda i: (0, 0))


def _rms_stats(x):
    r = lax.rsqrt(jnp.mean(x * x, axis=-1, keepdims=True) + EPS)
    return x * r, r


def _rms_bwd(dn, xhat, r, g):
    dxh = dn * g
    return r * (dxh - xhat * jnp.mean(dxh * xhat, axis=-1, keepdims=True))


def _acc_rows(ref, val, i):
    s = jnp.sum(val, axis=0, keepdims=True)

    @pl.when(i == 0)
    def _():
        ref[...] = s

    @pl.when(i > 0)
    def _():
        ref[...] += s


def _pre0_fwd(x, g):
    def body(x_ref, g_ref, h_ref, ht_ref):
        xhat, _ = _rms_stats(x_ref[...])
        h = xhat * g_ref[...]
        h_ref[...] = h.astype(BF16)
        ht_ref[...] = h.T.astype(BF16)

    return pl.pallas_call(
        body, grid=(S // TR,), in_specs=[_row_spec(), _vec_spec()],
        out_specs=[_row_spec(), pl.BlockSpec((D, TR), lambda i: (0, i))],
        out_shape=[jax.ShapeDtypeStruct((S, D), BF16), jax.ShapeDtypeStruct((D, S), BF16)],
        compiler_params=_params(("parallel",)), name="pre0_fwd",
    )(x, g)


def _post0_fwd(ycat, w_out, x, g_post, g_pre1, dep):
    def body(yc_ref, w_ref, x_ref, gp_ref, g1_ref, dep_ref, y_ref, x1_ref, h1_ref):
        y = _dot(yc_ref[...], w_ref[...])
        y_ref[...] = y
        yhat, _ = _rms_stats(y)
        x1 = x_ref[...] + yhat * gp_ref[...]
        x1_ref[...] = x1
        xhat, _ = _rms_stats(x1)
        h1_ref[...] = (xhat * g1_ref[...]).astype(BF16)

    return pl.pallas_call(
        body, grid=(S // TR,),
        in_specs=[_row_spec(), pl.BlockSpec((2048, D), lambda i: (0, 0)), _row_spec(), _vec_spec(), _vec_spec(),
                  ANY_SPEC],
        out_specs=[_row_spec(), _row_spec(), _row_spec()],
        out_shape=[jax.ShapeDtypeStruct((S, D), F32), jax.ShapeDtypeStruct((S, D), F32),
                   jax.ShapeDtypeStruct((S, D), BF16)],
        compiler_params=_params(("parallel",), VMEM_BIG), name="post0_fwd",
    )(ycat, w_out, x, g_post, g_pre1, dep)


def _post1_bwd(ycat, w_out, x1, target, g_post, dep):
    def body(yc_ref, w_ref, x1_ref, t_ref, g_ref, dep_ref, loss_ref, dx2_ref, dy_ref, dg_ref):
        i = pl.program_id(0)
        yhat, r = _rms_stats(_dot(yc_ref[...], w_ref[...]))
        g = g_ref[...]
        err = x1_ref[...] + yhat * g - t_ref[...]
        part = jnp.sum(jnp.sum(err * err, axis=-1, keepdims=True), axis=0, keepdims=True) * (0.5 / D)
        _acc_rows(loss_ref, jnp.broadcast_to(part, (1, 128)), i)
        dx2 = err * (1.0 / D)
        dx2_ref[...] = dx2
        _acc_rows(dg_ref, dx2 * yhat, i)
        dy_ref[...] = _rms_bwd(dx2, yhat, r, g).astype(BF16)

    return pl.pallas_call(
        body, grid=(S // TR,),
        in_specs=[_row_spec(), pl.BlockSpec((2048, D), lambda i: (0, 0)), _row_spec(), _row_spec(), _vec_spec(),
                  ANY_SPEC],
        out_specs=[_vec_spec(128), _row_spec(), _row_spec(), _vec_spec()],
        out_shape=[jax.ShapeDtypeStruct((1, 128), F32), jax.ShapeDtypeStruct((S, D), F32),
                   jax.ShapeDtypeStruct((S, D), BF16), jax.ShapeDtypeStruct((1, D), F32)],
        compiler_params=_params(("arbitrary",), VMEM_BIG), name="post1_bwd",
    )(ycat, w_out, x1, target, g_post, dep)


def _mid_bwd(dx2, dh1, x1, y0, g_pre1, g_post0):
    def body(dx2_ref, dh_ref, x1_ref, y_ref, g1_ref, gp_ref, dx1_ref, dy_ref, dg1_ref, dgp_ref):
        i = pl.program_id(0)
        xhat, r1 = _rms_stats(x1_ref[...])
        dh = dh_ref[...]
        _acc_rows(dg1_ref, dh * xhat, i)
        dx1 = dx2_ref[...] + _rms_bwd(dh, xhat, r1, g1_ref[...])
        dx1_ref[...] = dx1
        yhat, r0 = _rms_stats(y_ref[...])
        _acc_rows(dgp_ref, dx1 * yhat, i)
        dy_ref[...] = _rms_bwd(dx1, yhat, r0, gp_ref[...]).astype(BF16)

    return pl.pallas_call(
        body, grid=(S // TR,),
        in_specs=[_row_spec(), _row_spec(), _row_spec(), _row_spec(), _vec_spec(), _vec_spec()],
        out_specs=[_row_spec(), _row_spec(), _vec_spec(), _vec_spec()],
        out_shape=[jax.ShapeDtypeStruct((S, D), F32), jax.ShapeDtypeStruct((S, D), BF16),
                   jax.ShapeDtypeStruct((1, D), F32), jax.ShapeDtypeStruct((1, D), F32)],
        compiler_params=_params(("arbitrary",)), name="mid_bwd",
    )(dx2, dh1, x1, y0, g_pre1, g_post0)


def _pre0_bwd(dx1, dh0, x, g):
    def body(dx1_ref, dh_ref, x_ref, g_ref, gx_ref, dg_ref):
        i = pl.program_id(0)
        xhat, r = _rms_stats(x_ref[...])
        dh = dh_ref[...]
        _acc_rows(dg_ref, dh * xhat, i)
        gx_ref[...] = dx1_ref[...] + _rms_bwd(dh, xhat, r, g_ref[...])

    return pl.pallas_call(
        body, grid=(S // TR,), in_specs=[_row_spec(), _row_spec(), _row_spec(), _vec_spec()],
        out_specs=[_row_spec(), _vec_spec()],
        out_shape=[jax.ShapeDtypeStruct((S, D), F32), jax.ShapeDtypeStruct((1, D), F32)],
        compiler_params=_params(("arbitrary",)), name="pre0_bwd",
    )(dx1, dh0, x, g)


POOL_CH = 256


def _pool_apply(a, w, transpose):
    n = a.shape[0]
    row = lax.broadcasted_iota(jnp.int32, a.shape, 0)
    cnt = jnp.minimum(row + 1, w).astype(F32)
    s = a / cnt if transpose else a
    for k in (1, 2, 4, 8):
        if transpose:
            sh = jnp.where(row < n - k, pltpu.roll(s, n - k, 0), 0.0)
        else:
            sh = jnp.where(row >= k, pltpu.roll(s, k, 0), 0.0)
        s = jnp.where(w > k, s + sh, s)
    return s - a if transpose else s / cnt - a


def _pool_fwd(z0, pool_w, pool_scale):
    def body(a_ref, gate_ref, w_ref, sc_ref, out_ref):
        win = jnp.left_shift(2, pl.program_id(0))
        pooled = _pool_apply(a_ref[...], win, False)
        mixed = _dot(pooled.astype(BF16), w_ref[...])
        gate = gate_ref[...]
        out_ref[...] = (mixed * sc_ref[...] * (gate * _sigmoid(gate))).astype(BF16)

    return pl.pallas_call(
        body, grid=(4,),
        in_specs=[pl.BlockSpec((S, POOL_CH), lambda g: (0, g)), pl.BlockSpec((S, POOL_CH), lambda g: (0, 4 + g)),
                  pl.BlockSpec((None, POOL_CH, POOL_CH), lambda g: (g, 0, 0)),
                  pl.BlockSpec((1, POOL_CH), lambda g: (0, g))],
        out_specs=pl.BlockSpec((S, POOL_CH), lambda g: (0, g)),
        out_shape=jax.ShapeDtypeStruct((S, 2048), BF16),
        compiler_params=_params(("parallel",), VMEM_BIG), name="pool_fwd",
    )(z0, z0, pool_w, pool_scale)


def _pool_bwd(z0, dycat, pool_w, pool_scale):
    def body(a_ref, gate_ref, dy_ref, w_ref, sc_ref, da_ref, dgate_ref, dw_ref, dsc_ref):
        win = jnp.left_shift(2, pl.program_id(0))
        pooled = _pool_apply(a_ref[...], win, False).astype(BF16)
        w = w_ref[...]
        mixed = _dot(pooled, w)
        silu, dsilu = _silu_and_grad(gate_ref[...])
        dy = dy_ref[...]
        sc = sc_ref[...]
        dgate_ref[...] = (dy * (mixed * sc) * dsilu).astype(BF16)
        dms = dy * silu
        dsc_ref[...] = jnp.sum(dms * mixed, axis=0, keepdims=True)
        dmixed = (dms * sc).astype(BF16)
        dw_ref[...] = _dot(pooled, dmixed, TN)
        dpooled = _dot(dmixed, w, NT)
        da_ref[...] = _pool_apply(dpooled, win, True).astype(BF16)

    slab = lambda off: pl.BlockSpec((S, POOL_CH), lambda g: (0, off + g))
    return pl.pallas_call(
        body, grid=(4,),
        in_specs=[slab(0), slab(4), slab(0), pl.BlockSpec((None, POOL_CH, POOL_CH), lambda g: (g, 0, 0)),
                  pl.BlockSpec((1, POOL_CH), lambda g: (0, g))],
        out_specs=[slab(0), slab(0), pl.BlockSpec((None, POOL_CH, POOL_CH), lambda g: (g, 0, 0)),
                   pl.BlockSpec((1, POOL_CH), lambda g: (0, g))],
        out_shape=[jax.ShapeDtypeStruct((S, HALF), BF16), jax.ShapeDtypeStruct((S, HALF), BF16),
                   jax.ShapeDtypeStruct((4, POOL_CH, POOL_CH), F32), jax.ShapeDtypeStruct((1, HALF), F32)],
        compiler_params=_params(("parallel",), VMEM_BIG), name="pool_bwd",
    )(z0, z0, dycat, pool_w, pool_scale)


Q_COL, K_COL, V_COL, BG_COL = 2048 // 128, 5120 // 128, 8192 // 128, 11264 // 128
SCALE = HEAD_DIM ** -0.5


def _rope_tables():
    pos = jnp.arange(S, dtype=F32)
    inv_freq = jnp.power(ROPE_THETA, -jnp.arange(0, ROT_DIM, 2, dtype=F32) / ROT_DIM)
    ang = pos[:, None] * inv_freq[None, :]
    cos, sin = jnp.cos(ang), jnp.sin(ang)
    half = ROT_DIM // 2
    zeros = jnp.zeros((S, HEAD_DIM - ROT_DIM), F32)
    c = jnp.concatenate([cos, cos, jnp.ones((S, HEAD_DIM - ROT_DIM), F32)], axis=1)
    a = jnp.concatenate([-sin, jnp.zeros((S, half), F32), zeros], axis=1)
    b = jnp.concatenate([jnp.zeros((S, half), F32), sin, zeros], axis=1)
    return c, a, b


def _rope(t, c, a, b):
    half = ROT_DIM // 2
    return t * c + pltpu.roll(t, HEAD_DIM - half, 1) * a + pltpu.roll(t, half, 1) * b


def _rope_t(d, c, a, b):
    half = ROT_DIM // 2
    return d * c + pltpu.roll(d * a, half, 1) + pltpu.roll(d * b, HEAD_DIM - half, 1)


def _deinterleave(dst, src, dil, cast=None, dst_off=0):
    length = S // dil
    for r in range(dil):
        v = src[...] if dil == 1 else src[pl.ds(r, length, stride=dil), :]
        dst[dst_off + r * length:dst_off + (r + 1) * length, :] = v if cast is None else v.astype(cast)


def _interleave(dst, src, dil, src_off=0):
    length = S // dil
    for r in range(dil):
        if dil == 1:
            dst[...] = src[src_off:src_off + S, :]
        else:
            dst[pl.ds(r, length, stride=dil), :] = src[src_off + r * length:src_off + (r + 1) * length, :]


CU = 8
NUNITS = S // BLK
B_QK = (((2,), (2,)), ((0,), (0,)))
B_PV = (((2,), (1,)), ((0,), (0,)))
B_TN = (((1,), (1,)), ((0,), (0,)))


def _blocks(ref, first):
    return ref[first * BLK:(first + CU) * BLK, :].reshape(CU, BLK, HEAD_DIM)


def _chunk_scores(u0, nb, qd, kdp):
    q = _blocks(qd, u0)
    row = lax.broadcasted_iota(jnp.int32, (CU, BLK, BLK), 1)
    col = lax.broadcasted_iota(jnp.int32, (CU, BLK, BLK), 2)
    s_own = jnp.where(col <= row, _dot(q, _blocks(kdp, u0 + 1), B_QK) * SCALE, NEG)
    if nb == 1:
        return q, s_own, None
    unit = lax.broadcasted_iota(jnp.int32, (CU, BLK, BLK), 0) + u0
    s_prev = jnp.where((col >= row) & ((unit % nb) != 0), _dot(q, _blocks(kdp, u0), B_QK) * SCALE, NEG)
    return q, s_own, s_prev


def _qkv_prep(z0, tabs):
    def body(q_ref, k_ref, v_ref, c_ref, a_ref, b_ref, qo_ref, ko_ref, vo_ref, tmp):
        p = pl.program_id(1)
        ko_ref[0:BLK, :] = jnp.zeros((BLK, HEAD_DIM), BF16)
        vo_ref[0:BLK, :] = jnp.zeros((BLK, HEAD_DIM), BF16)
        for gi, (_, dil) in enumerate(PATTERNS):
            @pl.when(p == gi)
            def _(dil=dil):
                c, a, b = c_ref[...], a_ref[...], b_ref[...]
                tmp[...] = _rope(q_ref[...], c, a, b)
                _deinterleave(qo_ref, tmp, dil, BF16)
                tmp[...] = _rope(k_ref[...], c, a, b)
                _deinterleave(ko_ref, tmp, dil, BF16, BLK)
                _deinterleave(vo_ref, v_ref, dil, BF16, BLK)

    tab = pl.BlockSpec((S, HEAD_DIM), lambda h, p: (0, 0))
    out = pl.BlockSpec((S, HEAD_DIM), lambda h, p: (0, p * 8 + h))
    outp = pl.BlockSpec((S + BLK, HEAD_DIM), lambda h, p: (0, p * 8 + h))
    return pl.pallas_call(
        body, grid=(8, 3), in_specs=[_head_spec(Q_COL), _head_spec(K_COL), _head_spec(V_COL), tab, tab, tab],
        out_specs=[out, outp, outp],
        out_shape=[jax.ShapeDtypeStruct((S, 3072), BF16)] + [jax.ShapeDtypeStruct((S + BLK, 3072), BF16)] * 2,
        scratch_shapes=[pltpu.VMEM((S, HEAD_DIM), F32)],
        compiler_params=_params(("parallel", "arbitrary"), VMEM_BIG), name="qkv_prep",
    )(z0, z0, z0, *tabs)


def _attn_group_fwd(dil, qd, kdp, vdp, od, ld, og, lg):
    nb = S // dil // BLK
    for u0 in range(0, NUNITS, CU):
        _, s_own, s_prev = _chunk_scores(u0, nb, qd, kdp)
        m = jnp.max(s_own, axis=2, keepdims=True)
        if s_prev is not None:
            m = jnp.maximum(m, jnp.max(s_prev, axis=2, keepdims=True))
        p_own = jnp.exp(s_own - m)
        den = jnp.sum(p_own, axis=2, keepdims=True)
        acc = _dot(p_own.astype(BF16), _blocks(vdp, u0 + 1), B_PV)
        if s_prev is not None:
            p_prev = jnp.exp(s_prev - m)
            den = den + jnp.sum(p_prev, axis=2, keepdims=True)
            acc = acc + _dot(p_prev.astype(BF16), _blocks(vdp, u0), B_PV)
        rows = slice(u0 * BLK, (u0 + CU) * BLK)
        od[rows, :] = (acc / den).reshape(CU * BLK, HEAD_DIM)
        ld[rows, :] = jnp.broadcast_to(m + jnp.log(den), (CU, BLK, HEAD_DIM)).reshape(CU * BLK, HEAD_DIM)
    _interleave(og, od, dil)
    _interleave(lg, ld, dil)


def _group_weights(lgs):
    l0, l1, l2 = lgs[0][...], lgs[1][...], lgs[2][...]
    mx = jnp.maximum(l0, jnp.maximum(l1, l2))
    e0, e1, e2 = jnp.exp(l0 - mx), jnp.exp(l1 - mx), jnp.exp(l2 - mx)
    den = e0 + e1 + e2
    return e0 / den, e1 / den, e2 / den


def _head_spec(base):
    return pl.BlockSpec((S, HEAD_DIM), lambda h, p: (0, base + (p % 3) * 8 + h))


def _slab(dtype=F32, rows=S):
    return pltpu.VMEM((rows, HEAD_DIM), dtype)


def _attn_fwd(z0, qkv, ycat):
    def body(q_ref, k_ref, v_ref, gate_ref, ycat_ref, out_ref, og_ref, lg_ref,
             od, ld, og0, og1, og2, lg0, lg1, lg2):
        del ycat_ref
        p = pl.program_id(1)
        ogs, lgs = (og0, og1, og2), (lg0, lg1, lg2)
        for gi, (_, dil) in enumerate(PATTERNS):
            @pl.when(p == gi)
            def _(gi=gi, dil=dil):
                _attn_group_fwd(dil, q_ref, k_ref, v_ref, od, ld, ogs[gi], lgs[gi])
                og_ref[...] = ogs[gi][...]
                lg_ref[...] = lgs[gi][...]

        @pl.when(p == 2)
        def _():
            w0, w1, w2 = _group_weights(lgs)
            o = w0 * og0[...] + w1 * og1[...] + w2 * og2[...]
            gate = gate_ref[...]
            out_ref[...] = (o * (gate * _sigmoid(gate))).astype(BF16)

    grp = pl.BlockSpec((S, HEAD_DIM), lambda h, p: (0, p * 8 + h))
    grp_pad = pl.BlockSpec((S + BLK, HEAD_DIM), lambda h, p: (0, p * 8 + h))
    return pl.pallas_call(
        body, grid=(8, 3),
        in_specs=[grp, grp_pad, grp_pad, pl.BlockSpec((S, HEAD_DIM), lambda h, p: (0, BG_COL + h)), ANY_SPEC],
        out_specs=[pl.BlockSpec((S, HEAD_DIM), lambda h, p: (0, 8 + h)), grp, grp],
        out_shape=[jax.ShapeDtypeStruct((S, 2048), BF16), jax.ShapeDtypeStruct((S, 3072), F32),
                   jax.ShapeDtypeStruct((S, 3072), F32)],
        scratch_shapes=[_slab() for _ in range(8)],
        input_output_aliases={4: 0},
        compiler_params=_params(("parallel", "arbitrary"), VMEM_BIG), name="attn_fwd",
    )(*qkv, z0, ycat)


def _attn_bwd(z0, qkv, og, lg, dycat, tabs, dep):
    def body(q_ref, k_ref, v_ref, gate_ref, dy_ref, c_ref, a_ref, b_ref,
             og0_ref, og1_ref, og2_ref, lg0_ref, lg1_ref, lg2_ref, dep_ref,
             dq_ref, dk_ref, dv_ref, dbg_ref,
             tmp, ld, dg0, dg1, dg2, cg0, cg1, cg2, dod, cd, dqd, dkd, dvd):
        kd, vd = k_ref, v_ref
        p = pl.program_id(1)
        ogs, lgs, dgs, cgs = (og0_ref, og1_ref, og2_ref), (lg0_ref, lg1_ref, lg2_ref), (dg0, dg1, dg2), (cg0, cg1, cg2)

        @pl.when(p == 0)
        def _():
            w = _group_weights(lgs)
            o = w[0] * ogs[0][...] + w[1] * ogs[1][...] + w[2] * ogs[2][...]
            silu, dsilu = _silu_and_grad(gate_ref[...])
            dy = dy_ref[...]
            dbg_ref[...] = (dy * o * dsilu).astype(BF16)
            do = dy * silu
            dwbar = jnp.sum(do * o, axis=1, keepdims=True)
            for gi in range(3):
                dgs[gi][...] = w[gi] * do
                cgs[gi][...] = -w[gi] * dwbar

        for gi, (_, dil) in enumerate(PATTERNS):
            @pl.when(p == 1 + gi)
            def _(gi=gi, dil=dil):
                nb = S // dil // BLK
                qd = q_ref
                c, a, b = c_ref[...], a_ref[...], b_ref[...]
                _deinterleave(dod, dgs[gi], dil, BF16)
                _deinterleave(ld, lgs[gi], dil)
                _deinterleave(cd, cgs[gi], dil)
                dkd[...] = jnp.zeros_like(dkd)
                dvd[...] = jnp.zeros_like(dvd)
                flat = lambda t: t.reshape(CU * BLK, HEAD_DIM)
                for u0 in range(0, NUNITS, CU):
                    q, s_own, s_prev = _chunk_scores(u0, nb, qd, kd)
                    lse, cv, do = _blocks(ld, u0), _blocks(cd, u0), _blocks(dod, u0)
                    own = slice((u0 + 1) * BLK, (u0 + 1 + CU) * BLK)
                    p_own = jnp.exp(s_own - lse)
                    ds_own = (p_own * (_dot(do, _blocks(vd, u0 + 1), B_QK) + cv) * SCALE).astype(BF16)
                    dq = _dot(ds_own, _blocks(kd, u0 + 1), B_PV)
                    dkd[own, :] += flat(_dot(ds_own, q, B_TN))
                    dvd[own, :] += flat(_dot(p_own.astype(BF16), do, B_TN))
                    if s_prev is not None:
                        prev = slice(u0 * BLK, (u0 + CU) * BLK)
                        p_prev = jnp.exp(s_prev - lse)
                        ds_prev = (p_prev * (_dot(do, _blocks(vd, u0), B_QK) + cv) * SCALE).astype(BF16)
                        dq = dq + _dot(ds_prev, _blocks(kd, u0), B_PV)
                        dkd[prev, :] += flat(_dot(ds_prev, q, B_TN))
                        dvd[prev, :] += flat(_dot(p_prev.astype(BF16), do, B_TN))
                    dqd[u0 * BLK:(u0 + CU) * BLK, :] = flat(dq)
                _interleave(tmp, dqd, dil)
                dq_ref[...] = _rope_t(tmp[...], c, a, b).astype(BF16)
                _interleave(tmp, dkd, dil, BLK)
                dk_ref[...] = _rope_t(tmp[...], c, a, b).astype(BF16)
                _interleave(tmp, dvd, dil, BLK)
                dv_ref[...] = tmp[...].astype(BF16)

    tab = pl.BlockSpec((S, HEAD_DIM), lambda h, p: (0, 0))
    hspec = lambda base: pl.BlockSpec((S, HEAD_DIM), lambda h, p: (0, base + h))
    gspec = pl.BlockSpec((S, HEAD_DIM), lambda h, p: (0, jnp.maximum(p - 1, 0) * 8 + h))
    gspec_pad = pl.BlockSpec((S + BLK, HEAD_DIM), lambda h, p: (0, jnp.maximum(p - 1, 0) * 8 + h))
    return pl.pallas_call(
        body, grid=(8, 4),
        in_specs=[gspec, gspec_pad, gspec_pad, hspec(BG_COL), hspec(8), tab, tab, tab,
                  hspec(0), hspec(8), hspec(16), hspec(0), hspec(8), hspec(16), ANY_SPEC],
        out_specs=[gspec, gspec, gspec, hspec(0)],
        out_shape=[jax.ShapeDtypeStruct((S, 3072), BF16)] * 3 + [jax.ShapeDtypeStruct((S, HALF), BF16)],
        scratch_shapes=[_slab(), _slab()] + [_slab() for _ in range(6)]
                       + [_slab(BF16), _slab(), _slab(), _slab(F32, S + BLK), _slab(F32, S + BLK)],
        compiler_params=_params(("parallel", "arbitrary"), VMEM_BIG), name="attn_bwd",
    )(*qkv, z0, dycat, *tabs, og, og, og, lg, lg, lg, dep)


SGU_CH = 256
NCHUNK = TR // 128


def _ln_stats(x):
    mu = jnp.mean(x, axis=-1, keepdims=True)
    xc = x - mu
    r = lax.rsqrt(jnp.mean(xc * xc, axis=-1, keepdims=True) + EPS)
    return xc * r, r


def _ln_bwd(dy, xhat, r, g):
    dxh = dy * g
    return r * (dxh - jnp.mean(dxh, axis=-1, keepdims=True) - xhat * jnp.mean(dxh * xhat, axis=-1, keepdims=True))


def _tril_bf16(w):
    row = lax.broadcasted_iota(jnp.int32, w.shape, 0)
    col = lax.broadcasted_iota(jnp.int32, w.shape, 1)
    return jnp.where(row >= col, w, 0.0).astype(BF16)


def _sgu_gate(vn_s, s_s, w_ref, bb_ref):
    for h in range(4):
        wm = _tril_bf16(w_ref[h])
        bias = bb_ref[h]
        for ch in range(NCHUNK):
            rows, cols = slice(ch * 128, (ch + 1) * 128), slice(h * SGU_CH, (h + 1) * SGU_CH)
            s_s[rows, cols] = _dot(wm, vn_s[rows, cols]) + jnp.concatenate([bias, bias], axis=1)


WIN = HALO + TR
SUBL = 8


def _shifted_copies(dst, src):
    dst[0] = src[...]
    for b in range(1, SUBL):
        dst[b, 0:WIN - SUBL, :] = src[pl.ds(b, WIN - SUBL), :]


def _rows_at(copies, off, n):
    return copies[off % SUBL, pl.ds(off - off % SUBL, n), :]


def _conv_fwd(i, dval_ref, dglu_ref, hval_ref, hglu_ref, cw_ref, cb_ref, xw, xr, dcs):
    halo = hval_ref[...] * _sigmoid(hglu_ref[...])
    xw[0:HALO, :] = jnp.where(i > 0, halo, 0.0)
    xw[HALO:HALO + TR, :] = dval_ref[...] * _sigmoid(dglu_ref[...])
    _shifted_copies(xr, xw)
    for rb in range(TR // SUB):
        acc = jnp.broadcast_to(cb_ref[...], (SUB, HALF))
        for k in range(CONV_K):
            acc = acc + cw_ref[k:k + 1, :] * _rows_at(xr, rb * SUB + HALO - (CONV_K - 1) + k, SUB)
        dcs[rb * SUB:(rb + 1) * SUB, :] = acc


def _odd_in_specs():
    col = lambda j: pl.BlockSpec((TR, HALF), lambda i, *_: (i, j))
    prev = lambda j: pl.BlockSpec((HALO, HALF), lambda i, *_: (jnp.maximum(i * (TR // HALO) - 1, 0), j))
    return [col(0), col(1), col(2), col(3), col(4), col(5), prev(3), prev(4)]


def _full_spec(shape):
    return pl.BlockSpec(shape, lambda i, *_: (0,) * len(shape))


def _odd_fwd(z1, sgu_g, sgu_b, sgu_w, sgu_bb, conv_w, conv_b, cn_g, cn_b):
    def body(u_ref, v_ref, cg_ref, dval_ref, dglu_ref, dgate_ref, hval_ref, hglu_ref,
             g_ref, b_ref, w_ref, bb_ref, cw_ref, cb_ref, cng_ref, cnb_ref, out_ref, dcs, vn_s, s_s, xw, xr):
        i = pl.program_id(0)
        vhat, _ = _ln_stats(v_ref[...])
        vn_s[...] = (vhat * g_ref[...] + b_ref[...]).astype(BF16)
        _sgu_gate(vn_s, s_s, w_ref, bb_ref)
        cg = cg_ref[...]
        out_ref[:, 0:HALF] = (u_ref[...] * s_s[...] * (cg * _sigmoid(cg))).astype(BF16)
        _conv_fwd(i, dval_ref, dglu_ref, hval_ref, hglu_ref, cw_ref, cb_ref, xw, xr, dcs)
        dhat, _ = _ln_stats(dcs[...])
        dn = dhat * cng_ref[...] + cnb_ref[...]
        dgate = dgate_ref[...]
        out_ref[:, HALF:2 * HALF] = ((dn * _sigmoid(dn)) * (dgate * _sigmoid(dgate))).astype(BF16)

    vec = _full_spec((1, HALF))
    return pl.pallas_call(
        body, grid=(S // TR,),
        in_specs=_odd_in_specs() + [vec, vec, _full_spec((4, 128, 128)), _full_spec((4, 128, 128)),
                                    _full_spec((HALO, HALF)), vec, vec, vec],
        out_specs=[pl.BlockSpec((TR, 2048), lambda i: (i, 0)), pl.BlockSpec((TR, HALF), lambda i: (i, 0))],
        out_shape=[jax.ShapeDtypeStruct((S, 2048), BF16), jax.ShapeDtypeStruct((S, HALF), F32)],
        scratch_shapes=[pltpu.VMEM((TR, HALF), BF16), pltpu.VMEM((TR, HALF), F32),
                        pltpu.VMEM((WIN, HALF), F32), pltpu.VMEM((SUBL, WIN, HALF), F32)],
        compiler_params=_params(("parallel",), VMEM_BIG), name="odd_fwd",
    )(z1, z1, z1, z1, z1, z1, z1, z1, sgu_g, sgu_b, sgu_w, sgu_bb, conv_w, conv_b, cn_g, cn_b)


def _odd_bwd_a(z1, dc, dycat, sgu_g, sgu_b, sgu_w, sgu_bb, cn_g, cn_b):
    def body(u_ref, v_ref, cg_ref, dgate_ref, dcs, dy_ref, g_ref, b_ref, w_ref, bb_ref, cng_ref, cnb_ref,
             dz_ref, ddc_ref, dw_ref, dbb_ref, dg_ref, db_ref, dcng_ref, dcnb_ref, dcb_ref,
             vn_s, s_s, ds_s, dvn_s):
        i = pl.program_id(0)
        vhat, rv = _ln_stats(v_ref[...])
        g = g_ref[...]
        vn_s[...] = (vhat * g + b_ref[...]).astype(BF16)
        _sgu_gate(vn_s, s_s, w_ref, bb_ref)
        silu_c, dsilu_c = _silu_and_grad(cg_ref[...])
        dyc = dy_ref[:, 0:HALF]
        u = u_ref[...]
        s = s_s[...]
        dz_ref[:, 0:HALF] = (dyc * s * silu_c).astype(BF16)
        dz_ref[:, 2 * HALF:3 * HALF] = (dyc * u * s * dsilu_c).astype(BF16)
        ds_s[...] = dyc * u * silu_c

        @pl.when(i == 0)
        def _():
            dw_ref[...] = jnp.zeros_like(dw_ref)
            dbb_ref[...] = jnp.zeros_like(dbb_ref)

        tril = lax.broadcasted_iota(jnp.int32, (128, 128), 0) >= lax.broadcasted_iota(jnp.int32, (128, 128), 1)
        for h in range(4):
            wm = _tril_bf16(w_ref[h])
            for ch in range(NCHUNK):
                rows, cols = slice(ch * 128, (ch + 1) * 128), slice(h * SGU_CH, (h + 1) * SGU_CH)
                ds = ds_s[rows, cols]
                dsb = ds.astype(BF16)
                dw_ref[h] += jnp.where(tril, _dot(dsb, vn_s[rows, cols], NT), 0.0)
                dbb_ref[h] += jnp.broadcast_to(jnp.sum(ds, axis=1, keepdims=True), (128, 128))
                dvn_s[rows, cols] = _dot(wm, dsb, TN)
        dvn = dvn_s[...]
        _acc_rows(dg_ref, dvn * vhat, i)
        _acc_rows(db_ref, dvn, i)
        dz_ref[:, HALF:2 * HALF] = _ln_bwd(dvn, vhat, rv, g).astype(BF16)

        dhat, rd = _ln_stats(dcs[...])
        cng = cng_ref[...]
        silu_n, dsilu_n = _silu_and_grad(dhat * cng + cnb_ref[...])
        silu_g, dsilu_g = _silu_and_grad(dgate_ref[...])
        dyd = dy_ref[:, HALF:2 * HALF]
        dz_ref[:, 5 * HALF:6 * HALF] = (dyd * silu_n * dsilu_g).astype(BF16)
        ddn = dyd * silu_g * dsilu_n
        _acc_rows(dcng_ref, ddn * dhat, i)
        _acc_rows(dcnb_ref, ddn, i)
        ddc = _ln_bwd(ddn, dhat, rd, cng)
        ddc_ref[...] = ddc
        _acc_rows(dcb_ref, ddc, i)

    vec = _full_spec((1, HALF))
    sq = _full_spec((4, 128, 128))
    col = lambda j: pl.BlockSpec((TR, HALF), lambda i: (i, j))
    return pl.pallas_call(
        body, grid=(S // TR,),
        in_specs=[col(0), col(1), col(2), col(5), col(0), pl.BlockSpec((TR, 2048), lambda i: (i, 0)),
                  vec, vec, sq, sq, vec, vec],
        out_specs=[pl.BlockSpec((TR, ODD_IN), lambda i: (i, 0)), pl.BlockSpec((TR, HALF), lambda i: (i, 0)),
                   sq, sq, vec, vec, vec, vec, vec],
        out_shape=[jax.ShapeDtypeStruct((S, ODD_IN), BF16), jax.ShapeDtypeStruct((S, HALF), F32),
                   jax.ShapeDtypeStruct((4, 128, 128), F32), jax.ShapeDtypeStruct((4, 128, 128), F32)]
                  + [jax.ShapeDtypeStruct((1, HALF), F32)] * 5,
        scratch_shapes=[pltpu.VMEM((TR, HALF), BF16), pltpu.VMEM((TR, HALF), F32),
                        pltpu.VMEM((TR, HALF), F32), pltpu.VMEM((TR, HALF), F32)],
        compiler_params=_params(("arbitrary",), VMEM_BIG), name="odd_bwd_a",
    )(z1, z1, z1, z1, dc, dycat, sgu_g, sgu_b, sgu_w, sgu_bb, cn_g, cn_b)


def _odd_bwd_b(z1, ddc, dz1, conv_w):
    nt = S // TR

    def body(dval_ref, dglu_ref, hval_ref, hglu_ref, ddc_ref, hddc_ref, cw_ref, dz_in_ref,
             dz_ref, dcw_ref, xw, dwin, dxs, xr, dr):
        del dz_in_ref
        i, j = pl.program_id(0), pl.program_id(1)
        sg = _sigmoid(dglu_ref[...])
        dval = dval_ref[...]

        @pl.when(j == 0)
        def _():
            halo = hval_ref[...] * _sigmoid(hglu_ref[...])
            xw[0:HALO, :] = jnp.where(i > 0, halo, 0.0)
            xw[HALO:HALO + TR, :] = dval * sg
            dwin[0:TR, :] = ddc_ref[...]
            dwin[TR:TR + HALO, :] = jnp.where(i < nt - 1, hddc_ref[...], 0.0)
            _shifted_copies(xr, xw)
            _shifted_copies(dr, dwin)

            @pl.when(i == 0)
            def _():
                dcw_ref[...] = jnp.zeros_like(dcw_ref)

            for rb in range(TR // SUB):
                acc = jnp.zeros((SUB, HALF), F32)
                for k in range(CONV_K):
                    acc = acc + cw_ref[k:k + 1, :] * _rows_at(dr, rb * SUB + (CONV_K - 1) - k, SUB)
                dxs[rb * SUB:(rb + 1) * SUB, :] = acc
            for k in range(CONV_K):
                acc = jnp.zeros((SUB, HALF), F32)
                for rb in range(TR // SUB):
                    acc = acc + dwin[rb * SUB:(rb + 1) * SUB, :] * _rows_at(xr, rb * SUB + HALO - (CONV_K - 1) + k, SUB)
                dcw_ref[k:k + 1, :] += jnp.sum(acc, axis=0, keepdims=True)
            dz_ref[...] = (dxs[...] * sg).astype(BF16)

        @pl.when(j == 1)
        def _():
            dz_ref[...] = (dxs[...] * dval * sg * (1.0 - sg)).astype(BF16)

    col = lambda c: pl.BlockSpec((TR, HALF), lambda i, j: (i, c))
    prev = lambda c: pl.BlockSpec((HALO, HALF), lambda i, j: (jnp.maximum(i * (TR // HALO) - 1, 0), c))
    nxt = pl.BlockSpec((HALO, HALF), lambda i, j: (jnp.minimum((i + 1) * (TR // HALO), S // HALO - 1), 0))
    return pl.pallas_call(
        body, grid=(nt, 2),
        in_specs=[col(3), col(4), prev(3), prev(4), pl.BlockSpec((TR, HALF), lambda i, j: (i, 0)), nxt,
                  _full_spec((HALO, HALF)), pl.BlockSpec(memory_space=pl.ANY)],
        out_specs=[pl.BlockSpec((TR, HALF), lambda i, j: (i, 3 + j)), _full_spec((HALO, HALF))],
        out_shape=[jax.ShapeDtypeStruct((S, ODD_IN), BF16), jax.ShapeDtypeStruct((HALO, HALF), F32)],
        scratch_shapes=[pltpu.VMEM((WIN, HALF), F32), pltpu.VMEM((WIN, HALF), F32), pltpu.VMEM((TR, HALF), F32),
                        pltpu.VMEM((SUBL, WIN, HALF), F32), pltpu.VMEM((SUBL, WIN, HALF), F32)],
        input_output_aliases={7: 0},
        compiler_params=_params(("arbitrary", "arbitrary"), VMEM_BIG), name="odd_bwd_b",
    )(z1, z1, z1, z1, ddc, ddc, conv_w, dz1)


def _cast_bf16(w, name, piece=0, npieces=1):
    r, c = w.shape[0], w.shape[1] // npieces
    tr = min(r, 256)

    def body(i_ref, o_ref):
        o_ref[...] = i_ref[...].astype(BF16)

    return pl.pallas_call(
        body, grid=(r // tr,), in_specs=[pl.BlockSpec((tr, c), lambda i: (i, piece))],
        out_specs=pl.BlockSpec((tr, c), lambda i: (i, 0)), out_shape=jax.ShapeDtypeStruct((r, c), BF16),
        compiler_params=_params(("parallel",)), name=name,
    )(w)


def _adamw(w, g, m, v):
    m = ADAM_B1 * m + (1.0 - ADAM_B1) * g
    v = ADAM_B2 * v + (1.0 - ADAM_B2) * (g * g)
    m_hat = m / (1.0 - ADAM_B1 ** ADAM_STEP)
    v_hat = v / (1.0 - ADAM_B2 ** ADAM_STEP)
    delta = -ADAM_LR * (m_hat / (jnp.sqrt(v_hat) + ADAM_EPS) + ADAM_WD * w)
    return delta, m, v


def _adam_reduce(parts, w, m, v, name, dep=None, piece=0, npieces=1, prev=None):
    r, c = w.shape
    cp = c // npieces
    tr = min(r, 128)
    extra = ([] if dep is None else [dep]) + ([] if prev is None else list(prev))
    nparts = parts.shape[0]

    def body(p_ref, w_ref, m_ref, v_ref, *rest):
        g_ref, d_ref, nm_ref, nv_ref = rest[len(extra):]
        g = p_ref[0].astype(F32)
        for d in range(1, nparts):
            g = g + p_ref[d].astype(F32)
        g_ref[...] = g
        d_ref[...], nm_ref[...], nv_ref[...] = _adamw(w_ref[...], g, m_ref[...], v_ref[...])

    spec = pl.BlockSpec((tr, cp), lambda i: (i, piece))
    first = 4 + (0 if dep is None else 1)
    return pl.pallas_call(
        body, grid=(r // tr,),
        in_specs=[pl.BlockSpec((nparts, tr, cp), lambda i: (0, i, 0)), spec, spec, spec] + [ANY_SPEC] * len(extra),
        out_specs=[spec] * 4, out_shape=[jax.ShapeDtypeStruct((r, c), F32)] * 4,
        input_output_aliases={} if prev is None else {first + k: k for k in range(4)},
        compiler_params=_params(("parallel",), VMEM_BIG), name=name,
    )(parts, w, m, v, *extra)


def _arrived(x, name, dep=None):
    deps = [] if dep is None else [dep]

    def body(*refs):
        refs[-1][...] = jnp.zeros_like(refs[-1])

    return pl.pallas_call(
        body, in_specs=[ANY_SPEC] * (1 + len(deps)), out_specs=pl.BlockSpec(memory_space=pltpu.VMEM),
        out_shape=jax.ShapeDtypeStruct((8, 128), F32), name=name,
    )(x, *deps)


def _sum_parts(parts, name, dep=None):
    r = parts.shape[1]
    tr = 8
    for cand in (512, 256, 128, 64, 32, 16, 8):
        if r % cand == 0:
            tr = cand
            break
    deps = [] if dep is None else [dep]

    def body(p_ref, *rest):
        g = p_ref[0]
        for d in range(1, NDEV):
            g = g + p_ref[d]
        rest[-1][...] = g

    return pl.pallas_call(
        body, grid=(r // tr,), in_specs=[pl.BlockSpec((NDEV, tr, 128), lambda i: (0, i, 0))] + [ANY_SPEC] * len(deps),
        out_specs=pl.BlockSpec((tr, 128), lambda i: (i, 0)), out_shape=jax.ShapeDtypeStruct((r, 128), F32),
        compiler_params=_params(("parallel",)), name=name,
    )(parts, *deps)


def _sum_unpack(parts, rows, name, dep=None):
    deps = [] if dep is None else [dep]

    def body(p_ref, *outs):
        outs = outs[len(deps):]
        off = 0
        for o_ref, n in zip(outs, rows):
            acc = p_ref[0, off:off + n, :]
            for d in range(1, NDEV):
                acc = acc + p_ref[d, off:off + n, :]
            o_ref[...] = acc
            off += n

    return pl.pallas_call(
        body, grid=(1,), in_specs=[pl.BlockSpec(parts.shape, lambda i: (0, 0, 0))] + [ANY_SPEC] * len(deps),
        out_specs=[pl.BlockSpec((n, 128), lambda i: (0, 0)) for n in rows],
        out_shape=[jax.ShapeDtypeStruct((n, 128), F32) for n in rows],
        compiler_params=_params(("arbitrary",), VMEM_BIG), name=name,
    )(parts, *deps)


def _adam_small(ws, gs, g_specs, ms, vs, name):
    n = len(ws)

    def body(*refs):
        w_r, g_r, m_r, v_r = refs[:n], refs[n:2 * n], refs[2 * n:3 * n], refs[3 * n:4 * n]
        outs = refs[4 * n:]
        for i in range(n):
            g = g_r[i][...]
            outs[4 * i][...] = g
            outs[4 * i + 1][...], outs[4 * i + 2][...], outs[4 * i + 3][...] = _adamw(
                w_r[i][...], g, m_r[i][...], v_r[i][...])

    whole = lambda a: pl.BlockSpec(a.shape, lambda i, nd=a.ndim: (0,) * nd)
    outs = pl.pallas_call(
        body, grid=(1,),
        in_specs=[whole(a) for a in ws] + list(g_specs) + [whole(a) for a in ms] + [whole(a) for a in vs],
        out_specs=[whole(a) for a in ws for _ in range(4)],
        out_shape=[jax.ShapeDtypeStruct(a.shape, F32) for a in ws for _ in range(4)],
        compiler_params=_params(("arbitrary",), VMEM_BIG), name=name,
    )(*ws, *gs, *ms, *vs)
    return [outs[4 * i:4 * i + 4] for i in range(n)]


MASKS = [(mx, my, mc) for mx in (0, 1) for my in (0, 1) for mc in (0, 1)][1:]


def _sc_exchange(name, collective_id, arrays, scatter):
    nt = len(arrays)
    out_type = [jax.ShapeDtypeStruct(a.shape if scatter else (NDEV,) + a.shape, a.dtype) for a in arrays]

    def body(*refs):
        ins, outs = refs[:nt], refs[nt:2 * nt]
        send_sems, recv_sems, local_sems = refs[2 * nt:3 * nt], refs[3 * nt:4 * nt], refs[4 * nt:5 * nt]
        x, y, c = lax.axis_index("x"), lax.axis_index("y"), lax.axis_index("c")
        peers = [(mx + x - 2 * mx * x, my + y - 2 * my * y, mc + c - 2 * mc * c) for mx, my, mc in MASKS]
        barrier = pltpu.get_barrier_semaphore()
        for peer in peers:
            pl.semaphore_signal(barrier, inc=1, device_id=peer, device_id_type=MESH)
        pl.semaphore_wait(barrier, len(peers))
        me = 4 * x + 2 * y + c
        own = []
        for t in range(nt):
            cp = pltpu.make_async_copy(ins[t].at[me] if scatter else ins[t], outs[t].at[me], local_sems[t])
            cp.start()
            own.append(cp)
            for px, py, pc in peers:
                src = ins[t].at[4 * px + 2 * py + pc] if scatter else ins[t]
                pltpu.make_async_remote_copy(src_ref=src, dst_ref=outs[t].at[me], send_sem=send_sems[t],
                                             recv_sem=recv_sems[t], device_id=(px, py, pc), device_id_type=MESH).start()
        for t in range(nt):
            own[t].wait()
            seven = outs[t].at[pl.ds(0, NDEV - 1)]
            drain = pltpu.make_async_remote_copy(src_ref=seven, dst_ref=seven, send_sem=send_sems[t],
                                                 recv_sem=recv_sems[t], device_id=(x, y, c), device_id_type=MESH)
            drain.wait_send()
            drain.wait_recv()

    return pl.kernel(
        body, out_type=out_type, mesh=plsc.ScalarSubcoreMesh(axis_name="sequencer", num_cores=1),
        scratch_types=[pltpu.SemaphoreType.DMA] * (3 * nt),
        compiler_params=pltpu.CompilerParams(collective_id=collective_id), name=name,
    )(*arrays)


def _sc_gather_two_level(name, collective_id, arrays):
    nt = len(arrays)
    out_type = [jax.ShapeDtypeStruct((NDEV,) + a.shape, a.dtype) for a in arrays]

    def body(*refs):
        ins, outs = refs[:nt], refs[nt:2 * nt]
        sems = refs[2 * nt:]
        send_sems, sib_sems, local_sems = sems[:nt], sems[nt:2 * nt], sems[2 * nt:3 * nt]
        ici_sems = [sems[3 * nt + 3 * t:3 * nt + 3 * t + 3] for t in range(nt)]
        x, y, c = lax.axis_index("x"), lax.axis_index("y"), lax.axis_index("c")
        sibling = (x, y, 1 - c)
        chips = [(1 - x, y), (x, 1 - y), (1 - x, 1 - y)]
        barrier = pltpu.get_barrier_semaphore()
        for peer in [sibling] + [(cx, cy, c) for cx, cy in chips]:
            pl.semaphore_signal(barrier, inc=1, device_id=peer, device_id_type=MESH)
        pl.semaphore_wait(barrier, 4)
        me = 4 * x + 2 * y + c

        def push(t, src, slot, recv_sem, to):
            pltpu.make_async_remote_copy(src_ref=src, dst_ref=outs[t].at[slot], send_sem=send_sems[t],
                                         recv_sem=recv_sem, device_id=to, device_id_type=MESH).start()

        own = []
        for t in range(nt):
            cp = pltpu.make_async_copy(ins[t], outs[t].at[me], local_sems[t])
            cp.start()
            own.append(cp)
            for j, (cx, cy) in enumerate(chips):
                push(t, ins[t], me, ici_sems[t][j], (cx, cy, c))
            push(t, ins[t], me, sib_sems[t], sibling)
        for t in range(nt):
            for j, (cx, cy) in enumerate(chips):
                slot = 4 * cx + 2 * cy + c
                landed = outs[t].at[slot]
                pltpu.make_async_remote_copy(src_ref=landed, dst_ref=landed, send_sem=send_sems[t],
                                             recv_sem=ici_sems[t][j], device_id=(cx, cy, c),
                                             device_id_type=MESH).wait_recv()
                push(t, landed, slot, sib_sems[t], sibling)
        for t in range(nt):
            own[t].wait()
            four, seven = outs[t].at[pl.ds(0, 4)], outs[t].at[pl.ds(0, 7)]
            pltpu.make_async_remote_copy(src_ref=four, dst_ref=four, send_sem=send_sems[t], recv_sem=sib_sems[t],
                                         device_id=sibling, device_id_type=MESH).wait_recv()
            pltpu.make_async_remote_copy(src_ref=seven, dst_ref=seven, send_sem=send_sems[t], recv_sem=sib_sems[t],
                                         device_id=sibling, device_id_type=MESH).wait_send()

    return pl.kernel(
        body, out_type=out_type, mesh=plsc.ScalarSubcoreMesh(axis_name="sequencer", num_cores=1),
        scratch_types=[pltpu.SemaphoreType.DMA] * (6 * nt),
        compiler_params=pltpu.CompilerParams(collective_id=collective_id), name=name,
    )(*arrays)


def _sc_sibling_exchange(name, collective_id, src, out_shape, pieces):
    def body(src_ref, out_ref, send_sem, recv_sem):
        x, y, c = lax.axis_index("x"), lax.axis_index("y"), lax.axis_index("c")
        sibling = (x, y, 1 - c)
        barrier = pltpu.get_barrier_semaphore()
        pl.semaphore_signal(barrier, inc=1, device_id=sibling, device_id_type=MESH)
        pl.semaphore_wait(barrier, 1)
        for piece, lands in pieces(c, src_ref, out_ref):
            pltpu.make_async_remote_copy(src_ref=piece, dst_ref=lands, send_sem=send_sem, recv_sem=recv_sem,
                                         device_id=sibling, device_id_type=MESH).start()
        drain = pltpu.make_async_remote_copy(src_ref=out_ref, dst_ref=out_ref, send_sem=send_sem, recv_sem=recv_sem,
                                             device_id=sibling, device_id_type=MESH)
        drain.wait_send()
        drain.wait_recv()

    return pl.kernel(
        body, out_type=jax.ShapeDtypeStruct(out_shape, src.dtype),
        mesh=plsc.ScalarSubcoreMesh(axis_name="sequencer", num_cores=1), scratch_types=[pltpu.SemaphoreType.DMA] * 2,
        compiler_params=pltpu.CompilerParams(collective_id=collective_id), name=name,
    )(src)


def _swap_class_columns(name, collective_id, dz, nb, piece=0, npieces=1):
    w = nb // npieces
    return _sc_sibling_exchange(
        name, collective_id, dz, (S, 4 * w),
        lambda c, src, out: [(src.at[:, pl.ds((2 * j + 1 - c) * nb + piece * w, w)], out.at[:, pl.ds(j * w, w)])
                             for j in range(4)])


def _sc_chip_scatter(name, collective_id, q):
    def body(q_ref, out_ref, send_sem, recv_sem, local_sem):
        x, y, c = lax.axis_index("x"), lax.axis_index("y"), lax.axis_index("c")
        chips = [(1 - x, y), (x, 1 - y), (1 - x, 1 - y)]
        barrier = pltpu.get_barrier_semaphore()
        for cx, cy in chips:
            pl.semaphore_signal(barrier, inc=1, device_id=(cx, cy, c), device_id_type=MESH)
        pl.semaphore_wait(barrier, 3)
        mine = 2 * x + y
        own = pltpu.make_async_copy(q_ref.at[mine], out_ref.at[mine], local_sem)
        own.start()
        for cx, cy in chips:
            pltpu.make_async_remote_copy(src_ref=q_ref.at[2 * cx + cy], dst_ref=out_ref.at[mine], send_sem=send_sem,
                                         recv_sem=recv_sem, device_id=(cx, cy, c), device_id_type=MESH).start()
        own.wait()
        three = out_ref.at[pl.ds(0, 3)]
        drain = pltpu.make_async_remote_copy(src_ref=three, dst_ref=three, send_sem=send_sem, recv_sem=recv_sem,
                                             device_id=(x, y, c), device_id_type=MESH)
        drain.wait_send()
        drain.wait_recv()

    return pl.kernel(
        body, out_type=jax.ShapeDtypeStruct(q.shape, q.dtype),
        mesh=plsc.ScalarSubcoreMesh(axis_name="sequencer", num_cores=1), scratch_types=[pltpu.SemaphoreType.DMA] * 3,
        compiler_params=pltpu.CompilerParams(collective_id=collective_id), name=name,
    )(q)


def _mm_pair_dw(h_own, dz, h_sib, dz_sib, nb, name, dep=None, piece=0, npieces=1, h_transposed=False,
                one_call=False):
    nb = nb // npieces
    tn = 512 if nb % 512 == 0 else nb
    per = nb // tn
    dn = NN if h_transposed else TN
    o_spec = pl.BlockSpec((None, D, tn), lambda i, j, k: (j // per, 0, j % per))
    own_col = lambda i, j, k: (0, ((2 * (j // per) + lax.axis_index("c")) * npieces + piece) * per + j % per)
    if one_call:
        def fused(a0_ref, b0_ref, a1_ref, b1_ref, dep_ref, o_ref):
            acc = _dot(a0_ref[...], b0_ref[...], dn) + _dot(a1_ref[...], b1_ref[...], dn)
            o_ref[...] = acc.astype(BF16)

        whole = pl.BlockSpec((S, D), lambda i, j, k: (0, 0), pipeline_mode=pl.Buffered(1))
        return pl.pallas_call(
            fused, grid=(1, 4 * per, 1),
            in_specs=[whole, pl.BlockSpec((S, tn), own_col), whole, pl.BlockSpec((S, tn), lambda i, j, k: (0, j)),
                      ANY_SPEC],
            out_specs=o_spec, out_shape=jax.ShapeDtypeStruct((4, D, nb), BF16),
            compiler_params=_params(("parallel", "parallel", "arbitrary"), VMEM_BIG), name=name,
        )(h_own, dz, h_sib, dz_sib, dep)
    part = _matmul(
        h_own, dz, dn=dn, grid=(1, 4 * per, 1),
        a_spec=pl.BlockSpec((S, D), lambda i, j, k: (0, 0)), b_spec=pl.BlockSpec((S, tn), own_col),
        o_spec=o_spec, out_shape=(4, D, nb), out_dtype=F32, acc_shape=(D, tn), name=name + "_own", dep=dep)

    def body(a_ref, b_ref, p_ref, o_ref):
        o_ref[...] = (p_ref[...] + _dot(a_ref[...], b_ref[...], dn)).astype(BF16)

    return pl.pallas_call(
        body, grid=(1, 4 * per, 1),
        in_specs=[pl.BlockSpec((S, D), lambda i, j, k: (0, 0)), pl.BlockSpec((S, tn), lambda i, j, k: (0, j)), o_spec],
        out_specs=o_spec, out_shape=jax.ShapeDtypeStruct((4, D, nb), BF16),
        compiler_params=_params(("parallel", "parallel", "arbitrary"), VMEM_BIG), name=name + "_sibling",
    )(h_sib, dz_sib, part)


SMALL = {
    "e_pre_norm": ((2048,), None), "e_pool_w": ((4, 256, 256), 1), "e_pool_scale": ((1024,), None),
    "e_post_norm": ((2048,), None), "o_pre_norm": ((2048,), 0), "o_sgu_norm_g": ((1024,), 0),
    "o_sgu_norm_b": ((1024,), 0), "o_sgu_w": ((4, 128, 128), None), "o_sgu_b": ((4, 128), None),
    "o_conv_w": ((31, 1024), 1), "o_conv_b": ((1024,), 0), "o_conv_norm_g": ((1024,), 0),
    "o_conv_norm_b": ((1024,), 0), "o_post_norm": ((2048,), 0),
}
SMALL_SHARDED = [n for n, (_, ax) in SMALL.items() if ax is not None]


def _shard_shape(name):
    shape, ax = SMALL[name]
    if ax is None:
        return shape
    return tuple(s // NDEV if i == ax else s for i, s in enumerate(shape))


def _pack(arrs, row_multiple=1):
    flat = jnp.concatenate([a.reshape(-1) for a in arrs])
    pad = -flat.shape[0] % (128 * row_multiple)
    return jnp.concatenate([flat, jnp.zeros((pad,), F32)]).reshape(-1, 128)


def _small_views(name):
    shape, ax = SMALL[name]
    me = lambda: 4 * lax.axis_index("x") + 2 * lax.axis_index("y") + lax.axis_index("c")
    if ax is None:
        view = (int(np.prod(shape)) // 128, 128)
        return view, view, pl.BlockSpec(view, lambda i: (0, 0))
    if len(shape) == 1:
        n = shape[0] // NDEV
        return (1, n), (NDEV, 1, n), pl.BlockSpec((None, 1, n), lambda i: (me(), 0, 0))
    part = _shard_shape(name)
    return part, shape, pl.BlockSpec(part, lambda i: tuple(me() if d == ax else 0 for d in range(len(shape))))


WEIGHTS = ["e_pre_norm", "e_w_in", "e_pool_w", "e_pool_scale", "e_w_out", "e_post_norm", "o_pre_norm", "o_w_in",
           "o_sgu_norm_g", "o_sgu_norm_b", "o_sgu_w", "o_sgu_b", "o_conv_w", "o_conv_b", "o_conv_norm_g",
           "o_conv_norm_b", "o_w_out", "o_post_norm"]


def kernel(x, e_pre_norm, e_w_in, e_pool_w, e_pool_scale, e_w_out, e_post_norm, o_pre_norm, o_w_in, o_sgu_norm_g, o_sgu_norm_b, o_sgu_w, o_sgu_b, o_conv_w, o_conv_b, o_conv_norm_g, o_conv_norm_b, o_w_out, o_post_norm, loss_target, m_e_pre_norm, m_e_w_in, m_e_pool_w, m_e_pool_scale, m_e_w_out, m_e_post_norm, m_o_pre_norm, m_o_w_in, m_o_sgu_norm_g, m_o_sgu_norm_b, m_o_sgu_w, m_o_sgu_b, m_o_conv_w, m_o_conv_b, m_o_conv_norm_g, m_o_conv_norm_b, m_o_w_out, m_o_post_norm, v_e_pre_norm, v_e_w_in, v_e_pool_w, v_e_pool_scale, v_e_w_out, v_e_post_norm, v_o_pre_norm, v_o_w_in, v_o_sgu_norm_g, v_o_sgu_norm_b, v_o_sgu_w, v_o_sgu_b, v_o_conv_w, v_o_conv_b, v_o_conv_norm_g, v_o_conv_norm_b, v_o_w_out, v_o_post_norm):
    given = dict(locals())
    w = {n: given[n][0] for n in WEIGHTS}
    m = {n: given["m_" + n][0] for n in WEIGHTS}
    v = {n: given["v_" + n][0] for n in WEIGHTS}
    me = 4 * lax.axis_index("x") + 2 * lax.axis_index("y") + lax.axis_index("c")
    x, target = x[0], loss_target[0]
    row = lambda a: a.reshape(1, -1)

    lo, small_rows = _sc_gather_two_level(
        "gather_a0", 0, [_cast_bf16(w["e_w_in"], "cast_e_w_in_0", 0, 2), _pack([w[n] for n in SMALL_SHARDED])])
    hi, = _sc_gather_two_level("gather_a1", 12, [_cast_bf16(w["e_w_in"], "cast_e_w_in_1", 1, 2)])
    wg_e_in = (lo, hi)
    h0, h0t = _pre0_fwd(x, row(w["e_pre_norm"]))
    wg_e_out, = _sc_gather_two_level("gather_b", 1, [_cast_bf16(w["e_w_out"], "cast_e_w_out")])
    wg_o_in, wg_o_out = _sc_gather_two_level(
        "gather_c", 13, [_cast_bf16(w[n], "cast_" + n) for n in ("o_w_in", "o_w_out")])
    h0t_sib = _sc_sibling_exchange("swap_h0", 8, h0t, h0t.shape, lambda c, src, out: [(src, out)])
    p = {n: w[n] for n in SMALL if SMALL[n][1] is None}
    small_rows = small_rows.reshape(NDEV, -1)
    off = 0
    for n in SMALL_SHARDED:
        shp, ax = _shard_shape(n), SMALL[n][1]
        cnt = int(np.prod(shp))
        blk = small_rows[:, off:off + cnt].reshape((NDEV,) + shp)
        p[n] = jnp.moveaxis(blk, 0, ax).reshape(SMALL[n][0])
        off += cnt
    tabs = _rope_tables()
    pool_w_bf = p["e_pool_w"].astype(BF16)
    sgu_bb = jnp.broadcast_to(p["o_sgu_b"][:, :, None], (4, 128, 128))
    conv_w = jnp.concatenate([p["o_conv_w"], jnp.zeros((HALO - CONV_K, HALF), F32)], axis=0)
    odd_p = (row(p["o_sgu_norm_g"]), row(p["o_sgu_norm_b"]), p["o_sgu_w"], sgu_bb, conv_w,
             row(p["o_conv_b"]), row(p["o_conv_norm_g"]), row(p["o_conv_norm_b"]))

    z0 = _mm_in_halves(h0, wg_e_in, "mm_z0")
    ycat0 = _pool_fwd(z0, pool_w_bf, row(p["e_pool_scale"]))
    qkv = _qkv_prep(z0, tabs)
    ycat0, og, lg = _attn_fwd(z0, qkv, ycat0)
    w_out_e, w_out_o = wg_e_out.reshape(2048, D), wg_o_out.reshape(2048, D)
    y0, x1, h1 = _post0_fwd(ycat0, w_out_e, x, row(p["e_post_norm"]), row(p["o_pre_norm"]), h0t_sib)
    h1_sib = _sc_sibling_exchange("swap_h1", 11, h1, h1.shape, lambda c, src, out: [(src, out)])
    z1 = _mm_in(h1, wg_o_in, "mm_z1")
    ycat1, conv_out = _odd_fwd(z1, *odd_p)

    g = {}
    loss_part, dx2, dy1, g["o_post_norm"] = _post1_bwd(ycat1, w_out_o, x1, target, row(p["o_post_norm"]), h1_sib)
    parts = {}
    dw = _mm_out_dw(ycat1, dy1, "mm_dwout1").reshape(NDEV, 256, D)
    parts["o_w_out"], = _sc_exchange("scatter_o_w_out", 2, [dw], True)
    dycat1 = _mm_out_dx(dy1, w_out_o, "mm_dycat1", dw)
    dz1, ddc, g["o_sgu_w"], d_sgu_bb, g["o_sgu_norm_g"], g["o_sgu_norm_b"], g["o_conv_norm_g"], \
        g["o_conv_norm_b"], g["o_conv_b"] = _odd_bwd_a(z1, conv_out, dycat1, *odd_p[:4], *odd_p[6:])
    dz1, d_conv_w = _odd_bwd_b(z1, ddc, dz1, conv_w)
    g["o_sgu_b"] = d_sgu_bb[:, :, 0]
    g["o_conv_w"] = d_conv_w[:CONV_K]
    grads, deltas, new_m, new_v = {}, {}, {}, {}

    def adam(n, dep):
        grads[n], deltas[n], new_m[n], new_v[n] = _adam_reduce(parts[n], w[n], m[n], v[n], "adam_" + n, dep)
        return new_v[n]

    pin = _arrived(parts["o_w_out"], "arrived_o_w_out", d_conv_w)
    dz1_sib = _swap_class_columns("swap_dz1", 10, dz1, ODD_IN // NDEV)
    dw = _mm_pair_dw(h1, dz1, h1_sib, dz1_sib, ODD_IN // NDEV, "mm_dwin1", pin)
    parts["o_w_in"] = _sc_chip_scatter("scatter_o_w_in", 3, dw)
    dh1 = _mm_in_dx(dz1, wg_o_in, "mm_dh1", dw)
    dx1, dy0, g["o_pre_norm"], g["e_post_norm"] = _mid_bwd(dx2, dh1, x1, y0, row(p["o_pre_norm"]),
                                                           row(p["e_post_norm"]))
    dw = _mm_out_dw(ycat0, dy0, "mm_dwout0").reshape(NDEV, 256, D)
    parts["e_w_out"], = _sc_exchange("scatter_e_w_out", 4, [dw], True)
    dycat0 = _mm_out_dx(dy0, w_out_e, "mm_dycat0", dw)
    da_in, da_gate, g["e_pool_w"], g["e_pool_scale"] = _pool_bwd(z0, dycat0, pool_w_bf, row(p["e_pool_scale"]))
    late = [n for n in SMALL if n not in ("e_pre_norm", "o_sgu_b")] + ["o_sgu_b"]
    pieces = [g[n].reshape(SMALL[n][0]) for n in late[:-1]] + [jnp.broadcast_to(loss_part, (8, 128)), g[late[-1]]]
    recv_small, = _sc_gather_two_level("gather_small_grads", 6, [_pack(pieces, 512)])
    took = _arrived(parts["o_w_in"], "arrived_o_w_in")
    dq, dk, dv, dbg = _attn_bwd(z0, qkv, og, lg, dycat0, tabs, took)
    dz0 = jnp.concatenate([da_in, da_gate, dq, dk, dv, dbg], axis=1)
    took = _arrived(recv_small, "arrived_small_grads", _arrived(parts["e_w_out"], "arrived_e_w_out", dz0))
    nb = EVEN_IN // NDEV
    swapped = [_swap_class_columns("swap_dz0_%d" % half, (9, 14)[half], dz0, nb, half, 2) for half in (0, 1)]
    dw, e_w_in_parts = took, []
    for half in (0, 1):
        dw = _mm_pair_dw(h0t, dz0, h0t_sib, swapped[half], nb, "mm_dwin0_%d" % half, dw, half, 2, True, half == 1)
        e_w_in_parts.append(_sc_chip_scatter("scatter_e_w_in_%d" % half, (5, 15)[half], dw))
    pin = adam("e_w_out", adam("o_w_out", adam("o_w_in", dw)))
    rows = [int(np.prod(SMALL[n][0])) // 128 for n in late]
    sums = _sum_unpack(recv_small, rows[:-1] + [8, rows[-1]], "sum_small_grads", pin)
    summed = dict(zip(late, sums[:-2] + sums[-1:]))
    loss = sums[-2][0, 0]
    dh0 = _mm_in_dx_halves(dz0, wg_e_in, "mm_dh0", summed[late[0]])
    grad_x, g["e_pre_norm"] = _pre0_bwd(dx1, dh0, x, row(p["e_pre_norm"]))
    last, = _sc_exchange("gather_e_pre_norm_grad", 7, [g["e_pre_norm"].reshape(16, 128)], False)

    n = "e_w_in"
    out = _adam_reduce(e_w_in_parts[0], w[n], m[n], v[n], "adam_e_w_in_0", grad_x, 0, 2)
    out = _adam_reduce(e_w_in_parts[1], w[n], m[n], v[n], "adam_e_w_in_1", None, 1, 2, out)
    grads[n], deltas[n], new_m[n], new_v[n] = out
    summed["e_pre_norm"] = _sum_parts(last, "sum_e_pre_norm_grad", out[3])
    names = list(SMALL)
    views = [_small_views(n) for n in names]
    mine = lambda src: [src[n].reshape(vw[0]) for n, vw in zip(names, views)]
    res = _adam_small(mine(w), [summed[n].reshape(vw[1]) for n, vw in zip(names, views)], [vw[2] for vw in views],
                      mine(m), mine(v), "adam_small")
    for n, out in zip(names, res):
        grads[n], deltas[n], new_m[n], new_v[n] = [t.reshape(_shard_shape(n)) for t in out]

    lead = lambda a: a[None]
    return (loss, grad_x[None], *[lead(grads[n]) for n in WEIGHTS], *[lead(deltas[n]) for n in WEIGHTS],
            *[lead(new_m[n]) for n in WEIGHTS], *[lead(new_v[n]) for n in WEIGHTS])
```

```python
import numpy as np
import jax
import jax.numpy as jnp
from jax import lax
from jax.experimental import pallas as pl
from jax.experimental.pallas import tpu as pltpu
from jax.experimental.pallas import tpu_sc as plsc

F32 = jnp.float32
BF16 = jnp.bfloat16

S = 2048
D = 2048
NDEV = 8
EPS = 1e-6
NEG = -1e30
HEAD_DIM = 128
ROT_DIM = 32
ROPE_THETA = 500000.0
PATTERNS = ((128, 1), (512, 4), (2048, 16))
BLK = 128
EVEN_IN = 12288
ODD_IN = 6144
HALF = 1024
CONV_K = 31
HALO = 32
TR = 256
SUB = 16

ADAM_LR = 0.001
ADAM_B1 = 0.9
ADAM_B2 = 0.999
ADAM_EPS = 1e-08
ADAM_WD = 0.01
ADAM_STEP = 10

VMEM_BIG = 56 * 1024 * 1024
MESH = pl.DeviceIdType.MESH

NN = (((1,), (0,)), ((), ()))
NT = (((1,), (1,)), ((), ()))
TN = (((0,), (0,)), ((), ()))


def _dot(a, b, dn=NN):
    return lax.dot_general(a, b, dn, preferred_element_type=F32)


def _sigmoid(x):
    return 1.0 / (1.0 + jnp.exp(-x))


def _silu_and_grad(x):
    sg = _sigmoid(x)
    return x * sg, sg * (1.0 + x * (1.0 - sg))


def _params(sem, vmem=None):
    return pltpu.CompilerParams(dimension_semantics=sem, vmem_limit_bytes=vmem)


ANY_SPEC = pl.BlockSpec(memory_space=pl.ANY)


def _matmul(a, b, *, dn, grid, a_spec, b_spec, o_spec, out_shape, out_dtype, acc_shape, name, dep=None):
    nk = grid[2]
    deps = [] if dep is None else list(dep) if isinstance(dep, (tuple, list)) else [dep]

    def body(a_ref, b_ref, *rest):
        o_ref, acc = rest[len(deps)], rest[len(deps) + 1:]
        if nk == 1:
            o_ref[...] = _dot(a_ref[...], b_ref[...], dn).astype(o_ref.dtype)
            return
        acc_ref = acc[0]
        k = pl.program_id(2)

        @pl.when(k == 0)
        def _():
            acc_ref[...] = jnp.zeros_like(acc_ref)

        acc_ref[...] += _dot(a_ref[...], b_ref[...], dn)

        @pl.when(k == nk - 1)
        def _():
            o_ref[...] = acc_ref[...].astype(o_ref.dtype)

    return pl.pallas_call(
        body, grid=grid, in_specs=[a_spec, b_spec] + [ANY_SPEC] * len(deps), out_specs=o_spec,
        out_shape=jax.ShapeDtypeStruct(out_shape, out_dtype),
        scratch_shapes=[] if nk == 1 else [pltpu.VMEM(acc_shape, F32)],
        compiler_params=_params(("parallel", "parallel", "arbitrary"), VMEM_BIG), name=name,
    )(a, b, *deps)


TM = 2048


def _mm_in(h, wg, name):
    nb = wg.shape[2]
    tn = 512 if nb % 512 == 0 else nb
    per = nb // tn
    return _matmul(
        h, wg, dn=NN, grid=(S // TM, NDEV * per, 1),
        a_spec=pl.BlockSpec((TM, D), lambda i, j, k: (i, 0)),
        b_spec=pl.BlockSpec((None, D, tn), lambda i, j, k: (j // per, 0, j % per)),
        o_spec=pl.BlockSpec((TM, tn), lambda i, j, k: (i, j)),
        out_shape=(S, NDEV * nb), out_dtype=F32, acc_shape=(TM, tn), name=name)


def _mm_in_halves(h, wg_halves, name):
    hb = wg_halves[0].shape[2]
    z = None
    for half, wg in enumerate(wg_halves):
        prev = [] if z is None else [z]

        def body(a_ref, b_ref, *rest):
            rest[-1][...] = _dot(a_ref[...], b_ref[...])

        z = pl.pallas_call(
            body, grid=(NDEV,),
            in_specs=[pl.BlockSpec((S, D), lambda j: (0, 0)), pl.BlockSpec((None, D, hb), lambda j: (j, 0, 0))]
                     + [ANY_SPEC] * len(prev),
            out_specs=pl.BlockSpec((S, hb), lambda j, half=half: (0, 2 * j + half)),
            out_shape=jax.ShapeDtypeStruct((S, 2 * NDEV * hb), F32),
            input_output_aliases={2: 0} if prev else {},
            compiler_params=_params(("parallel",), VMEM_BIG), name="%s_%d" % (name, half),
        )(h, wg, *prev)
    return z


def _mm_in_dx_halves(dz, wg_halves, name, dep):
    hb = wg_halves[0].shape[2]
    nk = 2 * NDEV

    def body(a_ref, b0_ref, b1_ref, dep_ref, o_ref, acc_ref):
        k = pl.program_id(2)

        @pl.when(k == 0)
        def _():
            acc_ref[...] = jnp.zeros_like(acc_ref)

        @pl.when(k % 2 == 0)
        def _():
            acc_ref[...] += _dot(a_ref[...], b0_ref[...], NT)

        @pl.when(k % 2 == 1)
        def _():
            acc_ref[...] += _dot(a_ref[...], b1_ref[...], NT)

        @pl.when(k == nk - 1)
        def _():
            o_ref[...] = acc_ref[...]

    b_spec = pl.BlockSpec((None, 1024, hb), lambda i, j, k: (k // 2, j, 0))
    return pl.pallas_call(
        body, grid=(1, D // 1024, nk),
        in_specs=[pl.BlockSpec((S, hb), lambda i, j, k: (0, k)), b_spec, b_spec, ANY_SPEC],
        out_specs=pl.BlockSpec((S, 1024), lambda i, j, k: (0, j)), out_shape=jax.ShapeDtypeStruct((S, D), F32),
        scratch_shapes=[pltpu.VMEM((S, 1024), F32)],
        compiler_params=_params(("parallel", "parallel", "arbitrary"), VMEM_BIG), name=name,
    )(dz, *wg_halves, dep)


def _mm_in_dx(dz, wg, name, dep=None):
    nb = wg.shape[2]
    return _matmul(
        dz, wg, dn=NT, grid=(S // TM, D // 1024, NDEV),
        a_spec=pl.BlockSpec((TM, nb), lambda i, j, k: (i, k)),
        b_spec=pl.BlockSpec((None, 1024, nb), lambda i, j, k: (k, j, 0)),
        o_spec=pl.BlockSpec((TM, 1024), lambda i, j, k: (i, j)),
        out_shape=(S, D), out_dtype=F32, acc_shape=(TM, 1024), name=name, dep=dep)


def _mm_out_dx(dy, w, name, dep=None):
    return _matmul(
        dy, w, dn=NT, grid=(S // TM, 2048 // 512, 1),
        a_spec=pl.BlockSpec((TM, D), lambda i, j, k: (i, 0)),
        b_spec=pl.BlockSpec((512, D), lambda i, j, k: (j, 0)),
        o_spec=pl.BlockSpec((TM, 512), lambda i, j, k: (i, j)),
        out_shape=(S, 2048), out_dtype=F32, acc_shape=(TM, 512), name=name, dep=dep)


def _mm_out_dw(yc, dy, name):
    return _matmul(
        yc, dy, dn=TN, grid=(2048 // TM, D // 512, 1),
        a_spec=pl.BlockSpec((S, TM), lambda i, j, k: (0, i)),
        b_spec=pl.BlockSpec((S, 512), lambda i, j, k: (0, j)),
        o_spec=pl.BlockSpec((TM, 512), lambda i, j, k: (i, j)),
        out_shape=(2048, D), out_dtype=BF16, acc_shape=(TM, 512), name=name)


def _row_spec(w=D):
    return pl.BlockSpec((TR, w), lambda i: (i, 0))


def _vec_spec(w=D):
    return pl.BlockSpec((1, w), lambda i: (0, 0))


def _rms_stats(x):
    r = lax.rsqrt(jnp.mean(x * x, axis=-1, keepdims=True) + EPS)
    return x * r, r


def _rms_bwd(dn, xhat, r, g):
    dxh = dn * g
    return r * (dxh - xhat * jnp.mean(dxh * xhat, axis=-1, keepdims=True))


def _acc_rows(ref, val, i):
    s = jnp.sum(val, axis=0, keepdims=True)

    @pl.when(i == 0)
    def _():
        ref[...] = s

    @pl.when(i > 0)
    def _():
        ref[...] += s


def _pre0_fwd(x, g):
    def body(x_ref, g_ref, h_ref, ht_ref):
        xhat, _ = _rms_stats(x_ref[...])
        h = xhat * g_ref[...]
        h_ref[...] = h.astype(BF16)
        ht_ref[...] = h.T.astype(BF16)

    return pl.pallas_call(
        body, grid=(S // TR,), in_specs=[_row_spec(), _vec_spec()],
        out_specs=[_row_spec(), pl.BlockSpec((D, TR), lambda i: (0, i))],
        out_shape=[jax.ShapeDtypeStruct((S, D), BF16), jax.ShapeDtypeStruct((D, S), BF16)],
        compiler_params=_params(("parallel",)), name="pre0_fwd",
    )(x, g)


def _post0_fwd(ycat, w_out, x, g_post, g_pre1, dep):
    def body(yc_ref, w_ref, x_ref, gp_ref, g1_ref, dep_ref, y_ref, x1_ref, h1_ref):
        y = _dot(yc_ref[...], w_ref[...])
        y_ref[...] = y
        yhat, _ = _rms_stats(y)
        x1 = x_ref[...] + yhat * gp_ref[...]
        x1_ref[...] = x1
        xhat, _ = _rms_stats(x1)
        h1_ref[...] = (xhat * g1_ref[...]).astype(BF16)

    return pl.pallas_call(
        body, grid=(S // TR,),
        in_specs=[_row_spec(), pl.BlockSpec((2048, D), lambda i: (0, 0)), _row_spec(), _vec_spec(), _vec_spec(),
                  ANY_SPEC],
        out_specs=[_row_spec(), _row_spec(), _row_spec()],
        out_shape=[jax.ShapeDtypeStruct((S, D), F32), jax.ShapeDtypeStruct((S, D), F32),
                   jax.ShapeDtypeStruct((S, D), BF16)],
        compiler_params=_params(("parallel",), VMEM_BIG), name="post0_fwd",
    )(ycat, w_out, x, g_post, g_pre1, dep)


def _post1_bwd(ycat, w_out, x1, target, g_post, dep):
    def body(yc_ref, w_ref, x1_ref, t_ref, g_ref, dep_ref, loss_ref, dx2_ref, dy_ref, dg_ref):
        i = pl.program_id(0)
        yhat, r = _rms_stats(_dot(yc_ref[...], w_ref[...]))
        g = g_ref[...]
        err = x1_ref[...] + yhat * g - t_ref[...]
        part = jnp.sum(jnp.sum(err * err, axis=-1, keepdims=True), axis=0, keepdims=True) * (0.5 / D)
        _acc_rows(loss_ref, jnp.broadcast_to(part, (1, 128)), i)
        dx2 = err * (1.0 / D)
        dx2_ref[...] = dx2
        _acc_rows(dg_ref, dx2 * yhat, i)
        dy_ref[...] = _rms_bwd(dx2, yhat, r, g).astype(BF16)

    return pl.pallas_call(
        body, grid=(S // TR,),
        in_specs=[_row_spec(), pl.BlockSpec((2048, D), lambda i: (0, 0)), _row_spec(), _row_spec(), _vec_spec(),
                  ANY_SPEC],
        out_specs=[_vec_spec(128), _row_spec(), _row_spec(), _vec_spec()],
        out_shape=[jax.ShapeDtypeStruct((1, 128), F32), jax.ShapeDtypeStruct((S, D), F32),
                   jax.ShapeDtypeStruct((S, D), BF16), jax.ShapeDtypeStruct((1, D), F32)],
        compiler_params=_params(("arbitrary",), VMEM_BIG), name="post1_bwd",
    )(ycat, w_out, x1, target, g_post, dep)


def _mid_bwd(dx2, dh1, x1, y0, g_pre1, g_post0):
    def body(dx2_ref, dh_ref, x1_ref, y_ref, g1_ref, gp_ref, dx1_ref, dy_ref, dg1_ref, dgp_ref):
        i = pl.program_id(0)
        xhat, r1 = _rms_stats(x1_ref[...])
        dh = dh_ref[...]
        _acc_rows(dg1_ref, dh * xhat, i)
        dx1 = dx2_ref[...] + _rms_bwd(dh, xhat, r1, g1_ref[...])
        dx1_ref[...] = dx1
        yhat, r0 = _rms_stats(y_ref[...])
        _acc_rows(dgp_ref, dx1 * yhat, i)
        dy_ref[...] = _rms_bwd(dx1, yhat, r0, gp_ref[...]).astype(BF16)

    return pl.pallas_call(
        body, grid=(S // TR,),
        in_specs=[_row_spec(), _row_spec(), _row_spec(), _row_spec(), _vec_spec(), _vec_spec()],
        out_specs=[_row_spec(), _row_spec(), _vec_spec(), _vec_spec()],
        out_shape=[jax.ShapeDtypeStruct((S, D), F32), jax.ShapeDtypeStruct((S, D), BF16),
                   jax.ShapeDtypeStruct((1, D), F32), jax.ShapeDtypeStruct((1, D), F32)],
        compiler_params=_params(("arbitrary",)), name="mid_bwd",
    )(dx2, dh1, x1, y0, g_pre1, g_post0)


def _pre0_bwd(dx1, dh0, x, g):
    def body(dx1_ref, dh_ref, x_ref, g_ref, gx_ref, dg_ref):
        i = pl.program_id(0)
        xhat, r = _rms_stats(x_ref[...])
        dh = dh_ref[...]
        _acc_rows(dg_ref, dh * xhat, i)
        gx_ref[...] = dx1_ref[...] + _rms_bwd(dh, xhat, r, g_ref[...])

    return pl.pallas_call(
        body, grid=(S // TR,), in_specs=[_row_spec(), _row_spec(), _row_spec(), _vec_spec()],
        out_specs=[_row_spec(), _vec_spec()],
        out_shape=[jax.ShapeDtypeStruct((S, D), F32), jax.ShapeDtypeStruct((1, D), F32)],
        compiler_params=_params(("arbitrary",)), name="pre0_bwd",
    )(dx1, dh0, x, g)


POOL_CH = 256


def _pool_apply(a, w, transpose):
    n = a.shape[0]
    row = lax.broadcasted_iota(jnp.int32, a.shape, 0)
    cnt = jnp.minimum(row + 1, w).astype(F32)
    s = a / cnt if transpose else a
    for k in (1, 2, 4, 8):
        if transpose:
            sh = jnp.where(row < n - k, pltpu.roll(s, n - k, 0), 0.0)
        else:
            sh = jnp.where(row >= k, pltpu.roll(s, k, 0), 0.0)
        s = jnp.where(w > k, s + sh, s)
    return s - a if transpose else s / cnt - a


def _pool_fwd(z0, pool_w, pool_scale):
    def body(a_ref, gate_ref, w_ref, sc_ref, out_ref):
        win = jnp.left_shift(2, pl.program_id(0))
        pooled = _pool_apply(a_ref[...], win, False)
        mixed = _dot(pooled.astype(BF16), w_ref[...])
        gate = gate_ref[...]
        out_ref[...] = (mixed * sc_ref[...] * (gate * _sigmoid(gate))).astype(BF16)

    return pl.pallas_call(
        body, grid=(4,),
        in_specs=[pl.BlockSpec((S, POOL_CH), lambda g: (0, g)), pl.BlockSpec((S, POOL_CH), lambda g: (0, 4 + g)),
                  pl.BlockSpec((None, POOL_CH, POOL_CH), lambda g: (g, 0, 0)),
                  pl.BlockSpec((1, POOL_CH), lambda g: (0, g))],
        out_specs=pl.BlockSpec((S, POOL_CH), lambda g: (0, g)),
        out_shape=jax.ShapeDtypeStruct((S, 2048), BF16),
        compiler_params=_params(("parallel",), VMEM_BIG), name="pool_fwd",
    )(z0, z0, pool_w, pool_scale)


def _pool_bwd(z0, dycat, pool_w, pool_scale):
    def body(a_ref, gate_ref, dy_ref, w_ref, sc_ref, da_ref, dgate_ref, dw_ref, dsc_ref):
        win = jnp.left_shift(2, pl.program_id(0))
        pooled = _pool_apply(a_ref[...], win, False).astype(BF16)
        w = w_ref[...]
        mixed = _dot(pooled, w)
        silu, dsilu = _silu_and_grad(gate_ref[...])
        dy = dy_ref[...]
        sc = sc_ref[...]
        dgate_ref[...] = (dy * (mixed * sc) * dsilu).astype(BF16)
        dms = dy * silu
        dsc_ref[...] = jnp.sum(dms * mixed, axis=0, keepdims=True)
        dmixed = (dms * sc).astype(BF16)
        dw_ref[...] = _dot(pooled, dmixed, TN)
        dpooled = _dot(dmixed, w, NT)
        da_ref[...] = _pool_apply(dpooled, win, True).astype(BF16)

    slab = lambda off: pl.BlockSpec((S, POOL_CH), lambda g: (0, off + g))
    return pl.pallas_call(
        body, grid=(4,),
        in_specs=[slab(0), slab(4), slab(0), pl.BlockSpec((None, POOL_CH, POOL_CH), lambda g: (g, 0, 0)),
                  pl.BlockSpec((1, POOL_CH), lambda g: (0, g))],
        out_specs=[slab(0), slab(0), pl.BlockSpec((None, POOL_CH, POOL_CH), lambda g: (g, 0, 0)),
                   pl.BlockSpec((1, POOL_CH), lambda g: (0, g))],
        out_shape=[jax.ShapeDtypeStruct((S, HALF), BF16), jax.ShapeDtypeStruct((S, HALF), BF16),
                   jax.ShapeDtypeStruct((4, POOL_CH, POOL_CH), F32), jax.ShapeDtypeStruct((1, HALF), F32)],
        compiler_params=_params(("parallel",), VMEM_BIG), name="pool_bwd",
    )(z0, z0, dycat, pool_w, pool_scale)


Q_COL, K_COL, V_COL, BG_COL = 2048 // 128, 5120 // 128, 8192 // 128, 11264 // 128
SCALE = HEAD_DIM ** -0.5


def _rope_tables():
    pos = jnp.arange(S, dtype=F32)
    inv_freq = jnp.power(ROPE_THETA, -jnp.arange(0, ROT_DIM, 2, dtype=F32) / ROT_DIM)
    ang = pos[:, None] * inv_freq[None, :]
    cos, sin = jnp.cos(ang), jnp.sin(ang)
    half = ROT_DIM // 2
    zeros = jnp.zeros((S, HEAD_DIM - ROT_DIM), F32)
    c = jnp.concatenate([cos, cos, jnp.ones((S, HEAD_DIM - ROT_DIM), F32)], axis=1)
    a = jnp.concatenate([-sin, jnp.zeros((S, half), F32), zeros], axis=1)
    b = jnp.concatenate([jnp.zeros((S, half), F32), sin, zeros], axis=1)
    return c, a, b


def _rope(t, c, a, b):
    half = ROT_DIM // 2
    return t * c + pltpu.roll(t, HEAD_DIM - half, 1) * a + pltpu.roll(t, half, 1) * b


def _rope_t(d, c, a, b):
    half = ROT_DIM // 2
    return d * c + pltpu.roll(d * a, half, 1) + pltpu.roll(d * b, HEAD_DIM - half, 1)


def _deinterleave(dst, src, dil, cast=None, dst_off=0):
    length = S // dil
    for r in range(dil):
        v = src[...] if dil == 1 else src[pl.ds(r, length, stride=dil), :]
        dst[dst_off + r * length:dst_off + (r + 1) * length, :] = v if cast is None else v.astype(cast)


def _interleave(dst, src, dil, src_off=0):
    length = S // dil
    for r in range(dil):
        if dil == 1:
            dst[...] = src[src_off:src_off + S, :]
        else:
            dst[pl.ds(r, length, stride=dil), :] = src[src_off + r * length:src_off + (r + 1) * length, :]


CU = 8
NUNITS = S // BLK
B_QK = (((2,), (2,)), ((0,), (0,)))
B_PV = (((2,), (1,)), ((0,), (0,)))
B_TN = (((1,), (1,)), ((0,), (0,)))


def _blocks(ref, first):
    return ref[first * BLK:(first + CU) * BLK, :].reshape(CU, BLK, HEAD_DIM)


def _chunk_scores(u0, nb, qd, kdp):
    q = _blocks(qd, u0)
    row = lax.broadcasted_iota(jnp.int32, (CU, BLK, BLK), 1)
    col = lax.broadcasted_iota(jnp.int32, (CU, BLK, BLK), 2)
    s_own = jnp.where(col <= row, _dot(q, _blocks(kdp, u0 + 1), B_QK) * SCALE, NEG)
    if nb == 1:
        return q, s_own, None
    unit = lax.broadcasted_iota(jnp.int32, (CU, BLK, BLK), 0) + u0
    s_prev = jnp.where((col >= row) & ((unit % nb) != 0), _dot(q, _blocks(kdp, u0), B_QK) * SCALE, NEG)
    return q, s_own, s_prev


def _qkv_prep(z0, tabs):
    def body(q_ref, k_ref, v_ref, c_ref, a_ref, b_ref, qo_ref, ko_ref, vo_ref, tmp):
        p = pl.program_id(1)
        ko_ref[0:BLK, :] = jnp.zeros((BLK, HEAD_DIM), BF16)
        vo_ref[0:BLK, :] = jnp.zeros((BLK, HEAD_DIM), BF16)
        for gi, (_, dil) in enumerate(PATTERNS):
            @pl.when(p == gi)
            def _(dil=dil):
                c, a, b = c_ref[...], a_ref[...], b_ref[...]
                tmp[...] = _rope(q_ref[...], c, a, b)
                _deinterleave(qo_ref, tmp, dil, BF16)
                tmp[...] = _rope(k_ref[...], c, a, b)
                _deinterleave(ko_ref, tmp, dil, BF16, BLK)
                _deinterleave(vo_ref, v_ref, dil, BF16, BLK)

    tab = pl.BlockSpec((S, HEAD_DIM), lambda h, p: (0, 0))
    out = pl.BlockSpec((S, HEAD_DIM), lambda h, p: (0, p * 8 + h))
    outp = pl.BlockSpec((S + BLK, HEAD_DIM), lambda h, p: (0, p * 8 + h))
    return pl.pallas_call(
        body, grid=(8, 3), in_specs=[_head_spec(Q_COL), _head_spec(K_COL), _head_spec(V_COL), tab, tab, tab],
        out_specs=[out, outp, outp],
        out_shape=[jax.ShapeDtypeStruct((S, 3072), BF16)] + [jax.ShapeDtypeStruct((S + BLK, 3072), BF16)] * 2,
        scratch_shapes=[pltpu.VMEM((S, HEAD_DIM), F32)],
        compiler_params=_params(("parallel", "arbitrary"), VMEM_BIG), name="qkv_prep",
    )(z0, z0, z0, *tabs)


def _attn_group_fwd(dil, qd, kdp, vdp, od, ld, og, lg):
    nb = S // dil // BLK
    for u0 in range(0, NUNITS, CU):
        _, s_own, s_prev = _chunk_scores(u0, nb, qd, kdp)
        m = jnp.max(s_own, axis=2, keepdims=True)
        if s_prev is not None:
            m = jnp.maximum(m, jnp.max(s_prev, axis=2, keepdims=True))
        p_own = jnp.exp(s_own - m)
        den = jnp.sum(p_own, axis=2, keepdims=True)
        acc = _dot(p_own.astype(BF16), _blocks(vdp, u0 + 1), B_PV)
        if s_prev is not None:
            p_prev = jnp.exp(s_prev - m)
            den = den + jnp.sum(p_prev, axis=2, keepdims=True)
            acc = acc + _dot(p_prev.astype(BF16), _blocks(vdp, u0), B_PV)
        rows = slice(u0 * BLK, (u0 + CU) * BLK)
        od[rows, :] = (acc / den).reshape(CU * BLK, HEAD_DIM)
        ld[rows, :] = jnp.broadcast_to(m + jnp.log(den), (CU, BLK, HEAD_DIM)).reshape(CU * BLK, HEAD_DIM)
    _interleave(og, od, dil)
    _interleave(lg, ld, dil)


def _group_weights(lgs):
    l0, l1, l2 = lgs[0][...], lgs[1][...], lgs[2][...]
    mx = jnp.maximum(l0, jnp.maximum(l1, l2))
    e0, e1, e2 = jnp.exp(l0 - mx), jnp.exp(l1 - mx), jnp.exp(l2 - mx)
    den = e0 + e1 + e2
    return e0 / den, e1 / den, e2 / den


def _head_spec(base):
    return pl.BlockSpec((S, HEAD_DIM), lambda h, p: (0, base + (p % 3) * 8 + h))


def _slab(dtype=F32, rows=S):
    return pltpu.VMEM((rows, HEAD_DIM), dtype)


def _attn_fwd(z0, qkv, ycat):
    def body(q_ref, k_ref, v_ref, gate_ref, ycat_ref, out_ref, og_ref, lg_ref,
             od, ld, og0, og1, og2, lg0, lg1, lg2):
        del ycat_ref
        p = pl.program_id(1)
        ogs, lgs = (og0, og1, og2), (lg0, lg1, lg2)
        for gi, (_, dil) in enumerate(PATTERNS):
            @pl.when(p == gi)
            def _(gi=gi, dil=dil):
                _attn_group_fwd(dil, q_ref, k_ref, v_ref, od, ld, ogs[gi], lgs[gi])
                og_ref[...] = ogs[gi][...]
                lg_ref[...] = lgs[gi][...]

        @pl.when(p == 2)
        def _():
            w0, w1, w2 = _group_weights(lgs)
            o = w0 * og0[...] + w1 * og1[...] + w2 * og2[...]
            gate = gate_ref[...]
            out_ref[...] = (o * (gate * _sigmoid(gate))).astype(BF16)

    grp = pl.BlockSpec((S, HEAD_DIM), lambda h, p: (0, p * 8 + h))
    grp_pad = pl.BlockSpec((S + BLK, HEAD_DIM), lambda h, p: (0, p * 8 + h))
    return pl.pallas_call(
        body, grid=(8, 3),
        in_specs=[grp, grp_pad, grp_pad, pl.BlockSpec((S, HEAD_DIM), lambda h, p: (0, BG_COL + h)), ANY_SPEC],
        out_specs=[pl.BlockSpec((S, HEAD_DIM), lambda h, p: (0, 8 + h)), grp, grp],
        out_shape=[jax.ShapeDtypeStruct((S, 2048), BF16), jax.ShapeDtypeStruct((S, 3072), F32),
                   jax.ShapeDtypeStruct((S, 3072), F32)],
        scratch_shapes=[_slab() for _ in range(8)],
        input_output_aliases={4: 0},
        compiler_params=_params(("parallel", "arbitrary"), VMEM_BIG), name="attn_fwd",
    )(*qkv, z0, ycat)


def _attn_bwd(z0, qkv, og, lg, dycat, tabs, dep):
    def body(q_ref, k_ref, v_ref, gate_ref, dy_ref, c_ref, a_ref, b_ref,
             og0_ref, og1_ref, og2_ref, lg0_ref, lg1_ref, lg2_ref, dep_ref,
             dq_ref, dk_ref, dv_ref, dbg_ref,
             tmp, ld, dg0, dg1, dg2, cg0, cg1, cg2, dod, cd, dqd, dkd, dvd):
        kd, vd = k_ref, v_ref
        p = pl.program_id(1)
        ogs, lgs, dgs, cgs = (og0_ref, og1_ref, og2_ref), (lg0_ref, lg1_ref, lg2_ref), (dg0, dg1, dg2), (cg0, cg1, cg2)

        @pl.when(p == 0)
        def _():
            w = _group_weights(lgs)
            o = w[0] * ogs[0][...] + w[1] * ogs[1][...] + w[2] * ogs[2][...]
            silu, dsilu = _silu_and_grad(gate_ref[...])
            dy = dy_ref[...]
            dbg_ref[...] = (dy * o * dsilu).astype(BF16)
            do = dy * silu
            dwbar = jnp.sum(do * o, axis=1, keepdims=True)
            for gi in range(3):
                dgs[gi][...] = w[gi] * do
                cgs[gi][...] = -w[gi] * dwbar

        for gi, (_, dil) in enumerate(PATTERNS):
            @pl.when(p == 1 + gi)
            def _(gi=gi, dil=dil):
                nb = S // dil // BLK
                qd = q_ref
                c, a, b = c_ref[...], a_ref[...], b_ref[...]
                _deinterleave(dod, dgs[gi], dil, BF16)
                _deinterleave(ld, lgs[gi], dil)
                _deinterleave(cd, cgs[gi], dil)
                dkd[...] = jnp.zeros_like(dkd)
                dvd[...] = jnp.zeros_like(dvd)
                flat = lambda t: t.reshape(CU * BLK, HEAD_DIM)
                for u0 in range(0, NUNITS, CU):
                    q, s_own, s_prev = _chunk_scores(u0, nb, qd, kd)
                    lse, cv, do = _blocks(ld, u0), _blocks(cd, u0), _blocks(dod, u0)
                    own = slice((u0 + 1) * BLK, (u0 + 1 + CU) * BLK)
                    p_own = jnp.exp(s_own - lse)
                    ds_own = (p_own * (_dot(do, _blocks(vd, u0 + 1), B_QK) + cv) * SCALE).astype(BF16)
                    dq = _dot(ds_own, _blocks(kd, u0 + 1), B_PV)
                    dkd[own, :] += flat(_dot(ds_own, q, B_TN))
                    dvd[own, :] += flat(_dot(p_own.astype(BF16), do, B_TN))
                    if s_prev is not None:
                        prev = slice(u0 * BLK, (u0 + CU) * BLK)
                        p_prev = jnp.exp(s_prev - lse)
                        ds_prev = (p_prev * (_dot(do, _blocks(vd, u0), B_QK) + cv) * SCALE).astype(BF16)
                        dq = dq + _dot(ds_prev, _blocks(kd, u0), B_PV)
                        dkd[prev, :] += flat(_dot(ds_prev, q, B_TN))
                        dvd[prev, :] += flat(_dot(p_prev.astype(BF16), do, B_TN))
                    dqd[u0 * BLK:(u0 + CU) * BLK, :] = flat(dq)
                _interleave(tmp, dqd, dil)
                dq_ref[...] = _rope_t(tmp[...], c, a, b).astype(BF16)
                _interleave(tmp, dkd, dil, BLK)
                dk_ref[...] = _rope_t(tmp[...], c, a, b).astype(BF16)
                _interleave(tmp, dvd, dil, BLK)
                dv_ref[...] = tmp[...].astype(BF16)

    tab = pl.BlockSpec((S, HEAD_DIM), lambda h, p: (0, 0))
    hspec = lambda base: pl.BlockSpec((S, HEAD_DIM), lambda h, p: (0, base + h))
    gspec = pl.BlockSpec((S, HEAD_DIM), lambda h, p: (0, jnp.maximum(p - 1, 0) * 8 + h))
    gspec_pad = pl.BlockSpec((S + BLK, HEAD_DIM), lambda h, p: (0, jnp.maximum(p - 1, 0) * 8 + h))
    return pl.pallas_call(
        body, grid=(8, 4),
        in_specs=[gspec, gspec_pad, gspec_pad, hspec(BG_COL), hspec(8), tab, tab, tab,
                  hspec(0), hspec(8), hspec(16), hspec(0), hspec(8), hspec(16), ANY_SPEC],
        out_specs=[gspec, gspec, gspec, hspec(0)],
        out_shape=[jax.ShapeDtypeStruct((S, 3072), BF16)] * 3 + [jax.ShapeDtypeStruct((S, HALF), BF16)],
        scratch_shapes=[_slab(), _slab()] + [_slab() for _ in range(6)]
                       + [_slab(BF16), _slab(), _slab(), _slab(F32, S + BLK), _slab(F32, S + BLK)],
        compiler_params=_params(("parallel", "arbitrary"), VMEM_BIG), name="attn_bwd",
    )(*qkv, z0, dycat, *tabs, og, og, og, lg, lg, lg, dep)


SGU_CH = 256
NCHUNK = TR // 128


def _ln_stats(x):
    mu = jnp.mean(x, axis=-1, keepdims=True)
    xc = x - mu
    r = lax.rsqrt(jnp.mean(xc * xc, axis=-1, keepdims=True) + EPS)
    return xc * r, r


def _ln_bwd(dy, xhat, r, g):
    dxh = dy * g
    return r * (dxh - jnp.mean(dxh, axis=-1, keepdims=True) - xhat * jnp.mean(dxh * xhat, axis=-1, keepdims=True))


def _tril_bf16(w):
    row = lax.broadcasted_iota(jnp.int32, w.shape, 0)
    col = lax.broadcasted_iota(jnp.int32, w.shape, 1)
    return jnp.where(row >= col, w, 0.0).astype(BF16)


def _sgu_gate(vn_s, s_s, w_ref, bb_ref):
    for h in range(4):
        wm = _tril_bf16(w_ref[h])
        bias = bb_ref[h]
        for ch in range(NCHUNK):
            rows, cols = slice(ch * 128, (ch + 1) * 128), slice(h * SGU_CH, (h + 1) * SGU_CH)
            s_s[rows, cols] = _dot(wm, vn_s[rows, cols]) + jnp.concatenate([bias, bias], axis=1)


WIN = HALO + TR
SUBL = 8


def _shifted_copies(dst, src):
    dst[0] = src[...]
    for b in range(1, SUBL):
        dst[b, 0:WIN - SUBL, :] = src[pl.ds(b, WIN - SUBL), :]


def _rows_at(copies, off, n):
    return copies[off % SUBL, pl.ds(off - off % SUBL, n), :]


def _conv_fwd(i, dval_ref, dglu_ref, hval_ref, hglu_ref, cw_ref, cb_ref, xw, xr, dcs):
    halo = hval_ref[...] * _sigmoid(hglu_ref[...])
    xw[0:HALO, :] = jnp.where(i > 0, halo, 0.0)
    xw[HALO:HALO + TR, :] = dval_ref[...] * _sigmoid(dglu_ref[...])
    _shifted_copies(xr, xw)
    sub = 2 * SUB
    for rb in range(TR // sub):
        acc = jnp.broadcast_to(cb_ref[...], (sub, HALF))
        for k in range(CONV_K):
            acc = acc + cw_ref[k:k + 1, :] * _rows_at(xr, rb * sub + HALO - (CONV_K - 1) + k, sub)
        dcs[rb * sub:(rb + 1) * sub, :] = acc


def _odd_in_specs():
    col = lambda j: pl.BlockSpec((TR, HALF), lambda i, *_: (i, j))
    prev = lambda j: pl.BlockSpec((HALO, HALF), lambda i, *_: (jnp.maximum(i * (TR // HALO) - 1, 0), j))
    return [col(0), col(1), col(2), col(3), col(4), col(5), prev(3), prev(4)]


def _full_spec(shape):
    return pl.BlockSpec(shape, lambda i, *_: (0,) * len(shape))


def _odd_fwd(z1, sgu_g, sgu_b, sgu_w, sgu_bb, conv_w, conv_b, cn_g, cn_b):
    def body(u_ref, v_ref, cg_ref, dval_ref, dglu_ref, dgate_ref, hval_ref, hglu_ref,
             g_ref, b_ref, w_ref, bb_ref, cw_ref, cb_ref, cng_ref, cnb_ref, out_ref, dcs, vn_s, s_s, xw, xr):
        i = pl.program_id(0)
        vhat, _ = _ln_stats(v_ref[...])
        vn_s[...] = (vhat * g_ref[...] + b_ref[...]).astype(BF16)
        _sgu_gate(vn_s, s_s, w_ref, bb_ref)
        cg = cg_ref[...]
        out_ref[:, 0:HALF] = (u_ref[...] * s_s[...] * (cg * _sigmoid(cg))).astype(BF16)
        _conv_fwd(i, dval_ref, dglu_ref, hval_ref, hglu_ref, cw_ref, cb_ref, xw, xr, dcs)
        dhat, _ = _ln_stats(dcs[...])
        dn = dhat * cng_ref[...] + cnb_ref[...]
        dgate = dgate_ref[...]
        out_ref[:, HALF:2 * HALF] = ((dn * _sigmoid(dn)) * (dgate * _sigmoid(dgate))).astype(BF16)

    vec = _full_spec((1, HALF))
    return pl.pallas_call(
        body, grid=(S // TR,),
        in_specs=_odd_in_specs() + [vec, vec, _full_spec((4, 128, 128)), _full_spec((4, 128, 128)),
                                    _full_spec((HALO, HALF)), vec, vec, vec],
        out_specs=[pl.BlockSpec((TR, 2048), lambda i: (i, 0)), pl.BlockSpec((TR, HALF), lambda i: (i, 0))],
        out_shape=[jax.ShapeDtypeStruct((S, 2048), BF16), jax.ShapeDtypeStruct((S, HALF), F32)],
        scratch_shapes=[pltpu.VMEM((TR, HALF), BF16), pltpu.VMEM((TR, HALF), F32),
                        pltpu.VMEM((WIN, HALF), F32), pltpu.VMEM((SUBL, WIN, HALF), F32)],
        compiler_params=_params(("parallel",), VMEM_BIG), name="odd_fwd",
    )(z1, z1, z1, z1, z1, z1, z1, z1, sgu_g, sgu_b, sgu_w, sgu_bb, conv_w, conv_b, cn_g, cn_b)


def _odd_bwd_a(z1, dc, dycat, sgu_g, sgu_b, sgu_w, sgu_bb, cn_g, cn_b):
    def body(u_ref, v_ref, cg_ref, dgate_ref, dcs, dy_ref, g_ref, b_ref, w_ref, bb_ref, cng_ref, cnb_ref,
             dz_ref, ddc_ref, dw_ref, dbb_ref, dg_ref, db_ref, dcng_ref, dcnb_ref, dcb_ref,
             vn_s, s_s, ds_s, dvn_s):
        i = pl.program_id(0)
        vhat, rv = _ln_stats(v_ref[...])
        g = g_ref[...]
        vn_s[...] = (vhat * g + b_ref[...]).astype(BF16)
        _sgu_gate(vn_s, s_s, w_ref, bb_ref)
        silu_c, dsilu_c = _silu_and_grad(cg_ref[...])
        dyc = dy_ref[:, 0:HALF]
        u = u_ref[...]
        s = s_s[...]
        dz_ref[:, 0:HALF] = (dyc * s * silu_c).astype(BF16)
        dz_ref[:, 2 * HALF:3 * HALF] = (dyc * u * s * dsilu_c).astype(BF16)
        ds_s[...] = dyc * u * silu_c

        @pl.when(i == 0)
        def _():
            dw_ref[...] = jnp.zeros_like(dw_ref)
            dbb_ref[...] = jnp.zeros_like(dbb_ref)

        tril = lax.broadcasted_iota(jnp.int32, (128, 128), 0) >= lax.broadcasted_iota(jnp.int32, (128, 128), 1)
        for h in range(4):
            wm = _tril_bf16(w_ref[h])
            for ch in range(NCHUNK):
                rows, cols = slice(ch * 128, (ch + 1) * 128), slice(h * SGU_CH, (h + 1) * SGU_CH)
                ds = ds_s[rows, cols]
                dsb = ds.astype(BF16)
                dw_ref[h] += jnp.where(tril, _dot(dsb, vn_s[rows, cols], NT), 0.0)
                dbb_ref[h] += jnp.broadcast_to(jnp.sum(ds, axis=1, keepdims=True), (128, 128))
                dvn_s[rows, cols] = _dot(wm, dsb, TN)
        dvn = dvn_s[...]
        _acc_rows(dg_ref, dvn * vhat, i)
        _acc_rows(db_ref, dvn, i)
        dz_ref[:, HALF:2 * HALF] = _ln_bwd(dvn, vhat, rv, g).astype(BF16)

        dhat, rd = _ln_stats(dcs[...])
        cng = cng_ref[...]
        silu_n, dsilu_n = _silu_and_grad(dhat * cng + cnb_ref[...])
        silu_g, dsilu_g = _silu_and_grad(dgate_ref[...])
        dyd = dy_ref[:, HALF:2 * HALF]
        dz_ref[:, 5 * HALF:6 * HALF] = (dyd * silu_n * dsilu_g).astype(BF16)
        ddn = dyd * silu_g * dsilu_n
        _acc_rows(dcng_ref, ddn * dhat, i)
        _acc_rows(dcnb_ref, ddn, i)
        ddc = _ln_bwd(ddn, dhat, rd, cng)
        ddc_ref[...] = ddc
        _acc_rows(dcb_ref, ddc, i)

    vec = _full_spec((1, HALF))
    sq = _full_spec((4, 128, 128))
    col = lambda j: pl.BlockSpec((TR, HALF), lambda i: (i, j))
    return pl.pallas_call(
        body, grid=(S // TR,),
        in_specs=[col(0), col(1), col(2), col(5), col(0), pl.BlockSpec((TR, 2048), lambda i: (i, 0)),
                  vec, vec, sq, sq, vec, vec],
        out_specs=[pl.BlockSpec((TR, ODD_IN), lambda i: (i, 0)), pl.BlockSpec((TR, HALF), lambda i: (i, 0)),
                   sq, sq, vec, vec, vec, vec, vec],
        out_shape=[jax.ShapeDtypeStruct((S, ODD_IN), BF16), jax.ShapeDtypeStruct((S, HALF), F32),
                   jax.ShapeDtypeStruct((4, 128, 128), F32), jax.ShapeDtypeStruct((4, 128, 128), F32)]
                  + [jax.ShapeDtypeStruct((1, HALF), F32)] * 5,
        scratch_shapes=[pltpu.VMEM((TR, HALF), BF16), pltpu.VMEM((TR, HALF), F32),
                        pltpu.VMEM((TR, HALF), F32), pltpu.VMEM((TR, HALF), F32)],
        compiler_params=_params(("arbitrary",), VMEM_BIG), name="odd_bwd_a",
    )(z1, z1, z1, z1, dc, dycat, sgu_g, sgu_b, sgu_w, sgu_bb, cn_g, cn_b)


def _odd_bwd_b(z1, ddc, dz1, conv_w):
    nt = S // TR

    def body(dval_ref, dglu_ref, hval_ref, hglu_ref, ddc_ref, hddc_ref, cw_ref, dz_in_ref,
             dz_ref, dcw_ref, xw, dwin, dxs, xr, dr):
        del dz_in_ref
        i, j = pl.program_id(0), pl.program_id(1)
        sg = _sigmoid(dglu_ref[...])
        dval = dval_ref[...]

        @pl.when(j == 0)
        def _():
            halo = hval_ref[...] * _sigmoid(hglu_ref[...])
            xw[0:HALO, :] = jnp.where(i > 0, halo, 0.0)
            xw[HALO:HALO + TR, :] = dval * sg
            dwin[0:TR, :] = ddc_ref[...]
            dwin[TR:TR + HALO, :] = jnp.where(i < nt - 1, hddc_ref[...], 0.0)
            _shifted_copies(xr, xw)
            _shifted_copies(dr, dwin)

            @pl.when(i == 0)
            def _():
                dcw_ref[...] = jnp.zeros_like(dcw_ref)

            for rb in range(TR // SUB):
                acc = jnp.zeros((SUB, HALF), F32)
                for k in range(CONV_K):
                    acc = acc + cw_ref[k:k + 1, :] * _rows_at(dr, rb * SUB + (CONV_K - 1) - k, SUB)
                dxs[rb * SUB:(rb + 1) * SUB, :] = acc
            for k in range(CONV_K):
                acc = jnp.zeros((SUB, HALF), F32)
                for rb in range(TR // SUB):
                    acc = acc + dwin[rb * SUB:(rb + 1) * SUB, :] * _rows_at(xr, rb * SUB + HALO - (CONV_K - 1) + k, SUB)
                dcw_ref[k:k + 1, :] += jnp.sum(acc, axis=0, keepdims=True)
            dz_ref[...] = (dxs[...] * sg).astype(BF16)

        @pl.when(j == 1)
        def _():
            dz_ref[...] = (dxs[...] * dval * sg * (1.0 - sg)).astype(BF16)

    col = lambda c: pl.BlockSpec((TR, HALF), lambda i, j: (i, c))
    prev = lambda c: pl.BlockSpec((HALO, HALF), lambda i, j: (jnp.maximum(i * (TR // HALO) - 1, 0), c))
    nxt = pl.BlockSpec((HALO, HALF), lambda i, j: (jnp.minimum((i + 1) * (TR // HALO), S // HALO - 1), 0))
    return pl.pallas_call(
        body, grid=(nt, 2),
        in_specs=[col(3), col(4), prev(3), prev(4), pl.BlockSpec((TR, HALF), lambda i, j: (i, 0)), nxt,
                  _full_spec((HALO, HALF)), pl.BlockSpec(memory_space=pl.ANY)],
        out_specs=[pl.BlockSpec((TR, HALF), lambda i, j: (i, 3 + j)), _full_spec((HALO, HALF))],
        out_shape=[jax.ShapeDtypeStruct((S, ODD_IN), BF16), jax.ShapeDtypeStruct((HALO, HALF), F32)],
        scratch_shapes=[pltpu.VMEM((WIN, HALF), F32), pltpu.VMEM((WIN, HALF), F32), pltpu.VMEM((TR, HALF), F32),
                        pltpu.VMEM((SUBL, WIN, HALF), F32), pltpu.VMEM((SUBL, WIN, HALF), F32)],
        input_output_aliases={7: 0},
        compiler_params=_params(("arbitrary", "arbitrary"), VMEM_BIG), name="odd_bwd_b",
    )(z1, z1, z1, z1, ddc, ddc, conv_w, dz1)


def _cast_bf16(w, name, piece=0, npieces=1):
    r, c = w.shape[0], w.shape[1] // npieces
    tr = min(r, 256)

    def body(i_ref, o_ref):
        o_ref[...] = i_ref[...].astype(BF16)

    return pl.pallas_call(
        body, grid=(r // tr,), in_specs=[pl.BlockSpec((tr, c), lambda i: (i, piece))],
        out_specs=pl.BlockSpec((tr, c), lambda i: (i, 0)), out_shape=jax.ShapeDtypeStruct((r, c), BF16),
        compiler_params=_params(("parallel",)), name=name,
    )(w)


def _adamw(w, g, m, v):
    m = ADAM_B1 * m + (1.0 - ADAM_B1) * g
    v = ADAM_B2 * v + (1.0 - ADAM_B2) * (g * g)
    m_hat = m / (1.0 - ADAM_B1 ** ADAM_STEP)
    v_hat = v / (1.0 - ADAM_B2 ** ADAM_STEP)
    delta = -ADAM_LR * (m_hat / (jnp.sqrt(v_hat) + ADAM_EPS) + ADAM_WD * w)
    return delta, m, v


def _adam_reduce(parts, w, m, v, name, dep=None, piece=0, npieces=1, prev=None):
    r, c = w.shape
    cp = c // npieces
    tr = min(r, 128)
    extra = ([] if dep is None else [dep]) + ([] if prev is None else list(prev))
    nparts = parts.shape[0]

    def body(p_ref, w_ref, m_ref, v_ref, *rest):
        g_ref, d_ref, nm_ref, nv_ref = rest[len(extra):]
        g = p_ref[0].astype(F32)
        for d in range(1, nparts):
            g = g + p_ref[d].astype(F32)
        g_ref[...] = g
        d_ref[...], nm_ref[...], nv_ref[...] = _adamw(w_ref[...], g, m_ref[...], v_ref[...])

    spec = pl.BlockSpec((tr, cp), lambda i: (i, piece))
    first = 4 + (0 if dep is None else 1)
    return pl.pallas_call(
        body, grid=(r // tr,),
        in_specs=[pl.BlockSpec((nparts, tr, cp), lambda i: (0, i, 0)), spec, spec, spec] + [ANY_SPEC] * len(extra),
        out_specs=[spec] * 4, out_shape=[jax.ShapeDtypeStruct((r, c), F32)] * 4,
        input_output_aliases={} if prev is None else {first + k: k for k in range(4)},
        compiler_params=_params(("parallel",), VMEM_BIG), name=name,
    )(parts, w, m, v, *extra)


def _arrived(x, name, dep=None):
    deps = [] if dep is None else [dep]

    def body(*refs):
        refs[-1][...] = jnp.zeros_like(refs[-1])

    return pl.pallas_call(
        body, in_specs=[ANY_SPEC] * (1 + len(deps)), out_specs=pl.BlockSpec(memory_space=pltpu.VMEM),
        out_shape=jax.ShapeDtypeStruct((8, 128), F32), name=name,
    )(x, *deps)


def _sum_parts(parts, name, dep=None):
    r = parts.shape[1]
    tr = 8
    for cand in (512, 256, 128, 64, 32, 16, 8):
        if r % cand == 0:
            tr = cand
            break
    deps = [] if dep is None else [dep]

    def body(p_ref, *rest):
        g = p_ref[0]
        for d in range(1, NDEV):
            g = g + p_ref[d]
        rest[-1][...] = g

    return pl.pallas_call(
        body, grid=(r // tr,), in_specs=[pl.BlockSpec((NDEV, tr, 128), lambda i: (0, i, 0))] + [ANY_SPEC] * len(deps),
        out_specs=pl.BlockSpec((tr, 128), lambda i: (i, 0)), out_shape=jax.ShapeDtypeStruct((r, 128), F32),
        compiler_params=_params(("parallel",)), name=name,
    )(parts, *deps)


def _sum_unpack(parts, rows, name, dep=None):
    deps = [] if dep is None else [dep]

    def body(p_ref, *outs):
        outs = outs[len(deps):]
        off = 0
        for o_ref, n in zip(outs, rows):
            acc = p_ref[0, off:off + n, :]
            for d in range(1, NDEV):
                acc = acc + p_ref[d, off:off + n, :]
            o_ref[...] = acc
            off += n

    return pl.pallas_call(
        body, grid=(1,), in_specs=[pl.BlockSpec(parts.shape, lambda i: (0, 0, 0))] + [ANY_SPEC] * len(deps),
        out_specs=[pl.BlockSpec((n, 128), lambda i: (0, 0)) for n in rows],
        out_shape=[jax.ShapeDtypeStruct((n, 128), F32) for n in rows],
        compiler_params=_params(("arbitrary",), VMEM_BIG), name=name,
    )(parts, *deps)


def _adam_small(ws, gs, g_specs, ms, vs, name):
    n = len(ws)

    def body(*refs):
        w_r, g_r, m_r, v_r = refs[:n], refs[n:2 * n], refs[2 * n:3 * n], refs[3 * n:4 * n]
        outs = refs[4 * n:]
        for i in range(n):
            g = g_r[i][...]
            outs[4 * i][...] = g
            outs[4 * i + 1][...], outs[4 * i + 2][...], outs[4 * i + 3][...] = _adamw(
                w_r[i][...], g, m_r[i][...], v_r[i][...])

    whole = lambda a: pl.BlockSpec(a.shape, lambda i, nd=a.ndim: (0,) * nd)
    outs = pl.pallas_call(
        body, grid=(1,),
        in_specs=[whole(a) for a in ws] + list(g_specs) + [whole(a) for a in ms] + [whole(a) for a in vs],
        out_specs=[whole(a) for a in ws for _ in range(4)],
        out_shape=[jax.ShapeDtypeStruct(a.shape, F32) for a in ws for _ in range(4)],
        compiler_params=_params(("arbitrary",), VMEM_BIG), name=name,
    )(*ws, *gs, *ms, *vs)
    return [outs[4 * i:4 * i + 4] for i in range(n)]


MASKS = [(mx, my, mc) for mx in (0, 1) for my in (0, 1) for mc in (0, 1)][1:]


def _sc_exchange(name, collective_id, arrays, scatter):
    nt = len(arrays)
    out_type = [jax.ShapeDtypeStruct(a.shape if scatter else (NDEV,) + a.shape, a.dtype) for a in arrays]

    def body(*refs):
        ins, outs = refs[:nt], refs[nt:2 * nt]
        send_sems, recv_sems, local_sems = refs[2 * nt:3 * nt], refs[3 * nt:4 * nt], refs[4 * nt:5 * nt]
        x, y, c = lax.axis_index("x"), lax.axis_index("y"), lax.axis_index("c")
        peers = [(mx + x - 2 * mx * x, my + y - 2 * my * y, mc + c - 2 * mc * c) for mx, my, mc in MASKS]
        barrier = pltpu.get_barrier_semaphore()
        for peer in peers:
            pl.semaphore_signal(barrier, inc=1, device_id=peer, device_id_type=MESH)
        pl.semaphore_wait(barrier, len(peers))
        me = 4 * x + 2 * y + c
        own = []
        for t in range(nt):
            cp = pltpu.make_async_copy(ins[t].at[me] if scatter else ins[t], outs[t].at[me], local_sems[t])
            cp.start()
            own.append(cp)
            for px, py, pc in peers:
                src = ins[t].at[4 * px + 2 * py + pc] if scatter else ins[t]
                pltpu.make_async_remote_copy(src_ref=src, dst_ref=outs[t].at[me], send_sem=send_sems[t],
                                             recv_sem=recv_sems[t], device_id=(px, py, pc), device_id_type=MESH).start()
        for t in range(nt):
            own[t].wait()
            seven = outs[t].at[pl.ds(0, NDEV - 1)]
            drain = pltpu.make_async_remote_copy(src_ref=seven, dst_ref=seven, send_sem=send_sems[t],
                                                 recv_sem=recv_sems[t], device_id=(x, y, c), device_id_type=MESH)
            drain.wait_send()
            drain.wait_recv()

    return pl.kernel(
        body, out_type=out_type, mesh=plsc.ScalarSubcoreMesh(axis_name="sequencer", num_cores=1),
        scratch_types=[pltpu.SemaphoreType.DMA] * (3 * nt),
        compiler_params=pltpu.CompilerParams(collective_id=collective_id), name=name,
    )(*arrays)


def _sc_gather_two_level(name, collective_id, arrays):
    nt = len(arrays)
    out_type = [jax.ShapeDtypeStruct((NDEV,) + a.shape, a.dtype) for a in arrays]

    def body(*refs):
        ins, outs = refs[:nt], refs[nt:2 * nt]
        sems = refs[2 * nt:]
        send_sems, sib_sems, local_sems = sems[:nt], sems[nt:2 * nt], sems[2 * nt:3 * nt]
        ici_sems = [sems[3 * nt + 3 * t:3 * nt + 3 * t + 3] for t in range(nt)]
        x, y, c = lax.axis_index("x"), lax.axis_index("y"), lax.axis_index("c")
        sibling = (x, y, 1 - c)
        chips = [(1 - x, y), (x, 1 - y), (1 - x, 1 - y)]
        barrier = pltpu.get_barrier_semaphore()
        for peer in [sibling] + [(cx, cy, c) for cx, cy in chips]:
            pl.semaphore_signal(barrier, inc=1, device_id=peer, device_id_type=MESH)
        pl.semaphore_wait(barrier, 4)
        me = 4 * x + 2 * y + c

        def push(t, src, slot, recv_sem, to):
            pltpu.make_async_remote_copy(src_ref=src, dst_ref=outs[t].at[slot], send_sem=send_sems[t],
                                         recv_sem=recv_sem, device_id=to, device_id_type=MESH).start()

        own = []
        for t in range(nt):
            cp = pltpu.make_async_copy(ins[t], outs[t].at[me], local_sems[t])
            cp.start()
            own.append(cp)
            for j, (cx, cy) in enumerate(chips):
                push(t, ins[t], me, ici_sems[t][j], (cx, cy, c))
            push(t, ins[t], me, sib_sems[t], sibling)
        for t in range(nt):
            for j, (cx, cy) in enumerate(chips):
                slot = 4 * cx + 2 * cy + c
                landed = outs[t].at[slot]
                pltpu.make_async_remote_copy(src_ref=landed, dst_ref=landed, send_sem=send_sems[t],
                                             recv_sem=ici_sems[t][j], device_id=(cx, cy, c),
                                             device_id_type=MESH).wait_recv()
                push(t, landed, slot, sib_sems[t], sibling)
        for t in range(nt):
            own[t].wait()
            four, seven = outs[t].at[pl.ds(0, 4)], outs[t].at[pl.ds(0, 7)]
            pltpu.make_async_remote_copy(src_ref=four, dst_ref=four, send_sem=send_sems[t], recv_sem=sib_sems[t],
                                         device_id=sibling, device_id_type=MESH).wait_recv()
            pltpu.make_async_remote_copy(src_ref=seven, dst_ref=seven, send_sem=send_sems[t], recv_sem=sib_sems[t],
                                         device_id=sibling, device_id_type=MESH).wait_send()

    return pl.kernel(
        body, out_type=out_type, mesh=plsc.ScalarSubcoreMesh(axis_name="sequencer", num_cores=1),
        scratch_types=[pltpu.SemaphoreType.DMA] * (6 * nt),
        compiler_params=pltpu.CompilerParams(collective_id=collective_id), name=name,
    )(*arrays)


def _sc_sibling_exchange(name, collective_id, src, out_shape, pieces):
    def body(src_ref, out_ref, send_sem, recv_sem):
        x, y, c = lax.axis_index("x"), lax.axis_index("y"), lax.axis_index("c")
        sibling = (x, y, 1 - c)
        barrier = pltpu.get_barrier_semaphore()
        pl.semaphore_signal(barrier, inc=1, device_id=sibling, device_id_type=MESH)
        pl.semaphore_wait(barrier, 1)
        for piece, lands in pieces(c, src_ref, out_ref):
            pltpu.make_async_remote_copy(src_ref=piece, dst_ref=lands, send_sem=send_sem, recv_sem=recv_sem,
                                         device_id=sibling, device_id_type=MESH).start()
        drain = pltpu.make_async_remote_copy(src_ref=out_ref, dst_ref=out_ref, send_sem=send_sem, recv_sem=recv_sem,
                                             device_id=sibling, device_id_type=MESH)
        drain.wait_send()
        drain.wait_recv()

    return pl.kernel(
        body, out_type=jax.ShapeDtypeStruct(out_shape, src.dtype),
        mesh=plsc.ScalarSubcoreMesh(axis_name="sequencer", num_cores=1), scratch_types=[pltpu.SemaphoreType.DMA] * 2,
        compiler_params=pltpu.CompilerParams(collective_id=collective_id), name=name,
    )(src)


def _swap_class_columns(name, collective_id, dz, nb, piece=0, npieces=1):
    w = nb // npieces
    return _sc_sibling_exchange(
        name, collective_id, dz, (S, 4 * w),
        lambda c, src, out: [(src.at[:, pl.ds((2 * j + 1 - c) * nb + piece * w, w)], out.at[:, pl.ds(j * w, w)])
                             for j in range(4)])


def _sc_chip_scatter(name, collective_id, q):
    def body(q_ref, out_ref, send_sem, recv_sem, local_sem):
        x, y, c = lax.axis_index("x"), lax.axis_index("y"), lax.axis_index("c")
        chips = [(1 - x, y), (x, 1 - y), (1 - x, 1 - y)]
        barrier = pltpu.get_barrier_semaphore()
        for cx, cy in chips:
            pl.semaphore_signal(barrier, inc=1, device_id=(cx, cy, c), device_id_type=MESH)
        pl.semaphore_wait(barrier, 3)
        mine = 2 * x + y
        own = pltpu.make_async_copy(q_ref.at[mine], out_ref.at[mine], local_sem)
        own.start()
        for cx, cy in chips:
            pltpu.make_async_remote_copy(src_ref=q_ref.at[2 * cx + cy], dst_ref=out_ref.at[mine], send_sem=send_sem,
                                         recv_sem=recv_sem, device_id=(cx, cy, c), device_id_type=MESH).start()
        own.wait()
        three = out_ref.at[pl.ds(0, 3)]
        drain = pltpu.make_async_remote_copy(src_ref=three, dst_ref=three, send_sem=send_sem, recv_sem=recv_sem,
                                             device_id=(x, y, c), device_id_type=MESH)
        drain.wait_send()
        drain.wait_recv()

    return pl.kernel(
        body, out_type=jax.ShapeDtypeStruct(q.shape, q.dtype),
        mesh=plsc.ScalarSubcoreMesh(axis_name="sequencer", num_cores=1), scratch_types=[pltpu.SemaphoreType.DMA] * 3,
        compiler_params=pltpu.CompilerParams(collective_id=collective_id), name=name,
    )(q)


def _mm_pair_dw(h_own, dz, h_sib, dz_sib, nb, name, dep=None, piece=0, npieces=1, h_transposed=False,
                one_call=False):
    nb = nb // npieces
    tn = 512 if nb % 512 == 0 else nb
    per = nb // tn
    dn = NN if h_transposed else TN
    o_spec = pl.BlockSpec((None, D, tn), lambda i, j, k: (j // per, 0, j % per))
    own_col = lambda i, j, k: (0, ((2 * (j // per) + lax.axis_index("c")) * npieces + piece) * per + j % per)
    if one_call:
        def fused(a0_ref, b0_ref, a1_ref, b1_ref, dep_ref, o_ref):
            acc = _dot(a0_ref[...], b0_ref[...], dn) + _dot(a1_ref[...], b1_ref[...], dn)
            o_ref[...] = acc.astype(BF16)

        whole = pl.BlockSpec((S, D), lambda i, j, k: (0, 0), pipeline_mode=pl.Buffered(1))
        return pl.pallas_call(
            fused, grid=(1, 4 * per, 1),
            in_specs=[whole, pl.BlockSpec((S, tn), own_col), whole, pl.BlockSpec((S, tn), lambda i, j, k: (0, j)),
                      ANY_SPEC],
            out_specs=o_spec, out_shape=jax.ShapeDtypeStruct((4, D, nb), BF16),
            compiler_params=_params(("parallel", "parallel", "arbitrary"), VMEM_BIG), name=name,
        )(h_own, dz, h_sib, dz_sib, dep)
    part = _matmul(
        h_own, dz, dn=dn, grid=(1, 4 * per, 1),
        a_spec=pl.BlockSpec((S, D), lambda i, j, k: (0, 0)), b_spec=pl.BlockSpec((S, tn), own_col),
        o_spec=o_spec, out_shape=(4, D, nb), out_dtype=F32, acc_shape=(D, tn), name=name + "_own", dep=dep)

    def body(a_ref, b_ref, p_ref, o_ref):
        o_ref[...] = (p_ref[...] + _dot(a_ref[...], b_ref[...], dn)).astype(BF16)

    return pl.pallas_call(
        body, grid=(1, 4 * per, 1),
        in_specs=[pl.BlockSpec((S, D), lambda i, j, k: (0, 0)), pl.BlockSpec((S, tn), lambda i, j, k: (0, j)), o_spec],
        out_specs=o_spec, out_shape=jax.ShapeDtypeStruct((4, D, nb), BF16),
        compiler_params=_params(("parallel", "parallel", "arbitrary"), VMEM_BIG), name=name + "_sibling",
    )(h_sib, dz_sib, part)


SMALL = {
    "e_pre_norm": ((2048,), None), "e_pool_w": ((4, 256, 256), 1), "e_pool_scale": ((1024,), None),
    "e_post_norm": ((2048,), None), "o_pre_norm": ((2048,), 0), "o_sgu_norm_g": ((1024,), 0),
    "o_sgu_norm_b": ((1024,), 0), "o_sgu_w": ((4, 128, 128), None), "o_sgu_b": ((4, 128), None),
    "o_conv_w": ((31, 1024), 1), "o_conv_b": ((1024,), 0), "o_conv_norm_g": ((1024,), 0),
    "o_conv_norm_b": ((1024,), 0), "o_post_norm": ((2048,), 0),
}
SMALL_SHARDED = [n for n, (_, ax) in SMALL.items() if ax is not None]


def _shard_shape(name):
    shape, ax = SMALL[name]
    if ax is None:
        return shape
    return tuple(s // NDEV if i == ax else s for i, s in enumerate(shape))


def _pack(arrs, row_multiple=1):
    flat = jnp.concatenate([a.reshape(-1) for a in arrs])
    pad = -flat.shape[0] % (128 * row_multiple)
    return jnp.concatenate([flat, jnp.zeros((pad,), F32)]).reshape(-1, 128)


def _small_views(name):
    shape, ax = SMALL[name]
    me = lambda: 4 * lax.axis_index("x") + 2 * lax.axis_index("y") + lax.axis_index("c")
    if ax is None:
        view = (int(np.prod(shape)) // 128, 128)
        return view, view, pl.BlockSpec(view, lambda i: (0, 0))
    if len(shape) == 1:
        n = shape[0] // NDEV
        return (1, n), (NDEV, 1, n), pl.BlockSpec((None, 1, n), lambda i: (me(), 0, 0))
    part = _shard_shape(name)
    return part, shape, pl.BlockSpec(part, lambda i: tuple(me() if d == ax else 0 for d in range(len(shape))))


WEIGHTS = ["e_pre_norm", "e_w_in", "e_pool_w", "e_pool_scale", "e_w_out", "e_post_norm", "o_pre_norm", "o_w_in",
           "o_sgu_norm_g", "o_sgu_norm_b", "o_sgu_w", "o_sgu_b", "o_conv_w", "o_conv_b", "o_conv_norm_g",
           "o_conv_norm_b", "o_w_out", "o_post_norm"]


def kernel(x, e_pre_norm, e_w_in, e_pool_w, e_pool_scale, e_w_out, e_post_norm, o_pre_norm, o_w_in, o_sgu_norm_g, o_sgu_norm_b, o_sgu_w, o_sgu_b, o_conv_w, o_conv_b, o_conv_norm_g, o_conv_norm_b, o_w_out, o_post_norm, loss_target, m_e_pre_norm, m_e_w_in, m_e_pool_w, m_e_pool_scale, m_e_w_out, m_e_post_norm, m_o_pre_norm, m_o_w_in, m_o_sgu_norm_g, m_o_sgu_norm_b, m_o_sgu_w, m_o_sgu_b, m_o_conv_w, m_o_conv_b, m_o_conv_norm_g, m_o_conv_norm_b, m_o_w_out, m_o_post_norm, v_e_pre_norm, v_e_w_in, v_e_pool_w, v_e_pool_scale, v_e_w_out, v_e_post_norm, v_o_pre_norm, v_o_w_in, v_o_sgu_norm_g, v_o_sgu_norm_b, v_o_sgu_w, v_o_sgu_b, v_o_conv_w, v_o_conv_b, v_o_conv_norm_g, v_o_conv_norm_b, v_o_w_out, v_o_post_norm):
    given = dict(locals())
    w = {n: given[n][0] for n in WEIGHTS}
    m = {n: given["m_" + n][0] for n in WEIGHTS}
    v = {n: given["v_" + n][0] for n in WEIGHTS}
    me = 4 * lax.axis_index("x") + 2 * lax.axis_index("y") + lax.axis_index("c")
    x, target = x[0], loss_target[0]
    row = lambda a: a.reshape(1, -1)

    lo, small_rows = _sc_gather_two_level(
        "gather_a0", 0, [_cast_bf16(w["e_w_in"], "cast_e_w_in_0", 0, 2), _pack([w[n] for n in SMALL_SHARDED])])
    hi, = _sc_gather_two_level("gather_a1", 12, [_cast_bf16(w["e_w_in"], "cast_e_w_in_1", 1, 2)])
    wg_e_in = (lo, hi)
    h0, h0t = _pre0_fwd(x, row(w["e_pre_norm"]))
    wg_e_out, = _sc_gather_two_level("gather_b", 1, [_cast_bf16(w["e_w_out"], "cast_e_w_out")])
    wg_o_in, wg_o_out = _sc_gather_two_level(
        "gather_c", 13, [_cast_bf16(w[n], "cast_" + n) for n in ("o_w_in", "o_w_out")])
    h0t_sib = _sc_sibling_exchange("swap_h0", 8, h0t, h0t.shape, lambda c, src, out: [(src, out)])
    p = {n: w[n] for n in SMALL if SMALL[n][1] is None}
    small_rows = small_rows.reshape(NDEV, -1)
    off = 0
    for n in SMALL_SHARDED:
        shp, ax = _shard_shape(n), SMALL[n][1]
        cnt = int(np.prod(shp))
        blk = small_rows[:, off:off + cnt].reshape((NDEV,) + shp)
        p[n] = jnp.moveaxis(blk, 0, ax).reshape(SMALL[n][0])
        off += cnt
    tabs = _rope_tables()
    pool_w_bf = p["e_pool_w"].astype(BF16)
    sgu_bb = jnp.broadcast_to(p["o_sgu_b"][:, :, None], (4, 128, 128))
    conv_w = jnp.concatenate([p["o_conv_w"], jnp.zeros((HALO - CONV_K, HALF), F32)], axis=0)
    odd_p = (row(p["o_sgu_norm_g"]), row(p["o_sgu_norm_b"]), p["o_sgu_w"], sgu_bb, conv_w,
             row(p["o_conv_b"]), row(p["o_conv_norm_g"]), row(p["o_conv_norm_b"]))

    z0 = _mm_in_halves(h0, wg_e_in, "mm_z0")
    ycat0 = _pool_fwd(z0, pool_w_bf, row(p["e_pool_scale"]))
    qkv = _qkv_prep(z0, tabs)
    ycat0, og, lg = _attn_fwd(z0, qkv, ycat0)
    w_out_e, w_out_o = wg_e_out.reshape(2048, D), wg_o_out.reshape(2048, D)
    y0, x1, h1 = _post0_fwd(ycat0, w_out_e, x, row(p["e_post_norm"]), row(p["o_pre_norm"]), h0t_sib)
    h1_sib = _sc_sibling_exchange("swap_h1", 11, h1, h1.shape, lambda c, src, out: [(src, out)])
    z1 = _mm_in(h1, wg_o_in, "mm_z1")
    ycat1, conv_out = _odd_fwd(z1, *odd_p)

    g = {}
    loss_part, dx2, dy1, g["o_post_norm"] = _post1_bwd(ycat1, w_out_o, x1, target, row(p["o_post_norm"]), h1_sib)
    parts = {}
    dw = _mm_out_dw(ycat1, dy1, "mm_dwout1").reshape(NDEV, 256, D)
    parts["o_w_out"], = _sc_exchange("scatter_o_w_out", 2, [dw], True)
    dycat1 = _mm_out_dx(dy1, w_out_o, "mm_dycat1", dw)
    dz1, ddc, g["o_sgu_w"], d_sgu_bb, g["o_sgu_norm_g"], g["o_sgu_norm_b"], g["o_conv_norm_g"], \
        g["o_conv_norm_b"], g["o_conv_b"] = _odd_bwd_a(z1, conv_out, dycat1, *odd_p[:4], *odd_p[6:])
    dz1, d_conv_w = _odd_bwd_b(z1, ddc, dz1, conv_w)
    g["o_sgu_b"] = d_sgu_bb[:, :, 0]
    g["o_conv_w"] = d_conv_w[:CONV_K]
    grads, deltas, new_m, new_v = {}, {}, {}, {}

    def adam(n, dep):
        grads[n], deltas[n], new_m[n], new_v[n] = _adam_reduce(parts[n], w[n], m[n], v[n], "adam_" + n, dep)
        return new_v[n]

    pin = _arrived(parts["o_w_out"], "arrived_o_w_out", d_conv_w)
    dz1_sib = _swap_class_columns("swap_dz1", 10, dz1, ODD_IN // NDEV)
    dw = _mm_pair_dw(h1, dz1, h1_sib, dz1_sib, ODD_IN // NDEV, "mm_dwin1", pin)
    parts["o_w_in"] = _sc_chip_scatter("scatter_o_w_in", 3, dw)
    dh1 = _mm_in_dx(dz1, wg_o_in, "mm_dh1", dw)
    dx1, dy0, g["o_pre_norm"], g["e_post_norm"] = _mid_bwd(dx2, dh1, x1, y0, row(p["o_pre_norm"]),
                                                           row(p["e_post_norm"]))
    dw = _mm_out_dw(ycat0, dy0, "mm_dwout0").reshape(NDEV, 256, D)
    parts["e_w_out"], = _sc_exchange("scatter_e_w_out", 4, [dw], True)
    dycat0 = _mm_out_dx(dy0, w_out_e, "mm_dycat0", dw)
    da_in, da_gate, g["e_pool_w"], g["e_pool_scale"] = _pool_bwd(z0, dycat0, pool_w_bf, row(p["e_pool_scale"]))
    late = [n for n in SMALL if n not in ("e_pre_norm", "o_sgu_b")] + ["o_sgu_b"]
    pieces = [g[n].reshape(SMALL[n][0]) for n in late[:-1]] + [jnp.broadcast_to(loss_part, (8, 128)), g[late[-1]]]
    recv_small, = _sc_gather_two_level("gather_small_grads", 6, [_pack(pieces, 512)])
    took = _arrived(parts["o_w_in"], "arrived_o_w_in")
    dq, dk, dv, dbg = _attn_bwd(z0, qkv, og, lg, dycat0, tabs, took)
    dz0 = jnp.concatenate([da_in, da_gate, dq, dk, dv, dbg], axis=1)
    took = _arrived(recv_small, "arrived_small_grads", _arrived(parts["e_w_out"], "arrived_e_w_out", dz0))
    nb = EVEN_IN // NDEV
    swapped = [_swap_class_columns("swap_dz0_%d" % half, (9, 14)[half], dz0, nb, half, 2) for half in (0, 1)]
    dw, e_w_in_parts = took, []
    for half in (0, 1):
        dw = _mm_pair_dw(h0t, dz0, h0t_sib, swapped[half], nb, "mm_dwin0_%d" % half, dw, half, 2, True, half == 1)
        e_w_in_parts.append(_sc_chip_scatter("scatter_e_w_in_%d" % half, (5, 15)[half], dw))
    pin = adam("e_w_out", adam("o_w_out", adam("o_w_in", dw)))
    rows = [int(np.prod(SMALL[n][0])) // 128 for n in late]
    sums = _sum_unpack(recv_small, rows[:-1] + [8, rows[-1]], "sum_small_grads", pin)
    summed = dict(zip(late, sums[:-2] + sums[-1:]))
    loss = sums[-2][0, 0]
    dh0 = _mm_in_dx_halves(dz0, wg_e_in, "mm_dh0", summed[late[0]])
    grad_x, g["e_pre_norm"] = _pre0_bwd(dx1, dh0, x, row(p["e_pre_norm"]))
    last, = _sc_exchange("gather_e_pre_norm_grad", 7, [g["e_pre_norm"].reshape(16, 128)], False)

    n = "e_w_in"
    out = _adam_reduce(e_w_in_parts[0], w[n], m[n], v[n], "adam_e_w_in_0", grad_x, 0, 2)
    out = _adam_reduce(e_w_in_parts[1], w[n], m[n], v[n], "adam_e_w_in_1", None, 1, 2, out)
    grads[n], deltas[n], new_m[n], new_v[n] = out
    summed["e_pre_norm"] = _sum_parts(last, "sum_e_pre_norm_grad", out[3])
    names = list(SMALL)
    views = [_small_views(n) for n in names]
    mine = lambda src: [src[n].reshape(vw[0]) for n, vw in zip(names, views)]
    res = _adam_small(mine(w), [summed[n].reshape(vw[1]) for n, vw in zip(names, views)], [vw[2] for vw in views],
                      mine(m), mine(v), "adam_small")
    for n, out in zip(names, res):
        grads[n], deltas[n], new_m[n], new_v[n] = [t.reshape(_shard_shape(n)) for t in out]

    lead = lambda a: a[None]
    return (loss, grad_x[None], *[lead(grads[n]) for n in WEIGHTS], *[lead(deltas[n]) for n in WEIGHTS],
            *[lead(new_m[n]) for n in WEIGHTS], *[lead(new_v[n]) for n in WEIGHTS])
```

```python
import numpy as np
import jax
import jax.numpy as jnp
from jax import lax
from jax.experimental import pallas as pl
from jax.experimental.pallas import tpu as pltpu
from jax.experimental.pallas import tpu_sc as plsc

F32 = jnp.float32
BF16 = jnp.bfloat16

S = 2048
D = 2048
NDEV = 8
EPS = 1e-6
NEG = -1e30
HEAD_DIM = 128
ROT_DIM = 32
ROPE_THETA = 500000.0
PATTERNS = ((128, 1), (512, 4), (2048, 16))
BLK = 128
EVEN_IN = 12288
ODD_IN = 6144
HALF = 1024
CONV_K = 31
HALO = 32
TR = 256
SUB = 16

ADAM_LR = 0.001
ADAM_B1 = 0.9
ADAM_B2 = 0.999
ADAM_EPS = 1e-08
ADAM_WD = 0.01
ADAM_STEP = 10

VMEM_BIG = 56 * 1024 * 1024
MESH = pl.DeviceIdType.MESH

NN = (((1,), (0,)), ((), ()))
NT = (((1,), (1,)), ((), ()))
TN = (((0,), (0,)), ((), ()))


def _dot(a, b, dn=NN):
    return lax.dot_general(a, b, dn, preferred_element_type=F32)


def _sigmoid(x):
    return 1.0 / (1.0 + jnp.exp(-x))


def _silu_and_grad(x):
    sg = _sigmoid(x)
    return x * sg, sg * (1.0 + x * (1.0 - sg))


def _params(sem, vmem=None):
    return pltpu.CompilerParams(dimension_semantics=sem, vmem_limit_bytes=vmem)


ANY_SPEC = pl.BlockSpec(memory_space=pl.ANY)


def _matmul(a, b, *, dn, grid, a_spec, b_spec, o_spec, out_shape, out_dtype, acc_shape, name, dep=None):
    nk = grid[2]
    deps = [] if dep is None else list(dep) if isinstance(dep, (tuple, list)) else [dep]

    def body(a_ref, b_ref, *rest):
        o_ref, acc = rest[len(deps)], rest[len(deps) + 1:]
        if nk == 1:
            o_ref[...] = _dot(a_ref[...], b_ref[...], dn).astype(o_ref.dtype)
            return
        acc_ref = acc[0]
        k = pl.program_id(2)

        @pl.when(k == 0)
        def _():
            acc_ref[...] = jnp.zeros_like(acc_ref)

        acc_ref[...] += _dot(a_ref[...], b_ref[...], dn)

        @pl.when(k == nk - 1)
        def _():
            o_ref[...] = acc_ref[...].astype(o_ref.dtype)

    return pl.pallas_call(
        body, grid=grid, in_specs=[a_spec, b_spec] + [ANY_SPEC] * len(deps), out_specs=o_spec,
        out_shape=jax.ShapeDtypeStruct(out_shape, out_dtype),
        scratch_shapes=[] if nk == 1 else [pltpu.VMEM(acc_shape, F32)],
        compiler_params=_params(("parallel", "parallel", "arbitrary"), VMEM_BIG), name=name,
    )(a, b, *deps)


TM = 2048


def _mm_in(h, wg, name):
    nb = wg.shape[2]
    tn = 512 if nb % 512 == 0 else nb
    per = nb // tn
    return _matmul(
        h, wg, dn=NN, grid=(S // TM, NDEV * per, 1),
        a_spec=pl.BlockSpec((TM, D), lambda i, j, k: (i, 0)),
        b_spec=pl.BlockSpec((None, D, tn), lambda i, j, k: (j // per, 0, j % per)),
        o_spec=pl.BlockSpec((TM, tn), lambda i, j, k: (i, j)),
        out_shape=(S, NDEV * nb), out_dtype=F32, acc_shape=(TM, tn), name=name)


def _mm_in_halves(h, wg_halves, name):
    hb = wg_halves[0].shape[2]
    z = None
    for half, wg in enumerate(wg_halves):
        prev = [] if z is None else [z]

        def body(a_ref, b_ref, *rest):
            rest[-1][...] = _dot(a_ref[...], b_ref[...])

        z = pl.pallas_call(
            body, grid=(NDEV,),
            in_specs=[pl.BlockSpec((S, D), lambda j: (0, 0)), pl.BlockSpec((None, D, hb), lambda j: (j, 0, 0))]
                     + [ANY_SPEC] * len(prev),
            out_specs=pl.BlockSpec((S, hb), lambda j, half=half: (0, 2 * j + half)),
            out_shape=jax.ShapeDtypeStruct((S, 2 * NDEV * hb), F32),
            input_output_aliases={2: 0} if prev else {},
            compiler_params=_params(("parallel",), VMEM_BIG), name="%s_%d" % (name, half),
        )(h, wg, *prev)
    return z


def _mm_in_dx_halves(dz, wg_halves, name, dep):
    hb = wg_halves[0].shape[2]
    nk = 2 * NDEV

    def body(a_ref, b0_ref, b1_ref, dep_ref, o_ref, acc_ref):
        k = pl.program_id(2)

        @pl.when(k == 0)
        def _():
            acc_ref[...] = jnp.zeros_like(acc_ref)

        @pl.when(k % 2 == 0)
        def _():
            acc_ref[...] += _dot(a_ref[...], b0_ref[...], NT)

        @pl.when(k % 2 == 1)
        def _():
            acc_ref[...] += _dot(a_ref[...], b1_ref[...], NT)

        @pl.when(k == nk - 1)
        def _():
            o_ref[...] = acc_ref[...]

    b_spec = pl.BlockSpec((None, 1024, hb), lambda i, j, k: (k // 2, j, 0))
    return pl.pallas_call(
        body, grid=(1, D // 1024, nk),
        in_specs=[pl.BlockSpec((S, hb), lambda i, j, k: (0, k)), b_spec, b_spec, ANY_SPEC],
        out_specs=pl.BlockSpec((S, 1024), lambda i, j, k: (0, j)), out_shape=jax.ShapeDtypeStruct((S, D), F32),
        scratch_shapes=[pltpu.VMEM((S, 1024), F32)],
        compiler_params=_params(("parallel", "parallel", "arbitrary"), VMEM_BIG), name=name,
    )(dz, *wg_halves, dep)


def _mm_in_dx(dz, wg, name, dep=None):
    nb = wg.shape[2]
    return _matmul(
        dz, wg, dn=NT, grid=(S // TM, D // 1024, NDEV),
        a_spec=pl.BlockSpec((TM, nb), lambda i, j, k: (i, k)),
        b_spec=pl.BlockSpec((None, 1024, nb), lambda i, j, k: (k, j, 0)),
        o_spec=pl.BlockSpec((TM, 1024), lambda i, j, k: (i, j)),
        out_shape=(S, D), out_dtype=F32, acc_shape=(TM, 1024), name=name, dep=dep)


def _mm_out_dx(dy, w, name, dep=None):
    return _matmul(
        dy, w, dn=NT, grid=(S // TM, 2048 // 512, 1),
        a_spec=pl.BlockSpec((TM, D), lambda i, j, k: (i, 0)),
        b_spec=pl.BlockSpec((512, D), lambda i, j, k: (j, 0)),
        o_spec=pl.BlockSpec((TM, 512), lambda i, j, k: (i, j)),
        out_shape=(S, 2048), out_dtype=F32, acc_shape=(TM, 512), name=name, dep=dep)


def _mm_out_dw(yc, dy, name):
    return _matmul(
        yc, dy, dn=TN, grid=(2048 // TM, D // 512, 1),
        a_spec=pl.BlockSpec((S, TM), lambda i, j, k: (0, i)),
        b_spec=pl.BlockSpec((S, 512), lambda i, j, k: (0, j)),
        o_spec=pl.BlockSpec((TM, 512), lambda i, j, k: (i, j)),
        out_shape=(2048, D), out_dtype=BF16, acc_shape=(TM, 512), name=name)


def _row_spec(w=D):
    return pl.BlockSpec((TR, w), lambda i: (i, 0))


def _vec_spec(w=D):
    return pl.BlockSpec((1, w), lambda i: (0, 0))


def _rms_stats(x):
    r = lax.rsqrt(jnp.mean(x * x, axis=-1, keepdims=True) + EPS)
    return x * r, r


def _rms_bwd(dn, xhat, r, g):
    dxh = dn * g
    return r * (dxh - xhat * jnp.mean(dxh * xhat, axis=-1, keepdims=True))


def _acc_rows(ref, val, i):
    s = jnp.sum(val, axis=0, keepdims=True)

    @pl.when(i == 0)
    def _():
        ref[...] = s

    @pl.when(i > 0)
    def _():
        ref[...] += s


def _pre0_fwd(x, g):
    def body(x_ref, g_ref, h_ref, ht_ref):
        xhat, _ = _rms_stats(x_ref[...])
        h = xhat * g_ref[...]
        h_ref[...] = h.astype(BF16)
        ht_ref[...] = h.T.astype(BF16)

    return pl.pallas_call(
        body, grid=(S // TR,), in_specs=[_row_spec(), _vec_spec()],
        out_specs=[_row_spec(), pl.BlockSpec((D, TR), lambda i: (0, i))],
        out_shape=[jax.ShapeDtypeStruct((S, D), BF16), jax.ShapeDtypeStruct((D, S), BF16)],
        compiler_params=_params(("parallel",)), name="pre0_fwd",
    )(x, g)


def _post0_fwd(ycat, w_out, x, g_post, g_pre1, dep):
    def body(yc_ref, w_ref, x_ref, gp_ref, g1_ref, dep_ref, y_ref, x1_ref, h1_ref):
        y = _dot(yc_ref[...], w_ref[...])
        y_ref[...] = y
        yhat, _ = _rms_stats(y)
        x1 = x_ref[...] + yhat * gp_ref[...]
        x1_ref[...] = x1
        xhat, _ = _rms_stats(x1)
        h1_ref[...] = (xhat * g1_ref[...]).astype(BF16)

    return pl.pallas_call(
        body, grid=(S // TR,),
        in_specs=[_row_spec(), pl.BlockSpec((2048, D), lambda i: (0, 0)), _row_spec(), _vec_spec(), _vec_spec(),
                  ANY_SPEC],
        out_specs=[_row_spec(), _row_spec(), _row_spec()],
        out_shape=[jax.ShapeDtypeStruct((S, D), F32), jax.ShapeDtypeStruct((S, D), F32),
                   jax.ShapeDtypeStruct((S, D), BF16)],
        compiler_params=_params(("parallel",), VMEM_BIG), name="post0_fwd",
    )(ycat, w_out, x, g_post, g_pre1, dep)


def _post1_bwd(ycat, w_out, x1, target, g_post, dep):
    def body(yc_ref, w_ref, x1_ref, t_ref, g_ref, dep_ref, loss_ref, dx2_ref, dy_ref, dg_ref):
        i = pl.program_id(0)
        yhat, r = _rms_stats(_dot(yc_ref[...], w_ref[...]))
        g = g_ref[...]
        err = x1_ref[...] + yhat * g - t_ref[...]
        part = jnp.sum(jnp.sum(err * err, axis=-1, keepdims=True), axis=0, keepdims=True) * (0.5 / D)
        _acc_rows(loss_ref, jnp.broadcast_to(part, (1, 128)), i)
        dx2 = err * (1.0 / D)
        dx2_ref[...] = dx2
        _acc_rows(dg_ref, dx2 * yhat, i)
        dy_ref[...] = _rms_bwd(dx2, yhat, r, g).astype(BF16)

    return pl.pallas_call(
        body, grid=(S // TR,),
        in_specs=[_row_spec(), pl.BlockSpec((2048, D), lambda i: (0, 0)), _row_spec(), _row_spec(), _vec_spec(),
                  ANY_SPEC],
        out_specs=[_vec_spec(128), _row_spec(), _row_spec(), _vec_spec()],
        out_shape=[jax.ShapeDtypeStruct((1, 128), F32), jax.ShapeDtypeStruct((S, D), F32),
                   jax.ShapeDtypeStruct((S, D), BF16), jax.ShapeDtypeStruct((1, D), F32)],
        compiler_params=_params(("arbitrary",), VMEM_BIG), name="post1_bwd",
    )(ycat, w_out, x1, target, g_post, dep)


def _mid_bwd(dx2, dh1, x1, y0, g_pre1, g_post0):
    def body(dx2_ref, dh_ref, x1_ref, y_ref, g1_ref, gp_ref, dx1_ref, dy_ref, dg1_ref, dgp_ref):
        i = pl.program_id(0)
        xhat, r1 = _rms_stats(x1_ref[...])
        dh = dh_ref[...]
        _acc_rows(dg1_ref, dh * xhat, i)
        dx1 = dx2_ref[...] + _rms_bwd(dh, xhat, r1, g1_ref[...])
        dx1_ref[...] = dx1
        yhat, r0 = _rms_stats(y_ref[...])
        _acc_rows(dgp_ref, dx1 * yhat, i)
        dy_ref[...] = _rms_bwd(dx1, yhat, r0, gp_ref[...]).astype(BF16)

    return pl.pallas_call(
        body, grid=(S // TR,),
        in_specs=[_row_spec(), _row_spec(), _row_spec(), _row_spec(), _vec_spec(), _vec_spec()],
        out_specs=[_row_spec(), _row_spec(), _vec_spec(), _vec_spec()],
        out_shape=[jax.ShapeDtypeStruct((S, D), F32), jax.ShapeDtypeStruct((S, D), BF16),
                   jax.ShapeDtypeStruct((1, D), F32), jax.ShapeDtypeStruct((1, D), F32)],
        compiler_params=_params(("arbitrary",)), name="mid_bwd",
    )(dx2, dh1, x1, y0, g_pre1, g_post0)


def _pre0_bwd(dx1, dh0, x, g):
    def body(dx1_ref, dh_ref, x_ref, g_ref, gx_ref, dg_ref):
        i = pl.program_id(0)
        xhat, r = _rms_stats(x_ref[...])
        dh = dh_ref[...]
        _acc_rows(dg_ref, dh * xhat, i)
        gx_ref[...] = dx1_ref[...] + _rms_bwd(dh, xhat, r, g_ref[...])

    return pl.pallas_call(
        body, grid=(S // TR,), in_specs=[_row_spec(), _row_spec(), _row_spec(), _vec_spec()],
        out_specs=[_row_spec(), _vec_spec()],
        out_shape=[jax.ShapeDtypeStruct((S, D), F32), jax.ShapeDtypeStruct((1, D), F32)],
        compiler_params=_params(("arbitrary",)), name="pre0_bwd",
    )(dx1, dh0, x, g)


POOL_CH = 256


def _pool_apply(a, w, transpose):
    n = a.shape[0]
    row = lax.broadcasted_iota(jnp.int32, a.shape, 0)
    cnt = jnp.minimum(row + 1, w).astype(F32)
    s = a / cnt if transpose else a
    for k in (1, 2, 4, 8):
        if transpose:
            sh = jnp.where(row < n - k, pltpu.roll(s, n - k, 0), 0.0)
        else:
            sh = jnp.where(row >= k, pltpu.roll(s, k, 0), 0.0)
        s = jnp.where(w > k, s + sh, s)
    return s - a if transpose else s / cnt - a


def _pool_fwd(z0, pool_w, pool_scale):
    def body(a_ref, gate_ref, w_ref, sc_ref, out_ref):
        win = jnp.left_shift(2, pl.program_id(0))
        pooled = _pool_apply(a_ref[...], win, False)
        mixed = _dot(pooled.astype(BF16), w_ref[...])
        gate = gate_ref[...]
        out_ref[...] = (mixed * sc_ref[...] * (gate * _sigmoid(gate))).astype(BF16)

    return pl.pallas_call(
        body, grid=(4,),
        in_specs=[pl.BlockSpec((S, POOL_CH), lambda g: (0, g)), pl.BlockSpec((S, POOL_CH), lambda g: (0, 4 + g)),
                  pl.BlockSpec((None, POOL_CH, POOL_CH), lambda g: (g, 0, 0)),
                  pl.BlockSpec((1, POOL_CH), lambda g: (0, g))],
        out_specs=pl.BlockSpec((S, POOL_CH), lambda g: (0, g)),
        out_shape=jax.ShapeDtypeStruct((S, 2048), BF16),
        compiler_params=_params(("parallel",), VMEM_BIG), name="pool_fwd",
    )(z0, z0, pool_w, pool_scale)


def _pool_bwd(z0, dycat, pool_w, pool_scale):
    def body(a_ref, gate_ref, dy_ref, w_ref, sc_ref, da_ref, dgate_ref, dw_ref, dsc_ref):
        win = jnp.left_shift(2, pl.program_id(0))
        pooled = _pool_apply(a_ref[...], win, False).astype(BF16)
        w = w_ref[...]
        mixed = _dot(pooled, w)
        silu, dsilu = _silu_and_grad(gate_ref[...])
        dy = dy_ref[...]
        sc = sc_ref[...]
        dgate_ref[...] = (dy * (mixed * sc) * dsilu).astype(BF16)
        dms = dy * silu
        dsc_ref[...] = jnp.sum(dms * mixed, axis=0, keepdims=True)
        dmixed = (dms * sc).astype(BF16)
        dw_ref[...] = _dot(pooled, dmixed, TN)
        dpooled = _dot(dmixed, w, NT)
        da_ref[...] = _pool_apply(dpooled, win, True).astype(BF16)

    slab = lambda off: pl.BlockSpec((S, POOL_CH), lambda g: (0, off + g))
    return pl.pallas_call(
        body, grid=(4,),
        in_specs=[slab(0), slab(4), slab(0), pl.BlockSpec((None, POOL_CH, POOL_CH), lambda g: (g, 0, 0)),
                  pl.BlockSpec((1, POOL_CH), lambda g: (0, g))],
        out_specs=[slab(0), slab(0), pl.BlockSpec((None, POOL_CH, POOL_CH), lambda g: (g, 0, 0)),
                   pl.BlockSpec((1, POOL_CH), lambda g: (0, g))],
        out_shape=[jax.ShapeDtypeStruct((S, HALF), BF16), jax.ShapeDtypeStruct((S, HALF), BF16),
                   jax.ShapeDtypeStruct((4, POOL_CH, POOL_CH), F32), jax.ShapeDtypeStruct((1, HALF), F32)],
        compiler_params=_params(("parallel",), VMEM_BIG), name="pool_bwd",
    )(z0, z0, dycat, pool_w, pool_scale)


Q_COL, K_COL, V_COL, BG_COL = 2048 // 128, 5120 // 128, 8192 // 128, 11264 // 128
SCALE = HEAD_DIM ** -0.5


def _rope_tables():
    pos = jnp.arange(S, dtype=F32)
    inv_freq = jnp.power(ROPE_THETA, -jnp.arange(0, ROT_DIM, 2, dtype=F32) / ROT_DIM)
    ang = pos[:, None] * inv_freq[None, :]
    cos, sin = jnp.cos(ang), jnp.sin(ang)
    half = ROT_DIM // 2
    zeros = jnp.zeros((S, HEAD_DIM - ROT_DIM), F32)
    c = jnp.concatenate([cos, cos, jnp.ones((S, HEAD_DIM - ROT_DIM), F32)], axis=1)
    a = jnp.concatenate([-sin, jnp.zeros((S, half), F32), zeros], axis=1)
    b = jnp.concatenate([jnp.zeros((S, half), F32), sin, zeros], axis=1)
    return c, a, b


def _rope(t, c, a, b):
    half = ROT_DIM // 2
    return t * c + pltpu.roll(t, HEAD_DIM - half, 1) * a + pltpu.roll(t, half, 1) * b


def _rope_t(d, c, a, b):
    half = ROT_DIM // 2
    return d * c + pltpu.roll(d * a, half, 1) + pltpu.roll(d * b, HEAD_DIM - half, 1)


def _deinterleave(dst, src, dil, cast=None, dst_off=0):
    length = S // dil
    for r in range(dil):
        v = src[...] if dil == 1 else src[pl.ds(r, length, stride=dil), :]
        dst[dst_off + r * length:dst_off + (r + 1) * length, :] = v if cast is None else v.astype(cast)


def _interleave(dst, src, dil, src_off=0):
    length = S // dil
    for r in range(dil):
        if dil == 1:
            dst[...] = src[src_off:src_off + S, :]
        else:
            dst[pl.ds(r, length, stride=dil), :] = src[src_off + r * length:src_off + (r + 1) * length, :]


CU = 8
NUNITS = S // BLK
B_QK = (((2,), (2,)), ((0,), (0,)))
B_PV = (((2,), (1,)), ((0,), (0,)))
B_TN = (((1,), (1,)), ((0,), (0,)))


def _blocks(ref, first):
    return ref[first * BLK:(first + CU) * BLK, :].reshape(CU, BLK, HEAD_DIM)


def _chunk_scores(u0, nb, qd, kdp):
    q = _blocks(qd, u0)
    row = lax.broadcasted_iota(jnp.int32, (CU, BLK, BLK), 1)
    col = lax.broadcasted_iota(jnp.int32, (CU, BLK, BLK), 2)
    s_own = jnp.where(col <= row, _dot(q, _blocks(kdp, u0 + 1), B_QK) * SCALE, NEG)
    if nb == 1:
        return q, s_own, None
    unit = lax.broadcasted_iota(jnp.int32, (CU, BLK, BLK), 0) + u0
    s_prev = jnp.where((col >= row) & ((unit % nb) != 0), _dot(q, _blocks(kdp, u0), B_QK) * SCALE, NEG)
    return q, s_own, s_prev


def _qkv_prep(z0, tabs):
    def body(q_ref, k_ref, v_ref, c_ref, a_ref, b_ref, qo_ref, ko_ref, vo_ref, tmp):
        p = pl.program_id(1)
        ko_ref[0:BLK, :] = jnp.zeros((BLK, HEAD_DIM), BF16)
        vo_ref[0:BLK, :] = jnp.zeros((BLK, HEAD_DIM), BF16)
        for gi, (_, dil) in enumerate(PATTERNS):
            @pl.when(p == gi)
            def _(dil=dil):
                c, a, b = c_ref[...], a_ref[...], b_ref[...]
                tmp[...] = _rope(q_ref[...], c, a, b)
                _deinterleave(qo_ref, tmp, dil, BF16)
                tmp[...] = _rope(k_ref[...], c, a, b)
                _deinterleave(ko_ref, tmp, dil, BF16, BLK)
                _deinterleave(vo_ref, v_ref, dil, BF16, BLK)

    tab = pl.BlockSpec((S, HEAD_DIM), lambda h, p: (0, 0))
    out = pl.BlockSpec((S, HEAD_DIM), lambda h, p: (0, p * 8 + h))
    outp = pl.BlockSpec((S + BLK, HEAD_DIM), lambda h, p: (0, p * 8 + h))
    return pl.pallas_call(
        body, grid=(8, 3), in_specs=[_head_spec(Q_COL), _head_spec(K_COL), _head_spec(V_COL), tab, tab, tab],
        out_specs=[out, outp, outp],
        out_shape=[jax.ShapeDtypeStruct((S, 3072), BF16)] + [jax.ShapeDtypeStruct((S + BLK, 3072), BF16)] * 2,
        scratch_shapes=[pltpu.VMEM((S, HEAD_DIM), F32)],
        compiler_params=_params(("parallel", "arbitrary"), VMEM_BIG), name="qkv_prep",
    )(z0, z0, z0, *tabs)


def _attn_group_fwd(dil, qd, kdp, vdp, od, ld, og, lg):
    nb = S // dil // BLK
    for u0 in range(0, NUNITS, CU):
        _, s_own, s_prev = _chunk_scores(u0, nb, qd, kdp)
        m = jnp.max(s_own, axis=2, keepdims=True)
        if s_prev is not None:
            m = jnp.maximum(m, jnp.max(s_prev, axis=2, keepdims=True))
        p_own = jnp.exp(s_own - m)
        den = jnp.sum(p_own, axis=2, keepdims=True)
        acc = _dot(p_own.astype(BF16), _blocks(vdp, u0 + 1), B_PV)
        if s_prev is not None:
            p_prev = jnp.exp(s_prev - m)
            den = den + jnp.sum(p_prev, axis=2, keepdims=True)
            acc = acc + _dot(p_prev.astype(BF16), _blocks(vdp, u0), B_PV)
        rows = slice(u0 * BLK, (u0 + CU) * BLK)
        od[rows, :] = (acc / den).reshape(CU * BLK, HEAD_DIM)
        ld[rows, :] = jnp.broadcast_to(m + jnp.log(den), (CU, BLK, HEAD_DIM)).reshape(CU * BLK, HEAD_DIM)
    _interleave(og, od, dil)
    _interleave(lg, ld, dil)


def _group_weights(lgs):
    l0, l1, l2 = lgs[0][...], lgs[1][...], lgs[2][...]
    mx = jnp.maximum(l0, jnp.maximum(l1, l2))
    e0, e1, e2 = jnp.exp(l0 - mx), jnp.exp(l1 - mx), jnp.exp(l2 - mx)
    den = e0 + e1 + e2
    return e0 / den, e1 / den, e2 / den


def _head_spec(base):
    return pl.BlockSpec((S, HEAD_DIM), lambda h, p: (0, base + (p % 3) * 8 + h))


def _slab(dtype=F32, rows=S):
    return pltpu.VMEM((rows, HEAD_DIM), dtype)


def _attn_fwd(z0, qkv, ycat):
    def body(q_ref, k_ref, v_ref, gate_ref, ycat_ref, out_ref, og_ref, lg_ref,
             od, ld, og0, og1, og2, lg0, lg1, lg2):
        del ycat_ref
        p = pl.program_id(1)
        ogs, lgs = (og0, og1, og2), (lg0, lg1, lg2)
        for gi, (_, dil) in enumerate(PATTERNS):
            @pl.when(p == gi)
            def _(gi=gi, dil=dil):
                _attn_group_fwd(dil, q_ref, k_ref, v_ref, od, ld, ogs[gi], lgs[gi])
                og_ref[...] = ogs[gi][...]
                lg_ref[...] = lgs[gi][...]

        @pl.when(p == 2)
        def _():
            w0, w1, w2 = _group_weights(lgs)
            o = w0 * og0[...] + w1 * og1[...] + w2 * og2[...]
            gate = gate_ref[...]
            out_ref[...] = (o * (gate * _sigmoid(gate))).astype(BF16)

    grp = pl.BlockSpec((S, HEAD_DIM), lambda h, p: (0, p * 8 + h))
    grp_pad = pl.BlockSpec((S + BLK, HEAD_DIM), lambda h, p: (0, p * 8 + h))
    return pl.pallas_call(
        body, grid=(8, 3),
        in_specs=[grp, grp_pad, grp_pad, pl.BlockSpec((S, HEAD_DIM), lambda h, p: (0, BG_COL + h)), ANY_SPEC],
        out_specs=[pl.BlockSpec((S, HEAD_DIM), lambda h, p: (0, 8 + h)), grp, grp],
        out_shape=[jax.ShapeDtypeStruct((S, 2048), BF16), jax.ShapeDtypeStruct((S, 3072), F32),
                   jax.ShapeDtypeStruct((S, 3072), F32)],
        scratch_shapes=[_slab() for _ in range(8)],
        input_output_aliases={4: 0},
        compiler_params=_params(("parallel", "arbitrary"), VMEM_BIG), name="attn_fwd",
    )(*qkv, z0, ycat)


def _attn_bwd(z0, qkv, og, lg, dycat, tabs, dep):
    def body(q_ref, k_ref, v_ref, gate_ref, dy_ref, c_ref, a_ref, b_ref,
             og0_ref, og1_ref, og2_ref, lg0_ref, lg1_ref, lg2_ref, dep_ref,
             dq_ref, dk_ref, dv_ref, dbg_ref,
             tmp, ld, dg0, dg1, dg2, cg0, cg1, cg2, dod, cd, dqd, dkd, dvd):
        kd, vd = k_ref, v_ref
        p = pl.program_id(1)
        ogs, lgs, dgs, cgs = (og0_ref, og1_ref, og2_ref), (lg0_ref, lg1_ref, lg2_ref), (dg0, dg1, dg2), (cg0, cg1, cg2)

        @pl.when(p == 0)
        def _():
            w = _group_weights(lgs)
            o = w[0] * ogs[0][...] + w[1] * ogs[1][...] + w[2] * ogs[2][...]
            silu, dsilu = _silu_and_grad(gate_ref[...])
            dy = dy_ref[...]
            dbg_ref[...] = (dy * o * dsilu).astype(BF16)
            do = dy * silu
            dwbar = jnp.sum(do * o, axis=1, keepdims=True)
            for gi in range(3):
                dgs[gi][...] = w[gi] * do
                cgs[gi][...] = -w[gi] * dwbar

        for gi, (_, dil) in enumerate(PATTERNS):
            @pl.when(p == 1 + gi)
            def _(gi=gi, dil=dil):
                nb = S // dil // BLK
                qd = q_ref
                c, a, b = c_ref[...], a_ref[...], b_ref[...]
                _deinterleave(dod, dgs[gi], dil, BF16)
                _deinterleave(ld, lgs[gi], dil)
                _deinterleave(cd, cgs[gi], dil)
                dkd[...] = jnp.zeros_like(dkd)
                dvd[...] = jnp.zeros_like(dvd)
                flat = lambda t: t.reshape(CU * BLK, HEAD_DIM)
                for u0 in range(0, NUNITS, CU):
                    q, s_own, s_prev = _chunk_scores(u0, nb, qd, kd)
                    lse, cv, do = _blocks(ld, u0), _blocks(cd, u0), _blocks(dod, u0)
                    own = slice((u0 + 1) * BLK, (u0 + 1 + CU) * BLK)
                    p_own = jnp.exp(s_own - lse)
                    ds_own = (p_own * (_dot(do, _blocks(vd, u0 + 1), B_QK) + cv) * SCALE).astype(BF16)
                    dq = _dot(ds_own, _blocks(kd, u0 + 1), B_PV)
                    dkd[own, :] += flat(_dot(ds_own, q, B_TN))
                    dvd[own, :] += flat(_dot(p_own.astype(BF16), do, B_TN))
                    if s_prev is not None:
                        prev = slice(u0 * BLK, (u0 + CU) * BLK)
                        p_prev = jnp.exp(s_prev - lse)
                        ds_prev = (p_prev * (_dot(do, _blocks(vd, u0), B_QK) + cv) * SCALE).astype(BF16)
                        dq = dq + _dot(ds_prev, _blocks(kd, u0), B_PV)
                        dkd[prev, :] += flat(_dot(ds_prev, q, B_TN))
                        dvd[prev, :] += flat(_dot(p_prev.astype(BF16), do, B_TN))
                    dqd[u0 * BLK:(u0 + CU) * BLK, :] = flat(dq)
                _interleave(tmp, dqd, dil)
                dq_ref[...] = _rope_t(tmp[...], c, a, b).astype(BF16)
                _interleave(tmp, dkd, dil, BLK)
                dk_ref[...] = _rope_t(tmp[...], c, a, b).astype(BF16)
                _interleave(tmp, dvd, dil, BLK)
                dv_ref[...] = tmp[...].astype(BF16)

    tab = pl.BlockSpec((S, HEAD_DIM), lambda h, p: (0, 0))
    hspec = lambda base: pl.BlockSpec((S, HEAD_DIM), lambda h, p: (0, base + h))
    gspec = pl.BlockSpec((S, HEAD_DIM), lambda h, p: (0, jnp.maximum(p - 1, 0) * 8 + h))
    gspec_pad = pl.BlockSpec((S + BLK, HEAD_DIM), lambda h, p: (0, jnp.maximum(p - 1, 0) * 8 + h))
    return pl.pallas_call(
        body, grid=(8, 4),
        in_specs=[gspec, gspec_pad, gspec_pad, hspec(BG_COL), hspec(8), tab, tab, tab,
                  hspec(0), hspec(8), hspec(16), hspec(0), hspec(8), hspec(16), ANY_SPEC],
        out_specs=[gspec, gspec, gspec, hspec(0)],
        out_shape=[jax.ShapeDtypeStruct((S, 3072), BF16)] * 3 + [jax.ShapeDtypeStruct((S, HALF), BF16)],
        scratch_shapes=[_slab(), _slab()] + [_slab() for _ in range(6)]
                       + [_slab(BF16), _slab(), _slab(), _slab(F32, S + BLK), _slab(F32, S + BLK)],
        compiler_params=_params(("parallel", "arbitrary"), VMEM_BIG), name="attn_bwd",
    )(*qkv, z0, dycat, *tabs, og, og, og, lg, lg, lg, dep)


SGU_CH = 256
NCHUNK = TR // 128


def _ln_stats(x):
    mu = jnp.mean(x, axis=-1, keepdims=True)
    xc = x - mu
    r = lax.rsqrt(jnp.mean(xc * xc, axis=-1, keepdims=True) + EPS)
    return xc * r, r


def _ln_bwd(dy, xhat, r, g):
    dxh = dy * g
    return r * (dxh - jnp.mean(dxh, axis=-1, keepdims=True) - xhat * jnp.mean(dxh * xhat, axis=-1, keepdims=True))


def _tril_bf16(w):
    row = lax.broadcasted_iota(jnp.int32, w.shape, 0)
    col = lax.broadcasted_iota(jnp.int32, w.shape, 1)
    return jnp.where(row >= col, w, 0.0).astype(BF16)


def _sgu_gate(vn_s, s_s, w_ref, bb_ref):
    for h in range(4):
        wm = _tril_bf16(w_ref[h])
        bias = bb_ref[h]
        for ch in range(NCHUNK):
            rows, cols = slice(ch * 128, (ch + 1) * 128), slice(h * SGU_CH, (h + 1) * SGU_CH)
            s_s[rows, cols] = _dot(wm, vn_s[rows, cols]) + jnp.concatenate([bias, bias], axis=1)


WIN = HALO + TR
SUBL = 8


def _shifted_copies(dst, src):
    dst[0] = src[...]
    for b in range(1, SUBL):
        dst[b, 0:WIN - SUBL, :] = src[pl.ds(b, WIN - SUBL), :]


def _rows_at(copies, off, n):
    return copies[off % SUBL, pl.ds(off - off % SUBL, n), :]


def _conv_fwd(i, dval_ref, dglu_ref, hval_ref, hglu_ref, cw_ref, cb_ref, xw, xr, dcs):
    halo = hval_ref[...] * _sigmoid(hglu_ref[...])
    xw[0:HALO, :] = jnp.where(i > 0, halo, 0.0)
    xw[HALO:HALO + TR, :] = dval_ref[...] * _sigmoid(dglu_ref[...])
    _shifted_copies(xr, xw)
    sub = 2 * SUB
    for rb in range(TR // sub):
        acc = jnp.broadcast_to(cb_ref[...], (sub, HALF))
        for k in range(CONV_K):
            acc = acc + cw_ref[k:k + 1, :] * _rows_at(xr, rb * sub + HALO - (CONV_K - 1) + k, sub)
        dcs[rb * sub:(rb + 1) * sub, :] = acc


def _odd_in_specs():
    col = lambda j: pl.BlockSpec((TR, HALF), lambda i, *_: (i, j))
    prev = lambda j: pl.BlockSpec((HALO, HALF), lambda i, *_: (jnp.maximum(i * (TR // HALO) - 1, 0), j))
    return [col(0), col(1), col(2), col(3), col(4), col(5), prev(3), prev(4)]


def _full_spec(shape):
    return pl.BlockSpec(shape, lambda i, *_: (0,) * len(shape))


def _odd_fwd(z1, sgu_g, sgu_b, sgu_w, sgu_bb, conv_w, conv_b, cn_g, cn_b):
    def body(u_ref, v_ref, cg_ref, dval_ref, dglu_ref, dgate_ref, hval_ref, hglu_ref,
             g_ref, b_ref, w_ref, bb_ref, cw_ref, cb_ref, cng_ref, cnb_ref, out_ref, dcs, vn_s, s_s, xw, xr):
        i = pl.program_id(0)
        vhat, _ = _ln_stats(v_ref[...])
        vn_s[...] = (vhat * g_ref[...] + b_ref[...]).astype(BF16)
        _sgu_gate(vn_s, s_s, w_ref, bb_ref)
        cg = cg_ref[...]
        out_ref[:, 0:HALF] = (u_ref[...] * s_s[...] * (cg * _sigmoid(cg))).astype(BF16)
        _conv_fwd(i, dval_ref, dglu_ref, hval_ref, hglu_ref, cw_ref, cb_ref, xw, xr, dcs)
        dhat, _ = _ln_stats(dcs[...])
        dn = dhat * cng_ref[...] + cnb_ref[...]
        dgate = dgate_ref[...]
        out_ref[:, HALF:2 * HALF] = ((dn * _sigmoid(dn)) * (dgate * _sigmoid(dgate))).astype(BF16)

    vec = _full_spec((1, HALF))
    return pl.pallas_call(
        body, grid=(S // TR,),
        in_specs=_odd_in_specs() + [vec, vec, _full_spec((4, 128, 128)), _full_spec((4, 128, 128)),
                                    _full_spec((HALO, HALF)), vec, vec, vec],
        out_specs=[pl.BlockSpec((TR, 2048), lambda i: (i, 0)), pl.BlockSpec((TR, HALF), lambda i: (i, 0))],
        out_shape=[jax.ShapeDtypeStruct((S, 2048), BF16), jax.ShapeDtypeStruct((S, HALF), F32)],
        scratch_shapes=[pltpu.VMEM((TR, HALF), BF16), pltpu.VMEM((TR, HALF), F32),
                        pltpu.VMEM((WIN, HALF), F32), pltpu.VMEM((SUBL, WIN, HALF), F32)],
        compiler_params=_params(("parallel",), VMEM_BIG), name="odd_fwd",
    )(z1, z1, z1, z1, z1, z1, z1, z1, sgu_g, sgu_b, sgu_w, sgu_bb, conv_w, conv_b, cn_g, cn_b)


def _odd_bwd_a(z1, dc, dycat, sgu_g, sgu_b, sgu_w, sgu_bb, cn_g, cn_b):
    def body(u_ref, v_ref, cg_ref, dgate_ref, dcs, dy_ref, g_ref, b_ref, w_ref, bb_ref, cng_ref, cnb_ref,
             dz_ref, ddc_ref, dw_ref, dbb_ref, dg_ref, db_ref, dcng_ref, dcnb_ref, dcb_ref,
             vn_s, s_s, ds_s, dvn_s):
        i = pl.program_id(0)
        vhat, rv = _ln_stats(v_ref[...])
        g = g_ref[...]
        vn_s[...] = (vhat * g + b_ref[...]).astype(BF16)
        _sgu_gate(vn_s, s_s, w_ref, bb_ref)
        silu_c, dsilu_c = _silu_and_grad(cg_ref[...])
        dyc = dy_ref[:, 0:HALF]
        u = u_ref[...]
        s = s_s[...]
        dz_ref[:, 0:HALF] = (dyc * s * silu_c).astype(BF16)
        dz_ref[:, 2 * HALF:3 * HALF] = (dyc * u * s * dsilu_c).astype(BF16)
        ds_s[...] = dyc * u * silu_c

        @pl.when(i == 0)
        def _():
            dw_ref[...] = jnp.zeros_like(dw_ref)
            dbb_ref[...] = jnp.zeros_like(dbb_ref)

        tril = lax.broadcasted_iota(jnp.int32, (128, 128), 0) >= lax.broadcasted_iota(jnp.int32, (128, 128), 1)
        for h in range(4):
            wm = _tril_bf16(w_ref[h])
            for ch in range(NCHUNK):
                rows, cols = slice(ch * 128, (ch + 1) * 128), slice(h * SGU_CH, (h + 1) * SGU_CH)
                ds = ds_s[rows, cols]
                dsb = ds.astype(BF16)
                dw_ref[h] += jnp.where(tril, _dot(dsb, vn_s[rows, cols], NT), 0.0)
                dbb_ref[h] += jnp.broadcast_to(jnp.sum(ds, axis=1, keepdims=True), (128, 128))
                dvn_s[rows, cols] = _dot(wm, dsb, TN)
        dvn = dvn_s[...]
        _acc_rows(dg_ref, dvn * vhat, i)
        _acc_rows(db_ref, dvn, i)
        dz_ref[:, HALF:2 * HALF] = _ln_bwd(dvn, vhat, rv, g).astype(BF16)

        dhat, rd = _ln_stats(dcs[...])
        cng = cng_ref[...]
        silu_n, dsilu_n = _silu_and_grad(dhat * cng + cnb_ref[...])
        silu_g, dsilu_g = _silu_and_grad(dgate_ref[...])
        dyd = dy_ref[:, HALF:2 * HALF]
        dz_ref[:, 5 * HALF:6 * HALF] = (dyd * silu_n * dsilu_g).astype(BF16)
        ddn = dyd * silu_g * dsilu_n
        _acc_rows(dcng_ref, ddn * dhat, i)
        _acc_rows(dcnb_ref, ddn, i)
        ddc = _ln_bwd(ddn, dhat, rd, cng)
        ddc_ref[...] = ddc
        _acc_rows(dcb_ref, ddc, i)

    vec = _full_spec((1, HALF))
    sq = _full_spec((4, 128, 128))
    col = lambda j: pl.BlockSpec((TR, HALF), lambda i: (i, j))
    return pl.pallas_call(
        body, grid=(S // TR,),
        in_specs=[col(0), col(1), col(2), col(5), col(0), pl.BlockSpec((TR, 2048), lambda i: (i, 0)),
                  vec, vec, sq, sq, vec, vec],
        out_specs=[pl.BlockSpec((TR, ODD_IN), lambda i: (i, 0)), pl.BlockSpec((TR, HALF), lambda i: (i, 0)),
                   sq, sq, vec, vec, vec, vec, vec],
        out_shape=[jax.ShapeDtypeStruct((S, ODD_IN), BF16), jax.ShapeDtypeStruct((S, HALF), F32),
                   jax.ShapeDtypeStruct((4, 128, 128), F32), jax.ShapeDtypeStruct((4, 128, 128), F32)]
                  + [jax.ShapeDtypeStruct((1, HALF), F32)] * 5,
        scratch_shapes=[pltpu.VMEM((TR, HALF), BF16), pltpu.VMEM((TR, HALF), F32),
                        pltpu.VMEM((TR, HALF), F32), pltpu.VMEM((TR, HALF), F32)],
        compiler_params=_params(("arbitrary",), VMEM_BIG), name="odd_bwd_a",
    )(z1, z1, z1, z1, dc, dycat, sgu_g, sgu_b, sgu_w, sgu_bb, cn_g, cn_b)


def _odd_bwd_b(z1, ddc, dz1, conv_w):
    nt = S // TR

    def body(dval_ref, dglu_ref, hval_ref, hglu_ref, ddc_ref, hddc_ref, cw_ref, dz_in_ref,
             dz_ref, dcw_ref, xw, dwin, dxs, xr, dr):
        del dz_in_ref
        i, j = pl.program_id(0), pl.program_id(1)
        sg = _sigmoid(dglu_ref[...])
        dval = dval_ref[...]

        @pl.when(j == 0)
        def _():
            halo = hval_ref[...] * _sigmoid(hglu_ref[...])
            xw[0:HALO, :] = jnp.where(i > 0, halo, 0.0)
            xw[HALO:HALO + TR, :] = dval * sg
            dwin[0:TR, :] = ddc_ref[...]
            dwin[TR:TR + HALO, :] = jnp.where(i < nt - 1, hddc_ref[...], 0.0)
            _shifted_copies(xr, xw)
            _shifted_copies(dr, dwin)

            @pl.when(i == 0)
            def _():
                dcw_ref[...] = jnp.zeros_like(dcw_ref)

            for rb in range(TR // SUB):
                acc = jnp.zeros((SUB, HALF), F32)
                for k in range(CONV_K):
                    acc = acc + cw_ref[k:k + 1, :] * _rows_at(dr, rb * SUB + (CONV_K - 1) - k, SUB)
                dxs[rb * SUB:(rb + 1) * SUB, :] = acc
            for k in range(CONV_K):
                acc = jnp.zeros((SUB, HALF), F32)
                for rb in range(TR // SUB):
                    acc = acc + dwin[rb * SUB:(rb + 1) * SUB, :] * _rows_at(xr, rb * SUB + HALO - (CONV_K - 1) + k, SUB)
                dcw_ref[k:k + 1, :] += jnp.sum(acc, axis=0, keepdims=True)
            dz_ref[...] = (dxs[...] * sg).astype(BF16)

        @pl.when(j == 1)
        def _():
            dz_ref[...] = (dxs[...] * dval * sg * (1.0 - sg)).astype(BF16)

    col = lambda c: pl.BlockSpec((TR, HALF), lambda i, j: (i, c))
    prev = lambda c: pl.BlockSpec((HALO, HALF), lambda i, j: (jnp.maximum(i * (TR // HALO) - 1, 0), c))
    nxt = pl.BlockSpec((HALO, HALF), lambda i, j: (jnp.minimum((i + 1) * (TR // HALO), S // HALO - 1), 0))
    return pl.pallas_call(
        body, grid=(nt, 2),
        in_specs=[col(3), col(4), prev(3), prev(4), pl.BlockSpec((TR, HALF), lambda i, j: (i, 0)), nxt,
                  _full_spec((HALO, HALF)), pl.BlockSpec(memory_space=pl.ANY)],
        out_specs=[pl.BlockSpec((TR, HALF), lambda i, j: (i, 3 + j)), _full_spec((HALO, HALF))],
        out_shape=[jax.ShapeDtypeStruct((S, ODD_IN), BF16), jax.ShapeDtypeStruct((HALO, HALF), F32)],
        scratch_shapes=[pltpu.VMEM((WIN, HALF), F32), pltpu.VMEM((WIN, HALF), F32), pltpu.VMEM((TR, HALF), F32),
                        pltpu.VMEM((SUBL, WIN, HALF), F32), pltpu.VMEM((SUBL, WIN, HALF), F32)],
        input_output_aliases={7: 0},
        compiler_params=_params(("arbitrary", "arbitrary"), VMEM_BIG), name="odd_bwd_b",
    )(z1, z1, z1, z1, ddc, ddc, conv_w, dz1)


def _cast_bf16(w, name, piece=0, npieces=1):
    r, c = w.shape[0], w.shape[1] // npieces
    tr = min(r, 256)

    def body(i_ref, o_ref):
        o_ref[...] = i_ref[...].astype(BF16)

    return pl.pallas_call(
        body, grid=(r // tr,), in_specs=[pl.BlockSpec((tr, c), lambda i: (i, piece))],
        out_specs=pl.BlockSpec((tr, c), lambda i: (i, 0)), out_shape=jax.ShapeDtypeStruct((r, c), BF16),
        compiler_params=_params(("parallel",)), name=name,
    )(w)


def _adamw(w, g, m, v):
    m = ADAM_B1 * m + (1.0 - ADAM_B1) * g
    v = ADAM_B2 * v + (1.0 - ADAM_B2) * (g * g)
    m_hat = m / (1.0 - ADAM_B1 ** ADAM_STEP)
    v_hat = v / (1.0 - ADAM_B2 ** ADAM_STEP)
    delta = -ADAM_LR * (m_hat / (jnp.sqrt(v_hat) + ADAM_EPS) + ADAM_WD * w)
    return delta, m, v


def _adam_reduce(parts, w, m, v, name, dep=None, piece=0, npieces=1, prev=None):
    r, c = w.shape
    cp = c // npieces
    tr = min(r, 128)
    extra = ([] if dep is None else [dep]) + ([] if prev is None else list(prev))
    nparts = parts.shape[0]

    def body(p_ref, w_ref, m_ref, v_ref, *rest):
        g_ref, d_ref, nm_ref, nv_ref = rest[len(extra):]
        g = p_ref[0].astype(F32)
        for d in range(1, nparts):
            g = g + p_ref[d].astype(F32)
        g_ref[...] = g
        d_ref[...], nm_ref[...], nv_ref[...] = _adamw(w_ref[...], g, m_ref[...], v_ref[...])

    spec = pl.BlockSpec((tr, cp), lambda i: (i, piece))
    first = 4 + (0 if dep is None else 1)
    return pl.pallas_call(
        body, grid=(r // tr,),
        in_specs=[pl.BlockSpec((nparts, tr, cp), lambda i: (0, i, 0)), spec, spec, spec] + [ANY_SPEC] * len(extra),
        out_specs=[spec] * 4, out_shape=[jax.ShapeDtypeStruct((r, c), F32)] * 4,
        input_output_aliases={} if prev is None else {first + k: k for k in range(4)},
        compiler_params=_params(("parallel",), VMEM_BIG), name=name,
    )(parts, w, m, v, *extra)


def _arrived(x, name, dep=None):
    deps = [] if dep is None else [dep]

    def body(*refs):
        refs[-1][...] = jnp.zeros_like(refs[-1])

    return pl.pallas_call(
        body, in_specs=[ANY_SPEC] * (1 + len(deps)), out_specs=pl.BlockSpec(memory_space=pltpu.VMEM),
        out_shape=jax.ShapeDtypeStruct((8, 128), F32), name=name,
    )(x, *deps)


def _sum_parts(parts, name, dep=None):
    r = parts.shape[1]
    tr = 8
    for cand in (512, 256, 128, 64, 32, 16, 8):
        if r % cand == 0:
            tr = cand
            break
    deps = [] if dep is None else [dep]

    def body(p_ref, *rest):
        g = p_ref[0]
        for d in range(1, NDEV):
            g = g + p_ref[d]
        rest[-1][...] = g

    return pl.pallas_call(
        body, grid=(r // tr,), in_specs=[pl.BlockSpec((NDEV, tr, 128), lambda i: (0, i, 0))] + [ANY_SPEC] * len(deps),
        out_specs=pl.BlockSpec((tr, 128), lambda i: (i, 0)), out_shape=jax.ShapeDtypeStruct((r, 128), F32),
        compiler_params=_params(("parallel",)), name=name,
    )(parts, *deps)


def _sum_unpack(parts, rows, name, dep=None):
    deps = [] if dep is None else [dep]

    def body(p_ref, *outs):
        outs = outs[len(deps):]
        off = 0
        for o_ref, n in zip(outs, rows):
            acc = p_ref[0, off:off + n, :]
            for d in range(1, NDEV):
                acc = acc + p_ref[d, off:off + n, :]
            o_ref[...] = acc
            off += n

    return pl.pallas_call(
        body, grid=(1,), in_specs=[pl.BlockSpec(parts.shape, lambda i: (0, 0, 0))] + [ANY_SPEC] * len(deps),
        out_specs=[pl.BlockSpec((n, 128), lambda i: (0, 0)) for n in rows],
        out_shape=[jax.ShapeDtypeStruct((n, 128), F32) for n in rows],
        compiler_params=_params(("arbitrary",), VMEM_BIG), name=name,
    )(parts, *deps)


def _adam_small(ws, gs, g_specs, ms, vs, name):
    n = len(ws)

    def body(*refs):
        w_r, g_r, m_r, v_r = refs[:n], refs[n:2 * n], refs[2 * n:3 * n], refs[3 * n:4 * n]
        outs = refs[4 * n:]
        for i in range(n):
            g = g_r[i][...]
            outs[4 * i][...] = g
            outs[4 * i + 1][...], outs[4 * i + 2][...], outs[4 * i + 3][...] = _adamw(
                w_r[i][...], g, m_r[i][...], v_r[i][...])

    whole = lambda a: pl.BlockSpec(a.shape, lambda i, nd=a.ndim: (0,) * nd)
    outs = pl.pallas_call(
        body, grid=(1,),
        in_specs=[whole(a) for a in ws] + list(g_specs) + [whole(a) for a in ms] + [whole(a) for a in vs],
        out_specs=[whole(a) for a in ws for _ in range(4)],
        out_shape=[jax.ShapeDtypeStruct(a.shape, F32) for a in ws for _ in range(4)],
        compiler_params=_params(("arbitrary",), VMEM_BIG), name=name,
    )(*ws, *gs, *ms, *vs)
    return [outs[4 * i:4 * i + 4] for i in range(n)]


MASKS = [(mx, my, mc) for mx in (0, 1) for my in (0, 1) for mc in (0, 1)][1:]


def _sc_exchange(name, collective_id, arrays, scatter):
    nt = len(arrays)
    out_type = [jax.ShapeDtypeStruct(a.shape if scatter else (NDEV,) + a.shape, a.dtype) for a in arrays]

    def body(*refs):
        ins, outs = refs[:nt], refs[nt:2 * nt]
        send_sems, recv_sems, local_sems = refs[2 * nt:3 * nt], refs[3 * nt:4 * nt], refs[4 * nt:5 * nt]
        x, y, c = lax.axis_index("x"), lax.axis_index("y"), lax.axis_index("c")
        peers = [(mx + x - 2 * mx * x, my + y - 2 * my * y, mc + c - 2 * mc * c) for mx, my, mc in MASKS]
        barrier = pltpu.get_barrier_semaphore()
        for peer in peers:
            pl.semaphore_signal(barrier, inc=1, device_id=peer, device_id_type=MESH)
        pl.semaphore_wait(barrier, len(peers))
        me = 4 * x + 2 * y + c
        own = []
        for t in range(nt):
            cp = pltpu.make_async_copy(ins[t].at[me] if scatter else ins[t], outs[t].at[me], local_sems[t])
            cp.start()
            own.append(cp)
            for px, py, pc in peers:
                src = ins[t].at[4 * px + 2 * py + pc] if scatter else ins[t]
                pltpu.make_async_remote_copy(src_ref=src, dst_ref=outs[t].at[me], send_sem=send_sems[t],
                                             recv_sem=recv_sems[t], device_id=(px, py, pc), device_id_type=MESH).start()
        for t in range(nt):
            own[t].wait()
            seven = outs[t].at[pl.ds(0, NDEV - 1)]
            drain = pltpu.make_async_remote_copy(src_ref=seven, dst_ref=seven, send_sem=send_sems[t],
                                                 recv_sem=recv_sems[t], device_id=(x, y, c), device_id_type=MESH)
            drain.wait_send()
            drain.wait_recv()

    return pl.kernel(
        body, out_type=out_type, mesh=plsc.ScalarSubcoreMesh(axis_name="sequencer", num_cores=1),
        scratch_types=[pltpu.SemaphoreType.DMA] * (3 * nt),
        compiler_params=pltpu.CompilerParams(collective_id=collective_id), name=name,
    )(*arrays)


def _sc_gather_two_level(name, collective_id, arrays):
    nt = len(arrays)
    out_type = [jax.ShapeDtypeStruct((NDEV,) + a.shape, a.dtype) for a in arrays]

    def body(*refs):
        ins, outs = refs[:nt], refs[nt:2 * nt]
        sems = refs[2 * nt:]
        send_sems, sib_sems, local_sems = sems[:nt], sems[nt:2 * nt], sems[2 * nt:3 * nt]
        ici_sems = [sems[3 * nt + 3 * t:3 * nt + 3 * t + 3] for t in range(nt)]
        x, y, c = lax.axis_index("x"), lax.axis_index("y"), lax.axis_index("c")
        sibling = (x, y, 1 - c)
        chips = [(1 - x, y), (x, 1 - y), (1 - x, 1 - y)]
        barrier = pltpu.get_barrier_semaphore()
        for peer in [sibling] + [(cx, cy, c) for cx, cy in chips]:
            pl.semaphore_signal(barrier, inc=1, device_id=peer, device_id_type=MESH)
        pl.semaphore_wait(barrier, 4)
        me = 4 * x + 2 * y + c

        def push(t, src, slot, recv_sem, to):
            pltpu.make_async_remote_copy(src_ref=src, dst_ref=outs[t].at[slot], send_sem=send_sems[t],
                                         recv_sem=recv_sem, device_id=to, device_id_type=MESH).start()

        own = []
        for t in range(nt):
            cp = pltpu.make_async_copy(ins[t], outs[t].at[me], local_sems[t])
            cp.start()
            own.append(cp)
            for j, (cx, cy) in enumerate(chips):
                push(t, ins[t], me, ici_sems[t][j], (cx, cy, c))
            push(t, ins[t], me, sib_sems[t], sibling)
        for t in range(nt):
            for j, (cx, cy) in enumerate(chips):
                slot = 4 * cx + 2 * cy + c
                landed = outs[t].at[slot]
                pltpu.make_async_remote_copy(src_ref=landed, dst_ref=landed, send_sem=send_sems[t],
                                             recv_sem=ici_sems[t][j], device_id=(cx, cy, c),
                                             device_id_type=MESH).wait_recv()
                push(t, landed, slot, sib_sems[t], sibling)
        for t in range(nt):
            own[t].wait()
            four, seven = outs[t].at[pl.ds(0, 4)], outs[t].at[pl.ds(0, 7)]
            pltpu.make_async_remote_copy(src_ref=four, dst_ref=four, send_sem=send_sems[t], recv_sem=sib_sems[t],
                                         device_id=sibling, device_id_type=MESH).wait_recv()
            pltpu.make_async_remote_copy(src_ref=seven, dst_ref=seven, send_sem=send_sems[t], recv_sem=sib_sems[t],
                                         device_id=sibling, device_id_type=MESH).wait_send()

    return pl.kernel(
        body, out_type=out_type, mesh=plsc.ScalarSubcoreMesh(axis_name="sequencer", num_cores=1),
        scratch_types=[pltpu.SemaphoreType.DMA] * (6 * nt),
        compiler_params=pltpu.CompilerParams(collective_id=collective_id), name=name,
    )(*arrays)


def _sc_sibling_exchange(name, collective_id, src, out_shape, pieces, after=None):
    extra = [] if after is None else [after]

    def body(src_ref, *rest):
        out_ref, send_sem, recv_sem = rest[len(extra):]
        x, y, c = lax.axis_index("x"), lax.axis_index("y"), lax.axis_index("c")
        sibling = (x, y, 1 - c)
        barrier = pltpu.get_barrier_semaphore()
        pl.semaphore_signal(barrier, inc=1, device_id=sibling, device_id_type=MESH)
        pl.semaphore_wait(barrier, 1)
        for piece, lands in pieces(c, src_ref, out_ref):
            pltpu.make_async_remote_copy(src_ref=piece, dst_ref=lands, send_sem=send_sem, recv_sem=recv_sem,
                                         device_id=sibling, device_id_type=MESH).start()
        drain = pltpu.make_async_remote_copy(src_ref=out_ref, dst_ref=out_ref, send_sem=send_sem, recv_sem=recv_sem,
                                             device_id=sibling, device_id_type=MESH)
        drain.wait_send()
        drain.wait_recv()

    return pl.kernel(
        body, out_type=jax.ShapeDtypeStruct(out_shape, src.dtype),
        mesh=plsc.ScalarSubcoreMesh(axis_name="sequencer", num_cores=1), scratch_types=[pltpu.SemaphoreType.DMA] * 2,
        compiler_params=pltpu.CompilerParams(collective_id=collective_id), name=name,
    )(src, *extra)


def _swap_class_columns(name, collective_id, dz, nb, piece=0, npieces=1):
    w = nb // npieces
    return _sc_sibling_exchange(
        name, collective_id, dz, (S, 4 * w),
        lambda c, src, out: [(src.at[:, pl.ds((2 * j + 1 - c) * nb + piece * w, w)], out.at[:, pl.ds(j * w, w)])
                             for j in range(4)])


def _sc_chip_scatter(name, collective_id, q):
    def body(q_ref, out_ref, send_sem, recv_sem, local_sem):
        x, y, c = lax.axis_index("x"), lax.axis_index("y"), lax.axis_index("c")
        chips = [(1 - x, y), (x, 1 - y), (1 - x, 1 - y)]
        barrier = pltpu.get_barrier_semaphore()
        for cx, cy in chips:
            pl.semaphore_signal(barrier, inc=1, device_id=(cx, cy, c), device_id_type=MESH)
        pl.semaphore_wait(barrier, 3)
        mine = 2 * x + y
        own = pltpu.make_async_copy(q_ref.at[mine], out_ref.at[mine], local_sem)
        own.start()
        for cx, cy in chips:
            pltpu.make_async_remote_copy(src_ref=q_ref.at[2 * cx + cy], dst_ref=out_ref.at[mine], send_sem=send_sem,
                                         recv_sem=recv_sem, device_id=(cx, cy, c), device_id_type=MESH).start()
        own.wait()
        three = out_ref.at[pl.ds(0, 3)]
        drain = pltpu.make_async_remote_copy(src_ref=three, dst_ref=three, send_sem=send_sem, recv_sem=recv_sem,
                                             device_id=(x, y, c), device_id_type=MESH)
        drain.wait_send()
        drain.wait_recv()

    return pl.kernel(
        body, out_type=jax.ShapeDtypeStruct(q.shape, q.dtype),
        mesh=plsc.ScalarSubcoreMesh(axis_name="sequencer", num_cores=1), scratch_types=[pltpu.SemaphoreType.DMA] * 3,
        compiler_params=pltpu.CompilerParams(collective_id=collective_id), name=name,
    )(q)


def _mm_pair_dw(h_own, dz, h_sib, dz_sib, nb, name, dep=None, piece=0, npieces=1, h_transposed=False,
                one_call=False):
    nb = nb // npieces
    tn = 512 if nb % 512 == 0 else nb
    per = nb // tn
    dn = NN if h_transposed else TN
    o_spec = pl.BlockSpec((None, D, tn), lambda i, j, k: (j // per, 0, j % per))
    own_col = lambda i, j, k: (0, ((2 * (j // per) + lax.axis_index("c")) * npieces + piece) * per + j % per)
    if one_call:
        def fused(a0_ref, b0_ref, a1_ref, b1_ref, dep_ref, o_ref):
            acc = _dot(a0_ref[...], b0_ref[...], dn) + _dot(a1_ref[...], b1_ref[...], dn)
            o_ref[...] = acc.astype(BF16)

        whole = pl.BlockSpec((S, D), lambda i, j, k: (0, 0), pipeline_mode=pl.Buffered(1))
        return pl.pallas_call(
            fused, grid=(1, 4 * per, 1),
            in_specs=[whole, pl.BlockSpec((S, tn), own_col), whole, pl.BlockSpec((S, tn), lambda i, j, k: (0, j)),
                      ANY_SPEC],
            out_specs=o_spec, out_shape=jax.ShapeDtypeStruct((4, D, nb), BF16),
            compiler_params=_params(("parallel", "parallel", "arbitrary"), VMEM_BIG), name=name,
        )(h_own, dz, h_sib, dz_sib, dep)
    part = _matmul(
        h_own, dz, dn=dn, grid=(1, 4 * per, 1),
        a_spec=pl.BlockSpec((S, D), lambda i, j, k: (0, 0)), b_spec=pl.BlockSpec((S, tn), own_col),
        o_spec=o_spec, out_shape=(4, D, nb), out_dtype=F32, acc_shape=(D, tn), name=name + "_own", dep=dep)

    def body(a_ref, b_ref, p_ref, o_ref):
        o_ref[...] = (p_ref[...] + _dot(a_ref[...], b_ref[...], dn)).astype(BF16)

    return pl.pallas_call(
        body, grid=(1, 4 * per, 1),
        in_specs=[pl.BlockSpec((S, D), lambda i, j, k: (0, 0)), pl.BlockSpec((S, tn), lambda i, j, k: (0, j)), o_spec],
        out_specs=o_spec, out_shape=jax.ShapeDtypeStruct((4, D, nb), BF16),
        compiler_params=_params(("parallel", "parallel", "arbitrary"), VMEM_BIG), name=name + "_sibling",
    )(h_sib, dz_sib, part)


SMALL = {
    "e_pre_norm": ((2048,), None), "e_pool_w": ((4, 256, 256), 1), "e_pool_scale": ((1024,), None),
    "e_post_norm": ((2048,), None), "o_pre_norm": ((2048,), 0), "o_sgu_norm_g": ((1024,), 0),
    "o_sgu_norm_b": ((1024,), 0), "o_sgu_w": ((4, 128, 128), None), "o_sgu_b": ((4, 128), None),
    "o_conv_w": ((31, 1024), 1), "o_conv_b": ((1024,), 0), "o_conv_norm_g": ((1024,), 0),
    "o_conv_norm_b": ((1024,), 0), "o_post_norm": ((2048,), 0),
}
SMALL_SHARDED = [n for n, (_, ax) in SMALL.items() if ax is not None]


def _shard_shape(name):
    shape, ax = SMALL[name]
    if ax is None:
        return shape
    return tuple(s // NDEV if i == ax else s for i, s in enumerate(shape))


def _pack(arrs, row_multiple=1):
    flat = jnp.concatenate([a.reshape(-1) for a in arrs])
    pad = -flat.shape[0] % (128 * row_multiple)
    return jnp.concatenate([flat, jnp.zeros((pad,), F32)]).reshape(-1, 128)


def _small_views(name):
    shape, ax = SMALL[name]
    me = lambda: 4 * lax.axis_index("x") + 2 * lax.axis_index("y") + lax.axis_index("c")
    if ax is None:
        view = (int(np.prod(shape)) // 128, 128)
        return view, view, pl.BlockSpec(view, lambda i: (0, 0))
    if len(shape) == 1:
        n = shape[0] // NDEV
        return (1, n), (NDEV, 1, n), pl.BlockSpec((None, 1, n), lambda i: (me(), 0, 0))
    part = _shard_shape(name)
    return part, shape, pl.BlockSpec(part, lambda i: tuple(me() if d == ax else 0 for d in range(len(shape))))


WEIGHTS = ["e_pre_norm", "e_w_in", "e_pool_w", "e_pool_scale", "e_w_out", "e_post_norm", "o_pre_norm", "o_w_in",
           "o_sgu_norm_g", "o_sgu_norm_b", "o_sgu_w", "o_sgu_b", "o_conv_w", "o_conv_b", "o_conv_norm_g",
           "o_conv_norm_b", "o_w_out", "o_post_norm"]


def kernel(x, e_pre_norm, e_w_in, e_pool_w, e_pool_scale, e_w_out, e_post_norm, o_pre_norm, o_w_in, o_sgu_norm_g, o_sgu_norm_b, o_sgu_w, o_sgu_b, o_conv_w, o_conv_b, o_conv_norm_g, o_conv_norm_b, o_w_out, o_post_norm, loss_target, m_e_pre_norm, m_e_w_in, m_e_pool_w, m_e_pool_scale, m_e_w_out, m_e_post_norm, m_o_pre_norm, m_o_w_in, m_o_sgu_norm_g, m_o_sgu_norm_b, m_o_sgu_w, m_o_sgu_b, m_o_conv_w, m_o_conv_b, m_o_conv_norm_g, m_o_conv_norm_b, m_o_w_out, m_o_post_norm, v_e_pre_norm, v_e_w_in, v_e_pool_w, v_e_pool_scale, v_e_w_out, v_e_post_norm, v_o_pre_norm, v_o_w_in, v_o_sgu_norm_g, v_o_sgu_norm_b, v_o_sgu_w, v_o_sgu_b, v_o_conv_w, v_o_conv_b, v_o_conv_norm_g, v_o_conv_norm_b, v_o_w_out, v_o_post_norm):
    given = dict(locals())
    w = {n: given[n][0] for n in WEIGHTS}
    m = {n: given["m_" + n][0] for n in WEIGHTS}
    v = {n: given["v_" + n][0] for n in WEIGHTS}
    me = 4 * lax.axis_index("x") + 2 * lax.axis_index("y") + lax.axis_index("c")
    x, target = x[0], loss_target[0]
    row = lambda a: a.reshape(1, -1)

    lo, small_rows = _sc_gather_two_level(
        "gather_a0", 0, [_cast_bf16(w["e_w_in"], "cast_e_w_in_0", 0, 2), _pack([w[n] for n in SMALL_SHARDED])])
    hi, = _sc_gather_two_level("gather_a1", 12, [_cast_bf16(w["e_w_in"], "cast_e_w_in_1", 1, 2)])
    wg_e_in = (lo, hi)
    h0, h0t = _pre0_fwd(x, row(w["e_pre_norm"]))
    wg_e_out, = _sc_gather_two_level("gather_b", 1, [_cast_bf16(w["e_w_out"], "cast_e_w_out")])
    wg_o_in, wg_o_out = _sc_gather_two_level(
        "gather_c", 13, [_cast_bf16(w[n], "cast_" + n) for n in ("o_w_in", "o_w_out")])
    p = {n: w[n] for n in SMALL if SMALL[n][1] is None}
    small_rows = small_rows.reshape(NDEV, -1)
    off = 0
    for n in SMALL_SHARDED:
        shp, ax = _shard_shape(n), SMALL[n][1]
        cnt = int(np.prod(shp))
        blk = small_rows[:, off:off + cnt].reshape((NDEV,) + shp)
        p[n] = jnp.moveaxis(blk, 0, ax).reshape(SMALL[n][0])
        off += cnt
    tabs = _rope_tables()
    pool_w_bf = p["e_pool_w"].astype(BF16)
    sgu_bb = jnp.broadcast_to(p["o_sgu_b"][:, :, None], (4, 128, 128))
    conv_w = jnp.concatenate([p["o_conv_w"], jnp.zeros((HALO - CONV_K, HALF), F32)], axis=0)
    odd_p = (row(p["o_sgu_norm_g"]), row(p["o_sgu_norm_b"]), p["o_sgu_w"], sgu_bb, conv_w,
             row(p["o_conv_b"]), row(p["o_conv_norm_g"]), row(p["o_conv_norm_b"]))

    z0 = _mm_in_halves(h0, wg_e_in, "mm_z0")
    ycat0 = _pool_fwd(z0, pool_w_bf, row(p["e_pool_scale"]))
    qkv = _qkv_prep(z0, tabs)
    ycat0, og, lg = _attn_fwd(z0, qkv, ycat0)
    w_out_e, w_out_o = wg_e_out.reshape(2048, D), wg_o_out.reshape(2048, D)
    y0, x1, h1 = _post0_fwd(ycat0, w_out_e, x, row(p["e_post_norm"]), row(p["o_pre_norm"]), wg_e_out)
    h0t_sib = _sc_sibling_exchange("swap_h0", 8, h0t, h0t.shape, lambda c, src, out: [(src, out)], h1)
    h1_sib = _sc_sibling_exchange("swap_h1", 11, h1, h1.shape, lambda c, src, out: [(src, out)])
    z1 = _mm_in(h1, wg_o_in, "mm_z1")
    ycat1, conv_out = _odd_fwd(z1, *odd_p)

    g = {}
    loss_part, dx2, dy1, g["o_post_norm"] = _post1_bwd(ycat1, w_out_o, x1, target, row(p["o_post_norm"]),
                                                       _arrived(h1_sib, "arrived_h_sib", h0t_sib))
    parts = {}
    dw = _mm_out_dw(ycat1, dy1, "mm_dwout1").reshape(NDEV, 256, D)
    parts["o_w_out"], = _sc_exchange("scatter_o_w_out", 2, [dw], True)
    dycat1 = _mm_out_dx(dy1, w_out_o, "mm_dycat1", dw)
    dz1, ddc, g["o_sgu_w"], d_sgu_bb, g["o_sgu_norm_g"], g["o_sgu_norm_b"], g["o_conv_norm_g"], \
        g["o_conv_norm_b"], g["o_conv_b"] = _odd_bwd_a(z1, conv_out, dycat1, *odd_p[:4], *odd_p[6:])
    dz1, d_conv_w = _odd_bwd_b(z1, ddc, dz1, conv_w)
    g["o_sgu_b"] = d_sgu_bb[:, :, 0]
    g["o_conv_w"] = d_conv_w[:CONV_K]
    grads, deltas, new_m, new_v = {}, {}, {}, {}

    def adam(n, dep):
        grads[n], deltas[n], new_m[n], new_v[n] = _adam_reduce(parts[n], w[n], m[n], v[n], "adam_" + n, dep)
        return new_v[n]

    pin = _arrived(parts["o_w_out"], "arrived_o_w_out", d_conv_w)
    dz1_sib = _swap_class_columns("swap_dz1", 10, dz1, ODD_IN // NDEV)
    dw = _mm_pair_dw(h1, dz1, h1_sib, dz1_sib, ODD_IN // NDEV, "mm_dwin1", pin)
    parts["o_w_in"] = _sc_chip_scatter("scatter_o_w_in", 3, dw)
    dh1 = _mm_in_dx(dz1, wg_o_in, "mm_dh1", dw)
    dx1, dy0, g["o_pre_norm"], g["e_post_norm"] = _mid_bwd(dx2, dh1, x1, y0, row(p["o_pre_norm"]),
                                                           row(p["e_post_norm"]))
    dw = _mm_out_dw(ycat0, dy0, "mm_dwout0").reshape(NDEV, 256, D)
    parts["e_w_out"], = _sc_exchange("scatter_e_w_out", 4, [dw], True)
    dycat0 = _mm_out_dx(dy0, w_out_e, "mm_dycat0", dw)
    da_in, da_gate, g["e_pool_w"], g["e_pool_scale"] = _pool_bwd(z0, dycat0, pool_w_bf, row(p["e_pool_scale"]))
    late = [n for n in SMALL if n not in ("e_pre_norm", "o_sgu_b")] + ["o_sgu_b"]
    pieces = [g[n].reshape(SMALL[n][0]) for n in late[:-1]] + [jnp.broadcast_to(loss_part, (8, 128)), g[late[-1]]]
    recv_small, = _sc_gather_two_level("gather_small_grads", 6, [_pack(pieces, 512)])
    took = _arrived(parts["o_w_in"], "arrived_o_w_in")
    dq, dk, dv, dbg = _attn_bwd(z0, qkv, og, lg, dycat0, tabs, took)
    dz0 = jnp.concatenate([da_in, da_gate, dq, dk, dv, dbg], axis=1)
    took = _arrived(recv_small, "arrived_small_grads", _arrived(parts["e_w_out"], "arrived_e_w_out", dz0))
    nb = EVEN_IN // NDEV
    swapped = [_swap_class_columns("swap_dz0_%d" % half, (9, 14)[half], dz0, nb, half, 2) for half in (0, 1)]
    dw, e_w_in_parts = took, []
    for half in (0, 1):
        dw = _mm_pair_dw(h0t, dz0, h0t_sib, swapped[half], nb, "mm_dwin0_%d" % half, dw, half, 2, True, half == 1)
        e_w_in_parts.append(_sc_chip_scatter("scatter_e_w_in_%d" % half, (5, 15)[half], dw))
    pin = adam("e_w_out", adam("o_w_out", adam("o_w_in", dw)))
    rows = [int(np.prod(SMALL[n][0])) // 128 for n in late]
    sums = _sum_unpack(recv_small, rows[:-1] + [8, rows[-1]], "sum_small_grads", pin)
    summed = dict(zip(late, sums[:-2] + sums[-1:]))
    loss = sums[-2][0, 0]
    dh0 = _mm_in_dx_halves(dz0, wg_e_in, "mm_dh0", summed[late[0]])
    grad_x, g["e_pre_norm"] = _pre0_bwd(dx1, dh0, x, row(p["e_pre_norm"]))
    last, = _sc_exchange("gather_e_pre_norm_grad", 7, [g["e_pre_norm"].reshape(16, 128)], False)

    n = "e_w_in"
    out = _adam_reduce(e_w_in_parts[0], w[n], m[n], v[n], "adam_e_w_in_0", grad_x, 0, 2)
    out = _adam_reduce(e_w_in_parts[1], w[n], m[n], v[n], "adam_e_w_in_1", None, 1, 2, out)
    grads[n], deltas[n], new_m[n], new_v[n] = out
    summed["e_pre_norm"] = _sum_parts(last, "sum_e_pre_norm_grad", out[3])
    names = list(SMALL)
    views = [_small_views(n) for n in names]
    mine = lambda src: [src[n].reshape(vw[0]) for n, vw in zip(names, views)]
    res = _adam_small(mine(w), [summed[n].reshape(vw[1]) for n, vw in zip(names, views)], [vw[2] for vw in views],
                      mine(m), mine(v), "adam_small")
    for n, out in zip(names, res):
        grads[n], deltas[n], new_m[n], new_v[n] = [t.reshape(_shard_shape(n)) for t in out]

    lead = lambda a: a[None]
    return (loss, grad_x[None], *[lead(grads[n]) for n in WEIGHTS], *[lead(deltas[n]) for n in WEIGHTS],
            *[lead(new_m[n]) for n in WEIGHTS], *[lead(new_v[n]) for n in WEIGHTS])
```

```python
import numpy as np
import jax
import jax.numpy as jnp
from jax import lax
from jax.experimental import pallas as pl
from jax.experimental.pallas import tpu as pltpu
from jax.experimental.pallas import tpu_sc as plsc

F32 = jnp.float32
BF16 = jnp.bfloat16

S = 2048
D = 2048
NDEV = 8
EPS = 1e-6
NEG = -1e30
HEAD_DIM = 128
ROT_DIM = 32
ROPE_THETA = 500000.0
PATTERNS = ((128, 1), (512, 4), (2048, 16))
BLK = 128
EVEN_IN = 12288
ODD_IN = 6144
HALF = 1024
CONV_K = 31
HALO = 32
TR = 256
SUB = 16

ADAM_LR = 0.001
ADAM_B1 = 0.9
ADAM_B2 = 0.999
ADAM_EPS = 1e-08
ADAM_WD = 0.01
ADAM_STEP = 10

VMEM_BIG = 56 * 1024 * 1024
MESH = pl.DeviceIdType.MESH

NN = (((1,), (0,)), ((), ()))
NT = (((1,), (1,)), ((), ()))
TN = (((0,), (0,)), ((), ()))


def _dot(a, b, dn=NN):
    return lax.dot_general(a, b, dn, preferred_element_type=F32)


def _sigmoid(x):
    return 1.0 / (1.0 + jnp.exp(-x))


def _silu_and_grad(x):
    sg = _sigmoid(x)
    return x * sg, sg * (1.0 + x * (1.0 - sg))


def _params(sem, vmem=None):
    return pltpu.CompilerParams(dimension_semantics=sem, vmem_limit_bytes=vmem)


ANY_SPEC = pl.BlockSpec(memory_space=pl.ANY)


def _matmul(a, b, *, dn, grid, a_spec, b_spec, o_spec, out_shape, out_dtype, acc_shape, name, dep=None):
    nk = grid[2]
    deps = [] if dep is None else list(dep) if isinstance(dep, (tuple, list)) else [dep]

    def body(a_ref, b_ref, *rest):
        o_ref, acc = rest[len(deps)], rest[len(deps) + 1:]
        if nk == 1:
            o_ref[...] = _dot(a_ref[...], b_ref[...], dn).astype(o_ref.dtype)
            return
        acc_ref = acc[0]
        k = pl.program_id(2)

        @pl.when(k == 0)
        def _():
            acc_ref[...] = jnp.zeros_like(acc_ref)

        acc_ref[...] += _dot(a_ref[...], b_ref[...], dn)

        @pl.when(k == nk - 1)
        def _():
            o_ref[...] = acc_ref[...].astype(o_ref.dtype)

    return pl.pallas_call(
        body, grid=grid, in_specs=[a_spec, b_spec] + [ANY_SPEC] * len(deps), out_specs=o_spec,
        out_shape=jax.ShapeDtypeStruct(out_shape, out_dtype),
        scratch_shapes=[] if nk == 1 else [pltpu.VMEM(acc_shape, F32)],
        compiler_params=_params(("parallel", "parallel", "arbitrary"), VMEM_BIG), name=name,
    )(a, b, *deps)


TM = 2048


def _mm_in(h, wg, name):
    nb = wg.shape[2]
    tn = 512 if nb % 512 == 0 else nb
    per = nb // tn
    return _matmul(
        h, wg, dn=NN, grid=(S // TM, NDEV * per, 1),
        a_spec=pl.BlockSpec((TM, D), lambda i, j, k: (i, 0)),
        b_spec=pl.BlockSpec((None, D, tn), lambda i, j, k: (j // per, 0, j % per)),
        o_spec=pl.BlockSpec((TM, tn), lambda i, j, k: (i, j)),
        out_shape=(S, NDEV * nb), out_dtype=F32, acc_shape=(TM, tn), name=name)


def _mm_in_halves(h, wg_halves, name):
    hb = wg_halves[0].shape[2]
    z = None
    for half, wg in enumerate(wg_halves):
        prev = [] if z is None else [z]

        def body(a_ref, b_ref, *rest):
            rest[-1][...] = _dot(a_ref[...], b_ref[...])

        z = pl.pallas_call(
            body, grid=(NDEV,),
            in_specs=[pl.BlockSpec((S, D), lambda j: (0, 0)), pl.BlockSpec((None, D, hb), lambda j: (j, 0, 0))]
                     + [ANY_SPEC] * len(prev),
            out_specs=pl.BlockSpec((S, hb), lambda j, half=half: (0, 2 * j + half)),
            out_shape=jax.ShapeDtypeStruct((S, 2 * NDEV * hb), F32),
            input_output_aliases={2: 0} if prev else {},
            compiler_params=_params(("parallel",), VMEM_BIG), name="%s_%d" % (name, half),
        )(h, wg, *prev)
    return z


def _mm_in_dx_halves(dz, wg_halves, name, dep):
    hb = wg_halves[0].shape[2]
    nk = 2 * NDEV

    def body(a_ref, b0_ref, b1_ref, dep_ref, o_ref, acc_ref):
        k = pl.program_id(2)

        @pl.when(k == 0)
        def _():
            acc_ref[...] = jnp.zeros_like(acc_ref)

        @pl.when(k % 2 == 0)
        def _():
            acc_ref[...] += _dot(a_ref[...], b0_ref[...], NT)

        @pl.when(k % 2 == 1)
        def _():
            acc_ref[...] += _dot(a_ref[...], b1_ref[...], NT)

        @pl.when(k == nk - 1)
        def _():
            o_ref[...] = acc_ref[...]

    b_spec = pl.BlockSpec((None, 1024, hb), lambda i, j, k: (k // 2, j, 0))
    return pl.pallas_call(
        body, grid=(1, D // 1024, nk),
        in_specs=[pl.BlockSpec((S, hb), lambda i, j, k: (0, k)), b_spec, b_spec, ANY_SPEC],
        out_specs=pl.BlockSpec((S, 1024), lambda i, j, k: (0, j)), out_shape=jax.ShapeDtypeStruct((S, D), F32),
        scratch_shapes=[pltpu.VMEM((S, 1024), F32)],
        compiler_params=_params(("parallel", "parallel", "arbitrary"), VMEM_BIG), name=name,
    )(dz, *wg_halves, dep)


def _mm_in_dx(dz, wg, name, dep=None):
    nb = wg.shape[2]
    return _matmul(
        dz, wg, dn=NT, grid=(S // TM, D // 1024, NDEV),
        a_spec=pl.BlockSpec((TM, nb), lambda i, j, k: (i, k)),
        b_spec=pl.BlockSpec((None, 1024, nb), lambda i, j, k: (k, j, 0)),
        o_spec=pl.BlockSpec((TM, 1024), lambda i, j, k: (i, j)),
        out_shape=(S, D), out_dtype=F32, acc_shape=(TM, 1024), name=name, dep=dep)


def _mm_out_dx(dy, w, name, dep=None):
    return _matmul(
        dy, w, dn=NT, grid=(S // TM, 2048 // 512, 1),
        a_spec=pl.BlockSpec((TM, D), lambda i, j, k: (i, 0)),
        b_spec=pl.BlockSpec((512, D), lambda i, j, k: (j, 0)),
        o_spec=pl.BlockSpec((TM, 512), lambda i, j, k: (i, j)),
        out_shape=(S, 2048), out_dtype=F32, acc_shape=(TM, 512), name=name, dep=dep)


def _mm_out_dw(yc, dy, name):
    return _matmul(
        yc, dy, dn=TN, grid=(2048 // TM, D // 512, 1),
        a_spec=pl.BlockSpec((S, TM), lambda i, j, k: (0, i)),
        b_spec=pl.BlockSpec((S, 512), lambda i, j, k: (0, j)),
        o_spec=pl.BlockSpec((TM, 512), lambda i, j, k: (i, j)),
        out_shape=(2048, D), out_dtype=BF16, acc_shape=(TM, 512), name=name)


def _row_spec(w=D):
    return pl.BlockSpec((TR, w), lambda i: (i, 0))


def _vec_spec(w=D):
    return pl.BlockSpec((1, w), lambda i: (0, 0))


def _rms_stats(x):
    r = lax.rsqrt(jnp.mean(x * x, axis=-1, keepdims=True) + EPS)
    return x * r, r


def _rms_bwd(dn, xhat, r, g):
    dxh = dn * g
    return r * (dxh - xhat * jnp.mean(dxh * xhat, axis=-1, keepdims=True))


def _acc_rows(ref, val, i):
    s = jnp.sum(val, axis=0, keepdims=True)

    @pl.when(i == 0)
    def _():
        ref[...] = s

    @pl.when(i > 0)
    def _():
        ref[...] += s


def _pre0_fwd(x, g):
    def body(x_ref, g_ref, h_ref, ht_ref):
        xhat, _ = _rms_stats(x_ref[...])
        h = xhat * g_ref[...]
        h_ref[...] = h.astype(BF16)
        ht_ref[...] = h.T.astype(BF16)

    return pl.pallas_call(
        body, grid=(S // TR,), in_specs=[_row_spec(), _vec_spec()],
        out_specs=[_row_spec(), pl.BlockSpec((D, TR), lambda i: (0, i))],
        out_shape=[jax.ShapeDtypeStruct((S, D), BF16), jax.ShapeDtypeStruct((D, S), BF16)],
        compiler_params=_params(("parallel",)), name="pre0_fwd",
    )(x, g)


def _post0_fwd(ycat, w_out, x, g_post, g_pre1, dep):
    def body(yc_ref, w_ref, x_ref, gp_ref, g1_ref, dep_ref, y_ref, x1_ref, h1_ref):
        y = _dot(yc_ref[...], w_ref[...])
        y_ref[...] = y
        yhat, _ = _rms_stats(y)
        x1 = x_ref[...] + yhat * gp_ref[...]
        x1_ref[...] = x1
        xhat, _ = _rms_stats(x1)
        h1_ref[...] = (xhat * g1_ref[...]).astype(BF16)

    return pl.pallas_call(
        body, grid=(S // TR,),
        in_specs=[_row_spec(), pl.BlockSpec((2048, D), lambda i: (0, 0)), _row_spec(), _vec_spec(), _vec_spec(),
                  ANY_SPEC],
        out_specs=[_row_spec(), _row_spec(), _row_spec()],
        out_shape=[jax.ShapeDtypeStruct((S, D), F32), jax.ShapeDtypeStruct((S, D), F32),
                   jax.ShapeDtypeStruct((S, D), BF16)],
        compiler_params=_params(("parallel",), VMEM_BIG), name="post0_fwd",
    )(ycat, w_out, x, g_post, g_pre1, dep)


def _post1_bwd(ycat, w_out, x1, target, g_post, dep):
    def body(yc_ref, w_ref, x1_ref, t_ref, g_ref, dep_ref, loss_ref, dx2_ref, dy_ref, dg_ref):
        i = pl.program_id(0)
        yhat, r = _rms_stats(_dot(yc_ref[...], w_ref[...]))
        g = g_ref[...]
        err = x1_ref[...] + yhat * g - t_ref[...]
        part = jnp.sum(jnp.sum(err * err, axis=-1, keepdims=True), axis=0, keepdims=True) * (0.5 / D)
        _acc_rows(loss_ref, jnp.broadcast_to(part, (1, 128)), i)
        dx2 = err * (1.0 / D)
        dx2_ref[...] = dx2
        _acc_rows(dg_ref, dx2 * yhat, i)
        dy_ref[...] = _rms_bwd(dx2, yhat, r, g).astype(BF16)

    return pl.pallas_call(
        body, grid=(S // TR,),
        in_specs=[_row_spec(), pl.BlockSpec((2048, D), lambda i: (0, 0)), _row_spec(), _row_spec(), _vec_spec(),
                  ANY_SPEC],
        out_specs=[_vec_spec(128), _row_spec(), _row_spec(), _vec_spec()],
        out_shape=[jax.ShapeDtypeStruct((1, 128), F32), jax.ShapeDtypeStruct((S, D), F32),
                   jax.ShapeDtypeStruct((S, D), BF16), jax.ShapeDtypeStruct((1, D), F32)],
        compiler_params=_params(("arbitrary",), VMEM_BIG), name="post1_bwd",
    )(ycat, w_out, x1, target, g_post, dep)


def _mid_bwd(dx2, dh1, x1, y0, g_pre1, g_post0):
    def body(dx2_ref, dh_ref, x1_ref, y_ref, g1_ref, gp_ref, dx1_ref, dy_ref, dg1_ref, dgp_ref):
        i = pl.program_id(0)
        xhat, r1 = _rms_stats(x1_ref[...])
        dh = dh_ref[...]
        _acc_rows(dg1_ref, dh * xhat, i)
        dx1 = dx2_ref[...] + _rms_bwd(dh, xhat, r1, g1_ref[...])
        dx1_ref[...] = dx1
        yhat, r0 = _rms_stats(y_ref[...])
        _acc_rows(dgp_ref, dx1 * yhat, i)
        dy_ref[...] = _rms_bwd(dx1, yhat, r0, gp_ref[...]).astype(BF16)

    return pl.pallas_call(
        body, grid=(S // TR,),
        in_specs=[_row_spec(), _row_spec(), _row_spec(), _row_spec(), _vec_spec(), _vec_spec()],
        out_specs=[_row_spec(), _row_spec(), _vec_spec(), _vec_spec()],
        out_shape=[jax.ShapeDtypeStruct((S, D), F32), jax.ShapeDtypeStruct((S, D), BF16),
                   jax.ShapeDtypeStruct((1, D), F32), jax.ShapeDtypeStruct((1, D), F32)],
        compiler_params=_params(("arbitrary",)), name="mid_bwd",
    )(dx2, dh1, x1, y0, g_pre1, g_post0)


def _pre0_bwd(dx1, dh0, x, g):
    def body(dx1_ref, dh_ref, x_ref, g_ref, gx_ref, dg_ref):
        i = pl.program_id(0)
        xhat, r = _rms_stats(x_ref[...])
        dh = dh_ref[...]
        _acc_rows(dg_ref, dh * xhat, i)
        gx_ref[...] = dx1_ref[...] + _rms_bwd(dh, xhat, r, g_ref[...])

    return pl.pallas_call(
        body, grid=(S // TR,), in_specs=[_row_spec(), _row_spec(), _row_spec(), _vec_spec()],
        out_specs=[_row_spec(), _vec_spec()],
        out_shape=[jax.ShapeDtypeStruct((S, D), F32), jax.ShapeDtypeStruct((1, D), F32)],
        compiler_params=_params(("arbitrary",)), name="pre0_bwd",
    )(dx1, dh0, x, g)


POOL_CH = 256


def _pool_apply(a, w, transpose):
    n = a.shape[0]
    row = lax.broadcasted_iota(jnp.int32, a.shape, 0)
    cnt = jnp.minimum(row + 1, w).astype(F32)
    s = a / cnt if transpose else a
    for k in (1, 2, 4, 8):
        if transpose:
            sh = jnp.where(row < n - k, pltpu.roll(s, n - k, 0), 0.0)
        else:
            sh = jnp.where(row >= k, pltpu.roll(s, k, 0), 0.0)
        s = jnp.where(w > k, s + sh, s)
    return s - a if transpose else s / cnt - a


def _pool_fwd(z0, pool_w, pool_scale):
    def body(a_ref, gate_ref, w_ref, sc_ref, out_ref):
        win = jnp.left_shift(2, pl.program_id(0))
        pooled = _pool_apply(a_ref[...], win, False)
        mixed = _dot(pooled.astype(BF16), w_ref[...])
        gate = gate_ref[...]
        out_ref[...] = (mixed * sc_ref[...] * (gate * _sigmoid(gate))).astype(BF16)

    return pl.pallas_call(
        body, grid=(4,),
        in_specs=[pl.BlockSpec((S, POOL_CH), lambda g: (0, g)), pl.BlockSpec((S, POOL_CH), lambda g: (0, 4 + g)),
                  pl.BlockSpec((None, POOL_CH, POOL_CH), lambda g: (g, 0, 0)),
                  pl.BlockSpec((1, POOL_CH), lambda g: (0, g))],
        out_specs=pl.BlockSpec((S, POOL_CH), lambda g: (0, g)),
        out_shape=jax.ShapeDtypeStruct((S, 2048), BF16),
        compiler_params=_params(("parallel",), VMEM_BIG), name="pool_fwd",
    )(z0, z0, pool_w, pool_scale)


def _pool_bwd(z0, dycat, pool_w, pool_scale):
    def body(a_ref, gate_ref, dy_ref, w_ref, sc_ref, da_ref, dgate_ref, dw_ref, dsc_ref):
        win = jnp.left_shift(2, pl.program_id(0))
        pooled = _pool_apply(a_ref[...], win, False).astype(BF16)
        w = w_ref[...]
        mixed = _dot(pooled, w)
        silu, dsilu = _silu_and_grad(gate_ref[...])
        dy = dy_ref[...]
        sc = sc_ref[...]
        dgate_ref[...] = (dy * (mixed * sc) * dsilu).astype(BF16)
        dms = dy * silu
        dsc_ref[...] = jnp.sum(dms * mixed, axis=0, keepdims=True)
        dmixed = (dms * sc).astype(BF16)
        dw_ref[...] = _dot(pooled, dmixed, TN)
        dpooled = _dot(dmixed, w, NT)
        da_ref[...] = _pool_apply(dpooled, win, True).astype(BF16)

    slab = lambda off: pl.BlockSpec((S, POOL_CH), lambda g: (0, off + g))
    return pl.pallas_call(
        body, grid=(4,),
        in_specs=[slab(0), slab(4), slab(0), pl.BlockSpec((None, POOL_CH, POOL_CH), lambda g: (g, 0, 0)),
                  pl.BlockSpec((1, POOL_CH), lambda g: (0, g))],
        out_specs=[slab(0), slab(0), pl.BlockSpec((None, POOL_CH, POOL_CH), lambda g: (g, 0, 0)),
                   pl.BlockSpec((1, POOL_CH), lambda g: (0, g))],
        out_shape=[jax.ShapeDtypeStruct((S, HALF), BF16), jax.ShapeDtypeStruct((S, HALF), BF16),
                   jax.ShapeDtypeStruct((4, POOL_CH, POOL_CH), F32), jax.ShapeDtypeStruct((1, HALF), F32)],
        compiler_params=_params(("parallel",), VMEM_BIG), name="pool_bwd",
    )(z0, z0, dycat, pool_w, pool_scale)


Q_COL, K_COL, V_COL, BG_COL = 2048 // 128, 5120 // 128, 8192 // 128, 11264 // 128
SCALE = HEAD_DIM ** -0.5


def _rope_tables():
    pos = jnp.arange(S, dtype=F32)
    inv_freq = jnp.power(ROPE_THETA, -jnp.arange(0, ROT_DIM, 2, dtype=F32) / ROT_DIM)
    ang = pos[:, None] * inv_freq[None, :]
    cos, sin = jnp.cos(ang), jnp.sin(ang)
    half = ROT_DIM // 2
    zeros = jnp.zeros((S, HEAD_DIM - ROT_DIM), F32)
    c = jnp.concatenate([cos, cos, jnp.ones((S, HEAD_DIM - ROT_DIM), F32)], axis=1)
    a = jnp.concatenate([-sin, jnp.zeros((S, half), F32), zeros], axis=1)
    b = jnp.concatenate([jnp.zeros((S, half), F32), sin, zeros], axis=1)
    return c, a, b


def _rope(t, c, a, b):
    half = ROT_DIM // 2
    return t * c + pltpu.roll(t, HEAD_DIM - half, 1) * a + pltpu.roll(t, half, 1) * b


def _rope_t(d, c, a, b):
    half = ROT_DIM // 2
    return d * c + pltpu.roll(d * a, half, 1) + pltpu.roll(d * b, HEAD_DIM - half, 1)


def _deinterleave(dst, src, dil, cast=None, dst_off=0):
    length = S // dil
    for r in range(dil):
        v = src[...] if dil == 1 else src[pl.ds(r, length, stride=dil), :]
        dst[dst_off + r * length:dst_off + (r + 1) * length, :] = v if cast is None else v.astype(cast)


def _interleave(dst, src, dil, src_off=0):
    length = S // dil
    for r in range(dil):
        if dil == 1:
            dst[...] = src[src_off:src_off + S, :]
        else:
            dst[pl.ds(r, length, stride=dil), :] = src[src_off + r * length:src_off + (r + 1) * length, :]


CU = 8
NUNITS = S // BLK
B_QK = (((2,), (2,)), ((0,), (0,)))
B_PV = (((2,), (1,)), ((0,), (0,)))
B_TN = (((1,), (1,)), ((0,), (0,)))


def _blocks(ref, first):
    return ref[first * BLK:(first + CU) * BLK, :].reshape(CU, BLK, HEAD_DIM)


def _chunk_scores(u0, nb, qd, kdp):
    q = _blocks(qd, u0)
    row = lax.broadcasted_iota(jnp.int32, (CU, BLK, BLK), 1)
    col = lax.broadcasted_iota(jnp.int32, (CU, BLK, BLK), 2)
    s_own = jnp.where(col <= row, _dot(q, _blocks(kdp, u0 + 1), B_QK) * SCALE, NEG)
    if nb == 1:
        return q, s_own, None
    unit = lax.broadcasted_iota(jnp.int32, (CU, BLK, BLK), 0) + u0
    s_prev = jnp.where((col >= row) & ((unit % nb) != 0), _dot(q, _blocks(kdp, u0), B_QK) * SCALE, NEG)
    return q, s_own, s_prev


def _qkv_prep(z0, tabs):
    def body(q_ref, k_ref, v_ref, c_ref, a_ref, b_ref, qo_ref, ko_ref, vo_ref, tmp):
        p = pl.program_id(1)
        ko_ref[0:BLK, :] = jnp.zeros((BLK, HEAD_DIM), BF16)
        vo_ref[0:BLK, :] = jnp.zeros((BLK, HEAD_DIM), BF16)
        for gi, (_, dil) in enumerate(PATTERNS):
            @pl.when(p == gi)
            def _(dil=dil):
                c, a, b = c_ref[...], a_ref[...], b_ref[...]
                tmp[...] = _rope(q_ref[...], c, a, b)
                _deinterleave(qo_ref, tmp, dil, BF16)
                tmp[...] = _rope(k_ref[...], c, a, b)
                _deinterleave(ko_ref, tmp, dil, BF16, BLK)
                _deinterleave(vo_ref, v_ref, dil, BF16, BLK)

    tab = pl.BlockSpec((S, HEAD_DIM), lambda h, p: (0, 0))
    out = pl.BlockSpec((S, HEAD_DIM), lambda h, p: (0, p * 8 + h))
    outp = pl.BlockSpec((S + BLK, HEAD_DIM), lambda h, p: (0, p * 8 + h))
    return pl.pallas_call(
        body, grid=(8, 3), in_specs=[_head_spec(Q_COL), _head_spec(K_COL), _head_spec(V_COL), tab, tab, tab],
        out_specs=[out, outp, outp],
        out_shape=[jax.ShapeDtypeStruct((S, 3072), BF16)] + [jax.ShapeDtypeStruct((S + BLK, 3072), BF16)] * 2,
        scratch_shapes=[pltpu.VMEM((S, HEAD_DIM), F32)],
        compiler_params=_params(("parallel", "arbitrary"), VMEM_BIG), name="qkv_prep",
    )(z0, z0, z0, *tabs)


def _attn_group_fwd(dil, qd, kdp, vdp, od, ld, og, lg):
    nb = S // dil // BLK
    for u0 in range(0, NUNITS, CU):
        _, s_own, s_prev = _chunk_scores(u0, nb, qd, kdp)
        m = jnp.max(s_own, axis=2, keepdims=True)
        if s_prev is not None:
            m = jnp.maximum(m, jnp.max(s_prev, axis=2, keepdims=True))
        p_own = jnp.exp(s_own - m)
        den = jnp.sum(p_own, axis=2, keepdims=True)
        acc = _dot(p_own.astype(BF16), _blocks(vdp, u0 + 1), B_PV)
        if s_prev is not None:
            p_prev = jnp.exp(s_prev - m)
            den = den + jnp.sum(p_prev, axis=2, keepdims=True)
            acc = acc + _dot(p_prev.astype(BF16), _blocks(vdp, u0), B_PV)
        rows = slice(u0 * BLK, (u0 + CU) * BLK)
        od[rows, :] = (acc / den).reshape(CU * BLK, HEAD_DIM)
        ld[rows, :] = jnp.broadcast_to(m + jnp.log(den), (CU, BLK, HEAD_DIM)).reshape(CU * BLK, HEAD_DIM)
    _interleave(og, od, dil)
    _interleave(lg, ld, dil)


def _group_weights(lgs):
    l0, l1, l2 = lgs[0][...], lgs[1][...], lgs[2][...]
    mx = jnp.maximum(l0, jnp.maximum(l1, l2))
    e0, e1, e2 = jnp.exp(l0 - mx), jnp.exp(l1 - mx), jnp.exp(l2 - mx)
    den = e0 + e1 + e2
    return e0 / den, e1 / den, e2 / den


def _head_spec(base):
    return pl.BlockSpec((S, HEAD_DIM), lambda h, p: (0, base + (p % 3) * 8 + h))


def _slab(dtype=F32, rows=S):
    return pltpu.VMEM((rows, HEAD_DIM), dtype)


def _attn_fwd(z0, qkv, ycat):
    def body(q_ref, k_ref, v_ref, gate_ref, ycat_ref, out_ref, og_ref, lg_ref,
             od, ld, og0, og1, og2, lg0, lg1, lg2):
        del ycat_ref
        p = pl.program_id(1)
        ogs, lgs = (og0, og1, og2), (lg0, lg1, lg2)
        for gi, (_, dil) in enumerate(PATTERNS):
            @pl.when(p == gi)
            def _(gi=gi, dil=dil):
                _attn_group_fwd(dil, q_ref, k_ref, v_ref, od, ld, ogs[gi], lgs[gi])
                og_ref[...] = ogs[gi][...]
                lg_ref[...] = lgs[gi][...]

        @pl.when(p == 2)
        def _():
            w0, w1, w2 = _group_weights(lgs)
            o = w0 * og0[...] + w1 * og1[...] + w2 * og2[...]
            gate = gate_ref[...]
            out_ref[...] = (o * (gate * _sigmoid(gate))).astype(BF16)

    grp = pl.BlockSpec((S, HEAD_DIM), lambda h, p: (0, p * 8 + h))
    grp_pad = pl.BlockSpec((S + BLK, HEAD_DIM), lambda h, p: (0, p * 8 + h))
    return pl.pallas_call(
        body, grid=(8, 3),
        in_specs=[grp, grp_pad, grp_pad, pl.BlockSpec((S, HEAD_DIM), lambda h, p: (0, BG_COL + h)), ANY_SPEC],
        out_specs=[pl.BlockSpec((S, HEAD_DIM), lambda h, p: (0, 8 + h)), grp, grp],
        out_shape=[jax.ShapeDtypeStruct((S, 2048), BF16), jax.ShapeDtypeStruct((S, 3072), F32),
                   jax.ShapeDtypeStruct((S, 3072), F32)],
        scratch_shapes=[_slab() for _ in range(8)],
        input_output_aliases={4: 0},
        compiler_params=_params(("parallel", "arbitrary"), VMEM_BIG), name="attn_fwd",
    )(*qkv, z0, ycat)


def _attn_bwd(z0, qkv, og, lg, dycat, tabs, dep):
    def body(q_ref, k_ref, v_ref, gate_ref, dy_ref, c_ref, a_ref, b_ref,
             og0_ref, og1_ref, og2_ref, lg0_ref, lg1_ref, lg2_ref, dep_ref,
             dq_ref, dk_ref, dv_ref, dbg_ref,
             tmp, ld, dg0, dg1, dg2, cg0, cg1, cg2, dod, cd, dqd, dkd, dvd):
        kd, vd = k_ref, v_ref
        p = pl.program_id(1)
        ogs, lgs, dgs, cgs = (og0_ref, og1_ref, og2_ref), (lg0_ref, lg1_ref, lg2_ref), (dg0, dg1, dg2), (cg0, cg1, cg2)

        @pl.when(p == 0)
        def _():
            w = _group_weights(lgs)
            o = w[0] * ogs[0][...] + w[1] * ogs[1][...] + w[2] * ogs[2][...]
            silu, dsilu = _silu_and_grad(gate_ref[...])
            dy = dy_ref[...]
            dbg_ref[...] = (dy * o * dsilu).astype(BF16)
            do = dy * silu
            dwbar = jnp.sum(do * o, axis=1, keepdims=True)
            for gi in range(3):
                dgs[gi][...] = w[gi] * do
                cgs[gi][...] = -w[gi] * dwbar

        for gi, (_, dil) in enumerate(PATTERNS):
            @pl.when(p == 1 + gi)
            def _(gi=gi, dil=dil):
                nb = S // dil // BLK
                qd = q_ref
                c, a, b = c_ref[...], a_ref[...], b_ref[...]
                _deinterleave(dod, dgs[gi], dil, BF16)
                _deinterleave(ld, lgs[gi], dil)
                _deinterleave(cd, cgs[gi], dil)
                dkd[...] = jnp.zeros_like(dkd)
                dvd[...] = jnp.zeros_like(dvd)
                flat = lambda t: t.reshape(CU * BLK, HEAD_DIM)
                for u0 in range(0, NUNITS, CU):
                    q, s_own, s_prev = _chunk_scores(u0, nb, qd, kd)
                    lse, cv, do = _blocks(ld, u0), _blocks(cd, u0), _blocks(dod, u0)
                    own = slice((u0 + 1) * BLK, (u0 + 1 + CU) * BLK)
                    p_own = jnp.exp(s_own - lse)
                    ds_own = (p_own * (_dot(do, _blocks(vd, u0 + 1), B_QK) + cv) * SCALE).astype(BF16)
                    dq = _dot(ds_own, _blocks(kd, u0 + 1), B_PV)
                    dkd[own, :] += flat(_dot(ds_own, q, B_TN))
                    dvd[own, :] += flat(_dot(p_own.astype(BF16), do, B_TN))
                    if s_prev is not None:
                        prev = slice(u0 * BLK, (u0 + CU) * BLK)
                        p_prev = jnp.exp(s_prev - lse)
                        ds_prev = (p_prev * (_dot(do, _blocks(vd, u0), B_QK) + cv) * SCALE).astype(BF16)
                        dq = dq + _dot(ds_prev, _blocks(kd, u0), B_PV)
                        dkd[prev, :] += flat(_dot(ds_prev, q, B_TN))
                        dvd[prev, :] += flat(_dot(p_prev.astype(BF16), do, B_TN))
                    dqd[u0 * BLK:(u0 + CU) * BLK, :] = flat(dq)
                _interleave(tmp, dqd, dil)
                dq_ref[...] = _rope_t(tmp[...], c, a, b).astype(BF16)
                _interleave(tmp, dkd, dil, BLK)
                dk_ref[...] = _rope_t(tmp[...], c, a, b).astype(BF16)
                _interleave(tmp, dvd, dil, BLK)
                dv_ref[...] = tmp[...].astype(BF16)

    tab = pl.BlockSpec((S, HEAD_DIM), lambda h, p: (0, 0))
    hspec = lambda base: pl.BlockSpec((S, HEAD_DIM), lambda h, p: (0, base + h))
    gspec = pl.BlockSpec((S, HEAD_DIM), lambda h, p: (0, jnp.maximum(p - 1, 0) * 8 + h))
    gspec_pad = pl.BlockSpec((S + BLK, HEAD_DIM), lambda h, p: (0, jnp.maximum(p - 1, 0) * 8 + h))
    return pl.pallas_call(
        body, grid=(8, 4),
        in_specs=[gspec, gspec_pad, gspec_pad, hspec(BG_COL), hspec(8), tab, tab, tab,
                  hspec(0), hspec(8), hspec(16), hspec(0), hspec(8), hspec(16), ANY_SPEC],
        out_specs=[gspec, gspec, gspec, hspec(0)],
        out_shape=[jax.ShapeDtypeStruct((S, 3072), BF16)] * 3 + [jax.ShapeDtypeStruct((S, HALF), BF16)],
        scratch_shapes=[_slab(), _slab()] + [_slab() for _ in range(6)]
                       + [_slab(BF16), _slab(), _slab(), _slab(F32, S + BLK), _slab(F32, S + BLK)],
        compiler_params=_params(("parallel", "arbitrary"), VMEM_BIG), name="attn_bwd",
    )(*qkv, z0, dycat, *tabs, og, og, og, lg, lg, lg, dep)


SGU_CH = 256
NCHUNK = TR // 128


def _ln_stats(x):
    mu = jnp.mean(x, axis=-1, keepdims=True)
    xc = x - mu
    r = lax.rsqrt(jnp.mean(xc * xc, axis=-1, keepdims=True) + EPS)
    return xc * r, r


def _ln_bwd(dy, xhat, r, g):
    dxh = dy * g
    return r * (dxh - jnp.mean(dxh, axis=-1, keepdims=True) - xhat * jnp.mean(dxh * xhat, axis=-1, keepdims=True))


def _tril_bf16(w):
    row = lax.broadcasted_iota(jnp.int32, w.shape, 0)
    col = lax.broadcasted_iota(jnp.int32, w.shape, 1)
    return jnp.where(row >= col, w, 0.0).astype(BF16)


def _sgu_gate(vn_s, s_s, w_ref, bb_ref):
    for h in range(4):
        wm = _tril_bf16(w_ref[h])
        bias = bb_ref[h]
        for ch in range(NCHUNK):
            rows, cols = slice(ch * 128, (ch + 1) * 128), slice(h * SGU_CH, (h + 1) * SGU_CH)
            s_s[rows, cols] = _dot(wm, vn_s[rows, cols]) + jnp.concatenate([bias, bias], axis=1)


WIN = HALO + TR
SUBL = 8


def _shifted_copies(dst, src):
    dst[0] = src[...]
    for b in range(1, SUBL):
        dst[b, 0:WIN - SUBL, :] = src[pl.ds(b, WIN - SUBL), :]


def _rows_at(copies, off, n):
    return copies[off % SUBL, pl.ds(off - off % SUBL, n), :]


def _conv_fwd(i, dval_ref, dglu_ref, hval_ref, hglu_ref, cw_ref, cb_ref, xw, xr, dcs):
    halo = hval_ref[...] * _sigmoid(hglu_ref[...])
    xw[0:HALO, :] = jnp.where(i > 0, halo, 0.0)
    xw[HALO:HALO + TR, :] = dval_ref[...] * _sigmoid(dglu_ref[...])
    _shifted_copies(xr, xw)
    sub = 2 * SUB
    for rb in range(TR // sub):
        acc = jnp.broadcast_to(cb_ref[...], (sub, HALF))
        for k in range(CONV_K):
            acc = acc + cw_ref[k:k + 1, :] * _rows_at(xr, rb * sub + HALO - (CONV_K - 1) + k, sub)
        dcs[rb * sub:(rb + 1) * sub, :] = acc


def _odd_in_specs():
    col = lambda j: pl.BlockSpec((TR, HALF), lambda i, *_: (i, j))
    prev = lambda j: pl.BlockSpec((HALO, HALF), lambda i, *_: (jnp.maximum(i * (TR // HALO) - 1, 0), j))
    return [col(0), col(1), col(2), col(3), col(4), col(5), prev(3), prev(4)]


def _full_spec(shape):
    return pl.BlockSpec(shape, lambda i, *_: (0,) * len(shape))


def _odd_fwd(z1, sgu_g, sgu_b, sgu_w, sgu_bb, conv_w, conv_b, cn_g, cn_b):
    def body(u_ref, v_ref, cg_ref, dval_ref, dglu_ref, dgate_ref, hval_ref, hglu_ref,
             g_ref, b_ref, w_ref, bb_ref, cw_ref, cb_ref, cng_ref, cnb_ref, out_ref, dcs, vn_s, s_s, xw, xr):
        i = pl.program_id(0)
        vhat, _ = _ln_stats(v_ref[...])
        vn_s[...] = (vhat * g_ref[...] + b_ref[...]).astype(BF16)
        _sgu_gate(vn_s, s_s, w_ref, bb_ref)
        cg = cg_ref[...]
        out_ref[:, 0:HALF] = (u_ref[...] * s_s[...] * (cg * _sigmoid(cg))).astype(BF16)
        _conv_fwd(i, dval_ref, dglu_ref, hval_ref, hglu_ref, cw_ref, cb_ref, xw, xr, dcs)
        dhat, _ = _ln_stats(dcs[...])
        dn = dhat * cng_ref[...] + cnb_ref[...]
        dgate = dgate_ref[...]
        out_ref[:, HALF:2 * HALF] = ((dn * _sigmoid(dn)) * (dgate * _sigmoid(dgate))).astype(BF16)

    vec = _full_spec((1, HALF))
    return pl.pallas_call(
        body, grid=(S // TR,),
        in_specs=_odd_in_specs() + [vec, vec, _full_spec((4, 128, 128)), _full_spec((4, 128, 128)),
                                    _full_spec((HALO, HALF)), vec, vec, vec],
        out_specs=[pl.BlockSpec((TR, 2048), lambda i: (i, 0)), pl.BlockSpec((TR, HALF), lambda i: (i, 0))],
        out_shape=[jax.ShapeDtypeStruct((S, 2048), BF16), jax.ShapeDtypeStruct((S, HALF), F32)],
        scratch_shapes=[pltpu.VMEM((TR, HALF), BF16), pltpu.VMEM((TR, HALF), F32),
                        pltpu.VMEM((WIN, HALF), F32), pltpu.VMEM((SUBL, WIN, HALF), F32)],
        compiler_params=_params(("parallel",), VMEM_BIG), name="odd_fwd",
    )(z1, z1, z1, z1, z1, z1, z1, z1, sgu_g, sgu_b, sgu_w, sgu_bb, conv_w, conv_b, cn_g, cn_b)


def _odd_bwd_a(z1, dc, dycat, sgu_g, sgu_b, sgu_w, sgu_bb, cn_g, cn_b):
    def body(u_ref, v_ref, cg_ref, dgate_ref, dcs, dy_ref, g_ref, b_ref, w_ref, bb_ref, cng_ref, cnb_ref,
             dz_ref, ddc_ref, dw_ref, dbb_ref, dg_ref, db_ref, dcng_ref, dcnb_ref, dcb_ref,
             vn_s, s_s, ds_s, dvn_s):
        i = pl.program_id(0)
        vhat, rv = _ln_stats(v_ref[...])
        g = g_ref[...]
        vn_s[...] = (vhat * g + b_ref[...]).astype(BF16)
        _sgu_gate(vn_s, s_s, w_ref, bb_ref)
        silu_c, dsilu_c = _silu_and_grad(cg_ref[...])
        dyc = dy_ref[:, 0:HALF]
        u = u_ref[...]
        s = s_s[...]
        dz_ref[:, 0:HALF] = (dyc * s * silu_c).astype(BF16)
        dz_ref[:, 2 * HALF:3 * HALF] = (dyc * u * s * dsilu_c).astype(BF16)
        ds_s[...] = dyc * u * silu_c

        @pl.when(i == 0)
        def _():
            dw_ref[...] = jnp.zeros_like(dw_ref)
            dbb_ref[...] = jnp.zeros_like(dbb_ref)

        tril = lax.broadcasted_iota(jnp.int32, (128, 128), 0) >= lax.broadcasted_iota(jnp.int32, (128, 128), 1)
        for h in range(4):
            wm = _tril_bf16(w_ref[h])
            for ch in range(NCHUNK):
                rows, cols = slice(ch * 128, (ch + 1) * 128), slice(h * SGU_CH, (h + 1) * SGU_CH)
                ds = ds_s[rows, cols]
                dsb = ds.astype(BF16)
                dw_ref[h] += jnp.where(tril, _dot(dsb, vn_s[rows, cols], NT), 0.0)
                dbb_ref[h] += jnp.broadcast_to(jnp.sum(ds, axis=1, keepdims=True), (128, 128))
                dvn_s[rows, cols] = _dot(wm, dsb, TN)
        dvn = dvn_s[...]
        _acc_rows(dg_ref, dvn * vhat, i)
        _acc_rows(db_ref, dvn, i)
        dz_ref[:, HALF:2 * HALF] = _ln_bwd(dvn, vhat, rv, g).astype(BF16)

        dhat, rd = _ln_stats(dcs[...])
        cng = cng_ref[...]
        silu_n, dsilu_n = _silu_and_grad(dhat * cng + cnb_ref[...])
        silu_g, dsilu_g = _silu_and_grad(dgate_ref[...])
        dyd = dy_ref[:, HALF:2 * HALF]
        dz_ref[:, 5 * HALF:6 * HALF] = (dyd * silu_n * dsilu_g).astype(BF16)
        ddn = dyd * silu_g * dsilu_n
        _acc_rows(dcng_ref, ddn * dhat, i)
        _acc_rows(dcnb_ref, ddn, i)
        ddc = _ln_bwd(ddn, dhat, rd, cng)
        ddc_ref[...] = ddc
        _acc_rows(dcb_ref, ddc, i)

    vec = _full_spec((1, HALF))
    sq = _full_spec((4, 128, 128))
    col = lambda j: pl.BlockSpec((TR, HALF), lambda i: (i, j))
    return pl.pallas_call(
        body, grid=(S // TR,),
        in_specs=[col(0), col(1), col(2), col(5), col(0), pl.BlockSpec((TR, 2048), lambda i: (i, 0)),
                  vec, vec, sq, sq, vec, vec],
        out_specs=[pl.BlockSpec((TR, ODD_IN), lambda i: (i, 0)), pl.BlockSpec((TR, HALF), lambda i: (i, 0)),
                   sq, sq, vec, vec, vec, vec, vec],
        out_shape=[jax.ShapeDtypeStruct((S, ODD_IN), BF16), jax.ShapeDtypeStruct((S, HALF), F32),
                   jax.ShapeDtypeStruct((4, 128, 128), F32), jax.ShapeDtypeStruct((4, 128, 128), F32)]
                  + [jax.ShapeDtypeStruct((1, HALF), F32)] * 5,
        scratch_shapes=[pltpu.VMEM((TR, HALF), BF16), pltpu.VMEM((TR, HALF), F32),
                        pltpu.VMEM((TR, HALF), F32), pltpu.VMEM((TR, HALF), F32)],
        compiler_params=_params(("arbitrary",), VMEM_BIG), name="odd_bwd_a",
    )(z1, z1, z1, z1, dc, dycat, sgu_g, sgu_b, sgu_w, sgu_bb, cn_g, cn_b)


def _odd_bwd_b(z1, ddc, dz1, conv_w):
    nt = S // TR

    def body(dval_ref, dglu_ref, hval_ref, hglu_ref, ddc_ref, hddc_ref, cw_ref, dz_in_ref,
             dz_ref, dcw_ref, xw, dwin, dxs, xr, dr):
        del dz_in_ref
        i, j = pl.program_id(0), pl.program_id(1)
        sg = _sigmoid(dglu_ref[...])
        dval = dval_ref[...]

        @pl.when(j == 0)
        def _():
            halo = hval_ref[...] * _sigmoid(hglu_ref[...])
            xw[0:HALO, :] = jnp.where(i > 0, halo, 0.0)
            xw[HALO:HALO + TR, :] = dval * sg
            dwin[0:TR, :] = ddc_ref[...]
            dwin[TR:TR + HALO, :] = jnp.where(i < nt - 1, hddc_ref[...], 0.0)
            _shifted_copies(xr, xw)
            _shifted_copies(dr, dwin)

            @pl.when(i == 0)
            def _():
                dcw_ref[...] = jnp.zeros_like(dcw_ref)

            for rb in range(TR // SUB):
                acc = jnp.zeros((SUB, HALF), F32)
                for k in range(CONV_K):
                    acc = acc + cw_ref[k:k + 1, :] * _rows_at(dr, rb * SUB + (CONV_K - 1) - k, SUB)
                dxs[rb * SUB:(rb + 1) * SUB, :] = acc
            for k in range(CONV_K):
                acc = jnp.zeros((SUB, HALF), F32)
                for rb in range(TR // SUB):
                    acc = acc + dwin[rb * SUB:(rb + 1) * SUB, :] * _rows_at(xr, rb * SUB + HALO - (CONV_K - 1) + k, SUB)
                dcw_ref[k:k + 1, :] += jnp.sum(acc, axis=0, keepdims=True)
            dz_ref[...] = (dxs[...] * sg).astype(BF16)

        @pl.when(j == 1)
        def _():
            dz_ref[...] = (dxs[...] * dval * sg * (1.0 - sg)).astype(BF16)

    col = lambda c: pl.BlockSpec((TR, HALF), lambda i, j: (i, c))
    prev = lambda c: pl.BlockSpec((HALO, HALF), lambda i, j: (jnp.maximum(i * (TR // HALO) - 1, 0), c))
    nxt = pl.BlockSpec((HALO, HALF), lambda i, j: (jnp.minimum((i + 1) * (TR // HALO), S // HALO - 1), 0))
    return pl.pallas_call(
        body, grid=(nt, 2),
        in_specs=[col(3), col(4), prev(3), prev(4), pl.BlockSpec((TR, HALF), lambda i, j: (i, 0)), nxt,
                  _full_spec((HALO, HALF)), pl.BlockSpec(memory_space=pl.ANY)],
        out_specs=[pl.BlockSpec((TR, HALF), lambda i, j: (i, 3 + j)), _full_spec((HALO, HALF))],
        out_shape=[jax.ShapeDtypeStruct((S, ODD_IN), BF16), jax.ShapeDtypeStruct((HALO, HALF), F32)],
        scratch_shapes=[pltpu.VMEM((WIN, HALF), F32), pltpu.VMEM((WIN, HALF), F32), pltpu.VMEM((TR, HALF), F32),
                        pltpu.VMEM((SUBL, WIN, HALF), F32), pltpu.VMEM((SUBL, WIN, HALF), F32)],
        input_output_aliases={7: 0},
        compiler_params=_params(("arbitrary", "arbitrary"), VMEM_BIG), name="odd_bwd_b",
    )(z1, z1, z1, z1, ddc, ddc, conv_w, dz1)


def _cast_bf16(w, name, piece=0, npieces=1):
    r, c = w.shape[0], w.shape[1] // npieces
    tr = min(r, 256)

    def body(i_ref, o_ref):
        o_ref[...] = i_ref[...].astype(BF16)

    return pl.pallas_call(
        body, grid=(r // tr,), in_specs=[pl.BlockSpec((tr, c), lambda i: (i, piece))],
        out_specs=pl.BlockSpec((tr, c), lambda i: (i, 0)), out_shape=jax.ShapeDtypeStruct((r, c), BF16),
        compiler_params=_params(("parallel",)), name=name,
    )(w)


def _adamw(w, g, m, v):
    m = ADAM_B1 * m + (1.0 - ADAM_B1) * g
    v = ADAM_B2 * v + (1.0 - ADAM_B2) * (g * g)
    m_hat = m / (1.0 - ADAM_B1 ** ADAM_STEP)
    v_hat = v / (1.0 - ADAM_B2 ** ADAM_STEP)
    delta = -ADAM_LR * (m_hat / (jnp.sqrt(v_hat) + ADAM_EPS) + ADAM_WD * w)
    return delta, m, v


def _adam_reduce(parts, w, m, v, name, dep=None, piece=0, npieces=1, prev=None):
    r, c = w.shape
    cp = c // npieces
    tr = min(r, 128)
    extra = ([] if dep is None else [dep]) + ([] if prev is None else list(prev))
    nparts = parts.shape[0]

    def body(p_ref, w_ref, m_ref, v_ref, *rest):
        g_ref, d_ref, nm_ref, nv_ref = rest[len(extra):]
        g = p_ref[0].astype(F32)
        for d in range(1, nparts):
            g = g + p_ref[d].astype(F32)
        g_ref[...] = g
        d_ref[...], nm_ref[...], nv_ref[...] = _adamw(w_ref[...], g, m_ref[...], v_ref[...])

    spec = pl.BlockSpec((tr, cp), lambda i: (i, piece))
    first = 4 + (0 if dep is None else 1)
    return pl.pallas_call(
        body, grid=(r // tr,),
        in_specs=[pl.BlockSpec((nparts, tr, cp), lambda i: (0, i, 0)), spec, spec, spec] + [ANY_SPEC] * len(extra),
        out_specs=[spec] * 4, out_shape=[jax.ShapeDtypeStruct((r, c), F32)] * 4,
        input_output_aliases={} if prev is None else {first + k: k for k in range(4)},
        compiler_params=_params(("parallel",), VMEM_BIG), name=name,
    )(parts, w, m, v, *extra)


def _arrived(x, name, dep=None):
    deps = [] if dep is None else [dep]

    def body(*refs):
        refs[-1][...] = jnp.zeros_like(refs[-1])

    return pl.pallas_call(
        body, in_specs=[ANY_SPEC] * (1 + len(deps)), out_specs=pl.BlockSpec(memory_space=pltpu.VMEM),
        out_shape=jax.ShapeDtypeStruct((8, 128), F32), name=name,
    )(x, *deps)


def _sum_parts(parts, name, dep=None):
    r = parts.shape[1]
    tr = 8
    for cand in (512, 256, 128, 64, 32, 16, 8):
        if r % cand == 0:
            tr = cand
            break
    deps = [] if dep is None else [dep]

    def body(p_ref, *rest):
        g = p_ref[0]
        for d in range(1, NDEV):
            g = g + p_ref[d]
        rest[-1][...] = g

    return pl.pallas_call(
        body, grid=(r // tr,), in_specs=[pl.BlockSpec((NDEV, tr, 128), lambda i: (0, i, 0))] + [ANY_SPEC] * len(deps),
        out_specs=pl.BlockSpec((tr, 128), lambda i: (i, 0)), out_shape=jax.ShapeDtypeStruct((r, 128), F32),
        compiler_params=_params(("parallel",)), name=name,
    )(parts, *deps)


def _sum_unpack(parts, rows, name, dep=None):
    deps = [] if dep is None else [dep]

    def body(p_ref, *outs):
        outs = outs[len(deps):]
        off = 0
        for o_ref, n in zip(outs, rows):
            acc = p_ref[0, off:off + n, :]
            for d in range(1, NDEV):
                acc = acc + p_ref[d, off:off + n, :]
            o_ref[...] = acc
            off += n

    return pl.pallas_call(
        body, grid=(1,), in_specs=[pl.BlockSpec(parts.shape, lambda i: (0, 0, 0))] + [ANY_SPEC] * len(deps),
        out_specs=[pl.BlockSpec((n, 128), lambda i: (0, 0)) for n in rows],
        out_shape=[jax.ShapeDtypeStruct((n, 128), F32) for n in rows],
        compiler_params=_params(("arbitrary",), VMEM_BIG), name=name,
    )(parts, *deps)


def _adam_small(ws, gs, g_specs, ms, vs, name):
    n = len(ws)

    def body(*refs):
        w_r, g_r, m_r, v_r = refs[:n], refs[n:2 * n], refs[2 * n:3 * n], refs[3 * n:4 * n]
        outs = refs[4 * n:]
        for i in range(n):
            g = g_r[i][...]
            outs[4 * i][...] = g
            outs[4 * i + 1][...], outs[4 * i + 2][...], outs[4 * i + 3][...] = _adamw(
                w_r[i][...], g, m_r[i][...], v_r[i][...])

    whole = lambda a: pl.BlockSpec(a.shape, lambda i, nd=a.ndim: (0,) * nd)
    outs = pl.pallas_call(
        body, grid=(1,),
        in_specs=[whole(a) for a in ws] + list(g_specs) + [whole(a) for a in ms] + [whole(a) for a in vs],
        out_specs=[whole(a) for a in ws for _ in range(4)],
        out_shape=[jax.ShapeDtypeStruct(a.shape, F32) for a in ws for _ in range(4)],
        compiler_params=_params(("arbitrary",), VMEM_BIG), name=name,
    )(*ws, *gs, *ms, *vs)
    return [outs[4 * i:4 * i + 4] for i in range(n)]


MASKS = [(mx, my, mc) for mx in (0, 1) for my in (0, 1) for mc in (0, 1)][1:]


def _sc_exchange(name, collective_id, arrays, scatter):
    nt = len(arrays)
    out_type = [jax.ShapeDtypeStruct(a.shape if scatter else (NDEV,) + a.shape, a.dtype) for a in arrays]

    def body(*refs):
        ins, outs = refs[:nt], refs[nt:2 * nt]
        send_sems, recv_sems, local_sems = refs[2 * nt:3 * nt], refs[3 * nt:4 * nt], refs[4 * nt:5 * nt]
        x, y, c = lax.axis_index("x"), lax.axis_index("y"), lax.axis_index("c")
        peers = [(mx + x - 2 * mx * x, my + y - 2 * my * y, mc + c - 2 * mc * c) for mx, my, mc in MASKS]
        barrier = pltpu.get_barrier_semaphore()
        for peer in peers:
            pl.semaphore_signal(barrier, inc=1, device_id=peer, device_id_type=MESH)
        pl.semaphore_wait(barrier, len(peers))
        me = 4 * x + 2 * y + c
        own = []
        for t in range(nt):
            cp = pltpu.make_async_copy(ins[t].at[me] if scatter else ins[t], outs[t].at[me], local_sems[t])
            cp.start()
            own.append(cp)
            for px, py, pc in peers:
                src = ins[t].at[4 * px + 2 * py + pc] if scatter else ins[t]
                pltpu.make_async_remote_copy(src_ref=src, dst_ref=outs[t].at[me], send_sem=send_sems[t],
                                             recv_sem=recv_sems[t], device_id=(px, py, pc), device_id_type=MESH).start()
        for t in range(nt):
            own[t].wait()
            seven = outs[t].at[pl.ds(0, NDEV - 1)]
            drain = pltpu.make_async_remote_copy(src_ref=seven, dst_ref=seven, send_sem=send_sems[t],
                                                 recv_sem=recv_sems[t], device_id=(x, y, c), device_id_type=MESH)
            drain.wait_send()
            drain.wait_recv()

    return pl.kernel(
        body, out_type=out_type, mesh=plsc.ScalarSubcoreMesh(axis_name="sequencer", num_cores=1),
        scratch_types=[pltpu.SemaphoreType.DMA] * (3 * nt),
        compiler_params=pltpu.CompilerParams(collective_id=collective_id), name=name,
    )(*arrays)


def _sc_gather_two_level(name, collective_id, arrays):
    nt = len(arrays)
    out_type = [jax.ShapeDtypeStruct((NDEV,) + a.shape, a.dtype) for a in arrays]

    def body(*refs):
        ins, outs = refs[:nt], refs[nt:2 * nt]
        sems = refs[2 * nt:]
        send_sems, sib_sems, local_sems = sems[:nt], sems[nt:2 * nt], sems[2 * nt:3 * nt]
        ici_sems = [sems[3 * nt + 3 * t:3 * nt + 3 * t + 3] for t in range(nt)]
        x, y, c = lax.axis_index("x"), lax.axis_index("y"), lax.axis_index("c")
        sibling = (x, y, 1 - c)
        chips = [(1 - x, y), (x, 1 - y), (1 - x, 1 - y)]
        barrier = pltpu.get_barrier_semaphore()
        for peer in [sibling] + [(cx, cy, c) for cx, cy in chips]:
            pl.semaphore_signal(barrier, inc=1, device_id=peer, device_id_type=MESH)
        pl.semaphore_wait(barrier, 4)
        me = 4 * x + 2 * y + c

        def push(t, src, slot, recv_sem, to):
            pltpu.make_async_remote_copy(src_ref=src, dst_ref=outs[t].at[slot], send_sem=send_sems[t],
                                         recv_sem=recv_sem, device_id=to, device_id_type=MESH).start()

        own = []
        for t in range(nt):
            cp = pltpu.make_async_copy(ins[t], outs[t].at[me], local_sems[t])
            cp.start()
            own.append(cp)
            for j, (cx, cy) in enumerate(chips):
                push(t, ins[t], me, ici_sems[t][j], (cx, cy, c))
            push(t, ins[t], me, sib_sems[t], sibling)
        for t in range(nt):
            for j, (cx, cy) in enumerate(chips):
                slot = 4 * cx + 2 * cy + c
                landed = outs[t].at[slot]
                pltpu.make_async_remote_copy(src_ref=landed, dst_ref=landed, send_sem=send_sems[t],
                                             recv_sem=ici_sems[t][j], device_id=(cx, cy, c),
                                             device_id_type=MESH).wait_recv()
                push(t, landed, slot, sib_sems[t], sibling)
        for t in range(nt):
            own[t].wait()
            four, seven = outs[t].at[pl.ds(0, 4)], outs[t].at[pl.ds(0, 7)]
            pltpu.make_async_remote_copy(src_ref=four, dst_ref=four, send_sem=send_sems[t], recv_sem=sib_sems[t],
                                         device_id=sibling, device_id_type=MESH).wait_recv()
            pltpu.make_async_remote_copy(src_ref=seven, dst_ref=seven, send_sem=send_sems[t], recv_sem=sib_sems[t],
                                         device_id=sibling, device_id_type=MESH).wait_send()

    return pl.kernel(
        body, out_type=out_type, mesh=plsc.ScalarSubcoreMesh(axis_name="sequencer", num_cores=1),
        scratch_types=[pltpu.SemaphoreType.DMA] * (6 * nt),
        compiler_params=pltpu.CompilerParams(collective_id=collective_id), name=name,
    )(*arrays)


def _sc_sibling_exchange(name, collective_id, src, out_shape, pieces, after=None):
    extra = [] if after is None else [after]

    def body(src_ref, *rest):
        out_ref, send_sem, recv_sem = rest[len(extra):]
        x, y, c = lax.axis_index("x"), lax.axis_index("y"), lax.axis_index("c")
        sibling = (x, y, 1 - c)
        barrier = pltpu.get_barrier_semaphore()
        pl.semaphore_signal(barrier, inc=1, device_id=sibling, device_id_type=MESH)
        pl.semaphore_wait(barrier, 1)
        for piece, lands in pieces(c, src_ref, out_ref):
            pltpu.make_async_remote_copy(src_ref=piece, dst_ref=lands, send_sem=send_sem, recv_sem=recv_sem,
                                         device_id=sibling, device_id_type=MESH).start()
        drain = pltpu.make_async_remote_copy(src_ref=out_ref, dst_ref=out_ref, send_sem=send_sem, recv_sem=recv_sem,
                                             device_id=sibling, device_id_type=MESH)
        drain.wait_send()
        drain.wait_recv()

    return pl.kernel(
        body, out_type=jax.ShapeDtypeStruct(out_shape, src.dtype),
        mesh=plsc.ScalarSubcoreMesh(axis_name="sequencer", num_cores=1), scratch_types=[pltpu.SemaphoreType.DMA] * 2,
        compiler_params=pltpu.CompilerParams(collective_id=collective_id), name=name,
    )(src, *extra)


def _swap_class_columns(name, collective_id, dz, nb, piece=0, npieces=1):
    w = nb // npieces
    return _sc_sibling_exchange(
        name, collective_id, dz, (S, 4 * w),
        lambda c, src, out: [(src.at[:, pl.ds((2 * j + 1 - c) * nb + piece * w, w)], out.at[:, pl.ds(j * w, w)])
                             for j in range(4)])


def _sc_chip_scatter(name, collective_id, q):
    def body(q_ref, out_ref, send_sem, recv_sem, local_sem):
        x, y, c = lax.axis_index("x"), lax.axis_index("y"), lax.axis_index("c")
        chips = [(1 - x, y), (x, 1 - y), (1 - x, 1 - y)]
        barrier = pltpu.get_barrier_semaphore()
        for cx, cy in chips:
            pl.semaphore_signal(barrier, inc=1, device_id=(cx, cy, c), device_id_type=MESH)
        pl.semaphore_wait(barrier, 3)
        mine = 2 * x + y
        own = pltpu.make_async_copy(q_ref.at[mine], out_ref.at[mine], local_sem)
        own.start()
        for cx, cy in chips:
            pltpu.make_async_remote_copy(src_ref=q_ref.at[2 * cx + cy], dst_ref=out_ref.at[mine], send_sem=send_sem,
                                         recv_sem=recv_sem, device_id=(cx, cy, c), device_id_type=MESH).start()
        own.wait()
        three = out_ref.at[pl.ds(0, 3)]
        drain = pltpu.make_async_remote_copy(src_ref=three, dst_ref=three, send_sem=send_sem, recv_sem=recv_sem,
                                             device_id=(x, y, c), device_id_type=MESH)
        drain.wait_send()
        drain.wait_recv()

    return pl.kernel(
        body, out_type=jax.ShapeDtypeStruct(q.shape, q.dtype),
        mesh=plsc.ScalarSubcoreMesh(axis_name="sequencer", num_cores=1), scratch_types=[pltpu.SemaphoreType.DMA] * 3,
        compiler_params=pltpu.CompilerParams(collective_id=collective_id), name=name,
    )(q)


def _mm_pair_dw(h_own, dz, h_sib, dz_sib, nb, name, dep=None, piece=0, npieces=1, h_transposed=False,
                one_call=False):
    nb = nb // npieces
    tn = 512 if nb % 512 == 0 else nb
    per = nb // tn
    dn = NN if h_transposed else TN
    o_spec = pl.BlockSpec((None, D, tn), lambda i, j, k: (j // per, 0, j % per))
    own_col = lambda i, j, k: (0, ((2 * (j // per) + lax.axis_index("c")) * npieces + piece) * per + j % per)
    if one_call:
        def fused(a0_ref, b0_ref, a1_ref, b1_ref, dep_ref, o_ref):
            acc = _dot(a0_ref[...], b0_ref[...], dn) + _dot(a1_ref[...], b1_ref[...], dn)
            o_ref[...] = acc.astype(BF16)

        whole = pl.BlockSpec((S, D), lambda i, j, k: (0, 0), pipeline_mode=pl.Buffered(1))
        return pl.pallas_call(
            fused, grid=(1, 4 * per, 1),
            in_specs=[whole, pl.BlockSpec((S, tn), own_col), whole, pl.BlockSpec((S, tn), lambda i, j, k: (0, j)),
                      ANY_SPEC],
            out_specs=o_spec, out_shape=jax.ShapeDtypeStruct((4, D, nb), BF16),
            compiler_params=_params(("parallel", "parallel", "arbitrary"), VMEM_BIG), name=name,
        )(h_own, dz, h_sib, dz_sib, dep)
    part = _matmul(
        h_own, dz, dn=dn, grid=(1, 4 * per, 1),
        a_spec=pl.BlockSpec((S, D), lambda i, j, k: (0, 0)), b_spec=pl.BlockSpec((S, tn), own_col),
        o_spec=o_spec, out_shape=(4, D, nb), out_dtype=F32, acc_shape=(D, tn), name=name + "_own", dep=dep)

    def body(a_ref, b_ref, p_ref, o_ref):
        o_ref[...] = (p_ref[...] + _dot(a_ref[...], b_ref[...], dn)).astype(BF16)

    return pl.pallas_call(
        body, grid=(1, 4 * per, 1),
        in_specs=[pl.BlockSpec((S, D), lambda i, j, k: (0, 0)), pl.BlockSpec((S, tn), lambda i, j, k: (0, j)), o_spec],
        out_specs=o_spec, out_shape=jax.ShapeDtypeStruct((4, D, nb), BF16),
        compiler_params=_params(("parallel", "parallel", "arbitrary"), VMEM_BIG), name=name + "_sibling",
    )(h_sib, dz_sib, part)


SMALL = {
    "e_pre_norm": ((2048,), None), "e_pool_w": ((4, 256, 256), 1), "e_pool_scale": ((1024,), None),
    "e_post_norm": ((2048,), None), "o_pre_norm": ((2048,), 0), "o_sgu_norm_g": ((1024,), 0),
    "o_sgu_norm_b": ((1024,), 0), "o_sgu_w": ((4, 128, 128), None), "o_sgu_b": ((4, 128), None),
    "o_conv_w": ((31, 1024), 1), "o_conv_b": ((1024,), 0), "o_conv_norm_g": ((1024,), 0),
    "o_conv_norm_b": ((1024,), 0), "o_post_norm": ((2048,), 0),
}
SMALL_SHARDED = [n for n, (_, ax) in SMALL.items() if ax is not None]


def _shard_shape(name):
    shape, ax = SMALL[name]
    if ax is None:
        return shape
    return tuple(s // NDEV if i == ax else s for i, s in enumerate(shape))


def _pack(arrs, row_multiple=1):
    flat = jnp.concatenate([a.reshape(-1) for a in arrs])
    pad = -flat.shape[0] % (128 * row_multiple)
    return jnp.concatenate([flat, jnp.zeros((pad,), F32)]).reshape(-1, 128)


def _small_views(name):
    shape, ax = SMALL[name]
    me = lambda: 4 * lax.axis_index("x") + 2 * lax.axis_index("y") + lax.axis_index("c")
    if ax is None:
        view = (int(np.prod(shape)) // 128, 128)
        return view, view, pl.BlockSpec(view, lambda i: (0, 0))
    if len(shape) == 1:
        n = shape[0] // NDEV
        return (1, n), (NDEV, 1, n), pl.BlockSpec((None, 1, n), lambda i: (me(), 0, 0))
    part = _shard_shape(name)
    return part, shape, pl.BlockSpec(part, lambda i: tuple(me() if d == ax else 0 for d in range(len(shape))))


WEIGHTS = ["e_pre_norm", "e_w_in", "e_pool_w", "e_pool_scale", "e_w_out", "e_post_norm", "o_pre_norm", "o_w_in",
           "o_sgu_norm_g", "o_sgu_norm_b", "o_sgu_w", "o_sgu_b", "o_conv_w", "o_conv_b", "o_conv_norm_g",
           "o_conv_norm_b", "o_w_out", "o_post_norm"]


def kernel(x, e_pre_norm, e_w_in, e_pool_w, e_pool_scale, e_w_out, e_post_norm, o_pre_norm, o_w_in, o_sgu_norm_g, o_sgu_norm_b, o_sgu_w, o_sgu_b, o_conv_w, o_conv_b, o_conv_norm_g, o_conv_norm_b, o_w_out, o_post_norm, loss_target, m_e_pre_norm, m_e_w_in, m_e_pool_w, m_e_pool_scale, m_e_w_out, m_e_post_norm, m_o_pre_norm, m_o_w_in, m_o_sgu_norm_g, m_o_sgu_norm_b, m_o_sgu_w, m_o_sgu_b, m_o_conv_w, m_o_conv_b, m_o_conv_norm_g, m_o_conv_norm_b, m_o_w_out, m_o_post_norm, v_e_pre_norm, v_e_w_in, v_e_pool_w, v_e_pool_scale, v_e_w_out, v_e_post_norm, v_o_pre_norm, v_o_w_in, v_o_sgu_norm_g, v_o_sgu_norm_b, v_o_sgu_w, v_o_sgu_b, v_o_conv_w, v_o_conv_b, v_o_conv_norm_g, v_o_conv_norm_b, v_o_w_out, v_o_post_norm):
    given = dict(locals())
    w = {n: given[n][0] for n in WEIGHTS}
    m = {n: given["m_" + n][0] for n in WEIGHTS}
    v = {n: given["v_" + n][0] for n in WEIGHTS}
    me = 4 * lax.axis_index("x") + 2 * lax.axis_index("y") + lax.axis_index("c")
    x, target = x[0], loss_target[0]
    row = lambda a: a.reshape(1, -1)

    lo, small_rows = _sc_gather_two_level(
        "gather_a0", 0, [_cast_bf16(w["e_w_in"], "cast_e_w_in_0", 0, 2), _pack([w[n] for n in SMALL_SHARDED])])
    hi, = _sc_gather_two_level("gather_a1", 12, [_cast_bf16(w["e_w_in"], "cast_e_w_in_1", 1, 2)])
    wg_e_in = (lo, hi)
    h0, h0t = _pre0_fwd(x, row(w["e_pre_norm"]))
    wg_e_out, = _sc_gather_two_level("gather_b", 1, [_cast_bf16(w["e_w_out"], "cast_e_w_out")])
    wg_o_in, = _sc_gather_two_level("gather_c", 13, [_cast_bf16(w["o_w_in"], "cast_o_w_in")])
    wg_o_out, = _sc_gather_two_level("gather_d", 16, [_cast_bf16(w["o_w_out"], "cast_o_w_out")])
    p = {n: w[n] for n in SMALL if SMALL[n][1] is None}
    small_rows = small_rows.reshape(NDEV, -1)
    off = 0
    for n in SMALL_SHARDED:
        shp, ax = _shard_shape(n), SMALL[n][1]
        cnt = int(np.prod(shp))
        blk = small_rows[:, off:off + cnt].reshape((NDEV,) + shp)
        p[n] = jnp.moveaxis(blk, 0, ax).reshape(SMALL[n][0])
        off += cnt
    tabs = _rope_tables()
    pool_w_bf = p["e_pool_w"].astype(BF16)
    sgu_bb = jnp.broadcast_to(p["o_sgu_b"][:, :, None], (4, 128, 128))
    conv_w = jnp.concatenate([p["o_conv_w"], jnp.zeros((HALO - CONV_K, HALF), F32)], axis=0)
    odd_p = (row(p["o_sgu_norm_g"]), row(p["o_sgu_norm_b"]), p["o_sgu_w"], sgu_bb, conv_w,
             row(p["o_conv_b"]), row(p["o_conv_norm_g"]), row(p["o_conv_norm_b"]))

    z0 = _mm_in_halves(h0, wg_e_in, "mm_z0")
    ycat0 = _pool_fwd(z0, pool_w_bf, row(p["e_pool_scale"]))
    qkv = _qkv_prep(z0, tabs)
    ycat0, og, lg = _attn_fwd(z0, qkv, ycat0)
    w_out_e, w_out_o = wg_e_out.reshape(2048, D), wg_o_out.reshape(2048, D)
    y0, x1, h1 = _post0_fwd(ycat0, w_out_e, x, row(p["e_post_norm"]), row(p["o_pre_norm"]), wg_e_out)
    h0t_sib = _sc_sibling_exchange("swap_h0", 8, h0t, h0t.shape, lambda c, src, out: [(src, out)], h1)
    h1_sib = _sc_sibling_exchange("swap_h1", 11, h1, h1.shape, lambda c, src, out: [(src, out)])
    z1 = _mm_in(h1, wg_o_in, "mm_z1")
    ycat1, conv_out = _odd_fwd(z1, *odd_p)

    g = {}
    loss_part, dx2, dy1, g["o_post_norm"] = _post1_bwd(ycat1, w_out_o, x1, target, row(p["o_post_norm"]),
                                                       _arrived(h1_sib, "arrived_h_sib", h0t_sib))
    parts = {}
    dw = _mm_out_dw(ycat1, dy1, "mm_dwout1").reshape(NDEV, 256, D)
    parts["o_w_out"], = _sc_exchange("scatter_o_w_out", 2, [dw], True)
    dycat1 = _mm_out_dx(dy1, w_out_o, "mm_dycat1", dw)
    dz1, ddc, g["o_sgu_w"], d_sgu_bb, g["o_sgu_norm_g"], g["o_sgu_norm_b"], g["o_conv_norm_g"], \
        g["o_conv_norm_b"], g["o_conv_b"] = _odd_bwd_a(z1, conv_out, dycat1, *odd_p[:4], *odd_p[6:])
    dz1, d_conv_w = _odd_bwd_b(z1, ddc, dz1, conv_w)
    g["o_sgu_b"] = d_sgu_bb[:, :, 0]
    g["o_conv_w"] = d_conv_w[:CONV_K]
    grads, deltas, new_m, new_v = {}, {}, {}, {}

    def adam(n, dep):
        grads[n], deltas[n], new_m[n], new_v[n] = _adam_reduce(parts[n], w[n], m[n], v[n], "adam_" + n, dep)
        return new_v[n]

    pin = _arrived(parts["o_w_out"], "arrived_o_w_out", d_conv_w)
    dz1_sib = _swap_class_columns("swap_dz1", 10, dz1, ODD_IN // NDEV)
    dw = _mm_pair_dw(h1, dz1, h1_sib, dz1_sib, ODD_IN // NDEV, "mm_dwin1", pin)
    parts["o_w_in"] = _sc_chip_scatter("scatter_o_w_in", 3, dw)
    dh1 = _mm_in_dx(dz1, wg_o_in, "mm_dh1", dw)
    dx1, dy0, g["o_pre_norm"], g["e_post_norm"] = _mid_bwd(dx2, dh1, x1, y0, row(p["o_pre_norm"]),
                                                           row(p["e_post_norm"]))
    dw = _mm_out_dw(ycat0, dy0, "mm_dwout0").reshape(NDEV, 256, D)
    parts["e_w_out"], = _sc_exchange("scatter_e_w_out", 4, [dw], True)
    dycat0 = _mm_out_dx(dy0, w_out_e, "mm_dycat0", dw)
    da_in, da_gate, g["e_pool_w"], g["e_pool_scale"] = _pool_bwd(z0, dycat0, pool_w_bf, row(p["e_pool_scale"]))
    late = [n for n in SMALL if n not in ("e_pre_norm", "o_sgu_b")] + ["o_sgu_b"]
    pieces = [g[n].reshape(SMALL[n][0]) for n in late[:-1]] + [jnp.broadcast_to(loss_part, (8, 128)), g[late[-1]]]
    recv_small, = _sc_gather_two_level("gather_small_grads", 6, [_pack(pieces, 512)])
    took = _arrived(parts["o_w_in"], "arrived_o_w_in")
    dq, dk, dv, dbg = _attn_bwd(z0, qkv, og, lg, dycat0, tabs, took)
    dz0 = jnp.concatenate([da_in, da_gate, dq, dk, dv, dbg], axis=1)
    took = _arrived(recv_small, "arrived_small_grads", _arrived(parts["e_w_out"], "arrived_e_w_out", dz0))
    nb = EVEN_IN // NDEV
    swapped = [_swap_class_columns("swap_dz0_%d" % half, (9, 14)[half], dz0, nb, half, 2) for half in (0, 1)]
    dw, e_w_in_parts = took, []
    for half in (0, 1):
        dw = _mm_pair_dw(h0t, dz0, h0t_sib, swapped[half], nb, "mm_dwin0_%d" % half, dw, half, 2, True, half == 1)
        e_w_in_parts.append(_sc_chip_scatter("scatter_e_w_in_%d" % half, (5, 15)[half], dw))
    pin = adam("e_w_out", adam("o_w_out", adam("o_w_in", dw)))
    rows = [int(np.prod(SMALL[n][0])) // 128 for n in late]
    sums = _sum_unpack(recv_small, rows[:-1] + [8, rows[-1]], "sum_small_grads", pin)
    summed = dict(zip(late, sums[:-2] + sums[-1:]))
    loss = sums[-2][0, 0]
    dh0 = _mm_in_dx_halves(dz0, wg_e_in, "mm_dh0", summed[late[0]])
    grad_x, g["e_pre_norm"] = _pre0_bwd(dx1, dh0, x, row(p["e_pre_norm"]))
    last, = _sc_exchange("gather_e_pre_norm_grad", 7, [g["e_pre_norm"].reshape(16, 128)], False)

    n = "e_w_in"
    out = _adam_reduce(e_w_in_parts[0], w[n], m[n], v[n], "adam_e_w_in_0", grad_x, 0, 2)
    out = _adam_reduce(e_w_in_parts[1], w[n], m[n], v[n], "adam_e_w_in_1", None, 1, 2, out)
    grads[n], deltas[n], new_m[n], new_v[n] = out
    summed["e_pre_norm"] = _sum_parts(last, "sum_e_pre_norm_grad", out[3])
    names = list(SMALL)
    views = [_small_views(n) for n in names]
    mine = lambda src: [src[n].reshape(vw[0]) for n, vw in zip(names, views)]
    res = _adam_small(mine(w), [summed[n].reshape(vw[1]) for n, vw in zip(names, views)], [vw[2] for vw in views],
                      mine(m), mine(v), "adam_small")
    for n, out in zip(names, res):
        grads[n], deltas[n], new_m[n], new_v[n] = [t.reshape(_shard_shape(n)) for t in out]

    lead = lambda a: a[None]
    return (loss, grad_x[None], *[lead(grads[n]) for n in WEIGHTS], *[lead(deltas[n]) for n in WEIGHTS],
            *[lead(new_m[n]) for n in WEIGHTS], *[lead(new_v[n]) for n in WEIGHTS])
```

```python
import numpy as np
import jax
import jax.numpy as jnp
from jax import lax
from jax.experimental import pallas as pl
from jax.experimental.pallas import tpu as pltpu
from jax.experimental.pallas import tpu_sc as plsc

F32 = jnp.float32
BF16 = jnp.bfloat16

S = 2048
D = 2048
NDEV = 8
EPS = 1e-6
NEG = -1e30
HEAD_DIM = 128
ROT_DIM = 32
ROPE_THETA = 500000.0
PATTERNS = ((128, 1), (512, 4), (2048, 16))
BLK = 128
EVEN_IN = 12288
ODD_IN = 6144
HALF = 1024
CONV_K = 31
HALO = 32
TR = 256
SUB = 16

ADAM_LR = 0.001
ADAM_B1 = 0.9
ADAM_B2 = 0.999
ADAM_EPS = 1e-08
ADAM_WD = 0.01
ADAM_STEP = 10

VMEM_BIG = 56 * 1024 * 1024
MESH = pl.DeviceIdType.MESH

NN = (((1,), (0,)), ((), ()))
NT = (((1,), (1,)), ((), ()))
TN = (((0,), (0,)), ((), ()))


def _dot(a, b, dn=NN):
    return lax.dot_general(a, b, dn, preferred_element_type=F32)


def _sigmoid(x):
    return 1.0 / (1.0 + jnp.exp(-x))


def _silu_and_grad(x):
    sg = _sigmoid(x)
    return x * sg, sg * (1.0 + x * (1.0 - sg))


def _params(sem, vmem=None):
    return pltpu.CompilerParams(dimension_semantics=sem, vmem_limit_bytes=vmem)


ANY_SPEC = pl.BlockSpec(memory_space=pl.ANY)


def _matmul(a, b, *, dn, grid, a_spec, b_spec, o_spec, out_shape, out_dtype, acc_shape, name, dep=None):
    nk = grid[2]
    deps = [] if dep is None else list(dep) if isinstance(dep, (tuple, list)) else [dep]

    def body(a_ref, b_ref, *rest):
        o_ref, acc = rest[len(deps)], rest[len(deps) + 1:]
        if nk == 1:
            o_ref[...] = _dot(a_ref[...], b_ref[...], dn).astype(o_ref.dtype)
            return
        acc_ref = acc[0]
        k = pl.program_id(2)

        @pl.when(k == 0)
        def _():
            acc_ref[...] = jnp.zeros_like(acc_ref)

        acc_ref[...] += _dot(a_ref[...], b_ref[...], dn)

        @pl.when(k == nk - 1)
        def _():
            o_ref[...] = acc_ref[...].astype(o_ref.dtype)

    return pl.pallas_call(
        body, grid=grid, in_specs=[a_spec, b_spec] + [ANY_SPEC] * len(deps), out_specs=o_spec,
        out_shape=jax.ShapeDtypeStruct(out_shape, out_dtype),
        scratch_shapes=[] if nk == 1 else [pltpu.VMEM(acc_shape, F32)],
        compiler_params=_params(("parallel", "parallel", "arbitrary"), VMEM_BIG), name=name,
    )(a, b, *deps)


TM = 2048


def _mm_in(h, wg, name):
    nb = wg.shape[2]
    tn = 512 if nb % 512 == 0 else nb
    per = nb // tn
    return _matmul(
        h, wg, dn=NN, grid=(S // TM, NDEV * per, 1),
        a_spec=pl.BlockSpec((TM, D), lambda i, j, k: (i, 0)),
        b_spec=pl.BlockSpec((None, D, tn), lambda i, j, k: (j // per, 0, j % per)),
        o_spec=pl.BlockSpec((TM, tn), lambda i, j, k: (i, j)),
        out_shape=(S, NDEV * nb), out_dtype=F32, acc_shape=(TM, tn), name=name)


def _mm_in_halves(h, wg_halves, name):
    hb = wg_halves[0].shape[2]
    z = None
    for half, wg in enumerate(wg_halves):
        prev = [] if z is None else [z]

        def body(a_ref, b_ref, *rest):
            rest[-1][...] = _dot(a_ref[...], b_ref[...])

        z = pl.pallas_call(
            body, grid=(NDEV,),
            in_specs=[pl.BlockSpec((S, D), lambda j: (0, 0)), pl.BlockSpec((None, D, hb), lambda j: (j, 0, 0))]
                     + [ANY_SPEC] * len(prev),
            out_specs=pl.BlockSpec((S, hb), lambda j, half=half: (0, 2 * j + half)),
            out_shape=jax.ShapeDtypeStruct((S, 2 * NDEV * hb), F32),
            input_output_aliases={2: 0} if prev else {},
            compiler_params=_params(("parallel",), VMEM_BIG), name="%s_%d" % (name, half),
        )(h, wg, *prev)
    return z


def _mm_in_dx_halves(dz, wg_halves, name, dep):
    hb = wg_halves[0].shape[2]
    nk = 2 * NDEV

    def body(a_ref, b0_ref, b1_ref, dep_ref, o_ref, acc_ref):
        k = pl.program_id(2)

        @pl.when(k == 0)
        def _():
            acc_ref[...] = jnp.zeros_like(acc_ref)

        @pl.when(k % 2 == 0)
        def _():
            acc_ref[...] += _dot(a_ref[...], b0_ref[...], NT)

        @pl.when(k % 2 == 1)
        def _():
            acc_ref[...] += _dot(a_ref[...], b1_ref[...], NT)

        @pl.when(k == nk - 1)
        def _():
            o_ref[...] = acc_ref[...]

    b_spec = pl.BlockSpec((None, 1024, hb), lambda i, j, k: (k // 2, j, 0))
    return pl.pallas_call(
        body, grid=(1, D // 1024, nk),
        in_specs=[pl.BlockSpec((S, hb), lambda i, j, k: (0, k)), b_spec, b_spec, ANY_SPEC],
        out_specs=pl.BlockSpec((S, 1024), lambda i, j, k: (0, j)), out_shape=jax.ShapeDtypeStruct((S, D), F32),
        scratch_shapes=[pltpu.VMEM((S, 1024), F32)],
        compiler_params=_params(("parallel", "parallel", "arbitrary"), VMEM_BIG), name=name,
    )(dz, *wg_halves, dep)


def _mm_in_dx(dz, wg, name, dep=None):
    nb = wg.shape[2]
    return _matmul(
        dz, wg, dn=NT, grid=(S // TM, D // 1024, NDEV),
        a_spec=pl.BlockSpec((TM, nb), lambda i, j, k: (i, k)),
        b_spec=pl.BlockSpec((None, 1024, nb), lambda i, j, k: (k, j, 0)),
        o_spec=pl.BlockSpec((TM, 1024), lambda i, j, k: (i, j)),
        out_shape=(S, D), out_dtype=F32, acc_shape=(TM, 1024), name=name, dep=dep)


def _mm_out_dx(dy, w, name, dep=None):
    return _matmul(
        dy, w, dn=NT, grid=(S // TM, 2048 // 512, 1),
        a_spec=pl.BlockSpec((TM, D), lambda i, j, k: (i, 0)),
        b_spec=pl.BlockSpec((512, D), lambda i, j, k: (j, 0)),
        o_spec=pl.BlockSpec((TM, 512), lambda i, j, k: (i, j)),
        out_shape=(S, 2048), out_dtype=F32, acc_shape=(TM, 512), name=name, dep=dep)


def _mm_out_dw(yc, dy, name):
    return _matmul(
        yc, dy, dn=TN, grid=(2048 // TM, D // 512, 1),
        a_spec=pl.BlockSpec((S, TM), lambda i, j, k: (0, i)),
        b_spec=pl.BlockSpec((S, 512), lambda i, j, k: (0, j)),
        o_spec=pl.BlockSpec((TM, 512), lambda i, j, k: (i, j)),
        out_shape=(2048, D), out_dtype=BF16, acc_shape=(TM, 512), name=name)


def _row_spec(w=D):
    return pl.BlockSpec((TR, w), lambda i: (i, 0))


def _vec_spec(w=D):
    return pl.BlockSpec((1, w), lambda i: (0, 0))


def _rms_stats(x):
    r = lax.rsqrt(jnp.mean(x * x, axis=-1, keepdims=True) + EPS)
    return x * r, r


def _rms_bwd(dn, xhat, r, g):
    dxh = dn * g
    return r * (dxh - xhat * jnp.mean(dxh * xhat, axis=-1, keepdims=True))


def _acc_rows(ref, val, i):
    s = jnp.sum(val, axis=0, keepdims=True)

    @pl.when(i == 0)
    def _():
        ref[...] = s

    @pl.when(i > 0)
    def _():
        ref[...] += s


def _pre0_fwd(x, g):
    def body(x_ref, g_ref, h_ref, ht_ref):
        xhat, _ = _rms_stats(x_ref[...])
        h = xhat * g_ref[...]
        h_ref[...] = h.astype(BF16)
        ht_ref[...] = h.T.astype(BF16)

    return pl.pallas_call(
        body, grid=(S // TR,), in_specs=[_row_spec(), _vec_spec()],
        out_specs=[_row_spec(), pl.BlockSpec((D, TR), lambda i: (0, i))],
        out_shape=[jax.ShapeDtypeStruct((S, D), BF16), jax.ShapeDtypeStruct((D, S), BF16)],
        compiler_params=_params(("parallel",)), name="pre0_fwd",
    )(x, g)


def _post0_fwd(ycat, w_out, x, g_post, g_pre1, dep):
    def body(yc_ref, w_ref, x_ref, gp_ref, g1_ref, dep_ref, y_ref, x1_ref, h1_ref):
        y = _dot(yc_ref[...], w_ref[...])
        y_ref[...] = y
        yhat, _ = _rms_stats(y)
        x1 = x_ref[...] + yhat * gp_ref[...]
        x1_ref[...] = x1
        xhat, _ = _rms_stats(x1)
        h1_ref[...] = (xhat * g1_ref[...]).astype(BF16)

    return pl.pallas_call(
        body, grid=(S // TR,),
        in_specs=[_row_spec(), pl.BlockSpec((2048, D), lambda i: (0, 0)), _row_spec(), _vec_spec(), _vec_spec(),
                  ANY_SPEC],
        out_specs=[_row_spec(), _row_spec(), _row_spec()],
        out_shape=[jax.ShapeDtypeStruct((S, D), F32), jax.ShapeDtypeStruct((S, D), F32),
                   jax.ShapeDtypeStruct((S, D), BF16)],
        compiler_params=_params(("parallel",), VMEM_BIG), name="post0_fwd",
    )(ycat, w_out, x, g_post, g_pre1, dep)


def _post1_bwd(ycat, w_out, x1, target, g_post, dep):
    def body(yc_ref, w_ref, x1_ref, t_ref, g_ref, dep_ref, loss_ref, dx2_ref, dy_ref, dg_ref):
        i = pl.program_id(0)
        yhat, r = _rms_stats(_dot(yc_ref[...], w_ref[...]))
        g = g_ref[...]
        err = x1_ref[...] + yhat * g - t_ref[...]
        part = jnp.sum(jnp.sum(err * err, axis=-1, keepdims=True), axis=0, keepdims=True) * (0.5 / D)
        _acc_rows(loss_ref, jnp.broadcast_to(part, (1, 128)), i)
        dx2 = err * (1.0 / D)
        dx2_ref[...] = dx2
        _acc_rows(dg_ref, dx2 * yhat, i)
        dy_ref[...] = _rms_bwd(dx2, yhat, r, g).astype(BF16)

    return pl.pallas_call(
        body, grid=(S // TR,),
        in_specs=[_row_spec(), pl.BlockSpec((2048, D), lambda i: (0, 0)), _row_spec(), _row_spec(), _vec_spec(),
                  ANY_SPEC],
        out_specs=[_vec_spec(128), _row_spec(), _row_spec(), _vec_spec()],
        out_shape=[jax.ShapeDtypeStruct((1, 128), F32), jax.ShapeDtypeStruct((S, D), F32),
                   jax.ShapeDtypeStruct((S, D), BF16), jax.ShapeDtypeStruct((1, D), F32)],
        compiler_params=_params(("arbitrary",), VMEM_BIG), name="post1_bwd",
    )(ycat, w_out, x1, target, g_post, dep)


def _mid_bwd(dx2, dh1, x1, y0, g_pre1, g_post0):
    def body(dx2_ref, dh_ref, x1_ref, y_ref, g1_ref, gp_ref, dx1_ref, dy_ref, dg1_ref, dgp_ref):
        i = pl.program_id(0)
        xhat, r1 = _rms_stats(x1_ref[...])
        dh = dh_ref[...]
        _acc_rows(dg1_ref, dh * xhat, i)
        dx1 = dx2_ref[...] + _rms_bwd(dh, xhat, r1, g1_ref[...])
        dx1_ref[...] = dx1
        yhat, r0 = _rms_stats(y_ref[...])
        _acc_rows(dgp_ref, dx1 * yhat, i)
        dy_ref[...] = _rms_bwd(dx1, yhat, r0, gp_ref[...]).astype(BF16)

    return pl.pallas_call(
        body, grid=(S // TR,),
        in_specs=[_row_spec(), _row_spec(), _row_spec(), _row_spec(), _vec_spec(), _vec_spec()],
        out_specs=[_row_spec(), _row_spec(), _vec_spec(), _vec_spec()],
        out_shape=[jax.ShapeDtypeStruct((S, D), F32), jax.ShapeDtypeStruct((S, D), BF16),
                   jax.ShapeDtypeStruct((1, D), F32), jax.ShapeDtypeStruct((1, D), F32)],
        compiler_params=_params(("arbitrary",)), name="mid_bwd",
    )(dx2, dh1, x1, y0, g_pre1, g_post0)


def _pre0_bwd(dx1, dh0, x, g):
    def body(dx1_ref, dh_ref, x_ref, g_ref, gx_ref, dg_ref):
        i = pl.program_id(0)
        xhat, r = _rms_stats(x_ref[...])
        dh = dh_ref[...]
        _acc_rows(dg_ref, dh * xhat, i)
        gx_ref[...] = dx1_ref[...] + _rms_bwd(dh, xhat, r, g_ref[...])

    return pl.pallas_call(
        body, grid=(S // TR,), in_specs=[_row_spec(), _row_spec(), _row_spec(), _vec_spec()],
        out_specs=[_row_spec(), _vec_spec()],
        out_shape=[jax.ShapeDtypeStruct((S, D), F32), jax.ShapeDtypeStruct((1, D), F32)],
        compiler_params=_params(("arbitrary",)), name="pre0_bwd",
    )(dx1, dh0, x, g)


POOL_CH = 256


def _pool_apply(a, w, transpose):
    n = a.shape[0]
    row = lax.broadcasted_iota(jnp.int32, a.shape, 0)
    cnt = jnp.minimum(row + 1, w).astype(F32)
    s = a / cnt if transpose else a
    for k in (1, 2, 4, 8):
        if transpose:
            sh = jnp.where(row < n - k, pltpu.roll(s, n - k, 0), 0.0)
        else:
            sh = jnp.where(row >= k, pltpu.roll(s, k, 0), 0.0)
        s = jnp.where(w > k, s + sh, s)
    return s - a if transpose else s / cnt - a


def _pool_fwd(z0, pool_w, pool_scale):
    def body(a_ref, gate_ref, w_ref, sc_ref, out_ref):
        win = jnp.left_shift(2, pl.program_id(0))
        pooled = _pool_apply(a_ref[...], win, False)
        mixed = _dot(pooled.astype(BF16), w_ref[...])
        gate = gate_ref[...]
        out_ref[...] = (mixed * sc_ref[...] * (gate * _sigmoid(gate))).astype(BF16)

    return pl.pallas_call(
        body, grid=(4,),
        in_specs=[pl.BlockSpec((S, POOL_CH), lambda g: (0, g)), pl.BlockSpec((S, POOL_CH), lambda g: (0, 4 + g)),
                  pl.BlockSpec((None, POOL_CH, POOL_CH), lambda g: (g, 0, 0)),
                  pl.BlockSpec((1, POOL_CH), lambda g: (0, g))],
        out_specs=pl.BlockSpec((S, POOL_CH), lambda g: (0, g)),
        out_shape=jax.ShapeDtypeStruct((S, 2048), BF16),
        compiler_params=_params(("parallel",), VMEM_BIG), name="pool_fwd",
    )(z0, z0, pool_w, pool_scale)


def _pool_bwd(z0, dycat, pool_w, pool_scale):
    def body(a_ref, gate_ref, dy_ref, w_ref, sc_ref, da_ref, dgate_ref, dw_ref, dsc_ref):
        win = jnp.left_shift(2, pl.program_id(0))
        pooled = _pool_apply(a_ref[...], win, False).astype(BF16)
        w = w_ref[...]
        mixed = _dot(pooled, w)
        silu, dsilu = _silu_and_grad(gate_ref[...])
        dy = dy_ref[...]
        sc = sc_ref[...]
        dgate_ref[...] = (dy * (mixed * sc) * dsilu).astype(BF16)
        dms = dy * silu
        dsc_ref[...] = jnp.sum(dms * mixed, axis=0, keepdims=True)
        dmixed = (dms * sc).astype(BF16)
        dw_ref[...] = _dot(pooled, dmixed, TN)
        dpooled = _dot(dmixed, w, NT)
        da_ref[...] = _pool_apply(dpooled, win, True).astype(BF16)

    slab = lambda off: pl.BlockSpec((S, POOL_CH), lambda g: (0, off + g))
    return pl.pallas_call(
        body, grid=(4,),
        in_specs=[slab(0), slab(4), slab(0), pl.BlockSpec((None, POOL_CH, POOL_CH), lambda g: (g, 0, 0)),
                  pl.BlockSpec((1, POOL_CH), lambda g: (0, g))],
        out_specs=[slab(0), slab(0), pl.BlockSpec((None, POOL_CH, POOL_CH), lambda g: (g, 0, 0)),
                   pl.BlockSpec((1, POOL_CH), lambda g: (0, g))],
        out_shape=[jax.ShapeDtypeStruct((S, HALF), BF16), jax.ShapeDtypeStruct((S, HALF), BF16),
                   jax.ShapeDtypeStruct((4, POOL_CH, POOL_CH), F32), jax.ShapeDtypeStruct((1, HALF), F32)],
        compiler_params=_params(("parallel",), VMEM_BIG), name="pool_bwd",
    )(z0, z0, dycat, pool_w, pool_scale)


Q_COL, K_COL, V_COL, BG_COL = 2048 // 128, 5120 // 128, 8192 // 128, 11264 // 128
SCALE = HEAD_DIM ** -0.5


def _rope_tables():
    pos = jnp.arange(S, dtype=F32)
    inv_freq = jnp.power(ROPE_THETA, -jnp.arange(0, ROT_DIM, 2, dtype=F32) / ROT_DIM)
    ang = pos[:, None] * inv_freq[None, :]
    cos, sin = jnp.cos(ang), jnp.sin(ang)
    half = ROT_DIM // 2
    zeros = jnp.zeros((S, HEAD_DIM - ROT_DIM), F32)
    c = jnp.concatenate([cos, cos, jnp.ones((S, HEAD_DIM - ROT_DIM), F32)], axis=1)
    a = jnp.concatenate([-sin, jnp.zeros((S, half), F32), zeros], axis=1)
    b = jnp.concatenate([jnp.zeros((S, half), F32), sin, zeros], axis=1)
    return c, a, b


def _rope(t, c, a, b):
    half = ROT_DIM // 2
    return t * c + pltpu.roll(t, HEAD_DIM - half, 1) * a + pltpu.roll(t, half, 1) * b


def _rope_t(d, c, a, b):
    half = ROT_DIM // 2
    return d * c + pltpu.roll(d * a, half, 1) + pltpu.roll(d * b, HEAD_DIM - half, 1)


def _deinterleave(dst, src, dil, cast=None, dst_off=0):
    length = S // dil
    for r in range(dil):
        v = src[...] if dil == 1 else src[pl.ds(r, length, stride=dil), :]
        dst[dst_off + r * length:dst_off + (r + 1) * length, :] = v if cast is None else v.astype(cast)


def _interleave(dst, src, dil, src_off=0):
    length = S // dil
    for r in range(dil):
        if dil == 1:
            dst[...] = src[src_off:src_off + S, :]
        else:
            dst[pl.ds(r, length, stride=dil), :] = src[src_off + r * length:src_off + (r + 1) * length, :]


CU = 8
NUNITS = S // BLK
B_QK = (((2,), (2,)), ((0,), (0,)))
B_PV = (((2,), (1,)), ((0,), (0,)))
B_TN = (((1,), (1,)), ((0,), (0,)))


def _blocks(ref, first):
    return ref[first * BLK:(first + CU) * BLK, :].reshape(CU, BLK, HEAD_DIM)


def _chunk_scores(u0, nb, qd, kdp):
    q = _blocks(qd, u0)
    row = lax.broadcasted_iota(jnp.int32, (CU, BLK, BLK), 1)
    col = lax.broadcasted_iota(jnp.int32, (CU, BLK, BLK), 2)
    s_own = jnp.where(col <= row, _dot(q, _blocks(kdp, u0 + 1), B_QK) * SCALE, NEG)
    if nb == 1:
        return q, s_own, None
    unit = lax.broadcasted_iota(jnp.int32, (CU, BLK, BLK), 0) + u0
    s_prev = jnp.where((col >= row) & ((unit % nb) != 0), _dot(q, _blocks(kdp, u0), B_QK) * SCALE, NEG)
    return q, s_own, s_prev


def _qkv_prep(z0, tabs):
    def body(q_ref, k_ref, v_ref, c_ref, a_ref, b_ref, qo_ref, ko_ref, vo_ref, tmp):
        p = pl.program_id(1)
        ko_ref[0:BLK, :] = jnp.zeros((BLK, HEAD_DIM), BF16)
        vo_ref[0:BLK, :] = jnp.zeros((BLK, HEAD_DIM), BF16)
        for gi, (_, dil) in enumerate(PATTERNS):
            @pl.when(p == gi)
            def _(dil=dil):
                c, a, b = c_ref[...], a_ref[...], b_ref[...]
                tmp[...] = _rope(q_ref[...], c, a, b)
                _deinterleave(qo_ref, tmp, dil, BF16)
                tmp[...] = _rope(k_ref[...], c, a, b)
                _deinterleave(ko_ref, tmp, dil, BF16, BLK)
                _deinterleave(vo_ref, v_ref, dil, BF16, BLK)

    tab = pl.BlockSpec((S, HEAD_DIM), lambda h, p: (0, 0))
    out = pl.BlockSpec((S, HEAD_DIM), lambda h, p: (0, p * 8 + h))
    outp = pl.BlockSpec((S + BLK, HEAD_DIM), lambda h, p: (0, p * 8 + h))
    return pl.pallas_call(
        body, grid=(8, 3), in_specs=[_head_spec(Q_COL), _head_spec(K_COL), _head_spec(V_COL), tab, tab, tab],
        out_specs=[out, outp, outp],
        out_shape=[jax.ShapeDtypeStruct((S, 3072), BF16)] + [jax.ShapeDtypeStruct((S + BLK, 3072), BF16)] * 2,
        scratch_shapes=[pltpu.VMEM((S, HEAD_DIM), F32)],
        compiler_params=_params(("parallel", "arbitrary"), VMEM_BIG), name="qkv_prep",
    )(z0, z0, z0, *tabs)


def _attn_group_fwd(dil, qd, kdp, vdp, od, ld, og, lg):
    nb = S // dil // BLK
    for u0 in range(0, NUNITS, CU):
        _, s_own, s_prev = _chunk_scores(u0, nb, qd, kdp)
        m = jnp.max(s_own, axis=2, keepdims=True)
        if s_prev is not None:
            m = jnp.maximum(m, jnp.max(s_prev, axis=2, keepdims=True))
        p_own = jnp.exp(s_own - m)
        den = jnp.sum(p_own, axis=2, keepdims=True)
        acc = _dot(p_own.astype(BF16), _blocks(vdp, u0 + 1), B_PV)
        if s_prev is not None:
            p_prev = jnp.exp(s_prev - m)
            den = den + jnp.sum(p_prev, axis=2, keepdims=True)
            acc = acc + _dot(p_prev.astype(BF16), _blocks(vdp, u0), B_PV)
        rows = slice(u0 * BLK, (u0 + CU) * BLK)
        od[rows, :] = (acc / den).reshape(CU * BLK, HEAD_DIM)
        ld[rows, :] = jnp.broadcast_to(m + jnp.log(den), (CU, BLK, HEAD_DIM)).reshape(CU * BLK, HEAD_DIM)
    _interleave(og, od, dil)
    _interleave(lg, ld, dil)


def _group_weights(lgs):
    l0, l1, l2 = lgs[0][...], lgs[1][...], lgs[2][...]
    mx = jnp.maximum(l0, jnp.maximum(l1, l2))
    e0, e1, e2 = jnp.exp(l0 - mx), jnp.exp(l1 - mx), jnp.exp(l2 - mx)
    den = e0 + e1 + e2
    return e0 / den, e1 / den, e2 / den


def _head_spec(base):
    return pl.BlockSpec((S, HEAD_DIM), lambda h, p: (0, base + (p % 3) * 8 + h))


def _slab(dtype=F32, rows=S):
    return pltpu.VMEM((rows, HEAD_DIM), dtype)


def _attn_fwd(z0, qkv, ycat):
    def body(q_ref, k_ref, v_ref, gate_ref, ycat_ref, out_ref, og_ref, lg_ref,
             od, ld, og0, og1, og2, lg0, lg1, lg2):
        del ycat_ref
        p = pl.program_id(1)
        ogs, lgs = (og0, og1, og2), (lg0, lg1, lg2)
        for gi, (_, dil) in enumerate(PATTERNS):
            @pl.when(p == gi)
            def _(gi=gi, dil=dil):
                _attn_group_fwd(dil, q_ref, k_ref, v_ref, od, ld, ogs[gi], lgs[gi])
                og_ref[...] = ogs[gi][...]
                lg_ref[...] = lgs[gi][...]

        @pl.when(p == 2)
        def _():
            w0, w1, w2 = _group_weights(lgs)
            o = w0 * og0[...] + w1 * og1[...] + w2 * og2[...]
            gate = gate_ref[...]
            out_ref[...] = (o * (gate * _sigmoid(gate))).astype(BF16)

    grp = pl.BlockSpec((S, HEAD_DIM), lambda h, p: (0, p * 8 + h))
    grp_pad = pl.BlockSpec((S + BLK, HEAD_DIM), lambda h, p: (0, p * 8 + h))
    return pl.pallas_call(
        body, grid=(8, 3),
        in_specs=[grp, grp_pad, grp_pad, pl.BlockSpec((S, HEAD_DIM), lambda h, p: (0, BG_COL + h)), ANY_SPEC],
        out_specs=[pl.BlockSpec((S, HEAD_DIM), lambda h, p: (0, 8 + h)), grp, grp],
        out_shape=[jax.ShapeDtypeStruct((S, 2048), BF16), jax.ShapeDtypeStruct((S, 3072), F32),
                   jax.ShapeDtypeStruct((S, 3072), F32)],
        scratch_shapes=[_slab() for _ in range(8)],
        input_output_aliases={4: 0},
        compiler_params=_params(("parallel", "arbitrary"), VMEM_BIG), name="attn_fwd",
    )(*qkv, z0, ycat)


def _attn_bwd(z0, qkv, og, lg, dycat, tabs, dep):
    def body(q_ref, k_ref, v_ref, gate_ref, dy_ref, c_ref, a_ref, b_ref,
             og0_ref, og1_ref, og2_ref, lg0_ref, lg1_ref, lg2_ref, dep_ref,
             dq_ref, dk_ref, dv_ref, dbg_ref,
             tmp, ld, dg0, dg1, dg2, cg0, cg1, cg2, dod, cd, dqd, dkd, dvd):
        kd, vd = k_ref, v_ref
        p = pl.program_id(1)
        ogs, lgs, dgs, cgs = (og0_ref, og1_ref, og2_ref), (lg0_ref, lg1_ref, lg2_ref), (dg0, dg1, dg2), (cg0, cg1, cg2)

        @pl.when(p == 0)
        def _():
            w = _group_weights(lgs)
            o = w[0] * ogs[0][...] + w[1] * ogs[1][...] + w[2] * ogs[2][...]
            silu, dsilu = _silu_and_grad(gate_ref[...])
            dy = dy_ref[...]
            dbg_ref[...] = (dy * o * dsilu).astype(BF16)
            do = dy * silu
            dwbar = jnp.sum(do * o, axis=1, keepdims=True)
            for gi in range(3):
                dgs[gi][...] = w[gi] * do
                cgs[gi][...] = -w[gi] * dwbar

        for gi, (_, dil) in enumerate(PATTERNS):
            @pl.when(p == 1 + gi)
            def _(gi=gi, dil=dil):
                nb = S // dil // BLK
                qd = q_ref
                c, a, b = c_ref[...], a_ref[...], b_ref[...]
                _deinterleave(dod, dgs[gi], dil, BF16)
                _deinterleave(ld, lgs[gi], dil)
                _deinterleave(cd, cgs[gi], dil)
                dkd[...] = jnp.zeros_like(dkd)
                dvd[...] = jnp.zeros_like(dvd)
                flat = lambda t: t.reshape(CU * BLK, HEAD_DIM)
                for u0 in range(0, NUNITS, CU):
                    q, s_own, s_prev = _chunk_scores(u0, nb, qd, kd)
                    lse, cv, do = _blocks(ld, u0), _blocks(cd, u0), _blocks(dod, u0)
                    own = slice((u0 + 1) * BLK, (u0 + 1 + CU) * BLK)
                    p_own = jnp.exp(s_own - lse)
                    ds_own = (p_own * (_dot(do, _blocks(vd, u0 + 1), B_QK) + cv) * SCALE).astype(BF16)
                    dq = _dot(ds_own, _blocks(kd, u0 + 1), B_PV)
                    dkd[own, :] += flat(_dot(ds_own, q, B_TN))
                    dvd[own, :] += flat(_dot(p_own.astype(BF16), do, B_TN))
                    if s_prev is not None:
                        prev = slice(u0 * BLK, (u0 + CU) * BLK)
                        p_prev = jnp.exp(s_prev - lse)
                        ds_prev = (p_prev * (_dot(do, _blocks(vd, u0), B_QK) + cv) * SCALE).astype(BF16)
                        dq = dq + _dot(ds_prev, _blocks(kd, u0), B_PV)
                        dkd[prev, :] += flat(_dot(ds_prev, q, B_TN))
                        dvd[prev, :] += flat(_dot(p_prev.astype(BF16), do, B_TN))
                    dqd[u0 * BLK:(u0 + CU) * BLK, :] = flat(dq)
                _interleave(tmp, dqd, dil)
                dq_ref[...] = _rope_t(tmp[...], c, a, b).astype(BF16)
                _interleave(tmp, dkd, dil, BLK)
                dk_ref[...] = _rope_t(tmp[...], c, a, b).astype(BF16)
                _interleave(tmp, dvd, dil, BLK)
                dv_ref[...] = tmp[...].astype(BF16)

    tab = pl.BlockSpec((S, HEAD_DIM), lambda h, p: (0, 0))
    hspec = lambda base: pl.BlockSpec((S, HEAD_DIM), lambda h, p: (0, base + h))
    gspec = pl.BlockSpec((S, HEAD_DIM), lambda h, p: (0, jnp.maximum(p - 1, 0) * 8 + h))
    gspec_pad = pl.BlockSpec((S + BLK, HEAD_DIM), lambda h, p: (0, jnp.maximum(p - 1, 0) * 8 + h))
    return pl.pallas_call(
        body, grid=(8, 4),
        in_specs=[gspec, gspec_pad, gspec_pad, hspec(BG_COL), hspec(8), tab, tab, tab,
                  hspec(0), hspec(8), hspec(16), hspec(0), hspec(8), hspec(16), ANY_SPEC],
        out_specs=[gspec, gspec, gspec, hspec(0)],
        out_shape=[jax.ShapeDtypeStruct((S, 3072), BF16)] * 3 + [jax.ShapeDtypeStruct((S, HALF), BF16)],
        scratch_shapes=[_slab(), _slab()] + [_slab() for _ in range(6)]
                       + [_slab(BF16), _slab(), _slab(), _slab(F32, S + BLK), _slab(F32, S + BLK)],
        compiler_params=_params(("parallel", "arbitrary"), VMEM_BIG), name="attn_bwd",
    )(*qkv, z0, dycat, *tabs, og, og, og, lg, lg, lg, dep)


SGU_CH = 256
NCHUNK = TR // 128


def _ln_stats(x):
    mu = jnp.mean(x, axis=-1, keepdims=True)
    xc = x - mu
    r = lax.rsqrt(jnp.mean(xc * xc, axis=-1, keepdims=True) + EPS)
    return xc * r, r


def _ln_bwd(dy, xhat, r, g):
    dxh = dy * g
    return r * (dxh - jnp.mean(dxh, axis=-1, keepdims=True) - xhat * jnp.mean(dxh * xhat, axis=-1, keepdims=True))


def _tril_bf16(w):
    row = lax.broadcasted_iota(jnp.int32, w.shape, 0)
    col = lax.broadcasted_iota(jnp.int32, w.shape, 1)
    return jnp.where(row >= col, w, 0.0).astype(BF16)


def _sgu_gate(vn_s, s_s, w_ref, bb_ref):
    for h in range(4):
        wm = _tril_bf16(w_ref[h])
        bias = bb_ref[h]
        for ch in range(NCHUNK):
            rows, cols = slice(ch * 128, (ch + 1) * 128), slice(h * SGU_CH, (h + 1) * SGU_CH)
            s_s[rows, cols] = _dot(wm, vn_s[rows, cols]) + jnp.concatenate([bias, bias], axis=1)


WIN = HALO + TR
SUBL = 8


def _shifted_copies(dst, src):
    dst[0] = src[...]
    for b in range(1, SUBL):
        dst[b, 0:WIN - SUBL, :] = src[pl.ds(b, WIN - SUBL), :]


def _rows_at(copies, off, n):
    return copies[off % SUBL, pl.ds(off - off % SUBL, n), :]


def _conv_fwd(i, dval_ref, dglu_ref, hval_ref, hglu_ref, cw_ref, cb_ref, xw, xr, dcs):
    halo = hval_ref[...] * _sigmoid(hglu_ref[...])
    xw[0:HALO, :] = jnp.where(i > 0, halo, 0.0)
    xw[HALO:HALO + TR, :] = dval_ref[...] * _sigmoid(dglu_ref[...])
    _shifted_copies(xr, xw)
    sub = 2 * SUB
    for rb in range(TR // sub):
        acc = jnp.broadcast_to(cb_ref[...], (sub, HALF))
        for k in range(CONV_K):
            acc = acc + cw_ref[k:k + 1, :] * _rows_at(xr, rb * sub + HALO - (CONV_K - 1) + k, sub)
        dcs[rb * sub:(rb + 1) * sub, :] = acc


def _odd_in_specs():
    col = lambda j: pl.BlockSpec((TR, HALF), lambda i, *_: (i, j))
    prev = lambda j: pl.BlockSpec((HALO, HALF), lambda i, *_: (jnp.maximum(i * (TR // HALO) - 1, 0), j))
    return [col(0), col(1), col(2), col(3), col(4), col(5), prev(3), prev(4)]


def _full_spec(shape):
    return pl.BlockSpec(shape, lambda i, *_: (0,) * len(shape))


def _odd_fwd(z1, sgu_g, sgu_b, sgu_w, sgu_bb, conv_w, conv_b, cn_g, cn_b):
    def body(u_ref, v_ref, cg_ref, dval_ref, dglu_ref, dgate_ref, hval_ref, hglu_ref,
             g_ref, b_ref, w_ref, bb_ref, cw_ref, cb_ref, cng_ref, cnb_ref, out_ref, dcs, vn_s, s_s, xw, xr):
        i = pl.program_id(0)
        vhat, _ = _ln_stats(v_ref[...])
        vn_s[...] = (vhat * g_ref[...] + b_ref[...]).astype(BF16)
        _sgu_gate(vn_s, s_s, w_ref, bb_ref)
        cg = cg_ref[...]
        out_ref[:, 0:HALF] = (u_ref[...] * s_s[...] * (cg * _sigmoid(cg))).astype(BF16)
        _conv_fwd(i, dval_ref, dglu_ref, hval_ref, hglu_ref, cw_ref, cb_ref, xw, xr, dcs)
        dhat, _ = _ln_stats(dcs[...])
        dn = dhat * cng_ref[...] + cnb_ref[...]
        dgate = dgate_ref[...]
        out_ref[:, HALF:2 * HALF] = ((dn * _sigmoid(dn)) * (dgate * _sigmoid(dgate))).astype(BF16)

    vec = _full_spec((1, HALF))
    return pl.pallas_call(
        body, grid=(S // TR,),
        in_specs=_odd_in_specs() + [vec, vec, _full_spec((4, 128, 128)), _full_spec((4, 128, 128)),
                                    _full_spec((HALO, HALF)), vec, vec, vec],
        out_specs=[pl.BlockSpec((TR, 2048), lambda i: (i, 0)), pl.BlockSpec((TR, HALF), lambda i: (i, 0))],
        out_shape=[jax.ShapeDtypeStruct((S, 2048), BF16), jax.ShapeDtypeStruct((S, HALF), F32)],
        scratch_shapes=[pltpu.VMEM((TR, HALF), BF16), pltpu.VMEM((TR, HALF), F32),
                        pltpu.VMEM((WIN, HALF), F32), pltpu.VMEM((SUBL, WIN, HALF), F32)],
        compiler_params=_params(("parallel",), VMEM_BIG), name="odd_fwd",
    )(z1, z1, z1, z1, z1, z1, z1, z1, sgu_g, sgu_b, sgu_w, sgu_bb, conv_w, conv_b, cn_g, cn_b)


def _odd_bwd_a(z1, dc, dycat, sgu_g, sgu_b, sgu_w, sgu_bb, cn_g, cn_b):
    def body(u_ref, v_ref, cg_ref, dgate_ref, dcs, dy_ref, g_ref, b_ref, w_ref, bb_ref, cng_ref, cnb_ref,
             dz_ref, ddc_ref, dw_ref, dbb_ref, dg_ref, db_ref, dcng_ref, dcnb_ref, dcb_ref,
             vn_s, s_s, ds_s, dvn_s):
        i = pl.program_id(0)
        vhat, rv = _ln_stats(v_ref[...])
        g = g_ref[...]
        vn_s[...] = (vhat * g + b_ref[...]).astype(BF16)
        _sgu_gate(vn_s, s_s, w_ref, bb_ref)
        silu_c, dsilu_c = _silu_and_grad(cg_ref[...])
        dyc = dy_ref[:, 0:HALF]
        u = u_ref[...]
        s = s_s[...]
        dz_ref[:, 0:HALF] = (dyc * s * silu_c).astype(BF16)
        dz_ref[:, 2 * HALF:3 * HALF] = (dyc * u * s * dsilu_c).astype(BF16)
        ds_s[...] = dyc * u * silu_c

        @pl.when(i == 0)
        def _():
            dw_ref[...] = jnp.zeros_like(dw_ref)
            dbb_ref[...] = jnp.zeros_like(dbb_ref)

        tril = lax.broadcasted_iota(jnp.int32, (128, 128), 0) >= lax.broadcasted_iota(jnp.int32, (128, 128), 1)
        for h in range(4):
            wm = _tril_bf16(w_ref[h])
            for ch in range(NCHUNK):
                rows, cols = slice(ch * 128, (ch + 1) * 128), slice(h * SGU_CH, (h + 1) * SGU_CH)
                ds = ds_s[rows, cols]
                dsb = ds.astype(BF16)
                dw_ref[h] += jnp.where(tril, _dot(dsb, vn_s[rows, cols], NT), 0.0)
                dbb_ref[h] += jnp.broadcast_to(jnp.sum(ds, axis=1, keepdims=True), (128, 128))
                dvn_s[rows, cols] = _dot(wm, dsb, TN)
        dvn = dvn_s[...]
        _acc_rows(dg_ref, dvn * vhat, i)
        _acc_rows(db_ref, dvn, i)
        dz_ref[:, HALF:2 * HALF] = _ln_bwd(dvn, vhat, rv, g).astype(BF16)

        dhat, rd = _ln_stats(dcs[...])
        cng = cng_ref[...]
        silu_n, dsilu_n = _silu_and_grad(dhat * cng + cnb_ref[...])
        silu_g, dsilu_g = _silu_and_grad(dgate_ref[...])
        dyd = dy_ref[:, HALF:2 * HALF]
        dz_ref[:, 5 * HALF:6 * HALF] = (dyd * silu_n * dsilu_g).astype(BF16)
        ddn = dyd * silu_g * dsilu_n
        _acc_rows(dcng_ref, ddn * dhat, i)
        _acc_rows(dcnb_ref, ddn, i)
        ddc = _ln_bwd(ddn, dhat, rd, cng)
        ddc_ref[...] = ddc
        _acc_rows(dcb_ref, ddc, i)

    vec = _full_spec((1, HALF))
    sq = _full_spec((4, 128, 128))
    col = lambda j: pl.BlockSpec((TR, HALF), lambda i: (i, j))
    return pl.pallas_call(
        body, grid=(S // TR,),
        in_specs=[col(0), col(1), col(2), col(5), col(0), pl.BlockSpec((TR, 2048), lambda i: (i, 0)),
                  vec, vec, sq, sq, vec, vec],
        out_specs=[pl.BlockSpec((TR, ODD_IN), lambda i: (i, 0)), pl.BlockSpec((TR, HALF), lambda i: (i, 0)),
                   sq, sq, vec, vec, vec, vec, vec],
        out_shape=[jax.ShapeDtypeStruct((S, ODD_IN), BF16), jax.ShapeDtypeStruct((S, HALF), F32),
                   jax.ShapeDtypeStruct((4, 128, 128), F32), jax.ShapeDtypeStruct((4, 128, 128), F32)]
                  + [jax.ShapeDtypeStruct((1, HALF), F32)] * 5,
        scratch_shapes=[pltpu.VMEM((TR, HALF), BF16), pltpu.VMEM((TR, HALF), F32),
                        pltpu.VMEM((TR, HALF), F32), pltpu.VMEM((TR, HALF), F32)],
        compiler_params=_params(("arbitrary",), VMEM_BIG), name="odd_bwd_a",
    )(z1, z1, z1, z1, dc, dycat, sgu_g, sgu_b, sgu_w, sgu_bb, cn_g, cn_b)


def _odd_bwd_b(z1, ddc, dz1, conv_w):
    nt = S // TR

    def body(dval_ref, dglu_ref, hval_ref, hglu_ref, ddc_ref, hddc_ref, cw_ref, dz_in_ref,
             dz_ref, dcw_ref, xw, dwin, dxs, xr, dr):
        del dz_in_ref
        i, j = pl.program_id(0), pl.program_id(1)
        sg = _sigmoid(dglu_ref[...])
        dval = dval_ref[...]

        @pl.when(j == 0)
        def _():
            halo = hval_ref[...] * _sigmoid(hglu_ref[...])
            xw[0:HALO, :] = jnp.where(i > 0, halo, 0.0)
            xw[HALO:HALO + TR, :] = dval * sg
            dwin[0:TR, :] = ddc_ref[...]
            dwin[TR:TR + HALO, :] = jnp.where(i < nt - 1, hddc_ref[...], 0.0)
            _shifted_copies(xr, xw)
            _shifted_copies(dr, dwin)

            @pl.when(i == 0)
            def _():
                dcw_ref[...] = jnp.zeros_like(dcw_ref)

            for rb in range(TR // SUB):
                acc = jnp.zeros((SUB, HALF), F32)
                for k in range(CONV_K):
                    acc = acc + cw_ref[k:k + 1, :] * _rows_at(dr, rb * SUB + (CONV_K - 1) - k, SUB)
                dxs[rb * SUB:(rb + 1) * SUB, :] = acc
            for k in range(CONV_K):
                acc = jnp.zeros((SUB, HALF), F32)
                for rb in range(TR // SUB):
                    acc = acc + dwin[rb * SUB:(rb + 1) * SUB, :] * _rows_at(xr, rb * SUB + HALO - (CONV_K - 1) + k, SUB)
                dcw_ref[k:k + 1, :] += jnp.sum(acc, axis=0, keepdims=True)
            dz_ref[...] = (dxs[...] * sg).astype(BF16)

        @pl.when(j == 1)
        def _():
            dz_ref[...] = (dxs[...] * dval * sg * (1.0 - sg)).astype(BF16)

    col = lambda c: pl.BlockSpec((TR, HALF), lambda i, j: (i, c))
    prev = lambda c: pl.BlockSpec((HALO, HALF), lambda i, j: (jnp.maximum(i * (TR // HALO) - 1, 0), c))
    nxt = pl.BlockSpec((HALO, HALF), lambda i, j: (jnp.minimum((i + 1) * (TR // HALO), S // HALO - 1), 0))
    return pl.pallas_call(
        body, grid=(nt, 2),
        in_specs=[col(3), col(4), prev(3), prev(4), pl.BlockSpec((TR, HALF), lambda i, j: (i, 0)), nxt,
                  _full_spec((HALO, HALF)), pl.BlockSpec(memory_space=pl.ANY)],
        out_specs=[pl.BlockSpec((TR, HALF), lambda i, j: (i, 3 + j)), _full_spec((HALO, HALF))],
        out_shape=[jax.ShapeDtypeStruct((S, ODD_IN), BF16), jax.ShapeDtypeStruct((HALO, HALF), F32)],
        scratch_shapes=[pltpu.VMEM((WIN, HALF), F32), pltpu.VMEM((WIN, HALF), F32), pltpu.VMEM((TR, HALF), F32),
                        pltpu.VMEM((SUBL, WIN, HALF), F32), pltpu.VMEM((SUBL, WIN, HALF), F32)],
        input_output_aliases={7: 0},
        compiler_params=_params(("arbitrary", "arbitrary"), VMEM_BIG), name="odd_bwd_b",
    )(z1, z1, z1, z1, ddc, ddc, conv_w, dz1)


def _cast_bf16(w, name, piece=0, npieces=1):
    r, c = w.shape[0], w.shape[1] // npieces
    tr = min(r, 256)

    def body(i_ref, o_ref):
        o_ref[...] = i_ref[...].astype(BF16)

    return pl.pallas_call(
        body, grid=(r // tr,), in_specs=[pl.BlockSpec((tr, c), lambda i: (i, piece))],
        out_specs=pl.BlockSpec((tr, c), lambda i: (i, 0)), out_shape=jax.ShapeDtypeStruct((r, c), BF16),
        compiler_params=_params(("parallel",)), name=name,
    )(w)


def _adamw(w, g, m, v):
    m = ADAM_B1 * m + (1.0 - ADAM_B1) * g
    v = ADAM_B2 * v + (1.0 - ADAM_B2) * (g * g)
    m_hat = m / (1.0 - ADAM_B1 ** ADAM_STEP)
    v_hat = v / (1.0 - ADAM_B2 ** ADAM_STEP)
    delta = -ADAM_LR * (m_hat / (jnp.sqrt(v_hat) + ADAM_EPS) + ADAM_WD * w)
    return delta, m, v


def _adam_reduce(parts, w, m, v, name, dep=None, piece=0, npieces=1, prev=None):
    r, c = w.shape
    cp = c // npieces
    tr = min(r, 256)
    extra = ([] if dep is None else [dep]) + ([] if prev is None else list(prev))
    nparts = parts.shape[0]

    def body(p_ref, w_ref, m_ref, v_ref, *rest):
        g_ref, d_ref, nm_ref, nv_ref = rest[len(extra):]
        g = p_ref[0].astype(F32)
        for d in range(1, nparts):
            g = g + p_ref[d].astype(F32)
        g_ref[...] = g
        d_ref[...], nm_ref[...], nv_ref[...] = _adamw(w_ref[...], g, m_ref[...], v_ref[...])

    spec = pl.BlockSpec((tr, cp), lambda i: (i, piece))
    first = 4 + (0 if dep is None else 1)
    return pl.pallas_call(
        body, grid=(r // tr,),
        in_specs=[pl.BlockSpec((nparts, tr, cp), lambda i: (0, i, 0)), spec, spec, spec] + [ANY_SPEC] * len(extra),
        out_specs=[spec] * 4, out_shape=[jax.ShapeDtypeStruct((r, c), F32)] * 4,
        input_output_aliases={} if prev is None else {first + k: k for k in range(4)},
        compiler_params=_params(("parallel",), VMEM_BIG), name=name,
    )(parts, w, m, v, *extra)


def _arrived(x, name, dep=None):
    deps = [] if dep is None else [dep]

    def body(*refs):
        refs[-1][...] = jnp.zeros_like(refs[-1])

    return pl.pallas_call(
        body, in_specs=[ANY_SPEC] * (1 + len(deps)), out_specs=pl.BlockSpec(memory_space=pltpu.VMEM),
        out_shape=jax.ShapeDtypeStruct((8, 128), F32), name=name,
    )(x, *deps)


def _sum_parts(parts, name, dep=None):
    r = parts.shape[1]
    tr = 8
    for cand in (512, 256, 128, 64, 32, 16, 8):
        if r % cand == 0:
            tr = cand
            break
    deps = [] if dep is None else [dep]

    def body(p_ref, *rest):
        g = p_ref[0]
        for d in range(1, NDEV):
            g = g + p_ref[d]
        rest[-1][...] = g

    return pl.pallas_call(
        body, grid=(r // tr,), in_specs=[pl.BlockSpec((NDEV, tr, 128), lambda i: (0, i, 0))] + [ANY_SPEC] * len(deps),
        out_specs=pl.BlockSpec((tr, 128), lambda i: (i, 0)), out_shape=jax.ShapeDtypeStruct((r, 128), F32),
        compiler_params=_params(("parallel",)), name=name,
    )(parts, *deps)


def _sum_unpack(parts, rows, name, dep=None):
    deps = [] if dep is None else [dep]

    def body(p_ref, *outs):
        outs = outs[len(deps):]
        off = 0
        for o_ref, n in zip(outs, rows):
            acc = p_ref[0, off:off + n, :]
            for d in range(1, NDEV):
                acc = acc + p_ref[d, off:off + n, :]
            o_ref[...] = acc
            off += n

    return pl.pallas_call(
        body, grid=(1,), in_specs=[pl.BlockSpec(parts.shape, lambda i: (0, 0, 0))] + [ANY_SPEC] * len(deps),
        out_specs=[pl.BlockSpec((n, 128), lambda i: (0, 0)) for n in rows],
        out_shape=[jax.ShapeDtypeStruct((n, 128), F32) for n in rows],
        compiler_params=_params(("arbitrary",), VMEM_BIG), name=name,
    )(parts, *deps)


def _adam_small(ws, gs, g_specs, ms, vs, name):
    n = len(ws)

    def body(*refs):
        w_r, g_r, m_r, v_r = refs[:n], refs[n:2 * n], refs[2 * n:3 * n], refs[3 * n:4 * n]
        outs = refs[4 * n:]
        for i in range(n):
            g = g_r[i][...]
            outs[4 * i][...] = g
            outs[4 * i + 1][...], outs[4 * i + 2][...], outs[4 * i + 3][...] = _adamw(
                w_r[i][...], g, m_r[i][...], v_r[i][...])

    whole = lambda a: pl.BlockSpec(a.shape, lambda i, nd=a.ndim: (0,) * nd)
    outs = pl.pallas_call(
        body, grid=(1,),
        in_specs=[whole(a) for a in ws] + list(g_specs) + [whole(a) for a in ms] + [whole(a) for a in vs],
        out_specs=[whole(a) for a in ws for _ in range(4)],
        out_shape=[jax.ShapeDtypeStruct(a.shape, F32) for a in ws for _ in range(4)],
        compiler_params=_params(("arbitrary",), VMEM_BIG), name=name,
    )(*ws, *gs, *ms, *vs)
    return [outs[4 * i:4 * i + 4] for i in range(n)]


MASKS = [(mx, my, mc) for mx in (0, 1) for my in (0, 1) for mc in (0, 1)][1:]


def _sc_exchange(name, collective_id, arrays, scatter):
    nt = len(arrays)
    out_type = [jax.ShapeDtypeStruct(a.shape if scatter else (NDEV,) + a.shape, a.dtype) for a in arrays]

    def body(*refs):
        ins, outs = refs[:nt], refs[nt:2 * nt]
        send_sems, recv_sems, local_sems = refs[2 * nt:3 * nt], refs[3 * nt:4 * nt], refs[4 * nt:5 * nt]
        x, y, c = lax.axis_index("x"), lax.axis_index("y"), lax.axis_index("c")
        peers = [(mx + x - 2 * mx * x, my + y - 2 * my * y, mc + c - 2 * mc * c) for mx, my, mc in MASKS]
        barrier = pltpu.get_barrier_semaphore()
        for peer in peers:
            pl.semaphore_signal(barrier, inc=1, device_id=peer, device_id_type=MESH)
        pl.semaphore_wait(barrier, len(peers))
        me = 4 * x + 2 * y + c
        own = []
        for t in range(nt):
            cp = pltpu.make_async_copy(ins[t].at[me] if scatter else ins[t], outs[t].at[me], local_sems[t])
            cp.start()
            own.append(cp)
            for px, py, pc in peers:
                src = ins[t].at[4 * px + 2 * py + pc] if scatter else ins[t]
                pltpu.make_async_remote_copy(src_ref=src, dst_ref=outs[t].at[me], send_sem=send_sems[t],
                                             recv_sem=recv_sems[t], device_id=(px, py, pc), device_id_type=MESH).start()
        for t in range(nt):
            own[t].wait()
            seven = outs[t].at[pl.ds(0, NDEV - 1)]
            drain = pltpu.make_async_remote_copy(src_ref=seven, dst_ref=seven, send_sem=send_sems[t],
                                                 recv_sem=recv_sems[t], device_id=(x, y, c), device_id_type=MESH)
            drain.wait_send()
            drain.wait_recv()

    return pl.kernel(
        body, out_type=out_type, mesh=plsc.ScalarSubcoreMesh(axis_name="sequencer", num_cores=1),
        scratch_types=[pltpu.SemaphoreType.DMA] * (3 * nt),
        compiler_params=pltpu.CompilerParams(collective_id=collective_id), name=name,
    )(*arrays)


def _sc_gather_two_level(name, collective_id, arrays):
    nt = len(arrays)
    out_type = [jax.ShapeDtypeStruct((NDEV,) + a.shape, a.dtype) for a in arrays]

    def body(*refs):
        ins, outs = refs[:nt], refs[nt:2 * nt]
        sems = refs[2 * nt:]
        send_sems, sib_sems, local_sems = sems[:nt], sems[nt:2 * nt], sems[2 * nt:3 * nt]
        ici_sems = [sems[3 * nt + 3 * t:3 * nt + 3 * t + 3] for t in range(nt)]
        x, y, c = lax.axis_index("x"), lax.axis_index("y"), lax.axis_index("c")
        sibling = (x, y, 1 - c)
        chips = [(1 - x, y), (x, 1 - y), (1 - x, 1 - y)]
        barrier = pltpu.get_barrier_semaphore()
        for peer in [sibling] + [(cx, cy, c) for cx, cy in chips]:
            pl.semaphore_signal(barrier, inc=1, device_id=peer, device_id_type=MESH)
        pl.semaphore_wait(barrier, 4)
        me = 4 * x + 2 * y + c

        def push(t, src, slot, recv_sem, to):
            pltpu.make_async_remote_copy(src_ref=src, dst_ref=outs[t].at[slot], send_sem=send_sems[t],
                                         recv_sem=recv_sem, device_id=to, device_id_type=MESH).start()

        own = []
        for t in range(nt):
            cp = pltpu.make_async_copy(ins[t], outs[t].at[me], local_sems[t])
            cp.start()
            own.append(cp)
            for j, (cx, cy) in enumerate(chips):
                push(t, ins[t], me, ici_sems[t][j], (cx, cy, c))
            push(t, ins[t], me, sib_sems[t], sibling)
        for t in range(nt):
            for j, (cx, cy) in enumerate(chips):
                slot = 4 * cx + 2 * cy + c
                landed = outs[t].at[slot]
                pltpu.make_async_remote_copy(src_ref=landed, dst_ref=landed, send_sem=send_sems[t],
                                             recv_sem=ici_sems[t][j], device_id=(cx, cy, c),
                                             device_id_type=MESH).wait_recv()
                push(t, landed, slot, sib_sems[t], sibling)
        for t in range(nt):
            own[t].wait()
            four, seven = outs[t].at[pl.ds(0, 4)], outs[t].at[pl.ds(0, 7)]
            pltpu.make_async_remote_copy(src_ref=four, dst_ref=four, send_sem=send_sems[t], recv_sem=sib_sems[t],
                                         device_id=sibling, device_id_type=MESH).wait_recv()
            pltpu.make_async_remote_copy(src_ref=seven, dst_ref=seven, send_sem=send_sems[t], recv_sem=sib_sems[t],
                                         device_id=sibling, device_id_type=MESH).wait_send()

    return pl.kernel(
        body, out_type=out_type, mesh=plsc.ScalarSubcoreMesh(axis_name="sequencer", num_cores=1),
        scratch_types=[pltpu.SemaphoreType.DMA] * (6 * nt),
        compiler_params=pltpu.CompilerParams(collective_id=collective_id), name=name,
    )(*arrays)


def _sc_sibling_exchange(name, collective_id, src, out_shape, pieces, after=None):
    extra = [] if after is None else [after]

    def body(src_ref, *rest):
        out_ref, send_sem, recv_sem = rest[len(extra):]
        x, y, c = lax.axis_index("x"), lax.axis_index("y"), lax.axis_index("c")
        sibling = (x, y, 1 - c)
        barrier = pltpu.get_barrier_semaphore()
        pl.semaphore_signal(barrier, inc=1, device_id=sibling, device_id_type=MESH)
        pl.semaphore_wait(barrier, 1)
        for piece, lands in pieces(c, src_ref, out_ref):
            pltpu.make_async_remote_copy(src_ref=piece, dst_ref=lands, send_sem=send_sem, recv_sem=recv_sem,
                                         device_id=sibling, device_id_type=MESH).start()
        drain = pltpu.make_async_remote_copy(src_ref=out_ref, dst_ref=out_ref, send_sem=send_sem, recv_sem=recv_sem,
                                             device_id=sibling, device_id_type=MESH)
        drain.wait_send()
        drain.wait_recv()

    return pl.kernel(
        body, out_type=jax.ShapeDtypeStruct(out_shape, src.dtype),
        mesh=plsc.ScalarSubcoreMesh(axis_name="sequencer", num_cores=1), scratch_types=[pltpu.SemaphoreType.DMA] * 2,
        compiler_params=pltpu.CompilerParams(collective_id=collective_id), name=name,
    )(src, *extra)


def _swap_class_columns(name, collective_id, dz, nb, piece=0, npieces=1):
    w = nb // npieces
    return _sc_sibling_exchange(
        name, collective_id, dz, (S, 4 * w),
        lambda c, src, out: [(src.at[:, pl.ds((2 * j + 1 - c) * nb + piece * w, w)], out.at[:, pl.ds(j * w, w)])
                             for j in range(4)])


def _sc_chip_scatter(name, collective_id, q):
    def body(q_ref, out_ref, send_sem, recv_sem, local_sem):
        x, y, c = lax.axis_index("x"), lax.axis_index("y"), lax.axis_index("c")
        chips = [(1 - x, y), (x, 1 - y), (1 - x, 1 - y)]
        barrier = pltpu.get_barrier_semaphore()
        for cx, cy in chips:
            pl.semaphore_signal(barrier, inc=1, device_id=(cx, cy, c), device_id_type=MESH)
        pl.semaphore_wait(barrier, 3)
        mine = 2 * x + y
        own = pltpu.make_async_copy(q_ref.at[mine], out_ref.at[mine], local_sem)
        own.start()
        for cx, cy in chips:
            pltpu.make_async_remote_copy(src_ref=q_ref.at[2 * cx + cy], dst_ref=out_ref.at[mine], send_sem=send_sem,
                                         recv_sem=recv_sem, device_id=(cx, cy, c), device_id_type=MESH).start()
        own.wait()
        three = out_ref.at[pl.ds(0, 3)]
        drain = pltpu.make_async_remote_copy(src_ref=three, dst_ref=three, send_sem=send_sem, recv_sem=recv_sem,
                                             device_id=(x, y, c), device_id_type=MESH)
        drain.wait_send()
        drain.wait_recv()

    return pl.kernel(
        body, out_type=jax.ShapeDtypeStruct(q.shape, q.dtype),
        mesh=plsc.ScalarSubcoreMesh(axis_name="sequencer", num_cores=1), scratch_types=[pltpu.SemaphoreType.DMA] * 3,
        compiler_params=pltpu.CompilerParams(collective_id=collective_id), name=name,
    )(q)


def _mm_pair_dw(h_own, dz, h_sib, dz_sib, nb, name, dep=None, piece=0, npieces=1, h_transposed=False,
                one_call=False):
    nb = nb // npieces
    tn = 512 if nb % 512 == 0 else nb
    per = nb // tn
    dn = NN if h_transposed else TN
    o_spec = pl.BlockSpec((None, D, tn), lambda i, j, k: (j // per, 0, j % per))
    own_col = lambda i, j, k: (0, ((2 * (j // per) + lax.axis_index("c")) * npieces + piece) * per + j % per)
    if one_call:
        def fused(a0_ref, b0_ref, a1_ref, b1_ref, dep_ref, o_ref):
            acc = _dot(a0_ref[...], b0_ref[...], dn) + _dot(a1_ref[...], b1_ref[...], dn)
            o_ref[...] = acc.astype(BF16)

        whole = pl.BlockSpec((S, D), lambda i, j, k: (0, 0), pipeline_mode=pl.Buffered(1))
        return pl.pallas_call(
            fused, grid=(1, 4 * per, 1),
            in_specs=[whole, pl.BlockSpec((S, tn), own_col), whole, pl.BlockSpec((S, tn), lambda i, j, k: (0, j)),
                      ANY_SPEC],
            out_specs=o_spec, out_shape=jax.ShapeDtypeStruct((4, D, nb), BF16),
            compiler_params=_params(("parallel", "parallel", "arbitrary"), VMEM_BIG), name=name,
        )(h_own, dz, h_sib, dz_sib, dep)
    part = _matmul(
        h_own, dz, dn=dn, grid=(1, 4 * per, 1),
        a_spec=pl.BlockSpec((S, D), lambda i, j, k: (0, 0)), b_spec=pl.BlockSpec((S, tn), own_col),
        o_spec=o_spec, out_shape=(4, D, nb), out_dtype=F32, acc_shape=(D, tn), name=name + "_own", dep=dep)

    def body(a_ref, b_ref, p_ref, o_ref):
        o_ref[...] = (p_ref[...] + _dot(a_ref[...], b_ref[...], dn)).astype(BF16)

    return pl.pallas_call(
        body, grid=(1, 4 * per, 1),
        in_specs=[pl.BlockSpec((S, D), lambda i, j, k: (0, 0)), pl.BlockSpec((S, tn), lambda i, j, k: (0, j)), o_spec],
        out_specs=o_spec, out_shape=jax.ShapeDtypeStruct((4, D, nb), BF16),
        compiler_params=_params(("parallel", "parallel", "arbitrary"), VMEM_BIG), name=name + "_sibling",
    )(h_sib, dz_sib, part)


SMALL = {
    "e_pre_norm": ((2048,), None), "e_pool_w": ((4, 256, 256), 1), "e_pool_scale": ((1024,), None),
    "e_post_norm": ((2048,), None), "o_pre_norm": ((2048,), 0), "o_sgu_norm_g": ((1024,), 0),
    "o_sgu_norm_b": ((1024,), 0), "o_sgu_w": ((4, 128, 128), None), "o_sgu_b": ((4, 128), None),
    "o_conv_w": ((31, 1024), 1), "o_conv_b": ((1024,), 0), "o_conv_norm_g": ((1024,), 0),
    "o_conv_norm_b": ((1024,), 0), "o_post_norm": ((2048,), 0),
}
SMALL_SHARDED = [n for n, (_, ax) in SMALL.items() if ax is not None]


def _shard_shape(name):
    shape, ax = SMALL[name]
    if ax is None:
        return shape
    return tuple(s // NDEV if i == ax else s for i, s in enumerate(shape))


def _pack(arrs, row_multiple=1):
    flat = jnp.concatenate([a.reshape(-1) for a in arrs])
    pad = -flat.shape[0] % (128 * row_multiple)
    return jnp.concatenate([flat, jnp.zeros((pad,), F32)]).reshape(-1, 128)


def _small_views(name):
    shape, ax = SMALL[name]
    me = lambda: 4 * lax.axis_index("x") + 2 * lax.axis_index("y") + lax.axis_index("c")
    if ax is None:
        view = (int(np.prod(shape)) // 128, 128)
        return view, view, pl.BlockSpec(view, lambda i: (0, 0))
    if len(shape) == 1:
        n = shape[0] // NDEV
        return (1, n), (NDEV, 1, n), pl.BlockSpec((None, 1, n), lambda i: (me(), 0, 0))
    part = _shard_shape(name)
    return part, shape, pl.BlockSpec(part, lambda i: tuple(me() if d == ax else 0 for d in range(len(shape))))


WEIGHTS = ["e_pre_norm", "e_w_in", "e_pool_w", "e_pool_scale", "e_w_out", "e_post_norm", "o_pre_norm", "o_w_in",
           "o_sgu_norm_g", "o_sgu_norm_b", "o_sgu_w", "o_sgu_b", "o_conv_w", "o_conv_b", "o_conv_norm_g",
           "o_conv_norm_b", "o_w_out", "o_post_norm"]


def kernel(x, e_pre_norm, e_w_in, e_pool_w, e_pool_scale, e_w_out, e_post_norm, o_pre_norm, o_w_in, o_sgu_norm_g, o_sgu_norm_b, o_sgu_w, o_sgu_b, o_conv_w, o_conv_b, o_conv_norm_g, o_conv_norm_b, o_w_out, o_post_norm, loss_target, m_e_pre_norm, m_e_w_in, m_e_pool_w, m_e_pool_scale, m_e_w_out, m_e_post_norm, m_o_pre_norm, m_o_w_in, m_o_sgu_norm_g, m_o_sgu_norm_b, m_o_sgu_w, m_o_sgu_b, m_o_conv_w, m_o_conv_b, m_o_conv_norm_g, m_o_conv_norm_b, m_o_w_out, m_o_post_norm, v_e_pre_norm, v_e_w_in, v_e_pool_w, v_e_pool_scale, v_e_w_out, v_e_post_norm, v_o_pre_norm, v_o_w_in, v_o_sgu_norm_g, v_o_sgu_norm_b, v_o_sgu_w, v_o_sgu_b, v_o_conv_w, v_o_conv_b, v_o_conv_norm_g, v_o_conv_norm_b, v_o_w_out, v_o_post_norm):
    given = dict(locals())
    w = {n: given[n][0] for n in WEIGHTS}
    m = {n: given["m_" + n][0] for n in WEIGHTS}
    v = {n: given["v_" + n][0] for n in WEIGHTS}
    x, target = x[0], loss_target[0]
    row = lambda a: a.reshape(1, -1)

    lo, small_rows = _sc_gather_two_level(
        "gather_a0", 0, [_cast_bf16(w["e_w_in"], "cast_e_w_in_0", 0, 2), _pack([w[n] for n in SMALL_SHARDED])])
    hi, = _sc_gather_two_level("gather_a1", 12, [_cast_bf16(w["e_w_in"], "cast_e_w_in_1", 1, 2)])
    wg_e_in = (lo, hi)
    h0, h0t = _pre0_fwd(x, row(w["e_pre_norm"]))
    wg_e_out, = _sc_gather_two_level("gather_b", 1, [_cast_bf16(w["e_w_out"], "cast_e_w_out")])
    wg_o_in, = _sc_gather_two_level("gather_c", 13, [_cast_bf16(w["o_w_in"], "cast_o_w_in")])
    wg_o_out, = _sc_gather_two_level("gather_d", 16, [_cast_bf16(w["o_w_out"], "cast_o_w_out")])
    p = {n: w[n] for n in SMALL if SMALL[n][1] is None}
    small_rows = small_rows.reshape(NDEV, -1)
    off = 0
    for n in SMALL_SHARDED:
        shp, ax = _shard_shape(n), SMALL[n][1]
        cnt = int(np.prod(shp))
        blk = small_rows[:, off:off + cnt].reshape((NDEV,) + shp)
        p[n] = jnp.moveaxis(blk, 0, ax).reshape(SMALL[n][0])
        off += cnt
    tabs = _rope_tables()
    pool_w_bf = p["e_pool_w"].astype(BF16)
    sgu_bb = jnp.broadcast_to(p["o_sgu_b"][:, :, None], (4, 128, 128))
    conv_w = jnp.concatenate([p["o_conv_w"], jnp.zeros((HALO - CONV_K, HALF), F32)], axis=0)
    odd_p = (row(p["o_sgu_norm_g"]), row(p["o_sgu_norm_b"]), p["o_sgu_w"], sgu_bb, conv_w,
             row(p["o_conv_b"]), row(p["o_conv_norm_g"]), row(p["o_conv_norm_b"]))

    z0 = _mm_in_halves(h0, wg_e_in, "mm_z0")
    ycat0 = _pool_fwd(z0, pool_w_bf, row(p["e_pool_scale"]))
    qkv = _qkv_prep(z0, tabs)
    ycat0, og, lg = _attn_fwd(z0, qkv, ycat0)
    w_out_e, w_out_o = wg_e_out.reshape(2048, D), wg_o_out.reshape(2048, D)
    y0, x1, h1 = _post0_fwd(ycat0, w_out_e, x, row(p["e_post_norm"]), row(p["o_pre_norm"]), wg_e_out)
    h0t_sib = _sc_sibling_exchange("swap_h0", 8, h0t, h0t.shape, lambda c, src, out: [(src, out)], h1)
    h1_sib = _sc_sibling_exchange("swap_h1", 11, h1, h1.shape, lambda c, src, out: [(src, out)])
    z1 = _mm_in(h1, wg_o_in, "mm_z1")
    ycat1, conv_out = _odd_fwd(z1, *odd_p)

    g = {}
    loss_part, dx2, dy1, g["o_post_norm"] = _post1_bwd(ycat1, w_out_o, x1, target, row(p["o_post_norm"]),
                                                       _arrived(h1_sib, "arrived_h_sib", h0t_sib))
    parts = {}
    dw = _mm_out_dw(ycat1, dy1, "mm_dwout1").reshape(NDEV, 256, D)
    parts["o_w_out"], = _sc_exchange("scatter_o_w_out", 2, [dw], True)
    dycat1 = _mm_out_dx(dy1, w_out_o, "mm_dycat1", dw)
    dz1, ddc, g["o_sgu_w"], d_sgu_bb, g["o_sgu_norm_g"], g["o_sgu_norm_b"], g["o_conv_norm_g"], \
        g["o_conv_norm_b"], g["o_conv_b"] = _odd_bwd_a(z1, conv_out, dycat1, *odd_p[:4], *odd_p[6:])
    dz1, d_conv_w = _odd_bwd_b(z1, ddc, dz1, conv_w)
    g["o_sgu_b"] = d_sgu_bb[:, :, 0]
    g["o_conv_w"] = d_conv_w[:CONV_K]
    grads, deltas, new_m, new_v = {}, {}, {}, {}

    def adam(n, dep):
        grads[n], deltas[n], new_m[n], new_v[n] = _adam_reduce(parts[n], w[n], m[n], v[n], "adam_" + n, dep)
        return new_v[n]

    pin = _arrived(parts["o_w_out"], "arrived_o_w_out", d_conv_w)
    dz1_sib = _swap_class_columns("swap_dz1", 10, dz1, ODD_IN // NDEV)
    dw = _mm_pair_dw(h1, dz1, h1_sib, dz1_sib, ODD_IN // NDEV, "mm_dwin1", pin)
    parts["o_w_in"] = _sc_chip_scatter("scatter_o_w_in", 3, dw)
    dh1 = _mm_in_dx(dz1, wg_o_in, "mm_dh1", dw)
    dx1, dy0, g["o_pre_norm"], g["e_post_norm"] = _mid_bwd(dx2, dh1, x1, y0, row(p["o_pre_norm"]),
                                                           row(p["e_post_norm"]))
    dw = _mm_out_dw(ycat0, dy0, "mm_dwout0").reshape(NDEV, 256, D)
    parts["e_w_out"], = _sc_exchange("scatter_e_w_out", 4, [dw], True)
    dycat0 = _mm_out_dx(dy0, w_out_e, "mm_dycat0", dw)
    da_in, da_gate, g["e_pool_w"], g["e_pool_scale"] = _pool_bwd(z0, dycat0, pool_w_bf, row(p["e_pool_scale"]))
    late = [n for n in SMALL if n not in ("e_pre_norm", "o_sgu_b")] + ["o_sgu_b"]
    pieces = [g[n].reshape(SMALL[n][0]) for n in late[:-1]] + [jnp.broadcast_to(loss_part, (8, 128)), g[late[-1]]]
    recv_small, = _sc_gather_two_level("gather_small_grads", 6, [_pack(pieces, 512)])
    took = _arrived(parts["o_w_in"], "arrived_o_w_in")
    dq, dk, dv, dbg = _attn_bwd(z0, qkv, og, lg, dycat0, tabs, took)
    dz0 = jnp.concatenate([da_in, da_gate, dq, dk, dv, dbg], axis=1)
    took = _arrived(recv_small, "arrived_small_grads", _arrived(parts["e_w_out"], "arrived_e_w_out", dz0))
    nb = EVEN_IN // NDEV
    swapped = [_swap_class_columns("swap_dz0_%d" % half, (9, 14)[half], dz0, nb, half, 2) for half in (0, 1)]
    dw, e_w_in_parts = took, []
    for half in (0, 1):
        dw = _mm_pair_dw(h0t, dz0, h0t_sib, swapped[half], nb, "mm_dwin0_%d" % half, dw, half, 2, True, half == 1)
        e_w_in_parts.append(_sc_chip_scatter("scatter_e_w_in_%d" % half, (5, 15)[half], dw))
    pin = adam("e_w_out", adam("o_w_out", adam("o_w_in", dw)))
    rows = [int(np.prod(SMALL[n][0])) // 128 for n in late]
    sums = _sum_unpack(recv_small, rows[:-1] + [8, rows[-1]], "sum_small_grads", pin)
    summed = dict(zip(late, sums[:-2] + sums[-1:]))
    loss = sums[-2][0, 0]
    dh0 = _mm_in_dx_halves(dz0, wg_e_in, "mm_dh0", summed[late[0]])
    grad_x, g["e_pre_norm"] = _pre0_bwd(dx1, dh0, x, row(p["e_pre_norm"]))
    last, = _sc_exchange("gather_e_pre_norm_grad", 7, [g["e_pre_norm"].reshape(16, 128)], False)

    n = "e_w_in"
    out = _adam_reduce(e_w_in_parts[0], w[n], m[n], v[n], "adam_e_w_in_0", grad_x, 0, 2)
    out = _adam_reduce(e_w_in_parts[1], w[n], m[n], v[n], "adam_e_w_in_1", None, 1, 2, out)
    grads[n], deltas[n], new_m[n], new_v[n] = out
    summed["e_pre_norm"] = _sum_parts(last, "sum_e_pre_norm_grad", out[3])
    names = list(SMALL)
    views = [_small_views(n) for n in names]
    mine = lambda src: [src[n].reshape(vw[0]) for n, vw in zip(names, views)]
    res = _adam_small(mine(w), [summed[n].reshape(vw[1]) for n, vw in zip(names, views)], [vw[2] for vw in views],
                      mine(m), mine(v), "adam_small")
    for n, out in zip(names, res):
        grads[n], deltas[n], new_m[n], new_v[n] = [t.reshape(_shard_shape(n)) for t in out]

    lead = lambda a: a[None]
    return (loss, grad_x[None], *[lead(grads[n]) for n in WEIGHTS], *[lead(deltas[n]) for n in WEIGHTS],
            *[lead(new_m[n]) for n in WEIGHTS], *[lead(new_v[n]) for n in WEIGHTS])
```

```python
import numpy as np
import jax
import jax.numpy as jnp
from jax import lax
from jax.experimental import pallas as pl
from jax.experimental.pallas import tpu as pltpu
from jax.experimental.pallas import tpu_sc as plsc

F32 = jnp.float32
BF16 = jnp.bfloat16

S = 2048
D = 2048
NDEV = 8
EPS = 1e-6
NEG = -1e30
HEAD_DIM = 128
ROT_DIM = 32
ROPE_THETA = 500000.0
PATTERNS = ((128, 1), (512, 4), (2048, 16))
BLK = 128
EVEN_IN = 12288
ODD_IN = 6144
HALF = 1024
CONV_K = 31
HALO = 32
TR = 256
SUB = 16

ADAM_LR = 0.001
ADAM_B1 = 0.9
ADAM_B2 = 0.999
ADAM_EPS = 1e-08
ADAM_WD = 0.01
ADAM_STEP = 10

VMEM_BIG = 56 * 1024 * 1024
MESH = pl.DeviceIdType.MESH

NN = (((1,), (0,)), ((), ()))
NT = (((1,), (1,)), ((), ()))
TN = (((0,), (0,)), ((), ()))


def _dot(a, b, dn=NN):
    return lax.dot_general(a, b, dn, preferred_element_type=F32)


def _sigmoid(x):
    return 1.0 / (1.0 + jnp.exp(-x))


def _silu_and_grad(x):
    sg = _sigmoid(x)
    return x * sg, sg * (1.0 + x * (1.0 - sg))


def _params(sem, vmem=None):
    return pltpu.CompilerParams(dimension_semantics=sem, vmem_limit_bytes=vmem)


ANY_SPEC = pl.BlockSpec(memory_space=pl.ANY)


def _matmul(a, b, *, dn, grid, a_spec, b_spec, o_spec, out_shape, out_dtype, acc_shape, name, dep=None):
    nk = grid[2]
    deps = [] if dep is None else list(dep) if isinstance(dep, (tuple, list)) else [dep]

    def body(a_ref, b_ref, *rest):
        o_ref, acc = rest[len(deps)], rest[len(deps) + 1:]
        if nk == 1:
            o_ref[...] = _dot(a_ref[...], b_ref[...], dn).astype(o_ref.dtype)
            return
        acc_ref = acc[0]
        k = pl.program_id(2)

        @pl.when(k == 0)
        def _():
            acc_ref[...] = jnp.zeros_like(acc_ref)

        acc_ref[...] += _dot(a_ref[...], b_ref[...], dn)

        @pl.when(k == nk - 1)
        def _():
            o_ref[...] = acc_ref[...].astype(o_ref.dtype)

    return pl.pallas_call(
        body, grid=grid, in_specs=[a_spec, b_spec] + [ANY_SPEC] * len(deps), out_specs=o_spec,
        out_shape=jax.ShapeDtypeStruct(out_shape, out_dtype),
        scratch_shapes=[] if nk == 1 else [pltpu.VMEM(acc_shape, F32)],
        compiler_params=_params(("parallel", "parallel", "arbitrary"), VMEM_BIG), name=name,
    )(a, b, *deps)


TM = 2048


def _mm_in(h, wg, name):
    nb = wg.shape[2]
    tn = 512 if nb % 512 == 0 else nb
    per = nb // tn
    return _matmul(
        h, wg, dn=NN, grid=(S // TM, NDEV * per, 1),
        a_spec=pl.BlockSpec((TM, D), lambda i, j, k: (i, 0)),
        b_spec=pl.BlockSpec((None, D, tn), lambda i, j, k: (j // per, 0, j % per)),
        o_spec=pl.BlockSpec((TM, tn), lambda i, j, k: (i, j)),
        out_shape=(S, NDEV * nb), out_dtype=F32, acc_shape=(TM, tn), name=name)


def _mm_in_halves(h, wg_halves, name):
    hb = wg_halves[0].shape[2]
    z = None
    for half, wg in enumerate(wg_halves):
        prev = [] if z is None else [z]

        def body(a_ref, b_ref, *rest):
            rest[-1][...] = _dot(a_ref[...], b_ref[...])

        z = pl.pallas_call(
            body, grid=(NDEV,),
            in_specs=[pl.BlockSpec((S, D), lambda j: (0, 0)), pl.BlockSpec((None, D, hb), lambda j: (j, 0, 0))]
                     + [ANY_SPEC] * len(prev),
            out_specs=pl.BlockSpec((S, hb), lambda j, half=half: (0, 2 * j + half)),
            out_shape=jax.ShapeDtypeStruct((S, 2 * NDEV * hb), F32),
            input_output_aliases={2: 0} if prev else {},
            compiler_params=_params(("parallel",), VMEM_BIG), name="%s_%d" % (name, half),
        )(h, wg, *prev)
    return z


def _mm_in_dx_halves(dz, wg_halves, name, dep):
    hb = wg_halves[0].shape[2]
    nk = 2 * NDEV

    def body(a_ref, b0_ref, b1_ref, dep_ref, o_ref, acc_ref):
        k = pl.program_id(2)

        @pl.when(k == 0)
        def _():
            acc_ref[...] = jnp.zeros_like(acc_ref)

        @pl.when(k % 2 == 0)
        def _():
            acc_ref[...] += _dot(a_ref[...], b0_ref[...], NT)

        @pl.when(k % 2 == 1)
        def _():
            acc_ref[...] += _dot(a_ref[...], b1_ref[...], NT)

        @pl.when(k == nk - 1)
        def _():
            o_ref[...] = acc_ref[...]

    b_spec = pl.BlockSpec((None, 1024, hb), lambda i, j, k: (k // 2, j, 0))
    return pl.pallas_call(
        body, grid=(1, D // 1024, nk),
        in_specs=[pl.BlockSpec((S, hb), lambda i, j, k: (0, k)), b_spec, b_spec, ANY_SPEC],
        out_specs=pl.BlockSpec((S, 1024), lambda i, j, k: (0, j)), out_shape=jax.ShapeDtypeStruct((S, D), F32),
        scratch_shapes=[pltpu.VMEM((S, 1024), F32)],
        compiler_params=_params(("parallel", "parallel", "arbitrary"), VMEM_BIG), name=name,
    )(dz, *wg_halves, dep)


def _mm_in_dx(dz, wg, name, dep=None):
    nb = wg.shape[2]
    return _matmul(
        dz, wg, dn=NT, grid=(S // TM, D // 1024, NDEV),
        a_spec=pl.BlockSpec((TM, nb), lambda i, j, k: (i, k)),
        b_spec=pl.BlockSpec((None, 1024, nb), lambda i, j, k: (k, j, 0)),
        o_spec=pl.BlockSpec((TM, 1024), lambda i, j, k: (i, j)),
        out_shape=(S, D), out_dtype=F32, acc_shape=(TM, 1024), name=name, dep=dep)


def _mm_out_dx(dy, w, name, dep=None):
    return _matmul(
        dy, w, dn=NT, grid=(S // TM, 2048 // 512, 1),
        a_spec=pl.BlockSpec((TM, D), lambda i, j, k: (i, 0)),
        b_spec=pl.BlockSpec((512, D), lambda i, j, k: (j, 0)),
        o_spec=pl.BlockSpec((TM, 512), lambda i, j, k: (i, j)),
        out_shape=(S, 2048), out_dtype=F32, acc_shape=(TM, 512), name=name, dep=dep)


def _mm_out_dw(yc, dy, name):
    return _matmul(
        yc, dy, dn=TN, grid=(2048 // TM, D // 512, 1),
        a_spec=pl.BlockSpec((S, TM), lambda i, j, k: (0, i)),
        b_spec=pl.BlockSpec((S, 512), lambda i, j, k: (0, j)),
        o_spec=pl.BlockSpec((TM, 512), lambda i, j, k: (i, j)),
        out_shape=(2048, D), out_dtype=BF16, acc_shape=(TM, 512), name=name)


def _row_spec(w=D):
    return pl.BlockSpec((TR, w), lambda i: (i, 0))


def _vec_spec(w=D):
    return pl.BlockSpec((1, w), lambda i: (0, 0))


def _rms_stats(x):
    r = lax.rsqrt(jnp.mean(x * x, axis=-1, keepdims=True) + EPS)
    return x * r, r


def _rms_bwd(dn, xhat, r, g):
    dxh = dn * g
    return r * (dxh - xhat * jnp.mean(dxh * xhat, axis=-1, keepdims=True))


def _acc_rows(ref, val, i):
    s = jnp.sum(val, axis=0, keepdims=True)

    @pl.when(i == 0)
    def _():
        ref[...] = s

    @pl.when(i > 0)
    def _():
        ref[...] += s


def _pre0_fwd(x, g):
    def body(x_ref, g_ref, h_ref, ht_ref):
        xhat, _ = _rms_stats(x_ref[...])
        h = xhat * g_ref[...]
        h_ref[...] = h.astype(BF16)
        ht_ref[...] = h.T.astype(BF16)

    return pl.pallas_call(
        body, grid=(S // TR,), in_specs=[_row_spec(), _vec_spec()],
        out_specs=[_row_spec(), pl.BlockSpec((D, TR), lambda i: (0, i))],
        out_shape=[jax.ShapeDtypeStruct((S, D), BF16), jax.ShapeDtypeStruct((D, S), BF16)],
        compiler_params=_params(("parallel",)), name="pre0_fwd",
    )(x, g)


def _post0_fwd(ycat, w_out, x, g_post, g_pre1, dep):
    def body(yc_ref, w_ref, x_ref, gp_ref, g1_ref, dep_ref, y_ref, x1_ref, h1_ref):
        y = _dot(yc_ref[...], w_ref[...])
        y_ref[...] = y
        yhat, _ = _rms_stats(y)
        x1 = x_ref[...] + yhat * gp_ref[...]
        x1_ref[...] = x1
        xhat, _ = _rms_stats(x1)
        h1_ref[...] = (xhat * g1_ref[...]).astype(BF16)

    return pl.pallas_call(
        body, grid=(S // TR,),
        in_specs=[_row_spec(), pl.BlockSpec((2048, D), lambda i: (0, 0)), _row_spec(), _vec_spec(), _vec_spec(),
                  ANY_SPEC],
        out_specs=[_row_spec(), _row_spec(), _row_spec()],
        out_shape=[jax.ShapeDtypeStruct((S, D), F32), jax.ShapeDtypeStruct((S, D), F32),
                   jax.ShapeDtypeStruct((S, D), BF16)],
        compiler_params=_params(("parallel",), VMEM_BIG), name="post0_fwd",
    )(ycat, w_out, x, g_post, g_pre1, dep)


def _post1_bwd(ycat, w_out, x1, target, g_post, dep):
    def body(yc_ref, w_ref, x1_ref, t_ref, g_ref, dep_ref, loss_ref, dx2_ref, dy_ref, dg_ref):
        i = pl.program_id(0)
        yhat, r = _rms_stats(_dot(yc_ref[...], w_ref[...]))
        g = g_ref[...]
        err = x1_ref[...] + yhat * g - t_ref[...]
        part = jnp.sum(jnp.sum(err * err, axis=-1, keepdims=True), axis=0, keepdims=True) * (0.5 / D)
        _acc_rows(loss_ref, jnp.broadcast_to(part, (1, 128)), i)
        dx2 = err * (1.0 / D)
        dx2_ref[...] = dx2
        _acc_rows(dg_ref, dx2 * yhat, i)
        dy_ref[...] = _rms_bwd(dx2, yhat, r, g).astype(BF16)

    return pl.pallas_call(
        body, grid=(S // TR,),
        in_specs=[_row_spec(), pl.BlockSpec((2048, D), lambda i: (0, 0)), _row_spec(), _row_spec(), _vec_spec(),
                  ANY_SPEC],
        out_specs=[_vec_spec(128), _row_spec(), _row_spec(), _vec_spec()],
        out_shape=[jax.ShapeDtypeStruct((1, 128), F32), jax.ShapeDtypeStruct((S, D), F32),
                   jax.ShapeDtypeStruct((S, D), BF16), jax.ShapeDtypeStruct((1, D), F32)],
        compiler_params=_params(("arbitrary",), VMEM_BIG), name="post1_bwd",
    )(ycat, w_out, x1, target, g_post, dep)


def _mid_bwd(dx2, dh1, x1, y0, g_pre1, g_post0):
    def body(dx2_ref, dh_ref, x1_ref, y_ref, g1_ref, gp_ref, dx1_ref, dy_ref, dg1_ref, dgp_ref):
        i = pl.program_id(0)
        xhat, r1 = _rms_stats(x1_ref[...])
        dh = dh_ref[...]
        _acc_rows(dg1_ref, dh * xhat, i)
        dx1 = dx2_ref[...] + _rms_bwd(dh, xhat, r1, g1_ref[...])
        dx1_ref[...] = dx1
        yhat, r0 = _rms_stats(y_ref[...])
        _acc_rows(dgp_ref, dx1 * yhat, i)
        dy_ref[...] = _rms_bwd(dx1, yhat, r0, gp_ref[...]).astype(BF16)

    return pl.pallas_call(
        body, grid=(S // TR,),
        in_specs=[_row_spec(), _row_spec(), _row_spec(), _row_spec(), _vec_spec(), _vec_spec()],
        out_specs=[_row_spec(), _row_spec(), _vec_spec(), _vec_spec()],
        out_shape=[jax.ShapeDtypeStruct((S, D), F32), jax.ShapeDtypeStruct((S, D), BF16),
                   jax.ShapeDtypeStruct((1, D), F32), jax.ShapeDtypeStruct((1, D), F32)],
        compiler_params=_params(("arbitrary",)), name="mid_bwd",
    )(dx2, dh1, x1, y0, g_pre1, g_post0)


def _pre0_bwd(dx1, dh0, x, g):
    def body(dx1_ref, dh_ref, x_ref, g_ref, gx_ref, dg_ref):
        i = pl.program_id(0)
        xhat, r = _rms_stats(x_ref[...])
        dh = dh_ref[...]
        _acc_rows(dg_ref, dh * xhat, i)
        gx_ref[...] = dx1_ref[...] + _rms_bwd(dh, xhat, r, g_ref[...])

    return pl.pallas_call(
        body, grid=(S // TR,), in_specs=[_row_spec(), _row_spec(), _row_spec(), _vec_spec()],
        out_specs=[_row_spec(), _vec_spec()],
        out_shape=[jax.ShapeDtypeStruct((S, D), F32), jax.ShapeDtypeStruct((1, D), F32)],
        compiler_params=_params(("arbitrary",)), name="pre0_bwd",
    )(dx1, dh0, x, g)


POOL_CH = 256


def _pool_apply(a, w, transpose):
    n = a.shape[0]
    row = lax.broadcasted_iota(jnp.int32, a.shape, 0)
    cnt = jnp.minimum(row + 1, w).astype(F32)
    s = a / cnt if transpose else a
    for k in (1, 2, 4, 8):
        if transpose:
            sh = jnp.where(row < n - k, pltpu.roll(s, n - k, 0), 0.0)
        else:
            sh = jnp.where(row >= k, pltpu.roll(s, k, 0), 0.0)
        s = jnp.where(w > k, s + sh, s)
    return s - a if transpose else s / cnt - a


def _pool_fwd(z0, pool_w, pool_scale):
    def body(a_ref, gate_ref, w_ref, sc_ref, out_ref):
        win = jnp.left_shift(2, pl.program_id(0))
        pooled = _pool_apply(a_ref[...], win, False)
        mixed = _dot(pooled.astype(BF16), w_ref[...])
        gate = gate_ref[...]
        out_ref[...] = (mixed * sc_ref[...] * (gate * _sigmoid(gate))).astype(BF16)

    return pl.pallas_call(
        body, grid=(4,),
        in_specs=[pl.BlockSpec((S, POOL_CH), lambda g: (0, g)), pl.BlockSpec((S, POOL_CH), lambda g: (0, 4 + g)),
                  pl.BlockSpec((None, POOL_CH, POOL_CH), lambda g: (g, 0, 0)),
                  pl.BlockSpec((1, POOL_CH), lambda g: (0, g))],
        out_specs=pl.BlockSpec((S, POOL_CH), lambda g: (0, g)),
        out_shape=jax.ShapeDtypeStruct((S, 2048), BF16),
        compiler_params=_params(("parallel",), VMEM_BIG), name="pool_fwd",
    )(z0, z0, pool_w, pool_scale)


def _pool_bwd(z0, dycat, pool_w, pool_scale):
    def body(a_ref, gate_ref, dy_ref, w_ref, sc_ref, da_ref, dgate_ref, dw_ref, dsc_ref):
        win = jnp.left_shift(2, pl.program_id(0))
        pooled = _pool_apply(a_ref[...], win, False).astype(BF16)
        w = w_ref[...]
        mixed = _dot(pooled, w)
        silu, dsilu = _silu_and_grad(gate_ref[...])
        dy = dy_ref[...]
        sc = sc_ref[...]
        dgate_ref[...] = (dy * (mixed * sc) * dsilu).astype(BF16)
        dms = dy * silu
        dsc_ref[...] = jnp.sum(dms * mixed, axis=0, keepdims=True)
        dmixed = (dms * sc).astype(BF16)
        dw_ref[...] = _dot(pooled, dmixed, TN)
        dpooled = _dot(dmixed, w, NT)
        da_ref[...] = _pool_apply(dpooled, win, True).astype(BF16)

    slab = lambda off: pl.BlockSpec((S, POOL_CH), lambda g: (0, off + g))
    return pl.pallas_call(
        body, grid=(4,),
        in_specs=[slab(0), slab(4), slab(0), pl.BlockSpec((None, POOL_CH, POOL_CH), lambda g: (g, 0, 0)),
                  pl.BlockSpec((1, POOL_CH), lambda g: (0, g))],
        out_specs=[slab(0), slab(0), pl.BlockSpec((None, POOL_CH, POOL_CH), lambda g: (g, 0, 0)),
                   pl.BlockSpec((1, POOL_CH), lambda g: (0, g))],
        out_shape=[jax.ShapeDtypeStruct((S, HALF), BF16), jax.ShapeDtypeStruct((S, HALF), BF16),
                   jax.ShapeDtypeStruct((4, POOL_CH, POOL_CH), F32), jax.ShapeDtypeStruct((1, HALF), F32)],
        compiler_params=_params(("parallel",), VMEM_BIG), name="pool_bwd",
    )(z0, z0, dycat, pool_w, pool_scale)


Q_COL, K_COL, V_COL, BG_COL = 2048 // 128, 5120 // 128, 8192 // 128, 11264 // 128
SCALE = HEAD_DIM ** -0.5


def _rope_tables():
    pos = jnp.arange(S, dtype=F32)
    inv_freq = jnp.power(ROPE_THETA, -jnp.arange(0, ROT_DIM, 2, dtype=F32) / ROT_DIM)
    ang = pos[:, None] * inv_freq[None, :]
    cos, sin = jnp.cos(ang), jnp.sin(ang)
    half = ROT_DIM // 2
    zeros = jnp.zeros((S, HEAD_DIM - ROT_DIM), F32)
    c = jnp.concatenate([cos, cos, jnp.ones((S, HEAD_DIM - ROT_DIM), F32)], axis=1)
    a = jnp.concatenate([-sin, jnp.zeros((S, half), F32), zeros], axis=1)
    b = jnp.concatenate([jnp.zeros((S, half), F32), sin, zeros], axis=1)
    return c, a, b


def _rope(t, c, a, b):
    half = ROT_DIM // 2
    return t * c + pltpu.roll(t, HEAD_DIM - half, 1) * a + pltpu.roll(t, half, 1) * b


def _rope_t(d, c, a, b):
    half = ROT_DIM // 2
    return d * c + pltpu.roll(d * a, half, 1) + pltpu.roll(d * b, HEAD_DIM - half, 1)


def _deinterleave(dst, src, dil, cast=None, dst_off=0):
    length = S // dil
    for r in range(dil):
        v = src[...] if dil == 1 else src[pl.ds(r, length, stride=dil), :]
        dst[dst_off + r * length:dst_off + (r + 1) * length, :] = v if cast is None else v.astype(cast)


def _interleave(dst, src, dil, src_off=0):
    length = S // dil
    for r in range(dil):
        if dil == 1:
            dst[...] = src[src_off:src_off + S, :]
        else:
            dst[pl.ds(r, length, stride=dil), :] = src[src_off + r * length:src_off + (r + 1) * length, :]


CU = 8
NUNITS = S // BLK
B_QK = (((2,), (2,)), ((0,), (0,)))
B_PV = (((2,), (1,)), ((0,), (0,)))
B_TN = (((1,), (1,)), ((0,), (0,)))


def _blocks(ref, first):
    return ref[first * BLK:(first + CU) * BLK, :].reshape(CU, BLK, HEAD_DIM)


def _chunk_scores(u0, nb, qd, kdp):
    q = _blocks(qd, u0)
    row = lax.broadcasted_iota(jnp.int32, (CU, BLK, BLK), 1)
    col = lax.broadcasted_iota(jnp.int32, (CU, BLK, BLK), 2)
    s_own = jnp.where(col <= row, _dot(q, _blocks(kdp, u0 + 1), B_QK) * SCALE, NEG)
    if nb == 1:
        return q, s_own, None
    unit = lax.broadcasted_iota(jnp.int32, (CU, BLK, BLK), 0) + u0
    s_prev = jnp.where((col >= row) & ((unit % nb) != 0), _dot(q, _blocks(kdp, u0), B_QK) * SCALE, NEG)
    return q, s_own, s_prev


def _qkv_prep(z0, tabs):
    def body(q_ref, k_ref, v_ref, c_ref, a_ref, b_ref, qo_ref, ko_ref, vo_ref, tmp):
        p = pl.program_id(1)
        ko_ref[0:BLK, :] = jnp.zeros((BLK, HEAD_DIM), BF16)
        vo_ref[0:BLK, :] = jnp.zeros((BLK, HEAD_DIM), BF16)
        for gi, (_, dil) in enumerate(PATTERNS):
            @pl.when(p == gi)
            def _(dil=dil):
                c, a, b = c_ref[...], a_ref[...], b_ref[...]
                tmp[...] = _rope(q_ref[...], c, a, b)
                _deinterleave(qo_ref, tmp, dil, BF16)
                tmp[...] = _rope(k_ref[...], c, a, b)
                _deinterleave(ko_ref, tmp, dil, BF16, BLK)
                _deinterleave(vo_ref, v_ref, dil, BF16, BLK)

    tab = pl.BlockSpec((S, HEAD_DIM), lambda h, p: (0, 0))
    out = pl.BlockSpec((S, HEAD_DIM), lambda h, p: (0, p * 8 + h))
    outp = pl.BlockSpec((S + BLK, HEAD_DIM), lambda h, p: (0, p * 8 + h))
    return pl.pallas_call(
        body, grid=(8, 3), in_specs=[_head_spec(Q_COL), _head_spec(K_COL), _head_spec(V_COL), tab, tab, tab],
        out_specs=[out, outp, outp],
        out_shape=[jax.ShapeDtypeStruct((S, 3072), BF16)] + [jax.ShapeDtypeStruct((S + BLK, 3072), BF16)] * 2,
        scratch_shapes=[pltpu.VMEM((S, HEAD_DIM), F32)],
        compiler_params=_params(("parallel", "arbitrary"), VMEM_BIG), name="qkv_prep",
    )(z0, z0, z0, *tabs)


def _attn_group_fwd(dil, qd, kdp, vdp, od, ld, og, lg):
    nb = S // dil // BLK
    for u0 in range(0, NUNITS, CU):
        _, s_own, s_prev = _chunk_scores(u0, nb, qd, kdp)
        m = jnp.max(s_own, axis=2, keepdims=True)
        if s_prev is not None:
            m = jnp.maximum(m, jnp.max(s_prev, axis=2, keepdims=True))
        p_own = jnp.exp(s_own - m)
        den = jnp.sum(p_own, axis=2, keepdims=True)
        acc = _dot(p_own.astype(BF16), _blocks(vdp, u0 + 1), B_PV)
        if s_prev is not None:
            p_prev = jnp.exp(s_prev - m)
            den = den + jnp.sum(p_prev, axis=2, keepdims=True)
            acc = acc + _dot(p_prev.astype(BF16), _blocks(vdp, u0), B_PV)
        rows = slice(u0 * BLK, (u0 + CU) * BLK)
        od[rows, :] = (acc / den).reshape(CU * BLK, HEAD_DIM)
        ld[rows, :] = jnp.broadcast_to(m + jnp.log(den), (CU, BLK, HEAD_DIM)).reshape(CU * BLK, HEAD_DIM)
    _interleave(og, od, dil)
    _interleave(lg, ld, dil)


def _group_weights(lgs):
    l0, l1, l2 = lgs[0][...], lgs[1][...], lgs[2][...]
    mx = jnp.maximum(l0, jnp.maximum(l1, l2))
    e0, e1, e2 = jnp.exp(l0 - mx), jnp.exp(l1 - mx), jnp.exp(l2 - mx)
    den = e0 + e1 + e2
    return e0 / den, e1 / den, e2 / den


def _head_spec(base):
    return pl.BlockSpec((S, HEAD_DIM), lambda h, p: (0, base + (p % 3) * 8 + h))


def _slab(dtype=F32, rows=S):
    return pltpu.VMEM((rows, HEAD_DIM), dtype)


def _attn_fwd(z0, qkv, ycat):
    def body(q_ref, k_ref, v_ref, gate_ref, ycat_ref, out_ref, og_ref, lg_ref,
             od, ld, og0, og1, og2, lg0, lg1, lg2):
        del ycat_ref
        p = pl.program_id(1)
        ogs, lgs = (og0, og1, og2), (lg0, lg1, lg2)
        for gi, (_, dil) in enumerate(PATTERNS):
            @pl.when(p == gi)
            def _(gi=gi, dil=dil):
                _attn_group_fwd(dil, q_ref, k_ref, v_ref, od, ld, ogs[gi], lgs[gi])
                og_ref[...] = ogs[gi][...]
                lg_ref[...] = lgs[gi][...]

        @pl.when(p == 2)
        def _():
            w0, w1, w2 = _group_weights(lgs)
            o = w0 * og0[...] + w1 * og1[...] + w2 * og2[...]
            gate = gate_ref[...]
            out_ref[...] = (o * (gate * _sigmoid(gate))).astype(BF16)

    grp = pl.BlockSpec((S, HEAD_DIM), lambda h, p: (0, p * 8 + h))
    grp_pad = pl.BlockSpec((S + BLK, HEAD_DIM), lambda h, p: (0, p * 8 + h))
    return pl.pallas_call(
        body, grid=(8, 3),
        in_specs=[grp, grp_pad, grp_pad, pl.BlockSpec((S, HEAD_DIM), lambda h, p: (0, BG_COL + h)), ANY_SPEC],
        out_specs=[pl.BlockSpec((S, HEAD_DIM), lambda h, p: (0, 8 + h)), grp, grp],
        out_shape=[jax.ShapeDtypeStruct((S, 2048), BF16), jax.ShapeDtypeStruct((S, 3072), F32),
                   jax.ShapeDtypeStruct((S, 3072), F32)],
        scratch_shapes=[_slab() for _ in range(8)],
        input_output_aliases={4: 0},
        compiler_params=_params(("parallel", "arbitrary"), VMEM_BIG), name="attn_fwd",
    )(*qkv, z0, ycat)


def _attn_bwd(z0, qkv, og, lg, dycat, tabs, dep):
    def body(q_ref, k_ref, v_ref, gate_ref, dy_ref, c_ref, a_ref, b_ref,
             og0_ref, og1_ref, og2_ref, lg0_ref, lg1_ref, lg2_ref, dep_ref,
             dq_ref, dk_ref, dv_ref, dbg_ref,
             tmp, ld, dg0, dg1, dg2, cg0, cg1, cg2, dod, cd, dqd, dkd, dvd):
        kd, vd = k_ref, v_ref
        p = pl.program_id(1)
        ogs, lgs, dgs, cgs = (og0_ref, og1_ref, og2_ref), (lg0_ref, lg1_ref, lg2_ref), (dg0, dg1, dg2), (cg0, cg1, cg2)

        @pl.when(p == 0)
        def _():
            w = _group_weights(lgs)
            o = w[0] * ogs[0][...] + w[1] * ogs[1][...] + w[2] * ogs[2][...]
            silu, dsilu = _silu_and_grad(gate_ref[...])
            dy = dy_ref[...]
            dbg_ref[...] = (dy * o * dsilu).astype(BF16)
            do = dy * silu
            dwbar = jnp.sum(do * o, axis=1, keepdims=True)
            for gi in range(3):
                dgs[gi][...] = w[gi] * do
                cgs[gi][...] = -w[gi] * dwbar

        for gi, (_, dil) in enumerate(PATTERNS):
            @pl.when(p == 1 + gi)
            def _(gi=gi, dil=dil):
                nb = S // dil // BLK
                qd = q_ref
                c, a, b = c_ref[...], a_ref[...], b_ref[...]
                _deinterleave(dod, dgs[gi], dil, BF16)
                _deinterleave(ld, lgs[gi], dil)
                _deinterleave(cd, cgs[gi], dil)
                dkd[...] = jnp.zeros_like(dkd)
                dvd[...] = jnp.zeros_like(dvd)
                flat = lambda t: t.reshape(CU * BLK, HEAD_DIM)
                for u0 in range(0, NUNITS, CU):
                    q, s_own, s_prev = _chunk_scores(u0, nb, qd, kd)
                    lse, cv, do = _blocks(ld, u0), _blocks(cd, u0), _blocks(dod, u0)
                    own = slice((u0 + 1) * BLK, (u0 + 1 + CU) * BLK)
                    p_own = jnp.exp(s_own - lse)
                    ds_own = (p_own * (_dot(do, _blocks(vd, u0 + 1), B_QK) + cv) * SCALE).astype(BF16)
                    dq = _dot(ds_own, _blocks(kd, u0 + 1), B_PV)
                    dkd[own, :] += flat(_dot(ds_own, q, B_TN))
                    dvd[own, :] += flat(_dot(p_own.astype(BF16), do, B_TN))
                    if s_prev is not None:
                        prev = slice(u0 * BLK, (u0 + CU) * BLK)
                        p_prev = jnp.exp(s_prev - lse)
                        ds_prev = (p_prev * (_dot(do, _blocks(vd, u0), B_QK) + cv) * SCALE).astype(BF16)
                        dq = dq + _dot(ds_prev, _blocks(kd, u0), B_PV)
                        dkd[prev, :] += flat(_dot(ds_prev, q, B_TN))
                        dvd[prev, :] += flat(_dot(p_prev.astype(BF16), do, B_TN))
                    dqd[u0 * BLK:(u0 + CU) * BLK, :] = flat(dq)
                _interleave(tmp, dqd, dil)
                dq_ref[...] = _rope_t(tmp[...], c, a, b).astype(BF16)
                _interleave(tmp, dkd, dil, BLK)
                dk_ref[...] = _rope_t(tmp[...], c, a, b).astype(BF16)
                _interleave(tmp, dvd, dil, BLK)
                dv_ref[...] = tmp[...].astype(BF16)

    tab = pl.BlockSpec((S, HEAD_DIM), lambda h, p: (0, 0))
    hspec = lambda base: pl.BlockSpec((S, HEAD_DIM), lambda h, p: (0, base + h))
    gspec = pl.BlockSpec((S, HEAD_DIM), lambda h, p: (0, jnp.maximum(p - 1, 0) * 8 + h))
    gspec_pad = pl.BlockSpec((S + BLK, HEAD_DIM), lambda h, p: (0, jnp.maximum(p - 1, 0) * 8 + h))
    return pl.pallas_call(
        body, grid=(8, 4),
        in_specs=[gspec, gspec_pad, gspec_pad, hspec(BG_COL), hspec(8), tab, tab, tab,
                  hspec(0), hspec(8), hspec(16), hspec(0), hspec(8), hspec(16), ANY_SPEC],
        out_specs=[gspec, gspec, gspec, hspec(0)],
        out_shape=[jax.ShapeDtypeStruct((S, 3072), BF16)] * 3 + [jax.ShapeDtypeStruct((S, HALF), BF16)],
        scratch_shapes=[_slab(), _slab()] + [_slab() for _ in range(6)]
                       + [_slab(BF16), _slab(), _slab(), _slab(F32, S + BLK), _slab(F32, S + BLK)],
        compiler_params=_params(("parallel", "arbitrary"), VMEM_BIG), name="attn_bwd",
    )(*qkv, z0, dycat, *tabs, og, og, og, lg, lg, lg, dep)


SGU_CH = 256
NCHUNK = TR // 128


def _ln_stats(x):
    mu = jnp.mean(x, axis=-1, keepdims=True)
    xc = x - mu
    r = lax.rsqrt(jnp.mean(xc * xc, axis=-1, keepdims=True) + EPS)
    return xc * r, r


def _ln_bwd(dy, xhat, r, g):
    dxh = dy * g
    return r * (dxh - jnp.mean(dxh, axis=-1, keepdims=True) - xhat * jnp.mean(dxh * xhat, axis=-1, keepdims=True))


def _tril_bf16(w):
    row = lax.broadcasted_iota(jnp.int32, w.shape, 0)
    col = lax.broadcasted_iota(jnp.int32, w.shape, 1)
    return jnp.where(row >= col, w, 0.0).astype(BF16)


def _sgu_gate(vn_s, s_s, w_ref, bb_ref):
    for h in range(4):
        wm = _tril_bf16(w_ref[h])
        bias = bb_ref[h]
        for ch in range(NCHUNK):
            rows, cols = slice(ch * 128, (ch + 1) * 128), slice(h * SGU_CH, (h + 1) * SGU_CH)
            s_s[rows, cols] = _dot(wm, vn_s[rows, cols]) + jnp.concatenate([bias, bias], axis=1)


WIN = HALO + TR
SUBL = 8


def _shifted_copies(dst, src):
    dst[0] = src[...]
    for b in range(1, SUBL):
        dst[b, 0:WIN - SUBL, :] = src[pl.ds(b, WIN - SUBL), :]


def _rows_at(copies, off, n):
    return copies[off % SUBL, pl.ds(off - off % SUBL, n), :]


def _conv_fwd(i, dval_ref, dglu_ref, hval_ref, hglu_ref, cw_ref, cb_ref, xw, xr, dcs):
    halo = hval_ref[...] * _sigmoid(hglu_ref[...])
    xw[0:HALO, :] = jnp.where(i > 0, halo, 0.0)
    xw[HALO:HALO + TR, :] = dval_ref[...] * _sigmoid(dglu_ref[...])
    _shifted_copies(xr, xw)
    sub = 2 * SUB
    for rb in range(TR // sub):
        acc = jnp.broadcast_to(cb_ref[...], (sub, HALF))
        for k in range(CONV_K):
            acc = acc + cw_ref[k:k + 1, :] * _rows_at(xr, rb * sub + HALO - (CONV_K - 1) + k, sub)
        dcs[rb * sub:(rb + 1) * sub, :] = acc


def _odd_in_specs():
    col = lambda j: pl.BlockSpec((TR, HALF), lambda i, *_: (i, j))
    prev = lambda j: pl.BlockSpec((HALO, HALF), lambda i, *_: (jnp.maximum(i * (TR // HALO) - 1, 0), j))
    return [col(0), col(1), col(2), col(3), col(4), col(5), prev(3), prev(4)]


def _full_spec(shape):
    return pl.BlockSpec(shape, lambda i, *_: (0,) * len(shape))


def _odd_fwd(z1, sgu_g, sgu_b, sgu_w, sgu_bb, conv_w, conv_b, cn_g, cn_b):
    def body(u_ref, v_ref, cg_ref, dval_ref, dglu_ref, dgate_ref, hval_ref, hglu_ref,
             g_ref, b_ref, w_ref, bb_ref, cw_ref, cb_ref, cng_ref, cnb_ref, out_ref, dcs, vn_s, s_s, xw, xr):
        i = pl.program_id(0)
        vhat, _ = _ln_stats(v_ref[...])
        vn_s[...] = (vhat * g_ref[...] + b_ref[...]).astype(BF16)
        _sgu_gate(vn_s, s_s, w_ref, bb_ref)
        cg = cg_ref[...]
        out_ref[:, 0:HALF] = (u_ref[...] * s_s[...] * (cg * _sigmoid(cg))).astype(BF16)
        _conv_fwd(i, dval_ref, dglu_ref, hval_ref, hglu_ref, cw_ref, cb_ref, xw, xr, dcs)
        dhat, _ = _ln_stats(dcs[...])
        dn = dhat * cng_ref[...] + cnb_ref[...]
        dgate = dgate_ref[...]
        out_ref[:, HALF:2 * HALF] = ((dn * _sigmoid(dn)) * (dgate * _sigmoid(dgate))).astype(BF16)

    vec = _full_spec((1, HALF))
    return pl.pallas_call(
        body, grid=(S // TR,),
        in_specs=_odd_in_specs() + [vec, vec, _full_spec((4, 128, 128)), _full_spec((4, 128, 128)),
                                    _full_spec((HALO, HALF)), vec, vec, vec],
        out_specs=[pl.BlockSpec((TR, 2048), lambda i: (i, 0)), pl.BlockSpec((TR, HALF), lambda i: (i, 0))],
        out_shape=[jax.ShapeDtypeStruct((S, 2048), BF16), jax.ShapeDtypeStruct((S, HALF), F32)],
        scratch_shapes=[pltpu.VMEM((TR, HALF), BF16), pltpu.VMEM((TR, HALF), F32),
                        pltpu.VMEM((WIN, HALF), F32), pltpu.VMEM((SUBL, WIN, HALF), F32)],
        compiler_params=_params(("parallel",), VMEM_BIG), name="odd_fwd",
    )(z1, z1, z1, z1, z1, z1, z1, z1, sgu_g, sgu_b, sgu_w, sgu_bb, conv_w, conv_b, cn_g, cn_b)


def _odd_bwd_a(z1, dc, dycat, sgu_g, sgu_b, sgu_w, sgu_bb, cn_g, cn_b):
    def body(u_ref, v_ref, cg_ref, dgate_ref, dcs, dy_ref, g_ref, b_ref, w_ref, bb_ref, cng_ref, cnb_ref,
             dz_ref, ddc_ref, dw_ref, dbb_ref, dg_ref, db_ref, dcng_ref, dcnb_ref, dcb_ref,
             vn_s, s_s, ds_s, dvn_s):
        i = pl.program_id(0)
        vhat, rv = _ln_stats(v_ref[...])
        g = g_ref[...]
        vn_s[...] = (vhat * g + b_ref[...]).astype(BF16)
        _sgu_gate(vn_s, s_s, w_ref, bb_ref)
        silu_c, dsilu_c = _silu_and_grad(cg_ref[...])
        dyc = dy_ref[:, 0:HALF]
        u = u_ref[...]
        s = s_s[...]
        dz_ref[:, 0:HALF] = (dyc * s * silu_c).astype(BF16)
        dz_ref[:, 2 * HALF:3 * HALF] = (dyc * u * s * dsilu_c).astype(BF16)
        ds_s[...] = dyc * u * silu_c

        @pl.when(i == 0)
        def _():
            dw_ref[...] = jnp.zeros_like(dw_ref)
            dbb_ref[...] = jnp.zeros_like(dbb_ref)

        tril = lax.broadcasted_iota(jnp.int32, (128, 128), 0) >= lax.broadcasted_iota(jnp.int32, (128, 128), 1)
        for h in range(4):
            wm = _tril_bf16(w_ref[h])
            for ch in range(NCHUNK):
                rows, cols = slice(ch * 128, (ch + 1) * 128), slice(h * SGU_CH, (h + 1) * SGU_CH)
                ds = ds_s[rows, cols]
                dsb = ds.astype(BF16)
                dw_ref[h] += jnp.where(tril, _dot(dsb, vn_s[rows, cols], NT), 0.0)
                dbb_ref[h] += jnp.broadcast_to(jnp.sum(ds, axis=1, keepdims=True), (128, 128))
                dvn_s[rows, cols] = _dot(wm, dsb, TN)
        dvn = dvn_s[...]
        _acc_rows(dg_ref, dvn * vhat, i)
        _acc_rows(db_ref, dvn, i)
        dz_ref[:, HALF:2 * HALF] = _ln_bwd(dvn, vhat, rv, g).astype(BF16)

        dhat, rd = _ln_stats(dcs[...])
        cng = cng_ref[...]
        silu_n, dsilu_n = _silu_and_grad(dhat * cng + cnb_ref[...])
        silu_g, dsilu_g = _silu_and_grad(dgate_ref[...])
        dyd = dy_ref[:, HALF:2 * HALF]
        dz_ref[:, 5 * HALF:6 * HALF] = (dyd * silu_n * dsilu_g).astype(BF16)
        ddn = dyd * silu_g * dsilu_n
        _acc_rows(dcng_ref, ddn * dhat, i)
        _acc_rows(dcnb_ref, ddn, i)
        ddc = _ln_bwd(ddn, dhat, rd, cng)
        ddc_ref[...] = ddc
        _acc_rows(dcb_ref, ddc, i)

    vec = _full_spec((1, HALF))
    sq = _full_spec((4, 128, 128))
    col = lambda j: pl.BlockSpec((TR, HALF), lambda i: (i, j))
    return pl.pallas_call(
        body, grid=(S // TR,),
        in_specs=[col(0), col(1), col(2), col(5), col(0), pl.BlockSpec((TR, 2048), lambda i: (i, 0)),
                  vec, vec, sq, sq, vec, vec],
        out_specs=[pl.BlockSpec((TR, ODD_IN), lambda i: (i, 0)), pl.BlockSpec((TR, HALF), lambda i: (i, 0)),
                   sq, sq, vec, vec, vec, vec, vec],
        out_shape=[jax.ShapeDtypeStruct((S, ODD_IN), BF16), jax.ShapeDtypeStruct((S, HALF), F32),
                   jax.ShapeDtypeStruct((4, 128, 128), F32), jax.ShapeDtypeStruct((4, 128, 128), F32)]
                  + [jax.ShapeDtypeStruct((1, HALF), F32)] * 5,
        scratch_shapes=[pltpu.VMEM((TR, HALF), BF16), pltpu.VMEM((TR, HALF), F32),
                        pltpu.VMEM((TR, HALF), F32), pltpu.VMEM((TR, HALF), F32)],
        compiler_params=_params(("arbitrary",), VMEM_BIG), name="odd_bwd_a",
    )(z1, z1, z1, z1, dc, dycat, sgu_g, sgu_b, sgu_w, sgu_bb, cn_g, cn_b)


def _odd_bwd_b(z1, ddc, dz1, conv_w):
    nt = S // TR

    def body(dval_ref, dglu_ref, hval_ref, hglu_ref, ddc_ref, hddc_ref, cw_ref, dz_in_ref,
             dz_ref, dcw_ref, xw, dwin, dxs, xr, dr):
        del dz_in_ref
        i, j = pl.program_id(0), pl.program_id(1)
        sg = _sigmoid(dglu_ref[...])
        dval = dval_ref[...]

        @pl.when(j == 0)
        def _():
            halo = hval_ref[...] * _sigmoid(hglu_ref[...])
            xw[0:HALO, :] = jnp.where(i > 0, halo, 0.0)
            xw[HALO:HALO + TR, :] = dval * sg
            dwin[0:TR, :] = ddc_ref[...]
            dwin[TR:TR + HALO, :] = jnp.where(i < nt - 1, hddc_ref[...], 0.0)
            _shifted_copies(xr, xw)
            _shifted_copies(dr, dwin)

            @pl.when(i == 0)
            def _():
                dcw_ref[...] = jnp.zeros_like(dcw_ref)

            for rb in range(TR // SUB):
                acc = jnp.zeros((SUB, HALF), F32)
                for k in range(CONV_K):
                    acc = acc + cw_ref[k:k + 1, :] * _rows_at(dr, rb * SUB + (CONV_K - 1) - k, SUB)
                dxs[rb * SUB:(rb + 1) * SUB, :] = acc
            for k in range(CONV_K):
                acc = jnp.zeros((SUB, HALF), F32)
                for rb in range(TR // SUB):
                    acc = acc + dwin[rb * SUB:(rb + 1) * SUB, :] * _rows_at(xr, rb * SUB + HALO - (CONV_K - 1) + k, SUB)
                dcw_ref[k:k + 1, :] += jnp.sum(acc, axis=0, keepdims=True)
            dz_ref[...] = (dxs[...] * sg).astype(BF16)

        @pl.when(j == 1)
        def _():
            dz_ref[...] = (dxs[...] * dval * sg * (1.0 - sg)).astype(BF16)

    col = lambda c: pl.BlockSpec((TR, HALF), lambda i, j: (i, c))
    prev = lambda c: pl.BlockSpec((HALO, HALF), lambda i, j: (jnp.maximum(i * (TR // HALO) - 1, 0), c))
    nxt = pl.BlockSpec((HALO, HALF), lambda i, j: (jnp.minimum((i + 1) * (TR // HALO), S // HALO - 1), 0))
    return pl.pallas_call(
        body, grid=(nt, 2),
        in_specs=[col(3), col(4), prev(3), prev(4), pl.BlockSpec((TR, HALF), lambda i, j: (i, 0)), nxt,
                  _full_spec((HALO, HALF)), pl.BlockSpec(memory_space=pl.ANY)],
        out_specs=[pl.BlockSpec((TR, HALF), lambda i, j: (i, 3 + j)), _full_spec((HALO, HALF))],
        out_shape=[jax.ShapeDtypeStruct((S, ODD_IN), BF16), jax.ShapeDtypeStruct((HALO, HALF), F32)],
        scratch_shapes=[pltpu.VMEM((WIN, HALF), F32), pltpu.VMEM((WIN, HALF), F32), pltpu.VMEM((TR, HALF), F32),
                        pltpu.VMEM((SUBL, WIN, HALF), F32), pltpu.VMEM((SUBL, WIN, HALF), F32)],
        input_output_aliases={7: 0},
        compiler_params=_params(("arbitrary", "arbitrary"), VMEM_BIG), name="odd_bwd_b",
    )(z1, z1, z1, z1, ddc, ddc, conv_w, dz1)


def _conv_out_grad(dc, dyd, dgate, cng, cnb):
    dhat, rd = _ln_stats(dc)
    silu_n, dsilu_n = _silu_and_grad(dhat * cng + cnb)
    silu_g, dsilu_g = _silu_and_grad(dgate)
    ddn = dyd * silu_g * dsilu_n
    return _ln_bwd(ddn, dhat, rd, cng), ddn, dhat, dyd * silu_n * dsilu_g


def _odd_bwd(z1, dc, dycat, sgu_g, sgu_b, sgu_w, sgu_bb, conv_w, cn_g, cn_b):
    nt = S // TR

    def body(u_ref, v_ref, cg_ref, dval_ref, dglu_ref, dgate_ref, hval_ref, hglu_ref, dcs, dy_ref,
             ndc_ref, ndy_ref, ngate_ref, g_ref, b_ref, w_ref, bb_ref, cw_ref, cng_ref, cnb_ref,
             dz_ref, dw_ref, dbb_ref, dg_ref, db_ref, dcng_ref, dcnb_ref, dcb_ref, dcw_ref,
             vn_s, s_s, ds_s, dvn_s, xw, dwin, dxs, xr, dr):
        i = pl.program_id(0)
        vhat, rv = _ln_stats(v_ref[...])
        g = g_ref[...]
        vn_s[...] = (vhat * g + b_ref[...]).astype(BF16)
        _sgu_gate(vn_s, s_s, w_ref, bb_ref)
        silu_c, dsilu_c = _silu_and_grad(cg_ref[...])
        dyc = dy_ref[:, 0:HALF]
        u = u_ref[...]
        s = s_s[...]
        dz_ref[:, 0:HALF] = (dyc * s * silu_c).astype(BF16)
        dz_ref[:, 2 * HALF:3 * HALF] = (dyc * u * s * dsilu_c).astype(BF16)
        ds_s[...] = dyc * u * silu_c

        @pl.when(i == 0)
        def _():
            dw_ref[...] = jnp.zeros_like(dw_ref)
            dbb_ref[...] = jnp.zeros_like(dbb_ref)
            dcw_ref[...] = jnp.zeros_like(dcw_ref)

        tril = lax.broadcasted_iota(jnp.int32, (128, 128), 0) >= lax.broadcasted_iota(jnp.int32, (128, 128), 1)
        for h in range(4):
            wm = _tril_bf16(w_ref[h])
            for ch in range(NCHUNK):
                rows, cols = slice(ch * 128, (ch + 1) * 128), slice(h * SGU_CH, (h + 1) * SGU_CH)
                ds = ds_s[rows, cols]
                dsb = ds.astype(BF16)
                dw_ref[h] += jnp.where(tril, _dot(dsb, vn_s[rows, cols], NT), 0.0)
                dbb_ref[h] += jnp.broadcast_to(jnp.sum(ds, axis=1, keepdims=True), (128, 128))
                dvn_s[rows, cols] = _dot(wm, dsb, TN)
        dvn = dvn_s[...]
        _acc_rows(dg_ref, dvn * vhat, i)
        _acc_rows(db_ref, dvn, i)
        dz_ref[:, HALF:2 * HALF] = _ln_bwd(dvn, vhat, rv, g).astype(BF16)

        cng, cnb = cng_ref[...], cnb_ref[...]
        ddc, ddn, dhat, ddgate = _conv_out_grad(dcs[...], dy_ref[:, HALF:2 * HALF], dgate_ref[...], cng, cnb)
        dz_ref[:, 5 * HALF:6 * HALF] = ddgate.astype(BF16)
        _acc_rows(dcng_ref, ddn * dhat, i)
        _acc_rows(dcnb_ref, ddn, i)
        _acc_rows(dcb_ref, ddc, i)
        ddc_next = _conv_out_grad(ndc_ref[...], ndy_ref[:, HALF:2 * HALF], ngate_ref[...], cng, cnb)[0]

        sg = _sigmoid(dglu_ref[...])
        dval = dval_ref[...]
        halo = hval_ref[...] * _sigmoid(hglu_ref[...])
        xw[0:HALO, :] = jnp.where(i > 0, halo, 0.0)
        xw[HALO:HALO + TR, :] = dval * sg
        dwin[0:TR, :] = ddc
        dwin[TR:TR + HALO, :] = jnp.where(i < nt - 1, ddc_next, 0.0)
        _shifted_copies(xr, xw)
        _shifted_copies(dr, dwin)
        for rb in range(TR // SUB):
            acc = jnp.zeros((SUB, HALF), F32)
            for k in range(CONV_K):
                acc = acc + cw_ref[k:k + 1, :] * _rows_at(dr, rb * SUB + (CONV_K - 1) - k, SUB)
            dxs[rb * SUB:(rb + 1) * SUB, :] = acc
        for k in range(CONV_K):
            acc = jnp.zeros((SUB, HALF), F32)
            for rb in range(TR // SUB):
                acc = acc + dwin[rb * SUB:(rb + 1) * SUB, :] * _rows_at(xr, rb * SUB + HALO - (CONV_K - 1) + k, SUB)
            dcw_ref[k:k + 1, :] += jnp.sum(acc, axis=0, keepdims=True)
        dx = dxs[...]
        dz_ref[:, 3 * HALF:4 * HALF] = (dx * sg).astype(BF16)
        dz_ref[:, 4 * HALF:5 * HALF] = (dx * dval * sg * (1.0 - sg)).astype(BF16)

    vec = _full_spec((1, HALF))
    sq = _full_spec((4, 128, 128))
    col = lambda j: pl.BlockSpec((TR, HALF), lambda i: (i, j))
    prev = lambda j: pl.BlockSpec((HALO, HALF), lambda i: (jnp.maximum(i * (TR // HALO) - 1, 0), j))
    nxt_row = lambda i: jnp.minimum((i + 1) * (TR // HALO), S // HALO - 1)
    return pl.pallas_call(
        body, grid=(nt,),
        in_specs=[col(0), col(1), col(2), col(3), col(4), col(5), prev(3), prev(4), col(0),
                  pl.BlockSpec((TR, 2048), lambda i: (i, 0)),
                  pl.BlockSpec((HALO, HALF), lambda i: (nxt_row(i), 0)), pl.BlockSpec((HALO, 2048), lambda i: (nxt_row(i), 0)),
                  pl.BlockSpec((HALO, HALF), lambda i: (nxt_row(i), 5)),
                  vec, vec, sq, sq, _full_spec((HALO, HALF)), vec, vec],
        out_specs=[pl.BlockSpec((TR, ODD_IN), lambda i: (i, 0)), sq, sq, vec, vec, vec, vec, vec,
                   _full_spec((HALO, HALF))],
        out_shape=[jax.ShapeDtypeStruct((S, ODD_IN), BF16), jax.ShapeDtypeStruct((4, 128, 128), F32),
                   jax.ShapeDtypeStruct((4, 128, 128), F32)] + [jax.ShapeDtypeStruct((1, HALF), F32)] * 5
                  + [jax.ShapeDtypeStruct((HALO, HALF), F32)],
        scratch_shapes=[pltpu.VMEM((TR, HALF), BF16), pltpu.VMEM((TR, HALF), F32), pltpu.VMEM((TR, HALF), F32),
                        pltpu.VMEM((TR, HALF), F32), pltpu.VMEM((WIN, HALF), F32), pltpu.VMEM((WIN, HALF), F32),
                        pltpu.VMEM((TR, HALF), F32), pltpu.VMEM((SUBL, WIN, HALF), F32),
                        pltpu.VMEM((SUBL, WIN, HALF), F32)],
        compiler_params=_params(("arbitrary",), VMEM_BIG), name="odd_bwd",
    )(z1, z1, z1, z1, z1, z1, z1, z1, dc, dycat, dc, dycat, z1,
      sgu_g, sgu_b, sgu_w, sgu_bb, conv_w, cn_g, cn_b)


def _cast_bf16(w, name, piece=0, npieces=1):
    r, c = w.shape[0], w.shape[1] // npieces
    tr = min(r, 256)

    def body(i_ref, o_ref):
        o_ref[...] = i_ref[...].astype(BF16)

    return pl.pallas_call(
        body, grid=(r // tr,), in_specs=[pl.BlockSpec((tr, c), lambda i: (i, piece))],
        out_specs=pl.BlockSpec((tr, c), lambda i: (i, 0)), out_shape=jax.ShapeDtypeStruct((r, c), BF16),
        compiler_params=_params(("parallel",)), name=name,
    )(w)


def _adamw(w, g, m, v):
    m = ADAM_B1 * m + (1.0 - ADAM_B1) * g
    v = ADAM_B2 * v + (1.0 - ADAM_B2) * (g * g)
    m_hat = m / (1.0 - ADAM_B1 ** ADAM_STEP)
    v_hat = v / (1.0 - ADAM_B2 ** ADAM_STEP)
    delta = -ADAM_LR * (m_hat / (jnp.sqrt(v_hat) + ADAM_EPS) + ADAM_WD * w)
    return delta, m, v


def _adam_reduce(parts, w, m, v, name, dep=None, piece=0, npieces=1, prev=None):
    r, c = w.shape
    cp = c // npieces
    tr = min(r, 128)
    extra = ([] if dep is None else [dep]) + ([] if prev is None else list(prev))
    nparts = parts.shape[0]

    def body(p_ref, w_ref, m_ref, v_ref, *rest):
        g_ref, d_ref, nm_ref, nv_ref = rest[len(extra):]
        g = p_ref[0].astype(F32)
        for d in range(1, nparts):
            g = g + p_ref[d].astype(F32)
        g_ref[...] = g
        d_ref[...], nm_ref[...], nv_ref[...] = _adamw(w_ref[...], g, m_ref[...], v_ref[...])

    spec = pl.BlockSpec((tr, cp), lambda i: (i, piece))
    first = 4 + (0 if dep is None else 1)
    return pl.pallas_call(
        body, grid=(r // tr,),
        in_specs=[pl.BlockSpec((nparts, tr, cp), lambda i: (0, i, 0)), spec, spec, spec] + [ANY_SPEC] * len(extra),
        out_specs=[spec] * 4, out_shape=[jax.ShapeDtypeStruct((r, c), F32)] * 4,
        input_output_aliases={} if prev is None else {first + k: k for k in range(4)},
        compiler_params=_params(("parallel",), VMEM_BIG), name=name,
    )(parts, w, m, v, *extra)


def _arrived(x, name, dep=None):
    deps = [] if dep is None else [dep]

    def body(*refs):
        refs[-1][...] = jnp.zeros_like(refs[-1])

    return pl.pallas_call(
        body, in_specs=[ANY_SPEC] * (1 + len(deps)), out_specs=pl.BlockSpec(memory_space=pltpu.VMEM),
        out_shape=jax.ShapeDtypeStruct((8, 128), F32), name=name,
    )(x, *deps)


def _sum_parts(parts, name, dep=None):
    r = parts.shape[1]
    tr = 8
    for cand in (512, 256, 128, 64, 32, 16, 8):
        if r % cand == 0:
            tr = cand
            break
    deps = [] if dep is None else [dep]

    def body(p_ref, *rest):
        g = p_ref[0]
        for d in range(1, NDEV):
            g = g + p_ref[d]
        rest[-1][...] = g

    return pl.pallas_call(
        body, grid=(r // tr,), in_specs=[pl.BlockSpec((NDEV, tr, 128), lambda i: (0, i, 0))] + [ANY_SPEC] * len(deps),
        out_specs=pl.BlockSpec((tr, 128), lambda i: (i, 0)), out_shape=jax.ShapeDtypeStruct((r, 128), F32),
        compiler_params=_params(("parallel",)), name=name,
    )(parts, *deps)


def _sum_unpack(parts, rows, name, dep=None):
    deps = [] if dep is None else [dep]

    def body(p_ref, *outs):
        outs = outs[len(deps):]
        off = 0
        for o_ref, n in zip(outs, rows):
            acc = p_ref[0, off:off + n, :]
            for d in range(1, NDEV):
                acc = acc + p_ref[d, off:off + n, :]
            o_ref[...] = acc
            off += n

    return pl.pallas_call(
        body, grid=(1,), in_specs=[pl.BlockSpec(parts.shape, lambda i: (0, 0, 0))] + [ANY_SPEC] * len(deps),
        out_specs=[pl.BlockSpec((n, 128), lambda i: (0, 0)) for n in rows],
        out_shape=[jax.ShapeDtypeStruct((n, 128), F32) for n in rows],
        compiler_params=_params(("arbitrary",), VMEM_BIG), name=name,
    )(parts, *deps)


def _adam_small(ws, gs, g_specs, ms, vs, name):
    n = len(ws)

    def body(*refs):
        w_r, g_r, m_r, v_r = refs[:n], refs[n:2 * n], refs[2 * n:3 * n], refs[3 * n:4 * n]
        outs = refs[4 * n:]
        for i in range(n):
            g = g_r[i][...]
            outs[4 * i][...] = g
            outs[4 * i + 1][...], outs[4 * i + 2][...], outs[4 * i + 3][...] = _adamw(
                w_r[i][...], g, m_r[i][...], v_r[i][...])

    whole = lambda a: pl.BlockSpec(a.shape, lambda i, nd=a.ndim: (0,) * nd)
    outs = pl.pallas_call(
        body, grid=(1,),
        in_specs=[whole(a) for a in ws] + list(g_specs) + [whole(a) for a in ms] + [whole(a) for a in vs],
        out_specs=[whole(a) for a in ws for _ in range(4)],
        out_shape=[jax.ShapeDtypeStruct(a.shape, F32) for a in ws for _ in range(4)],
        compiler_params=_params(("arbitrary",), VMEM_BIG), name=name,
    )(*ws, *gs, *ms, *vs)
    return [outs[4 * i:4 * i + 4] for i in range(n)]


MASKS = [(mx, my, mc) for mx in (0, 1) for my in (0, 1) for mc in (0, 1)][1:]


def _sc_exchange(name, collective_id, arrays, scatter):
    nt = len(arrays)
    out_type = [jax.ShapeDtypeStruct(a.shape if scatter else (NDEV,) + a.shape, a.dtype) for a in arrays]

    def body(*refs):
        ins, outs = refs[:nt], refs[nt:2 * nt]
        send_sems, recv_sems, local_sems = refs[2 * nt:3 * nt], refs[3 * nt:4 * nt], refs[4 * nt:5 * nt]
        x, y, c = lax.axis_index("x"), lax.axis_index("y"), lax.axis_index("c")
        peers = [(mx + x - 2 * mx * x, my + y - 2 * my * y, mc + c - 2 * mc * c) for mx, my, mc in MASKS]
        barrier = pltpu.get_barrier_semaphore()
        for peer in peers:
            pl.semaphore_signal(barrier, inc=1, device_id=peer, device_id_type=MESH)
        pl.semaphore_wait(barrier, len(peers))
        me = 4 * x + 2 * y + c
        own = []
        for t in range(nt):
            cp = pltpu.make_async_copy(ins[t].at[me] if scatter else ins[t], outs[t].at[me], local_sems[t])
            cp.start()
            own.append(cp)
            for px, py, pc in peers:
                src = ins[t].at[4 * px + 2 * py + pc] if scatter else ins[t]
                pltpu.make_async_remote_copy(src_ref=src, dst_ref=outs[t].at[me], send_sem=send_sems[t],
                                             recv_sem=recv_sems[t], device_id=(px, py, pc), device_id_type=MESH).start()
        for t in range(nt):
            own[t].wait()
            seven = outs[t].at[pl.ds(0, NDEV - 1)]
            drain = pltpu.make_async_remote_copy(src_ref=seven, dst_ref=seven, send_sem=send_sems[t],
                                                 recv_sem=recv_sems[t], device_id=(x, y, c), device_id_type=MESH)
            drain.wait_send()
            drain.wait_recv()

    return pl.kernel(
        body, out_type=out_type, mesh=plsc.ScalarSubcoreMesh(axis_name="sequencer", num_cores=1),
        scratch_types=[pltpu.SemaphoreType.DMA] * (3 * nt),
        compiler_params=pltpu.CompilerParams(collective_id=collective_id), name=name,
    )(*arrays)


def _sc_gather_two_level(name, collective_id, arrays):
    nt = len(arrays)
    out_type = [jax.ShapeDtypeStruct((NDEV,) + a.shape, a.dtype) for a in arrays]

    def body(*refs):
        ins, outs = refs[:nt], refs[nt:2 * nt]
        sems = refs[2 * nt:]
        send_sems, sib_sems, local_sems = sems[:nt], sems[nt:2 * nt], sems[2 * nt:3 * nt]
        ici_sems = [sems[3 * nt + 3 * t:3 * nt + 3 * t + 3] for t in range(nt)]
        x, y, c = lax.axis_index("x"), lax.axis_index("y"), lax.axis_index("c")
        sibling = (x, y, 1 - c)
        chips = [(1 - x, y), (x, 1 - y), (1 - x, 1 - y)]
        barrier = pltpu.get_barrier_semaphore()
        for peer in [sibling] + [(cx, cy, c) for cx, cy in chips]:
            pl.semaphore_signal(barrier, inc=1, device_id=peer, device_id_type=MESH)
        pl.semaphore_wait(barrier, 4)
        me = 4 * x + 2 * y + c

        def push(t, src, slot, recv_sem, to):
            pltpu.make_async_remote_copy(src_ref=src, dst_ref=outs[t].at[slot], send_sem=send_sems[t],
                                         recv_sem=recv_sem, device_id=to, device_id_type=MESH).start()

        own = []
        for t in range(nt):
            cp = pltpu.make_async_copy(ins[t], outs[t].at[me], local_sems[t])
            cp.start()
            own.append(cp)
            for j, (cx, cy) in enumerate(chips):
                push(t, ins[t], me, ici_sems[t][j], (cx, cy, c))
            push(t, ins[t], me, sib_sems[t], sibling)
        for t in range(nt):
            for j, (cx, cy) in enumerate(chips):
                slot = 4 * cx + 2 * cy + c
                landed = outs[t].at[slot]
                pltpu.make_async_remote_copy(src_ref=landed, dst_ref=landed, send_sem=send_sems[t],
                                             recv_sem=ici_sems[t][j], device_id=(cx, cy, c),
                                             device_id_type=MESH).wait_recv()
                push(t, landed, slot, sib_sems[t], sibling)
        for t in range(nt):
            own[t].wait()
            four, seven = outs[t].at[pl.ds(0, 4)], outs[t].at[pl.ds(0, 7)]
            pltpu.make_async_remote_copy(src_ref=four, dst_ref=four, send_sem=send_sems[t], recv_sem=sib_sems[t],
                                         device_id=sibling, device_id_type=MESH).wait_recv()
            pltpu.make_async_remote_copy(src_ref=seven, dst_ref=seven, send_sem=send_sems[t], recv_sem=sib_sems[t],
                                         device_id=sibling, device_id_type=MESH).wait_send()

    return pl.kernel(
        body, out_type=out_type, mesh=plsc.ScalarSubcoreMesh(axis_name="sequencer", num_cores=1),
        scratch_types=[pltpu.SemaphoreType.DMA] * (6 * nt),
        compiler_params=pltpu.CompilerParams(collective_id=collective_id), name=name,
    )(*arrays)


def _sc_sibling_exchange(name, collective_id, src, out_shape, pieces, after=None):
    extra = [] if after is None else [after]

    def body(src_ref, *rest):
        out_ref, send_sem, recv_sem = rest[len(extra):]
        x, y, c = lax.axis_index("x"), lax.axis_index("y"), lax.axis_index("c")
        sibling = (x, y, 1 - c)
        barrier = pltpu.get_barrier_semaphore()
        pl.semaphore_signal(barrier, inc=1, device_id=sibling, device_id_type=MESH)
        pl.semaphore_wait(barrier, 1)
        for piece, lands in pieces(c, src_ref, out_ref):
            pltpu.make_async_remote_copy(src_ref=piece, dst_ref=lands, send_sem=send_sem, recv_sem=recv_sem,
                                         device_id=sibling, device_id_type=MESH).start()
        drain = pltpu.make_async_remote_copy(src_ref=out_ref, dst_ref=out_ref, send_sem=send_sem, recv_sem=recv_sem,
                                             device_id=sibling, device_id_type=MESH)
        drain.wait_send()
        drain.wait_recv()

    return pl.kernel(
        body, out_type=jax.ShapeDtypeStruct(out_shape, src.dtype),
        mesh=plsc.ScalarSubcoreMesh(axis_name="sequencer", num_cores=1), scratch_types=[pltpu.SemaphoreType.DMA] * 2,
        compiler_params=pltpu.CompilerParams(collective_id=collective_id), name=name,
    )(src, *extra)


def _swap_class_columns(name, collective_id, dz, nb, piece=0, npieces=1):
    w = nb // npieces
    return _sc_sibling_exchange(
        name, collective_id, dz, (S, 4 * w),
        lambda c, src, out: [(src.at[:, pl.ds((2 * j + 1 - c) * nb + piece * w, w)], out.at[:, pl.ds(j * w, w)])
                             for j in range(4)])


def _sc_chip_scatter(name, collective_id, q):
    def body(q_ref, out_ref, send_sem, recv_sem, local_sem):
        x, y, c = lax.axis_index("x"), lax.axis_index("y"), lax.axis_index("c")
        chips = [(1 - x, y), (x, 1 - y), (1 - x, 1 - y)]
        barrier = pltpu.get_barrier_semaphore()
        for cx, cy in chips:
            pl.semaphore_signal(barrier, inc=1, device_id=(cx, cy, c), device_id_type=MESH)
        pl.semaphore_wait(barrier, 3)
        mine = 2 * x + y
        own = pltpu.make_async_copy(q_ref.at[mine], out_ref.at[mine], local_sem)
        own.start()
        for cx, cy in chips:
            pltpu.make_async_remote_copy(src_ref=q_ref.at[2 * cx + cy], dst_ref=out_ref.at[mine], send_sem=send_sem,
                                         recv_sem=recv_sem, device_id=(cx, cy, c), device_id_type=MESH).start()
        own.wait()
        three = out_ref.at[pl.ds(0, 3)]
        drain = pltpu.make_async_remote_copy(src_ref=three, dst_ref=three, send_sem=send_sem, recv_sem=recv_sem,
                                             device_id=(x, y, c), device_id_type=MESH)
        drain.wait_send()
        drain.wait_recv()

    return pl.kernel(
        body, out_type=jax.ShapeDtypeStruct(q.shape, q.dtype),
        mesh=plsc.ScalarSubcoreMesh(axis_name="sequencer", num_cores=1), scratch_types=[pltpu.SemaphoreType.DMA] * 3,
        compiler_params=pltpu.CompilerParams(collective_id=collective_id), name=name,
    )(q)


def _mm_pair_dw(h_own, dz, h_sib, dz_sib, nb, name, dep=None, piece=0, npieces=1, h_transposed=False,
                one_call=False):
    nb = nb // npieces
    tn = 512 if nb % 512 == 0 else nb
    per = nb // tn
    dn = NN if h_transposed else TN
    o_spec = pl.BlockSpec((None, D, tn), lambda i, j, k: (j // per, 0, j % per))
    own_col = lambda i, j, k: (0, ((2 * (j // per) + lax.axis_index("c")) * npieces + piece) * per + j % per)
    if one_call:
        def fused(a0_ref, b0_ref, a1_ref, b1_ref, dep_ref, o_ref):
            acc = _dot(a0_ref[...], b0_ref[...], dn) + _dot(a1_ref[...], b1_ref[...], dn)
            o_ref[...] = acc.astype(BF16)

        whole = pl.BlockSpec((S, D), lambda i, j, k: (0, 0), pipeline_mode=pl.Buffered(1))
        return pl.pallas_call(
            fused, grid=(1, 4 * per, 1),
            in_specs=[whole, pl.BlockSpec((S, tn), own_col), whole, pl.BlockSpec((S, tn), lambda i, j, k: (0, j)),
                      ANY_SPEC],
            out_specs=o_spec, out_shape=jax.ShapeDtypeStruct((4, D, nb), BF16),
            compiler_params=_params(("parallel", "parallel", "arbitrary"), VMEM_BIG), name=name,
        )(h_own, dz, h_sib, dz_sib, dep)
    part = _matmul(
        h_own, dz, dn=dn, grid=(1, 4 * per, 1),
        a_spec=pl.BlockSpec((S, D), lambda i, j, k: (0, 0)), b_spec=pl.BlockSpec((S, tn), own_col),
        o_spec=o_spec, out_shape=(4, D, nb), out_dtype=F32, acc_shape=(D, tn), name=name + "_own", dep=dep)

    def body(a_ref, b_ref, p_ref, o_ref):
        o_ref[...] = (p_ref[...] + _dot(a_ref[...], b_ref[...], dn)).astype(BF16)

    return pl.pallas_call(
        body, grid=(1, 4 * per, 1),
        in_specs=[pl.BlockSpec((S, D), lambda i, j, k: (0, 0)), pl.BlockSpec((S, tn), lambda i, j, k: (0, j)), o_spec],
        out_specs=o_spec, out_shape=jax.ShapeDtypeStruct((4, D, nb), BF16),
        compiler_params=_params(("parallel", "parallel", "arbitrary"), VMEM_BIG), name=name + "_sibling",
    )(h_sib, dz_sib, part)


SMALL = {
    "e_pre_norm": ((2048,), None), "e_pool_w": ((4, 256, 256), 1), "e_pool_scale": ((1024,), None),
    "e_post_norm": ((2048,), None), "o_pre_norm": ((2048,), 0), "o_sgu_norm_g": ((1024,), 0),
    "o_sgu_norm_b": ((1024,), 0), "o_sgu_w": ((4, 128, 128), None), "o_sgu_b": ((4, 128), None),
    "o_conv_w": ((31, 1024), 1), "o_conv_b": ((1024,), 0), "o_conv_norm_g": ((1024,), 0),
    "o_conv_norm_b": ((1024,), 0), "o_post_norm": ((2048,), 0),
}
SMALL_SHARDED = [n for n, (_, ax) in SMALL.items() if ax is not None]


def _shard_shape(name):
    shape, ax = SMALL[name]
    if ax is None:
        return shape
    return tuple(s // NDEV if i == ax else s for i, s in enumerate(shape))


def _pack(arrs, row_multiple=1):
    flat = jnp.concatenate([a.reshape(-1) for a in arrs])
    pad = -flat.shape[0] % (128 * row_multiple)
    return jnp.concatenate([flat, jnp.zeros((pad,), F32)]).reshape(-1, 128)


def _small_views(name):
    shape, ax = SMALL[name]
    me = lambda: 4 * lax.axis_index("x") + 2 * lax.axis_index("y") + lax.axis_index("c")
    if ax is None:
        view = (int(np.prod(shape)) // 128, 128)
        return view, view, pl.BlockSpec(view, lambda i: (0, 0))
    if len(shape) == 1:
        n = shape[0] // NDEV
        return (1, n), (NDEV, 1, n), pl.BlockSpec((None, 1, n), lambda i: (me(), 0, 0))
    part = _shard_shape(name)
    return part, shape, pl.BlockSpec(part, lambda i: tuple(me() if d == ax else 0 for d in range(len(shape))))


WEIGHTS = ["e_pre_norm", "e_w_in", "e_pool_w", "e_pool_scale", "e_w_out", "e_post_norm", "o_pre_norm", "o_w_in",
           "o_sgu_norm_g", "o_sgu_norm_b", "o_sgu_w", "o_sgu_b", "o_conv_w", "o_conv_b", "o_conv_norm_g",
           "o_conv_norm_b", "o_w_out", "o_post_norm"]


def kernel(x, e_pre_norm, e_w_in, e_pool_w, e_pool_scale, e_w_out, e_post_norm, o_pre_norm, o_w_in, o_sgu_norm_g, o_sgu_norm_b, o_sgu_w, o_sgu_b, o_conv_w, o_conv_b, o_conv_norm_g, o_conv_norm_b, o_w_out, o_post_norm, loss_target, m_e_pre_norm, m_e_w_in, m_e_pool_w, m_e_pool_scale, m_e_w_out, m_e_post_norm, m_o_pre_norm, m_o_w_in, m_o_sgu_norm_g, m_o_sgu_norm_b, m_o_sgu_w, m_o_sgu_b, m_o_conv_w, m_o_conv_b, m_o_conv_norm_g, m_o_conv_norm_b, m_o_w_out, m_o_post_norm, v_e_pre_norm, v_e_w_in, v_e_pool_w, v_e_pool_scale, v_e_w_out, v_e_post_norm, v_o_pre_norm, v_o_w_in, v_o_sgu_norm_g, v_o_sgu_norm_b, v_o_sgu_w, v_o_sgu_b, v_o_conv_w, v_o_conv_b, v_o_conv_norm_g, v_o_conv_norm_b, v_o_w_out, v_o_post_norm):
    given = dict(locals())
    w = {n: given[n][0] for n in WEIGHTS}
    m = {n: given["m_" + n][0] for n in WEIGHTS}
    v = {n: given["v_" + n][0] for n in WEIGHTS}
    me = 4 * lax.axis_index("x") + 2 * lax.axis_index("y") + lax.axis_index("c")
    x, target = x[0], loss_target[0]
    row = lambda a: a.reshape(1, -1)

    lo, small_rows = _sc_gather_two_level(
        "gather_a0", 0, [_cast_bf16(w["e_w_in"], "cast_e_w_in_0", 0, 2), _pack([w[n] for n in SMALL_SHARDED])])
    hi, = _sc_gather_two_level("gather_a1", 12, [_cast_bf16(w["e_w_in"], "cast_e_w_in_1", 1, 2)])
    wg_e_in = (lo, hi)
    h0, h0t = _pre0_fwd(x, row(w["e_pre_norm"]))
    wg_e_out, = _sc_gather_two_level("gather_b", 1, [_cast_bf16(w["e_w_out"], "cast_e_w_out")])
    wg_o_in, = _sc_gather_two_level("gather_c", 13, [_cast_bf16(w["o_w_in"], "cast_o_w_in")])
    wg_o_out, = _sc_gather_two_level("gather_d", 16, [_cast_bf16(w["o_w_out"], "cast_o_w_out")])
    p = {n: w[n] for n in SMALL if SMALL[n][1] is None}
    small_rows = small_rows.reshape(NDEV, -1)
    off = 0
    for n in SMALL_SHARDED:
        shp, ax = _shard_shape(n), SMALL[n][1]
        cnt = int(np.prod(shp))
        blk = small_rows[:, off:off + cnt].reshape((NDEV,) + shp)
        p[n] = jnp.moveaxis(blk, 0, ax).reshape(SMALL[n][0])
        off += cnt
    tabs = _rope_tables()
    pool_w_bf = p["e_pool_w"].astype(BF16)
    sgu_bb = jnp.broadcast_to(p["o_sgu_b"][:, :, None], (4, 128, 128))
    conv_w = jnp.concatenate([p["o_conv_w"], jnp.zeros((HALO - CONV_K, HALF), F32)], axis=0)
    odd_p = (row(p["o_sgu_norm_g"]), row(p["o_sgu_norm_b"]), p["o_sgu_w"], sgu_bb, conv_w,
             row(p["o_conv_b"]), row(p["o_conv_norm_g"]), row(p["o_conv_norm_b"]))

    z0 = _mm_in_halves(h0, wg_e_in, "mm_z0")
    ycat0 = _pool_fwd(z0, pool_w_bf, row(p["e_pool_scale"]))
    qkv = _qkv_prep(z0, tabs)
    ycat0, og, lg = _attn_fwd(z0, qkv, ycat0)
    w_out_e, w_out_o = wg_e_out.reshape(2048, D), wg_o_out.reshape(2048, D)
    y0, x1, h1 = _post0_fwd(ycat0, w_out_e, x, row(p["e_post_norm"]), row(p["o_pre_norm"]), wg_e_out)
    h0t_sib = _sc_sibling_exchange("swap_h0", 8, h0t, h0t.shape, lambda c, src, out: [(src, out)], h1)
    h1_sib = _sc_sibling_exchange("swap_h1", 11, h1, h1.shape, lambda c, src, out: [(src, out)])
    z1 = _mm_in(h1, wg_o_in, "mm_z1")
    ycat1, conv_out = _odd_fwd(z1, *odd_p)

    g = {}
    loss_part, dx2, dy1, g["o_post_norm"] = _post1_bwd(ycat1, w_out_o, x1, target, row(p["o_post_norm"]),
                                                       _arrived(h1_sib, "arrived_h_sib", h0t_sib))
    parts = {}
    dw = _mm_out_dw(ycat1, dy1, "mm_dwout1").reshape(NDEV, 256, D)
    parts["o_w_out"], = _sc_exchange("scatter_o_w_out", 2, [dw], True)
    dycat1 = _mm_out_dx(dy1, w_out_o, "mm_dycat1", dw)
    dz1, g["o_sgu_w"], d_sgu_bb, g["o_sgu_norm_g"], g["o_sgu_norm_b"], g["o_conv_norm_g"], g["o_conv_norm_b"], \
        g["o_conv_b"], d_conv_w = _odd_bwd(z1, conv_out, dycat1, *odd_p[:4], conv_w, *odd_p[6:])
    g["o_sgu_b"] = d_sgu_bb[:, :, 0]
    g["o_conv_w"] = d_conv_w[:CONV_K]
    grads, deltas, new_m, new_v = {}, {}, {}, {}

    def adam(n, dep):
        grads[n], deltas[n], new_m[n], new_v[n] = _adam_reduce(parts[n], w[n], m[n], v[n], "adam_" + n, dep)
        return new_v[n]

    pin = _arrived(parts["o_w_out"], "arrived_o_w_out", d_conv_w)
    dz1_sib = _swap_class_columns("swap_dz1", 10, dz1, ODD_IN // NDEV)
    dw = _mm_pair_dw(h1, dz1, h1_sib, dz1_sib, ODD_IN // NDEV, "mm_dwin1", pin)
    parts["o_w_in"] = _sc_chip_scatter("scatter_o_w_in", 3, dw)
    dh1 = _mm_in_dx(dz1, wg_o_in, "mm_dh1", dw)
    dx1, dy0, g["o_pre_norm"], g["e_post_norm"] = _mid_bwd(dx2, dh1, x1, y0, row(p["o_pre_norm"]),
                                                           row(p["e_post_norm"]))
    dw = _mm_out_dw(ycat0, dy0, "mm_dwout0").reshape(NDEV, 256, D)
    parts["e_w_out"], = _sc_exchange("scatter_e_w_out", 4, [dw], True)
    dycat0 = _mm_out_dx(dy0, w_out_e, "mm_dycat0", dw)
    da_in, da_gate, g["e_pool_w"], g["e_pool_scale"] = _pool_bwd(z0, dycat0, pool_w_bf, row(p["e_pool_scale"]))
    late = [n for n in SMALL if n not in ("e_pre_norm", "o_sgu_b")] + ["o_sgu_b"]
    pieces = [g[n].reshape(SMALL[n][0]) for n in late[:-1]] + [jnp.broadcast_to(loss_part, (8, 128)), g[late[-1]]]
    recv_small, = _sc_gather_two_level("gather_small_grads", 6, [_pack(pieces, 512)])
    took = _arrived(parts["o_w_in"], "arrived_o_w_in")
    dq, dk, dv, dbg = _attn_bwd(z0, qkv, og, lg, dycat0, tabs, took)
    dz0 = jnp.concatenate([da_in, da_gate, dq, dk, dv, dbg], axis=1)
    took = _arrived(recv_small, "arrived_small_grads", _arrived(parts["e_w_out"], "arrived_e_w_out", dz0))
    nb = EVEN_IN // NDEV
    swapped = [_swap_class_columns("swap_dz0_%d" % half, (9, 14)[half], dz0, nb, half, 2) for half in (0, 1)]
    dw, e_w_in_parts = took, []
    for half in (0, 1):
        dw = _mm_pair_dw(h0t, dz0, h0t_sib, swapped[half], nb, "mm_dwin0_%d" % half, dw, half, 2, True, half == 1)
        e_w_in_parts.append(_sc_chip_scatter("scatter_e_w_in_%d" % half, (5, 15)[half], dw))
    pin = adam("e_w_out", adam("o_w_out", adam("o_w_in", dw)))
    rows = [int(np.prod(SMALL[n][0])) // 128 for n in late]
    sums = _sum_unpack(recv_small, rows[:-1] + [8, rows[-1]], "sum_small_grads", pin)
    summed = dict(zip(late, sums[:-2] + sums[-1:]))
    loss = sums[-2][0, 0]
    dh0 = _mm_in_dx_halves(dz0, wg_e_in, "mm_dh0", summed[late[0]])
    grad_x, g["e_pre_norm"] = _pre0_bwd(dx1, dh0, x, row(p["e_pre_norm"]))
    last, = _sc_exchange("gather_e_pre_norm_grad", 7, [g["e_pre_norm"].reshape(16, 128)], False)

    n = "e_w_in"
    out = _adam_reduce(e_w_in_parts[0], w[n], m[n], v[n], "adam_e_w_in_0", grad_x, 0, 2)
    out = _adam_reduce(e_w_in_parts[1], w[n], m[n], v[n], "adam_e_w_in_1", None, 1, 2, out)
    grads[n], deltas[n], new_m[n], new_v[n] = out
    summed["e_pre_norm"] = _sum_parts(last, "sum_e_pre_norm_grad", out[3])
    names = list(SMALL)
    views = [_small_views(n) for n in names]
    mine = lambda src: [src[n].reshape(vw[0]) for n, vw in zip(names, views)]
    res = _adam_small(mine(w), [summed[n].reshape(vw[1]) for n, vw in zip(names, views)], [vw[2] for vw in views],
                      mine(m), mine(v), "adam_small")
    for n, out in zip(names, res):
        grads[n], deltas[n], new_m[n], new_v[n] = [t.reshape(_shard_shape(n)) for t in out]

    lead = lambda a: a[None]
    return (loss, grad_x[None], *[lead(grads[n]) for n in WEIGHTS], *[lead(deltas[n]) for n in WEIGHTS],
            *[lead(new_m[n]) for n in WEIGHTS], *[lead(new_v[n]) for n in WEIGHTS])
```

```python
import numpy as np
import jax
import jax.numpy as jnp
from jax import lax
from jax.experimental import pallas as pl
from jax.experimental.pallas import tpu as pltpu
from jax.experimental.pallas import tpu_sc as plsc

F32 = jnp.float32
BF16 = jnp.bfloat16

S = 2048
D = 2048
NDEV = 8
EPS = 1e-6
NEG = -1e30
HEAD_DIM = 128
ROT_DIM = 32
ROPE_THETA = 500000.0
PATTERNS = ((128, 1), (512, 4), (2048, 16))
BLK = 128
EVEN_IN = 12288
ODD_IN = 6144
HALF = 1024
CONV_K = 31
HALO = 32
TR = 256
SUB = 16

ADAM_LR = 0.001
ADAM_B1 = 0.9
ADAM_B2 = 0.999
ADAM_EPS = 1e-08
ADAM_WD = 0.01
ADAM_STEP = 10

VMEM_BIG = 56 * 1024 * 1024
MESH = pl.DeviceIdType.MESH

NN = (((1,), (0,)), ((), ()))
NT = (((1,), (1,)), ((), ()))
TN = (((0,), (0,)), ((), ()))


def _dot(a, b, dn=NN):
    return lax.dot_general(a, b, dn, preferred_element_type=F32)


def _sigmoid(x):
    return 1.0 / (1.0 + jnp.exp(-x))


def _silu_and_grad(x):
    sg = _sigmoid(x)
    return x * sg, sg * (1.0 + x * (1.0 - sg))


def _params(sem, vmem=None):
    return pltpu.CompilerParams(dimension_semantics=sem, vmem_limit_bytes=vmem)


ANY_SPEC = pl.BlockSpec(memory_space=pl.ANY)


def _matmul(a, b, *, dn, grid, a_spec, b_spec, o_spec, out_shape, out_dtype, acc_shape, name, dep=None):
    nk = grid[2]
    deps = [] if dep is None else list(dep) if isinstance(dep, (tuple, list)) else [dep]

    def body(a_ref, b_ref, *rest):
        o_ref, acc = rest[len(deps)], rest[len(deps) + 1:]
        if nk == 1:
            o_ref[...] = _dot(a_ref[...], b_ref[...], dn).astype(o_ref.dtype)
            return
        acc_ref = acc[0]
        k = pl.program_id(2)

        @pl.when(k == 0)
        def _():
            acc_ref[...] = jnp.zeros_like(acc_ref)

        acc_ref[...] += _dot(a_ref[...], b_ref[...], dn)

        @pl.when(k == nk - 1)
        def _():
            o_ref[...] = acc_ref[...].astype(o_ref.dtype)

    return pl.pallas_call(
        body, grid=grid, in_specs=[a_spec, b_spec] + [ANY_SPEC] * len(deps), out_specs=o_spec,
        out_shape=jax.ShapeDtypeStruct(out_shape, out_dtype),
        scratch_shapes=[] if nk == 1 else [pltpu.VMEM(acc_shape, F32)],
        compiler_params=_params(("parallel", "parallel", "arbitrary"), VMEM_BIG), name=name,
    )(a, b, *deps)


TM = 2048


def _mm_in(h, wg, name):
    nb = wg.shape[2]
    tn = 512 if nb % 512 == 0 else nb
    per = nb // tn
    return _matmul(
        h, wg, dn=NN, grid=(S // TM, NDEV * per, 1),
        a_spec=pl.BlockSpec((TM, D), lambda i, j, k: (i, 0)),
        b_spec=pl.BlockSpec((None, D, tn), lambda i, j, k: (j // per, 0, j % per)),
        o_spec=pl.BlockSpec((TM, tn), lambda i, j, k: (i, j)),
        out_shape=(S, NDEV * nb), out_dtype=F32, acc_shape=(TM, tn), name=name)


def _mm_in_halves(h, wg_halves, name):
    hb = wg_halves[0].shape[2]
    z = None
    for half, wg in enumerate(wg_halves):
        prev = [] if z is None else [z]

        def body(a_ref, b_ref, *rest):
            rest[-1][...] = _dot(a_ref[...], b_ref[...])

        z = pl.pallas_call(
            body, grid=(NDEV,),
            in_specs=[pl.BlockSpec((S, D), lambda j: (0, 0)), pl.BlockSpec((None, D, hb), lambda j: (j, 0, 0))]
                     + [ANY_SPEC] * len(prev),
            out_specs=pl.BlockSpec((S, hb), lambda j, half=half: (0, 2 * j + half)),
            out_shape=jax.ShapeDtypeStruct((S, 2 * NDEV * hb), F32),
            input_output_aliases={2: 0} if prev else {},
            compiler_params=_params(("parallel",), VMEM_BIG), name="%s_%d" % (name, half),
        )(h, wg, *prev)
    return z


def _mm_in_dx_halves(dz, wg_halves, name, dep):
    hb = wg_halves[0].shape[2]
    nk = 2 * NDEV

    def body(a_ref, b0_ref, b1_ref, dep_ref, o_ref, acc_ref):
        k = pl.program_id(2)

        @pl.when(k == 0)
        def _():
            acc_ref[...] = jnp.zeros_like(acc_ref)

        @pl.when(k % 2 == 0)
        def _():
            acc_ref[...] += _dot(a_ref[...], b0_ref[...], NT)

        @pl.when(k % 2 == 1)
        def _():
            acc_ref[...] += _dot(a_ref[...], b1_ref[...], NT)

        @pl.when(k == nk - 1)
        def _():
            o_ref[...] = acc_ref[...]

    b_spec = pl.BlockSpec((None, 1024, hb), lambda i, j, k: (k // 2, j, 0))
    return pl.pallas_call(
        body, grid=(1, D // 1024, nk),
        in_specs=[pl.BlockSpec((S, hb), lambda i, j, k: (0, k)), b_spec, b_spec, ANY_SPEC],
        out_specs=pl.BlockSpec((S, 1024), lambda i, j, k: (0, j)), out_shape=jax.ShapeDtypeStruct((S, D), F32),
        scratch_shapes=[pltpu.VMEM((S, 1024), F32)],
        compiler_params=_params(("parallel", "parallel", "arbitrary"), VMEM_BIG), name=name,
    )(dz, *wg_halves, dep)


def _mm_in_dx(dz, wg, name, dep=None):
    nb = wg.shape[2]
    return _matmul(
        dz, wg, dn=NT, grid=(S // TM, D // 1024, NDEV),
        a_spec=pl.BlockSpec((TM, nb), lambda i, j, k: (i, k)),
        b_spec=pl.BlockSpec((None, 1024, nb), lambda i, j, k: (k, j, 0)),
        o_spec=pl.BlockSpec((TM, 1024), lambda i, j, k: (i, j)),
        out_shape=(S, D), out_dtype=F32, acc_shape=(TM, 1024), name=name, dep=dep)


def _mm_out_dx(dy, w, name, dep=None):
    return _matmul(
        dy, w, dn=NT, grid=(S // TM, 2048 // 512, 1),
        a_spec=pl.BlockSpec((TM, D), lambda i, j, k: (i, 0)),
        b_spec=pl.BlockSpec((512, D), lambda i, j, k: (j, 0)),
        o_spec=pl.BlockSpec((TM, 512), lambda i, j, k: (i, j)),
        out_shape=(S, 2048), out_dtype=F32, acc_shape=(TM, 512), name=name, dep=dep)


def _mm_out_dw(yc, dy, name):
    return _matmul(
        yc, dy, dn=TN, grid=(2048 // TM, D // 512, 1),
        a_spec=pl.BlockSpec((S, TM), lambda i, j, k: (0, i)),
        b_spec=pl.BlockSpec((S, 512), lambda i, j, k: (0, j)),
        o_spec=pl.BlockSpec((TM, 512), lambda i, j, k: (i, j)),
        out_shape=(2048, D), out_dtype=BF16, acc_shape=(TM, 512), name=name)


def _row_spec(w=D):
    return pl.BlockSpec((TR, w), lambda i: (i, 0))


def _vec_spec(w=D):
    return pl.BlockSpec((1, w), lambda i: (0, 0))


def _rms_stats(x):
    r = lax.rsqrt(jnp.mean(x * x, axis=-1, keepdims=True) + EPS)
    return x * r, r


def _rms_bwd(dn, xhat, r, g):
    dxh = dn * g
    return r * (dxh - xhat * jnp.mean(dxh * xhat, axis=-1, keepdims=True))


def _acc_rows(ref, val, i):
    s = jnp.sum(val, axis=0, keepdims=True)

    @pl.when(i == 0)
    def _():
        ref[...] = s

    @pl.when(i > 0)
    def _():
        ref[...] += s


def _pre0_fwd(x, g):
    def body(x_ref, g_ref, h_ref, ht_ref):
        xhat, _ = _rms_stats(x_ref[...])
        h = xhat * g_ref[...]
        h_ref[...] = h.astype(BF16)
        ht_ref[...] = h.T.astype(BF16)

    return pl.pallas_call(
        body, grid=(S // TR,), in_specs=[_row_spec(), _vec_spec()],
        out_specs=[_row_spec(), pl.BlockSpec((D, TR), lambda i: (0, i))],
        out_shape=[jax.ShapeDtypeStruct((S, D), BF16), jax.ShapeDtypeStruct((D, S), BF16)],
        compiler_params=_params(("parallel",)), name="pre0_fwd",
    )(x, g)


def _post0_fwd(ycat, w_out, x, g_post, g_pre1, dep):
    def body(yc_ref, w_ref, x_ref, gp_ref, g1_ref, dep_ref, y_ref, x1_ref, h1_ref):
        y = _dot(yc_ref[...], w_ref[...])
        y_ref[...] = y
        yhat, _ = _rms_stats(y)
        x1 = x_ref[...] + yhat * gp_ref[...]
        x1_ref[...] = x1
        xhat, _ = _rms_stats(x1)
        h1_ref[...] = (xhat * g1_ref[...]).astype(BF16)

    return pl.pallas_call(
        body, grid=(S // TR,),
        in_specs=[_row_spec(), pl.BlockSpec((2048, D), lambda i: (0, 0)), _row_spec(), _vec_spec(), _vec_spec(),
                  ANY_SPEC],
        out_specs=[_row_spec(), _row_spec(), _row_spec()],
        out_shape=[jax.ShapeDtypeStruct((S, D), F32), jax.ShapeDtypeStruct((S, D), F32),
                   jax.ShapeDtypeStruct((S, D), BF16)],
        compiler_params=_params(("parallel",), VMEM_BIG), name="post0_fwd",
    )(ycat, w_out, x, g_post, g_pre1, dep)


def _post1_bwd(ycat, w_out, x1, target, g_post, dep):
    def body(yc_ref, w_ref, x1_ref, t_ref, g_ref, dep_ref, loss_ref, dx2_ref, dy_ref, dg_ref):
        i = pl.program_id(0)
        yhat, r = _rms_stats(_dot(yc_ref[...], w_ref[...]))
        g = g_ref[...]
        err = x1_ref[...] + yhat * g - t_ref[...]
        part = jnp.sum(jnp.sum(err * err, axis=-1, keepdims=True), axis=0, keepdims=True) * (0.5 / D)
        _acc_rows(loss_ref, jnp.broadcast_to(part, (1, 128)), i)
        dx2 = err * (1.0 / D)
        dx2_ref[...] = dx2
        _acc_rows(dg_ref, dx2 * yhat, i)
        dy_ref[...] = _rms_bwd(dx2, yhat, r, g).astype(BF16)

    return pl.pallas_call(
        body, grid=(S // TR,),
        in_specs=[_row_spec(), pl.BlockSpec((2048, D), lambda i: (0, 0)), _row_spec(), _row_spec(), _vec_spec(),
                  ANY_SPEC],
        out_specs=[_vec_spec(128), _row_spec(), _row_spec(), _vec_spec()],
        out_shape=[jax.ShapeDtypeStruct((1, 128), F32), jax.ShapeDtypeStruct((S, D), F32),
                   jax.ShapeDtypeStruct((S, D), BF16), jax.ShapeDtypeStruct((1, D), F32)],
        compiler_params=_params(("arbitrary",), VMEM_BIG), name="post1_bwd",
    )(ycat, w_out, x1, target, g_post, dep)


def _mid_bwd(dx2, dh1, x1, y0, g_pre1, g_post0):
    def body(dx2_ref, dh_ref, x1_ref, y_ref, g1_ref, gp_ref, dx1_ref, dy_ref, dg1_ref, dgp_ref):
        i = pl.program_id(0)
        xhat, r1 = _rms_stats(x1_ref[...])
        dh = dh_ref[...]
        _acc_rows(dg1_ref, dh * xhat, i)
        dx1 = dx2_ref[...] + _rms_bwd(dh, xhat, r1, g1_ref[...])
        dx1_ref[...] = dx1
        yhat, r0 = _rms_stats(y_ref[...])
        _acc_rows(dgp_ref, dx1 * yhat, i)
        dy_ref[...] = _rms_bwd(dx1, yhat, r0, gp_ref[...]).astype(BF16)

    return pl.pallas_call(
        body, grid=(S // TR,),
        in_specs=[_row_spec(), _row_spec(), _row_spec(), _row_spec(), _vec_spec(), _vec_spec()],
        out_specs=[_row_spec(), _row_spec(), _vec_spec(), _vec_spec()],
        out_shape=[jax.ShapeDtypeStruct((S, D), F32), jax.ShapeDtypeStruct((S, D), BF16),
                   jax.ShapeDtypeStruct((1, D), F32), jax.ShapeDtypeStruct((1, D), F32)],
        compiler_params=_params(("arbitrary",)), name="mid_bwd",
    )(dx2, dh1, x1, y0, g_pre1, g_post0)


def _pre0_bwd(dx1, dh0, x, g):
    def body(dx1_ref, dh_ref, x_ref, g_ref, gx_ref, dg_ref):
        i = pl.program_id(0)
        xhat, r = _rms_stats(x_ref[...])
        dh = dh_ref[...]
        _acc_rows(dg_ref, dh * xhat, i)
        gx_ref[...] = dx1_ref[...] + _rms_bwd(dh, xhat, r, g_ref[...])

    return pl.pallas_call(
        body, grid=(S // TR,), in_specs=[_row_spec(), _row_spec(), _row_spec(), _vec_spec()],
        out_specs=[_row_spec(), _vec_spec()],
        out_shape=[jax.ShapeDtypeStruct((S, D), F32), jax.ShapeDtypeStruct((1, D), F32)],
        compiler_params=_params(("arbitrary",)), name="pre0_bwd",
    )(dx1, dh0, x, g)


POOL_CH = 256


def _pool_apply(a, w, transpose):
    n = a.shape[0]
    row = lax.broadcasted_iota(jnp.int32, a.shape, 0)
    cnt = jnp.minimum(row + 1, w).astype(F32)
    s = a / cnt if transpose else a
    for k in (1, 2, 4, 8):
        if transpose:
            sh = jnp.where(row < n - k, pltpu.roll(s, n - k, 0), 0.0)
        else:
            sh = jnp.where(row >= k, pltpu.roll(s, k, 0), 0.0)
        s = jnp.where(w > k, s + sh, s)
    return s - a if transpose else s / cnt - a


def _pool_fwd(z0, pool_w, pool_scale):
    def body(a_ref, gate_ref, w_ref, sc_ref, out_ref):
        win = jnp.left_shift(2, pl.program_id(0))
        pooled = _pool_apply(a_ref[...], win, False)
        mixed = _dot(pooled.astype(BF16), w_ref[...])
        gate = gate_ref[...]
        out_ref[...] = (mixed * sc_ref[...] * (gate * _sigmoid(gate))).astype(BF16)

    return pl.pallas_call(
        body, grid=(4,),
        in_specs=[pl.BlockSpec((S, POOL_CH), lambda g: (0, g)), pl.BlockSpec((S, POOL_CH), lambda g: (0, 4 + g)),
                  pl.BlockSpec((None, POOL_CH, POOL_CH), lambda g: (g, 0, 0)),
                  pl.BlockSpec((1, POOL_CH), lambda g: (0, g))],
        out_specs=pl.BlockSpec((S, POOL_CH), lambda g: (0, g)),
        out_shape=jax.ShapeDtypeStruct((S, 2048), BF16),
        compiler_params=_params(("parallel",), VMEM_BIG), name="pool_fwd",
    )(z0, z0, pool_w, pool_scale)


def _pool_bwd(z0, dycat, pool_w, pool_scale):
    def body(a_ref, gate_ref, dy_ref, w_ref, sc_ref, da_ref, dgate_ref, dw_ref, dsc_ref):
        win = jnp.left_shift(2, pl.program_id(0))
        pooled = _pool_apply(a_ref[...], win, False).astype(BF16)
        w = w_ref[...]
        mixed = _dot(pooled, w)
        silu, dsilu = _silu_and_grad(gate_ref[...])
        dy = dy_ref[...]
        sc = sc_ref[...]
        dgate_ref[...] = (dy * (mixed * sc) * dsilu).astype(BF16)
        dms = dy * silu
        dsc_ref[...] = jnp.sum(dms * mixed, axis=0, keepdims=True)
        dmixed = (dms * sc).astype(BF16)
        dw_ref[...] = _dot(pooled, dmixed, TN)
        dpooled = _dot(dmixed, w, NT)
        da_ref[...] = _pool_apply(dpooled, win, True).astype(BF16)

    slab = lambda off: pl.BlockSpec((S, POOL_CH), lambda g: (0, off + g))
    return pl.pallas_call(
        body, grid=(4,),
        in_specs=[slab(0), slab(4), slab(0), pl.BlockSpec((None, POOL_CH, POOL_CH), lambda g: (g, 0, 0)),
                  pl.BlockSpec((1, POOL_CH), lambda g: (0, g))],
        out_specs=[slab(0), slab(0), pl.BlockSpec((None, POOL_CH, POOL_CH), lambda g: (g, 0, 0)),
                   pl.BlockSpec((1, POOL_CH), lambda g: (0, g))],
        out_shape=[jax.ShapeDtypeStruct((S, HALF), BF16), jax.ShapeDtypeStruct((S, HALF), BF16),
                   jax.ShapeDtypeStruct((4, POOL_CH, POOL_CH), F32), jax.ShapeDtypeStruct((1, HALF), F32)],
        compiler_params=_params(("parallel",), VMEM_BIG), name="pool_bwd",
    )(z0, z0, dycat, pool_w, pool_scale)


Q_COL, K_COL, V_COL, BG_COL = 2048 // 128, 5120 // 128, 8192 // 128, 11264 // 128
SCALE = HEAD_DIM ** -0.5


def _rope_tables():
    pos = jnp.arange(S, dtype=F32)
    inv_freq = jnp.power(ROPE_THETA, -jnp.arange(0, ROT_DIM, 2, dtype=F32) / ROT_DIM)
    ang = pos[:, None] * inv_freq[None, :]
    cos, sin = jnp.cos(ang), jnp.sin(ang)
    half = ROT_DIM // 2
    zeros = jnp.zeros((S, HEAD_DIM - ROT_DIM), F32)
    c = jnp.concatenate([cos, cos, jnp.ones((S, HEAD_DIM - ROT_DIM), F32)], axis=1)
    a = jnp.concatenate([-sin, jnp.zeros((S, half), F32), zeros], axis=1)
    b = jnp.concatenate([jnp.zeros((S, half), F32), sin, zeros], axis=1)
    return c, a, b


def _rope(t, c, a, b):
    half = ROT_DIM // 2
    return t * c + pltpu.roll(t, HEAD_DIM - half, 1) * a + pltpu.roll(t, half, 1) * b


def _rope_t(d, c, a, b):
    half = ROT_DIM // 2
    return d * c + pltpu.roll(d * a, half, 1) + pltpu.roll(d * b, HEAD_DIM - half, 1)


def _deinterleave(dst, src, dil, cast=None, dst_off=0):
    length = S // dil
    for r in range(dil):
        v = src[...] if dil == 1 else src[pl.ds(r, length, stride=dil), :]
        dst[dst_off + r * length:dst_off + (r + 1) * length, :] = v if cast is None else v.astype(cast)


def _interleave(dst, src, dil, src_off=0):
    length = S // dil
    for r in range(dil):
        if dil == 1:
            dst[...] = src[src_off:src_off + S, :]
        else:
            dst[pl.ds(r, length, stride=dil), :] = src[src_off + r * length:src_off + (r + 1) * length, :]


CU = 8
NUNITS = S // BLK
B_QK = (((2,), (2,)), ((0,), (0,)))
B_PV = (((2,), (1,)), ((0,), (0,)))
B_TN = (((1,), (1,)), ((0,), (0,)))


def _blocks(ref, first):
    return ref[first * BLK:(first + CU) * BLK, :].reshape(CU, BLK, HEAD_DIM)


def _chunk_scores(u0, nb, qd, kdp):
    q = _blocks(qd, u0)
    row = lax.broadcasted_iota(jnp.int32, (CU, BLK, BLK), 1)
    col = lax.broadcasted_iota(jnp.int32, (CU, BLK, BLK), 2)
    s_own = jnp.where(col <= row, _dot(q, _blocks(kdp, u0 + 1), B_QK) * SCALE, NEG)
    if nb == 1:
        return q, s_own, None
    unit = lax.broadcasted_iota(jnp.int32, (CU, BLK, BLK), 0) + u0
    s_prev = jnp.where((col >= row) & ((unit % nb) != 0), _dot(q, _blocks(kdp, u0), B_QK) * SCALE, NEG)
    return q, s_own, s_prev


def _qkv_prep(z0, tabs):
    def body(q_ref, k_ref, v_ref, c_ref, a_ref, b_ref, qo_ref, ko_ref, vo_ref, tmp):
        p = pl.program_id(1)
        ko_ref[0:BLK, :] = jnp.zeros((BLK, HEAD_DIM), BF16)
        vo_ref[0:BLK, :] = jnp.zeros((BLK, HEAD_DIM), BF16)
        for gi, (_, dil) in enumerate(PATTERNS):
            @pl.when(p == gi)
            def _(dil=dil):
                c, a, b = c_ref[...], a_ref[...], b_ref[...]
                tmp[...] = _rope(q_ref[...], c, a, b)
                _deinterleave(qo_ref, tmp, dil, BF16)
                tmp[...] = _rope(k_ref[...], c, a, b)
                _deinterleave(ko_ref, tmp, dil, BF16, BLK)
                _deinterleave(vo_ref, v_ref, dil, BF16, BLK)

    tab = pl.BlockSpec((S, HEAD_DIM), lambda h, p: (0, 0))
    out = pl.BlockSpec((S, HEAD_DIM), lambda h, p: (0, p * 8 + h))
    outp = pl.BlockSpec((S + BLK, HEAD_DIM), lambda h, p: (0, p * 8 + h))
    return pl.pallas_call(
        body, grid=(8, 3), in_specs=[_head_spec(Q_COL), _head_spec(K_COL), _head_spec(V_COL), tab, tab, tab],
        out_specs=[out, outp, outp],
        out_shape=[jax.ShapeDtypeStruct((S, 3072), BF16)] + [jax.ShapeDtypeStruct((S + BLK, 3072), BF16)] * 2,
        scratch_shapes=[pltpu.VMEM((S, HEAD_DIM), F32)],
        compiler_params=_params(("parallel", "arbitrary"), VMEM_BIG), name="qkv_prep",
    )(z0, z0, z0, *tabs)


def _attn_group_fwd(dil, qd, kdp, vdp, od, ld, og, lg):
    nb = S // dil // BLK
    for u0 in range(0, NUNITS, CU):
        _, s_own, s_prev = _chunk_scores(u0, nb, qd, kdp)
        m = jnp.max(s_own, axis=2, keepdims=True)
        if s_prev is not None:
            m = jnp.maximum(m, jnp.max(s_prev, axis=2, keepdims=True))
        p_own = jnp.exp(s_own - m)
        den = jnp.sum(p_own, axis=2, keepdims=True)
        acc = _dot(p_own.astype(BF16), _blocks(vdp, u0 + 1), B_PV)
        if s_prev is not None:
            p_prev = jnp.exp(s_prev - m)
            den = den + jnp.sum(p_prev, axis=2, keepdims=True)
            acc = acc + _dot(p_prev.astype(BF16), _blocks(vdp, u0), B_PV)
        rows = slice(u0 * BLK, (u0 + CU) * BLK)
        od[rows, :] = (acc / den).reshape(CU * BLK, HEAD_DIM)
        ld[rows, :] = jnp.broadcast_to(m + jnp.log(den), (CU, BLK, HEAD_DIM)).reshape(CU * BLK, HEAD_DIM)
    _interleave(og, od, dil)
    _interleave(lg, ld, dil)


def _group_weights(lgs):
    l0, l1, l2 = lgs[0][...], lgs[1][...], lgs[2][...]
    mx = jnp.maximum(l0, jnp.maximum(l1, l2))
    e0, e1, e2 = jnp.exp(l0 - mx), jnp.exp(l1 - mx), jnp.exp(l2 - mx)
    den = e0 + e1 + e2
    return e0 / den, e1 / den, e2 / den


def _head_spec(base):
    return pl.BlockSpec((S, HEAD_DIM), lambda h, p: (0, base + (p % 3) * 8 + h))


def _slab(dtype=F32, rows=S):
    return pltpu.VMEM((rows, HEAD_DIM), dtype)


def _attn_fwd(z0, tabs, ycat):
    def body(q_ref, k_ref, v_ref, c_ref, a_ref, b_ref, gate_ref, ycat_ref, out_ref, og_ref, lg_ref,
             qo_ref, ko_ref, vo_ref, tmp, od, ld, og0, og1, og2, lg0, lg1, lg2):
        del ycat_ref
        p = pl.program_id(1)
        ogs, lgs = (og0, og1, og2), (lg0, lg1, lg2)
        ko_ref[0:BLK, :] = jnp.zeros((BLK, HEAD_DIM), BF16)
        vo_ref[0:BLK, :] = jnp.zeros((BLK, HEAD_DIM), BF16)
        for gi, (_, dil) in enumerate(PATTERNS):
            @pl.when(p == gi)
            def _(gi=gi, dil=dil):
                c, a, b = c_ref[...], a_ref[...], b_ref[...]
                tmp[...] = _rope(q_ref[...], c, a, b)
                _deinterleave(qo_ref, tmp, dil, BF16)
                tmp[...] = _rope(k_ref[...], c, a, b)
                _deinterleave(ko_ref, tmp, dil, BF16, BLK)
                _deinterleave(vo_ref, v_ref, dil, BF16, BLK)
                _attn_group_fwd(dil, qo_ref, ko_ref, vo_ref, od, ld, ogs[gi], lgs[gi])
                og_ref[...] = ogs[gi][...]
                lg_ref[...] = lgs[gi][...]

        @pl.when(p == 2)
        def _():
            w0, w1, w2 = _group_weights(lgs)
            o = w0 * og0[...] + w1 * og1[...] + w2 * og2[...]
            gate = gate_ref[...]
            out_ref[...] = (o * (gate * _sigmoid(gate))).astype(BF16)

    grp = pl.BlockSpec((S, HEAD_DIM), lambda h, p: (0, p * 8 + h))
    grp_pad = pl.BlockSpec((S + BLK, HEAD_DIM), lambda h, p: (0, p * 8 + h))
    tab = pl.BlockSpec((S, HEAD_DIM), lambda h, p: (0, 0))
    outs = pl.pallas_call(
        body, grid=(8, 3),
        in_specs=[_head_spec(Q_COL), _head_spec(K_COL), _head_spec(V_COL), tab, tab, tab,
                  pl.BlockSpec((S, HEAD_DIM), lambda h, p: (0, BG_COL + h)), ANY_SPEC],
        out_specs=[pl.BlockSpec((S, HEAD_DIM), lambda h, p: (0, 8 + h)), grp, grp, grp, grp_pad, grp_pad],
        out_shape=[jax.ShapeDtypeStruct((S, 2048), BF16), jax.ShapeDtypeStruct((S, 3072), F32),
                   jax.ShapeDtypeStruct((S, 3072), F32), jax.ShapeDtypeStruct((S, 3072), BF16),
                   jax.ShapeDtypeStruct((S + BLK, 3072), BF16), jax.ShapeDtypeStruct((S + BLK, 3072), BF16)],
        scratch_shapes=[_slab() for _ in range(9)],
        input_output_aliases={7: 0},
        compiler_params=_params(("parallel", "arbitrary"), VMEM_BIG), name="attn_fwd",
    )(z0, z0, z0, *tabs, z0, ycat)
    return outs[0], outs[1], outs[2], tuple(outs[3:])


def _attn_bwd(z0, qkv, og, lg, dycat, tabs, dep):
    def body(q_ref, k_ref, v_ref, gate_ref, dy_ref, c_ref, a_ref, b_ref,
             og0_ref, og1_ref, og2_ref, lg0_ref, lg1_ref, lg2_ref, dep_ref,
             dq_ref, dk_ref, dv_ref, dbg_ref,
             tmp, ld, dg0, dg1, dg2, cg0, cg1, cg2, dod, cd, dqd, dkd, dvd):
        kd, vd = k_ref, v_ref
        p = pl.program_id(1)
        ogs, lgs, dgs, cgs = (og0_ref, og1_ref, og2_ref), (lg0_ref, lg1_ref, lg2_ref), (dg0, dg1, dg2), (cg0, cg1, cg2)

        @pl.when(p == 0)
        def _():
            w = _group_weights(lgs)
            o = w[0] * ogs[0][...] + w[1] * ogs[1][...] + w[2] * ogs[2][...]
            silu, dsilu = _silu_and_grad(gate_ref[...])
            dy = dy_ref[...]
            dbg_ref[...] = (dy * o * dsilu).astype(BF16)
            do = dy * silu
            dwbar = jnp.sum(do * o, axis=1, keepdims=True)
            for gi in range(3):
                dgs[gi][...] = w[gi] * do
                cgs[gi][...] = -w[gi] * dwbar

        for gi, (_, dil) in enumerate(PATTERNS):
            @pl.when(p == 1 + gi)
            def _(gi=gi, dil=dil):
                nb = S // dil // BLK
                qd = q_ref
                c, a, b = c_ref[...], a_ref[...], b_ref[...]
                _deinterleave(dod, dgs[gi], dil, BF16)
                _deinterleave(ld, lgs[gi], dil)
                _deinterleave(cd, cgs[gi], dil)
                dkd[...] = jnp.zeros_like(dkd)
                dvd[...] = jnp.zeros_like(dvd)
                flat = lambda t: t.reshape(CU * BLK, HEAD_DIM)
                for u0 in range(0, NUNITS, CU):
                    q, s_own, s_prev = _chunk_scores(u0, nb, qd, kd)
                    lse, cv, do = _blocks(ld, u0), _blocks(cd, u0), _blocks(dod, u0)
                    own = slice((u0 + 1) * BLK, (u0 + 1 + CU) * BLK)
                    p_own = jnp.exp(s_own - lse)
                    ds_own = (p_own * (_dot(do, _blocks(vd, u0 + 1), B_QK) + cv) * SCALE).astype(BF16)
                    dq = _dot(ds_own, _blocks(kd, u0 + 1), B_PV)
                    dkd[own, :] += flat(_dot(ds_own, q, B_TN))
                    dvd[own, :] += flat(_dot(p_own.astype(BF16), do, B_TN))
                    if s_prev is not None:
                        prev = slice(u0 * BLK, (u0 + CU) * BLK)
                        p_prev = jnp.exp(s_prev - lse)
                        ds_prev = (p_prev * (_dot(do, _blocks(vd, u0), B_QK) + cv) * SCALE).astype(BF16)
                        dq = dq + _dot(ds_prev, _blocks(kd, u0), B_PV)
                        dkd[prev, :] += flat(_dot(ds_prev, q, B_TN))
                        dvd[prev, :] += flat(_dot(p_prev.astype(BF16), do, B_TN))
                    dqd[u0 * BLK:(u0 + CU) * BLK, :] = flat(dq)
                _interleave(tmp, dqd, dil)
                dq_ref[...] = _rope_t(tmp[...], c, a, b).astype(BF16)
                _interleave(tmp, dkd, dil, BLK)
                dk_ref[...] = _rope_t(tmp[...], c, a, b).astype(BF16)
                _interleave(tmp, dvd, dil, BLK)
                dv_ref[...] = tmp[...].astype(BF16)

    tab = pl.BlockSpec((S, HEAD_DIM), lambda h, p: (0, 0))
    hspec = lambda base: pl.BlockSpec((S, HEAD_DIM), lambda h, p: (0, base + h))
    gspec = pl.BlockSpec((S, HEAD_DIM), lambda h, p: (0, jnp.maximum(p - 1, 0) * 8 + h))
    gspec_pad = pl.BlockSpec((S + BLK, HEAD_DIM), lambda h, p: (0, jnp.maximum(p - 1, 0) * 8 + h))
    return pl.pallas_call(
        body, grid=(8, 4),
        in_specs=[gspec, gspec_pad, gspec_pad, hspec(BG_COL), hspec(8), tab, tab, tab,
                  hspec(0), hspec(8), hspec(16), hspec(0), hspec(8), hspec(16), ANY_SPEC],
        out_specs=[gspec, gspec, gspec, hspec(0)],
        out_shape=[jax.ShapeDtypeStruct((S, 3072), BF16)] * 3 + [jax.ShapeDtypeStruct((S, HALF), BF16)],
        scratch_shapes=[_slab(), _slab()] + [_slab() for _ in range(6)]
                       + [_slab(BF16), _slab(), _slab(), _slab(F32, S + BLK), _slab(F32, S + BLK)],
        compiler_params=_params(("parallel", "arbitrary"), VMEM_BIG), name="attn_bwd",
    )(*qkv, z0, dycat, *tabs, og, og, og, lg, lg, lg, dep)


SGU_CH = 256
NCHUNK = TR // 128


def _ln_stats(x):
    mu = jnp.mean(x, axis=-1, keepdims=True)
    xc = x - mu
    r = lax.rsqrt(jnp.mean(xc * xc, axis=-1, keepdims=True) + EPS)
    return xc * r, r


def _ln_bwd(dy, xhat, r, g):
    dxh = dy * g
    return r * (dxh - jnp.mean(dxh, axis=-1, keepdims=True) - xhat * jnp.mean(dxh * xhat, axis=-1, keepdims=True))


def _tril_bf16(w):
    row = lax.broadcasted_iota(jnp.int32, w.shape, 0)
    col = lax.broadcasted_iota(jnp.int32, w.shape, 1)
    return jnp.where(row >= col, w, 0.0).astype(BF16)


def _sgu_gate(vn_s, s_s, w_ref, bb_ref):
    for h in range(4):
        wm = _tril_bf16(w_ref[h])
        bias = bb_ref[h]
        for ch in range(NCHUNK):
            rows, cols = slice(ch * 128, (ch + 1) * 128), slice(h * SGU_CH, (h + 1) * SGU_CH)
            s_s[rows, cols] = _dot(wm, vn_s[rows, cols]) + jnp.concatenate([bias, bias], axis=1)


WIN = HALO + TR
SUBL = 8


def _shifted_copies(dst, src):
    dst[0] = src[...]
    for b in range(1, SUBL):
        dst[b, 0:WIN - SUBL, :] = src[pl.ds(b, WIN - SUBL), :]


def _rows_at(copies, off, n):
    return copies[off % SUBL, pl.ds(off - off % SUBL, n), :]


def _conv_fwd(i, dval_ref, dglu_ref, hval_ref, hglu_ref, cw_ref, cb_ref, xw, xr, dcs):
    halo = hval_ref[...] * _sigmoid(hglu_ref[...])
    xw[0:HALO, :] = jnp.where(i > 0, halo, 0.0)
    xw[HALO:HALO + TR, :] = dval_ref[...] * _sigmoid(dglu_ref[...])
    _shifted_copies(xr, xw)
    sub = 2 * SUB
    for rb in range(TR // sub):
        acc = jnp.broadcast_to(cb_ref[...], (sub, HALF))
        for k in range(CONV_K):
            acc = acc + cw_ref[k:k + 1, :] * _rows_at(xr, rb * sub + HALO - (CONV_K - 1) + k, sub)
        dcs[rb * sub:(rb + 1) * sub, :] = acc


def _odd_in_specs():
    col = lambda j: pl.BlockSpec((TR, HALF), lambda i, *_: (i, j))
    prev = lambda j: pl.BlockSpec((HALO, HALF), lambda i, *_: (jnp.maximum(i * (TR // HALO) - 1, 0), j))
    return [col(0), col(1), col(2), col(3), col(4), col(5), prev(3), prev(4)]


def _full_spec(shape):
    return pl.BlockSpec(shape, lambda i, *_: (0,) * len(shape))


def _odd_fwd(z1, sgu_g, sgu_b, sgu_w, sgu_bb, conv_w, conv_b, cn_g, cn_b):
    def body(u_ref, v_ref, cg_ref, dval_ref, dglu_ref, dgate_ref, hval_ref, hglu_ref,
             g_ref, b_ref, w_ref, bb_ref, cw_ref, cb_ref, cng_ref, cnb_ref, out_ref, dcs, vn_s, s_s, xw, xr):
        i = pl.program_id(0)
        vhat, _ = _ln_stats(v_ref[...])
        vn_s[...] = (vhat * g_ref[...] + b_ref[...]).astype(BF16)
        _sgu_gate(vn_s, s_s, w_ref, bb_ref)
        cg = cg_ref[...]
        out_ref[:, 0:HALF] = (u_ref[...] * s_s[...] * (cg * _sigmoid(cg))).astype(BF16)
        _conv_fwd(i, dval_ref, dglu_ref, hval_ref, hglu_ref, cw_ref, cb_ref, xw, xr, dcs)
        dhat, _ = _ln_stats(dcs[...])
        dn = dhat * cng_ref[...] + cnb_ref[...]
        dgate = dgate_ref[...]
        out_ref[:, HALF:2 * HALF] = ((dn * _sigmoid(dn)) * (dgate * _sigmoid(dgate))).astype(BF16)

    vec = _full_spec((1, HALF))
    return pl.pallas_call(
        body, grid=(S // TR,),
        in_specs=_odd_in_specs() + [vec, vec, _full_spec((4, 128, 128)), _full_spec((4, 128, 128)),
                                    _full_spec((HALO, HALF)), vec, vec, vec],
        out_specs=[pl.BlockSpec((TR, 2048), lambda i: (i, 0)), pl.BlockSpec((TR, HALF), lambda i: (i, 0))],
        out_shape=[jax.ShapeDtypeStruct((S, 2048), BF16), jax.ShapeDtypeStruct((S, HALF), F32)],
        scratch_shapes=[pltpu.VMEM((TR, HALF), BF16), pltpu.VMEM((TR, HALF), F32),
                        pltpu.VMEM((WIN, HALF), F32), pltpu.VMEM((SUBL, WIN, HALF), F32)],
        compiler_params=_params(("parallel",), VMEM_BIG), name="odd_fwd",
    )(z1, z1, z1, z1, z1, z1, z1, z1, sgu_g, sgu_b, sgu_w, sgu_bb, conv_w, conv_b, cn_g, cn_b)


def _odd_bwd_a(z1, dc, dycat, sgu_g, sgu_b, sgu_w, sgu_bb, cn_g, cn_b):
    def body(u_ref, v_ref, cg_ref, dgate_ref, dcs, dy_ref, g_ref, b_ref, w_ref, bb_ref, cng_ref, cnb_ref,
             dz_ref, ddc_ref, dw_ref, dbb_ref, dg_ref, db_ref, dcng_ref, dcnb_ref, dcb_ref,
             vn_s, s_s, ds_s, dvn_s):
        i = pl.program_id(0)
        vhat, rv = _ln_stats(v_ref[...])
        g = g_ref[...]
        vn_s[...] = (vhat * g + b_ref[...]).astype(BF16)
        _sgu_gate(vn_s, s_s, w_ref, bb_ref)
        silu_c, dsilu_c = _silu_and_grad(cg_ref[...])
        dyc = dy_ref[:, 0:HALF]
        u = u_ref[...]
        s = s_s[...]
        dz_ref[:, 0:HALF] = (dyc * s * silu_c).astype(BF16)
        dz_ref[:, 2 * HALF:3 * HALF] = (dyc * u * s * dsilu_c).astype(BF16)
        ds_s[...] = dyc * u * silu_c

        @pl.when(i == 0)
        def _():
            dw_ref[...] = jnp.zeros_like(dw_ref)
            dbb_ref[...] = jnp.zeros_like(dbb_ref)

        tril = lax.broadcasted_iota(jnp.int32, (128, 128), 0) >= lax.broadcasted_iota(jnp.int32, (128, 128), 1)
        for h in range(4):
            wm = _tril_bf16(w_ref[h])
            for ch in range(NCHUNK):
                rows, cols = slice(ch * 128, (ch + 1) * 128), slice(h * SGU_CH, (h + 1) * SGU_CH)
                ds = ds_s[rows, cols]
                dsb = ds.astype(BF16)
                dw_ref[h] += jnp.where(tril, _dot(dsb, vn_s[rows, cols], NT), 0.0)
                dbb_ref[h] += jnp.broadcast_to(jnp.sum(ds, axis=1, keepdims=True), (128, 128))
                dvn_s[rows, cols] = _dot(wm, dsb, TN)
        dvn = dvn_s[...]
        _acc_rows(dg_ref, dvn * vhat, i)
        _acc_rows(db_ref, dvn, i)
        dz_ref[:, HALF:2 * HALF] = _ln_bwd(dvn, vhat, rv, g).astype(BF16)

        dhat, rd = _ln_stats(dcs[...])
        cng = cng_ref[...]
        silu_n, dsilu_n = _silu_and_grad(dhat * cng + cnb_ref[...])
        silu_g, dsilu_g = _silu_and_grad(dgate_ref[...])
        dyd = dy_ref[:, HALF:2 * HALF]
        dz_ref[:, 5 * HALF:6 * HALF] = (dyd * silu_n * dsilu_g).astype(BF16)
        ddn = dyd * silu_g * dsilu_n
        _acc_rows(dcng_ref, ddn * dhat, i)
        _acc_rows(dcnb_ref, ddn, i)
        ddc = _ln_bwd(ddn, dhat, rd, cng)
        ddc_ref[...] = ddc
        _acc_rows(dcb_ref, ddc, i)

    vec = _full_spec((1, HALF))
    sq = _full_spec((4, 128, 128))
    col = lambda j: pl.BlockSpec((TR, HALF), lambda i: (i, j))
    return pl.pallas_call(
        body, grid=(S // TR,),
        in_specs=[col(0), col(1), col(2), col(5), col(0), pl.BlockSpec((TR, 2048), lambda i: (i, 0)),
                  vec, vec, sq, sq, vec, vec],
        out_specs=[pl.BlockSpec((TR, ODD_IN), lambda i: (i, 0)), pl.BlockSpec((TR, HALF), lambda i: (i, 0)),
                   sq, sq, vec, vec, vec, vec, vec],
        out_shape=[jax.ShapeDtypeStruct((S, ODD_IN), BF16), jax.ShapeDtypeStruct((S, HALF), F32),
                   jax.ShapeDtypeStruct((4, 128, 128), F32), jax.ShapeDtypeStruct((4, 128, 128), F32)]
                  + [jax.ShapeDtypeStruct((1, HALF), F32)] * 5,
        scratch_shapes=[pltpu.VMEM((TR, HALF), BF16), pltpu.VMEM((TR, HALF), F32),
                        pltpu.VMEM((TR, HALF), F32), pltpu.VMEM((TR, HALF), F32)],
        compiler_params=_params(("arbitrary",), VMEM_BIG), name="odd_bwd_a",
    )(z1, z1, z1, z1, dc, dycat, sgu_g, sgu_b, sgu_w, sgu_bb, cn_g, cn_b)


def _odd_bwd_b(z1, ddc, dz1, conv_w):
    nt = S // TR

    def body(dval_ref, dglu_ref, hval_ref, hglu_ref, ddc_ref, hddc_ref, cw_ref, dz_in_ref,
             dz_ref, dcw_ref, xw, dwin, dxs, xr, dr):
        del dz_in_ref
        i, j = pl.program_id(0), pl.program_id(1)
        sg = _sigmoid(dglu_ref[...])
        dval = dval_ref[...]

        @pl.when(j == 0)
        def _():
            halo = hval_ref[...] * _sigmoid(hglu_ref[...])
            xw[0:HALO, :] = jnp.where(i > 0, halo, 0.0)
            xw[HALO:HALO + TR, :] = dval * sg
            dwin[0:TR, :] = ddc_ref[...]
            dwin[TR:TR + HALO, :] = jnp.where(i < nt - 1, hddc_ref[...], 0.0)
            _shifted_copies(xr, xw)
            _shifted_copies(dr, dwin)

            @pl.when(i == 0)
            def _():
                dcw_ref[...] = jnp.zeros_like(dcw_ref)

            for rb in range(TR // SUB):
                acc = jnp.zeros((SUB, HALF), F32)
                for k in range(CONV_K):
                    acc = acc + cw_ref[k:k + 1, :] * _rows_at(dr, rb * SUB + (CONV_K - 1) - k, SUB)
                dxs[rb * SUB:(rb + 1) * SUB, :] = acc
            for k in range(CONV_K):
                acc = jnp.zeros((SUB, HALF), F32)
                for rb in range(TR // SUB):
                    acc = acc + dwin[rb * SUB:(rb + 1) * SUB, :] * _rows_at(xr, rb * SUB + HALO - (CONV_K - 1) + k, SUB)
                dcw_ref[k:k + 1, :] += jnp.sum(acc, axis=0, keepdims=True)
            dz_ref[...] = (dxs[...] * sg).astype(BF16)

        @pl.when(j == 1)
        def _():
            dz_ref[...] = (dxs[...] * dval * sg * (1.0 - sg)).astype(BF16)

    col = lambda c: pl.BlockSpec((TR, HALF), lambda i, j: (i, c))
    prev = lambda c: pl.BlockSpec((HALO, HALF), lambda i, j: (jnp.maximum(i * (TR // HALO) - 1, 0), c))
    nxt = pl.BlockSpec((HALO, HALF), lambda i, j: (jnp.minimum((i + 1) * (TR // HALO), S // HALO - 1), 0))
    return pl.pallas_call(
        body, grid=(nt, 2),
        in_specs=[col(3), col(4), prev(3), prev(4), pl.BlockSpec((TR, HALF), lambda i, j: (i, 0)), nxt,
                  _full_spec((HALO, HALF)), pl.BlockSpec(memory_space=pl.ANY)],
        out_specs=[pl.BlockSpec((TR, HALF), lambda i, j: (i, 3 + j)), _full_spec((HALO, HALF))],
        out_shape=[jax.ShapeDtypeStruct((S, ODD_IN), BF16), jax.ShapeDtypeStruct((HALO, HALF), F32)],
        scratch_shapes=[pltpu.VMEM((WIN, HALF), F32), pltpu.VMEM((WIN, HALF), F32), pltpu.VMEM((TR, HALF), F32),
                        pltpu.VMEM((SUBL, WIN, HALF), F32), pltpu.VMEM((SUBL, WIN, HALF), F32)],
        input_output_aliases={7: 0},
        compiler_params=_params(("arbitrary", "arbitrary"), VMEM_BIG), name="odd_bwd_b",
    )(z1, z1, z1, z1, ddc, ddc, conv_w, dz1)


def _conv_out_grad(dc, dyd, dgate, cng, cnb):
    dhat, rd = _ln_stats(dc)
    silu_n, dsilu_n = _silu_and_grad(dhat * cng + cnb)
    silu_g, dsilu_g = _silu_and_grad(dgate)
    ddn = dyd * silu_g * dsilu_n
    return _ln_bwd(ddn, dhat, rd, cng), ddn, dhat, dyd * silu_n * dsilu_g


def _odd_bwd(z1, dc, dycat, sgu_g, sgu_b, sgu_w, sgu_bb, conv_w, cn_g, cn_b):
    nt = S // TR

    def body(u_ref, v_ref, cg_ref, dval_ref, dglu_ref, dgate_ref, hval_ref, hglu_ref, dcs, dy_ref,
             ndc_ref, ndy_ref, ngate_ref, g_ref, b_ref, w_ref, bb_ref, cw_ref, cng_ref, cnb_ref,
             dz_ref, dw_ref, dbb_ref, dg_ref, db_ref, dcng_ref, dcnb_ref, dcb_ref, dcw_ref,
             vn_s, s_s, ds_s, dvn_s, xw, dwin, dxs, xr, dr):
        i = pl.program_id(0)
        vhat, rv = _ln_stats(v_ref[...])
        g = g_ref[...]
        vn_s[...] = (vhat * g + b_ref[...]).astype(BF16)
        _sgu_gate(vn_s, s_s, w_ref, bb_ref)
        silu_c, dsilu_c = _silu_and_grad(cg_ref[...])
        dyc = dy_ref[:, 0:HALF]
        u = u_ref[...]
        s = s_s[...]
        dz_ref[:, 0:HALF] = (dyc * s * silu_c).astype(BF16)
        dz_ref[:, 2 * HALF:3 * HALF] = (dyc * u * s * dsilu_c).astype(BF16)
        ds_s[...] = dyc * u * silu_c

        @pl.when(i == 0)
        def _():
            dw_ref[...] = jnp.zeros_like(dw_ref)
            dbb_ref[...] = jnp.zeros_like(dbb_ref)
            dcw_ref[...] = jnp.zeros_like(dcw_ref)

        tril = lax.broadcasted_iota(jnp.int32, (128, 128), 0) >= lax.broadcasted_iota(jnp.int32, (128, 128), 1)
        for h in range(4):
            wm = _tril_bf16(w_ref[h])
            for ch in range(NCHUNK):
                rows, cols = slice(ch * 128, (ch + 1) * 128), slice(h * SGU_CH, (h + 1) * SGU_CH)
                ds = ds_s[rows, cols]
                dsb = ds.astype(BF16)
                dw_ref[h] += jnp.where(tril, _dot(dsb, vn_s[rows, cols], NT), 0.0)
                dbb_ref[h] += jnp.broadcast_to(jnp.sum(ds, axis=1, keepdims=True), (128, 128))
                dvn_s[rows, cols] = _dot(wm, dsb, TN)
        dvn = dvn_s[...]
        _acc_rows(dg_ref, dvn * vhat, i)
        _acc_rows(db_ref, dvn, i)
        dz_ref[:, HALF:2 * HALF] = _ln_bwd(dvn, vhat, rv, g).astype(BF16)

        cng, cnb = cng_ref[...], cnb_ref[...]
        ddc, ddn, dhat, ddgate = _conv_out_grad(dcs[...], dy_ref[:, HALF:2 * HALF], dgate_ref[...], cng, cnb)
        dz_ref[:, 5 * HALF:6 * HALF] = ddgate.astype(BF16)
        _acc_rows(dcng_ref, ddn * dhat, i)
        _acc_rows(dcnb_ref, ddn, i)
        _acc_rows(dcb_ref, ddc, i)
        ddc_next = _conv_out_grad(ndc_ref[...], ndy_ref[:, HALF:2 * HALF], ngate_ref[...], cng, cnb)[0]

        sg = _sigmoid(dglu_ref[...])
        dval = dval_ref[...]
        halo = hval_ref[...] * _sigmoid(hglu_ref[...])
        xw[0:HALO, :] = jnp.where(i > 0, halo, 0.0)
        xw[HALO:HALO + TR, :] = dval * sg
        dwin[0:TR, :] = ddc
        dwin[TR:TR + HALO, :] = jnp.where(i < nt - 1, ddc_next, 0.0)
        _shifted_copies(xr, xw)
        _shifted_copies(dr, dwin)
        for rb in range(TR // SUB):
            acc = jnp.zeros((SUB, HALF), F32)
            for k in range(CONV_K):
                acc = acc + cw_ref[k:k + 1, :] * _rows_at(dr, rb * SUB + (CONV_K - 1) - k, SUB)
            dxs[rb * SUB:(rb + 1) * SUB, :] = acc
        for k in range(CONV_K):
            acc = jnp.zeros((SUB, HALF), F32)
            for rb in range(TR // SUB):
                acc = acc + dwin[rb * SUB:(rb + 1) * SUB, :] * _rows_at(xr, rb * SUB + HALO - (CONV_K - 1) + k, SUB)
            dcw_ref[k:k + 1, :] += jnp.sum(acc, axis=0, keepdims=True)
        dx = dxs[...]
        dz_ref[:, 3 * HALF:4 * HALF] = (dx * sg).astype(BF16)
        dz_ref[:, 4 * HALF:5 * HALF] = (dx * dval * sg * (1.0 - sg)).astype(BF16)

    vec = _full_spec((1, HALF))
    sq = _full_spec((4, 128, 128))
    col = lambda j: pl.BlockSpec((TR, HALF), lambda i: (i, j))
    prev = lambda j: pl.BlockSpec((HALO, HALF), lambda i: (jnp.maximum(i * (TR // HALO) - 1, 0), j))
    nxt_row = lambda i: jnp.minimum((i + 1) * (TR // HALO), S // HALO - 1)
    return pl.pallas_call(
        body, grid=(nt,),
        in_specs=[col(0), col(1), col(2), col(3), col(4), col(5), prev(3), prev(4), col(0),
                  pl.BlockSpec((TR, 2048), lambda i: (i, 0)),
                  pl.BlockSpec((HALO, HALF), lambda i: (nxt_row(i), 0)), pl.BlockSpec((HALO, 2048), lambda i: (nxt_row(i), 0)),
                  pl.BlockSpec((HALO, HALF), lambda i: (nxt_row(i), 5)),
                  vec, vec, sq, sq, _full_spec((HALO, HALF)), vec, vec],
        out_specs=[pl.BlockSpec((TR, ODD_IN), lambda i: (i, 0)), sq, sq, vec, vec, vec, vec, vec,
                   _full_spec((HALO, HALF))],
        out_shape=[jax.ShapeDtypeStruct((S, ODD_IN), BF16), jax.ShapeDtypeStruct((4, 128, 128), F32),
                   jax.ShapeDtypeStruct((4, 128, 128), F32)] + [jax.ShapeDtypeStruct((1, HALF), F32)] * 5
                  + [jax.ShapeDtypeStruct((HALO, HALF), F32)],
        scratch_shapes=[pltpu.VMEM((TR, HALF), BF16), pltpu.VMEM((TR, HALF), F32), pltpu.VMEM((TR, HALF), F32),
                        pltpu.VMEM((TR, HALF), F32), pltpu.VMEM((WIN, HALF), F32), pltpu.VMEM((WIN, HALF), F32),
                        pltpu.VMEM((TR, HALF), F32), pltpu.VMEM((SUBL, WIN, HALF), F32),
                        pltpu.VMEM((SUBL, WIN, HALF), F32)],
        compiler_params=_params(("arbitrary",), VMEM_BIG), name="odd_bwd",
    )(z1, z1, z1, z1, z1, z1, z1, z1, dc, dycat, dc, dycat, z1,
      sgu_g, sgu_b, sgu_w, sgu_bb, conv_w, cn_g, cn_b)


def _cast_bf16(w, name, piece=0, npieces=1):
    r, c = w.shape[0], w.shape[1] // npieces
    tr = min(r, 256)

    def body(i_ref, o_ref):
        o_ref[...] = i_ref[...].astype(BF16)

    return pl.pallas_call(
        body, grid=(r // tr,), in_specs=[pl.BlockSpec((tr, c), lambda i: (i, piece))],
        out_specs=pl.BlockSpec((tr, c), lambda i: (i, 0)), out_shape=jax.ShapeDtypeStruct((r, c), BF16),
        compiler_params=_params(("parallel",)), name=name,
    )(w)


def _adamw(w, g, m, v):
    m = ADAM_B1 * m + (1.0 - ADAM_B1) * g
    v = ADAM_B2 * v + (1.0 - ADAM_B2) * (g * g)
    m_hat = m / (1.0 - ADAM_B1 ** ADAM_STEP)
    v_hat = v / (1.0 - ADAM_B2 ** ADAM_STEP)
    delta = -ADAM_LR * (m_hat / (jnp.sqrt(v_hat) + ADAM_EPS) + ADAM_WD * w)
    return delta, m, v


def _adam_reduce(parts, w, m, v, name, dep=None, piece=0, npieces=1, prev=None):
    r, c = w.shape
    cp = c // npieces
    tr = min(r, 128)
    extra = ([] if dep is None else [dep]) + ([] if prev is None else list(prev))
    nparts = parts.shape[0]

    def body(p_ref, w_ref, m_ref, v_ref, *rest):
        g_ref, d_ref, nm_ref, nv_ref = rest[len(extra):]
        g = p_ref[0].astype(F32)
        for d in range(1, nparts):
            g = g + p_ref[d].astype(F32)
        g_ref[...] = g
        d_ref[...], nm_ref[...], nv_ref[...] = _adamw(w_ref[...], g, m_ref[...], v_ref[...])

    spec = pl.BlockSpec((tr, cp), lambda i: (i, piece))
    first = 4 + (0 if dep is None else 1)
    return pl.pallas_call(
        body, grid=(r // tr,),
        in_specs=[pl.BlockSpec((nparts, tr, cp), lambda i: (0, i, 0)), spec, spec, spec] + [ANY_SPEC] * len(extra),
        out_specs=[spec] * 4, out_shape=[jax.ShapeDtypeStruct((r, c), F32)] * 4,
        input_output_aliases={} if prev is None else {first + k: k for k in range(4)},
        compiler_params=_params(("parallel",), VMEM_BIG), name=name,
    )(parts, w, m, v, *extra)


def _arrived(x, name, dep=None):
    deps = [] if dep is None else [dep]

    def body(*refs):
        refs[-1][...] = jnp.zeros_like(refs[-1])

    return pl.pallas_call(
        body, in_specs=[ANY_SPEC] * (1 + len(deps)), out_specs=pl.BlockSpec(memory_space=pltpu.VMEM),
        out_shape=jax.ShapeDtypeStruct((8, 128), F32), name=name,
    )(x, *deps)


def _sum_parts(parts, name, dep=None):
    r = parts.shape[1]
    tr = 8
    for cand in (512, 256, 128, 64, 32, 16, 8):
        if r % cand == 0:
            tr = cand
            break
    deps = [] if dep is None else [dep]

    def body(p_ref, *rest):
        g = p_ref[0]
        for d in range(1, NDEV):
            g = g + p_ref[d]
        rest[-1][...] = g

    return pl.pallas_call(
        body, grid=(r // tr,), in_specs=[pl.BlockSpec((NDEV, tr, 128), lambda i: (0, i, 0))] + [ANY_SPEC] * len(deps),
        out_specs=pl.BlockSpec((tr, 128), lambda i: (i, 0)), out_shape=jax.ShapeDtypeStruct((r, 128), F32),
        compiler_params=_params(("parallel",)), name=name,
    )(parts, *deps)


def _sum_unpack(parts, rows, name, dep=None):
    deps = [] if dep is None else [dep]

    def body(p_ref, *outs):
        outs = outs[len(deps):]
        off = 0
        for o_ref, n in zip(outs, rows):
            acc = p_ref[0, off:off + n, :]
            for d in range(1, NDEV):
                acc = acc + p_ref[d, off:off + n, :]
            o_ref[...] = acc
            off += n

    return pl.pallas_call(
        body, grid=(1,), in_specs=[pl.BlockSpec(parts.shape, lambda i: (0, 0, 0))] + [ANY_SPEC] * len(deps),
        out_specs=[pl.BlockSpec((n, 128), lambda i: (0, 0)) for n in rows],
        out_shape=[jax.ShapeDtypeStruct((n, 128), F32) for n in rows],
        compiler_params=_params(("arbitrary",), VMEM_BIG), name=name,
    )(parts, *deps)


def _adam_small(ws, gs, g_specs, ms, vs, name):
    n = len(ws)

    def body(*refs):
        w_r, g_r, m_r, v_r = refs[:n], refs[n:2 * n], refs[2 * n:3 * n], refs[3 * n:4 * n]
        outs = refs[4 * n:]
        for i in range(n):
            g = g_r[i][...]
            outs[4 * i][...] = g
            outs[4 * i + 1][...], outs[4 * i + 2][...], outs[4 * i + 3][...] = _adamw(
                w_r[i][...], g, m_r[i][...], v_r[i][...])

    whole = lambda a: pl.BlockSpec(a.shape, lambda i, nd=a.ndim: (0,) * nd)
    outs = pl.pallas_call(
        body, grid=(1,),
        in_specs=[whole(a) for a in ws] + list(g_specs) + [whole(a) for a in ms] + [whole(a) for a in vs],
        out_specs=[whole(a) for a in ws for _ in range(4)],
        out_shape=[jax.ShapeDtypeStruct(a.shape, F32) for a in ws for _ in range(4)],
        compiler_params=_params(("arbitrary",), VMEM_BIG), name=name,
    )(*ws, *gs, *ms, *vs)
    return [outs[4 * i:4 * i + 4] for i in range(n)]


MASKS = [(mx, my, mc) for mx in (0, 1) for my in (0, 1) for mc in (0, 1)][1:]


def _sc_exchange(name, collective_id, arrays, scatter):
    nt = len(arrays)
    out_type = [jax.ShapeDtypeStruct(a.shape if scatter else (NDEV,) + a.shape, a.dtype) for a in arrays]

    def body(*refs):
        ins, outs = refs[:nt], refs[nt:2 * nt]
        send_sems, recv_sems, local_sems = refs[2 * nt:3 * nt], refs[3 * nt:4 * nt], refs[4 * nt:5 * nt]
        x, y, c = lax.axis_index("x"), lax.axis_index("y"), lax.axis_index("c")
        peers = [(mx + x - 2 * mx * x, my + y - 2 * my * y, mc + c - 2 * mc * c) for mx, my, mc in MASKS]
        barrier = pltpu.get_barrier_semaphore()
        for peer in peers:
            pl.semaphore_signal(barrier, inc=1, device_id=peer, device_id_type=MESH)
        pl.semaphore_wait(barrier, len(peers))
        me = 4 * x + 2 * y + c
        own = []
        for t in range(nt):
            cp = pltpu.make_async_copy(ins[t].at[me] if scatter else ins[t], outs[t].at[me], local_sems[t])
            cp.start()
            own.append(cp)
            for px, py, pc in peers:
                src = ins[t].at[4 * px + 2 * py + pc] if scatter else ins[t]
                pltpu.make_async_remote_copy(src_ref=src, dst_ref=outs[t].at[me], send_sem=send_sems[t],
                                             recv_sem=recv_sems[t], device_id=(px, py, pc), device_id_type=MESH).start()
        for t in range(nt):
            own[t].wait()
            seven = outs[t].at[pl.ds(0, NDEV - 1)]
            drain = pltpu.make_async_remote_copy(src_ref=seven, dst_ref=seven, send_sem=send_sems[t],
                                                 recv_sem=recv_sems[t], device_id=(x, y, c), device_id_type=MESH)
            drain.wait_send()
            drain.wait_recv()

    return pl.kernel(
        body, out_type=out_type, mesh=plsc.ScalarSubcoreMesh(axis_name="sequencer", num_cores=1),
        scratch_types=[pltpu.SemaphoreType.DMA] * (3 * nt),
        compiler_params=pltpu.CompilerParams(collective_id=collective_id), name=name,
    )(*arrays)


def _sc_gather_two_level(name, collective_id, arrays):
    nt = len(arrays)
    out_type = [jax.ShapeDtypeStruct((NDEV,) + a.shape, a.dtype) for a in arrays]

    def body(*refs):
        ins, outs = refs[:nt], refs[nt:2 * nt]
        sems = refs[2 * nt:]
        send_sems, sib_sems, local_sems = sems[:nt], sems[nt:2 * nt], sems[2 * nt:3 * nt]
        ici_sems = [sems[3 * nt + 3 * t:3 * nt + 3 * t + 3] for t in range(nt)]
        x, y, c = lax.axis_index("x"), lax.axis_index("y"), lax.axis_index("c")
        sibling = (x, y, 1 - c)
        chips = [(1 - x, y), (x, 1 - y), (1 - x, 1 - y)]
        barrier = pltpu.get_barrier_semaphore()
        for peer in [sibling] + [(cx, cy, c) for cx, cy in chips]:
            pl.semaphore_signal(barrier, inc=1, device_id=peer, device_id_type=MESH)
        pl.semaphore_wait(barrier, 4)
        me = 4 * x + 2 * y + c

        def push(t, src, slot, recv_sem, to):
            pltpu.make_async_remote_copy(src_ref=src, dst_ref=outs[t].at[slot], send_sem=send_sems[t],
                                         recv_sem=recv_sem, device_id=to, device_id_type=MESH).start()

        own = []
        for t in range(nt):
            cp = pltpu.make_async_copy(ins[t], outs[t].at[me], local_sems[t])
            cp.start()
            own.append(cp)
            for j, (cx, cy) in enumerate(chips):
                push(t, ins[t], me, ici_sems[t][j], (cx, cy, c))
            push(t, ins[t], me, sib_sems[t], sibling)
        for t in range(nt):
            for j, (cx, cy) in enumerate(chips):
                slot = 4 * cx + 2 * cy + c
                landed = outs[t].at[slot]
                pltpu.make_async_remote_copy(src_ref=landed, dst_ref=landed, send_sem=send_sems[t],
                                             recv_sem=ici_sems[t][j], device_id=(cx, cy, c),
                                             device_id_type=MESH).wait_recv()
                push(t, landed, slot, sib_sems[t], sibling)
        for t in range(nt):
            own[t].wait()
            four, seven = outs[t].at[pl.ds(0, 4)], outs[t].at[pl.ds(0, 7)]
            pltpu.make_async_remote_copy(src_ref=four, dst_ref=four, send_sem=send_sems[t], recv_sem=sib_sems[t],
                                         device_id=sibling, device_id_type=MESH).wait_recv()
            pltpu.make_async_remote_copy(src_ref=seven, dst_ref=seven, send_sem=send_sems[t], recv_sem=sib_sems[t],
                                         device_id=sibling, device_id_type=MESH).wait_send()

    return pl.kernel(
        body, out_type=out_type, mesh=plsc.ScalarSubcoreMesh(axis_name="sequencer", num_cores=1),
        scratch_types=[pltpu.SemaphoreType.DMA] * (6 * nt),
        compiler_params=pltpu.CompilerParams(collective_id=collective_id), name=name,
    )(*arrays)


def _sc_sibling_exchange(name, collective_id, src, out_shape, pieces, after=None):
    extra = [] if after is None else [after]

    def body(src_ref, *rest):
        out_ref, send_sem, recv_sem = rest[len(extra):]
        x, y, c = lax.axis_index("x"), lax.axis_index("y"), lax.axis_index("c")
        sibling = (x, y, 1 - c)
        barrier = pltpu.get_barrier_semaphore()
        pl.semaphore_signal(barrier, inc=1, device_id=sibling, device_id_type=MESH)
        pl.semaphore_wait(barrier, 1)
        for piece, lands in pieces(c, src_ref, out_ref):
            pltpu.make_async_remote_copy(src_ref=piece, dst_ref=lands, send_sem=send_sem, recv_sem=recv_sem,
                                         device_id=sibling, device_id_type=MESH).start()
        drain = pltpu.make_async_remote_copy(src_ref=out_ref, dst_ref=out_ref, send_sem=send_sem, recv_sem=recv_sem,
                                             device_id=sibling, device_id_type=MESH)
        drain.wait_send()
        drain.wait_recv()

    return pl.kernel(
        body, out_type=jax.ShapeDtypeStruct(out_shape, src.dtype),
        mesh=plsc.ScalarSubcoreMesh(axis_name="sequencer", num_cores=1), scratch_types=[pltpu.SemaphoreType.DMA] * 2,
        compiler_params=pltpu.CompilerParams(collective_id=collective_id), name=name,
    )(src, *extra)


def _swap_class_columns(name, collective_id, dz, nb, piece=0, npieces=1):
    w = nb // npieces
    return _sc_sibling_exchange(
        name, collective_id, dz, (S, 4 * w),
        lambda c, src, out: [(src.at[:, pl.ds((2 * j + 1 - c) * nb + piece * w, w)], out.at[:, pl.ds(j * w, w)])
                             for j in range(4)])


def _sc_chip_scatter(name, collective_id, q):
    def body(q_ref, out_ref, send_sem, recv_sem, local_sem):
        x, y, c = lax.axis_index("x"), lax.axis_index("y"), lax.axis_index("c")
        chips = [(1 - x, y), (x, 1 - y), (1 - x, 1 - y)]
        barrier = pltpu.get_barrier_semaphore()
        for cx, cy in chips:
            pl.semaphore_signal(barrier, inc=1, device_id=(cx, cy, c), device_id_type=MESH)
        pl.semaphore_wait(barrier, 3)
        mine = 2 * x + y
        own = pltpu.make_async_copy(q_ref.at[mine], out_ref.at[mine], local_sem)
        own.start()
        for cx, cy in chips:
            pltpu.make_async_remote_copy(src_ref=q_ref.at[2 * cx + cy], dst_ref=out_ref.at[mine], send_sem=send_sem,
                                         recv_sem=recv_sem, device_id=(cx, cy, c), device_id_type=MESH).start()
        own.wait()
        three = out_ref.at[pl.ds(0, 3)]
        drain = pltpu.make_async_remote_copy(src_ref=three, dst_ref=three, send_sem=send_sem, recv_sem=recv_sem,
                                             device_id=(x, y, c), device_id_type=MESH)
        drain.wait_send()
        drain.wait_recv()

    return pl.kernel(
        body, out_type=jax.ShapeDtypeStruct(q.shape, q.dtype),
        mesh=plsc.ScalarSubcoreMesh(axis_name="sequencer", num_cores=1), scratch_types=[pltpu.SemaphoreType.DMA] * 3,
        compiler_params=pltpu.CompilerParams(collective_id=collective_id), name=name,
    )(q)


def _mm_pair_dw(h_own, dz, h_sib, dz_sib, nb, name, dep=None, piece=0, npieces=1, h_transposed=False,
                one_call=False):
    nb = nb // npieces
    tn = 512 if nb % 512 == 0 else nb
    per = nb // tn
    dn = NN if h_transposed else TN
    o_spec = pl.BlockSpec((None, D, tn), lambda i, j, k: (j // per, 0, j % per))
    own_col = lambda i, j, k: (0, ((2 * (j // per) + lax.axis_index("c")) * npieces + piece) * per + j % per)
    if one_call:
        def fused(a0_ref, b0_ref, a1_ref, b1_ref, dep_ref, o_ref):
            acc = _dot(a0_ref[...], b0_ref[...], dn) + _dot(a1_ref[...], b1_ref[...], dn)
            o_ref[...] = acc.astype(BF16)

        whole = pl.BlockSpec((S, D), lambda i, j, k: (0, 0), pipeline_mode=pl.Buffered(1))
        return pl.pallas_call(
            fused, grid=(1, 4 * per, 1),
            in_specs=[whole, pl.BlockSpec((S, tn), own_col), whole, pl.BlockSpec((S, tn), lambda i, j, k: (0, j)),
                      ANY_SPEC],
            out_specs=o_spec, out_shape=jax.ShapeDtypeStruct((4, D, nb), BF16),
            compiler_params=_params(("parallel", "parallel", "arbitrary"), VMEM_BIG), name=name,
        )(h_own, dz, h_sib, dz_sib, dep)
    part = _matmul(
        h_own, dz, dn=dn, grid=(1, 4 * per, 1),
        a_spec=pl.BlockSpec((S, D), lambda i, j, k: (0, 0)), b_spec=pl.BlockSpec((S, tn), own_col),
        o_spec=o_spec, out_shape=(4, D, nb), out_dtype=F32, acc_shape=(D, tn), name=name + "_own", dep=dep)

    def body(a_ref, b_ref, p_ref, o_ref):
        o_ref[...] = (p_ref[...] + _dot(a_ref[...], b_ref[...], dn)).astype(BF16)

    return pl.pallas_call(
        body, grid=(1, 4 * per, 1),
        in_specs=[pl.BlockSpec((S, D), lambda i, j, k: (0, 0)), pl.BlockSpec((S, tn), lambda i, j, k: (0, j)), o_spec],
        out_specs=o_spec, out_shape=jax.ShapeDtypeStruct((4, D, nb), BF16),
        compiler_params=_params(("parallel", "parallel", "arbitrary"), VMEM_BIG), name=name + "_sibling",
    )(h_sib, dz_sib, part)


SMALL = {
    "e_pre_norm": ((2048,), None), "e_pool_w": ((4, 256, 256), 1), "e_pool_scale": ((1024,), None),
    "e_post_norm": ((2048,), None), "o_pre_norm": ((2048,), 0), "o_sgu_norm_g": ((1024,), 0),
    "o_sgu_norm_b": ((1024,), 0), "o_sgu_w": ((4, 128, 128), None), "o_sgu_b": ((4, 128), None),
    "o_conv_w": ((31, 1024), 1), "o_conv_b": ((1024,), 0), "o_conv_norm_g": ((1024,), 0),
    "o_conv_norm_b": ((1024,), 0), "o_post_norm": ((2048,), 0),
}
SMALL_SHARDED = [n for n, (_, ax) in SMALL.items() if ax is not None]


def _shard_shape(name):
    shape, ax = SMALL[name]
    if ax is None:
        return shape
    return tuple(s // NDEV if i == ax else s for i, s in enumerate(shape))


def _pack(arrs, row_multiple=1):
    flat = jnp.concatenate([a.reshape(-1) for a in arrs])
    pad = -flat.shape[0] % (128 * row_multiple)
    return jnp.concatenate([flat, jnp.zeros((pad,), F32)]).reshape(-1, 128)


def _small_views(name):
    shape, ax = SMALL[name]
    me = lambda: 4 * lax.axis_index("x") + 2 * lax.axis_index("y") + lax.axis_index("c")
    if ax is None:
        view = (int(np.prod(shape)) // 128, 128)
        return view, view, pl.BlockSpec(view, lambda i: (0, 0))
    if len(shape) == 1:
        n = shape[0] // NDEV
        return (1, n), (NDEV, 1, n), pl.BlockSpec((None, 1, n), lambda i: (me(), 0, 0))
    part = _shard_shape(name)
    return part, shape, pl.BlockSpec(part, lambda i: tuple(me() if d == ax else 0 for d in range(len(shape))))


WEIGHTS = ["e_pre_norm", "e_w_in", "e_pool_w", "e_pool_scale", "e_w_out", "e_post_norm", "o_pre_norm", "o_w_in",
           "o_sgu_norm_g", "o_sgu_norm_b", "o_sgu_w", "o_sgu_b", "o_conv_w", "o_conv_b", "o_conv_norm_g",
           "o_conv_norm_b", "o_w_out", "o_post_norm"]


def kernel(x, e_pre_norm, e_w_in, e_pool_w, e_pool_scale, e_w_out, e_post_norm, o_pre_norm, o_w_in, o_sgu_norm_g, o_sgu_norm_b, o_sgu_w, o_sgu_b, o_conv_w, o_conv_b, o_conv_norm_g, o_conv_norm_b, o_w_out, o_post_norm, loss_target, m_e_pre_norm, m_e_w_in, m_e_pool_w, m_e_pool_scale, m_e_w_out, m_e_post_norm, m_o_pre_norm, m_o_w_in, m_o_sgu_norm_g, m_o_sgu_norm_b, m_o_sgu_w, m_o_sgu_b, m_o_conv_w, m_o_conv_b, m_o_conv_norm_g, m_o_conv_norm_b, m_o_w_out, m_o_post_norm, v_e_pre_norm, v_e_w_in, v_e_pool_w, v_e_pool_scale, v_e_w_out, v_e_post_norm, v_o_pre_norm, v_o_w_in, v_o_sgu_norm_g, v_o_sgu_norm_b, v_o_sgu_w, v_o_sgu_b, v_o_conv_w, v_o_conv_b, v_o_conv_norm_g, v_o_conv_norm_b, v_o_w_out, v_o_post_norm):
    given = dict(locals())
    w = {n: given[n][0] for n in WEIGHTS}
    m = {n: given["m_" + n][0] for n in WEIGHTS}
    v = {n: given["v_" + n][0] for n in WEIGHTS}
    me = 4 * lax.axis_index("x") + 2 * lax.axis_index("y") + lax.axis_index("c")
    x, target = x[0], loss_target[0]
    row = lambda a: a.reshape(1, -1)

    lo, small_rows = _sc_gather_two_level(
        "gather_a0", 0, [_cast_bf16(w["e_w_in"], "cast_e_w_in_0", 0, 2), _pack([w[n] for n in SMALL_SHARDED])])
    hi, = _sc_gather_two_level("gather_a1", 12, [_cast_bf16(w["e_w_in"], "cast_e_w_in_1", 1, 2)])
    wg_e_in = (lo, hi)
    h0, h0t = _pre0_fwd(x, row(w["e_pre_norm"]))
    wg_e_out, = _sc_gather_two_level("gather_b", 1, [_cast_bf16(w["e_w_out"], "cast_e_w_out")])
    wg_o_in, = _sc_gather_two_level("gather_c", 13, [_cast_bf16(w["o_w_in"], "cast_o_w_in")])
    wg_o_out, = _sc_gather_two_level("gather_d", 16, [_cast_bf16(w["o_w_out"], "cast_o_w_out")])
    p = {n: w[n] for n in SMALL if SMALL[n][1] is None}
    small_rows = small_rows.reshape(NDEV, -1)
    off = 0
    for n in SMALL_SHARDED:
        shp, ax = _shard_shape(n), SMALL[n][1]
        cnt = int(np.prod(shp))
        blk = small_rows[:, off:off + cnt].reshape((NDEV,) + shp)
        p[n] = jnp.moveaxis(blk, 0, ax).reshape(SMALL[n][0])
        off += cnt
    tabs = _rope_tables()
    pool_w_bf = p["e_pool_w"].astype(BF16)
    sgu_bb = jnp.broadcast_to(p["o_sgu_b"][:, :, None], (4, 128, 128))
    conv_w = jnp.concatenate([p["o_conv_w"], jnp.zeros((HALO - CONV_K, HALF), F32)], axis=0)
    odd_p = (row(p["o_sgu_norm_g"]), row(p["o_sgu_norm_b"]), p["o_sgu_w"], sgu_bb, conv_w,
             row(p["o_conv_b"]), row(p["o_conv_norm_g"]), row(p["o_conv_norm_b"]))

    z0 = _mm_in_halves(h0, wg_e_in, "mm_z0")
    ycat0 = _pool_fwd(z0, pool_w_bf, row(p["e_pool_scale"]))
    ycat0, og, lg, qkv = _attn_fwd(z0, tabs, ycat0)
    w_out_e, w_out_o = wg_e_out.reshape(2048, D), wg_o_out.reshape(2048, D)
    y0, x1, h1 = _post0_fwd(ycat0, w_out_e, x, row(p["e_post_norm"]), row(p["o_pre_norm"]), wg_e_out)
    h0t_sib = _sc_sibling_exchange("swap_h0", 8, h0t, h0t.shape, lambda c, src, out: [(src, out)], h1)
    h1_sib = _sc_sibling_exchange("swap_h1", 11, h1, h1.shape, lambda c, src, out: [(src, out)])
    z1 = _mm_in(h1, wg_o_in, "mm_z1")
    ycat1, conv_out = _odd_fwd(z1, *odd_p)

    g = {}
    loss_part, dx2, dy1, g["o_post_norm"] = _post1_bwd(ycat1, w_out_o, x1, target, row(p["o_post_norm"]),
                                                       _arrived(h1_sib, "arrived_h_sib", h0t_sib))
    parts = {}
    dw = _mm_out_dw(ycat1, dy1, "mm_dwout1").reshape(NDEV, 256, D)
    parts["o_w_out"], = _sc_exchange("scatter_o_w_out", 2, [dw], True)
    dycat1 = _mm_out_dx(dy1, w_out_o, "mm_dycat1", dw)
    dz1, g["o_sgu_w"], d_sgu_bb, g["o_sgu_norm_g"], g["o_sgu_norm_b"], g["o_conv_norm_g"], g["o_conv_norm_b"], \
        g["o_conv_b"], d_conv_w = _odd_bwd(z1, conv_out, dycat1, *odd_p[:4], conv_w, *odd_p[6:])
    g["o_sgu_b"] = d_sgu_bb[:, :, 0]
    g["o_conv_w"] = d_conv_w[:CONV_K]
    grads, deltas, new_m, new_v = {}, {}, {}, {}

    def adam(n, dep):
        grads[n], deltas[n], new_m[n], new_v[n] = _adam_reduce(parts[n], w[n], m[n], v[n], "adam_" + n, dep)
        return new_v[n]

    pin = _arrived(parts["o_w_out"], "arrived_o_w_out", d_conv_w)
    dz1_sib = _swap_class_columns("swap_dz1", 10, dz1, ODD_IN // NDEV)
    dw = _mm_pair_dw(h1, dz1, h1_sib, dz1_sib, ODD_IN // NDEV, "mm_dwin1", pin)
    parts["o_w_in"] = _sc_chip_scatter("scatter_o_w_in", 3, dw)
    dh1 = _mm_in_dx(dz1, wg_o_in, "mm_dh1", dw)
    dx1, dy0, g["o_pre_norm"], g["e_post_norm"] = _mid_bwd(dx2, dh1, x1, y0, row(p["o_pre_norm"]),
                                                           row(p["e_post_norm"]))
    dw = _mm_out_dw(ycat0, dy0, "mm_dwout0").reshape(NDEV, 256, D)
    parts["e_w_out"], = _sc_exchange("scatter_e_w_out", 4, [dw], True)
    dycat0 = _mm_out_dx(dy0, w_out_e, "mm_dycat0", dw)
    da_in, da_gate, g["e_pool_w"], g["e_pool_scale"] = _pool_bwd(z0, dycat0, pool_w_bf, row(p["e_pool_scale"]))
    late = [n for n in SMALL if n not in ("e_pre_norm", "o_sgu_b")] + ["o_sgu_b"]
    pieces = [g[n].reshape(SMALL[n][0]) for n in late[:-1]] + [jnp.broadcast_to(loss_part, (8, 128)), g[late[-1]]]
    recv_small, = _sc_gather_two_level("gather_small_grads", 6, [_pack(pieces, 512)])
    took = _arrived(parts["o_w_in"], "arrived_o_w_in")
    dq, dk, dv, dbg = _attn_bwd(z0, qkv, og, lg, dycat0, tabs, took)
    dz0 = jnp.concatenate([da_in, da_gate, dq, dk, dv, dbg], axis=1)
    took = _arrived(recv_small, "arrived_small_grads", _arrived(parts["e_w_out"], "arrived_e_w_out", dz0))
    nb = EVEN_IN // NDEV
    swapped = [_swap_class_columns("swap_dz0_%d" % half, (9, 14)[half], dz0, nb, half, 2) for half in (0, 1)]
    dw, e_w_in_parts = took, []
    for half in (0, 1):
        dw = _mm_pair_dw(h0t, dz0, h0t_sib, swapped[half], nb, "mm_dwin0_%d" % half, dw, half, 2, True, half == 1)
        e_w_in_parts.append(_sc_chip_scatter("scatter_e_w_in_%d" % half, (5, 15)[half], dw))
    pin = adam("e_w_out", adam("o_w_out", adam("o_w_in", dw)))
    rows = [int(np.prod(SMALL[n][0])) // 128 for n in late]
    sums = _sum_unpack(recv_small, rows[:-1] + [8, rows[-1]], "sum_small_grads", pin)
    summed = dict(zip(late, sums[:-2] + sums[-1:]))
    loss = sums[-2][0, 0]
    dh0 = _mm_in_dx_halves(dz0, wg_e_in, "mm_dh0", summed[late[0]])
    grad_x, g["e_pre_norm"] = _pre0_bwd(dx1, dh0, x, row(p["e_pre_norm"]))
    last, = _sc_exchange("gather_e_pre_norm_grad", 7, [g["e_pre_norm"].reshape(16, 128)], False)

    n = "e_w_in"
    out = _adam_reduce(e_w_in_parts[0], w[n], m[n], v[n], "adam_e_w_in_0", grad_x, 0, 2)
    out = _adam_reduce(e_w_in_parts[1], w[n], m[n], v[n], "adam_e_w_in_1", None, 1, 2, out)
    grads[n], deltas[n], new_m[n], new_v[n] = out
    summed["e_pre_norm"] = _sum_parts(last, "sum_e_pre_norm_grad", out[3])
    names = list(SMALL)
    views = [_small_views(n) for n in names]
    mine = lambda src: [src[n].reshape(vw[0]) for n, vw in zip(names, views)]
    res = _adam_small(mine(w), [summed[n].reshape(vw[1]) for n, vw in zip(names, views)], [vw[2] for vw in views],
                      mine(m), mine(v), "adam_small")
    for n, out in zip(names, res):
        grads[n], deltas[n], new_m[n], new_v[n] = [t.reshape(_shard_shape(n)) for t in out]

    lead = lambda a: a[None]
    return (loss, grad_x[None], *[lead(grads[n]) for n in WEIGHTS], *[lead(deltas[n]) for n in WEIGHTS],
            *[lead(new_m[n]) for n in WEIGHTS], *[lead(new_v[n]) for n in WEIGHTS])
```

```python
import numpy as np
import jax
import jax.numpy as jnp
from jax import lax
from jax.experimental import pallas as pl
from jax.experimental.pallas import tpu as pltpu
from jax.experimental.pallas import tpu_sc as plsc

F32 = jnp.float32
BF16 = jnp.bfloat16

S = 2048
D = 2048
NDEV = 8
EPS = 1e-6
NEG = -1e30
HEAD_DIM = 128
ROT_DIM = 32
ROPE_THETA = 500000.0
PATTERNS = ((128, 1), (512, 4), (2048, 16))
BLK = 128
EVEN_IN = 12288
ODD_IN = 6144
HALF = 1024
CONV_K = 31
HALO = 32
TR = 256
SUB = 16

ADAM_LR = 0.001
ADAM_B1 = 0.9
ADAM_B2 = 0.999
ADAM_EPS = 1e-08
ADAM_WD = 0.01
ADAM_STEP = 10

VMEM_BIG = 56 * 1024 * 1024
MESH = pl.DeviceIdType.MESH

NN = (((1,), (0,)), ((), ()))
NT = (((1,), (1,)), ((), ()))
TN = (((0,), (0,)), ((), ()))


def _dot(a, b, dn=NN):
    return lax.dot_general(a, b, dn, preferred_element_type=F32)


def _sigmoid(x):
    return 1.0 / (1.0 + jnp.exp(-x))


def _silu_and_grad(x):
    sg = _sigmoid(x)
    return x * sg, sg * (1.0 + x * (1.0 - sg))


def _params(sem, vmem=None):
    return pltpu.CompilerParams(dimension_semantics=sem, vmem_limit_bytes=vmem)


ANY_SPEC = pl.BlockSpec(memory_space=pl.ANY)


def _matmul(a, b, *, dn, grid, a_spec, b_spec, o_spec, out_shape, out_dtype, acc_shape, name, dep=None):
    nk = grid[2]
    deps = [] if dep is None else list(dep) if isinstance(dep, (tuple, list)) else [dep]

    def body(a_ref, b_ref, *rest):
        o_ref, acc = rest[len(deps)], rest[len(deps) + 1:]
        if nk == 1:
            o_ref[...] = _dot(a_ref[...], b_ref[...], dn).astype(o_ref.dtype)
            return
        acc_ref = acc[0]
        k = pl.program_id(2)

        @pl.when(k == 0)
        def _():
            acc_ref[...] = jnp.zeros_like(acc_ref)

        acc_ref[...] += _dot(a_ref[...], b_ref[...], dn)

        @pl.when(k == nk - 1)
        def _():
            o_ref[...] = acc_ref[...].astype(o_ref.dtype)

    return pl.pallas_call(
        body, grid=grid, in_specs=[a_spec, b_spec] + [ANY_SPEC] * len(deps), out_specs=o_spec,
        out_shape=jax.ShapeDtypeStruct(out_shape, out_dtype),
        scratch_shapes=[] if nk == 1 else [pltpu.VMEM(acc_shape, F32)],
        compiler_params=_params(("parallel", "parallel", "arbitrary"), VMEM_BIG), name=name,
    )(a, b, *deps)


TM = 2048


def _mm_in(h, wg, name):
    nb = wg.shape[2]
    tn = 512 if nb % 512 == 0 else nb
    per = nb // tn
    return _matmul(
        h, wg, dn=NN, grid=(S // TM, NDEV * per, 1),
        a_spec=pl.BlockSpec((TM, D), lambda i, j, k: (i, 0)),
        b_spec=pl.BlockSpec((None, D, tn), lambda i, j, k: (j // per, 0, j % per)),
        o_spec=pl.BlockSpec((TM, tn), lambda i, j, k: (i, j)),
        out_shape=(S, NDEV * nb), out_dtype=F32, acc_shape=(TM, tn), name=name)


def _mm_in_halves(h, wg_halves, name):
    hb = wg_halves[0].shape[2]
    z = None
    for half, wg in enumerate(wg_halves):
        prev = [] if z is None else [z]

        def body(a_ref, b_ref, *rest):
            rest[-1][...] = _dot(a_ref[...], b_ref[...])

        z = pl.pallas_call(
            body, grid=(NDEV,),
            in_specs=[pl.BlockSpec((S, D), lambda j: (0, 0)), pl.BlockSpec((None, D, hb), lambda j: (j, 0, 0))]
                     + [ANY_SPEC] * len(prev),
            out_specs=pl.BlockSpec((S, hb), lambda j, half=half: (0, 2 * j + half)),
            out_shape=jax.ShapeDtypeStruct((S, 2 * NDEV * hb), F32),
            input_output_aliases={2: 0} if prev else {},
            compiler_params=_params(("parallel",), VMEM_BIG), name="%s_%d" % (name, half),
        )(h, wg, *prev)
    return z


def _mm_in_dx_halves(dz, wg_halves, name, dep):
    hb = wg_halves[0].shape[2]
    nk = 2 * NDEV

    def body(a_ref, b0_ref, b1_ref, dep_ref, o_ref, acc_ref):
        k = pl.program_id(2)

        @pl.when(k == 0)
        def _():
            acc_ref[...] = jnp.zeros_like(acc_ref)

        @pl.when(k % 2 == 0)
        def _():
            acc_ref[...] += _dot(a_ref[...], b0_ref[...], NT)

        @pl.when(k % 2 == 1)
        def _():
            acc_ref[...] += _dot(a_ref[...], b1_ref[...], NT)

        @pl.when(k == nk - 1)
        def _():
            o_ref[...] = acc_ref[...]

    b_spec = pl.BlockSpec((None, 1024, hb), lambda i, j, k: (k // 2, j, 0))
    return pl.pallas_call(
        body, grid=(1, D // 1024, nk),
        in_specs=[pl.BlockSpec((S, hb), lambda i, j, k: (0, k)), b_spec, b_spec, ANY_SPEC],
        out_specs=pl.BlockSpec((S, 1024), lambda i, j, k: (0, j)), out_shape=jax.ShapeDtypeStruct((S, D), F32),
        scratch_shapes=[pltpu.VMEM((S, 1024), F32)],
        compiler_params=_params(("parallel", "parallel", "arbitrary"), VMEM_BIG), name=name,
    )(dz, *wg_halves, dep)


def _mm_in_dx(dz, wg, name, dep=None):
    nb = wg.shape[2]
    return _matmul(
        dz, wg, dn=NT, grid=(S // TM, D // 1024, NDEV),
        a_spec=pl.BlockSpec((TM, nb), lambda i, j, k: (i, k)),
        b_spec=pl.BlockSpec((None, 1024, nb), lambda i, j, k: (k, j, 0)),
        o_spec=pl.BlockSpec((TM, 1024), lambda i, j, k: (i, j)),
        out_shape=(S, D), out_dtype=F32, acc_shape=(TM, 1024), name=name, dep=dep)


def _mm_out_dx(dy, w, name, dep=None):
    return _matmul(
        dy, w, dn=NT, grid=(S // TM, 2048 // 512, 1),
        a_spec=pl.BlockSpec((TM, D), lambda i, j, k: (i, 0)),
        b_spec=pl.BlockSpec((512, D), lambda i, j, k: (j, 0)),
        o_spec=pl.BlockSpec((TM, 512), lambda i, j, k: (i, j)),
        out_shape=(S, 2048), out_dtype=F32, acc_shape=(TM, 512), name=name, dep=dep)


def _mm_out_dw(yc, dy, name):
    return _matmul(
        yc, dy, dn=TN, grid=(2048 // TM, D // 512, 1),
        a_spec=pl.BlockSpec((S, TM), lambda i, j, k: (0, i)),
        b_spec=pl.BlockSpec((S, 512), lambda i, j, k: (0, j)),
        o_spec=pl.BlockSpec((TM, 512), lambda i, j, k: (i, j)),
        out_shape=(2048, D), out_dtype=BF16, acc_shape=(TM, 512), name=name)


def _row_spec(w=D):
    return pl.BlockSpec((TR, w), lambda i: (i, 0))


def _vec_spec(w=D):
    return pl.BlockSpec((1, w), lambda i: (0, 0))


def _rms_stats(x):
    r = lax.rsqrt(jnp.mean(x * x, axis=-1, keepdims=True) + EPS)
    return x * r, r


def _rms_bwd(dn, xhat, r, g):
    dxh = dn * g
    return r * (dxh - xhat * jnp.mean(dxh * xhat, axis=-1, keepdims=True))


def _acc_rows(ref, val, i):
    s = jnp.sum(val, axis=0, keepdims=True)

    @pl.when(i == 0)
    def _():
        ref[...] = s

    @pl.when(i > 0)
    def _():
        ref[...] += s


def _pre0_fwd(x, g):
    def body(x_ref, g_ref, h_ref, ht_ref):
        xhat, _ = _rms_stats(x_ref[...])
        h = xhat * g_ref[...]
        h_ref[...] = h.astype(BF16)
        ht_ref[...] = h.T.astype(BF16)

    return pl.pallas_call(
        body, grid=(S // TR,), in_specs=[_row_spec(), _vec_spec()],
        out_specs=[_row_spec(), pl.BlockSpec((D, TR), lambda i: (0, i))],
        out_shape=[jax.ShapeDtypeStruct((S, D), BF16), jax.ShapeDtypeStruct((D, S), BF16)],
        compiler_params=_params(("parallel",)), name="pre0_fwd",
    )(x, g)


def _post0_fwd(ycat, w_out, x, g_post, g_pre1, dep):
    def body(yc_ref, w_ref, x_ref, gp_ref, g1_ref, dep_ref, y_ref, x1_ref, h1_ref):
        y = _dot(yc_ref[...], w_ref[...])
        y_ref[...] = y
        yhat, _ = _rms_stats(y)
        x1 = x_ref[...] + yhat * gp_ref[...]
        x1_ref[...] = x1
        xhat, _ = _rms_stats(x1)
        h1_ref[...] = (xhat * g1_ref[...]).astype(BF16)

    return pl.pallas_call(
        body, grid=(S // TR,),
        in_specs=[_row_spec(), pl.BlockSpec((2048, D), lambda i: (0, 0)), _row_spec(), _vec_spec(), _vec_spec(),
                  ANY_SPEC],
        out_specs=[_row_spec(), _row_spec(), _row_spec()],
        out_shape=[jax.ShapeDtypeStruct((S, D), F32), jax.ShapeDtypeStruct((S, D), F32),
                   jax.ShapeDtypeStruct((S, D), BF16)],
        compiler_params=_params(("parallel",), VMEM_BIG), name="post0_fwd",
    )(ycat, w_out, x, g_post, g_pre1, dep)


def _post1_bwd(ycat, w_out, x1, target, g_post, dep):
    def body(yc_ref, w_ref, x1_ref, t_ref, g_ref, dep_ref, loss_ref, dx2_ref, dy_ref, dg_ref):
        i = pl.program_id(0)
        yhat, r = _rms_stats(_dot(yc_ref[...], w_ref[...]))
        g = g_ref[...]
        err = x1_ref[...] + yhat * g - t_ref[...]
        part = jnp.sum(jnp.sum(err * err, axis=-1, keepdims=True), axis=0, keepdims=True) * (0.5 / D)
        _acc_rows(loss_ref, jnp.broadcast_to(part, (1, 128)), i)
        dx2 = err * (1.0 / D)
        dx2_ref[...] = dx2
        _acc_rows(dg_ref, dx2 * yhat, i)
        dy_ref[...] = _rms_bwd(dx2, yhat, r, g).astype(BF16)

    return pl.pallas_call(
        body, grid=(S // TR,),
        in_specs=[_row_spec(), pl.BlockSpec((2048, D), lambda i: (0, 0)), _row_spec(), _row_spec(), _vec_spec(),
                  ANY_SPEC],
        out_specs=[_vec_spec(128), _row_spec(), _row_spec(), _vec_spec()],
        out_shape=[jax.ShapeDtypeStruct((1, 128), F32), jax.ShapeDtypeStruct((S, D), F32),
                   jax.ShapeDtypeStruct((S, D), BF16), jax.ShapeDtypeStruct((1, D), F32)],
        compiler_params=_params(("arbitrary",), VMEM_BIG), name="post1_bwd",
    )(ycat, w_out, x1, target, g_post, dep)


def _mid_bwd(dx2, dh1, x1, y0, g_pre1, g_post0):
    def body(dx2_ref, dh_ref, x1_ref, y_ref, g1_ref, gp_ref, dx1_ref, dy_ref, dg1_ref, dgp_ref):
        i = pl.program_id(0)
        xhat, r1 = _rms_stats(x1_ref[...])
        dh = dh_ref[...]
        _acc_rows(dg1_ref, dh * xhat, i)
        dx1 = dx2_ref[...] + _rms_bwd(dh, xhat, r1, g1_ref[...])
        dx1_ref[...] = dx1
        yhat, r0 = _rms_stats(y_ref[...])
        _acc_rows(dgp_ref, dx1 * yhat, i)
        dy_ref[...] = _rms_bwd(dx1, yhat, r0, gp_ref[...]).astype(BF16)

    return pl.pallas_call(
        body, grid=(S // TR,),
        in_specs=[_row_spec(), _row_spec(), _row_spec(), _row_spec(), _vec_spec(), _vec_spec()],
        out_specs=[_row_spec(), _row_spec(), _vec_spec(), _vec_spec()],
        out_shape=[jax.ShapeDtypeStruct((S, D), F32), jax.ShapeDtypeStruct((S, D), BF16),
                   jax.ShapeDtypeStruct((1, D), F32), jax.ShapeDtypeStruct((1, D), F32)],
        compiler_params=_params(("arbitrary",)), name="mid_bwd",
    )(dx2, dh1, x1, y0, g_pre1, g_post0)


def _pre0_bwd(dx1, dh0, x, g):
    def body(dx1_ref, dh_ref, x_ref, g_ref, gx_ref, dg_ref):
        i = pl.program_id(0)
        xhat, r = _rms_stats(x_ref[...])
        dh = dh_ref[...]
        _acc_rows(dg_ref, dh * xhat, i)
        gx_ref[...] = dx1_ref[...] + _rms_bwd(dh, xhat, r, g_ref[...])

    return pl.pallas_call(
        body, grid=(S // TR,), in_specs=[_row_spec(), _row_spec(), _row_spec(), _vec_spec()],
        out_specs=[_row_spec(), _vec_spec()],
        out_shape=[jax.ShapeDtypeStruct((S, D), F32), jax.ShapeDtypeStruct((1, D), F32)],
        compiler_params=_params(("arbitrary",)), name="pre0_bwd",
    )(dx1, dh0, x, g)


POOL_CH = 256


def _pool_apply(a, w, transpose):
    n = a.shape[0]
    row = lax.broadcasted_iota(jnp.int32, a.shape, 0)
    cnt = jnp.minimum(row + 1, w).astype(F32)
    s = a / cnt if transpose else a
    for k in (1, 2, 4, 8):
        if transpose:
            sh = jnp.where(row < n - k, pltpu.roll(s, n - k, 0), 0.0)
        else:
            sh = jnp.where(row >= k, pltpu.roll(s, k, 0), 0.0)
        s = jnp.where(w > k, s + sh, s)
    return s - a if transpose else s / cnt - a


def _pool_fwd(z0, pool_w, pool_scale):
    def body(a_ref, gate_ref, w_ref, sc_ref, out_ref):
        win = jnp.left_shift(2, pl.program_id(0))
        pooled = _pool_apply(a_ref[...], win, False)
        mixed = _dot(pooled.astype(BF16), w_ref[...])
        gate = gate_ref[...]
        out_ref[...] = (mixed * sc_ref[...] * (gate * _sigmoid(gate))).astype(BF16)

    return pl.pallas_call(
        body, grid=(4,),
        in_specs=[pl.BlockSpec((S, POOL_CH), lambda g: (0, g)), pl.BlockSpec((S, POOL_CH), lambda g: (0, 4 + g)),
                  pl.BlockSpec((None, POOL_CH, POOL_CH), lambda g: (g, 0, 0)),
                  pl.BlockSpec((1, POOL_CH), lambda g: (0, g))],
        out_specs=pl.BlockSpec((S, POOL_CH), lambda g: (0, g)),
        out_shape=jax.ShapeDtypeStruct((S, 2048), BF16),
        compiler_params=_params(("parallel",), VMEM_BIG), name="pool_fwd",
    )(z0, z0, pool_w, pool_scale)


def _pool_bwd(z0, dy0, w_out, pool_w, pool_scale, dep):
    def body(a_ref, gate_ref, dy0_ref, wo_ref, w_ref, sc_ref, dep_ref, da_ref, dgate_ref, dw_ref, dsc_ref):
        win = jnp.left_shift(2, pl.program_id(0))
        pooled = _pool_apply(a_ref[...], win, False).astype(BF16)
        w = w_ref[...]
        mixed = _dot(pooled, w)
        silu, dsilu = _silu_and_grad(gate_ref[...])
        dy = _dot(dy0_ref[...], wo_ref[...], NT)
        sc = sc_ref[...]
        dgate_ref[...] = (dy * (mixed * sc) * dsilu).astype(BF16)
        dms = dy * silu
        dsc_ref[...] = jnp.sum(dms * mixed, axis=0, keepdims=True)
        dmixed = (dms * sc).astype(BF16)
        dw_ref[...] = _dot(pooled, dmixed, TN)
        dpooled = _dot(dmixed, w, NT)
        da_ref[...] = _pool_apply(dpooled, win, True).astype(BF16)

    slab = lambda off: pl.BlockSpec((S, POOL_CH), lambda g: (0, off + g))
    return pl.pallas_call(
        body, grid=(4,),
        in_specs=[slab(0), slab(4), pl.BlockSpec((S, D), lambda g: (0, 0), pipeline_mode=pl.Buffered(1)),
                  pl.BlockSpec((POOL_CH, D), lambda g: (g, 0)),
                  pl.BlockSpec((None, POOL_CH, POOL_CH), lambda g: (g, 0, 0)),
                  pl.BlockSpec((1, POOL_CH), lambda g: (0, g)), ANY_SPEC],
        out_specs=[slab(0), slab(0), pl.BlockSpec((None, POOL_CH, POOL_CH), lambda g: (g, 0, 0)),
                   pl.BlockSpec((1, POOL_CH), lambda g: (0, g))],
        out_shape=[jax.ShapeDtypeStruct((S, HALF), BF16), jax.ShapeDtypeStruct((S, HALF), BF16),
                   jax.ShapeDtypeStruct((4, POOL_CH, POOL_CH), F32), jax.ShapeDtypeStruct((1, HALF), F32)],
        compiler_params=_params(("parallel",), VMEM_BIG), name="pool_bwd",
    )(z0, z0, dy0, w_out, pool_w, pool_scale, dep)


Q_COL, K_COL, V_COL, BG_COL = 2048 // 128, 5120 // 128, 8192 // 128, 11264 // 128
SCALE = HEAD_DIM ** -0.5


def _rope_tables():
    pos = jnp.arange(S, dtype=F32)
    inv_freq = jnp.power(ROPE_THETA, -jnp.arange(0, ROT_DIM, 2, dtype=F32) / ROT_DIM)
    ang = pos[:, None] * inv_freq[None, :]
    cos, sin = jnp.cos(ang), jnp.sin(ang)
    half = ROT_DIM // 2
    zeros = jnp.zeros((S, HEAD_DIM - ROT_DIM), F32)
    c = jnp.concatenate([cos, cos, jnp.ones((S, HEAD_DIM - ROT_DIM), F32)], axis=1)
    a = jnp.concatenate([-sin, jnp.zeros((S, half), F32), zeros], axis=1)
    b = jnp.concatenate([jnp.zeros((S, half), F32), sin, zeros], axis=1)
    return c, a, b


def _rope(t, c, a, b):
    half = ROT_DIM // 2
    return t * c + pltpu.roll(t, HEAD_DIM - half, 1) * a + pltpu.roll(t, half, 1) * b


def _rope_t(d, c, a, b):
    half = ROT_DIM // 2
    return d * c + pltpu.roll(d * a, half, 1) + pltpu.roll(d * b, HEAD_DIM - half, 1)


def _deinterleave(dst, src, dil, cast=None, dst_off=0):
    length = S // dil
    for r in range(dil):
        v = src[...] if dil == 1 else src[pl.ds(r, length, stride=dil), :]
        dst[dst_off + r * length:dst_off + (r + 1) * length, :] = v if cast is None else v.astype(cast)


def _interleave(dst, src, dil, src_off=0):
    length = S // dil
    for r in range(dil):
        if dil == 1:
            dst[...] = src[src_off:src_off + S, :]
        else:
            dst[pl.ds(r, length, stride=dil), :] = src[src_off + r * length:src_off + (r + 1) * length, :]


CU = 8
NUNITS = S // BLK
B_QK = (((2,), (2,)), ((0,), (0,)))
B_PV = (((2,), (1,)), ((0,), (0,)))
B_TN = (((1,), (1,)), ((0,), (0,)))


def _blocks(ref, first):
    return ref[first * BLK:(first + CU) * BLK, :].reshape(CU, BLK, HEAD_DIM)


def _chunk_scores(u0, nb, qd, kdp):
    q = _blocks(qd, u0)
    row = lax.broadcasted_iota(jnp.int32, (CU, BLK, BLK), 1)
    col = lax.broadcasted_iota(jnp.int32, (CU, BLK, BLK), 2)
    s_own = jnp.where(col <= row, _dot(q, _blocks(kdp, u0 + 1), B_QK) * SCALE, NEG)
    if nb == 1:
        return q, s_own, None
    unit = lax.broadcasted_iota(jnp.int32, (CU, BLK, BLK), 0) + u0
    s_prev = jnp.where((col >= row) & ((unit % nb) != 0), _dot(q, _blocks(kdp, u0), B_QK) * SCALE, NEG)
    return q, s_own, s_prev


def _qkv_prep(z0, tabs):
    def body(q_ref, k_ref, v_ref, c_ref, a_ref, b_ref, qo_ref, ko_ref, vo_ref, tmp):
        p = pl.program_id(1)
        ko_ref[0:BLK, :] = jnp.zeros((BLK, HEAD_DIM), BF16)
        vo_ref[0:BLK, :] = jnp.zeros((BLK, HEAD_DIM), BF16)
        for gi, (_, dil) in enumerate(PATTERNS):
            @pl.when(p == gi)
            def _(dil=dil):
                c, a, b = c_ref[...], a_ref[...], b_ref[...]
                tmp[...] = _rope(q_ref[...], c, a, b)
                _deinterleave(qo_ref, tmp, dil, BF16)
                tmp[...] = _rope(k_ref[...], c, a, b)
                _deinterleave(ko_ref, tmp, dil, BF16, BLK)
                _deinterleave(vo_ref, v_ref, dil, BF16, BLK)

    tab = pl.BlockSpec((S, HEAD_DIM), lambda h, p: (0, 0))
    out = pl.BlockSpec((S, HEAD_DIM), lambda h, p: (0, p * 8 + h))
    outp = pl.BlockSpec((S + BLK, HEAD_DIM), lambda h, p: (0, p * 8 + h))
    return pl.pallas_call(
        body, grid=(8, 3), in_specs=[_head_spec(Q_COL), _head_spec(K_COL), _head_spec(V_COL), tab, tab, tab],
        out_specs=[out, outp, outp],
        out_shape=[jax.ShapeDtypeStruct((S, 3072), BF16)] + [jax.ShapeDtypeStruct((S + BLK, 3072), BF16)] * 2,
        scratch_shapes=[pltpu.VMEM((S, HEAD_DIM), F32)],
        compiler_params=_params(("parallel", "arbitrary"), VMEM_BIG), name="qkv_prep",
    )(z0, z0, z0, *tabs)


def _attn_group_fwd(dil, qd, kdp, vdp, od, ld, og, lg):
    nb = S // dil // BLK
    for u0 in range(0, NUNITS, CU):
        _, s_own, s_prev = _chunk_scores(u0, nb, qd, kdp)
        m = jnp.max(s_own, axis=2, keepdims=True)
        if s_prev is not None:
            m = jnp.maximum(m, jnp.max(s_prev, axis=2, keepdims=True))
        p_own = jnp.exp(s_own - m)
        den = jnp.sum(p_own, axis=2, keepdims=True)
        acc = _dot(p_own.astype(BF16), _blocks(vdp, u0 + 1), B_PV)
        if s_prev is not None:
            p_prev = jnp.exp(s_prev - m)
            den = den + jnp.sum(p_prev, axis=2, keepdims=True)
            acc = acc + _dot(p_prev.astype(BF16), _blocks(vdp, u0), B_PV)
        rows = slice(u0 * BLK, (u0 + CU) * BLK)
        od[rows, :] = (acc / den).reshape(CU * BLK, HEAD_DIM)
        ld[rows, :] = jnp.broadcast_to(m + jnp.log(den), (CU, BLK, HEAD_DIM)).reshape(CU * BLK, HEAD_DIM)
    _interleave(og, od, dil)
    _interleave(lg, ld, dil)


def _group_weights(lgs):
    l0, l1, l2 = lgs[0][...], lgs[1][...], lgs[2][...]
    mx = jnp.maximum(l0, jnp.maximum(l1, l2))
    e0, e1, e2 = jnp.exp(l0 - mx), jnp.exp(l1 - mx), jnp.exp(l2 - mx)
    den = e0 + e1 + e2
    return e0 / den, e1 / den, e2 / den


def _head_spec(base):
    return pl.BlockSpec((S, HEAD_DIM), lambda h, p: (0, base + (p % 3) * 8 + h))


def _slab(dtype=F32, rows=S):
    return pltpu.VMEM((rows, HEAD_DIM), dtype)


def _attn_fwd(z0, tabs, ycat):
    def body(q_ref, k_ref, v_ref, c_ref, a_ref, b_ref, gate_ref, ycat_ref, out_ref, og_ref, lg_ref,
             qo_ref, ko_ref, vo_ref, tmp, od, ld, og0, og1, og2, lg0, lg1, lg2):
        del ycat_ref
        p = pl.program_id(1)
        ogs, lgs = (og0, og1, og2), (lg0, lg1, lg2)
        ko_ref[0:BLK, :] = jnp.zeros((BLK, HEAD_DIM), BF16)
        vo_ref[0:BLK, :] = jnp.zeros((BLK, HEAD_DIM), BF16)
        for gi, (_, dil) in enumerate(PATTERNS):
            @pl.when(p == gi)
            def _(gi=gi, dil=dil):
                c, a, b = c_ref[...], a_ref[...], b_ref[...]
                tmp[...] = _rope(q_ref[...], c, a, b)
                _deinterleave(qo_ref, tmp, dil, BF16)
                tmp[...] = _rope(k_ref[...], c, a, b)
                _deinterleave(ko_ref, tmp, dil, BF16, BLK)
                _deinterleave(vo_ref, v_ref, dil, BF16, BLK)
                _attn_group_fwd(dil, qo_ref, ko_ref, vo_ref, od, ld, ogs[gi], lgs[gi])
                og_ref[...] = ogs[gi][...]
                lg_ref[...] = lgs[gi][...]

        @pl.when(p == 2)
        def _():
            w0, w1, w2 = _group_weights(lgs)
            o = w0 * og0[...] + w1 * og1[...] + w2 * og2[...]
            gate = gate_ref[...]
            out_ref[...] = (o * (gate * _sigmoid(gate))).astype(BF16)

    grp = pl.BlockSpec((S, HEAD_DIM), lambda h, p: (0, p * 8 + h))
    grp_pad = pl.BlockSpec((S + BLK, HEAD_DIM), lambda h, p: (0, p * 8 + h))
    tab = pl.BlockSpec((S, HEAD_DIM), lambda h, p: (0, 0))
    outs = pl.pallas_call(
        body, grid=(8, 3),
        in_specs=[_head_spec(Q_COL), _head_spec(K_COL), _head_spec(V_COL), tab, tab, tab,
                  pl.BlockSpec((S, HEAD_DIM), lambda h, p: (0, BG_COL + h)), ANY_SPEC],
        out_specs=[pl.BlockSpec((S, HEAD_DIM), lambda h, p: (0, 8 + h)), grp, grp, grp, grp_pad, grp_pad],
        out_shape=[jax.ShapeDtypeStruct((S, 2048), BF16), jax.ShapeDtypeStruct((S, 3072), F32),
                   jax.ShapeDtypeStruct((S, 3072), F32), jax.ShapeDtypeStruct((S, 3072), BF16),
                   jax.ShapeDtypeStruct((S + BLK, 3072), BF16), jax.ShapeDtypeStruct((S + BLK, 3072), BF16)],
        scratch_shapes=[_slab() for _ in range(9)],
        input_output_aliases={7: 0},
        compiler_params=_params(("parallel", "arbitrary"), VMEM_BIG), name="attn_fwd",
    )(z0, z0, z0, *tabs, z0, ycat)
    return outs[0], outs[1], outs[2], tuple(outs[3:])


def _attn_bwd(z0, qkv, og, lg, dy0, w_out, tabs, dep):
    def body(q_ref, k_ref, v_ref, gate_ref, dy0_ref, wo_ref, c_ref, a_ref, b_ref,
             og0_ref, og1_ref, og2_ref, lg0_ref, lg1_ref, lg2_ref, dep_ref,
             dq_ref, dk_ref, dv_ref, dbg_ref,
             tmp, ld, dg0, dg1, dg2, cg0, cg1, cg2, dod, cd, dqd, dkd, dvd):
        kd, vd = k_ref, v_ref
        p = pl.program_id(1)
        ogs, lgs, dgs, cgs = (og0_ref, og1_ref, og2_ref), (lg0_ref, lg1_ref, lg2_ref), (dg0, dg1, dg2), (cg0, cg1, cg2)

        @pl.when(p == 0)
        def _():
            w = _group_weights(lgs)
            o = w[0] * ogs[0][...] + w[1] * ogs[1][...] + w[2] * ogs[2][...]
            silu, dsilu = _silu_and_grad(gate_ref[...])
            dy = _dot(dy0_ref[...], wo_ref[...], NT)
            dbg_ref[...] = (dy * o * dsilu).astype(BF16)
            do = dy * silu
            dwbar = jnp.sum(do * o, axis=1, keepdims=True)
            for gi in range(3):
                dgs[gi][...] = w[gi] * do
                cgs[gi][...] = -w[gi] * dwbar

        for gi, (_, dil) in enumerate(PATTERNS):
            @pl.when(p == 1 + gi)
            def _(gi=gi, dil=dil):
                nb = S // dil // BLK
                qd = q_ref
                c, a, b = c_ref[...], a_ref[...], b_ref[...]
                _deinterleave(dod, dgs[gi], dil, BF16)
                _deinterleave(ld, lgs[gi], dil)
                _deinterleave(cd, cgs[gi], dil)
                dkd[...] = jnp.zeros_like(dkd)
                dvd[...] = jnp.zeros_like(dvd)
                flat = lambda t: t.reshape(CU * BLK, HEAD_DIM)
                for u0 in range(0, NUNITS, CU):
                    q, s_own, s_prev = _chunk_scores(u0, nb, qd, kd)
                    lse, cv, do = _blocks(ld, u0), _blocks(cd, u0), _blocks(dod, u0)
                    own = slice((u0 + 1) * BLK, (u0 + 1 + CU) * BLK)
                    p_own = jnp.exp(s_own - lse)
                    ds_own = (p_own * (_dot(do, _blocks(vd, u0 + 1), B_QK) + cv) * SCALE).astype(BF16)
                    dq = _dot(ds_own, _blocks(kd, u0 + 1), B_PV)
                    dkd[own, :] += flat(_dot(ds_own, q, B_TN))
                    dvd[own, :] += flat(_dot(p_own.astype(BF16), do, B_TN))
                    if s_prev is not None:
                        prev = slice(u0 * BLK, (u0 + CU) * BLK)
                        p_prev = jnp.exp(s_prev - lse)
                        ds_prev = (p_prev * (_dot(do, _blocks(vd, u0), B_QK) + cv) * SCALE).astype(BF16)
                        dq = dq + _dot(ds_prev, _blocks(kd, u0), B_PV)
                        dkd[prev, :] += flat(_dot(ds_prev, q, B_TN))
                        dvd[prev, :] += flat(_dot(p_prev.astype(BF16), do, B_TN))
                    dqd[u0 * BLK:(u0 + CU) * BLK, :] = flat(dq)
                _interleave(tmp, dqd, dil)
                dq_ref[...] = _rope_t(tmp[...], c, a, b).astype(BF16)
                _interleave(tmp, dkd, dil, BLK)
                dk_ref[...] = _rope_t(tmp[...], c, a, b).astype(BF16)
                _interleave(tmp, dvd, dil, BLK)
                dv_ref[...] = tmp[...].astype(BF16)

    tab = pl.BlockSpec((S, HEAD_DIM), lambda h, p: (0, 0))
    hspec = lambda base: pl.BlockSpec((S, HEAD_DIM), lambda h, p: (0, base + h))
    gspec = pl.BlockSpec((S, HEAD_DIM), lambda h, p: (0, jnp.maximum(p - 1, 0) * 8 + h))
    gspec_pad = pl.BlockSpec((S + BLK, HEAD_DIM), lambda h, p: (0, jnp.maximum(p - 1, 0) * 8 + h))
    return pl.pallas_call(
        body, grid=(8, 4),
        in_specs=[gspec, gspec_pad, gspec_pad, hspec(BG_COL),
                  pl.BlockSpec((S, D), lambda h, p: (0, 0), pipeline_mode=pl.Buffered(1)),
                  pl.BlockSpec((HEAD_DIM, D), lambda h, p: (8 + h, 0)), tab, tab, tab,
                  hspec(0), hspec(8), hspec(16), hspec(0), hspec(8), hspec(16), ANY_SPEC],
        out_specs=[gspec, gspec, gspec, hspec(0)],
        out_shape=[jax.ShapeDtypeStruct((S, 3072), BF16)] * 3 + [jax.ShapeDtypeStruct((S, HALF), BF16)],
        scratch_shapes=[_slab(), _slab()] + [_slab() for _ in range(6)]
                       + [_slab(BF16), _slab(), _slab(), _slab(F32, S + BLK), _slab(F32, S + BLK)],
        compiler_params=_params(("parallel", "arbitrary"), VMEM_BIG), name="attn_bwd",
    )(*qkv, z0, dy0, w_out, *tabs, og, og, og, lg, lg, lg, dep)


SGU_CH = 256
NCHUNK = TR // 128


def _ln_stats(x):
    mu = jnp.mean(x, axis=-1, keepdims=True)
    xc = x - mu
    r = lax.rsqrt(jnp.mean(xc * xc, axis=-1, keepdims=True) + EPS)
    return xc * r, r


def _ln_bwd(dy, xhat, r, g):
    dxh = dy * g
    return r * (dxh - jnp.mean(dxh, axis=-1, keepdims=True) - xhat * jnp.mean(dxh * xhat, axis=-1, keepdims=True))


def _tril_bf16(w):
    row = lax.broadcasted_iota(jnp.int32, w.shape, 0)
    col = lax.broadcasted_iota(jnp.int32, w.shape, 1)
    return jnp.where(row >= col, w, 0.0).astype(BF16)


def _sgu_gate(vn_s, s_s, w_ref, bb_ref):
    for h in range(4):
        wm = _tril_bf16(w_ref[h])
        bias = bb_ref[h]
        for ch in range(NCHUNK):
            rows, cols = slice(ch * 128, (ch + 1) * 128), slice(h * SGU_CH, (h + 1) * SGU_CH)
            s_s[rows, cols] = _dot(wm, vn_s[rows, cols]) + jnp.concatenate([bias, bias], axis=1)


WIN = HALO + TR
SUBL = 8


def _shifted_copies(dst, src):
    dst[0] = src[...]
    for b in range(1, SUBL):
        dst[b, 0:WIN - SUBL, :] = src[pl.ds(b, WIN - SUBL), :]


def _rows_at(copies, off, n):
    return copies[off % SUBL, pl.ds(off - off % SUBL, n), :]


def _conv_fwd(i, dval_ref, dglu_ref, hval_ref, hglu_ref, cw_ref, cb_ref, xw, xr, dcs):
    halo = hval_ref[...] * _sigmoid(hglu_ref[...])
    xw[0:HALO, :] = jnp.where(i > 0, halo, 0.0)
    xw[HALO:HALO + TR, :] = dval_ref[...] * _sigmoid(dglu_ref[...])
    _shifted_copies(xr, xw)
    sub = 2 * SUB
    for rb in range(TR // sub):
        acc = jnp.broadcast_to(cb_ref[...], (sub, HALF))
        for k in range(CONV_K):
            acc = acc + cw_ref[k:k + 1, :] * _rows_at(xr, rb * sub + HALO - (CONV_K - 1) + k, sub)
        dcs[rb * sub:(rb + 1) * sub, :] = acc


def _odd_in_specs():
    col = lambda j: pl.BlockSpec((TR, HALF), lambda i, *_: (i, j))
    prev = lambda j: pl.BlockSpec((HALO, HALF), lambda i, *_: (jnp.maximum(i * (TR // HALO) - 1, 0), j))
    return [col(0), col(1), col(2), col(3), col(4), col(5), prev(3), prev(4)]


def _full_spec(shape):
    return pl.BlockSpec(shape, lambda i, *_: (0,) * len(shape))


def _odd_fwd(z1, sgu_g, sgu_b, sgu_w, sgu_bb, conv_w, conv_b, cn_g, cn_b):
    def body(u_ref, v_ref, cg_ref, dval_ref, dglu_ref, dgate_ref, hval_ref, hglu_ref,
             g_ref, b_ref, w_ref, bb_ref, cw_ref, cb_ref, cng_ref, cnb_ref, out_ref, dcs, vn_s, s_s, xw, xr):
        i = pl.program_id(0)
        vhat, _ = _ln_stats(v_ref[...])
        vn_s[...] = (vhat * g_ref[...] + b_ref[...]).astype(BF16)
        _sgu_gate(vn_s, s_s, w_ref, bb_ref)
        cg = cg_ref[...]
        out_ref[:, 0:HALF] = (u_ref[...] * s_s[...] * (cg * _sigmoid(cg))).astype(BF16)
        _conv_fwd(i, dval_ref, dglu_ref, hval_ref, hglu_ref, cw_ref, cb_ref, xw, xr, dcs)
        dhat, _ = _ln_stats(dcs[...])
        dn = dhat * cng_ref[...] + cnb_ref[...]
        dgate = dgate_ref[...]
        out_ref[:, HALF:2 * HALF] = ((dn * _sigmoid(dn)) * (dgate * _sigmoid(dgate))).astype(BF16)

    vec = _full_spec((1, HALF))
    return pl.pallas_call(
        body, grid=(S // TR,),
        in_specs=_odd_in_specs() + [vec, vec, _full_spec((4, 128, 128)), _full_spec((4, 128, 128)),
                                    _full_spec((HALO, HALF)), vec, vec, vec],
        out_specs=[pl.BlockSpec((TR, 2048), lambda i: (i, 0)), pl.BlockSpec((TR, HALF), lambda i: (i, 0))],
        out_shape=[jax.ShapeDtypeStruct((S, 2048), BF16), jax.ShapeDtypeStruct((S, HALF), F32)],
        scratch_shapes=[pltpu.VMEM((TR, HALF), BF16), pltpu.VMEM((TR, HALF), F32),
                        pltpu.VMEM((WIN, HALF), F32), pltpu.VMEM((SUBL, WIN, HALF), F32)],
        compiler_params=_params(("parallel",), VMEM_BIG), name="odd_fwd",
    )(z1, z1, z1, z1, z1, z1, z1, z1, sgu_g, sgu_b, sgu_w, sgu_bb, conv_w, conv_b, cn_g, cn_b)


def _odd_bwd_a(z1, dc, dycat, sgu_g, sgu_b, sgu_w, sgu_bb, cn_g, cn_b):
    def body(u_ref, v_ref, cg_ref, dgate_ref, dcs, dy_ref, g_ref, b_ref, w_ref, bb_ref, cng_ref, cnb_ref,
             dz_ref, ddc_ref, dw_ref, dbb_ref, dg_ref, db_ref, dcng_ref, dcnb_ref, dcb_ref,
             vn_s, s_s, ds_s, dvn_s):
        i = pl.program_id(0)
        vhat, rv = _ln_stats(v_ref[...])
        g = g_ref[...]
        vn_s[...] = (vhat * g + b_ref[...]).astype(BF16)
        _sgu_gate(vn_s, s_s, w_ref, bb_ref)
        silu_c, dsilu_c = _silu_and_grad(cg_ref[...])
        dyc = dy_ref[:, 0:HALF]
        u = u_ref[...]
        s = s_s[...]
        dz_ref[:, 0:HALF] = (dyc * s * silu_c).astype(BF16)
        dz_ref[:, 2 * HALF:3 * HALF] = (dyc * u * s * dsilu_c).astype(BF16)
        ds_s[...] = dyc * u * silu_c

        @pl.when(i == 0)
        def _():
            dw_ref[...] = jnp.zeros_like(dw_ref)
            dbb_ref[...] = jnp.zeros_like(dbb_ref)

        tril = lax.broadcasted_iota(jnp.int32, (128, 128), 0) >= lax.broadcasted_iota(jnp.int32, (128, 128), 1)
        for h in range(4):
            wm = _tril_bf16(w_ref[h])
            for ch in range(NCHUNK):
                rows, cols = slice(ch * 128, (ch + 1) * 128), slice(h * SGU_CH, (h + 1) * SGU_CH)
                ds = ds_s[rows, cols]
                dsb = ds.astype(BF16)
                dw_ref[h] += jnp.where(tril, _dot(dsb, vn_s[rows, cols], NT), 0.0)
                dbb_ref[h] += jnp.broadcast_to(jnp.sum(ds, axis=1, keepdims=True), (128, 128))
                dvn_s[rows, cols] = _dot(wm, dsb, TN)
        dvn = dvn_s[...]
        _acc_rows(dg_ref, dvn * vhat, i)
        _acc_rows(db_ref, dvn, i)
        dz_ref[:, HALF:2 * HALF] = _ln_bwd(dvn, vhat, rv, g).astype(BF16)

        dhat, rd = _ln_stats(dcs[...])
        cng = cng_ref[...]
        silu_n, dsilu_n = _silu_and_grad(dhat * cng + cnb_ref[...])
        silu_g, dsilu_g = _silu_and_grad(dgate_ref[...])
        dyd = dy_ref[:, HALF:2 * HALF]
        dz_ref[:, 5 * HALF:6 * HALF] = (dyd * silu_n * dsilu_g).astype(BF16)
        ddn = dyd * silu_g * dsilu_n
        _acc_rows(dcng_ref, ddn * dhat, i)
        _acc_rows(dcnb_ref, ddn, i)
        ddc = _ln_bwd(ddn, dhat, rd, cng)
        ddc_ref[...] = ddc
        _acc_rows(dcb_ref, ddc, i)

    vec = _full_spec((1, HALF))
    sq = _full_spec((4, 128, 128))
    col = lambda j: pl.BlockSpec((TR, HALF), lambda i: (i, j))
    return pl.pallas_call(
        body, grid=(S // TR,),
        in_specs=[col(0), col(1), col(2), col(5), col(0), pl.BlockSpec((TR, 2048), lambda i: (i, 0)),
                  vec, vec, sq, sq, vec, vec],
        out_specs=[pl.BlockSpec((TR, ODD_IN), lambda i: (i, 0)), pl.BlockSpec((TR, HALF), lambda i: (i, 0)),
                   sq, sq, vec, vec, vec, vec, vec],
        out_shape=[jax.ShapeDtypeStruct((S, ODD_IN), BF16), jax.ShapeDtypeStruct((S, HALF), F32),
                   jax.ShapeDtypeStruct((4, 128, 128), F32), jax.ShapeDtypeStruct((4, 128, 128), F32)]
                  + [jax.ShapeDtypeStruct((1, HALF), F32)] * 5,
        scratch_shapes=[pltpu.VMEM((TR, HALF), BF16), pltpu.VMEM((TR, HALF), F32),
                        pltpu.VMEM((TR, HALF), F32), pltpu.VMEM((TR, HALF), F32)],
        compiler_params=_params(("arbitrary",), VMEM_BIG), name="odd_bwd_a",
    )(z1, z1, z1, z1, dc, dycat, sgu_g, sgu_b, sgu_w, sgu_bb, cn_g, cn_b)


def _odd_bwd_b(z1, ddc, dz1, conv_w):
    nt = S // TR

    def body(dval_ref, dglu_ref, hval_ref, hglu_ref, ddc_ref, hddc_ref, cw_ref, dz_in_ref,
             dz_ref, dcw_ref, xw, dwin, dxs, xr, dr):
        del dz_in_ref
        i, j = pl.program_id(0), pl.program_id(1)
        sg = _sigmoid(dglu_ref[...])
        dval = dval_ref[...]

        @pl.when(j == 0)
        def _():
            halo = hval_ref[...] * _sigmoid(hglu_ref[...])
            xw[0:HALO, :] = jnp.where(i > 0, halo, 0.0)
            xw[HALO:HALO + TR, :] = dval * sg
            dwin[0:TR, :] = ddc_ref[...]
            dwin[TR:TR + HALO, :] = jnp.where(i < nt - 1, hddc_ref[...], 0.0)
            _shifted_copies(xr, xw)
            _shifted_copies(dr, dwin)

            @pl.when(i == 0)
            def _():
                dcw_ref[...] = jnp.zeros_like(dcw_ref)

            for rb in range(TR // SUB):
                acc = jnp.zeros((SUB, HALF), F32)
                for k in range(CONV_K):
                    acc = acc + cw_ref[k:k + 1, :] * _rows_at(dr, rb * SUB + (CONV_K - 1) - k, SUB)
                dxs[rb * SUB:(rb + 1) * SUB, :] = acc
            for k in range(CONV_K):
                acc = jnp.zeros((SUB, HALF), F32)
                for rb in range(TR // SUB):
                    acc = acc + dwin[rb * SUB:(rb + 1) * SUB, :] * _rows_at(xr, rb * SUB + HALO - (CONV_K - 1) + k, SUB)
                dcw_ref[k:k + 1, :] += jnp.sum(acc, axis=0, keepdims=True)
            dz_ref[...] = (dxs[...] * sg).astype(BF16)

        @pl.when(j == 1)
        def _():
            dz_ref[...] = (dxs[...] * dval * sg * (1.0 - sg)).astype(BF16)

    col = lambda c: pl.BlockSpec((TR, HALF), lambda i, j: (i, c))
    prev = lambda c: pl.BlockSpec((HALO, HALF), lambda i, j: (jnp.maximum(i * (TR // HALO) - 1, 0), c))
    nxt = pl.BlockSpec((HALO, HALF), lambda i, j: (jnp.minimum((i + 1) * (TR // HALO), S // HALO - 1), 0))
    return pl.pallas_call(
        body, grid=(nt, 2),
        in_specs=[col(3), col(4), prev(3), prev(4), pl.BlockSpec((TR, HALF), lambda i, j: (i, 0)), nxt,
                  _full_spec((HALO, HALF)), pl.BlockSpec(memory_space=pl.ANY)],
        out_specs=[pl.BlockSpec((TR, HALF), lambda i, j: (i, 3 + j)), _full_spec((HALO, HALF))],
        out_shape=[jax.ShapeDtypeStruct((S, ODD_IN), BF16), jax.ShapeDtypeStruct((HALO, HALF), F32)],
        scratch_shapes=[pltpu.VMEM((WIN, HALF), F32), pltpu.VMEM((WIN, HALF), F32), pltpu.VMEM((TR, HALF), F32),
                        pltpu.VMEM((SUBL, WIN, HALF), F32), pltpu.VMEM((SUBL, WIN, HALF), F32)],
        input_output_aliases={7: 0},
        compiler_params=_params(("arbitrary", "arbitrary"), VMEM_BIG), name="odd_bwd_b",
    )(z1, z1, z1, z1, ddc, ddc, conv_w, dz1)


def _conv_out_grad(dc, dyd, dgate, cng, cnb):
    dhat, rd = _ln_stats(dc)
    silu_n, dsilu_n = _silu_and_grad(dhat * cng + cnb)
    silu_g, dsilu_g = _silu_and_grad(dgate)
    ddn = dyd * silu_g * dsilu_n
    return _ln_bwd(ddn, dhat, rd, cng), ddn, dhat, dyd * silu_n * dsilu_g


def _odd_bwd(z1, dc, dycat, sgu_g, sgu_b, sgu_w, sgu_bb, conv_w, cn_g, cn_b):
    nt = S // TR

    def body(u_ref, v_ref, cg_ref, dval_ref, dglu_ref, dgate_ref, hval_ref, hglu_ref, dcs, dy_ref,
             ndc_ref, ndy_ref, ngate_ref, g_ref, b_ref, w_ref, bb_ref, cw_ref, cng_ref, cnb_ref,
             dz_ref, dw_ref, dbb_ref, dg_ref, db_ref, dcng_ref, dcnb_ref, dcb_ref, dcw_ref,
             vn_s, s_s, ds_s, dvn_s, xw, dwin, dxs, xr, dr):
        i = pl.program_id(0)
        vhat, rv = _ln_stats(v_ref[...])
        g = g_ref[...]
        vn_s[...] = (vhat * g + b_ref[...]).astype(BF16)
        _sgu_gate(vn_s, s_s, w_ref, bb_ref)
        silu_c, dsilu_c = _silu_and_grad(cg_ref[...])
        dyc = dy_ref[:, 0:HALF]
        u = u_ref[...]
        s = s_s[...]
        dz_ref[:, 0:HALF] = (dyc * s * silu_c).astype(BF16)
        dz_ref[:, 2 * HALF:3 * HALF] = (dyc * u * s * dsilu_c).astype(BF16)
        ds_s[...] = dyc * u * silu_c

        @pl.when(i == 0)
        def _():
            dw_ref[...] = jnp.zeros_like(dw_ref)
            dbb_ref[...] = jnp.zeros_like(dbb_ref)
            dcw_ref[...] = jnp.zeros_like(dcw_ref)

        tril = lax.broadcasted_iota(jnp.int32, (128, 128), 0) >= lax.broadcasted_iota(jnp.int32, (128, 128), 1)
        for h in range(4):
            wm = _tril_bf16(w_ref[h])
            for ch in range(NCHUNK):
                rows, cols = slice(ch * 128, (ch + 1) * 128), slice(h * SGU_CH, (h + 1) * SGU_CH)
                ds = ds_s[rows, cols]
                dsb = ds.astype(BF16)
                dw_ref[h] += jnp.where(tril, _dot(dsb, vn_s[rows, cols], NT), 0.0)
                dbb_ref[h] += jnp.broadcast_to(jnp.sum(ds, axis=1, keepdims=True), (128, 128))
                dvn_s[rows, cols] = _dot(wm, dsb, TN)
        dvn = dvn_s[...]
        _acc_rows(dg_ref, dvn * vhat, i)
        _acc_rows(db_ref, dvn, i)
        dz_ref[:, HALF:2 * HALF] = _ln_bwd(dvn, vhat, rv, g).astype(BF16)

        cng, cnb = cng_ref[...], cnb_ref[...]
        ddc, ddn, dhat, ddgate = _conv_out_grad(dcs[...], dy_ref[:, HALF:2 * HALF], dgate_ref[...], cng, cnb)
        dz_ref[:, 5 * HALF:6 * HALF] = ddgate.astype(BF16)
        _acc_rows(dcng_ref, ddn * dhat, i)
        _acc_rows(dcnb_ref, ddn, i)
        _acc_rows(dcb_ref, ddc, i)
        ddc_next = _conv_out_grad(ndc_ref[...], ndy_ref[:, HALF:2 * HALF], ngate_ref[...], cng, cnb)[0]

        sg = _sigmoid(dglu_ref[...])
        dval = dval_ref[...]
        halo = hval_ref[...] * _sigmoid(hglu_ref[...])
        xw[0:HALO, :] = jnp.where(i > 0, halo, 0.0)
        xw[HALO:HALO + TR, :] = dval * sg
        dwin[0:TR, :] = ddc
        dwin[TR:TR + HALO, :] = jnp.where(i < nt - 1, ddc_next, 0.0)
        _shifted_copies(xr, xw)
        _shifted_copies(dr, dwin)
        for rb in range(TR // SUB):
            acc = jnp.zeros((SUB, HALF), F32)
            for k in range(CONV_K):
                acc = acc + cw_ref[k:k + 1, :] * _rows_at(dr, rb * SUB + (CONV_K - 1) - k, SUB)
            dxs[rb * SUB:(rb + 1) * SUB, :] = acc
        for k in range(CONV_K):
            acc = jnp.zeros((SUB, HALF), F32)
            for rb in range(TR // SUB):
                acc = acc + dwin[rb * SUB:(rb + 1) * SUB, :] * _rows_at(xr, rb * SUB + HALO - (CONV_K - 1) + k, SUB)
            dcw_ref[k:k + 1, :] += jnp.sum(acc, axis=0, keepdims=True)
        dx = dxs[...]
        dz_ref[:, 3 * HALF:4 * HALF] = (dx * sg).astype(BF16)
        dz_ref[:, 4 * HALF:5 * HALF] = (dx * dval * sg * (1.0 - sg)).astype(BF16)

    vec = _full_spec((1, HALF))
    sq = _full_spec((4, 128, 128))
    col = lambda j: pl.BlockSpec((TR, HALF), lambda i: (i, j))
    prev = lambda j: pl.BlockSpec((HALO, HALF), lambda i: (jnp.maximum(i * (TR // HALO) - 1, 0), j))
    nxt_row = lambda i: jnp.minimum((i + 1) * (TR // HALO), S // HALO - 1)
    return pl.pallas_call(
        body, grid=(nt,),
        in_specs=[col(0), col(1), col(2), col(3), col(4), col(5), prev(3), prev(4), col(0),
                  pl.BlockSpec((TR, 2048), lambda i: (i, 0)),
                  pl.BlockSpec((HALO, HALF), lambda i: (nxt_row(i), 0)), pl.BlockSpec((HALO, 2048), lambda i: (nxt_row(i), 0)),
                  pl.BlockSpec((HALO, HALF), lambda i: (nxt_row(i), 5)),
                  vec, vec, sq, sq, _full_spec((HALO, HALF)), vec, vec],
        out_specs=[pl.BlockSpec((TR, ODD_IN), lambda i: (i, 0)), sq, sq, vec, vec, vec, vec, vec,
                   _full_spec((HALO, HALF))],
        out_shape=[jax.ShapeDtypeStruct((S, ODD_IN), BF16), jax.ShapeDtypeStruct((4, 128, 128), F32),
                   jax.ShapeDtypeStruct((4, 128, 128), F32)] + [jax.ShapeDtypeStruct((1, HALF), F32)] * 5
                  + [jax.ShapeDtypeStruct((HALO, HALF), F32)],
        scratch_shapes=[pltpu.VMEM((TR, HALF), BF16), pltpu.VMEM((TR, HALF), F32), pltpu.VMEM((TR, HALF), F32),
                        pltpu.VMEM((TR, HALF), F32), pltpu.VMEM((WIN, HALF), F32), pltpu.VMEM((WIN, HALF), F32),
                        pltpu.VMEM((TR, HALF), F32), pltpu.VMEM((SUBL, WIN, HALF), F32),
                        pltpu.VMEM((SUBL, WIN, HALF), F32)],
        compiler_params=_params(("arbitrary",), VMEM_BIG), name="odd_bwd",
    )(z1, z1, z1, z1, z1, z1, z1, z1, dc, dycat, dc, dycat, z1,
      sgu_g, sgu_b, sgu_w, sgu_bb, conv_w, cn_g, cn_b)


def _cast_bf16(w, name, piece=0, npieces=1):
    r, c = w.shape[0], w.shape[1] // npieces
    tr = min(r, 256)

    def body(i_ref, o_ref):
        o_ref[...] = i_ref[...].astype(BF16)

    return pl.pallas_call(
        body, grid=(r // tr,), in_specs=[pl.BlockSpec((tr, c), lambda i: (i, piece))],
        out_specs=pl.BlockSpec((tr, c), lambda i: (i, 0)), out_shape=jax.ShapeDtypeStruct((r, c), BF16),
        compiler_params=_params(("parallel",)), name=name,
    )(w)


def _adamw(w, g, m, v):
    m = ADAM_B1 * m + (1.0 - ADAM_B1) * g
    v = ADAM_B2 * v + (1.0 - ADAM_B2) * (g * g)
    m_hat = m / (1.0 - ADAM_B1 ** ADAM_STEP)
    v_hat = v / (1.0 - ADAM_B2 ** ADAM_STEP)
    delta = -ADAM_LR * (m_hat / (jnp.sqrt(v_hat) + ADAM_EPS) + ADAM_WD * w)
    return delta, m, v


def _adam_reduce(parts, w, m, v, name, dep=None, piece=0, npieces=1, prev=None):
    r, c = w.shape
    cp = c // npieces
    tr = min(r, 128)
    extra = ([] if dep is None else [dep]) + ([] if prev is None else list(prev))
    nparts = parts.shape[0]

    def body(p_ref, w_ref, m_ref, v_ref, *rest):
        g_ref, d_ref, nm_ref, nv_ref = rest[len(extra):]
        g = p_ref[0].astype(F32)
        for d in range(1, nparts):
            g = g + p_ref[d].astype(F32)
        g_ref[...] = g
        d_ref[...], nm_ref[...], nv_ref[...] = _adamw(w_ref[...], g, m_ref[...], v_ref[...])

    spec = pl.BlockSpec((tr, cp), lambda i: (i, piece))
    first = 4 + (0 if dep is None else 1)
    return pl.pallas_call(
        body, grid=(r // tr,),
        in_specs=[pl.BlockSpec((nparts, tr, cp), lambda i: (0, i, 0)), spec, spec, spec] + [ANY_SPEC] * len(extra),
        out_specs=[spec] * 4, out_shape=[jax.ShapeDtypeStruct((r, c), F32)] * 4,
        input_output_aliases={} if prev is None else {first + k: k for k in range(4)},
        compiler_params=_params(("parallel",), VMEM_BIG), name=name,
    )(parts, w, m, v, *extra)


def _arrived(x, name, dep=None):
    deps = [] if dep is None else [dep]

    def body(*refs):
        refs[-1][...] = jnp.zeros_like(refs[-1])

    return pl.pallas_call(
        body, in_specs=[ANY_SPEC] * (1 + len(deps)), out_specs=pl.BlockSpec(memory_space=pltpu.VMEM),
        out_shape=jax.ShapeDtypeStruct((8, 128), F32), name=name,
    )(x, *deps)


def _sum_parts(parts, name, dep=None):
    r = parts.shape[1]
    tr = 8
    for cand in (512, 256, 128, 64, 32, 16, 8):
        if r % cand == 0:
            tr = cand
            break
    deps = [] if dep is None else [dep]

    def body(p_ref, *rest):
        g = p_ref[0]
        for d in range(1, NDEV):
            g = g + p_ref[d]
        rest[-1][...] = g

    return pl.pallas_call(
        body, grid=(r // tr,), in_specs=[pl.BlockSpec((NDEV, tr, 128), lambda i: (0, i, 0))] + [ANY_SPEC] * len(deps),
        out_specs=pl.BlockSpec((tr, 128), lambda i: (i, 0)), out_shape=jax.ShapeDtypeStruct((r, 128), F32),
        compiler_params=_params(("parallel",)), name=name,
    )(parts, *deps)


def _sum_unpack(parts, rows, name, dep=None):
    deps = [] if dep is None else [dep]

    def body(p_ref, *outs):
        outs = outs[len(deps):]
        off = 0
        for o_ref, n in zip(outs, rows):
            acc = p_ref[0, off:off + n, :]
            for d in range(1, NDEV):
                acc = acc + p_ref[d, off:off + n, :]
            o_ref[...] = acc
            off += n

    return pl.pallas_call(
        body, grid=(1,), in_specs=[pl.BlockSpec(parts.shape, lambda i: (0, 0, 0))] + [ANY_SPEC] * len(deps),
        out_specs=[pl.BlockSpec((n, 128), lambda i: (0, 0)) for n in rows],
        out_shape=[jax.ShapeDtypeStruct((n, 128), F32) for n in rows],
        compiler_params=_params(("arbitrary",), VMEM_BIG), name=name,
    )(parts, *deps)


def _adam_small(ws, gs, g_specs, ms, vs, name):
    n = len(ws)

    def body(*refs):
        w_r, g_r, m_r, v_r = refs[:n], refs[n:2 * n], refs[2 * n:3 * n], refs[3 * n:4 * n]
        outs = refs[4 * n:]
        for i in range(n):
            g = g_r[i][...]
            outs[4 * i][...] = g
            outs[4 * i + 1][...], outs[4 * i + 2][...], outs[4 * i + 3][...] = _adamw(
                w_r[i][...], g, m_r[i][...], v_r[i][...])

    whole = lambda a: pl.BlockSpec(a.shape, lambda i, nd=a.ndim: (0,) * nd)
    outs = pl.pallas_call(
        body, grid=(1,),
        in_specs=[whole(a) for a in ws] + list(g_specs) + [whole(a) for a in ms] + [whole(a) for a in vs],
        out_specs=[whole(a) for a in ws for _ in range(4)],
        out_shape=[jax.ShapeDtypeStruct(a.shape, F32) for a in ws for _ in range(4)],
        compiler_params=_params(("arbitrary",), VMEM_BIG), name=name,
    )(*ws, *gs, *ms, *vs)
    return [outs[4 * i:4 * i + 4] for i in range(n)]


MASKS = [(mx, my, mc) for mx in (0, 1) for my in (0, 1) for mc in (0, 1)][1:]


def _sc_exchange(name, collective_id, arrays, scatter):
    nt = len(arrays)
    out_type = [jax.ShapeDtypeStruct(a.shape if scatter else (NDEV,) + a.shape, a.dtype) for a in arrays]

    def body(*refs):
        ins, outs = refs[:nt], refs[nt:2 * nt]
        send_sems, recv_sems, local_sems = refs[2 * nt:3 * nt], refs[3 * nt:4 * nt], refs[4 * nt:5 * nt]
        x, y, c = lax.axis_index("x"), lax.axis_index("y"), lax.axis_index("c")
        peers = [(mx + x - 2 * mx * x, my + y - 2 * my * y, mc + c - 2 * mc * c) for mx, my, mc in MASKS]
        barrier = pltpu.get_barrier_semaphore()
        for peer in peers:
            pl.semaphore_signal(barrier, inc=1, device_id=peer, device_id_type=MESH)
        pl.semaphore_wait(barrier, len(peers))
        me = 4 * x + 2 * y + c
        own = []
        for t in range(nt):
            cp = pltpu.make_async_copy(ins[t].at[me] if scatter else ins[t], outs[t].at[me], local_sems[t])
            cp.start()
            own.append(cp)
            for px, py, pc in peers:
                src = ins[t].at[4 * px + 2 * py + pc] if scatter else ins[t]
                pltpu.make_async_remote_copy(src_ref=src, dst_ref=outs[t].at[me], send_sem=send_sems[t],
                                             recv_sem=recv_sems[t], device_id=(px, py, pc), device_id_type=MESH).start()
        for t in range(nt):
            own[t].wait()
            seven = outs[t].at[pl.ds(0, NDEV - 1)]
            drain = pltpu.make_async_remote_copy(src_ref=seven, dst_ref=seven, send_sem=send_sems[t],
                                                 recv_sem=recv_sems[t], device_id=(x, y, c), device_id_type=MESH)
            drain.wait_send()
            drain.wait_recv()

    return pl.kernel(
        body, out_type=out_type, mesh=plsc.ScalarSubcoreMesh(axis_name="sequencer", num_cores=1),
        scratch_types=[pltpu.SemaphoreType.DMA] * (3 * nt),
        compiler_params=pltpu.CompilerParams(collective_id=collective_id), name=name,
    )(*arrays)


def _sc_gather_two_level(name, collective_id, arrays):
    nt = len(arrays)
    out_type = [jax.ShapeDtypeStruct((NDEV,) + a.shape, a.dtype) for a in arrays]

    def body(*refs):
        ins, outs = refs[:nt], refs[nt:2 * nt]
        sems = refs[2 * nt:]
        send_sems, sib_sems, local_sems = sems[:nt], sems[nt:2 * nt], sems[2 * nt:3 * nt]
        ici_sems = [sems[3 * nt + 3 * t:3 * nt + 3 * t + 3] for t in range(nt)]
        x, y, c = lax.axis_index("x"), lax.axis_index("y"), lax.axis_index("c")
        sibling = (x, y, 1 - c)
        chips = [(1 - x, y), (x, 1 - y), (1 - x, 1 - y)]
        barrier = pltpu.get_barrier_semaphore()
        for peer in [sibling] + [(cx, cy, c) for cx, cy in chips]:
            pl.semaphore_signal(barrier, inc=1, device_id=peer, device_id_type=MESH)
        pl.semaphore_wait(barrier, 4)
        me = 4 * x + 2 * y + c

        def push(t, src, slot, recv_sem, to):
            pltpu.make_async_remote_copy(src_ref=src, dst_ref=outs[t].at[slot], send_sem=send_sems[t],
                                         recv_sem=recv_sem, device_id=to, device_id_type=MESH).start()

        own = []
        for t in range(nt):
            cp = pltpu.make_async_copy(ins[t], outs[t].at[me], local_sems[t])
            cp.start()
            own.append(cp)
            for j, (cx, cy) in enumerate(chips):
                push(t, ins[t], me, ici_sems[t][j], (cx, cy, c))
            push(t, ins[t], me, sib_sems[t], sibling)
        for t in range(nt):
            for j, (cx, cy) in enumerate(chips):
                slot = 4 * cx + 2 * cy + c
                landed = outs[t].at[slot]
                pltpu.make_async_remote_copy(src_ref=landed, dst_ref=landed, send_sem=send_sems[t],
                                             recv_sem=ici_sems[t][j], device_id=(cx, cy, c),
                                             device_id_type=MESH).wait_recv()
                push(t, landed, slot, sib_sems[t], sibling)
        for t in range(nt):
            own[t].wait()
            four, seven = outs[t].at[pl.ds(0, 4)], outs[t].at[pl.ds(0, 7)]
            pltpu.make_async_remote_copy(src_ref=four, dst_ref=four, send_sem=send_sems[t], recv_sem=sib_sems[t],
                                         device_id=sibling, device_id_type=MESH).wait_recv()
            pltpu.make_async_remote_copy(src_ref=seven, dst_ref=seven, send_sem=send_sems[t], recv_sem=sib_sems[t],
                                         device_id=sibling, device_id_type=MESH).wait_send()

    return pl.kernel(
        body, out_type=out_type, mesh=plsc.ScalarSubcoreMesh(axis_name="sequencer", num_cores=1),
        scratch_types=[pltpu.SemaphoreType.DMA] * (6 * nt),
        compiler_params=pltpu.CompilerParams(collective_id=collective_id), name=name,
    )(*arrays)


def _sc_sibling_exchange(name, collective_id, src, out_shape, pieces, after=None):
    extra = [] if after is None else [after]

    def body(src_ref, *rest):
        out_ref, send_sem, recv_sem = rest[len(extra):]
        x, y, c = lax.axis_index("x"), lax.axis_index("y"), lax.axis_index("c")
        sibling = (x, y, 1 - c)
        barrier = pltpu.get_barrier_semaphore()
        pl.semaphore_signal(barrier, inc=1, device_id=sibling, device_id_type=MESH)
        pl.semaphore_wait(barrier, 1)
        for piece, lands in pieces(c, src_ref, out_ref):
            pltpu.make_async_remote_copy(src_ref=piece, dst_ref=lands, send_sem=send_sem, recv_sem=recv_sem,
                                         device_id=sibling, device_id_type=MESH).start()
        drain = pltpu.make_async_remote_copy(src_ref=out_ref, dst_ref=out_ref, send_sem=send_sem, recv_sem=recv_sem,
                                             device_id=sibling, device_id_type=MESH)
        drain.wait_send()
        drain.wait_recv()

    return pl.kernel(
        body, out_type=jax.ShapeDtypeStruct(out_shape, src.dtype),
        mesh=plsc.ScalarSubcoreMesh(axis_name="sequencer", num_cores=1), scratch_types=[pltpu.SemaphoreType.DMA] * 2,
        compiler_params=pltpu.CompilerParams(collective_id=collective_id), name=name,
    )(src, *extra)


def _swap_class_columns(name, collective_id, dz, nb, piece=0, npieces=1):
    w = nb // npieces
    return _sc_sibling_exchange(
        name, collective_id, dz, (S, 4 * w),
        lambda c, src, out: [(src.at[:, pl.ds((2 * j + 1 - c) * nb + piece * w, w)], out.at[:, pl.ds(j * w, w)])
                             for j in range(4)])


def _sc_chip_scatter(name, collective_id, q):
    def body(q_ref, out_ref, send_sem, recv_sem, local_sem):
        x, y, c = lax.axis_index("x"), lax.axis_index("y"), lax.axis_index("c")
        chips = [(1 - x, y), (x, 1 - y), (1 - x, 1 - y)]
        barrier = pltpu.get_barrier_semaphore()
        for cx, cy in chips:
            pl.semaphore_signal(barrier, inc=1, device_id=(cx, cy, c), device_id_type=MESH)
        pl.semaphore_wait(barrier, 3)
        mine = 2 * x + y
        own = pltpu.make_async_copy(q_ref.at[mine], out_ref.at[mine], local_sem)
        own.start()
        for cx, cy in chips:
            pltpu.make_async_remote_copy(src_ref=q_ref.at[2 * cx + cy], dst_ref=out_ref.at[mine], send_sem=send_sem,
                                         recv_sem=recv_sem, device_id=(cx, cy, c), device_id_type=MESH).start()
        own.wait()
        three = out_ref.at[pl.ds(0, 3)]
        drain = pltpu.make_async_remote_copy(src_ref=three, dst_ref=three, send_sem=send_sem, recv_sem=recv_sem,
                                             device_id=(x, y, c), device_id_type=MESH)
        drain.wait_send()
        drain.wait_recv()

    return pl.kernel(
        body, out_type=jax.ShapeDtypeStruct(q.shape, q.dtype),
        mesh=plsc.ScalarSubcoreMesh(axis_name="sequencer", num_cores=1), scratch_types=[pltpu.SemaphoreType.DMA] * 3,
        compiler_params=pltpu.CompilerParams(collective_id=collective_id), name=name,
    )(q)


def _mm_pair_dw(h_own, dz, h_sib, dz_sib, nb, name, dep=None, piece=0, npieces=1, h_transposed=False,
                one_call=False):
    nb = nb // npieces
    tn = 512 if nb % 512 == 0 else nb
    per = nb // tn
    dn = NN if h_transposed else TN
    o_spec = pl.BlockSpec((None, D, tn), lambda i, j, k: (j // per, 0, j % per))
    own_col = lambda i, j, k: (0, ((2 * (j // per) + lax.axis_index("c")) * npieces + piece) * per + j % per)
    if one_call:
        def fused(a0_ref, b0_ref, a1_ref, b1_ref, dep_ref, o_ref):
            acc = _dot(a0_ref[...], b0_ref[...], dn) + _dot(a1_ref[...], b1_ref[...], dn)
            o_ref[...] = acc.astype(BF16)

        whole = pl.BlockSpec((S, D), lambda i, j, k: (0, 0), pipeline_mode=pl.Buffered(1))
        return pl.pallas_call(
            fused, grid=(1, 4 * per, 1),
            in_specs=[whole, pl.BlockSpec((S, tn), own_col), whole, pl.BlockSpec((S, tn), lambda i, j, k: (0, j)),
                      ANY_SPEC],
            out_specs=o_spec, out_shape=jax.ShapeDtypeStruct((4, D, nb), BF16),
            compiler_params=_params(("parallel", "parallel", "arbitrary"), VMEM_BIG), name=name,
        )(h_own, dz, h_sib, dz_sib, dep)
    part = _matmul(
        h_own, dz, dn=dn, grid=(1, 4 * per, 1),
        a_spec=pl.BlockSpec((S, D), lambda i, j, k: (0, 0)), b_spec=pl.BlockSpec((S, tn), own_col),
        o_spec=o_spec, out_shape=(4, D, nb), out_dtype=F32, acc_shape=(D, tn), name=name + "_own", dep=dep)

    def body(a_ref, b_ref, p_ref, o_ref):
        o_ref[...] = (p_ref[...] + _dot(a_ref[...], b_ref[...], dn)).astype(BF16)

    return pl.pallas_call(
        body, grid=(1, 4 * per, 1),
        in_specs=[pl.BlockSpec((S, D), lambda i, j, k: (0, 0)), pl.BlockSpec((S, tn), lambda i, j, k: (0, j)), o_spec],
        out_specs=o_spec, out_shape=jax.ShapeDtypeStruct((4, D, nb), BF16),
        compiler_params=_params(("parallel", "parallel", "arbitrary"), VMEM_BIG), name=name + "_sibling",
    )(h_sib, dz_sib, part)


SMALL = {
    "e_pre_norm": ((2048,), None), "e_pool_w": ((4, 256, 256), 1), "e_pool_scale": ((1024,), None),
    "e_post_norm": ((2048,), None), "o_pre_norm": ((2048,), 0), "o_sgu_norm_g": ((1024,), 0),
    "o_sgu_norm_b": ((1024,), 0), "o_sgu_w": ((4, 128, 128), None), "o_sgu_b": ((4, 128), None),
    "o_conv_w": ((31, 1024), 1), "o_conv_b": ((1024,), 0), "o_conv_norm_g": ((1024,), 0),
    "o_conv_norm_b": ((1024,), 0), "o_post_norm": ((2048,), 0),
}
SMALL_SHARDED = [n for n, (_, ax) in SMALL.items() if ax is not None]


def _shard_shape(name):
    shape, ax = SMALL[name]
    if ax is None:
        return shape
    return tuple(s // NDEV if i == ax else s for i, s in enumerate(shape))


def _pack(arrs, row_multiple=1):
    flat = jnp.concatenate([a.reshape(-1) for a in arrs])
    pad = -flat.shape[0] % (128 * row_multiple)
    return jnp.concatenate([flat, jnp.zeros((pad,), F32)]).reshape(-1, 128)


def _small_views(name):
    shape, ax = SMALL[name]
    me = lambda: 4 * lax.axis_index("x") + 2 * lax.axis_index("y") + lax.axis_index("c")
    if ax is None:
        view = (int(np.prod(shape)) // 128, 128)
        return view, view, pl.BlockSpec(view, lambda i: (0, 0))
    if len(shape) == 1:
        n = shape[0] // NDEV
        return (1, n), (NDEV, 1, n), pl.BlockSpec((None, 1, n), lambda i: (me(), 0, 0))
    part = _shard_shape(name)
    return part, shape, pl.BlockSpec(part, lambda i: tuple(me() if d == ax else 0 for d in range(len(shape))))


WEIGHTS = ["e_pre_norm", "e_w_in", "e_pool_w", "e_pool_scale", "e_w_out", "e_post_norm", "o_pre_norm", "o_w_in",
           "o_sgu_norm_g", "o_sgu_norm_b", "o_sgu_w", "o_sgu_b", "o_conv_w", "o_conv_b", "o_conv_norm_g",
           "o_conv_norm_b", "o_w_out", "o_post_norm"]


def kernel(x, e_pre_norm, e_w_in, e_pool_w, e_pool_scale, e_w_out, e_post_norm, o_pre_norm, o_w_in, o_sgu_norm_g, o_sgu_norm_b, o_sgu_w, o_sgu_b, o_conv_w, o_conv_b, o_conv_norm_g, o_conv_norm_b, o_w_out, o_post_norm, loss_target, m_e_pre_norm, m_e_w_in, m_e_pool_w, m_e_pool_scale, m_e_w_out, m_e_post_norm, m_o_pre_norm, m_o_w_in, m_o_sgu_norm_g, m_o_sgu_norm_b, m_o_sgu_w, m_o_sgu_b, m_o_conv_w, m_o_conv_b, m_o_conv_norm_g, m_o_conv_norm_b, m_o_w_out, m_o_post_norm, v_e_pre_norm, v_e_w_in, v_e_pool_w, v_e_pool_scale, v_e_w_out, v_e_post_norm, v_o_pre_norm, v_o_w_in, v_o_sgu_norm_g, v_o_sgu_norm_b, v_o_sgu_w, v_o_sgu_b, v_o_conv_w, v_o_conv_b, v_o_conv_norm_g, v_o_conv_norm_b, v_o_w_out, v_o_post_norm):
    given = dict(locals())
    w = {n: given[n][0] for n in WEIGHTS}
    m = {n: given["m_" + n][0] for n in WEIGHTS}
    v = {n: given["v_" + n][0] for n in WEIGHTS}
    me = 4 * lax.axis_index("x") + 2 * lax.axis_index("y") + lax.axis_index("c")
    x, target = x[0], loss_target[0]
    row = lambda a: a.reshape(1, -1)

    lo, small_rows = _sc_gather_two_level(
        "gather_a0", 0, [_cast_bf16(w["e_w_in"], "cast_e_w_in_0", 0, 2), _pack([w[n] for n in SMALL_SHARDED])])
    hi, = _sc_gather_two_level("gather_a1", 12, [_cast_bf16(w["e_w_in"], "cast_e_w_in_1", 1, 2)])
    wg_e_in = (lo, hi)
    h0, h0t = _pre0_fwd(x, row(w["e_pre_norm"]))
    wg_e_out, = _sc_gather_two_level("gather_b", 1, [_cast_bf16(w["e_w_out"], "cast_e_w_out")])
    wg_o_in, = _sc_gather_two_level("gather_c", 13, [_cast_bf16(w["o_w_in"], "cast_o_w_in")])
    wg_o_out, = _sc_gather_two_level("gather_d", 16, [_cast_bf16(w["o_w_out"], "cast_o_w_out")])
    p = {n: w[n] for n in SMALL if SMALL[n][1] is None}
    small_rows = small_rows.reshape(NDEV, -1)
    off = 0
    for n in SMALL_SHARDED:
        shp, ax = _shard_shape(n), SMALL[n][1]
        cnt = int(np.prod(shp))
        blk = small_rows[:, off:off + cnt].reshape((NDEV,) + shp)
        p[n] = jnp.moveaxis(blk, 0, ax).reshape(SMALL[n][0])
        off += cnt
    tabs = _rope_tables()
    pool_w_bf = p["e_pool_w"].astype(BF16)
    sgu_bb = jnp.broadcast_to(p["o_sgu_b"][:, :, None], (4, 128, 128))
    conv_w = jnp.concatenate([p["o_conv_w"], jnp.zeros((HALO - CONV_K, HALF), F32)], axis=0)
    odd_p = (row(p["o_sgu_norm_g"]), row(p["o_sgu_norm_b"]), p["o_sgu_w"], sgu_bb, conv_w,
             row(p["o_conv_b"]), row(p["o_conv_norm_g"]), row(p["o_conv_norm_b"]))

    z0 = _mm_in_halves(h0, wg_e_in, "mm_z0")
    ycat0 = _pool_fwd(z0, pool_w_bf, row(p["e_pool_scale"]))
    ycat0, og, lg, qkv = _attn_fwd(z0, tabs, ycat0)
    w_out_e, w_out_o = wg_e_out.reshape(2048, D), wg_o_out.reshape(2048, D)
    y0, x1, h1 = _post0_fwd(ycat0, w_out_e, x, row(p["e_post_norm"]), row(p["o_pre_norm"]), wg_e_out)
    h0t_sib = _sc_sibling_exchange("swap_h0", 8, h0t, h0t.shape, lambda c, src, out: [(src, out)], h1)
    h1_sib = _sc_sibling_exchange("swap_h1", 11, h1, h1.shape, lambda c, src, out: [(src, out)])
    z1 = _mm_in(h1, wg_o_in, "mm_z1")
    ycat1, conv_out = _odd_fwd(z1, *odd_p)

    g = {}
    loss_part, dx2, dy1, g["o_post_norm"] = _post1_bwd(ycat1, w_out_o, x1, target, row(p["o_post_norm"]),
                                                       _arrived(h1_sib, "arrived_h_sib", h0t_sib))
    parts = {}
    dw = _mm_out_dw(ycat1, dy1, "mm_dwout1").reshape(NDEV, 256, D)
    parts["o_w_out"], = _sc_exchange("scatter_o_w_out", 2, [dw], True)
    dycat1 = _mm_out_dx(dy1, w_out_o, "mm_dycat1", dw)
    dz1, g["o_sgu_w"], d_sgu_bb, g["o_sgu_norm_g"], g["o_sgu_norm_b"], g["o_conv_norm_g"], g["o_conv_norm_b"], \
        g["o_conv_b"], d_conv_w = _odd_bwd(z1, conv_out, dycat1, *odd_p[:4], conv_w, *odd_p[6:])
    g["o_sgu_b"] = d_sgu_bb[:, :, 0]
    g["o_conv_w"] = d_conv_w[:CONV_K]
    grads, deltas, new_m, new_v = {}, {}, {}, {}

    def adam(n, dep):
        grads[n], deltas[n], new_m[n], new_v[n] = _adam_reduce(parts[n], w[n], m[n], v[n], "adam_" + n, dep)
        return new_v[n]

    pin = _arrived(parts["o_w_out"], "arrived_o_w_out", d_conv_w)
    dz1_sib = _swap_class_columns("swap_dz1", 10, dz1, ODD_IN // NDEV)
    dw = _mm_pair_dw(h1, dz1, h1_sib, dz1_sib, ODD_IN // NDEV, "mm_dwin1", pin)
    parts["o_w_in"] = _sc_chip_scatter("scatter_o_w_in", 3, dw)
    dh1 = _mm_in_dx(dz1, wg_o_in, "mm_dh1", dw)
    dx1, dy0, g["o_pre_norm"], g["e_post_norm"] = _mid_bwd(dx2, dh1, x1, y0, row(p["o_pre_norm"]),
                                                           row(p["e_post_norm"]))
    dw = _mm_out_dw(ycat0, dy0, "mm_dwout0").reshape(NDEV, 256, D)
    parts["e_w_out"], = _sc_exchange("scatter_e_w_out", 4, [dw], True)
    da_in, da_gate, g["e_pool_w"], g["e_pool_scale"] = _pool_bwd(z0, dy0, w_out_e, pool_w_bf,
                                                                 row(p["e_pool_scale"]), dw)
    late = [n for n in SMALL if n not in ("e_pre_norm", "o_sgu_b")] + ["o_sgu_b"]
    pieces = [g[n].reshape(SMALL[n][0]) for n in late[:-1]] + [jnp.broadcast_to(loss_part, (8, 128)), g[late[-1]]]
    recv_small, = _sc_gather_two_level("gather_small_grads", 6, [_pack(pieces, 512)])
    took = _arrived(parts["o_w_in"], "arrived_o_w_in")
    dq, dk, dv, dbg = _attn_bwd(z0, qkv, og, lg, dy0, w_out_e, tabs, took)
    dz0 = jnp.concatenate([da_in, da_gate, dq, dk, dv, dbg], axis=1)
    took = _arrived(recv_small, "arrived_small_grads", _arrived(parts["e_w_out"], "arrived_e_w_out", dz0))
    nb = EVEN_IN // NDEV
    swapped = [_swap_class_columns("swap_dz0_%d" % half, (9, 14)[half], dz0, nb, half, 2) for half in (0, 1)]
    dw, e_w_in_parts = took, []
    for half in (0, 1):
        dw = _mm_pair_dw(h0t, dz0, h0t_sib, swapped[half], nb, "mm_dwin0_%d" % half, dw, half, 2, True, half == 1)
        e_w_in_parts.append(_sc_chip_scatter("scatter_e_w_in_%d" % half, (5, 15)[half], dw))
    pin = adam("e_w_out", adam("o_w_out", adam("o_w_in", dw)))
    rows = [int(np.prod(SMALL[n][0])) // 128 for n in late]
    sums = _sum_unpack(recv_small, rows[:-1] + [8, rows[-1]], "sum_small_grads", pin)
    summed = dict(zip(late, sums[:-2] + sums[-1:]))
    loss = sums[-2][0, 0]
    dh0 = _mm_in_dx_halves(dz0, wg_e_in, "mm_dh0", summed[late[0]])
    grad_x, g["e_pre_norm"] = _pre0_bwd(dx1, dh0, x, row(p["e_pre_norm"]))
    last, = _sc_exchange("gather_e_pre_norm_grad", 7, [g["e_pre_norm"].reshape(16, 128)], False)

    n = "e_w_in"
    out = _adam_reduce(e_w_in_parts[0], w[n], m[n], v[n], "adam_e_w_in_0", grad_x, 0, 2)
    out = _adam_reduce(e_w_in_parts[1], w[n], m[n], v[n], "adam_e_w_in_1", None, 1, 2, out)
    grads[n], deltas[n], new_m[n], new_v[n] = out
    summed["e_pre_norm"] = _sum_parts(last, "sum_e_pre_norm_grad", out[3])
    names = list(SMALL)
    views = [_small_views(n) for n in names]
    mine = lambda src: [src[n].reshape(vw[0]) for n, vw in zip(names, views)]
    res = _adam_small(mine(w), [summed[n].reshape(vw[1]) for n, vw in zip(names, views)], [vw[2] for vw in views],
                      mine(m), mine(v), "adam_small")
    for n, out in zip(names, res):
        grads[n], deltas[n], new_m[n], new_v[n] = [t.reshape(_shard_shape(n)) for t in out]

    lead = lambda a: a[None]
    return (loss, grad_x[None], *[lead(grads[n]) for n in WEIGHTS], *[lead(deltas[n]) for n in WEIGHTS],
            *[lead(new_m[n]) for n in WEIGHTS], *[lead(new_v[n]) for n in WEIGHTS])
```

```python
import numpy as np
import jax
import jax.numpy as jnp
from jax import lax
from jax.experimental import pallas as pl
from jax.experimental.pallas import tpu as pltpu
from jax.experimental.pallas import tpu_sc as plsc

F32 = jnp.float32
BF16 = jnp.bfloat16

S = 2048
D = 2048
NDEV = 8
EPS = 1e-6
NEG = -1e30
HEAD_DIM = 128
ROT_DIM = 32
ROPE_THETA = 500000.0
PATTERNS = ((128, 1), (512, 4), (2048, 16))
BLK = 128
EVEN_IN = 12288
ODD_IN = 6144
HALF = 1024
CONV_K = 31
HALO = 32
TR = 256
SUB = 16

ADAM_LR = 0.001
ADAM_B1 = 0.9
ADAM_B2 = 0.999
ADAM_EPS = 1e-08
ADAM_WD = 0.01
ADAM_STEP = 10

VMEM_BIG = 56 * 1024 * 1024
MESH = pl.DeviceIdType.MESH

NN = (((1,), (0,)), ((), ()))
NT = (((1,), (1,)), ((), ()))
TN = (((0,), (0,)), ((), ()))


def _dot(a, b, dn=NN):
    return lax.dot_general(a, b, dn, preferred_element_type=F32)


def _sigmoid(x):
    return 1.0 / (1.0 + jnp.exp(-x))


def _silu_and_grad(x):
    sg = _sigmoid(x)
    return x * sg, sg * (1.0 + x * (1.0 - sg))


def _params(sem, vmem=None):
    return pltpu.CompilerParams(dimension_semantics=sem, vmem_limit_bytes=vmem)


ANY_SPEC = pl.BlockSpec(memory_space=pl.ANY)


def _matmul(a, b, *, dn, grid, a_spec, b_spec, o_spec, out_shape, out_dtype, acc_shape, name, dep=None):
    nk = grid[2]
    deps = [] if dep is None else list(dep) if isinstance(dep, (tuple, list)) else [dep]

    def body(a_ref, b_ref, *rest):
        o_ref, acc = rest[len(deps)], rest[len(deps) + 1:]
        if nk == 1:
            o_ref[...] = _dot(a_ref[...], b_ref[...], dn).astype(o_ref.dtype)
            return
        acc_ref = acc[0]
        k = pl.program_id(2)

        @pl.when(k == 0)
        def _():
            acc_ref[...] = jnp.zeros_like(acc_ref)

        acc_ref[...] += _dot(a_ref[...], b_ref[...], dn)

        @pl.when(k == nk - 1)
        def _():
            o_ref[...] = acc_ref[...].astype(o_ref.dtype)

    return pl.pallas_call(
        body, grid=grid, in_specs=[a_spec, b_spec] + [ANY_SPEC] * len(deps), out_specs=o_spec,
        out_shape=jax.ShapeDtypeStruct(out_shape, out_dtype),
        scratch_shapes=[] if nk == 1 else [pltpu.VMEM(acc_shape, F32)],
        compiler_params=_params(("parallel", "parallel", "arbitrary"), VMEM_BIG), name=name,
    )(a, b, *deps)


TM = 2048


def _mm_in(h, wg, name):
    nb = wg.shape[2]
    tn = 512 if nb % 512 == 0 else nb
    per = nb // tn
    return _matmul(
        h, wg, dn=NN, grid=(S // TM, NDEV * per, 1),
        a_spec=pl.BlockSpec((TM, D), lambda i, j, k: (i, 0)),
        b_spec=pl.BlockSpec((None, D, tn), lambda i, j, k: (j // per, 0, j % per)),
        o_spec=pl.BlockSpec((TM, tn), lambda i, j, k: (i, j)),
        out_shape=(S, NDEV * nb), out_dtype=F32, acc_shape=(TM, tn), name=name)


def _mm_in_halves(h, wg_halves, name):
    hb = wg_halves[0].shape[2]
    z = None
    for half, wg in enumerate(wg_halves):
        prev = [] if z is None else [z]

        def body(a_ref, b_ref, *rest):
            rest[-1][...] = _dot(a_ref[...], b_ref[...])

        z = pl.pallas_call(
            body, grid=(NDEV,),
            in_specs=[pl.BlockSpec((S, D), lambda j: (0, 0)), pl.BlockSpec((None, D, hb), lambda j: (j, 0, 0))]
                     + [ANY_SPEC] * len(prev),
            out_specs=pl.BlockSpec((S, hb), lambda j, half=half: (0, 2 * j + half)),
            out_shape=jax.ShapeDtypeStruct((S, 2 * NDEV * hb), F32),
            input_output_aliases={2: 0} if prev else {},
            compiler_params=_params(("parallel",), VMEM_BIG), name="%s_%d" % (name, half),
        )(h, wg, *prev)
    return z


def _mm_in_dx_halves(dz, wg_halves, name, dep):
    hb = wg_halves[0].shape[2]
    nk = 2 * NDEV

    def body(a_ref, b0_ref, b1_ref, dep_ref, o_ref, acc_ref):
        k = pl.program_id(2)

        @pl.when(k == 0)
        def _():
            acc_ref[...] = jnp.zeros_like(acc_ref)

        @pl.when(k % 2 == 0)
        def _():
            acc_ref[...] += _dot(a_ref[...], b0_ref[...], NT)

        @pl.when(k % 2 == 1)
        def _():
            acc_ref[...] += _dot(a_ref[...], b1_ref[...], NT)

        @pl.when(k == nk - 1)
        def _():
            o_ref[...] = acc_ref[...]

    b_spec = pl.BlockSpec((None, 1024, hb), lambda i, j, k: (k // 2, j, 0))
    return pl.pallas_call(
        body, grid=(1, D // 1024, nk),
        in_specs=[pl.BlockSpec((S, hb), lambda i, j, k: (0, k)), b_spec, b_spec, ANY_SPEC],
        out_specs=pl.BlockSpec((S, 1024), lambda i, j, k: (0, j)), out_shape=jax.ShapeDtypeStruct((S, D), F32),
        scratch_shapes=[pltpu.VMEM((S, 1024), F32)],
        compiler_params=_params(("parallel", "parallel", "arbitrary"), VMEM_BIG), name=name,
    )(dz, *wg_halves, dep)


def _mm_in_dx(dz, wg, name, dep=None):
    nb = wg.shape[2]
    return _matmul(
        dz, wg, dn=NT, grid=(S // TM, D // 1024, NDEV),
        a_spec=pl.BlockSpec((TM, nb), lambda i, j, k: (i, k)),
        b_spec=pl.BlockSpec((None, 1024, nb), lambda i, j, k: (k, j, 0)),
        o_spec=pl.BlockSpec((TM, 1024), lambda i, j, k: (i, j)),
        out_shape=(S, D), out_dtype=F32, acc_shape=(TM, 1024), name=name, dep=dep)


def _mm_out_dx(dy, w, name, dep=None):
    return _matmul(
        dy, w, dn=NT, grid=(S // TM, 2048 // 512, 1),
        a_spec=pl.BlockSpec((TM, D), lambda i, j, k: (i, 0)),
        b_spec=pl.BlockSpec((512, D), lambda i, j, k: (j, 0)),
        o_spec=pl.BlockSpec((TM, 512), lambda i, j, k: (i, j)),
        out_shape=(S, 2048), out_dtype=F32, acc_shape=(TM, 512), name=name, dep=dep)


def _mm_out_dw(yc, dy, name):
    return _matmul(
        yc, dy, dn=TN, grid=(2048 // TM, D // 512, 1),
        a_spec=pl.BlockSpec((S, TM), lambda i, j, k: (0, i)),
        b_spec=pl.BlockSpec((S, 512), lambda i, j, k: (0, j)),
        o_spec=pl.BlockSpec((TM, 512), lambda i, j, k: (i, j)),
        out_shape=(2048, D), out_dtype=BF16, acc_shape=(TM, 512), name=name)


def _row_spec(w=D):
    return pl.BlockSpec((TR, w), lambda i: (i, 0))


def _vec_spec(w=D):
    return pl.BlockSpec((1, w), lambda i: (0, 0))


def _rms_stats(x):
    r = lax.rsqrt(jnp.mean(x * x, axis=-1, keepdims=True) + EPS)
    return x * r, r


def _rms_bwd(dn, xhat, r, g):
    dxh = dn * g
    return r * (dxh - xhat * jnp.mean(dxh * xhat, axis=-1, keepdims=True))


def _acc_rows(ref, val, i):
    s = jnp.sum(val, axis=0, keepdims=True)

    @pl.when(i == 0)
    def _():
        ref[...] = s

    @pl.when(i > 0)
    def _():
        ref[...] += s


def _pre0_fwd(x, g):
    def body(x_ref, g_ref, h_ref, ht_ref):
        xhat, _ = _rms_stats(x_ref[...])
        h = xhat * g_ref[...]
        h_ref[...] = h.astype(BF16)
        ht_ref[...] = h.T.astype(BF16)

    return pl.pallas_call(
        body, grid=(S // TR,), in_specs=[_row_spec(), _vec_spec()],
        out_specs=[_row_spec(), pl.BlockSpec((D, TR), lambda i: (0, i))],
        out_shape=[jax.ShapeDtypeStruct((S, D), BF16), jax.ShapeDtypeStruct((D, S), BF16)],
        compiler_params=_params(("parallel",)), name="pre0_fwd",
    )(x, g)


def _post0_fwd(ycat, w_out, x, g_post, g_pre1, dep):
    def body(yc_ref, w_ref, x_ref, gp_ref, g1_ref, dep_ref, y_ref, x1_ref, h1_ref):
        y = _dot(yc_ref[...], w_ref[...])
        y_ref[...] = y
        yhat, _ = _rms_stats(y)
        x1 = x_ref[...] + yhat * gp_ref[...]
        x1_ref[...] = x1
        xhat, _ = _rms_stats(x1)
        h1_ref[...] = (xhat * g1_ref[...]).astype(BF16)

    return pl.pallas_call(
        body, grid=(S // TR,),
        in_specs=[_row_spec(), pl.BlockSpec((2048, D), lambda i: (0, 0)), _row_spec(), _vec_spec(), _vec_spec(),
                  ANY_SPEC],
        out_specs=[_row_spec(), _row_spec(), _row_spec()],
        out_shape=[jax.ShapeDtypeStruct((S, D), F32), jax.ShapeDtypeStruct((S, D), F32),
                   jax.ShapeDtypeStruct((S, D), BF16)],
        compiler_params=_params(("parallel",), VMEM_BIG), name="post0_fwd",
    )(ycat, w_out, x, g_post, g_pre1, dep)


def _post1_bwd(ycat, w_out, x1, target, g_post, dep):
    def body(yc_ref, w_ref, x1_ref, t_ref, g_ref, dep_ref, loss_ref, dx2_ref, dy_ref, dg_ref):
        i = pl.program_id(0)
        yhat, r = _rms_stats(_dot(yc_ref[...], w_ref[...]))
        g = g_ref[...]
        err = x1_ref[...] + yhat * g - t_ref[...]
        part = jnp.sum(jnp.sum(err * err, axis=-1, keepdims=True), axis=0, keepdims=True) * (0.5 / D)
        _acc_rows(loss_ref, jnp.broadcast_to(part, (1, 128)), i)
        dx2 = err * (1.0 / D)
        dx2_ref[...] = dx2
        _acc_rows(dg_ref, dx2 * yhat, i)
        dy_ref[...] = _rms_bwd(dx2, yhat, r, g).astype(BF16)

    return pl.pallas_call(
        body, grid=(S // TR,),
        in_specs=[_row_spec(), pl.BlockSpec((2048, D), lambda i: (0, 0)), _row_spec(), _row_spec(), _vec_spec(),
                  ANY_SPEC],
        out_specs=[_vec_spec(128), _row_spec(), _row_spec(), _vec_spec()],
        out_shape=[jax.ShapeDtypeStruct((1, 128), F32), jax.ShapeDtypeStruct((S, D), F32),
                   jax.ShapeDtypeStruct((S, D), BF16), jax.ShapeDtypeStruct((1, D), F32)],
        compiler_params=_params(("arbitrary",), VMEM_BIG), name="post1_bwd",
    )(ycat, w_out, x1, target, g_post, dep)


def _mid_bwd(dx2, dh1, x1, y0, g_pre1, g_post0):
    def body(dx2_ref, dh_ref, x1_ref, y_ref, g1_ref, gp_ref, dx1_ref, dy_ref, dg1_ref, dgp_ref):
        i = pl.program_id(0)
        xhat, r1 = _rms_stats(x1_ref[...])
        dh = dh_ref[...]
        _acc_rows(dg1_ref, dh * xhat, i)
        dx1 = dx2_ref[...] + _rms_bwd(dh, xhat, r1, g1_ref[...])
        dx1_ref[...] = dx1
        yhat, r0 = _rms_stats(y_ref[...])
        _acc_rows(dgp_ref, dx1 * yhat, i)
        dy_ref[...] = _rms_bwd(dx1, yhat, r0, gp_ref[...]).astype(BF16)

    return pl.pallas_call(
        body, grid=(S // TR,),
        in_specs=[_row_spec(), _row_spec(), _row_spec(), _row_spec(), _vec_spec(), _vec_spec()],
        out_specs=[_row_spec(), _row_spec(), _vec_spec(), _vec_spec()],
        out_shape=[jax.ShapeDtypeStruct((S, D), F32), jax.ShapeDtypeStruct((S, D), BF16),
                   jax.ShapeDtypeStruct((1, D), F32), jax.ShapeDtypeStruct((1, D), F32)],
        compiler_params=_params(("arbitrary",)), name="mid_bwd",
    )(dx2, dh1, x1, y0, g_pre1, g_post0)


def _pre0_bwd(dx1, dh0, x, g):
    def body(dx1_ref, dh_ref, x_ref, g_ref, gx_ref, dg_ref):
        i = pl.program_id(0)
        xhat, r = _rms_stats(x_ref[...])
        dh = dh_ref[...]
        _acc_rows(dg_ref, dh * xhat, i)
        gx_ref[...] = dx1_ref[...] + _rms_bwd(dh, xhat, r, g_ref[...])

    return pl.pallas_call(
        body, grid=(S // TR,), in_specs=[_row_spec(), _row_spec(), _row_spec(), _vec_spec()],
        out_specs=[_row_spec(), _vec_spec()],
        out_shape=[jax.ShapeDtypeStruct((S, D), F32), jax.ShapeDtypeStruct((1, D), F32)],
        compiler_params=_params(("arbitrary",)), name="pre0_bwd",
    )(dx1, dh0, x, g)


POOL_CH = 256


def _pool_apply(a, w, transpose):
    n = a.shape[0]
    row = lax.broadcasted_iota(jnp.int32, a.shape, 0)
    cnt = jnp.minimum(row + 1, w).astype(F32)
    s = a / cnt if transpose else a
    for k in (1, 2, 4, 8):
        if transpose:
            sh = jnp.where(row < n - k, pltpu.roll(s, n - k, 0), 0.0)
        else:
            sh = jnp.where(row >= k, pltpu.roll(s, k, 0), 0.0)
        s = jnp.where(w > k, s + sh, s)
    return s - a if transpose else s / cnt - a


def _pool_fwd(z0, pool_w, pool_scale):
    def body(a_ref, gate_ref, w_ref, sc_ref, out_ref):
        win = jnp.left_shift(2, pl.program_id(0))
        pooled = _pool_apply(a_ref[...], win, False)
        mixed = _dot(pooled.astype(BF16), w_ref[...])
        gate = gate_ref[...]
        out_ref[...] = (mixed * sc_ref[...] * (gate * _sigmoid(gate))).astype(BF16)

    return pl.pallas_call(
        body, grid=(4,),
        in_specs=[pl.BlockSpec((S, POOL_CH), lambda g: (0, g)), pl.BlockSpec((S, POOL_CH), lambda g: (0, 4 + g)),
                  pl.BlockSpec((None, POOL_CH, POOL_CH), lambda g: (g, 0, 0)),
                  pl.BlockSpec((1, POOL_CH), lambda g: (0, g))],
        out_specs=pl.BlockSpec((S, POOL_CH), lambda g: (0, g)),
        out_shape=jax.ShapeDtypeStruct((S, 2048), BF16),
        compiler_params=_params(("parallel",), VMEM_BIG), name="pool_fwd",
    )(z0, z0, pool_w, pool_scale)


def _pool_bwd(z0, dy0, w_out, pool_w, pool_scale, dep):
    def body(a_ref, gate_ref, dy0_ref, wo_ref, w_ref, sc_ref, dep_ref, da_ref, dgate_ref, dw_ref, dsc_ref):
        win = jnp.left_shift(2, pl.program_id(0))
        pooled = _pool_apply(a_ref[...], win, False).astype(BF16)
        w = w_ref[...]
        mixed = _dot(pooled, w)
        silu, dsilu = _silu_and_grad(gate_ref[...])
        dy = _dot(dy0_ref[...], wo_ref[...], NT)
        sc = sc_ref[...]
        dgate_ref[...] = (dy * (mixed * sc) * dsilu).astype(BF16)
        dms = dy * silu
        dsc_ref[...] = jnp.sum(dms * mixed, axis=0, keepdims=True)
        dmixed = (dms * sc).astype(BF16)
        dw_ref[...] = _dot(pooled, dmixed, TN)
        dpooled = _dot(dmixed, w, NT)
        da_ref[...] = _pool_apply(dpooled, win, True).astype(BF16)

    slab = lambda off: pl.BlockSpec((S, POOL_CH), lambda g: (0, off + g))
    return pl.pallas_call(
        body, grid=(4,),
        in_specs=[slab(0), slab(4), pl.BlockSpec((S, D), lambda g: (0, 0), pipeline_mode=pl.Buffered(1)),
                  pl.BlockSpec((POOL_CH, D), lambda g: (g, 0)),
                  pl.BlockSpec((None, POOL_CH, POOL_CH), lambda g: (g, 0, 0)),
                  pl.BlockSpec((1, POOL_CH), lambda g: (0, g)), ANY_SPEC],
        out_specs=[slab(0), slab(0), pl.BlockSpec((None, POOL_CH, POOL_CH), lambda g: (g, 0, 0)),
                   pl.BlockSpec((1, POOL_CH), lambda g: (0, g))],
        out_shape=[jax.ShapeDtypeStruct((S, HALF), BF16), jax.ShapeDtypeStruct((S, HALF), BF16),
                   jax.ShapeDtypeStruct((4, POOL_CH, POOL_CH), F32), jax.ShapeDtypeStruct((1, HALF), F32)],
        compiler_params=_params(("parallel",), VMEM_BIG), name="pool_bwd",
    )(z0, z0, dy0, w_out, pool_w, pool_scale, dep)


Q_COL, K_COL, V_COL, BG_COL = 2048 // 128, 5120 // 128, 8192 // 128, 11264 // 128
SCALE = HEAD_DIM ** -0.5


def _rope_tables():
    pos = jnp.arange(S, dtype=F32)
    inv_freq = jnp.power(ROPE_THETA, -jnp.arange(0, ROT_DIM, 2, dtype=F32) / ROT_DIM)
    ang = pos[:, None] * inv_freq[None, :]
    cos, sin = jnp.cos(ang), jnp.sin(ang)
    half = ROT_DIM // 2
    zeros = jnp.zeros((S, HEAD_DIM - ROT_DIM), F32)
    c = jnp.concatenate([cos, cos, jnp.ones((S, HEAD_DIM - ROT_DIM), F32)], axis=1)
    a = jnp.concatenate([-sin, jnp.zeros((S, half), F32), zeros], axis=1)
    b = jnp.concatenate([jnp.zeros((S, half), F32), sin, zeros], axis=1)
    return c, a, b


def _rope(t, c, a, b):
    half = ROT_DIM // 2
    return t * c + pltpu.roll(t, HEAD_DIM - half, 1) * a + pltpu.roll(t, half, 1) * b


def _rope_t(d, c, a, b):
    half = ROT_DIM // 2
    return d * c + pltpu.roll(d * a, half, 1) + pltpu.roll(d * b, HEAD_DIM - half, 1)


def _deinterleave(dst, src, dil, cast=None, dst_off=0):
    length = S // dil
    for r in range(dil):
        v = src[...] if dil == 1 else src[pl.ds(r, length, stride=dil), :]
        dst[dst_off + r * length:dst_off + (r + 1) * length, :] = v if cast is None else v.astype(cast)


def _interleave(dst, src, dil, src_off=0):
    length = S // dil
    for r in range(dil):
        if dil == 1:
            dst[...] = src[src_off:src_off + S, :]
        else:
            dst[pl.ds(r, length, stride=dil), :] = src[src_off + r * length:src_off + (r + 1) * length, :]


CU = 8
NUNITS = S // BLK
B_QK = (((2,), (2,)), ((0,), (0,)))
B_PV = (((2,), (1,)), ((0,), (0,)))
B_TN = (((1,), (1,)), ((0,), (0,)))


def _blocks(ref, first):
    return ref[first * BLK:(first + CU) * BLK, :].reshape(CU, BLK, HEAD_DIM)


def _chunk_scores(u0, nb, qd, kdp):
    q = _blocks(qd, u0)
    row = lax.broadcasted_iota(jnp.int32, (CU, BLK, BLK), 1)
    col = lax.broadcasted_iota(jnp.int32, (CU, BLK, BLK), 2)
    s_own = jnp.where(col <= row, _dot(q, _blocks(kdp, u0 + 1), B_QK) * SCALE, NEG)
    if nb == 1:
        return q, s_own, None
    unit = lax.broadcasted_iota(jnp.int32, (CU, BLK, BLK), 0) + u0
    s_prev = jnp.where((col >= row) & ((unit % nb) != 0), _dot(q, _blocks(kdp, u0), B_QK) * SCALE, NEG)
    return q, s_own, s_prev


def _qkv_prep(z0, tabs):
    def body(q_ref, k_ref, v_ref, c_ref, a_ref, b_ref, qo_ref, ko_ref, vo_ref, tmp):
        p = pl.program_id(1)
        ko_ref[0:BLK, :] = jnp.zeros((BLK, HEAD_DIM), BF16)
        vo_ref[0:BLK, :] = jnp.zeros((BLK, HEAD_DIM), BF16)
        for gi, (_, dil) in enumerate(PATTERNS):
            @pl.when(p == gi)
            def _(dil=dil):
                c, a, b = c_ref[...], a_ref[...], b_ref[...]
                tmp[...] = _rope(q_ref[...], c, a, b)
                _deinterleave(qo_ref, tmp, dil, BF16)
                tmp[...] = _rope(k_ref[...], c, a, b)
                _deinterleave(ko_ref, tmp, dil, BF16, BLK)
                _deinterleave(vo_ref, v_ref, dil, BF16, BLK)

    tab = pl.BlockSpec((S, HEAD_DIM), lambda h, p: (0, 0))
    out = pl.BlockSpec((S, HEAD_DIM), lambda h, p: (0, p * 8 + h))
    outp = pl.BlockSpec((S + BLK, HEAD_DIM), lambda h, p: (0, p * 8 + h))
    return pl.pallas_call(
        body, grid=(8, 3), in_specs=[_head_spec(Q_COL), _head_spec(K_COL), _head_spec(V_COL), tab, tab, tab],
        out_specs=[out, outp, outp],
        out_shape=[jax.ShapeDtypeStruct((S, 3072), BF16)] + [jax.ShapeDtypeStruct((S + BLK, 3072), BF16)] * 2,
        scratch_shapes=[pltpu.VMEM((S, HEAD_DIM), F32)],
        compiler_params=_params(("parallel", "arbitrary"), VMEM_BIG), name="qkv_prep",
    )(z0, z0, z0, *tabs)


def _attn_group_fwd(dil, qd, kdp, vdp, od, ld, og, lg):
    nb = S // dil // BLK
    for u0 in range(0, NUNITS, CU):
        _, s_own, s_prev = _chunk_scores(u0, nb, qd, kdp)
        m = jnp.max(s_own, axis=2, keepdims=True)
        if s_prev is not None:
            m = jnp.maximum(m, jnp.max(s_prev, axis=2, keepdims=True))
        p_own = jnp.exp(s_own - m)
        den = jnp.sum(p_own, axis=2, keepdims=True)
        acc = _dot(p_own.astype(BF16), _blocks(vdp, u0 + 1), B_PV)
        if s_prev is not None:
            p_prev = jnp.exp(s_prev - m)
            den = den + jnp.sum(p_prev, axis=2, keepdims=True)
            acc = acc + _dot(p_prev.astype(BF16), _blocks(vdp, u0), B_PV)
        rows = slice(u0 * BLK, (u0 + CU) * BLK)
        od[rows, :] = (acc / den).reshape(CU * BLK, HEAD_DIM)
        ld[rows, :] = jnp.broadcast_to(m + jnp.log(den), (CU, BLK, HEAD_DIM)).reshape(CU * BLK, HEAD_DIM)
    _interleave(og, od, dil)
    _interleave(lg, ld, dil)


def _group_weights(lgs):
    l0, l1, l2 = lgs[0][...], lgs[1][...], lgs[2][...]
    mx = jnp.maximum(l0, jnp.maximum(l1, l2))
    e0, e1, e2 = jnp.exp(l0 - mx), jnp.exp(l1 - mx), jnp.exp(l2 - mx)
    den = e0 + e1 + e2
    return e0 / den, e1 / den, e2 / den


def _head_spec(base):
    return pl.BlockSpec((S, HEAD_DIM), lambda h, p: (0, base + (p % 3) * 8 + h))


def _slab(dtype=F32, rows=S):
    return pltpu.VMEM((rows, HEAD_DIM), dtype)


def _attn_fwd(z0, tabs, ycat):
    def body(q_ref, k_ref, v_ref, c_ref, a_ref, b_ref, gate_ref, ycat_ref, out_ref, og_ref, lg_ref,
             qo_ref, ko_ref, vo_ref, tmp, od, ld, og0, og1, og2, lg0, lg1, lg2):
        del ycat_ref
        p = pl.program_id(1)
        ogs, lgs = (og0, og1, og2), (lg0, lg1, lg2)
        ko_ref[0:BLK, :] = jnp.zeros((BLK, HEAD_DIM), BF16)
        vo_ref[0:BLK, :] = jnp.zeros((BLK, HEAD_DIM), BF16)
        for gi, (_, dil) in enumerate(PATTERNS):
            @pl.when(p == gi)
            def _(gi=gi, dil=dil):
                c, a, b = c_ref[...], a_ref[...], b_ref[...]
                tmp[...] = _rope(q_ref[...], c, a, b)
                _deinterleave(qo_ref, tmp, dil, BF16)
                tmp[...] = _rope(k_ref[...], c, a, b)
                _deinterleave(ko_ref, tmp, dil, BF16, BLK)
                _deinterleave(vo_ref, v_ref, dil, BF16, BLK)
                _attn_group_fwd(dil, qo_ref, ko_ref, vo_ref, od, ld, ogs[gi], lgs[gi])
                og_ref[...] = ogs[gi][...]
                lg_ref[...] = lgs[gi][...]

        @pl.when(p == 2)
        def _():
            w0, w1, w2 = _group_weights(lgs)
            o = w0 * og0[...] + w1 * og1[...] + w2 * og2[...]
            gate = gate_ref[...]
            out_ref[...] = (o * (gate * _sigmoid(gate))).astype(BF16)

    grp = pl.BlockSpec((S, HEAD_DIM), lambda h, p: (0, p * 8 + h))
    grp_pad = pl.BlockSpec((S + BLK, HEAD_DIM), lambda h, p: (0, p * 8 + h))
    tab = pl.BlockSpec((S, HEAD_DIM), lambda h, p: (0, 0))
    outs = pl.pallas_call(
        body, grid=(8, 3),
        in_specs=[_head_spec(Q_COL), _head_spec(K_COL), _head_spec(V_COL), tab, tab, tab,
                  pl.BlockSpec((S, HEAD_DIM), lambda h, p: (0, BG_COL + h)), ANY_SPEC],
        out_specs=[pl.BlockSpec((S, HEAD_DIM), lambda h, p: (0, 8 + h)), grp, grp, grp, grp_pad, grp_pad],
        out_shape=[jax.ShapeDtypeStruct((S, 2048), BF16), jax.ShapeDtypeStruct((S, 3072), F32),
                   jax.ShapeDtypeStruct((S, 3072), F32), jax.ShapeDtypeStruct((S, 3072), BF16),
                   jax.ShapeDtypeStruct((S + BLK, 3072), BF16), jax.ShapeDtypeStruct((S + BLK, 3072), BF16)],
        scratch_shapes=[_slab() for _ in range(9)],
        input_output_aliases={7: 0},
        compiler_params=_params(("parallel", "arbitrary"), VMEM_BIG), name="attn_fwd",
    )(z0, z0, z0, *tabs, z0, ycat)
    return outs[0], outs[1], outs[2], tuple(outs[3:])


def _attn_bwd(z0, qkv, og, lg, dy0, w_out, tabs, dep, da_in, da_gate):
    def body(q_ref, k_ref, v_ref, gate_ref, dy0_ref, wo_ref, c_ref, a_ref, b_ref,
             og0_ref, og1_ref, og2_ref, lg0_ref, lg1_ref, lg2_ref, dep_ref, da_in_ref, da_gate_ref,
             dz_ref,
             tmp, ld, dg0, dg1, dg2, cg0, cg1, cg2, dod, cd, dqd, dkd, dvd, stq, stk, stv, stb, sems):
        kd, vd = k_ref, v_ref
        h, p = pl.program_id(0), pl.program_id(1)
        ogs, lgs, dgs, cgs = (og0_ref, og1_ref, og2_ref), (lg0_ref, lg1_ref, lg2_ref), (dg0, dg1, dg2), (cg0, cg1, cg2)

        def put(src, blk, i):
            cols = pl.ds(pl.multiple_of(blk * HEAD_DIM, HEAD_DIM), HEAD_DIM)
            return pltpu.make_async_copy(src, dz_ref.at[:, cols], sems.at[i])

        pool_copies = (pltpu.make_async_copy(da_in_ref, dz_ref.at[:, 0:HALF], sems.at[4]),
                       pltpu.make_async_copy(da_gate_ref, dz_ref.at[:, HALF:2 * HALF], sems.at[5]))

        @pl.when(p == 0)
        def _():
            w = _group_weights(lgs)
            o = w[0] * ogs[0][...] + w[1] * ogs[1][...] + w[2] * ogs[2][...]
            silu, dsilu = _silu_and_grad(gate_ref[...])
            dy = _dot(dy0_ref[...], wo_ref[...], NT)

            @pl.when(h == 0)
            def _():
                for cp in pool_copies:
                    cp.start()

            @pl.when(h > 0)
            def _():
                put(stb, 0, 3).wait()

            stb[...] = (dy * o * dsilu).astype(BF16)
            put(stb, BG_COL + h, 3).start()
            do = dy * silu
            dwbar = jnp.sum(do * o, axis=1, keepdims=True)
            for gi in range(3):
                dgs[gi][...] = w[gi] * do
                cgs[gi][...] = -w[gi] * dwbar

        for gi, (_, dil) in enumerate(PATTERNS):
            @pl.when(p == 1 + gi)
            def _(gi=gi, dil=dil):
                nb = S // dil // BLK
                qd = q_ref
                c, a, b = c_ref[...], a_ref[...], b_ref[...]
                _deinterleave(dod, dgs[gi], dil, BF16)
                _deinterleave(ld, lgs[gi], dil)
                _deinterleave(cd, cgs[gi], dil)
                dkd[...] = jnp.zeros_like(dkd)
                dvd[...] = jnp.zeros_like(dvd)
                flat = lambda t: t.reshape(CU * BLK, HEAD_DIM)
                for u0 in range(0, NUNITS, CU):
                    q, s_own, s_prev = _chunk_scores(u0, nb, qd, kd)
                    lse, cv, do = _blocks(ld, u0), _blocks(cd, u0), _blocks(dod, u0)
                    own = slice((u0 + 1) * BLK, (u0 + 1 + CU) * BLK)
                    p_own = jnp.exp(s_own - lse)
                    ds_own = (p_own * (_dot(do, _blocks(vd, u0 + 1), B_QK) + cv) * SCALE).astype(BF16)
                    dq = _dot(ds_own, _blocks(kd, u0 + 1), B_PV)
                    dkd[own, :] += flat(_dot(ds_own, q, B_TN))
                    dvd[own, :] += flat(_dot(p_own.astype(BF16), do, B_TN))
                    if s_prev is not None:
                        prev = slice(u0 * BLK, (u0 + CU) * BLK)
                        p_prev = jnp.exp(s_prev - lse)
                        ds_prev = (p_prev * (_dot(do, _blocks(vd, u0), B_QK) + cv) * SCALE).astype(BF16)
                        dq = dq + _dot(ds_prev, _blocks(kd, u0), B_PV)
                        dkd[prev, :] += flat(_dot(ds_prev, q, B_TN))
                        dvd[prev, :] += flat(_dot(p_prev.astype(BF16), do, B_TN))
                    dqd[u0 * BLK:(u0 + CU) * BLK, :] = flat(dq)
                stage = ((stq, Q_COL), (stk, K_COL), (stv, V_COL))

                def drain():
                    for i, (st, _) in enumerate(stage):
                        put(st, 0, i).wait()

                if gi == 0:
                    pl.when(h > 0)(drain)
                else:
                    drain()
                _interleave(tmp, dqd, dil)
                stq[...] = _rope_t(tmp[...], c, a, b).astype(BF16)
                _interleave(tmp, dkd, dil, BLK)
                stk[...] = _rope_t(tmp[...], c, a, b).astype(BF16)
                _interleave(tmp, dvd, dil, BLK)
                stv[...] = tmp[...].astype(BF16)
                for i, (st, col) in enumerate(stage):
                    put(st, col + gi * 8 + h, i).start()
                if gi == 2:
                    @pl.when(h == pl.num_programs(0) - 1)
                    def _():
                        drain()
                        put(stb, 0, 3).wait()
                        for cp in pool_copies:
                            cp.wait()

    tab = pl.BlockSpec((S, HEAD_DIM), lambda h, p: (0, 0))
    hspec = lambda base: pl.BlockSpec((S, HEAD_DIM), lambda h, p: (0, base + h))
    gspec = pl.BlockSpec((S, HEAD_DIM), lambda h, p: (0, jnp.maximum(p - 1, 0) * 8 + h))
    gspec_pad = pl.BlockSpec((S + BLK, HEAD_DIM), lambda h, p: (0, jnp.maximum(p - 1, 0) * 8 + h))
    return pl.pallas_call(
        body, grid=(8, 4),
        in_specs=[gspec, gspec_pad, gspec_pad, hspec(BG_COL),
                  pl.BlockSpec((S, D), lambda h, p: (0, 0), pipeline_mode=pl.Buffered(1)),
                  pl.BlockSpec((HEAD_DIM, D), lambda h, p: (8 + h, 0)), tab, tab, tab,
                  hspec(0), hspec(8), hspec(16), hspec(0), hspec(8), hspec(16), ANY_SPEC, ANY_SPEC, ANY_SPEC],
        out_specs=ANY_SPEC,
        out_shape=jax.ShapeDtypeStruct((S, z0.shape[1]), BF16),
        scratch_shapes=[_slab(), _slab()] + [_slab() for _ in range(6)]
                       + [_slab(BF16), _slab(), _slab(), _slab(F32, S + BLK), _slab(F32, S + BLK)]
                       + [_slab(BF16) for _ in range(4)] + [pltpu.SemaphoreType.DMA((6,))],
        compiler_params=_params(("arbitrary", "arbitrary"), VMEM_BIG), name="attn_bwd",
    )(*qkv, z0, dy0, w_out, *tabs, og, og, og, lg, lg, lg, dep, da_in, da_gate)


SGU_CH = 256
NCHUNK = TR // 128


def _ln_stats(x):
    mu = jnp.mean(x, axis=-1, keepdims=True)
    xc = x - mu
    r = lax.rsqrt(jnp.mean(xc * xc, axis=-1, keepdims=True) + EPS)
    return xc * r, r


def _ln_bwd(dy, xhat, r, g):
    dxh = dy * g
    return r * (dxh - jnp.mean(dxh, axis=-1, keepdims=True) - xhat * jnp.mean(dxh * xhat, axis=-1, keepdims=True))


def _tril_bf16(w):
    row = lax.broadcasted_iota(jnp.int32, w.shape, 0)
    col = lax.broadcasted_iota(jnp.int32, w.shape, 1)
    return jnp.where(row >= col, w, 0.0).astype(BF16)


def _sgu_gate(vn_s, s_s, w_ref, bb_ref):
    for h in range(4):
        wm = _tril_bf16(w_ref[h])
        bias = bb_ref[h]
        for ch in range(NCHUNK):
            rows, cols = slice(ch * 128, (ch + 1) * 128), slice(h * SGU_CH, (h + 1) * SGU_CH)
            s_s[rows, cols] = _dot(wm, vn_s[rows, cols]) + jnp.concatenate([bias, bias], axis=1)


WIN = HALO + TR
SUBL = 8


def _shifted_copies(dst, src):
    dst[0] = src[...]
    for b in range(1, SUBL):
        dst[b, 0:WIN - SUBL, :] = src[pl.ds(b, WIN - SUBL), :]


def _rows_at(copies, off, n):
    return copies[off % SUBL, pl.ds(off - off % SUBL, n), :]


def _conv_fwd(i, dval_ref, dglu_ref, hval_ref, hglu_ref, cw_ref, cb_ref, xw, xr, dcs):
    halo = hval_ref[...] * _sigmoid(hglu_ref[...])
    xw[0:HALO, :] = jnp.where(i > 0, halo, 0.0)
    xw[HALO:HALO + TR, :] = dval_ref[...] * _sigmoid(dglu_ref[...])
    _shifted_copies(xr, xw)
    sub = 2 * SUB
    for rb in range(TR // sub):
        acc = jnp.broadcast_to(cb_ref[...], (sub, HALF))
        for k in range(CONV_K):
            acc = acc + cw_ref[k:k + 1, :] * _rows_at(xr, rb * sub + HALO - (CONV_K - 1) + k, sub)
        dcs[rb * sub:(rb + 1) * sub, :] = acc


def _odd_in_specs():
    col = lambda j: pl.BlockSpec((TR, HALF), lambda i, *_: (i, j))
    prev = lambda j: pl.BlockSpec((HALO, HALF), lambda i, *_: (jnp.maximum(i * (TR // HALO) - 1, 0), j))
    return [col(0), col(1), col(2), col(3), col(4), col(5), prev(3), prev(4)]


def _full_spec(shape):
    return pl.BlockSpec(shape, lambda i, *_: (0,) * len(shape))


def _odd_fwd(z1, sgu_g, sgu_b, sgu_w, sgu_bb, conv_w, conv_b, cn_g, cn_b):
    def body(u_ref, v_ref, cg_ref, dval_ref, dglu_ref, dgate_ref, hval_ref, hglu_ref,
             g_ref, b_ref, w_ref, bb_ref, cw_ref, cb_ref, cng_ref, cnb_ref, out_ref, dcs, vn_s, s_s, xw, xr):
        i = pl.program_id(0)
        vhat, _ = _ln_stats(v_ref[...])
        vn_s[...] = (vhat * g_ref[...] + b_ref[...]).astype(BF16)
        _sgu_gate(vn_s, s_s, w_ref, bb_ref)
        cg = cg_ref[...]
        out_ref[:, 0:HALF] = (u_ref[...] * s_s[...] * (cg * _sigmoid(cg))).astype(BF16)
        _conv_fwd(i, dval_ref, dglu_ref, hval_ref, hglu_ref, cw_ref, cb_ref, xw, xr, dcs)
        dhat, _ = _ln_stats(dcs[...])
        dn = dhat * cng_ref[...] + cnb_ref[...]
        dgate = dgate_ref[...]
        out_ref[:, HALF:2 * HALF] = ((dn * _sigmoid(dn)) * (dgate * _sigmoid(dgate))).astype(BF16)

    vec = _full_spec((1, HALF))
    return pl.pallas_call(
        body, grid=(S // TR,),
        in_specs=_odd_in_specs() + [vec, vec, _full_spec((4, 128, 128)), _full_spec((4, 128, 128)),
                                    _full_spec((HALO, HALF)), vec, vec, vec],
        out_specs=[pl.BlockSpec((TR, 2048), lambda i: (i, 0)), pl.BlockSpec((TR, HALF), lambda i: (i, 0))],
        out_shape=[jax.ShapeDtypeStruct((S, 2048), BF16), jax.ShapeDtypeStruct((S, HALF), F32)],
        scratch_shapes=[pltpu.VMEM((TR, HALF), BF16), pltpu.VMEM((TR, HALF), F32),
                        pltpu.VMEM((WIN, HALF), F32), pltpu.VMEM((SUBL, WIN, HALF), F32)],
        compiler_params=_params(("parallel",), VMEM_BIG), name="odd_fwd",
    )(z1, z1, z1, z1, z1, z1, z1, z1, sgu_g, sgu_b, sgu_w, sgu_bb, conv_w, conv_b, cn_g, cn_b)


def _odd_bwd_a(z1, dc, dycat, sgu_g, sgu_b, sgu_w, sgu_bb, cn_g, cn_b):
    def body(u_ref, v_ref, cg_ref, dgate_ref, dcs, dy_ref, g_ref, b_ref, w_ref, bb_ref, cng_ref, cnb_ref,
             dz_ref, ddc_ref, dw_ref, dbb_ref, dg_ref, db_ref, dcng_ref, dcnb_ref, dcb_ref,
             vn_s, s_s, ds_s, dvn_s):
        i = pl.program_id(0)
        vhat, rv = _ln_stats(v_ref[...])
        g = g_ref[...]
        vn_s[...] = (vhat * g + b_ref[...]).astype(BF16)
        _sgu_gate(vn_s, s_s, w_ref, bb_ref)
        silu_c, dsilu_c = _silu_and_grad(cg_ref[...])
        dyc = dy_ref[:, 0:HALF]
        u = u_ref[...]
        s = s_s[...]
        dz_ref[:, 0:HALF] = (dyc * s * silu_c).astype(BF16)
        dz_ref[:, 2 * HALF:3 * HALF] = (dyc * u * s * dsilu_c).astype(BF16)
        ds_s[...] = dyc * u * silu_c

        @pl.when(i == 0)
        def _():
            dw_ref[...] = jnp.zeros_like(dw_ref)
            dbb_ref[...] = jnp.zeros_like(dbb_ref)

        tril = lax.broadcasted_iota(jnp.int32, (128, 128), 0) >= lax.broadcasted_iota(jnp.int32, (128, 128), 1)
        for h in range(4):
            wm = _tril_bf16(w_ref[h])
            for ch in range(NCHUNK):
                rows, cols = slice(ch * 128, (ch + 1) * 128), slice(h * SGU_CH, (h + 1) * SGU_CH)
                ds = ds_s[rows, cols]
                dsb = ds.astype(BF16)
                dw_ref[h] += jnp.where(tril, _dot(dsb, vn_s[rows, cols], NT), 0.0)
                dbb_ref[h] += jnp.broadcast_to(jnp.sum(ds, axis=1, keepdims=True), (128, 128))
                dvn_s[rows, cols] = _dot(wm, dsb, TN)
        dvn = dvn_s[...]
        _acc_rows(dg_ref, dvn * vhat, i)
        _acc_rows(db_ref, dvn, i)
        dz_ref[:, HALF:2 * HALF] = _ln_bwd(dvn, vhat, rv, g).astype(BF16)

        dhat, rd = _ln_stats(dcs[...])
        cng = cng_ref[...]
        silu_n, dsilu_n = _silu_and_grad(dhat * cng + cnb_ref[...])
        silu_g, dsilu_g = _silu_and_grad(dgate_ref[...])
        dyd = dy_ref[:, HALF:2 * HALF]
        dz_ref[:, 5 * HALF:6 * HALF] = (dyd * silu_n * dsilu_g).astype(BF16)
        ddn = dyd * silu_g * dsilu_n
        _acc_rows(dcng_ref, ddn * dhat, i)
        _acc_rows(dcnb_ref, ddn, i)
        ddc = _ln_bwd(ddn, dhat, rd, cng)
        ddc_ref[...] = ddc
        _acc_rows(dcb_ref, ddc, i)

    vec = _full_spec((1, HALF))
    sq = _full_spec((4, 128, 128))
    col = lambda j: pl.BlockSpec((TR, HALF), lambda i: (i, j))
    return pl.pallas_call(
        body, grid=(S // TR,),
        in_specs=[col(0), col(1), col(2), col(5), col(0), pl.BlockSpec((TR, 2048), lambda i: (i, 0)),
                  vec, vec, sq, sq, vec, vec],
        out_specs=[pl.BlockSpec((TR, ODD_IN), lambda i: (i, 0)), pl.BlockSpec((TR, HALF), lambda i: (i, 0)),
                   sq, sq, vec, vec, vec, vec, vec],
        out_shape=[jax.ShapeDtypeStruct((S, ODD_IN), BF16), jax.ShapeDtypeStruct((S, HALF), F32),
                   jax.ShapeDtypeStruct((4, 128, 128), F32), jax.ShapeDtypeStruct((4, 128, 128), F32)]
                  + [jax.ShapeDtypeStruct((1, HALF), F32)] * 5,
        scratch_shapes=[pltpu.VMEM((TR, HALF), BF16), pltpu.VMEM((TR, HALF), F32),
                        pltpu.VMEM((TR, HALF), F32), pltpu.VMEM((TR, HALF), F32)],
        compiler_params=_params(("arbitrary",), VMEM_BIG), name="odd_bwd_a",
    )(z1, z1, z1, z1, dc, dycat, sgu_g, sgu_b, sgu_w, sgu_bb, cn_g, cn_b)


def _odd_bwd_b(z1, ddc, dz1, conv_w):
    nt = S // TR

    def body(dval_ref, dglu_ref, hval_ref, hglu_ref, ddc_ref, hddc_ref, cw_ref, dz_in_ref,
             dz_ref, dcw_ref, xw, dwin, dxs, xr, dr):
        del dz_in_ref
        i, j = pl.program_id(0), pl.program_id(1)
        sg = _sigmoid(dglu_ref[...])
        dval = dval_ref[...]

        @pl.when(j == 0)
        def _():
            halo = hval_ref[...] * _sigmoid(hglu_ref[...])
            xw[0:HALO, :] = jnp.where(i > 0, halo, 0.0)
            xw[HALO:HALO + TR, :] = dval * sg
            dwin[0:TR, :] = ddc_ref[...]
            dwin[TR:TR + HALO, :] = jnp.where(i < nt - 1, hddc_ref[...], 0.0)
            _shifted_copies(xr, xw)
            _shifted_copies(dr, dwin)

            @pl.when(i == 0)
            def _():
                dcw_ref[...] = jnp.zeros_like(dcw_ref)

            for rb in range(TR // SUB):
                acc = jnp.zeros((SUB, HALF), F32)
                for k in range(CONV_K):
                    acc = acc + cw_ref[k:k + 1, :] * _rows_at(dr, rb * SUB + (CONV_K - 1) - k, SUB)
                dxs[rb * SUB:(rb + 1) * SUB, :] = acc
            for k in range(CONV_K):
                acc = jnp.zeros((SUB, HALF), F32)
                for rb in range(TR // SUB):
                    acc = acc + dwin[rb * SUB:(rb + 1) * SUB, :] * _rows_at(xr, rb * SUB + HALO - (CONV_K - 1) + k, SUB)
                dcw_ref[k:k + 1, :] += jnp.sum(acc, axis=0, keepdims=True)
            dz_ref[...] = (dxs[...] * sg).astype(BF16)

        @pl.when(j == 1)
        def _():
            dz_ref[...] = (dxs[...] * dval * sg * (1.0 - sg)).astype(BF16)

    col = lambda c: pl.BlockSpec((TR, HALF), lambda i, j: (i, c))
    prev = lambda c: pl.BlockSpec((HALO, HALF), lambda i, j: (jnp.maximum(i * (TR // HALO) - 1, 0), c))
    nxt = pl.BlockSpec((HALO, HALF), lambda i, j: (jnp.minimum((i + 1) * (TR // HALO), S // HALO - 1), 0))
    return pl.pallas_call(
        body, grid=(nt, 2),
        in_specs=[col(3), col(4), prev(3), prev(4), pl.BlockSpec((TR, HALF), lambda i, j: (i, 0)), nxt,
                  _full_spec((HALO, HALF)), pl.BlockSpec(memory_space=pl.ANY)],
        out_specs=[pl.BlockSpec((TR, HALF), lambda i, j: (i, 3 + j)), _full_spec((HALO, HALF))],
        out_shape=[jax.ShapeDtypeStruct((S, ODD_IN), BF16), jax.ShapeDtypeStruct((HALO, HALF), F32)],
        scratch_shapes=[pltpu.VMEM((WIN, HALF), F32), pltpu.VMEM((WIN, HALF), F32), pltpu.VMEM((TR, HALF), F32),
                        pltpu.VMEM((SUBL, WIN, HALF), F32), pltpu.VMEM((SUBL, WIN, HALF), F32)],
        input_output_aliases={7: 0},
        compiler_params=_params(("arbitrary", "arbitrary"), VMEM_BIG), name="odd_bwd_b",
    )(z1, z1, z1, z1, ddc, ddc, conv_w, dz1)


def _conv_out_grad(dc, dyd, dgate, cng, cnb):
    dhat, rd = _ln_stats(dc)
    silu_n, dsilu_n = _silu_and_grad(dhat * cng + cnb)
    silu_g, dsilu_g = _silu_and_grad(dgate)
    ddn = dyd * silu_g * dsilu_n
    return _ln_bwd(ddn, dhat, rd, cng), ddn, dhat, dyd * silu_n * dsilu_g


def _odd_bwd(z1, dc, dycat, sgu_g, sgu_b, sgu_w, sgu_bb, conv_w, cn_g, cn_b):
    nt = S // TR

    def body(u_ref, v_ref, cg_ref, dval_ref, dglu_ref, dgate_ref, hval_ref, hglu_ref, dcs, dy_ref,
             ndc_ref, ndy_ref, ngate_ref, g_ref, b_ref, w_ref, bb_ref, cw_ref, cng_ref, cnb_ref,
             dz_ref, dw_ref, dbb_ref, dg_ref, db_ref, dcng_ref, dcnb_ref, dcb_ref, dcw_ref,
             vn_s, s_s, ds_s, dvn_s, xw, dwin, dxs, xr, dr):
        i = pl.program_id(0)
        vhat, rv = _ln_stats(v_ref[...])
        g = g_ref[...]
        vn_s[...] = (vhat * g + b_ref[...]).astype(BF16)
        _sgu_gate(vn_s, s_s, w_ref, bb_ref)
        silu_c, dsilu_c = _silu_and_grad(cg_ref[...])
        dyc = dy_ref[:, 0:HALF]
        u = u_ref[...]
        s = s_s[...]
        dz_ref[:, 0:HALF] = (dyc * s * silu_c).astype(BF16)
        dz_ref[:, 2 * HALF:3 * HALF] = (dyc * u * s * dsilu_c).astype(BF16)
        ds_s[...] = dyc * u * silu_c

        @pl.when(i == 0)
        def _():
            dw_ref[...] = jnp.zeros_like(dw_ref)
            dbb_ref[...] = jnp.zeros_like(dbb_ref)
            dcw_ref[...] = jnp.zeros_like(dcw_ref)

        tril = lax.broadcasted_iota(jnp.int32, (128, 128), 0) >= lax.broadcasted_iota(jnp.int32, (128, 128), 1)
        for h in range(4):
            wm = _tril_bf16(w_ref[h])
            for ch in range(NCHUNK):
                rows, cols = slice(ch * 128, (ch + 1) * 128), slice(h * SGU_CH, (h + 1) * SGU_CH)
                ds = ds_s[rows, cols]
                dsb = ds.astype(BF16)
                dw_ref[h] += jnp.where(tril, _dot(dsb, vn_s[rows, cols], NT), 0.0)
                dbb_ref[h] += jnp.broadcast_to(jnp.sum(ds, axis=1, keepdims=True), (128, 128))
                dvn_s[rows, cols] = _dot(wm, dsb, TN)
        dvn = dvn_s[...]
        _acc_rows(dg_ref, dvn * vhat, i)
        _acc_rows(db_ref, dvn, i)
        dz_ref[:, HALF:2 * HALF] = _ln_bwd(dvn, vhat, rv, g).astype(BF16)

        cng, cnb = cng_ref[...], cnb_ref[...]
        ddc, ddn, dhat, ddgate = _conv_out_grad(dcs[...], dy_ref[:, HALF:2 * HALF], dgate_ref[...], cng, cnb)
        dz_ref[:, 5 * HALF:6 * HALF] = ddgate.astype(BF16)
        _acc_rows(dcng_ref, ddn * dhat, i)
        _acc_rows(dcnb_ref, ddn, i)
        _acc_rows(dcb_ref, ddc, i)
        ddc_next = _conv_out_grad(ndc_ref[...], ndy_ref[:, HALF:2 * HALF], ngate_ref[...], cng, cnb)[0]

        sg = _sigmoid(dglu_ref[...])
        dval = dval_ref[...]
        halo = hval_ref[...] * _sigmoid(hglu_ref[...])
        xw[0:HALO, :] = jnp.where(i > 0, halo, 0.0)
        xw[HALO:HALO + TR, :] = dval * sg
        dwin[0:TR, :] = ddc
        dwin[TR:TR + HALO, :] = jnp.where(i < nt - 1, ddc_next, 0.0)
        _shifted_copies(xr, xw)
        _shifted_copies(dr, dwin)
        for rb in range(TR // SUB):
            acc = jnp.zeros((SUB, HALF), F32)
            for k in range(CONV_K):
                acc = acc + cw_ref[k:k + 1, :] * _rows_at(dr, rb * SUB + (CONV_K - 1) - k, SUB)
            dxs[rb * SUB:(rb + 1) * SUB, :] = acc
        for k in range(CONV_K):
            acc = jnp.zeros((SUB, HALF), F32)
            for rb in range(TR // SUB):
                acc = acc + dwin[rb * SUB:(rb + 1) * SUB, :] * _rows_at(xr, rb * SUB + HALO - (CONV_K - 1) + k, SUB)
            dcw_ref[k:k + 1, :] += jnp.sum(acc, axis=0, keepdims=True)
        dx = dxs[...]
        dz_ref[:, 3 * HALF:4 * HALF] = (dx * sg).astype(BF16)
        dz_ref[:, 4 * HALF:5 * HALF] = (dx * dval * sg * (1.0 - sg)).astype(BF16)

    vec = _full_spec((1, HALF))
    sq = _full_spec((4, 128, 128))
    col = lambda j: pl.BlockSpec((TR, HALF), lambda i: (i, j))
    prev = lambda j: pl.BlockSpec((HALO, HALF), lambda i: (jnp.maximum(i * (TR // HALO) - 1, 0), j))
    nxt_row = lambda i: jnp.minimum((i + 1) * (TR // HALO), S // HALO - 1)
    return pl.pallas_call(
        body, grid=(nt,),
        in_specs=[col(0), col(1), col(2), col(3), col(4), col(5), prev(3), prev(4), col(0),
                  pl.BlockSpec((TR, 2048), lambda i: (i, 0)),
                  pl.BlockSpec((HALO, HALF), lambda i: (nxt_row(i), 0)), pl.BlockSpec((HALO, 2048), lambda i: (nxt_row(i), 0)),
                  pl.BlockSpec((HALO, HALF), lambda i: (nxt_row(i), 5)),
                  vec, vec, sq, sq, _full_spec((HALO, HALF)), vec, vec],
        out_specs=[pl.BlockSpec((TR, ODD_IN), lambda i: (i, 0)), sq, sq, vec, vec, vec, vec, vec,
                   _full_spec((HALO, HALF))],
        out_shape=[jax.ShapeDtypeStruct((S, ODD_IN), BF16), jax.ShapeDtypeStruct((4, 128, 128), F32),
                   jax.ShapeDtypeStruct((4, 128, 128), F32)] + [jax.ShapeDtypeStruct((1, HALF), F32)] * 5
                  + [jax.ShapeDtypeStruct((HALO, HALF), F32)],
        scratch_shapes=[pltpu.VMEM((TR, HALF), BF16), pltpu.VMEM((TR, HALF), F32), pltpu.VMEM((TR, HALF), F32),
                        pltpu.VMEM((TR, HALF), F32), pltpu.VMEM((WIN, HALF), F32), pltpu.VMEM((WIN, HALF), F32),
                        pltpu.VMEM((TR, HALF), F32), pltpu.VMEM((SUBL, WIN, HALF), F32),
                        pltpu.VMEM((SUBL, WIN, HALF), F32)],
        compiler_params=_params(("arbitrary",), VMEM_BIG), name="odd_bwd",
    )(z1, z1, z1, z1, z1, z1, z1, z1, dc, dycat, dc, dycat, z1,
      sgu_g, sgu_b, sgu_w, sgu_bb, conv_w, cn_g, cn_b)


def _cast_bf16(w, name, piece=0, npieces=1):
    r, c = w.shape[0], w.shape[1] // npieces
    tr = min(r, 256)

    def body(i_ref, o_ref):
        o_ref[...] = i_ref[...].astype(BF16)

    return pl.pallas_call(
        body, grid=(r // tr,), in_specs=[pl.BlockSpec((tr, c), lambda i: (i, piece))],
        out_specs=pl.BlockSpec((tr, c), lambda i: (i, 0)), out_shape=jax.ShapeDtypeStruct((r, c), BF16),
        compiler_params=_params(("parallel",)), name=name,
    )(w)


def _adamw(w, g, m, v):
    m = ADAM_B1 * m + (1.0 - ADAM_B1) * g
    v = ADAM_B2 * v + (1.0 - ADAM_B2) * (g * g)
    m_hat = m / (1.0 - ADAM_B1 ** ADAM_STEP)
    v_hat = v / (1.0 - ADAM_B2 ** ADAM_STEP)
    delta = -ADAM_LR * (m_hat / (jnp.sqrt(v_hat) + ADAM_EPS) + ADAM_WD * w)
    return delta, m, v


def _adam_reduce(parts, w, m, v, name, dep=None, piece=0, npieces=1, prev=None):
    r, c = w.shape
    cp = c // npieces
    tr = min(r, 128)
    extra = ([] if dep is None else [dep]) + ([] if prev is None else list(prev))
    nparts = parts.shape[0]

    def body(p_ref, w_ref, m_ref, v_ref, *rest):
        g_ref, d_ref, nm_ref, nv_ref = rest[len(extra):]
        g = p_ref[0].astype(F32)
        for d in range(1, nparts):
            g = g + p_ref[d].astype(F32)
        g_ref[...] = g
        d_ref[...], nm_ref[...], nv_ref[...] = _adamw(w_ref[...], g, m_ref[...], v_ref[...])

    spec = pl.BlockSpec((tr, cp), lambda i: (i, piece))
    first = 4 + (0 if dep is None else 1)
    return pl.pallas_call(
        body, grid=(r // tr,),
        in_specs=[pl.BlockSpec((nparts, tr, cp), lambda i: (0, i, 0)), spec, spec, spec] + [ANY_SPEC] * len(extra),
        out_specs=[spec] * 4, out_shape=[jax.ShapeDtypeStruct((r, c), F32)] * 4,
        input_output_aliases={} if prev is None else {first + k: k for k in range(4)},
        compiler_params=_params(("parallel",), VMEM_BIG), name=name,
    )(parts, w, m, v, *extra)


def _arrived(x, name, dep=None):
    deps = [] if dep is None else [dep]

    def body(*refs):
        refs[-1][...] = jnp.zeros_like(refs[-1])

    return pl.pallas_call(
        body, in_specs=[ANY_SPEC] * (1 + len(deps)), out_specs=pl.BlockSpec(memory_space=pltpu.VMEM),
        out_shape=jax.ShapeDtypeStruct((8, 128), F32), name=name,
    )(x, *deps)


def _sum_parts(parts, name, dep=None):
    r = parts.shape[1]
    tr = 8
    for cand in (512, 256, 128, 64, 32, 16, 8):
        if r % cand == 0:
            tr = cand
            break
    deps = [] if dep is None else [dep]

    def body(p_ref, *rest):
        g = p_ref[0]
        for d in range(1, NDEV):
            g = g + p_ref[d]
        rest[-1][...] = g

    return pl.pallas_call(
        body, grid=(r // tr,), in_specs=[pl.BlockSpec((NDEV, tr, 128), lambda i: (0, i, 0))] + [ANY_SPEC] * len(deps),
        out_specs=pl.BlockSpec((tr, 128), lambda i: (i, 0)), out_shape=jax.ShapeDtypeStruct((r, 128), F32),
        compiler_params=_params(("parallel",)), name=name,
    )(parts, *deps)


def _sum_unpack(parts, rows, name, dep=None):
    deps = [] if dep is None else [dep]

    def body(p_ref, *outs):
        outs = outs[len(deps):]
        off = 0
        for o_ref, n in zip(outs, rows):
            acc = p_ref[0, off:off + n, :]
            for d in range(1, NDEV):
                acc = acc + p_ref[d, off:off + n, :]
            o_ref[...] = acc
            off += n

    return pl.pallas_call(
        body, grid=(1,), in_specs=[pl.BlockSpec(parts.shape, lambda i: (0, 0, 0))] + [ANY_SPEC] * len(deps),
        out_specs=[pl.BlockSpec((n, 128), lambda i: (0, 0)) for n in rows],
        out_shape=[jax.ShapeDtypeStruct((n, 128), F32) for n in rows],
        compiler_params=_params(("arbitrary",), VMEM_BIG), name=name,
    )(parts, *deps)


def _adam_small(ws, gs, g_specs, ms, vs, name):
    n = len(ws)

    def body(*refs):
        w_r, g_r, m_r, v_r = refs[:n], refs[n:2 * n], refs[2 * n:3 * n], refs[3 * n:4 * n]
        outs = refs[4 * n:]
        for i in range(n):
            g = g_r[i][...]
            outs[4 * i][...] = g
            outs[4 * i + 1][...], outs[4 * i + 2][...], outs[4 * i + 3][...] = _adamw(
                w_r[i][...], g, m_r[i][...], v_r[i][...])

    whole = lambda a: pl.BlockSpec(a.shape, lambda i, nd=a.ndim: (0,) * nd)
    outs = pl.pallas_call(
        body, grid=(1,),
        in_specs=[whole(a) for a in ws] + list(g_specs) + [whole(a) for a in ms] + [whole(a) for a in vs],
        out_specs=[whole(a) for a in ws for _ in range(4)],
        out_shape=[jax.ShapeDtypeStruct(a.shape, F32) for a in ws for _ in range(4)],
        compiler_params=_params(("arbitrary",), VMEM_BIG), name=name,
    )(*ws, *gs, *ms, *vs)
    return [outs[4 * i:4 * i + 4] for i in range(n)]


MASKS = [(mx, my, mc) for mx in (0, 1) for my in (0, 1) for mc in (0, 1)][1:]


def _sc_exchange(name, collective_id, arrays, scatter):
    nt = len(arrays)
    out_type = [jax.ShapeDtypeStruct(a.shape if scatter else (NDEV,) + a.shape, a.dtype) for a in arrays]

    def body(*refs):
        ins, outs = refs[:nt], refs[nt:2 * nt]
        send_sems, recv_sems, local_sems = refs[2 * nt:3 * nt], refs[3 * nt:4 * nt], refs[4 * nt:5 * nt]
        x, y, c = lax.axis_index("x"), lax.axis_index("y"), lax.axis_index("c")
        peers = [(mx + x - 2 * mx * x, my + y - 2 * my * y, mc + c - 2 * mc * c) for mx, my, mc in MASKS]
        barrier = pltpu.get_barrier_semaphore()
        for peer in peers:
            pl.semaphore_signal(barrier, inc=1, device_id=peer, device_id_type=MESH)
        pl.semaphore_wait(barrier, len(peers))
        me = 4 * x + 2 * y + c
        own = []
        for t in range(nt):
            cp = pltpu.make_async_copy(ins[t].at[me] if scatter else ins[t], outs[t].at[me], local_sems[t])
            cp.start()
            own.append(cp)
            for px, py, pc in peers:
                src = ins[t].at[4 * px + 2 * py + pc] if scatter else ins[t]
                pltpu.make_async_remote_copy(src_ref=src, dst_ref=outs[t].at[me], send_sem=send_sems[t],
                                             recv_sem=recv_sems[t], device_id=(px, py, pc), device_id_type=MESH).start()
        for t in range(nt):
            own[t].wait()
            seven = outs[t].at[pl.ds(0, NDEV - 1)]
            drain = pltpu.make_async_remote_copy(src_ref=seven, dst_ref=seven, send_sem=send_sems[t],
                                                 recv_sem=recv_sems[t], device_id=(x, y, c), device_id_type=MESH)
            drain.wait_send()
            drain.wait_recv()

    return pl.kernel(
        body, out_type=out_type, mesh=plsc.ScalarSubcoreMesh(axis_name="sequencer", num_cores=1),
        scratch_types=[pltpu.SemaphoreType.DMA] * (3 * nt),
        compiler_params=pltpu.CompilerParams(collective_id=collective_id), name=name,
    )(*arrays)


def _sc_gather_two_level(name, collective_id, arrays):
    nt = len(arrays)
    out_type = [jax.ShapeDtypeStruct((NDEV,) + a.shape, a.dtype) for a in arrays]

    def body(*refs):
        ins, outs = refs[:nt], refs[nt:2 * nt]
        sems = refs[2 * nt:]
        send_sems, sib_sems, local_sems = sems[:nt], sems[nt:2 * nt], sems[2 * nt:3 * nt]
        ici_sems = [sems[3 * nt + 3 * t:3 * nt + 3 * t + 3] for t in range(nt)]
        x, y, c = lax.axis_index("x"), lax.axis_index("y"), lax.axis_index("c")
        sibling = (x, y, 1 - c)
        chips = [(1 - x, y), (x, 1 - y), (1 - x, 1 - y)]
        barrier = pltpu.get_barrier_semaphore()
        for peer in [sibling] + [(cx, cy, c) for cx, cy in chips]:
            pl.semaphore_signal(barrier, inc=1, device_id=peer, device_id_type=MESH)
        pl.semaphore_wait(barrier, 4)
        me = 4 * x + 2 * y + c

        def push(t, src, slot, recv_sem, to):
            pltpu.make_async_remote_copy(src_ref=src, dst_ref=outs[t].at[slot], send_sem=send_sems[t],
                                         recv_sem=recv_sem, device_id=to, device_id_type=MESH).start()

        own = []
        for t in range(nt):
            cp = pltpu.make_async_copy(ins[t], outs[t].at[me], local_sems[t])
            cp.start()
            own.append(cp)
            for j, (cx, cy) in enumerate(chips):
                push(t, ins[t], me, ici_sems[t][j], (cx, cy, c))
            push(t, ins[t], me, sib_sems[t], sibling)
        for t in range(nt):
            for j, (cx, cy) in enumerate(chips):
                slot = 4 * cx + 2 * cy + c
                landed = outs[t].at[slot]
                pltpu.make_async_remote_copy(src_ref=landed, dst_ref=landed, send_sem=send_sems[t],
                                             recv_sem=ici_sems[t][j], device_id=(cx, cy, c),
                                             device_id_type=MESH).wait_recv()
                push(t, landed, slot, sib_sems[t], sibling)
        for t in range(nt):
            own[t].wait()
            four, seven = outs[t].at[pl.ds(0, 4)], outs[t].at[pl.ds(0, 7)]
            pltpu.make_async_remote_copy(src_ref=four, dst_ref=four, send_sem=send_sems[t], recv_sem=sib_sems[t],
                                         device_id=sibling, device_id_type=MESH).wait_recv()
            pltpu.make_async_remote_copy(src_ref=seven, dst_ref=seven, send_sem=send_sems[t], recv_sem=sib_sems[t],
                                         device_id=sibling, device_id_type=MESH).wait_send()

    return pl.kernel(
        body, out_type=out_type, mesh=plsc.ScalarSubcoreMesh(axis_name="sequencer", num_cores=1),
        scratch_types=[pltpu.SemaphoreType.DMA] * (6 * nt),
        compiler_params=pltpu.CompilerParams(collective_id=collective_id), name=name,
    )(*arrays)


def _sc_sibling_exchange(name, collective_id, src, out_shape, pieces, after=None):
    extra = [] if after is None else [after]

    def body(src_ref, *rest):
        out_ref, send_sem, recv_sem = rest[len(extra):]
        x, y, c = lax.axis_index("x"), lax.axis_index("y"), lax.axis_index("c")
        sibling = (x, y, 1 - c)
        barrier = pltpu.get_barrier_semaphore()
        pl.semaphore_signal(barrier, inc=1, device_id=sibling, device_id_type=MESH)
        pl.semaphore_wait(barrier, 1)
        for piece, lands in pieces(c, src_ref, out_ref):
            pltpu.make_async_remote_copy(src_ref=piece, dst_ref=lands, send_sem=send_sem, recv_sem=recv_sem,
                                         device_id=sibling, device_id_type=MESH).start()
        drain = pltpu.make_async_remote_copy(src_ref=out_ref, dst_ref=out_ref, send_sem=send_sem, recv_sem=recv_sem,
                                             device_id=sibling, device_id_type=MESH)
        drain.wait_send()
        drain.wait_recv()

    return pl.kernel(
        body, out_type=jax.ShapeDtypeStruct(out_shape, src.dtype),
        mesh=plsc.ScalarSubcoreMesh(axis_name="sequencer", num_cores=1), scratch_types=[pltpu.SemaphoreType.DMA] * 2,
        compiler_params=pltpu.CompilerParams(collective_id=collective_id), name=name,
    )(src, *extra)


def _swap_class_columns(name, collective_id, dz, nb, piece=0, npieces=1):
    w = nb // npieces
    return _sc_sibling_exchange(
        name, collective_id, dz, (S, 4 * w),
        lambda c, src, out: [(src.at[:, pl.ds((2 * j + 1 - c) * nb + piece * w, w)], out.at[:, pl.ds(j * w, w)])
                             for j in range(4)])


def _sc_chip_scatter(name, collective_id, q):
    def body(q_ref, out_ref, send_sem, recv_sem, local_sem):
        x, y, c = lax.axis_index("x"), lax.axis_index("y"), lax.axis_index("c")
        chips = [(1 - x, y), (x, 1 - y), (1 - x, 1 - y)]
        barrier = pltpu.get_barrier_semaphore()
        for cx, cy in chips:
            pl.semaphore_signal(barrier, inc=1, device_id=(cx, cy, c), device_id_type=MESH)
        pl.semaphore_wait(barrier, 3)
        mine = 2 * x + y
        own = pltpu.make_async_copy(q_ref.at[mine], out_ref.at[mine], local_sem)
        own.start()
        for cx, cy in chips:
            pltpu.make_async_remote_copy(src_ref=q_ref.at[2 * cx + cy], dst_ref=out_ref.at[mine], send_sem=send_sem,
                                         recv_sem=recv_sem, device_id=(cx, cy, c), device_id_type=MESH).start()
        own.wait()
        three = out_ref.at[pl.ds(0, 3)]
        drain = pltpu.make_async_remote_copy(src_ref=three, dst_ref=three, send_sem=send_sem, recv_sem=recv_sem,
                                             device_id=(x, y, c), device_id_type=MESH)
        drain.wait_send()
        drain.wait_recv()

    return pl.kernel(
        body, out_type=jax.ShapeDtypeStruct(q.shape, q.dtype),
        mesh=plsc.ScalarSubcoreMesh(axis_name="sequencer", num_cores=1), scratch_types=[pltpu.SemaphoreType.DMA] * 3,
        compiler_params=pltpu.CompilerParams(collective_id=collective_id), name=name,
    )(q)


def _mm_pair_dw(h_own, dz, h_sib, dz_sib, nb, name, dep=None, piece=0, npieces=1, h_transposed=False,
                one_call=False):
    nb = nb // npieces
    tn = 512 if nb % 512 == 0 else nb
    per = nb // tn
    dn = NN if h_transposed else TN
    o_spec = pl.BlockSpec((None, D, tn), lambda i, j, k: (j // per, 0, j % per))
    own_col = lambda i, j, k: (0, ((2 * (j // per) + lax.axis_index("c")) * npieces + piece) * per + j % per)
    if one_call:
        def fused(a0_ref, b0_ref, a1_ref, b1_ref, dep_ref, o_ref):
            acc = _dot(a0_ref[...], b0_ref[...], dn) + _dot(a1_ref[...], b1_ref[...], dn)
            o_ref[...] = acc.astype(BF16)

        whole = pl.BlockSpec((S, D), lambda i, j, k: (0, 0), pipeline_mode=pl.Buffered(1))
        return pl.pallas_call(
            fused, grid=(1, 4 * per, 1),
            in_specs=[whole, pl.BlockSpec((S, tn), own_col), whole, pl.BlockSpec((S, tn), lambda i, j, k: (0, j)),
                      ANY_SPEC],
            out_specs=o_spec, out_shape=jax.ShapeDtypeStruct((4, D, nb), BF16),
            compiler_params=_params(("parallel", "parallel", "arbitrary"), VMEM_BIG), name=name,
        )(h_own, dz, h_sib, dz_sib, dep)
    part = _matmul(
        h_own, dz, dn=dn, grid=(1, 4 * per, 1),
        a_spec=pl.BlockSpec((S, D), lambda i, j, k: (0, 0)), b_spec=pl.BlockSpec((S, tn), own_col),
        o_spec=o_spec, out_shape=(4, D, nb), out_dtype=F32, acc_shape=(D, tn), name=name + "_own", dep=dep)

    def body(a_ref, b_ref, p_ref, o_ref):
        o_ref[...] = (p_ref[...] + _dot(a_ref[...], b_ref[...], dn)).astype(BF16)

    return pl.pallas_call(
        body, grid=(1, 4 * per, 1),
        in_specs=[pl.BlockSpec((S, D), lambda i, j, k: (0, 0)), pl.BlockSpec((S, tn), lambda i, j, k: (0, j)), o_spec],
        out_specs=o_spec, out_shape=jax.ShapeDtypeStruct((4, D, nb), BF16),
        compiler_params=_params(("parallel", "parallel", "arbitrary"), VMEM_BIG), name=name + "_sibling",
    )(h_sib, dz_sib, part)


SMALL = {
    "e_pre_norm": ((2048,), None), "e_pool_w": ((4, 256, 256), 1), "e_pool_scale": ((1024,), None),
    "e_post_norm": ((2048,), None), "o_pre_norm": ((2048,), 0), "o_sgu_norm_g": ((1024,), 0),
    "o_sgu_norm_b": ((1024,), 0), "o_sgu_w": ((4, 128, 128), None), "o_sgu_b": ((4, 128), None),
    "o_conv_w": ((31, 1024), 1), "o_conv_b": ((1024,), 0), "o_conv_norm_g": ((1024,), 0),
    "o_conv_norm_b": ((1024,), 0), "o_post_norm": ((2048,), 0),
}
SMALL_SHARDED = [n for n, (_, ax) in SMALL.items() if ax is not None]


def _shard_shape(name):
    shape, ax = SMALL[name]
    if ax is None:
        return shape
    return tuple(s // NDEV if i == ax else s for i, s in enumerate(shape))


def _pack(arrs, row_multiple=1):
    flat = jnp.concatenate([a.reshape(-1) for a in arrs])
    pad = -flat.shape[0] % (128 * row_multiple)
    return jnp.concatenate([flat, jnp.zeros((pad,), F32)]).reshape(-1, 128)


def _small_views(name):
    shape, ax = SMALL[name]
    me = lambda: 4 * lax.axis_index("x") + 2 * lax.axis_index("y") + lax.axis_index("c")
    if ax is None:
        view = (int(np.prod(shape)) // 128, 128)
        return view, view, pl.BlockSpec(view, lambda i: (0, 0))
    if len(shape) == 1:
        n = shape[0] // NDEV
        return (1, n), (NDEV, 1, n), pl.BlockSpec((None, 1, n), lambda i: (me(), 0, 0))
    part = _shard_shape(name)
    return part, shape, pl.BlockSpec(part, lambda i: tuple(me() if d == ax else 0 for d in range(len(shape))))


WEIGHTS = ["e_pre_norm", "e_w_in", "e_pool_w", "e_pool_scale", "e_w_out", "e_post_norm", "o_pre_norm", "o_w_in",
           "o_sgu_norm_g", "o_sgu_norm_b", "o_sgu_w", "o_sgu_b", "o_conv_w", "o_conv_b", "o_conv_norm_g",
           "o_conv_norm_b", "o_w_out", "o_post_norm"]


def kernel(x, e_pre_norm, e_w_in, e_pool_w, e_pool_scale, e_w_out, e_post_norm, o_pre_norm, o_w_in, o_sgu_norm_g, o_sgu_norm_b, o_sgu_w, o_sgu_b, o_conv_w, o_conv_b, o_conv_norm_g, o_conv_norm_b, o_w_out, o_post_norm, loss_target, m_e_pre_norm, m_e_w_in, m_e_pool_w, m_e_pool_scale, m_e_w_out, m_e_post_norm, m_o_pre_norm, m_o_w_in, m_o_sgu_norm_g, m_o_sgu_norm_b, m_o_sgu_w, m_o_sgu_b, m_o_conv_w, m_o_conv_b, m_o_conv_norm_g, m_o_conv_norm_b, m_o_w_out, m_o_post_norm, v_e_pre_norm, v_e_w_in, v_e_pool_w, v_e_pool_scale, v_e_w_out, v_e_post_norm, v_o_pre_norm, v_o_w_in, v_o_sgu_norm_g, v_o_sgu_norm_b, v_o_sgu_w, v_o_sgu_b, v_o_conv_w, v_o_conv_b, v_o_conv_norm_g, v_o_conv_norm_b, v_o_w_out, v_o_post_norm):
    given = dict(locals())
    w = {n: given[n][0] for n in WEIGHTS}
    m = {n: given["m_" + n][0] for n in WEIGHTS}
    v = {n: given["v_" + n][0] for n in WEIGHTS}
    me = 4 * lax.axis_index("x") + 2 * lax.axis_index("y") + lax.axis_index("c")
    x, target = x[0], loss_target[0]
    row = lambda a: a.reshape(1, -1)

    lo, small_rows = _sc_gather_two_level(
        "gather_a0", 0, [_cast_bf16(w["e_w_in"], "cast_e_w_in_0", 0, 2), _pack([w[n] for n in SMALL_SHARDED])])
    hi, = _sc_gather_two_level("gather_a1", 12, [_cast_bf16(w["e_w_in"], "cast_e_w_in_1", 1, 2)])
    wg_e_in = (lo, hi)
    h0, h0t = _pre0_fwd(x, row(w["e_pre_norm"]))
    wg_e_out, = _sc_gather_two_level("gather_b", 1, [_cast_bf16(w["e_w_out"], "cast_e_w_out")])
    wg_o_in, = _sc_gather_two_level("gather_c", 13, [_cast_bf16(w["o_w_in"], "cast_o_w_in")])
    wg_o_out, = _sc_gather_two_level("gather_d", 16, [_cast_bf16(w["o_w_out"], "cast_o_w_out")])
    p = {n: w[n] for n in SMALL if SMALL[n][1] is None}
    small_rows = small_rows.reshape(NDEV, -1)
    off = 0
    for n in SMALL_SHARDED:
        shp, ax = _shard_shape(n), SMALL[n][1]
        cnt = int(np.prod(shp))
        blk = small_rows[:, off:off + cnt].reshape((NDEV,) + shp)
        p[n] = jnp.moveaxis(blk, 0, ax).reshape(SMALL[n][0])
        off += cnt
    tabs = _rope_tables()
    pool_w_bf = p["e_pool_w"].astype(BF16)
    sgu_bb = jnp.broadcast_to(p["o_sgu_b"][:, :, None], (4, 128, 128))
    conv_w = jnp.concatenate([p["o_conv_w"], jnp.zeros((HALO - CONV_K, HALF), F32)], axis=0)
    odd_p = (row(p["o_sgu_norm_g"]), row(p["o_sgu_norm_b"]), p["o_sgu_w"], sgu_bb, conv_w,
             row(p["o_conv_b"]), row(p["o_conv_norm_g"]), row(p["o_conv_norm_b"]))

    z0 = _mm_in_halves(h0, wg_e_in, "mm_z0")
    ycat0 = _pool_fwd(z0, pool_w_bf, row(p["e_pool_scale"]))
    ycat0, og, lg, qkv = _attn_fwd(z0, tabs, ycat0)
    w_out_e, w_out_o = wg_e_out.reshape(2048, D), wg_o_out.reshape(2048, D)
    y0, x1, h1 = _post0_fwd(ycat0, w_out_e, x, row(p["e_post_norm"]), row(p["o_pre_norm"]), wg_e_out)
    h0t_sib = _sc_sibling_exchange("swap_h0", 8, h0t, h0t.shape, lambda c, src, out: [(src, out)], h1)
    h1_sib = _sc_sibling_exchange("swap_h1", 11, h1, h1.shape, lambda c, src, out: [(src, out)])
    z1 = _mm_in(h1, wg_o_in, "mm_z1")
    ycat1, conv_out = _odd_fwd(z1, *odd_p)

    g = {}
    loss_part, dx2, dy1, g["o_post_norm"] = _post1_bwd(ycat1, w_out_o, x1, target, row(p["o_post_norm"]),
                                                       _arrived(h1_sib, "arrived_h_sib", h0t_sib))
    parts = {}
    dw = _mm_out_dw(ycat1, dy1, "mm_dwout1").reshape(NDEV, 256, D)
    parts["o_w_out"], = _sc_exchange("scatter_o_w_out", 2, [dw], True)
    dycat1 = _mm_out_dx(dy1, w_out_o, "mm_dycat1", dw)
    dz1, g["o_sgu_w"], d_sgu_bb, g["o_sgu_norm_g"], g["o_sgu_norm_b"], g["o_conv_norm_g"], g["o_conv_norm_b"], \
        g["o_conv_b"], d_conv_w = _odd_bwd(z1, conv_out, dycat1, *odd_p[:4], conv_w, *odd_p[6:])
    g["o_sgu_b"] = d_sgu_bb[:, :, 0]
    g["o_conv_w"] = d_conv_w[:CONV_K]
    grads, deltas, new_m, new_v = {}, {}, {}, {}

    def adam(n, dep):
        grads[n], deltas[n], new_m[n], new_v[n] = _adam_reduce(parts[n], w[n], m[n], v[n], "adam_" + n, dep)
        return new_v[n]

    pin = _arrived(parts["o_w_out"], "arrived_o_w_out", d_conv_w)
    dz1_sib = _swap_class_columns("swap_dz1", 10, dz1, ODD_IN // NDEV)
    dw = _mm_pair_dw(h1, dz1, h1_sib, dz1_sib, ODD_IN // NDEV, "mm_dwin1", pin)
    parts["o_w_in"] = _sc_chip_scatter("scatter_o_w_in", 3, dw)
    dh1 = _mm_in_dx(dz1, wg_o_in, "mm_dh1", dw)
    dx1, dy0, g["o_pre_norm"], g["e_post_norm"] = _mid_bwd(dx2, dh1, x1, y0, row(p["o_pre_norm"]),
                                                           row(p["e_post_norm"]))
    dw = _mm_out_dw(ycat0, dy0, "mm_dwout0").reshape(NDEV, 256, D)
    parts["e_w_out"], = _sc_exchange("scatter_e_w_out", 4, [dw], True)
    da_in, da_gate, g["e_pool_w"], g["e_pool_scale"] = _pool_bwd(z0, dy0, w_out_e, pool_w_bf,
                                                                 row(p["e_pool_scale"]), dw)
    late = [n for n in SMALL if n not in ("e_pre_norm", "o_sgu_b")] + ["o_sgu_b"]
    pieces = [g[n].reshape(SMALL[n][0]) for n in late[:-1]] + [jnp.broadcast_to(loss_part, (8, 128)), g[late[-1]]]
    recv_small, = _sc_gather_two_level("gather_small_grads", 6, [_pack(pieces, 512)])
    took = _arrived(parts["o_w_in"], "arrived_o_w_in")
    dz0 = _attn_bwd(z0, qkv, og, lg, dy0, w_out_e, tabs, took, da_in, da_gate)
    took = _arrived(recv_small, "arrived_small_grads", _arrived(parts["e_w_out"], "arrived_e_w_out", dz0))
    nb = EVEN_IN // NDEV
    swapped = [_swap_class_columns("swap_dz0_%d" % half, (9, 14)[half], dz0, nb, half, 2) for half in (0, 1)]
    dw, e_w_in_parts = took, []
    for half in (0, 1):
        dw = _mm_pair_dw(h0t, dz0, h0t_sib, swapped[half], nb, "mm_dwin0_%d" % half, dw, half, 2, True, half == 1)
        e_w_in_parts.append(_sc_chip_scatter("scatter_e_w_in_%d" % half, (5, 15)[half], dw))
    pin = adam("e_w_out", adam("o_w_out", adam("o_w_in", dw)))
    rows = [int(np.prod(SMALL[n][0])) // 128 for n in late]
    sums = _sum_unpack(recv_small, rows[:-1] + [8, rows[-1]], "sum_small_grads", pin)
    summed = dict(zip(late, sums[:-2] + sums[-1:]))
    loss = sums[-2][0, 0]
    dh0 = _mm_in_dx_halves(dz0, wg_e_in, "mm_dh0", summed[late[0]])
    grad_x, g["e_pre_norm"] = _pre0_bwd(dx1, dh0, x, row(p["e_pre_norm"]))
    last, = _sc_exchange("gather_e_pre_norm_grad", 7, [g["e_pre_norm"].reshape(16, 128)], False)

    n = "e_w_in"
    out = _adam_reduce(e_w_in_parts[0], w[n], m[n], v[n], "adam_e_w_in_0", grad_x, 0, 2)
    out = _adam_reduce(e_w_in_parts[1], w[n], m[n], v[n], "adam_e_w_in_1", None, 1, 2, out)
    grads[n], deltas[n], new_m[n], new_v[n] = out
    summed["e_pre_norm"] = _sum_parts(last, "sum_e_pre_norm_grad", out[3])
    names = list(SMALL)
    views = [_small_views(n) for n in names]
    mine = lambda src: [src[n].reshape(vw[0]) for n, vw in zip(names, views)]
    res = _adam_small(mine(w), [summed[n].reshape(vw[1]) for n, vw in zip(names, views)], [vw[2] for vw in views],
                      mine(m), mine(v), "adam_small")
    for n, out in zip(names, res):
        grads[n], deltas[n], new_m[n], new_v[n] = [t.reshape(_shard_shape(n)) for t in out]

    lead = lambda a: a[None]
    return (loss, grad_x[None], *[lead(grads[n]) for n in WEIGHTS], *[lead(deltas[n]) for n in WEIGHTS],
            *[lead(new_m[n]) for n in WEIGHTS], *[lead(new_v[n]) for n in WEIGHTS])
```

```python
import numpy as np
import jax
import jax.numpy as jnp
from jax import lax
from jax.experimental import pallas as pl
from jax.experimental.pallas import tpu as pltpu
from jax.experimental.pallas import tpu_sc as plsc

F32 = jnp.float32
BF16 = jnp.bfloat16

S = 2048
D = 2048
NDEV = 8
EPS = 1e-6
NEG = -1e30
HEAD_DIM = 128
ROT_DIM = 32
ROPE_THETA = 500000.0
PATTERNS = ((128, 1), (512, 4), (2048, 16))
BLK = 128
EVEN_IN = 12288
ODD_IN = 6144
HALF = 1024
CONV_K = 31
HALO = 32
TR = 256
SUB = 16

ADAM_LR = 0.001
ADAM_B1 = 0.9
ADAM_B2 = 0.999
ADAM_EPS = 1e-08
ADAM_WD = 0.01
ADAM_STEP = 10

VMEM_BIG = 56 * 1024 * 1024
MESH = pl.DeviceIdType.MESH

NN = (((1,), (0,)), ((), ()))
NT = (((1,), (1,)), ((), ()))
TN = (((0,), (0,)), ((), ()))


def _dot(a, b, dn=NN):
    return lax.dot_general(a, b, dn, preferred_element_type=F32)


def _sigmoid(x):
    return 1.0 / (1.0 + jnp.exp(-x))


def _silu_and_grad(x):
    sg = _sigmoid(x)
    return x * sg, sg * (1.0 + x * (1.0 - sg))


def _params(sem, vmem=None):
    return pltpu.CompilerParams(dimension_semantics=sem, vmem_limit_bytes=vmem)


ANY_SPEC = pl.BlockSpec(memory_space=pl.ANY)


def _matmul(a, b, *, dn, grid, a_spec, b_spec, o_spec, out_shape, out_dtype, acc_shape, name, dep=None):
    nk = grid[2]
    deps = [] if dep is None else list(dep) if isinstance(dep, (tuple, list)) else [dep]

    def body(a_ref, b_ref, *rest):
        o_ref, acc = rest[len(deps)], rest[len(deps) + 1:]
        if nk == 1:
            o_ref[...] = _dot(a_ref[...], b_ref[...], dn).astype(o_ref.dtype)
            return
        acc_ref = acc[0]
        k = pl.program_id(2)

        @pl.when(k == 0)
        def _():
            acc_ref[...] = jnp.zeros_like(acc_ref)

        acc_ref[...] += _dot(a_ref[...], b_ref[...], dn)

        @pl.when(k == nk - 1)
        def _():
            o_ref[...] = acc_ref[...].astype(o_ref.dtype)

    return pl.pallas_call(
        body, grid=grid, in_specs=[a_spec, b_spec] + [ANY_SPEC] * len(deps), out_specs=o_spec,
        out_shape=jax.ShapeDtypeStruct(out_shape, out_dtype),
        scratch_shapes=[] if nk == 1 else [pltpu.VMEM(acc_shape, F32)],
        compiler_params=_params(("parallel", "parallel", "arbitrary"), VMEM_BIG), name=name,
    )(a, b, *deps)


TM = 2048


def _mm_in(h, wg, name):
    nb = wg.shape[2]
    tn = 512 if nb % 512 == 0 else nb
    per = nb // tn
    return _matmul(
        h, wg, dn=NN, grid=(S // TM, NDEV * per, 1),
        a_spec=pl.BlockSpec((TM, D), lambda i, j, k: (i, 0)),
        b_spec=pl.BlockSpec((None, D, tn), lambda i, j, k: (j // per, 0, j % per)),
        o_spec=pl.BlockSpec((TM, tn), lambda i, j, k: (i, j)),
        out_shape=(S, NDEV * nb), out_dtype=F32, acc_shape=(TM, tn), name=name)


def _mm_in_halves(h, wg_halves, name):
    hb = wg_halves[0].shape[2]
    z = None
    for half, wg in enumerate(wg_halves):
        prev = [] if z is None else [z]

        def body(a_ref, b_ref, *rest):
            rest[-1][...] = _dot(a_ref[...], b_ref[...])

        z = pl.pallas_call(
            body, grid=(NDEV,),
            in_specs=[pl.BlockSpec((S, D), lambda j: (0, 0)), pl.BlockSpec((None, D, hb), lambda j: (j, 0, 0))]
                     + [ANY_SPEC] * len(prev),
            out_specs=pl.BlockSpec((S, hb), lambda j, half=half: (0, 2 * j + half)),
            out_shape=jax.ShapeDtypeStruct((S, 2 * NDEV * hb), F32),
            input_output_aliases={2: 0} if prev else {},
            compiler_params=_params(("parallel",), VMEM_BIG), name="%s_%d" % (name, half),
        )(h, wg, *prev)
    return z


def _mm_in_dx_halves(dz, wg_halves, name, dep):
    hb = wg_halves[0].shape[2]
    nk = 2 * NDEV

    def body(a_ref, b0_ref, b1_ref, dep_ref, o_ref, acc_ref):
        k = pl.program_id(2)

        @pl.when(k == 0)
        def _():
            acc_ref[...] = jnp.zeros_like(acc_ref)

        @pl.when(k % 2 == 0)
        def _():
            acc_ref[...] += _dot(a_ref[...], b0_ref[...], NT)

        @pl.when(k % 2 == 1)
        def _():
            acc_ref[...] += _dot(a_ref[...], b1_ref[...], NT)

        @pl.when(k == nk - 1)
        def _():
            o_ref[...] = acc_ref[...]

    b_spec = pl.BlockSpec((None, 1024, hb), lambda i, j, k: (k // 2, j, 0))
    return pl.pallas_call(
        body, grid=(1, D // 1024, nk),
        in_specs=[pl.BlockSpec((S, hb), lambda i, j, k: (0, k)), b_spec, b_spec, ANY_SPEC],
        out_specs=pl.BlockSpec((S, 1024), lambda i, j, k: (0, j)), out_shape=jax.ShapeDtypeStruct((S, D), F32),
        scratch_shapes=[pltpu.VMEM((S, 1024), F32)],
        compiler_params=_params(("parallel", "parallel", "arbitrary"), VMEM_BIG), name=name,
    )(dz, *wg_halves, dep)


def _mm_in_dx(dz, wg, name, dep=None):
    nb = wg.shape[2]
    return _matmul(
        dz, wg, dn=NT, grid=(S // TM, D // 1024, NDEV),
        a_spec=pl.BlockSpec((TM, nb), lambda i, j, k: (i, k)),
        b_spec=pl.BlockSpec((None, 1024, nb), lambda i, j, k: (k, j, 0)),
        o_spec=pl.BlockSpec((TM, 1024), lambda i, j, k: (i, j)),
        out_shape=(S, D), out_dtype=F32, acc_shape=(TM, 1024), name=name, dep=dep)


def _mm_out_dx(dy, w, name, dep=None):
    return _matmul(
        dy, w, dn=NT, grid=(S // TM, 2048 // 512, 1),
        a_spec=pl.BlockSpec((TM, D), lambda i, j, k: (i, 0)),
        b_spec=pl.BlockSpec((512, D), lambda i, j, k: (j, 0)),
        o_spec=pl.BlockSpec((TM, 512), lambda i, j, k: (i, j)),
        out_shape=(S, 2048), out_dtype=F32, acc_shape=(TM, 512), name=name, dep=dep)


def _mm_out_dw(yc, dy, name):
    return _matmul(
        yc, dy, dn=TN, grid=(2048 // TM, D // 512, 1),
        a_spec=pl.BlockSpec((S, TM), lambda i, j, k: (0, i)),
        b_spec=pl.BlockSpec((S, 512), lambda i, j, k: (0, j)),
        o_spec=pl.BlockSpec((TM, 512), lambda i, j, k: (i, j)),
        out_shape=(2048, D), out_dtype=BF16, acc_shape=(TM, 512), name=name)


def _row_spec(w=D):
    return pl.BlockSpec((TR, w), lambda i: (i, 0))


def _vec_spec(w=D):
    return pl.BlockSpec((1, w), lambda i: (0, 0))


def _rms_stats(x):
    r = lax.rsqrt(jnp.mean(x * x, axis=-1, keepdims=True) + EPS)
    return x * r, r


def _rms_bwd(dn, xhat, r, g):
    dxh = dn * g
    return r * (dxh - xhat * jnp.mean(dxh * xhat, axis=-1, keepdims=True))


def _acc_rows(ref, val, i):
    s = jnp.sum(val, axis=0, keepdims=True)

    @pl.when(i == 0)
    def _():
        ref[...] = s

    @pl.when(i > 0)
    def _():
        ref[...] += s


def _pre0_fwd(x, g):
    def body(x_ref, g_ref, h_ref, ht_ref):
        xhat, _ = _rms_stats(x_ref[...])
        h = xhat * g_ref[...]
        h_ref[...] = h.astype(BF16)
        ht_ref[...] = h.T.astype(BF16)

    return pl.pallas_call(
        body, grid=(S // TR,), in_specs=[_row_spec(), _vec_spec()],
        out_specs=[_row_spec(), pl.BlockSpec((D, TR), lambda i: (0, i))],
        out_shape=[jax.ShapeDtypeStruct((S, D), BF16), jax.ShapeDtypeStruct((D, S), BF16)],
        compiler_params=_params(("parallel",)), name="pre0_fwd",
    )(x, g)


def _post0_fwd(ycat, w_out, x, g_post, g_pre1, dep):
    def body(yc_ref, w_ref, x_ref, gp_ref, g1_ref, dep_ref, y_ref, x1_ref, h1_ref):
        y = _dot(yc_ref[...], w_ref[...])
        y_ref[...] = y
        yhat, _ = _rms_stats(y)
        x1 = x_ref[...] + yhat * gp_ref[...]
        x1_ref[...] = x1
        xhat, _ = _rms_stats(x1)
        h1_ref[...] = (xhat * g1_ref[...]).astype(BF16)

    return pl.pallas_call(
        body, grid=(S // TR,),
        in_specs=[_row_spec(), pl.BlockSpec((2048, D), lambda i: (0, 0)), _row_spec(), _vec_spec(), _vec_spec(),
                  ANY_SPEC],
        out_specs=[_row_spec(), _row_spec(), _row_spec()],
        out_shape=[jax.ShapeDtypeStruct((S, D), F32), jax.ShapeDtypeStruct((S, D), F32),
                   jax.ShapeDtypeStruct((S, D), BF16)],
        compiler_params=_params(("parallel",), VMEM_BIG), name="post0_fwd",
    )(ycat, w_out, x, g_post, g_pre1, dep)


def _post1_bwd(ycat, w_out, x1, target, g_post, dep):
    def body(yc_ref, w_ref, x1_ref, t_ref, g_ref, dep_ref, loss_ref, dx2_ref, dy_ref, dg_ref):
        i = pl.program_id(0)
        yhat, r = _rms_stats(_dot(yc_ref[...], w_ref[...]))
        g = g_ref[...]
        err = x1_ref[...] + yhat * g - t_ref[...]
        part = jnp.sum(jnp.sum(err * err, axis=-1, keepdims=True), axis=0, keepdims=True) * (0.5 / D)
        _acc_rows(loss_ref, jnp.broadcast_to(part, (1, 128)), i)
        dx2 = err * (1.0 / D)
        dx2_ref[...] = dx2
        _acc_rows(dg_ref, dx2 * yhat, i)
        dy_ref[...] = _rms_bwd(dx2, yhat, r, g).astype(BF16)

    return pl.pallas_call(
        body, grid=(S // TR,),
        in_specs=[_row_spec(), pl.BlockSpec((2048, D), lambda i: (0, 0)), _row_spec(), _row_spec(), _vec_spec(),
                  ANY_SPEC],
        out_specs=[_vec_spec(128), _row_spec(), _row_spec(), _vec_spec()],
        out_shape=[jax.ShapeDtypeStruct((1, 128), F32), jax.ShapeDtypeStruct((S, D), F32),
                   jax.ShapeDtypeStruct((S, D), BF16), jax.ShapeDtypeStruct((1, D), F32)],
        compiler_params=_params(("arbitrary",), VMEM_BIG), name="post1_bwd",
    )(ycat, w_out, x1, target, g_post, dep)


def _mid_bwd(dx2, dh1, x1, y0, g_pre1, g_post0):
    def body(dx2_ref, dh_ref, x1_ref, y_ref, g1_ref, gp_ref, dx1_ref, dy_ref, dg1_ref, dgp_ref):
        i = pl.program_id(0)
        xhat, r1 = _rms_stats(x1_ref[...])
        dh = dh_ref[...]
        _acc_rows(dg1_ref, dh * xhat, i)
        dx1 = dx2_ref[...] + _rms_bwd(dh, xhat, r1, g1_ref[...])
        dx1_ref[...] = dx1
        yhat, r0 = _rms_stats(y_ref[...])
        _acc_rows(dgp_ref, dx1 * yhat, i)
        dy_ref[...] = _rms_bwd(dx1, yhat, r0, gp_ref[...]).astype(BF16)

    return pl.pallas_call(
        body, grid=(S // TR,),
        in_specs=[_row_spec(), _row_spec(), _row_spec(), _row_spec(), _vec_spec(), _vec_spec()],
        out_specs=[_row_spec(), _row_spec(), _vec_spec(), _vec_spec()],
        out_shape=[jax.ShapeDtypeStruct((S, D), F32), jax.ShapeDtypeStruct((S, D), BF16),
                   jax.ShapeDtypeStruct((1, D), F32), jax.ShapeDtypeStruct((1, D), F32)],
        compiler_params=_params(("arbitrary",)), name="mid_bwd",
    )(dx2, dh1, x1, y0, g_pre1, g_post0)


def _pre0_bwd(dx1, dh0, x, g):
    def body(dx1_ref, dh_ref, x_ref, g_ref, gx_ref, dg_ref):
        i = pl.program_id(0)
        xhat, r = _rms_stats(x_ref[...])
        dh = dh_ref[...]
        _acc_rows(dg_ref, dh * xhat, i)
        gx_ref[...] = dx1_ref[...] + _rms_bwd(dh, xhat, r, g_ref[...])

    return pl.pallas_call(
        body, grid=(S // TR,), in_specs=[_row_spec(), _row_spec(), _row_spec(), _vec_spec()],
        out_specs=[_row_spec(), _vec_spec()],
        out_shape=[jax.ShapeDtypeStruct((S, D), F32), jax.ShapeDtypeStruct((1, D), F32)],
        compiler_params=_params(("arbitrary",)), name="pre0_bwd",
    )(dx1, dh0, x, g)


POOL_CH = 256


def _pool_apply(a, w, transpose):
    n = a.shape[0]
    row = lax.broadcasted_iota(jnp.int32, a.shape, 0)
    cnt = jnp.minimum(row + 1, w).astype(F32)
    s = a / cnt if transpose else a
    for k in (1, 2, 4, 8):
        if transpose:
            sh = jnp.where(row < n - k, pltpu.roll(s, n - k, 0), 0.0)
        else:
            sh = jnp.where(row >= k, pltpu.roll(s, k, 0), 0.0)
        s = jnp.where(w > k, s + sh, s)
    return s - a if transpose else s / cnt - a


def _pool_fwd(z0, pool_w, pool_scale):
    def body(a_ref, gate_ref, w_ref, sc_ref, out_ref):
        win = jnp.left_shift(2, pl.program_id(0))
        pooled = _pool_apply(a_ref[...], win, False)
        mixed = _dot(pooled.astype(BF16), w_ref[...])
        gate = gate_ref[...]
        out_ref[...] = (mixed * sc_ref[...] * (gate * _sigmoid(gate))).astype(BF16)

    return pl.pallas_call(
        body, grid=(4,),
        in_specs=[pl.BlockSpec((S, POOL_CH), lambda g: (0, g)), pl.BlockSpec((S, POOL_CH), lambda g: (0, 4 + g)),
                  pl.BlockSpec((None, POOL_CH, POOL_CH), lambda g: (g, 0, 0)),
                  pl.BlockSpec((1, POOL_CH), lambda g: (0, g))],
        out_specs=pl.BlockSpec((S, POOL_CH), lambda g: (0, g)),
        out_shape=jax.ShapeDtypeStruct((S, 2048), BF16),
        compiler_params=_params(("parallel",), VMEM_BIG), name="pool_fwd",
    )(z0, z0, pool_w, pool_scale)


def _pool_bwd(z0, dycat, pool_w, pool_scale):
    def body(a_ref, gate_ref, dy_ref, w_ref, sc_ref, da_ref, dgate_ref, dw_ref, dsc_ref):
        win = jnp.left_shift(2, pl.program_id(0))
        pooled = _pool_apply(a_ref[...], win, False).astype(BF16)
        w = w_ref[...]
        mixed = _dot(pooled, w)
        silu, dsilu = _silu_and_grad(gate_ref[...])
        dy = dy_ref[...]
        sc = sc_ref[...]
        dgate_ref[...] = (dy * (mixed * sc) * dsilu).astype(BF16)
        dms = dy * silu
        dsc_ref[...] = jnp.sum(dms * mixed, axis=0, keepdims=True)
        dmixed = (dms * sc).astype(BF16)
        dw_ref[...] = _dot(pooled, dmixed, TN)
        dpooled = _dot(dmixed, w, NT)
        da_ref[...] = _pool_apply(dpooled, win, True).astype(BF16)

    slab = lambda off: pl.BlockSpec((S, POOL_CH), lambda g: (0, off + g))
    return pl.pallas_call(
        body, grid=(4,),
        in_specs=[slab(0), slab(4), slab(0), pl.BlockSpec((None, POOL_CH, POOL_CH), lambda g: (g, 0, 0)),
                  pl.BlockSpec((1, POOL_CH), lambda g: (0, g))],
        out_specs=[slab(0), slab(0), pl.BlockSpec((None, POOL_CH, POOL_CH), lambda g: (g, 0, 0)),
                   pl.BlockSpec((1, POOL_CH), lambda g: (0, g))],
        out_shape=[jax.ShapeDtypeStruct((S, HALF), BF16), jax.ShapeDtypeStruct((S, HALF), BF16),
                   jax.ShapeDtypeStruct((4, POOL_CH, POOL_CH), F32), jax.ShapeDtypeStruct((1, HALF), F32)],
        compiler_params=_params(("parallel",), VMEM_BIG), name="pool_bwd",
    )(z0, z0, dycat, pool_w, pool_scale)


Q_COL, K_COL, V_COL, BG_COL = 2048 // 128, 5120 // 128, 8192 // 128, 11264 // 128
SCALE = HEAD_DIM ** -0.5


def _rope_tables():
    pos = jnp.arange(S, dtype=F32)
    inv_freq = jnp.power(ROPE_THETA, -jnp.arange(0, ROT_DIM, 2, dtype=F32) / ROT_DIM)
    ang = pos[:, None] * inv_freq[None, :]
    cos, sin = jnp.cos(ang), jnp.sin(ang)
    half = ROT_DIM // 2
    zeros = jnp.zeros((S, HEAD_DIM - ROT_DIM), F32)
    c = jnp.concatenate([cos, cos, jnp.ones((S, HEAD_DIM - ROT_DIM), F32)], axis=1)
    a = jnp.concatenate([-sin, jnp.zeros((S, half), F32), zeros], axis=1)
    b = jnp.concatenate([jnp.zeros((S, half), F32), sin, zeros], axis=1)
    return c, a, b


def _rope(t, c, a, b):
    half = ROT_DIM // 2
    return t * c + pltpu.roll(t, HEAD_DIM - half, 1) * a + pltpu.roll(t, half, 1) * b


def _rope_t(d, c, a, b):
    half = ROT_DIM // 2
    return d * c + pltpu.roll(d * a, half, 1) + pltpu.roll(d * b, HEAD_DIM - half, 1)


def _deinterleave(dst, src, dil, cast=None, dst_off=0):
    length = S // dil
    for r in range(dil):
        v = src[...] if dil == 1 else src[pl.ds(r, length, stride=dil), :]
        dst[dst_off + r * length:dst_off + (r + 1) * length, :] = v if cast is None else v.astype(cast)


def _interleave(dst, src, dil, src_off=0):
    length = S // dil
    for r in range(dil):
        if dil == 1:
            dst[...] = src[src_off:src_off + S, :]
        else:
            dst[pl.ds(r, length, stride=dil), :] = src[src_off + r * length:src_off + (r + 1) * length, :]


CU = 8
NUNITS = S // BLK
B_QK = (((2,), (2,)), ((0,), (0,)))
B_PV = (((2,), (1,)), ((0,), (0,)))
B_TN = (((1,), (1,)), ((0,), (0,)))


def _blocks(ref, first):
    return ref[first * BLK:(first + CU) * BLK, :].reshape(CU, BLK, HEAD_DIM)


def _chunk_scores(u0, nb, qd, kdp):
    q = _blocks(qd, u0)
    row = lax.broadcasted_iota(jnp.int32, (CU, BLK, BLK), 1)
    col = lax.broadcasted_iota(jnp.int32, (CU, BLK, BLK), 2)
    s_own = jnp.where(col <= row, _dot(q, _blocks(kdp, u0 + 1), B_QK) * SCALE, NEG)
    if nb == 1:
        return q, s_own, None
    unit = lax.broadcasted_iota(jnp.int32, (CU, BLK, BLK), 0) + u0
    s_prev = jnp.where((col >= row) & ((unit % nb) != 0), _dot(q, _blocks(kdp, u0), B_QK) * SCALE, NEG)
    return q, s_own, s_prev


def _qkv_prep(z0, tabs):
    def body(q_ref, k_ref, v_ref, c_ref, a_ref, b_ref, qo_ref, ko_ref, vo_ref, tmp):
        p = pl.program_id(1)
        ko_ref[0:BLK, :] = jnp.zeros((BLK, HEAD_DIM), BF16)
        vo_ref[0:BLK, :] = jnp.zeros((BLK, HEAD_DIM), BF16)
        for gi, (_, dil) in enumerate(PATTERNS):
            @pl.when(p == gi)
            def _(dil=dil):
                c, a, b = c_ref[...], a_ref[...], b_ref[...]
                tmp[...] = _rope(q_ref[...], c, a, b)
                _deinterleave(qo_ref, tmp, dil, BF16)
                tmp[...] = _rope(k_ref[...], c, a, b)
                _deinterleave(ko_ref, tmp, dil, BF16, BLK)
                _deinterleave(vo_ref, v_ref, dil, BF16, BLK)

    tab = pl.BlockSpec((S, HEAD_DIM), lambda h, p: (0, 0))
    out = pl.BlockSpec((S, HEAD_DIM), lambda h, p: (0, p * 8 + h))
    outp = pl.BlockSpec((S + BLK, HEAD_DIM), lambda h, p: (0, p * 8 + h))
    return pl.pallas_call(
        body, grid=(8, 3), in_specs=[_head_spec(Q_COL), _head_spec(K_COL), _head_spec(V_COL), tab, tab, tab],
        out_specs=[out, outp, outp],
        out_shape=[jax.ShapeDtypeStruct((S, 3072), BF16)] + [jax.ShapeDtypeStruct((S + BLK, 3072), BF16)] * 2,
        scratch_shapes=[pltpu.VMEM((S, HEAD_DIM), F32)],
        compiler_params=_params(("parallel", "arbitrary"), VMEM_BIG), name="qkv_prep",
    )(z0, z0, z0, *tabs)


def _attn_group_fwd(dil, qd, kdp, vdp, od, ld, og, lg):
    nb = S // dil // BLK
    for u0 in range(0, NUNITS, CU):
        _, s_own, s_prev = _chunk_scores(u0, nb, qd, kdp)
        m = jnp.max(s_own, axis=2, keepdims=True)
        if s_prev is not None:
            m = jnp.maximum(m, jnp.max(s_prev, axis=2, keepdims=True))
        p_own = jnp.exp(s_own - m)
        den = jnp.sum(p_own, axis=2, keepdims=True)
        acc = _dot(p_own.astype(BF16), _blocks(vdp, u0 + 1), B_PV)
        if s_prev is not None:
            p_prev = jnp.exp(s_prev - m)
            den = den + jnp.sum(p_prev, axis=2, keepdims=True)
            acc = acc + _dot(p_prev.astype(BF16), _blocks(vdp, u0), B_PV)
        rows = slice(u0 * BLK, (u0 + CU) * BLK)
        od[rows, :] = (acc / den).reshape(CU * BLK, HEAD_DIM)
        ld[rows, :] = jnp.broadcast_to(m + jnp.log(den), (CU, BLK, HEAD_DIM)).reshape(CU * BLK, HEAD_DIM)
    _interleave(og, od, dil)
    _interleave(lg, ld, dil)


def _group_weights(lgs):
    l0, l1, l2 = lgs[0][...], lgs[1][...], lgs[2][...]
    mx = jnp.maximum(l0, jnp.maximum(l1, l2))
    e0, e1, e2 = jnp.exp(l0 - mx), jnp.exp(l1 - mx), jnp.exp(l2 - mx)
    den = e0 + e1 + e2
    return e0 / den, e1 / den, e2 / den


def _head_spec(base):
    return pl.BlockSpec((S, HEAD_DIM), lambda h, p: (0, base + (p % 3) * 8 + h))


def _slab(dtype=F32, rows=S):
    return pltpu.VMEM((rows, HEAD_DIM), dtype)


def _attn_fwd(z0, tabs, ycat):
    def body(q_ref, k_ref, v_ref, c_ref, a_ref, b_ref, gate_ref, ycat_ref, out_ref, og_ref, lg_ref,
             qo_ref, ko_ref, vo_ref, tmp, od, ld, og0, og1, og2, lg0, lg1, lg2):
        del ycat_ref
        p = pl.program_id(1)
        ogs, lgs = (og0, og1, og2), (lg0, lg1, lg2)
        ko_ref[0:BLK, :] = jnp.zeros((BLK, HEAD_DIM), BF16)
        vo_ref[0:BLK, :] = jnp.zeros((BLK, HEAD_DIM), BF16)
        for gi, (_, dil) in enumerate(PATTERNS):
            @pl.when(p == gi)
            def _(gi=gi, dil=dil):
                c, a, b = c_ref[...], a_ref[...], b_ref[...]
                tmp[...] = _rope(q_ref[...], c, a, b)
                _deinterleave(qo_ref, tmp, dil, BF16)
                tmp[...] = _rope(k_ref[...], c, a, b)
                _deinterleave(ko_ref, tmp, dil, BF16, BLK)
                _deinterleave(vo_ref, v_ref, dil, BF16, BLK)
                _attn_group_fwd(dil, qo_ref, ko_ref, vo_ref, od, ld, ogs[gi], lgs[gi])
                og_ref[...] = ogs[gi][...]
                lg_ref[...] = lgs[gi][...]

        @pl.when(p == 2)
        def _():
            w0, w1, w2 = _group_weights(lgs)
            o = w0 * og0[...] + w1 * og1[...] + w2 * og2[...]
            gate = gate_ref[...]
            out_ref[...] = (o * (gate * _sigmoid(gate))).astype(BF16)

    grp = pl.BlockSpec((S, HEAD_DIM), lambda h, p: (0, p * 8 + h))
    grp_pad = pl.BlockSpec((S + BLK, HEAD_DIM), lambda h, p: (0, p * 8 + h))
    tab = pl.BlockSpec((S, HEAD_DIM), lambda h, p: (0, 0))
    outs = pl.pallas_call(
        body, grid=(8, 3),
        in_specs=[_head_spec(Q_COL), _head_spec(K_COL), _head_spec(V_COL), tab, tab, tab,
                  pl.BlockSpec((S, HEAD_DIM), lambda h, p: (0, BG_COL + h)), ANY_SPEC],
        out_specs=[pl.BlockSpec((S, HEAD_DIM), lambda h, p: (0, 8 + h)), grp, grp, grp, grp_pad, grp_pad],
        out_shape=[jax.ShapeDtypeStruct((S, 2048), BF16), jax.ShapeDtypeStruct((S, 3072), F32),
                   jax.ShapeDtypeStruct((S, 3072), F32), jax.ShapeDtypeStruct((S, 3072), BF16),
                   jax.ShapeDtypeStruct((S + BLK, 3072), BF16), jax.ShapeDtypeStruct((S + BLK, 3072), BF16)],
        scratch_shapes=[_slab() for _ in range(9)],
        input_output_aliases={7: 0},
        compiler_params=_params(("parallel", "arbitrary"), VMEM_BIG), name="attn_fwd",
    )(z0, z0, z0, *tabs, z0, ycat)
    return outs[0], outs[1], outs[2], tuple(outs[3:])


def _attn_bwd(z0, qkv, og, lg, dycat, tabs, dep):
    def body(q_ref, k_ref, v_ref, gate_ref, dy_ref, c_ref, a_ref, b_ref,
             og0_ref, og1_ref, og2_ref, lg0_ref, lg1_ref, lg2_ref, dep_ref,
             dq_ref, dk_ref, dv_ref, dbg_ref,
             tmp, ld, dg0, dg1, dg2, cg0, cg1, cg2, dod, cd, dqd, dkd, dvd):
        kd, vd = k_ref, v_ref
        p = pl.program_id(1)
        ogs, lgs, dgs, cgs = (og0_ref, og1_ref, og2_ref), (lg0_ref, lg1_ref, lg2_ref), (dg0, dg1, dg2), (cg0, cg1, cg2)

        @pl.when(p == 0)
        def _():
            w = _group_weights(lgs)
            o = w[0] * ogs[0][...] + w[1] * ogs[1][...] + w[2] * ogs[2][...]
            silu, dsilu = _silu_and_grad(gate_ref[...])
            dy = dy_ref[...]
            dbg_ref[...] = (dy * o * dsilu).astype(BF16)
            do = dy * silu
            dwbar = jnp.sum(do * o, axis=1, keepdims=True)
            for gi in range(3):
                dgs[gi][...] = w[gi] * do
                cgs[gi][...] = -w[gi] * dwbar

        for gi, (_, dil) in enumerate(PATTERNS):
            @pl.when(p == 1 + gi)
            def _(gi=gi, dil=dil):
                nb = S // dil // BLK
                qd = q_ref
                c, a, b = c_ref[...], a_ref[...], b_ref[...]
                _deinterleave(dod, dgs[gi], dil, BF16)
                _deinterleave(ld, lgs[gi], dil)
                _deinterleave(cd, cgs[gi], dil)
                dkd[...] = jnp.zeros_like(dkd)
                dvd[...] = jnp.zeros_like(dvd)
                flat = lambda t: t.reshape(CU * BLK, HEAD_DIM)
                for u0 in range(0, NUNITS, CU):
                    q, s_own, s_prev = _chunk_scores(u0, nb, qd, kd)
                    lse, cv, do = _blocks(ld, u0), _blocks(cd, u0), _blocks(dod, u0)
                    own = slice((u0 + 1) * BLK, (u0 + 1 + CU) * BLK)
                    p_own = jnp.exp(s_own - lse)
                    ds_own = (p_own * (_dot(do, _blocks(vd, u0 + 1), B_QK) + cv) * SCALE).astype(BF16)
                    dq = _dot(ds_own, _blocks(kd, u0 + 1), B_PV)
                    dkd[own, :] += flat(_dot(ds_own, q, B_TN))
                    dvd[own, :] += flat(_dot(p_own.astype(BF16), do, B_TN))
                    if s_prev is not None:
                        prev = slice(u0 * BLK, (u0 + CU) * BLK)
                        p_prev = jnp.exp(s_prev - lse)
                        ds_prev = (p_prev * (_dot(do, _blocks(vd, u0), B_QK) + cv) * SCALE).astype(BF16)
                        dq = dq + _dot(ds_prev, _blocks(kd, u0), B_PV)
                        dkd[prev, :] += flat(_dot(ds_prev, q, B_TN))
                        dvd[prev, :] += flat(_dot(p_prev.astype(BF16), do, B_TN))
                    dqd[u0 * BLK:(u0 + CU) * BLK, :] = flat(dq)
                _interleave(tmp, dqd, dil)
                dq_ref[...] = _rope_t(tmp[...], c, a, b).astype(BF16)
                _interleave(tmp, dkd, dil, BLK)
                dk_ref[...] = _rope_t(tmp[...], c, a, b).astype(BF16)
                _interleave(tmp, dvd, dil, BLK)
                dv_ref[...] = tmp[...].astype(BF16)

    tab = pl.BlockSpec((S, HEAD_DIM), lambda h, p: (0, 0))
    hspec = lambda base: pl.BlockSpec((S, HEAD_DIM), lambda h, p: (0, base + h))
    gspec = pl.BlockSpec((S, HEAD_DIM), lambda h, p: (0, jnp.maximum(p - 1, 0) * 8 + h))
    gspec_pad = pl.BlockSpec((S + BLK, HEAD_DIM), lambda h, p: (0, jnp.maximum(p - 1, 0) * 8 + h))
    return pl.pallas_call(
        body, grid=(8, 4),
        in_specs=[gspec, gspec_pad, gspec_pad, hspec(BG_COL), hspec(8), tab, tab, tab,
                  hspec(0), hspec(8), hspec(16), hspec(0), hspec(8), hspec(16), ANY_SPEC],
        out_specs=[gspec, gspec, gspec, hspec(0)],
        out_shape=[jax.ShapeDtypeStruct((S, 3072), BF16)] * 3 + [jax.ShapeDtypeStruct((S, HALF), BF16)],
        scratch_shapes=[_slab(), _slab()] + [_slab() for _ in range(6)]
                       + [_slab(BF16), _slab(), _slab(), _slab(F32, S + BLK), _slab(F32, S + BLK)],
        compiler_params=_params(("parallel", "arbitrary"), VMEM_BIG), name="attn_bwd",
    )(*qkv, z0, dycat, *tabs, og, og, og, lg, lg, lg, dep)


SGU_CH = 256
NCHUNK = TR // 128


def _ln_stats(x):
    mu = jnp.mean(x, axis=-1, keepdims=True)
    xc = x - mu
    r = lax.rsqrt(jnp.mean(xc * xc, axis=-1, keepdims=True) + EPS)
    return xc * r, r


def _ln_bwd(dy, xhat, r, g):
    dxh = dy * g
    return r * (dxh - jnp.mean(dxh, axis=-1, keepdims=True) - xhat * jnp.mean(dxh * xhat, axis=-1, keepdims=True))


def _tril_bf16(w):
    row = lax.broadcasted_iota(jnp.int32, w.shape, 0)
    col = lax.broadcasted_iota(jnp.int32, w.shape, 1)
    return jnp.where(row >= col, w, 0.0).astype(BF16)


def _sgu_gate(vn_s, s_s, w_ref, bb_ref):
    for h in range(4):
        wm = _tril_bf16(w_ref[h])
        bias = bb_ref[h]
        for ch in range(NCHUNK):
            rows, cols = slice(ch * 128, (ch + 1) * 128), slice(h * SGU_CH, (h + 1) * SGU_CH)
            s_s[rows, cols] = _dot(wm, vn_s[rows, cols]) + jnp.concatenate([bias, bias], axis=1)


WIN = HALO + TR
SUBL = 8


def _shifted_copies(dst, src):
    dst[0] = src[...]
    for b in range(1, SUBL):
        dst[b, 0:WIN - SUBL, :] = src[pl.ds(b, WIN - SUBL), :]


def _rows_at(copies, off, n):
    return copies[off % SUBL, pl.ds(off - off % SUBL, n), :]


def _conv_fwd(i, dval_ref, dglu_ref, hval_ref, hglu_ref, cw_ref, cb_ref, xw, xr, dcs):
    halo = hval_ref[...] * _sigmoid(hglu_ref[...])
    xw[0:HALO, :] = jnp.where(i > 0, halo, 0.0)
    xw[HALO:HALO + TR, :] = dval_ref[...] * _sigmoid(dglu_ref[...])
    _shifted_copies(xr, xw)
    sub = 2 * SUB
    for rb in range(TR // sub):
        acc = jnp.broadcast_to(cb_ref[...], (sub, HALF))
        for k in range(CONV_K):
            acc = acc + cw_ref[k:k + 1, :] * _rows_at(xr, rb * sub + HALO - (CONV_K - 1) + k, sub)
        dcs[rb * sub:(rb + 1) * sub, :] = acc


def _odd_in_specs():
    col = lambda j: pl.BlockSpec((TR, HALF), lambda i, *_: (i, j))
    prev = lambda j: pl.BlockSpec((HALO, HALF), lambda i, *_: (jnp.maximum(i * (TR // HALO) - 1, 0), j))
    return [col(0), col(1), col(2), col(3), col(4), col(5), prev(3), prev(4)]


def _full_spec(shape):
    return pl.BlockSpec(shape, lambda i, *_: (0,) * len(shape))


def _odd_fwd(z1, sgu_g, sgu_b, sgu_w, sgu_bb, conv_w, conv_b, cn_g, cn_b):
    def body(u_ref, v_ref, cg_ref, dval_ref, dglu_ref, dgate_ref, hval_ref, hglu_ref,
             g_ref, b_ref, w_ref, bb_ref, cw_ref, cb_ref, cng_ref, cnb_ref, out_ref, dcs, vn_s, s_s, xw, xr):
        i = pl.program_id(0)
        vhat, _ = _ln_stats(v_ref[...])
        vn_s[...] = (vhat * g_ref[...] + b_ref[...]).astype(BF16)
        _sgu_gate(vn_s, s_s, w_ref, bb_ref)
        cg = cg_ref[...]
        out_ref[:, 0:HALF] = (u_ref[...] * s_s[...] * (cg * _sigmoid(cg))).astype(BF16)
        _conv_fwd(i, dval_ref, dglu_ref, hval_ref, hglu_ref, cw_ref, cb_ref, xw, xr, dcs)
        dhat, _ = _ln_stats(dcs[...])
        dn = dhat * cng_ref[...] + cnb_ref[...]
        dgate = dgate_ref[...]
        out_ref[:, HALF:2 * HALF] = ((dn * _sigmoid(dn)) * (dgate * _sigmoid(dgate))).astype(BF16)

    vec = _full_spec((1, HALF))
    return pl.pallas_call(
        body, grid=(S // TR,),
        in_specs=_odd_in_specs() + [vec, vec, _full_spec((4, 128, 128)), _full_spec((4, 128, 128)),
                                    _full_spec((HALO, HALF)), vec, vec, vec],
        out_specs=[pl.BlockSpec((TR, 2048), lambda i: (i, 0)), pl.BlockSpec((TR, HALF), lambda i: (i, 0))],
        out_shape=[jax.ShapeDtypeStruct((S, 2048), BF16), jax.ShapeDtypeStruct((S, HALF), F32)],
        scratch_shapes=[pltpu.VMEM((TR, HALF), BF16), pltpu.VMEM((TR, HALF), F32),
                        pltpu.VMEM((WIN, HALF), F32), pltpu.VMEM((SUBL, WIN, HALF), F32)],
        compiler_params=_params(("parallel",), VMEM_BIG), name="odd_fwd",
    )(z1, z1, z1, z1, z1, z1, z1, z1, sgu_g, sgu_b, sgu_w, sgu_bb, conv_w, conv_b, cn_g, cn_b)


def _odd_bwd_a(z1, dc, dycat, sgu_g, sgu_b, sgu_w, sgu_bb, cn_g, cn_b):
    def body(u_ref, v_ref, cg_ref, dgate_ref, dcs, dy_ref, g_ref, b_ref, w_ref, bb_ref, cng_ref, cnb_ref,
             dz_ref, ddc_ref, dw_ref, dbb_ref, dg_ref, db_ref, dcng_ref, dcnb_ref, dcb_ref,
             vn_s, s_s, ds_s, dvn_s):
        i = pl.program_id(0)
        vhat, rv = _ln_stats(v_ref[...])
        g = g_ref[...]
        vn_s[...] = (vhat * g + b_ref[...]).astype(BF16)
        _sgu_gate(vn_s, s_s, w_ref, bb_ref)
        silu_c, dsilu_c = _silu_and_grad(cg_ref[...])
        dyc = dy_ref[:, 0:HALF]
        u = u_ref[...]
        s = s_s[...]
        dz_ref[:, 0:HALF] = (dyc * s * silu_c).astype(BF16)
        dz_ref[:, 2 * HALF:3 * HALF] = (dyc * u * s * dsilu_c).astype(BF16)
        ds_s[...] = dyc * u * silu_c

        @pl.when(i == 0)
        def _():
            dw_ref[...] = jnp.zeros_like(dw_ref)
            dbb_ref[...] = jnp.zeros_like(dbb_ref)

        tril = lax.broadcasted_iota(jnp.int32, (128, 128), 0) >= lax.broadcasted_iota(jnp.int32, (128, 128), 1)
        for h in range(4):
            wm = _tril_bf16(w_ref[h])
            for ch in range(NCHUNK):
                rows, cols = slice(ch * 128, (ch + 1) * 128), slice(h * SGU_CH, (h + 1) * SGU_CH)
                ds = ds_s[rows, cols]
                dsb = ds.astype(BF16)
                dw_ref[h] += jnp.where(tril, _dot(dsb, vn_s[rows, cols], NT), 0.0)
                dbb_ref[h] += jnp.broadcast_to(jnp.sum(ds, axis=1, keepdims=True), (128, 128))
                dvn_s[rows, cols] = _dot(wm, dsb, TN)
        dvn = dvn_s[...]
        _acc_rows(dg_ref, dvn * vhat, i)
        _acc_rows(db_ref, dvn, i)
        dz_ref[:, HALF:2 * HALF] = _ln_bwd(dvn, vhat, rv, g).astype(BF16)

        dhat, rd = _ln_stats(dcs[...])
        cng = cng_ref[...]
        silu_n, dsilu_n = _silu_and_grad(dhat * cng + cnb_ref[...])
        silu_g, dsilu_g = _silu_and_grad(dgate_ref[...])
        dyd = dy_ref[:, HALF:2 * HALF]
        dz_ref[:, 5 * HALF:6 * HALF] = (dyd * silu_n * dsilu_g).astype(BF16)
        ddn = dyd * silu_g * dsilu_n
        _acc_rows(dcng_ref, ddn * dhat, i)
        _acc_rows(dcnb_ref, ddn, i)
        ddc = _ln_bwd(ddn, dhat, rd, cng)
        ddc_ref[...] = ddc
        _acc_rows(dcb_ref, ddc, i)

    vec = _full_spec((1, HALF))
    sq = _full_spec((4, 128, 128))
    col = lambda j: pl.BlockSpec((TR, HALF), lambda i: (i, j))
    return pl.pallas_call(
        body, grid=(S // TR,),
        in_specs=[col(0), col(1), col(2), col(5), col(0), pl.BlockSpec((TR, 2048), lambda i: (i, 0)),
                  vec, vec, sq, sq, vec, vec],
        out_specs=[pl.BlockSpec((TR, ODD_IN), lambda i: (i, 0)), pl.BlockSpec((TR, HALF), lambda i: (i, 0)),
                   sq, sq, vec, vec, vec, vec, vec],
        out_shape=[jax.ShapeDtypeStruct((S, ODD_IN), BF16), jax.ShapeDtypeStruct((S, HALF), F32),
                   jax.ShapeDtypeStruct((4, 128, 128), F32), jax.ShapeDtypeStruct((4, 128, 128), F32)]
                  + [jax.ShapeDtypeStruct((1, HALF), F32)] * 5,
        scratch_shapes=[pltpu.VMEM((TR, HALF), BF16), pltpu.VMEM((TR, HALF), F32),
                        pltpu.VMEM((TR, HALF), F32), pltpu.VMEM((TR, HALF), F32)],
        compiler_params=_params(("arbitrary",), VMEM_BIG), name="odd_bwd_a",
    )(z1, z1, z1, z1, dc, dycat, sgu_g, sgu_b, sgu_w, sgu_bb, cn_g, cn_b)


def _odd_bwd_b(z1, ddc, dz1, conv_w):
    nt = S // TR

    def body(dval_ref, dglu_ref, hval_ref, hglu_ref, ddc_ref, hddc_ref, cw_ref, dz_in_ref,
             dz_ref, dcw_ref, xw, dwin, dxs, xr, dr):
        del dz_in_ref
        i, j = pl.program_id(0), pl.program_id(1)
        sg = _sigmoid(dglu_ref[...])
        dval = dval_ref[...]

        @pl.when(j == 0)
        def _():
            halo = hval_ref[...] * _sigmoid(hglu_ref[...])
            xw[0:HALO, :] = jnp.where(i > 0, halo, 0.0)
            xw[HALO:HALO + TR, :] = dval * sg
            dwin[0:TR, :] = ddc_ref[...]
            dwin[TR:TR + HALO, :] = jnp.where(i < nt - 1, hddc_ref[...], 0.0)
            _shifted_copies(xr, xw)
            _shifted_copies(dr, dwin)

            @pl.when(i == 0)
            def _():
                dcw_ref[...] = jnp.zeros_like(dcw_ref)

            for rb in range(TR // SUB):
                acc = jnp.zeros((SUB, HALF), F32)
                for k in range(CONV_K):
                    acc = acc + cw_ref[k:k + 1, :] * _rows_at(dr, rb * SUB + (CONV_K - 1) - k, SUB)
                dxs[rb * SUB:(rb + 1) * SUB, :] = acc
            for k in range(CONV_K):
                acc = jnp.zeros((SUB, HALF), F32)
                for rb in range(TR // SUB):
                    acc = acc + dwin[rb * SUB:(rb + 1) * SUB, :] * _rows_at(xr, rb * SUB + HALO - (CONV_K - 1) + k, SUB)
                dcw_ref[k:k + 1, :] += jnp.sum(acc, axis=0, keepdims=True)
            dz_ref[...] = (dxs[...] * sg).astype(BF16)

        @pl.when(j == 1)
        def _():
            dz_ref[...] = (dxs[...] * dval * sg * (1.0 - sg)).astype(BF16)

    col = lambda c: pl.BlockSpec((TR, HALF), lambda i, j: (i, c))
    prev = lambda c: pl.BlockSpec((HALO, HALF), lambda i, j: (jnp.maximum(i * (TR // HALO) - 1, 0), c))
    nxt = pl.BlockSpec((HALO, HALF), lambda i, j: (jnp.minimum((i + 1) * (TR // HALO), S // HALO - 1), 0))
    return pl.pallas_call(
        body, grid=(nt, 2),
        in_specs=[col(3), col(4), prev(3), prev(4), pl.BlockSpec((TR, HALF), lambda i, j: (i, 0)), nxt,
                  _full_spec((HALO, HALF)), pl.BlockSpec(memory_space=pl.ANY)],
        out_specs=[pl.BlockSpec((TR, HALF), lambda i, j: (i, 3 + j)), _full_spec((HALO, HALF))],
        out_shape=[jax.ShapeDtypeStruct((S, ODD_IN), BF16), jax.ShapeDtypeStruct((HALO, HALF), F32)],
        scratch_shapes=[pltpu.VMEM((WIN, HALF), F32), pltpu.VMEM((WIN, HALF), F32), pltpu.VMEM((TR, HALF), F32),
                        pltpu.VMEM((SUBL, WIN, HALF), F32), pltpu.VMEM((SUBL, WIN, HALF), F32)],
        input_output_aliases={7: 0},
        compiler_params=_params(("arbitrary", "arbitrary"), VMEM_BIG), name="odd_bwd_b",
    )(z1, z1, z1, z1, ddc, ddc, conv_w, dz1)


def _conv_out_grad(dc, dyd, dgate, cng, cnb):
    dhat, rd = _ln_stats(dc)
    silu_n, dsilu_n = _silu_and_grad(dhat * cng + cnb)
    silu_g, dsilu_g = _silu_and_grad(dgate)
    ddn = dyd * silu_g * dsilu_n
    return _ln_bwd(ddn, dhat, rd, cng), ddn, dhat, dyd * silu_n * dsilu_g


def _odd_bwd(z1, dc, dycat, sgu_g, sgu_b, sgu_w, sgu_bb, conv_w, cn_g, cn_b):
    nt = S // TR

    def body(u_ref, v_ref, cg_ref, dval_ref, dglu_ref, dgate_ref, hval_ref, hglu_ref, dcs, dy_ref,
             ndc_ref, ndy_ref, ngate_ref, g_ref, b_ref, w_ref, bb_ref, cw_ref, cng_ref, cnb_ref,
             dz_ref, dw_ref, dbb_ref, dg_ref, db_ref, dcng_ref, dcnb_ref, dcb_ref, dcw_ref,
             vn_s, s_s, ds_s, dvn_s, xw, dwin, dxs, xr, dr):
        i = pl.program_id(0)
        vhat, rv = _ln_stats(v_ref[...])
        g = g_ref[...]
        vn_s[...] = (vhat * g + b_ref[...]).astype(BF16)
        _sgu_gate(vn_s, s_s, w_ref, bb_ref)
        silu_c, dsilu_c = _silu_and_grad(cg_ref[...])
        dyc = dy_ref[:, 0:HALF]
        u = u_ref[...]
        s = s_s[...]
        dz_ref[:, 0:HALF] = (dyc * s * silu_c).astype(BF16)
        dz_ref[:, 2 * HALF:3 * HALF] = (dyc * u * s * dsilu_c).astype(BF16)
        ds_s[...] = dyc * u * silu_c

        @pl.when(i == 0)
        def _():
            dw_ref[...] = jnp.zeros_like(dw_ref)
            dbb_ref[...] = jnp.zeros_like(dbb_ref)
            dcw_ref[...] = jnp.zeros_like(dcw_ref)

        tril = lax.broadcasted_iota(jnp.int32, (128, 128), 0) >= lax.broadcasted_iota(jnp.int32, (128, 128), 1)
        for h in range(4):
            wm = _tril_bf16(w_ref[h])
            for ch in range(NCHUNK):
                rows, cols = slice(ch * 128, (ch + 1) * 128), slice(h * SGU_CH, (h + 1) * SGU_CH)
                ds = ds_s[rows, cols]
                dsb = ds.astype(BF16)
                dw_ref[h] += jnp.where(tril, _dot(dsb, vn_s[rows, cols], NT), 0.0)
                dbb_ref[h] += jnp.broadcast_to(jnp.sum(ds, axis=1, keepdims=True), (128, 128))
                dvn_s[rows, cols] = _dot(wm, dsb, TN)
        dvn = dvn_s[...]
        _acc_rows(dg_ref, dvn * vhat, i)
        _acc_rows(db_ref, dvn, i)
        dz_ref[:, HALF:2 * HALF] = _ln_bwd(dvn, vhat, rv, g).astype(BF16)

        cng, cnb = cng_ref[...], cnb_ref[...]
        ddc, ddn, dhat, ddgate = _conv_out_grad(dcs[...], dy_ref[:, HALF:2 * HALF], dgate_ref[...], cng, cnb)
        dz_ref[:, 5 * HALF:6 * HALF] = ddgate.astype(BF16)
        _acc_rows(dcng_ref, ddn * dhat, i)
        _acc_rows(dcnb_ref, ddn, i)
        _acc_rows(dcb_ref, ddc, i)
        ddc_next = _conv_out_grad(ndc_ref[...], ndy_ref[:, HALF:2 * HALF], ngate_ref[...], cng, cnb)[0]

        sg = _sigmoid(dglu_ref[...])
        dval = dval_ref[...]
        halo = hval_ref[...] * _sigmoid(hglu_ref[...])
        xw[0:HALO, :] = jnp.where(i > 0, halo, 0.0)
        xw[HALO:HALO + TR, :] = dval * sg
        dwin[0:TR, :] = ddc
        dwin[TR:TR + HALO, :] = jnp.where(i < nt - 1, ddc_next, 0.0)
        _shifted_copies(xr, xw)
        _shifted_copies(dr, dwin)
        for rb in range(TR // SUB):
            acc = jnp.zeros((SUB, HALF), F32)
            for k in range(CONV_K):
                acc = acc + cw_ref[k:k + 1, :] * _rows_at(dr, rb * SUB + (CONV_K - 1) - k, SUB)
            dxs[rb * SUB:(rb + 1) * SUB, :] = acc
        for k in range(CONV_K):
            acc = jnp.zeros((SUB, HALF), F32)
            for rb in range(TR // SUB):
                acc = acc + dwin[rb * SUB:(rb + 1) * SUB, :] * _rows_at(xr, rb * SUB + HALO - (CONV_K - 1) + k, SUB)
            dcw_ref[k:k + 1, :] += jnp.sum(acc, axis=0, keepdims=True)
        dx = dxs[...]
        dz_ref[:, 3 * HALF:4 * HALF] = (dx * sg).astype(BF16)
        dz_ref[:, 4 * HALF:5 * HALF] = (dx * dval * sg * (1.0 - sg)).astype(BF16)

    vec = _full_spec((1, HALF))
    sq = _full_spec((4, 128, 128))
    col = lambda j: pl.BlockSpec((TR, HALF), lambda i: (i, j))
    prev = lambda j: pl.BlockSpec((HALO, HALF), lambda i: (jnp.maximum(i * (TR // HALO) - 1, 0), j))
    nxt_row = lambda i: jnp.minimum((i + 1) * (TR // HALO), S // HALO - 1)
    return pl.pallas_call(
        body, grid=(nt,),
        in_specs=[col(0), col(1), col(2), col(3), col(4), col(5), prev(3), prev(4), col(0),
                  pl.BlockSpec((TR, 2048), lambda i: (i, 0)),
                  pl.BlockSpec((HALO, HALF), lambda i: (nxt_row(i), 0)), pl.BlockSpec((HALO, 2048), lambda i: (nxt_row(i), 0)),
                  pl.BlockSpec((HALO, HALF), lambda i: (nxt_row(i), 5)),
                  vec, vec, sq, sq, _full_spec((HALO, HALF)), vec, vec],
        out_specs=[pl.BlockSpec((TR, ODD_IN), lambda i: (i, 0)), sq, sq, vec, vec, vec, vec, vec,
                   _full_spec((HALO, HALF))],
        out_shape=[jax.ShapeDtypeStruct((S, ODD_IN), BF16), jax.ShapeDtypeStruct((4, 128, 128), F32),
                   jax.ShapeDtypeStruct((4, 128, 128), F32)] + [jax.ShapeDtypeStruct((1, HALF), F32)] * 5
                  + [jax.ShapeDtypeStruct((HALO, HALF), F32)],
        scratch_shapes=[pltpu.VMEM((TR, HALF), BF16), pltpu.VMEM((TR, HALF), F32), pltpu.VMEM((TR, HALF), F32),
                        pltpu.VMEM((TR, HALF), F32), pltpu.VMEM((WIN, HALF), F32), pltpu.VMEM((WIN, HALF), F32),
                        pltpu.VMEM((TR, HALF), F32), pltpu.VMEM((SUBL, WIN, HALF), F32),
                        pltpu.VMEM((SUBL, WIN, HALF), F32)],
        compiler_params=_params(("arbitrary",), VMEM_BIG), name="odd_bwd",
    )(z1, z1, z1, z1, z1, z1, z1, z1, dc, dycat, dc, dycat, z1,
      sgu_g, sgu_b, sgu_w, sgu_bb, conv_w, cn_g, cn_b)


def _join_columns(parts, name):
    widths = [a.shape[1] for a in parts]
    offs = [sum(widths[:i]) for i in range(len(parts))]

    def body(*refs):
        o_ref = refs[-1]
        for a_ref, off, n in zip(refs[:-1], offs, widths):
            o_ref[:, off:off + n] = a_ref[...]

    return pl.pallas_call(
        body, grid=(S // TR,),
        in_specs=[pl.BlockSpec((TR, n), lambda i: (i, 0)) for n in widths],
        out_specs=pl.BlockSpec((TR, sum(widths)), lambda i: (i, 0)),
        out_shape=jax.ShapeDtypeStruct((S, sum(widths)), parts[0].dtype),
        compiler_params=_params(("parallel",), VMEM_BIG), name=name,
    )(*parts)


def _cast_bf16(w, name, piece=0, npieces=1):
    r, c = w.shape[0], w.shape[1] // npieces
    tr = min(r, 256)

    def body(i_ref, o_ref):
        o_ref[...] = i_ref[...].astype(BF16)

    return pl.pallas_call(
        body, grid=(r // tr,), in_specs=[pl.BlockSpec((tr, c), lambda i: (i, piece))],
        out_specs=pl.BlockSpec((tr, c), lambda i: (i, 0)), out_shape=jax.ShapeDtypeStruct((r, c), BF16),
        compiler_params=_params(("parallel",)), name=name,
    )(w)


def _adamw(w, g, m, v):
    m = ADAM_B1 * m + (1.0 - ADAM_B1) * g
    v = ADAM_B2 * v + (1.0 - ADAM_B2) * (g * g)
    m_hat = m / (1.0 - ADAM_B1 ** ADAM_STEP)
    v_hat = v / (1.0 - ADAM_B2 ** ADAM_STEP)
    delta = -ADAM_LR * (m_hat / (jnp.sqrt(v_hat) + ADAM_EPS) + ADAM_WD * w)
    return delta, m, v


def _adam_reduce(parts, w, m, v, name, dep=None, piece=0, npieces=1, prev=None):
    r, c = w.shape
    cp = c // npieces
    tr = min(r, 128)
    extra = ([] if dep is None else [dep]) + ([] if prev is None else list(prev))
    nparts = parts.shape[0]

    def body(p_ref, w_ref, m_ref, v_ref, *rest):
        g_ref, d_ref, nm_ref, nv_ref = rest[len(extra):]
        g = p_ref[0].astype(F32)
        for d in range(1, nparts):
            g = g + p_ref[d].astype(F32)
        g_ref[...] = g
        d_ref[...], nm_ref[...], nv_ref[...] = _adamw(w_ref[...], g, m_ref[...], v_ref[...])

    spec = pl.BlockSpec((tr, cp), lambda i: (i, piece))
    first = 4 + (0 if dep is None else 1)
    return pl.pallas_call(
        body, grid=(r // tr,),
        in_specs=[pl.BlockSpec((nparts, tr, cp), lambda i: (0, i, 0)), spec, spec, spec] + [ANY_SPEC] * len(extra),
        out_specs=[spec] * 4, out_shape=[jax.ShapeDtypeStruct((r, c), F32)] * 4,
        input_output_aliases={} if prev is None else {first + k: k for k in range(4)},
        compiler_params=_params(("parallel",), VMEM_BIG), name=name,
    )(parts, w, m, v, *extra)


def _arrived(x, name, dep=None):
    deps = [] if dep is None else [dep]

    def body(*refs):
        refs[-1][...] = jnp.zeros_like(refs[-1])

    return pl.pallas_call(
        body, in_specs=[ANY_SPEC] * (1 + len(deps)), out_specs=pl.BlockSpec(memory_space=pltpu.VMEM),
        out_shape=jax.ShapeDtypeStruct((8, 128), F32), name=name,
    )(x, *deps)


def _sum_parts(parts, name, dep=None):
    r = parts.shape[1]
    tr = 8
    for cand in (512, 256, 128, 64, 32, 16, 8):
        if r % cand == 0:
            tr = cand
            break
    deps = [] if dep is None else [dep]

    def body(p_ref, *rest):
        g = p_ref[0]
        for d in range(1, NDEV):
            g = g + p_ref[d]
        rest[-1][...] = g

    return pl.pallas_call(
        body, grid=(r // tr,), in_specs=[pl.BlockSpec((NDEV, tr, 128), lambda i: (0, i, 0))] + [ANY_SPEC] * len(deps),
        out_specs=pl.BlockSpec((tr, 128), lambda i: (i, 0)), out_shape=jax.ShapeDtypeStruct((r, 128), F32),
        compiler_params=_params(("parallel",)), name=name,
    )(parts, *deps)


def _sum_unpack(parts, rows, name, dep=None):
    deps = [] if dep is None else [dep]

    def body(p_ref, *outs):
        outs = outs[len(deps):]
        off = 0
        for o_ref, n in zip(outs, rows):
            acc = p_ref[0, off:off + n, :]
            for d in range(1, NDEV):
                acc = acc + p_ref[d, off:off + n, :]
            o_ref[...] = acc
            off += n

    return pl.pallas_call(
        body, grid=(1,), in_specs=[pl.BlockSpec(parts.shape, lambda i: (0, 0, 0))] + [ANY_SPEC] * len(deps),
        out_specs=[pl.BlockSpec((n, 128), lambda i: (0, 0)) for n in rows],
        out_shape=[jax.ShapeDtypeStruct((n, 128), F32) for n in rows],
        compiler_params=_params(("arbitrary",), VMEM_BIG), name=name,
    )(parts, *deps)


def _adam_small(ws, gs, g_specs, ms, vs, name):
    n = len(ws)

    def body(*refs):
        w_r, g_r, m_r, v_r = refs[:n], refs[n:2 * n], refs[2 * n:3 * n], refs[3 * n:4 * n]
        outs = refs[4 * n:]
        for i in range(n):
            g = g_r[i][...]
            outs[4 * i][...] = g
            outs[4 * i + 1][...], outs[4 * i + 2][...], outs[4 * i + 3][...] = _adamw(
                w_r[i][...], g, m_r[i][...], v_r[i][...])

    whole = lambda a: pl.BlockSpec(a.shape, lambda i, nd=a.ndim: (0,) * nd)
    outs = pl.pallas_call(
        body, grid=(1,),
        in_specs=[whole(a) for a in ws] + list(g_specs) + [whole(a) for a in ms] + [whole(a) for a in vs],
        out_specs=[whole(a) for a in ws for _ in range(4)],
        out_shape=[jax.ShapeDtypeStruct(a.shape, F32) for a in ws for _ in range(4)],
        compiler_params=_params(("arbitrary",), VMEM_BIG), name=name,
    )(*ws, *gs, *ms, *vs)
    return [outs[4 * i:4 * i + 4] for i in range(n)]


MASKS = [(mx, my, mc) for mx in (0, 1) for my in (0, 1) for mc in (0, 1)][1:]


def _sc_exchange(name, collective_id, arrays, scatter):
    nt = len(arrays)
    out_type = [jax.ShapeDtypeStruct(a.shape if scatter else (NDEV,) + a.shape, a.dtype) for a in arrays]

    def body(*refs):
        ins, outs = refs[:nt], refs[nt:2 * nt]
        send_sems, recv_sems, local_sems = refs[2 * nt:3 * nt], refs[3 * nt:4 * nt], refs[4 * nt:5 * nt]
        x, y, c = lax.axis_index("x"), lax.axis_index("y"), lax.axis_index("c")
        peers = [(mx + x - 2 * mx * x, my + y - 2 * my * y, mc + c - 2 * mc * c) for mx, my, mc in MASKS]
        barrier = pltpu.get_barrier_semaphore()
        for peer in peers:
            pl.semaphore_signal(barrier, inc=1, device_id=peer, device_id_type=MESH)
        pl.semaphore_wait(barrier, len(peers))
        me = 4 * x + 2 * y + c
        own = []
        for t in range(nt):
            cp = pltpu.make_async_copy(ins[t].at[me] if scatter else ins[t], outs[t].at[me], local_sems[t])
            cp.start()
            own.append(cp)
            for px, py, pc in peers:
                src = ins[t].at[4 * px + 2 * py + pc] if scatter else ins[t]
                pltpu.make_async_remote_copy(src_ref=src, dst_ref=outs[t].at[me], send_sem=send_sems[t],
                                             recv_sem=recv_sems[t], device_id=(px, py, pc), device_id_type=MESH).start()
        for t in range(nt):
            own[t].wait()
            seven = outs[t].at[pl.ds(0, NDEV - 1)]
            drain = pltpu.make_async_remote_copy(src_ref=seven, dst_ref=seven, send_sem=send_sems[t],
                                                 recv_sem=recv_sems[t], device_id=(x, y, c), device_id_type=MESH)
            drain.wait_send()
            drain.wait_recv()

    return pl.kernel(
        body, out_type=out_type, mesh=plsc.ScalarSubcoreMesh(axis_name="sequencer", num_cores=1),
        scratch_types=[pltpu.SemaphoreType.DMA] * (3 * nt),
        compiler_params=pltpu.CompilerParams(collective_id=collective_id), name=name,
    )(*arrays)


def _sc_gather_two_level(name, collective_id, arrays):
    nt = len(arrays)
    out_type = [jax.ShapeDtypeStruct((NDEV,) + a.shape, a.dtype) for a in arrays]

    def body(*refs):
        ins, outs = refs[:nt], refs[nt:2 * nt]
        sems = refs[2 * nt:]
        send_sems, sib_sems, local_sems = sems[:nt], sems[nt:2 * nt], sems[2 * nt:3 * nt]
        ici_sems = [sems[3 * nt + 3 * t:3 * nt + 3 * t + 3] for t in range(nt)]
        x, y, c = lax.axis_index("x"), lax.axis_index("y"), lax.axis_index("c")
        sibling = (x, y, 1 - c)
        chips = [(1 - x, y), (x, 1 - y), (1 - x, 1 - y)]
        barrier = pltpu.get_barrier_semaphore()
        for peer in [sibling] + [(cx, cy, c) for cx, cy in chips]:
            pl.semaphore_signal(barrier, inc=1, device_id=peer, device_id_type=MESH)
        pl.semaphore_wait(barrier, 4)
        me = 4 * x + 2 * y + c

        def push(t, src, slot, recv_sem, to):
            pltpu.make_async_remote_copy(src_ref=src, dst_ref=outs[t].at[slot], send_sem=send_sems[t],
                                         recv_sem=recv_sem, device_id=to, device_id_type=MESH).start()

        own = []
        for t in range(nt):
            cp = pltpu.make_async_copy(ins[t], outs[t].at[me], local_sems[t])
            cp.start()
            own.append(cp)
            for j, (cx, cy) in enumerate(chips):
                push(t, ins[t], me, ici_sems[t][j], (cx, cy, c))
            push(t, ins[t], me, sib_sems[t], sibling)
        for t in range(nt):
            for j, (cx, cy) in enumerate(chips):
                slot = 4 * cx + 2 * cy + c
                landed = outs[t].at[slot]
                pltpu.make_async_remote_copy(src_ref=landed, dst_ref=landed, send_sem=send_sems[t],
                                             recv_sem=ici_sems[t][j], device_id=(cx, cy, c),
                                             device_id_type=MESH).wait_recv()
                push(t, landed, slot, sib_sems[t], sibling)
        for t in range(nt):
            own[t].wait()
            four, seven = outs[t].at[pl.ds(0, 4)], outs[t].at[pl.ds(0, 7)]
            pltpu.make_async_remote_copy(src_ref=four, dst_ref=four, send_sem=send_sems[t], recv_sem=sib_sems[t],
                                         device_id=sibling, device_id_type=MESH).wait_recv()
            pltpu.make_async_remote_copy(src_ref=seven, dst_ref=seven, send_sem=send_sems[t], recv_sem=sib_sems[t],
                                         device_id=sibling, device_id_type=MESH).wait_send()

    return pl.kernel(
        body, out_type=out_type, mesh=plsc.ScalarSubcoreMesh(axis_name="sequencer", num_cores=1),
        scratch_types=[pltpu.SemaphoreType.DMA] * (6 * nt),
        compiler_params=pltpu.CompilerParams(collective_id=collective_id), name=name,
    )(*arrays)


def _sc_sibling_exchange(name, collective_id, src, out_shape, pieces, after=None):
    extra = [] if after is None else [after]

    def body(src_ref, *rest):
        out_ref, send_sem, recv_sem = rest[len(extra):]
        x, y, c = lax.axis_index("x"), lax.axis_index("y"), lax.axis_index("c")
        sibling = (x, y, 1 - c)
        barrier = pltpu.get_barrier_semaphore()
        pl.semaphore_signal(barrier, inc=1, device_id=sibling, device_id_type=MESH)
        pl.semaphore_wait(barrier, 1)
        for piece, lands in pieces(c, src_ref, out_ref):
            pltpu.make_async_remote_copy(src_ref=piece, dst_ref=lands, send_sem=send_sem, recv_sem=recv_sem,
                                         device_id=sibling, device_id_type=MESH).start()
        drain = pltpu.make_async_remote_copy(src_ref=out_ref, dst_ref=out_ref, send_sem=send_sem, recv_sem=recv_sem,
                                             device_id=sibling, device_id_type=MESH)
        drain.wait_send()
        drain.wait_recv()

    return pl.kernel(
        body, out_type=jax.ShapeDtypeStruct(out_shape, src.dtype),
        mesh=plsc.ScalarSubcoreMesh(axis_name="sequencer", num_cores=1), scratch_types=[pltpu.SemaphoreType.DMA] * 2,
        compiler_params=pltpu.CompilerParams(collective_id=collective_id), name=name,
    )(src, *extra)


def _swap_class_columns(name, collective_id, dz, nb, piece=0, npieces=1):
    w = nb // npieces
    return _sc_sibling_exchange(
        name, collective_id, dz, (S, 4 * w),
        lambda c, src, out: [(src.at[:, pl.ds((2 * j + 1 - c) * nb + piece * w, w)], out.at[:, pl.ds(j * w, w)])
                             for j in range(4)])


def _sc_chip_scatter(name, collective_id, q):
    def body(q_ref, out_ref, send_sem, recv_sem, local_sem):
        x, y, c = lax.axis_index("x"), lax.axis_index("y"), lax.axis_index("c")
        chips = [(1 - x, y), (x, 1 - y), (1 - x, 1 - y)]
        barrier = pltpu.get_barrier_semaphore()
        for cx, cy in chips:
            pl.semaphore_signal(barrier, inc=1, device_id=(cx, cy, c), device_id_type=MESH)
        pl.semaphore_wait(barrier, 3)
        mine = 2 * x + y
        own = pltpu.make_async_copy(q_ref.at[mine], out_ref.at[mine], local_sem)
        own.start()
        for cx, cy in chips:
            pltpu.make_async_remote_copy(src_ref=q_ref.at[2 * cx + cy], dst_ref=out_ref.at[mine], send_sem=send_sem,
                                         recv_sem=recv_sem, device_id=(cx, cy, c), device_id_type=MESH).start()
        own.wait()
        three = out_ref.at[pl.ds(0, 3)]
        drain = pltpu.make_async_remote_copy(src_ref=three, dst_ref=three, send_sem=send_sem, recv_sem=recv_sem,
                                             device_id=(x, y, c), device_id_type=MESH)
        drain.wait_send()
        drain.wait_recv()

    return pl.kernel(
        body, out_type=jax.ShapeDtypeStruct(q.shape, q.dtype),
        mesh=plsc.ScalarSubcoreMesh(axis_name="sequencer", num_cores=1), scratch_types=[pltpu.SemaphoreType.DMA] * 3,
        compiler_params=pltpu.CompilerParams(collective_id=collective_id), name=name,
    )(q)


def _mm_pair_dw(h_own, dz, h_sib, dz_sib, nb, name, dep=None, piece=0, npieces=1, h_transposed=False,
                one_call=False):
    nb = nb // npieces
    tn = 512 if nb % 512 == 0 else nb
    per = nb // tn
    dn = NN if h_transposed else TN
    o_spec = pl.BlockSpec((None, D, tn), lambda i, j, k: (j // per, 0, j % per))
    own_col = lambda i, j, k: (0, ((2 * (j // per) + lax.axis_index("c")) * npieces + piece) * per + j % per)
    if one_call:
        def fused(a0_ref, b0_ref, a1_ref, b1_ref, dep_ref, o_ref):
            acc = _dot(a0_ref[...], b0_ref[...], dn) + _dot(a1_ref[...], b1_ref[...], dn)
            o_ref[...] = acc.astype(BF16)

        whole = pl.BlockSpec((S, D), lambda i, j, k: (0, 0), pipeline_mode=pl.Buffered(1))
        return pl.pallas_call(
            fused, grid=(1, 4 * per, 1),
            in_specs=[whole, pl.BlockSpec((S, tn), own_col), whole, pl.BlockSpec((S, tn), lambda i, j, k: (0, j)),
                      ANY_SPEC],
            out_specs=o_spec, out_shape=jax.ShapeDtypeStruct((4, D, nb), BF16),
            compiler_params=_params(("parallel", "parallel", "arbitrary"), VMEM_BIG), name=name,
        )(h_own, dz, h_sib, dz_sib, dep)
    part = _matmul(
        h_own, dz, dn=dn, grid=(1, 4 * per, 1),
        a_spec=pl.BlockSpec((S, D), lambda i, j, k: (0, 0)), b_spec=pl.BlockSpec((S, tn), own_col),
        o_spec=o_spec, out_shape=(4, D, nb), out_dtype=F32, acc_shape=(D, tn), name=name + "_own", dep=dep)

    def body(a_ref, b_ref, p_ref, o_ref):
        o_ref[...] = (p_ref[...] + _dot(a_ref[...], b_ref[...], dn)).astype(BF16)

    return pl.pallas_call(
        body, grid=(1, 4 * per, 1),
        in_specs=[pl.BlockSpec((S, D), lambda i, j, k: (0, 0)), pl.BlockSpec((S, tn), lambda i, j, k: (0, j)), o_spec],
        out_specs=o_spec, out_shape=jax.ShapeDtypeStruct((4, D, nb), BF16),
        compiler_params=_params(("parallel", "parallel", "arbitrary"), VMEM_BIG), name=name + "_sibling",
    )(h_sib, dz_sib, part)


SMALL = {
    "e_pre_norm": ((2048,), None), "e_pool_w": ((4, 256, 256), 1), "e_pool_scale": ((1024,), None),
    "e_post_norm": ((2048,), None), "o_pre_norm": ((2048,), 0), "o_sgu_norm_g": ((1024,), 0),
    "o_sgu_norm_b": ((1024,), 0), "o_sgu_w": ((4, 128, 128), None), "o_sgu_b": ((4, 128), None),
    "o_conv_w": ((31, 1024), 1), "o_conv_b": ((1024,), 0), "o_conv_norm_g": ((1024,), 0),
    "o_conv_norm_b": ((1024,), 0), "o_post_norm": ((2048,), 0),
}
SMALL_SHARDED = [n for n, (_, ax) in SMALL.items() if ax is not None]


def _shard_shape(name):
    shape, ax = SMALL[name]
    if ax is None:
        return shape
    return tuple(s // NDEV if i == ax else s for i, s in enumerate(shape))


def _pack(arrs, row_multiple=1):
    flat = jnp.concatenate([a.reshape(-1) for a in arrs])
    pad = -flat.shape[0] % (128 * row_multiple)
    return jnp.concatenate([flat, jnp.zeros((pad,), F32)]).reshape(-1, 128)


def _small_views(name):
    shape, ax = SMALL[name]
    me = lambda: 4 * lax.axis_index("x") + 2 * lax.axis_index("y") + lax.axis_index("c")
    if ax is None:
        view = (int(np.prod(shape)) // 128, 128)
        return view, view, pl.BlockSpec(view, lambda i: (0, 0))
    if len(shape) == 1:
        n = shape[0] // NDEV
        return (1, n), (NDEV, 1, n), pl.BlockSpec((None, 1, n), lambda i: (me(), 0, 0))
    part = _shard_shape(name)
    return part, shape, pl.BlockSpec(part, lambda i: tuple(me() if d == ax else 0 for d in range(len(shape))))


WEIGHTS = ["e_pre_norm", "e_w_in", "e_pool_w", "e_pool_scale", "e_w_out", "e_post_norm", "o_pre_norm", "o_w_in",
           "o_sgu_norm_g", "o_sgu_norm_b", "o_sgu_w", "o_sgu_b", "o_conv_w", "o_conv_b", "o_conv_norm_g",
           "o_conv_norm_b", "o_w_out", "o_post_norm"]


def kernel(x, e_pre_norm, e_w_in, e_pool_w, e_pool_scale, e_w_out, e_post_norm, o_pre_norm, o_w_in, o_sgu_norm_g, o_sgu_norm_b, o_sgu_w, o_sgu_b, o_conv_w, o_conv_b, o_conv_norm_g, o_conv_norm_b, o_w_out, o_post_norm, loss_target, m_e_pre_norm, m_e_w_in, m_e_pool_w, m_e_pool_scale, m_e_w_out, m_e_post_norm, m_o_pre_norm, m_o_w_in, m_o_sgu_norm_g, m_o_sgu_norm_b, m_o_sgu_w, m_o_sgu_b, m_o_conv_w, m_o_conv_b, m_o_conv_norm_g, m_o_conv_norm_b, m_o_w_out, m_o_post_norm, v_e_pre_norm, v_e_w_in, v_e_pool_w, v_e_pool_scale, v_e_w_out, v_e_post_norm, v_o_pre_norm, v_o_w_in, v_o_sgu_norm_g, v_o_sgu_norm_b, v_o_sgu_w, v_o_sgu_b, v_o_conv_w, v_o_conv_b, v_o_conv_norm_g, v_o_conv_norm_b, v_o_w_out, v_o_post_norm):
    given = dict(locals())
    w = {n: given[n][0] for n in WEIGHTS}
    m = {n: given["m_" + n][0] for n in WEIGHTS}
    v = {n: given["v_" + n][0] for n in WEIGHTS}
    me = 4 * lax.axis_index("x") + 2 * lax.axis_index("y") + lax.axis_index("c")
    x, target = x[0], loss_target[0]
    row = lambda a: a.reshape(1, -1)

    lo, small_rows = _sc_gather_two_level(
        "gather_a0", 0, [_cast_bf16(w["e_w_in"], "cast_e_w_in_0", 0, 2), _pack([w[n] for n in SMALL_SHARDED])])
    hi, = _sc_gather_two_level("gather_a1", 12, [_cast_bf16(w["e_w_in"], "cast_e_w_in_1", 1, 2)])
    wg_e_in = (lo, hi)
    h0, h0t = _pre0_fwd(x, row(w["e_pre_norm"]))
    wg_e_out, = _sc_gather_two_level("gather_b", 1, [_cast_bf16(w["e_w_out"], "cast_e_w_out")])
    wg_o_in, = _sc_gather_two_level("gather_c", 13, [_cast_bf16(w["o_w_in"], "cast_o_w_in")])
    wg_o_out, = _sc_gather_two_level("gather_d", 16, [_cast_bf16(w["o_w_out"], "cast_o_w_out")])
    p = {n: w[n] for n in SMALL if SMALL[n][1] is None}
    small_rows = small_rows.reshape(NDEV, -1)
    off = 0
    for n in SMALL_SHARDED:
        shp, ax = _shard_shape(n), SMALL[n][1]
        cnt = int(np.prod(shp))
        blk = small_rows[:, off:off + cnt].reshape((NDEV,) + shp)
        p[n] = jnp.moveaxis(blk, 0, ax).reshape(SMALL[n][0])
        off += cnt
    tabs = _rope_tables()
    pool_w_bf = p["e_pool_w"].astype(BF16)
    sgu_bb = jnp.broadcast_to(p["o_sgu_b"][:, :, None], (4, 128, 128))
    conv_w = jnp.concatenate([p["o_conv_w"], jnp.zeros((HALO - CONV_K, HALF), F32)], axis=0)
    odd_p = (row(p["o_sgu_norm_g"]), row(p["o_sgu_norm_b"]), p["o_sgu_w"], sgu_bb, conv_w,
             row(p["o_conv_b"]), row(p["o_conv_norm_g"]), row(p["o_conv_norm_b"]))

    z0 = _mm_in_halves(h0, wg_e_in, "mm_z0")
    ycat0 = _pool_fwd(z0, pool_w_bf, row(p["e_pool_scale"]))
    ycat0, og, lg, qkv = _attn_fwd(z0, tabs, ycat0)
    w_out_e, w_out_o = wg_e_out.reshape(2048, D), wg_o_out.reshape(2048, D)
    y0, x1, h1 = _post0_fwd(ycat0, w_out_e, x, row(p["e_post_norm"]), row(p["o_pre_norm"]), wg_e_out)
    h0t_sib = _sc_sibling_exchange("swap_h0", 8, h0t, h0t.shape, lambda c, src, out: [(src, out)], h1)
    h1_sib = _sc_sibling_exchange("swap_h1", 11, h1, h1.shape, lambda c, src, out: [(src, out)])
    z1 = _mm_in(h1, wg_o_in, "mm_z1")
    ycat1, conv_out = _odd_fwd(z1, *odd_p)

    g = {}
    loss_part, dx2, dy1, g["o_post_norm"] = _post1_bwd(ycat1, w_out_o, x1, target, row(p["o_post_norm"]),
                                                       _arrived(h1_sib, "arrived_h_sib", h0t_sib))
    parts = {}
    dw = _mm_out_dw(ycat1, dy1, "mm_dwout1").reshape(NDEV, 256, D)
    parts["o_w_out"], = _sc_exchange("scatter_o_w_out", 2, [dw], True)
    dycat1 = _mm_out_dx(dy1, w_out_o, "mm_dycat1", dw)
    dz1, g["o_sgu_w"], d_sgu_bb, g["o_sgu_norm_g"], g["o_sgu_norm_b"], g["o_conv_norm_g"], g["o_conv_norm_b"], \
        g["o_conv_b"], d_conv_w = _odd_bwd(z1, conv_out, dycat1, *odd_p[:4], conv_w, *odd_p[6:])
    g["o_sgu_b"] = d_sgu_bb[:, :, 0]
    g["o_conv_w"] = d_conv_w[:CONV_K]
    grads, deltas, new_m, new_v = {}, {}, {}, {}

    def adam(n, dep):
        grads[n], deltas[n], new_m[n], new_v[n] = _adam_reduce(parts[n], w[n], m[n], v[n], "adam_" + n, dep)
        return new_v[n]

    pin = _arrived(parts["o_w_out"], "arrived_o_w_out", d_conv_w)
    dz1_sib = _swap_class_columns("swap_dz1", 10, dz1, ODD_IN // NDEV)
    dw = _mm_pair_dw(h1, dz1, h1_sib, dz1_sib, ODD_IN // NDEV, "mm_dwin1", pin)
    parts["o_w_in"] = _sc_chip_scatter("scatter_o_w_in", 3, dw)
    dh1 = _mm_in_dx(dz1, wg_o_in, "mm_dh1", dw)
    dx1, dy0, g["o_pre_norm"], g["e_post_norm"] = _mid_bwd(dx2, dh1, x1, y0, row(p["o_pre_norm"]),
                                                           row(p["e_post_norm"]))
    dw = _mm_out_dw(ycat0, dy0, "mm_dwout0").reshape(NDEV, 256, D)
    parts["e_w_out"], = _sc_exchange("scatter_e_w_out", 4, [dw], True)
    dycat0 = _mm_out_dx(dy0, w_out_e, "mm_dycat0", dw)
    da_in, da_gate, g["e_pool_w"], g["e_pool_scale"] = _pool_bwd(z0, dycat0, pool_w_bf, row(p["e_pool_scale"]))
    late = [n for n in SMALL if n not in ("e_pre_norm", "o_sgu_b")] + ["o_sgu_b"]
    pieces = [g[n].reshape(SMALL[n][0]) for n in late[:-1]] + [jnp.broadcast_to(loss_part, (8, 128)), g[late[-1]]]
    recv_small, = _sc_gather_two_level("gather_small_grads", 6, [_pack(pieces, 512)])
    took = _arrived(parts["o_w_in"], "arrived_o_w_in")
    dq, dk, dv, dbg = _attn_bwd(z0, qkv, og, lg, dycat0, tabs, took)
    dz0 = _join_columns([da_in, da_gate, dq, dk, dv, dbg], "join_dz0")
    took = _arrived(recv_small, "arrived_small_grads", _arrived(parts["e_w_out"], "arrived_e_w_out", dz0))
    nb = EVEN_IN // NDEV
    swapped = [_swap_class_columns("swap_dz0_%d" % half, (9, 14)[half], dz0, nb, half, 2) for half in (0, 1)]
    dw, e_w_in_parts = took, []
    for half in (0, 1):
        dw = _mm_pair_dw(h0t, dz0, h0t_sib, swapped[half], nb, "mm_dwin0_%d" % half, dw, half, 2, True, half == 1)
        e_w_in_parts.append(_sc_chip_scatter("scatter_e_w_in_%d" % half, (5, 15)[half], dw))
    pin = adam("e_w_out", adam("o_w_out", adam("o_w_in", dw)))
    rows = [int(np.prod(SMALL[n][0])) // 128 for n in late]
    sums = _sum_unpack(recv_small, rows[:-1] + [8, rows[-1]], "sum_small_grads", pin)
    summed = dict(zip(late, sums[:-2] + sums[-1:]))
    loss = sums[-2][0, 0]
    dh0 = _mm_in_dx_halves(dz0, wg_e_in, "mm_dh0", summed[late[0]])
    grad_x, g["e_pre_norm"] = _pre0_bwd(dx1, dh0, x, row(p["e_pre_norm"]))
    last, = _sc_exchange("gather_e_pre_norm_grad", 7, [g["e_pre_norm"].reshape(16, 128)], False)

    n = "e_w_in"
    out = _adam_reduce(e_w_in_parts[0], w[n], m[n], v[n], "adam_e_w_in_0", grad_x, 0, 2)
    out = _adam_reduce(e_w_in_parts[1], w[n], m[n], v[n], "adam_e_w_in_1", None, 1, 2, out)
    grads[n], deltas[n], new_m[n], new_v[n] = out
    summed["e_pre_norm"] = _sum_parts(last, "sum_e_pre_norm_grad", out[3])
    names = list(SMALL)
    views = [_small_views(n) for n in names]
    mine = lambda src: [src[n].reshape(vw[0]) for n, vw in zip(names, views)]
    res = _adam_small(mine(w), [summed[n].reshape(vw[1]) for n, vw in zip(names, views)], [vw[2] for vw in views],
                      mine(m), mine(v), "adam_small")
    for n, out in zip(names, res):
        grads[n], deltas[n], new_m[n], new_v[n] = [t.reshape(_shard_shape(n)) for t in out]

    lead = lambda a: a[None]
    return (loss, grad_x[None], *[lead(grads[n]) for n in WEIGHTS], *[lead(deltas[n]) for n in WEIGHTS],
            *[lead(new_m[n]) for n in WEIGHTS], *[lead(new_v[n]) for n in WEIGHTS])
```
